```python
import math
import jax
import jax.numpy as jnp
from jax import lax
import numpy as np

D_MODEL = 1024
BATCH = 16
SEQ = 2048
DEPTH = 1

HEAD_DIM = 64
N_HEADS_SB = 8
N_HEADS_DIL = 8
D_SB = N_HEADS_SB * HEAD_DIM
D_DIL = N_HEADS_DIL * HEAD_DIM
D_MIX = D_SB + D_DIL
D_FF = 2816
DIL_CONFIGS = ((128, 1), (512, 4), (2048, 16))
BLOCK = 128
N_BUCKETS = 32
MAX_DISTANCE = 2048
N_MOD = 9
EPS = 1e-6
NEG_INF = -1e30

kernel_name = 'hybrid_stickbreak_dilated_macaron_adaln'


def rmsnorm(x, g):
    x32 = x.astype(jnp.float32)
    y = x32 * lax.rsqrt(jnp.mean(x32 * x32, axis=-1, keepdims=True) + EPS)
    return y.astype(x.dtype) * g


def modulate(x, g, shift, scale):
    return rmsnorm(x, g) * (1 + scale[:, None, :]) + shift[:, None, :]


def swiglu(h, w_gate, w_up, w_down):
    return (jax.nn.silu(h @ w_gate) * (h @ w_up)) @ w_down


def to_heads(a, n_heads):
    b, s, _ = a.shape
    return a.reshape(b, s, n_heads, HEAD_DIM).transpose(0, 2, 1, 3)


def from_heads(a):
    b, h, s, d = a.shape
    return a.transpose(0, 2, 1, 3).reshape(b, s, h * d)


def t5_causal_bucket(n):
    max_exact = N_BUCKETS // 2
    nf = np.maximum(n, 1).astype(np.float32)
    large = max_exact + (np.log(nf / max_exact) / math.log(MAX_DISTANCE / max_exact)
                         * (N_BUCKETS - max_exact)).astype(np.int32)
    large = np.minimum(large, N_BUCKETS - 1)
    return np.where(n < max_exact, n, large).astype(np.int32)


def stick_breaking_attention(q, k, v):
    b, h, s, dh = q.shape
    n_blk = s // BLOCK
    scale = dh ** -0.5
    key_pos = jnp.arange(s)

    def one_block(blk):
        qb = lax.dynamic_slice_in_dim(q, blk * BLOCK, BLOCK, axis=2)
        z = jnp.einsum('bhtd,bhsd->bhts', qb, k).astype(jnp.float32) * scale
        q_pos = blk * BLOCK + jnp.arange(BLOCK)
        causal = key_pos[None, :] < q_pos[:, None]
        log_not = jnp.where(causal, jax.nn.log_sigmoid(-z), 0.0)
        suffix = lax.cumsum(log_not, axis=3, reverse=True) - log_not
        w = jnp.where(causal, jnp.exp(jax.nn.log_sigmoid(z) + suffix), 0.0)
        return jnp.einsum('bhts,bhsd->bhtd', w.astype(v.dtype), v)

    out = lax.map(one_block, jnp.arange(n_blk))
    return out.transpose(1, 2, 0, 3, 4).reshape(b, h, s, dh)


def dilated_config(q, k, v, bias_table, window, dilation):
    b, h, s, dh = q.shape
    n_steps = window // dilation
    sub_len = s // dilation
    n_blk = -(-sub_len // BLOCK)
    pad = n_blk * BLOCK - sub_len

    def to_residue(a):
        a = a.reshape(b, h, sub_len, dilation, dh).transpose(0, 1, 3, 2, 4)
        a = jnp.pad(a, ((0, 0), (0, 0), (0, 0), (0, pad), (0, 0)))
        return a.reshape(b, h, dilation, n_blk, BLOCK, dh)

    def with_prev(a):
        prev = jnp.pad(a, ((0, 0), (0, 0), (0, 0), (1, 0), (0, 0), (0, 0)))[:, :, :, :-1]
        return jnp.concatenate([prev, a], axis=4)

    qr = to_residue(q)
    kb = with_prev(to_residue(k))
    vb = with_prev(to_residue(v))
    z = jnp.einsum('bhrnqd,bhrnkd->bhrnqk', qr, kb).astype(jnp.float32) * (dh ** -0.5)

    step = BLOCK + np.arange(BLOCK)[:, None] - np.arange(2 * BLOCK)[None, :]
    in_band = (step >= 0) & (step <= n_steps)
    has_prev = (np.arange(n_blk)[:, None, None] > 0) | (np.arange(2 * BLOCK)[None, None, :] >= BLOCK)
    valid = in_band[None] & has_prev

    bias = bias_table[t5_causal_bucket(np.arange(n_steps + 1) * dilation)]
    bias = bias[np.clip(step, 0, n_steps)].astype(jnp.float32)
    z = z + bias.transpose(2, 0, 1)[None, :, None, None]
    z = jnp.where(valid[None, None, None], z, NEG_INF)

    m = jnp.max(z, axis=-1, keepdims=True)
    e = jnp.exp(z - m)
    denom = jnp.sum(e, axis=-1)
    o = jnp.einsum('bhrnqk,bhrnkd->bhrnqd', e.astype(v.dtype), vb) / denom[..., None].astype(v.dtype)
    lse = m[..., 0] + jnp.log(denom)

    def from_residue(a):
        a = a.reshape(b, h, dilation, n_blk * BLOCK, a.shape[-1])[:, :, :, :sub_len]
        return a.transpose(0, 1, 3, 2, 4).reshape(b, h, s, a.shape[-1])

    return from_residue(o), from_residue(lse[..., None])[..., 0]


def dilated_attention(q, k, v, bias_table):
    outs, lses = [], []
    for window, dilation in DIL_CONFIGS:
        o, lse = dilated_config(q, k, v, bias_table, window, dilation)
        outs.append(o)
        lses.append(lse)
    alpha = jax.nn.softmax(jnp.stack(lses, axis=0), axis=0)
    return jnp.einsum('cbhs,cbhsd->bhsd', alpha.astype(q.dtype), jnp.stack(outs, axis=0))


def token_mixer(h, w_in, g_sb_out, g_dil_out, w_out, rel_bias):
    qkv = h @ w_in
    q_sb, k_sb, v_sb, q_dil, k_dil, v_dil = jnp.split(
        qkv, [D_SB, 2 * D_SB, 3 * D_SB, 3 * D_SB + D_DIL, 3 * D_SB + 2 * D_DIL], axis=-1)
    o_sb = stick_breaking_attention(to_heads(q_sb, N_HEADS_SB), to_heads(k_sb, N_HEADS_SB),
                                    to_heads(v_sb, N_HEADS_SB))
    o_dil = dilated_attention(to_heads(q_dil, N_HEADS_DIL), to_heads(k_dil, N_HEADS_DIL),
                              to_heads(v_dil, N_HEADS_DIL), rel_bias)
    o_sb = rmsnorm(o_sb, g_sb_out[:, None, :])
    o_dil = rmsnorm(o_dil, g_dil_out[:, None, :])
    o = jnp.concatenate([from_heads(o_sb), from_heads(o_dil)], axis=-1)
    return o @ w_out


def _fwd_setup_inputs(seed: int = 0) -> dict:
    key = jax.random.key(seed)
    ks = jax.random.split(key, 20)
    f32 = jnp.float32

    def nrm(k, shape, scale):
        return jax.random.normal(k, shape, f32) * scale

    def gain(k, shape):
        return 1.0 + 0.05 * jax.random.normal(k, shape, f32)

    L, D = DEPTH, D_MODEL
    return {
        'x': nrm(ks[0], (BATCH, SEQ, D), 1.0),
        'c': nrm(ks[1], (BATCH, D), 1.0),
        'w_ada': nrm(ks[2], (L, D, N_MOD * D), 0.5 * D ** -0.5),
        'b_ada': nrm(ks[3], (L, N_MOD * D), 0.02),
        'g_ffn1': gain(ks[4], (L, D)),
        'w1_gate': nrm(ks[5], (L, D, D_FF), D ** -0.5),
        'w1_up': nrm(ks[6], (L, D, D_FF), D ** -0.5),
        'w1_down': nrm(ks[7], (L, D_FF, D), D_FF ** -0.5),
        'g_mix': gain(ks[8], (L, D)),
        'w_in': nrm(ks[9], (L, D, 3 * D_MIX), D ** -0.5),
        'g_sb_out': gain(ks[10], (L, N_HEADS_SB, HEAD_DIM)),
        'g_dil_out': gain(ks[11], (L, N_HEADS_DIL, HEAD_DIM)),
        'w_out': nrm(ks[12], (L, D_MIX, D), D_MIX ** -0.5),
        'rel_bias': nrm(ks[13], (N_BUCKETS, N_HEADS_DIL), 0.5),
        'g_ffn2': gain(ks[14], (L, D)),
        'w2_gate': nrm(ks[15], (L, D, D_FF), D ** -0.5),
        'w2_up': nrm(ks[16], (L, D, D_FF), D ** -0.5),
        'w2_down': nrm(ks[17], (L, D_FF, D), D_FF ** -0.5),
        'g_final': gain(ks[18], (D,)),
    }


def _fwd_reference(x, c, w_ada, b_ada, g_ffn1, w1_gate, w1_up, w1_down, g_mix, w_in, g_sb_out,
              g_dil_out, w_out, rel_bias, g_ffn2, w2_gate, w2_up, w2_down, g_final):
    for l in range(DEPTH):
        mod = (jax.nn.silu(c) @ w_ada[l] + b_ada[l]).reshape(c.shape[0], N_MOD, D_MODEL)
        sh1, sc1, gt1, sh2, sc2, gt2, sh3, sc3, gt3 = [mod[:, i] for i in range(N_MOD)]
        h = modulate(x, g_ffn1[l], sh1, sc1)
        x = x + 0.5 * gt1[:, None, :] * swiglu(h, w1_gate[l], w1_up[l], w1_down[l])
        h = modulate(x, g_mix[l], sh2, sc2)
        x = x + gt2[:, None, :] * token_mixer(h, w_in[l], g_sb_out[l], g_dil_out[l], w_out[l], rel_bias)
        h = modulate(x, g_ffn2[l], sh3, sc3)
        x = x + 0.5 * gt3[:, None, :] * swiglu(h, w2_gate[l], w2_up[l], w2_down[l])
    return rmsnorm(x, g_final)


import jax as _jax
import jax.numpy as _jnp

TWIN_FORMAT = 'train_step'
FWD_PARAMS = ['x', 'c', 'w_ada', 'b_ada', 'g_ffn1', 'w1_gate', 'w1_up', 'w1_down', 'g_mix', 'w_in', 'g_sb_out', 'g_dil_out', 'w_out', 'rel_bias', 'g_ffn2', 'w2_gate', 'w2_up', 'w2_down', 'g_final']
TWIN_WEIGHTS = ['w_ada', 'b_ada', 'g_ffn1', 'w1_gate', 'w1_up', 'w1_down', 'g_mix', 'w_in', 'g_sb_out', 'g_dil_out', 'w_out', 'rel_bias', 'g_ffn2', 'w2_gate', 'w2_up', 'w2_down', 'g_final']
TWIN_DIFF_INPUT = 'x'
TWIN_INPUTS = ['x', 'c', 'w_ada', 'b_ada', 'g_ffn1', 'w1_gate', 'w1_up', 'w1_down', 'g_mix', 'w_in', 'g_sb_out', 'g_dil_out', 'w_out', 'rel_bias', 'g_ffn2', 'w2_gate', 'w2_up', 'w2_down', 'g_final', 'loss_target', 'm_w_ada', 'm_b_ada', 'm_g_ffn1', 'm_w1_gate', 'm_w1_up', 'm_w1_down', 'm_g_mix', 'm_w_in', 'm_g_sb_out', 'm_g_dil_out', 'm_w_out', 'm_rel_bias', 'm_g_ffn2', 'm_w2_gate', 'm_w2_up', 'm_w2_down', 'm_g_final', 'v_w_ada', 'v_b_ada', 'v_g_ffn1', 'v_w1_gate', 'v_w1_up', 'v_w1_down', 'v_g_mix', 'v_w_in', 'v_g_sb_out', 'v_g_dil_out', 'v_w_out', 'v_rel_bias', 'v_g_ffn2', 'v_w2_gate', 'v_w2_up', 'v_w2_down', 'v_g_final']
TWIN_OUTPUTS = ['loss', 'grad_x', 'grad_w_ada', 'grad_b_ada', 'grad_g_ffn1', 'grad_w1_gate', 'grad_w1_up', 'grad_w1_down', 'grad_g_mix', 'grad_w_in', 'grad_g_sb_out', 'grad_g_dil_out', 'grad_w_out', 'grad_rel_bias', 'grad_g_ffn2', 'grad_w2_gate', 'grad_w2_up', 'grad_w2_down', 'grad_g_final', 'delta_w_ada', 'delta_b_ada', 'delta_g_ffn1', 'delta_w1_gate', 'delta_w1_up', 'delta_w1_down', 'delta_g_mix', 'delta_w_in', 'delta_g_sb_out', 'delta_g_dil_out', 'delta_w_out', 'delta_rel_bias', 'delta_g_ffn2', 'delta_w2_gate', 'delta_w2_up', 'delta_w2_down', 'delta_g_final', 'new_m_w_ada', 'new_m_b_ada', 'new_m_g_ffn1', 'new_m_w1_gate', 'new_m_w1_up', 'new_m_w1_down', 'new_m_g_mix', 'new_m_w_in', 'new_m_g_sb_out', 'new_m_g_dil_out', 'new_m_w_out', 'new_m_rel_bias', 'new_m_g_ffn2', 'new_m_w2_gate', 'new_m_w2_up', 'new_m_w2_down', 'new_m_g_final', 'new_v_w_ada', 'new_v_b_ada', 'new_v_g_ffn1', 'new_v_w1_gate', 'new_v_w1_up', 'new_v_w1_down', 'new_v_g_mix', 'new_v_w_in', 'new_v_g_sb_out', 'new_v_g_dil_out', 'new_v_w_out', 'new_v_rel_bias', 'new_v_g_ffn2', 'new_v_w2_gate', 'new_v_w2_up', 'new_v_w2_down', 'new_v_g_final']
TWIN_LEAF_KINDS = {'loss': 'loss', 'grad_x': 'grad_x', 'grad_w_ada': 'grad_w', 'grad_b_ada': 'grad_w', 'grad_g_ffn1': 'grad_w', 'grad_w1_gate': 'grad_w', 'grad_w1_up': 'grad_w', 'grad_w1_down': 'grad_w', 'grad_g_mix': 'grad_w', 'grad_w_in': 'grad_w', 'grad_g_sb_out': 'grad_w', 'grad_g_dil_out': 'grad_w', 'grad_w_out': 'grad_w', 'grad_rel_bias': 'grad_w', 'grad_g_ffn2': 'grad_w', 'grad_w2_gate': 'grad_w', 'grad_w2_up': 'grad_w', 'grad_w2_down': 'grad_w', 'grad_g_final': 'grad_w', 'delta_w_ada': 'delta_w', 'delta_b_ada': 'delta_w', 'delta_g_ffn1': 'delta_w', 'delta_w1_gate': 'delta_w', 'delta_w1_up': 'delta_w', 'delta_w1_down': 'delta_w', 'delta_g_mix': 'delta_w', 'delta_w_in': 'delta_w', 'delta_g_sb_out': 'delta_w', 'delta_g_dil_out': 'delta_w', 'delta_w_out': 'delta_w', 'delta_rel_bias': 'delta_w', 'delta_g_ffn2': 'delta_w', 'delta_w2_gate': 'delta_w', 'delta_w2_up': 'delta_w', 'delta_w2_down': 'delta_w', 'delta_g_final': 'delta_w', 'new_m_w_ada': 'new_m', 'new_m_b_ada': 'new_m', 'new_m_g_ffn1': 'new_m', 'new_m_w1_gate': 'new_m', 'new_m_w1_up': 'new_m', 'new_m_w1_down': 'new_m', 'new_m_g_mix': 'new_m', 'new_m_w_in': 'new_m', 'new_m_g_sb_out': 'new_m', 'new_m_g_dil_out': 'new_m', 'new_m_w_out': 'new_m', 'new_m_rel_bias': 'new_m', 'new_m_g_ffn2': 'new_m', 'new_m_w2_gate': 'new_m', 'new_m_w2_up': 'new_m', 'new_m_w2_down': 'new_m', 'new_m_g_final': 'new_m', 'new_v_w_ada': 'new_v', 'new_v_b_ada': 'new_v', 'new_v_g_ffn1': 'new_v', 'new_v_w1_gate': 'new_v', 'new_v_w1_up': 'new_v', 'new_v_w1_down': 'new_v', 'new_v_g_mix': 'new_v', 'new_v_w_in': 'new_v', 'new_v_g_sb_out': 'new_v', 'new_v_g_dil_out': 'new_v', 'new_v_w_out': 'new_v', 'new_v_rel_bias': 'new_v', 'new_v_g_ffn2': 'new_v', 'new_v_w2_gate': 'new_v', 'new_v_w2_up': 'new_v', 'new_v_w2_down': 'new_v', 'new_v_g_final': 'new_v'}


def _forward(args):
    return _fwd_reference(*[args[k] for k in FWD_PARAMS])


def _output_shape():
    out = _jax.eval_shape(lambda: _forward(_fwd_setup_inputs(0)))
    return out.shape, out.dtype

N_MICROBATCH = 1
ADAM_LR = 0.001
ADAM_B1 = 0.9
ADAM_B2 = 0.999
ADAM_EPS = 1e-08
ADAM_WD = 0.01
ADAM_STEP = 10
PER_EXAMPLE_BATCH_AXIS = {'x': 0, 'c': 0, 'loss_target': 0}
SHARED_INPUTS = []
_WEIGHT_DTYPES = {'w_ada': _jnp.float32, 'b_ada': _jnp.float32, 'g_ffn1': _jnp.float32, 'w1_gate': _jnp.float32, 'w1_up': _jnp.float32, 'w1_down': _jnp.float32, 'g_mix': _jnp.float32, 'w_in': _jnp.float32, 'g_sb_out': _jnp.float32, 'g_dil_out': _jnp.float32, 'w_out': _jnp.float32, 'rel_bias': _jnp.float32, 'g_ffn2': _jnp.float32, 'w2_gate': _jnp.float32, 'w2_up': _jnp.float32, 'w2_down': _jnp.float32, 'g_final': _jnp.float32}
MOMENT_SCALE = {'w_ada': 1.047524e-01, 'b_ada': 1.897729e-01, 'g_ffn1': 2.803833e-02, 'w1_gate': 1.216761e-02, 'w1_up': 1.182214e-02, 'w1_down': 1.958274e-02, 'g_mix': 5.825876e-02, 'w_in': 5.583749e-02, 'g_sb_out': 6.945201e-02, 'g_dil_out': 1.247605e-01, 'w_out': 9.199991e-02, 'rel_bias': 3.295565e-02, 'g_ffn2': 2.747305e-02, 'w2_gate': 1.169929e-02, 'w2_up': 1.141523e-02, 'w2_down': 1.900769e-02, 'g_final': 3.215685e+01}


def _to_microbatches(a, axis):
    t = _jnp.moveaxis(a, axis, 0)
    t = t.reshape((N_MICROBATCH, t.shape[0] // N_MICROBATCH) + t.shape[1:])
    return _jnp.moveaxis(t, 1, axis + 1)


def setup_inputs(seed: int = 0) -> dict:
    inp = _fwd_setup_inputs(seed)
    key = _jax.random.fold_in(_jax.random.key(seed), 7919)
    shape, _ = _output_shape()
    out = dict(inp)
    out["loss_target"] = _jax.random.normal(_jax.random.fold_in(key, 0), shape, _jnp.float32)
    for i, name in enumerate(TWIN_WEIGHTS):
        w = inp[name].astype(_jnp.float32)
        if MOMENT_SCALE is None:
            s = _jnp.sqrt(_jnp.mean(_jnp.square(w)) + 1e-30)
        else:
            s = MOMENT_SCALE[name]
        km, kv = _jax.random.split(_jax.random.fold_in(key, i + 1))
        out[name] = w
        out["m_" + name] = s * _jax.random.normal(km, w.shape, _jnp.float32)
        out["v_" + name] = (s * s) * _jax.random.uniform(kv, w.shape, _jnp.float32, 0.5, 1.5)
    if N_MICROBATCH > 1:
        for name, axis in PER_EXAMPLE_BATCH_AXIS.items():
            out[name] = _to_microbatches(out[name], axis)
    return {'x': out['x'], 'c': out['c'], 'w_ada': out['w_ada'], 'b_ada': out['b_ada'], 'g_ffn1': out['g_ffn1'], 'w1_gate': out['w1_gate'], 'w1_up': out['w1_up'], 'w1_down': out['w1_down'], 'g_mix': out['g_mix'], 'w_in': out['w_in'], 'g_sb_out': out['g_sb_out'], 'g_dil_out': out['g_dil_out'], 'w_out': out['w_out'], 'rel_bias': out['rel_bias'], 'g_ffn2': out['g_ffn2'], 'w2_gate': out['w2_gate'], 'w2_up': out['w2_up'], 'w2_down': out['w2_down'], 'g_final': out['g_final'], 'loss_target': out['loss_target'], 'm_w_ada': out['m_w_ada'], 'm_b_ada': out['m_b_ada'], 'm_g_ffn1': out['m_g_ffn1'], 'm_w1_gate': out['m_w1_gate'], 'm_w1_up': out['m_w1_up'], 'm_w1_down': out['m_w1_down'], 'm_g_mix': out['m_g_mix'], 'm_w_in': out['m_w_in'], 'm_g_sb_out': out['m_g_sb_out'], 'm_g_dil_out': out['m_g_dil_out'], 'm_w_out': out['m_w_out'], 'm_rel_bias': out['m_rel_bias'], 'm_g_ffn2': out['m_g_ffn2'], 'm_w2_gate': out['m_w2_gate'], 'm_w2_up': out['m_w2_up'], 'm_w2_down': out['m_w2_down'], 'm_g_final': out['m_g_final'], 'v_w_ada': out['v_w_ada'], 'v_b_ada': out['v_b_ada'], 'v_g_ffn1': out['v_g_ffn1'], 'v_w1_gate': out['v_w1_gate'], 'v_w1_up': out['v_w1_up'], 'v_w1_down': out['v_w1_down'], 'v_g_mix': out['v_g_mix'], 'v_w_in': out['v_w_in'], 'v_g_sb_out': out['v_g_sb_out'], 'v_g_dil_out': out['v_g_dil_out'], 'v_w_out': out['v_w_out'], 'v_rel_bias': out['v_rel_bias'], 'v_g_ffn2': out['v_g_ffn2'], 'v_w2_gate': out['v_w2_gate'], 'v_w2_up': out['v_w2_up'], 'v_w2_down': out['v_w2_down'], 'v_g_final': out['v_g_final']}


def _loss(weights, diff, rest, loss_target):
    with _jax.named_scope("forward"):
        args = {**rest, TWIN_DIFF_INPUT: diff, **{k: w.astype(_WEIGHT_DTYPES[k]) for k, w in weights.items()}}
        y = _forward(args)
    with _jax.named_scope("loss_head"):
        err = _jnp.square(y.astype(_jnp.float32) - loss_target)
        return 0.5 * _jnp.sum(_jnp.mean(err, axis=-1)) if err.ndim else 0.5 * err


def _adamw(w, g, m, v):
    m = ADAM_B1 * m + (1.0 - ADAM_B1) * g
    v = ADAM_B2 * v + (1.0 - ADAM_B2) * _jnp.square(g)
    m_hat = m / (1.0 - ADAM_B1 ** ADAM_STEP)
    v_hat = v / (1.0 - ADAM_B2 ** ADAM_STEP)
    delta = -ADAM_LR * (m_hat / (_jnp.sqrt(v_hat) + ADAM_EPS) + ADAM_WD * w)
    return delta, m, v


def reference(x, c, w_ada, b_ada, g_ffn1, w1_gate, w1_up, w1_down, g_mix, w_in, g_sb_out, g_dil_out, w_out, rel_bias, g_ffn2, w2_gate, w2_up, w2_down, g_final, loss_target, m_w_ada, m_b_ada, m_g_ffn1, m_w1_gate, m_w1_up, m_w1_down, m_g_mix, m_w_in, m_g_sb_out, m_g_dil_out, m_w_out, m_rel_bias, m_g_ffn2, m_w2_gate, m_w2_up, m_w2_down, m_g_final, v_w_ada, v_b_ada, v_g_ffn1, v_w1_gate, v_w1_up, v_w1_down, v_g_mix, v_w_in, v_g_sb_out, v_g_dil_out, v_w_out, v_rel_bias, v_g_ffn2, v_w2_gate, v_w2_up, v_w2_down, v_g_final):
    given = dict(x=x, c=c, w_ada=w_ada, b_ada=b_ada, g_ffn1=g_ffn1, w1_gate=w1_gate, w1_up=w1_up, w1_down=w1_down, g_mix=g_mix, w_in=w_in, g_sb_out=g_sb_out, g_dil_out=g_dil_out, w_out=w_out, rel_bias=rel_bias, g_ffn2=g_ffn2, w2_gate=w2_gate, w2_up=w2_up, w2_down=w2_down, g_final=g_final, loss_target=loss_target, m_w_ada=m_w_ada, m_b_ada=m_b_ada, m_g_ffn1=m_g_ffn1, m_w1_gate=m_w1_gate, m_w1_up=m_w1_up, m_w1_down=m_w1_down, m_g_mix=m_g_mix, m_w_in=m_w_in, m_g_sb_out=m_g_sb_out, m_g_dil_out=m_g_dil_out, m_w_out=m_w_out, m_rel_bias=m_rel_bias, m_g_ffn2=m_g_ffn2, m_w2_gate=m_w2_gate, m_w2_up=m_w2_up, m_w2_down=m_w2_down, m_g_final=m_g_final, v_w_ada=v_w_ada, v_b_ada=v_b_ada, v_g_ffn1=v_g_ffn1, v_w1_gate=v_w1_gate, v_w1_up=v_w1_up, v_w1_down=v_w1_down, v_g_mix=v_g_mix, v_w_in=v_w_in, v_g_sb_out=v_g_sb_out, v_g_dil_out=v_g_dil_out, v_w_out=v_w_out, v_rel_bias=v_rel_bias, v_g_ffn2=v_g_ffn2, v_w2_gate=v_w2_gate, v_w2_up=v_w2_up, v_w2_down=v_w2_down, v_g_final=v_g_final)
    weights = {n: given[n] for n in TWIN_WEIGHTS}
    shared = {n: given[n] for n in SHARED_INPUTS}
    per_example = {n: given[n] for n in ['x', 'c']}
    grad_fn = _jax.value_and_grad(_loss, argnums=(0, 1))

    def one_microbatch(ex, loss_target):
        ex = dict(ex)
        diff = ex.pop(TWIN_DIFF_INPUT)
        return grad_fn(weights, diff, {**shared, **ex}, loss_target)

    if N_MICROBATCH == 1:
        loss, (grad_w, grad_x) = one_microbatch(per_example, given["loss_target"])
    else:
        def body(carry, xs):
            loss_sum, grad_sum = carry
            l_k, (gw_k, gx_k) = one_microbatch(xs[0], xs[1])
            with _jax.named_scope("update"):
                return (loss_sum + l_k, _jax.tree.map(_jnp.add, grad_sum, gw_k)), gx_k

        init = (_jnp.zeros((), _jnp.float32), _jax.tree.map(_jnp.zeros_like, weights))
        (loss, grad_w), grad_x = _jax.lax.scan(body, init, (per_example, given["loss_target"]))
    with _jax.named_scope("update"):
        delta_w, new_m, new_v = {}, {}, {}
        for n in TWIN_WEIGHTS:
            delta_w[n], new_m[n], new_v[n] = _adamw(weights[n], grad_w[n], given["m_" + n], given["v_" + n])
    return (loss, grad_x, *[grad_w[n] for n in TWIN_WEIGHTS], *[delta_w[n] for n in TWIN_WEIGHTS],
            *[new_m[n] for n in TWIN_WEIGHTS], *[new_v[n] for n in TWIN_WEIGHTS])
```

```python
import functools
import math

import numpy as np
import jax
import jax.numpy as jnp
from jax import lax
from jax.experimental import pallas as pl
from jax.experimental.pallas import tpu as pltpu

F32 = jnp.float32
BF16 = jnp.bfloat16

EPS = 1e-6
NEG_INF = -1e30
HEAD_DIM = 64
LANES = 128
DIL_BLOCK = 128
DILATIONS = (1, 4, 16)
N_BUCKETS = 32
MAX_DISTANCE = 2048
N_MOD = 9
N_DEV = 8
SB_BLOCK = 256
TOKEN_TILE = 512
VMEM_LIMIT_BYTES = 48 * 1024 * 1024

ADAM_LR = 0.001
ADAM_B1 = 0.9
ADAM_B2 = 0.999
ADAM_EPS = 1e-08
ADAM_WD = 0.01
ADAM_STEP = 10

NT_DIMS = (((1,), (1,)), ((), ()))
TN_DIMS = (((0,), (0,)), ((), ()))


def _params(*sem):
    return pltpu.CompilerParams(dimension_semantics=sem, vmem_limit_bytes=VMEM_LIMIT_BYTES)


def _dot(a, b):
    return jnp.dot(a, b, preferred_element_type=F32)


def _dot_nt(a, b):
    return lax.dot_general(a, b, NT_DIMS, preferred_element_type=F32)


def _dot_tn(a, b):
    return lax.dot_general(a, b, TN_DIMS, preferred_element_type=F32)


def _split_dot(a, b):
    hi = a.astype(BF16)
    lo = (a - hi.astype(F32)).astype(BF16)
    return _dot(hi, b) + _dot(lo, b)


def _sigmoid(z):
    return 1.0 / (1.0 + jnp.exp(-z))


def _norm(x):
    r = lax.rsqrt(jnp.mean(x * x, axis=-1, keepdims=True) + EPS)
    return x * r, r


def _modulate(x, g, mod_ref, k):
    n, _ = _norm(x)
    shift = mod_ref[3 * k:3 * k + 1, :]
    scale = mod_ref[3 * k + 1:3 * k + 2, :]
    return n * g * (1.0 + scale) + shift


def _modulate_bwd(dh, x, g, mod_ref, k):
    n, r = _norm(x)
    scale = mod_ref[3 * k + 1:3 * k + 2, :]
    dshift = jnp.sum(dh, axis=0, keepdims=True)
    dscale = jnp.sum(dh * n * g, axis=0, keepdims=True)
    dg = jnp.sum(dh * n * (1.0 + scale), axis=0, keepdims=True)
    dn = dh * g * (1.0 + scale)
    dx = r * (dn - n * jnp.mean(dn * n, axis=-1, keepdims=True))
    return dx, dshift, dscale, dg


def _exchange(arrays, scatter, name):
    n = len(arrays)
    out_shape = []
    for a in arrays:
        block = a.shape[1:] if scatter else a.shape
        out_shape.append(jax.ShapeDtypeStruct((N_DEV,) + tuple(block), a.dtype))

    def body(*refs):
        in_refs, out_refs = refs[:n], refs[n:2 * n]
        send_sems, recv_sems, local_sems = refs[2 * n:]
        x, y, c = lax.axis_index("x"), lax.axis_index("y"), lax.axis_index("c")
        me = 4 * x + 2 * y + c

        def copy(a, k):
            px = 1 - x if k & 4 else x
            py = 1 - y if k & 2 else y
            pc = 1 - c if k & 1 else c
            peer = 4 * px + 2 * py + pc
            src = in_refs[a].at[peer] if scatter else in_refs[a]
            return pltpu.make_async_remote_copy(
                src_ref=src, dst_ref=out_refs[a].at[me],
                send_sem=send_sems.at[a * (N_DEV - 1) + k - 1], recv_sem=recv_sems.at[a * (N_DEV - 1) + k - 1],
                device_id=(px, py, pc), device_id_type=pl.DeviceIdType.MESH)

        local = []
        for a in range(n):
            src = in_refs[a].at[me] if scatter else in_refs[a]
            local.append(pltpu.make_async_copy(src, out_refs[a].at[me], local_sems.at[a]))
            local[-1].start()
        remote = [copy(a, k) for a in range(n) for k in range(1, N_DEV)]
        for cp in remote:
            cp.start()
        for cp in remote:
            cp.wait_recv()
        for cp in remote:
            cp.wait_send()
        for cp in local:
            cp.wait()

    any_spec = pl.BlockSpec(memory_space=pl.ANY)
    return pl.pallas_call(
        body, name=name, out_shape=out_shape,
        in_specs=[any_spec] * n, out_specs=[any_spec] * n,
        scratch_shapes=[pltpu.SemaphoreType.DMA((n * (N_DEV - 1),)),
                        pltpu.SemaphoreType.DMA((n * (N_DEV - 1),)),
                        pltpu.SemaphoreType.DMA((n,))],
    )(*arrays)


def _ada_fwd(c_all, w, b):
    def body(c_ref, w_ref, b_ref, o_ref):
        cv = c_ref[...]
        s = (cv * _sigmoid(cv)).astype(BF16)
        o_ref[...] = _dot(s, w_ref[...].astype(BF16)) + b_ref[...]

    return pl.pallas_call(
        body, name="ada_fwd", out_shape=jax.ShapeDtypeStruct((c_all.shape[0], w.shape[1]), F32),
        compiler_params=pltpu.CompilerParams(vmem_limit_bytes=VMEM_LIMIT_BYTES),
    )(c_all, w, b)


def _ada_bwd(c_all, dmod_cols, dmod_all):
    def body(c_ref, dc_ref, da_ref, gw_ref, gb_ref):
        cv = c_ref[...]
        s = cv * _sigmoid(cv)
        gw_ref[...] = lax.dot_general(s, dc_ref[...], TN_DIMS, preferred_element_type=F32,
                                      precision=lax.Precision.HIGHEST)
        gb_ref[...] = jnp.sum(da_ref[...], axis=0, keepdims=True)

    return pl.pallas_call(
        body, name="ada_bwd",
        out_shape=(jax.ShapeDtypeStruct((c_all.shape[1], dmod_cols.shape[1]), F32),
                   jax.ShapeDtypeStruct((1, dmod_all.shape[1]), F32)),
        compiler_params=pltpu.CompilerParams(vmem_limit_bytes=VMEM_LIMIT_BYTES),
    )(c_all, dmod_cols, dmod_all)


def _ffn_fwd(x, mod, g, wg, wu, wd, k, tm):
    t, d = x.shape
    ns, _, fs = wg.shape
    nt = t // tm
    tpb = nt // mod.shape[0]

    def body(x_ref, mod_ref, g_ref, wg_ref, wu_ref, wd_ref, xo_ref, f_ref, gg_ref, uu_ref, h_sc, acc):
        j = pl.program_id(1)

        @pl.when(j == 0)
        def _():
            h_sc[...] = _modulate(x_ref[...], g_ref[...], mod_ref, k).astype(BF16)
            acc[...] = jnp.zeros_like(acc)

        h = h_sc[...]
        gate = _dot(h, wg_ref[...])
        up = _dot(h, wu_ref[...])
        act = gate * _sigmoid(gate) * up
        acc[...] += _dot(act.astype(BF16), wd_ref[...])
        gg_ref[...] = gate
        uu_ref[...] = up

        @pl.when(j == ns - 1)
        def _():
            f = acc[...]
            f_ref[...] = f
            xo_ref[...] = x_ref[...] + 0.5 * mod_ref[3 * k + 2:3 * k + 3, :] * f

    tok = pl.BlockSpec((tm, d), lambda i, j: (i, 0))
    hid = pl.BlockSpec((None, tm, fs), lambda i, j: (j, i, 0))
    return pl.pallas_call(
        body, name=f"ffn_fwd{k}", grid=(nt, ns),
        in_specs=[tok,
                  pl.BlockSpec((None, N_MOD, d), lambda i, j: (i // tpb, 0, 0)),
                  pl.BlockSpec((1, d), lambda i, j: (0, 0)),
                  pl.BlockSpec((None, d, fs), lambda i, j: (j, 0, 0)),
                  pl.BlockSpec((None, d, fs), lambda i, j: (j, 0, 0)),
                  pl.BlockSpec((None, fs, d), lambda i, j: (j, 0, 0))],
        out_specs=[tok, tok, hid, hid],
        out_shape=[jax.ShapeDtypeStruct((t, d), F32), jax.ShapeDtypeStruct((t, d), F32),
                   jax.ShapeDtypeStruct((ns, t, fs), F32), jax.ShapeDtypeStruct((ns, t, fs), F32)],
        scratch_shapes=[pltpu.VMEM((tm, d), BF16), pltpu.VMEM((tm, d), F32)],
        compiler_params=_params("arbitrary", "arbitrary"),
    )(x, mod, g, wg, wu, wd)


def _ffn_bwd(dxo, x, f, mod, g, gate, up, wg, wu, wd, k, tm):
    t, d = x.shape
    ns, _, fs = wg.shape
    nt = t // tm
    nb = mod.shape[0]
    tpb = nt // nb

    def body(dxo_ref, x_ref, f_ref, mod_ref, g_ref, gg_ref, uu_ref, wg_ref, wu_ref, wd_ref,
             dx_ref, dgg_ref, duu_ref, act_ref, h_ref, df_ref, dmod_ref, dg_ref, acc):
        i, j = pl.program_id(0), pl.program_id(1)

        @pl.when(j == 0)
        def _():
            df = 0.5 * mod_ref[3 * k + 2:3 * k + 3, :] * dxo_ref[...]
            df_ref[...] = df.astype(BF16)
            h_ref[...] = _modulate(x_ref[...], g_ref[...], mod_ref, k).astype(BF16)
            acc[...] = jnp.zeros_like(acc)

        dact = _dot_nt(df_ref[...], wd_ref[...])
        gv, uv = gg_ref[...], uu_ref[...]
        sig = _sigmoid(gv)
        s = gv * sig
        act_ref[...] = (s * uv).astype(BF16)
        dup = (dact * s).astype(BF16)
        dgate = (dact * uv * (sig * (1.0 + gv * (1.0 - sig)))).astype(BF16)
        duu_ref[...] = dup
        dgg_ref[...] = dgate
        acc[...] += _dot_nt(dgate, wg_ref[...]) + _dot_nt(dup, wu_ref[...])

        @pl.when(j == ns - 1)
        def _():
            dx, dshift, dscale, dg = _modulate_bwd(acc[...], x_ref[...], g_ref[...], mod_ref, k)
            dxo_v = dxo_ref[...]
            dx_ref[...] = dxo_v + dx
            dgt = jnp.sum(0.5 * f_ref[...] * dxo_v, axis=0, keepdims=True)

            @pl.when(i % tpb == 0)
            def _():
                dmod_ref[...] = jnp.zeros_like(dmod_ref)

            @pl.when(i == 0)
            def _():
                dg_ref[...] = jnp.zeros_like(dg_ref)

            dmod_ref[0:1, :] += dshift
            dmod_ref[1:2, :] += dscale
            dmod_ref[2:3, :] += dgt
            dg_ref[0:1, :] += dg

    tok = pl.BlockSpec((tm, d), lambda i, j: (i, 0))
    hid = pl.BlockSpec((None, tm, fs), lambda i, j: (j, i, 0))
    return pl.pallas_call(
        body, name=f"ffn_bwd{k}", grid=(nt, ns),
        in_specs=[tok, tok, tok,
                  pl.BlockSpec((None, N_MOD, d), lambda i, j: (i // tpb, 0, 0)),
                  pl.BlockSpec((1, d), lambda i, j: (0, 0)),
                  hid, hid,
                  pl.BlockSpec((None, d, fs), lambda i, j: (j, 0, 0)),
                  pl.BlockSpec((None, d, fs), lambda i, j: (j, 0, 0)),
                  pl.BlockSpec((None, fs, d), lambda i, j: (j, 0, 0))],
        out_specs=[tok, hid, hid, hid, tok, tok,
                   pl.BlockSpec((None, 8, d), lambda i, j: (i // tpb, 0, 0)),
                   pl.BlockSpec((8, d), lambda i, j: (0, 0))],
        out_shape=[jax.ShapeDtypeStruct((t, d), F32),
                   jax.ShapeDtypeStruct((ns, t, fs), BF16), jax.ShapeDtypeStruct((ns, t, fs), BF16),
                   jax.ShapeDtypeStruct((ns, t, fs), BF16),
                   jax.ShapeDtypeStruct((t, d), BF16), jax.ShapeDtypeStruct((t, d), BF16),
                   jax.ShapeDtypeStruct((nb, 8, d), F32), jax.ShapeDtypeStruct((8, d), F32)],
        scratch_shapes=[pltpu.VMEM((tm, d), F32)],
        compiler_params=_params("arbitrary", "arbitrary"),
    )(dxo, x, f, mod, g, gate, up, wg, wu, wd)


def _mm_tn(a, b, a_spec, b_spec, out_shape, out_spec, grid, name):
    block = tuple(out_shape[1:])
    last = grid[1] - 1

    def body(a_ref, b_ref, o_ref, acc):
        i = pl.program_id(1)

        @pl.when(i == 0)
        def _():
            acc[...] = jnp.zeros_like(acc)

        acc[...] += _dot_tn(a_ref[...], b_ref[...])

        @pl.when(i == last)
        def _():
            o_ref[...] = acc[...].astype(o_ref.dtype)

    return pl.pallas_call(
        body, name=name, grid=grid, in_specs=[a_spec, b_spec], out_specs=out_spec,
        out_shape=jax.ShapeDtypeStruct(out_shape, BF16),
        scratch_shapes=[pltpu.VMEM(block, F32)],
        compiler_params=_params("arbitrary", "arbitrary"),
    )(a, b)


def _ffn_weight_grads(h, dgate, dup, act, df, tm, tag):
    t, d = h.shape
    ns, _, fs = dgate.shape
    grid = (ns, t // tm)
    tok = pl.BlockSpec((tm, d), lambda j, i: (i, 0))
    hid = pl.BlockSpec((None, tm, fs), lambda j, i: (j, i, 0))
    col = pl.BlockSpec((None, d, fs), lambda j, i: (j, 0, 0))
    row = pl.BlockSpec((None, fs, d), lambda j, i: (j, 0, 0))
    gwg = _mm_tn(h, dgate, tok, hid, (ns, d, fs), col, grid, f"grad_wg{tag}")
    gwu = _mm_tn(h, dup, tok, hid, (ns, d, fs), col, grid, f"grad_wu{tag}")
    gwd = _mm_tn(act, df, hid, tok, (ns, fs, d), row, grid, f"grad_wd{tag}")
    return gwg, gwu, gwd


def _qkv_fwd(x, mod, g, win, tm):
    t, d = x.shape
    ns, _, cs = win.shape
    nt = t // tm
    tpb = nt // mod.shape[0]

    def body(x_ref, mod_ref, g_ref, w_ref, qkv_ref, h_ref):
        @pl.when(pl.program_id(1) == 0)
        def _():
            h_ref[...] = _modulate(x_ref[...], g_ref[...], mod_ref, 1).astype(BF16)

        qkv_ref[...] = _dot(h_ref[...], w_ref[...]).astype(BF16)

    tok = pl.BlockSpec((tm, d), lambda i, j: (i, 0))
    return pl.pallas_call(
        body, name="qkv_fwd", grid=(nt, ns),
        in_specs=[tok,
                  pl.BlockSpec((None, N_MOD, d), lambda i, j: (i // tpb, 0, 0)),
                  pl.BlockSpec((1, d), lambda i, j: (0, 0)),
                  pl.BlockSpec((None, d, cs), lambda i, j: (j, 0, 0))],
        out_specs=[pl.BlockSpec((tm, cs), lambda i, j: (i, j)), tok],
        out_shape=[jax.ShapeDtypeStruct((t, ns * cs), BF16), jax.ShapeDtypeStruct((t, d), BF16)],
        compiler_params=_params("arbitrary", "arbitrary"),
    )(x, mod, g, win)


def _qkv_bwd(dqkv, dxo, x, mod, g, win, tm):
    t, d = x.shape
    ns, _, cs = win.shape
    nt = t // tm
    nb = mod.shape[0]
    tpb = nt // nb

    def body(dq_ref, dxo_ref, x_ref, mod_ref, g_ref, w_ref, dx_ref, dmod_ref, dg_ref, acc):
        i, j = pl.program_id(0), pl.program_id(1)

        @pl.when(j == 0)
        def _():
            acc[...] = jnp.zeros_like(acc)

        acc[...] += _dot_nt(dq_ref[...], w_ref[...])

        @pl.when(j == ns - 1)
        def _():
            dx, dshift, dscale, dg = _modulate_bwd(acc[...], x_ref[...], g_ref[...], mod_ref, 1)
            dx_ref[...] = dxo_ref[...] + dx

            @pl.when(i % tpb == 0)
            def _():
                dmod_ref[...] = jnp.zeros_like(dmod_ref)

            @pl.when(i == 0)
            def _():
                dg_ref[...] = jnp.zeros_like(dg_ref)

            dmod_ref[0:1, :] += dshift
            dmod_ref[1:2, :] += dscale
            dg_ref[0:1, :] += dg

    tok = pl.BlockSpec((tm, d), lambda i, j: (i, 0))
    return pl.pallas_call(
        body, name="qkv_bwd", grid=(nt, ns),
        in_specs=[pl.BlockSpec((tm, cs), lambda i, j: (i, j)), tok, tok,
                  pl.BlockSpec((None, N_MOD, d), lambda i, j: (i // tpb, 0, 0)),
                  pl.BlockSpec((1, d), lambda i, j: (0, 0)),
                  pl.BlockSpec((None, d, cs), lambda i, j: (j, 0, 0))],
        out_specs=[tok,
                   pl.BlockSpec((None, 8, d), lambda i, j: (i // tpb, 0, 0)),
                   pl.BlockSpec((8, d), lambda i, j: (0, 0))],
        out_shape=[jax.ShapeDtypeStruct((t, d), F32),
                   jax.ShapeDtypeStruct((nb, 8, d), F32), jax.ShapeDtypeStruct((8, d), F32)],
        scratch_shapes=[pltpu.VMEM((tm, d), F32)],
        compiler_params=_params("arbitrary", "arbitrary"),
    )(dqkv, dxo, x, mod, g, win)


def _heads(a):
    return [a[:, h * HEAD_DIM:(h + 1) * HEAD_DIM] for h in range(LANES // HEAD_DIM)]


def _sb_logits(qh, kh, tri, causal):
    z = _dot_nt(qh, kh) * (HEAD_DIM ** -0.5)
    e = jnp.exp(-jnp.abs(z))
    log_not = -(jnp.maximum(z, 0.0) + jnp.log(1.0 + e))
    if causal is not None:
        log_not = jnp.where(causal, log_not, 0.0)
    return z, e, _split_dot(log_not, tri)


def _sb_masks():
    rows = lax.broadcasted_iota(jnp.int32, (SB_BLOCK, SB_BLOCK), 0)
    cols = lax.broadcasted_iota(jnp.int32, (SB_BLOCK, SB_BLOCK), 1)
    return (rows >= cols).astype(BF16), (rows <= cols).astype(BF16), cols < rows


def _sb_fwd(qkv, nb, seq):
    t = qkv.shape[0]
    n_pairs = (qkv.shape[1] // 6) // LANES
    tb = SB_BLOCK
    n_blk = seq // tb

    def body(q_ref, k_ref, v_ref, o_ref, c_ref):
        tri, _, causal = _sb_masks()

        def key_block(qh, kj, carry, mask):
            ks = pl.multiple_of(kj * tb, tb)
            kh, vh = _heads(k_ref[pl.ds(ks, tb), :]), _heads(v_ref[pl.ds(ks, tb), :])
            out = []
            for h in range(2):
                o, c = carry[h]
                z, _, suffix = _sb_logits(qh[h], kh[h], tri, mask)
                w = jnp.exp(z + suffix + c)
                if mask is not None:
                    w = jnp.where(mask, w, 0.0)
                out.append((o + _dot(w.astype(BF16), vh[h]), c + suffix[:, 0:1]))
            return tuple(out)

        def query_block(qi, _):
            qs = pl.multiple_of(qi * tb, tb)
            qh = _heads(q_ref[pl.ds(qs, tb), :])
            zero = (jnp.zeros((tb, HEAD_DIM), F32), jnp.zeros((tb, 1), F32))
            carry = key_block(qh, qi, (zero, zero), causal)
            carry = lax.fori_loop(0, qi, lambda it, cr: key_block(qh, qi - 1 - it, cr, None), carry)
            o_ref[pl.ds(qs, tb), :] = jnp.concatenate([carry[0][0], carry[1][0]], axis=1)
            c_ref[pl.ds(qs, tb), :] = jnp.concatenate(
                [jnp.broadcast_to(carry[h][1], (tb, HEAD_DIM)) for h in range(2)], axis=1)
            return 0

        lax.fori_loop(0, n_blk, query_block, 0)

    def spec(offset):
        return pl.BlockSpec((seq, LANES), lambda b, p: (b, offset + p))

    out = jax.ShapeDtypeStruct((t, n_pairs * LANES), F32)
    return pl.pallas_call(
        body, name="sb_fwd", grid=(nb, n_pairs),
        in_specs=[spec(0), spec(n_pairs), spec(2 * n_pairs)],
        out_specs=[spec(0), spec(0)], out_shape=[out, out],
        compiler_params=_params("arbitrary", "arbitrary"),
    )(qkv, qkv, qkv)


def _sb_bwd(qkv, do, csum, nb, seq):
    t = qkv.shape[0]
    n_pairs = (qkv.shape[1] // 6) // LANES
    tb = SB_BLOCK
    n_blk = seq // tb
    scale = HEAD_DIM ** -0.5

    def body(q_ref, k_ref, v_ref, do_ref, c_ref, dq_ref, dk_ref, dv_ref, dk_acc, dv_acc):
        tri, tri_prefix, causal = _sb_masks()
        dk_acc[...] = jnp.zeros_like(dk_acc)
        dv_acc[...] = jnp.zeros_like(dv_acc)

        def key_block(qh, doh, ch, kj, carry, mask):
            ks = pl.multiple_of(kj * tb, tb)
            kh, vh = _heads(k_ref[pl.ds(ks, tb), :]), _heads(v_ref[pl.ds(ks, tb), :])
            out, dks, dvs = [], [], []
            for h in range(2):
                dq, left, dleft = carry[h]
                z, e, suffix = _sb_logits(qh[h], kh[h], tri, mask)
                left = left + suffix[:, 0:1]
                w = jnp.exp(z + suffix + (ch[h] - left))
                if mask is not None:
                    w = jnp.where(mask, w, 0.0)
                dlw = w * _dot_nt(doh[h], vh[h])
                dprefix = _split_dot(dlw, tri_prefix)
                sig = jnp.where(z >= 0.0, 1.0, e) / (1.0 + e)
                dz = dlw * (1.0 - sig) - (dleft + dprefix - dlw) * sig
                if mask is not None:
                    dz = jnp.where(mask, dz, 0.0)
                dzb = (dz * scale).astype(BF16)
                dks.append(_dot_tn(dzb, qh[h]))
                dvs.append(_dot_tn(w.astype(BF16), doh[h]))
                out.append((dq + _dot(dzb, kh[h]), left, dleft + dprefix[:, tb - 1:tb]))
            dk_acc[pl.ds(ks, tb), :] += jnp.concatenate(dks, axis=1)
            dv_acc[pl.ds(ks, tb), :] += jnp.concatenate(dvs, axis=1)
            return tuple(out)

        def query_block(qi, _):
            qs = pl.multiple_of(qi * tb, tb)
            qh = _heads(q_ref[pl.ds(qs, tb), :])
            doh = _heads(do_ref[pl.ds(qs, tb), :].astype(BF16))
            cv = c_ref[pl.ds(qs, tb), :]
            ch = [cv[:, h * HEAD_DIM:h * HEAD_DIM + 1] for h in range(2)]
            zero = (jnp.zeros((tb, HEAD_DIM), F32), jnp.zeros((tb, 1), F32), jnp.zeros((tb, 1), F32))
            carry = lax.fori_loop(
                0, qi, lambda kj, cr: key_block(qh, doh, ch, kj, cr, None), (zero, zero))
            carry = key_block(qh, doh, ch, qi, carry, causal)
            dq_ref[pl.ds(qs, tb), :] = jnp.concatenate([carry[0][0], carry[1][0]], axis=1).astype(BF16)
            return 0

        lax.fori_loop(0, n_blk, query_block, 0)
        dk_ref[...] = dk_acc[...].astype(BF16)
        dv_ref[...] = dv_acc[...].astype(BF16)

    def spec(offset):
        return pl.BlockSpec((seq, LANES), lambda b, p: (b, offset + p))

    out = jax.ShapeDtypeStruct((t, n_pairs * LANES), BF16)
    return pl.pallas_call(
        body, name="sb_bwd", grid=(nb, n_pairs),
        in_specs=[spec(0), spec(n_pairs), spec(2 * n_pairs), spec(0), spec(0)],
        out_specs=[spec(0), spec(0), spec(0)],
        out_shape=[out, out, out],
        scratch_shapes=[pltpu.VMEM((seq, LANES), F32), pltpu.VMEM((seq, LANES), F32)],
        compiler_params=_params("arbitrary", "arbitrary"),
    )(qkv, qkv, qkv, do, csum)


def _dil_block_scores(qh, kph, kch, bias_ref, h, has_prev, band_prev, band_cur):
    scale = HEAD_DIM ** -0.5
    zp = _dot_nt(qh, kph) * scale + bias_ref[h, :, 0:DIL_BLOCK]
    zc = _dot_nt(qh, kch) * scale + bias_ref[h, :, DIL_BLOCK:2 * DIL_BLOCK]
    zp = jnp.where(band_prev, zp, NEG_INF) + jnp.where(has_prev, 0.0, NEG_INF)
    zc = jnp.where(band_cur, zc, NEG_INF)
    return zp, zc


def _dil_bands():
    rows = lax.broadcasted_iota(jnp.int32, (DIL_BLOCK, DIL_BLOCK), 0)
    cols = lax.broadcasted_iota(jnp.int32, (DIL_BLOCK, DIL_BLOCK), 1)
    return cols >= rows, cols <= rows


def _dil_blocks_per_seq(cfg, n_blk):
    per_seq = jnp.int32(n_blk // DILATIONS[0])
    for i, dil in enumerate(DILATIONS[1:], 1):
        per_seq = jnp.where(cfg == i, n_blk // dil, per_seq)
    return per_seq


def _dil_fwd(qkvd, bias, nb, seq):
    n_cfg, t, width = qkvd.shape
    n_pairs = (width // 3) // LANES
    bq = DIL_BLOCK
    n_blk = seq // bq

    def body(q_ref, k_ref, v_ref, bias_ref, o_ref, lse_ref):
        per_seq = _dil_blocks_per_seq(pl.program_id(0), n_blk)
        band_prev, band_cur = _dil_bands()

        def block(n, _):
            has_prev = (n & (per_seq - 1)) != 0
            qs = pl.multiple_of(n * bq, bq)
            ps = pl.multiple_of(jnp.maximum(n - 1, 0) * bq, bq)
            qh = _heads(q_ref[pl.ds(qs, bq), :])
            kp, kc = _heads(k_ref[pl.ds(ps, bq), :]), _heads(k_ref[pl.ds(qs, bq), :])
            vp, vc = _heads(v_ref[pl.ds(ps, bq), :]), _heads(v_ref[pl.ds(qs, bq), :])
            outs, lses = [], []
            for h in range(2):
                zp, zc = _dil_block_scores(qh[h], kp[h], kc[h], bias_ref, h, has_prev, band_prev, band_cur)
                m = jnp.maximum(jnp.max(zp, axis=1, keepdims=True), jnp.max(zc, axis=1, keepdims=True))
                ep, ec = jnp.exp(zp - m), jnp.exp(zc - m)
                den = jnp.sum(ep, axis=1, keepdims=True) + jnp.sum(ec, axis=1, keepdims=True)
                o = (_dot(ep.astype(BF16), vp[h]) + _dot(ec.astype(BF16), vc[h])) / den
                outs.append(o)
                lses.append(jnp.broadcast_to(m + jnp.log(den), (bq, HEAD_DIM)))
            o_ref[pl.ds(qs, bq), :] = jnp.concatenate(outs, axis=1)
            lse_ref[pl.ds(qs, bq), :] = jnp.concatenate(lses, axis=1)
            return 0

        lax.fori_loop(0, n_blk, block, 0)

    def spec(offset):
        return pl.BlockSpec((None, seq, LANES), lambda g, b, p: (g, b, offset + p))

    out = jax.ShapeDtypeStruct((n_cfg, t, n_pairs * LANES), F32)
    return pl.pallas_call(
        body, name="dil_fwd", grid=(n_cfg, nb, n_pairs),
        in_specs=[spec(0), spec(n_pairs), spec(2 * n_pairs),
                  pl.BlockSpec((None, 2, bq, 2 * bq), lambda g, b, p: (g, p, 0, 0))],
        out_specs=[spec(0), spec(0)], out_shape=[out, out],
        compiler_params=_params("arbitrary", "arbitrary", "arbitrary"),
    )(qkvd, qkvd, qkvd, bias)


def _dil_bwd(qkvd, bias, do, lse, delta, nb, seq):
    n_cfg, t, width = qkvd.shape
    n_pairs = (width // 3) // LANES
    bq = DIL_BLOCK
    n_blk = seq // bq
    scale = HEAD_DIM ** -0.5

    def body(q_ref, k_ref, v_ref, bias_ref, do_ref, lse_ref, dl_ref, dq_ref, dk_ref, dv_ref, db_ref):
        per_seq = _dil_blocks_per_seq(pl.program_id(0), n_blk)
        band_prev, band_cur = _dil_bands()
        dk_ref[...] = jnp.zeros_like(dk_ref)
        dv_ref[...] = jnp.zeros_like(dv_ref)

        @pl.when(pl.program_id(2) == 0)
        def _():
            db_ref[...] = jnp.zeros_like(db_ref)

        def block(n, _):
            has_prev = (n & (per_seq - 1)) != 0
            qs = pl.multiple_of(n * bq, bq)
            ps = pl.multiple_of(jnp.maximum(n - 1, 0) * bq, bq)
            qh = _heads(q_ref[pl.ds(qs, bq), :])
            kp, kc = _heads(k_ref[pl.ds(ps, bq), :]), _heads(k_ref[pl.ds(qs, bq), :])
            vp, vc = _heads(v_ref[pl.ds(ps, bq), :]), _heads(v_ref[pl.ds(qs, bq), :])
            doh = _heads(do_ref[pl.ds(qs, bq), :].astype(BF16))
            lse_v, dl_v = lse_ref[pl.ds(qs, bq), :], dl_ref[pl.ds(qs, bq), :]
            dqs, dkp, dkc, dvp, dvc = [], [], [], [], []
            for h in range(2):
                zp, zc = _dil_block_scores(qh[h], kp[h], kc[h], bias_ref, h, has_prev, band_prev, band_cur)
                lse_h = lse_v[:, h * HEAD_DIM:h * HEAD_DIM + 1]
                dl_h = dl_v[:, h * HEAD_DIM:h * HEAD_DIM + 1]
                pp, pc = jnp.exp(zp - lse_h), jnp.exp(zc - lse_h)
                dzp = pp * (_dot_nt(doh[h], vp[h]) - dl_h)
                dzc = pc * (_dot_nt(doh[h], vc[h]) - dl_h)
                db_ref[h, :, 0:bq] += dzp
                db_ref[h, :, bq:2 * bq] += dzc
                dzp_b, dzc_b = (dzp * scale).astype(BF16), (dzc * scale).astype(BF16)
                dqs.append(_dot(dzp_b, kp[h]) + _dot(dzc_b, kc[h]))
                dkp.append(_dot_tn(dzp_b, qh[h]))
                dkc.append(_dot_tn(dzc_b, qh[h]))
                dvp.append(_dot_tn(pp.astype(BF16), doh[h]))
                dvc.append(_dot_tn(pc.astype(BF16), doh[h]))
            dq_ref[pl.ds(qs, bq), :] = jnp.concatenate(dqs, axis=1)
            dk_ref[pl.ds(ps, bq), :] += jnp.concatenate(dkp, axis=1)
            dk_ref[pl.ds(qs, bq), :] += jnp.concatenate(dkc, axis=1)
            dv_ref[pl.ds(ps, bq), :] += jnp.concatenate(dvp, axis=1)
            dv_ref[pl.ds(qs, bq), :] += jnp.concatenate(dvc, axis=1)
            return 0

        lax.fori_loop(0, n_blk, block, 0)

    def spec(offset):
        return pl.BlockSpec((None, seq, LANES), lambda g, p, b: (g, b, offset + p))

    bias_spec = pl.BlockSpec((None, 2, bq, 2 * bq), lambda g, p, b: (g, p, 0, 0))
    out = jax.ShapeDtypeStruct((n_cfg, t, n_pairs * LANES), F32)
    return pl.pallas_call(
        body, name="dil_bwd", grid=(n_cfg, n_pairs, nb),
        in_specs=[spec(0), spec(n_pairs), spec(2 * n_pairs), bias_spec, spec(0), spec(0), spec(0)],
        out_specs=[spec(0), spec(0), spec(0), bias_spec],
        out_shape=[out, out, out, jax.ShapeDtypeStruct(bias.shape, F32)],
        compiler_params=_params("arbitrary", "arbitrary", "arbitrary"),
    )(qkvd, qkvd, qkvd, bias, do, lse, delta)


def _head_mean(v, gmat):
    return _split_dot(v, gmat) * (1.0 / HEAD_DIM)


def _mix_out_fwd(osb, oc, lse, gsb, gdil, gmat, wout, x, mod, tm):
    t, d = x.shape
    ds = osb.shape[1]
    nt = t // tm
    tpb = nt // mod.shape[0]

    def body(osb_ref, oc_ref, lse_ref, gsb_ref, gdil_ref, gm_ref, w_ref, x_ref, mod_ref,
             xo_ref, on_ref, m_ref, odil_ref, ld_ref):
        lses = [lse_ref[i] for i in range(len(DILATIONS))]
        top = functools.reduce(jnp.maximum, lses)
        total = top + jnp.log(sum(jnp.exp(l - top) for l in lses))
        odil = sum(jnp.exp(l - total) * oc_ref[i] for i, l in enumerate(lses))
        odil_ref[...] = odil
        ld_ref[...] = total
        gm = gm_ref[...]
        parts = []
        for o, g_ref in ((osb_ref[...], gsb_ref), (odil, gdil_ref)):
            parts.append(o * lax.rsqrt(_head_mean(o * o, gm) + EPS) * g_ref[...])
        on = jnp.concatenate(parts, axis=1).astype(BF16)
        on_ref[...] = on
        m = _dot(on, w_ref[...])
        m_ref[...] = m
        xo_ref[...] = x_ref[...] + mod_ref[5:6, :] * m

    tok = pl.BlockSpec((tm, d), lambda i: (i, 0))
    hd = pl.BlockSpec((tm, ds), lambda i: (i, 0))
    hd3 = pl.BlockSpec((len(DILATIONS), tm, ds), lambda i: (0, i, 0))
    gain = pl.BlockSpec((1, ds), lambda i: (0, 0))
    return pl.pallas_call(
        body, name="mix_out_fwd", grid=(nt,),
        in_specs=[hd, hd3, hd3, gain, gain,
                  pl.BlockSpec((ds, ds), lambda i: (0, 0)),
                  pl.BlockSpec(wout.shape, lambda i: (0, 0)),
                  tok, pl.BlockSpec((None, N_MOD, d), lambda i: (i // tpb, 0, 0))],
        out_specs=[tok, pl.BlockSpec((tm, 2 * ds), lambda i: (i, 0)), tok, hd, hd],
        out_shape=[jax.ShapeDtypeStruct((t, d), F32), jax.ShapeDtypeStruct((t, 2 * ds), BF16),
                   jax.ShapeDtypeStruct((t, d), F32), jax.ShapeDtypeStruct((t, ds), F32),
                   jax.ShapeDtypeStruct((t, ds), F32)],
        compiler_params=_params("arbitrary"),
    )(osb, oc, lse, gsb, gdil, gmat, wout, x, mod)


def _mix_out_bwd(dxo, m, mod, wout, osb, odil, gsb, gdil, gmat, tm):
    t, d = dxo.shape
    ds = osb.shape[1]
    nt = t // tm
    nb = mod.shape[0]
    tpb = nt // nb

    def body(dxo_ref, m_ref, mod_ref, w_ref, osb_ref, odil_ref, gsb_ref, gdil_ref, gm_ref,
             dm_ref, dosb_ref, dodil_ref, dldil_ref, dmod_ref, dg_ref):
        i = pl.program_id(0)
        dxo_v = dxo_ref[...]
        dm = (mod_ref[5:6, :] * dxo_v).astype(BF16)
        dm_ref[...] = dm
        dgt = jnp.sum(m_ref[...] * dxo_v, axis=0, keepdims=True)
        don = _dot_nt(dm, w_ref[...])
        gm = gm_ref[...]

        @pl.when(i % tpb == 0)
        def _():
            dmod_ref[...] = jnp.zeros_like(dmod_ref)

        @pl.when(i == 0)
        def _():
            dg_ref[...] = jnp.zeros_like(dg_ref)

        dmod_ref[2:3, :] += dgt
        groups = ((osb_ref, gsb_ref, dosb_ref), (odil_ref, gdil_ref, dodil_ref))
        for k, (o_ref, g_ref, do_ref) in enumerate(groups):
            o = o_ref[...]
            dn_out = don[:, k * ds:(k + 1) * ds]
            r = lax.rsqrt(_head_mean(o * o, gm) + EPS)
            n = o * r
            dg_ref[0:1, k * ds:(k + 1) * ds] += jnp.sum(dn_out * n, axis=0, keepdims=True)
            dn = dn_out * g_ref[...]
            do = r * (dn - n * _head_mean(dn * n, gm))
            do_ref[...] = do
            if k == 1:
                dldil_ref[...] = _head_mean(do * o, gm) * float(HEAD_DIM)

    tok = pl.BlockSpec((tm, d), lambda i: (i, 0))
    hd = pl.BlockSpec((tm, ds), lambda i: (i, 0))
    gain = pl.BlockSpec((1, ds), lambda i: (0, 0))
    hds = jax.ShapeDtypeStruct((t, ds), F32)
    return pl.pallas_call(
        body, name="mix_out_bwd", grid=(nt,),
        in_specs=[tok, tok, pl.BlockSpec((None, N_MOD, d), lambda i: (i // tpb, 0, 0)),
                  pl.BlockSpec(wout.shape, lambda i: (0, 0)), hd, hd, gain, gain,
                  pl.BlockSpec((ds, ds), lambda i: (0, 0))],
        out_specs=[tok, hd, hd, hd,
                   pl.BlockSpec((None, 8, d), lambda i: (i // tpb, 0, 0)),
                   pl.BlockSpec((8, 2 * ds), lambda i: (0, 0))],
        out_shape=[jax.ShapeDtypeStruct((t, d), BF16), hds, hds, hds,
                   jax.ShapeDtypeStruct((nb, 8, d), F32), jax.ShapeDtypeStruct((8, 2 * ds), F32)],
        compiler_params=_params("arbitrary"),
    )(dxo, m, mod, wout, osb, odil, gsb, gdil, gmat)


def _loss_head(x, target, g, tm):
    t, d = x.shape

    def body(x_ref, t_ref, g_ref, dx_ref, acc_ref):
        @pl.when(pl.program_id(0) == 0)
        def _():
            acc_ref[...] = jnp.zeros_like(acc_ref)

        n, r = _norm(x_ref[...])
        gv = g_ref[...]
        err = n * gv - t_ref[...]
        dy = err * (1.0 / d)
        acc_ref[0:1, :] += jnp.sum(err * err, axis=0, keepdims=True)
        acc_ref[1:2, :] += jnp.sum(dy * n, axis=0, keepdims=True)
        dn = dy * gv
        dx_ref[...] = r * (dn - n * jnp.mean(dn * n, axis=-1, keepdims=True))

    tok = pl.BlockSpec((tm, d), lambda i: (i, 0))
    return pl.pallas_call(
        body, name="loss_head", grid=(t // tm,),
        in_specs=[tok, tok, pl.BlockSpec((1, d), lambda i: (0, 0))],
        out_specs=[tok, pl.BlockSpec((8, d), lambda i: (0, 0))],
        out_shape=[jax.ShapeDtypeStruct((t, d), F32), jax.ShapeDtypeStruct((8, d), F32)],
        compiler_params=_params("arbitrary"),
    )(x, target, g)


def _row_tile(rows):
    if rows <= 256:
        return rows
    for cand in range(256, 15, -16):
        if rows % cand == 0:
            return cand
    return rows


def _adamw(w, parts, m, v, name):
    rows, cols = w.shape
    n_parts = parts.shape[0]
    tr = _row_tile(rows)
    c1 = 1.0 / (1.0 - ADAM_B1 ** ADAM_STEP)
    c2 = 1.0 / (1.0 - ADAM_B2 ** ADAM_STEP)

    def body(w_ref, p_ref, m_ref, v_ref, g_ref, d_ref, nm_ref, nv_ref):
        g = p_ref[0].astype(F32)
        for i in range(1, n_parts):
            g = g + p_ref[i].astype(F32)
        nm = ADAM_B1 * m_ref[...] + (1.0 - ADAM_B1) * g
        nv = ADAM_B2 * v_ref[...] + (1.0 - ADAM_B2) * (g * g)
        g_ref[...] = g
        nm_ref[...] = nm
        nv_ref[...] = nv
        d_ref[...] = -ADAM_LR * ((nm * c1) / (jnp.sqrt(nv * c2) + ADAM_EPS) + ADAM_WD * w_ref[...])

    blk = pl.BlockSpec((tr, cols), lambda i: (i, 0))
    out = jax.ShapeDtypeStruct((rows, cols), F32)
    return pl.pallas_call(
        body, name=name, grid=(rows // tr,),
        in_specs=[blk, pl.BlockSpec((n_parts, tr, cols), lambda i: (0, i, 0)), blk, blk],
        out_specs=[blk, blk, blk, blk], out_shape=[out, out, out, out],
        compiler_params=_params("arbitrary"),
    )(w, parts, m, v)


def _t5_bucket(n):
    max_exact = N_BUCKETS // 2
    nf = np.maximum(n, 1).astype(np.float32)
    large = max_exact + (np.log(nf / max_exact) / math.log(MAX_DISTANCE / max_exact)
                         * (N_BUCKETS - max_exact)).astype(np.int32)
    large = np.minimum(large, N_BUCKETS - 1)
    return np.where(n < max_exact, n, large).astype(np.int32)


def _bucket_onehot():
    table = np.zeros((len(DILATIONS), 2 * DIL_BLOCK + 1, N_BUCKETS), np.float32)
    for i, dil in enumerate(DILATIONS):
        buckets = _t5_bucket(np.arange(DIL_BLOCK + 1) * dil)
        for m in range(DIL_BLOCK + 1):
            table[i, m, buckets[DIL_BLOCK - m]] = 1.0
    return table


def _bias_blocks(rel_bias):
    row = jnp.einsum("cmn,nh->chm", _bucket_onehot(), rel_bias, precision=lax.Precision.HIGHEST)
    n_cfg, n_heads, width = row.shape
    tiled = jnp.tile(row, (1, 1, DIL_BLOCK))[..., :DIL_BLOCK * (width - 1)]
    return tiled.reshape(n_cfg, n_heads, DIL_BLOCK, width - 1)


def _bias_blocks_bwd(dblocks):
    n_cfg, n_heads = dblocks.shape[:2]
    width = 2 * DIL_BLOCK + 1
    flat = dblocks.reshape(n_cfg, n_heads, DIL_BLOCK * (width - 1))
    flat = jnp.pad(flat, ((0, 0), (0, 0), (0, DIL_BLOCK)))
    drow = jnp.sum(flat.reshape(n_cfg, n_heads, DIL_BLOCK, width), axis=2)
    return jnp.einsum("chm,cmn->nh", drow, _bucket_onehot(), precision=lax.Precision.HIGHEST)


def _to_residue(a, nb, dil):
    t, f = a.shape
    seq = t // nb
    return a.reshape(nb, seq // dil, dil, f).transpose(0, 2, 1, 3).reshape(t, f)


def _from_residue(a, nb, dil):
    t, f = a.shape
    seq = t // nb
    return a.reshape(nb, dil, seq // dil, f).transpose(0, 2, 1, 3).reshape(t, f)


def _stack_residue(a, nb):
    return jnp.stack([_to_residue(a, nb, dil) for dil in DILATIONS])


def _pad_to(a, axis, size):
    pad = [(0, 0)] * a.ndim
    pad[axis] = (0, size - a.shape[axis])
    return jnp.pad(a, pad)


def _lane_pad(n):
    return -(-n // LANES) * LANES


def _local_step(x, target, mod, gains, weights, rel_bias, tm):
    nb, seq, d = x.shape
    t = nb * seq
    g_ffn1, g_mix, g_sb, g_dil, g_ffn2, g_final = gains
    wg1, wu1, wd1, win, wout, wg2, wu2, wd2 = weights
    x0 = x.reshape(t, d)
    ds = g_sb.shape[1]
    gmat = jnp.asarray(np.kron(np.eye(ds // HEAD_DIM), np.ones((HEAD_DIM, HEAD_DIM))), BF16)
    bias = _bias_blocks(rel_bias)
    wout2 = wout.reshape(-1, d)

    x1, f1, gate1, up1 = _ffn_fwd(x0, mod, g_ffn1, wg1, wu1, wd1, 0, tm)
    qkv, h2 = _qkv_fwd(x1, mod, g_mix, win, tm)
    osb, csb = _sb_fwd(qkv, nb, seq)
    qkvd = _stack_residue(qkv[:, 3 * ds:], nb)
    oc_r, lse_r = _dil_fwd(qkvd, bias, nb, seq)
    oc = jnp.stack([_from_residue(oc_r[i], nb, dil) for i, dil in enumerate(DILATIONS)])
    lse = jnp.stack([_from_residue(lse_r[i], nb, dil) for i, dil in enumerate(DILATIONS)])
    x2, on, mix, odil, ldil = _mix_out_fwd(osb, oc, lse, g_sb, g_dil, gmat, wout2, x1, mod, tm)
    x3, f3, gate3, up3 = _ffn_fwd(x2, mod, g_ffn2, wg2, wu2, wd2, 2, tm)
    dx3, head = _loss_head(x3, target.reshape(t, d), g_final, tm)
    loss_sum = 0.5 * jnp.sum(head[0]) / d
    dg_final = head[1:2]

    dx2, dgate3, dup3, act3, h3, df3, dmod3, dg_ffn2 = _ffn_bwd(
        dx3, x2, f3, mod, g_ffn2, gate3, up3, wg2, wu2, wd2, 2, tm)
    gwg2, gwu2, gwd2 = _ffn_weight_grads(h3, dgate3, dup3, act3, df3, tm, 2)

    dm, dosb, dodil, dldil, dmod2b, dg_heads = _mix_out_bwd(
        dx2, mix, mod, wout2, osb, odil, g_sb, g_dil, gmat, tm)
    n_out = wout.shape[0]
    gwout = _mm_tn(on, dm,
                   pl.BlockSpec((tm, wout.shape[1]), lambda j, i: (i, j)),
                   pl.BlockSpec((tm, d), lambda j, i: (i, 0)),
                   wout.shape, pl.BlockSpec((None, wout.shape[1], d), lambda j, i: (j, 0, 0)),
                   (n_out, t // tm), "grad_wout")

    dq_sb, dk_sb, dv_sb = _sb_bwd(qkv, dosb, csb, nb, seq)
    dq_r, dk_r, dv_r, dbias = _dil_bwd(qkvd, bias, _stack_residue(dodil, nb), _stack_residue(ldil, nb),
                                       _stack_residue(dldil, nb), nb, seq)
    dqkv_dil = [sum(_from_residue(a[i], nb, dil) for i, dil in enumerate(DILATIONS)).astype(BF16)
                for a in (dq_r, dk_r, dv_r)]
    dqkv = jnp.concatenate([dq_sb, dk_sb, dv_sb] + dqkv_dil, axis=1)
    drel = _bias_blocks_bwd(dbias)

    dx1, dmod2a, dg_mix = _qkv_bwd(dqkv, dx2, x1, mod, g_mix, win, tm)
    n_in, _, cs = win.shape
    gwin = _mm_tn(h2, dqkv,
                  pl.BlockSpec((tm, d), lambda j, i: (i, 0)),
                  pl.BlockSpec((tm, cs), lambda j, i: (i, j)),
                  win.shape, pl.BlockSpec((None, d, cs), lambda j, i: (j, 0, 0)),
                  (n_in, t // tm), "grad_win")

    dx0, dgate1, dup1, act1, h1, df1, dmod1, dg_ffn1 = _ffn_bwd(
        dx1, x0, f1, mod, g_ffn1, gate1, up1, wg1, wu1, wd1, 0, tm)
    gwg1, gwu1, gwd1 = _ffn_weight_grads(h1, dgate1, dup1, act1, df1, tm, 0)

    dmod = jnp.concatenate([dmod1[:, 0:3], dmod2a[:, 0:2], dmod2b[:, 2:3], dmod3[:, 0:3]], axis=1)
    wgrads = (gwg1, gwu1, gwd1, gwin, gwout, gwg2, gwu2, gwd2)
    ggrads = (dg_ffn1[0:1], dg_mix[0:1], dg_heads[0:1], drel, dg_ffn2[0:1], dg_final)
    return loss_sum, dx0.reshape(nb, seq, d), wgrads, dmod, ggrads


def kernel(x, c, w_ada, b_ada, g_ffn1, w1_gate, w1_up, w1_down, g_mix, w_in, g_sb_out, g_dil_out, w_out, rel_bias, g_ffn2, w2_gate, w2_up, w2_down, g_final, loss_target, m_w_ada, m_b_ada, m_g_ffn1, m_w1_gate, m_w1_up, m_w1_down, m_g_mix, m_w_in, m_g_sb_out, m_g_dil_out, m_w_out, m_rel_bias, m_g_ffn2, m_w2_gate, m_w2_up, m_w2_down, m_g_final, v_w_ada, v_b_ada, v_g_ffn1, v_w1_gate, v_w1_up, v_w1_down, v_g_mix, v_w_in, v_g_sb_out, v_g_dil_out, v_w_out, v_rel_bias, v_g_ffn2, v_w2_gate, v_w2_up, v_w2_down, v_g_final):
    nb, seq, d = x.shape
    me = 4 * lax.axis_index("x") + 2 * lax.axis_index("y") + lax.axis_index("c")
    tm = min(TOKEN_TILE, seq)
    fs = w1_gate.shape[2]
    fs_pad = _lane_pad(fs)
    ada_cols = w_ada.shape[2]

    def col_shard(w):
        return _pad_to(w[0].astype(BF16), 1, fs_pad)

    def row_shard(w):
        return _pad_to(w[0].astype(BF16), 0, fs_pad)

    shards = [col_shard(w1_gate), col_shard(w1_up), row_shard(w1_down), w_in[0].astype(BF16),
              w_out[0].astype(BF16), col_shard(w2_gate), col_shard(w2_up), row_shard(w2_down)]
    gathered = _exchange([_pad_to(c, 0, 8)] + shards, False, "gather_weights")
    c_all = gathered[0][:, :nb].reshape(N_DEV * nb, d)
    weights = gathered[1:]

    b_cols = lax.dynamic_slice(b_ada, (0, me * ada_cols), (1, ada_cols))
    mod_part = _ada_fwd(c_all, w_ada[0], b_cols)
    mod_all = _exchange([mod_part], False, "gather_mod")[0]
    mod = lax.dynamic_slice(mod_all, (0, me * nb, 0), (N_DEV, nb, ada_cols))
    mod = mod.transpose(1, 0, 2).reshape(nb, N_MOD, d)

    n_sb = g_sb_out.shape[1] * g_sb_out.shape[2]
    gains = (g_ffn1, g_mix, g_sb_out.reshape(1, n_sb), g_dil_out.reshape(1, -1), g_ffn2,
             g_final.reshape(1, d))
    loss_sum, grad_x, wgrads, dmod, ggrads = _local_step(x, loss_target, mod, gains, weights, rel_bias, tm)
    loss = lax.psum(loss_sum, ("x", "y", "c"))

    dg_ffn1, dg_mix, dg_heads, drel, dg_ffn2, dg_final = ggrads
    width = max(d, dg_heads.shape[1], drel.size)
    small = jnp.concatenate(
        [_pad_to(a.reshape(1, -1), 1, width) for a in (dg_ffn1, dg_mix, dg_ffn2, dg_final, dg_heads, drel)]
        + [jnp.zeros((2, width), F32)], axis=0)
    dmod_all, small_all = _exchange([_pad_to(dmod.reshape(nb, N_MOD * d), 0, 8), small], False, "gather_small")
    dmod_all = dmod_all[:, :nb].reshape(N_DEV * nb, N_MOD * d)
    dmod_cols = lax.dynamic_slice(dmod_all, (0, me * ada_cols), (N_DEV * nb, ada_cols))
    gw_ada, gb_ada = _ada_bwd(c_all, dmod_cols, dmod_all)
    parts = _exchange(list(wgrads), True, "scatter_grads")

    def small_part(row, size, shape):
        return small_all[:, row, :size].reshape((N_DEV,) + shape)

    n_rel = rel_bias.shape
    updates = {
        "w_ada": (w_ada[0], gw_ada[None], m_w_ada[0], v_w_ada[0]),
        "b_ada": (b_ada, gb_ada[None], m_b_ada, v_b_ada),
        "g_ffn1": (g_ffn1, small_part(0, d, (1, d)), m_g_ffn1, v_g_ffn1),
        "w1_gate": (w1_gate[0], parts[0][:, :, :fs], m_w1_gate[0], v_w1_gate[0]),
        "w1_up": (w1_up[0], parts[1][:, :, :fs], m_w1_up[0], v_w1_up[0]),
        "w1_down": (w1_down[0], parts[2][:, :fs, :], m_w1_down[0], v_w1_down[0]),
        "g_mix": (g_mix, small_part(1, d, (1, d)), m_g_mix, v_g_mix),
        "w_in": (w_in[0], parts[3], m_w_in[0], v_w_in[0]),
        "g_sb_out": (g_sb_out[0], small_all[:, 4, :n_sb].reshape((N_DEV,) + g_sb_out.shape[1:]),
                     m_g_sb_out[0], v_g_sb_out[0]),
        "g_dil_out": (g_dil_out[0], small_all[:, 4, n_sb:dg_heads.shape[1]].reshape((N_DEV,) + g_dil_out.shape[1:]),
                      m_g_dil_out[0], v_g_dil_out[0]),
        "w_out": (w_out[0], parts[4], m_w_out[0], v_w_out[0]),
        "rel_bias": (rel_bias, small_part(5, drel.size, n_rel), m_rel_bias, v_rel_bias),
        "g_ffn2": (g_ffn2, small_part(2, d, (1, d)), m_g_ffn2, v_g_ffn2),
        "w2_gate": (w2_gate[0], parts[5][:, :, :fs], m_w2_gate[0], v_w2_gate[0]),
        "w2_up": (w2_up[0], parts[6][:, :, :fs], m_w2_up[0], v_w2_up[0]),
        "w2_down": (w2_down[0], parts[7][:, :fs, :], m_w2_down[0], v_w2_down[0]),
        "g_final": (g_final.reshape(1, d), small_part(3, d, (1, d)), m_g_final.reshape(1, d), v_g_final.reshape(1, d)),
    }
    shapes = {"w_ada": w_ada.shape, "b_ada": b_ada.shape, "g_ffn1": g_ffn1.shape, "w1_gate": w1_gate.shape,
              "w1_up": w1_up.shape, "w1_down": w1_down.shape, "g_mix": g_mix.shape, "w_in": w_in.shape,
              "g_sb_out": g_sb_out.shape, "g_dil_out": g_dil_out.shape, "w_out": w_out.shape,
              "rel_bias": rel_bias.shape, "g_ffn2": g_ffn2.shape, "w2_gate": w2_gate.shape,
              "w2_up": w2_up.shape, "w2_down": w2_down.shape, "g_final": g_final.shape}
    grads, deltas, new_m, new_v = [], [], [], []
    for name, (w, p, m, v) in updates.items():
        g, dw, nm, nv = _adamw(w, p, m, v, f"adamw_{name}")
        grads.append(g.reshape(shapes[name]))
        deltas.append(dw.reshape(shapes[name]))
        new_m.append(nm.reshape(shapes[name]))
        new_v.append(nv.reshape(shapes[name]))
    return (loss, grad_x, *grads, *deltas, *new_m, *new_v)
```

```python
import functools
import math

import numpy as np
import jax
import jax.numpy as jnp
from jax import lax
from jax.experimental import pallas as pl
from jax.experimental.pallas import tpu as pltpu

F32 = jnp.float32
BF16 = jnp.bfloat16

EPS = 1e-6
NEG_INF = -1e30
HEAD_DIM = 64
LANES = 128
DIL_BLOCK = 128
DILATIONS = (1, 4, 16)
N_BUCKETS = 32
MAX_DISTANCE = 2048
N_MOD = 9
N_DEV = 8
SB_BLOCK = 256
TOKEN_TILE = 512
VMEM_LIMIT_BYTES = 48 * 1024 * 1024

ADAM_LR = 0.001
ADAM_B1 = 0.9
ADAM_B2 = 0.999
ADAM_EPS = 1e-08
ADAM_WD = 0.01
ADAM_STEP = 10

NT_DIMS = (((1,), (1,)), ((), ()))
TN_DIMS = (((0,), (0,)), ((), ()))


def _params(*sem):
    return pltpu.CompilerParams(dimension_semantics=sem, vmem_limit_bytes=VMEM_LIMIT_BYTES)


def _dot(a, b):
    return jnp.dot(a, b, preferred_element_type=F32)


def _dot_nt(a, b):
    return lax.dot_general(a, b, NT_DIMS, preferred_element_type=F32)


def _dot_tn(a, b):
    return lax.dot_general(a, b, TN_DIMS, preferred_element_type=F32)


def _split_dot(a, b):
    hi = a.astype(BF16)
    lo = (a - hi.astype(F32)).astype(BF16)
    return _dot(hi, b) + _dot(lo, b)


def _sigmoid(z):
    return 1.0 / (1.0 + jnp.exp(-z))


def _norm(x):
    r = lax.rsqrt(jnp.mean(x * x, axis=-1, keepdims=True) + EPS)
    return x * r, r


def _modulate(x, g, mod_ref, k):
    n, _ = _norm(x)
    shift = mod_ref[3 * k:3 * k + 1, :]
    scale = mod_ref[3 * k + 1:3 * k + 2, :]
    return n * g * (1.0 + scale) + shift


def _modulate_bwd(dh, x, g, mod_ref, k):
    n, r = _norm(x)
    scale = mod_ref[3 * k + 1:3 * k + 2, :]
    dshift = jnp.sum(dh, axis=0, keepdims=True)
    dscale = jnp.sum(dh * n * g, axis=0, keepdims=True)
    dg = jnp.sum(dh * n * (1.0 + scale), axis=0, keepdims=True)
    dn = dh * g * (1.0 + scale)
    dx = r * (dn - n * jnp.mean(dn * n, axis=-1, keepdims=True))
    return dx, dshift, dscale, dg


class _Exchange:
    def __init__(self, arrays, scatter):
        self.arrays = list(arrays)
        self.scatter = scatter
        self.n = len(self.arrays)
        self.out_shape = [
            jax.ShapeDtypeStruct((N_DEV,) + tuple(a.shape[1:] if scatter else a.shape), a.dtype)
            for a in self.arrays]
        n_remote = self.n * (N_DEV - 1)
        self.scratch_shapes = [pltpu.SemaphoreType.DMA((n_remote,)), pltpu.SemaphoreType.DMA((n_remote,)),
                               pltpu.SemaphoreType.DMA((self.n,))]

    def _copies(self, in_refs, out_refs, sems):
        send_sems, recv_sems, local_sems = sems
        x, y, c = lax.axis_index("x"), lax.axis_index("y"), lax.axis_index("c")
        me = 4 * x + 2 * y + c
        local, remote = [], []
        for a in range(self.n):
            src = in_refs[a].at[me] if self.scatter else in_refs[a]
            local.append(pltpu.make_async_copy(src, out_refs[a].at[me], local_sems.at[a]))
            for k in range(1, N_DEV):
                px = 1 - x if k & 4 else x
                py = 1 - y if k & 2 else y
                pc = 1 - c if k & 1 else c
                src = in_refs[a].at[4 * px + 2 * py + pc] if self.scatter else in_refs[a]
                sem = a * (N_DEV - 1) + k - 1
                remote.append(pltpu.make_async_remote_copy(
                    src_ref=src, dst_ref=out_refs[a].at[me],
                    send_sem=send_sems.at[sem], recv_sem=recv_sems.at[sem],
                    device_id=(px, py, pc), device_id_type=pl.DeviceIdType.MESH))
        return local, remote

    def start(self, in_refs, out_refs, sems):
        local, remote = self._copies(in_refs, out_refs, sems)
        for cp in local + remote:
            cp.start()

    def wait(self, in_refs, out_refs, sems):
        local, remote = self._copies(in_refs, out_refs, sems)
        for cp in remote:
            cp.wait_recv()
        for cp in remote:
            cp.wait_send()
        for cp in local:
            cp.wait()


def _call(body, *, name, args, in_specs, out_specs, out_shape, scratch_shapes=(), grid=(),
          params=None, exchange=None):
    n_in, n_out = len(args), len(out_shape)
    if exchange is None:
        outs = pl.pallas_call(
            body, name=name, grid=grid, in_specs=list(in_specs), out_specs=list(out_specs),
            out_shape=list(out_shape), scratch_shapes=list(scratch_shapes), compiler_params=params,
        )(*args)
        return list(outs), []
    n_ex = exchange.n

    def wrapped(*refs):
        ins, refs = refs[:n_in], refs[n_in:]
        ex_in, refs = refs[:n_ex], refs[n_ex:]
        outs, refs = refs[:n_out], refs[n_out:]
        ex_out, refs = refs[:n_ex], refs[n_ex:]
        scratch, sems = refs[:len(refs) - 3], refs[len(refs) - 3:]
        if not grid:
            exchange.start(ex_in, ex_out, sems)
            body(*ins, *outs, *scratch)
            exchange.wait(ex_in, ex_out, sems)
            return
        first = functools.reduce(jnp.logical_and, [pl.program_id(a) == 0 for a in range(len(grid))])
        last = functools.reduce(jnp.logical_and, [pl.program_id(a) == grid[a] - 1 for a in range(len(grid))])

        @pl.when(first)
        def _():
            exchange.start(ex_in, ex_out, sems)

        body(*ins, *outs, *scratch)

        @pl.when(last)
        def _():
            exchange.wait(ex_in, ex_out, sems)

    any_spec = pl.BlockSpec(memory_space=pl.ANY)
    outs = pl.pallas_call(
        wrapped, name=name, grid=grid,
        in_specs=list(in_specs) + [any_spec] * n_ex, out_specs=list(out_specs) + [any_spec] * n_ex,
        out_shape=list(out_shape) + exchange.out_shape,
        scratch_shapes=list(scratch_shapes) + exchange.scratch_shapes, compiler_params=params,
    )(*args, *exchange.arrays)
    return list(outs[:n_out]), list(outs[n_out:])


def _exchange(arrays, scatter, name):
    return _call(lambda: None, name=name, args=(), in_specs=(), out_specs=(), out_shape=(),
                 exchange=_Exchange(arrays, scatter))[1]


def _ada_fwd(c_all, w, b):
    def body(c_ref, w_ref, b_ref, o_ref):
        cv = c_ref[...]
        s = (cv * _sigmoid(cv)).astype(BF16)
        o_ref[...] = _dot(s, w_ref[...].astype(BF16)) + b_ref[...]

    return pl.pallas_call(
        body, name="ada_fwd", out_shape=jax.ShapeDtypeStruct((c_all.shape[0], w.shape[1]), F32),
        compiler_params=pltpu.CompilerParams(vmem_limit_bytes=VMEM_LIMIT_BYTES),
    )(c_all, w, b)


def _ada_bwd(c_all, dmod_cols, dmod_all):
    def body(c_ref, dc_ref, da_ref, gw_ref, gb_ref):
        cv = c_ref[...]
        s = cv * _sigmoid(cv)
        gw_ref[...] = lax.dot_general(s, dc_ref[...], TN_DIMS, preferred_element_type=F32,
                                      precision=lax.Precision.HIGHEST)
        gb_ref[...] = jnp.sum(da_ref[...], axis=0, keepdims=True)

    return pl.pallas_call(
        body, name="ada_bwd",
        out_shape=(jax.ShapeDtypeStruct((c_all.shape[1], dmod_cols.shape[1]), F32),
                   jax.ShapeDtypeStruct((1, dmod_all.shape[1]), F32)),
        compiler_params=pltpu.CompilerParams(vmem_limit_bytes=VMEM_LIMIT_BYTES),
    )(c_all, dmod_cols, dmod_all)


def _ffn_fwd(x, mod, g, wg, wu, wd, k, tm, exchange=None):
    t, d = x.shape
    ns, _, fs = wg.shape
    nt = t // tm
    tpb = nt // mod.shape[0]

    def body(x_ref, mod_ref, g_ref, wg_ref, wu_ref, wd_ref, xo_ref, f_ref, gg_ref, uu_ref, h_sc, acc):
        j = pl.program_id(1)

        @pl.when(j == 0)
        def _():
            h_sc[...] = _modulate(x_ref[...], g_ref[...], mod_ref, k).astype(BF16)
            acc[...] = jnp.zeros_like(acc)

        h = h_sc[...]
        gate = _dot(h, wg_ref[...])
        up = _dot(h, wu_ref[...])
        act = gate * _sigmoid(gate) * up
        acc[...] += _dot(act.astype(BF16), wd_ref[...])
        gg_ref[...] = gate
        uu_ref[...] = up

        @pl.when(j == ns - 1)
        def _():
            f = acc[...]
            f_ref[...] = f
            xo_ref[...] = x_ref[...] + 0.5 * mod_ref[3 * k + 2:3 * k + 3, :] * f

    tok = pl.BlockSpec((tm, d), lambda i, j: (i, 0))
    hid = pl.BlockSpec((None, tm, fs), lambda i, j: (j, i, 0))
    return _call(
        body, name=f"ffn_fwd{k}", grid=(nt, ns), args=(x, mod, g, wg, wu, wd),
        in_specs=[tok,
                  pl.BlockSpec((None, N_MOD, d), lambda i, j: (i // tpb, 0, 0)),
                  pl.BlockSpec((1, d), lambda i, j: (0, 0)),
                  pl.BlockSpec((None, d, fs), lambda i, j: (j, 0, 0)),
                  pl.BlockSpec((None, d, fs), lambda i, j: (j, 0, 0)),
                  pl.BlockSpec((None, fs, d), lambda i, j: (j, 0, 0))],
        out_specs=[tok, tok, hid, hid],
        out_shape=[jax.ShapeDtypeStruct((t, d), F32), jax.ShapeDtypeStruct((t, d), F32),
                   jax.ShapeDtypeStruct((ns, t, fs), F32), jax.ShapeDtypeStruct((ns, t, fs), F32)],
        scratch_shapes=[pltpu.VMEM((tm, d), BF16), pltpu.VMEM((tm, d), F32)],
        params=_params("arbitrary", "arbitrary"), exchange=exchange)


def _ffn_bwd(dxo, x, f, mod, g, gate, up, wg, wu, wd, k, tm, exchange=None):
    t, d = x.shape
    ns, _, fs = wg.shape
    nt = t // tm
    nb = mod.shape[0]
    tpb = nt // nb

    def body(dxo_ref, x_ref, f_ref, mod_ref, g_ref, gg_ref, uu_ref, wg_ref, wu_ref, wd_ref,
             dx_ref, dgg_ref, duu_ref, act_ref, h_ref, df_ref, dmod_ref, dg_ref, acc):
        i, j = pl.program_id(0), pl.program_id(1)

        @pl.when(j == 0)
        def _():
            df = 0.5 * mod_ref[3 * k + 2:3 * k + 3, :] * dxo_ref[...]
            df_ref[...] = df.astype(BF16)
            h_ref[...] = _modulate(x_ref[...], g_ref[...], mod_ref, k).astype(BF16)
            acc[...] = jnp.zeros_like(acc)

        dact = _dot_nt(df_ref[...], wd_ref[...])
        gv, uv = gg_ref[...], uu_ref[...]
        sig = _sigmoid(gv)
        s = gv * sig
        act_ref[...] = (s * uv).astype(BF16)
        dup = (dact * s).astype(BF16)
        dgate = (dact * uv * (sig * (1.0 + gv * (1.0 - sig)))).astype(BF16)
        duu_ref[...] = dup
        dgg_ref[...] = dgate
        acc[...] += _dot_nt(dgate, wg_ref[...]) + _dot_nt(dup, wu_ref[...])

        @pl.when(j == ns - 1)
        def _():
            dx, dshift, dscale, dg = _modulate_bwd(acc[...], x_ref[...], g_ref[...], mod_ref, k)
            dxo_v = dxo_ref[...]
            dx_ref[...] = dxo_v + dx
            dgt = jnp.sum(0.5 * f_ref[...] * dxo_v, axis=0, keepdims=True)

            @pl.when(i % tpb == 0)
            def _():
                dmod_ref[...] = jnp.zeros_like(dmod_ref)

            @pl.when(i == 0)
            def _():
                dg_ref[...] = jnp.zeros_like(dg_ref)

            dmod_ref[0:1, :] += dshift
            dmod_ref[1:2, :] += dscale
            dmod_ref[2:3, :] += dgt
            dg_ref[0:1, :] += dg

    tok = pl.BlockSpec((tm, d), lambda i, j: (i, 0))
    hid = pl.BlockSpec((None, tm, fs), lambda i, j: (j, i, 0))
    return _call(
        body, name=f"ffn_bwd{k}", grid=(nt, ns), args=(dxo, x, f, mod, g, gate, up, wg, wu, wd),
        in_specs=[tok, tok, tok,
                  pl.BlockSpec((None, N_MOD, d), lambda i, j: (i // tpb, 0, 0)),
                  pl.BlockSpec((1, d), lambda i, j: (0, 0)),
                  hid, hid,
                  pl.BlockSpec((None, d, fs), lambda i, j: (j, 0, 0)),
                  pl.BlockSpec((None, d, fs), lambda i, j: (j, 0, 0)),
                  pl.BlockSpec((None, fs, d), lambda i, j: (j, 0, 0))],
        out_specs=[tok, hid, hid, hid, tok, tok,
                   pl.BlockSpec((None, 8, d), lambda i, j: (i // tpb, 0, 0)),
                   pl.BlockSpec((8, d), lambda i, j: (0, 0))],
        out_shape=[jax.ShapeDtypeStruct((t, d), F32),
                   jax.ShapeDtypeStruct((ns, t, fs), BF16), jax.ShapeDtypeStruct((ns, t, fs), BF16),
                   jax.ShapeDtypeStruct((ns, t, fs), BF16),
                   jax.ShapeDtypeStruct((t, d), BF16), jax.ShapeDtypeStruct((t, d), BF16),
                   jax.ShapeDtypeStruct((nb, 8, d), F32), jax.ShapeDtypeStruct((8, d), F32)],
        scratch_shapes=[pltpu.VMEM((tm, d), F32)],
        params=_params("arbitrary", "arbitrary"), exchange=exchange)


def _mm_tn(a, b, a_spec, b_spec, out_shape, out_spec, grid, name):
    block = tuple(out_shape[1:])
    last = grid[1] - 1

    def body(a_ref, b_ref, o_ref, acc):
        i = pl.program_id(1)

        @pl.when(i == 0)
        def _():
            acc[...] = jnp.zeros_like(acc)

        acc[...] += _dot_tn(a_ref[...], b_ref[...])

        @pl.when(i == last)
        def _():
            o_ref[...] = acc[...].astype(o_ref.dtype)

    return pl.pallas_call(
        body, name=name, grid=grid, in_specs=[a_spec, b_spec], out_specs=out_spec,
        out_shape=jax.ShapeDtypeStruct(out_shape, BF16),
        scratch_shapes=[pltpu.VMEM(block, F32)],
        compiler_params=_params("arbitrary", "arbitrary"),
    )(a, b)


def _ffn_weight_grads(h, dgate, dup, act, df, tm, tag):
    t, d = h.shape
    ns, _, fs = dgate.shape
    grid = (ns, t // tm)
    tok = pl.BlockSpec((tm, d), lambda j, i: (i, 0))
    hid = pl.BlockSpec((None, tm, fs), lambda j, i: (j, i, 0))
    col = pl.BlockSpec((None, d, fs), lambda j, i: (j, 0, 0))
    row = pl.BlockSpec((None, fs, d), lambda j, i: (j, 0, 0))
    gwg = _mm_tn(h, dgate, tok, hid, (ns, d, fs), col, grid, f"grad_wg{tag}")
    gwu = _mm_tn(h, dup, tok, hid, (ns, d, fs), col, grid, f"grad_wu{tag}")
    gwd = _mm_tn(act, df, hid, tok, (ns, fs, d), row, grid, f"grad_wd{tag}")
    return gwg, gwu, gwd


def _qkv_fwd(x, mod, g, win, tm):
    t, d = x.shape
    ns, _, cs = win.shape
    nt = t // tm
    tpb = nt // mod.shape[0]

    def body(x_ref, mod_ref, g_ref, w_ref, qkv_ref, h_ref):
        @pl.when(pl.program_id(1) == 0)
        def _():
            h_ref[...] = _modulate(x_ref[...], g_ref[...], mod_ref, 1).astype(BF16)

        qkv_ref[...] = _dot(h_ref[...], w_ref[...]).astype(BF16)

    tok = pl.BlockSpec((tm, d), lambda i, j: (i, 0))
    return pl.pallas_call(
        body, name="qkv_fwd", grid=(nt, ns),
        in_specs=[tok,
                  pl.BlockSpec((None, N_MOD, d), lambda i, j: (i // tpb, 0, 0)),
                  pl.BlockSpec((1, d), lambda i, j: (0, 0)),
                  pl.BlockSpec((None, d, cs), lambda i, j: (j, 0, 0))],
        out_specs=[pl.BlockSpec((tm, cs), lambda i, j: (i, j)), tok],
        out_shape=[jax.ShapeDtypeStruct((t, ns * cs), BF16), jax.ShapeDtypeStruct((t, d), BF16)],
        compiler_params=_params("arbitrary", "arbitrary"),
    )(x, mod, g, win)


def _qkv_bwd(dqkv, dxo, x, mod, g, win, tm):
    t, d = x.shape
    ns, _, cs = win.shape
    nt = t // tm
    nb = mod.shape[0]
    tpb = nt // nb

    def body(dq_ref, dxo_ref, x_ref, mod_ref, g_ref, w_ref, dx_ref, dmod_ref, dg_ref, acc):
        i, j = pl.program_id(0), pl.program_id(1)

        @pl.when(j == 0)
        def _():
            acc[...] = jnp.zeros_like(acc)

        acc[...] += _dot_nt(dq_ref[...], w_ref[...])

        @pl.when(j == ns - 1)
        def _():
            dx, dshift, dscale, dg = _modulate_bwd(acc[...], x_ref[...], g_ref[...], mod_ref, 1)
            dx_ref[...] = dxo_ref[...] + dx

            @pl.when(i % tpb == 0)
            def _():
                dmod_ref[...] = jnp.zeros_like(dmod_ref)

            @pl.when(i == 0)
            def _():
                dg_ref[...] = jnp.zeros_like(dg_ref)

            dmod_ref[0:1, :] += dshift
            dmod_ref[1:2, :] += dscale
            dg_ref[0:1, :] += dg

    tok = pl.BlockSpec((tm, d), lambda i, j: (i, 0))
    return pl.pallas_call(
        body, name="qkv_bwd", grid=(nt, ns),
        in_specs=[pl.BlockSpec((tm, cs), lambda i, j: (i, j)), tok, tok,
                  pl.BlockSpec((None, N_MOD, d), lambda i, j: (i // tpb, 0, 0)),
                  pl.BlockSpec((1, d), lambda i, j: (0, 0)),
                  pl.BlockSpec((None, d, cs), lambda i, j: (j, 0, 0))],
        out_specs=[tok,
                   pl.BlockSpec((None, 8, d), lambda i, j: (i // tpb, 0, 0)),
                   pl.BlockSpec((8, d), lambda i, j: (0, 0))],
        out_shape=[jax.ShapeDtypeStruct((t, d), F32),
                   jax.ShapeDtypeStruct((nb, 8, d), F32), jax.ShapeDtypeStruct((8, d), F32)],
        scratch_shapes=[pltpu.VMEM((tm, d), F32)],
        compiler_params=_params("arbitrary", "arbitrary"),
    )(dqkv, dxo, x, mod, g, win)


def _heads(a):
    return [a[:, h * HEAD_DIM:(h + 1) * HEAD_DIM] for h in range(LANES // HEAD_DIM)]


def _sb_logits(qh, kh, tri, causal):
    z = _dot_nt(qh, kh) * (HEAD_DIM ** -0.5)
    e = jnp.exp(-jnp.abs(z))
    log_not = -(jnp.maximum(z, 0.0) + jnp.log(1.0 + e))
    if causal is not None:
        log_not = jnp.where(causal, log_not, 0.0)
    return z, e, _split_dot(log_not, tri)


def _sb_masks():
    rows = lax.broadcasted_iota(jnp.int32, (SB_BLOCK, SB_BLOCK), 0)
    cols = lax.broadcasted_iota(jnp.int32, (SB_BLOCK, SB_BLOCK), 1)
    return (rows >= cols).astype(BF16), (rows <= cols).astype(BF16), cols < rows


def _sb_fwd(qkv, nb, seq, exchange=None):
    t = qkv.shape[0]
    n_pairs = (qkv.shape[1] // 6) // LANES
    tb = SB_BLOCK
    n_blk = seq // tb

    def body(q_ref, k_ref, v_ref, o_ref, c_ref):
        tri, _, causal = _sb_masks()

        def key_block(qh, kj, carry, mask):
            ks = pl.multiple_of(kj * tb, tb)
            kh, vh = _heads(k_ref[pl.ds(ks, tb), :]), _heads(v_ref[pl.ds(ks, tb), :])
            out = []
            for h in range(2):
                o, c = carry[h]
                z, _, suffix = _sb_logits(qh[h], kh[h], tri, mask)
                w = jnp.exp(z + suffix + c)
                if mask is not None:
                    w = jnp.where(mask, w, 0.0)
                out.append((o + _dot(w.astype(BF16), vh[h]), c + suffix[:, 0:1]))
            return tuple(out)

        def query_block(qi, _):
            qs = pl.multiple_of(qi * tb, tb)
            qh = _heads(q_ref[pl.ds(qs, tb), :])
            zero = (jnp.zeros((tb, HEAD_DIM), F32), jnp.zeros((tb, 1), F32))
            carry = key_block(qh, qi, (zero, zero), causal)
            carry = lax.fori_loop(0, qi, lambda it, cr: key_block(qh, qi - 1 - it, cr, None), carry)
            o_ref[pl.ds(qs, tb), :] = jnp.concatenate([carry[0][0], carry[1][0]], axis=1)
            c_ref[pl.ds(qs, tb), :] = jnp.concatenate(
                [jnp.broadcast_to(carry[h][1], (tb, HEAD_DIM)) for h in range(2)], axis=1)
            return 0

        lax.fori_loop(0, n_blk, query_block, 0)

    def spec(offset):
        return pl.BlockSpec((seq, LANES), lambda b, p: (b, offset + p))

    out = jax.ShapeDtypeStruct((t, n_pairs * LANES), F32)
    return _call(
        body, name="sb_fwd", grid=(nb, n_pairs), args=(qkv, qkv, qkv),
        in_specs=[spec(0), spec(n_pairs), spec(2 * n_pairs)],
        out_specs=[spec(0), spec(0)], out_shape=[out, out],
        params=_params("arbitrary", "arbitrary"), exchange=exchange)


def _sb_bwd(qkv, do, csum, nb, seq, exchange=None):
    t = qkv.shape[0]
    n_pairs = (qkv.shape[1] // 6) // LANES
    tb = SB_BLOCK
    n_blk = seq // tb
    scale = HEAD_DIM ** -0.5

    def body(q_ref, k_ref, v_ref, do_ref, c_ref, dq_ref, dk_ref, dv_ref, dk_acc, dv_acc):
        tri, tri_prefix, causal = _sb_masks()
        dk_acc[...] = jnp.zeros_like(dk_acc)
        dv_acc[...] = jnp.zeros_like(dv_acc)

        def key_block(qh, doh, ch, kj, carry, mask):
            ks = pl.multiple_of(kj * tb, tb)
            kh, vh = _heads(k_ref[pl.ds(ks, tb), :]), _heads(v_ref[pl.ds(ks, tb), :])
            out, dks, dvs = [], [], []
            for h in range(2):
                dq, left, dleft = carry[h]
                z, e, suffix = _sb_logits(qh[h], kh[h], tri, mask)
                left = left + suffix[:, 0:1]
                w = jnp.exp(z + suffix + (ch[h] - left))
                if mask is not None:
                    w = jnp.where(mask, w, 0.0)
                dlw = w * _dot_nt(doh[h], vh[h])
                dprefix = _split_dot(dlw, tri_prefix)
                sig = jnp.where(z >= 0.0, 1.0, e) / (1.0 + e)
                dz = dlw * (1.0 - sig) - (dleft + dprefix - dlw) * sig
                if mask is not None:
                    dz = jnp.where(mask, dz, 0.0)
                dzb = (dz * scale).astype(BF16)
                dks.append(_dot_tn(dzb, qh[h]))
                dvs.append(_dot_tn(w.astype(BF16), doh[h]))
                out.append((dq + _dot(dzb, kh[h]), left, dleft + dprefix[:, tb - 1:tb]))
            dk_acc[pl.ds(ks, tb), :] += jnp.concatenate(dks, axis=1)
            dv_acc[pl.ds(ks, tb), :] += jnp.concatenate(dvs, axis=1)
            return tuple(out)

        def query_block(qi, _):
            qs = pl.multiple_of(qi * tb, tb)
            qh = _heads(q_ref[pl.ds(qs, tb), :])
            doh = _heads(do_ref[pl.ds(qs, tb), :].astype(BF16))
            cv = c_ref[pl.ds(qs, tb), :]
            ch = [cv[:, h * HEAD_DIM:h * HEAD_DIM + 1] for h in range(2)]
            zero = (jnp.zeros((tb, HEAD_DIM), F32), jnp.zeros((tb, 1), F32), jnp.zeros((tb, 1), F32))
            carry = lax.fori_loop(
                0, qi, lambda kj, cr: key_block(qh, doh, ch, kj, cr, None), (zero, zero))
            carry = key_block(qh, doh, ch, qi, carry, causal)
            dq_ref[pl.ds(qs, tb), :] = jnp.concatenate([carry[0][0], carry[1][0]], axis=1).astype(BF16)
            return 0

        lax.fori_loop(0, n_blk, query_block, 0)
        dk_ref[...] = dk_acc[...].astype(BF16)
        dv_ref[...] = dv_acc[...].astype(BF16)

    def spec(offset):
        return pl.BlockSpec((seq, LANES), lambda b, p: (b, offset + p))

    out = jax.ShapeDtypeStruct((t, n_pairs * LANES), BF16)
    return _call(
        body, name="sb_bwd", grid=(nb, n_pairs), args=(qkv, qkv, qkv, do, csum),
        in_specs=[spec(0), spec(n_pairs), spec(2 * n_pairs), spec(0), spec(0)],
        out_specs=[spec(0), spec(0), spec(0)],
        out_shape=[out, out, out],
        scratch_shapes=[pltpu.VMEM((seq, LANES), F32), pltpu.VMEM((seq, LANES), F32)],
        params=_params("arbitrary", "arbitrary"), exchange=exchange)


def _dil_block_scores(qh, kph, kch, bias_ref, h, has_prev, band_prev, band_cur):
    scale = HEAD_DIM ** -0.5
    zp = _dot_nt(qh, kph) * scale + bias_ref[h, :, 0:DIL_BLOCK]
    zc = _dot_nt(qh, kch) * scale + bias_ref[h, :, DIL_BLOCK:2 * DIL_BLOCK]
    zp = jnp.where(band_prev, zp, NEG_INF) + jnp.where(has_prev, 0.0, NEG_INF)
    zc = jnp.where(band_cur, zc, NEG_INF)
    return zp, zc


def _dil_bands():
    rows = lax.broadcasted_iota(jnp.int32, (DIL_BLOCK, DIL_BLOCK), 0)
    cols = lax.broadcasted_iota(jnp.int32, (DIL_BLOCK, DIL_BLOCK), 1)
    return cols >= rows, cols <= rows


def _dil_blocks_per_seq(cfg, n_blk):
    per_seq = jnp.int32(n_blk // DILATIONS[0])
    for i, dil in enumerate(DILATIONS[1:], 1):
        per_seq = jnp.where(cfg == i, n_blk // dil, per_seq)
    return per_seq


def _dil_fwd(qkvd, bias, nb, seq):
    n_cfg, t, width = qkvd.shape
    n_pairs = (width // 3) // LANES
    bq = DIL_BLOCK
    n_blk = seq // bq

    def body(q_ref, k_ref, v_ref, bias_ref, o_ref, lse_ref):
        per_seq = _dil_blocks_per_seq(pl.program_id(0), n_blk)
        band_prev, band_cur = _dil_bands()

        def block(n, _):
            has_prev = (n & (per_seq - 1)) != 0
            qs = pl.multiple_of(n * bq, bq)
            ps = pl.multiple_of(jnp.maximum(n - 1, 0) * bq, bq)
            qh = _heads(q_ref[pl.ds(qs, bq), :])
            kp, kc = _heads(k_ref[pl.ds(ps, bq), :]), _heads(k_ref[pl.ds(qs, bq), :])
            vp, vc = _heads(v_ref[pl.ds(ps, bq), :]), _heads(v_ref[pl.ds(qs, bq), :])
            outs, lses = [], []
            for h in range(2):
                zp, zc = _dil_block_scores(qh[h], kp[h], kc[h], bias_ref, h, has_prev, band_prev, band_cur)
                m = jnp.maximum(jnp.max(zp, axis=1, keepdims=True), jnp.max(zc, axis=1, keepdims=True))
                ep, ec = jnp.exp(zp - m), jnp.exp(zc - m)
                den = jnp.sum(ep, axis=1, keepdims=True) + jnp.sum(ec, axis=1, keepdims=True)
                o = (_dot(ep.astype(BF16), vp[h]) + _dot(ec.astype(BF16), vc[h])) / den
                outs.append(o)
                lses.append(jnp.broadcast_to(m + jnp.log(den), (bq, HEAD_DIM)))
            o_ref[pl.ds(qs, bq), :] = jnp.concatenate(outs, axis=1)
            lse_ref[pl.ds(qs, bq), :] = jnp.concatenate(lses, axis=1)
            return 0

        lax.fori_loop(0, n_blk, block, 0)

    def spec(offset):
        return pl.BlockSpec((None, seq, LANES), lambda g, b, p: (g, b, offset + p))

    out = jax.ShapeDtypeStruct((n_cfg, t, n_pairs * LANES), F32)
    return pl.pallas_call(
        body, name="dil_fwd", grid=(n_cfg, nb, n_pairs),
        in_specs=[spec(0), spec(n_pairs), spec(2 * n_pairs),
                  pl.BlockSpec((None, 2, bq, 2 * bq), lambda g, b, p: (g, p, 0, 0))],
        out_specs=[spec(0), spec(0)], out_shape=[out, out],
        compiler_params=_params("arbitrary", "arbitrary", "arbitrary"),
    )(qkvd, qkvd, qkvd, bias)


def _dil_bwd(qkvd, bias, do, lse, delta, nb, seq):
    n_cfg, t, width = qkvd.shape
    n_pairs = (width // 3) // LANES
    bq = DIL_BLOCK
    n_blk = seq // bq
    scale = HEAD_DIM ** -0.5

    def body(q_ref, k_ref, v_ref, bias_ref, do_ref, lse_ref, dl_ref, dq_ref, dk_ref, dv_ref, db_ref):
        per_seq = _dil_blocks_per_seq(pl.program_id(0), n_blk)
        band_prev, band_cur = _dil_bands()
        dk_ref[...] = jnp.zeros_like(dk_ref)
        dv_ref[...] = jnp.zeros_like(dv_ref)

        @pl.when(pl.program_id(2) == 0)
        def _():
            db_ref[...] = jnp.zeros_like(db_ref)

        def block(n, _):
            has_prev = (n & (per_seq - 1)) != 0
            qs = pl.multiple_of(n * bq, bq)
            ps = pl.multiple_of(jnp.maximum(n - 1, 0) * bq, bq)
            qh = _heads(q_ref[pl.ds(qs, bq), :])
            kp, kc = _heads(k_ref[pl.ds(ps, bq), :]), _heads(k_ref[pl.ds(qs, bq), :])
            vp, vc = _heads(v_ref[pl.ds(ps, bq), :]), _heads(v_ref[pl.ds(qs, bq), :])
            doh = _heads(do_ref[pl.ds(qs, bq), :].astype(BF16))
            lse_v, dl_v = lse_ref[pl.ds(qs, bq), :], dl_ref[pl.ds(qs, bq), :]
            dqs, dkp, dkc, dvp, dvc = [], [], [], [], []
            for h in range(2):
                zp, zc = _dil_block_scores(qh[h], kp[h], kc[h], bias_ref, h, has_prev, band_prev, band_cur)
                lse_h = lse_v[:, h * HEAD_DIM:h * HEAD_DIM + 1]
                dl_h = dl_v[:, h * HEAD_DIM:h * HEAD_DIM + 1]
                pp, pc = jnp.exp(zp - lse_h), jnp.exp(zc - lse_h)
                dzp = pp * (_dot_nt(doh[h], vp[h]) - dl_h)
                dzc = pc * (_dot_nt(doh[h], vc[h]) - dl_h)
                db_ref[h, :, 0:bq] += dzp
                db_ref[h, :, bq:2 * bq] += dzc
                dzp_b, dzc_b = (dzp * scale).astype(BF16), (dzc * scale).astype(BF16)
                dqs.append(_dot(dzp_b, kp[h]) + _dot(dzc_b, kc[h]))
                dkp.append(_dot_tn(dzp_b, qh[h]))
                dkc.append(_dot_tn(dzc_b, qh[h]))
                dvp.append(_dot_tn(pp.astype(BF16), doh[h]))
                dvc.append(_dot_tn(pc.astype(BF16), doh[h]))
            dq_ref[pl.ds(qs, bq), :] = jnp.concatenate(dqs, axis=1)
            dk_ref[pl.ds(ps, bq), :] += jnp.concatenate(dkp, axis=1)
            dk_ref[pl.ds(qs, bq), :] += jnp.concatenate(dkc, axis=1)
            dv_ref[pl.ds(ps, bq), :] += jnp.concatenate(dvp, axis=1)
            dv_ref[pl.ds(qs, bq), :] += jnp.concatenate(dvc, axis=1)
            return 0

        lax.fori_loop(0, n_blk, block, 0)

    def spec(offset):
        return pl.BlockSpec((None, seq, LANES), lambda g, p, b: (g, b, offset + p))

    bias_spec = pl.BlockSpec((None, 2, bq, 2 * bq), lambda g, p, b: (g, p, 0, 0))
    out = jax.ShapeDtypeStruct((n_cfg, t, n_pairs * LANES), F32)
    return pl.pallas_call(
        body, name="dil_bwd", grid=(n_cfg, n_pairs, nb),
        in_specs=[spec(0), spec(n_pairs), spec(2 * n_pairs), bias_spec, spec(0), spec(0), spec(0)],
        out_specs=[spec(0), spec(0), spec(0), bias_spec],
        out_shape=[out, out, out, jax.ShapeDtypeStruct(bias.shape, F32)],
        compiler_params=_params("arbitrary", "arbitrary", "arbitrary"),
    )(qkvd, qkvd, qkvd, bias, do, lse, delta)


def _head_mean(v, gmat):
    return _split_dot(v, gmat) * (1.0 / HEAD_DIM)


def _mix_out_fwd(osb, oc, lse, gsb, gdil, gmat, wout, x, mod, tm):
    t, d = x.shape
    ds = osb.shape[1]
    nt = t // tm
    tpb = nt // mod.shape[0]

    def body(osb_ref, oc_ref, lse_ref, gsb_ref, gdil_ref, gm_ref, w_ref, x_ref, mod_ref,
             xo_ref, on_ref, m_ref, odil_ref, ld_ref):
        lses = [lse_ref[i] for i in range(len(DILATIONS))]
        top = functools.reduce(jnp.maximum, lses)
        total = top + jnp.log(sum(jnp.exp(l - top) for l in lses))
        odil = sum(jnp.exp(l - total) * oc_ref[i] for i, l in enumerate(lses))
        odil_ref[...] = odil
        ld_ref[...] = total
        gm = gm_ref[...]
        parts = []
        for o, g_ref in ((osb_ref[...], gsb_ref), (odil, gdil_ref)):
            parts.append(o * lax.rsqrt(_head_mean(o * o, gm) + EPS) * g_ref[...])
        on = jnp.concatenate(parts, axis=1).astype(BF16)
        on_ref[...] = on
        m = _dot(on, w_ref[...])
        m_ref[...] = m
        xo_ref[...] = x_ref[...] + mod_ref[5:6, :] * m

    tok = pl.BlockSpec((tm, d), lambda i: (i, 0))
    hd = pl.BlockSpec((tm, ds), lambda i: (i, 0))
    hd3 = pl.BlockSpec((len(DILATIONS), tm, ds), lambda i: (0, i, 0))
    gain = pl.BlockSpec((1, ds), lambda i: (0, 0))
    return pl.pallas_call(
        body, name="mix_out_fwd", grid=(nt,),
        in_specs=[hd, hd3, hd3, gain, gain,
                  pl.BlockSpec((ds, ds), lambda i: (0, 0)),
                  pl.BlockSpec(wout.shape, lambda i: (0, 0)),
                  tok, pl.BlockSpec((None, N_MOD, d), lambda i: (i // tpb, 0, 0))],
        out_specs=[tok, pl.BlockSpec((tm, 2 * ds), lambda i: (i, 0)), tok, hd, hd],
        out_shape=[jax.ShapeDtypeStruct((t, d), F32), jax.ShapeDtypeStruct((t, 2 * ds), BF16),
                   jax.ShapeDtypeStruct((t, d), F32), jax.ShapeDtypeStruct((t, ds), F32),
                   jax.ShapeDtypeStruct((t, ds), F32)],
        compiler_params=_params("arbitrary"),
    )(osb, oc, lse, gsb, gdil, gmat, wout, x, mod)


def _mix_out_bwd(dxo, m, mod, wout, osb, odil, gsb, gdil, gmat, tm):
    t, d = dxo.shape
    ds = osb.shape[1]
    nt = t // tm
    nb = mod.shape[0]
    tpb = nt // nb

    def body(dxo_ref, m_ref, mod_ref, w_ref, osb_ref, odil_ref, gsb_ref, gdil_ref, gm_ref,
             dm_ref, dosb_ref, dodil_ref, dldil_ref, dmod_ref, dg_ref):
        i = pl.program_id(0)
        dxo_v = dxo_ref[...]
        dm = (mod_ref[5:6, :] * dxo_v).astype(BF16)
        dm_ref[...] = dm
        dgt = jnp.sum(m_ref[...] * dxo_v, axis=0, keepdims=True)
        don = _dot_nt(dm, w_ref[...])
        gm = gm_ref[...]

        @pl.when(i % tpb == 0)
        def _():
            dmod_ref[...] = jnp.zeros_like(dmod_ref)

        @pl.when(i == 0)
        def _():
            dg_ref[...] = jnp.zeros_like(dg_ref)

        dmod_ref[2:3, :] += dgt
        groups = ((osb_ref, gsb_ref, dosb_ref), (odil_ref, gdil_ref, dodil_ref))
        for k, (o_ref, g_ref, do_ref) in enumerate(groups):
            o = o_ref[...]
            dn_out = don[:, k * ds:(k + 1) * ds]
            r = lax.rsqrt(_head_mean(o * o, gm) + EPS)
            n = o * r
            dg_ref[0:1, k * ds:(k + 1) * ds] += jnp.sum(dn_out * n, axis=0, keepdims=True)
            dn = dn_out * g_ref[...]
            do = r * (dn - n * _head_mean(dn * n, gm))
            do_ref[...] = do
            if k == 1:
                dldil_ref[...] = _head_mean(do * o, gm) * float(HEAD_DIM)

    tok = pl.BlockSpec((tm, d), lambda i: (i, 0))
    hd = pl.BlockSpec((tm, ds), lambda i: (i, 0))
    gain = pl.BlockSpec((1, ds), lambda i: (0, 0))
    hds = jax.ShapeDtypeStruct((t, ds), F32)
    return pl.pallas_call(
        body, name="mix_out_bwd", grid=(nt,),
        in_specs=[tok, tok, pl.BlockSpec((None, N_MOD, d), lambda i: (i // tpb, 0, 0)),
                  pl.BlockSpec(wout.shape, lambda i: (0, 0)), hd, hd, gain, gain,
                  pl.BlockSpec((ds, ds), lambda i: (0, 0))],
        out_specs=[tok, hd, hd, hd,
                   pl.BlockSpec((None, 8, d), lambda i: (i // tpb, 0, 0)),
                   pl.BlockSpec((8, 2 * ds), lambda i: (0, 0))],
        out_shape=[jax.ShapeDtypeStruct((t, d), BF16), hds, hds, hds,
                   jax.ShapeDtypeStruct((nb, 8, d), F32), jax.ShapeDtypeStruct((8, 2 * ds), F32)],
        compiler_params=_params("arbitrary"),
    )(dxo, m, mod, wout, osb, odil, gsb, gdil, gmat)


def _loss_head(x, target, g, tm):
    t, d = x.shape

    def body(x_ref, t_ref, g_ref, dx_ref, acc_ref):
        @pl.when(pl.program_id(0) == 0)
        def _():
            acc_ref[...] = jnp.zeros_like(acc_ref)

        n, r = _norm(x_ref[...])
        gv = g_ref[...]
        err = n * gv - t_ref[...]
        dy = err * (1.0 / d)
        acc_ref[0:1, :] += jnp.sum(err * err, axis=0, keepdims=True)
        acc_ref[1:2, :] += jnp.sum(dy * n, axis=0, keepdims=True)
        dn = dy * gv
        dx_ref[...] = r * (dn - n * jnp.mean(dn * n, axis=-1, keepdims=True))

    tok = pl.BlockSpec((tm, d), lambda i: (i, 0))
    return pl.pallas_call(
        body, name="loss_head", grid=(t // tm,),
        in_specs=[tok, tok, pl.BlockSpec((1, d), lambda i: (0, 0))],
        out_specs=[tok, pl.BlockSpec((8, d), lambda i: (0, 0))],
        out_shape=[jax.ShapeDtypeStruct((t, d), F32), jax.ShapeDtypeStruct((8, d), F32)],
        compiler_params=_params("arbitrary"),
    )(x, target, g)


def _row_tile(rows):
    if rows <= 256:
        return rows
    for cand in range(256, 15, -16):
        if rows % cand == 0:
            return cand
    return rows


def _adamw(w, parts, m, v, name):
    rows, cols = w.shape
    n_parts = parts.shape[0]
    tr = _row_tile(rows)
    c1 = 1.0 / (1.0 - ADAM_B1 ** ADAM_STEP)
    c2 = 1.0 / (1.0 - ADAM_B2 ** ADAM_STEP)

    def body(w_ref, p_ref, m_ref, v_ref, g_ref, d_ref, nm_ref, nv_ref):
        g = p_ref[0].astype(F32)
        for i in range(1, n_parts):
            g = g + p_ref[i].astype(F32)
        nm = ADAM_B1 * m_ref[...] + (1.0 - ADAM_B1) * g
        nv = ADAM_B2 * v_ref[...] + (1.0 - ADAM_B2) * (g * g)
        g_ref[...] = g
        nm_ref[...] = nm
        nv_ref[...] = nv
        d_ref[...] = -ADAM_LR * ((nm * c1) / (jnp.sqrt(nv * c2) + ADAM_EPS) + ADAM_WD * w_ref[...])

    blk = pl.BlockSpec((tr, cols), lambda i: (i, 0))
    out = jax.ShapeDtypeStruct((rows, cols), F32)
    return pl.pallas_call(
        body, name=name, grid=(rows // tr,),
        in_specs=[blk, pl.BlockSpec((n_parts, tr, cols), lambda i: (0, i, 0)), blk, blk],
        out_specs=[blk, blk, blk, blk], out_shape=[out, out, out, out],
        compiler_params=_params("arbitrary"),
    )(w, parts, m, v)


def _t5_bucket(n):
    max_exact = N_BUCKETS // 2
    nf = np.maximum(n, 1).astype(np.float32)
    large = max_exact + (np.log(nf / max_exact) / math.log(MAX_DISTANCE / max_exact)
                         * (N_BUCKETS - max_exact)).astype(np.int32)
    large = np.minimum(large, N_BUCKETS - 1)
    return np.where(n < max_exact, n, large).astype(np.int32)


def _bucket_onehot():
    table = np.zeros((len(DILATIONS), 2 * DIL_BLOCK + 1, N_BUCKETS), np.float32)
    for i, dil in enumerate(DILATIONS):
        buckets = _t5_bucket(np.arange(DIL_BLOCK + 1) * dil)
        for m in range(DIL_BLOCK + 1):
            table[i, m, buckets[DIL_BLOCK - m]] = 1.0
    return table


def _bias_blocks(rel_bias):
    row = jnp.einsum("cmn,nh->chm", _bucket_onehot(), rel_bias, precision=lax.Precision.HIGHEST)
    n_cfg, n_heads, width = row.shape
    tiled = jnp.tile(row, (1, 1, DIL_BLOCK))[..., :DIL_BLOCK * (width - 1)]
    return tiled.reshape(n_cfg, n_heads, DIL_BLOCK, width - 1)


def _bias_blocks_bwd(dblocks):
    n_cfg, n_heads = dblocks.shape[:2]
    width = 2 * DIL_BLOCK + 1
    flat = dblocks.reshape(n_cfg, n_heads, DIL_BLOCK * (width - 1))
    flat = jnp.pad(flat, ((0, 0), (0, 0), (0, DIL_BLOCK)))
    drow = jnp.sum(flat.reshape(n_cfg, n_heads, DIL_BLOCK, width), axis=2)
    return jnp.einsum("chm,cmn->nh", drow, _bucket_onehot(), precision=lax.Precision.HIGHEST)


def _to_residue(a, nb, dil):
    t, f = a.shape
    seq = t // nb
    return a.reshape(nb, seq // dil, dil, f).transpose(0, 2, 1, 3).reshape(t, f)


def _from_residue(a, nb, dil):
    t, f = a.shape
    seq = t // nb
    return a.reshape(nb, dil, seq // dil, f).transpose(0, 2, 1, 3).reshape(t, f)


def _stack_residue(a, nb):
    return jnp.stack([_to_residue(a, nb, dil) for dil in DILATIONS])


def _pad_to(a, axis, size):
    pad = [(0, 0)] * a.ndim
    pad[axis] = (0, size - a.shape[axis])
    return jnp.pad(a, pad)


def _lane_pad(n):
    return -(-n // LANES) * LANES


def _local_step(x, target, mod, gains, weights, rel_bias, tm, distributed):
    nb, seq, d = x.shape
    t = nb * seq
    g_ffn1, g_mix, g_sb, g_dil, g_ffn2, g_final = gains
    wg1, wu1, wd1 = weights[:3]
    x0 = x.reshape(t, d)
    ds = g_sb.shape[1]
    gmat = jnp.asarray(np.kron(np.eye(ds // HEAD_DIM), np.ones((HEAD_DIM, HEAD_DIM))), BF16)
    bias = _bias_blocks(rel_bias)

    def beside(arrays, scatter):
        return _Exchange(arrays, scatter) if distributed else None

    (x1, f1, gate1, up1), got = _ffn_fwd(x0, mod, g_ffn1, wg1, wu1, wd1, 0, tm, beside(weights[3:5], False))
    win, wout = got if distributed else weights[3:5]
    wout2 = wout.reshape(-1, d)
    qkv, h2 = _qkv_fwd(x1, mod, g_mix, win, tm)
    (osb, csb), got = _sb_fwd(qkv, nb, seq, beside(weights[5:8], False))
    wg2, wu2, wd2 = got if distributed else weights[5:8]
    qkvd = _stack_residue(qkv[:, 3 * ds:], nb)
    oc_r, lse_r = _dil_fwd(qkvd, bias, nb, seq)
    oc = jnp.stack([_from_residue(oc_r[i], nb, dil) for i, dil in enumerate(DILATIONS)])
    lse = jnp.stack([_from_residue(lse_r[i], nb, dil) for i, dil in enumerate(DILATIONS)])
    x2, on, mix, odil, ldil = _mix_out_fwd(osb, oc, lse, g_sb, g_dil, gmat, wout2, x1, mod, tm)
    (x3, f3, gate3, up3), _ = _ffn_fwd(x2, mod, g_ffn2, wg2, wu2, wd2, 2, tm)
    dx3, head = _loss_head(x3, target.reshape(t, d), g_final, tm)
    loss_sum = 0.5 * jnp.sum(head[0]) / d
    dg_final = head[1:2]

    (dx2, dgate3, dup3, act3, h3, df3, dmod3, dg_ffn2), _ = _ffn_bwd(
        dx3, x2, f3, mod, g_ffn2, gate3, up3, wg2, wu2, wd2, 2, tm)
    gwg2, gwu2, gwd2 = _ffn_weight_grads(h3, dgate3, dup3, act3, df3, tm, 2)

    dm, dosb, dodil, dldil, dmod2b, dg_heads = _mix_out_bwd(
        dx2, mix, mod, wout2, osb, odil, g_sb, g_dil, gmat, tm)
    n_out = wout.shape[0]
    gwout = _mm_tn(on, dm,
                   pl.BlockSpec((tm, wout.shape[1]), lambda j, i: (i, j)),
                   pl.BlockSpec((tm, d), lambda j, i: (i, 0)),
                   wout.shape, pl.BlockSpec((None, wout.shape[1], d), lambda j, i: (j, 0, 0)),
                   (n_out, t // tm), "grad_wout")

    (dq_sb, dk_sb, dv_sb), parts_late = _sb_bwd(qkv, dosb, csb, nb, seq,
                                                beside([gwout, gwg2, gwu2, gwd2], True))
    dq_r, dk_r, dv_r, dbias = _dil_bwd(qkvd, bias, _stack_residue(dodil, nb), _stack_residue(ldil, nb),
                                       _stack_residue(dldil, nb), nb, seq)
    dqkv_dil = [sum(_from_residue(a[i], nb, dil) for i, dil in enumerate(DILATIONS)).astype(BF16)
                for a in (dq_r, dk_r, dv_r)]
    dqkv = jnp.concatenate([dq_sb, dk_sb, dv_sb] + dqkv_dil, axis=1)
    drel = _bias_blocks_bwd(dbias)

    dx1, dmod2a, dg_mix = _qkv_bwd(dqkv, dx2, x1, mod, g_mix, win, tm)
    n_in, _, cs = win.shape
    gwin = _mm_tn(h2, dqkv,
                  pl.BlockSpec((tm, d), lambda j, i: (i, 0)),
                  pl.BlockSpec((tm, cs), lambda j, i: (i, j)),
                  win.shape, pl.BlockSpec((None, d, cs), lambda j, i: (j, 0, 0)),
                  (n_in, t // tm), "grad_win")

    (dx0, dgate1, dup1, act1, h1, df1, dmod1, dg_ffn1), parts_mid = _ffn_bwd(
        dx1, x0, f1, mod, g_ffn1, gate1, up1, wg1, wu1, wd1, 0, tm, beside([gwin], True))
    gwg1, gwu1, gwd1 = _ffn_weight_grads(h1, dgate1, dup1, act1, df1, tm, 0)

    dmod = jnp.concatenate([dmod1[:, 0:3], dmod2a[:, 0:2], dmod2b[:, 2:3], dmod3[:, 0:3]], axis=1)
    wgrads = (gwg1, gwu1, gwd1, gwin, gwout, gwg2, gwu2, gwd2)
    if distributed:
        wgrads = tuple(_exchange([gwg1, gwu1, gwd1], True, "scatter_grads")) + tuple(parts_mid + parts_late)
    ggrads = (dg_ffn1[0:1], dg_mix[0:1], dg_heads[0:1], drel, dg_ffn2[0:1], dg_final)
    return loss_sum, dx0.reshape(nb, seq, d), wgrads, dmod, ggrads


def kernel(x, c, w_ada, b_ada, g_ffn1, w1_gate, w1_up, w1_down, g_mix, w_in, g_sb_out, g_dil_out, w_out, rel_bias, g_ffn2, w2_gate, w2_up, w2_down, g_final, loss_target, m_w_ada, m_b_ada, m_g_ffn1, m_w1_gate, m_w1_up, m_w1_down, m_g_mix, m_w_in, m_g_sb_out, m_g_dil_out, m_w_out, m_rel_bias, m_g_ffn2, m_w2_gate, m_w2_up, m_w2_down, m_g_final, v_w_ada, v_b_ada, v_g_ffn1, v_w1_gate, v_w1_up, v_w1_down, v_g_mix, v_w_in, v_g_sb_out, v_g_dil_out, v_w_out, v_rel_bias, v_g_ffn2, v_w2_gate, v_w2_up, v_w2_down, v_g_final):
    nb, seq, d = x.shape
    me = 4 * lax.axis_index("x") + 2 * lax.axis_index("y") + lax.axis_index("c")
    tm = min(TOKEN_TILE, seq)
    fs = w1_gate.shape[2]
    fs_pad = _lane_pad(fs)
    ada_cols = w_ada.shape[2]

    def col_shard(w):
        return _pad_to(w[0].astype(BF16), 1, fs_pad)

    def row_shard(w):
        return _pad_to(w[0].astype(BF16), 0, fs_pad)

    shards = [col_shard(w1_gate), col_shard(w1_up), row_shard(w1_down), w_in[0].astype(BF16),
              w_out[0].astype(BF16), col_shard(w2_gate), col_shard(w2_up), row_shard(w2_down)]
    gathered = _exchange([_pad_to(c, 0, 8)] + shards[:3], False, "gather_first")
    c_all = gathered[0][:, :nb].reshape(N_DEV * nb, d)
    weights = gathered[1:] + shards[3:]

    b_cols = lax.dynamic_slice(b_ada, (0, me * ada_cols), (1, ada_cols))
    mod_part = _ada_fwd(c_all, w_ada[0], b_cols)
    mod_all = _exchange([mod_part], False, "gather_mod")[0]
    mod = lax.dynamic_slice(mod_all, (0, me * nb, 0), (N_DEV, nb, ada_cols))
    mod = mod.transpose(1, 0, 2).reshape(nb, N_MOD, d)

    n_sb = g_sb_out.shape[1] * g_sb_out.shape[2]
    gains = (g_ffn1, g_mix, g_sb_out.reshape(1, n_sb), g_dil_out.reshape(1, -1), g_ffn2,
             g_final.reshape(1, d))
    loss_sum, grad_x, parts, dmod, ggrads = _local_step(x, loss_target, mod, gains, weights, rel_bias, tm, True)
    loss = lax.psum(loss_sum, ("x", "y", "c"))

    dg_ffn1, dg_mix, dg_heads, drel, dg_ffn2, dg_final = ggrads
    width = max(d, dg_heads.shape[1], drel.size)
    small = jnp.concatenate(
        [_pad_to(a.reshape(1, -1), 1, width) for a in (dg_ffn1, dg_mix, dg_ffn2, dg_final, dg_heads, drel)]
        + [jnp.zeros((2, width), F32)], axis=0)
    dmod_all, small_all = _exchange([_pad_to(dmod.reshape(nb, N_MOD * d), 0, 8), small], False, "gather_small")
    dmod_all = dmod_all[:, :nb].reshape(N_DEV * nb, N_MOD * d)
    dmod_cols = lax.dynamic_slice(dmod_all, (0, me * ada_cols), (N_DEV * nb, ada_cols))
    gw_ada, gb_ada = _ada_bwd(c_all, dmod_cols, dmod_all)

    def small_part(row, size, shape):
        return small_all[:, row, :size].reshape((N_DEV,) + shape)

    n_rel = rel_bias.shape
    updates = {
        "w_ada": (w_ada[0], gw_ada[None], m_w_ada[0], v_w_ada[0]),
        "b_ada": (b_ada, gb_ada[None], m_b_ada, v_b_ada),
        "g_ffn1": (g_ffn1, small_part(0, d, (1, d)), m_g_ffn1, v_g_ffn1),
        "w1_gate": (w1_gate[0], parts[0][:, :, :fs], m_w1_gate[0], v_w1_gate[0]),
        "w1_up": (w1_up[0], parts[1][:, :, :fs], m_w1_up[0], v_w1_up[0]),
        "w1_down": (w1_down[0], parts[2][:, :fs, :], m_w1_down[0], v_w1_down[0]),
        "g_mix": (g_mix, small_part(1, d, (1, d)), m_g_mix, v_g_mix),
        "w_in": (w_in[0], parts[3], m_w_in[0], v_w_in[0]),
        "g_sb_out": (g_sb_out[0], small_all[:, 4, :n_sb].reshape((N_DEV,) + g_sb_out.shape[1:]),
                     m_g_sb_out[0], v_g_sb_out[0]),
        "g_dil_out": (g_dil_out[0], small_all[:, 4, n_sb:dg_heads.shape[1]].reshape((N_DEV,) + g_dil_out.shape[1:]),
                      m_g_dil_out[0], v_g_dil_out[0]),
        "w_out": (w_out[0], parts[4], m_w_out[0], v_w_out[0]),
        "rel_bias": (rel_bias, small_part(5, drel.size, n_rel), m_rel_bias, v_rel_bias),
        "g_ffn2": (g_ffn2, small_part(2, d, (1, d)), m_g_ffn2, v_g_ffn2),
        "w2_gate": (w2_gate[0], parts[5][:, :, :fs], m_w2_gate[0], v_w2_gate[0]),
        "w2_up": (w2_up[0], parts[6][:, :, :fs], m_w2_up[0], v_w2_up[0]),
        "w2_down": (w2_down[0], parts[7][:, :fs, :], m_w2_down[0], v_w2_down[0]),
        "g_final": (g_final.reshape(1, d), small_part(3, d, (1, d)), m_g_final.reshape(1, d), v_g_final.reshape(1, d)),
    }
    shapes = {"w_ada": w_ada.shape, "b_ada": b_ada.shape, "g_ffn1": g_ffn1.shape, "w1_gate": w1_gate.shape,
              "w1_up": w1_up.shape, "w1_down": w1_down.shape, "g_mix": g_mix.shape, "w_in": w_in.shape,
              "g_sb_out": g_sb_out.shape, "g_dil_out": g_dil_out.shape, "w_out": w_out.shape,
              "rel_bias": rel_bias.shape, "g_ffn2": g_ffn2.shape, "w2_gate": w2_gate.shape,
              "w2_up": w2_up.shape, "w2_down": w2_down.shape, "g_final": g_final.shape}
    grads, deltas, new_m, new_v = [], [], [], []
    for name, (w, p, m, v) in updates.items():
        g, dw, nm, nv = _adamw(w, p, m, v, f"adamw_{name}")
        grads.append(g.reshape(shapes[name]))
        deltas.append(dw.reshape(shapes[name]))
        new_m.append(nm.reshape(shapes[name]))
        new_v.append(nv.reshape(shapes[name]))
    return (loss, grad_x, *grads, *deltas, *new_m, *new_v)
```

```python
import functools
import math

import numpy as np
import jax
import jax.numpy as jnp
from jax import lax
from jax.experimental import pallas as pl
from jax.experimental.pallas import tpu as pltpu

F32 = jnp.float32
BF16 = jnp.bfloat16

EPS = 1e-6
NEG_INF = -1e30
HEAD_DIM = 64
LANES = 128
DIL_BLOCK = 128
DILATIONS = (1, 4, 16)
N_BUCKETS = 32
MAX_DISTANCE = 2048
N_MOD = 9
N_DEV = 8
SB_BLOCK = 256
SB_HEADS = 4
SB_WIDTH = SB_HEADS * HEAD_DIM
DIL_HEADS = 4
DIL_WIDTH = DIL_HEADS * HEAD_DIM
TOKEN_TILE = 512
GRAD_TILE = 1024
VMEM_LIMIT_BYTES = 48 * 1024 * 1024

ADAM_LR = 0.001
ADAM_B1 = 0.9
ADAM_B2 = 0.999
ADAM_EPS = 1e-08
ADAM_WD = 0.01
ADAM_STEP = 10

NT_DIMS = (((1,), (1,)), ((), ()))
TN_DIMS = (((0,), (0,)), ((), ()))


def _params(*sem):
    return pltpu.CompilerParams(dimension_semantics=sem, vmem_limit_bytes=VMEM_LIMIT_BYTES)


def _dot(a, b):
    return jnp.dot(a, b, preferred_element_type=F32)


def _dot_nt(a, b):
    return lax.dot_general(a, b, NT_DIMS, preferred_element_type=F32)


def _dot_tn(a, b):
    return lax.dot_general(a, b, TN_DIMS, preferred_element_type=F32)


def _split_dot(a, b):
    hi = a.astype(BF16)
    lo = (a - hi.astype(F32)).astype(BF16)
    return _dot(hi, b) + _dot(lo, b)


def _sigmoid(z):
    return 1.0 / (1.0 + jnp.exp(-z))


def _norm(x):
    r = lax.rsqrt(jnp.mean(x * x, axis=-1, keepdims=True) + EPS)
    return x * r, r


def _modulate(x, g, mod_ref, k):
    n, _ = _norm(x)
    shift = mod_ref[3 * k:3 * k + 1, :]
    scale = mod_ref[3 * k + 1:3 * k + 2, :]
    return n * g * (1.0 + scale) + shift


def _modulate_bwd(dh, x, g, mod_ref, k):
    n, r = _norm(x)
    scale = mod_ref[3 * k + 1:3 * k + 2, :]
    dshift = jnp.sum(dh, axis=0, keepdims=True)
    dscale = jnp.sum(dh * n * g, axis=0, keepdims=True)
    dg = jnp.sum(dh * n * (1.0 + scale), axis=0, keepdims=True)
    dn = dh * g * (1.0 + scale)
    dx = r * (dn - n * jnp.mean(dn * n, axis=-1, keepdims=True))
    return dx, dshift, dscale, dg


class _Exchange:
    def __init__(self, arrays, scatter):
        self.arrays = list(arrays)
        self.scatter = scatter
        self.n = len(self.arrays)
        self.out_shape = [
            jax.ShapeDtypeStruct((N_DEV,) + tuple(a.shape[1:] if scatter else a.shape), a.dtype)
            for a in self.arrays]
        n_remote = self.n * (N_DEV - 1)
        self.scratch_shapes = [pltpu.SemaphoreType.DMA((n_remote,)), pltpu.SemaphoreType.DMA((n_remote,)),
                               pltpu.SemaphoreType.DMA((self.n,))]

    def _copies(self, in_refs, out_refs, sems):
        send_sems, recv_sems, local_sems = sems
        x, y, c = lax.axis_index("x"), lax.axis_index("y"), lax.axis_index("c")
        me = 4 * x + 2 * y + c
        local, remote = [], []
        for a in range(self.n):
            src = in_refs[a].at[me] if self.scatter else in_refs[a]
            local.append(pltpu.make_async_copy(src, out_refs[a].at[me], local_sems.at[a]))
            for k in range(1, N_DEV):
                px = 1 - x if k & 4 else x
                py = 1 - y if k & 2 else y
                pc = 1 - c if k & 1 else c
                src = in_refs[a].at[4 * px + 2 * py + pc] if self.scatter else in_refs[a]
                sem = a * (N_DEV - 1) + k - 1
                remote.append(pltpu.make_async_remote_copy(
                    src_ref=src, dst_ref=out_refs[a].at[me],
                    send_sem=send_sems.at[sem], recv_sem=recv_sems.at[sem],
                    device_id=(px, py, pc), device_id_type=pl.DeviceIdType.MESH))
        return local, remote

    def start(self, in_refs, out_refs, sems):
        local, remote = self._copies(in_refs, out_refs, sems)
        for cp in local + remote:
            cp.start()

    def wait(self, in_refs, out_refs, sems):
        local, remote = self._copies(in_refs, out_refs, sems)
        for cp in remote:
            cp.wait_recv()
        for cp in remote:
            cp.wait_send()
        for cp in local:
            cp.wait()


def _call(body, *, name, args, in_specs, out_specs, out_shape, scratch_shapes=(), grid=(),
          params=None, exchange=None):
    n_in, n_out = len(args), len(out_shape)
    if exchange is None:
        outs = pl.pallas_call(
            body, name=name, grid=grid, in_specs=list(in_specs), out_specs=list(out_specs),
            out_shape=list(out_shape), scratch_shapes=list(scratch_shapes), compiler_params=params,
        )(*args)
        return list(outs), []
    n_ex = exchange.n

    def wrapped(*refs):
        ins, refs = refs[:n_in], refs[n_in:]
        ex_in, refs = refs[:n_ex], refs[n_ex:]
        outs, refs = refs[:n_out], refs[n_out:]
        ex_out, refs = refs[:n_ex], refs[n_ex:]
        scratch, sems = refs[:len(refs) - 3], refs[len(refs) - 3:]
        if not grid:
            exchange.start(ex_in, ex_out, sems)
            body(*ins, *outs, *scratch)
            exchange.wait(ex_in, ex_out, sems)
            return
        first = functools.reduce(jnp.logical_and, [pl.program_id(a) == 0 for a in range(len(grid))])
        last = functools.reduce(jnp.logical_and, [pl.program_id(a) == grid[a] - 1 for a in range(len(grid))])

        @pl.when(first)
        def _():
            exchange.start(ex_in, ex_out, sems)

        body(*ins, *outs, *scratch)

        @pl.when(last)
        def _():
            exchange.wait(ex_in, ex_out, sems)

    any_spec = pl.BlockSpec(memory_space=pl.ANY)
    outs = pl.pallas_call(
        wrapped, name=name, grid=grid,
        in_specs=list(in_specs) + [any_spec] * n_ex, out_specs=list(out_specs) + [any_spec] * n_ex,
        out_shape=list(out_shape) + exchange.out_shape,
        scratch_shapes=list(scratch_shapes) + exchange.scratch_shapes, compiler_params=params,
    )(*args, *exchange.arrays)
    return list(outs[:n_out]), list(outs[n_out:])


def _exchange(arrays, scatter, name):
    return _call(lambda: None, name=name, args=(), in_specs=(), out_specs=(), out_shape=(),
                 exchange=_Exchange(arrays, scatter))[1]


def _ada_fwd(c_all, w, b):
    def body(c_ref, w_ref, b_ref, o_ref):
        cv = c_ref[...]
        s = (cv * _sigmoid(cv)).astype(BF16)
        o_ref[...] = _dot(s, w_ref[...].astype(BF16)) + b_ref[...]

    return pl.pallas_call(
        body, name="ada_fwd", out_shape=jax.ShapeDtypeStruct((c_all.shape[0], w.shape[1]), F32),
        compiler_params=pltpu.CompilerParams(vmem_limit_bytes=VMEM_LIMIT_BYTES),
    )(c_all, w, b)


def _ada_bwd(c_all, dmod_cols, dmod_all):
    def body(c_ref, dc_ref, da_ref, gw_ref, gb_ref):
        cv = c_ref[...]
        s = cv * _sigmoid(cv)
        gw_ref[...] = lax.dot_general(s, dc_ref[...], TN_DIMS, preferred_element_type=F32,
                                      precision=lax.Precision.HIGHEST)
        gb_ref[...] = jnp.sum(da_ref[...], axis=0, keepdims=True)

    return pl.pallas_call(
        body, name="ada_bwd",
        out_shape=(jax.ShapeDtypeStruct((c_all.shape[1], dmod_cols.shape[1]), F32),
                   jax.ShapeDtypeStruct((1, dmod_all.shape[1]), F32)),
        compiler_params=pltpu.CompilerParams(vmem_limit_bytes=VMEM_LIMIT_BYTES),
    )(c_all, dmod_cols, dmod_all)


def _ffn_fwd(x, mod, g, wg, wu, wd, k, tm, exchange=None):
    t, d = x.shape
    ns, _, fs = wg.shape
    nt = t // tm
    tpb = nt // mod.shape[0]

    def body(x_ref, mod_ref, g_ref, wg_ref, wu_ref, wd_ref, xo_ref, f_ref, gg_ref, uu_ref, h_sc, acc):
        j = pl.program_id(1)

        @pl.when(j == 0)
        def _():
            h_sc[...] = _modulate(x_ref[...], g_ref[...], mod_ref, k).astype(BF16)
            acc[...] = jnp.zeros_like(acc)

        h = h_sc[...]
        gate = _dot(h, wg_ref[...])
        up = _dot(h, wu_ref[...])
        act = gate * _sigmoid(gate) * up
        acc[...] += _dot(act.astype(BF16), wd_ref[...])
        gg_ref[...] = gate
        uu_ref[...] = up

        @pl.when(j == ns - 1)
        def _():
            f = acc[...]
            f_ref[...] = f
            xo_ref[...] = x_ref[...] + 0.5 * mod_ref[3 * k + 2:3 * k + 3, :] * f

    tok = pl.BlockSpec((tm, d), lambda i, j: (i, 0))
    hid = pl.BlockSpec((None, tm, fs), lambda i, j: (j, i, 0))
    return _call(
        body, name=f"ffn_fwd{k}", grid=(nt, ns), args=(x, mod, g, wg, wu, wd),
        in_specs=[tok,
                  pl.BlockSpec((None, N_MOD, d), lambda i, j: (i // tpb, 0, 0)),
                  pl.BlockSpec((1, d), lambda i, j: (0, 0)),
                  pl.BlockSpec((None, d, fs), lambda i, j: (j, 0, 0)),
                  pl.BlockSpec((None, d, fs), lambda i, j: (j, 0, 0)),
                  pl.BlockSpec((None, fs, d), lambda i, j: (j, 0, 0))],
        out_specs=[tok, tok, hid, hid],
        out_shape=[jax.ShapeDtypeStruct((t, d), F32), jax.ShapeDtypeStruct((t, d), F32),
                   jax.ShapeDtypeStruct((ns, t, fs), F32), jax.ShapeDtypeStruct((ns, t, fs), F32)],
        scratch_shapes=[pltpu.VMEM((tm, d), BF16), pltpu.VMEM((tm, d), F32)],
        params=_params("arbitrary", "arbitrary"), exchange=exchange)


def _ffn_bwd(dxo, x, f, mod, g, gate, up, wg, wu, wd, k, tm, exchange=None):
    t, d = x.shape
    ns, _, fs = wg.shape
    nt = t // tm
    nb = mod.shape[0]
    tpb = nt // nb

    def body(dxo_ref, x_ref, f_ref, mod_ref, g_ref, gg_ref, uu_ref, wg_ref, wu_ref, wd_ref,
             dx_ref, dgg_ref, duu_ref, act_ref, h_ref, df_ref, dmod_ref, dg_ref, acc):
        i, j = pl.program_id(0), pl.program_id(1)

        @pl.when(j == 0)
        def _():
            df = 0.5 * mod_ref[3 * k + 2:3 * k + 3, :] * dxo_ref[...]
            df_ref[...] = df.astype(BF16)
            h_ref[...] = _modulate(x_ref[...], g_ref[...], mod_ref, k).astype(BF16)
            acc[...] = jnp.zeros_like(acc)

        dact = _dot_nt(df_ref[...], wd_ref[...])
        gv, uv = gg_ref[...], uu_ref[...]
        sig = _sigmoid(gv)
        s = gv * sig
        act_ref[...] = (s * uv).astype(BF16)
        dup = (dact * s).astype(BF16)
        dgate = (dact * uv * (sig * (1.0 + gv * (1.0 - sig)))).astype(BF16)
        duu_ref[...] = dup
        dgg_ref[...] = dgate
        acc[...] += _dot_nt(dgate, wg_ref[...]) + _dot_nt(dup, wu_ref[...])

        @pl.when(j == ns - 1)
        def _():
            dx, dshift, dscale, dg = _modulate_bwd(acc[...], x_ref[...], g_ref[...], mod_ref, k)
            dxo_v = dxo_ref[...]
            dx_ref[...] = dxo_v + dx
            dgt = jnp.sum(0.5 * f_ref[...] * dxo_v, axis=0, keepdims=True)

            @pl.when(i % tpb == 0)
            def _():
                dmod_ref[...] = jnp.zeros_like(dmod_ref)

            @pl.when(i == 0)
            def _():
                dg_ref[...] = jnp.zeros_like(dg_ref)

            dmod_ref[0:1, :] += dshift
            dmod_ref[1:2, :] += dscale
            dmod_ref[2:3, :] += dgt
            dg_ref[0:1, :] += dg

    tok = pl.BlockSpec((tm, d), lambda i, j: (i, 0))
    hid = pl.BlockSpec((None, tm, fs), lambda i, j: (j, i, 0))
    return _call(
        body, name=f"ffn_bwd{k}", grid=(nt, ns), args=(dxo, x, f, mod, g, gate, up, wg, wu, wd),
        in_specs=[tok, tok, tok,
                  pl.BlockSpec((None, N_MOD, d), lambda i, j: (i // tpb, 0, 0)),
                  pl.BlockSpec((1, d), lambda i, j: (0, 0)),
                  hid, hid,
                  pl.BlockSpec((None, d, fs), lambda i, j: (j, 0, 0)),
                  pl.BlockSpec((None, d, fs), lambda i, j: (j, 0, 0)),
                  pl.BlockSpec((None, fs, d), lambda i, j: (j, 0, 0))],
        out_specs=[tok, hid, hid, hid, tok, tok,
                   pl.BlockSpec((None, 8, d), lambda i, j: (i // tpb, 0, 0)),
                   pl.BlockSpec((8, d), lambda i, j: (0, 0))],
        out_shape=[jax.ShapeDtypeStruct((t, d), F32),
                   jax.ShapeDtypeStruct((ns, t, fs), BF16), jax.ShapeDtypeStruct((ns, t, fs), BF16),
                   jax.ShapeDtypeStruct((ns, t, fs), BF16),
                   jax.ShapeDtypeStruct((t, d), BF16), jax.ShapeDtypeStruct((t, d), BF16),
                   jax.ShapeDtypeStruct((nb, 8, d), F32), jax.ShapeDtypeStruct((8, d), F32)],
        scratch_shapes=[pltpu.VMEM((tm, d), F32)],
        params=_params("arbitrary", "arbitrary"), exchange=exchange)


def _mm_tn(a, b, a_spec, b_spec, out_shape, out_spec, grid, name):
    block = tuple(out_shape[1:])
    last = grid[1] - 1
    flip = block[0] > block[1]
    if flip:
        block = block[::-1]

    def body(a_ref, b_ref, o_ref, acc):
        i = pl.program_id(1)

        @pl.when(i == 0)
        def _():
            acc[...] = jnp.zeros_like(acc)

        if flip:
            acc[...] += _dot_tn(b_ref[...], a_ref[...])
        else:
            acc[...] += _dot_tn(a_ref[...], b_ref[...])

        @pl.when(i == last)
        def _():
            total = acc[...]
            o_ref[...] = (total.T if flip else total).astype(o_ref.dtype)

    return pl.pallas_call(
        body, name=name, grid=grid, in_specs=[a_spec, b_spec], out_specs=out_spec,
        out_shape=jax.ShapeDtypeStruct(out_shape, BF16),
        scratch_shapes=[pltpu.VMEM(block, F32)],
        compiler_params=_params("arbitrary", "arbitrary"),
    )(a, b)


def _ffn_weight_grads(h, dgate, dup, act, df, tm, tag):
    t, d = h.shape
    ns, _, fs = dgate.shape
    grid = (ns, t // tm)
    tok = pl.BlockSpec((tm, d), lambda j, i: (i, 0))
    hid = pl.BlockSpec((None, tm, fs), lambda j, i: (j, i, 0))
    col = pl.BlockSpec((None, d, fs), lambda j, i: (j, 0, 0))
    row = pl.BlockSpec((None, fs, d), lambda j, i: (j, 0, 0))
    gwg = _mm_tn(h, dgate, tok, hid, (ns, d, fs), col, grid, f"grad_wg{tag}")
    gwu = _mm_tn(h, dup, tok, hid, (ns, d, fs), col, grid, f"grad_wu{tag}")
    gwd = _mm_tn(act, df, hid, tok, (ns, fs, d), row, grid, f"grad_wd{tag}")
    return gwg, gwu, gwd


def _qkv_fwd(x, mod, g, win, tm):
    t, d = x.shape
    ns, _, cs = win.shape
    nt = t // tm
    tpb = nt // mod.shape[0]

    def body(x_ref, mod_ref, g_ref, w_ref, qkv_ref, h_ref):
        @pl.when(pl.program_id(1) == 0)
        def _():
            h_ref[...] = _modulate(x_ref[...], g_ref[...], mod_ref, 1).astype(BF16)

        qkv_ref[...] = _dot(h_ref[...], w_ref[...]).astype(BF16)

    tok = pl.BlockSpec((tm, d), lambda i, j: (i, 0))
    return pl.pallas_call(
        body, name="qkv_fwd", grid=(nt, ns),
        in_specs=[tok,
                  pl.BlockSpec((None, N_MOD, d), lambda i, j: (i // tpb, 0, 0)),
                  pl.BlockSpec((1, d), lambda i, j: (0, 0)),
                  pl.BlockSpec((None, d, cs), lambda i, j: (j, 0, 0))],
        out_specs=[pl.BlockSpec((tm, cs), lambda i, j: (i, j)), tok],
        out_shape=[jax.ShapeDtypeStruct((t, ns * cs), BF16), jax.ShapeDtypeStruct((t, d), BF16)],
        compiler_params=_params("arbitrary", "arbitrary"),
    )(x, mod, g, win)


def _qkv_bwd(dqkv, dxo, x, mod, g, win, tm):
    t, d = x.shape
    ns, _, cs = win.shape
    nt = t // tm
    nb = mod.shape[0]
    tpb = nt // nb

    def body(dq_ref, dxo_ref, x_ref, mod_ref, g_ref, w_ref, dx_ref, dmod_ref, dg_ref, acc):
        i, j = pl.program_id(0), pl.program_id(1)

        @pl.when(j == 0)
        def _():
            acc[...] = jnp.zeros_like(acc)

        acc[...] += _dot_nt(dq_ref[...], w_ref[...])

        @pl.when(j == ns - 1)
        def _():
            dx, dshift, dscale, dg = _modulate_bwd(acc[...], x_ref[...], g_ref[...], mod_ref, 1)
            dx_ref[...] = dxo_ref[...] + dx

            @pl.when(i % tpb == 0)
            def _():
                dmod_ref[...] = jnp.zeros_like(dmod_ref)

            @pl.when(i == 0)
            def _():
                dg_ref[...] = jnp.zeros_like(dg_ref)

            dmod_ref[0:1, :] += dshift
            dmod_ref[1:2, :] += dscale
            dg_ref[0:1, :] += dg

    tok = pl.BlockSpec((tm, d), lambda i, j: (i, 0))
    return pl.pallas_call(
        body, name="qkv_bwd", grid=(nt, ns),
        in_specs=[pl.BlockSpec((tm, cs), lambda i, j: (i, j)), tok, tok,
                  pl.BlockSpec((None, N_MOD, d), lambda i, j: (i // tpb, 0, 0)),
                  pl.BlockSpec((1, d), lambda i, j: (0, 0)),
                  pl.BlockSpec((None, d, cs), lambda i, j: (j, 0, 0))],
        out_specs=[tok,
                   pl.BlockSpec((None, 8, d), lambda i, j: (i // tpb, 0, 0)),
                   pl.BlockSpec((8, d), lambda i, j: (0, 0))],
        out_shape=[jax.ShapeDtypeStruct((t, d), F32),
                   jax.ShapeDtypeStruct((nb, 8, d), F32), jax.ShapeDtypeStruct((8, d), F32)],
        scratch_shapes=[pltpu.VMEM((tm, d), F32)],
        compiler_params=_params("arbitrary", "arbitrary"),
    )(dqkv, dxo, x, mod, g, win)


def _heads(a):
    return [a[:, h * HEAD_DIM:(h + 1) * HEAD_DIM] for h in range(a.shape[1] // HEAD_DIM)]


def _sb_logits(qh, kh, tri, causal):
    zs = [_dot_nt(q, k) * (HEAD_DIM ** -0.5) for q, k in zip(qh, kh)]
    es = [jnp.exp(-jnp.abs(z)) for z in zs]
    log_nots = [-(jnp.maximum(z, 0.0) + jnp.log(1.0 + e)) for z, e in zip(zs, es)]
    if causal is not None:
        log_nots = [jnp.where(causal, ln, 0.0) for ln in log_nots]
    return zs, es, [_split_dot(ln, tri) for ln in log_nots]


def _sb_masks():
    rows = lax.broadcasted_iota(jnp.int32, (SB_BLOCK, SB_BLOCK), 0)
    cols = lax.broadcasted_iota(jnp.int32, (SB_BLOCK, SB_BLOCK), 1)
    return (rows >= cols).astype(BF16), (rows <= cols).astype(BF16), cols < rows


def _sb_fwd(qkv, nb, seq, exchange=None):
    t = qkv.shape[0]
    n_pairs = (qkv.shape[1] // 6) // SB_WIDTH
    tb = SB_BLOCK
    n_blk = seq // tb

    def body(q_ref, k_ref, v_ref, o_ref, c_ref):
        tri, _, causal = _sb_masks()

        def key_block(qh, kj, carry, mask):
            ks = pl.multiple_of(kj * tb, tb)
            kh, vh = _heads(k_ref[pl.ds(ks, tb), :]), _heads(v_ref[pl.ds(ks, tb), :])
            zs, _, suffixes = _sb_logits(qh, kh, tri, mask)
            ws = [jnp.exp(z + suffix + cr[1]) for z, suffix, cr in zip(zs, suffixes, carry)]
            if mask is not None:
                ws = [jnp.where(mask, w, 0.0) for w in ws]
            pv = [_dot(w.astype(BF16), v) for w, v in zip(ws, vh)]
            return tuple((cr[0] + p, cr[1] + suffix[:, 0:1]) for cr, p, suffix in zip(carry, pv, suffixes))

        def query_block(qi, _):
            qs = pl.multiple_of(qi * tb, tb)
            qh = _heads(q_ref[pl.ds(qs, tb), :])
            zero = (jnp.zeros((tb, HEAD_DIM), F32), jnp.zeros((tb, 1), F32))
            carry = key_block(qh, qi, (zero,) * SB_HEADS, causal)
            carry = lax.fori_loop(0, qi, lambda it, cr: key_block(qh, qi - 1 - it, cr, None), carry)
            o_ref[pl.ds(qs, tb), :] = jnp.concatenate([cr[0] for cr in carry], axis=1)
            c_ref[pl.ds(qs, tb), :] = jnp.concatenate(
                [jnp.broadcast_to(cr[1], (tb, HEAD_DIM)) for cr in carry], axis=1)
            return 0

        lax.fori_loop(0, n_blk, query_block, 0)

    def spec(offset):
        return pl.BlockSpec((seq, SB_WIDTH), lambda b, p: (b, offset + p))

    out = jax.ShapeDtypeStruct((t, n_pairs * SB_WIDTH), F32)
    return _call(
        body, name="sb_fwd", grid=(nb, n_pairs), args=(qkv, qkv, qkv),
        in_specs=[spec(0), spec(n_pairs), spec(2 * n_pairs)],
        out_specs=[spec(0), spec(0)], out_shape=[out, out],
        params=_params("arbitrary", "arbitrary"), exchange=exchange)


def _sb_bwd(qkv, do, csum, nb, seq, exchange=None):
    t = qkv.shape[0]
    n_pairs = (qkv.shape[1] // 6) // SB_WIDTH
    tb = SB_BLOCK
    n_blk = seq // tb
    scale = HEAD_DIM ** -0.5

    def body(q_ref, k_ref, v_ref, do_ref, c_ref, dq_ref, dk_ref, dv_ref, dk_acc, dv_acc):
        tri, tri_prefix, causal = _sb_masks()
        dk_acc[...] = jnp.zeros_like(dk_acc)
        dv_acc[...] = jnp.zeros_like(dv_acc)

        def key_block(qh, doh, ch, kj, carry, mask):
            ks = pl.multiple_of(kj * tb, tb)
            kh, vh = _heads(k_ref[pl.ds(ks, tb), :]), _heads(v_ref[pl.ds(ks, tb), :])
            heads = range(SB_HEADS)
            zs, es, suffixes = _sb_logits(qh, kh, tri, mask)
            dws = [_dot_nt(doh[h], vh[h]) for h in heads]
            lefts = [carry[h][1] + suffixes[h][:, 0:1] for h in heads]
            ws = [jnp.exp(zs[h] + suffixes[h] + (ch[h] - lefts[h])) for h in heads]
            if mask is not None:
                ws = [jnp.where(mask, w, 0.0) for w in ws]
            dlws = [ws[h] * dws[h] for h in heads]
            dprefixes = [_split_dot(dlw, tri_prefix) for dlw in dlws]
            dvs = [_dot_tn(ws[h].astype(BF16), doh[h]) for h in heads]
            dzbs = []
            for h in heads:
                sig = jnp.where(zs[h] >= 0.0, 1.0, es[h]) / (1.0 + es[h])
                dz = dlws[h] * (1.0 - sig) - (carry[h][2] + dprefixes[h] - dlws[h]) * sig
                if mask is not None:
                    dz = jnp.where(mask, dz, 0.0)
                dzbs.append((dz * scale).astype(BF16))
            dks = [_dot_tn(dzbs[h], qh[h]) for h in heads]
            dqs = [_dot(dzbs[h], kh[h]) for h in heads]
            dk_acc[pl.ds(ks, tb), :] += jnp.concatenate(dks, axis=1)
            dv_acc[pl.ds(ks, tb), :] += jnp.concatenate(dvs, axis=1)
            return tuple((carry[h][0] + dqs[h], lefts[h], carry[h][2] + dprefixes[h][:, tb - 1:tb])
                         for h in heads)

        def query_block(qi, _):
            qs = pl.multiple_of(qi * tb, tb)
            qh = _heads(q_ref[pl.ds(qs, tb), :])
            doh = _heads(do_ref[pl.ds(qs, tb), :].astype(BF16))
            cv = c_ref[pl.ds(qs, tb), :]
            ch = [cv[:, h * HEAD_DIM:h * HEAD_DIM + 1] for h in range(SB_HEADS)]
            zero = (jnp.zeros((tb, HEAD_DIM), F32), jnp.zeros((tb, 1), F32), jnp.zeros((tb, 1), F32))
            carry = lax.fori_loop(
                0, qi, lambda kj, cr: key_block(qh, doh, ch, kj, cr, None), (zero,) * SB_HEADS)
            carry = key_block(qh, doh, ch, qi, carry, causal)
            dq_ref[pl.ds(qs, tb), :] = jnp.concatenate([cr[0] for cr in carry], axis=1).astype(BF16)
            return 0

        lax.fori_loop(0, n_blk, query_block, 0)
        dk_ref[...] = dk_acc[...].astype(BF16)
        dv_ref[...] = dv_acc[...].astype(BF16)

    def spec(offset):
        return pl.BlockSpec((seq, SB_WIDTH), lambda b, p: (b, offset + p))

    out = jax.ShapeDtypeStruct((t, n_pairs * SB_WIDTH), BF16)
    return _call(
        body, name="sb_bwd", grid=(nb, n_pairs), args=(qkv, qkv, qkv, do, csum),
        in_specs=[spec(0), spec(n_pairs), spec(2 * n_pairs), spec(0), spec(0)],
        out_specs=[spec(0), spec(0), spec(0)],
        out_shape=[out, out, out],
        scratch_shapes=[pltpu.VMEM((seq, SB_WIDTH), F32), pltpu.VMEM((seq, SB_WIDTH), F32)],
        params=_params("arbitrary", "arbitrary"), exchange=exchange)


def _dil_block_scores(qh, kph, kch, bias_ref, has_prev, band_prev, band_cur):
    scale = HEAD_DIM ** -0.5
    heads = range(len(qh))
    no_prev = jnp.where(has_prev, 0.0, NEG_INF)
    zps = [_dot_nt(qh[h], kph[h]) for h in heads]
    zcs = [_dot_nt(qh[h], kch[h]) for h in heads]
    zps = [jnp.where(band_prev, zps[h] * scale + bias_ref[h, :, 0:DIL_BLOCK], NEG_INF) + no_prev for h in heads]
    zcs = [jnp.where(band_cur, zcs[h] * scale + bias_ref[h, :, DIL_BLOCK:2 * DIL_BLOCK], NEG_INF) for h in heads]
    return zps, zcs


def _dil_bands():
    rows = lax.broadcasted_iota(jnp.int32, (DIL_BLOCK, DIL_BLOCK), 0)
    cols = lax.broadcasted_iota(jnp.int32, (DIL_BLOCK, DIL_BLOCK), 1)
    return cols >= rows, cols <= rows


def _dil_blocks_per_seq(cfg, n_blk):
    per_seq = jnp.int32(n_blk // DILATIONS[0])
    for i, dil in enumerate(DILATIONS[1:], 1):
        per_seq = jnp.where(cfg == i, n_blk // dil, per_seq)
    return per_seq


def _dil_fwd(qkvd, bias, nb, seq):
    n_cfg, t, width = qkvd.shape
    n_pairs = (width // 3) // DIL_WIDTH
    bq = DIL_BLOCK
    n_blk = seq // bq
    heads = range(DIL_HEADS)

    def body(q_ref, k_ref, v_ref, bias_ref, o_ref, lse_ref):
        per_seq = _dil_blocks_per_seq(pl.program_id(0), n_blk)
        band_prev, band_cur = _dil_bands()

        def block(n, _):
            has_prev = (n & (per_seq - 1)) != 0
            qs = pl.multiple_of(n * bq, bq)
            ps = pl.multiple_of(jnp.maximum(n - 1, 0) * bq, bq)
            qh = _heads(q_ref[pl.ds(qs, bq), :])
            kp, kc = _heads(k_ref[pl.ds(ps, bq), :]), _heads(k_ref[pl.ds(qs, bq), :])
            vp, vc = _heads(v_ref[pl.ds(ps, bq), :]), _heads(v_ref[pl.ds(qs, bq), :])
            zps, zcs = _dil_block_scores(qh, kp, kc, bias_ref, has_prev, band_prev, band_cur)
            ms = [jnp.maximum(jnp.max(zps[h], axis=1, keepdims=True), jnp.max(zcs[h], axis=1, keepdims=True))
                  for h in heads]
            eps = [jnp.exp(zps[h] - ms[h]) for h in heads]
            ecs = [jnp.exp(zcs[h] - ms[h]) for h in heads]
            pvs = [_dot(eps[h].astype(BF16), vp[h]) + _dot(ecs[h].astype(BF16), vc[h]) for h in heads]
            dens = [jnp.sum(eps[h], axis=1, keepdims=True) + jnp.sum(ecs[h], axis=1, keepdims=True) for h in heads]
            o_ref[pl.ds(qs, bq), :] = jnp.concatenate([pvs[h] / dens[h] for h in heads], axis=1)
            lse_ref[pl.ds(qs, bq), :] = jnp.concatenate(
                [jnp.broadcast_to(ms[h] + jnp.log(dens[h]), (bq, HEAD_DIM)) for h in heads], axis=1)
            return 0

        lax.fori_loop(0, n_blk, block, 0)

    def spec(offset):
        return pl.BlockSpec((None, seq, DIL_WIDTH), lambda g, b, p: (g, b, offset + p))

    out = jax.ShapeDtypeStruct((n_cfg, t, n_pairs * DIL_WIDTH), F32)
    return pl.pallas_call(
        body, name="dil_fwd", grid=(n_cfg, nb, n_pairs),
        in_specs=[spec(0), spec(n_pairs), spec(2 * n_pairs),
                  pl.BlockSpec((None, DIL_HEADS, bq, 2 * bq), lambda g, b, p: (g, p, 0, 0))],
        out_specs=[spec(0), spec(0)], out_shape=[out, out],
        compiler_params=_params("arbitrary", "arbitrary", "arbitrary"),
    )(qkvd, qkvd, qkvd, bias)


def _dil_bwd(qkvd, bias, do, lse, delta, nb, seq):
    n_cfg, t, width = qkvd.shape
    n_pairs = (width // 3) // DIL_WIDTH
    bq = DIL_BLOCK
    n_blk = seq // bq
    scale = HEAD_DIM ** -0.5
    heads = range(DIL_HEADS)

    def body(q_ref, k_ref, v_ref, bias_ref, do_ref, lse_ref, dl_ref, dq_ref, dk_ref, dv_ref, db_ref):
        per_seq = _dil_blocks_per_seq(pl.program_id(0), n_blk)
        band_prev, band_cur = _dil_bands()
        dk_ref[...] = jnp.zeros_like(dk_ref)
        dv_ref[...] = jnp.zeros_like(dv_ref)

        @pl.when(pl.program_id(2) == 0)
        def _():
            db_ref[...] = jnp.zeros_like(db_ref)

        def block(n, _):
            has_prev = (n & (per_seq - 1)) != 0
            qs = pl.multiple_of(n * bq, bq)
            ps = pl.multiple_of(jnp.maximum(n - 1, 0) * bq, bq)
            qh = _heads(q_ref[pl.ds(qs, bq), :])
            kp, kc = _heads(k_ref[pl.ds(ps, bq), :]), _heads(k_ref[pl.ds(qs, bq), :])
            vp, vc = _heads(v_ref[pl.ds(ps, bq), :]), _heads(v_ref[pl.ds(qs, bq), :])
            doh = _heads(do_ref[pl.ds(qs, bq), :].astype(BF16))
            lse_v, dl_v = lse_ref[pl.ds(qs, bq), :], dl_ref[pl.ds(qs, bq), :]
            zps, zcs = _dil_block_scores(qh, kp, kc, bias_ref, has_prev, band_prev, band_cur)
            dpp = [_dot_nt(doh[h], vp[h]) for h in heads]
            dpc = [_dot_nt(doh[h], vc[h]) for h in heads]
            lse_h = [lse_v[:, h * HEAD_DIM:h * HEAD_DIM + 1] for h in heads]
            dl_h = [dl_v[:, h * HEAD_DIM:h * HEAD_DIM + 1] for h in heads]
            pps = [jnp.exp(zps[h] - lse_h[h]) for h in heads]
            pcs = [jnp.exp(zcs[h] - lse_h[h]) for h in heads]
            dvp = [_dot_tn(pps[h].astype(BF16), doh[h]) for h in heads]
            dvc = [_dot_tn(pcs[h].astype(BF16), doh[h]) for h in heads]
            dzps = [pps[h] * (dpp[h] - dl_h[h]) for h in heads]
            dzcs = [pcs[h] * (dpc[h] - dl_h[h]) for h in heads]
            dzp_b = [(dzps[h] * scale).astype(BF16) for h in heads]
            dzc_b = [(dzcs[h] * scale).astype(BF16) for h in heads]
            dqs = [_dot(dzp_b[h], kp[h]) + _dot(dzc_b[h], kc[h]) for h in heads]
            dkp = [_dot_tn(dzp_b[h], qh[h]) for h in heads]
            dkc = [_dot_tn(dzc_b[h], qh[h]) for h in heads]
            for h in heads:
                db_ref[h, :, 0:bq] += dzps[h]
                db_ref[h, :, bq:2 * bq] += dzcs[h]
            dq_ref[pl.ds(qs, bq), :] = jnp.concatenate(dqs, axis=1)
            dk_ref[pl.ds(ps, bq), :] += jnp.concatenate(dkp, axis=1)
            dk_ref[pl.ds(qs, bq), :] += jnp.concatenate(dkc, axis=1)
            dv_ref[pl.ds(ps, bq), :] += jnp.concatenate(dvp, axis=1)
            dv_ref[pl.ds(qs, bq), :] += jnp.concatenate(dvc, axis=1)
            return 0

        lax.fori_loop(0, n_blk, block, 0)

    def spec(offset):
        return pl.BlockSpec((None, seq, DIL_WIDTH), lambda g, p, b: (g, b, offset + p))

    bias_spec = pl.BlockSpec((None, DIL_HEADS, bq, 2 * bq), lambda g, p, b: (g, p, 0, 0))
    out = jax.ShapeDtypeStruct((n_cfg, t, n_pairs * DIL_WIDTH), F32)
    return pl.pallas_call(
        body, name="dil_bwd", grid=(n_cfg, n_pairs, nb),
        in_specs=[spec(0), spec(n_pairs), spec(2 * n_pairs), bias_spec, spec(0), spec(0), spec(0)],
        out_specs=[spec(0), spec(0), spec(0), bias_spec],
        out_shape=[out, out, out, jax.ShapeDtypeStruct(bias.shape, F32)],
        compiler_params=_params("arbitrary", "arbitrary", "arbitrary"),
    )(qkvd, qkvd, qkvd, bias, do, lse, delta)


def _head_mean(v, gmat):
    return _split_dot(v, gmat) * (1.0 / HEAD_DIM)


def _mix_out_fwd(osb, oc, lse, gsb, gdil, gmat, wout, x, mod, tm):
    t, d = x.shape
    ds = osb.shape[1]
    nt = t // tm
    tpb = nt // mod.shape[0]

    def body(osb_ref, oc_ref, lse_ref, gsb_ref, gdil_ref, gm_ref, w_ref, x_ref, mod_ref,
             xo_ref, on_ref, m_ref, odil_ref, ld_ref):
        lses = [lse_ref[i] for i in range(len(DILATIONS))]
        top = functools.reduce(jnp.maximum, lses)
        total = top + jnp.log(sum(jnp.exp(l - top) for l in lses))
        odil = sum(jnp.exp(l - total) * oc_ref[i] for i, l in enumerate(lses))
        odil_ref[...] = odil
        ld_ref[...] = total
        gm = gm_ref[...]
        parts = []
        for o, g_ref in ((osb_ref[...], gsb_ref), (odil, gdil_ref)):
            parts.append(o * lax.rsqrt(_head_mean(o * o, gm) + EPS) * g_ref[...])
        on = jnp.concatenate(parts, axis=1).astype(BF16)
        on_ref[...] = on
        m = _dot(on, w_ref[...])
        m_ref[...] = m
        xo_ref[...] = x_ref[...] + mod_ref[5:6, :] * m

    tok = pl.BlockSpec((tm, d), lambda i: (i, 0))
    hd = pl.BlockSpec((tm, ds), lambda i: (i, 0))
    hd3 = pl.BlockSpec((len(DILATIONS), tm, ds), lambda i: (0, i, 0))
    gain = pl.BlockSpec((1, ds), lambda i: (0, 0))
    return pl.pallas_call(
        body, name="mix_out_fwd", grid=(nt,),
        in_specs=[hd, hd3, hd3, gain, gain,
                  pl.BlockSpec((ds, ds), lambda i: (0, 0)),
                  pl.BlockSpec(wout.shape, lambda i: (0, 0)),
                  tok, pl.BlockSpec((None, N_MOD, d), lambda i: (i // tpb, 0, 0))],
        out_specs=[tok, pl.BlockSpec((tm, 2 * ds), lambda i: (i, 0)), tok, hd, hd],
        out_shape=[jax.ShapeDtypeStruct((t, d), F32), jax.ShapeDtypeStruct((t, 2 * ds), BF16),
                   jax.ShapeDtypeStruct((t, d), F32), jax.ShapeDtypeStruct((t, ds), F32),
                   jax.ShapeDtypeStruct((t, ds), F32)],
        compiler_params=_params("arbitrary"),
    )(osb, oc, lse, gsb, gdil, gmat, wout, x, mod)


def _mix_out_bwd(dxo, m, mod, wout, osb, odil, gsb, gdil, gmat, tm):
    t, d = dxo.shape
    ds = osb.shape[1]
    nt = t // tm
    nb = mod.shape[0]
    tpb = nt // nb

    def body(dxo_ref, m_ref, mod_ref, w_ref, osb_ref, odil_ref, gsb_ref, gdil_ref, gm_ref,
             dm_ref, dosb_ref, dodil_ref, dldil_ref, dmod_ref, dg_ref):
        i = pl.program_id(0)
        dxo_v = dxo_ref[...]
        dm = (mod_ref[5:6, :] * dxo_v).astype(BF16)
        dm_ref[...] = dm
        dgt = jnp.sum(m_ref[...] * dxo_v, axis=0, keepdims=True)
        don = _dot_nt(dm, w_ref[...])
        gm = gm_ref[...]

        @pl.when(i % tpb == 0)
        def _():
            dmod_ref[...] = jnp.zeros_like(dmod_ref)

        @pl.when(i == 0)
        def _():
            dg_ref[...] = jnp.zeros_like(dg_ref)

        dmod_ref[2:3, :] += dgt
        groups = ((osb_ref, gsb_ref, dosb_ref), (odil_ref, gdil_ref, dodil_ref))
        for k, (o_ref, g_ref, do_ref) in enumerate(groups):
            o = o_ref[...]
            dn_out = don[:, k * ds:(k + 1) * ds]
            r = lax.rsqrt(_head_mean(o * o, gm) + EPS)
            n = o * r
            dg_ref[0:1, k * ds:(k + 1) * ds] += jnp.sum(dn_out * n, axis=0, keepdims=True)
            dn = dn_out * g_ref[...]
            do = r * (dn - n * _head_mean(dn * n, gm))
            do_ref[...] = do
            if k == 1:
                dldil_ref[...] = _head_mean(do * o, gm) * float(HEAD_DIM)

    tok = pl.BlockSpec((tm, d), lambda i: (i, 0))
    hd = pl.BlockSpec((tm, ds), lambda i: (i, 0))
    gain = pl.BlockSpec((1, ds), lambda i: (0, 0))
    hds = jax.ShapeDtypeStruct((t, ds), F32)
    return pl.pallas_call(
        body, name="mix_out_bwd", grid=(nt,),
        in_specs=[tok, tok, pl.BlockSpec((None, N_MOD, d), lambda i: (i // tpb, 0, 0)),
                  pl.BlockSpec(wout.shape, lambda i: (0, 0)), hd, hd, gain, gain,
                  pl.BlockSpec((ds, ds), lambda i: (0, 0))],
        out_specs=[tok, hd, hd, hd,
                   pl.BlockSpec((None, 8, d), lambda i: (i // tpb, 0, 0)),
                   pl.BlockSpec((8, 2 * ds), lambda i: (0, 0))],
        out_shape=[jax.ShapeDtypeStruct((t, d), BF16), hds, hds, hds,
                   jax.ShapeDtypeStruct((nb, 8, d), F32), jax.ShapeDtypeStruct((8, 2 * ds), F32)],
        compiler_params=_params("arbitrary"),
    )(dxo, m, mod, wout, osb, odil, gsb, gdil, gmat)


def _loss_head(x, target, g, tm):
    t, d = x.shape

    def body(x_ref, t_ref, g_ref, dx_ref, acc_ref):
        @pl.when(pl.program_id(0) == 0)
        def _():
            acc_ref[...] = jnp.zeros_like(acc_ref)

        n, r = _norm(x_ref[...])
        gv = g_ref[...]
        err = n * gv - t_ref[...]
        dy = err * (1.0 / d)
        acc_ref[0:1, :] += jnp.sum(err * err, axis=0, keepdims=True)
        acc_ref[1:2, :] += jnp.sum(dy * n, axis=0, keepdims=True)
        dn = dy * gv
        dx_ref[...] = r * (dn - n * jnp.mean(dn * n, axis=-1, keepdims=True))

    tok = pl.BlockSpec((tm, d), lambda i: (i, 0))
    return pl.pallas_call(
        body, name="loss_head", grid=(t // tm,),
        in_specs=[tok, tok, pl.BlockSpec((1, d), lambda i: (0, 0))],
        out_specs=[tok, pl.BlockSpec((8, d), lambda i: (0, 0))],
        out_shape=[jax.ShapeDtypeStruct((t, d), F32), jax.ShapeDtypeStruct((8, d), F32)],
        compiler_params=_params("arbitrary"),
    )(x, target, g)


def _row_tile(rows):
    if rows <= 256:
        return rows
    for cand in range(256, 15, -16):
        if rows % cand == 0:
            return cand
    return rows


def _adamw(w, parts, m, v, name):
    rows, cols = w.shape
    n_parts = parts.shape[0]
    tr = _row_tile(rows)
    c1 = 1.0 / (1.0 - ADAM_B1 ** ADAM_STEP)
    c2 = 1.0 / (1.0 - ADAM_B2 ** ADAM_STEP)

    def body(w_ref, p_ref, m_ref, v_ref, g_ref, d_ref, nm_ref, nv_ref):
        g = p_ref[0].astype(F32)
        for i in range(1, n_parts):
            g = g + p_ref[i].astype(F32)
        nm = ADAM_B1 * m_ref[...] + (1.0 - ADAM_B1) * g
        nv = ADAM_B2 * v_ref[...] + (1.0 - ADAM_B2) * (g * g)
        g_ref[...] = g
        nm_ref[...] = nm
        nv_ref[...] = nv
        d_ref[...] = -ADAM_LR * ((nm * c1) / (jnp.sqrt(nv * c2) + ADAM_EPS) + ADAM_WD * w_ref[...])

    blk = pl.BlockSpec((tr, cols), lambda i: (i, 0))
    out = jax.ShapeDtypeStruct((rows, cols), F32)
    return pl.pallas_call(
        body, name=name, grid=(rows // tr,),
        in_specs=[blk, pl.BlockSpec((n_parts, tr, cols), lambda i: (0, i, 0)), blk, blk],
        out_specs=[blk, blk, blk, blk], out_shape=[out, out, out, out],
        compiler_params=_params("arbitrary"),
    )(w, parts, m, v)


def _t5_bucket(n):
    max_exact = N_BUCKETS // 2
    nf = np.maximum(n, 1).astype(np.float32)
    large = max_exact + (np.log(nf / max_exact) / math.log(MAX_DISTANCE / max_exact)
                         * (N_BUCKETS - max_exact)).astype(np.int32)
    large = np.minimum(large, N_BUCKETS - 1)
    return np.where(n < max_exact, n, large).astype(np.int32)


def _bucket_onehot():
    table = np.zeros((len(DILATIONS), 2 * DIL_BLOCK + 1, N_BUCKETS), np.float32)
    for i, dil in enumerate(DILATIONS):
        buckets = _t5_bucket(np.arange(DIL_BLOCK + 1) * dil)
        for m in range(DIL_BLOCK + 1):
            table[i, m, buckets[DIL_BLOCK - m]] = 1.0
    return table


def _bias_blocks(rel_bias):
    row = jnp.einsum("cmn,nh->chm", _bucket_onehot(), rel_bias, precision=lax.Precision.HIGHEST)
    n_cfg, n_heads, width = row.shape
    tiled = jnp.tile(row, (1, 1, DIL_BLOCK))[..., :DIL_BLOCK * (width - 1)]
    return tiled.reshape(n_cfg, n_heads, DIL_BLOCK, width - 1)


def _bias_blocks_bwd(dblocks):
    n_cfg, n_heads = dblocks.shape[:2]
    width = 2 * DIL_BLOCK + 1
    flat = dblocks.reshape(n_cfg, n_heads, DIL_BLOCK * (width - 1))
    flat = jnp.pad(flat, ((0, 0), (0, 0), (0, DIL_BLOCK)))
    drow = jnp.sum(flat.reshape(n_cfg, n_heads, DIL_BLOCK, width), axis=2)
    return jnp.einsum("chm,cmn->nh", drow, _bucket_onehot(), precision=lax.Precision.HIGHEST)


def _to_residue(a, nb, dil):
    t, f = a.shape
    seq = t // nb
    return a.reshape(nb, seq // dil, dil, f).transpose(0, 2, 1, 3).reshape(t, f)


def _from_residue(a, nb, dil):
    t, f = a.shape
    seq = t // nb
    return a.reshape(nb, dil, seq // dil, f).transpose(0, 2, 1, 3).reshape(t, f)


def _stack_residue(a, nb):
    return jnp.stack([_to_residue(a, nb, dil) for dil in DILATIONS])


def _pad_to(a, axis, size):
    pad = [(0, 0)] * a.ndim
    pad[axis] = (0, size - a.shape[axis])
    return jnp.pad(a, pad)


def _lane_pad(n):
    return -(-n // LANES) * LANES


def _local_step(x, target, mod, gains, weights, rel_bias, tm, distributed):
    nb, seq, d = x.shape
    t = nb * seq
    g_ffn1, g_mix, g_sb, g_dil, g_ffn2, g_final = gains
    wg1, wu1, wd1 = weights[:3]
    x0 = x.reshape(t, d)
    ds = g_sb.shape[1]
    gmat = jnp.asarray(np.kron(np.eye(ds // HEAD_DIM), np.ones((HEAD_DIM, HEAD_DIM))), BF16)
    bias = _bias_blocks(rel_bias)

    def beside(arrays, scatter):
        return _Exchange(arrays, scatter) if distributed else None

    (x1, f1, gate1, up1), got = _ffn_fwd(x0, mod, g_ffn1, wg1, wu1, wd1, 0, tm, beside(weights[3:5], False))
    win, wout = got if distributed else weights[3:5]
    wout2 = wout.reshape(-1, d)
    qkv, h2 = _qkv_fwd(x1, mod, g_mix, win, tm)
    (osb, csb), got = _sb_fwd(qkv, nb, seq, beside(weights[5:8], False))
    wg2, wu2, wd2 = got if distributed else weights[5:8]
    qkvd = _stack_residue(qkv[:, 3 * ds:], nb)
    oc_r, lse_r = _dil_fwd(qkvd, bias, nb, seq)
    oc = jnp.stack([_from_residue(oc_r[i], nb, dil) for i, dil in enumerate(DILATIONS)])
    lse = jnp.stack([_from_residue(lse_r[i], nb, dil) for i, dil in enumerate(DILATIONS)])
    x2, on, mix, odil, ldil = _mix_out_fwd(osb, oc, lse, g_sb, g_dil, gmat, wout2, x1, mod, tm)
    (x3, f3, gate3, up3), _ = _ffn_fwd(x2, mod, g_ffn2, wg2, wu2, wd2, 2, tm)
    dx3, head = _loss_head(x3, target.reshape(t, d), g_final, tm)
    loss_sum = 0.5 * jnp.sum(head[0]) / d
    dg_final = head[1:2]

    (dx2, dgate3, dup3, act3, h3, df3, dmod3, dg_ffn2), _ = _ffn_bwd(
        dx3, x2, f3, mod, g_ffn2, gate3, up3, wg2, wu2, wd2, 2, tm)
    tg = min(GRAD_TILE, t)
    gwg2, gwu2, gwd2 = _ffn_weight_grads(h3, dgate3, dup3, act3, df3, tg, 2)

    dm, dosb, dodil, dldil, dmod2b, dg_heads = _mix_out_bwd(
        dx2, mix, mod, wout2, osb, odil, g_sb, g_dil, gmat, tm)
    n_out = wout.shape[0]
    gwout = _mm_tn(on, dm,
                   pl.BlockSpec((tg, wout.shape[1]), lambda j, i: (i, j)),
                   pl.BlockSpec((tg, d), lambda j, i: (i, 0)),
                   wout.shape, pl.BlockSpec((None, wout.shape[1], d), lambda j, i: (j, 0, 0)),
                   (n_out, t // tg), "grad_wout")

    (dq_sb, dk_sb, dv_sb), parts_late = _sb_bwd(qkv, dosb, csb, nb, seq,
                                                beside([gwout, gwg2, gwu2, gwd2], True))
    dq_r, dk_r, dv_r, dbias = _dil_bwd(qkvd, bias, _stack_residue(dodil, nb), _stack_residue(ldil, nb),
                                       _stack_residue(dldil, nb), nb, seq)
    dqkv_dil = [sum(_from_residue(a[i], nb, dil) for i, dil in enumerate(DILATIONS)).astype(BF16)
                for a in (dq_r, dk_r, dv_r)]
    dqkv = jnp.concatenate([dq_sb, dk_sb, dv_sb] + dqkv_dil, axis=1)
    drel = _bias_blocks_bwd(dbias)

    dx1, dmod2a, dg_mix = _qkv_bwd(dqkv, dx2, x1, mod, g_mix, win, tm)
    n_in, _, cs = win.shape
    gwin = _mm_tn(h2, dqkv,
                  pl.BlockSpec((tg, d), lambda j, i: (i, 0)),
                  pl.BlockSpec((tg, cs), lambda j, i: (i, j)),
                  win.shape, pl.BlockSpec((None, d, cs), lambda j, i: (j, 0, 0)),
                  (n_in, t // tg), "grad_win")

    (dx0, dgate1, dup1, act1, h1, df1, dmod1, dg_ffn1), parts_mid = _ffn_bwd(
        dx1, x0, f1, mod, g_ffn1, gate1, up1, wg1, wu1, wd1, 0, tm, beside([gwin], True))
    gwg1, gwu1, gwd1 = _ffn_weight_grads(h1, dgate1, dup1, act1, df1, tg, 0)

    dmod = jnp.concatenate([dmod1[:, 0:3], dmod2a[:, 0:2], dmod2b[:, 2:3], dmod3[:, 0:3]], axis=1)
    wgrads = (gwg1, gwu1, gwd1, gwin, gwout, gwg2, gwu2, gwd2)
    if distributed:
        wgrads = tuple(_exchange([gwg1, gwu1, gwd1], True, "scatter_grads")) + tuple(parts_mid + parts_late)
    ggrads = (dg_ffn1[0:1], dg_mix[0:1], dg_heads[0:1], drel, dg_ffn2[0:1], dg_final)
    return loss_sum, dx0.reshape(nb, seq, d), wgrads, dmod, ggrads


def kernel(x, c, w_ada, b_ada, g_ffn1, w1_gate, w1_up, w1_down, g_mix, w_in, g_sb_out, g_dil_out, w_out, rel_bias, g_ffn2, w2_gate, w2_up, w2_down, g_final, loss_target, m_w_ada, m_b_ada, m_g_ffn1, m_w1_gate, m_w1_up, m_w1_down, m_g_mix, m_w_in, m_g_sb_out, m_g_dil_out, m_w_out, m_rel_bias, m_g_ffn2, m_w2_gate, m_w2_up, m_w2_down, m_g_final, v_w_ada, v_b_ada, v_g_ffn1, v_w1_gate, v_w1_up, v_w1_down, v_g_mix, v_w_in, v_g_sb_out, v_g_dil_out, v_w_out, v_rel_bias, v_g_ffn2, v_w2_gate, v_w2_up, v_w2_down, v_g_final):
    nb, seq, d = x.shape
    me = 4 * lax.axis_index("x") + 2 * lax.axis_index("y") + lax.axis_index("c")
    tm = min(TOKEN_TILE, seq)
    fs = w1_gate.shape[2]
    fs_pad = _lane_pad(fs)
    ada_cols = w_ada.shape[2]

    def col_shard(w):
        return _pad_to(w[0].astype(BF16), 1, fs_pad)

    def row_shard(w):
        return _pad_to(w[0].astype(BF16), 0, fs_pad)

    shards = [col_shard(w1_gate), col_shard(w1_up), row_shard(w1_down), w_in[0].astype(BF16),
              w_out[0].astype(BF16), col_shard(w2_gate), col_shard(w2_up), row_shard(w2_down)]
    gathered = _exchange([_pad_to(c, 0, 8)] + shards[:3], False, "gather_first")
    c_all = gathered[0][:, :nb].reshape(N_DEV * nb, d)
    weights = gathered[1:] + shards[3:]

    b_cols = lax.dynamic_slice(b_ada, (0, me * ada_cols), (1, ada_cols))
    mod_part = _ada_fwd(c_all, w_ada[0], b_cols)
    mod_all = _exchange([mod_part], False, "gather_mod")[0]
    mod = lax.dynamic_slice(mod_all, (0, me * nb, 0), (N_DEV, nb, ada_cols))
    mod = mod.transpose(1, 0, 2).reshape(nb, N_MOD, d)

    n_sb = g_sb_out.shape[1] * g_sb_out.shape[2]
    gains = (g_ffn1, g_mix, g_sb_out.reshape(1, n_sb), g_dil_out.reshape(1, -1), g_ffn2,
             g_final.reshape(1, d))
    loss_sum, grad_x, parts, dmod, ggrads = _local_step(x, loss_target, mod, gains, weights, rel_bias, tm, True)
    loss = lax.psum(loss_sum, ("x", "y", "c"))

    dg_ffn1, dg_mix, dg_heads, drel, dg_ffn2, dg_final = ggrads
    width = max(d, dg_heads.shape[1], drel.size)
    small = jnp.concatenate(
        [_pad_to(a.reshape(1, -1), 1, width) for a in (dg_ffn1, dg_mix, dg_ffn2, dg_final, dg_heads, drel)]
        + [jnp.zeros((2, width), F32)], axis=0)
    dmod_all, small_all = _exchange([_pad_to(dmod.reshape(nb, N_MOD * d), 0, 8), small], False, "gather_small")
    dmod_all = dmod_all[:, :nb].reshape(N_DEV * nb, N_MOD * d)
    dmod_cols = lax.dynamic_slice(dmod_all, (0, me * ada_cols), (N_DEV * nb, ada_cols))
    gw_ada, gb_ada = _ada_bwd(c_all, dmod_cols, dmod_all)

    def small_part(row, size, shape):
        return small_all[:, row, :size].reshape((N_DEV,) + shape)

    n_rel = rel_bias.shape
    updates = {
        "w_ada": (w_ada[0], gw_ada[None], m_w_ada[0], v_w_ada[0]),
        "b_ada": (b_ada, gb_ada[None], m_b_ada, v_b_ada),
        "g_ffn1": (g_ffn1, small_part(0, d, (1, d)), m_g_ffn1, v_g_ffn1),
        "w1_gate": (w1_gate[0], parts[0][:, :, :fs], m_w1_gate[0], v_w1_gate[0]),
        "w1_up": (w1_up[0], parts[1][:, :, :fs], m_w1_up[0], v_w1_up[0]),
        "w1_down": (w1_down[0], parts[2][:, :fs, :], m_w1_down[0], v_w1_down[0]),
        "g_mix": (g_mix, small_part(1, d, (1, d)), m_g_mix, v_g_mix),
        "w_in": (w_in[0], parts[3], m_w_in[0], v_w_in[0]),
        "g_sb_out": (g_sb_out[0], small_all[:, 4, :n_sb].reshape((N_DEV,) + g_sb_out.shape[1:]),
                     m_g_sb_out[0], v_g_sb_out[0]),
        "g_dil_out": (g_dil_out[0], small_all[:, 4, n_sb:dg_heads.shape[1]].reshape((N_DEV,) + g_dil_out.shape[1:]),
                      m_g_dil_out[0], v_g_dil_out[0]),
        "w_out": (w_out[0], parts[4], m_w_out[0], v_w_out[0]),
        "rel_bias": (rel_bias, small_part(5, drel.size, n_rel), m_rel_bias, v_rel_bias),
        "g_ffn2": (g_ffn2, small_part(2, d, (1, d)), m_g_ffn2, v_g_ffn2),
        "w2_gate": (w2_gate[0], parts[5][:, :, :fs], m_w2_gate[0], v_w2_gate[0]),
        "w2_up": (w2_up[0], parts[6][:, :, :fs], m_w2_up[0], v_w2_up[0]),
        "w2_down": (w2_down[0], parts[7][:, :fs, :], m_w2_down[0], v_w2_down[0]),
        "g_final": (g_final.reshape(1, d), small_part(3, d, (1, d)), m_g_final.reshape(1, d), v_g_final.reshape(1, d)),
    }
    shapes = {"w_ada": w_ada.shape, "b_ada": b_ada.shape, "g_ffn1": g_ffn1.shape, "w1_gate": w1_gate.shape,
              "w1_up": w1_up.shape, "w1_down": w1_down.shape, "g_mix": g_mix.shape, "w_in": w_in.shape,
              "g_sb_out": g_sb_out.shape, "g_dil_out": g_dil_out.shape, "w_out": w_out.shape,
              "rel_bias": rel_bias.shape, "g_ffn2": g_ffn2.shape, "w2_gate": w2_gate.shape,
              "w2_up": w2_up.shape, "w2_down": w2_down.shape, "g_final": g_final.shape}
    grads, deltas, new_m, new_v = [], [], [], []
    for name, (w, p, m, v) in updates.items():
        g, dw, nm, nv = _adamw(w, p, m, v, f"adamw_{name}")
        grads.append(g.reshape(shapes[name]))
        deltas.append(dw.reshape(shapes[name]))
        new_m.append(nm.reshape(shapes[name]))
        new_v.append(nv.reshape(shapes[name]))
    return (loss, grad_x, *grads, *deltas, *new_m, *new_v)
```

```python
import functools
import math

import numpy as np
import jax
import jax.numpy as jnp
from jax import lax
from jax.experimental import pallas as pl
from jax.experimental.pallas import tpu as pltpu

F32 = jnp.float32
BF16 = jnp.bfloat16

EPS = 1e-6
NEG_INF = -1e30
HEAD_DIM = 64
LANES = 128
DIL_BLOCK = 128
DILATIONS = (1, 4, 16)
N_BUCKETS = 32
MAX_DISTANCE = 2048
N_MOD = 9
N_DEV = 8
SB_BLOCK = 256
SB_HEADS = 4
SB_WIDTH = SB_HEADS * HEAD_DIM
DIL_HEADS = 4
DIL_WIDTH = DIL_HEADS * HEAD_DIM
TOKEN_TILE = 512
PROJ_TILE = 1024
GRAD_TILE = 1024
VMEM_LIMIT_BYTES = 56 * 1024 * 1024

ADAM_LR = 0.001
ADAM_B1 = 0.9
ADAM_B2 = 0.999
ADAM_EPS = 1e-08
ADAM_WD = 0.01
ADAM_STEP = 10

NT_DIMS = (((1,), (1,)), ((), ()))
TN_DIMS = (((0,), (0,)), ((), ()))


def _params(*sem):
    return pltpu.CompilerParams(dimension_semantics=sem, vmem_limit_bytes=VMEM_LIMIT_BYTES)


def _once(spec):
    return pl.BlockSpec(spec.block_shape, spec.index_map, pipeline_mode=pl.Buffered(1))


def _dot(a, b):
    return jnp.dot(a, b, preferred_element_type=F32)


def _dot_nt(a, b):
    return lax.dot_general(a, b, NT_DIMS, preferred_element_type=F32)


def _dot_tn(a, b):
    return lax.dot_general(a, b, TN_DIMS, preferred_element_type=F32)


def _split_dot(a, b):
    hi = a.astype(BF16)
    lo = (a - hi.astype(F32)).astype(BF16)
    return _dot(hi, b) + _dot(lo, b)


def _sigmoid(z):
    return 1.0 / (1.0 + jnp.exp(-z))


def _norm(x):
    r = lax.rsqrt(jnp.mean(x * x, axis=-1, keepdims=True) + EPS)
    return x * r, r


def _modulate(x, g, mod_ref, k):
    n, _ = _norm(x)
    shift = mod_ref[3 * k:3 * k + 1, :]
    scale = mod_ref[3 * k + 1:3 * k + 2, :]
    return n * g * (1.0 + scale) + shift


def _modulate_bwd(dh, x, g, mod_ref, k):
    n, r = _norm(x)
    scale = mod_ref[3 * k + 1:3 * k + 2, :]
    dshift = jnp.sum(dh, axis=0, keepdims=True)
    dscale = jnp.sum(dh * n * g, axis=0, keepdims=True)
    dg = jnp.sum(dh * n * (1.0 + scale), axis=0, keepdims=True)
    dn = dh * g * (1.0 + scale)
    dx = r * (dn - n * jnp.mean(dn * n, axis=-1, keepdims=True))
    return dx, dshift, dscale, dg


class _Exchange:
    def __init__(self, arrays, scatter, relay=False):
        assert not (scatter and relay)
        self.arrays = list(arrays)
        self.scatter = scatter
        self.relay = relay
        self.n = len(self.arrays)
        self.out_shape = [
            jax.ShapeDtypeStruct((N_DEV,) + tuple(a.shape[1:] if scatter else a.shape), a.dtype)
            for a in self.arrays]
        n_remote = self.n * (N_DEV - 1)
        self.scratch_shapes = [pltpu.SemaphoreType.DMA((n_remote,)), pltpu.SemaphoreType.DMA((n_remote,)),
                               pltpu.SemaphoreType.DMA((self.n,))]

    def _copies(self, in_refs, out_refs, sems):
        send_sems, recv_sems, local_sems = sems
        x, y, c = lax.axis_index("x"), lax.axis_index("y"), lax.axis_index("c")
        me = 4 * x + 2 * y + c
        local, remote, relayed = [], {}, {}
        for a in range(self.n):
            src = in_refs[a].at[me] if self.scatter else in_refs[a]
            local.append(pltpu.make_async_copy(src, out_refs[a].at[me], local_sems.at[a]))
            for k in range(1, N_DEV):
                px = 1 - x if k & 4 else x
                py = 1 - y if k & 2 else y
                pc = 1 - c if k & 1 else c
                sem = a * (N_DEV - 1) + k - 1
                if self.relay and k & 1 and k > 1:
                    slot = 4 * px + 2 * py + c
                    relayed[a, k] = pltpu.make_async_remote_copy(
                        src_ref=out_refs[a].at[slot], dst_ref=out_refs[a].at[slot],
                        send_sem=send_sems.at[sem], recv_sem=recv_sems.at[sem],
                        device_id=(x, y, 1 - c), device_id_type=pl.DeviceIdType.MESH)
                    continue
                src = in_refs[a].at[4 * px + 2 * py + pc] if self.scatter else in_refs[a]
                remote[a, k] = pltpu.make_async_remote_copy(
                    src_ref=src, dst_ref=out_refs[a].at[me],
                    send_sem=send_sems.at[sem], recv_sem=recv_sems.at[sem],
                    device_id=(px, py, pc), device_id_type=pl.DeviceIdType.MESH)
        return local, remote, relayed

    def start(self, in_refs, out_refs, sems):
        local, remote, _ = self._copies(in_refs, out_refs, sems)
        for cp in local + list(remote.values()):
            cp.start()

    def wait(self, in_refs, out_refs, sems):
        local, remote, relayed = self._copies(in_refs, out_refs, sems)
        for (a, k), cp in relayed.items():
            remote[a, k - 1].wait_recv()
            cp.start()
        for (a, k), cp in remote.items():
            if (a, k + 1) not in relayed:
                cp.wait_recv()
        for cp in relayed.values():
            cp.wait_recv()
        for cp in list(remote.values()) + list(relayed.values()):
            cp.wait_send()
        for cp in local:
            cp.wait()


def _call(body, *, name, args, in_specs, out_specs, out_shape, scratch_shapes=(), grid=(),
          params=None, exchange=None):
    n_in, n_out = len(args), len(out_shape)
    if exchange is None:
        outs = pl.pallas_call(
            body, name=name, grid=grid, in_specs=list(in_specs), out_specs=list(out_specs),
            out_shape=list(out_shape), scratch_shapes=list(scratch_shapes), compiler_params=params,
        )(*args)
        return list(outs), []
    n_ex = exchange.n

    def wrapped(*refs):
        ins, refs = refs[:n_in], refs[n_in:]
        ex_in, refs = refs[:n_ex], refs[n_ex:]
        outs, refs = refs[:n_out], refs[n_out:]
        ex_out, refs = refs[:n_ex], refs[n_ex:]
        scratch, sems = refs[:len(refs) - 3], refs[len(refs) - 3:]
        if not grid:
            exchange.start(ex_in, ex_out, sems)
            body(*ins, *outs, *scratch)
            exchange.wait(ex_in, ex_out, sems)
            return
        first = functools.reduce(jnp.logical_and, [pl.program_id(a) == 0 for a in range(len(grid))])
        last = functools.reduce(jnp.logical_and, [pl.program_id(a) == grid[a] - 1 for a in range(len(grid))])

        @pl.when(first)
        def _():
            exchange.start(ex_in, ex_out, sems)

        body(*ins, *outs, *scratch)

        @pl.when(last)
        def _():
            exchange.wait(ex_in, ex_out, sems)

    any_spec = pl.BlockSpec(memory_space=pl.ANY)
    outs = pl.pallas_call(
        wrapped, name=name, grid=grid,
        in_specs=list(in_specs) + [any_spec] * n_ex, out_specs=list(out_specs) + [any_spec] * n_ex,
        out_shape=list(out_shape) + exchange.out_shape,
        scratch_shapes=list(scratch_shapes) + exchange.scratch_shapes, compiler_params=params,
    )(*args, *exchange.arrays)
    return list(outs[:n_out]), list(outs[n_out:])


def _exchange(arrays, scatter, name, relay=False):
    return _call(lambda: None, name=name, args=(), in_specs=(), out_specs=(), out_shape=(),
                 exchange=_Exchange(arrays, scatter, relay))[1]


def _ada_fwd(c_all, w, b):
    def body(c_ref, w_ref, b_ref, o_ref):
        cv = c_ref[...]
        s = (cv * _sigmoid(cv)).astype(BF16)
        o_ref[...] = _dot(s, w_ref[...].astype(BF16)) + b_ref[...]

    return pl.pallas_call(
        body, name="ada_fwd", out_shape=jax.ShapeDtypeStruct((c_all.shape[0], w.shape[1]), F32),
        compiler_params=pltpu.CompilerParams(vmem_limit_bytes=VMEM_LIMIT_BYTES),
    )(c_all, w, b)


def _ada_bwd(c_all, dmod_cols, dmod_all):
    def body(c_ref, dc_ref, da_ref, gw_ref, gb_ref):
        cv = c_ref[...]
        s = cv * _sigmoid(cv)
        gw_ref[...] = lax.dot_general(s, dc_ref[...], TN_DIMS, preferred_element_type=F32,
                                      precision=lax.Precision.HIGHEST)
        gb_ref[...] = jnp.sum(da_ref[...], axis=0, keepdims=True)

    return pl.pallas_call(
        body, name="ada_bwd",
        out_shape=(jax.ShapeDtypeStruct((c_all.shape[1], dmod_cols.shape[1]), F32),
                   jax.ShapeDtypeStruct((1, dmod_all.shape[1]), F32)),
        compiler_params=pltpu.CompilerParams(vmem_limit_bytes=VMEM_LIMIT_BYTES),
    )(c_all, dmod_cols, dmod_all)


def _ffn_fwd(x, mod, g, wg, wu, wd, k, tm, exchange=None):
    t, d = x.shape
    ns, _, fs = wg.shape
    nt = t // tm
    tpb = nt // mod.shape[0]

    def body(x_ref, mod_ref, g_ref, wg_ref, wu_ref, wd_ref, xo_ref, f_ref, gg_ref, uu_ref, h_sc, acc):
        j = pl.program_id(1)

        @pl.when(j == 0)
        def _():
            h_sc[...] = _modulate(x_ref[...], g_ref[...], mod_ref, k).astype(BF16)
            acc[...] = jnp.zeros_like(acc)

        h = h_sc[...]
        gate = _dot(h, wg_ref[...])
        up = _dot(h, wu_ref[...])
        act = gate * _sigmoid(gate) * up
        acc[...] += _dot(act.astype(BF16), wd_ref[...])
        gg_ref[...] = gate.astype(BF16)
        uu_ref[...] = up.astype(BF16)

        @pl.when(j == ns - 1)
        def _():
            f = acc[...]
            f_ref[...] = f.astype(BF16)
            xo_ref[...] = x_ref[...] + 0.5 * mod_ref[3 * k + 2:3 * k + 3, :] * f

    tok = pl.BlockSpec((tm, d), lambda i, j: (i, 0))
    hid = pl.BlockSpec((None, tm, fs), lambda i, j: (j, i, 0))
    return _call(
        body, name=f"ffn_fwd{k}", grid=(nt, ns), args=(x, mod, g, wg, wu, wd),
        in_specs=[_once(tok),
                  pl.BlockSpec((None, N_MOD, d), lambda i, j: (i // tpb, 0, 0)),
                  pl.BlockSpec((1, d), lambda i, j: (0, 0)),
                  pl.BlockSpec((None, d, fs), lambda i, j: (j, 0, 0)),
                  pl.BlockSpec((None, d, fs), lambda i, j: (j, 0, 0)),
                  pl.BlockSpec((None, fs, d), lambda i, j: (j, 0, 0))],
        out_specs=[tok, tok, hid, hid],
        out_shape=[jax.ShapeDtypeStruct((t, d), F32), jax.ShapeDtypeStruct((t, d), BF16),
                   jax.ShapeDtypeStruct((ns, t, fs), BF16), jax.ShapeDtypeStruct((ns, t, fs), BF16)],
        scratch_shapes=[pltpu.VMEM((tm, d), BF16), pltpu.VMEM((tm, d), F32)],
        params=_params("arbitrary", "arbitrary"), exchange=exchange)


def _ffn_bwd(dxo, x, f, mod, g, gate, up, wg, wu, wd, k, tm, exchange=None):
    t, d = x.shape
    ns, _, fs = wg.shape
    nt = t // tm
    nb = mod.shape[0]
    tpb = nt // nb

    def body(dxo_ref, x_ref, f_ref, mod_ref, g_ref, gg_ref, uu_ref, wg_ref, wu_ref, wd_ref,
             dx_ref, dgg_ref, duu_ref, act_ref, h_ref, df_ref, dmod_ref, dg_ref, acc):
        i, j = pl.program_id(0), pl.program_id(1)

        @pl.when(j == 0)
        def _():
            df = 0.5 * mod_ref[3 * k + 2:3 * k + 3, :] * dxo_ref[...]
            df_ref[...] = df.astype(BF16)
            h_ref[...] = _modulate(x_ref[...], g_ref[...], mod_ref, k).astype(BF16)
            acc[...] = jnp.zeros_like(acc)

        dact = _dot_nt(df_ref[...], wd_ref[...])
        gv, uv = gg_ref[...].astype(F32), uu_ref[...].astype(F32)
        sig = _sigmoid(gv)
        s = gv * sig
        act_ref[...] = (s * uv).astype(BF16)
        dup = (dact * s).astype(BF16)
        dgate = (dact * uv * (sig * (1.0 + gv * (1.0 - sig)))).astype(BF16)
        duu_ref[...] = dup
        dgg_ref[...] = dgate
        acc[...] += _dot_nt(dgate, wg_ref[...]) + _dot_nt(dup, wu_ref[...])

        @pl.when(j == ns - 1)
        def _():
            dx, dshift, dscale, dg = _modulate_bwd(acc[...], x_ref[...], g_ref[...], mod_ref, k)
            dxo_v = dxo_ref[...]
            dx_ref[...] = dxo_v + dx
            dgt = jnp.sum(0.5 * f_ref[...].astype(F32) * dxo_v, axis=0, keepdims=True)

            @pl.when(i % tpb == 0)
            def _():
                dmod_ref[...] = jnp.zeros_like(dmod_ref)

            @pl.when(i == 0)
            def _():
                dg_ref[...] = jnp.zeros_like(dg_ref)

            dmod_ref[0:1, :] += dshift
            dmod_ref[1:2, :] += dscale
            dmod_ref[2:3, :] += dgt
            dg_ref[0:1, :] += dg

    tok = pl.BlockSpec((tm, d), lambda i, j: (i, 0))
    hid = pl.BlockSpec((None, tm, fs), lambda i, j: (j, i, 0))
    return _call(
        body, name=f"ffn_bwd{k}", grid=(nt, ns), args=(dxo, x, f, mod, g, gate, up, wg, wu, wd),
        in_specs=[_once(tok), _once(tok), _once(tok),
                  pl.BlockSpec((None, N_MOD, d), lambda i, j: (i // tpb, 0, 0)),
                  pl.BlockSpec((1, d), lambda i, j: (0, 0)),
                  hid, hid,
                  pl.BlockSpec((None, d, fs), lambda i, j: (j, 0, 0)),
                  pl.BlockSpec((None, d, fs), lambda i, j: (j, 0, 0)),
                  pl.BlockSpec((None, fs, d), lambda i, j: (j, 0, 0))],
        out_specs=[tok, hid, hid, hid, tok, tok,
                   pl.BlockSpec((None, 8, d), lambda i, j: (i // tpb, 0, 0)),
                   pl.BlockSpec((8, d), lambda i, j: (0, 0))],
        out_shape=[jax.ShapeDtypeStruct((t, d), F32),
                   jax.ShapeDtypeStruct((ns, t, fs), BF16), jax.ShapeDtypeStruct((ns, t, fs), BF16),
                   jax.ShapeDtypeStruct((ns, t, fs), BF16),
                   jax.ShapeDtypeStruct((t, d), BF16), jax.ShapeDtypeStruct((t, d), BF16),
                   jax.ShapeDtypeStruct((nb, 8, d), F32), jax.ShapeDtypeStruct((8, d), F32)],
        scratch_shapes=[pltpu.VMEM((tm, d), F32)],
        params=_params("arbitrary", "arbitrary"), exchange=exchange)


def _mm_tn(a, b, a_spec, b_spec, out_shape, out_spec, grid, name, exchange=None):
    block = tuple(out_shape[1:])
    last = grid[1] - 1
    flip = block[0] > block[1]
    if flip:
        block = block[::-1]

    def body(a_ref, b_ref, o_ref, acc):
        i = pl.program_id(1)

        @pl.when(i == 0)
        def _():
            acc[...] = jnp.zeros_like(acc)

        if flip:
            acc[...] += _dot_tn(b_ref[...], a_ref[...])
        else:
            acc[...] += _dot_tn(a_ref[...], b_ref[...])

        @pl.when(i == last)
        def _():
            total = acc[...]
            o_ref[...] = (total.T if flip else total).astype(o_ref.dtype)

    outs, sent = _call(
        body, name=name, grid=grid, args=(a, b), in_specs=[a_spec, b_spec], out_specs=[out_spec],
        out_shape=[jax.ShapeDtypeStruct(out_shape, BF16)],
        scratch_shapes=[pltpu.VMEM(block, F32)],
        params=_params("arbitrary", "arbitrary"), exchange=exchange)
    return (outs[0], sent) if exchange is not None else outs[0]


def _ffn_weight_grads(h, dgate, dup, act, df, tm, tag, stream=False):
    t, d = h.shape
    ns, _, fs = dgate.shape
    grid = (ns, t // tm)
    tok = pl.BlockSpec((tm, d), lambda j, i: (i, 0))
    hid = pl.BlockSpec((None, tm, fs), lambda j, i: (j, i, 0))
    col = pl.BlockSpec((None, d, fs), lambda j, i: (j, 0, 0))
    row = pl.BlockSpec((None, fs, d), lambda j, i: (j, 0, 0))
    gwg = _mm_tn(h, dgate, tok, hid, (ns, d, fs), col, grid, f"grad_wg{tag}")
    if not stream:
        gwu = _mm_tn(h, dup, tok, hid, (ns, d, fs), col, grid, f"grad_wu{tag}")
        gwd = _mm_tn(act, df, hid, tok, (ns, fs, d), row, grid, f"grad_wd{tag}")
        return gwg, gwu, gwd
    gwu, sent_g = _mm_tn(h, dup, tok, hid, (ns, d, fs), col, grid, f"grad_wu{tag}", _Exchange([gwg], True))
    gwd, sent_u = _mm_tn(act, df, hid, tok, (ns, fs, d), row, grid, f"grad_wd{tag}", _Exchange([gwu], True))
    sent_d = _exchange([gwd], True, "scatter_last")
    return sent_g[0], sent_u[0], sent_d[0]


def _qkv_fwd(x, mod, g, win, tm):
    t, d = x.shape
    ns, _, cs = win.shape
    nt = t // tm
    tpb = nt // mod.shape[0]

    def body(x_ref, mod_ref, g_ref, w_ref, qkv_ref, h_ref):
        @pl.when(pl.program_id(1) == 0)
        def _():
            h_ref[...] = _modulate(x_ref[...], g_ref[...], mod_ref, 1).astype(BF16)

        qkv_ref[...] = _dot(h_ref[...], w_ref[...]).astype(BF16)

    tok = pl.BlockSpec((tm, d), lambda i, j: (i, 0))
    return pl.pallas_call(
        body, name="qkv_fwd", grid=(nt, ns),
        in_specs=[_once(tok),
                  pl.BlockSpec((None, N_MOD, d), lambda i, j: (i // tpb, 0, 0)),
                  pl.BlockSpec((1, d), lambda i, j: (0, 0)),
                  pl.BlockSpec((None, d, cs), lambda i, j: (j, 0, 0))],
        out_specs=[pl.BlockSpec((tm, cs), lambda i, j: (i, j)), tok],
        out_shape=[jax.ShapeDtypeStruct((t, ns * cs), BF16), jax.ShapeDtypeStruct((t, d), BF16)],
        compiler_params=_params("arbitrary", "arbitrary"),
    )(x, mod, g, win)


def _qkv_bwd(dqkv, dxo, x, mod, g, win, tm):
    t, d = x.shape
    ns, _, cs = win.shape
    nt = t // tm
    nb = mod.shape[0]
    tpb = nt // nb

    def body(dq_ref, dxo_ref, x_ref, mod_ref, g_ref, w_ref, dx_ref, dmod_ref, dg_ref, acc):
        i, j = pl.program_id(0), pl.program_id(1)

        @pl.when(j == 0)
        def _():
            acc[...] = jnp.zeros_like(acc)

        acc[...] += _dot_nt(dq_ref[...], w_ref[...])

        @pl.when(j == ns - 1)
        def _():
            dx, dshift, dscale, dg = _modulate_bwd(acc[...], x_ref[...], g_ref[...], mod_ref, 1)
            dx_ref[...] = dxo_ref[...] + dx

            @pl.when(i % tpb == 0)
            def _():
                dmod_ref[...] = jnp.zeros_like(dmod_ref)

            @pl.when(i == 0)
            def _():
                dg_ref[...] = jnp.zeros_like(dg_ref)

            dmod_ref[0:1, :] += dshift
            dmod_ref[1:2, :] += dscale
            dg_ref[0:1, :] += dg

    tok = pl.BlockSpec((tm, d), lambda i, j: (i, 0))
    return pl.pallas_call(
        body, name="qkv_bwd", grid=(nt, ns),
        in_specs=[pl.BlockSpec((tm, cs), lambda i, j: (i, j)), _once(tok), _once(tok),
                  pl.BlockSpec((None, N_MOD, d), lambda i, j: (i // tpb, 0, 0)),
                  pl.BlockSpec((1, d), lambda i, j: (0, 0)),
                  pl.BlockSpec((None, d, cs), lambda i, j: (j, 0, 0))],
        out_specs=[tok,
                   pl.BlockSpec((None, 8, d), lambda i, j: (i // tpb, 0, 0)),
                   pl.BlockSpec((8, d), lambda i, j: (0, 0))],
        out_shape=[jax.ShapeDtypeStruct((t, d), F32),
                   jax.ShapeDtypeStruct((nb, 8, d), F32), jax.ShapeDtypeStruct((8, d), F32)],
        scratch_shapes=[pltpu.VMEM((tm, d), F32)],
        compiler_params=_params("arbitrary", "arbitrary"),
    )(dqkv, dxo, x, mod, g, win)


def _heads(a):
    return [a[:, h * HEAD_DIM:(h + 1) * HEAD_DIM] for h in range(a.shape[1] // HEAD_DIM)]


def _sb_logits(qh, kh, tri, causal):
    zs = [_dot_nt(q, k) * (HEAD_DIM ** -0.5) for q, k in zip(qh, kh)]
    es = [jnp.exp(-jnp.abs(z)) for z in zs]
    log_nots = [-(jnp.maximum(z, 0.0) + jnp.log(1.0 + e)) for z, e in zip(zs, es)]
    if causal is not None:
        log_nots = [jnp.where(causal, ln, 0.0) for ln in log_nots]
    return zs, es, [_split_dot(ln, tri) for ln in log_nots]


def _sb_masks():
    rows = lax.broadcasted_iota(jnp.int32, (SB_BLOCK, SB_BLOCK), 0)
    cols = lax.broadcasted_iota(jnp.int32, (SB_BLOCK, SB_BLOCK), 1)
    return (rows >= cols).astype(BF16), (rows <= cols).astype(BF16), cols < rows


def _sb_fwd(qkv, nb, seq, exchange=None):
    t = qkv.shape[0]
    n_pairs = (qkv.shape[1] // 6) // SB_WIDTH
    tb = SB_BLOCK
    n_blk = seq // tb

    def body(q_ref, k_ref, v_ref, o_ref, c_ref):
        tri, _, causal = _sb_masks()

        def key_block(qh, kj, carry, mask):
            ks = pl.multiple_of(kj * tb, tb)
            kh, vh = _heads(k_ref[pl.ds(ks, tb), :]), _heads(v_ref[pl.ds(ks, tb), :])
            zs, _, suffixes = _sb_logits(qh, kh, tri, mask)
            ws = [jnp.exp(z + suffix + cr[1]) for z, suffix, cr in zip(zs, suffixes, carry)]
            if mask is not None:
                ws = [jnp.where(mask, w, 0.0) for w in ws]
            pv = [_dot(w.astype(BF16), v) for w, v in zip(ws, vh)]
            return tuple((cr[0] + p, cr[1] + suffix[:, 0:1]) for cr, p, suffix in zip(carry, pv, suffixes))

        def query_block(qi, _):
            qs = pl.multiple_of(qi * tb, tb)
            qh = _heads(q_ref[pl.ds(qs, tb), :])
            zero = (jnp.zeros((tb, HEAD_DIM), F32), jnp.zeros((tb, 1), F32))
            carry = key_block(qh, qi, (zero,) * SB_HEADS, causal)
            carry = lax.fori_loop(0, qi, lambda it, cr: key_block(qh, qi - 1 - it, cr, None), carry)
            o_ref[pl.ds(qs, tb), :] = jnp.concatenate([cr[0] for cr in carry], axis=1)
            c_ref[pl.ds(qs, tb), :] = jnp.concatenate(
                [jnp.broadcast_to(cr[1], (tb, HEAD_DIM)) for cr in carry], axis=1)
            return 0

        lax.fori_loop(0, n_blk, query_block, 0)

    def spec(offset):
        return pl.BlockSpec((seq, SB_WIDTH), lambda b, p: (b, offset + p))

    out = jax.ShapeDtypeStruct((t, n_pairs * SB_WIDTH), F32)
    return _call(
        body, name="sb_fwd", grid=(nb, n_pairs), args=(qkv, qkv, qkv),
        in_specs=[spec(0), spec(n_pairs), spec(2 * n_pairs)],
        out_specs=[spec(0), spec(0)], out_shape=[out, out],
        params=_params("arbitrary", "arbitrary"), exchange=exchange)


def _sb_bwd(qkv, do, csum, nb, seq, exchange=None):
    t = qkv.shape[0]
    n_pairs = (qkv.shape[1] // 6) // SB_WIDTH
    tb = SB_BLOCK
    n_blk = seq // tb
    scale = HEAD_DIM ** -0.5

    def body(q_ref, k_ref, v_ref, do_ref, c_ref, dq_ref, dk_ref, dv_ref, dk_acc, dv_acc):
        tri, tri_prefix, causal = _sb_masks()
        dk_acc[...] = jnp.zeros_like(dk_acc)
        dv_acc[...] = jnp.zeros_like(dv_acc)

        def key_block(qh, doh, ch, kj, carry, mask):
            ks = pl.multiple_of(kj * tb, tb)
            kh, vh = _heads(k_ref[pl.ds(ks, tb), :]), _heads(v_ref[pl.ds(ks, tb), :])
            heads = range(SB_HEADS)
            zs, es, suffixes = _sb_logits(qh, kh, tri, mask)
            dws = [_dot_nt(doh[h], vh[h]) for h in heads]
            lefts = [carry[h][1] + suffixes[h][:, 0:1] for h in heads]
            ws = [jnp.exp(zs[h] + suffixes[h] + (ch[h] - lefts[h])) for h in heads]
            if mask is not None:
                ws = [jnp.where(mask, w, 0.0) for w in ws]
            dlws = [ws[h] * dws[h] for h in heads]
            dprefixes = [_split_dot(dlw, tri_prefix) for dlw in dlws]
            dvs = [_dot_tn(ws[h].astype(BF16), doh[h]) for h in heads]
            dzbs = []
            for h in heads:
                sig = jnp.where(zs[h] >= 0.0, 1.0, es[h]) / (1.0 + es[h])
                dz = dlws[h] * (1.0 - sig) - (carry[h][2] + dprefixes[h] - dlws[h]) * sig
                if mask is not None:
                    dz = jnp.where(mask, dz, 0.0)
                dzbs.append((dz * scale).astype(BF16))
            dks = [_dot_tn(dzbs[h], qh[h]) for h in heads]
            dqs = [_dot(dzbs[h], kh[h]) for h in heads]
            dk_acc[pl.ds(ks, tb), :] += jnp.concatenate(dks, axis=1)
            dv_acc[pl.ds(ks, tb), :] += jnp.concatenate(dvs, axis=1)
            return tuple((carry[h][0] + dqs[h], lefts[h], carry[h][2] + dprefixes[h][:, tb - 1:tb])
                         for h in heads)

        def query_block(qi, _):
            qs = pl.multiple_of(qi * tb, tb)
            qh = _heads(q_ref[pl.ds(qs, tb), :])
            doh = _heads(do_ref[pl.ds(qs, tb), :].astype(BF16))
            cv = c_ref[pl.ds(qs, tb), :]
            ch = [cv[:, h * HEAD_DIM:h * HEAD_DIM + 1] for h in range(SB_HEADS)]
            zero = (jnp.zeros((tb, HEAD_DIM), F32), jnp.zeros((tb, 1), F32), jnp.zeros((tb, 1), F32))
            carry = lax.fori_loop(
                0, qi, lambda kj, cr: key_block(qh, doh, ch, kj, cr, None), (zero,) * SB_HEADS)
            carry = key_block(qh, doh, ch, qi, carry, causal)
            dq_ref[pl.ds(qs, tb), :] = jnp.concatenate([cr[0] for cr in carry], axis=1).astype(BF16)
            return 0

        lax.fori_loop(0, n_blk, query_block, 0)
        dk_ref[...] = dk_acc[...].astype(BF16)
        dv_ref[...] = dv_acc[...].astype(BF16)

    def spec(offset):
        return pl.BlockSpec((seq, SB_WIDTH), lambda b, p: (b, offset + p))

    out = jax.ShapeDtypeStruct((t, n_pairs * SB_WIDTH), BF16)
    return _call(
        body, name="sb_bwd", grid=(nb, n_pairs), args=(qkv, qkv, qkv, do, csum),
        in_specs=[spec(0), spec(n_pairs), spec(2 * n_pairs), spec(0), spec(0)],
        out_specs=[spec(0), spec(0), spec(0)],
        out_shape=[out, out, out],
        scratch_shapes=[pltpu.VMEM((seq, SB_WIDTH), F32), pltpu.VMEM((seq, SB_WIDTH), F32)],
        params=_params("arbitrary", "arbitrary"), exchange=exchange)


def _dil_block_scores(qh, kph, kch, bias_ref, has_prev, band_prev, band_cur):
    scale = HEAD_DIM ** -0.5
    heads = range(len(qh))
    no_prev = jnp.where(has_prev, 0.0, NEG_INF)
    zps = [_dot_nt(qh[h], kph[h]) for h in heads]
    zcs = [_dot_nt(qh[h], kch[h]) for h in heads]
    zps = [jnp.where(band_prev, zps[h] * scale + bias_ref[h, :, 0:DIL_BLOCK], NEG_INF) + no_prev for h in heads]
    zcs = [jnp.where(band_cur, zcs[h] * scale + bias_ref[h, :, DIL_BLOCK:2 * DIL_BLOCK], NEG_INF) for h in heads]
    return zps, zcs


def _dil_bands():
    rows = lax.broadcasted_iota(jnp.int32, (DIL_BLOCK, DIL_BLOCK), 0)
    cols = lax.broadcasted_iota(jnp.int32, (DIL_BLOCK, DIL_BLOCK), 1)
    return cols >= rows, cols <= rows


def _dil_blocks_per_seq(cfg, n_blk):
    per_seq = jnp.int32(n_blk // DILATIONS[0])
    for i, dil in enumerate(DILATIONS[1:], 1):
        per_seq = jnp.where(cfg == i, n_blk // dil, per_seq)
    return per_seq


def _dil_fwd(qkvd, bias, nb, seq, exchange=None):
    n_cfg, t, width = qkvd.shape
    n_pairs = (width // 3) // DIL_WIDTH
    bq = DIL_BLOCK
    n_blk = seq // bq
    heads = range(DIL_HEADS)

    def body(q_ref, k_ref, v_ref, bias_ref, o_ref, lse_ref):
        per_seq = _dil_blocks_per_seq(pl.program_id(0), n_blk)
        band_prev, band_cur = _dil_bands()

        def block(n, _):
            has_prev = (n & (per_seq - 1)) != 0
            qs = pl.multiple_of(n * bq, bq)
            ps = pl.multiple_of(jnp.maximum(n - 1, 0) * bq, bq)
            qh = _heads(q_ref[pl.ds(qs, bq), :])
            kp, kc = _heads(k_ref[pl.ds(ps, bq), :]), _heads(k_ref[pl.ds(qs, bq), :])
            vp, vc = _heads(v_ref[pl.ds(ps, bq), :]), _heads(v_ref[pl.ds(qs, bq), :])
            zps, zcs = _dil_block_scores(qh, kp, kc, bias_ref, has_prev, band_prev, band_cur)
            ms = [jnp.maximum(jnp.max(zps[h], axis=1, keepdims=True), jnp.max(zcs[h], axis=1, keepdims=True))
                  for h in heads]
            eps = [jnp.exp(zps[h] - ms[h]) for h in heads]
            ecs = [jnp.exp(zcs[h] - ms[h]) for h in heads]
            pvs = [_dot(eps[h].astype(BF16), vp[h]) + _dot(ecs[h].astype(BF16), vc[h]) for h in heads]
            dens = [jnp.sum(eps[h], axis=1, keepdims=True) + jnp.sum(ecs[h], axis=1, keepdims=True) for h in heads]
            o_ref[pl.ds(qs, bq), :] = jnp.concatenate([pvs[h] / dens[h] for h in heads], axis=1)
            lse_ref[pl.ds(qs, bq), :] = jnp.concatenate(
                [jnp.broadcast_to(ms[h] + jnp.log(dens[h]), (bq, HEAD_DIM)) for h in heads], axis=1)
            return 0

        lax.fori_loop(0, n_blk, block, 0)

    def spec(offset):
        return pl.BlockSpec((None, seq, DIL_WIDTH), lambda g, b, p: (g, b, offset + p))

    out = jax.ShapeDtypeStruct((n_cfg, t, n_pairs * DIL_WIDTH), F32)
    return _call(
        body, name="dil_fwd", grid=(n_cfg, nb, n_pairs), args=(qkvd, qkvd, qkvd, bias),
        in_specs=[spec(0), spec(n_pairs), spec(2 * n_pairs),
                  pl.BlockSpec((None, DIL_HEADS, bq, 2 * bq), lambda g, b, p: (g, p, 0, 0))],
        out_specs=[spec(0), spec(0)], out_shape=[out, out],
        params=_params("arbitrary", "arbitrary", "arbitrary"), exchange=exchange)


def _dil_bwd(qkvd, bias, do, lse, delta, nb, seq):
    n_cfg, t, width = qkvd.shape
    n_pairs = (width // 3) // DIL_WIDTH
    bq = DIL_BLOCK
    n_blk = seq // bq
    scale = HEAD_DIM ** -0.5
    heads = range(DIL_HEADS)

    def body(q_ref, k_ref, v_ref, bias_ref, do_ref, lse_ref, dl_ref, dq_ref, dk_ref, dv_ref, db_ref):
        per_seq = _dil_blocks_per_seq(pl.program_id(0), n_blk)
        band_prev, band_cur = _dil_bands()
        dk_ref[...] = jnp.zeros_like(dk_ref)
        dv_ref[...] = jnp.zeros_like(dv_ref)

        @pl.when(pl.program_id(2) == 0)
        def _():
            db_ref[...] = jnp.zeros_like(db_ref)

        def block(n, _):
            has_prev = (n & (per_seq - 1)) != 0
            qs = pl.multiple_of(n * bq, bq)
            ps = pl.multiple_of(jnp.maximum(n - 1, 0) * bq, bq)
            qh = _heads(q_ref[pl.ds(qs, bq), :])
            kp, kc = _heads(k_ref[pl.ds(ps, bq), :]), _heads(k_ref[pl.ds(qs, bq), :])
            vp, vc = _heads(v_ref[pl.ds(ps, bq), :]), _heads(v_ref[pl.ds(qs, bq), :])
            doh = _heads(do_ref[pl.ds(qs, bq), :].astype(BF16))
            lse_v, dl_v = lse_ref[pl.ds(qs, bq), :], dl_ref[pl.ds(qs, bq), :]
            zps, zcs = _dil_block_scores(qh, kp, kc, bias_ref, has_prev, band_prev, band_cur)
            dpp = [_dot_nt(doh[h], vp[h]) for h in heads]
            dpc = [_dot_nt(doh[h], vc[h]) for h in heads]
            lse_h = [lse_v[:, h * HEAD_DIM:h * HEAD_DIM + 1] for h in heads]
            dl_h = [dl_v[:, h * HEAD_DIM:h * HEAD_DIM + 1] for h in heads]
            pps = [jnp.exp(zps[h] - lse_h[h]) for h in heads]
            pcs = [jnp.exp(zcs[h] - lse_h[h]) for h in heads]
            dvp = [_dot_tn(pps[h].astype(BF16), doh[h]) for h in heads]
            dvc = [_dot_tn(pcs[h].astype(BF16), doh[h]) for h in heads]
            dzps = [pps[h] * (dpp[h] - dl_h[h]) for h in heads]
            dzcs = [pcs[h] * (dpc[h] - dl_h[h]) for h in heads]
            dzp_b = [(dzps[h] * scale).astype(BF16) for h in heads]
            dzc_b = [(dzcs[h] * scale).astype(BF16) for h in heads]
            dqs = [_dot(dzp_b[h], kp[h]) + _dot(dzc_b[h], kc[h]) for h in heads]
            dkp = [_dot_tn(dzp_b[h], qh[h]) for h in heads]
            dkc = [_dot_tn(dzc_b[h], qh[h]) for h in heads]
            for h in heads:
                db_ref[h, :, 0:bq] += dzps[h]
                db_ref[h, :, bq:2 * bq] += dzcs[h]
            dq_ref[pl.ds(qs, bq), :] = jnp.concatenate(dqs, axis=1)
            dk_ref[pl.ds(ps, bq), :] += jnp.concatenate(dkp, axis=1)
            dk_ref[pl.ds(qs, bq), :] += jnp.concatenate(dkc, axis=1)
            dv_ref[pl.ds(ps, bq), :] += jnp.concatenate(dvp, axis=1)
            dv_ref[pl.ds(qs, bq), :] += jnp.concatenate(dvc, axis=1)
            return 0

        lax.fori_loop(0, n_blk, block, 0)

    def spec(offset):
        return pl.BlockSpec((None, seq, DIL_WIDTH), lambda g, p, b: (g, b, offset + p))

    bias_spec = pl.BlockSpec((None, DIL_HEADS, bq, 2 * bq), lambda g, p, b: (g, p, 0, 0))
    out = jax.ShapeDtypeStruct((n_cfg, t, n_pairs * DIL_WIDTH), F32)
    return pl.pallas_call(
        body, name="dil_bwd", grid=(n_cfg, n_pairs, nb),
        in_specs=[spec(0), spec(n_pairs), spec(2 * n_pairs), bias_spec, spec(0), spec(0), spec(0)],
        out_specs=[spec(0), spec(0), spec(0), bias_spec],
        out_shape=[out, out, out, jax.ShapeDtypeStruct(bias.shape, F32)],
        compiler_params=_params("arbitrary", "arbitrary", "arbitrary"),
    )(qkvd, qkvd, qkvd, bias, do, lse, delta)


def _head_mean(v, gmat):
    return _split_dot(v, gmat) * (1.0 / HEAD_DIM)


def _mix_out_fwd(osb, oc, lse, gsb, gdil, gmat, wout, x, mod, tm):
    t, d = x.shape
    ds = osb.shape[1]
    nt = t // tm
    tpb = nt // mod.shape[0]

    def body(osb_ref, oc_ref, lse_ref, gsb_ref, gdil_ref, gm_ref, w_ref, x_ref, mod_ref,
             xo_ref, on_ref, m_ref, odil_ref, ld_ref):
        lses = [lse_ref[i] for i in range(len(DILATIONS))]
        top = functools.reduce(jnp.maximum, lses)
        total = top + jnp.log(sum(jnp.exp(l - top) for l in lses))
        odil = sum(jnp.exp(l - total) * oc_ref[i] for i, l in enumerate(lses))
        odil_ref[...] = odil
        ld_ref[...] = total
        gm = gm_ref[...]
        parts = []
        for o, g_ref in ((osb_ref[...], gsb_ref), (odil, gdil_ref)):
            parts.append(o * lax.rsqrt(_head_mean(o * o, gm) + EPS) * g_ref[...])
        on = jnp.concatenate(parts, axis=1).astype(BF16)
        on_ref[...] = on
        m = _dot(on, w_ref[...])
        m_ref[...] = m
        xo_ref[...] = x_ref[...] + mod_ref[5:6, :] * m

    tok = pl.BlockSpec((tm, d), lambda i: (i, 0))
    hd = pl.BlockSpec((tm, ds), lambda i: (i, 0))
    hd3 = pl.BlockSpec((len(DILATIONS), tm, ds), lambda i: (0, i, 0))
    gain = pl.BlockSpec((1, ds), lambda i: (0, 0))
    return pl.pallas_call(
        body, name="mix_out_fwd", grid=(nt,),
        in_specs=[hd, hd3, hd3, gain, gain,
                  pl.BlockSpec((ds, ds), lambda i: (0, 0)),
                  pl.BlockSpec(wout.shape, lambda i: (0, 0)),
                  tok, pl.BlockSpec((None, N_MOD, d), lambda i: (i // tpb, 0, 0))],
        out_specs=[tok, pl.BlockSpec((tm, 2 * ds), lambda i: (i, 0)), tok, hd, hd],
        out_shape=[jax.ShapeDtypeStruct((t, d), F32), jax.ShapeDtypeStruct((t, 2 * ds), BF16),
                   jax.ShapeDtypeStruct((t, d), F32), jax.ShapeDtypeStruct((t, ds), F32),
                   jax.ShapeDtypeStruct((t, ds), F32)],
        compiler_params=_params("arbitrary"),
    )(osb, oc, lse, gsb, gdil, gmat, wout, x, mod)


def _mix_out_bwd(dxo, m, mod, wout, osb, odil, gsb, gdil, gmat, tm):
    t, d = dxo.shape
    ds = osb.shape[1]
    nt = t // tm
    nb = mod.shape[0]
    tpb = nt // nb

    def body(dxo_ref, m_ref, mod_ref, w_ref, osb_ref, odil_ref, gsb_ref, gdil_ref, gm_ref,
             dm_ref, dosb_ref, dodil_ref, dldil_ref, dmod_ref, dg_ref):
        i = pl.program_id(0)
        dxo_v = dxo_ref[...]
        dm = (mod_ref[5:6, :] * dxo_v).astype(BF16)
        dm_ref[...] = dm
        dgt = jnp.sum(m_ref[...] * dxo_v, axis=0, keepdims=True)
        don = _dot_nt(dm, w_ref[...])
        gm = gm_ref[...]

        @pl.when(i % tpb == 0)
        def _():
            dmod_ref[...] = jnp.zeros_like(dmod_ref)

        @pl.when(i == 0)
        def _():
            dg_ref[...] = jnp.zeros_like(dg_ref)

        dmod_ref[2:3, :] += dgt
        groups = ((osb_ref, gsb_ref, dosb_ref), (odil_ref, gdil_ref, dodil_ref))
        for k, (o_ref, g_ref, do_ref) in enumerate(groups):
            o = o_ref[...]
            dn_out = don[:, k * ds:(k + 1) * ds]
            r = lax.rsqrt(_head_mean(o * o, gm) + EPS)
            n = o * r
            dg_ref[0:1, k * ds:(k + 1) * ds] += jnp.sum(dn_out * n, axis=0, keepdims=True)
            dn = dn_out * g_ref[...]
            do = r * (dn - n * _head_mean(dn * n, gm))
            do_ref[...] = do
            if k == 1:
                dldil_ref[...] = _head_mean(do * o, gm) * float(HEAD_DIM)

    tok = pl.BlockSpec((tm, d), lambda i: (i, 0))
    hd = pl.BlockSpec((tm, ds), lambda i: (i, 0))
    gain = pl.BlockSpec((1, ds), lambda i: (0, 0))
    hds = jax.ShapeDtypeStruct((t, ds), F32)
    return pl.pallas_call(
        body, name="mix_out_bwd", grid=(nt,),
        in_specs=[tok, tok, pl.BlockSpec((None, N_MOD, d), lambda i: (i // tpb, 0, 0)),
                  pl.BlockSpec(wout.shape, lambda i: (0, 0)), hd, hd, gain, gain,
                  pl.BlockSpec((ds, ds), lambda i: (0, 0))],
        out_specs=[tok, hd, hd, hd,
                   pl.BlockSpec((None, 8, d), lambda i: (i // tpb, 0, 0)),
                   pl.BlockSpec((8, 2 * ds), lambda i: (0, 0))],
        out_shape=[jax.ShapeDtypeStruct((t, d), BF16), hds, hds, hds,
                   jax.ShapeDtypeStruct((nb, 8, d), F32), jax.ShapeDtypeStruct((8, 2 * ds), F32)],
        compiler_params=_params("arbitrary"),
    )(dxo, m, mod, wout, osb, odil, gsb, gdil, gmat)


def _loss_head(x, target, g, tm):
    t, d = x.shape

    def body(x_ref, t_ref, g_ref, dx_ref, acc_ref):
        @pl.when(pl.program_id(0) == 0)
        def _():
            acc_ref[...] = jnp.zeros_like(acc_ref)

        n, r = _norm(x_ref[...])
        gv = g_ref[...]
        err = n * gv - t_ref[...]
        dy = err * (1.0 / d)
        acc_ref[0:1, :] += jnp.sum(err * err, axis=0, keepdims=True)
        acc_ref[1:2, :] += jnp.sum(dy * n, axis=0, keepdims=True)
        dn = dy * gv
        dx_ref[...] = r * (dn - n * jnp.mean(dn * n, axis=-1, keepdims=True))

    tok = pl.BlockSpec((tm, d), lambda i: (i, 0))
    return pl.pallas_call(
        body, name="loss_head", grid=(t // tm,),
        in_specs=[tok, tok, pl.BlockSpec((1, d), lambda i: (0, 0))],
        out_specs=[tok, pl.BlockSpec((8, d), lambda i: (0, 0))],
        out_shape=[jax.ShapeDtypeStruct((t, d), F32), jax.ShapeDtypeStruct((8, d), F32)],
        compiler_params=_params("arbitrary"),
    )(x, target, g)


def _row_tile(rows):
    if rows <= 256:
        return rows
    for cand in range(256, 15, -16):
        if rows % cand == 0:
            return cand
    return rows


def _adamw(w, parts, m, v, name):
    rows, cols = w.shape
    n_parts = parts.shape[0]
    tr = _row_tile(rows)
    c1 = 1.0 / (1.0 - ADAM_B1 ** ADAM_STEP)
    c2 = 1.0 / (1.0 - ADAM_B2 ** ADAM_STEP)

    def body(w_ref, p_ref, m_ref, v_ref, g_ref, d_ref, nm_ref, nv_ref):
        g = p_ref[0].astype(F32)
        for i in range(1, n_parts):
            g = g + p_ref[i].astype(F32)
        nm = ADAM_B1 * m_ref[...] + (1.0 - ADAM_B1) * g
        nv = ADAM_B2 * v_ref[...] + (1.0 - ADAM_B2) * (g * g)
        g_ref[...] = g
        nm_ref[...] = nm
        nv_ref[...] = nv
        d_ref[...] = -ADAM_LR * ((nm * c1) / (jnp.sqrt(nv * c2) + ADAM_EPS) + ADAM_WD * w_ref[...])

    blk = pl.BlockSpec((tr, cols), lambda i: (i, 0))
    out = jax.ShapeDtypeStruct((rows, cols), F32)
    return pl.pallas_call(
        body, name=name, grid=(rows // tr,),
        in_specs=[blk, pl.BlockSpec((n_parts, tr, cols), lambda i: (0, i, 0)), blk, blk],
        out_specs=[blk, blk, blk, blk], out_shape=[out, out, out, out],
        compiler_params=_params("arbitrary"),
    )(w, parts, m, v)


def _t5_bucket(n):
    max_exact = N_BUCKETS // 2
    nf = np.maximum(n, 1).astype(np.float32)
    large = max_exact + (np.log(nf / max_exact) / math.log(MAX_DISTANCE / max_exact)
                         * (N_BUCKETS - max_exact)).astype(np.int32)
    large = np.minimum(large, N_BUCKETS - 1)
    return np.where(n < max_exact, n, large).astype(np.int32)


def _bucket_onehot():
    table = np.zeros((len(DILATIONS), 2 * DIL_BLOCK + 1, N_BUCKETS), np.float32)
    for i, dil in enumerate(DILATIONS):
        buckets = _t5_bucket(np.arange(DIL_BLOCK + 1) * dil)
        for m in range(DIL_BLOCK + 1):
            table[i, m, buckets[DIL_BLOCK - m]] = 1.0
    return table


def _bias_blocks(rel_bias):
    row = jnp.einsum("cmn,nh->chm", _bucket_onehot(), rel_bias, precision=lax.Precision.HIGHEST)
    n_cfg, n_heads, width = row.shape
    tiled = jnp.tile(row, (1, 1, DIL_BLOCK))[..., :DIL_BLOCK * (width - 1)]
    return tiled.reshape(n_cfg, n_heads, DIL_BLOCK, width - 1)


def _bias_blocks_bwd(dblocks):
    n_cfg, n_heads = dblocks.shape[:2]
    width = 2 * DIL_BLOCK + 1
    flat = dblocks.reshape(n_cfg, n_heads, DIL_BLOCK * (width - 1))
    flat = jnp.pad(flat, ((0, 0), (0, 0), (0, DIL_BLOCK)))
    drow = jnp.sum(flat.reshape(n_cfg, n_heads, DIL_BLOCK, width), axis=2)
    return jnp.einsum("chm,cmn->nh", drow, _bucket_onehot(), precision=lax.Precision.HIGHEST)


def _to_residue(a, nb, dil):
    t, f = a.shape
    seq = t // nb
    return a.reshape(nb, seq // dil, dil, f).transpose(0, 2, 1, 3).reshape(t, f)


def _from_residue(a, nb, dil):
    t, f = a.shape
    seq = t // nb
    return a.reshape(nb, dil, seq // dil, f).transpose(0, 2, 1, 3).reshape(t, f)


def _stack_residue(a, nb):
    return jnp.stack([_to_residue(a, nb, dil) for dil in DILATIONS])


def _pad_to(a, axis, size):
    pad = [(0, 0)] * a.ndim
    pad[axis] = (0, size - a.shape[axis])
    return jnp.pad(a, pad)


def _lane_pad(n):
    return -(-n // LANES) * LANES


def _local_step(x, target, mod, gains, weights, rel_bias, tm, distributed):
    nb, seq, d = x.shape
    t = nb * seq
    g_ffn1, g_mix, g_sb, g_dil, g_ffn2, g_final = gains
    wg1, wu1, wd1 = weights[:3]
    x0 = x.reshape(t, d)
    ds = g_sb.shape[1]
    gmat = jnp.asarray(np.kron(np.eye(ds // HEAD_DIM), np.ones((HEAD_DIM, HEAD_DIM))), BF16)
    bias = _bias_blocks(rel_bias)

    def beside(arrays, scatter):
        return _Exchange(arrays, scatter) if distributed else None

    tp, tg = min(PROJ_TILE, seq), min(GRAD_TILE, t)

    (x1, f1, gate1, up1), got = _ffn_fwd(x0, mod, g_ffn1, wg1, wu1, wd1, 0, tp, beside(weights[3:5], False))
    win, wout = got if distributed else weights[3:5]
    wout2 = wout.reshape(-1, d)
    qkv, h2 = _qkv_fwd(x1, mod, g_mix, win, tp)
    (osb, csb), got = _sb_fwd(qkv, nb, seq, beside(weights[5:7], False))
    wg2, wu2 = got if distributed else weights[5:7]
    qkvd = _stack_residue(qkv[:, 3 * ds:], nb)
    (oc_r, lse_r), got = _dil_fwd(qkvd, bias, nb, seq, beside(weights[7:8], False))
    wd2 = got[0] if distributed else weights[7]
    oc = jnp.stack([_from_residue(oc_r[i], nb, dil) for i, dil in enumerate(DILATIONS)])
    lse = jnp.stack([_from_residue(lse_r[i], nb, dil) for i, dil in enumerate(DILATIONS)])
    x2, on, mix, odil, ldil = _mix_out_fwd(osb, oc, lse, g_sb, g_dil, gmat, wout2, x1, mod, tm)
    (x3, f3, gate3, up3), _ = _ffn_fwd(x2, mod, g_ffn2, wg2, wu2, wd2, 2, tp)
    dx3, head = _loss_head(x3, target.reshape(t, d), g_final, tm)
    loss_sum = 0.5 * jnp.sum(head[0]) / d
    dg_final = head[1:2]

    (dx2, dgate3, dup3, act3, h3, df3, dmod3, dg_ffn2), _ = _ffn_bwd(
        dx3, x2, f3, mod, g_ffn2, gate3, up3, wg2, wu2, wd2, 2, tp)
    gwg2, gwu2, gwd2 = _ffn_weight_grads(h3, dgate3, dup3, act3, df3, tg, 2)

    dm, dosb, dodil, dldil, dmod2b, dg_heads = _mix_out_bwd(
        dx2, mix, mod, wout2, osb, odil, g_sb, g_dil, gmat, tm)
    n_out = wout.shape[0]
    gwout = _mm_tn(on, dm,
                   pl.BlockSpec((tg, wout.shape[1]), lambda j, i: (i, j)),
                   pl.BlockSpec((tg, d), lambda j, i: (i, 0)),
                   wout.shape, pl.BlockSpec((None, wout.shape[1], d), lambda j, i: (j, 0, 0)),
                   (n_out, t // tg), "grad_wout")

    (dq_sb, dk_sb, dv_sb), parts_late = _sb_bwd(qkv, dosb, csb, nb, seq,
                                                beside([gwout, gwg2, gwu2, gwd2], True))
    dq_r, dk_r, dv_r, dbias = _dil_bwd(qkvd, bias, _stack_residue(dodil, nb), _stack_residue(ldil, nb),
                                       _stack_residue(dldil, nb), nb, seq)
    dqkv_dil = [sum(_from_residue(a[i], nb, dil) for i, dil in enumerate(DILATIONS)).astype(BF16)
                for a in (dq_r, dk_r, dv_r)]
    dqkv = jnp.concatenate([dq_sb, dk_sb, dv_sb] + dqkv_dil, axis=1)
    drel = _bias_blocks_bwd(dbias)

    dx1, dmod2a, dg_mix = _qkv_bwd(dqkv, dx2, x1, mod, g_mix, win, tp)
    n_in, _, cs = win.shape
    gwin = _mm_tn(h2, dqkv,
                  pl.BlockSpec((tg, d), lambda j, i: (i, 0)),
                  pl.BlockSpec((tg, cs), lambda j, i: (i, j)),
                  win.shape, pl.BlockSpec((None, d, cs), lambda j, i: (j, 0, 0)),
                  (n_in, t // tg), "grad_win")

    (dx0, dgate1, dup1, act1, h1, df1, dmod1, dg_ffn1), parts_mid = _ffn_bwd(
        dx1, x0, f1, mod, g_ffn1, gate1, up1, wg1, wu1, wd1, 0, tp, beside([gwin], True))
    gw1 = _ffn_weight_grads(h1, dgate1, dup1, act1, df1, tg, 0, stream=distributed)

    dmod = jnp.concatenate([dmod1[:, 0:3], dmod2a[:, 0:2], dmod2b[:, 2:3], dmod3[:, 0:3]], axis=1)
    wgrads = tuple(gw1) + (tuple(parts_mid + parts_late) if distributed else (gwin, gwout, gwg2, gwu2, gwd2))
    ggrads = (dg_ffn1[0:1], dg_mix[0:1], dg_heads[0:1], drel, dg_ffn2[0:1], dg_final)
    return loss_sum, dx0.reshape(nb, seq, d), wgrads, dmod, ggrads


def kernel(x, c, w_ada, b_ada, g_ffn1, w1_gate, w1_up, w1_down, g_mix, w_in, g_sb_out, g_dil_out, w_out, rel_bias, g_ffn2, w2_gate, w2_up, w2_down, g_final, loss_target, m_w_ada, m_b_ada, m_g_ffn1, m_w1_gate, m_w1_up, m_w1_down, m_g_mix, m_w_in, m_g_sb_out, m_g_dil_out, m_w_out, m_rel_bias, m_g_ffn2, m_w2_gate, m_w2_up, m_w2_down, m_g_final, v_w_ada, v_b_ada, v_g_ffn1, v_w1_gate, v_w1_up, v_w1_down, v_g_mix, v_w_in, v_g_sb_out, v_g_dil_out, v_w_out, v_rel_bias, v_g_ffn2, v_w2_gate, v_w2_up, v_w2_down, v_g_final):
    nb, seq, d = x.shape
    me = 4 * lax.axis_index("x") + 2 * lax.axis_index("y") + lax.axis_index("c")
    tm = min(TOKEN_TILE, seq)
    fs = w1_gate.shape[2]
    fs_pad = _lane_pad(fs)
    ada_cols = w_ada.shape[2]

    def col_shard(w):
        return _pad_to(w[0].astype(BF16), 1, fs_pad)

    def row_shard(w):
        return _pad_to(w[0].astype(BF16), 0, fs_pad)

    shards = [col_shard(w1_gate), col_shard(w1_up), row_shard(w1_down), w_in[0].astype(BF16),
              w_out[0].astype(BF16), col_shard(w2_gate), col_shard(w2_up), row_shard(w2_down)]
    gathered = _exchange([_pad_to(c, 0, 8)] + shards[:3], False, "gather_first", relay=True)
    c_all = gathered[0][:, :nb].reshape(N_DEV * nb, d)
    weights = gathered[1:] + shards[3:]

    b_cols = lax.dynamic_slice(b_ada, (0, me * ada_cols), (1, ada_cols))
    mod_part = _ada_fwd(c_all, w_ada[0], b_cols)
    mod_all = _exchange([mod_part], False, "gather_mod")[0]
    mod = lax.dynamic_slice(mod_all, (0, me * nb, 0), (N_DEV, nb, ada_cols))
    mod = mod.transpose(1, 0, 2).reshape(nb, N_MOD, d)

    n_sb = g_sb_out.shape[1] * g_sb_out.shape[2]
    gains = (g_ffn1, g_mix, g_sb_out.reshape(1, n_sb), g_dil_out.reshape(1, -1), g_ffn2,
             g_final.reshape(1, d))
    loss_sum, grad_x, parts, dmod, ggrads = _local_step(x, loss_target, mod, gains, weights, rel_bias, tm, True)
    loss = lax.psum(loss_sum, ("x", "y", "c"))

    dg_ffn1, dg_mix, dg_heads, drel, dg_ffn2, dg_final = ggrads
    width = max(d, dg_heads.shape[1], drel.size)
    small = jnp.concatenate(
        [_pad_to(a.reshape(1, -1), 1, width) for a in (dg_ffn1, dg_mix, dg_ffn2, dg_final, dg_heads, drel)]
        + [jnp.zeros((2, width), F32)], axis=0)
    dmod_all, small_all = _exchange([_pad_to(dmod.reshape(nb, N_MOD * d), 0, 8), small], False, "gather_small")
    dmod_all = dmod_all[:, :nb].reshape(N_DEV * nb, N_MOD * d)
    dmod_cols = lax.dynamic_slice(dmod_all, (0, me * ada_cols), (N_DEV * nb, ada_cols))
    gw_ada, gb_ada = _ada_bwd(c_all, dmod_cols, dmod_all)

    def small_part(row, size, shape):
        return small_all[:, row, :size].reshape((N_DEV,) + shape)

    n_rel = rel_bias.shape
    updates = {
        "w_ada": (w_ada[0], gw_ada[None], m_w_ada[0], v_w_ada[0]),
        "b_ada": (b_ada, gb_ada[None], m_b_ada, v_b_ada),
        "g_ffn1": (g_ffn1, small_part(0, d, (1, d)), m_g_ffn1, v_g_ffn1),
        "w1_gate": (w1_gate[0], parts[0][:, :, :fs], m_w1_gate[0], v_w1_gate[0]),
        "w1_up": (w1_up[0], parts[1][:, :, :fs], m_w1_up[0], v_w1_up[0]),
        "w1_down": (w1_down[0], parts[2][:, :fs, :], m_w1_down[0], v_w1_down[0]),
        "g_mix": (g_mix, small_part(1, d, (1, d)), m_g_mix, v_g_mix),
        "w_in": (w_in[0], parts[3], m_w_in[0], v_w_in[0]),
        "g_sb_out": (g_sb_out[0], small_all[:, 4, :n_sb].reshape((N_DEV,) + g_sb_out.shape[1:]),
                     m_g_sb_out[0], v_g_sb_out[0]),
        "g_dil_out": (g_dil_out[0], small_all[:, 4, n_sb:dg_heads.shape[1]].reshape((N_DEV,) + g_dil_out.shape[1:]),
                      m_g_dil_out[0], v_g_dil_out[0]),
        "w_out": (w_out[0], parts[4], m_w_out[0], v_w_out[0]),
        "rel_bias": (rel_bias, small_part(5, drel.size, n_rel), m_rel_bias, v_rel_bias),
        "g_ffn2": (g_ffn2, small_part(2, d, (1, d)), m_g_ffn2, v_g_ffn2),
        "w2_gate": (w2_gate[0], parts[5][:, :, :fs], m_w2_gate[0], v_w2_gate[0]),
        "w2_up": (w2_up[0], parts[6][:, :, :fs], m_w2_up[0], v_w2_up[0]),
        "w2_down": (w2_down[0], parts[7][:, :fs, :], m_w2_down[0], v_w2_down[0]),
        "g_final": (g_final.reshape(1, d), small_part(3, d, (1, d)), m_g_final.reshape(1, d), v_g_final.reshape(1, d)),
    }
    shapes = {"w_ada": w_ada.shape, "b_ada": b_ada.shape, "g_ffn1": g_ffn1.shape, "w1_gate": w1_gate.shape,
              "w1_up": w1_up.shape, "w1_down": w1_down.shape, "g_mix": g_mix.shape, "w_in": w_in.shape,
              "g_sb_out": g_sb_out.shape, "g_dil_out": g_dil_out.shape, "w_out": w_out.shape,
              "rel_bias": rel_bias.shape, "g_ffn2": g_ffn2.shape, "w2_gate": w2_gate.shape,
              "w2_up": w2_up.shape, "w2_down": w2_down.shape, "g_final": g_final.shape}
    grads, deltas, new_m, new_v = [], [], [], []
    for name, (w, p, m, v) in updates.items():
        g, dw, nm, nv = _adamw(w, p, m, v, f"adamw_{name}")
        grads.append(g.reshape(shapes[name]))
        deltas.append(dw.reshape(shapes[name]))
        new_m.append(nm.reshape(shapes[name]))
        new_v.append(nv.reshape(shapes[name]))
    return (loss, grad_x, *grads, *deltas, *new_m, *new_v)
```

```python
import functools
import math

import numpy as np
import jax
import jax.numpy as jnp
from jax import lax
from jax.experimental import pallas as pl
from jax.experimental.pallas import tpu as pltpu

F32 = jnp.float32
BF16 = jnp.bfloat16

EPS = 1e-6
NEG_INF = -1e30
HEAD_DIM = 64
LANES = 128
DIL_BLOCK = 128
DILATIONS = (1, 4, 16)
N_BUCKETS = 32
MAX_DISTANCE = 2048
N_MOD = 9
N_DEV = 8
SB_BLOCK = 256
SB_HEADS = 4
SB_WIDTH = SB_HEADS * HEAD_DIM
DIL_HEADS = 4
DIL_WIDTH = DIL_HEADS * HEAD_DIM
TOKEN_TILE = 512
PROJ_TILE = 1024
GRAD_TILE = 1024
VMEM_LIMIT_BYTES = 56 * 1024 * 1024

ADAM_LR = 0.001
ADAM_B1 = 0.9
ADAM_B2 = 0.999
ADAM_EPS = 1e-08
ADAM_WD = 0.01
ADAM_STEP = 10

NT_DIMS = (((1,), (1,)), ((), ()))
TN_DIMS = (((0,), (0,)), ((), ()))


def _params(*sem):
    return pltpu.CompilerParams(dimension_semantics=sem, vmem_limit_bytes=VMEM_LIMIT_BYTES)


def _once(spec):
    return pl.BlockSpec(spec.block_shape, spec.index_map, pipeline_mode=pl.Buffered(1))


def _dot(a, b):
    return jnp.dot(a, b, preferred_element_type=F32)


def _dot_nt(a, b):
    return lax.dot_general(a, b, NT_DIMS, preferred_element_type=F32)


def _dot_tn(a, b):
    return lax.dot_general(a, b, TN_DIMS, preferred_element_type=F32)


def _split_dot(a, b):
    hi = a.astype(BF16)
    lo = (a - hi.astype(F32)).astype(BF16)
    return _dot(hi, b) + _dot(lo, b)


def _sigmoid(z):
    return 1.0 / (1.0 + jnp.exp(-z))


def _norm(x):
    r = lax.rsqrt(jnp.mean(x * x, axis=-1, keepdims=True) + EPS)
    return x * r, r


def _modulate(x, g, mod_ref, k):
    n, _ = _norm(x)
    shift = mod_ref[3 * k:3 * k + 1, :]
    scale = mod_ref[3 * k + 1:3 * k + 2, :]
    return n * g * (1.0 + scale) + shift


def _modulate_bwd(dh, x, g, mod_ref, k):
    n, r = _norm(x)
    scale = mod_ref[3 * k + 1:3 * k + 2, :]
    dshift = jnp.sum(dh, axis=0, keepdims=True)
    dscale = jnp.sum(dh * n * g, axis=0, keepdims=True)
    dg = jnp.sum(dh * n * (1.0 + scale), axis=0, keepdims=True)
    dn = dh * g * (1.0 + scale)
    dx = r * (dn - n * jnp.mean(dn * n, axis=-1, keepdims=True))
    return dx, dshift, dscale, dg


class _Exchange:
    def __init__(self, arrays, scatter, relay=False):
        assert not (scatter and relay)
        self.arrays = list(arrays)
        self.scatter = scatter
        self.relay = relay
        self.n = len(self.arrays)
        self.out_shape = [
            jax.ShapeDtypeStruct((N_DEV,) + tuple(a.shape[1:] if scatter else a.shape), a.dtype)
            for a in self.arrays]
        n_remote = self.n * (N_DEV - 1)
        self.scratch_shapes = [pltpu.SemaphoreType.DMA((n_remote,)), pltpu.SemaphoreType.DMA((n_remote,)),
                               pltpu.SemaphoreType.DMA((self.n,))]

    def _copies(self, in_refs, out_refs, sems):
        send_sems, recv_sems, local_sems = sems
        x, y, c = lax.axis_index("x"), lax.axis_index("y"), lax.axis_index("c")
        me = 4 * x + 2 * y + c
        local, remote, relayed = [], {}, {}
        for a in range(self.n):
            src = in_refs[a].at[me] if self.scatter else in_refs[a]
            local.append(pltpu.make_async_copy(src, out_refs[a].at[me], local_sems.at[a]))
            for k in range(1, N_DEV):
                px = 1 - x if k & 4 else x
                py = 1 - y if k & 2 else y
                pc = 1 - c if k & 1 else c
                sem = a * (N_DEV - 1) + k - 1
                if self.relay and k & 1 and k > 1:
                    slot = 4 * px + 2 * py + c
                    relayed[a, k] = pltpu.make_async_remote_copy(
                        src_ref=out_refs[a].at[slot], dst_ref=out_refs[a].at[slot],
                        send_sem=send_sems.at[sem], recv_sem=recv_sems.at[sem],
                        device_id=(x, y, 1 - c), device_id_type=pl.DeviceIdType.MESH)
                    continue
                src = in_refs[a].at[4 * px + 2 * py + pc] if self.scatter else in_refs[a]
                remote[a, k] = pltpu.make_async_remote_copy(
                    src_ref=src, dst_ref=out_refs[a].at[me],
                    send_sem=send_sems.at[sem], recv_sem=recv_sems.at[sem],
                    device_id=(px, py, pc), device_id_type=pl.DeviceIdType.MESH)
        return local, remote, relayed

    def start(self, in_refs, out_refs, sems):
        local, remote, _ = self._copies(in_refs, out_refs, sems)
        for cp in local + list(remote.values()):
            cp.start()

    def wait(self, in_refs, out_refs, sems):
        local, remote, relayed = self._copies(in_refs, out_refs, sems)
        for (a, k), cp in relayed.items():
            remote[a, k - 1].wait_recv()
            cp.start()
        for (a, k), cp in remote.items():
            if (a, k + 1) not in relayed:
                cp.wait_recv()
        for cp in relayed.values():
            cp.wait_recv()
        for cp in list(remote.values()) + list(relayed.values()):
            cp.wait_send()
        for cp in local:
            cp.wait()


def _call(body, *, name, args, in_specs, out_specs, out_shape, scratch_shapes=(), grid=(),
          params=None, exchange=None):
    n_in, n_out = len(args), len(out_shape)
    if exchange is None:
        outs = pl.pallas_call(
            body, name=name, grid=grid, in_specs=list(in_specs), out_specs=list(out_specs),
            out_shape=list(out_shape), scratch_shapes=list(scratch_shapes), compiler_params=params,
        )(*args)
        return list(outs), []
    n_ex = exchange.n

    def wrapped(*refs):
        ins, refs = refs[:n_in], refs[n_in:]
        ex_in, refs = refs[:n_ex], refs[n_ex:]
        outs, refs = refs[:n_out], refs[n_out:]
        ex_out, refs = refs[:n_ex], refs[n_ex:]
        scratch, sems = refs[:len(refs) - 3], refs[len(refs) - 3:]
        if not grid:
            exchange.start(ex_in, ex_out, sems)
            body(*ins, *outs, *scratch)
            exchange.wait(ex_in, ex_out, sems)
            return
        first = functools.reduce(jnp.logical_and, [pl.program_id(a) == 0 for a in range(len(grid))])
        last = functools.reduce(jnp.logical_and, [pl.program_id(a) == grid[a] - 1 for a in range(len(grid))])

        @pl.when(first)
        def _():
            exchange.start(ex_in, ex_out, sems)

        body(*ins, *outs, *scratch)

        @pl.when(last)
        def _():
            exchange.wait(ex_in, ex_out, sems)

    any_spec = pl.BlockSpec(memory_space=pl.ANY)
    outs = pl.pallas_call(
        wrapped, name=name, grid=grid,
        in_specs=list(in_specs) + [any_spec] * n_ex, out_specs=list(out_specs) + [any_spec] * n_ex,
        out_shape=list(out_shape) + exchange.out_shape,
        scratch_shapes=list(scratch_shapes) + exchange.scratch_shapes, compiler_params=params,
    )(*args, *exchange.arrays)
    return list(outs[:n_out]), list(outs[n_out:])


def _exchange(arrays, scatter, name, relay=False):
    return _call(lambda: None, name=name, args=(), in_specs=(), out_specs=(), out_shape=(),
                 exchange=_Exchange(arrays, scatter, relay))[1]


def _ada_fwd(c_all, w, b):
    def body(c_ref, w_ref, b_ref, o_ref):
        cv = c_ref[...]
        s = (cv * _sigmoid(cv)).astype(BF16)
        o_ref[...] = _dot(s, w_ref[...].astype(BF16)) + b_ref[...]

    return pl.pallas_call(
        body, name="ada_fwd", out_shape=jax.ShapeDtypeStruct((c_all.shape[0], w.shape[1]), F32),
        compiler_params=pltpu.CompilerParams(vmem_limit_bytes=VMEM_LIMIT_BYTES),
    )(c_all, w, b)


def _ada_bwd(c_all, dmod_cols, dmod_all):
    def body(c_ref, dc_ref, da_ref, gw_ref, gb_ref):
        cv = c_ref[...]
        s = cv * _sigmoid(cv)
        gw_ref[...] = lax.dot_general(s, dc_ref[...], TN_DIMS, preferred_element_type=F32,
                                      precision=lax.Precision.HIGHEST)
        gb_ref[...] = jnp.sum(da_ref[...], axis=0, keepdims=True)

    return pl.pallas_call(
        body, name="ada_bwd",
        out_shape=(jax.ShapeDtypeStruct((c_all.shape[1], dmod_cols.shape[1]), F32),
                   jax.ShapeDtypeStruct((1, dmod_all.shape[1]), F32)),
        compiler_params=pltpu.CompilerParams(vmem_limit_bytes=VMEM_LIMIT_BYTES),
    )(c_all, dmod_cols, dmod_all)


def _ffn_fwd(x, mod, g, wg, wu, wd, k, tm, exchange=None):
    t, d = x.shape
    ns, _, fs = wg.shape
    nt = t // tm
    tpb = nt // mod.shape[0]

    def body(x_ref, mod_ref, g_ref, wg_ref, wu_ref, wd_ref, xo_ref, f_ref, gg_ref, uu_ref, h_sc, acc):
        j = pl.program_id(1)

        @pl.when(j == 0)
        def _():
            h_sc[...] = _modulate(x_ref[...], g_ref[...], mod_ref, k).astype(BF16)
            acc[...] = jnp.zeros_like(acc)

        h = h_sc[...]
        gate = _dot(h, wg_ref[...])
        up = _dot(h, wu_ref[...])
        act = gate * _sigmoid(gate) * up
        acc[...] += _dot(act.astype(BF16), wd_ref[...])
        gg_ref[...] = gate.astype(BF16)
        uu_ref[...] = up.astype(BF16)

        @pl.when(j == ns - 1)
        def _():
            f = acc[...]
            f_ref[...] = f.astype(BF16)
            xo_ref[...] = x_ref[...] + 0.5 * mod_ref[3 * k + 2:3 * k + 3, :] * f

    tok = pl.BlockSpec((tm, d), lambda i, j: (i, 0))
    hid = pl.BlockSpec((None, tm, fs), lambda i, j: (j, i, 0))
    return _call(
        body, name=f"ffn_fwd{k}", grid=(nt, ns), args=(x, mod, g, wg, wu, wd),
        in_specs=[_once(tok),
                  pl.BlockSpec((None, N_MOD, d), lambda i, j: (i // tpb, 0, 0)),
                  pl.BlockSpec((1, d), lambda i, j: (0, 0)),
                  pl.BlockSpec((None, d, fs), lambda i, j: (j, 0, 0)),
                  pl.BlockSpec((None, d, fs), lambda i, j: (j, 0, 0)),
                  pl.BlockSpec((None, fs, d), lambda i, j: (j, 0, 0))],
        out_specs=[tok, tok, hid, hid],
        out_shape=[jax.ShapeDtypeStruct((t, d), F32), jax.ShapeDtypeStruct((t, d), BF16),
                   jax.ShapeDtypeStruct((ns, t, fs), BF16), jax.ShapeDtypeStruct((ns, t, fs), BF16)],
        scratch_shapes=[pltpu.VMEM((tm, d), BF16), pltpu.VMEM((tm, d), F32)],
        params=_params("arbitrary", "arbitrary"), exchange=exchange)


def _ffn_bwd(dxo, x, f, mod, g, gate, up, wg, wu, wd, k, tm, exchange=None):
    t, d = x.shape
    ns, _, fs = wg.shape
    nt = t // tm
    nb = mod.shape[0]
    tpb = nt // nb

    def body(dxo_ref, x_ref, f_ref, mod_ref, g_ref, gg_ref, uu_ref, wg_ref, wu_ref, wd_ref,
             dx_ref, dgg_ref, duu_ref, act_ref, h_ref, df_ref, dmod_ref, dg_ref, acc):
        i, j = pl.program_id(0), pl.program_id(1)

        @pl.when(j == 0)
        def _():
            df = 0.5 * mod_ref[3 * k + 2:3 * k + 3, :] * dxo_ref[...]
            df_ref[...] = df.astype(BF16)
            h_ref[...] = _modulate(x_ref[...], g_ref[...], mod_ref, k).astype(BF16)
            acc[...] = jnp.zeros_like(acc)

        dact = _dot_nt(df_ref[...], wd_ref[...])
        gv, uv = gg_ref[...].astype(F32), uu_ref[...].astype(F32)
        sig = _sigmoid(gv)
        s = gv * sig
        act_ref[...] = (s * uv).astype(BF16)
        dup = (dact * s).astype(BF16)
        dgate = (dact * uv * (sig * (1.0 + gv * (1.0 - sig)))).astype(BF16)
        duu_ref[...] = dup
        dgg_ref[...] = dgate
        acc[...] += _dot_nt(dgate, wg_ref[...]) + _dot_nt(dup, wu_ref[...])

        @pl.when(j == ns - 1)
        def _():
            dx, dshift, dscale, dg = _modulate_bwd(acc[...], x_ref[...], g_ref[...], mod_ref, k)
            dxo_v = dxo_ref[...]
            dx_ref[...] = dxo_v + dx
            dgt = jnp.sum(0.5 * f_ref[...].astype(F32) * dxo_v, axis=0, keepdims=True)

            @pl.when(i % tpb == 0)
            def _():
                dmod_ref[...] = jnp.zeros_like(dmod_ref)

            @pl.when(i == 0)
            def _():
                dg_ref[...] = jnp.zeros_like(dg_ref)

            dmod_ref[0:1, :] += dshift
            dmod_ref[1:2, :] += dscale
            dmod_ref[2:3, :] += dgt
            dg_ref[0:1, :] += dg

    tok = pl.BlockSpec((tm, d), lambda i, j: (i, 0))
    hid = pl.BlockSpec((None, tm, fs), lambda i, j: (j, i, 0))
    return _call(
        body, name=f"ffn_bwd{k}", grid=(nt, ns), args=(dxo, x, f, mod, g, gate, up, wg, wu, wd),
        in_specs=[_once(tok), _once(tok), _once(tok),
                  pl.BlockSpec((None, N_MOD, d), lambda i, j: (i // tpb, 0, 0)),
                  pl.BlockSpec((1, d), lambda i, j: (0, 0)),
                  hid, hid,
                  pl.BlockSpec((None, d, fs), lambda i, j: (j, 0, 0)),
                  pl.BlockSpec((None, d, fs), lambda i, j: (j, 0, 0)),
                  pl.BlockSpec((None, fs, d), lambda i, j: (j, 0, 0))],
        out_specs=[tok, hid, hid, hid, tok, tok,
                   pl.BlockSpec((None, 8, d), lambda i, j: (i // tpb, 0, 0)),
                   pl.BlockSpec((8, d), lambda i, j: (0, 0))],
        out_shape=[jax.ShapeDtypeStruct((t, d), F32),
                   jax.ShapeDtypeStruct((ns, t, fs), BF16), jax.ShapeDtypeStruct((ns, t, fs), BF16),
                   jax.ShapeDtypeStruct((ns, t, fs), BF16),
                   jax.ShapeDtypeStruct((t, d), BF16), jax.ShapeDtypeStruct((t, d), BF16),
                   jax.ShapeDtypeStruct((nb, 8, d), F32), jax.ShapeDtypeStruct((8, d), F32)],
        scratch_shapes=[pltpu.VMEM((tm, d), F32)],
        params=_params("arbitrary", "arbitrary"), exchange=exchange)


def _mm_tn(a, b, a_spec, b_spec, out_shape, out_spec, grid, name, exchange=None):
    block = tuple(out_shape[1:])
    last = grid[1] - 1
    flip = block[0] > block[1]
    if flip:
        block = block[::-1]

    def body(a_ref, b_ref, o_ref, acc):
        i = pl.program_id(1)

        @pl.when(i == 0)
        def _():
            acc[...] = jnp.zeros_like(acc)

        if flip:
            acc[...] += _dot_tn(b_ref[...], a_ref[...])
        else:
            acc[...] += _dot_tn(a_ref[...], b_ref[...])

        @pl.when(i == last)
        def _():
            total = acc[...]
            o_ref[...] = (total.T if flip else total).astype(o_ref.dtype)

    outs, sent = _call(
        body, name=name, grid=grid, args=(a, b), in_specs=[a_spec, b_spec], out_specs=[out_spec],
        out_shape=[jax.ShapeDtypeStruct(out_shape, BF16)],
        scratch_shapes=[pltpu.VMEM(block, F32)],
        params=_params("arbitrary", "arbitrary"), exchange=exchange)
    return (outs[0], sent) if exchange is not None else outs[0]


def _ffn_weight_grads(h, dgate, dup, act, df, tm, tag, stream=False):
    t, d = h.shape
    ns, _, fs = dgate.shape
    grid = (ns, t // tm)
    tok = pl.BlockSpec((tm, d), lambda j, i: (i, 0))
    hid = pl.BlockSpec((None, tm, fs), lambda j, i: (j, i, 0))
    col = pl.BlockSpec((None, d, fs), lambda j, i: (j, 0, 0))
    row = pl.BlockSpec((None, fs, d), lambda j, i: (j, 0, 0))
    gwg = _mm_tn(h, dgate, tok, hid, (ns, d, fs), col, grid, f"grad_wg{tag}")
    if not stream:
        gwu = _mm_tn(h, dup, tok, hid, (ns, d, fs), col, grid, f"grad_wu{tag}")
        gwd = _mm_tn(act, df, hid, tok, (ns, fs, d), row, grid, f"grad_wd{tag}")
        return gwg, gwu, gwd
    gwu, sent_g = _mm_tn(h, dup, tok, hid, (ns, d, fs), col, grid, f"grad_wu{tag}", _Exchange([gwg], True))
    gwd, sent_u = _mm_tn(act, df, hid, tok, (ns, fs, d), row, grid, f"grad_wd{tag}", _Exchange([gwu], True))
    sent_d = _exchange([gwd], True, "scatter_last")
    return sent_g[0], sent_u[0], sent_d[0]


def _stage_shape(rows, cols):
    return pltpu.VMEM((cols // LANES, rows, LANES), F32)


def _stage(value, stage_ref):
    for k in range(stage_ref.shape[0]):
        stage_ref[k] = value[:, k * LANES:(k + 1) * LANES]


def _to_residue_rows(stage_ref, dst_ref, dil):
    rows = stage_ref.shape[1] // dil
    for r in range(dil):
        for k in range(stage_ref.shape[0]):
            dst_ref[r, :, k * LANES:(k + 1) * LANES] = (
                stage_ref.at[k][pl.ds(r, rows, stride=dil), :].astype(dst_ref.dtype))


def _from_residue_rows(src_ref, stage_ref, dil):
    rows = stage_ref.shape[1] // dil
    chunks = range(stage_ref.shape[0])
    for r in range(dil):
        for k in chunks:
            stage_ref.at[k][pl.ds(r, rows, stride=dil), :] = src_ref[r, :, k * LANES:(k + 1) * LANES].astype(F32)
    return jnp.concatenate([stage_ref[k] for k in chunks], axis=1)


def _residue_shape(nb, seq, width, dil, dtype):
    return jax.ShapeDtypeStruct((nb, dil, seq // dil, width), dtype)


def _residue_spec(tm, tpb, cols, dil, col_block):
    return pl.BlockSpec((None, dil, tm // dil, cols),
                        lambda i, *rest: (i // tpb, 0, i % tpb, col_block(i, *rest)))


def _qkv_fwd(x, mod, g, win, tm):
    t, d = x.shape
    ns, _, cs = win.shape
    nt = t // tm
    nb = mod.shape[0]
    tpb = nt // nb
    seq = t // nb
    half = ns // 2
    n_res = len(DILATIONS) - 1

    def body(x_ref, mod_ref, g_ref, w_ref, sb_ref, dil_ref, *rest):
        res_refs, h_ref, sc = rest[:n_res], rest[n_res], rest[n_res + 1]
        j = pl.program_id(1)

        @pl.when(j == 0)
        def _():
            h_ref[...] = _modulate(x_ref[...], g_ref[...], mod_ref, 1).astype(BF16)

        res = _dot(h_ref[...], w_ref[...])

        @pl.when(j < half)
        def _():
            sb_ref[...] = res.astype(BF16)

        @pl.when(j >= half)
        def _():
            dil_ref[...] = res.astype(BF16)
            _stage(res, sc)
            for ref, dil in zip(res_refs, DILATIONS[1:]):
                _to_residue_rows(sc, ref, dil)

    def dil_col(i, j):
        return jnp.maximum(j - half, 0)

    tok = pl.BlockSpec((tm, d), lambda i, j: (i, 0))
    wide = jax.ShapeDtypeStruct((t, half * cs), BF16)
    outs = pl.pallas_call(
        body, name="qkv_fwd", grid=(nt, ns),
        in_specs=[_once(tok),
                  pl.BlockSpec((None, N_MOD, d), lambda i, j: (i // tpb, 0, 0)),
                  pl.BlockSpec((1, d), lambda i, j: (0, 0)),
                  pl.BlockSpec((None, d, cs), lambda i, j: (j, 0, 0))],
        out_specs=[pl.BlockSpec((tm, cs), lambda i, j: (i, jnp.minimum(j, half - 1))),
                   pl.BlockSpec((tm, cs), lambda i, j: (i, dil_col(i, j)))]
        + [_residue_spec(tm, tpb, cs, dil, dil_col) for dil in DILATIONS[1:]] + [tok],
        out_shape=[wide, wide] + [_residue_shape(nb, seq, half * cs, dil, BF16) for dil in DILATIONS[1:]]
        + [jax.ShapeDtypeStruct((t, d), BF16)],
        scratch_shapes=[_stage_shape(tm, cs)],
        compiler_params=_params("arbitrary", "arbitrary"),
    )(x, mod, g, win)
    qkv_dil = [outs[1]] + [a.reshape(t, half * cs) for a in outs[2:2 + n_res]]
    return outs[0], qkv_dil, outs[-1]


def _qkv_bwd(dqkv, dxo, x, mod, g, win, tm):
    t, d = x.shape
    ns, _, cs = win.shape
    nt = t // tm
    nb = mod.shape[0]
    tpb = nt // nb

    def body(dq_ref, dxo_ref, x_ref, mod_ref, g_ref, w_ref, dx_ref, dmod_ref, dg_ref, acc):
        i, j = pl.program_id(0), pl.program_id(1)

        @pl.when(j == 0)
        def _():
            acc[...] = jnp.zeros_like(acc)

        acc[...] += _dot_nt(dq_ref[...], w_ref[...])

        @pl.when(j == ns - 1)
        def _():
            dx, dshift, dscale, dg = _modulate_bwd(acc[...], x_ref[...], g_ref[...], mod_ref, 1)
            dx_ref[...] = dxo_ref[...] + dx

            @pl.when(i % tpb == 0)
            def _():
                dmod_ref[...] = jnp.zeros_like(dmod_ref)

            @pl.when(i == 0)
            def _():
                dg_ref[...] = jnp.zeros_like(dg_ref)

            dmod_ref[0:1, :] += dshift
            dmod_ref[1:2, :] += dscale
            dg_ref[0:1, :] += dg

    tok = pl.BlockSpec((tm, d), lambda i, j: (i, 0))
    return pl.pallas_call(
        body, name="qkv_bwd", grid=(nt, ns),
        in_specs=[pl.BlockSpec((tm, cs), lambda i, j: (i, j)), _once(tok), _once(tok),
                  pl.BlockSpec((None, N_MOD, d), lambda i, j: (i // tpb, 0, 0)),
                  pl.BlockSpec((1, d), lambda i, j: (0, 0)),
                  pl.BlockSpec((None, d, cs), lambda i, j: (j, 0, 0))],
        out_specs=[tok,
                   pl.BlockSpec((None, 8, d), lambda i, j: (i // tpb, 0, 0)),
                   pl.BlockSpec((8, d), lambda i, j: (0, 0))],
        out_shape=[jax.ShapeDtypeStruct((t, d), F32),
                   jax.ShapeDtypeStruct((nb, 8, d), F32), jax.ShapeDtypeStruct((8, d), F32)],
        scratch_shapes=[pltpu.VMEM((tm, d), F32)],
        compiler_params=_params("arbitrary", "arbitrary"),
    )(dqkv, dxo, x, mod, g, win)


def _heads(a):
    return [a[:, h * HEAD_DIM:(h + 1) * HEAD_DIM] for h in range(a.shape[1] // HEAD_DIM)]


def _sb_logits(qh, kh, tri, causal):
    zs = [_dot_nt(q, k) * (HEAD_DIM ** -0.5) for q, k in zip(qh, kh)]
    es = [jnp.exp(-jnp.abs(z)) for z in zs]
    log_nots = [-(jnp.maximum(z, 0.0) + jnp.log(1.0 + e)) for z, e in zip(zs, es)]
    if causal is not None:
        log_nots = [jnp.where(causal, ln, 0.0) for ln in log_nots]
    return zs, es, [_split_dot(ln, tri) for ln in log_nots]


def _sb_masks():
    rows = lax.broadcasted_iota(jnp.int32, (SB_BLOCK, SB_BLOCK), 0)
    cols = lax.broadcasted_iota(jnp.int32, (SB_BLOCK, SB_BLOCK), 1)
    return (rows >= cols).astype(BF16), (rows <= cols).astype(BF16), cols < rows


def _sb_fwd(qkv, nb, seq, exchange=None):
    t = qkv.shape[0]
    n_pairs = (qkv.shape[1] // 3) // SB_WIDTH
    tb = SB_BLOCK
    n_blk = seq // tb

    def body(q_ref, k_ref, v_ref, o_ref, c_ref):
        tri, _, causal = _sb_masks()

        def key_block(qh, kj, carry, mask):
            ks = pl.multiple_of(kj * tb, tb)
            kh, vh = _heads(k_ref[pl.ds(ks, tb), :]), _heads(v_ref[pl.ds(ks, tb), :])
            zs, _, suffixes = _sb_logits(qh, kh, tri, mask)
            ws = [jnp.exp(z + suffix + cr[1]) for z, suffix, cr in zip(zs, suffixes, carry)]
            if mask is not None:
                ws = [jnp.where(mask, w, 0.0) for w in ws]
            pv = [_dot(w.astype(BF16), v) for w, v in zip(ws, vh)]
            return tuple((cr[0] + p, cr[1] + suffix[:, 0:1]) for cr, p, suffix in zip(carry, pv, suffixes))

        def query_block(qi, _):
            qs = pl.multiple_of(qi * tb, tb)
            qh = _heads(q_ref[pl.ds(qs, tb), :])
            zero = (jnp.zeros((tb, HEAD_DIM), F32), jnp.zeros((tb, 1), F32))
            carry = key_block(qh, qi, (zero,) * SB_HEADS, causal)
            carry = lax.fori_loop(0, qi, lambda it, cr: key_block(qh, qi - 1 - it, cr, None), carry)
            o_ref[pl.ds(qs, tb), :] = jnp.concatenate([cr[0] for cr in carry], axis=1)
            c_ref[pl.ds(qs, tb), :] = jnp.concatenate(
                [jnp.broadcast_to(cr[1], (tb, HEAD_DIM)) for cr in carry], axis=1)
            return 0

        lax.fori_loop(0, n_blk, query_block, 0)

    def spec(offset):
        return pl.BlockSpec((seq, SB_WIDTH), lambda b, p: (b, offset + p))

    out = jax.ShapeDtypeStruct((t, n_pairs * SB_WIDTH), F32)
    return _call(
        body, name="sb_fwd", grid=(nb, n_pairs), args=(qkv, qkv, qkv),
        in_specs=[spec(0), spec(n_pairs), spec(2 * n_pairs)],
        out_specs=[spec(0), spec(0)], out_shape=[out, out],
        params=_params("arbitrary", "arbitrary"), exchange=exchange)


def _sb_bwd(qkv, do, csum, nb, seq, exchange=None):
    t = qkv.shape[0]
    n_pairs = (qkv.shape[1] // 3) // SB_WIDTH
    tb = SB_BLOCK
    n_blk = seq // tb
    scale = HEAD_DIM ** -0.5

    def body(q_ref, k_ref, v_ref, do_ref, c_ref, dq_ref, dk_ref, dv_ref, dk_acc, dv_acc):
        tri, tri_prefix, causal = _sb_masks()
        dk_acc[...] = jnp.zeros_like(dk_acc)
        dv_acc[...] = jnp.zeros_like(dv_acc)

        def key_block(qh, doh, ch, kj, carry, mask):
            ks = pl.multiple_of(kj * tb, tb)
            kh, vh = _heads(k_ref[pl.ds(ks, tb), :]), _heads(v_ref[pl.ds(ks, tb), :])
            heads = range(SB_HEADS)
            zs, es, suffixes = _sb_logits(qh, kh, tri, mask)
            dws = [_dot_nt(doh[h], vh[h]) for h in heads]
            lefts = [carry[h][1] + suffixes[h][:, 0:1] for h in heads]
            ws = [jnp.exp(zs[h] + suffixes[h] + (ch[h] - lefts[h])) for h in heads]
            if mask is not None:
                ws = [jnp.where(mask, w, 0.0) for w in ws]
            dlws = [ws[h] * dws[h] for h in heads]
            dprefixes = [_split_dot(dlw, tri_prefix) for dlw in dlws]
            dvs = [_dot_tn(ws[h].astype(BF16), doh[h]) for h in heads]
            dzbs = []
            for h in heads:
                sig = jnp.where(zs[h] >= 0.0, 1.0, es[h]) / (1.0 + es[h])
                dz = dlws[h] * (1.0 - sig) - (carry[h][2] + dprefixes[h] - dlws[h]) * sig
                if mask is not None:
                    dz = jnp.where(mask, dz, 0.0)
                dzbs.append((dz * scale).astype(BF16))
            dks = [_dot_tn(dzbs[h], qh[h]) for h in heads]
            dqs = [_dot(dzbs[h], kh[h]) for h in heads]
            dk_acc[pl.ds(ks, tb), :] += jnp.concatenate(dks, axis=1)
            dv_acc[pl.ds(ks, tb), :] += jnp.concatenate(dvs, axis=1)
            return tuple((carry[h][0] + dqs[h], lefts[h], carry[h][2] + dprefixes[h][:, tb - 1:tb])
                         for h in heads)

        def query_block(qi, _):
            qs = pl.multiple_of(qi * tb, tb)
            qh = _heads(q_ref[pl.ds(qs, tb), :])
            doh = _heads(do_ref[pl.ds(qs, tb), :].astype(BF16))
            cv = c_ref[pl.ds(qs, tb), :]
            ch = [cv[:, h * HEAD_DIM:h * HEAD_DIM + 1] for h in range(SB_HEADS)]
            zero = (jnp.zeros((tb, HEAD_DIM), F32), jnp.zeros((tb, 1), F32), jnp.zeros((tb, 1), F32))
            carry = lax.fori_loop(
                0, qi, lambda kj, cr: key_block(qh, doh, ch, kj, cr, None), (zero,) * SB_HEADS)
            carry = key_block(qh, doh, ch, qi, carry, causal)
            dq_ref[pl.ds(qs, tb), :] = jnp.concatenate([cr[0] for cr in carry], axis=1).astype(BF16)
            return 0

        lax.fori_loop(0, n_blk, query_block, 0)
        dk_ref[...] = dk_acc[...].astype(BF16)
        dv_ref[...] = dv_acc[...].astype(BF16)

    def spec(offset):
        return pl.BlockSpec((seq, SB_WIDTH), lambda b, p: (b, offset + p))

    out = jax.ShapeDtypeStruct((t, n_pairs * SB_WIDTH), BF16)
    return _call(
        body, name="sb_bwd", grid=(nb, n_pairs), args=(qkv, qkv, qkv, do, csum),
        in_specs=[spec(0), spec(n_pairs), spec(2 * n_pairs), spec(0), spec(0)],
        out_specs=[spec(0), spec(0), spec(0)],
        out_shape=[out, out, out],
        scratch_shapes=[pltpu.VMEM((seq, SB_WIDTH), F32), pltpu.VMEM((seq, SB_WIDTH), F32)],
        params=_params("arbitrary", "arbitrary"), exchange=exchange)


def _dil_block_scores(qh, kph, kch, bias_ref, has_prev, band_prev, band_cur):
    scale = HEAD_DIM ** -0.5
    heads = range(len(qh))
    no_prev = jnp.where(has_prev, 0.0, NEG_INF)
    zps = [_dot_nt(qh[h], kph[h]) for h in heads]
    zcs = [_dot_nt(qh[h], kch[h]) for h in heads]
    zps = [jnp.where(band_prev, zps[h] * scale + bias_ref[h, :, 0:DIL_BLOCK], NEG_INF) + no_prev for h in heads]
    zcs = [jnp.where(band_cur, zcs[h] * scale + bias_ref[h, :, DIL_BLOCK:2 * DIL_BLOCK], NEG_INF) for h in heads]
    return zps, zcs


def _dil_bands():
    rows = lax.broadcasted_iota(jnp.int32, (DIL_BLOCK, DIL_BLOCK), 0)
    cols = lax.broadcasted_iota(jnp.int32, (DIL_BLOCK, DIL_BLOCK), 1)
    return cols >= rows, cols <= rows


def _dil_fwd(qkv, bias, nb, seq, dil, exchange=None):
    t, width = qkv.shape
    n_pairs = (width // 3) // DIL_WIDTH
    bq = DIL_BLOCK
    n_blk = seq // bq
    per_seq = n_blk // dil
    heads = range(DIL_HEADS)

    def body(q_ref, k_ref, v_ref, bias_ref, o_ref, lse_ref):
        band_prev, band_cur = _dil_bands()

        def block(n, _):
            has_prev = (n & (per_seq - 1)) != 0
            qs = pl.multiple_of(n * bq, bq)
            ps = pl.multiple_of(jnp.maximum(n - 1, 0) * bq, bq)
            qh = _heads(q_ref[pl.ds(qs, bq), :])
            kp, kc = _heads(k_ref[pl.ds(ps, bq), :]), _heads(k_ref[pl.ds(qs, bq), :])
            vp, vc = _heads(v_ref[pl.ds(ps, bq), :]), _heads(v_ref[pl.ds(qs, bq), :])
            zps, zcs = _dil_block_scores(qh, kp, kc, bias_ref, has_prev, band_prev, band_cur)
            ms = [jnp.maximum(jnp.max(zps[h], axis=1, keepdims=True), jnp.max(zcs[h], axis=1, keepdims=True))
                  for h in heads]
            eps = [jnp.exp(zps[h] - ms[h]) for h in heads]
            ecs = [jnp.exp(zcs[h] - ms[h]) for h in heads]
            pvs = [_dot(eps[h].astype(BF16), vp[h]) + _dot(ecs[h].astype(BF16), vc[h]) for h in heads]
            dens = [jnp.sum(eps[h], axis=1, keepdims=True) + jnp.sum(ecs[h], axis=1, keepdims=True) for h in heads]
            o_ref[pl.ds(qs, bq), :] = jnp.concatenate([pvs[h] / dens[h] for h in heads], axis=1)
            lse_ref[pl.ds(qs, bq), :] = jnp.concatenate(
                [jnp.broadcast_to(ms[h] + jnp.log(dens[h]), (bq, HEAD_DIM)) for h in heads], axis=1)
            return 0

        lax.fori_loop(0, n_blk, block, 0)

    def spec(offset):
        return pl.BlockSpec((seq, DIL_WIDTH), lambda b, p: (b, offset + p))

    out = jax.ShapeDtypeStruct((t, n_pairs * DIL_WIDTH), F32)
    return _call(
        body, name=f"dil_fwd{dil}", grid=(nb, n_pairs), args=(qkv, qkv, qkv, bias),
        in_specs=[spec(0), spec(n_pairs), spec(2 * n_pairs),
                  pl.BlockSpec((DIL_HEADS, bq, 2 * bq), lambda b, p: (p, 0, 0))],
        out_specs=[spec(0), spec(0)], out_shape=[out, out],
        params=_params("arbitrary", "arbitrary"), exchange=exchange)


def _dil_bwd(qkv, bias, do, lse, delta, nb, seq, dil):
    t, width = qkv.shape
    n_pairs = (width // 3) // DIL_WIDTH
    bq = DIL_BLOCK
    n_blk = seq // bq
    per_seq = n_blk // dil
    scale = HEAD_DIM ** -0.5
    heads = range(DIL_HEADS)

    def body(q_ref, k_ref, v_ref, bias_ref, do_ref, lse_ref, dl_ref, dq_ref, dk_ref, dv_ref, db_ref):
        band_prev, band_cur = _dil_bands()
        dk_ref[...] = jnp.zeros_like(dk_ref)
        dv_ref[...] = jnp.zeros_like(dv_ref)

        @pl.when(pl.program_id(1) == 0)
        def _():
            db_ref[...] = jnp.zeros_like(db_ref)

        def block(n, _):
            has_prev = (n & (per_seq - 1)) != 0
            qs = pl.multiple_of(n * bq, bq)
            ps = pl.multiple_of(jnp.maximum(n - 1, 0) * bq, bq)
            qh = _heads(q_ref[pl.ds(qs, bq), :])
            kp, kc = _heads(k_ref[pl.ds(ps, bq), :]), _heads(k_ref[pl.ds(qs, bq), :])
            vp, vc = _heads(v_ref[pl.ds(ps, bq), :]), _heads(v_ref[pl.ds(qs, bq), :])
            doh = _heads(do_ref[pl.ds(qs, bq), :].astype(BF16))
            lse_v, dl_v = lse_ref[pl.ds(qs, bq), :], dl_ref[pl.ds(qs, bq), :]
            zps, zcs = _dil_block_scores(qh, kp, kc, bias_ref, has_prev, band_prev, band_cur)
            dpp = [_dot_nt(doh[h], vp[h]) for h in heads]
            dpc = [_dot_nt(doh[h], vc[h]) for h in heads]
            lse_h = [lse_v[:, h * HEAD_DIM:h * HEAD_DIM + 1] for h in heads]
            dl_h = [dl_v[:, h * HEAD_DIM:h * HEAD_DIM + 1] for h in heads]
            pps = [jnp.exp(zps[h] - lse_h[h]) for h in heads]
            pcs = [jnp.exp(zcs[h] - lse_h[h]) for h in heads]
            dvp = [_dot_tn(pps[h].astype(BF16), doh[h]) for h in heads]
            dvc = [_dot_tn(pcs[h].astype(BF16), doh[h]) for h in heads]
            dzps = [pps[h] * (dpp[h] - dl_h[h]) for h in heads]
            dzcs = [pcs[h] * (dpc[h] - dl_h[h]) for h in heads]
            dzp_b = [(dzps[h] * scale).astype(BF16) for h in heads]
            dzc_b = [(dzcs[h] * scale).astype(BF16) for h in heads]
            dqs = [_dot(dzp_b[h], kp[h]) + _dot(dzc_b[h], kc[h]) for h in heads]
            dkp = [_dot_tn(dzp_b[h], qh[h]) for h in heads]
            dkc = [_dot_tn(dzc_b[h], qh[h]) for h in heads]
            for h in heads:
                db_ref[h, :, 0:bq] += dzps[h]
                db_ref[h, :, bq:2 * bq] += dzcs[h]
            dq_ref[pl.ds(qs, bq), :] = jnp.concatenate(dqs, axis=1)
            dk_ref[pl.ds(ps, bq), :] += jnp.concatenate(dkp, axis=1)
            dk_ref[pl.ds(qs, bq), :] += jnp.concatenate(dkc, axis=1)
            dv_ref[pl.ds(ps, bq), :] += jnp.concatenate(dvp, axis=1)
            dv_ref[pl.ds(qs, bq), :] += jnp.concatenate(dvc, axis=1)
            return 0

        lax.fori_loop(0, n_blk, block, 0)

    def spec(offset):
        return pl.BlockSpec((seq, DIL_WIDTH), lambda p, b: (b, offset + p))

    bias_spec = pl.BlockSpec((DIL_HEADS, bq, 2 * bq), lambda p, b: (p, 0, 0))
    out = jax.ShapeDtypeStruct((t, n_pairs * DIL_WIDTH), F32)
    return pl.pallas_call(
        body, name=f"dil_bwd{dil}", grid=(n_pairs, nb),
        in_specs=[spec(0), spec(n_pairs), spec(2 * n_pairs), bias_spec, spec(0), spec(0), spec(0)],
        out_specs=[spec(0), spec(0), spec(0), bias_spec],
        out_shape=[out, out, out, jax.ShapeDtypeStruct(bias.shape, F32)],
        compiler_params=_params("arbitrary", "arbitrary"),
    )(qkv, qkv, qkv, bias, do, lse, delta)


def _head_mean(v, gmat):
    return _split_dot(v, gmat) * (1.0 / HEAD_DIM)


def _residue_views(arrays, nb, seq):
    return [a if dil == 1 else a.reshape(nb, dil, seq // dil, a.shape[1]) for a, dil in zip(arrays, DILATIONS)]


def _mix_out_fwd(osb, ocs, lses, gsb, gdil, gmat, wout, x, mod, tm):
    t, d = x.shape
    ds = osb.shape[1]
    nt = t // tm
    nb = mod.shape[0]
    tpb = nt // nb
    seq = t // nb
    n_cfg = len(DILATIONS)

    def body(osb_ref, *refs):
        oc_refs, lse_refs = refs[:n_cfg], refs[n_cfg:2 * n_cfg]
        gsb_ref, gdil_ref, gm_ref, w_ref, x_ref, mod_ref = refs[2 * n_cfg:2 * n_cfg + 6]
        xo_ref, on_ref, m_ref, odil_ref = refs[2 * n_cfg + 6:2 * n_cfg + 10]
        ld_refs = refs[2 * n_cfg + 10:3 * n_cfg + 10]
        stages, sc = refs[3 * n_cfg + 10:]
        ocv, lsev = [oc_refs[0][...]], [lse_refs[0][...]]
        for i, dil in enumerate(DILATIONS[1:]):
            ocv.append(_from_residue_rows(oc_refs[i + 1], stages.at[2 * i], dil))
            lsev.append(_from_residue_rows(lse_refs[i + 1], stages.at[2 * i + 1], dil))
        top = functools.reduce(jnp.maximum, lsev)
        total = top + jnp.log(sum(jnp.exp(l - top) for l in lsev))
        odil = sum(jnp.exp(l - total) * o for o, l in zip(ocv, lsev))
        odil_ref[...] = odil
        ld_refs[0][...] = total
        _stage(total, sc)
        for ref, dil in zip(ld_refs[1:], DILATIONS[1:]):
            _to_residue_rows(sc, ref, dil)
        gm = gm_ref[...]
        parts = []
        for o, g_ref in ((osb_ref[...], gsb_ref), (odil, gdil_ref)):
            parts.append(o * lax.rsqrt(_head_mean(o * o, gm) + EPS) * g_ref[...])
        on = jnp.concatenate(parts, axis=1).astype(BF16)
        on_ref[...] = on
        m = _dot(on, w_ref[...])
        m_ref[...] = m
        xo_ref[...] = x_ref[...] + mod_ref[5:6, :] * m

    tok = pl.BlockSpec((tm, d), lambda i: (i, 0))
    hd = pl.BlockSpec((tm, ds), lambda i: (i, 0))
    res = [hd] + [_residue_spec(tm, tpb, ds, dil, lambda i: 0) for dil in DILATIONS[1:]]
    res_shape = [jax.ShapeDtypeStruct((t, ds), F32)] + [_residue_shape(nb, seq, ds, dil, F32) for dil in DILATIONS[1:]]
    gain = pl.BlockSpec((1, ds), lambda i: (0, 0))
    outs = pl.pallas_call(
        body, name="mix_out_fwd", grid=(nt,),
        in_specs=[hd] + res + res + [gain, gain,
                  pl.BlockSpec((ds, ds), lambda i: (0, 0)),
                  pl.BlockSpec(wout.shape, lambda i: (0, 0)),
                  tok, pl.BlockSpec((None, N_MOD, d), lambda i: (i // tpb, 0, 0))],
        out_specs=[tok, pl.BlockSpec((tm, 2 * ds), lambda i: (i, 0)), tok, hd] + res,
        out_shape=[jax.ShapeDtypeStruct((t, d), F32), jax.ShapeDtypeStruct((t, 2 * ds), BF16),
                   jax.ShapeDtypeStruct((t, d), F32), jax.ShapeDtypeStruct((t, ds), F32)] + res_shape,
        scratch_shapes=[pltpu.VMEM((2 * (n_cfg - 1), ds // LANES, tm, LANES), F32), _stage_shape(tm, ds)],
        compiler_params=_params("arbitrary"),
    )(osb, *_residue_views(ocs, nb, seq), *_residue_views(lses, nb, seq), gsb, gdil, gmat, wout, x, mod)
    return outs[0], outs[1], outs[2], outs[3], [a.reshape(t, ds) for a in outs[4:]]


def _mix_out_bwd(dxo, m, mod, wout, osb, odil, gsb, gdil, gmat, tm):
    t, d = dxo.shape
    ds = osb.shape[1]
    nt = t // tm
    nb = mod.shape[0]
    tpb = nt // nb
    seq = t // nb
    n_cfg = len(DILATIONS)

    def body(dxo_ref, m_ref, mod_ref, w_ref, osb_ref, odil_ref, gsb_ref, gdil_ref, gm_ref,
             dm_ref, dosb_ref, *rest):
        do_refs, dl_refs = rest[:n_cfg], rest[n_cfg:2 * n_cfg]
        dmod_ref, dg_ref, sc = rest[2 * n_cfg:]
        dodil_ref, dldil_ref = do_refs[0], dl_refs[0]
        i = pl.program_id(0)
        dxo_v = dxo_ref[...]
        dm = (mod_ref[5:6, :] * dxo_v).astype(BF16)
        dm_ref[...] = dm
        dgt = jnp.sum(m_ref[...] * dxo_v, axis=0, keepdims=True)
        don = _dot_nt(dm, w_ref[...])
        gm = gm_ref[...]

        @pl.when(i % tpb == 0)
        def _():
            dmod_ref[...] = jnp.zeros_like(dmod_ref)

        @pl.when(i == 0)
        def _():
            dg_ref[...] = jnp.zeros_like(dg_ref)

        dmod_ref[2:3, :] += dgt
        groups = ((osb_ref, gsb_ref, dosb_ref), (odil_ref, gdil_ref, dodil_ref))
        for k, (o_ref, g_ref, do_ref) in enumerate(groups):
            o = o_ref[...]
            dn_out = don[:, k * ds:(k + 1) * ds]
            r = lax.rsqrt(_head_mean(o * o, gm) + EPS)
            n = o * r
            dg_ref[0:1, k * ds:(k + 1) * ds] += jnp.sum(dn_out * n, axis=0, keepdims=True)
            dn = dn_out * g_ref[...]
            do = r * (dn - n * _head_mean(dn * n, gm))
            do_ref[...] = do
            if k == 1:
                delta = _head_mean(do * o, gm) * float(HEAD_DIM)
                dldil_ref[...] = delta
                for value, refs in ((do, do_refs), (delta, dl_refs)):
                    _stage(value, sc)
                    for ref, dil in zip(refs[1:], DILATIONS[1:]):
                        _to_residue_rows(sc, ref, dil)

    tok = pl.BlockSpec((tm, d), lambda i: (i, 0))
    hd = pl.BlockSpec((tm, ds), lambda i: (i, 0))
    res = [hd] + [_residue_spec(tm, tpb, ds, dil, lambda i: 0) for dil in DILATIONS[1:]]
    res_shape = [jax.ShapeDtypeStruct((t, ds), F32)] + [_residue_shape(nb, seq, ds, dil, F32) for dil in DILATIONS[1:]]
    gain = pl.BlockSpec((1, ds), lambda i: (0, 0))
    outs = pl.pallas_call(
        body, name="mix_out_bwd", grid=(nt,),
        in_specs=[tok, tok, pl.BlockSpec((None, N_MOD, d), lambda i: (i // tpb, 0, 0)),
                  pl.BlockSpec(wout.shape, lambda i: (0, 0)), hd, hd, gain, gain,
                  pl.BlockSpec((ds, ds), lambda i: (0, 0))],
        out_specs=[tok, hd] + res + res
        + [pl.BlockSpec((None, 8, d), lambda i: (i // tpb, 0, 0)), pl.BlockSpec((8, 2 * ds), lambda i: (0, 0))],
        out_shape=[jax.ShapeDtypeStruct((t, d), BF16), jax.ShapeDtypeStruct((t, ds), F32)] + res_shape + res_shape
        + [jax.ShapeDtypeStruct((nb, 8, d), F32), jax.ShapeDtypeStruct((8, 2 * ds), F32)],
        scratch_shapes=[_stage_shape(tm, ds)],
        compiler_params=_params("arbitrary"),
    )(dxo, m, mod, wout, osb, odil, gsb, gdil, gmat)
    flat = [a.reshape(t, ds) for a in outs[2:2 + 2 * n_cfg]]
    return outs[0], outs[1], flat[:n_cfg], flat[n_cfg:], outs[-2], outs[-1]


def _merge_dqkv(sb_parts, dil_parts, nb, tm):
    t, ds = sb_parts[0].shape
    nt = t // tm
    tpb = nt // nb
    seq = t // nb
    n_cfg = len(DILATIONS)

    def body(*refs):
        sb_refs, dil_refs = refs[:3], refs[3:3 + 3 * n_cfg]
        o_ref, sc = refs[3 + 3 * n_cfg:]
        for k in range(3):
            o_ref[:, k * ds:(k + 1) * ds] = sb_refs[k][...]
            total = dil_refs[k * n_cfg][...]
            for i, dil in enumerate(DILATIONS[1:]):
                total = total + _from_residue_rows(dil_refs[k * n_cfg + i + 1], sc, dil)
            o_ref[:, (3 + k) * ds:(4 + k) * ds] = total.astype(BF16)

    hd = pl.BlockSpec((tm, ds), lambda i: (i, 0))
    res = [hd] + [_residue_spec(tm, tpb, ds, dil, lambda i: 0) for dil in DILATIONS[1:]]
    views = [v for parts in dil_parts for v in _residue_views(parts, nb, seq)]
    return pl.pallas_call(
        body, name="merge_dqkv", grid=(nt,),
        in_specs=[hd] * 3 + res * 3,
        out_specs=pl.BlockSpec((tm, 6 * ds), lambda i: (i, 0)),
        out_shape=jax.ShapeDtypeStruct((t, 6 * ds), BF16),
        scratch_shapes=[_stage_shape(tm, ds)],
        compiler_params=_params("arbitrary"),
    )(*sb_parts, *views)


def _loss_head(x, target, g, tm):
    t, d = x.shape

    def body(x_ref, t_ref, g_ref, dx_ref, acc_ref):
        @pl.when(pl.program_id(0) == 0)
        def _():
            acc_ref[...] = jnp.zeros_like(acc_ref)

        n, r = _norm(x_ref[...])
        gv = g_ref[...]
        err = n * gv - t_ref[...]
        dy = err * (1.0 / d)
        acc_ref[0:1, :] += jnp.sum(err * err, axis=0, keepdims=True)
        acc_ref[1:2, :] += jnp.sum(dy * n, axis=0, keepdims=True)
        dn = dy * gv
        dx_ref[...] = r * (dn - n * jnp.mean(dn * n, axis=-1, keepdims=True))

    tok = pl.BlockSpec((tm, d), lambda i: (i, 0))
    return pl.pallas_call(
        body, name="loss_head", grid=(t // tm,),
        in_specs=[tok, tok, pl.BlockSpec((1, d), lambda i: (0, 0))],
        out_specs=[tok, pl.BlockSpec((8, d), lambda i: (0, 0))],
        out_shape=[jax.ShapeDtypeStruct((t, d), F32), jax.ShapeDtypeStruct((8, d), F32)],
        compiler_params=_params("arbitrary"),
    )(x, target, g)


def _row_tile(rows):
    if rows <= 256:
        return rows
    for cand in range(256, 15, -16):
        if rows % cand == 0:
            return cand
    return rows


def _adamw(w, parts, m, v, name):
    rows, cols = w.shape
    n_parts = parts.shape[0]
    tr = _row_tile(rows)
    c1 = 1.0 / (1.0 - ADAM_B1 ** ADAM_STEP)
    c2 = 1.0 / (1.0 - ADAM_B2 ** ADAM_STEP)

    def body(w_ref, p_ref, m_ref, v_ref, g_ref, d_ref, nm_ref, nv_ref):
        g = p_ref[0].astype(F32)
        for i in range(1, n_parts):
            g = g + p_ref[i].astype(F32)
        nm = ADAM_B1 * m_ref[...] + (1.0 - ADAM_B1) * g
        nv = ADAM_B2 * v_ref[...] + (1.0 - ADAM_B2) * (g * g)
        g_ref[...] = g
        nm_ref[...] = nm
        nv_ref[...] = nv
        d_ref[...] = -ADAM_LR * ((nm * c1) / (jnp.sqrt(nv * c2) + ADAM_EPS) + ADAM_WD * w_ref[...])

    blk = pl.BlockSpec((tr, cols), lambda i: (i, 0))
    out = jax.ShapeDtypeStruct((rows, cols), F32)
    return pl.pallas_call(
        body, name=name, grid=(rows // tr,),
        in_specs=[blk, pl.BlockSpec((n_parts, tr, cols), lambda i: (0, i, 0)), blk, blk],
        out_specs=[blk, blk, blk, blk], out_shape=[out, out, out, out],
        compiler_params=_params("arbitrary"),
    )(w, parts, m, v)


def _t5_bucket(n):
    max_exact = N_BUCKETS // 2
    nf = np.maximum(n, 1).astype(np.float32)
    large = max_exact + (np.log(nf / max_exact) / math.log(MAX_DISTANCE / max_exact)
                         * (N_BUCKETS - max_exact)).astype(np.int32)
    large = np.minimum(large, N_BUCKETS - 1)
    return np.where(n < max_exact, n, large).astype(np.int32)


def _bucket_onehot():
    table = np.zeros((len(DILATIONS), 2 * DIL_BLOCK + 1, N_BUCKETS), np.float32)
    for i, dil in enumerate(DILATIONS):
        buckets = _t5_bucket(np.arange(DIL_BLOCK + 1) * dil)
        for m in range(DIL_BLOCK + 1):
            table[i, m, buckets[DIL_BLOCK - m]] = 1.0
    return table


def _bias_blocks(rel_bias):
    row = jnp.einsum("cmn,nh->chm", _bucket_onehot(), rel_bias, precision=lax.Precision.HIGHEST)
    n_cfg, n_heads, width = row.shape
    tiled = jnp.tile(row, (1, 1, DIL_BLOCK))[..., :DIL_BLOCK * (width - 1)]
    return tiled.reshape(n_cfg, n_heads, DIL_BLOCK, width - 1)


def _bias_blocks_bwd(dblocks):
    n_cfg, n_heads = dblocks.shape[:2]
    width = 2 * DIL_BLOCK + 1
    flat = dblocks.reshape(n_cfg, n_heads, DIL_BLOCK * (width - 1))
    flat = jnp.pad(flat, ((0, 0), (0, 0), (0, DIL_BLOCK)))
    drow = jnp.sum(flat.reshape(n_cfg, n_heads, DIL_BLOCK, width), axis=2)
    return jnp.einsum("chm,cmn->nh", drow, _bucket_onehot(), precision=lax.Precision.HIGHEST)


def _pad_to(a, axis, size):
    pad = [(0, 0)] * a.ndim
    pad[axis] = (0, size - a.shape[axis])
    return jnp.pad(a, pad)


def _lane_pad(n):
    return -(-n // LANES) * LANES


def _local_step(x, target, mod, gains, weights, rel_bias, tm, distributed):
    nb, seq, d = x.shape
    t = nb * seq
    g_ffn1, g_mix, g_sb, g_dil, g_ffn2, g_final = gains
    wg1, wu1, wd1 = weights[:3]
    x0 = x.reshape(t, d)
    ds = g_sb.shape[1]
    gmat = jnp.asarray(np.kron(np.eye(ds // HEAD_DIM), np.ones((HEAD_DIM, HEAD_DIM))), BF16)
    bias = _bias_blocks(rel_bias)

    def beside(arrays, scatter):
        return _Exchange(arrays, scatter) if distributed else None

    tp, tg = min(PROJ_TILE, seq), min(GRAD_TILE, t)

    (x1, f1, gate1, up1), got = _ffn_fwd(x0, mod, g_ffn1, wg1, wu1, wd1, 0, tp, beside(weights[3:5], False))
    win, wout = got if distributed else weights[3:5]
    wout2 = wout.reshape(-1, d)
    qkv, qkvd, h2 = _qkv_fwd(x1, mod, g_mix, win, tp)
    (osb, csb), got = _sb_fwd(qkv, nb, seq, beside(weights[5:7], False))
    wg2, wu2 = got if distributed else weights[5:7]
    ocs, lses = [], []
    for i, dil in enumerate(DILATIONS):
        (oc, lse), got = _dil_fwd(qkvd[i], bias[i], nb, seq, dil, beside(weights[7:8], False) if i == 0 else None)
        if i == 0:
            wd2 = got[0] if distributed else weights[7]
        ocs.append(oc)
        lses.append(lse)
    x2, on, mix, odil, ldil = _mix_out_fwd(osb, ocs, lses, g_sb, g_dil, gmat, wout2, x1, mod, tm)
    (x3, f3, gate3, up3), _ = _ffn_fwd(x2, mod, g_ffn2, wg2, wu2, wd2, 2, tp)
    dx3, head = _loss_head(x3, target.reshape(t, d), g_final, tm)
    loss_sum = 0.5 * jnp.sum(head[0]) / d
    dg_final = head[1:2]

    (dx2, dgate3, dup3, act3, h3, df3, dmod3, dg_ffn2), _ = _ffn_bwd(
        dx3, x2, f3, mod, g_ffn2, gate3, up3, wg2, wu2, wd2, 2, tp)
    gwg2, gwu2, gwd2 = _ffn_weight_grads(h3, dgate3, dup3, act3, df3, tg, 2)

    dm, dosb, dodil, dldil, dmod2b, dg_heads = _mix_out_bwd(
        dx2, mix, mod, wout2, osb, odil, g_sb, g_dil, gmat, tm)
    n_out = wout.shape[0]
    gwout = _mm_tn(on, dm,
                   pl.BlockSpec((tg, wout.shape[1]), lambda j, i: (i, j)),
                   pl.BlockSpec((tg, d), lambda j, i: (i, 0)),
                   wout.shape, pl.BlockSpec((None, wout.shape[1], d), lambda j, i: (j, 0, 0)),
                   (n_out, t // tg), "grad_wout")

    (dq_sb, dk_sb, dv_sb), parts_late = _sb_bwd(qkv, dosb, csb, nb, seq,
                                                beside([gwout, gwg2, gwu2, gwd2], True))
    dil_grads = [_dil_bwd(qkvd[i], bias[i], dodil[i], ldil[i], dldil[i], nb, seq, dil)
                 for i, dil in enumerate(DILATIONS)]
    dqkv = _merge_dqkv([dq_sb, dk_sb, dv_sb], [[g[k] for g in dil_grads] for k in range(3)], nb, tm)
    drel = _bias_blocks_bwd(jnp.stack([g[3] for g in dil_grads]))

    dx1, dmod2a, dg_mix = _qkv_bwd(dqkv, dx2, x1, mod, g_mix, win, tp)
    n_in, _, cs = win.shape
    gwin = _mm_tn(h2, dqkv,
                  pl.BlockSpec((tg, d), lambda j, i: (i, 0)),
                  pl.BlockSpec((tg, cs), lambda j, i: (i, j)),
                  win.shape, pl.BlockSpec((None, d, cs), lambda j, i: (j, 0, 0)),
                  (n_in, t // tg), "grad_win")

    (dx0, dgate1, dup1, act1, h1, df1, dmod1, dg_ffn1), parts_mid = _ffn_bwd(
        dx1, x0, f1, mod, g_ffn1, gate1, up1, wg1, wu1, wd1, 0, tp, beside([gwin], True))
    gw1 = _ffn_weight_grads(h1, dgate1, dup1, act1, df1, tg, 0, stream=distributed)

    dmod = jnp.concatenate([dmod1[:, 0:3], dmod2a[:, 0:2], dmod2b[:, 2:3], dmod3[:, 0:3]], axis=1)
    wgrads = tuple(gw1) + (tuple(parts_mid + parts_late) if distributed else (gwin, gwout, gwg2, gwu2, gwd2))
    ggrads = (dg_ffn1[0:1], dg_mix[0:1], dg_heads[0:1], drel, dg_ffn2[0:1], dg_final)
    return loss_sum, dx0.reshape(nb, seq, d), wgrads, dmod, ggrads


def kernel(x, c, w_ada, b_ada, g_ffn1, w1_gate, w1_up, w1_down, g_mix, w_in, g_sb_out, g_dil_out, w_out, rel_bias, g_ffn2, w2_gate, w2_up, w2_down, g_final, loss_target, m_w_ada, m_b_ada, m_g_ffn1, m_w1_gate, m_w1_up, m_w1_down, m_g_mix, m_w_in, m_g_sb_out, m_g_dil_out, m_w_out, m_rel_bias, m_g_ffn2, m_w2_gate, m_w2_up, m_w2_down, m_g_final, v_w_ada, v_b_ada, v_g_ffn1, v_w1_gate, v_w1_up, v_w1_down, v_g_mix, v_w_in, v_g_sb_out, v_g_dil_out, v_w_out, v_rel_bias, v_g_ffn2, v_w2_gate, v_w2_up, v_w2_down, v_g_final):
    nb, seq, d = x.shape
    me = 4 * lax.axis_index("x") + 2 * lax.axis_index("y") + lax.axis_index("c")
    tm = min(TOKEN_TILE, seq)
    fs = w1_gate.shape[2]
    fs_pad = _lane_pad(fs)
    ada_cols = w_ada.shape[2]

    def col_shard(w):
        return _pad_to(w[0].astype(BF16), 1, fs_pad)

    def row_shard(w):
        return _pad_to(w[0].astype(BF16), 0, fs_pad)

    shards = [col_shard(w1_gate), col_shard(w1_up), row_shard(w1_down), w_in[0].astype(BF16),
              w_out[0].astype(BF16), col_shard(w2_gate), col_shard(w2_up), row_shard(w2_down)]
    gathered = _exchange([_pad_to(c, 0, 8)] + shards[:3], False, "gather_first", relay=True)
    c_all = gathered[0][:, :nb].reshape(N_DEV * nb, d)
    weights = gathered[1:] + shards[3:]

    b_cols = lax.dynamic_slice(b_ada, (0, me * ada_cols), (1, ada_cols))
    mod_part = _ada_fwd(c_all, w_ada[0], b_cols)
    mod_all = _exchange([mod_part], False, "gather_mod")[0]
    mod = lax.dynamic_slice(mod_all, (0, me * nb, 0), (N_DEV, nb, ada_cols))
    mod = mod.transpose(1, 0, 2).reshape(nb, N_MOD, d)

    n_sb = g_sb_out.shape[1] * g_sb_out.shape[2]
    gains = (g_ffn1, g_mix, g_sb_out.reshape(1, n_sb), g_dil_out.reshape(1, -1), g_ffn2,
             g_final.reshape(1, d))
    loss_sum, grad_x, parts, dmod, ggrads = _local_step(x, loss_target, mod, gains, weights, rel_bias, tm, True)
    loss = lax.psum(loss_sum, ("x", "y", "c"))

    dg_ffn1, dg_mix, dg_heads, drel, dg_ffn2, dg_final = ggrads
    width = max(d, dg_heads.shape[1], drel.size)
    small = jnp.concatenate(
        [_pad_to(a.reshape(1, -1), 1, width) for a in (dg_ffn1, dg_mix, dg_ffn2, dg_final, dg_heads, drel)]
        + [jnp.zeros((2, width), F32)], axis=0)
    dmod_all, small_all = _exchange([_pad_to(dmod.reshape(nb, N_MOD * d), 0, 8), small], False, "gather_small")
    dmod_all = dmod_all[:, :nb].reshape(N_DEV * nb, N_MOD * d)
    dmod_cols = lax.dynamic_slice(dmod_all, (0, me * ada_cols), (N_DEV * nb, ada_cols))
    gw_ada, gb_ada = _ada_bwd(c_all, dmod_cols, dmod_all)

    def small_part(row, size, shape):
        return small_all[:, row, :size].reshape((N_DEV,) + shape)

    n_rel = rel_bias.shape
    updates = {
        "w_ada": (w_ada[0], gw_ada[None], m_w_ada[0], v_w_ada[0]),
        "b_ada": (b_ada, gb_ada[None], m_b_ada, v_b_ada),
        "g_ffn1": (g_ffn1, small_part(0, d, (1, d)), m_g_ffn1, v_g_ffn1),
        "w1_gate": (w1_gate[0], parts[0][:, :, :fs], m_w1_gate[0], v_w1_gate[0]),
        "w1_up": (w1_up[0], parts[1][:, :, :fs], m_w1_up[0], v_w1_up[0]),
        "w1_down": (w1_down[0], parts[2][:, :fs, :], m_w1_down[0], v_w1_down[0]),
        "g_mix": (g_mix, small_part(1, d, (1, d)), m_g_mix, v_g_mix),
        "w_in": (w_in[0], parts[3], m_w_in[0], v_w_in[0]),
        "g_sb_out": (g_sb_out[0], small_all[:, 4, :n_sb].reshape((N_DEV,) + g_sb_out.shape[1:]),
                     m_g_sb_out[0], v_g_sb_out[0]),
        "g_dil_out": (g_dil_out[0], small_all[:, 4, n_sb:dg_heads.shape[1]].reshape((N_DEV,) + g_dil_out.shape[1:]),
                      m_g_dil_out[0], v_g_dil_out[0]),
        "w_out": (w_out[0], parts[4], m_w_out[0], v_w_out[0]),
        "rel_bias": (rel_bias, small_part(5, drel.size, n_rel), m_rel_bias, v_rel_bias),
        "g_ffn2": (g_ffn2, small_part(2, d, (1, d)), m_g_ffn2, v_g_ffn2),
        "w2_gate": (w2_gate[0], parts[5][:, :, :fs], m_w2_gate[0], v_w2_gate[0]),
        "w2_up": (w2_up[0], parts[6][:, :, :fs], m_w2_up[0], v_w2_up[0]),
        "w2_down": (w2_down[0], parts[7][:, :fs, :], m_w2_down[0], v_w2_down[0]),
        "g_final": (g_final.reshape(1, d), small_part(3, d, (1, d)), m_g_final.reshape(1, d), v_g_final.reshape(1, d)),
    }
    shapes = {"w_ada": w_ada.shape, "b_ada": b_ada.shape, "g_ffn1": g_ffn1.shape, "w1_gate": w1_gate.shape,
              "w1_up": w1_up.shape, "w1_down": w1_down.shape, "g_mix": g_mix.shape, "w_in": w_in.shape,
              "g_sb_out": g_sb_out.shape, "g_dil_out": g_dil_out.shape, "w_out": w_out.shape,
              "rel_bias": rel_bias.shape, "g_ffn2": g_ffn2.shape, "w2_gate": w2_gate.shape,
              "w2_up": w2_up.shape, "w2_down": w2_down.shape, "g_final": g_final.shape}
    grads, deltas, new_m, new_v = [], [], [], []
    for name, (w, p, m, v) in updates.items():
        g, dw, nm, nv = _adamw(w, p, m, v, f"adamw_{name}")
        grads.append(g.reshape(shapes[name]))
        deltas.append(dw.reshape(shapes[name]))
        new_m.append(nm.reshape(shapes[name]))
        new_v.append(nv.reshape(shapes[name]))
    return (loss, grad_x, *grads, *deltas, *new_m, *new_v)
```

```python
import functools
import math

import numpy as np
import jax
import jax.numpy as jnp
from jax import lax
from jax.experimental import pallas as pl
from jax.experimental.pallas import tpu as pltpu

F32 = jnp.float32
BF16 = jnp.bfloat16

EPS = 1e-6
NEG_INF = -1e30
HEAD_DIM = 64
LANES = 128
DIL_BLOCK = 128
DILATIONS = (1, 4, 16)
N_BUCKETS = 32
MAX_DISTANCE = 2048
N_MOD = 9
N_DEV = 8
SB_BLOCK = 256
SB_HEADS = 4
SB_WIDTH = SB_HEADS * HEAD_DIM
DIL_HEADS = 4
DIL_WIDTH = DIL_HEADS * HEAD_DIM
TOKEN_TILE = 512
PROJ_TILE = 1024
GRAD_TILE = 1024
FFN_CHUNKS = 2
VMEM_LIMIT_BYTES = 56 * 1024 * 1024

ADAM_LR = 0.001
ADAM_B1 = 0.9
ADAM_B2 = 0.999
ADAM_EPS = 1e-08
ADAM_WD = 0.01
ADAM_STEP = 10

NT_DIMS = (((1,), (1,)), ((), ()))
TN_DIMS = (((0,), (0,)), ((), ()))


def _params(*sem):
    return pltpu.CompilerParams(dimension_semantics=sem, vmem_limit_bytes=VMEM_LIMIT_BYTES)


def _once(spec):
    return pl.BlockSpec(spec.block_shape, spec.index_map, pipeline_mode=pl.Buffered(1))


def _dot(a, b):
    return jnp.dot(a, b, preferred_element_type=F32)


def _dot_nt(a, b):
    return lax.dot_general(a, b, NT_DIMS, preferred_element_type=F32)


def _dot_tn(a, b):
    return lax.dot_general(a, b, TN_DIMS, preferred_element_type=F32)


def _split_dot(a, b):
    hi = a.astype(BF16)
    lo = (a - hi.astype(F32)).astype(BF16)
    return _dot(hi, b) + _dot(lo, b)


def _sigmoid(z):
    return 1.0 / (1.0 + jnp.exp(-z))


def _norm(x):
    r = lax.rsqrt(jnp.mean(x * x, axis=-1, keepdims=True) + EPS)
    return x * r, r


def _modulate(x, g, mod_ref, k):
    n, _ = _norm(x)
    shift = mod_ref[3 * k:3 * k + 1, :]
    scale = mod_ref[3 * k + 1:3 * k + 2, :]
    return n * g * (1.0 + scale) + shift


def _modulate_bwd(dh, x, g, mod_ref, k):
    n, r = _norm(x)
    scale = mod_ref[3 * k + 1:3 * k + 2, :]
    dshift = jnp.sum(dh, axis=0, keepdims=True)
    dscale = jnp.sum(dh * n * g, axis=0, keepdims=True)
    dg = jnp.sum(dh * n * (1.0 + scale), axis=0, keepdims=True)
    dn = dh * g * (1.0 + scale)
    dx = r * (dn - n * jnp.mean(dn * n, axis=-1, keepdims=True))
    return dx, dshift, dscale, dg


class _Exchange:
    def __init__(self, arrays, scatter, relay=False):
        assert not (scatter and relay)
        self.arrays = list(arrays)
        self.scatter = scatter
        self.relay = relay
        self.n = len(self.arrays)
        self.out_shape = [
            jax.ShapeDtypeStruct((N_DEV,) + tuple(a.shape[1:] if scatter else a.shape), a.dtype)
            for a in self.arrays]
        n_remote = self.n * (N_DEV - 1)
        self.scratch_shapes = [pltpu.SemaphoreType.DMA((n_remote,)), pltpu.SemaphoreType.DMA((n_remote,)),
                               pltpu.SemaphoreType.DMA((self.n,))]

    def _copies(self, in_refs, out_refs, sems):
        send_sems, recv_sems, local_sems = sems
        x, y, c = lax.axis_index("x"), lax.axis_index("y"), lax.axis_index("c")
        me = 4 * x + 2 * y + c
        local, remote, relayed = [], {}, {}
        for a in range(self.n):
            src = in_refs[a].at[me] if self.scatter else in_refs[a]
            local.append(pltpu.make_async_copy(src, out_refs[a].at[me], local_sems.at[a]))
            for k in range(1, N_DEV):
                px = 1 - x if k & 4 else x
                py = 1 - y if k & 2 else y
                pc = 1 - c if k & 1 else c
                sem = a * (N_DEV - 1) + k - 1
                if self.relay and k & 1 and k > 1:
                    slot = 4 * px + 2 * py + c
                    relayed[a, k] = pltpu.make_async_remote_copy(
                        src_ref=out_refs[a].at[slot], dst_ref=out_refs[a].at[slot],
                        send_sem=send_sems.at[sem], recv_sem=recv_sems.at[sem],
                        device_id=(x, y, 1 - c), device_id_type=pl.DeviceIdType.MESH)
                    continue
                src = in_refs[a].at[4 * px + 2 * py + pc] if self.scatter else in_refs[a]
                remote[a, k] = pltpu.make_async_remote_copy(
                    src_ref=src, dst_ref=out_refs[a].at[me],
                    send_sem=send_sems.at[sem], recv_sem=recv_sems.at[sem],
                    device_id=(px, py, pc), device_id_type=pl.DeviceIdType.MESH)
        return local, remote, relayed

    def start(self, in_refs, out_refs, sems):
        local, remote, _ = self._copies(in_refs, out_refs, sems)
        for cp in local + list(remote.values()):
            cp.start()

    def wait(self, in_refs, out_refs, sems):
        local, remote, relayed = self._copies(in_refs, out_refs, sems)
        for (a, k), cp in relayed.items():
            remote[a, k - 1].wait_recv()
            cp.start()
        for (a, k), cp in remote.items():
            if (a, k + 1) not in relayed:
                cp.wait_recv()
        for cp in relayed.values():
            cp.wait_recv()
        for cp in list(remote.values()) + list(relayed.values()):
            cp.wait_send()
        for cp in local:
            cp.wait()


def _call(body, *, name, args, in_specs, out_specs, out_shape, scratch_shapes=(), grid=(),
          params=None, exchange=None):
    n_in, n_out = len(args), len(out_shape)
    if exchange is None:
        outs = pl.pallas_call(
            body, name=name, grid=grid, in_specs=list(in_specs), out_specs=list(out_specs),
            out_shape=list(out_shape), scratch_shapes=list(scratch_shapes), compiler_params=params,
        )(*args)
        return list(outs), []
    n_ex = exchange.n

    def wrapped(*refs):
        ins, refs = refs[:n_in], refs[n_in:]
        ex_in, refs = refs[:n_ex], refs[n_ex:]
        outs, refs = refs[:n_out], refs[n_out:]
        ex_out, refs = refs[:n_ex], refs[n_ex:]
        scratch, sems = refs[:len(refs) - 3], refs[len(refs) - 3:]
        if not grid:
            exchange.start(ex_in, ex_out, sems)
            body(*ins, *outs, *scratch)
            exchange.wait(ex_in, ex_out, sems)
            return
        first = functools.reduce(jnp.logical_and, [pl.program_id(a) == 0 for a in range(len(grid))])
        last = functools.reduce(jnp.logical_and, [pl.program_id(a) == grid[a] - 1 for a in range(len(grid))])

        @pl.when(first)
        def _():
            exchange.start(ex_in, ex_out, sems)

        body(*ins, *outs, *scratch)

        @pl.when(last)
        def _():
            exchange.wait(ex_in, ex_out, sems)

    any_spec = pl.BlockSpec(memory_space=pl.ANY)
    outs = pl.pallas_call(
        wrapped, name=name, grid=grid,
        in_specs=list(in_specs) + [any_spec] * n_ex, out_specs=list(out_specs) + [any_spec] * n_ex,
        out_shape=list(out_shape) + exchange.out_shape,
        scratch_shapes=list(scratch_shapes) + exchange.scratch_shapes, compiler_params=params,
    )(*args, *exchange.arrays)
    return list(outs[:n_out]), list(outs[n_out:])


def _exchange(arrays, scatter, name, relay=False):
    return _call(lambda: None, name=name, args=(), in_specs=(), out_specs=(), out_shape=(),
                 exchange=_Exchange(arrays, scatter, relay))[1]


def _ada_fwd(c_all, w, b):
    def body(c_ref, w_ref, b_ref, o_ref):
        cv = c_ref[...]
        s = (cv * _sigmoid(cv)).astype(BF16)
        o_ref[...] = _dot(s, w_ref[...].astype(BF16)) + b_ref[...]

    return pl.pallas_call(
        body, name="ada_fwd", out_shape=jax.ShapeDtypeStruct((c_all.shape[0], w.shape[1]), F32),
        compiler_params=pltpu.CompilerParams(vmem_limit_bytes=VMEM_LIMIT_BYTES),
    )(c_all, w, b)


def _ada_bwd(c_all, dmod_cols, dmod_all):
    def body(c_ref, dc_ref, da_ref, gw_ref, gb_ref):
        cv = c_ref[...]
        s = cv * _sigmoid(cv)
        gw_ref[...] = lax.dot_general(s, dc_ref[...], TN_DIMS, preferred_element_type=F32,
                                      precision=lax.Precision.HIGHEST)
        gb_ref[...] = jnp.sum(da_ref[...], axis=0, keepdims=True)

    return pl.pallas_call(
        body, name="ada_bwd",
        out_shape=(jax.ShapeDtypeStruct((c_all.shape[1], dmod_cols.shape[1]), F32),
                   jax.ShapeDtypeStruct((1, dmod_all.shape[1]), F32)),
        compiler_params=pltpu.CompilerParams(vmem_limit_bytes=VMEM_LIMIT_BYTES),
    )(c_all, dmod_cols, dmod_all)


def _ffn_fwd(x, mod, g, wg, wu, wd, k, tm, exchange=None):
    t, d = x.shape
    ns, _, fs = wg.shape
    nt = t // tm
    tpb = nt // mod.shape[0]
    rows = tm // FFN_CHUNKS

    def body(x_ref, mod_ref, g_ref, wg_ref, wu_ref, wd_ref, xo_ref, f_ref, gg_ref, uu_ref, h_sc, acc):
        j = pl.program_id(1)

        @pl.when(j == 0)
        def _():
            h_sc[...] = _modulate(x_ref[...], g_ref[...], mod_ref, k).astype(BF16)
            acc[...] = jnp.zeros_like(acc)

        chunks = [pl.ds(c * rows, rows) for c in range(FFN_CHUNKS)]
        wg, wu, wd = wg_ref[...], wu_ref[...], wd_ref[...]
        gates, ups = [], []
        for rs in chunks:
            h = h_sc[rs, :]
            gates.append(_dot(h, wg))
            ups.append(_dot(h, wu))
        acts = [(g * _sigmoid(g) * u).astype(BF16) for g, u in zip(gates, ups)]
        for rs, g, u in zip(chunks, gates, ups):
            gg_ref[rs, :] = g.astype(BF16)
            uu_ref[rs, :] = u.astype(BF16)
        downs = [_dot(a, wd) for a in acts]
        for rs, dn in zip(chunks, downs):
            acc[rs, :] += dn

        @pl.when(j == ns - 1)
        def _():
            f = acc[...]
            f_ref[...] = f.astype(BF16)
            xo_ref[...] = x_ref[...] + 0.5 * mod_ref[3 * k + 2:3 * k + 3, :] * f

    tok = pl.BlockSpec((tm, d), lambda i, j: (i, 0))
    hid = pl.BlockSpec((None, tm, fs), lambda i, j: (j, i, 0))
    return _call(
        body, name=f"ffn_fwd{k}", grid=(nt, ns), args=(x, mod, g, wg, wu, wd),
        in_specs=[_once(tok),
                  pl.BlockSpec((None, N_MOD, d), lambda i, j: (i // tpb, 0, 0)),
                  pl.BlockSpec((1, d), lambda i, j: (0, 0)),
                  pl.BlockSpec((None, d, fs), lambda i, j: (j, 0, 0)),
                  pl.BlockSpec((None, d, fs), lambda i, j: (j, 0, 0)),
                  pl.BlockSpec((None, fs, d), lambda i, j: (j, 0, 0))],
        out_specs=[tok, tok, hid, hid],
        out_shape=[jax.ShapeDtypeStruct((t, d), F32), jax.ShapeDtypeStruct((t, d), BF16),
                   jax.ShapeDtypeStruct((ns, t, fs), BF16), jax.ShapeDtypeStruct((ns, t, fs), BF16)],
        scratch_shapes=[pltpu.VMEM((tm, d), BF16), pltpu.VMEM((tm, d), F32)],
        params=_params("arbitrary", "arbitrary"), exchange=exchange)


def _ffn_bwd(dxo, x, f, mod, g, gate, up, wg, wu, wd, k, tm, exchange=None):
    t, d = x.shape
    ns, _, fs = wg.shape
    nt = t // tm
    nb = mod.shape[0]
    tpb = nt // nb
    rows = tm // FFN_CHUNKS

    def body(dxo_ref, x_ref, f_ref, mod_ref, g_ref, gg_ref, uu_ref, wg_ref, wu_ref, wd_ref,
             dx_ref, dgg_ref, duu_ref, act_ref, h_ref, df_ref, dmod_ref, dg_ref, acc):
        i, j = pl.program_id(0), pl.program_id(1)

        @pl.when(j == 0)
        def _():
            df = 0.5 * mod_ref[3 * k + 2:3 * k + 3, :] * dxo_ref[...]
            df_ref[...] = df.astype(BF16)
            h_ref[...] = _modulate(x_ref[...], g_ref[...], mod_ref, k).astype(BF16)
            acc[...] = jnp.zeros_like(acc)

        chunks = [pl.ds(c * rows, rows) for c in range(FFN_CHUNKS)]
        wg, wu, wd = wg_ref[...], wu_ref[...], wd_ref[...]
        dacts = [_dot_nt(df_ref[rs, :], wd) for rs in chunks]
        dgates, dups = [], []
        for rs, dact in zip(chunks, dacts):
            gv, uv = gg_ref[rs, :].astype(F32), uu_ref[rs, :].astype(F32)
            sig = _sigmoid(gv)
            s = gv * sig
            act_ref[rs, :] = (s * uv).astype(BF16)
            dups.append((dact * s).astype(BF16))
            dgates.append((dact * uv * (sig * (1.0 + gv * (1.0 - sig)))).astype(BF16))
        dhs = [_dot_nt(dg, wg) + _dot_nt(du, wu) for dg, du in zip(dgates, dups)]
        for rs, dg, du, dh in zip(chunks, dgates, dups, dhs):
            dgg_ref[rs, :] = dg
            duu_ref[rs, :] = du
            acc[rs, :] += dh

        @pl.when(j == ns - 1)
        def _():
            dx, dshift, dscale, dg = _modulate_bwd(acc[...], x_ref[...], g_ref[...], mod_ref, k)
            dxo_v = dxo_ref[...]
            dx_ref[...] = dxo_v + dx
            dgt = jnp.sum(0.5 * f_ref[...].astype(F32) * dxo_v, axis=0, keepdims=True)

            @pl.when(i % tpb == 0)
            def _():
                dmod_ref[...] = jnp.zeros_like(dmod_ref)

            @pl.when(i == 0)
            def _():
                dg_ref[...] = jnp.zeros_like(dg_ref)

            dmod_ref[0:1, :] += dshift
            dmod_ref[1:2, :] += dscale
            dmod_ref[2:3, :] += dgt
            dg_ref[0:1, :] += dg

    tok = pl.BlockSpec((tm, d), lambda i, j: (i, 0))
    hid = pl.BlockSpec((None, tm, fs), lambda i, j: (j, i, 0))
    return _call(
        body, name=f"ffn_bwd{k}", grid=(nt, ns), args=(dxo, x, f, mod, g, gate, up, wg, wu, wd),
        in_specs=[_once(tok), _once(tok), _once(tok),
                  pl.BlockSpec((None, N_MOD, d), lambda i, j: (i // tpb, 0, 0)),
                  pl.BlockSpec((1, d), lambda i, j: (0, 0)),
                  hid, hid,
                  pl.BlockSpec((None, d, fs), lambda i, j: (j, 0, 0)),
                  pl.BlockSpec((None, d, fs), lambda i, j: (j, 0, 0)),
                  pl.BlockSpec((None, fs, d), lambda i, j: (j, 0, 0))],
        out_specs=[tok, hid, hid, hid, tok, tok,
                   pl.BlockSpec((None, 8, d), lambda i, j: (i // tpb, 0, 0)),
                   pl.BlockSpec((8, d), lambda i, j: (0, 0))],
        out_shape=[jax.ShapeDtypeStruct((t, d), F32),
                   jax.ShapeDtypeStruct((ns, t, fs), BF16), jax.ShapeDtypeStruct((ns, t, fs), BF16),
                   jax.ShapeDtypeStruct((ns, t, fs), BF16),
                   jax.ShapeDtypeStruct((t, d), BF16), jax.ShapeDtypeStruct((t, d), BF16),
                   jax.ShapeDtypeStruct((nb, 8, d), F32), jax.ShapeDtypeStruct((8, d), F32)],
        scratch_shapes=[pltpu.VMEM((tm, d), F32)],
        params=_params("arbitrary", "arbitrary"), exchange=exchange)


def _mm_tn(a, b, a_spec, b_spec, out_shape, n_tiles, name, exchange=None):
    n_out = out_shape[0]
    block = tuple(out_shape[1:])
    last = n_tiles - 1
    flip = block[0] > block[1]
    if flip:
        block = block[::-1]

    def body(a_ref, b_ref, o_ref, acc):
        i, j = pl.program_id(0), pl.program_id(1)
        prod = _dot_tn(b_ref[...], a_ref[...]) if flip else _dot_tn(a_ref[...], b_ref[...])

        @pl.when(i == 0)
        def _():
            acc[j] = prod

        @pl.when(i > 0)
        def _():
            acc[j] += prod

        @pl.when(i == last)
        def _():
            total = acc[j]
            o_ref[j] = (total.T if flip else total).astype(o_ref.dtype)

    outs, sent = _call(
        body, name=name, grid=(n_tiles, n_out), args=(a, b), in_specs=[a_spec, b_spec],
        out_specs=[pl.BlockSpec(out_shape, lambda i, j: (0,) * len(out_shape))],
        out_shape=[jax.ShapeDtypeStruct(out_shape, BF16)],
        scratch_shapes=[pltpu.VMEM((n_out,) + block, F32)],
        params=_params("arbitrary", "arbitrary"), exchange=exchange)
    return (outs[0], sent) if exchange is not None else outs[0]


def _ffn_weight_grads(h, dgate, dup, act, df, tm, tag, stream=False):
    t, d = h.shape
    ns, _, fs = dgate.shape
    nt = t // tm
    tok = pl.BlockSpec((tm, d), lambda i, j: (i, 0))
    hid = pl.BlockSpec((None, tm, fs), lambda i, j: (j, i, 0))
    gwg = _mm_tn(h, dgate, tok, hid, (ns, d, fs), nt, f"grad_wg{tag}")
    if not stream:
        gwu = _mm_tn(h, dup, tok, hid, (ns, d, fs), nt, f"grad_wu{tag}")
        gwd = _mm_tn(act, df, hid, tok, (ns, fs, d), nt, f"grad_wd{tag}")
        return gwg, gwu, gwd
    gwu, sent_g = _mm_tn(h, dup, tok, hid, (ns, d, fs), nt, f"grad_wu{tag}", _Exchange([gwg], True))
    gwd, sent_u = _mm_tn(act, df, hid, tok, (ns, fs, d), nt, f"grad_wd{tag}", _Exchange([gwu], True))
    return sent_g[0], sent_u[0], gwd


def _stage_shape(rows, cols):
    return pltpu.VMEM((cols // LANES, rows, LANES), F32)


def _stage(value, stage_ref):
    for k in range(stage_ref.shape[0]):
        stage_ref[k] = value[:, k * LANES:(k + 1) * LANES]


def _to_residue_rows(stage_ref, dst_ref, dil):
    rows = stage_ref.shape[1] // dil
    for r in range(dil):
        for k in range(stage_ref.shape[0]):
            dst_ref[r, :, k * LANES:(k + 1) * LANES] = (
                stage_ref.at[k][pl.ds(r, rows, stride=dil), :].astype(dst_ref.dtype))


def _from_residue_rows(src_ref, stage_ref, dil):
    rows = stage_ref.shape[1] // dil
    chunks = range(stage_ref.shape[0])
    for r in range(dil):
        for k in chunks:
            stage_ref.at[k][pl.ds(r, rows, stride=dil), :] = src_ref[r, :, k * LANES:(k + 1) * LANES].astype(F32)
    return jnp.concatenate([stage_ref[k] for k in chunks], axis=1)


def _residue_shape(nb, seq, width, dil, dtype):
    return jax.ShapeDtypeStruct((nb, dil, seq // dil, width), dtype)


def _residue_spec(tm, tpb, cols, dil, col_block):
    return pl.BlockSpec((None, dil, tm // dil, cols),
                        lambda i, *rest: (i // tpb, 0, i % tpb, col_block(i, *rest)))


def _qkv_fwd(x, mod, g, win, tm):
    t, d = x.shape
    ns, _, cs = win.shape
    nt = t // tm
    nb = mod.shape[0]
    tpb = nt // nb
    seq = t // nb
    half = ns // 2
    n_res = len(DILATIONS) - 1

    def body(x_ref, mod_ref, g_ref, w_ref, sb_ref, dil_ref, *rest):
        res_refs, h_ref, sc = rest[:n_res], rest[n_res], rest[n_res + 1]
        j = pl.program_id(1)

        @pl.when(j == 0)
        def _():
            h_ref[...] = _modulate(x_ref[...], g_ref[...], mod_ref, 1).astype(BF16)

        res = _dot(h_ref[...], w_ref[...])

        @pl.when(j < half)
        def _():
            sb_ref[...] = res.astype(BF16)

        @pl.when(j >= half)
        def _():
            dil_ref[...] = res.astype(BF16)
            _stage(res, sc)
            for ref, dil in zip(res_refs, DILATIONS[1:]):
                _to_residue_rows(sc, ref, dil)

    def dil_col(i, j):
        return jnp.maximum(j - half, 0)

    tok = pl.BlockSpec((tm, d), lambda i, j: (i, 0))
    wide = jax.ShapeDtypeStruct((t, half * cs), BF16)
    outs = pl.pallas_call(
        body, name="qkv_fwd", grid=(nt, ns),
        in_specs=[_once(tok),
                  pl.BlockSpec((None, N_MOD, d), lambda i, j: (i // tpb, 0, 0)),
                  pl.BlockSpec((1, d), lambda i, j: (0, 0)),
                  pl.BlockSpec((None, d, cs), lambda i, j: (j, 0, 0))],
        out_specs=[pl.BlockSpec((tm, cs), lambda i, j: (i, jnp.minimum(j, half - 1))),
                   pl.BlockSpec((tm, cs), lambda i, j: (i, dil_col(i, j)))]
        + [_residue_spec(tm, tpb, cs, dil, dil_col) for dil in DILATIONS[1:]] + [tok],
        out_shape=[wide, wide] + [_residue_shape(nb, seq, half * cs, dil, BF16) for dil in DILATIONS[1:]]
        + [jax.ShapeDtypeStruct((t, d), BF16)],
        scratch_shapes=[_stage_shape(tm, cs)],
        compiler_params=_params("arbitrary", "arbitrary"),
    )(x, mod, g, win)
    qkv_dil = [outs[1]] + [a.reshape(t, half * cs) for a in outs[2:2 + n_res]]
    return outs[0], qkv_dil, outs[-1]


def _qkv_bwd(dqkv, dxo, x, mod, g, win, tm):
    t, d = x.shape
    ns, _, cs = win.shape
    nt = t // tm
    nb = mod.shape[0]
    tpb = nt // nb

    def body(dq_ref, dxo_ref, x_ref, mod_ref, g_ref, w_ref, dx_ref, dmod_ref, dg_ref, acc):
        i, j = pl.program_id(0), pl.program_id(1)

        @pl.when(j == 0)
        def _():
            acc[...] = jnp.zeros_like(acc)

        acc[...] += _dot_nt(dq_ref[...], w_ref[...])

        @pl.when(j == ns - 1)
        def _():
            dx, dshift, dscale, dg = _modulate_bwd(acc[...], x_ref[...], g_ref[...], mod_ref, 1)
            dx_ref[...] = dxo_ref[...] + dx

            @pl.when(i % tpb == 0)
            def _():
                dmod_ref[...] = jnp.zeros_like(dmod_ref)

            @pl.when(i == 0)
            def _():
                dg_ref[...] = jnp.zeros_like(dg_ref)

            dmod_ref[0:1, :] += dshift
            dmod_ref[1:2, :] += dscale
            dg_ref[0:1, :] += dg

    tok = pl.BlockSpec((tm, d), lambda i, j: (i, 0))
    return pl.pallas_call(
        body, name="qkv_bwd", grid=(nt, ns),
        in_specs=[pl.BlockSpec((tm, cs), lambda i, j: (i, j)), _once(tok), _once(tok),
                  pl.BlockSpec((None, N_MOD, d), lambda i, j: (i // tpb, 0, 0)),
                  pl.BlockSpec((1, d), lambda i, j: (0, 0)),
                  pl.BlockSpec((None, d, cs), lambda i, j: (j, 0, 0))],
        out_specs=[tok,
                   pl.BlockSpec((None, 8, d), lambda i, j: (i // tpb, 0, 0)),
                   pl.BlockSpec((8, d), lambda i, j: (0, 0))],
        out_shape=[jax.ShapeDtypeStruct((t, d), F32),
                   jax.ShapeDtypeStruct((nb, 8, d), F32), jax.ShapeDtypeStruct((8, d), F32)],
        scratch_shapes=[pltpu.VMEM((tm, d), F32)],
        compiler_params=_params("arbitrary", "arbitrary"),
    )(dqkv, dxo, x, mod, g, win)


def _heads(a):
    return [a[:, h * HEAD_DIM:(h + 1) * HEAD_DIM] for h in range(a.shape[1] // HEAD_DIM)]


def _scaled(q):
    return (q.astype(F32) * (HEAD_DIM ** -0.5)).astype(BF16)


def _sb_logits(qh, kh, tri, causal):
    zs = [_dot_nt(q, k) for q, k in zip(qh, kh)]
    es = [jnp.exp(-jnp.abs(z)) for z in zs]
    log_nots = [-(jnp.maximum(z, 0.0) + jnp.log(1.0 + e)) for z, e in zip(zs, es)]
    if causal is not None:
        log_nots = [jnp.where(causal, ln, 0.0) for ln in log_nots]
    return zs, es, [_split_dot(ln, tri) for ln in log_nots]


def _sb_masks():
    rows = lax.broadcasted_iota(jnp.int32, (SB_BLOCK, SB_BLOCK), 0)
    cols = lax.broadcasted_iota(jnp.int32, (SB_BLOCK, SB_BLOCK), 1)
    return (rows >= cols).astype(BF16), (rows <= cols).astype(BF16), cols < rows


def _sb_fwd(qkv, nb, seq, exchange=None):
    t = qkv.shape[0]
    n_pairs = (qkv.shape[1] // 3) // SB_WIDTH
    tb = SB_BLOCK
    n_blk = seq // tb

    def body(q_ref, k_ref, v_ref, o_ref, c_ref):
        tri, _, causal = _sb_masks()

        def key_block(qh, kj, carry, mask):
            ks = pl.multiple_of(kj * tb, tb)
            kh, vh = _heads(k_ref[pl.ds(ks, tb), :]), _heads(v_ref[pl.ds(ks, tb), :])
            zs, _, suffixes = _sb_logits(qh, kh, tri, mask)
            ws = [jnp.exp(z + suffix + cr[1]) for z, suffix, cr in zip(zs, suffixes, carry)]
            if mask is not None:
                ws = [jnp.where(mask, w, 0.0) for w in ws]
            pv = [_dot(w.astype(BF16), v) for w, v in zip(ws, vh)]
            return tuple((cr[0] + p, cr[1] + suffix[:, 0:1]) for cr, p, suffix in zip(carry, pv, suffixes))

        def query_block(qi, _):
            qs = pl.multiple_of(qi * tb, tb)
            qh = _heads(_scaled(q_ref[pl.ds(qs, tb), :]))
            zero = (jnp.zeros((tb, HEAD_DIM), F32), jnp.zeros((tb, 1), F32))
            carry = key_block(qh, qi, (zero,) * SB_HEADS, causal)
            carry = lax.fori_loop(0, qi, lambda it, cr: key_block(qh, qi - 1 - it, cr, None), carry)
            o_ref[pl.ds(qs, tb), :] = jnp.concatenate([cr[0] for cr in carry], axis=1)
            c_ref[pl.ds(qs, tb), :] = jnp.concatenate(
                [jnp.broadcast_to(cr[1], (tb, HEAD_DIM)) for cr in carry], axis=1)
            return 0

        lax.fori_loop(0, n_blk, query_block, 0)

    def spec(offset):
        return pl.BlockSpec((seq, SB_WIDTH), lambda b, p: (b, offset + p))

    out = jax.ShapeDtypeStruct((t, n_pairs * SB_WIDTH), F32)
    return _call(
        body, name="sb_fwd", grid=(nb, n_pairs), args=(qkv, qkv, qkv),
        in_specs=[spec(0), spec(n_pairs), spec(2 * n_pairs)],
        out_specs=[spec(0), spec(0)], out_shape=[out, out],
        params=_params("arbitrary", "arbitrary"), exchange=exchange)


def _sb_bwd(qkv, do, csum, nb, seq, exchange=None):
    t = qkv.shape[0]
    n_pairs = (qkv.shape[1] // 3) // SB_WIDTH
    tb = SB_BLOCK
    n_blk = seq // tb
    scale = HEAD_DIM ** -0.5

    def body(q_ref, k_ref, v_ref, do_ref, c_ref, dq_ref, dk_ref, dv_ref, dkt_acc, dvt_acc):
        tri, tri_prefix, causal = _sb_masks()
        dkt_acc[...] = jnp.zeros_like(dkt_acc)
        dvt_acc[...] = jnp.zeros_like(dvt_acc)

        def key_block(qh, qth, doh, doth, ch, kj, carry, mask):
            ks = pl.multiple_of(kj * tb, tb)
            kh, vh = _heads(k_ref[pl.ds(ks, tb), :]), _heads(v_ref[pl.ds(ks, tb), :])
            heads = range(SB_HEADS)
            zs, es, suffixes = _sb_logits(qh, kh, tri, mask)
            dws = [_dot_nt(doh[h], vh[h]) for h in heads]
            lefts = [carry[h][1] + suffixes[h][:, 0:1] for h in heads]
            ws = [jnp.exp(zs[h] + suffixes[h] + (ch[h] - lefts[h])) for h in heads]
            if mask is not None:
                ws = [jnp.where(mask, w, 0.0) for w in ws]
            dlws = [ws[h] * dws[h] for h in heads]
            dprefixes = [_split_dot(dlw, tri_prefix) for dlw in dlws]
            dvts = [_dot(doth[h], ws[h].astype(BF16)) for h in heads]
            dzbs = []
            for h in heads:
                sig = jnp.where(zs[h] >= 0.0, 1.0, es[h]) / (1.0 + es[h])
                dz = dlws[h] - sig * (carry[h][2] + dprefixes[h])
                if mask is not None:
                    dz = jnp.where(mask, dz, 0.0)
                dzbs.append(dz.astype(BF16))
            dkts = [_dot(qth[h], dzbs[h]) for h in heads]
            dqs = [_dot(dzbs[h], kh[h]) for h in heads]
            dkt_acc[:, pl.ds(ks, tb)] += jnp.concatenate(dkts, axis=0)
            dvt_acc[:, pl.ds(ks, tb)] += jnp.concatenate(dvts, axis=0)
            return tuple((carry[h][0] + dqs[h], lefts[h], carry[h][2] + dprefixes[h][:, tb - 1:tb])
                         for h in heads)

        def query_block(qi, _):
            qs = pl.multiple_of(qi * tb, tb)
            q = _scaled(q_ref[pl.ds(qs, tb), :])
            do = do_ref[pl.ds(qs, tb), :]
            qh, doh = _heads(q), _heads(do.astype(BF16))
            qth = [a.astype(F32).T.astype(BF16) for a in qh]
            doth = [a.T.astype(BF16) for a in _heads(do)]
            cv = c_ref[pl.ds(qs, tb), :]
            ch = [cv[:, h * HEAD_DIM:h * HEAD_DIM + 1] for h in range(SB_HEADS)]
            zero = (jnp.zeros((tb, HEAD_DIM), F32), jnp.zeros((tb, 1), F32), jnp.zeros((tb, 1), F32))
            carry = lax.fori_loop(
                0, qi, lambda kj, cr: key_block(qh, qth, doh, doth, ch, kj, cr, None), (zero,) * SB_HEADS)
            carry = key_block(qh, qth, doh, doth, ch, qi, carry, causal)
            dq = jnp.concatenate([cr[0] for cr in carry], axis=1) * scale
            dq_ref[pl.ds(qs, tb), :] = dq.astype(BF16)
            return 0

        lax.fori_loop(0, n_blk, query_block, 0)
        dk_ref[...] = dkt_acc[...].T.astype(BF16)
        dv_ref[...] = dvt_acc[...].T.astype(BF16)

    def spec(offset):
        return pl.BlockSpec((seq, SB_WIDTH), lambda b, p: (b, offset + p))

    out = jax.ShapeDtypeStruct((t, n_pairs * SB_WIDTH), BF16)
    return _call(
        body, name="sb_bwd", grid=(nb, n_pairs), args=(qkv, qkv, qkv, do, csum),
        in_specs=[spec(0), spec(n_pairs), spec(2 * n_pairs), spec(0), spec(0)],
        out_specs=[spec(0), spec(0), spec(0)],
        out_shape=[out, out, out],
        scratch_shapes=[pltpu.VMEM((SB_WIDTH, seq), F32), pltpu.VMEM((SB_WIDTH, seq), F32)],
        params=_params("arbitrary", "arbitrary"), exchange=exchange)


def _dil_block_scores(qh, kph, kch, bias_ref, has_prev, band_prev, band_cur):
    scale = HEAD_DIM ** -0.5
    heads = range(len(qh))
    no_prev = jnp.where(has_prev, 0.0, NEG_INF)
    zps = [_dot_nt(qh[h], kph[h]) for h in heads]
    zcs = [_dot_nt(qh[h], kch[h]) for h in heads]
    zps = [jnp.where(band_prev, zps[h] * scale + bias_ref[h, :, 0:DIL_BLOCK], NEG_INF) + no_prev for h in heads]
    zcs = [jnp.where(band_cur, zcs[h] * scale + bias_ref[h, :, DIL_BLOCK:2 * DIL_BLOCK], NEG_INF) for h in heads]
    return zps, zcs


def _dil_bands():
    rows = lax.broadcasted_iota(jnp.int32, (DIL_BLOCK, DIL_BLOCK), 0)
    cols = lax.broadcasted_iota(jnp.int32, (DIL_BLOCK, DIL_BLOCK), 1)
    return cols >= rows, cols <= rows


def _dil_fwd(qkv, bias, nb, seq, dil, exchange=None):
    t, width = qkv.shape
    n_pairs = (width // 3) // DIL_WIDTH
    bq = DIL_BLOCK
    n_blk = seq // bq
    per_seq = n_blk // dil
    heads = range(DIL_HEADS)

    def body(q_ref, k_ref, v_ref, bias_ref, o_ref, lse_ref):
        band_prev, band_cur = _dil_bands()

        def block(n, _):
            has_prev = (n & (per_seq - 1)) != 0
            qs = pl.multiple_of(n * bq, bq)
            ps = pl.multiple_of(jnp.maximum(n - 1, 0) * bq, bq)
            qh = _heads(q_ref[pl.ds(qs, bq), :])
            kp, kc = _heads(k_ref[pl.ds(ps, bq), :]), _heads(k_ref[pl.ds(qs, bq), :])
            vp, vc = _heads(v_ref[pl.ds(ps, bq), :]), _heads(v_ref[pl.ds(qs, bq), :])
            zps, zcs = _dil_block_scores(qh, kp, kc, bias_ref, has_prev, band_prev, band_cur)
            ms = [jnp.maximum(jnp.max(zps[h], axis=1, keepdims=True), jnp.max(zcs[h], axis=1, keepdims=True))
                  for h in heads]
            eps = [jnp.exp(zps[h] - ms[h]) for h in heads]
            ecs = [jnp.exp(zcs[h] - ms[h]) for h in heads]
            pvs = [_dot(eps[h].astype(BF16), vp[h]) + _dot(ecs[h].astype(BF16), vc[h]) for h in heads]
            dens = [jnp.sum(eps[h], axis=1, keepdims=True) + jnp.sum(ecs[h], axis=1, keepdims=True) for h in heads]
            o_ref[pl.ds(qs, bq), :] = jnp.concatenate([pvs[h] / dens[h] for h in heads], axis=1)
            lse_ref[pl.ds(qs, bq), :] = jnp.concatenate(
                [jnp.broadcast_to(ms[h] + jnp.log(dens[h]), (bq, HEAD_DIM)) for h in heads], axis=1)
            return 0

        lax.fori_loop(0, n_blk, block, 0)

    def spec(offset):
        return pl.BlockSpec((seq, DIL_WIDTH), lambda b, p: (b, offset + p))

    out = jax.ShapeDtypeStruct((t, n_pairs * DIL_WIDTH), F32)
    return _call(
        body, name=f"dil_fwd{dil}", grid=(nb, n_pairs), args=(qkv, qkv, qkv, bias),
        in_specs=[spec(0), spec(n_pairs), spec(2 * n_pairs),
                  pl.BlockSpec((DIL_HEADS, bq, 2 * bq), lambda b, p: (p, 0, 0))],
        out_specs=[spec(0), spec(0)], out_shape=[out, out],
        params=_params("arbitrary", "arbitrary"), exchange=exchange)


def _dil_bwd(qkv, bias, do, lse, delta, nb, seq, dil):
    t, width = qkv.shape
    n_pairs = (width // 3) // DIL_WIDTH
    bq = DIL_BLOCK
    n_blk = seq // bq
    per_seq = n_blk // dil
    scale = HEAD_DIM ** -0.5
    heads = range(DIL_HEADS)

    def body(q_ref, k_ref, v_ref, bias_ref, do_ref, lse_ref, dl_ref, dq_ref, dk_ref, dv_ref, db_ref):
        band_prev, band_cur = _dil_bands()
        dk_ref[...] = jnp.zeros_like(dk_ref)
        dv_ref[...] = jnp.zeros_like(dv_ref)

        @pl.when(pl.program_id(1) == 0)
        def _():
            db_ref[...] = jnp.zeros_like(db_ref)

        def block(n, _):
            has_prev = (n & (per_seq - 1)) != 0
            qs = pl.multiple_of(n * bq, bq)
            ps = pl.multiple_of(jnp.maximum(n - 1, 0) * bq, bq)
            qh = _heads(q_ref[pl.ds(qs, bq), :])
            kp, kc = _heads(k_ref[pl.ds(ps, bq), :]), _heads(k_ref[pl.ds(qs, bq), :])
            vp, vc = _heads(v_ref[pl.ds(ps, bq), :]), _heads(v_ref[pl.ds(qs, bq), :])
            doh = _heads(do_ref[pl.ds(qs, bq), :].astype(BF16))
            lse_v, dl_v = lse_ref[pl.ds(qs, bq), :], dl_ref[pl.ds(qs, bq), :]
            zps, zcs = _dil_block_scores(qh, kp, kc, bias_ref, has_prev, band_prev, band_cur)
            dpp = [_dot_nt(doh[h], vp[h]) for h in heads]
            dpc = [_dot_nt(doh[h], vc[h]) for h in heads]
            lse_h = [lse_v[:, h * HEAD_DIM:h * HEAD_DIM + 1] for h in heads]
            dl_h = [dl_v[:, h * HEAD_DIM:h * HEAD_DIM + 1] for h in heads]
            pps = [jnp.exp(zps[h] - lse_h[h]) for h in heads]
            pcs = [jnp.exp(zcs[h] - lse_h[h]) for h in heads]
            dvp = [_dot_tn(pps[h].astype(BF16), doh[h]) for h in heads]
            dvc = [_dot_tn(pcs[h].astype(BF16), doh[h]) for h in heads]
            dzps = [pps[h] * (dpp[h] - dl_h[h]) for h in heads]
            dzcs = [pcs[h] * (dpc[h] - dl_h[h]) for h in heads]
            dzp_b = [(dzps[h] * scale).astype(BF16) for h in heads]
            dzc_b = [(dzcs[h] * scale).astype(BF16) for h in heads]
            dqs = [_dot(dzp_b[h], kp[h]) + _dot(dzc_b[h], kc[h]) for h in heads]
            dkp = [_dot_tn(dzp_b[h], qh[h]) for h in heads]
            dkc = [_dot_tn(dzc_b[h], qh[h]) for h in heads]
            for h in heads:
                db_ref[h, :, 0:bq] += dzps[h]
                db_ref[h, :, bq:2 * bq] += dzcs[h]
            dq_ref[pl.ds(qs, bq), :] = jnp.concatenate(dqs, axis=1)
            dk_ref[pl.ds(ps, bq), :] += jnp.concatenate(dkp, axis=1)
            dk_ref[pl.ds(qs, bq), :] += jnp.concatenate(dkc, axis=1)
            dv_ref[pl.ds(ps, bq), :] += jnp.concatenate(dvp, axis=1)
            dv_ref[pl.ds(qs, bq), :] += jnp.concatenate(dvc, axis=1)
            return 0

        lax.fori_loop(0, n_blk, block, 0)

    def spec(offset):
        return pl.BlockSpec((seq, DIL_WIDTH), lambda p, b: (b, offset + p))

    bias_spec = pl.BlockSpec((DIL_HEADS, bq, 2 * bq), lambda p, b: (p, 0, 0))
    out = jax.ShapeDtypeStruct((t, n_pairs * DIL_WIDTH), F32)
    return pl.pallas_call(
        body, name=f"dil_bwd{dil}", grid=(n_pairs, nb),
        in_specs=[spec(0), spec(n_pairs), spec(2 * n_pairs), bias_spec, spec(0), spec(0), spec(0)],
        out_specs=[spec(0), spec(0), spec(0), bias_spec],
        out_shape=[out, out, out, jax.ShapeDtypeStruct(bias.shape, F32)],
        compiler_params=_params("arbitrary", "arbitrary"),
    )(qkv, qkv, qkv, bias, do, lse, delta)


def _head_mean(v, gmat):
    return _split_dot(v, gmat) * (1.0 / HEAD_DIM)


def _residue_views(arrays, nb, seq):
    return [a if dil == 1 else a.reshape(nb, dil, seq // dil, a.shape[1]) for a, dil in zip(arrays, DILATIONS)]


def _mix_out_fwd(osb, ocs, lses, gsb, gdil, gmat, wout, x, mod, tm):
    t, d = x.shape
    ds = osb.shape[1]
    nt = t // tm
    nb = mod.shape[0]
    tpb = nt // nb
    seq = t // nb
    n_cfg = len(DILATIONS)

    def body(osb_ref, *refs):
        oc_refs, lse_refs = refs[:n_cfg], refs[n_cfg:2 * n_cfg]
        gsb_ref, gdil_ref, gm_ref, w_ref, x_ref, mod_ref = refs[2 * n_cfg:2 * n_cfg + 6]
        xo_ref, on_ref, m_ref, odil_ref = refs[2 * n_cfg + 6:2 * n_cfg + 10]
        ld_refs = refs[2 * n_cfg + 10:3 * n_cfg + 10]
        stages, sc = refs[3 * n_cfg + 10:]
        ocv, lsev = [oc_refs[0][...]], [lse_refs[0][...]]
        for i, dil in enumerate(DILATIONS[1:]):
            ocv.append(_from_residue_rows(oc_refs[i + 1], stages.at[2 * i], dil))
            lsev.append(_from_residue_rows(lse_refs[i + 1], stages.at[2 * i + 1], dil))
        top = functools.reduce(jnp.maximum, lsev)
        total = top + jnp.log(sum(jnp.exp(l - top) for l in lsev))
        odil = sum(jnp.exp(l - total) * o for o, l in zip(ocv, lsev))
        odil_ref[...] = odil
        ld_refs[0][...] = total
        _stage(total, sc)
        for ref, dil in zip(ld_refs[1:], DILATIONS[1:]):
            _to_residue_rows(sc, ref, dil)
        gm = gm_ref[...]
        parts = []
        for o, g_ref in ((osb_ref[...], gsb_ref), (odil, gdil_ref)):
            parts.append(o * lax.rsqrt(_head_mean(o * o, gm) + EPS) * g_ref[...])
        on = jnp.concatenate(parts, axis=1).astype(BF16)
        on_ref[...] = on
        m = _dot(on, w_ref[...])
        m_ref[...] = m
        xo_ref[...] = x_ref[...] + mod_ref[5:6, :] * m

    tok = pl.BlockSpec((tm, d), lambda i: (i, 0))
    hd = pl.BlockSpec((tm, ds), lambda i: (i, 0))
    res = [hd] + [_residue_spec(tm, tpb, ds, dil, lambda i: 0) for dil in DILATIONS[1:]]
    res_shape = [jax.ShapeDtypeStruct((t, ds), F32)] + [_residue_shape(nb, seq, ds, dil, F32) for dil in DILATIONS[1:]]
    gain = pl.BlockSpec((1, ds), lambda i: (0, 0))
    outs = pl.pallas_call(
        body, name="mix_out_fwd", grid=(nt,),
        in_specs=[hd] + res + res + [gain, gain,
                  pl.BlockSpec((ds, ds), lambda i: (0, 0)),
                  pl.BlockSpec(wout.shape, lambda i: (0, 0)),
                  tok, pl.BlockSpec((None, N_MOD, d), lambda i: (i // tpb, 0, 0))],
        out_specs=[tok, pl.BlockSpec((tm, 2 * ds), lambda i: (i, 0)), tok, hd] + res,
        out_shape=[jax.ShapeDtypeStruct((t, d), F32), jax.ShapeDtypeStruct((t, 2 * ds), BF16),
                   jax.ShapeDtypeStruct((t, d), F32), jax.ShapeDtypeStruct((t, ds), F32)] + res_shape,
        scratch_shapes=[pltpu.VMEM((2 * (n_cfg - 1), ds // LANES, tm, LANES), F32), _stage_shape(tm, ds)],
        compiler_params=_params("arbitrary"),
    )(osb, *_residue_views(ocs, nb, seq), *_residue_views(lses, nb, seq), gsb, gdil, gmat, wout, x, mod)
    return outs[0], outs[1], outs[2], outs[3], [a.reshape(t, ds) for a in outs[4:]]


def _mix_out_bwd(dxo, m, mod, wout, osb, odil, gsb, gdil, gmat, tm):
    t, d = dxo.shape
    ds = osb.shape[1]
    nt = t // tm
    nb = mod.shape[0]
    tpb = nt // nb
    seq = t // nb
    n_cfg = len(DILATIONS)

    def body(dxo_ref, m_ref, mod_ref, w_ref, osb_ref, odil_ref, gsb_ref, gdil_ref, gm_ref,
             dm_ref, dosb_ref, *rest):
        do_refs, dl_refs = rest[:n_cfg], rest[n_cfg:2 * n_cfg]
        dmod_ref, dg_ref, sc = rest[2 * n_cfg:]
        dodil_ref, dldil_ref = do_refs[0], dl_refs[0]
        i = pl.program_id(0)
        dxo_v = dxo_ref[...]
        dm = (mod_ref[5:6, :] * dxo_v).astype(BF16)
        dm_ref[...] = dm
        dgt = jnp.sum(m_ref[...] * dxo_v, axis=0, keepdims=True)
        don = _dot_nt(dm, w_ref[...])
        gm = gm_ref[...]

        @pl.when(i % tpb == 0)
        def _():
            dmod_ref[...] = jnp.zeros_like(dmod_ref)

        @pl.when(i == 0)
        def _():
            dg_ref[...] = jnp.zeros_like(dg_ref)

        dmod_ref[2:3, :] += dgt
        groups = ((osb_ref, gsb_ref, dosb_ref), (odil_ref, gdil_ref, dodil_ref))
        for k, (o_ref, g_ref, do_ref) in enumerate(groups):
            o = o_ref[...]
            dn_out = don[:, k * ds:(k + 1) * ds]
            r = lax.rsqrt(_head_mean(o * o, gm) + EPS)
            n = o * r
            dg_ref[0:1, k * ds:(k + 1) * ds] += jnp.sum(dn_out * n, axis=0, keepdims=True)
            dn = dn_out * g_ref[...]
            do = r * (dn - n * _head_mean(dn * n, gm))
            do_ref[...] = do
            if k == 1:
                delta = _head_mean(do * o, gm) * float(HEAD_DIM)
                dldil_ref[...] = delta
                for value, refs in ((do, do_refs), (delta, dl_refs)):
                    _stage(value, sc)
                    for ref, dil in zip(refs[1:], DILATIONS[1:]):
                        _to_residue_rows(sc, ref, dil)

    tok = pl.BlockSpec((tm, d), lambda i: (i, 0))
    hd = pl.BlockSpec((tm, ds), lambda i: (i, 0))
    res = [hd] + [_residue_spec(tm, tpb, ds, dil, lambda i: 0) for dil in DILATIONS[1:]]
    res_shape = [jax.ShapeDtypeStruct((t, ds), F32)] + [_residue_shape(nb, seq, ds, dil, F32) for dil in DILATIONS[1:]]
    gain = pl.BlockSpec((1, ds), lambda i: (0, 0))
    outs = pl.pallas_call(
        body, name="mix_out_bwd", grid=(nt,),
        in_specs=[tok, tok, pl.BlockSpec((None, N_MOD, d), lambda i: (i // tpb, 0, 0)),
                  pl.BlockSpec(wout.shape, lambda i: (0, 0)), hd, hd, gain, gain,
                  pl.BlockSpec((ds, ds), lambda i: (0, 0))],
        out_specs=[tok, hd] + res + res
        + [pl.BlockSpec((None, 8, d), lambda i: (i // tpb, 0, 0)), pl.BlockSpec((8, 2 * ds), lambda i: (0, 0))],
        out_shape=[jax.ShapeDtypeStruct((t, d), BF16), jax.ShapeDtypeStruct((t, ds), F32)] + res_shape + res_shape
        + [jax.ShapeDtypeStruct((nb, 8, d), F32), jax.ShapeDtypeStruct((8, 2 * ds), F32)],
        scratch_shapes=[_stage_shape(tm, ds)],
        compiler_params=_params("arbitrary"),
    )(dxo, m, mod, wout, osb, odil, gsb, gdil, gmat)
    flat = [a.reshape(t, ds) for a in outs[2:2 + 2 * n_cfg]]
    return outs[0], outs[1], flat[:n_cfg], flat[n_cfg:], outs[-2], outs[-1]


def _merge_dqkv(sb_parts, dil_parts, nb, tm):
    t, ds = sb_parts[0].shape
    nt = t // tm
    tpb = nt // nb
    seq = t // nb
    n_cfg = len(DILATIONS)

    def body(*refs):
        sb_refs, dil_refs = refs[:3], refs[3:3 + 3 * n_cfg]
        o_ref, sc = refs[3 + 3 * n_cfg:]
        for k in range(3):
            o_ref[:, k * ds:(k + 1) * ds] = sb_refs[k][...]
            total = dil_refs[k * n_cfg][...]
            for i, dil in enumerate(DILATIONS[1:]):
                total = total + _from_residue_rows(dil_refs[k * n_cfg + i + 1], sc, dil)
            o_ref[:, (3 + k) * ds:(4 + k) * ds] = total.astype(BF16)

    hd = pl.BlockSpec((tm, ds), lambda i: (i, 0))
    res = [hd] + [_residue_spec(tm, tpb, ds, dil, lambda i: 0) for dil in DILATIONS[1:]]
    views = [v for parts in dil_parts for v in _residue_views(parts, nb, seq)]
    return pl.pallas_call(
        body, name="merge_dqkv", grid=(nt,),
        in_specs=[hd] * 3 + res * 3,
        out_specs=pl.BlockSpec((tm, 6 * ds), lambda i: (i, 0)),
        out_shape=jax.ShapeDtypeStruct((t, 6 * ds), BF16),
        scratch_shapes=[_stage_shape(tm, ds)],
        compiler_params=_params("arbitrary"),
    )(*sb_parts, *views)


def _loss_head(x, target, g, tm):
    t, d = x.shape

    def body(x_ref, t_ref, g_ref, dx_ref, acc_ref):
        @pl.when(pl.program_id(0) == 0)
        def _():
            acc_ref[...] = jnp.zeros_like(acc_ref)

        n, r = _norm(x_ref[...])
        gv = g_ref[...]
        err = n * gv - t_ref[...]
        dy = err * (1.0 / d)
        acc_ref[0:1, :] += jnp.sum(err * err, axis=0, keepdims=True)
        acc_ref[1:2, :] += jnp.sum(dy * n, axis=0, keepdims=True)
        dn = dy * gv
        dx_ref[...] = r * (dn - n * jnp.mean(dn * n, axis=-1, keepdims=True))

    tok = pl.BlockSpec((tm, d), lambda i: (i, 0))
    return pl.pallas_call(
        body, name="loss_head", grid=(t // tm,),
        in_specs=[tok, tok, pl.BlockSpec((1, d), lambda i: (0, 0))],
        out_specs=[tok, pl.BlockSpec((8, d), lambda i: (0, 0))],
        out_shape=[jax.ShapeDtypeStruct((t, d), F32), jax.ShapeDtypeStruct((8, d), F32)],
        compiler_params=_params("arbitrary"),
    )(x, target, g)


def _row_tile(rows):
    if rows <= 256:
        return rows
    for cand in range(256, 15, -16):
        if rows % cand == 0:
            return cand
    return rows


def _adamw(w, parts, m, v, name):
    rows, cols = w.shape
    n_parts = parts.shape[0]
    tr = _row_tile(rows)
    c1 = 1.0 / (1.0 - ADAM_B1 ** ADAM_STEP)
    c2 = 1.0 / (1.0 - ADAM_B2 ** ADAM_STEP)

    def body(w_ref, p_ref, m_ref, v_ref, g_ref, d_ref, nm_ref, nv_ref):
        g = p_ref[0].astype(F32)
        for i in range(1, n_parts):
            g = g + p_ref[i].astype(F32)
        nm = ADAM_B1 * m_ref[...] + (1.0 - ADAM_B1) * g
        nv = ADAM_B2 * v_ref[...] + (1.0 - ADAM_B2) * (g * g)
        g_ref[...] = g
        nm_ref[...] = nm
        nv_ref[...] = nv
        d_ref[...] = -ADAM_LR * ((nm * c1) / (jnp.sqrt(nv * c2) + ADAM_EPS) + ADAM_WD * w_ref[...])

    blk = pl.BlockSpec((tr, cols), lambda i: (i, 0))
    out = jax.ShapeDtypeStruct((rows, cols), F32)
    return pl.pallas_call(
        body, name=name, grid=(rows // tr,),
        in_specs=[blk, pl.BlockSpec((n_parts, tr, cols), lambda i: (0, i, 0)), blk, blk],
        out_specs=[blk, blk, blk, blk], out_shape=[out, out, out, out],
        compiler_params=_params("arbitrary"),
    )(w, parts, m, v)


def _t5_bucket(n):
    max_exact = N_BUCKETS // 2
    nf = np.maximum(n, 1).astype(np.float32)
    large = max_exact + (np.log(nf / max_exact) / math.log(MAX_DISTANCE / max_exact)
                         * (N_BUCKETS - max_exact)).astype(np.int32)
    large = np.minimum(large, N_BUCKETS - 1)
    return np.where(n < max_exact, n, large).astype(np.int32)


def _bucket_onehot():
    table = np.zeros((len(DILATIONS), 2 * DIL_BLOCK + 1, N_BUCKETS), np.float32)
    for i, dil in enumerate(DILATIONS):
        buckets = _t5_bucket(np.arange(DIL_BLOCK + 1) * dil)
        for m in range(DIL_BLOCK + 1):
            table[i, m, buckets[DIL_BLOCK - m]] = 1.0
    return table


def _bias_blocks(rel_bias):
    row = jnp.einsum("cmn,nh->chm", _bucket_onehot(), rel_bias, precision=lax.Precision.HIGHEST)
    n_cfg, n_heads, width = row.shape
    tiled = jnp.tile(row, (1, 1, DIL_BLOCK))[..., :DIL_BLOCK * (width - 1)]
    return tiled.reshape(n_cfg, n_heads, DIL_BLOCK, width - 1)


def _bias_blocks_bwd(dblocks):
    n_cfg, n_heads = dblocks.shape[:2]
    width = 2 * DIL_BLOCK + 1
    flat = dblocks.reshape(n_cfg, n_heads, DIL_BLOCK * (width - 1))
    flat = jnp.pad(flat, ((0, 0), (0, 0), (0, DIL_BLOCK)))
    drow = jnp.sum(flat.reshape(n_cfg, n_heads, DIL_BLOCK, width), axis=2)
    return jnp.einsum("chm,cmn->nh", drow, _bucket_onehot(), precision=lax.Precision.HIGHEST)


def _pad_to(a, axis, size):
    pad = [(0, 0)] * a.ndim
    pad[axis] = (0, size - a.shape[axis])
    return jnp.pad(a, pad)


def _lane_pad(n):
    return -(-n // LANES) * LANES


def _local_step(x, target, mod, gains, weights, rel_bias, tm, distributed):
    nb, seq, d = x.shape
    t = nb * seq
    g_ffn1, g_mix, g_sb, g_dil, g_ffn2, g_final = gains
    wg1, wu1, wd1 = weights[:3]
    x0 = x.reshape(t, d)
    ds = g_sb.shape[1]
    gmat = jnp.asarray(np.kron(np.eye(ds // HEAD_DIM), np.ones((HEAD_DIM, HEAD_DIM))), BF16)
    bias = _bias_blocks(rel_bias)

    def beside(arrays, scatter):
        return _Exchange(arrays, scatter) if distributed else None

    tp, tg = min(PROJ_TILE, seq), min(GRAD_TILE, t)

    (x1, f1, gate1, up1), got = _ffn_fwd(x0, mod, g_ffn1, wg1, wu1, wd1, 0, tp, beside(weights[3:5], False))
    win, wout = got if distributed else weights[3:5]
    wout2 = wout.reshape(-1, d)
    qkv, qkvd, h2 = _qkv_fwd(x1, mod, g_mix, win, tp)
    (osb, csb), got = _sb_fwd(qkv, nb, seq, beside(weights[5:7], False))
    wg2, wu2 = got if distributed else weights[5:7]
    ocs, lses = [], []
    for i, dil in enumerate(DILATIONS):
        (oc, lse), got = _dil_fwd(qkvd[i], bias[i], nb, seq, dil, beside(weights[7:8], False) if i == 0 else None)
        if i == 0:
            wd2 = got[0] if distributed else weights[7]
        ocs.append(oc)
        lses.append(lse)
    x2, on, mix, odil, ldil = _mix_out_fwd(osb, ocs, lses, g_sb, g_dil, gmat, wout2, x1, mod, tm)
    (x3, f3, gate3, up3), _ = _ffn_fwd(x2, mod, g_ffn2, wg2, wu2, wd2, 2, tp)
    dx3, head = _loss_head(x3, target.reshape(t, d), g_final, tm)
    loss_sum = 0.5 * jnp.sum(head[0]) / d
    dg_final = head[1:2]

    (dx2, dgate3, dup3, act3, h3, df3, dmod3, dg_ffn2), _ = _ffn_bwd(
        dx3, x2, f3, mod, g_ffn2, gate3, up3, wg2, wu2, wd2, 2, tp)
    gwg2, gwu2, gwd2 = _ffn_weight_grads(h3, dgate3, dup3, act3, df3, tg, 2)

    dm, dosb, dodil, dldil, dmod2b, dg_heads = _mix_out_bwd(
        dx2, mix, mod, wout2, osb, odil, g_sb, g_dil, gmat, tm)
    n_out = wout.shape[0]
    gwout = _mm_tn(on, dm,
                   pl.BlockSpec((tg, wout.shape[1]), lambda i, j: (i, j)),
                   pl.BlockSpec((tg, d), lambda i, j: (i, 0)),
                   wout.shape, t // tg, "grad_wout")

    (dq_sb, dk_sb, dv_sb), parts_late = _sb_bwd(qkv, dosb, csb, nb, seq,
                                                beside([gwout, gwg2, gwu2, gwd2], True))
    dil_grads = [_dil_bwd(qkvd[i], bias[i], dodil[i], ldil[i], dldil[i], nb, seq, dil)
                 for i, dil in enumerate(DILATIONS)]
    dqkv = _merge_dqkv([dq_sb, dk_sb, dv_sb], [[g[k] for g in dil_grads] for k in range(3)], nb, tm)
    drel = _bias_blocks_bwd(jnp.stack([g[3] for g in dil_grads]))

    dx1, dmod2a, dg_mix = _qkv_bwd(dqkv, dx2, x1, mod, g_mix, win, tp)
    n_in, _, cs = win.shape
    gwin = _mm_tn(h2, dqkv,
                  pl.BlockSpec((tg, d), lambda i, j: (i, 0)),
                  pl.BlockSpec((tg, cs), lambda i, j: (i, j)),
                  win.shape, t // tg, "grad_win")

    (dx0, dgate1, dup1, act1, h1, df1, dmod1, dg_ffn1), parts_mid = _ffn_bwd(
        dx1, x0, f1, mod, g_ffn1, gate1, up1, wg1, wu1, wd1, 0, tp, beside([gwin], True))
    gw1 = _ffn_weight_grads(h1, dgate1, dup1, act1, df1, tg, 0, stream=distributed)

    dmod = jnp.concatenate([dmod1[:, 0:3], dmod2a[:, 0:2], dmod2b[:, 2:3], dmod3[:, 0:3]], axis=1)
    wgrads = tuple(gw1) + (tuple(parts_mid + parts_late) if distributed else (gwin, gwout, gwg2, gwu2, gwd2))
    ggrads = (dg_ffn1[0:1], dg_mix[0:1], dg_heads[0:1], drel, dg_ffn2[0:1], dg_final)
    return loss_sum, dx0.reshape(nb, seq, d), wgrads, dmod, ggrads


def kernel(x, c, w_ada, b_ada, g_ffn1, w1_gate, w1_up, w1_down, g_mix, w_in, g_sb_out, g_dil_out, w_out, rel_bias, g_ffn2, w2_gate, w2_up, w2_down, g_final, loss_target, m_w_ada, m_b_ada, m_g_ffn1, m_w1_gate, m_w1_up, m_w1_down, m_g_mix, m_w_in, m_g_sb_out, m_g_dil_out, m_w_out, m_rel_bias, m_g_ffn2, m_w2_gate, m_w2_up, m_w2_down, m_g_final, v_w_ada, v_b_ada, v_g_ffn1, v_w1_gate, v_w1_up, v_w1_down, v_g_mix, v_w_in, v_g_sb_out, v_g_dil_out, v_w_out, v_rel_bias, v_g_ffn2, v_w2_gate, v_w2_up, v_w2_down, v_g_final):
    nb, seq, d = x.shape
    me = 4 * lax.axis_index("x") + 2 * lax.axis_index("y") + lax.axis_index("c")
    tm = min(TOKEN_TILE, seq)
    fs = w1_gate.shape[2]
    fs_pad = _lane_pad(fs)
    ada_cols = w_ada.shape[2]

    def col_shard(w):
        return _pad_to(w[0].astype(BF16), 1, fs_pad)

    def row_shard(w):
        return _pad_to(w[0].astype(BF16), 0, fs_pad)

    shards = [col_shard(w1_gate), col_shard(w1_up), row_shard(w1_down), w_in[0].astype(BF16),
              w_out[0].astype(BF16), col_shard(w2_gate), col_shard(w2_up), row_shard(w2_down)]
    gathered = _exchange([_pad_to(c, 0, 8)] + shards[:3], False, "gather_first", relay=True)
    c_all = gathered[0][:, :nb].reshape(N_DEV * nb, d)
    weights = gathered[1:] + shards[3:]

    b_cols = lax.dynamic_slice(b_ada, (0, me * ada_cols), (1, ada_cols))
    mod_part = _ada_fwd(c_all, w_ada[0], b_cols)
    mod_all = _exchange([mod_part], False, "gather_mod")[0]
    mod = lax.dynamic_slice(mod_all, (0, me * nb, 0), (N_DEV, nb, ada_cols))
    mod = mod.transpose(1, 0, 2).reshape(nb, N_MOD, d)

    n_sb = g_sb_out.shape[1] * g_sb_out.shape[2]
    gains = (g_ffn1, g_mix, g_sb_out.reshape(1, n_sb), g_dil_out.reshape(1, -1), g_ffn2,
             g_final.reshape(1, d))
    loss_sum, grad_x, parts, dmod, ggrads = _local_step(x, loss_target, mod, gains, weights, rel_bias, tm, True)
    loss = lax.psum(loss_sum, ("x", "y", "c"))

    dg_ffn1, dg_mix, dg_heads, drel, dg_ffn2, dg_final = ggrads
    width = max(d, dg_heads.shape[1], drel.size)
    small = jnp.concatenate(
        [_pad_to(a.reshape(1, -1), 1, width) for a in (dg_ffn1, dg_mix, dg_ffn2, dg_final, dg_heads, drel)]
        + [jnp.zeros((2, width), F32)], axis=0)
    dmod_pad = _pad_to(dmod.reshape(nb, N_MOD * d), 0, 8)
    last_part, dmod_all, small_all = _exchange(
        [parts[2], jnp.broadcast_to(dmod_pad, (N_DEV,) + dmod_pad.shape),
         jnp.broadcast_to(small, (N_DEV,) + small.shape)], True, "scatter_last")
    parts = parts[:2] + (last_part,) + parts[3:]
    dmod_all = dmod_all[:, :nb].reshape(N_DEV * nb, N_MOD * d)
    dmod_cols = lax.dynamic_slice(dmod_all, (0, me * ada_cols), (N_DEV * nb, ada_cols))
    gw_ada, gb_ada = _ada_bwd(c_all, dmod_cols, dmod_all)

    def small_part(row, size, shape):
        return small_all[:, row, :size].reshape((N_DEV,) + shape)

    n_rel = rel_bias.shape
    updates = {
        "w_ada": (w_ada[0], gw_ada[None], m_w_ada[0], v_w_ada[0]),
        "b_ada": (b_ada, gb_ada[None], m_b_ada, v_b_ada),
        "g_ffn1": (g_ffn1, small_part(0, d, (1, d)), m_g_ffn1, v_g_ffn1),
        "w1_gate": (w1_gate[0], parts[0][:, :, :fs], m_w1_gate[0], v_w1_gate[0]),
        "w1_up": (w1_up[0], parts[1][:, :, :fs], m_w1_up[0], v_w1_up[0]),
        "w1_down": (w1_down[0], parts[2][:, :fs, :], m_w1_down[0], v_w1_down[0]),
        "g_mix": (g_mix, small_part(1, d, (1, d)), m_g_mix, v_g_mix),
        "w_in": (w_in[0], parts[3], m_w_in[0], v_w_in[0]),
        "g_sb_out": (g_sb_out[0], small_all[:, 4, :n_sb].reshape((N_DEV,) + g_sb_out.shape[1:]),
                     m_g_sb_out[0], v_g_sb_out[0]),
        "g_dil_out": (g_dil_out[0], small_all[:, 4, n_sb:dg_heads.shape[1]].reshape((N_DEV,) + g_dil_out.shape[1:]),
                      m_g_dil_out[0], v_g_dil_out[0]),
        "w_out": (w_out[0], parts[4], m_w_out[0], v_w_out[0]),
        "rel_bias": (rel_bias, small_part(5, drel.size, n_rel), m_rel_bias, v_rel_bias),
        "g_ffn2": (g_ffn2, small_part(2, d, (1, d)), m_g_ffn2, v_g_ffn2),
        "w2_gate": (w2_gate[0], parts[5][:, :, :fs], m_w2_gate[0], v_w2_gate[0]),
        "w2_up": (w2_up[0], parts[6][:, :, :fs], m_w2_up[0], v_w2_up[0]),
        "w2_down": (w2_down[0], parts[7][:, :fs, :], m_w2_down[0], v_w2_down[0]),
        "g_final": (g_final.reshape(1, d), small_part(3, d, (1, d)), m_g_final.reshape(1, d), v_g_final.reshape(1, d)),
    }
    shapes = {"w_ada": w_ada.shape, "b_ada": b_ada.shape, "g_ffn1": g_ffn1.shape, "w1_gate": w1_gate.shape,
              "w1_up": w1_up.shape, "w1_down": w1_down.shape, "g_mix": g_mix.shape, "w_in": w_in.shape,
              "g_sb_out": g_sb_out.shape, "g_dil_out": g_dil_out.shape, "w_out": w_out.shape,
              "rel_bias": rel_bias.shape, "g_ffn2": g_ffn2.shape, "w2_gate": w2_gate.shape,
              "w2_up": w2_up.shape, "w2_down": w2_down.shape, "g_final": g_final.shape}
    grads, deltas, new_m, new_v = [], [], [], []
    for name, (w, p, m, v) in updates.items():
        g, dw, nm, nv = _adamw(w, p, m, v, f"adamw_{name}")
        grads.append(g.reshape(shapes[name]))
        deltas.append(dw.reshape(shapes[name]))
        new_m.append(nm.reshape(shapes[name]))
        new_v.append(nv.reshape(shapes[name]))
    return (loss, grad_x, *grads, *deltas, *new_m, *new_v)
```

```python
import functools
import math

import numpy as np
import jax
import jax.numpy as jnp
from jax import lax
from jax.experimental import pallas as pl
from jax.experimental.pallas import tpu as pltpu

F32 = jnp.float32
BF16 = jnp.bfloat16

EPS = 1e-6
NEG_INF = -1e30
HEAD_DIM = 64
LANES = 128
DIL_BLOCK = 128
DILATIONS = (1, 4, 16)
N_BUCKETS = 32
MAX_DISTANCE = 2048
N_MOD = 9
N_DEV = 8
SB_BLOCK = 256
SB_HEADS = 4
SB_WIDTH = SB_HEADS * HEAD_DIM
DIL_HEADS = 4
DIL_WIDTH = DIL_HEADS * HEAD_DIM
TOKEN_TILE = 512
PROJ_TILE = 1024
GRAD_TILE = 1024
FFN_CHUNKS = 2
VMEM_LIMIT_BYTES = 56 * 1024 * 1024

ADAM_LR = 0.001
ADAM_B1 = 0.9
ADAM_B2 = 0.999
ADAM_EPS = 1e-08
ADAM_WD = 0.01
ADAM_STEP = 10

NT_DIMS = (((1,), (1,)), ((), ()))
TN_DIMS = (((0,), (0,)), ((), ()))


def _params(*sem):
    return pltpu.CompilerParams(dimension_semantics=sem, vmem_limit_bytes=VMEM_LIMIT_BYTES)


def _once(spec):
    return pl.BlockSpec(spec.block_shape, spec.index_map, pipeline_mode=pl.Buffered(1))


def _dot(a, b):
    return jnp.dot(a, b, preferred_element_type=F32)


def _dot_nt(a, b):
    return lax.dot_general(a, b, NT_DIMS, preferred_element_type=F32)


def _dot_tn(a, b):
    return lax.dot_general(a, b, TN_DIMS, preferred_element_type=F32)


def _split_dot(a, b):
    hi = a.astype(BF16)
    lo = (a - hi.astype(F32)).astype(BF16)
    return _dot(hi, b) + _dot(lo, b)


def _sigmoid(z):
    return 1.0 / (1.0 + jnp.exp(-z))


def _norm(x):
    r = lax.rsqrt(jnp.mean(x * x, axis=-1, keepdims=True) + EPS)
    return x * r, r


def _modulate(x, g, mod_ref, k):
    n, _ = _norm(x)
    shift = mod_ref[3 * k:3 * k + 1, :]
    scale = mod_ref[3 * k + 1:3 * k + 2, :]
    return n * g * (1.0 + scale) + shift


def _modulate_bwd(dh, x, g, mod_ref, k):
    n, r = _norm(x)
    scale = mod_ref[3 * k + 1:3 * k + 2, :]
    dshift = jnp.sum(dh, axis=0, keepdims=True)
    dscale = jnp.sum(dh * n * g, axis=0, keepdims=True)
    dg = jnp.sum(dh * n * (1.0 + scale), axis=0, keepdims=True)
    dn = dh * g * (1.0 + scale)
    dx = r * (dn - n * jnp.mean(dn * n, axis=-1, keepdims=True))
    return dx, dshift, dscale, dg


class _Exchange:
    def __init__(self, arrays, scatter, relay=False):
        assert not (scatter and relay)
        self.arrays = list(arrays)
        self.scatter = scatter
        self.relay = relay
        self.n = len(self.arrays)
        self.out_shape = [
            jax.ShapeDtypeStruct((N_DEV,) + tuple(a.shape[1:] if scatter else a.shape), a.dtype)
            for a in self.arrays]
        n_remote = self.n * (N_DEV - 1)
        self.scratch_shapes = [pltpu.SemaphoreType.DMA((n_remote,)), pltpu.SemaphoreType.DMA((n_remote,)),
                               pltpu.SemaphoreType.DMA((self.n,))]

    def _copies(self, in_refs, out_refs, sems):
        send_sems, recv_sems, local_sems = sems
        x, y, c = lax.axis_index("x"), lax.axis_index("y"), lax.axis_index("c")
        me = 4 * x + 2 * y + c
        local, remote, relayed = [], {}, {}
        for a in range(self.n):
            src = in_refs[a].at[me] if self.scatter else in_refs[a]
            local.append(pltpu.make_async_copy(src, out_refs[a].at[me], local_sems.at[a]))
            for k in range(1, N_DEV):
                px = 1 - x if k & 4 else x
                py = 1 - y if k & 2 else y
                pc = 1 - c if k & 1 else c
                sem = a * (N_DEV - 1) + k - 1
                if self.relay and k & 1 and k > 1:
                    slot = 4 * px + 2 * py + c
                    relayed[a, k] = pltpu.make_async_remote_copy(
                        src_ref=out_refs[a].at[slot], dst_ref=out_refs[a].at[slot],
                        send_sem=send_sems.at[sem], recv_sem=recv_sems.at[sem],
                        device_id=(x, y, 1 - c), device_id_type=pl.DeviceIdType.MESH)
                    continue
                src = in_refs[a].at[4 * px + 2 * py + pc] if self.scatter else in_refs[a]
                remote[a, k] = pltpu.make_async_remote_copy(
                    src_ref=src, dst_ref=out_refs[a].at[me],
                    send_sem=send_sems.at[sem], recv_sem=recv_sems.at[sem],
                    device_id=(px, py, pc), device_id_type=pl.DeviceIdType.MESH)
        return local, remote, relayed

    def start(self, in_refs, out_refs, sems):
        local, remote, _ = self._copies(in_refs, out_refs, sems)
        for cp in local + list(remote.values()):
            cp.start()

    def wait(self, in_refs, out_refs, sems):
        local, remote, relayed = self._copies(in_refs, out_refs, sems)
        for (a, k), cp in relayed.items():
            remote[a, k - 1].wait_recv()
            cp.start()
        for (a, k), cp in remote.items():
            if (a, k + 1) not in relayed:
                cp.wait_recv()
        for cp in relayed.values():
            cp.wait_recv()
        for cp in list(remote.values()) + list(relayed.values()):
            cp.wait_send()
        for cp in local:
            cp.wait()


def _call(body, *, name, args, in_specs, out_specs, out_shape, scratch_shapes=(), grid=(),
          params=None, exchange=None):
    n_in, n_out = len(args), len(out_shape)
    if exchange is None:
        outs = pl.pallas_call(
            body, name=name, grid=grid, in_specs=list(in_specs), out_specs=list(out_specs),
            out_shape=list(out_shape), scratch_shapes=list(scratch_shapes), compiler_params=params,
        )(*args)
        return list(outs), []
    n_ex = exchange.n

    def wrapped(*refs):
        ins, refs = refs[:n_in], refs[n_in:]
        ex_in, refs = refs[:n_ex], refs[n_ex:]
        outs, refs = refs[:n_out], refs[n_out:]
        ex_out, refs = refs[:n_ex], refs[n_ex:]
        scratch, sems = refs[:len(refs) - 3], refs[len(refs) - 3:]
        if not grid:
            exchange.start(ex_in, ex_out, sems)
            body(*ins, *outs, *scratch)
            exchange.wait(ex_in, ex_out, sems)
            return
        first = functools.reduce(jnp.logical_and, [pl.program_id(a) == 0 for a in range(len(grid))])
        last = functools.reduce(jnp.logical_and, [pl.program_id(a) == grid[a] - 1 for a in range(len(grid))])

        @pl.when(first)
        def _():
            exchange.start(ex_in, ex_out, sems)

        body(*ins, *outs, *scratch)

        @pl.when(last)
        def _():
            exchange.wait(ex_in, ex_out, sems)

    any_spec = pl.BlockSpec(memory_space=pl.ANY)
    outs = pl.pallas_call(
        wrapped, name=name, grid=grid,
        in_specs=list(in_specs) + [any_spec] * n_ex, out_specs=list(out_specs) + [any_spec] * n_ex,
        out_shape=list(out_shape) + exchange.out_shape,
        scratch_shapes=list(scratch_shapes) + exchange.scratch_shapes, compiler_params=params,
    )(*args, *exchange.arrays)
    return list(outs[:n_out]), list(outs[n_out:])


def _exchange(arrays, scatter, name, relay=False):
    return _call(lambda: None, name=name, args=(), in_specs=(), out_specs=(), out_shape=(),
                 exchange=_Exchange(arrays, scatter, relay))[1]


def _first_exchange(c_pad, shards, w, b):
    rows, d = c_pad.shape
    cols = w.shape[1]
    ex_c = _Exchange([c_pad], False)
    ex_w = _Exchange(shards, False, relay=True)
    ex_m = _Exchange([jax.ShapeDtypeStruct((N_DEV * rows, cols), F32)], False)
    n_w = ex_w.n

    def body(*refs):
        c_ref, w_refs, wa_ref, b_ref = refs[0], refs[1:1 + n_w], refs[1 + n_w], refs[2 + n_w]
        outs = refs[3 + n_w:]
        cg_ref, wg_refs, mg_ref = outs[0], outs[1:1 + n_w], outs[1 + n_w]
        scratch = outs[2 + n_w:]
        sems_c, sems_w, sems_m, c_vm, m_vm = scratch[0:3], scratch[3:6], scratch[6:9], scratch[9], scratch[10]
        ex_w.start(w_refs, wg_refs, sems_w)
        ex_c.start([c_ref], [cg_ref], sems_c)
        ex_c.wait([c_ref], [cg_ref], sems_c)
        pltpu.sync_copy(cg_ref, c_vm)
        cv = c_vm[...].reshape(N_DEV * rows, d)
        s = (cv * _sigmoid(cv)).astype(BF16)
        m_vm[...] = _dot(s, wa_ref[...].astype(BF16)) + b_ref[...]
        ex_m.start([m_vm], [mg_ref], sems_m)
        ex_m.wait([m_vm], [mg_ref], sems_m)
        ex_w.wait(w_refs, wg_refs, sems_w)

    any_spec = pl.BlockSpec(memory_space=pl.ANY)
    vmem_spec = pl.BlockSpec(memory_space=pltpu.VMEM)
    outs = pl.pallas_call(
        body, name="first_exchange",
        in_specs=[any_spec] * (1 + n_w) + [vmem_spec, vmem_spec],
        out_specs=[any_spec] * (2 + n_w),
        out_shape=ex_c.out_shape + ex_w.out_shape + ex_m.out_shape,
        scratch_shapes=ex_c.scratch_shapes + ex_w.scratch_shapes + ex_m.scratch_shapes
        + [pltpu.VMEM((N_DEV, rows, d), F32), pltpu.VMEM((N_DEV * rows, cols), F32)],
        compiler_params=pltpu.CompilerParams(vmem_limit_bytes=VMEM_LIMIT_BYTES),
    )(c_pad, *shards, w, b)
    return outs[0], outs[1 + n_w], list(outs[1:1 + n_w])


def _ada_bwd(c_all, dmod_cols, dmod_all):
    def body(c_ref, dc_ref, da_ref, gw_ref, gb_ref):
        cv = c_ref[...]
        s = cv * _sigmoid(cv)
        gw_ref[...] = lax.dot_general(s, dc_ref[...], TN_DIMS, preferred_element_type=F32,
                                      precision=lax.Precision.HIGHEST)
        gb_ref[...] = jnp.sum(da_ref[...], axis=0, keepdims=True)

    return pl.pallas_call(
        body, name="ada_bwd",
        out_shape=(jax.ShapeDtypeStruct((c_all.shape[1], dmod_cols.shape[1]), F32),
                   jax.ShapeDtypeStruct((1, dmod_all.shape[1]), F32)),
        compiler_params=pltpu.CompilerParams(vmem_limit_bytes=VMEM_LIMIT_BYTES),
    )(c_all, dmod_cols, dmod_all)


def _ffn_fwd(x, mod, g, wg, wu, wd, k, tm, exchange=None):
    t, d = x.shape
    ns, _, fs = wg.shape
    nt = t // tm
    tpb = nt // mod.shape[0]
    rows = tm // FFN_CHUNKS

    def body(x_ref, mod_ref, g_ref, wg_ref, wu_ref, wd_ref, xo_ref, f_ref, gg_ref, uu_ref, h_sc, acc):
        j = pl.program_id(1)

        @pl.when(j == 0)
        def _():
            h_sc[...] = _modulate(x_ref[...], g_ref[...], mod_ref, k).astype(BF16)
            acc[...] = jnp.zeros_like(acc)

        chunks = [pl.ds(c * rows, rows) for c in range(FFN_CHUNKS)]
        wg, wu, wd = wg_ref[...], wu_ref[...], wd_ref[...]
        gates, ups = [], []
        for rs in chunks:
            h = h_sc[rs, :]
            gates.append(_dot(h, wg))
            ups.append(_dot(h, wu))
        acts = [(g * _sigmoid(g) * u).astype(BF16) for g, u in zip(gates, ups)]
        for rs, g, u in zip(chunks, gates, ups):
            gg_ref[rs, :] = g.astype(BF16)
            uu_ref[rs, :] = u.astype(BF16)
        downs = [_dot(a, wd) for a in acts]
        for rs, dn in zip(chunks, downs):
            acc[rs, :] += dn

        @pl.when(j == ns - 1)
        def _():
            f = acc[...]
            f_ref[...] = f.astype(BF16)
            xo_ref[...] = x_ref[...] + 0.5 * mod_ref[3 * k + 2:3 * k + 3, :] * f

    tok = pl.BlockSpec((tm, d), lambda i, j: (i, 0))
    hid = pl.BlockSpec((None, tm, fs), lambda i, j: (j, i, 0))
    return _call(
        body, name=f"ffn_fwd{k}", grid=(nt, ns), args=(x, mod, g, wg, wu, wd),
        in_specs=[_once(tok),
                  pl.BlockSpec((None, N_MOD, d), lambda i, j: (i // tpb, 0, 0)),
                  pl.BlockSpec((1, d), lambda i, j: (0, 0)),
                  pl.BlockSpec((None, d, fs), lambda i, j: (j, 0, 0)),
                  pl.BlockSpec((None, d, fs), lambda i, j: (j, 0, 0)),
                  pl.BlockSpec((None, fs, d), lambda i, j: (j, 0, 0))],
        out_specs=[tok, tok, hid, hid],
        out_shape=[jax.ShapeDtypeStruct((t, d), F32), jax.ShapeDtypeStruct((t, d), BF16),
                   jax.ShapeDtypeStruct((ns, t, fs), BF16), jax.ShapeDtypeStruct((ns, t, fs), BF16)],
        scratch_shapes=[pltpu.VMEM((tm, d), BF16), pltpu.VMEM((tm, d), F32)],
        params=_params("arbitrary", "arbitrary"), exchange=exchange)


def _ffn_bwd(dxo, x, f, mod, g, gate, up, wg, wu, wd, k, tm, exchange=None):
    t, d = x.shape
    ns, _, fs = wg.shape
    nt = t // tm
    nb = mod.shape[0]
    tpb = nt // nb
    rows = tm // FFN_CHUNKS

    def body(dxo_ref, x_ref, f_ref, mod_ref, g_ref, gg_ref, uu_ref, wg_ref, wu_ref, wd_ref,
             dx_ref, dgg_ref, duu_ref, act_ref, h_ref, df_ref, dmod_ref, dg_ref, acc):
        i, j = pl.program_id(0), pl.program_id(1)

        @pl.when(j == 0)
        def _():
            df = 0.5 * mod_ref[3 * k + 2:3 * k + 3, :] * dxo_ref[...]
            df_ref[...] = df.astype(BF16)
            h_ref[...] = _modulate(x_ref[...], g_ref[...], mod_ref, k).astype(BF16)
            acc[...] = jnp.zeros_like(acc)

        chunks = [pl.ds(c * rows, rows) for c in range(FFN_CHUNKS)]
        wg, wu, wd = wg_ref[...], wu_ref[...], wd_ref[...]
        dacts = [_dot_nt(df_ref[rs, :], wd) for rs in chunks]
        dgates, dups = [], []
        for rs, dact in zip(chunks, dacts):
            gv, uv = gg_ref[rs, :].astype(F32), uu_ref[rs, :].astype(F32)
            sig = _sigmoid(gv)
            s = gv * sig
            act_ref[rs, :] = (s * uv).astype(BF16)
            dups.append((dact * s).astype(BF16))
            dgates.append((dact * uv * (sig * (1.0 + gv * (1.0 - sig)))).astype(BF16))
        dhs = [_dot_nt(dg, wg) + _dot_nt(du, wu) for dg, du in zip(dgates, dups)]
        for rs, dg, du, dh in zip(chunks, dgates, dups, dhs):
            dgg_ref[rs, :] = dg
            duu_ref[rs, :] = du
            acc[rs, :] += dh

        @pl.when(j == ns - 1)
        def _():
            dx, dshift, dscale, dg = _modulate_bwd(acc[...], x_ref[...], g_ref[...], mod_ref, k)
            dxo_v = dxo_ref[...]
            dx_ref[...] = dxo_v + dx
            dgt = jnp.sum(0.5 * f_ref[...].astype(F32) * dxo_v, axis=0, keepdims=True)

            @pl.when(i % tpb == 0)
            def _():
                dmod_ref[...] = jnp.zeros_like(dmod_ref)

            @pl.when(i == 0)
            def _():
                dg_ref[...] = jnp.zeros_like(dg_ref)

            dmod_ref[0:1, :] += dshift
            dmod_ref[1:2, :] += dscale
            dmod_ref[2:3, :] += dgt
            dg_ref[0:1, :] += dg

    tok = pl.BlockSpec((tm, d), lambda i, j: (i, 0))
    hid = pl.BlockSpec((None, tm, fs), lambda i, j: (j, i, 0))
    return _call(
        body, name=f"ffn_bwd{k}", grid=(nt, ns), args=(dxo, x, f, mod, g, gate, up, wg, wu, wd),
        in_specs=[_once(tok), _once(tok), _once(tok),
                  pl.BlockSpec((None, N_MOD, d), lambda i, j: (i // tpb, 0, 0)),
                  pl.BlockSpec((1, d), lambda i, j: (0, 0)),
                  hid, hid,
                  pl.BlockSpec((None, d, fs), lambda i, j: (j, 0, 0)),
                  pl.BlockSpec((None, d, fs), lambda i, j: (j, 0, 0)),
                  pl.BlockSpec((None, fs, d), lambda i, j: (j, 0, 0))],
        out_specs=[tok, hid, hid, hid, tok, tok,
                   pl.BlockSpec((None, 8, d), lambda i, j: (i // tpb, 0, 0)),
                   pl.BlockSpec((8, d), lambda i, j: (0, 0))],
        out_shape=[jax.ShapeDtypeStruct((t, d), F32),
                   jax.ShapeDtypeStruct((ns, t, fs), BF16), jax.ShapeDtypeStruct((ns, t, fs), BF16),
                   jax.ShapeDtypeStruct((ns, t, fs), BF16),
                   jax.ShapeDtypeStruct((t, d), BF16), jax.ShapeDtypeStruct((t, d), BF16),
                   jax.ShapeDtypeStruct((nb, 8, d), F32), jax.ShapeDtypeStruct((8, d), F32)],
        scratch_shapes=[pltpu.VMEM((tm, d), F32)],
        params=_params("arbitrary", "arbitrary"), exchange=exchange)


def _mm_tn(a, b, a_spec, b_spec, out_shape, n_tiles, name, exchange=None):
    n_out = out_shape[0]
    block = tuple(out_shape[1:])
    last = n_tiles - 1
    flip = block[0] > block[1]
    if flip:
        block = block[::-1]

    def body(a_ref, b_ref, o_ref, acc):
        i, j = pl.program_id(0), pl.program_id(1)
        prod = _dot_tn(b_ref[...], a_ref[...]) if flip else _dot_tn(a_ref[...], b_ref[...])

        @pl.when(i == 0)
        def _():
            acc[j] = prod

        @pl.when(i > 0)
        def _():
            acc[j] += prod

        @pl.when(i == last)
        def _():
            total = acc[j]
            o_ref[j] = (total.T if flip else total).astype(o_ref.dtype)

    outs, sent = _call(
        body, name=name, grid=(n_tiles, n_out), args=(a, b), in_specs=[a_spec, b_spec],
        out_specs=[pl.BlockSpec(out_shape, lambda i, j: (0,) * len(out_shape))],
        out_shape=[jax.ShapeDtypeStruct(out_shape, BF16)],
        scratch_shapes=[pltpu.VMEM((n_out,) + block, F32)],
        params=_params("arbitrary", "arbitrary"), exchange=exchange)
    return (outs[0], sent) if exchange is not None else outs[0]


def _ffn_weight_grads(h, dgate, dup, act, df, tm, tag, stream=False):
    t, d = h.shape
    ns, _, fs = dgate.shape
    nt = t // tm
    tok = pl.BlockSpec((tm, d), lambda i, j: (i, 0))
    hid = pl.BlockSpec((None, tm, fs), lambda i, j: (j, i, 0))
    gwg = _mm_tn(h, dgate, tok, hid, (ns, d, fs), nt, f"grad_wg{tag}")
    if not stream:
        gwu = _mm_tn(h, dup, tok, hid, (ns, d, fs), nt, f"grad_wu{tag}")
        gwd = _mm_tn(act, df, hid, tok, (ns, fs, d), nt, f"grad_wd{tag}")
        return gwg, gwu, gwd
    gwu, sent_g = _mm_tn(h, dup, tok, hid, (ns, d, fs), nt, f"grad_wu{tag}", _Exchange([gwg], True))
    gwd, sent_u = _mm_tn(act, df, hid, tok, (ns, fs, d), nt, f"grad_wd{tag}", _Exchange([gwu], True))
    return sent_g[0], sent_u[0], gwd


def _stage_shape(rows, cols):
    return pltpu.VMEM((cols // LANES, rows, LANES), F32)


def _stage(value, stage_ref):
    for k in range(stage_ref.shape[0]):
        stage_ref[k] = value[:, k * LANES:(k + 1) * LANES]


def _to_residue_rows(stage_ref, dst_ref, dil):
    rows = stage_ref.shape[1] // dil
    for r in range(dil):
        for k in range(stage_ref.shape[0]):
            dst_ref[r, :, k * LANES:(k + 1) * LANES] = (
                stage_ref.at[k][pl.ds(r, rows, stride=dil), :].astype(dst_ref.dtype))


def _from_residue_rows(src_ref, stage_ref, dil):
    rows = stage_ref.shape[1] // dil
    chunks = range(stage_ref.shape[0])
    for r in range(dil):
        for k in chunks:
            stage_ref.at[k][pl.ds(r, rows, stride=dil), :] = src_ref[r, :, k * LANES:(k + 1) * LANES].astype(F32)
    return jnp.concatenate([stage_ref[k] for k in chunks], axis=1)


def _residue_shape(nb, seq, width, dil, dtype):
    return jax.ShapeDtypeStruct((nb, dil, seq // dil, width), dtype)


def _residue_spec(tm, tpb, cols, dil, col_block):
    return pl.BlockSpec((None, dil, tm // dil, cols),
                        lambda i, *rest: (i // tpb, 0, i % tpb, col_block(i, *rest)))


def _qkv_fwd(x, mod, g, win, tm):
    t, d = x.shape
    ns, _, cs = win.shape
    nt = t // tm
    nb = mod.shape[0]
    tpb = nt // nb
    seq = t // nb
    half = ns // 2
    n_res = len(DILATIONS) - 1

    def body(x_ref, mod_ref, g_ref, w_ref, sb_ref, dil_ref, *rest):
        res_refs, h_ref, sc = rest[:n_res], rest[n_res], rest[n_res + 1]
        j = pl.program_id(1)

        @pl.when(j == 0)
        def _():
            h_ref[...] = _modulate(x_ref[...], g_ref[...], mod_ref, 1).astype(BF16)

        res = _dot(h_ref[...], w_ref[...])

        @pl.when(j < half)
        def _():
            sb_ref[...] = res.astype(BF16)

        @pl.when(j >= half)
        def _():
            dil_ref[...] = res.astype(BF16)
            _stage(res, sc)
            for ref, dil in zip(res_refs, DILATIONS[1:]):
                _to_residue_rows(sc, ref, dil)

    def dil_col(i, j):
        return jnp.maximum(j - half, 0)

    tok = pl.BlockSpec((tm, d), lambda i, j: (i, 0))
    wide = jax.ShapeDtypeStruct((t, half * cs), BF16)
    outs = pl.pallas_call(
        body, name="qkv_fwd", grid=(nt, ns),
        in_specs=[_once(tok),
                  pl.BlockSpec((None, N_MOD, d), lambda i, j: (i // tpb, 0, 0)),
                  pl.BlockSpec((1, d), lambda i, j: (0, 0)),
                  pl.BlockSpec((None, d, cs), lambda i, j: (j, 0, 0))],
        out_specs=[pl.BlockSpec((tm, cs), lambda i, j: (i, jnp.minimum(j, half - 1))),
                   pl.BlockSpec((tm, cs), lambda i, j: (i, dil_col(i, j)))]
        + [_residue_spec(tm, tpb, cs, dil, dil_col) for dil in DILATIONS[1:]] + [tok],
        out_shape=[wide, wide] + [_residue_shape(nb, seq, half * cs, dil, BF16) for dil in DILATIONS[1:]]
        + [jax.ShapeDtypeStruct((t, d), BF16)],
        scratch_shapes=[_stage_shape(tm, cs)],
        compiler_params=_params("arbitrary", "arbitrary"),
    )(x, mod, g, win)
    qkv_dil = [outs[1]] + [a.reshape(t, half * cs) for a in outs[2:2 + n_res]]
    return outs[0], qkv_dil, outs[-1]


def _qkv_bwd(dqkv, dxo, x, mod, g, win, tm):
    t, d = x.shape
    ns, _, cs = win.shape
    nt = t // tm
    nb = mod.shape[0]
    tpb = nt // nb

    def body(dq_ref, dxo_ref, x_ref, mod_ref, g_ref, w_ref, dx_ref, dmod_ref, dg_ref, acc):
        i, j = pl.program_id(0), pl.program_id(1)

        @pl.when(j == 0)
        def _():
            acc[...] = jnp.zeros_like(acc)

        acc[...] += _dot_nt(dq_ref[...], w_ref[...])

        @pl.when(j == ns - 1)
        def _():
            dx, dshift, dscale, dg = _modulate_bwd(acc[...], x_ref[...], g_ref[...], mod_ref, 1)
            dx_ref[...] = dxo_ref[...] + dx

            @pl.when(i % tpb == 0)
            def _():
                dmod_ref[...] = jnp.zeros_like(dmod_ref)

            @pl.when(i == 0)
            def _():
                dg_ref[...] = jnp.zeros_like(dg_ref)

            dmod_ref[0:1, :] += dshift
            dmod_ref[1:2, :] += dscale
            dg_ref[0:1, :] += dg

    tok = pl.BlockSpec((tm, d), lambda i, j: (i, 0))
    return pl.pallas_call(
        body, name="qkv_bwd", grid=(nt, ns),
        in_specs=[pl.BlockSpec((tm, cs), lambda i, j: (i, j)), _once(tok), _once(tok),
                  pl.BlockSpec((None, N_MOD, d), lambda i, j: (i // tpb, 0, 0)),
                  pl.BlockSpec((1, d), lambda i, j: (0, 0)),
                  pl.BlockSpec((None, d, cs), lambda i, j: (j, 0, 0))],
        out_specs=[tok,
                   pl.BlockSpec((None, 8, d), lambda i, j: (i // tpb, 0, 0)),
                   pl.BlockSpec((8, d), lambda i, j: (0, 0))],
        out_shape=[jax.ShapeDtypeStruct((t, d), F32),
                   jax.ShapeDtypeStruct((nb, 8, d), F32), jax.ShapeDtypeStruct((8, d), F32)],
        scratch_shapes=[pltpu.VMEM((tm, d), F32)],
        compiler_params=_params("arbitrary", "arbitrary"),
    )(dqkv, dxo, x, mod, g, win)


def _heads(a):
    return [a[:, h * HEAD_DIM:(h + 1) * HEAD_DIM] for h in range(a.shape[1] // HEAD_DIM)]


def _own_lanes():
    lane = lax.broadcasted_iota(jnp.int32, (1, LANES), 1)
    return [lane < HEAD_DIM, lane >= HEAD_DIM]


def _pair_tiles(a):
    return [a[:, (h // 2) * LANES:(h // 2 + 1) * LANES] for h in range(a.shape[1] // HEAD_DIM)]


def _own_tiles(a, own):
    return [jnp.where(own[h % 2], tile, jnp.zeros_like(tile)) for h, tile in enumerate(_pair_tiles(a))]


def _merge_tiles(per_head, own):
    return jnp.concatenate([jnp.where(own[0], per_head[h], per_head[h + 1])
                            for h in range(0, len(per_head), 2)], axis=1)


def _scaled(q):
    return (q.astype(F32) * (HEAD_DIM ** -0.5)).astype(BF16)


def _sb_logits(qh, kh, tri, causal):
    zs = [_dot_nt(q, k) for q, k in zip(qh, kh)]
    es = [jnp.exp(-jnp.abs(z)) for z in zs]
    log_nots = [-(jnp.maximum(z, 0.0) + jnp.log(1.0 + e)) for z, e in zip(zs, es)]
    if causal is not None:
        log_nots = [jnp.where(causal, ln, 0.0) for ln in log_nots]
    return zs, es, [_split_dot(ln, tri) for ln in log_nots]


def _sb_masks():
    rows = lax.broadcasted_iota(jnp.int32, (SB_BLOCK, SB_BLOCK), 0)
    cols = lax.broadcasted_iota(jnp.int32, (SB_BLOCK, SB_BLOCK), 1)
    return (rows >= cols).astype(BF16), (rows <= cols).astype(BF16), cols < rows


def _sb_fwd(qkv, nb, seq, exchange=None):
    t = qkv.shape[0]
    n_pairs = (qkv.shape[1] // 3) // SB_WIDTH
    tb = SB_BLOCK
    n_blk = seq // tb

    def body(q_ref, k_ref, v_ref, o_ref, c_ref):
        tri, _, causal = _sb_masks()
        own = _own_lanes()

        def key_block(qh, kj, carry, mask):
            ks = pl.multiple_of(kj * tb, tb)
            kh, vh = _pair_tiles(k_ref[pl.ds(ks, tb), :]), _pair_tiles(v_ref[pl.ds(ks, tb), :])
            zs, _, suffixes = _sb_logits(qh, kh, tri, mask)
            ws = [jnp.exp(z + suffix + cr[1]) for z, suffix, cr in zip(zs, suffixes, carry)]
            if mask is not None:
                ws = [jnp.where(mask, w, 0.0) for w in ws]
            pv = [_dot(w.astype(BF16), v) for w, v in zip(ws, vh)]
            return tuple((cr[0] + p, cr[1] + suffix[:, 0:1]) for cr, p, suffix in zip(carry, pv, suffixes))

        def query_block(qi, _):
            qs = pl.multiple_of(qi * tb, tb)
            qh = _own_tiles(_scaled(q_ref[pl.ds(qs, tb), :]), own)
            zero = (jnp.zeros((tb, LANES), F32), jnp.zeros((tb, 1), F32))
            carry = key_block(qh, qi, (zero,) * SB_HEADS, causal)
            carry = lax.fori_loop(0, qi, lambda it, cr: key_block(qh, qi - 1 - it, cr, None), carry)
            o_ref[pl.ds(qs, tb), :] = _merge_tiles([cr[0] for cr in carry], own)
            c_ref[pl.ds(qs, tb), :] = _merge_tiles([jnp.broadcast_to(cr[1], (tb, LANES)) for cr in carry], own)
            return 0

        lax.fori_loop(0, n_blk, query_block, 0)

    def spec(offset):
        return pl.BlockSpec((seq, SB_WIDTH), lambda b, p: (b, offset + p))

    out = jax.ShapeDtypeStruct((t, n_pairs * SB_WIDTH), F32)
    return _call(
        body, name="sb_fwd", grid=(nb, n_pairs), args=(qkv, qkv, qkv),
        in_specs=[spec(0), spec(n_pairs), spec(2 * n_pairs)],
        out_specs=[spec(0), spec(0)], out_shape=[out, out],
        params=_params("arbitrary", "arbitrary"), exchange=exchange)


def _sb_bwd(qkv, do, csum, nb, seq, exchange=None):
    t = qkv.shape[0]
    n_pairs = (qkv.shape[1] // 3) // SB_WIDTH
    tb = SB_BLOCK
    n_blk = seq // tb
    scale = HEAD_DIM ** -0.5

    def body(q_ref, k_ref, v_ref, do_ref, c_ref, dq_ref, dk_ref, dv_ref, dkt_acc, dvt_acc):
        tri, tri_prefix, causal = _sb_masks()
        own = _own_lanes()
        dkt_acc[...] = jnp.zeros_like(dkt_acc)
        dvt_acc[...] = jnp.zeros_like(dvt_acc)

        def key_block(qh, qth, doh, doth, ch, kj, carry, mask):
            ks = pl.multiple_of(kj * tb, tb)
            kh, vh = _pair_tiles(k_ref[pl.ds(ks, tb), :]), _pair_tiles(v_ref[pl.ds(ks, tb), :])
            heads = range(SB_HEADS)
            zs, es, suffixes = _sb_logits(qh, kh, tri, mask)
            dws = [_dot_nt(doh[h], vh[h]) for h in heads]
            lefts = [carry[h][1] + suffixes[h][:, 0:1] for h in heads]
            ws = [jnp.exp(zs[h] + suffixes[h] + (ch[h] - lefts[h])) for h in heads]
            if mask is not None:
                ws = [jnp.where(mask, w, 0.0) for w in ws]
            dlws = [ws[h] * dws[h] for h in heads]
            dprefixes = [_split_dot(dlw, tri_prefix) for dlw in dlws]
            dvts = [_dot(doth[h], ws[h].astype(BF16)) for h in heads]
            dzbs = []
            for h in heads:
                sig = jnp.where(zs[h] >= 0.0, 1.0, es[h]) * pl.reciprocal(1.0 + es[h], approx=True)
                dz = dlws[h] - sig * (carry[h][2] + dprefixes[h])
                if mask is not None:
                    dz = jnp.where(mask, dz, 0.0)
                dzbs.append(dz.astype(BF16))
            dkts = [_dot(qth[h], dzbs[h]) for h in heads]
            dqs = [_dot(dzbs[h], kh[h]) for h in heads]
            dkt_acc[:, pl.ds(ks, tb)] += jnp.concatenate([dkts[h] + dkts[h + 1] for h in heads[::2]], axis=0)
            dvt_acc[:, pl.ds(ks, tb)] += jnp.concatenate([dvts[h] + dvts[h + 1] for h in heads[::2]], axis=0)
            return tuple((carry[h][0] + dqs[h], lefts[h], carry[h][2] + dprefixes[h][:, tb - 1:tb])
                         for h in heads)

        def query_block(qi, _):
            qs = pl.multiple_of(qi * tb, tb)
            qh = _own_tiles(_scaled(q_ref[pl.ds(qs, tb), :]), own)
            doh = _own_tiles(do_ref[pl.ds(qs, tb), :], own)
            qth = [a.astype(F32).T.astype(BF16) for a in qh]
            doth = [a.T.astype(BF16) for a in doh]
            doh = [a.astype(BF16) for a in doh]
            cv = c_ref[pl.ds(qs, tb), :]
            ch = [cv[:, h * HEAD_DIM:h * HEAD_DIM + 1] for h in range(SB_HEADS)]
            zero = (jnp.zeros((tb, LANES), F32), jnp.zeros((tb, 1), F32), jnp.zeros((tb, 1), F32))
            carry = lax.fori_loop(
                0, qi, lambda kj, cr: key_block(qh, qth, doh, doth, ch, kj, cr, None), (zero,) * SB_HEADS)
            carry = key_block(qh, qth, doh, doth, ch, qi, carry, causal)
            dq = _merge_tiles([cr[0] for cr in carry], own) * scale
            dq_ref[pl.ds(qs, tb), :] = dq.astype(BF16)
            return 0

        lax.fori_loop(0, n_blk, query_block, 0)
        dk_ref[...] = dkt_acc[...].T.astype(BF16)
        dv_ref[...] = dvt_acc[...].T.astype(BF16)

    def spec(offset):
        return pl.BlockSpec((seq, SB_WIDTH), lambda b, p: (b, offset + p))

    out = jax.ShapeDtypeStruct((t, n_pairs * SB_WIDTH), BF16)
    return _call(
        body, name="sb_bwd", grid=(nb, n_pairs), args=(qkv, qkv, qkv, do, csum),
        in_specs=[spec(0), spec(n_pairs), spec(2 * n_pairs), spec(0), spec(0)],
        out_specs=[spec(0), spec(0), spec(0)],
        out_shape=[out, out, out],
        scratch_shapes=[pltpu.VMEM((SB_WIDTH, seq), F32), pltpu.VMEM((SB_WIDTH, seq), F32)],
        params=_params("arbitrary", "arbitrary"), exchange=exchange)


def _dil_block_scores(qh, kph, kch, bias_ref, has_prev, band_prev, band_cur):
    scale = HEAD_DIM ** -0.5
    heads = range(len(qh))
    no_prev = jnp.where(has_prev, 0.0, NEG_INF)
    zps = [_dot_nt(qh[h], kph[h]) for h in heads]
    zcs = [_dot_nt(qh[h], kch[h]) for h in heads]
    zps = [jnp.where(band_prev, zps[h] * scale + bias_ref[h, :, 0:DIL_BLOCK], NEG_INF) + no_prev for h in heads]
    zcs = [jnp.where(band_cur, zcs[h] * scale + bias_ref[h, :, DIL_BLOCK:2 * DIL_BLOCK], NEG_INF) for h in heads]
    return zps, zcs


def _dil_bands():
    rows = lax.broadcasted_iota(jnp.int32, (DIL_BLOCK, DIL_BLOCK), 0)
    cols = lax.broadcasted_iota(jnp.int32, (DIL_BLOCK, DIL_BLOCK), 1)
    return cols >= rows, cols <= rows


def _dil_fwd(qkv, bias, nb, seq, dil, exchange=None):
    t, width = qkv.shape
    n_pairs = (width // 3) // DIL_WIDTH
    bq = DIL_BLOCK
    n_blk = seq // bq
    per_seq = n_blk // dil
    heads = range(DIL_HEADS)

    def body(q_ref, k_ref, v_ref, bias_ref, o_ref, lse_ref):
        band_prev, band_cur = _dil_bands()
        own = _own_lanes()

        def block(n, _):
            has_prev = (n & (per_seq - 1)) != 0
            qs = pl.multiple_of(n * bq, bq)
            ps = pl.multiple_of(jnp.maximum(n - 1, 0) * bq, bq)
            qh = _own_tiles(q_ref[pl.ds(qs, bq), :], own)
            kp, kc = _pair_tiles(k_ref[pl.ds(ps, bq), :]), _pair_tiles(k_ref[pl.ds(qs, bq), :])
            vp, vc = _pair_tiles(v_ref[pl.ds(ps, bq), :]), _pair_tiles(v_ref[pl.ds(qs, bq), :])
            zps, zcs = _dil_block_scores(qh, kp, kc, bias_ref, has_prev, band_prev, band_cur)
            ms = [jnp.maximum(jnp.max(zps[h], axis=1, keepdims=True), jnp.max(zcs[h], axis=1, keepdims=True))
                  for h in heads]
            eps = [jnp.exp(zps[h] - ms[h]) for h in heads]
            ecs = [jnp.exp(zcs[h] - ms[h]) for h in heads]
            pvs = [_dot(eps[h].astype(BF16), vp[h]) + _dot(ecs[h].astype(BF16), vc[h]) for h in heads]
            dens = [jnp.sum(eps[h], axis=1, keepdims=True) + jnp.sum(ecs[h], axis=1, keepdims=True) for h in heads]
            o_ref[pl.ds(qs, bq), :] = _merge_tiles([pvs[h] / dens[h] for h in heads], own)
            lse_ref[pl.ds(qs, bq), :] = _merge_tiles(
                [jnp.broadcast_to(ms[h] + jnp.log(dens[h]), (bq, LANES)) for h in heads], own)
            return 0

        lax.fori_loop(0, n_blk, block, 0)

    def spec(offset):
        return pl.BlockSpec((seq, DIL_WIDTH), lambda b, p: (b, offset + p))

    out = jax.ShapeDtypeStruct((t, n_pairs * DIL_WIDTH), F32)
    return _call(
        body, name=f"dil_fwd{dil}", grid=(nb, n_pairs), args=(qkv, qkv, qkv, bias),
        in_specs=[spec(0), spec(n_pairs), spec(2 * n_pairs),
                  pl.BlockSpec((DIL_HEADS, bq, 2 * bq), lambda b, p: (p, 0, 0))],
        out_specs=[spec(0), spec(0)], out_shape=[out, out],
        params=_params("arbitrary", "arbitrary"), exchange=exchange)


def _dil_bwd(qkv, bias, do, lse, delta, nb, seq, dil):
    t, width = qkv.shape
    n_pairs = (width // 3) // DIL_WIDTH
    bq = DIL_BLOCK
    n_blk = seq // bq
    per_seq = n_blk // dil
    scale = HEAD_DIM ** -0.5
    heads = range(DIL_HEADS)

    def body(q_ref, k_ref, v_ref, bias_ref, do_ref, lse_ref, dl_ref, dq_ref, dk_ref, dv_ref, db_ref,
             dk_acc, dv_acc):
        band_prev, band_cur = _dil_bands()
        own = _own_lanes()
        dk_acc[...] = jnp.zeros_like(dk_acc)
        dv_acc[...] = jnp.zeros_like(dv_acc)

        @pl.when(pl.program_id(1) == 0)
        def _():
            db_ref[...] = jnp.zeros_like(db_ref)

        def block(n, _):
            has_prev = (n & (per_seq - 1)) != 0
            qs = pl.multiple_of(n * bq, bq)
            ps = pl.multiple_of(jnp.maximum(n - 1, 0) * bq, bq)
            qh = _own_tiles(q_ref[pl.ds(qs, bq), :], own)
            kp, kc = _pair_tiles(k_ref[pl.ds(ps, bq), :]), _pair_tiles(k_ref[pl.ds(qs, bq), :])
            vp, vc = _pair_tiles(v_ref[pl.ds(ps, bq), :]), _pair_tiles(v_ref[pl.ds(qs, bq), :])
            doh = _own_tiles(do_ref[pl.ds(qs, bq), :].astype(BF16), own)
            lse_v, dl_v = lse_ref[pl.ds(qs, bq), :], dl_ref[pl.ds(qs, bq), :]
            zps, zcs = _dil_block_scores(qh, kp, kc, bias_ref, has_prev, band_prev, band_cur)
            dpp = [_dot_nt(doh[h], vp[h]) for h in heads]
            dpc = [_dot_nt(doh[h], vc[h]) for h in heads]
            lse_h = [lse_v[:, h * HEAD_DIM:h * HEAD_DIM + 1] for h in heads]
            dl_h = [dl_v[:, h * HEAD_DIM:h * HEAD_DIM + 1] for h in heads]
            pps = [jnp.exp(zps[h] - lse_h[h]) for h in heads]
            pcs = [jnp.exp(zcs[h] - lse_h[h]) for h in heads]
            dvp = [_dot_tn(pps[h].astype(BF16), doh[h]) for h in heads]
            dvc = [_dot_tn(pcs[h].astype(BF16), doh[h]) for h in heads]
            dzps = [pps[h] * (dpp[h] - dl_h[h]) for h in heads]
            dzcs = [pcs[h] * (dpc[h] - dl_h[h]) for h in heads]
            dzp_b = [(dzps[h] * scale).astype(BF16) for h in heads]
            dzc_b = [(dzcs[h] * scale).astype(BF16) for h in heads]
            dqs = [_dot(dzp_b[h], kp[h]) + _dot(dzc_b[h], kc[h]) for h in heads]
            dkp = [_dot_tn(dzp_b[h], qh[h]) for h in heads]
            dkc = [_dot_tn(dzc_b[h], qh[h]) for h in heads]
            for h in heads:
                db_ref[h, :, 0:bq] += dzps[h]
                db_ref[h, :, bq:2 * bq] += dzcs[h]
            def pair_sums(per_head):
                return jnp.concatenate([per_head[h] + per_head[h + 1] for h in heads[::2]], axis=1)

            dq_ref[pl.ds(qs, bq), :] = _merge_tiles(dqs, own).astype(BF16)
            dk_acc[pl.ds(ps, bq), :] += pair_sums(dkp)
            dk_acc[pl.ds(qs, bq), :] += pair_sums(dkc)
            dv_acc[pl.ds(ps, bq), :] += pair_sums(dvp)
            dv_acc[pl.ds(qs, bq), :] += pair_sums(dvc)
            return 0

        lax.fori_loop(0, n_blk, block, 0)
        dk_ref[...] = dk_acc[...].astype(BF16)
        dv_ref[...] = dv_acc[...].astype(BF16)

    def spec(offset):
        return pl.BlockSpec((seq, DIL_WIDTH), lambda p, b: (b, offset + p))

    bias_spec = pl.BlockSpec((DIL_HEADS, bq, 2 * bq), lambda p, b: (p, 0, 0))
    out = jax.ShapeDtypeStruct((t, n_pairs * DIL_WIDTH), BF16)
    return pl.pallas_call(
        body, name=f"dil_bwd{dil}", grid=(n_pairs, nb),
        in_specs=[spec(0), spec(n_pairs), spec(2 * n_pairs), bias_spec, spec(0), spec(0), spec(0)],
        out_specs=[spec(0), spec(0), spec(0), bias_spec],
        out_shape=[out, out, out, jax.ShapeDtypeStruct(bias.shape, F32)],
        scratch_shapes=[pltpu.VMEM((seq, DIL_WIDTH), F32), pltpu.VMEM((seq, DIL_WIDTH), F32)],
        compiler_params=_params("arbitrary", "arbitrary"),
    )(qkv, qkv, qkv, bias, do, lse, delta)


def _head_blocks(width):
    rows = lax.broadcasted_iota(jnp.int32, (width, width), 0) // HEAD_DIM
    cols = lax.broadcasted_iota(jnp.int32, (width, width), 1) // HEAD_DIM
    return (rows == cols).astype(BF16)


def _head_mean(v, gmat):
    return _split_dot(v, gmat) * (1.0 / HEAD_DIM)


def _residue_views(arrays, nb, seq):
    return [a if dil == 1 else a.reshape(nb, dil, seq // dil, a.shape[1]) for a, dil in zip(arrays, DILATIONS)]


def _mix_out_fwd(osb, ocs, lses, gsb, gdil, wout, x, mod, tm):
    t, d = x.shape
    ds = osb.shape[1]
    nt = t // tm
    nb = mod.shape[0]
    tpb = nt // nb
    seq = t // nb
    n_cfg = len(DILATIONS)

    def body(osb_ref, *refs):
        oc_refs, lse_refs = refs[:n_cfg], refs[n_cfg:2 * n_cfg]
        gsb_ref, gdil_ref, w_ref, x_ref, mod_ref = refs[2 * n_cfg:2 * n_cfg + 5]
        xo_ref, on_ref, m_ref, odil_ref = refs[2 * n_cfg + 5:2 * n_cfg + 9]
        ld_refs = refs[2 * n_cfg + 9:3 * n_cfg + 9]
        stages, sc = refs[3 * n_cfg + 9:]
        ocv, lsev = [oc_refs[0][...]], [lse_refs[0][...]]
        for i, dil in enumerate(DILATIONS[1:]):
            ocv.append(_from_residue_rows(oc_refs[i + 1], stages.at[2 * i], dil))
            lsev.append(_from_residue_rows(lse_refs[i + 1], stages.at[2 * i + 1], dil))
        top = functools.reduce(jnp.maximum, lsev)
        total = top + jnp.log(sum(jnp.exp(l - top) for l in lsev))
        odil = sum(jnp.exp(l - total) * o for o, l in zip(ocv, lsev))
        odil_ref[...] = odil
        ld_refs[0][...] = total
        _stage(total, sc)
        for ref, dil in zip(ld_refs[1:], DILATIONS[1:]):
            _to_residue_rows(sc, ref, dil)
        gm = _head_blocks(ds)
        parts = []
        for o, g_ref in ((osb_ref[...], gsb_ref), (odil, gdil_ref)):
            parts.append(o * lax.rsqrt(_head_mean(o * o, gm) + EPS) * g_ref[...])
        on = jnp.concatenate(parts, axis=1).astype(BF16)
        on_ref[...] = on
        m = _dot(on, w_ref[...])
        m_ref[...] = m
        xo_ref[...] = x_ref[...] + mod_ref[5:6, :] * m

    tok = pl.BlockSpec((tm, d), lambda i: (i, 0))
    hd = pl.BlockSpec((tm, ds), lambda i: (i, 0))
    res = [hd] + [_residue_spec(tm, tpb, ds, dil, lambda i: 0) for dil in DILATIONS[1:]]
    res_shape = [jax.ShapeDtypeStruct((t, ds), F32)] + [_residue_shape(nb, seq, ds, dil, F32) for dil in DILATIONS[1:]]
    gain = pl.BlockSpec((1, ds), lambda i: (0, 0))
    outs = pl.pallas_call(
        body, name="mix_out_fwd", grid=(nt,),
        in_specs=[hd] + res + res + [gain, gain,
                  pl.BlockSpec(wout.shape, lambda i: (0, 0)),
                  tok, pl.BlockSpec((None, N_MOD, d), lambda i: (i // tpb, 0, 0))],
        out_specs=[tok, pl.BlockSpec((tm, 2 * ds), lambda i: (i, 0)), tok, hd] + res,
        out_shape=[jax.ShapeDtypeStruct((t, d), F32), jax.ShapeDtypeStruct((t, 2 * ds), BF16),
                   jax.ShapeDtypeStruct((t, d), F32), jax.ShapeDtypeStruct((t, ds), F32)] + res_shape,
        scratch_shapes=[pltpu.VMEM((2 * (n_cfg - 1), ds // LANES, tm, LANES), F32), _stage_shape(tm, ds)],
        compiler_params=_params("arbitrary"),
    )(osb, *_residue_views(ocs, nb, seq), *_residue_views(lses, nb, seq), gsb, gdil, wout, x, mod)
    return outs[0], outs[1], outs[2], outs[3], [a.reshape(t, ds) for a in outs[4:]]


def _mix_out_bwd(dxo, m, mod, wout, osb, odil, gsb, gdil, tm):
    t, d = dxo.shape
    ds = osb.shape[1]
    nt = t // tm
    nb = mod.shape[0]
    tpb = nt // nb
    seq = t // nb
    n_cfg = len(DILATIONS)

    def body(dxo_ref, m_ref, mod_ref, w_ref, osb_ref, odil_ref, gsb_ref, gdil_ref,
             dm_ref, dosb_ref, *rest):
        do_refs, dl_refs = rest[:n_cfg], rest[n_cfg:2 * n_cfg]
        dmod_ref, dg_ref, sc = rest[2 * n_cfg:]
        dodil_ref, dldil_ref = do_refs[0], dl_refs[0]
        i = pl.program_id(0)
        dxo_v = dxo_ref[...]
        dm = (mod_ref[5:6, :] * dxo_v).astype(BF16)
        dm_ref[...] = dm
        dgt = jnp.sum(m_ref[...] * dxo_v, axis=0, keepdims=True)
        don = _dot_nt(dm, w_ref[...])
        gm = _head_blocks(ds)

        @pl.when(i % tpb == 0)
        def _():
            dmod_ref[...] = jnp.zeros_like(dmod_ref)

        @pl.when(i == 0)
        def _():
            dg_ref[...] = jnp.zeros_like(dg_ref)

        dmod_ref[2:3, :] += dgt
        groups = ((osb_ref, gsb_ref, dosb_ref), (odil_ref, gdil_ref, dodil_ref))
        for k, (o_ref, g_ref, do_ref) in enumerate(groups):
            o = o_ref[...]
            dn_out = don[:, k * ds:(k + 1) * ds]
            r = lax.rsqrt(_head_mean(o * o, gm) + EPS)
            n = o * r
            dg_ref[0:1, k * ds:(k + 1) * ds] += jnp.sum(dn_out * n, axis=0, keepdims=True)
            dn = dn_out * g_ref[...]
            do = r * (dn - n * _head_mean(dn * n, gm))
            do_ref[...] = do
            if k == 1:
                delta = _head_mean(do * o, gm) * float(HEAD_DIM)
                dldil_ref[...] = delta
                for value, refs in ((do, do_refs), (delta, dl_refs)):
                    _stage(value, sc)
                    for ref, dil in zip(refs[1:], DILATIONS[1:]):
                        _to_residue_rows(sc, ref, dil)

    tok = pl.BlockSpec((tm, d), lambda i: (i, 0))
    hd = pl.BlockSpec((tm, ds), lambda i: (i, 0))
    res = [hd] + [_residue_spec(tm, tpb, ds, dil, lambda i: 0) for dil in DILATIONS[1:]]
    res_shape = [jax.ShapeDtypeStruct((t, ds), F32)] + [_residue_shape(nb, seq, ds, dil, F32) for dil in DILATIONS[1:]]
    gain = pl.BlockSpec((1, ds), lambda i: (0, 0))
    outs = pl.pallas_call(
        body, name="mix_out_bwd", grid=(nt,),
        in_specs=[tok, tok, pl.BlockSpec((None, N_MOD, d), lambda i: (i // tpb, 0, 0)),
                  pl.BlockSpec(wout.shape, lambda i: (0, 0)), hd, hd, gain, gain],
        out_specs=[tok, hd] + res + res
        + [pl.BlockSpec((None, 8, d), lambda i: (i // tpb, 0, 0)), pl.BlockSpec((8, 2 * ds), lambda i: (0, 0))],
        out_shape=[jax.ShapeDtypeStruct((t, d), BF16), jax.ShapeDtypeStruct((t, ds), F32)] + res_shape + res_shape
        + [jax.ShapeDtypeStruct((nb, 8, d), F32), jax.ShapeDtypeStruct((8, 2 * ds), F32)],
        scratch_shapes=[_stage_shape(tm, ds)],
        compiler_params=_params("arbitrary"),
    )(dxo, m, mod, wout, osb, odil, gsb, gdil)
    flat = [a.reshape(t, ds) for a in outs[2:2 + 2 * n_cfg]]
    return outs[0], outs[1], flat[:n_cfg], flat[n_cfg:], outs[-2], outs[-1]


def _merge_dqkv(sb_parts, dil_parts, nb, tm):
    t, ds = sb_parts[0].shape
    nt = t // tm
    tpb = nt // nb
    seq = t // nb
    n_cfg = len(DILATIONS)

    def body(*refs):
        sb_refs, dil_refs = refs[:3], refs[3:3 + 3 * n_cfg]
        o_ref, sc = refs[3 + 3 * n_cfg:]
        for k in range(3):
            o_ref[:, k * ds:(k + 1) * ds] = sb_refs[k][...]
            total = dil_refs[k * n_cfg][...].astype(F32)
            for i, dil in enumerate(DILATIONS[1:]):
                total = total + _from_residue_rows(dil_refs[k * n_cfg + i + 1], sc, dil)
            o_ref[:, (3 + k) * ds:(4 + k) * ds] = total.astype(BF16)

    hd = pl.BlockSpec((tm, ds), lambda i: (i, 0))
    res = [hd] + [_residue_spec(tm, tpb, ds, dil, lambda i: 0) for dil in DILATIONS[1:]]
    views = [v for parts in dil_parts for v in _residue_views(parts, nb, seq)]
    return pl.pallas_call(
        body, name="merge_dqkv", grid=(nt,),
        in_specs=[hd] * 3 + res * 3,
        out_specs=pl.BlockSpec((tm, 6 * ds), lambda i: (i, 0)),
        out_shape=jax.ShapeDtypeStruct((t, 6 * ds), BF16),
        scratch_shapes=[_stage_shape(tm, ds)],
        compiler_params=_params("arbitrary"),
    )(*sb_parts, *views)


def _loss_head(x, target, g, tm):
    t, d = x.shape

    def body(x_ref, t_ref, g_ref, dx_ref, acc_ref):
        @pl.when(pl.program_id(0) == 0)
        def _():
            acc_ref[...] = jnp.zeros_like(acc_ref)

        n, r = _norm(x_ref[...])
        gv = g_ref[...]
        err = n * gv - t_ref[...]
        dy = err * (1.0 / d)
        acc_ref[0:1, :] += jnp.sum(err * err, axis=0, keepdims=True)
        acc_ref[1:2, :] += jnp.sum(dy * n, axis=0, keepdims=True)
        dn = dy * gv
        dx_ref[...] = r * (dn - n * jnp.mean(dn * n, axis=-1, keepdims=True))

    tok = pl.BlockSpec((tm, d), lambda i: (i, 0))
    return pl.pallas_call(
        body, name="loss_head", grid=(t // tm,),
        in_specs=[tok, tok, pl.BlockSpec((1, d), lambda i: (0, 0))],
        out_specs=[tok, pl.BlockSpec((8, d), lambda i: (0, 0))],
        out_shape=[jax.ShapeDtypeStruct((t, d), F32), jax.ShapeDtypeStruct((8, d), F32)],
        compiler_params=_params("arbitrary"),
    )(x, target, g)


def _row_tile(rows):
    if rows <= 256:
        return rows
    for cand in range(256, 15, -16):
        if rows % cand == 0:
            return cand
    return rows


def _adamw(w, parts, m, v, name):
    rows, cols = w.shape
    n_parts = parts.shape[0]
    tr = _row_tile(rows)
    c1 = 1.0 / (1.0 - ADAM_B1 ** ADAM_STEP)
    c2 = 1.0 / (1.0 - ADAM_B2 ** ADAM_STEP)

    def body(w_ref, p_ref, m_ref, v_ref, g_ref, d_ref, nm_ref, nv_ref):
        g = p_ref[0].astype(F32)
        for i in range(1, n_parts):
            g = g + p_ref[i].astype(F32)
        nm = ADAM_B1 * m_ref[...] + (1.0 - ADAM_B1) * g
        nv = ADAM_B2 * v_ref[...] + (1.0 - ADAM_B2) * (g * g)
        g_ref[...] = g
        nm_ref[...] = nm
        nv_ref[...] = nv
        d_ref[...] = -ADAM_LR * ((nm * c1) / (jnp.sqrt(nv * c2) + ADAM_EPS) + ADAM_WD * w_ref[...])

    blk = pl.BlockSpec((tr, cols), lambda i: (i, 0))
    out = jax.ShapeDtypeStruct((rows, cols), F32)
    return pl.pallas_call(
        body, name=name, grid=(rows // tr,),
        in_specs=[blk, pl.BlockSpec((n_parts, tr, cols), lambda i: (0, i, 0)), blk, blk],
        out_specs=[blk, blk, blk, blk], out_shape=[out, out, out, out],
        compiler_params=_params("arbitrary"),
    )(w, parts, m, v)


def _t5_bucket(n):
    max_exact = N_BUCKETS // 2
    nf = np.maximum(n, 1).astype(np.float32)
    large = max_exact + (np.log(nf / max_exact) / math.log(MAX_DISTANCE / max_exact)
                         * (N_BUCKETS - max_exact)).astype(np.int32)
    large = np.minimum(large, N_BUCKETS - 1)
    return np.where(n < max_exact, n, large).astype(np.int32)


def _bucket_onehot():
    table = np.zeros((len(DILATIONS), 2 * DIL_BLOCK + 1, N_BUCKETS), np.float32)
    for i, dil in enumerate(DILATIONS):
        buckets = _t5_bucket(np.arange(DIL_BLOCK + 1) * dil)
        for m in range(DIL_BLOCK + 1):
            table[i, m, buckets[DIL_BLOCK - m]] = 1.0
    return table


def _bias_blocks(rel_bias):
    row = jnp.einsum("cmn,nh->chm", _bucket_onehot(), rel_bias, precision=lax.Precision.HIGHEST)
    n_cfg, n_heads, width = row.shape
    tiled = jnp.tile(row, (1, 1, DIL_BLOCK))[..., :DIL_BLOCK * (width - 1)]
    return tiled.reshape(n_cfg, n_heads, DIL_BLOCK, width - 1)


def _bias_blocks_bwd(dblocks):
    n_cfg, n_heads = dblocks.shape[:2]
    width = 2 * DIL_BLOCK + 1
    flat = dblocks.reshape(n_cfg, n_heads, DIL_BLOCK * (width - 1))
    flat = jnp.pad(flat, ((0, 0), (0, 0), (0, DIL_BLOCK)))
    drow = jnp.sum(flat.reshape(n_cfg, n_heads, DIL_BLOCK, width), axis=2)
    return jnp.einsum("chm,cmn->nh", drow, _bucket_onehot(), precision=lax.Precision.HIGHEST)


def _pad_to(a, axis, size):
    pad = [(0, 0)] * a.ndim
    pad[axis] = (0, size - a.shape[axis])
    return jnp.pad(a, pad)


def _lane_pad(n):
    return -(-n // LANES) * LANES


def _local_step(x, target, mod, gains, weights, rel_bias, tm, distributed):
    nb, seq, d = x.shape
    t = nb * seq
    g_ffn1, g_mix, g_sb, g_dil, g_ffn2, g_final = gains
    wg1, wu1, wd1 = weights[:3]
    x0 = x.reshape(t, d)
    ds = g_sb.shape[1]
    bias = _bias_blocks(rel_bias)

    def beside(arrays, scatter):
        return _Exchange(arrays, scatter) if distributed else None

    tp, tg = min(PROJ_TILE, seq), min(GRAD_TILE, t)

    (x1, f1, gate1, up1), got = _ffn_fwd(x0, mod, g_ffn1, wg1, wu1, wd1, 0, tp, beside(weights[3:5], False))
    win, wout = got if distributed else weights[3:5]
    wout2 = wout.reshape(-1, d)
    qkv, qkvd, h2 = _qkv_fwd(x1, mod, g_mix, win, tp)
    (osb, csb), got = _sb_fwd(qkv, nb, seq, beside(weights[5:7], False))
    wg2, wu2 = got if distributed else weights[5:7]
    ocs, lses = [], []
    for i, dil in enumerate(DILATIONS):
        (oc, lse), got = _dil_fwd(qkvd[i], bias[i], nb, seq, dil, beside(weights[7:8], False) if i == 0 else None)
        if i == 0:
            wd2 = got[0] if distributed else weights[7]
        ocs.append(oc)
        lses.append(lse)
    x2, on, mix, odil, ldil = _mix_out_fwd(osb, ocs, lses, g_sb, g_dil, wout2, x1, mod, tm)
    (x3, f3, gate3, up3), _ = _ffn_fwd(x2, mod, g_ffn2, wg2, wu2, wd2, 2, tp)
    dx3, head = _loss_head(x3, target.reshape(t, d), g_final, tm)
    loss_sum = 0.5 * jnp.sum(head[0]) / d
    dg_final = head[1:2]

    (dx2, dgate3, dup3, act3, h3, df3, dmod3, dg_ffn2), _ = _ffn_bwd(
        dx3, x2, f3, mod, g_ffn2, gate3, up3, wg2, wu2, wd2, 2, tp)
    gwg2, gwu2, gwd2 = _ffn_weight_grads(h3, dgate3, dup3, act3, df3, tg, 2)

    dm, dosb, dodil, dldil, dmod2b, dg_heads = _mix_out_bwd(
        dx2, mix, mod, wout2, osb, odil, g_sb, g_dil, tm)
    n_out = wout.shape[0]
    gwout = _mm_tn(on, dm,
                   pl.BlockSpec((tg, wout.shape[1]), lambda i, j: (i, j)),
                   pl.BlockSpec((tg, d), lambda i, j: (i, 0)),
                   wout.shape, t // tg, "grad_wout")

    (dq_sb, dk_sb, dv_sb), parts_late = _sb_bwd(qkv, dosb, csb, nb, seq,
                                                beside([gwout, gwg2, gwu2, gwd2], True))
    dil_grads = [_dil_bwd(qkvd[i], bias[i], dodil[i], ldil[i], dldil[i], nb, seq, dil)
                 for i, dil in enumerate(DILATIONS)]
    dqkv = _merge_dqkv([dq_sb, dk_sb, dv_sb], [[g[k] for g in dil_grads] for k in range(3)], nb, tm)
    drel = _bias_blocks_bwd(jnp.stack([g[3] for g in dil_grads]))

    dx1, dmod2a, dg_mix = _qkv_bwd(dqkv, dx2, x1, mod, g_mix, win, tp)
    n_in, _, cs = win.shape
    gwin = _mm_tn(h2, dqkv,
                  pl.BlockSpec((tg, d), lambda i, j: (i, 0)),
                  pl.BlockSpec((tg, cs), lambda i, j: (i, j)),
                  win.shape, t // tg, "grad_win")

    (dx0, dgate1, dup1, act1, h1, df1, dmod1, dg_ffn1), parts_mid = _ffn_bwd(
        dx1, x0, f1, mod, g_ffn1, gate1, up1, wg1, wu1, wd1, 0, tp, beside([gwin], True))
    gw1 = _ffn_weight_grads(h1, dgate1, dup1, act1, df1, tg, 0, stream=distributed)

    dmod = jnp.concatenate([dmod1[:, 0:3], dmod2a[:, 0:2], dmod2b[:, 2:3], dmod3[:, 0:3]], axis=1)
    wgrads = tuple(gw1) + (tuple(parts_mid + parts_late) if distributed else (gwin, gwout, gwg2, gwu2, gwd2))
    ggrads = (dg_ffn1[0:1], dg_mix[0:1], dg_heads[0:1], drel, dg_ffn2[0:1], dg_final)
    return loss_sum, dx0.reshape(nb, seq, d), wgrads, dmod, ggrads


def kernel(x, c, w_ada, b_ada, g_ffn1, w1_gate, w1_up, w1_down, g_mix, w_in, g_sb_out, g_dil_out, w_out, rel_bias, g_ffn2, w2_gate, w2_up, w2_down, g_final, loss_target, m_w_ada, m_b_ada, m_g_ffn1, m_w1_gate, m_w1_up, m_w1_down, m_g_mix, m_w_in, m_g_sb_out, m_g_dil_out, m_w_out, m_rel_bias, m_g_ffn2, m_w2_gate, m_w2_up, m_w2_down, m_g_final, v_w_ada, v_b_ada, v_g_ffn1, v_w1_gate, v_w1_up, v_w1_down, v_g_mix, v_w_in, v_g_sb_out, v_g_dil_out, v_w_out, v_rel_bias, v_g_ffn2, v_w2_gate, v_w2_up, v_w2_down, v_g_final):
    nb, seq, d = x.shape
    me = 4 * lax.axis_index("x") + 2 * lax.axis_index("y") + lax.axis_index("c")
    tm = min(TOKEN_TILE, seq)
    fs = w1_gate.shape[2]
    fs_pad = _lane_pad(fs)
    ada_cols = w_ada.shape[2]

    def col_shard(w):
        return _pad_to(w[0].astype(BF16), 1, fs_pad)

    def row_shard(w):
        return _pad_to(w[0].astype(BF16), 0, fs_pad)

    shards = [col_shard(w1_gate), col_shard(w1_up), row_shard(w1_down), w_in[0].astype(BF16),
              w_out[0].astype(BF16), col_shard(w2_gate), col_shard(w2_up), row_shard(w2_down)]
    b_cols = lax.dynamic_slice(b_ada, (0, me * ada_cols), (1, ada_cols))
    c_every, mod_all, first = _first_exchange(_pad_to(c, 0, 8), shards[:3], w_ada[0], b_cols)
    c_all = c_every[:, :nb].reshape(N_DEV * nb, d)
    weights = first + shards[3:]
    mod = lax.dynamic_slice(mod_all, (0, me * 8, 0), (N_DEV, nb, ada_cols))
    mod = mod.transpose(1, 0, 2).reshape(nb, N_MOD, d)

    n_sb = g_sb_out.shape[1] * g_sb_out.shape[2]
    gains = (g_ffn1, g_mix, g_sb_out.reshape(1, n_sb), g_dil_out.reshape(1, -1), g_ffn2,
             g_final.reshape(1, d))
    loss_sum, grad_x, parts, dmod, ggrads = _local_step(x, loss_target, mod, gains, weights, rel_bias, tm, True)
    loss = lax.psum(loss_sum, ("x", "y", "c"))

    dg_ffn1, dg_mix, dg_heads, drel, dg_ffn2, dg_final = ggrads
    width = max(d, dg_heads.shape[1], drel.size)
    small = jnp.concatenate(
        [_pad_to(a.reshape(1, -1), 1, width) for a in (dg_ffn1, dg_mix, dg_ffn2, dg_final, dg_heads, drel)]
        + [jnp.zeros((2, width), F32)], axis=0)
    dmod_pad = _pad_to(dmod.reshape(nb, N_MOD * d), 0, 8)
    last_part, dmod_all, small_all = _exchange(
        [parts[2], jnp.broadcast_to(dmod_pad, (N_DEV,) + dmod_pad.shape),
         jnp.broadcast_to(small, (N_DEV,) + small.shape)], True, "scatter_last")
    parts = parts[:2] + (last_part,) + parts[3:]
    dmod_all = dmod_all[:, :nb].reshape(N_DEV * nb, N_MOD * d)
    dmod_cols = lax.dynamic_slice(dmod_all, (0, me * ada_cols), (N_DEV * nb, ada_cols))
    gw_ada, gb_ada = _ada_bwd(c_all, dmod_cols, dmod_all)

    def small_part(row, size, shape):
        return small_all[:, row, :size].reshape((N_DEV,) + shape)

    n_rel = rel_bias.shape
    updates = {
        "w_ada": (w_ada[0], gw_ada[None], m_w_ada[0], v_w_ada[0]),
        "b_ada": (b_ada, gb_ada[None], m_b_ada, v_b_ada),
        "g_ffn1": (g_ffn1, small_part(0, d, (1, d)), m_g_ffn1, v_g_ffn1),
        "w1_gate": (w1_gate[0], parts[0][:, :, :fs], m_w1_gate[0], v_w1_gate[0]),
        "w1_up": (w1_up[0], parts[1][:, :, :fs], m_w1_up[0], v_w1_up[0]),
        "w1_down": (w1_down[0], parts[2][:, :fs, :], m_w1_down[0], v_w1_down[0]),
        "g_mix": (g_mix, small_part(1, d, (1, d)), m_g_mix, v_g_mix),
        "w_in": (w_in[0], parts[3], m_w_in[0], v_w_in[0]),
        "g_sb_out": (g_sb_out[0], small_all[:, 4, :n_sb].reshape((N_DEV,) + g_sb_out.shape[1:]),
                     m_g_sb_out[0], v_g_sb_out[0]),
        "g_dil_out": (g_dil_out[0], small_all[:, 4, n_sb:dg_heads.shape[1]].reshape((N_DEV,) + g_dil_out.shape[1:]),
                      m_g_dil_out[0], v_g_dil_out[0]),
        "w_out": (w_out[0], parts[4], m_w_out[0], v_w_out[0]),
        "rel_bias": (rel_bias, small_part(5, drel.size, n_rel), m_rel_bias, v_rel_bias),
        "g_ffn2": (g_ffn2, small_part(2, d, (1, d)), m_g_ffn2, v_g_ffn2),
        "w2_gate": (w2_gate[0], parts[5][:, :, :fs], m_w2_gate[0], v_w2_gate[0]),
        "w2_up": (w2_up[0], parts[6][:, :, :fs], m_w2_up[0], v_w2_up[0]),
        "w2_down": (w2_down[0], parts[7][:, :fs, :], m_w2_down[0], v_w2_down[0]),
        "g_final": (g_final.reshape(1, d), small_part(3, d, (1, d)), m_g_final.reshape(1, d), v_g_final.reshape(1, d)),
    }
    shapes = {"w_ada": w_ada.shape, "b_ada": b_ada.shape, "g_ffn1": g_ffn1.shape, "w1_gate": w1_gate.shape,
              "w1_up": w1_up.shape, "w1_down": w1_down.shape, "g_mix": g_mix.shape, "w_in": w_in.shape,
              "g_sb_out": g_sb_out.shape, "g_dil_out": g_dil_out.shape, "w_out": w_out.shape,
              "rel_bias": rel_bias.shape, "g_ffn2": g_ffn2.shape, "w2_gate": w2_gate.shape,
              "w2_up": w2_up.shape, "w2_down": w2_down.shape, "g_final": g_final.shape}
    grads, deltas, new_m, new_v = [], [], [], []
    for name, (w, p, m, v) in updates.items():
        g, dw, nm, nv = _adamw(w, p, m, v, f"adamw_{name}")
        grads.append(g.reshape(shapes[name]))
        deltas.append(dw.reshape(shapes[name]))
        new_m.append(nm.reshape(shapes[name]))
        new_v.append(nv.reshape(shapes[name]))
    return (loss, grad_x, *grads, *deltas, *new_m, *new_v)
```

```python
import functools
import math

import numpy as np
import jax
import jax.numpy as jnp
from jax import lax
from jax.experimental import pallas as pl
from jax.experimental.pallas import tpu as pltpu

F32 = jnp.float32
BF16 = jnp.bfloat16

EPS = 1e-6
NEG_INF = -1e30
HEAD_DIM = 64
LANES = 128
DIL_BLOCK = 128
DILATIONS = (1, 4, 16)
N_BUCKETS = 32
MAX_DISTANCE = 2048
N_MOD = 9
N_DEV = 8
SB_BLOCK = 256
SB_HEADS = 4
SB_WIDTH = SB_HEADS * HEAD_DIM
DIL_HEADS = 4
DIL_WIDTH = DIL_HEADS * HEAD_DIM
TOKEN_TILE = 512
PROJ_TILE = 1024
GRAD_TILE = 1024
FFN_CHUNKS = 2
VMEM_LIMIT_BYTES = 56 * 1024 * 1024

ADAM_LR = 0.001
ADAM_B1 = 0.9
ADAM_B2 = 0.999
ADAM_EPS = 1e-08
ADAM_WD = 0.01
ADAM_STEP = 10

NT_DIMS = (((1,), (1,)), ((), ()))
TN_DIMS = (((0,), (0,)), ((), ()))


def _params(*sem):
    return pltpu.CompilerParams(dimension_semantics=sem, vmem_limit_bytes=VMEM_LIMIT_BYTES)


def _once(spec):
    return pl.BlockSpec(spec.block_shape, spec.index_map, pipeline_mode=pl.Buffered(1))


def _dot(a, b):
    return jnp.dot(a, b, preferred_element_type=F32)


def _dot_nt(a, b):
    return lax.dot_general(a, b, NT_DIMS, preferred_element_type=F32)


def _dot_tn(a, b):
    return lax.dot_general(a, b, TN_DIMS, preferred_element_type=F32)


def _split_dot(a, b):
    hi = a.astype(BF16)
    lo = (a - hi.astype(F32)).astype(BF16)
    return _dot(hi, b) + _dot(lo, b)


def _sigmoid(z):
    return 1.0 / (1.0 + jnp.exp(-z))


def _norm(x):
    r = lax.rsqrt(jnp.mean(x * x, axis=-1, keepdims=True) + EPS)
    return x * r, r


def _modulate(x, g, mod_ref, k):
    n, _ = _norm(x)
    shift = mod_ref[3 * k:3 * k + 1, :]
    scale = mod_ref[3 * k + 1:3 * k + 2, :]
    return n * g * (1.0 + scale) + shift


def _modulate_bwd(dh, x, g, mod_ref, k):
    n, r = _norm(x)
    scale = mod_ref[3 * k + 1:3 * k + 2, :]
    dshift = jnp.sum(dh, axis=0, keepdims=True)
    dscale = jnp.sum(dh * n * g, axis=0, keepdims=True)
    dg = jnp.sum(dh * n * (1.0 + scale), axis=0, keepdims=True)
    dn = dh * g * (1.0 + scale)
    dx = r * (dn - n * jnp.mean(dn * n, axis=-1, keepdims=True))
    return dx, dshift, dscale, dg


class _Exchange:
    def __init__(self, arrays, scatter, relay=False):
        assert not (scatter and relay)
        self.arrays = list(arrays)
        self.scatter = scatter
        self.relay = relay
        self.n = len(self.arrays)
        self.out_shape = [
            jax.ShapeDtypeStruct((N_DEV,) + tuple(a.shape[1:] if scatter else a.shape), a.dtype)
            for a in self.arrays]
        n_remote = self.n * (N_DEV - 1)
        self.scratch_shapes = [pltpu.SemaphoreType.DMA((n_remote,)), pltpu.SemaphoreType.DMA((n_remote,)),
                               pltpu.SemaphoreType.DMA((self.n,))]

    def _copies(self, in_refs, out_refs, sems):
        send_sems, recv_sems, local_sems = sems
        x, y, c = lax.axis_index("x"), lax.axis_index("y"), lax.axis_index("c")
        me = 4 * x + 2 * y + c
        local, remote, relayed = [], {}, {}
        for a in range(self.n):
            src = in_refs[a].at[me] if self.scatter else in_refs[a]
            local.append(pltpu.make_async_copy(src, out_refs[a].at[me], local_sems.at[a]))
            for k in range(1, N_DEV):
                px = 1 - x if k & 4 else x
                py = 1 - y if k & 2 else y
                pc = 1 - c if k & 1 else c
                sem = a * (N_DEV - 1) + k - 1
                if self.relay and k & 1 and k > 1:
                    slot = 4 * px + 2 * py + c
                    relayed[a, k] = pltpu.make_async_remote_copy(
                        src_ref=out_refs[a].at[slot], dst_ref=out_refs[a].at[slot],
                        send_sem=send_sems.at[sem], recv_sem=recv_sems.at[sem],
                        device_id=(x, y, 1 - c), device_id_type=pl.DeviceIdType.MESH)
                    continue
                src = in_refs[a].at[4 * px + 2 * py + pc] if self.scatter else in_refs[a]
                remote[a, k] = pltpu.make_async_remote_copy(
                    src_ref=src, dst_ref=out_refs[a].at[me],
                    send_sem=send_sems.at[sem], recv_sem=recv_sems.at[sem],
                    device_id=(px, py, pc), device_id_type=pl.DeviceIdType.MESH)
        return local, remote, relayed

    def start(self, in_refs, out_refs, sems):
        local, remote, _ = self._copies(in_refs, out_refs, sems)
        for cp in local + list(remote.values()):
            cp.start()

    def wait(self, in_refs, out_refs, sems):
        local, remote, relayed = self._copies(in_refs, out_refs, sems)
        for (a, k), cp in relayed.items():
            remote[a, k - 1].wait_recv()
            cp.start()
        for (a, k), cp in remote.items():
            if (a, k + 1) not in relayed:
                cp.wait_recv()
        for cp in relayed.values():
            cp.wait_recv()
        for cp in list(remote.values()) + list(relayed.values()):
            cp.wait_send()
        for cp in local:
            cp.wait()


def _call(body, *, name, args, in_specs, out_specs, out_shape, scratch_shapes=(), grid=(),
          params=None, exchange=None):
    n_in, n_out = len(args), len(out_shape)
    if exchange is None:
        outs = pl.pallas_call(
            body, name=name, grid=grid, in_specs=list(in_specs), out_specs=list(out_specs),
            out_shape=list(out_shape), scratch_shapes=list(scratch_shapes), compiler_params=params,
        )(*args)
        return list(outs), []
    n_ex = exchange.n

    def wrapped(*refs):
        ins, refs = refs[:n_in], refs[n_in:]
        ex_in, refs = refs[:n_ex], refs[n_ex:]
        outs, refs = refs[:n_out], refs[n_out:]
        ex_out, refs = refs[:n_ex], refs[n_ex:]
        scratch, sems = refs[:len(refs) - 3], refs[len(refs) - 3:]
        if not grid:
            exchange.start(ex_in, ex_out, sems)
            body(*ins, *outs, *scratch)
            exchange.wait(ex_in, ex_out, sems)
            return
        first = functools.reduce(jnp.logical_and, [pl.program_id(a) == 0 for a in range(len(grid))])
        last = functools.reduce(jnp.logical_and, [pl.program_id(a) == grid[a] - 1 for a in range(len(grid))])

        @pl.when(first)
        def _():
            exchange.start(ex_in, ex_out, sems)

        body(*ins, *outs, *scratch)

        @pl.when(last)
        def _():
            exchange.wait(ex_in, ex_out, sems)

    any_spec = pl.BlockSpec(memory_space=pl.ANY)
    outs = pl.pallas_call(
        wrapped, name=name, grid=grid,
        in_specs=list(in_specs) + [any_spec] * n_ex, out_specs=list(out_specs) + [any_spec] * n_ex,
        out_shape=list(out_shape) + exchange.out_shape,
        scratch_shapes=list(scratch_shapes) + exchange.scratch_shapes, compiler_params=params,
    )(*args, *exchange.arrays)
    return list(outs[:n_out]), list(outs[n_out:])


def _exchange(arrays, scatter, name, relay=False):
    return _call(lambda: None, name=name, args=(), in_specs=(), out_specs=(), out_shape=(),
                 exchange=_Exchange(arrays, scatter, relay))[1]


def _first_exchange(c_pad, shards, w, b):
    rows, d = c_pad.shape
    cols = w.shape[1]
    ex_c = _Exchange([c_pad], False)
    ex_w = _Exchange(shards, False, relay=True)
    ex_m = _Exchange([jax.ShapeDtypeStruct((N_DEV * rows, cols), F32)], False)
    n_w = ex_w.n

    def body(*refs):
        c_ref, w_refs, wa_ref, b_ref = refs[0], refs[1:1 + n_w], refs[1 + n_w], refs[2 + n_w]
        outs = refs[3 + n_w:]
        cg_ref, wg_refs, mg_ref = outs[0], outs[1:1 + n_w], outs[1 + n_w]
        scratch = outs[2 + n_w:]
        sems_c, sems_w, sems_m, c_vm, m_vm = scratch[0:3], scratch[3:6], scratch[6:9], scratch[9], scratch[10]
        ex_c.start([c_ref], [cg_ref], sems_c)
        ex_c.wait([c_ref], [cg_ref], sems_c)
        pltpu.sync_copy(cg_ref, c_vm)
        cv = c_vm[...].reshape(N_DEV * rows, d)
        s = (cv * _sigmoid(cv)).astype(BF16)
        m_vm[...] = _dot(s, wa_ref[...].astype(BF16)) + b_ref[...]
        ex_m.start([m_vm], [mg_ref], sems_m)
        ex_w.start(w_refs, wg_refs, sems_w)
        ex_m.wait([m_vm], [mg_ref], sems_m)
        ex_w.wait(w_refs, wg_refs, sems_w)

    any_spec = pl.BlockSpec(memory_space=pl.ANY)
    vmem_spec = pl.BlockSpec(memory_space=pltpu.VMEM)
    outs = pl.pallas_call(
        body, name="first_exchange",
        in_specs=[any_spec] * (1 + n_w) + [vmem_spec, vmem_spec],
        out_specs=[any_spec] * (2 + n_w),
        out_shape=ex_c.out_shape + ex_w.out_shape + ex_m.out_shape,
        scratch_shapes=ex_c.scratch_shapes + ex_w.scratch_shapes + ex_m.scratch_shapes
        + [pltpu.VMEM((N_DEV, rows, d), F32), pltpu.VMEM((N_DEV * rows, cols), F32)],
        compiler_params=pltpu.CompilerParams(vmem_limit_bytes=VMEM_LIMIT_BYTES),
    )(c_pad, *shards, w, b)
    return outs[0], outs[1 + n_w], list(outs[1:1 + n_w])


def _ada_bwd(c_all, dmod_cols, dmod_all):
    def body(c_ref, dc_ref, da_ref, gw_ref, gb_ref):
        cv = c_ref[...]
        s = cv * _sigmoid(cv)
        gw_ref[...] = lax.dot_general(s, dc_ref[...], TN_DIMS, preferred_element_type=F32,
                                      precision=lax.Precision.HIGHEST)
        gb_ref[...] = jnp.sum(da_ref[...], axis=0, keepdims=True)

    return pl.pallas_call(
        body, name="ada_bwd",
        out_shape=(jax.ShapeDtypeStruct((c_all.shape[1], dmod_cols.shape[1]), F32),
                   jax.ShapeDtypeStruct((1, dmod_all.shape[1]), F32)),
        compiler_params=pltpu.CompilerParams(vmem_limit_bytes=VMEM_LIMIT_BYTES),
    )(c_all, dmod_cols, dmod_all)


def _ffn_fwd(x, mod, g, wg, wu, wd, k, tm, exchange=None):
    t, d = x.shape
    ns, _, fs = wg.shape
    nt = t // tm
    tpb = nt // mod.shape[0]
    rows = tm // FFN_CHUNKS

    def body(x_ref, mod_ref, g_ref, wg_ref, wu_ref, wd_ref, xo_ref, f_ref, gg_ref, uu_ref, h_sc, acc):
        j = pl.program_id(1)

        @pl.when(j == 0)
        def _():
            h_sc[...] = _modulate(x_ref[...], g_ref[...], mod_ref, k).astype(BF16)
            acc[...] = jnp.zeros_like(acc)

        chunks = [pl.ds(c * rows, rows) for c in range(FFN_CHUNKS)]
        wg, wu, wd = wg_ref[...], wu_ref[...], wd_ref[...]
        gates, ups = [], []
        for rs in chunks:
            h = h_sc[rs, :]
            gates.append(_dot(h, wg))
            ups.append(_dot(h, wu))
        acts = [(g * _sigmoid(g) * u).astype(BF16) for g, u in zip(gates, ups)]
        for rs, g, u in zip(chunks, gates, ups):
            gg_ref[rs, :] = g.astype(BF16)
            uu_ref[rs, :] = u.astype(BF16)
        downs = [_dot(a, wd) for a in acts]
        for rs, dn in zip(chunks, downs):
            acc[rs, :] += dn

        @pl.when(j == ns - 1)
        def _():
            f = acc[...]
            f_ref[...] = f.astype(BF16)
            xo_ref[...] = x_ref[...] + 0.5 * mod_ref[3 * k + 2:3 * k + 3, :] * f

    tok = pl.BlockSpec((tm, d), lambda i, j: (i, 0))
    hid = pl.BlockSpec((None, tm, fs), lambda i, j: (j, i, 0))
    return _call(
        body, name=f"ffn_fwd{k}", grid=(nt, ns), args=(x, mod, g, wg, wu, wd),
        in_specs=[_once(tok),
                  pl.BlockSpec((None, N_MOD, d), lambda i, j: (i // tpb, 0, 0)),
                  pl.BlockSpec((1, d), lambda i, j: (0, 0)),
                  pl.BlockSpec((None, d, fs), lambda i, j: (j, 0, 0)),
                  pl.BlockSpec((None, d, fs), lambda i, j: (j, 0, 0)),
                  pl.BlockSpec((None, fs, d), lambda i, j: (j, 0, 0))],
        out_specs=[tok, tok, hid, hid],
        out_shape=[jax.ShapeDtypeStruct((t, d), F32), jax.ShapeDtypeStruct((t, d), BF16),
                   jax.ShapeDtypeStruct((ns, t, fs), BF16), jax.ShapeDtypeStruct((ns, t, fs), BF16)],
        scratch_shapes=[pltpu.VMEM((tm, d), BF16), pltpu.VMEM((tm, d), F32)],
        params=_params("arbitrary", "arbitrary"), exchange=exchange)


def _ffn_bwd(dxo, x, f, mod, g, gate, up, wg, wu, wd, k, tm, exchange=None):
    t, d = x.shape
    ns, _, fs = wg.shape
    nt = t // tm
    nb = mod.shape[0]
    tpb = nt // nb
    rows = tm // FFN_CHUNKS

    def body(dxo_ref, x_ref, f_ref, mod_ref, g_ref, gg_ref, uu_ref, wg_ref, wu_ref, wd_ref,
             dx_ref, dgg_ref, duu_ref, act_ref, h_ref, df_ref, dmod_ref, dg_ref, acc):
        i, j = pl.program_id(0), pl.program_id(1)

        @pl.when(j == 0)
        def _():
            df = 0.5 * mod_ref[3 * k + 2:3 * k + 3, :] * dxo_ref[...]
            df_ref[...] = df.astype(BF16)
            h_ref[...] = _modulate(x_ref[...], g_ref[...], mod_ref, k).astype(BF16)
            acc[...] = jnp.zeros_like(acc)

        chunks = [pl.ds(c * rows, rows) for c in range(FFN_CHUNKS)]
        wg, wu, wd = wg_ref[...], wu_ref[...], wd_ref[...]
        dacts = [_dot_nt(df_ref[rs, :], wd) for rs in chunks]
        dgates, dups = [], []
        for rs, dact in zip(chunks, dacts):
            gv, uv = gg_ref[rs, :].astype(F32), uu_ref[rs, :].astype(F32)
            sig = _sigmoid(gv)
            s = gv * sig
            act_ref[rs, :] = (s * uv).astype(BF16)
            dups.append((dact * s).astype(BF16))
            dgates.append((dact * uv * (sig * (1.0 + gv * (1.0 - sig)))).astype(BF16))
        dhs = [_dot_nt(dg, wg) + _dot_nt(du, wu) for dg, du in zip(dgates, dups)]
        for rs, dg, du, dh in zip(chunks, dgates, dups, dhs):
            dgg_ref[rs, :] = dg
            duu_ref[rs, :] = du
            acc[rs, :] += dh

        @pl.when(j == ns - 1)
        def _():
            dx, dshift, dscale, dg = _modulate_bwd(acc[...], x_ref[...], g_ref[...], mod_ref, k)
            dxo_v = dxo_ref[...]
            dx_ref[...] = dxo_v + dx
            dgt = jnp.sum(0.5 * f_ref[...].astype(F32) * dxo_v, axis=0, keepdims=True)

            @pl.when(i % tpb == 0)
            def _():
                dmod_ref[...] = jnp.zeros_like(dmod_ref)

            @pl.when(i == 0)
            def _():
                dg_ref[...] = jnp.zeros_like(dg_ref)

            dmod_ref[0:1, :] += dshift
            dmod_ref[1:2, :] += dscale
            dmod_ref[2:3, :] += dgt
            dg_ref[0:1, :] += dg

    tok = pl.BlockSpec((tm, d), lambda i, j: (i, 0))
    hid = pl.BlockSpec((None, tm, fs), lambda i, j: (j, i, 0))
    return _call(
        body, name=f"ffn_bwd{k}", grid=(nt, ns), args=(dxo, x, f, mod, g, gate, up, wg, wu, wd),
        in_specs=[_once(tok), _once(tok), _once(tok),
                  pl.BlockSpec((None, N_MOD, d), lambda i, j: (i // tpb, 0, 0)),
                  pl.BlockSpec((1, d), lambda i, j: (0, 0)),
                  hid, hid,
                  pl.BlockSpec((None, d, fs), lambda i, j: (j, 0, 0)),
                  pl.BlockSpec((None, d, fs), lambda i, j: (j, 0, 0)),
                  pl.BlockSpec((None, fs, d), lambda i, j: (j, 0, 0))],
        out_specs=[tok, hid, hid, hid, tok, tok,
                   pl.BlockSpec((None, 8, d), lambda i, j: (i // tpb, 0, 0)),
                   pl.BlockSpec((8, d), lambda i, j: (0, 0))],
        out_shape=[jax.ShapeDtypeStruct((t, d), F32),
                   jax.ShapeDtypeStruct((ns, t, fs), BF16), jax.ShapeDtypeStruct((ns, t, fs), BF16),
                   jax.ShapeDtypeStruct((ns, t, fs), BF16),
                   jax.ShapeDtypeStruct((t, d), BF16), jax.ShapeDtypeStruct((t, d), BF16),
                   jax.ShapeDtypeStruct((nb, 8, d), F32), jax.ShapeDtypeStruct((8, d), F32)],
        scratch_shapes=[pltpu.VMEM((tm, d), F32)],
        params=_params("arbitrary", "arbitrary"), exchange=exchange)


def _mm_tn(a, b, a_spec, b_spec, out_shape, n_tiles, name, exchange=None):
    n_out = out_shape[0]
    block = tuple(out_shape[1:])
    last = n_tiles - 1
    flip = block[0] > block[1]
    if flip:
        block = block[::-1]

    def body(a_ref, b_ref, o_ref, acc):
        i, j = pl.program_id(0), pl.program_id(1)
        prod = _dot_tn(b_ref[...], a_ref[...]) if flip else _dot_tn(a_ref[...], b_ref[...])

        @pl.when(i == 0)
        def _():
            acc[j] = prod

        @pl.when(i > 0)
        def _():
            acc[j] += prod

        @pl.when(i == last)
        def _():
            total = acc[j]
            o_ref[j] = (total.T if flip else total).astype(o_ref.dtype)

    outs, sent = _call(
        body, name=name, grid=(n_tiles, n_out), args=(a, b), in_specs=[a_spec, b_spec],
        out_specs=[pl.BlockSpec(out_shape, lambda i, j: (0,) * len(out_shape))],
        out_shape=[jax.ShapeDtypeStruct(out_shape, BF16)],
        scratch_shapes=[pltpu.VMEM((n_out,) + block, F32)],
        params=_params("arbitrary", "arbitrary"), exchange=exchange)
    return (outs[0], sent) if exchange is not None else outs[0]


def _ffn_weight_grads(h, dgate, dup, act, df, tm, tag, stream=False):
    t, d = h.shape
    ns, _, fs = dgate.shape
    nt = t // tm
    tok = pl.BlockSpec((tm, d), lambda i, j: (i, 0))
    hid = pl.BlockSpec((None, tm, fs), lambda i, j: (j, i, 0))
    gwg = _mm_tn(h, dgate, tok, hid, (ns, d, fs), nt, f"grad_wg{tag}")
    if not stream:
        gwu = _mm_tn(h, dup, tok, hid, (ns, d, fs), nt, f"grad_wu{tag}")
        gwd = _mm_tn(act, df, hid, tok, (ns, fs, d), nt, f"grad_wd{tag}")
        return gwg, gwu, gwd
    gwu, sent_g = _mm_tn(h, dup, tok, hid, (ns, d, fs), nt, f"grad_wu{tag}", _Exchange([gwg], True))
    gwd, sent_u = _mm_tn(act, df, hid, tok, (ns, fs, d), nt, f"grad_wd{tag}", _Exchange([gwu], True))
    return sent_g[0], sent_u[0], gwd


def _stage_shape(rows, cols):
    return pltpu.VMEM((cols // LANES, rows, LANES), F32)


def _stage(value, stage_ref):
    for k in range(stage_ref.shape[0]):
        stage_ref[k] = value[:, k * LANES:(k + 1) * LANES]


def _to_residue_rows(stage_ref, dst_ref, dil):
    rows = stage_ref.shape[1] // dil
    for r in range(dil):
        for k in range(stage_ref.shape[0]):
            dst_ref[r, :, k * LANES:(k + 1) * LANES] = (
                stage_ref.at[k][pl.ds(r, rows, stride=dil), :].astype(dst_ref.dtype))


def _from_residue_rows(src_ref, stage_ref, dil):
    rows = stage_ref.shape[1] // dil
    chunks = range(stage_ref.shape[0])
    for r in range(dil):
        for k in chunks:
            stage_ref.at[k][pl.ds(r, rows, stride=dil), :] = src_ref[r, :, k * LANES:(k + 1) * LANES].astype(F32)
    return jnp.concatenate([stage_ref[k] for k in chunks], axis=1)


def _residue_shape(nb, seq, width, dil, dtype):
    return jax.ShapeDtypeStruct((nb, dil, seq // dil, width), dtype)


def _residue_spec(tm, tpb, cols, dil, col_block):
    return pl.BlockSpec((None, dil, tm // dil, cols),
                        lambda i, *rest: (i // tpb, 0, i % tpb, col_block(i, *rest)))


def _qkv_fwd(x, mod, g, win, tm):
    t, d = x.shape
    ns, _, cs = win.shape
    nt = t // tm
    nb = mod.shape[0]
    tpb = nt // nb
    seq = t // nb
    half = ns // 2
    n_res = len(DILATIONS) - 1

    def body(x_ref, mod_ref, g_ref, w_ref, sb_ref, dil_ref, *rest):
        res_refs, h_ref, sc = rest[:n_res], rest[n_res], rest[n_res + 1]
        j = pl.program_id(1)

        @pl.when(j == 0)
        def _():
            h_ref[...] = _modulate(x_ref[...], g_ref[...], mod_ref, 1).astype(BF16)

        res = _dot(h_ref[...], w_ref[...])

        @pl.when(j < half)
        def _():
            sb_ref[...] = res.astype(BF16)

        @pl.when(j >= half)
        def _():
            dil_ref[...] = res.astype(BF16)
            _stage(res, sc)
            for ref, dil in zip(res_refs, DILATIONS[1:]):
                _to_residue_rows(sc, ref, dil)

    def dil_col(i, j):
        return jnp.maximum(j - half, 0)

    tok = pl.BlockSpec((tm, d), lambda i, j: (i, 0))
    wide = jax.ShapeDtypeStruct((t, half * cs), BF16)
    outs = pl.pallas_call(
        body, name="qkv_fwd", grid=(nt, ns),
        in_specs=[_once(tok),
                  pl.BlockSpec((None, N_MOD, d), lambda i, j: (i // tpb, 0, 0)),
                  pl.BlockSpec((1, d), lambda i, j: (0, 0)),
                  pl.BlockSpec((None, d, cs), lambda i, j: (j, 0, 0))],
        out_specs=[pl.BlockSpec((tm, cs), lambda i, j: (i, jnp.minimum(j, half - 1))),
                   pl.BlockSpec((tm, cs), lambda i, j: (i, dil_col(i, j)))]
        + [_residue_spec(tm, tpb, cs, dil, dil_col) for dil in DILATIONS[1:]] + [tok],
        out_shape=[wide, wide] + [_residue_shape(nb, seq, half * cs, dil, BF16) for dil in DILATIONS[1:]]
        + [jax.ShapeDtypeStruct((t, d), BF16)],
        scratch_shapes=[_stage_shape(tm, cs)],
        compiler_params=_params("arbitrary", "arbitrary"),
    )(x, mod, g, win)
    qkv_dil = [outs[1]] + [a.reshape(t, half * cs) for a in outs[2:2 + n_res]]
    return outs[0], qkv_dil, outs[-1]


def _qkv_bwd(dqkv, dxo, x, mod, g, win, tm, exchange=None):
    t, d = x.shape
    ns, _, cs = win.shape
    nt = t // tm
    nb = mod.shape[0]
    tpb = nt // nb

    def body(dq_ref, dxo_ref, x_ref, mod_ref, g_ref, w_ref, dx_ref, dmod_ref, dg_ref, acc):
        i, j = pl.program_id(0), pl.program_id(1)

        @pl.when(j == 0)
        def _():
            acc[...] = jnp.zeros_like(acc)

        acc[...] += _dot_nt(dq_ref[...], w_ref[...])

        @pl.when(j == ns - 1)
        def _():
            dx, dshift, dscale, dg = _modulate_bwd(acc[...], x_ref[...], g_ref[...], mod_ref, 1)
            dx_ref[...] = dxo_ref[...] + dx

            @pl.when(i % tpb == 0)
            def _():
                dmod_ref[...] = jnp.zeros_like(dmod_ref)

            @pl.when(i == 0)
            def _():
                dg_ref[...] = jnp.zeros_like(dg_ref)

            dmod_ref[0:1, :] += dshift
            dmod_ref[1:2, :] += dscale
            dg_ref[0:1, :] += dg

    tok = pl.BlockSpec((tm, d), lambda i, j: (i, 0))
    return _call(
        body, name="qkv_bwd", grid=(nt, ns), args=(dqkv, dxo, x, mod, g, win),
        in_specs=[pl.BlockSpec((tm, cs), lambda i, j: (i, j)), _once(tok), _once(tok),
                  pl.BlockSpec((None, N_MOD, d), lambda i, j: (i // tpb, 0, 0)),
                  pl.BlockSpec((1, d), lambda i, j: (0, 0)),
                  pl.BlockSpec((None, d, cs), lambda i, j: (j, 0, 0))],
        out_specs=[tok,
                   pl.BlockSpec((None, 8, d), lambda i, j: (i // tpb, 0, 0)),
                   pl.BlockSpec((8, d), lambda i, j: (0, 0))],
        out_shape=[jax.ShapeDtypeStruct((t, d), F32),
                   jax.ShapeDtypeStruct((nb, 8, d), F32), jax.ShapeDtypeStruct((8, d), F32)],
        scratch_shapes=[pltpu.VMEM((tm, d), F32)],
        params=_params("arbitrary", "arbitrary"), exchange=exchange)


def _heads(a):
    return [a[:, h * HEAD_DIM:(h + 1) * HEAD_DIM] for h in range(a.shape[1] // HEAD_DIM)]


def _own_lanes():
    lane = lax.broadcasted_iota(jnp.int32, (1, LANES), 1)
    return [lane < HEAD_DIM, lane >= HEAD_DIM]


def _pair_tiles(a):
    return [a[:, (h // 2) * LANES:(h // 2 + 1) * LANES] for h in range(a.shape[1] // HEAD_DIM)]


def _own_tiles(a, own):
    return [jnp.where(own[h % 2], tile, jnp.zeros_like(tile)) for h, tile in enumerate(_pair_tiles(a))]


def _merge_tiles(per_head, own):
    return jnp.concatenate([jnp.where(own[0], per_head[h], per_head[h + 1])
                            for h in range(0, len(per_head), 2)], axis=1)


def _scaled(q):
    return (q.astype(F32) * (HEAD_DIM ** -0.5)).astype(BF16)


def _sb_logits(qh, kh, tri, causal):
    zs = [_dot_nt(q, k) for q, k in zip(qh, kh)]
    es = [jnp.exp(-jnp.abs(z)) for z in zs]
    log_nots = [-(jnp.maximum(z, 0.0) + jnp.log(1.0 + e)) for z, e in zip(zs, es)]
    if causal is not None:
        log_nots = [jnp.where(causal, ln, 0.0) for ln in log_nots]
    return zs, es, [_split_dot(ln, tri) for ln in log_nots]


def _sb_masks():
    rows = lax.broadcasted_iota(jnp.int32, (SB_BLOCK, SB_BLOCK), 0)
    cols = lax.broadcasted_iota(jnp.int32, (SB_BLOCK, SB_BLOCK), 1)
    return (rows >= cols).astype(BF16), (rows <= cols).astype(BF16), cols < rows


def _sb_fwd(qkv, nb, seq, exchange=None):
    t = qkv.shape[0]
    n_pairs = (qkv.shape[1] // 3) // SB_WIDTH
    tb = SB_BLOCK
    n_blk = seq // tb

    def body(q_ref, k_ref, v_ref, o_ref, c_ref):
        tri, _, causal = _sb_masks()
        own = _own_lanes()

        def key_block(qh, kj, carry, mask):
            ks = pl.multiple_of(kj * tb, tb)
            kh, vh = _pair_tiles(k_ref[pl.ds(ks, tb), :]), _pair_tiles(v_ref[pl.ds(ks, tb), :])
            zs, _, suffixes = _sb_logits(qh, kh, tri, mask)
            ws = [jnp.exp(z + suffix + cr[1]) for z, suffix, cr in zip(zs, suffixes, carry)]
            if mask is not None:
                ws = [jnp.where(mask, w, 0.0) for w in ws]
            pv = [_dot(w.astype(BF16), v) for w, v in zip(ws, vh)]
            return tuple((cr[0] + p, cr[1] + suffix[:, 0:1]) for cr, p, suffix in zip(carry, pv, suffixes))

        def query_block(qi, _):
            qs = pl.multiple_of(qi * tb, tb)
            qh = _own_tiles(_scaled(q_ref[pl.ds(qs, tb), :]), own)
            zero = (jnp.zeros((tb, LANES), F32), jnp.zeros((tb, 1), F32))
            carry = key_block(qh, qi, (zero,) * SB_HEADS, causal)
            carry = lax.fori_loop(0, qi, lambda it, cr: key_block(qh, qi - 1 - it, cr, None), carry)
            o_ref[pl.ds(qs, tb), :] = _merge_tiles([cr[0] for cr in carry], own)
            c_ref[pl.ds(qs, tb), :] = _merge_tiles([jnp.broadcast_to(cr[1], (tb, LANES)) for cr in carry], own)
            return 0

        lax.fori_loop(0, n_blk, query_block, 0)

    def spec(offset):
        return pl.BlockSpec((seq, SB_WIDTH), lambda b, p: (b, offset + p))

    out = jax.ShapeDtypeStruct((t, n_pairs * SB_WIDTH), F32)
    return _call(
        body, name="sb_fwd", grid=(nb, n_pairs), args=(qkv, qkv, qkv),
        in_specs=[spec(0), spec(n_pairs), spec(2 * n_pairs)],
        out_specs=[spec(0), spec(0)], out_shape=[out, out],
        params=_params("arbitrary", "arbitrary"), exchange=exchange)


def _sb_bwd(qkv, do, csum, nb, seq, exchange=None):
    t = qkv.shape[0]
    n_pairs = (qkv.shape[1] // 3) // SB_WIDTH
    tb = SB_BLOCK
    n_blk = seq // tb
    scale = HEAD_DIM ** -0.5

    def body(q_ref, k_ref, v_ref, do_ref, c_ref, dq_ref, dk_ref, dv_ref, dkt_acc, dvt_acc):
        tri, tri_prefix, causal = _sb_masks()
        own = _own_lanes()
        dkt_acc[...] = jnp.zeros_like(dkt_acc)
        dvt_acc[...] = jnp.zeros_like(dvt_acc)

        def key_block(qh, qth, doh, doth, ch, kj, carry, mask):
            ks = pl.multiple_of(kj * tb, tb)
            kh, vh = _pair_tiles(k_ref[pl.ds(ks, tb), :]), _pair_tiles(v_ref[pl.ds(ks, tb), :])
            heads = range(SB_HEADS)
            zs, es, suffixes = _sb_logits(qh, kh, tri, mask)
            dws = [_dot_nt(doh[h], vh[h]) for h in heads]
            lefts = [carry[h][1] + suffixes[h][:, 0:1] for h in heads]
            ws = [jnp.exp(zs[h] + suffixes[h] + (ch[h] - lefts[h])) for h in heads]
            if mask is not None:
                ws = [jnp.where(mask, w, 0.0) for w in ws]
            dlws = [ws[h] * dws[h] for h in heads]
            dprefixes = [_split_dot(dlw, tri_prefix) for dlw in dlws]
            dvts = [_dot(doth[h], ws[h].astype(BF16)) for h in heads]
            dzbs = []
            for h in heads:
                sig = jnp.where(zs[h] >= 0.0, 1.0, es[h]) * pl.reciprocal(1.0 + es[h], approx=True)
                dz = dlws[h] - sig * (carry[h][2] + dprefixes[h])
                if mask is not None:
                    dz = jnp.where(mask, dz, 0.0)
                dzbs.append(dz.astype(BF16))
            dkts = [_dot(qth[h], dzbs[h]) for h in heads]
            dqs = [_dot(dzbs[h], kh[h]) for h in heads]
            dkt_acc[:, pl.ds(ks, tb)] += jnp.concatenate([dkts[h] + dkts[h + 1] for h in heads[::2]], axis=0)
            dvt_acc[:, pl.ds(ks, tb)] += jnp.concatenate([dvts[h] + dvts[h + 1] for h in heads[::2]], axis=0)
            return tuple((carry[h][0] + dqs[h], lefts[h], carry[h][2] + dprefixes[h][:, tb - 1:tb])
                         for h in heads)

        def query_block(qi, _):
            qs = pl.multiple_of(qi * tb, tb)
            qh = _own_tiles(_scaled(q_ref[pl.ds(qs, tb), :]), own)
            doh = _own_tiles(do_ref[pl.ds(qs, tb), :], own)
            qth = [a.astype(F32).T.astype(BF16) for a in qh]
            doth = [a.T.astype(BF16) for a in doh]
            doh = [a.astype(BF16) for a in doh]
            cv = c_ref[pl.ds(qs, tb), :]
            ch = [cv[:, h * HEAD_DIM:h * HEAD_DIM + 1] for h in range(SB_HEADS)]
            zero = (jnp.zeros((tb, LANES), F32), jnp.zeros((tb, 1), F32), jnp.zeros((tb, 1), F32))
            carry = lax.fori_loop(
                0, qi, lambda kj, cr: key_block(qh, qth, doh, doth, ch, kj, cr, None), (zero,) * SB_HEADS)
            carry = key_block(qh, qth, doh, doth, ch, qi, carry, causal)
            dq = _merge_tiles([cr[0] for cr in carry], own) * scale
            dq_ref[pl.ds(qs, tb), :] = dq.astype(BF16)
            return 0

        lax.fori_loop(0, n_blk, query_block, 0)
        dk_ref[...] = dkt_acc[...].T.astype(BF16)
        dv_ref[...] = dvt_acc[...].T.astype(BF16)

    def spec(offset):
        return pl.BlockSpec((seq, SB_WIDTH), lambda b, p: (b, offset + p))

    out = jax.ShapeDtypeStruct((t, n_pairs * SB_WIDTH), BF16)
    return _call(
        body, name="sb_bwd", grid=(nb, n_pairs), args=(qkv, qkv, qkv, do, csum),
        in_specs=[spec(0), spec(n_pairs), spec(2 * n_pairs), spec(0), spec(0)],
        out_specs=[spec(0), spec(0), spec(0)],
        out_shape=[out, out, out],
        scratch_shapes=[pltpu.VMEM((SB_WIDTH, seq), F32), pltpu.VMEM((SB_WIDTH, seq), F32)],
        params=_params("arbitrary", "arbitrary"), exchange=exchange)


def _dil_block_scores(qh, kph, kch, bias_ref, has_prev, band_prev, band_cur):
    scale = HEAD_DIM ** -0.5
    heads = range(len(qh))
    no_prev = jnp.where(has_prev, 0.0, NEG_INF)
    zps = [_dot_nt(qh[h], kph[h]) for h in heads]
    zcs = [_dot_nt(qh[h], kch[h]) for h in heads]
    zps = [jnp.where(band_prev, zps[h] * scale + bias_ref[h, :, 0:DIL_BLOCK], NEG_INF) + no_prev for h in heads]
    zcs = [jnp.where(band_cur, zcs[h] * scale + bias_ref[h, :, DIL_BLOCK:2 * DIL_BLOCK], NEG_INF) for h in heads]
    return zps, zcs


def _dil_bands():
    rows = lax.broadcasted_iota(jnp.int32, (DIL_BLOCK, DIL_BLOCK), 0)
    cols = lax.broadcasted_iota(jnp.int32, (DIL_BLOCK, DIL_BLOCK), 1)
    return cols >= rows, cols <= rows


def _dil_fwd(qkv, bias, nb, seq, dil, exchange=None):
    t, width = qkv.shape
    n_pairs = (width // 3) // DIL_WIDTH
    bq = DIL_BLOCK
    n_blk = seq // bq
    per_seq = n_blk // dil
    heads = range(DIL_HEADS)

    def body(q_ref, k_ref, v_ref, bias_ref, o_ref, lse_ref):
        band_prev, band_cur = _dil_bands()
        own = _own_lanes()

        def block(n, _):
            has_prev = (n & (per_seq - 1)) != 0
            qs = pl.multiple_of(n * bq, bq)
            ps = pl.multiple_of(jnp.maximum(n - 1, 0) * bq, bq)
            qh = _own_tiles(q_ref[pl.ds(qs, bq), :], own)
            kp, kc = _pair_tiles(k_ref[pl.ds(ps, bq), :]), _pair_tiles(k_ref[pl.ds(qs, bq), :])
            vp, vc = _pair_tiles(v_ref[pl.ds(ps, bq), :]), _pair_tiles(v_ref[pl.ds(qs, bq), :])
            zps, zcs = _dil_block_scores(qh, kp, kc, bias_ref, has_prev, band_prev, band_cur)
            ms = [jnp.maximum(jnp.max(zps[h], axis=1, keepdims=True), jnp.max(zcs[h], axis=1, keepdims=True))
                  for h in heads]
            eps = [jnp.exp(zps[h] - ms[h]) for h in heads]
            ecs = [jnp.exp(zcs[h] - ms[h]) for h in heads]
            pvs = [_dot(eps[h].astype(BF16), vp[h]) + _dot(ecs[h].astype(BF16), vc[h]) for h in heads]
            dens = [jnp.sum(eps[h], axis=1, keepdims=True) + jnp.sum(ecs[h], axis=1, keepdims=True) for h in heads]
            o_ref[pl.ds(qs, bq), :] = _merge_tiles([pvs[h] / dens[h] for h in heads], own)
            lse_ref[pl.ds(qs, bq), :] = _merge_tiles(
                [jnp.broadcast_to(ms[h] + jnp.log(dens[h]), (bq, LANES)) for h in heads], own)
            return 0

        lax.fori_loop(0, n_blk, block, 0)

    def spec(offset):
        return pl.BlockSpec((seq, DIL_WIDTH), lambda b, p: (b, offset + p))

    out = jax.ShapeDtypeStruct((t, n_pairs * DIL_WIDTH), F32)
    return _call(
        body, name=f"dil_fwd{dil}", grid=(nb, n_pairs), args=(qkv, qkv, qkv, bias),
        in_specs=[spec(0), spec(n_pairs), spec(2 * n_pairs),
                  pl.BlockSpec((DIL_HEADS, bq, 2 * bq), lambda b, p: (p, 0, 0))],
        out_specs=[spec(0), spec(0)], out_shape=[out, out],
        params=_params("arbitrary", "arbitrary"), exchange=exchange)


def _dil_bwd(qkv, bias, do, lse, delta, nb, seq, dil):
    t, width = qkv.shape
    n_pairs = (width // 3) // DIL_WIDTH
    bq = DIL_BLOCK
    n_blk = seq // bq
    per_seq = n_blk // dil
    scale = HEAD_DIM ** -0.5
    heads = range(DIL_HEADS)

    def body(q_ref, k_ref, v_ref, bias_ref, do_ref, lse_ref, dl_ref, dq_ref, dk_ref, dv_ref, db_ref,
             dk_acc, dv_acc):
        band_prev, band_cur = _dil_bands()
        own = _own_lanes()
        dk_acc[...] = jnp.zeros_like(dk_acc)
        dv_acc[...] = jnp.zeros_like(dv_acc)

        @pl.when(pl.program_id(1) == 0)
        def _():
            db_ref[...] = jnp.zeros_like(db_ref)

        def block(n, _):
            has_prev = (n & (per_seq - 1)) != 0
            qs = pl.multiple_of(n * bq, bq)
            ps = pl.multiple_of(jnp.maximum(n - 1, 0) * bq, bq)
            qh = _own_tiles(q_ref[pl.ds(qs, bq), :], own)
            kp, kc = _pair_tiles(k_ref[pl.ds(ps, bq), :]), _pair_tiles(k_ref[pl.ds(qs, bq), :])
            vp, vc = _pair_tiles(v_ref[pl.ds(ps, bq), :]), _pair_tiles(v_ref[pl.ds(qs, bq), :])
            doh = _own_tiles(do_ref[pl.ds(qs, bq), :].astype(BF16), own)
            lse_v, dl_v = lse_ref[pl.ds(qs, bq), :], dl_ref[pl.ds(qs, bq), :]
            zps, zcs = _dil_block_scores(qh, kp, kc, bias_ref, has_prev, band_prev, band_cur)
            dpp = [_dot_nt(doh[h], vp[h]) for h in heads]
            dpc = [_dot_nt(doh[h], vc[h]) for h in heads]
            lse_h = [lse_v[:, h * HEAD_DIM:h * HEAD_DIM + 1] for h in heads]
            dl_h = [dl_v[:, h * HEAD_DIM:h * HEAD_DIM + 1] for h in heads]
            pps = [jnp.exp(zps[h] - lse_h[h]) for h in heads]
            pcs = [jnp.exp(zcs[h] - lse_h[h]) for h in heads]
            dvp = [_dot_tn(pps[h].astype(BF16), doh[h]) for h in heads]
            dvc = [_dot_tn(pcs[h].astype(BF16), doh[h]) for h in heads]
            dzps = [pps[h] * (dpp[h] - dl_h[h]) for h in heads]
            dzcs = [pcs[h] * (dpc[h] - dl_h[h]) for h in heads]
            dzp_b = [(dzps[h] * scale).astype(BF16) for h in heads]
            dzc_b = [(dzcs[h] * scale).astype(BF16) for h in heads]
            dqs = [_dot(dzp_b[h], kp[h]) + _dot(dzc_b[h], kc[h]) for h in heads]
            dkp = [_dot_tn(dzp_b[h], qh[h]) for h in heads]
            dkc = [_dot_tn(dzc_b[h], qh[h]) for h in heads]
            for h in heads:
                db_ref[h, :, 0:bq] += dzps[h]
                db_ref[h, :, bq:2 * bq] += dzcs[h]
            def pair_sums(per_head):
                return jnp.concatenate([per_head[h] + per_head[h + 1] for h in heads[::2]], axis=1)

            dq_ref[pl.ds(qs, bq), :] = _merge_tiles(dqs, own).astype(BF16)
            dk_acc[pl.ds(ps, bq), :] += pair_sums(dkp)
            dk_acc[pl.ds(qs, bq), :] += pair_sums(dkc)
            dv_acc[pl.ds(ps, bq), :] += pair_sums(dvp)
            dv_acc[pl.ds(qs, bq), :] += pair_sums(dvc)
            return 0

        lax.fori_loop(0, n_blk, block, 0)
        dk_ref[...] = dk_acc[...].astype(BF16)
        dv_ref[...] = dv_acc[...].astype(BF16)

    def spec(offset):
        return pl.BlockSpec((seq, DIL_WIDTH), lambda p, b: (b, offset + p))

    bias_spec = pl.BlockSpec((DIL_HEADS, bq, 2 * bq), lambda p, b: (p, 0, 0))
    out = jax.ShapeDtypeStruct((t, n_pairs * DIL_WIDTH), BF16)
    return pl.pallas_call(
        body, name=f"dil_bwd{dil}", grid=(n_pairs, nb),
        in_specs=[spec(0), spec(n_pairs), spec(2 * n_pairs), bias_spec, spec(0), spec(0), spec(0)],
        out_specs=[spec(0), spec(0), spec(0), bias_spec],
        out_shape=[out, out, out, jax.ShapeDtypeStruct(bias.shape, F32)],
        scratch_shapes=[pltpu.VMEM((seq, DIL_WIDTH), F32), pltpu.VMEM((seq, DIL_WIDTH), F32)],
        compiler_params=_params("arbitrary", "arbitrary"),
    )(qkv, qkv, qkv, bias, do, lse, delta)


def _head_blocks(width):
    rows = lax.broadcasted_iota(jnp.int32, (width, width), 0) // HEAD_DIM
    cols = lax.broadcasted_iota(jnp.int32, (width, width), 1) // HEAD_DIM
    return (rows == cols).astype(BF16)


def _head_mean(v, gmat):
    return _split_dot(v, gmat) * (1.0 / HEAD_DIM)


def _residue_views(arrays, nb, seq):
    return [a if dil == 1 else a.reshape(nb, dil, seq // dil, a.shape[1]) for a, dil in zip(arrays, DILATIONS)]


def _mix_out_fwd(osb, ocs, lses, gsb, gdil, wout, x, mod, tm):
    t, d = x.shape
    ds = osb.shape[1]
    nt = t // tm
    nb = mod.shape[0]
    tpb = nt // nb
    seq = t // nb
    n_cfg = len(DILATIONS)

    def body(osb_ref, *refs):
        oc_refs, lse_refs = refs[:n_cfg], refs[n_cfg:2 * n_cfg]
        gsb_ref, gdil_ref, w_ref, x_ref, mod_ref = refs[2 * n_cfg:2 * n_cfg + 5]
        xo_ref, on_ref, m_ref, odil_ref = refs[2 * n_cfg + 5:2 * n_cfg + 9]
        ld_refs = refs[2 * n_cfg + 9:3 * n_cfg + 9]
        stages, sc = refs[3 * n_cfg + 9:]
        ocv, lsev = [oc_refs[0][...]], [lse_refs[0][...]]
        for i, dil in enumerate(DILATIONS[1:]):
            ocv.append(_from_residue_rows(oc_refs[i + 1], stages.at[2 * i], dil))
            lsev.append(_from_residue_rows(lse_refs[i + 1], stages.at[2 * i + 1], dil))
        top = functools.reduce(jnp.maximum, lsev)
        total = top + jnp.log(sum(jnp.exp(l - top) for l in lsev))
        odil = sum(jnp.exp(l - total) * o for o, l in zip(ocv, lsev))
        odil_ref[...] = odil
        ld_refs[0][...] = total
        _stage(total, sc)
        for ref, dil in zip(ld_refs[1:], DILATIONS[1:]):
            _to_residue_rows(sc, ref, dil)
        gm = _head_blocks(ds)
        parts = []
        for o, g_ref in ((osb_ref[...], gsb_ref), (odil, gdil_ref)):
            parts.append(o * lax.rsqrt(_head_mean(o * o, gm) + EPS) * g_ref[...])
        on = jnp.concatenate(parts, axis=1).astype(BF16)
        on_ref[...] = on
        m = _dot(on, w_ref[...])
        m_ref[...] = m
        xo_ref[...] = x_ref[...] + mod_ref[5:6, :] * m

    tok = pl.BlockSpec((tm, d), lambda i: (i, 0))
    hd = pl.BlockSpec((tm, ds), lambda i: (i, 0))
    res = [hd] + [_residue_spec(tm, tpb, ds, dil, lambda i: 0) for dil in DILATIONS[1:]]
    res_shape = [jax.ShapeDtypeStruct((t, ds), F32)] + [_residue_shape(nb, seq, ds, dil, F32) for dil in DILATIONS[1:]]
    gain = pl.BlockSpec((1, ds), lambda i: (0, 0))
    outs = pl.pallas_call(
        body, name="mix_out_fwd", grid=(nt,),
        in_specs=[hd] + res + res + [gain, gain,
                  pl.BlockSpec(wout.shape, lambda i: (0, 0)),
                  tok, pl.BlockSpec((None, N_MOD, d), lambda i: (i // tpb, 0, 0))],
        out_specs=[tok, pl.BlockSpec((tm, 2 * ds), lambda i: (i, 0)), tok, hd] + res,
        out_shape=[jax.ShapeDtypeStruct((t, d), F32), jax.ShapeDtypeStruct((t, 2 * ds), BF16),
                   jax.ShapeDtypeStruct((t, d), F32), jax.ShapeDtypeStruct((t, ds), F32)] + res_shape,
        scratch_shapes=[pltpu.VMEM((2 * (n_cfg - 1), ds // LANES, tm, LANES), F32), _stage_shape(tm, ds)],
        compiler_params=_params("arbitrary"),
    )(osb, *_residue_views(ocs, nb, seq), *_residue_views(lses, nb, seq), gsb, gdil, wout, x, mod)
    return outs[0], outs[1], outs[2], outs[3], [a.reshape(t, ds) for a in outs[4:]]


def _mix_out_bwd(dxo, m, mod, wout, osb, odil, gsb, gdil, tm):
    t, d = dxo.shape
    ds = osb.shape[1]
    nt = t // tm
    nb = mod.shape[0]
    tpb = nt // nb
    seq = t // nb
    n_cfg = len(DILATIONS)

    def body(dxo_ref, m_ref, mod_ref, w_ref, osb_ref, odil_ref, gsb_ref, gdil_ref,
             dm_ref, dosb_ref, *rest):
        do_refs, dl_refs = rest[:n_cfg], rest[n_cfg:2 * n_cfg]
        dmod_ref, dg_ref, sc = rest[2 * n_cfg:]
        dodil_ref, dldil_ref = do_refs[0], dl_refs[0]
        i = pl.program_id(0)
        dxo_v = dxo_ref[...]
        dm = (mod_ref[5:6, :] * dxo_v).astype(BF16)
        dm_ref[...] = dm
        dgt = jnp.sum(m_ref[...] * dxo_v, axis=0, keepdims=True)
        don = _dot_nt(dm, w_ref[...])
        gm = _head_blocks(ds)

        @pl.when(i % tpb == 0)
        def _():
            dmod_ref[...] = jnp.zeros_like(dmod_ref)

        @pl.when(i == 0)
        def _():
            dg_ref[...] = jnp.zeros_like(dg_ref)

        dmod_ref[2:3, :] += dgt
        groups = ((osb_ref, gsb_ref, dosb_ref), (odil_ref, gdil_ref, dodil_ref))
        for k, (o_ref, g_ref, do_ref) in enumerate(groups):
            o = o_ref[...]
            dn_out = don[:, k * ds:(k + 1) * ds]
            r = lax.rsqrt(_head_mean(o * o, gm) + EPS)
            n = o * r
            dg_ref[0:1, k * ds:(k + 1) * ds] += jnp.sum(dn_out * n, axis=0, keepdims=True)
            dn = dn_out * g_ref[...]
            do = r * (dn - n * _head_mean(dn * n, gm))
            do_ref[...] = do
            if k == 1:
                delta = _head_mean(do * o, gm) * float(HEAD_DIM)
                dldil_ref[...] = delta
                for value, refs in ((do, do_refs), (delta, dl_refs)):
                    _stage(value, sc)
                    for ref, dil in zip(refs[1:], DILATIONS[1:]):
                        _to_residue_rows(sc, ref, dil)

    tok = pl.BlockSpec((tm, d), lambda i: (i, 0))
    hd = pl.BlockSpec((tm, ds), lambda i: (i, 0))
    res = [hd] + [_residue_spec(tm, tpb, ds, dil, lambda i: 0) for dil in DILATIONS[1:]]
    res_shape = [jax.ShapeDtypeStruct((t, ds), F32)] + [_residue_shape(nb, seq, ds, dil, F32) for dil in DILATIONS[1:]]
    gain = pl.BlockSpec((1, ds), lambda i: (0, 0))
    outs = pl.pallas_call(
        body, name="mix_out_bwd", grid=(nt,),
        in_specs=[tok, tok, pl.BlockSpec((None, N_MOD, d), lambda i: (i // tpb, 0, 0)),
                  pl.BlockSpec(wout.shape, lambda i: (0, 0)), hd, hd, gain, gain],
        out_specs=[tok, hd] + res + res
        + [pl.BlockSpec((None, 8, d), lambda i: (i // tpb, 0, 0)), pl.BlockSpec((8, 2 * ds), lambda i: (0, 0))],
        out_shape=[jax.ShapeDtypeStruct((t, d), BF16), jax.ShapeDtypeStruct((t, ds), F32)] + res_shape + res_shape
        + [jax.ShapeDtypeStruct((nb, 8, d), F32), jax.ShapeDtypeStruct((8, 2 * ds), F32)],
        scratch_shapes=[_stage_shape(tm, ds)],
        compiler_params=_params("arbitrary"),
    )(dxo, m, mod, wout, osb, odil, gsb, gdil)
    flat = [a.reshape(t, ds) for a in outs[2:2 + 2 * n_cfg]]
    return outs[0], outs[1], flat[:n_cfg], flat[n_cfg:], outs[-2], outs[-1]


def _merge_dqkv(sb_parts, dil_parts, nb, tm):
    t, ds = sb_parts[0].shape
    nt = t // tm
    tpb = nt // nb
    seq = t // nb
    n_cfg = len(DILATIONS)

    def body(*refs):
        sb_refs, dil_refs = refs[:3], refs[3:3 + 3 * n_cfg]
        o_ref, sc = refs[3 + 3 * n_cfg:]
        for k in range(3):
            o_ref[:, k * ds:(k + 1) * ds] = sb_refs[k][...]
            total = dil_refs[k * n_cfg][...].astype(F32)
            for i, dil in enumerate(DILATIONS[1:]):
                total = total + _from_residue_rows(dil_refs[k * n_cfg + i + 1], sc, dil)
            o_ref[:, (3 + k) * ds:(4 + k) * ds] = total.astype(BF16)

    hd = pl.BlockSpec((tm, ds), lambda i: (i, 0))
    res = [hd] + [_residue_spec(tm, tpb, ds, dil, lambda i: 0) for dil in DILATIONS[1:]]
    views = [v for parts in dil_parts for v in _residue_views(parts, nb, seq)]
    return pl.pallas_call(
        body, name="merge_dqkv", grid=(nt,),
        in_specs=[hd] * 3 + res * 3,
        out_specs=pl.BlockSpec((tm, 6 * ds), lambda i: (i, 0)),
        out_shape=jax.ShapeDtypeStruct((t, 6 * ds), BF16),
        scratch_shapes=[_stage_shape(tm, ds)],
        compiler_params=_params("arbitrary"),
    )(*sb_parts, *views)


def _loss_head(x, target, g, tm):
    t, d = x.shape

    def body(x_ref, t_ref, g_ref, dx_ref, acc_ref):
        @pl.when(pl.program_id(0) == 0)
        def _():
            acc_ref[...] = jnp.zeros_like(acc_ref)

        n, r = _norm(x_ref[...])
        gv = g_ref[...]
        err = n * gv - t_ref[...]
        dy = err * (1.0 / d)
        acc_ref[0:1, :] += jnp.sum(err * err, axis=0, keepdims=True)
        acc_ref[1:2, :] += jnp.sum(dy * n, axis=0, keepdims=True)
        dn = dy * gv
        dx_ref[...] = r * (dn - n * jnp.mean(dn * n, axis=-1, keepdims=True))

    tok = pl.BlockSpec((tm, d), lambda i: (i, 0))
    return pl.pallas_call(
        body, name="loss_head", grid=(t // tm,),
        in_specs=[tok, tok, pl.BlockSpec((1, d), lambda i: (0, 0))],
        out_specs=[tok, pl.BlockSpec((8, d), lambda i: (0, 0))],
        out_shape=[jax.ShapeDtypeStruct((t, d), F32), jax.ShapeDtypeStruct((8, d), F32)],
        compiler_params=_params("arbitrary"),
    )(x, target, g)


def _row_tile(rows):
    if rows <= 256:
        return rows
    for cand in range(256, 15, -16):
        if rows % cand == 0:
            return cand
    return rows


def _adamw(w, parts, m, v, name):
    rows, cols = w.shape
    n_parts = parts.shape[0]
    tr = _row_tile(rows)
    c1 = 1.0 / (1.0 - ADAM_B1 ** ADAM_STEP)
    c2 = 1.0 / (1.0 - ADAM_B2 ** ADAM_STEP)

    def body(w_ref, p_ref, m_ref, v_ref, g_ref, d_ref, nm_ref, nv_ref):
        g = p_ref[0].astype(F32)
        for i in range(1, n_parts):
            g = g + p_ref[i].astype(F32)
        nm = ADAM_B1 * m_ref[...] + (1.0 - ADAM_B1) * g
        nv = ADAM_B2 * v_ref[...] + (1.0 - ADAM_B2) * (g * g)
        g_ref[...] = g
        nm_ref[...] = nm
        nv_ref[...] = nv
        d_ref[...] = -ADAM_LR * ((nm * c1) / (jnp.sqrt(nv * c2) + ADAM_EPS) + ADAM_WD * w_ref[...])

    blk = pl.BlockSpec((tr, cols), lambda i: (i, 0))
    out = jax.ShapeDtypeStruct((rows, cols), F32)
    return pl.pallas_call(
        body, name=name, grid=(rows // tr,),
        in_specs=[blk, pl.BlockSpec((n_parts, tr, cols), lambda i: (0, i, 0)), blk, blk],
        out_specs=[blk, blk, blk, blk], out_shape=[out, out, out, out],
        compiler_params=_params("arbitrary"),
    )(w, parts, m, v)


def _t5_bucket(n):
    max_exact = N_BUCKETS // 2
    nf = np.maximum(n, 1).astype(np.float32)
    large = max_exact + (np.log(nf / max_exact) / math.log(MAX_DISTANCE / max_exact)
                         * (N_BUCKETS - max_exact)).astype(np.int32)
    large = np.minimum(large, N_BUCKETS - 1)
    return np.where(n < max_exact, n, large).astype(np.int32)


def _bucket_onehot():
    table = np.zeros((len(DILATIONS), 2 * DIL_BLOCK + 1, N_BUCKETS), np.float32)
    for i, dil in enumerate(DILATIONS):
        buckets = _t5_bucket(np.arange(DIL_BLOCK + 1) * dil)
        for m in range(DIL_BLOCK + 1):
            table[i, m, buckets[DIL_BLOCK - m]] = 1.0
    return table


def _bias_blocks(rel_bias):
    row = jnp.einsum("cmn,nh->chm", _bucket_onehot(), rel_bias, precision=lax.Precision.HIGHEST)
    n_cfg, n_heads, width = row.shape
    tiled = jnp.tile(row, (1, 1, DIL_BLOCK))[..., :DIL_BLOCK * (width - 1)]
    return tiled.reshape(n_cfg, n_heads, DIL_BLOCK, width - 1)


def _bias_blocks_bwd(dblocks):
    n_cfg, n_heads = dblocks.shape[:2]
    width = 2 * DIL_BLOCK + 1
    flat = dblocks.reshape(n_cfg, n_heads, DIL_BLOCK * (width - 1))
    flat = jnp.pad(flat, ((0, 0), (0, 0), (0, DIL_BLOCK)))
    drow = jnp.sum(flat.reshape(n_cfg, n_heads, DIL_BLOCK, width), axis=2)
    return jnp.einsum("chm,cmn->nh", drow, _bucket_onehot(), precision=lax.Precision.HIGHEST)


def _pad_to(a, axis, size):
    pad = [(0, 0)] * a.ndim
    pad[axis] = (0, size - a.shape[axis])
    return jnp.pad(a, pad)


def _lane_pad(n):
    return -(-n // LANES) * LANES


def _local_step(x, target, mod, gains, weights, rel_bias, tm, distributed):
    nb, seq, d = x.shape
    t = nb * seq
    g_ffn1, g_mix, g_sb, g_dil, g_ffn2, g_final = gains
    wg1, wu1, wd1 = weights[:3]
    x0 = x.reshape(t, d)
    ds = g_sb.shape[1]
    bias = _bias_blocks(rel_bias)

    def beside(arrays, scatter):
        return _Exchange(arrays, scatter) if distributed else None

    tp, tg = min(PROJ_TILE, seq), min(GRAD_TILE, t)

    (x1, f1, gate1, up1), got = _ffn_fwd(x0, mod, g_ffn1, wg1, wu1, wd1, 0, tp, beside(weights[3:5], False))
    win, wout = got if distributed else weights[3:5]
    wout2 = wout.reshape(-1, d)
    qkv, qkvd, h2 = _qkv_fwd(x1, mod, g_mix, win, tp)
    (osb, csb), got = _sb_fwd(qkv, nb, seq, beside(weights[5:7], False))
    wg2, wu2 = got if distributed else weights[5:7]
    ocs, lses = [], []
    for i, dil in enumerate(DILATIONS):
        (oc, lse), got = _dil_fwd(qkvd[i], bias[i], nb, seq, dil, beside(weights[7:8], False) if i == 0 else None)
        if i == 0:
            wd2 = got[0] if distributed else weights[7]
        ocs.append(oc)
        lses.append(lse)
    x2, on, mix, odil, ldil = _mix_out_fwd(osb, ocs, lses, g_sb, g_dil, wout2, x1, mod, tm)
    (x3, f3, gate3, up3), _ = _ffn_fwd(x2, mod, g_ffn2, wg2, wu2, wd2, 2, tp)
    dx3, head = _loss_head(x3, target.reshape(t, d), g_final, tm)
    loss_sum = 0.5 * jnp.sum(head[0]) / d
    dg_final = head[1:2]

    (dx2, dgate3, dup3, act3, h3, df3, dmod3, dg_ffn2), _ = _ffn_bwd(
        dx3, x2, f3, mod, g_ffn2, gate3, up3, wg2, wu2, wd2, 2, tp)
    gwg2, gwu2, gwd2 = _ffn_weight_grads(h3, dgate3, dup3, act3, df3, tg, 2)

    dm, dosb, dodil, dldil, dmod2b, dg_heads = _mix_out_bwd(
        dx2, mix, mod, wout2, osb, odil, g_sb, g_dil, tm)
    n_out = wout.shape[0]
    gwout = _mm_tn(on, dm,
                   pl.BlockSpec((tg, wout.shape[1]), lambda i, j: (i, j)),
                   pl.BlockSpec((tg, d), lambda i, j: (i, 0)),
                   wout.shape, t // tg, "grad_wout")

    (dq_sb, dk_sb, dv_sb), parts_late = _sb_bwd(qkv, dosb, csb, nb, seq,
                                                beside([gwout, gwg2, gwu2, gwd2], True))
    dil_grads = [_dil_bwd(qkvd[i], bias[i], dodil[i], ldil[i], dldil[i], nb, seq, dil)
                 for i, dil in enumerate(DILATIONS)]
    dqkv = _merge_dqkv([dq_sb, dk_sb, dv_sb], [[g[k] for g in dil_grads] for k in range(3)], nb, tm)
    drel = _bias_blocks_bwd(jnp.stack([g[3] for g in dil_grads]))

    cs = win.shape[2]
    gwin = _mm_tn(h2, dqkv,
                  pl.BlockSpec((tg, d), lambda i, j: (i, 0)),
                  pl.BlockSpec((tg, cs), lambda i, j: (i, j)),
                  win.shape, t // tg, "grad_win")
    (dx1, dmod2a, dg_mix), parts_mid = _qkv_bwd(dqkv, dx2, x1, mod, g_mix, win, tp, beside([gwin], True))

    (dx0, dgate1, dup1, act1, h1, df1, dmod1, dg_ffn1), _ = _ffn_bwd(
        dx1, x0, f1, mod, g_ffn1, gate1, up1, wg1, wu1, wd1, 0, tp)
    gw1 = _ffn_weight_grads(h1, dgate1, dup1, act1, df1, tg, 0, stream=distributed)

    dmod = jnp.concatenate([dmod1[:, 0:3], dmod2a[:, 0:2], dmod2b[:, 2:3], dmod3[:, 0:3]], axis=1)
    wgrads = tuple(gw1) + (tuple(parts_mid + parts_late) if distributed else (gwin, gwout, gwg2, gwu2, gwd2))
    ggrads = (dg_ffn1[0:1], dg_mix[0:1], dg_heads[0:1], drel, dg_ffn2[0:1], dg_final)
    return loss_sum, dx0.reshape(nb, seq, d), wgrads, dmod, ggrads


def kernel(x, c, w_ada, b_ada, g_ffn1, w1_gate, w1_up, w1_down, g_mix, w_in, g_sb_out, g_dil_out, w_out, rel_bias, g_ffn2, w2_gate, w2_up, w2_down, g_final, loss_target, m_w_ada, m_b_ada, m_g_ffn1, m_w1_gate, m_w1_up, m_w1_down, m_g_mix, m_w_in, m_g_sb_out, m_g_dil_out, m_w_out, m_rel_bias, m_g_ffn2, m_w2_gate, m_w2_up, m_w2_down, m_g_final, v_w_ada, v_b_ada, v_g_ffn1, v_w1_gate, v_w1_up, v_w1_down, v_g_mix, v_w_in, v_g_sb_out, v_g_dil_out, v_w_out, v_rel_bias, v_g_ffn2, v_w2_gate, v_w2_up, v_w2_down, v_g_final):
    nb, seq, d = x.shape
    me = 4 * lax.axis_index("x") + 2 * lax.axis_index("y") + lax.axis_index("c")
    tm = min(TOKEN_TILE, seq)
    fs = w1_gate.shape[2]
    fs_pad = _lane_pad(fs)
    ada_cols = w_ada.shape[2]

    def col_shard(w):
        return _pad_to(w[0].astype(BF16), 1, fs_pad)

    def row_shard(w):
        return _pad_to(w[0].astype(BF16), 0, fs_pad)

    shards = [col_shard(w1_gate), col_shard(w1_up), row_shard(w1_down), w_in[0].astype(BF16),
              w_out[0].astype(BF16), col_shard(w2_gate), col_shard(w2_up), row_shard(w2_down)]
    b_cols = lax.dynamic_slice(b_ada, (0, me * ada_cols), (1, ada_cols))
    c_every, mod_all, first = _first_exchange(_pad_to(c, 0, 8), shards[:3], w_ada[0], b_cols)
    c_all = c_every[:, :nb].reshape(N_DEV * nb, d)
    weights = first + shards[3:]
    mod = lax.dynamic_slice(mod_all, (0, me * 8, 0), (N_DEV, nb, ada_cols))
    mod = mod.transpose(1, 0, 2).reshape(nb, N_MOD, d)

    n_sb = g_sb_out.shape[1] * g_sb_out.shape[2]
    gains = (g_ffn1, g_mix, g_sb_out.reshape(1, n_sb), g_dil_out.reshape(1, -1), g_ffn2,
             g_final.reshape(1, d))
    loss_sum, grad_x, parts, dmod, ggrads = _local_step(x, loss_target, mod, gains, weights, rel_bias, tm, True)
    loss = lax.psum(loss_sum, ("x", "y", "c"))

    dg_ffn1, dg_mix, dg_heads, drel, dg_ffn2, dg_final = ggrads
    width = max(d, dg_heads.shape[1], drel.size)
    small = jnp.concatenate(
        [_pad_to(a.reshape(1, -1), 1, width) for a in (dg_ffn1, dg_mix, dg_ffn2, dg_final, dg_heads, drel)]
        + [jnp.zeros((2, width), F32)], axis=0)
    dmod_pad = _pad_to(dmod.reshape(nb, N_MOD * d), 0, 8)
    last_part, dmod_all, small_all = _exchange(
        [parts[2], jnp.broadcast_to(dmod_pad, (N_DEV,) + dmod_pad.shape),
         jnp.broadcast_to(small, (N_DEV,) + small.shape)], True, "scatter_last")
    parts = parts[:2] + (last_part,) + parts[3:]
    dmod_all = dmod_all[:, :nb].reshape(N_DEV * nb, N_MOD * d)
    dmod_cols = lax.dynamic_slice(dmod_all, (0, me * ada_cols), (N_DEV * nb, ada_cols))
    gw_ada, gb_ada = _ada_bwd(c_all, dmod_cols, dmod_all)

    def small_part(row, size, shape):
        return small_all[:, row, :size].reshape((N_DEV,) + shape)

    n_rel = rel_bias.shape
    updates = {
        "w_ada": (w_ada[0], gw_ada[None], m_w_ada[0], v_w_ada[0]),
        "b_ada": (b_ada, gb_ada[None], m_b_ada, v_b_ada),
        "g_ffn1": (g_ffn1, small_part(0, d, (1, d)), m_g_ffn1, v_g_ffn1),
        "w1_gate": (w1_gate[0], parts[0][:, :, :fs], m_w1_gate[0], v_w1_gate[0]),
        "w1_up": (w1_up[0], parts[1][:, :, :fs], m_w1_up[0], v_w1_up[0]),
        "w1_down": (w1_down[0], parts[2][:, :fs, :], m_w1_down[0], v_w1_down[0]),
        "g_mix": (g_mix, small_part(1, d, (1, d)), m_g_mix, v_g_mix),
        "w_in": (w_in[0], parts[3], m_w_in[0], v_w_in[0]),
        "g_sb_out": (g_sb_out[0], small_all[:, 4, :n_sb].reshape((N_DEV,) + g_sb_out.shape[1:]),
                     m_g_sb_out[0], v_g_sb_out[0]),
        "g_dil_out": (g_dil_out[0], small_all[:, 4, n_sb:dg_heads.shape[1]].reshape((N_DEV,) + g_dil_out.shape[1:]),
                      m_g_dil_out[0], v_g_dil_out[0]),
        "w_out": (w_out[0], parts[4], m_w_out[0], v_w_out[0]),
        "rel_bias": (rel_bias, small_part(5, drel.size, n_rel), m_rel_bias, v_rel_bias),
        "g_ffn2": (g_ffn2, small_part(2, d, (1, d)), m_g_ffn2, v_g_ffn2),
        "w2_gate": (w2_gate[0], parts[5][:, :, :fs], m_w2_gate[0], v_w2_gate[0]),
        "w2_up": (w2_up[0], parts[6][:, :, :fs], m_w2_up[0], v_w2_up[0]),
        "w2_down": (w2_down[0], parts[7][:, :fs, :], m_w2_down[0], v_w2_down[0]),
        "g_final": (g_final.reshape(1, d), small_part(3, d, (1, d)), m_g_final.reshape(1, d), v_g_final.reshape(1, d)),
    }
    shapes = {"w_ada": w_ada.shape, "b_ada": b_ada.shape, "g_ffn1": g_ffn1.shape, "w1_gate": w1_gate.shape,
              "w1_up": w1_up.shape, "w1_down": w1_down.shape, "g_mix": g_mix.shape, "w_in": w_in.shape,
              "g_sb_out": g_sb_out.shape, "g_dil_out": g_dil_out.shape, "w_out": w_out.shape,
              "rel_bias": rel_bias.shape, "g_ffn2": g_ffn2.shape, "w2_gate": w2_gate.shape,
              "w2_up": w2_up.shape, "w2_down": w2_down.shape, "g_final": g_final.shape}
    grads, deltas, new_m, new_v = [], [], [], []
    for name, (w, p, m, v) in updates.items():
        g, dw, nm, nv = _adamw(w, p, m, v, f"adamw_{name}")
        grads.append(g.reshape(shapes[name]))
        deltas.append(dw.reshape(shapes[name]))
        new_m.append(nm.reshape(shapes[name]))
        new_v.append(nv.reshape(shapes[name]))
    return (loss, grad_x, *grads, *deltas, *new_m, *new_v)
```

```python
import functools
import math

import numpy as np
import jax
import jax.numpy as jnp
from jax import lax
from jax.experimental import pallas as pl
from jax.experimental.pallas import tpu as pltpu

F32 = jnp.float32
BF16 = jnp.bfloat16

EPS = 1e-6
NEG_INF = -1e30
HEAD_DIM = 64
LANES = 128
DIL_BLOCK = 128
DILATIONS = (1, 4, 16)
N_BUCKETS = 32
MAX_DISTANCE = 2048
N_MOD = 9
N_DEV = 8
SB_BLOCK = 256
SB_HEADS = 4
SB_WIDTH = SB_HEADS * HEAD_DIM
DIL_HEADS = 4
DIL_WIDTH = DIL_HEADS * HEAD_DIM
TOKEN_TILE = 512
PROJ_TILE = 1024
GRAD_TILE = 1024
FFN_CHUNKS = 2
VMEM_LIMIT_BYTES = 56 * 1024 * 1024

ADAM_LR = 0.001
ADAM_B1 = 0.9
ADAM_B2 = 0.999
ADAM_EPS = 1e-08
ADAM_WD = 0.01
ADAM_STEP = 10

NT_DIMS = (((1,), (1,)), ((), ()))
TN_DIMS = (((0,), (0,)), ((), ()))


def _params(*sem):
    return pltpu.CompilerParams(dimension_semantics=sem, vmem_limit_bytes=VMEM_LIMIT_BYTES)


def _once(spec):
    return pl.BlockSpec(spec.block_shape, spec.index_map, pipeline_mode=pl.Buffered(1))


def _dot(a, b):
    return jnp.dot(a, b, preferred_element_type=F32)


def _dot_nt(a, b):
    return lax.dot_general(a, b, NT_DIMS, preferred_element_type=F32)


def _dot_tn(a, b):
    return lax.dot_general(a, b, TN_DIMS, preferred_element_type=F32)


def _split_dot(a, b):
    hi = a.astype(BF16)
    lo = (a - hi.astype(F32)).astype(BF16)
    return _dot(hi, b) + _dot(lo, b)


def _sigmoid(z):
    return 1.0 / (1.0 + jnp.exp(-z))


def _norm(x):
    r = lax.rsqrt(jnp.mean(x * x, axis=-1, keepdims=True) + EPS)
    return x * r, r


def _modulate(x, g, mod_ref, k):
    n, _ = _norm(x)
    shift = mod_ref[3 * k:3 * k + 1, :]
    scale = mod_ref[3 * k + 1:3 * k + 2, :]
    return n * g * (1.0 + scale) + shift


def _modulate_bwd(dh, x, g, mod_ref, k):
    n, r = _norm(x)
    scale = mod_ref[3 * k + 1:3 * k + 2, :]
    dshift = jnp.sum(dh, axis=0, keepdims=True)
    dscale = jnp.sum(dh * n * g, axis=0, keepdims=True)
    dg = jnp.sum(dh * n * (1.0 + scale), axis=0, keepdims=True)
    dn = dh * g * (1.0 + scale)
    dx = r * (dn - n * jnp.mean(dn * n, axis=-1, keepdims=True))
    return dx, dshift, dscale, dg


class _Exchange:
    def __init__(self, arrays, scatter, relay=False):
        assert not (scatter and relay)
        self.arrays = list(arrays)
        self.scatter = scatter
        self.relay = relay
        self.n = len(self.arrays)
        self.out_shape = [
            jax.ShapeDtypeStruct((N_DEV,) + tuple(a.shape[1:] if scatter else a.shape), a.dtype)
            for a in self.arrays]
        n_remote = self.n * (N_DEV - 1)
        self.scratch_shapes = [pltpu.SemaphoreType.DMA((n_remote,)), pltpu.SemaphoreType.DMA((n_remote,)),
                               pltpu.SemaphoreType.DMA((self.n,))]

    def _copies(self, in_refs, out_refs, sems):
        send_sems, recv_sems, local_sems = sems
        x, y, c = lax.axis_index("x"), lax.axis_index("y"), lax.axis_index("c")
        me = 4 * x + 2 * y + c
        local, remote, relayed = [], {}, {}
        for a in range(self.n):
            src = in_refs[a].at[me] if self.scatter else in_refs[a]
            local.append(pltpu.make_async_copy(src, out_refs[a].at[me], local_sems.at[a]))
            for k in range(1, N_DEV):
                px = 1 - x if k & 4 else x
                py = 1 - y if k & 2 else y
                pc = 1 - c if k & 1 else c
                sem = a * (N_DEV - 1) + k - 1
                if self.relay and k & 1 and k > 1:
                    slot = 4 * px + 2 * py + c
                    relayed[a, k] = pltpu.make_async_remote_copy(
                        src_ref=out_refs[a].at[slot], dst_ref=out_refs[a].at[slot],
                        send_sem=send_sems.at[sem], recv_sem=recv_sems.at[sem],
                        device_id=(x, y, 1 - c), device_id_type=pl.DeviceIdType.MESH)
                    continue
                src = in_refs[a].at[4 * px + 2 * py + pc] if self.scatter else in_refs[a]
                remote[a, k] = pltpu.make_async_remote_copy(
                    src_ref=src, dst_ref=out_refs[a].at[me],
                    send_sem=send_sems.at[sem], recv_sem=recv_sems.at[sem],
                    device_id=(px, py, pc), device_id_type=pl.DeviceIdType.MESH)
        return local, remote, relayed

    def start(self, in_refs, out_refs, sems):
        local, remote, _ = self._copies(in_refs, out_refs, sems)
        for cp in local + list(remote.values()):
            cp.start()

    def wait(self, in_refs, out_refs, sems):
        local, remote, relayed = self._copies(in_refs, out_refs, sems)
        for (a, k), cp in relayed.items():
            remote[a, k - 1].wait_recv()
            cp.start()
        for (a, k), cp in remote.items():
            if (a, k + 1) not in relayed:
                cp.wait_recv()
        for cp in relayed.values():
            cp.wait_recv()
        for cp in list(remote.values()) + list(relayed.values()):
            cp.wait_send()
        for cp in local:
            cp.wait()


def _call(body, *, name, args, in_specs, out_specs, out_shape, scratch_shapes=(), grid=(),
          params=None, exchange=None):
    n_in, n_out = len(args), len(out_shape)
    if exchange is None:
        outs = pl.pallas_call(
            body, name=name, grid=grid, in_specs=list(in_specs), out_specs=list(out_specs),
            out_shape=list(out_shape), scratch_shapes=list(scratch_shapes), compiler_params=params,
        )(*args)
        return list(outs), []
    n_ex = exchange.n

    def wrapped(*refs):
        ins, refs = refs[:n_in], refs[n_in:]
        ex_in, refs = refs[:n_ex], refs[n_ex:]
        outs, refs = refs[:n_out], refs[n_out:]
        ex_out, refs = refs[:n_ex], refs[n_ex:]
        scratch, sems = refs[:len(refs) - 3], refs[len(refs) - 3:]
        if not grid:
            exchange.start(ex_in, ex_out, sems)
            body(*ins, *outs, *scratch)
            exchange.wait(ex_in, ex_out, sems)
            return
        first = functools.reduce(jnp.logical_and, [pl.program_id(a) == 0 for a in range(len(grid))])
        last = functools.reduce(jnp.logical_and, [pl.program_id(a) == grid[a] - 1 for a in range(len(grid))])

        @pl.when(first)
        def _():
            exchange.start(ex_in, ex_out, sems)

        body(*ins, *outs, *scratch)

        @pl.when(last)
        def _():
            exchange.wait(ex_in, ex_out, sems)

    any_spec = pl.BlockSpec(memory_space=pl.ANY)
    outs = pl.pallas_call(
        wrapped, name=name, grid=grid,
        in_specs=list(in_specs) + [any_spec] * n_ex, out_specs=list(out_specs) + [any_spec] * n_ex,
        out_shape=list(out_shape) + exchange.out_shape,
        scratch_shapes=list(scratch_shapes) + exchange.scratch_shapes, compiler_params=params,
    )(*args, *exchange.arrays)
    return list(outs[:n_out]), list(outs[n_out:])


def _exchange(arrays, scatter, name, relay=False):
    return _call(lambda: None, name=name, args=(), in_specs=(), out_specs=(), out_shape=(),
                 exchange=_Exchange(arrays, scatter, relay))[1]


def _first_exchange(c_pad, shards, w, b):
    rows, d = c_pad.shape
    cols = w.shape[1]
    ex_c = _Exchange([c_pad], False)
    ex_w = _Exchange(shards, False, relay=True)
    ex_m = _Exchange([jax.ShapeDtypeStruct((N_DEV * rows, cols), F32)], False)
    n_w = ex_w.n

    def body(*refs):
        c_ref, w_refs, wa_ref, b_ref = refs[0], refs[1:1 + n_w], refs[1 + n_w], refs[2 + n_w]
        outs = refs[3 + n_w:]
        cg_ref, wg_refs, mg_ref = outs[0], outs[1:1 + n_w], outs[1 + n_w]
        scratch = outs[2 + n_w:]
        sems_c, sems_w, sems_m, c_vm, m_vm = scratch[0:3], scratch[3:6], scratch[6:9], scratch[9], scratch[10]
        ex_c.start([c_ref], [cg_ref], sems_c)
        ex_c.wait([c_ref], [cg_ref], sems_c)
        pltpu.sync_copy(cg_ref, c_vm)
        cv = c_vm[...].reshape(N_DEV * rows, d)
        s = (cv * _sigmoid(cv)).astype(BF16)
        m_vm[...] = _dot(s, wa_ref[...].astype(BF16)) + b_ref[...]
        ex_m.start([m_vm], [mg_ref], sems_m)
        ex_w.start(w_refs, wg_refs, sems_w)
        ex_m.wait([m_vm], [mg_ref], sems_m)
        ex_w.wait(w_refs, wg_refs, sems_w)

    any_spec = pl.BlockSpec(memory_space=pl.ANY)
    vmem_spec = pl.BlockSpec(memory_space=pltpu.VMEM)
    outs = pl.pallas_call(
        body, name="first_exchange",
        in_specs=[any_spec] * (1 + n_w) + [vmem_spec, vmem_spec],
        out_specs=[any_spec] * (2 + n_w),
        out_shape=ex_c.out_shape + ex_w.out_shape + ex_m.out_shape,
        scratch_shapes=ex_c.scratch_shapes + ex_w.scratch_shapes + ex_m.scratch_shapes
        + [pltpu.VMEM((N_DEV, rows, d), F32), pltpu.VMEM((N_DEV * rows, cols), F32)],
        compiler_params=pltpu.CompilerParams(vmem_limit_bytes=VMEM_LIMIT_BYTES),
    )(c_pad, *shards, w, b)
    return outs[0], outs[1 + n_w], list(outs[1:1 + n_w])


def _ada_bwd(c_all, dmod_cols, dmod_all):
    def body(c_ref, dc_ref, da_ref, gw_ref, gb_ref):
        cv = c_ref[...]
        s = cv * _sigmoid(cv)
        gw_ref[...] = lax.dot_general(s, dc_ref[...], TN_DIMS, preferred_element_type=F32,
                                      precision=lax.Precision.HIGHEST)
        gb_ref[...] = jnp.sum(da_ref[...], axis=0, keepdims=True)

    return pl.pallas_call(
        body, name="ada_bwd",
        out_shape=(jax.ShapeDtypeStruct((c_all.shape[1], dmod_cols.shape[1]), F32),
                   jax.ShapeDtypeStruct((1, dmod_all.shape[1]), F32)),
        compiler_params=pltpu.CompilerParams(vmem_limit_bytes=VMEM_LIMIT_BYTES),
    )(c_all, dmod_cols, dmod_all)


def _ffn_fwd(x, mod, g, wg, wu, wd, k, tm, exchange=None):
    t, d = x.shape
    ns, _, fs = wg.shape
    nt = t // tm
    tpb = nt // mod.shape[0]
    rows = tm // FFN_CHUNKS

    def body(x_ref, mod_ref, g_ref, wg_ref, wu_ref, wd_ref, xo_ref, f_ref, gg_ref, uu_ref, h_sc, acc):
        j = pl.program_id(1)

        @pl.when(j == 0)
        def _():
            h_sc[...] = _modulate(x_ref[...], g_ref[...], mod_ref, k).astype(BF16)
            acc[...] = jnp.zeros_like(acc)

        chunks = [pl.ds(c * rows, rows) for c in range(FFN_CHUNKS)]
        wg, wu, wd = wg_ref[...], wu_ref[...], wd_ref[...]
        gates, ups = [], []
        for rs in chunks:
            h = h_sc[rs, :]
            gates.append(_dot(h, wg))
            ups.append(_dot(h, wu))
        acts = [(g * _sigmoid(g) * u).astype(BF16) for g, u in zip(gates, ups)]
        for rs, g, u in zip(chunks, gates, ups):
            gg_ref[rs, :] = g.astype(BF16)
            uu_ref[rs, :] = u.astype(BF16)
        downs = [_dot(a, wd) for a in acts]
        for rs, dn in zip(chunks, downs):
            acc[rs, :] += dn

        @pl.when(j == ns - 1)
        def _():
            f = acc[...]
            f_ref[...] = f.astype(BF16)
            xo_ref[...] = x_ref[...] + 0.5 * mod_ref[3 * k + 2:3 * k + 3, :] * f

    tok = pl.BlockSpec((tm, d), lambda i, j: (i, 0))
    hid = pl.BlockSpec((None, tm, fs), lambda i, j: (j, i, 0))
    return _call(
        body, name=f"ffn_fwd{k}", grid=(nt, ns), args=(x, mod, g, wg, wu, wd),
        in_specs=[tok,
                  pl.BlockSpec((None, N_MOD, d), lambda i, j: (i // tpb, 0, 0)),
                  pl.BlockSpec((1, d), lambda i, j: (0, 0)),
                  pl.BlockSpec((None, d, fs), lambda i, j: (j, 0, 0)),
                  pl.BlockSpec((None, d, fs), lambda i, j: (j, 0, 0)),
                  pl.BlockSpec((None, fs, d), lambda i, j: (j, 0, 0))],
        out_specs=[tok, tok, hid, hid],
        out_shape=[jax.ShapeDtypeStruct((t, d), F32), jax.ShapeDtypeStruct((t, d), BF16),
                   jax.ShapeDtypeStruct((ns, t, fs), BF16), jax.ShapeDtypeStruct((ns, t, fs), BF16)],
        scratch_shapes=[pltpu.VMEM((tm, d), BF16), pltpu.VMEM((tm, d), F32)],
        params=_params("arbitrary", "arbitrary"), exchange=exchange)


def _ffn_bwd(dxo, x, f, mod, g, gate, up, wg, wu, wd, k, tm, exchange=None):
    t, d = x.shape
    ns, _, fs = wg.shape
    nt = t // tm
    nb = mod.shape[0]
    tpb = nt // nb
    rows = tm // FFN_CHUNKS

    def body(dxo_ref, x_ref, f_ref, mod_ref, g_ref, gg_ref, uu_ref, wg_ref, wu_ref, wd_ref,
             dx_ref, dgg_ref, duu_ref, act_ref, h_ref, df_ref, dmod_ref, dg_ref, acc):
        i, j = pl.program_id(0), pl.program_id(1)

        @pl.when(j == 0)
        def _():
            df = 0.5 * mod_ref[3 * k + 2:3 * k + 3, :] * dxo_ref[...]
            df_ref[...] = df.astype(BF16)
            h_ref[...] = _modulate(x_ref[...], g_ref[...], mod_ref, k).astype(BF16)
            acc[...] = jnp.zeros_like(acc)

        chunks = [pl.ds(c * rows, rows) for c in range(FFN_CHUNKS)]
        wg, wu, wd = wg_ref[...], wu_ref[...], wd_ref[...]
        dacts = [_dot_nt(df_ref[rs, :], wd) for rs in chunks]
        dgates, dups = [], []
        for rs, dact in zip(chunks, dacts):
            gv, uv = gg_ref[rs, :].astype(F32), uu_ref[rs, :].astype(F32)
            sig = _sigmoid(gv)
            s = gv * sig
            act_ref[rs, :] = (s * uv).astype(BF16)
            dups.append((dact * s).astype(BF16))
            dgates.append((dact * uv * (sig * (1.0 + gv * (1.0 - sig)))).astype(BF16))
        dhs = [_dot_nt(dg, wg) + _dot_nt(du, wu) for dg, du in zip(dgates, dups)]
        for rs, dg, du, dh in zip(chunks, dgates, dups, dhs):
            dgg_ref[rs, :] = dg
            duu_ref[rs, :] = du
            acc[rs, :] += dh

        @pl.when(j == ns - 1)
        def _():
            dx, dshift, dscale, dg = _modulate_bwd(acc[...], x_ref[...], g_ref[...], mod_ref, k)
            dxo_v = dxo_ref[...]
            dx_ref[...] = dxo_v + dx
            dgt = jnp.sum(0.5 * f_ref[...].astype(F32) * dxo_v, axis=0, keepdims=True)

            @pl.when(i % tpb == 0)
            def _():
                dmod_ref[...] = jnp.zeros_like(dmod_ref)

            @pl.when(i == 0)
            def _():
                dg_ref[...] = jnp.zeros_like(dg_ref)

            dmod_ref[0:1, :] += dshift
            dmod_ref[1:2, :] += dscale
            dmod_ref[2:3, :] += dgt
            dg_ref[0:1, :] += dg

    tok = pl.BlockSpec((tm, d), lambda i, j: (i, 0))
    hid = pl.BlockSpec((None, tm, fs), lambda i, j: (j, i, 0))
    return _call(
        body, name=f"ffn_bwd{k}", grid=(nt, ns), args=(dxo, x, f, mod, g, gate, up, wg, wu, wd),
        in_specs=[tok, _once(tok), _once(tok),
                  pl.BlockSpec((None, N_MOD, d), lambda i, j: (i // tpb, 0, 0)),
                  pl.BlockSpec((1, d), lambda i, j: (0, 0)),
                  hid, hid,
                  pl.BlockSpec((None, d, fs), lambda i, j: (j, 0, 0)),
                  pl.BlockSpec((None, d, fs), lambda i, j: (j, 0, 0)),
                  pl.BlockSpec((None, fs, d), lambda i, j: (j, 0, 0))],
        out_specs=[tok, hid, hid, hid, tok, tok,
                   pl.BlockSpec((None, 8, d), lambda i, j: (i // tpb, 0, 0)),
                   pl.BlockSpec((8, d), lambda i, j: (0, 0))],
        out_shape=[jax.ShapeDtypeStruct((t, d), F32),
                   jax.ShapeDtypeStruct((ns, t, fs), BF16), jax.ShapeDtypeStruct((ns, t, fs), BF16),
                   jax.ShapeDtypeStruct((ns, t, fs), BF16),
                   jax.ShapeDtypeStruct((t, d), BF16), jax.ShapeDtypeStruct((t, d), BF16),
                   jax.ShapeDtypeStruct((nb, 8, d), F32), jax.ShapeDtypeStruct((8, d), F32)],
        scratch_shapes=[pltpu.VMEM((tm, d), F32)],
        params=_params("arbitrary", "arbitrary"), exchange=exchange)


def _mm_tn(a, b, a_spec, b_spec, out_shape, n_tiles, name, exchange=None, keep_transposed=False):
    n_out = out_shape[0]
    block = tuple(out_shape[1:])
    last = n_tiles - 1
    flip = block[0] > block[1]
    if flip:
        block = block[::-1]
    if flip and keep_transposed:
        flip_back, out_shape = False, (n_out,) + block
    else:
        flip_back = flip

    def body(a_ref, b_ref, o_ref, acc):
        i, j = pl.program_id(0), pl.program_id(1)
        prod = _dot_tn(b_ref[...], a_ref[...]) if flip else _dot_tn(a_ref[...], b_ref[...])

        @pl.when(i == 0)
        def _():
            acc[j] = prod

        @pl.when(i > 0)
        def _():
            acc[j] += prod

        @pl.when(i == last)
        def _():
            total = acc[j]
            o_ref[j] = (total.T if flip_back else total).astype(o_ref.dtype)

    outs, sent = _call(
        body, name=name, grid=(n_tiles, n_out), args=(a, b), in_specs=[a_spec, b_spec],
        out_specs=[pl.BlockSpec(out_shape, lambda i, j: (0,) * len(out_shape))],
        out_shape=[jax.ShapeDtypeStruct(out_shape, BF16)],
        scratch_shapes=[pltpu.VMEM((n_out,) + block, F32)],
        params=_params("arbitrary", "arbitrary"), exchange=exchange)
    return (outs[0], sent) if exchange is not None else outs[0]


def _ffn_weight_grads(h, dgate, dup, act, df, tm, tag, stream=False):
    t, d = h.shape
    ns, _, fs = dgate.shape
    nt = t // tm
    tok = pl.BlockSpec((tm, d), lambda i, j: (i, 0))
    hid = pl.BlockSpec((None, tm, fs), lambda i, j: (j, i, 0))
    gwg = _mm_tn(h, dgate, tok, hid, (ns, d, fs), nt, f"grad_wg{tag}", keep_transposed=True)
    if not stream:
        gwu = _mm_tn(h, dup, tok, hid, (ns, d, fs), nt, f"grad_wu{tag}", keep_transposed=True)
        gwd = _mm_tn(act, df, hid, tok, (ns, fs, d), nt, f"grad_wd{tag}")
        return gwg, gwu, gwd
    gwu, sent_g = _mm_tn(h, dup, tok, hid, (ns, d, fs), nt, f"grad_wu{tag}", _Exchange([gwg], True),
                         keep_transposed=True)
    gwd, sent_u = _mm_tn(act, df, hid, tok, (ns, fs, d), nt, f"grad_wd{tag}", _Exchange([gwu], True))
    return sent_g[0], sent_u[0], gwd


def _stage_shape(rows, cols):
    return pltpu.VMEM((cols // LANES, rows, LANES), F32)


def _stage(value, stage_ref):
    for k in range(stage_ref.shape[0]):
        stage_ref[k] = value[:, k * LANES:(k + 1) * LANES]


def _to_residue_rows(stage_ref, dst_ref, dil):
    rows = stage_ref.shape[1] // dil
    for r in range(dil):
        for k in range(stage_ref.shape[0]):
            dst_ref[r, :, k * LANES:(k + 1) * LANES] = (
                stage_ref.at[k][pl.ds(r, rows, stride=dil), :].astype(dst_ref.dtype))


def _from_residue_rows(src_ref, stage_ref, dil):
    rows = stage_ref.shape[1] // dil
    chunks = range(stage_ref.shape[0])
    for r in range(dil):
        for k in chunks:
            stage_ref.at[k][pl.ds(r, rows, stride=dil), :] = src_ref[r, :, k * LANES:(k + 1) * LANES].astype(F32)
    return jnp.concatenate([stage_ref[k] for k in chunks], axis=1)


def _residue_shape(nb, seq, width, dil, dtype):
    return jax.ShapeDtypeStruct((nb, dil, seq // dil, width), dtype)


def _residue_spec(tm, tpb, cols, dil, col_block):
    return pl.BlockSpec((None, dil, tm // dil, cols),
                        lambda i, *rest: (i // tpb, 0, i % tpb, col_block(i, *rest)))


def _qkv_fwd(x, mod, g, win, tm):
    t, d = x.shape
    ns, _, cs = win.shape
    nt = t // tm
    nb = mod.shape[0]
    tpb = nt // nb
    seq = t // nb
    half = ns // 2
    n_res = len(DILATIONS) - 1

    def body(x_ref, mod_ref, g_ref, w_ref, sb_ref, dil_ref, *rest):
        res_refs, h_ref, sc = rest[:n_res], rest[n_res], rest[n_res + 1]
        j = pl.program_id(1)

        @pl.when(j == 0)
        def _():
            h_ref[...] = _modulate(x_ref[...], g_ref[...], mod_ref, 1).astype(BF16)

        res = _dot(h_ref[...], w_ref[...])

        @pl.when(j < half)
        def _():
            sb_ref[...] = res.astype(BF16)

        @pl.when(j >= half)
        def _():
            dil_ref[...] = res.astype(BF16)
            _stage(res, sc)
            for ref, dil in zip(res_refs, DILATIONS[1:]):
                _to_residue_rows(sc, ref, dil)

    def dil_col(i, j):
        return jnp.maximum(j - half, 0)

    tok = pl.BlockSpec((tm, d), lambda i, j: (i, 0))
    wide = jax.ShapeDtypeStruct((t, half * cs), BF16)
    outs = pl.pallas_call(
        body, name="qkv_fwd", grid=(nt, ns),
        in_specs=[tok,
                  pl.BlockSpec((None, N_MOD, d), lambda i, j: (i // tpb, 0, 0)),
                  pl.BlockSpec((1, d), lambda i, j: (0, 0)),
                  pl.BlockSpec((None, d, cs), lambda i, j: (j, 0, 0))],
        out_specs=[pl.BlockSpec((tm, cs), lambda i, j: (i, jnp.minimum(j, half - 1))),
                   pl.BlockSpec((tm, cs), lambda i, j: (i, dil_col(i, j)))]
        + [_residue_spec(tm, tpb, cs, dil, dil_col) for dil in DILATIONS[1:]] + [tok],
        out_shape=[wide, wide] + [_residue_shape(nb, seq, half * cs, dil, BF16) for dil in DILATIONS[1:]]
        + [jax.ShapeDtypeStruct((t, d), BF16)],
        scratch_shapes=[_stage_shape(tm, cs)],
        compiler_params=_params("arbitrary", "arbitrary"),
    )(x, mod, g, win)
    qkv_dil = [outs[1]] + [a.reshape(t, half * cs) for a in outs[2:2 + n_res]]
    return outs[0], qkv_dil, outs[-1]


def _qkv_bwd(dqkv, dxo, x, mod, g, win, tm, exchange=None):
    t, d = x.shape
    ns, _, cs = win.shape
    nt = t // tm
    nb = mod.shape[0]
    tpb = nt // nb

    def body(dq_ref, dxo_ref, x_ref, mod_ref, g_ref, w_ref, dx_ref, dmod_ref, dg_ref, acc):
        i, j = pl.program_id(0), pl.program_id(1)

        @pl.when(j == 0)
        def _():
            acc[...] = jnp.zeros_like(acc)

        acc[...] += _dot_nt(dq_ref[...], w_ref[...])

        @pl.when(j == ns - 1)
        def _():
            dx, dshift, dscale, dg = _modulate_bwd(acc[...], x_ref[...], g_ref[...], mod_ref, 1)
            dx_ref[...] = dxo_ref[...] + dx

            @pl.when(i % tpb == 0)
            def _():
                dmod_ref[...] = jnp.zeros_like(dmod_ref)

            @pl.when(i == 0)
            def _():
                dg_ref[...] = jnp.zeros_like(dg_ref)

            dmod_ref[0:1, :] += dshift
            dmod_ref[1:2, :] += dscale
            dg_ref[0:1, :] += dg

    tok = pl.BlockSpec((tm, d), lambda i, j: (i, 0))
    return _call(
        body, name="qkv_bwd", grid=(nt, ns), args=(dqkv, dxo, x, mod, g, win),
        in_specs=[pl.BlockSpec((tm, cs), lambda i, j: (i, j)), tok, tok,
                  pl.BlockSpec((None, N_MOD, d), lambda i, j: (i // tpb, 0, 0)),
                  pl.BlockSpec((1, d), lambda i, j: (0, 0)),
                  pl.BlockSpec((None, d, cs), lambda i, j: (j, 0, 0))],
        out_specs=[tok,
                   pl.BlockSpec((None, 8, d), lambda i, j: (i // tpb, 0, 0)),
                   pl.BlockSpec((8, d), lambda i, j: (0, 0))],
        out_shape=[jax.ShapeDtypeStruct((t, d), F32),
                   jax.ShapeDtypeStruct((nb, 8, d), F32), jax.ShapeDtypeStruct((8, d), F32)],
        scratch_shapes=[pltpu.VMEM((tm, d), F32)],
        params=_params("arbitrary", "arbitrary"), exchange=exchange)


def _heads(a):
    return [a[:, h * HEAD_DIM:(h + 1) * HEAD_DIM] for h in range(a.shape[1] // HEAD_DIM)]


def _own_lanes():
    lane = lax.broadcasted_iota(jnp.int32, (1, LANES), 1)
    return [lane < HEAD_DIM, lane >= HEAD_DIM]


def _pair_tiles(a):
    return [a[:, (h // 2) * LANES:(h // 2 + 1) * LANES] for h in range(a.shape[1] // HEAD_DIM)]


def _own_tiles(a, own):
    return [jnp.where(own[h % 2], tile, jnp.zeros_like(tile)) for h, tile in enumerate(_pair_tiles(a))]


def _merge_tiles(per_head, own):
    return jnp.concatenate([jnp.where(own[0], per_head[h], per_head[h + 1])
                            for h in range(0, len(per_head), 2)], axis=1)


def _scaled(q):
    return (q.astype(F32) * (HEAD_DIM ** -0.5)).astype(BF16)


def _sb_logits(qh, kh, tri, causal):
    zs = [_dot_nt(q, k) for q, k in zip(qh, kh)]
    es = [jnp.exp(-jnp.abs(z)) for z in zs]
    log_nots = [-(jnp.maximum(z, 0.0) + jnp.log(1.0 + e)) for z, e in zip(zs, es)]
    if causal is not None:
        log_nots = [jnp.where(causal, ln, 0.0) for ln in log_nots]
    return zs, es, [_split_dot(ln, tri) for ln in log_nots]


def _sb_masks():
    rows = lax.broadcasted_iota(jnp.int32, (SB_BLOCK, SB_BLOCK), 0)
    cols = lax.broadcasted_iota(jnp.int32, (SB_BLOCK, SB_BLOCK), 1)
    return (rows >= cols).astype(BF16), (rows <= cols).astype(BF16), cols < rows


def _sb_fwd(qkv, nb, seq, exchange=None):
    t = qkv.shape[0]
    n_pairs = (qkv.shape[1] // 3) // SB_WIDTH
    tb = SB_BLOCK
    n_blk = seq // tb

    def body(q_ref, k_ref, v_ref, o_ref, c_ref):
        tri, _, causal = _sb_masks()
        own = _own_lanes()

        def key_block(qh, kj, carry, mask):
            ks = pl.multiple_of(kj * tb, tb)
            kh, vh = _pair_tiles(k_ref[pl.ds(ks, tb), :]), _pair_tiles(v_ref[pl.ds(ks, tb), :])
            zs, _, suffixes = _sb_logits(qh, kh, tri, mask)
            ws = [jnp.exp(z + suffix + cr[1]) for z, suffix, cr in zip(zs, suffixes, carry)]
            if mask is not None:
                ws = [jnp.where(mask, w, 0.0) for w in ws]
            pv = [_dot(w.astype(BF16), v) for w, v in zip(ws, vh)]
            return tuple((cr[0] + p, cr[1] + suffix[:, 0:1]) for cr, p, suffix in zip(carry, pv, suffixes))

        def query_block(qi, _):
            qs = pl.multiple_of(qi * tb, tb)
            qh = _own_tiles(_scaled(q_ref[pl.ds(qs, tb), :]), own)
            zero = (jnp.zeros((tb, LANES), F32), jnp.zeros((tb, 1), F32))
            carry = key_block(qh, qi, (zero,) * SB_HEADS, causal)
            carry = lax.fori_loop(0, qi, lambda it, cr: key_block(qh, qi - 1 - it, cr, None), carry)
            o_ref[pl.ds(qs, tb), :] = _merge_tiles([cr[0] for cr in carry], own)
            c_ref[pl.ds(qs, tb), :] = _merge_tiles([jnp.broadcast_to(cr[1], (tb, LANES)) for cr in carry], own)
            return 0

        lax.fori_loop(0, n_blk, query_block, 0)

    def spec(offset):
        return pl.BlockSpec((seq, SB_WIDTH), lambda b, p: (b, offset + p))

    out = jax.ShapeDtypeStruct((t, n_pairs * SB_WIDTH), F32)
    return _call(
        body, name="sb_fwd", grid=(nb, n_pairs), args=(qkv, qkv, qkv),
        in_specs=[spec(0), spec(n_pairs), spec(2 * n_pairs)],
        out_specs=[spec(0), spec(0)], out_shape=[out, out],
        params=_params("arbitrary", "arbitrary"), exchange=exchange)


def _sb_bwd(qkv, do, csum, nb, seq, exchange=None):
    t = qkv.shape[0]
    n_pairs = (qkv.shape[1] // 3) // SB_WIDTH
    tb = SB_BLOCK
    n_blk = seq // tb
    scale = HEAD_DIM ** -0.5

    def body(q_ref, k_ref, v_ref, do_ref, c_ref, dq_ref, dk_ref, dv_ref, dkt_acc, dvt_acc):
        tri, tri_prefix, causal = _sb_masks()
        own = _own_lanes()
        dkt_acc[...] = jnp.zeros_like(dkt_acc)
        dvt_acc[...] = jnp.zeros_like(dvt_acc)

        def key_block(qh, qth, doh, doth, ch, kj, carry, mask):
            ks = pl.multiple_of(kj * tb, tb)
            kh, vh = _pair_tiles(k_ref[pl.ds(ks, tb), :]), _pair_tiles(v_ref[pl.ds(ks, tb), :])
            heads = range(SB_HEADS)
            zs, es, suffixes = _sb_logits(qh, kh, tri, mask)
            dws = [_dot_nt(doh[h], vh[h]) for h in heads]
            lefts = [carry[h][1] + suffixes[h][:, 0:1] for h in heads]
            ws = [jnp.exp(zs[h] + suffixes[h] + (ch[h] - lefts[h])) for h in heads]
            if mask is not None:
                ws = [jnp.where(mask, w, 0.0) for w in ws]
            dlws = [ws[h] * dws[h] for h in heads]
            dprefixes = [_split_dot(dlw, tri_prefix) for dlw in dlws]
            dvts = [_dot(doth[h], ws[h].astype(BF16)) for h in heads]
            dzbs = []
            for h in heads:
                sig = jnp.where(zs[h] >= 0.0, 1.0, es[h]) * pl.reciprocal(1.0 + es[h], approx=True)
                dz = dlws[h] - sig * (carry[h][2] + dprefixes[h])
                if mask is not None:
                    dz = jnp.where(mask, dz, 0.0)
                dzbs.append(dz.astype(BF16))
            dkts = [_dot(qth[h], dzbs[h]) for h in heads]
            dqs = [_dot(dzbs[h], kh[h]) for h in heads]
            dkt_acc[:, pl.ds(ks, tb)] += jnp.concatenate([dkts[h] + dkts[h + 1] for h in heads[::2]], axis=0)
            dvt_acc[:, pl.ds(ks, tb)] += jnp.concatenate([dvts[h] + dvts[h + 1] for h in heads[::2]], axis=0)
            return tuple((carry[h][0] + dqs[h], lefts[h], carry[h][2] + dprefixes[h][:, tb - 1:tb])
                         for h in heads)

        def query_block(qi, _):
            qs = pl.multiple_of(qi * tb, tb)
            qh = _own_tiles(_scaled(q_ref[pl.ds(qs, tb), :]), own)
            doh = _own_tiles(do_ref[pl.ds(qs, tb), :], own)
            qth = [a.astype(F32).T.astype(BF16) for a in qh]
            doth = [a.T.astype(BF16) for a in doh]
            doh = [a.astype(BF16) for a in doh]
            cv = c_ref[pl.ds(qs, tb), :]
            ch = [cv[:, h * HEAD_DIM:h * HEAD_DIM + 1] for h in range(SB_HEADS)]
            zero = (jnp.zeros((tb, LANES), F32), jnp.zeros((tb, 1), F32), jnp.zeros((tb, 1), F32))
            carry = lax.fori_loop(
                0, qi, lambda kj, cr: key_block(qh, qth, doh, doth, ch, kj, cr, None), (zero,) * SB_HEADS)
            carry = key_block(qh, qth, doh, doth, ch, qi, carry, causal)
            dq = _merge_tiles([cr[0] for cr in carry], own) * scale
            dq_ref[pl.ds(qs, tb), :] = dq.astype(BF16)
            return 0

        lax.fori_loop(0, n_blk, query_block, 0)
        dk_ref[...] = dkt_acc[...].T.astype(BF16)
        dv_ref[...] = dvt_acc[...].T.astype(BF16)

    def spec(offset):
        return pl.BlockSpec((seq, SB_WIDTH), lambda b, p: (b, offset + p))

    out = jax.ShapeDtypeStruct((t, n_pairs * SB_WIDTH), BF16)
    return _call(
        body, name="sb_bwd", grid=(nb, n_pairs), args=(qkv, qkv, qkv, do, csum),
        in_specs=[spec(0), spec(n_pairs), spec(2 * n_pairs), spec(0), spec(0)],
        out_specs=[spec(0), spec(0), spec(0)],
        out_shape=[out, out, out],
        scratch_shapes=[pltpu.VMEM((SB_WIDTH, seq), F32), pltpu.VMEM((SB_WIDTH, seq), F32)],
        params=_params("arbitrary", "arbitrary"), exchange=exchange)


def _dil_block_scores(qh, kph, kch, bias_ref, has_prev, band_prev, band_cur):
    scale = HEAD_DIM ** -0.5
    heads = range(len(qh))
    no_prev = jnp.where(has_prev, 0.0, NEG_INF)
    zps = [_dot_nt(qh[h], kph[h]) for h in heads]
    zcs = [_dot_nt(qh[h], kch[h]) for h in heads]
    zps = [jnp.where(band_prev, zps[h] * scale + bias_ref[h, :, 0:DIL_BLOCK], NEG_INF) + no_prev for h in heads]
    zcs = [jnp.where(band_cur, zcs[h] * scale + bias_ref[h, :, DIL_BLOCK:2 * DIL_BLOCK], NEG_INF) for h in heads]
    return zps, zcs


def _dil_bands():
    rows = lax.broadcasted_iota(jnp.int32, (DIL_BLOCK, DIL_BLOCK), 0)
    cols = lax.broadcasted_iota(jnp.int32, (DIL_BLOCK, DIL_BLOCK), 1)
    return cols >= rows, cols <= rows


def _dil_fwd(qkv, bias, nb, seq, dil, exchange=None):
    t, width = qkv.shape
    n_pairs = (width // 3) // DIL_WIDTH
    bq = DIL_BLOCK
    n_blk = seq // bq
    per_seq = n_blk // dil
    heads = range(DIL_HEADS)

    def body(q_ref, k_ref, v_ref, bias_ref, o_ref, lse_ref):
        band_prev, band_cur = _dil_bands()
        own = _own_lanes()

        def block(n, _):
            has_prev = (n & (per_seq - 1)) != 0
            qs = pl.multiple_of(n * bq, bq)
            ps = pl.multiple_of(jnp.maximum(n - 1, 0) * bq, bq)
            qh = _own_tiles(q_ref[pl.ds(qs, bq), :], own)
            kp, kc = _pair_tiles(k_ref[pl.ds(ps, bq), :]), _pair_tiles(k_ref[pl.ds(qs, bq), :])
            vp, vc = _pair_tiles(v_ref[pl.ds(ps, bq), :]), _pair_tiles(v_ref[pl.ds(qs, bq), :])
            zps, zcs = _dil_block_scores(qh, kp, kc, bias_ref, has_prev, band_prev, band_cur)
            ms = [jnp.maximum(jnp.max(zps[h], axis=1, keepdims=True), jnp.max(zcs[h], axis=1, keepdims=True))
                  for h in heads]
            eps = [jnp.exp(zps[h] - ms[h]) for h in heads]
            ecs = [jnp.exp(zcs[h] - ms[h]) for h in heads]
            pvs = [_dot(eps[h].astype(BF16), vp[h]) + _dot(ecs[h].astype(BF16), vc[h]) for h in heads]
            dens = [jnp.sum(eps[h], axis=1, keepdims=True) + jnp.sum(ecs[h], axis=1, keepdims=True) for h in heads]
            o_ref[pl.ds(qs, bq), :] = _merge_tiles([pvs[h] / dens[h] for h in heads], own)
            lse_ref[pl.ds(qs, bq), :] = _merge_tiles(
                [jnp.broadcast_to(ms[h] + jnp.log(dens[h]), (bq, LANES)) for h in heads], own)
            return 0

        lax.fori_loop(0, n_blk, block, 0)

    def spec(offset):
        return pl.BlockSpec((seq, DIL_WIDTH), lambda b, p: (b, offset + p))

    out = jax.ShapeDtypeStruct((t, n_pairs * DIL_WIDTH), F32)
    return _call(
        body, name=f"dil_fwd{dil}", grid=(nb, n_pairs), args=(qkv, qkv, qkv, bias),
        in_specs=[spec(0), spec(n_pairs), spec(2 * n_pairs),
                  pl.BlockSpec((DIL_HEADS, bq, 2 * bq), lambda b, p: (p, 0, 0))],
        out_specs=[spec(0), spec(0)], out_shape=[out, out],
        params=_params("arbitrary", "arbitrary"), exchange=exchange)


def _dil_bwd(qkv, bias, do, lse, delta, nb, seq, dil):
    t, width = qkv.shape
    n_pairs = (width // 3) // DIL_WIDTH
    bq = DIL_BLOCK
    n_blk = seq // bq
    per_seq = n_blk // dil
    scale = HEAD_DIM ** -0.5
    heads = range(DIL_HEADS)

    def body(q_ref, k_ref, v_ref, bias_ref, do_ref, lse_ref, dl_ref, dq_ref, dk_ref, dv_ref, db_ref,
             dk_acc, dv_acc):
        band_prev, band_cur = _dil_bands()
        own = _own_lanes()
        dk_acc[...] = jnp.zeros_like(dk_acc)
        dv_acc[...] = jnp.zeros_like(dv_acc)

        @pl.when(pl.program_id(1) == 0)
        def _():
            db_ref[...] = jnp.zeros_like(db_ref)

        def block(n, _):
            has_prev = (n & (per_seq - 1)) != 0
            qs = pl.multiple_of(n * bq, bq)
            ps = pl.multiple_of(jnp.maximum(n - 1, 0) * bq, bq)
            qh = _own_tiles(q_ref[pl.ds(qs, bq), :], own)
            kp, kc = _pair_tiles(k_ref[pl.ds(ps, bq), :]), _pair_tiles(k_ref[pl.ds(qs, bq), :])
            vp, vc = _pair_tiles(v_ref[pl.ds(ps, bq), :]), _pair_tiles(v_ref[pl.ds(qs, bq), :])
            doh = _own_tiles(do_ref[pl.ds(qs, bq), :].astype(BF16), own)
            lse_v, dl_v = lse_ref[pl.ds(qs, bq), :], dl_ref[pl.ds(qs, bq), :]
            zps, zcs = _dil_block_scores(qh, kp, kc, bias_ref, has_prev, band_prev, band_cur)
            dpp = [_dot_nt(doh[h], vp[h]) for h in heads]
            dpc = [_dot_nt(doh[h], vc[h]) for h in heads]
            lse_h = [lse_v[:, h * HEAD_DIM:h * HEAD_DIM + 1] for h in heads]
            dl_h = [dl_v[:, h * HEAD_DIM:h * HEAD_DIM + 1] for h in heads]
            pps = [jnp.exp(zps[h] - lse_h[h]) for h in heads]
            pcs = [jnp.exp(zcs[h] - lse_h[h]) for h in heads]
            dvp = [_dot_tn(pps[h].astype(BF16), doh[h]) for h in heads]
            dvc = [_dot_tn(pcs[h].astype(BF16), doh[h]) for h in heads]
            dzps = [pps[h] * (dpp[h] - dl_h[h]) for h in heads]
            dzcs = [pcs[h] * (dpc[h] - dl_h[h]) for h in heads]
            dzp_b = [(dzps[h] * scale).astype(BF16) for h in heads]
            dzc_b = [(dzcs[h] * scale).astype(BF16) for h in heads]
            dqs = [_dot(dzp_b[h], kp[h]) + _dot(dzc_b[h], kc[h]) for h in heads]
            dkp = [_dot_tn(dzp_b[h], qh[h]) for h in heads]
            dkc = [_dot_tn(dzc_b[h], qh[h]) for h in heads]
            for h in heads:
                db_ref[h, :, 0:bq] += dzps[h]
                db_ref[h, :, bq:2 * bq] += dzcs[h]
            def pair_sums(per_head):
                return jnp.concatenate([per_head[h] + per_head[h + 1] for h in heads[::2]], axis=1)

            dq_ref[pl.ds(qs, bq), :] = _merge_tiles(dqs, own).astype(BF16)
            dk_acc[pl.ds(ps, bq), :] += pair_sums(dkp)
            dk_acc[pl.ds(qs, bq), :] += pair_sums(dkc)
            dv_acc[pl.ds(ps, bq), :] += pair_sums(dvp)
            dv_acc[pl.ds(qs, bq), :] += pair_sums(dvc)
            return 0

        lax.fori_loop(0, n_blk, block, 0)
        dk_ref[...] = dk_acc[...].astype(BF16)
        dv_ref[...] = dv_acc[...].astype(BF16)

    def spec(offset):
        return pl.BlockSpec((seq, DIL_WIDTH), lambda p, b: (b, offset + p))

    bias_spec = pl.BlockSpec((DIL_HEADS, bq, 2 * bq), lambda p, b: (p, 0, 0))
    out = jax.ShapeDtypeStruct((t, n_pairs * DIL_WIDTH), BF16)
    return pl.pallas_call(
        body, name=f"dil_bwd{dil}", grid=(n_pairs, nb),
        in_specs=[spec(0), spec(n_pairs), spec(2 * n_pairs), bias_spec, spec(0), spec(0), spec(0)],
        out_specs=[spec(0), spec(0), spec(0), bias_spec],
        out_shape=[out, out, out, jax.ShapeDtypeStruct(bias.shape, F32)],
        scratch_shapes=[pltpu.VMEM((seq, DIL_WIDTH), F32), pltpu.VMEM((seq, DIL_WIDTH), F32)],
        compiler_params=_params("arbitrary", "arbitrary"),
    )(qkv, qkv, qkv, bias, do, lse, delta)


def _head_blocks(width):
    rows = lax.broadcasted_iota(jnp.int32, (width, width), 0) // HEAD_DIM
    cols = lax.broadcasted_iota(jnp.int32, (width, width), 1) // HEAD_DIM
    return (rows == cols).astype(BF16)


def _head_mean(v, gmat):
    return _split_dot(v, gmat) * (1.0 / HEAD_DIM)


def _residue_views(arrays, nb, seq):
    return [a if dil == 1 else a.reshape(nb, dil, seq // dil, a.shape[1]) for a, dil in zip(arrays, DILATIONS)]


def _mix_out_fwd(osb, ocs, lses, gsb, gdil, wout, x, mod, tm):
    t, d = x.shape
    ds = osb.shape[1]
    nt = t // tm
    nb = mod.shape[0]
    tpb = nt // nb
    seq = t // nb
    n_cfg = len(DILATIONS)

    def body(osb_ref, *refs):
        oc_refs, lse_refs = refs[:n_cfg], refs[n_cfg:2 * n_cfg]
        gsb_ref, gdil_ref, w_ref, x_ref, mod_ref = refs[2 * n_cfg:2 * n_cfg + 5]
        xo_ref, on_ref, m_ref, odil_ref = refs[2 * n_cfg + 5:2 * n_cfg + 9]
        ld_refs = refs[2 * n_cfg + 9:3 * n_cfg + 9]
        stages, sc = refs[3 * n_cfg + 9:]
        ocv, lsev = [oc_refs[0][...]], [lse_refs[0][...]]
        for i, dil in enumerate(DILATIONS[1:]):
            ocv.append(_from_residue_rows(oc_refs[i + 1], stages.at[2 * i], dil))
            lsev.append(_from_residue_rows(lse_refs[i + 1], stages.at[2 * i + 1], dil))
        top = functools.reduce(jnp.maximum, lsev)
        total = top + jnp.log(sum(jnp.exp(l - top) for l in lsev))
        odil = sum(jnp.exp(l - total) * o for o, l in zip(ocv, lsev))
        odil_ref[...] = odil
        ld_refs[0][...] = total
        _stage(total, sc)
        for ref, dil in zip(ld_refs[1:], DILATIONS[1:]):
            _to_residue_rows(sc, ref, dil)
        gm = _head_blocks(ds)
        parts = []
        for o, g_ref in ((osb_ref[...], gsb_ref), (odil, gdil_ref)):
            parts.append(o * lax.rsqrt(_head_mean(o * o, gm) + EPS) * g_ref[...])
        on = jnp.concatenate(parts, axis=1).astype(BF16)
        on_ref[...] = on
        m = _dot(on, w_ref[...])
        m_ref[...] = m
        xo_ref[...] = x_ref[...] + mod_ref[5:6, :] * m

    tok = pl.BlockSpec((tm, d), lambda i: (i, 0))
    hd = pl.BlockSpec((tm, ds), lambda i: (i, 0))
    res = [hd] + [_residue_spec(tm, tpb, ds, dil, lambda i: 0) for dil in DILATIONS[1:]]
    res_shape = [jax.ShapeDtypeStruct((t, ds), F32)] + [_residue_shape(nb, seq, ds, dil, F32) for dil in DILATIONS[1:]]
    gain = pl.BlockSpec((1, ds), lambda i: (0, 0))
    outs = pl.pallas_call(
        body, name="mix_out_fwd", grid=(nt,),
        in_specs=[hd] + res + res + [gain, gain,
                  pl.BlockSpec(wout.shape, lambda i: (0, 0)),
                  tok, pl.BlockSpec((None, N_MOD, d), lambda i: (i // tpb, 0, 0))],
        out_specs=[tok, pl.BlockSpec((tm, 2 * ds), lambda i: (i, 0)), tok, hd] + res,
        out_shape=[jax.ShapeDtypeStruct((t, d), F32), jax.ShapeDtypeStruct((t, 2 * ds), BF16),
                   jax.ShapeDtypeStruct((t, d), F32), jax.ShapeDtypeStruct((t, ds), F32)] + res_shape,
        scratch_shapes=[pltpu.VMEM((2 * (n_cfg - 1), ds // LANES, tm, LANES), F32), _stage_shape(tm, ds)],
        compiler_params=_params("arbitrary"),
    )(osb, *_residue_views(ocs, nb, seq), *_residue_views(lses, nb, seq), gsb, gdil, wout, x, mod)
    return outs[0], outs[1], outs[2], outs[3], [a.reshape(t, ds) for a in outs[4:]]


def _mix_out_bwd(dxo, m, mod, wout, osb, odil, gsb, gdil, tm):
    t, d = dxo.shape
    ds = osb.shape[1]
    nt = t // tm
    nb = mod.shape[0]
    tpb = nt // nb
    seq = t // nb
    n_cfg = len(DILATIONS)

    def body(dxo_ref, m_ref, mod_ref, w_ref, osb_ref, odil_ref, gsb_ref, gdil_ref,
             dm_ref, dosb_ref, *rest):
        do_refs, dl_refs = rest[:n_cfg], rest[n_cfg:2 * n_cfg]
        dmod_ref, dg_ref, sc = rest[2 * n_cfg:]
        dodil_ref, dldil_ref = do_refs[0], dl_refs[0]
        i = pl.program_id(0)
        dxo_v = dxo_ref[...]
        dm = (mod_ref[5:6, :] * dxo_v).astype(BF16)
        dm_ref[...] = dm
        dgt = jnp.sum(m_ref[...] * dxo_v, axis=0, keepdims=True)
        don = _dot_nt(dm, w_ref[...])
        gm = _head_blocks(ds)

        @pl.when(i % tpb == 0)
        def _():
            dmod_ref[...] = jnp.zeros_like(dmod_ref)

        @pl.when(i == 0)
        def _():
            dg_ref[...] = jnp.zeros_like(dg_ref)

        dmod_ref[2:3, :] += dgt
        groups = ((osb_ref, gsb_ref, dosb_ref), (odil_ref, gdil_ref, dodil_ref))
        for k, (o_ref, g_ref, do_ref) in enumerate(groups):
            o = o_ref[...]
            dn_out = don[:, k * ds:(k + 1) * ds]
            r = lax.rsqrt(_head_mean(o * o, gm) + EPS)
            n = o * r
            dg_ref[0:1, k * ds:(k + 1) * ds] += jnp.sum(dn_out * n, axis=0, keepdims=True)
            dn = dn_out * g_ref[...]
            do = r * (dn - n * _head_mean(dn * n, gm))
            do_ref[...] = do
            if k == 1:
                delta = _head_mean(do * o, gm) * float(HEAD_DIM)
                dldil_ref[...] = delta
                for value, refs in ((do, do_refs), (delta, dl_refs)):
                    _stage(value, sc)
                    for ref, dil in zip(refs[1:], DILATIONS[1:]):
                        _to_residue_rows(sc, ref, dil)

    tok = pl.BlockSpec((tm, d), lambda i: (i, 0))
    hd = pl.BlockSpec((tm, ds), lambda i: (i, 0))
    res = [hd] + [_residue_spec(tm, tpb, ds, dil, lambda i: 0) for dil in DILATIONS[1:]]
    res_shape = [jax.ShapeDtypeStruct((t, ds), F32)] + [_residue_shape(nb, seq, ds, dil, F32) for dil in DILATIONS[1:]]
    gain = pl.BlockSpec((1, ds), lambda i: (0, 0))
    outs = pl.pallas_call(
        body, name="mix_out_bwd", grid=(nt,),
        in_specs=[tok, tok, pl.BlockSpec((None, N_MOD, d), lambda i: (i // tpb, 0, 0)),
                  pl.BlockSpec(wout.shape, lambda i: (0, 0)), hd, hd, gain, gain],
        out_specs=[tok, hd] + res + res
        + [pl.BlockSpec((None, 8, d), lambda i: (i // tpb, 0, 0)), pl.BlockSpec((8, 2 * ds), lambda i: (0, 0))],
        out_shape=[jax.ShapeDtypeStruct((t, d), BF16), jax.ShapeDtypeStruct((t, ds), F32)] + res_shape + res_shape
        + [jax.ShapeDtypeStruct((nb, 8, d), F32), jax.ShapeDtypeStruct((8, 2 * ds), F32)],
        scratch_shapes=[_stage_shape(tm, ds)],
        compiler_params=_params("arbitrary"),
    )(dxo, m, mod, wout, osb, odil, gsb, gdil)
    flat = [a.reshape(t, ds) for a in outs[2:2 + 2 * n_cfg]]
    return outs[0], outs[1], flat[:n_cfg], flat[n_cfg:], outs[-2], outs[-1]


def _merge_dqkv(sb_parts, dil_parts, nb, tm):
    t, ds = sb_parts[0].shape
    nt = t // tm
    tpb = nt // nb
    seq = t // nb
    n_cfg = len(DILATIONS)

    def body(*refs):
        sb_refs, dil_refs = refs[:3], refs[3:3 + 3 * n_cfg]
        o_ref, sc = refs[3 + 3 * n_cfg:]
        for k in range(3):
            o_ref[:, k * ds:(k + 1) * ds] = sb_refs[k][...]
            total = dil_refs[k * n_cfg][...].astype(F32)
            for i, dil in enumerate(DILATIONS[1:]):
                total = total + _from_residue_rows(dil_refs[k * n_cfg + i + 1], sc, dil)
            o_ref[:, (3 + k) * ds:(4 + k) * ds] = total.astype(BF16)

    hd = pl.BlockSpec((tm, ds), lambda i: (i, 0))
    res = [hd] + [_residue_spec(tm, tpb, ds, dil, lambda i: 0) for dil in DILATIONS[1:]]
    views = [v for parts in dil_parts for v in _residue_views(parts, nb, seq)]
    return pl.pallas_call(
        body, name="merge_dqkv", grid=(nt,),
        in_specs=[hd] * 3 + res * 3,
        out_specs=pl.BlockSpec((tm, 6 * ds), lambda i: (i, 0)),
        out_shape=jax.ShapeDtypeStruct((t, 6 * ds), BF16),
        scratch_shapes=[_stage_shape(tm, ds)],
        compiler_params=_params("arbitrary"),
    )(*sb_parts, *views)


def _loss_head(x, target, g, tm):
    t, d = x.shape

    def body(x_ref, t_ref, g_ref, dx_ref, acc_ref):
        @pl.when(pl.program_id(0) == 0)
        def _():
            acc_ref[...] = jnp.zeros_like(acc_ref)

        n, r = _norm(x_ref[...])
        gv = g_ref[...]
        err = n * gv - t_ref[...]
        dy = err * (1.0 / d)
        acc_ref[0:1, :] += jnp.sum(err * err, axis=0, keepdims=True)
        acc_ref[1:2, :] += jnp.sum(dy * n, axis=0, keepdims=True)
        dn = dy * gv
        dx_ref[...] = r * (dn - n * jnp.mean(dn * n, axis=-1, keepdims=True))

    tok = pl.BlockSpec((tm, d), lambda i: (i, 0))
    return pl.pallas_call(
        body, name="loss_head", grid=(t // tm,),
        in_specs=[tok, tok, pl.BlockSpec((1, d), lambda i: (0, 0))],
        out_specs=[tok, pl.BlockSpec((8, d), lambda i: (0, 0))],
        out_shape=[jax.ShapeDtypeStruct((t, d), F32), jax.ShapeDtypeStruct((8, d), F32)],
        compiler_params=_params("arbitrary"),
    )(x, target, g)


def _row_tile(rows):
    if rows <= 256:
        return rows
    for cand in range(256, 15, -16):
        if rows % cand == 0:
            return cand
    return rows


def _adamw(w, parts, m, v, name, transposed=False):
    rows, cols = w.shape
    n_parts = parts.shape[0]
    tr = _row_tile(rows)
    c1 = 1.0 / (1.0 - ADAM_B1 ** ADAM_STEP)
    c2 = 1.0 / (1.0 - ADAM_B2 ** ADAM_STEP)

    def body(w_ref, p_ref, m_ref, v_ref, g_ref, d_ref, nm_ref, nv_ref):
        g = p_ref[0].astype(F32)
        for i in range(1, n_parts):
            g = g + p_ref[i].astype(F32)
        wv, mv, vv = w_ref[...], m_ref[...], v_ref[...]
        if transposed:
            wv, mv, vv = wv.T, mv.T, vv.T
        nm = ADAM_B1 * mv + (1.0 - ADAM_B1) * g
        nv = ADAM_B2 * vv + (1.0 - ADAM_B2) * (g * g)
        g_ref[...] = g
        nm_ref[...] = nm
        nv_ref[...] = nv
        d_ref[...] = -ADAM_LR * ((nm * c1) / (jnp.sqrt(nv * c2) + ADAM_EPS) + ADAM_WD * wv)

    blk = pl.BlockSpec((tr, cols), lambda i: (i, 0))
    if transposed:
        oblk = pl.BlockSpec((cols, tr), lambda i: (0, i))
        pblk = pl.BlockSpec((n_parts, cols, tr), lambda i: (0, 0, i))
        out = jax.ShapeDtypeStruct((cols, rows), F32)
    else:
        oblk, pblk = blk, pl.BlockSpec((n_parts, tr, cols), lambda i: (0, i, 0))
        out = jax.ShapeDtypeStruct((rows, cols), F32)
    return pl.pallas_call(
        body, name=name, grid=(rows // tr,),
        in_specs=[blk, pblk, blk, blk],
        out_specs=[oblk, oblk, oblk, oblk], out_shape=[out, out, out, out],
        compiler_params=_params("arbitrary"),
    )(w, parts, m, v)


def _t5_bucket(n):
    max_exact = N_BUCKETS // 2
    nf = np.maximum(n, 1).astype(np.float32)
    large = max_exact + (np.log(nf / max_exact) / math.log(MAX_DISTANCE / max_exact)
                         * (N_BUCKETS - max_exact)).astype(np.int32)
    large = np.minimum(large, N_BUCKETS - 1)
    return np.where(n < max_exact, n, large).astype(np.int32)


def _bucket_onehot():
    table = np.zeros((len(DILATIONS), 2 * DIL_BLOCK + 1, N_BUCKETS), np.float32)
    for i, dil in enumerate(DILATIONS):
        buckets = _t5_bucket(np.arange(DIL_BLOCK + 1) * dil)
        for m in range(DIL_BLOCK + 1):
            table[i, m, buckets[DIL_BLOCK - m]] = 1.0
    return table


def _bias_blocks(rel_bias):
    row = jnp.einsum("cmn,nh->chm", _bucket_onehot(), rel_bias, precision=lax.Precision.HIGHEST)
    n_cfg, n_heads, width = row.shape
    tiled = jnp.tile(row, (1, 1, DIL_BLOCK))[..., :DIL_BLOCK * (width - 1)]
    return tiled.reshape(n_cfg, n_heads, DIL_BLOCK, width - 1)


def _bias_blocks_bwd(dblocks):
    n_cfg, n_heads = dblocks.shape[:2]
    width = 2 * DIL_BLOCK + 1
    flat = dblocks.reshape(n_cfg, n_heads, DIL_BLOCK * (width - 1))
    flat = jnp.pad(flat, ((0, 0), (0, 0), (0, DIL_BLOCK)))
    drow = jnp.sum(flat.reshape(n_cfg, n_heads, DIL_BLOCK, width), axis=2)
    return jnp.einsum("chm,cmn->nh", drow, _bucket_onehot(), precision=lax.Precision.HIGHEST)


def _pad_to(a, axis, size):
    pad = [(0, 0)] * a.ndim
    pad[axis] = (0, size - a.shape[axis])
    return jnp.pad(a, pad)


def _lane_pad(n):
    return -(-n // LANES) * LANES


def _local_step(x, target, mod, gains, weights, rel_bias, tm, distributed):
    nb, seq, d = x.shape
    t = nb * seq
    g_ffn1, g_mix, g_sb, g_dil, g_ffn2, g_final = gains
    wg1, wu1, wd1 = weights[:3]
    x0 = x.reshape(t, d)
    ds = g_sb.shape[1]
    bias = _bias_blocks(rel_bias)

    def beside(arrays, scatter):
        return _Exchange(arrays, scatter) if distributed else None

    tp, tg = min(PROJ_TILE, seq), min(GRAD_TILE, t)

    (x1, f1, gate1, up1), got = _ffn_fwd(x0, mod, g_ffn1, wg1, wu1, wd1, 0, tp, beside(weights[3:5], False))
    win, wout = got if distributed else weights[3:5]
    wout2 = wout.reshape(-1, d)
    qkv, qkvd, h2 = _qkv_fwd(x1, mod, g_mix, win, tp)
    (osb, csb), got = _sb_fwd(qkv, nb, seq, beside(weights[5:7], False))
    wg2, wu2 = got if distributed else weights[5:7]
    ocs, lses = [], []
    for i, dil in enumerate(DILATIONS):
        (oc, lse), got = _dil_fwd(qkvd[i], bias[i], nb, seq, dil, beside(weights[7:8], False) if i == 0 else None)
        if i == 0:
            wd2 = got[0] if distributed else weights[7]
        ocs.append(oc)
        lses.append(lse)
    x2, on, mix, odil, ldil = _mix_out_fwd(osb, ocs, lses, g_sb, g_dil, wout2, x1, mod, tm)
    (x3, f3, gate3, up3), _ = _ffn_fwd(x2, mod, g_ffn2, wg2, wu2, wd2, 2, tp)
    dx3, head = _loss_head(x3, target.reshape(t, d), g_final, tm)
    loss_sum = 0.5 * jnp.sum(head[0]) / d
    dg_final = head[1:2]

    (dx2, dgate3, dup3, act3, h3, df3, dmod3, dg_ffn2), _ = _ffn_bwd(
        dx3, x2, f3, mod, g_ffn2, gate3, up3, wg2, wu2, wd2, 2, tp)
    gwg2, gwu2, gwd2 = _ffn_weight_grads(h3, dgate3, dup3, act3, df3, tg, 2)

    dm, dosb, dodil, dldil, dmod2b, dg_heads = _mix_out_bwd(
        dx2, mix, mod, wout2, osb, odil, g_sb, g_dil, tm)
    n_out = wout.shape[0]
    gwout = _mm_tn(on, dm,
                   pl.BlockSpec((tg, wout.shape[1]), lambda i, j: (i, j)),
                   pl.BlockSpec((tg, d), lambda i, j: (i, 0)),
                   wout.shape, t // tg, "grad_wout")

    (dq_sb, dk_sb, dv_sb), parts_late = _sb_bwd(qkv, dosb, csb, nb, seq,
                                                beside([gwout, gwg2, gwu2, gwd2], True))
    dil_grads = [_dil_bwd(qkvd[i], bias[i], dodil[i], ldil[i], dldil[i], nb, seq, dil)
                 for i, dil in enumerate(DILATIONS)]
    dqkv = _merge_dqkv([dq_sb, dk_sb, dv_sb], [[g[k] for g in dil_grads] for k in range(3)], nb, tm)
    drel = _bias_blocks_bwd(jnp.stack([g[3] for g in dil_grads]))

    cs = win.shape[2]
    gwin = _mm_tn(h2, dqkv,
                  pl.BlockSpec((tg, d), lambda i, j: (i, 0)),
                  pl.BlockSpec((tg, cs), lambda i, j: (i, j)),
                  win.shape, t // tg, "grad_win")
    (dx1, dmod2a, dg_mix), parts_mid = _qkv_bwd(dqkv, dx2, x1, mod, g_mix, win, tp, beside([gwin], True))

    (dx0, dgate1, dup1, act1, h1, df1, dmod1, dg_ffn1), _ = _ffn_bwd(
        dx1, x0, f1, mod, g_ffn1, gate1, up1, wg1, wu1, wd1, 0, tp)
    gw1 = _ffn_weight_grads(h1, dgate1, dup1, act1, df1, tg, 0, stream=distributed)

    dmod = jnp.concatenate([dmod1[:, 0:3], dmod2a[:, 0:2], dmod2b[:, 2:3], dmod3[:, 0:3]], axis=1)
    wgrads = tuple(gw1) + (tuple(parts_mid + parts_late) if distributed else (gwin, gwout, gwg2, gwu2, gwd2))
    ggrads = (dg_ffn1[0:1], dg_mix[0:1], dg_heads[0:1], drel, dg_ffn2[0:1], dg_final)
    return loss_sum, dx0.reshape(nb, seq, d), wgrads, dmod, ggrads


def kernel(x, c, w_ada, b_ada, g_ffn1, w1_gate, w1_up, w1_down, g_mix, w_in, g_sb_out, g_dil_out, w_out, rel_bias, g_ffn2, w2_gate, w2_up, w2_down, g_final, loss_target, m_w_ada, m_b_ada, m_g_ffn1, m_w1_gate, m_w1_up, m_w1_down, m_g_mix, m_w_in, m_g_sb_out, m_g_dil_out, m_w_out, m_rel_bias, m_g_ffn2, m_w2_gate, m_w2_up, m_w2_down, m_g_final, v_w_ada, v_b_ada, v_g_ffn1, v_w1_gate, v_w1_up, v_w1_down, v_g_mix, v_w_in, v_g_sb_out, v_g_dil_out, v_w_out, v_rel_bias, v_g_ffn2, v_w2_gate, v_w2_up, v_w2_down, v_g_final):
    nb, seq, d = x.shape
    me = 4 * lax.axis_index("x") + 2 * lax.axis_index("y") + lax.axis_index("c")
    tm = min(TOKEN_TILE, seq)
    fs = w1_gate.shape[2]
    fs_pad = _lane_pad(fs)
    ada_cols = w_ada.shape[2]

    def col_shard(w):
        return _pad_to(w[0].astype(BF16), 1, fs_pad)

    def row_shard(w):
        return _pad_to(w[0].astype(BF16), 0, fs_pad)

    shards = [col_shard(w1_gate), col_shard(w1_up), row_shard(w1_down), w_in[0].astype(BF16),
              w_out[0].astype(BF16), col_shard(w2_gate), col_shard(w2_up), row_shard(w2_down)]
    b_cols = lax.dynamic_slice(b_ada, (0, me * ada_cols), (1, ada_cols))
    c_every, mod_all, first = _first_exchange(_pad_to(c, 0, 8), shards[:3], w_ada[0], b_cols)
    c_all = c_every[:, :nb].reshape(N_DEV * nb, d)
    weights = first + shards[3:]
    mod = lax.dynamic_slice(mod_all, (0, me * 8, 0), (N_DEV, nb, ada_cols))
    mod = mod.transpose(1, 0, 2).reshape(nb, N_MOD, d)

    n_sb = g_sb_out.shape[1] * g_sb_out.shape[2]
    gains = (g_ffn1, g_mix, g_sb_out.reshape(1, n_sb), g_dil_out.reshape(1, -1), g_ffn2,
             g_final.reshape(1, d))
    loss_sum, grad_x, parts, dmod, ggrads = _local_step(x, loss_target, mod, gains, weights, rel_bias, tm, True)
    loss = lax.psum(loss_sum, ("x", "y", "c"))

    dg_ffn1, dg_mix, dg_heads, drel, dg_ffn2, dg_final = ggrads
    width = max(d, dg_heads.shape[1], drel.size)
    small = jnp.concatenate(
        [_pad_to(a.reshape(1, -1), 1, width) for a in (dg_ffn1, dg_mix, dg_ffn2, dg_final, dg_heads, drel)]
        + [jnp.zeros((2, width), F32)], axis=0)
    dmod_pad = _pad_to(dmod.reshape(nb, N_MOD * d), 0, 8)
    last_part, dmod_all, small_all = _exchange(
        [parts[2], jnp.broadcast_to(dmod_pad, (N_DEV,) + dmod_pad.shape),
         jnp.broadcast_to(small, (N_DEV,) + small.shape)], True, "scatter_last")
    parts = parts[:2] + (last_part,) + parts[3:]
    dmod_all = dmod_all[:, :nb].reshape(N_DEV * nb, N_MOD * d)
    dmod_cols = lax.dynamic_slice(dmod_all, (0, me * ada_cols), (N_DEV * nb, ada_cols))
    gw_ada, gb_ada = _ada_bwd(c_all, dmod_cols, dmod_all)

    def small_part(row, size, shape):
        return small_all[:, row, :size].reshape((N_DEV,) + shape)

    n_rel = rel_bias.shape
    updates = {
        "w_ada": (w_ada[0], gw_ada[None], m_w_ada[0], v_w_ada[0]),
        "b_ada": (b_ada, gb_ada[None], m_b_ada, v_b_ada),
        "g_ffn1": (g_ffn1, small_part(0, d, (1, d)), m_g_ffn1, v_g_ffn1),
        "w1_gate": (w1_gate[0], parts[0][:, :fs, :], m_w1_gate[0], v_w1_gate[0]),
        "w1_up": (w1_up[0], parts[1][:, :fs, :], m_w1_up[0], v_w1_up[0]),
        "w1_down": (w1_down[0], parts[2][:, :fs, :], m_w1_down[0], v_w1_down[0]),
        "g_mix": (g_mix, small_part(1, d, (1, d)), m_g_mix, v_g_mix),
        "w_in": (w_in[0], parts[3], m_w_in[0], v_w_in[0]),
        "g_sb_out": (g_sb_out[0], small_all[:, 4, :n_sb].reshape((N_DEV,) + g_sb_out.shape[1:]),
                     m_g_sb_out[0], v_g_sb_out[0]),
        "g_dil_out": (g_dil_out[0], small_all[:, 4, n_sb:dg_heads.shape[1]].reshape((N_DEV,) + g_dil_out.shape[1:]),
                      m_g_dil_out[0], v_g_dil_out[0]),
        "w_out": (w_out[0], parts[4], m_w_out[0], v_w_out[0]),
        "rel_bias": (rel_bias, small_part(5, drel.size, n_rel), m_rel_bias, v_rel_bias),
        "g_ffn2": (g_ffn2, small_part(2, d, (1, d)), m_g_ffn2, v_g_ffn2),
        "w2_gate": (w2_gate[0], parts[5][:, :fs, :], m_w2_gate[0], v_w2_gate[0]),
        "w2_up": (w2_up[0], parts[6][:, :fs, :], m_w2_up[0], v_w2_up[0]),
        "w2_down": (w2_down[0], parts[7][:, :fs, :], m_w2_down[0], v_w2_down[0]),
        "g_final": (g_final.reshape(1, d), small_part(3, d, (1, d)), m_g_final.reshape(1, d), v_g_final.reshape(1, d)),
    }
    shapes = {"w_ada": w_ada.shape, "b_ada": b_ada.shape, "g_ffn1": g_ffn1.shape, "w1_gate": w1_gate.shape,
              "w1_up": w1_up.shape, "w1_down": w1_down.shape, "g_mix": g_mix.shape, "w_in": w_in.shape,
              "g_sb_out": g_sb_out.shape, "g_dil_out": g_dil_out.shape, "w_out": w_out.shape,
              "rel_bias": rel_bias.shape, "g_ffn2": g_ffn2.shape, "w2_gate": w2_gate.shape,
              "w2_up": w2_up.shape, "w2_down": w2_down.shape, "g_final": g_final.shape}
    grads, deltas, new_m, new_v = [], [], [], []
    for name, (w, p, m, v) in updates.items():
        transposed = name in ("w1_gate", "w1_up", "w2_gate", "w2_up")
        outs = _adamw(w, p, m, v, f"adamw_{name}", transposed)
        for dst, a in zip((grads, deltas, new_m, new_v), outs):
            dst.append((a.T if transposed else a).reshape(shapes[name]))
    return (loss, grad_x, *grads, *deltas, *new_m, *new_v)
```

```python
import functools
import math

import numpy as np
import jax
import jax.numpy as jnp
from jax import lax
from jax.experimental import pallas as pl
from jax.experimental.pallas import tpu as pltpu

F32 = jnp.float32
BF16 = jnp.bfloat16

EPS = 1e-6
NEG_INF = -1e30
HEAD_DIM = 64
LANES = 128
DIL_BLOCK = 128
DILATIONS = (1, 4, 16)
N_BUCKETS = 32
MAX_DISTANCE = 2048
N_MOD = 9
N_DEV = 8
SB_BLOCK = 256
SB_HEADS = 4
SB_WIDTH = SB_HEADS * HEAD_DIM
DIL_HEADS = 4
DIL_WIDTH = DIL_HEADS * HEAD_DIM
TOKEN_TILE = 512
PROJ_TILE = 1024
GRAD_TILE = 1024
FFN_CHUNKS = 2
VMEM_LIMIT_BYTES = 56 * 1024 * 1024

ADAM_LR = 0.001
ADAM_B1 = 0.9
ADAM_B2 = 0.999
ADAM_EPS = 1e-08
ADAM_WD = 0.01
ADAM_STEP = 10

NT_DIMS = (((1,), (1,)), ((), ()))
TN_DIMS = (((0,), (0,)), ((), ()))


def _params(*sem):
    return pltpu.CompilerParams(dimension_semantics=sem, vmem_limit_bytes=VMEM_LIMIT_BYTES)


def _once(spec):
    return pl.BlockSpec(spec.block_shape, spec.index_map, pipeline_mode=pl.Buffered(1))


def _dot(a, b):
    return jnp.dot(a, b, preferred_element_type=F32)


def _dot_nt(a, b):
    return lax.dot_general(a, b, NT_DIMS, preferred_element_type=F32)


def _dot_tn(a, b):
    return lax.dot_general(a, b, TN_DIMS, preferred_element_type=F32)


def _split_dot(a, b):
    hi = a.astype(BF16)
    lo = (a - hi.astype(F32)).astype(BF16)
    return _dot(hi, b) + _dot(lo, b)


def _sigmoid(z):
    return 1.0 / (1.0 + jnp.exp(-z))


def _norm(x):
    r = lax.rsqrt(jnp.mean(x * x, axis=-1, keepdims=True) + EPS)
    return x * r, r


def _modulate(x, g, mod_ref, k):
    n, _ = _norm(x)
    shift = mod_ref[3 * k:3 * k + 1, :]
    scale = mod_ref[3 * k + 1:3 * k + 2, :]
    return n * g * (1.0 + scale) + shift


def _modulate_bwd(dh, x, g, mod_ref, k):
    n, r = _norm(x)
    scale = mod_ref[3 * k + 1:3 * k + 2, :]
    dshift = jnp.sum(dh, axis=0, keepdims=True)
    dscale = jnp.sum(dh * n * g, axis=0, keepdims=True)
    dg = jnp.sum(dh * n * (1.0 + scale), axis=0, keepdims=True)
    dn = dh * g * (1.0 + scale)
    dx = r * (dn - n * jnp.mean(dn * n, axis=-1, keepdims=True))
    return dx, dshift, dscale, dg


class _Exchange:
    def __init__(self, arrays, scatter, relay=False, chips=None):
        assert not (scatter and relay)
        self.arrays = list(arrays)
        self.scatter = scatter
        self.relay = relay
        self.n = len(self.arrays)
        self.chips = list(chips) if chips is not None else [False] * self.n
        assert scatter or not any(self.chips)
        self.out_shape = [
            jax.ShapeDtypeStruct((N_DEV // 2 if ch else N_DEV,) + tuple(a.shape[1:] if scatter else a.shape), a.dtype)
            for a, ch in zip(self.arrays, self.chips)]
        n_remote = self.n * (N_DEV - 1)
        self.scratch_shapes = [pltpu.SemaphoreType.DMA((n_remote,)), pltpu.SemaphoreType.DMA((n_remote,)),
                               pltpu.SemaphoreType.DMA((self.n,))]

    def _copies(self, in_refs, out_refs, sems):
        send_sems, recv_sems, local_sems = sems
        x, y, c = lax.axis_index("x"), lax.axis_index("y"), lax.axis_index("c")
        me = 4 * x + 2 * y + c
        local, remote, relayed = [], {}, {}
        for a in range(self.n):
            if self.chips[a]:
                mine = 2 * x + y
                local.append(pltpu.make_async_copy(in_refs[a].at[mine], out_refs[a].at[mine], local_sems.at[a]))
                for k in (2, 4, 6):
                    px = 1 - x if k & 4 else x
                    py = 1 - y if k & 2 else y
                    sem = a * (N_DEV - 1) + k - 1
                    remote[a, k] = pltpu.make_async_remote_copy(
                        src_ref=in_refs[a].at[2 * px + py], dst_ref=out_refs[a].at[mine],
                        send_sem=send_sems.at[sem], recv_sem=recv_sems.at[sem],
                        device_id=(px, py, c), device_id_type=pl.DeviceIdType.MESH)
                continue
            src = in_refs[a].at[me] if self.scatter else in_refs[a]
            local.append(pltpu.make_async_copy(src, out_refs[a].at[me], local_sems.at[a]))
            for k in range(1, N_DEV):
                px = 1 - x if k & 4 else x
                py = 1 - y if k & 2 else y
                pc = 1 - c if k & 1 else c
                sem = a * (N_DEV - 1) + k - 1
                if self.relay and k & 1 and k > 1:
                    slot = 4 * px + 2 * py + c
                    relayed[a, k] = pltpu.make_async_remote_copy(
                        src_ref=out_refs[a].at[slot], dst_ref=out_refs[a].at[slot],
                        send_sem=send_sems.at[sem], recv_sem=recv_sems.at[sem],
                        device_id=(x, y, 1 - c), device_id_type=pl.DeviceIdType.MESH)
                    continue
                src = in_refs[a].at[4 * px + 2 * py + pc] if self.scatter else in_refs[a]
                remote[a, k] = pltpu.make_async_remote_copy(
                    src_ref=src, dst_ref=out_refs[a].at[me],
                    send_sem=send_sems.at[sem], recv_sem=recv_sems.at[sem],
                    device_id=(px, py, pc), device_id_type=pl.DeviceIdType.MESH)
        return local, remote, relayed

    def start(self, in_refs, out_refs, sems):
        local, remote, _ = self._copies(in_refs, out_refs, sems)
        for cp in local + list(remote.values()):
            cp.start()

    def wait(self, in_refs, out_refs, sems):
        local, remote, relayed = self._copies(in_refs, out_refs, sems)
        for (a, k), cp in relayed.items():
            remote[a, k - 1].wait_recv()
            cp.start()
        for (a, k), cp in remote.items():
            if (a, k + 1) not in relayed:
                cp.wait_recv()
        for cp in relayed.values():
            cp.wait_recv()
        for cp in list(remote.values()) + list(relayed.values()):
            cp.wait_send()
        for cp in local:
            cp.wait()


def _call(body, *, name, args, in_specs, out_specs, out_shape, scratch_shapes=(), grid=(),
          params=None, exchange=None):
    n_in, n_out = len(args), len(out_shape)
    if exchange is None:
        outs = pl.pallas_call(
            body, name=name, grid=grid, in_specs=list(in_specs), out_specs=list(out_specs),
            out_shape=list(out_shape), scratch_shapes=list(scratch_shapes), compiler_params=params,
        )(*args)
        return list(outs), []
    n_ex = exchange.n

    def wrapped(*refs):
        ins, refs = refs[:n_in], refs[n_in:]
        ex_in, refs = refs[:n_ex], refs[n_ex:]
        outs, refs = refs[:n_out], refs[n_out:]
        ex_out, refs = refs[:n_ex], refs[n_ex:]
        scratch, sems = refs[:len(refs) - 3], refs[len(refs) - 3:]
        if not grid:
            exchange.start(ex_in, ex_out, sems)
            body(*ins, *outs, *scratch)
            exchange.wait(ex_in, ex_out, sems)
            return
        first = functools.reduce(jnp.logical_and, [pl.program_id(a) == 0 for a in range(len(grid))])
        last = functools.reduce(jnp.logical_and, [pl.program_id(a) == grid[a] - 1 for a in range(len(grid))])

        @pl.when(first)
        def _():
            exchange.start(ex_in, ex_out, sems)

        body(*ins, *outs, *scratch)

        @pl.when(last)
        def _():
            exchange.wait(ex_in, ex_out, sems)

    any_spec = pl.BlockSpec(memory_space=pl.ANY)
    outs = pl.pallas_call(
        wrapped, name=name, grid=grid,
        in_specs=list(in_specs) + [any_spec] * n_ex, out_specs=list(out_specs) + [any_spec] * n_ex,
        out_shape=list(out_shape) + exchange.out_shape,
        scratch_shapes=list(scratch_shapes) + exchange.scratch_shapes, compiler_params=params,
    )(*args, *exchange.arrays)
    return list(outs[:n_out]), list(outs[n_out:])


def _exchange(arrays, scatter, name, relay=False, chips=None):
    return _call(lambda: None, name=name, args=(), in_specs=(), out_specs=(), out_shape=(),
                 exchange=_Exchange(arrays, scatter, relay, chips))[1]


def _first_exchange(c_pad, shards, w, b):
    rows, d = c_pad.shape
    cols = w.shape[1]
    ex_c = _Exchange([c_pad], False)
    ex_w = _Exchange(shards, False, relay=True)
    ex_m = _Exchange([jax.ShapeDtypeStruct((N_DEV * rows, cols), F32)], False)
    n_w = ex_w.n

    def body(*refs):
        c_ref, w_refs, wa_ref, b_ref = refs[0], refs[1:1 + n_w], refs[1 + n_w], refs[2 + n_w]
        outs = refs[3 + n_w:]
        cg_ref, wg_refs, mg_ref = outs[0], outs[1:1 + n_w], outs[1 + n_w]
        scratch = outs[2 + n_w:]
        sems_c, sems_w, sems_m, c_vm, m_vm = scratch[0:3], scratch[3:6], scratch[6:9], scratch[9], scratch[10]
        ex_c.start([c_ref], [cg_ref], sems_c)
        ex_c.wait([c_ref], [cg_ref], sems_c)
        pltpu.sync_copy(cg_ref, c_vm)
        cv = c_vm[...].reshape(N_DEV * rows, d)
        s = (cv * _sigmoid(cv)).astype(BF16)
        m_vm[...] = _dot(s, wa_ref[...].astype(BF16)) + b_ref[...]
        ex_m.start([m_vm], [mg_ref], sems_m)
        ex_w.start(w_refs, wg_refs, sems_w)
        ex_m.wait([m_vm], [mg_ref], sems_m)
        ex_w.wait(w_refs, wg_refs, sems_w)

    any_spec = pl.BlockSpec(memory_space=pl.ANY)
    vmem_spec = pl.BlockSpec(memory_space=pltpu.VMEM)
    outs = pl.pallas_call(
        body, name="first_exchange",
        in_specs=[any_spec] * (1 + n_w) + [vmem_spec, vmem_spec],
        out_specs=[any_spec] * (2 + n_w),
        out_shape=ex_c.out_shape + ex_w.out_shape + ex_m.out_shape,
        scratch_shapes=ex_c.scratch_shapes + ex_w.scratch_shapes + ex_m.scratch_shapes
        + [pltpu.VMEM((N_DEV, rows, d), F32), pltpu.VMEM((N_DEV * rows, cols), F32)],
        compiler_params=pltpu.CompilerParams(vmem_limit_bytes=VMEM_LIMIT_BYTES),
    )(c_pad, *shards, w, b)
    return outs[0], outs[1 + n_w], list(outs[1:1 + n_w])


def _ada_bwd(c_all, dmod_cols, dmod_all):
    def body(c_ref, dc_ref, da_ref, gw_ref, gb_ref):
        cv = c_ref[...]
        s = cv * _sigmoid(cv)
        gw_ref[...] = lax.dot_general(s, dc_ref[...], TN_DIMS, preferred_element_type=F32,
                                      precision=lax.Precision.HIGHEST)
        gb_ref[...] = jnp.sum(da_ref[...], axis=0, keepdims=True)

    return pl.pallas_call(
        body, name="ada_bwd",
        out_shape=(jax.ShapeDtypeStruct((c_all.shape[1], dmod_cols.shape[1]), F32),
                   jax.ShapeDtypeStruct((1, dmod_all.shape[1]), F32)),
        compiler_params=pltpu.CompilerParams(vmem_limit_bytes=VMEM_LIMIT_BYTES),
    )(c_all, dmod_cols, dmod_all)


def _ffn_fwd(x, mod, g, wg, wu, wd, k, tm, exchange=None):
    t, d = x.shape
    ns, _, fs = wg.shape
    nt = t // tm
    tpb = nt // mod.shape[0]
    rows = tm // FFN_CHUNKS

    def body(x_ref, mod_ref, g_ref, wg_ref, wu_ref, wd_ref, xo_ref, f_ref, gg_ref, uu_ref, h_sc, acc):
        j = pl.program_id(1)

        @pl.when(j == 0)
        def _():
            h_sc[...] = _modulate(x_ref[...], g_ref[...], mod_ref, k).astype(BF16)
            acc[...] = jnp.zeros_like(acc)

        chunks = [pl.ds(c * rows, rows) for c in range(FFN_CHUNKS)]
        wg, wu, wd = wg_ref[...], wu_ref[...], wd_ref[...]
        gates, ups = [], []
        for rs in chunks:
            h = h_sc[rs, :]
            gates.append(_dot(h, wg))
            ups.append(_dot(h, wu))
        acts = [(g * _sigmoid(g) * u).astype(BF16) for g, u in zip(gates, ups)]
        for rs, g, u in zip(chunks, gates, ups):
            gg_ref[rs, :] = g.astype(BF16)
            uu_ref[rs, :] = u.astype(BF16)
        downs = [_dot(a, wd) for a in acts]
        for rs, dn in zip(chunks, downs):
            acc[rs, :] += dn

        @pl.when(j == ns - 1)
        def _():
            f = acc[...]
            f_ref[...] = f.astype(BF16)
            xo_ref[...] = x_ref[...] + 0.5 * mod_ref[3 * k + 2:3 * k + 3, :] * f

    tok = pl.BlockSpec((tm, d), lambda i, j: (i, 0))
    hid = pl.BlockSpec((None, tm, fs), lambda i, j: (j, i, 0))
    return _call(
        body, name=f"ffn_fwd{k}", grid=(nt, ns), args=(x, mod, g, wg, wu, wd),
        in_specs=[tok,
                  pl.BlockSpec((None, N_MOD, d), lambda i, j: (i // tpb, 0, 0)),
                  pl.BlockSpec((1, d), lambda i, j: (0, 0)),
                  pl.BlockSpec((None, d, fs), lambda i, j: (j, 0, 0)),
                  pl.BlockSpec((None, d, fs), lambda i, j: (j, 0, 0)),
                  pl.BlockSpec((None, fs, d), lambda i, j: (j, 0, 0))],
        out_specs=[tok, tok, hid, hid],
        out_shape=[jax.ShapeDtypeStruct((t, d), F32), jax.ShapeDtypeStruct((t, d), BF16),
                   jax.ShapeDtypeStruct((ns, t, fs), BF16), jax.ShapeDtypeStruct((ns, t, fs), BF16)],
        scratch_shapes=[pltpu.VMEM((tm, d), BF16), pltpu.VMEM((tm, d), F32)],
        params=_params("arbitrary", "arbitrary"), exchange=exchange)


def _ffn_bwd(dxo, x, f, mod, g, gate, up, wg, wu, wd, k, tm, exchange=None):
    t, d = x.shape
    ns, _, fs = wg.shape
    nt = t // tm
    nb = mod.shape[0]
    tpb = nt // nb
    rows = tm // FFN_CHUNKS

    def body(dxo_ref, x_ref, f_ref, mod_ref, g_ref, gg_ref, uu_ref, wg_ref, wu_ref, wd_ref,
             dx_ref, dgg_ref, duu_ref, act_ref, h_ref, df_ref, dmod_ref, dg_ref, acc):
        i, j = pl.program_id(0), pl.program_id(1)

        @pl.when(j == 0)
        def _():
            df = 0.5 * mod_ref[3 * k + 2:3 * k + 3, :] * dxo_ref[...]
            df_ref[...] = df.astype(BF16)
            h_ref[...] = _modulate(x_ref[...], g_ref[...], mod_ref, k).astype(BF16)
            acc[...] = jnp.zeros_like(acc)

        chunks = [pl.ds(c * rows, rows) for c in range(FFN_CHUNKS)]
        wg, wu, wd = wg_ref[...], wu_ref[...], wd_ref[...]
        dacts = [_dot_nt(df_ref[rs, :], wd) for rs in chunks]
        dgates, dups = [], []
        for rs, dact in zip(chunks, dacts):
            gv, uv = gg_ref[rs, :].astype(F32), uu_ref[rs, :].astype(F32)
            sig = _sigmoid(gv)
            s = gv * sig
            act_ref[rs, :] = (s * uv).astype(BF16)
            dups.append((dact * s).astype(BF16))
            dgates.append((dact * uv * (sig * (1.0 + gv * (1.0 - sig)))).astype(BF16))
        dhs = [_dot_nt(dg, wg) + _dot_nt(du, wu) for dg, du in zip(dgates, dups)]
        for rs, dg, du, dh in zip(chunks, dgates, dups, dhs):
            dgg_ref[rs, :] = dg
            duu_ref[rs, :] = du
            acc[rs, :] += dh

        @pl.when(j == ns - 1)
        def _():
            dx, dshift, dscale, dg = _modulate_bwd(acc[...], x_ref[...], g_ref[...], mod_ref, k)
            dxo_v = dxo_ref[...]
            dx_ref[...] = dxo_v + dx
            dgt = jnp.sum(0.5 * f_ref[...].astype(F32) * dxo_v, axis=0, keepdims=True)

            @pl.when(i % tpb == 0)
            def _():
                dmod_ref[...] = jnp.zeros_like(dmod_ref)

            @pl.when(i == 0)
            def _():
                dg_ref[...] = jnp.zeros_like(dg_ref)

            dmod_ref[0:1, :] += dshift
            dmod_ref[1:2, :] += dscale
            dmod_ref[2:3, :] += dgt
            dg_ref[0:1, :] += dg

    tok = pl.BlockSpec((tm, d), lambda i, j: (i, 0))
    hid = pl.BlockSpec((None, tm, fs), lambda i, j: (j, i, 0))
    return _call(
        body, name=f"ffn_bwd{k}", grid=(nt, ns), args=(dxo, x, f, mod, g, gate, up, wg, wu, wd),
        in_specs=[tok, _once(tok), _once(tok),
                  pl.BlockSpec((None, N_MOD, d), lambda i, j: (i // tpb, 0, 0)),
                  pl.BlockSpec((1, d), lambda i, j: (0, 0)),
                  hid, hid,
                  pl.BlockSpec((None, d, fs), lambda i, j: (j, 0, 0)),
                  pl.BlockSpec((None, d, fs), lambda i, j: (j, 0, 0)),
                  pl.BlockSpec((None, fs, d), lambda i, j: (j, 0, 0))],
        out_specs=[tok, hid, hid, hid, tok, tok,
                   pl.BlockSpec((None, 8, d), lambda i, j: (i // tpb, 0, 0)),
                   pl.BlockSpec((8, d), lambda i, j: (0, 0))],
        out_shape=[jax.ShapeDtypeStruct((t, d), F32),
                   jax.ShapeDtypeStruct((ns, t, fs), BF16), jax.ShapeDtypeStruct((ns, t, fs), BF16),
                   jax.ShapeDtypeStruct((ns, t, fs), BF16),
                   jax.ShapeDtypeStruct((t, d), BF16), jax.ShapeDtypeStruct((t, d), BF16),
                   jax.ShapeDtypeStruct((nb, 8, d), F32), jax.ShapeDtypeStruct((8, d), F32)],
        scratch_shapes=[pltpu.VMEM((tm, d), F32)],
        params=_params("arbitrary", "arbitrary"), exchange=exchange)


def _mm_tn(a, b, a_spec, b_spec, out_shape, n_tiles, name, exchange=None, keep_transposed=False,
           pair_reduce=False):
    n_out = out_shape[0]
    block = tuple(out_shape[1:])
    last = n_tiles - 1
    flip = block[0] > block[1]
    if flip:
        block = block[::-1]
    if flip and keep_transposed:
        flip_back, out_shape = False, (n_out,) + block
    else:
        flip_back = flip
    full_shape = tuple(out_shape)
    n_pairs = n_out // 2
    if pair_reduce:
        out_shape = (n_pairs,) + full_shape[1:]

    def body(a_ref, b_ref, o_ref, acc, *pair):
        i, j = pl.program_id(0), pl.program_id(1)
        prod = _dot_tn(b_ref[...], a_ref[...]) if flip else _dot_tn(a_ref[...], b_ref[...])
        full_ref = pair[0] if pair_reduce else o_ref

        @pl.when(i == 0)
        def _():
            acc[j] = prod

        @pl.when(i > 0)
        def _():
            acc[j] += prod

        @pl.when(i == last)
        def _():
            total = acc[j]
            full_ref[j] = (total.T if flip_back else total).astype(BF16)

        if pair_reduce:
            _, landed, send_sems, recv_sems = pair

            @pl.when(jnp.logical_and(i == last, j == n_out - 1))
            def _():
                x, y, c = lax.axis_index("x"), lax.axis_index("y"), lax.axis_index("c")
                copies = [pltpu.make_async_remote_copy(
                    src_ref=full_ref.at[2 * q + 1 - c], dst_ref=landed.at[q],
                    send_sem=send_sems.at[q], recv_sem=recv_sems.at[q],
                    device_id=(x, y, 1 - c), device_id_type=pl.DeviceIdType.MESH) for q in range(n_pairs)]
                for cp in copies:
                    cp.start()
                for q, cp in enumerate(copies):
                    cp.wait_recv()
                    o_ref[q] = (full_ref[2 * q + c].astype(F32) + landed[q].astype(F32)).astype(BF16)
                for cp in copies:
                    cp.wait_send()

    scratch = [pltpu.VMEM((n_out,) + block, F32)]
    if pair_reduce:
        scratch += [pltpu.VMEM(full_shape, BF16), pltpu.VMEM(out_shape, BF16),
                    pltpu.SemaphoreType.DMA((n_pairs,)), pltpu.SemaphoreType.DMA((n_pairs,))]
    outs, sent = _call(
        body, name=name, grid=(n_tiles, n_out), args=(a, b), in_specs=[a_spec, b_spec],
        out_specs=[pl.BlockSpec(out_shape, lambda i, j: (0,) * len(out_shape))],
        out_shape=[jax.ShapeDtypeStruct(out_shape, BF16)],
        scratch_shapes=scratch,
        params=_params("arbitrary", "arbitrary"), exchange=exchange)
    return (outs[0], sent) if exchange is not None else outs[0]


def _ffn_weight_grads(h, dgate, dup, act, df, tm, tag, stream=False):
    t, d = h.shape
    ns, _, fs = dgate.shape
    nt = t // tm
    tok = pl.BlockSpec((tm, d), lambda i, j: (i, 0))
    hid = pl.BlockSpec((None, tm, fs), lambda i, j: (j, i, 0))
    if not stream:
        gwg = _mm_tn(h, dgate, tok, hid, (ns, d, fs), nt, f"grad_wg{tag}", keep_transposed=True)
        gwu = _mm_tn(h, dup, tok, hid, (ns, d, fs), nt, f"grad_wu{tag}", keep_transposed=True)
        gwd = _mm_tn(act, df, hid, tok, (ns, fs, d), nt, f"grad_wd{tag}")
        return gwg, gwu, gwd
    gwg = _mm_tn(h, dgate, tok, hid, (ns, d, fs), nt, f"grad_wg{tag}", keep_transposed=True, pair_reduce=True)
    gwu, sent_g = _mm_tn(h, dup, tok, hid, (ns, d, fs), nt, f"grad_wu{tag}",
                         _Exchange([gwg], True, chips=[True]), keep_transposed=True, pair_reduce=True)
    gwd, sent_u = _mm_tn(act, df, hid, tok, (ns, fs, d), nt, f"grad_wd{tag}",
                         _Exchange([gwu], True, chips=[True]), pair_reduce=True)
    return sent_g[0], sent_u[0], gwd


def _stage_shape(rows, cols):
    return pltpu.VMEM((cols // LANES, rows, LANES), F32)


def _stage(value, stage_ref):
    for k in range(stage_ref.shape[0]):
        stage_ref[k] = value[:, k * LANES:(k + 1) * LANES]


def _to_residue_rows(stage_ref, dst_ref, dil):
    rows = stage_ref.shape[1] // dil
    for r in range(dil):
        for k in range(stage_ref.shape[0]):
            dst_ref[r, :, k * LANES:(k + 1) * LANES] = (
                stage_ref.at[k][pl.ds(r, rows, stride=dil), :].astype(dst_ref.dtype))


def _from_residue_rows(src_ref, stage_ref, dil):
    rows = stage_ref.shape[1] // dil
    chunks = range(stage_ref.shape[0])
    for r in range(dil):
        for k in chunks:
            stage_ref.at[k][pl.ds(r, rows, stride=dil), :] = src_ref[r, :, k * LANES:(k + 1) * LANES].astype(F32)
    return jnp.concatenate([stage_ref[k] for k in chunks], axis=1)


def _residue_shape(nb, seq, width, dil, dtype):
    return jax.ShapeDtypeStruct((nb, dil, seq // dil, width), dtype)


def _residue_spec(tm, tpb, cols, dil, col_block):
    return pl.BlockSpec((None, dil, tm // dil, cols),
                        lambda i, *rest: (i // tpb, 0, i % tpb, col_block(i, *rest)))


def _qkv_fwd(x, mod, g, win, tm):
    t, d = x.shape
    ns, _, cs = win.shape
    nt = t // tm
    nb = mod.shape[0]
    tpb = nt // nb
    seq = t // nb
    half = ns // 2
    n_res = len(DILATIONS) - 1

    def body(x_ref, mod_ref, g_ref, w_ref, sb_ref, dil_ref, *rest):
        res_refs, h_ref, sc = rest[:n_res], rest[n_res], rest[n_res + 1]
        j = pl.program_id(1)

        @pl.when(j == 0)
        def _():
            h_ref[...] = _modulate(x_ref[...], g_ref[...], mod_ref, 1).astype(BF16)

        res = _dot(h_ref[...], w_ref[...])

        @pl.when(j < half)
        def _():
            sb_ref[...] = res.astype(BF16)

        @pl.when(j >= half)
        def _():
            dil_ref[...] = res.astype(BF16)
            _stage(res, sc)
            for ref, dil in zip(res_refs, DILATIONS[1:]):
                _to_residue_rows(sc, ref, dil)

    def dil_col(i, j):
        return jnp.maximum(j - half, 0)

    tok = pl.BlockSpec((tm, d), lambda i, j: (i, 0))
    wide = jax.ShapeDtypeStruct((t, half * cs), BF16)
    outs = pl.pallas_call(
        body, name="qkv_fwd", grid=(nt, ns),
        in_specs=[tok,
                  pl.BlockSpec((None, N_MOD, d), lambda i, j: (i // tpb, 0, 0)),
                  pl.BlockSpec((1, d), lambda i, j: (0, 0)),
                  pl.BlockSpec((None, d, cs), lambda i, j: (j, 0, 0))],
        out_specs=[pl.BlockSpec((tm, cs), lambda i, j: (i, jnp.minimum(j, half - 1))),
                   pl.BlockSpec((tm, cs), lambda i, j: (i, dil_col(i, j)))]
        + [_residue_spec(tm, tpb, cs, dil, dil_col) for dil in DILATIONS[1:]] + [tok],
        out_shape=[wide, wide] + [_residue_shape(nb, seq, half * cs, dil, BF16) for dil in DILATIONS[1:]]
        + [jax.ShapeDtypeStruct((t, d), BF16)],
        scratch_shapes=[_stage_shape(tm, cs)],
        compiler_params=_params("arbitrary", "arbitrary"),
    )(x, mod, g, win)
    qkv_dil = [outs[1]] + [a.reshape(t, half * cs) for a in outs[2:2 + n_res]]
    return outs[0], qkv_dil, outs[-1]


def _qkv_bwd(dqkv, dxo, x, mod, g, win, tm, exchange=None):
    t, d = x.shape
    ns, _, cs = win.shape
    nt = t // tm
    nb = mod.shape[0]
    tpb = nt // nb

    def body(dq_ref, dxo_ref, x_ref, mod_ref, g_ref, w_ref, dx_ref, dmod_ref, dg_ref, acc):
        i, j = pl.program_id(0), pl.program_id(1)

        @pl.when(j == 0)
        def _():
            acc[...] = jnp.zeros_like(acc)

        acc[...] += _dot_nt(dq_ref[...], w_ref[...])

        @pl.when(j == ns - 1)
        def _():
            dx, dshift, dscale, dg = _modulate_bwd(acc[...], x_ref[...], g_ref[...], mod_ref, 1)
            dx_ref[...] = dxo_ref[...] + dx

            @pl.when(i % tpb == 0)
            def _():
                dmod_ref[...] = jnp.zeros_like(dmod_ref)

            @pl.when(i == 0)
            def _():
                dg_ref[...] = jnp.zeros_like(dg_ref)

            dmod_ref[0:1, :] += dshift
            dmod_ref[1:2, :] += dscale
            dg_ref[0:1, :] += dg

    tok = pl.BlockSpec((tm, d), lambda i, j: (i, 0))
    return _call(
        body, name="qkv_bwd", grid=(nt, ns), args=(dqkv, dxo, x, mod, g, win),
        in_specs=[pl.BlockSpec((tm, cs), lambda i, j: (i, j)), tok, tok,
                  pl.BlockSpec((None, N_MOD, d), lambda i, j: (i // tpb, 0, 0)),
                  pl.BlockSpec((1, d), lambda i, j: (0, 0)),
                  pl.BlockSpec((None, d, cs), lambda i, j: (j, 0, 0))],
        out_specs=[tok,
                   pl.BlockSpec((None, 8, d), lambda i, j: (i // tpb, 0, 0)),
                   pl.BlockSpec((8, d), lambda i, j: (0, 0))],
        out_shape=[jax.ShapeDtypeStruct((t, d), F32),
                   jax.ShapeDtypeStruct((nb, 8, d), F32), jax.ShapeDtypeStruct((8, d), F32)],
        scratch_shapes=[pltpu.VMEM((tm, d), F32)],
        params=_params("arbitrary", "arbitrary"), exchange=exchange)


def _heads(a):
    return [a[:, h * HEAD_DIM:(h + 1) * HEAD_DIM] for h in range(a.shape[1] // HEAD_DIM)]


def _own_lanes():
    lane = lax.broadcasted_iota(jnp.int32, (1, LANES), 1)
    return [lane < HEAD_DIM, lane >= HEAD_DIM]


def _pair_tiles(a):
    return [a[:, (h // 2) * LANES:(h // 2 + 1) * LANES] for h in range(a.shape[1] // HEAD_DIM)]


def _own_tiles(a, own):
    return [jnp.where(own[h % 2], tile, jnp.zeros_like(tile)) for h, tile in enumerate(_pair_tiles(a))]


def _merge_tiles(per_head, own):
    return jnp.concatenate([jnp.where(own[0], per_head[h], per_head[h + 1])
                            for h in range(0, len(per_head), 2)], axis=1)


def _scaled(q):
    return (q.astype(F32) * (HEAD_DIM ** -0.5)).astype(BF16)


def _sb_logits(qh, kh, tri, causal):
    zs = [_dot_nt(q, k) for q, k in zip(qh, kh)]
    es = [jnp.exp(-jnp.abs(z)) for z in zs]
    log_nots = [-(jnp.maximum(z, 0.0) + jnp.log(1.0 + e)) for z, e in zip(zs, es)]
    if causal is not None:
        log_nots = [jnp.where(causal, ln, 0.0) for ln in log_nots]
    return zs, es, [_split_dot(ln, tri) for ln in log_nots]


def _sb_masks():
    rows = lax.broadcasted_iota(jnp.int32, (SB_BLOCK, SB_BLOCK), 0)
    cols = lax.broadcasted_iota(jnp.int32, (SB_BLOCK, SB_BLOCK), 1)
    return (rows >= cols).astype(BF16), (rows <= cols).astype(BF16), cols < rows


def _sb_fwd(qkv, nb, seq, exchange=None):
    t = qkv.shape[0]
    n_pairs = (qkv.shape[1] // 3) // SB_WIDTH
    tb = SB_BLOCK
    n_blk = seq // tb

    def body(q_ref, k_ref, v_ref, o_ref, c_ref):
        tri, _, causal = _sb_masks()
        own = _own_lanes()

        def key_block(qh, kj, carry, mask):
            ks = pl.multiple_of(kj * tb, tb)
            kh, vh = _pair_tiles(k_ref[pl.ds(ks, tb), :]), _pair_tiles(v_ref[pl.ds(ks, tb), :])
            zs, _, suffixes = _sb_logits(qh, kh, tri, mask)
            ws = [jnp.exp(z + suffix + cr[1]) for z, suffix, cr in zip(zs, suffixes, carry)]
            if mask is not None:
                ws = [jnp.where(mask, w, 0.0) for w in ws]
            pv = [_dot(w.astype(BF16), v) for w, v in zip(ws, vh)]
            return tuple((cr[0] + p, cr[1] + suffix[:, 0:1]) for cr, p, suffix in zip(carry, pv, suffixes))

        def query_block(qi, _):
            qs = pl.multiple_of(qi * tb, tb)
            qh = _own_tiles(_scaled(q_ref[pl.ds(qs, tb), :]), own)
            zero = (jnp.zeros((tb, LANES), F32), jnp.zeros((tb, 1), F32))
            carry = key_block(qh, qi, (zero,) * SB_HEADS, causal)
            carry = lax.fori_loop(0, qi, lambda it, cr: key_block(qh, qi - 1 - it, cr, None), carry)
            o_ref[pl.ds(qs, tb), :] = _merge_tiles([cr[0] for cr in carry], own)
            c_ref[pl.ds(qs, tb), :] = _merge_tiles([jnp.broadcast_to(cr[1], (tb, LANES)) for cr in carry], own)
            return 0

        lax.fori_loop(0, n_blk, query_block, 0)

    def spec(offset):
        return pl.BlockSpec((seq, SB_WIDTH), lambda b, p: (b, offset + p))

    out = jax.ShapeDtypeStruct((t, n_pairs * SB_WIDTH), F32)
    return _call(
        body, name="sb_fwd", grid=(nb, n_pairs), args=(qkv, qkv, qkv),
        in_specs=[spec(0), spec(n_pairs), spec(2 * n_pairs)],
        out_specs=[spec(0), spec(0)], out_shape=[out, out],
        params=_params("arbitrary", "arbitrary"), exchange=exchange)


def _sb_bwd(qkv, do, csum, nb, seq, exchange=None):
    t = qkv.shape[0]
    n_pairs = (qkv.shape[1] // 3) // SB_WIDTH
    tb = SB_BLOCK
    n_blk = seq // tb
    scale = HEAD_DIM ** -0.5

    def body(q_ref, k_ref, v_ref, do_ref, c_ref, dq_ref, dk_ref, dv_ref, dkt_acc, dvt_acc):
        tri, tri_prefix, causal = _sb_masks()
        own = _own_lanes()
        dkt_acc[...] = jnp.zeros_like(dkt_acc)
        dvt_acc[...] = jnp.zeros_like(dvt_acc)

        def key_block(qh, qth, doh, doth, ch, kj, carry, mask):
            ks = pl.multiple_of(kj * tb, tb)
            kh, vh = _pair_tiles(k_ref[pl.ds(ks, tb), :]), _pair_tiles(v_ref[pl.ds(ks, tb), :])
            heads = range(SB_HEADS)
            zs, es, suffixes = _sb_logits(qh, kh, tri, mask)
            dws = [_dot_nt(doh[h], vh[h]) for h in heads]
            lefts = [carry[h][1] + suffixes[h][:, 0:1] for h in heads]
            ws = [jnp.exp(zs[h] + suffixes[h] + (ch[h] - lefts[h])) for h in heads]
            if mask is not None:
                ws = [jnp.where(mask, w, 0.0) for w in ws]
            dlws = [ws[h] * dws[h] for h in heads]
            dprefixes = [_split_dot(dlw, tri_prefix) for dlw in dlws]
            dvts = [_dot(doth[h], ws[h].astype(BF16)) for h in heads]
            dzbs = []
            for h in heads:
                sig = jnp.where(zs[h] >= 0.0, 1.0, es[h]) * pl.reciprocal(1.0 + es[h], approx=True)
                dz = dlws[h] - sig * (carry[h][2] + dprefixes[h])
                if mask is not None:
                    dz = jnp.where(mask, dz, 0.0)
                dzbs.append(dz.astype(BF16))
            dkts = [_dot(qth[h], dzbs[h]) for h in heads]
            dqs = [_dot(dzbs[h], kh[h]) for h in heads]
            dkt_acc[:, pl.ds(ks, tb)] += jnp.concatenate([dkts[h] + dkts[h + 1] for h in heads[::2]], axis=0)
            dvt_acc[:, pl.ds(ks, tb)] += jnp.concatenate([dvts[h] + dvts[h + 1] for h in heads[::2]], axis=0)
            return tuple((carry[h][0] + dqs[h], lefts[h], carry[h][2] + dprefixes[h][:, tb - 1:tb])
                         for h in heads)

        def query_block(qi, _):
            qs = pl.multiple_of(qi * tb, tb)
            qh = _own_tiles(_scaled(q_ref[pl.ds(qs, tb), :]), own)
            doh = _own_tiles(do_ref[pl.ds(qs, tb), :], own)
            qth = [a.astype(F32).T.astype(BF16) for a in qh]
            doth = [a.T.astype(BF16) for a in doh]
            doh = [a.astype(BF16) for a in doh]
            cv = c_ref[pl.ds(qs, tb), :]
            ch = [cv[:, h * HEAD_DIM:h * HEAD_DIM + 1] for h in range(SB_HEADS)]
            zero = (jnp.zeros((tb, LANES), F32), jnp.zeros((tb, 1), F32), jnp.zeros((tb, 1), F32))
            carry = lax.fori_loop(
                0, qi, lambda kj, cr: key_block(qh, qth, doh, doth, ch, kj, cr, None), (zero,) * SB_HEADS)
            carry = key_block(qh, qth, doh, doth, ch, qi, carry, causal)
            dq = _merge_tiles([cr[0] for cr in carry], own) * scale
            dq_ref[pl.ds(qs, tb), :] = dq.astype(BF16)
            return 0

        lax.fori_loop(0, n_blk, query_block, 0)
        dk_ref[...] = dkt_acc[...].T.astype(BF16)
        dv_ref[...] = dvt_acc[...].T.astype(BF16)

    def spec(offset):
        return pl.BlockSpec((seq, SB_WIDTH), lambda b, p: (b, offset + p))

    out = jax.ShapeDtypeStruct((t, n_pairs * SB_WIDTH), BF16)
    return _call(
        body, name="sb_bwd", grid=(nb, n_pairs), args=(qkv, qkv, qkv, do, csum),
        in_specs=[spec(0), spec(n_pairs), spec(2 * n_pairs), spec(0), spec(0)],
        out_specs=[spec(0), spec(0), spec(0)],
        out_shape=[out, out, out],
        scratch_shapes=[pltpu.VMEM((SB_WIDTH, seq), F32), pltpu.VMEM((SB_WIDTH, seq), F32)],
        params=_params("arbitrary", "arbitrary"), exchange=exchange)


def _dil_block_scores(qh, kph, kch, bias_ref, has_prev, band_prev, band_cur):
    scale = HEAD_DIM ** -0.5
    heads = range(len(qh))
    no_prev = jnp.where(has_prev, 0.0, NEG_INF)
    zps = [_dot_nt(qh[h], kph[h]) for h in heads]
    zcs = [_dot_nt(qh[h], kch[h]) for h in heads]
    zps = [jnp.where(band_prev, zps[h] * scale + bias_ref[h, :, 0:DIL_BLOCK], NEG_INF) + no_prev for h in heads]
    zcs = [jnp.where(band_cur, zcs[h] * scale + bias_ref[h, :, DIL_BLOCK:2 * DIL_BLOCK], NEG_INF) for h in heads]
    return zps, zcs


def _dil_bands():
    rows = lax.broadcasted_iota(jnp.int32, (DIL_BLOCK, DIL_BLOCK), 0)
    cols = lax.broadcasted_iota(jnp.int32, (DIL_BLOCK, DIL_BLOCK), 1)
    return cols >= rows, cols <= rows


def _dil_fwd(qkv, bias, nb, seq, dil, exchange=None):
    t, width = qkv.shape
    n_pairs = (width // 3) // DIL_WIDTH
    bq = DIL_BLOCK
    n_blk = seq // bq
    per_seq = n_blk // dil
    heads = range(DIL_HEADS)

    def body(q_ref, k_ref, v_ref, bias_ref, o_ref, lse_ref):
        band_prev, band_cur = _dil_bands()
        own = _own_lanes()

        def block(n, _):
            has_prev = (n & (per_seq - 1)) != 0
            qs = pl.multiple_of(n * bq, bq)
            ps = pl.multiple_of(jnp.maximum(n - 1, 0) * bq, bq)
            qh = _own_tiles(q_ref[pl.ds(qs, bq), :], own)
            kp, kc = _pair_tiles(k_ref[pl.ds(ps, bq), :]), _pair_tiles(k_ref[pl.ds(qs, bq), :])
            vp, vc = _pair_tiles(v_ref[pl.ds(ps, bq), :]), _pair_tiles(v_ref[pl.ds(qs, bq), :])
            zps, zcs = _dil_block_scores(qh, kp, kc, bias_ref, has_prev, band_prev, band_cur)
            ms = [jnp.maximum(jnp.max(zps[h], axis=1, keepdims=True), jnp.max(zcs[h], axis=1, keepdims=True))
                  for h in heads]
            eps = [jnp.exp(zps[h] - ms[h]) for h in heads]
            ecs = [jnp.exp(zcs[h] - ms[h]) for h in heads]
            pvs = [_dot(eps[h].astype(BF16), vp[h]) + _dot(ecs[h].astype(BF16), vc[h]) for h in heads]
            dens = [jnp.sum(eps[h], axis=1, keepdims=True) + jnp.sum(ecs[h], axis=1, keepdims=True) for h in heads]
            o_ref[pl.ds(qs, bq), :] = _merge_tiles([pvs[h] / dens[h] for h in heads], own)
            lse_ref[pl.ds(qs, bq), :] = _merge_tiles(
                [jnp.broadcast_to(ms[h] + jnp.log(dens[h]), (bq, LANES)) for h in heads], own)
            return 0

        lax.fori_loop(0, n_blk, block, 0)

    def spec(offset):
        return pl.BlockSpec((seq, DIL_WIDTH), lambda b, p: (b, offset + p))

    out = jax.ShapeDtypeStruct((t, n_pairs * DIL_WIDTH), F32)
    return _call(
        body, name=f"dil_fwd{dil}", grid=(nb, n_pairs), args=(qkv, qkv, qkv, bias),
        in_specs=[spec(0), spec(n_pairs), spec(2 * n_pairs),
                  pl.BlockSpec((DIL_HEADS, bq, 2 * bq), lambda b, p: (p, 0, 0))],
        out_specs=[spec(0), spec(0)], out_shape=[out, out],
        params=_params("arbitrary", "arbitrary"), exchange=exchange)


def _dil_bwd(qkv, bias, do, lse, delta, nb, seq, dil):
    t, width = qkv.shape
    n_pairs = (width // 3) // DIL_WIDTH
    bq = DIL_BLOCK
    n_blk = seq // bq
    per_seq = n_blk // dil
    scale = HEAD_DIM ** -0.5
    heads = range(DIL_HEADS)

    def body(q_ref, k_ref, v_ref, bias_ref, do_ref, lse_ref, dl_ref, dq_ref, dk_ref, dv_ref, db_ref,
             dk_acc, dv_acc):
        band_prev, band_cur = _dil_bands()
        own = _own_lanes()
        dk_acc[...] = jnp.zeros_like(dk_acc)
        dv_acc[...] = jnp.zeros_like(dv_acc)

        @pl.when(pl.program_id(1) == 0)
        def _():
            db_ref[...] = jnp.zeros_like(db_ref)

        def block(n, _):
            has_prev = (n & (per_seq - 1)) != 0
            qs = pl.multiple_of(n * bq, bq)
            ps = pl.multiple_of(jnp.maximum(n - 1, 0) * bq, bq)
            qh = _own_tiles(q_ref[pl.ds(qs, bq), :], own)
            kp, kc = _pair_tiles(k_ref[pl.ds(ps, bq), :]), _pair_tiles(k_ref[pl.ds(qs, bq), :])
            vp, vc = _pair_tiles(v_ref[pl.ds(ps, bq), :]), _pair_tiles(v_ref[pl.ds(qs, bq), :])
            doh = _own_tiles(do_ref[pl.ds(qs, bq), :].astype(BF16), own)
            lse_v, dl_v = lse_ref[pl.ds(qs, bq), :], dl_ref[pl.ds(qs, bq), :]
            zps, zcs = _dil_block_scores(qh, kp, kc, bias_ref, has_prev, band_prev, band_cur)
            dpp = [_dot_nt(doh[h], vp[h]) for h in heads]
            dpc = [_dot_nt(doh[h], vc[h]) for h in heads]
            lse_h = [lse_v[:, h * HEAD_DIM:h * HEAD_DIM + 1] for h in heads]
            dl_h = [dl_v[:, h * HEAD_DIM:h * HEAD_DIM + 1] for h in heads]
            pps = [jnp.exp(zps[h] - lse_h[h]) for h in heads]
            pcs = [jnp.exp(zcs[h] - lse_h[h]) for h in heads]
            dvp = [_dot_tn(pps[h].astype(BF16), doh[h]) for h in heads]
            dvc = [_dot_tn(pcs[h].astype(BF16), doh[h]) for h in heads]
            dzps = [pps[h] * (dpp[h] - dl_h[h]) for h in heads]
            dzcs = [pcs[h] * (dpc[h] - dl_h[h]) for h in heads]
            dzp_b = [(dzps[h] * scale).astype(BF16) for h in heads]
            dzc_b = [(dzcs[h] * scale).astype(BF16) for h in heads]
            dqs = [_dot(dzp_b[h], kp[h]) + _dot(dzc_b[h], kc[h]) for h in heads]
            dkp = [_dot_tn(dzp_b[h], qh[h]) for h in heads]
            dkc = [_dot_tn(dzc_b[h], qh[h]) for h in heads]
            for h in heads:
                db_ref[h, :, 0:bq] += dzps[h]
                db_ref[h, :, bq:2 * bq] += dzcs[h]
            def pair_sums(per_head):
                return jnp.concatenate([per_head[h] + per_head[h + 1] for h in heads[::2]], axis=1)

            dq_ref[pl.ds(qs, bq), :] = _merge_tiles(dqs, own).astype(BF16)
            dk_acc[pl.ds(ps, bq), :] += pair_sums(dkp)
            dk_acc[pl.ds(qs, bq), :] += pair_sums(dkc)
            dv_acc[pl.ds(ps, bq), :] += pair_sums(dvp)
            dv_acc[pl.ds(qs, bq), :] += pair_sums(dvc)
            return 0

        lax.fori_loop(0, n_blk, block, 0)
        dk_ref[...] = dk_acc[...].astype(BF16)
        dv_ref[...] = dv_acc[...].astype(BF16)

    def spec(offset):
        return pl.BlockSpec((seq, DIL_WIDTH), lambda p, b: (b, offset + p))

    bias_spec = pl.BlockSpec((DIL_HEADS, bq, 2 * bq), lambda p, b: (p, 0, 0))
    out = jax.ShapeDtypeStruct((t, n_pairs * DIL_WIDTH), BF16)
    return pl.pallas_call(
        body, name=f"dil_bwd{dil}", grid=(n_pairs, nb),
        in_specs=[spec(0), spec(n_pairs), spec(2 * n_pairs), bias_spec, spec(0), spec(0), spec(0)],
        out_specs=[spec(0), spec(0), spec(0), bias_spec],
        out_shape=[out, out, out, jax.ShapeDtypeStruct(bias.shape, F32)],
        scratch_shapes=[pltpu.VMEM((seq, DIL_WIDTH), F32), pltpu.VMEM((seq, DIL_WIDTH), F32)],
        compiler_params=_params("arbitrary", "arbitrary"),
    )(qkv, qkv, qkv, bias, do, lse, delta)


def _head_blocks(width):
    rows = lax.broadcasted_iota(jnp.int32, (width, width), 0) // HEAD_DIM
    cols = lax.broadcasted_iota(jnp.int32, (width, width), 1) // HEAD_DIM
    return (rows == cols).astype(BF16)


def _head_mean(v, gmat):
    return _split_dot(v, gmat) * (1.0 / HEAD_DIM)


def _residue_views(arrays, nb, seq):
    return [a if dil == 1 else a.reshape(nb, dil, seq // dil, a.shape[1]) for a, dil in zip(arrays, DILATIONS)]


def _mix_out_fwd(osb, ocs, lses, gsb, gdil, wout, x, mod, tm):
    t, d = x.shape
    ds = osb.shape[1]
    nt = t // tm
    nb = mod.shape[0]
    tpb = nt // nb
    seq = t // nb
    n_cfg = len(DILATIONS)

    def body(osb_ref, *refs):
        oc_refs, lse_refs = refs[:n_cfg], refs[n_cfg:2 * n_cfg]
        gsb_ref, gdil_ref, w_ref, x_ref, mod_ref = refs[2 * n_cfg:2 * n_cfg + 5]
        xo_ref, on_ref, m_ref, odil_ref = refs[2 * n_cfg + 5:2 * n_cfg + 9]
        ld_refs = refs[2 * n_cfg + 9:3 * n_cfg + 9]
        stages, sc = refs[3 * n_cfg + 9:]
        ocv, lsev = [oc_refs[0][...]], [lse_refs[0][...]]
        for i, dil in enumerate(DILATIONS[1:]):
            ocv.append(_from_residue_rows(oc_refs[i + 1], stages.at[2 * i], dil))
            lsev.append(_from_residue_rows(lse_refs[i + 1], stages.at[2 * i + 1], dil))
        top = functools.reduce(jnp.maximum, lsev)
        total = top + jnp.log(sum(jnp.exp(l - top) for l in lsev))
        odil = sum(jnp.exp(l - total) * o for o, l in zip(ocv, lsev))
        odil_ref[...] = odil
        ld_refs[0][...] = total
        _stage(total, sc)
        for ref, dil in zip(ld_refs[1:], DILATIONS[1:]):
            _to_residue_rows(sc, ref, dil)
        gm = _head_blocks(ds)
        parts = []
        for o, g_ref in ((osb_ref[...], gsb_ref), (odil, gdil_ref)):
            parts.append(o * lax.rsqrt(_head_mean(o * o, gm) + EPS) * g_ref[...])
        on = jnp.concatenate(parts, axis=1).astype(BF16)
        on_ref[...] = on
        m = _dot(on, w_ref[...])
        m_ref[...] = m
        xo_ref[...] = x_ref[...] + mod_ref[5:6, :] * m

    tok = pl.BlockSpec((tm, d), lambda i: (i, 0))
    hd = pl.BlockSpec((tm, ds), lambda i: (i, 0))
    res = [hd] + [_residue_spec(tm, tpb, ds, dil, lambda i: 0) for dil in DILATIONS[1:]]
    res_shape = [jax.ShapeDtypeStruct((t, ds), F32)] + [_residue_shape(nb, seq, ds, dil, F32) for dil in DILATIONS[1:]]
    gain = pl.BlockSpec((1, ds), lambda i: (0, 0))
    outs = pl.pallas_call(
        body, name="mix_out_fwd", grid=(nt,),
        in_specs=[hd] + res + res + [gain, gain,
                  pl.BlockSpec(wout.shape, lambda i: (0, 0)),
                  tok, pl.BlockSpec((None, N_MOD, d), lambda i: (i // tpb, 0, 0))],
        out_specs=[tok, pl.BlockSpec((tm, 2 * ds), lambda i: (i, 0)), tok, hd] + res,
        out_shape=[jax.ShapeDtypeStruct((t, d), F32), jax.ShapeDtypeStruct((t, 2 * ds), BF16),
                   jax.ShapeDtypeStruct((t, d), F32), jax.ShapeDtypeStruct((t, ds), F32)] + res_shape,
        scratch_shapes=[pltpu.VMEM((2 * (n_cfg - 1), ds // LANES, tm, LANES), F32), _stage_shape(tm, ds)],
        compiler_params=_params("arbitrary"),
    )(osb, *_residue_views(ocs, nb, seq), *_residue_views(lses, nb, seq), gsb, gdil, wout, x, mod)
    return outs[0], outs[1], outs[2], outs[3], [a.reshape(t, ds) for a in outs[4:]]


def _mix_out_bwd(dxo, m, mod, wout, osb, odil, gsb, gdil, tm):
    t, d = dxo.shape
    ds = osb.shape[1]
    nt = t // tm
    nb = mod.shape[0]
    tpb = nt // nb
    seq = t // nb
    n_cfg = len(DILATIONS)

    def body(dxo_ref, m_ref, mod_ref, w_ref, osb_ref, odil_ref, gsb_ref, gdil_ref,
             dm_ref, dosb_ref, *rest):
        do_refs, dl_refs = rest[:n_cfg], rest[n_cfg:2 * n_cfg]
        dmod_ref, dg_ref, sc = rest[2 * n_cfg:]
        dodil_ref, dldil_ref = do_refs[0], dl_refs[0]
        i = pl.program_id(0)
        dxo_v = dxo_ref[...]
        dm = (mod_ref[5:6, :] * dxo_v).astype(BF16)
        dm_ref[...] = dm
        dgt = jnp.sum(m_ref[...] * dxo_v, axis=0, keepdims=True)
        don = _dot_nt(dm, w_ref[...])
        gm = _head_blocks(ds)

        @pl.when(i % tpb == 0)
        def _():
            dmod_ref[...] = jnp.zeros_like(dmod_ref)

        @pl.when(i == 0)
        def _():
            dg_ref[...] = jnp.zeros_like(dg_ref)

        dmod_ref[2:3, :] += dgt
        groups = ((osb_ref, gsb_ref, dosb_ref), (odil_ref, gdil_ref, dodil_ref))
        for k, (o_ref, g_ref, do_ref) in enumerate(groups):
            o = o_ref[...]
            dn_out = don[:, k * ds:(k + 1) * ds]
            r = lax.rsqrt(_head_mean(o * o, gm) + EPS)
            n = o * r
            dg_ref[0:1, k * ds:(k + 1) * ds] += jnp.sum(dn_out * n, axis=0, keepdims=True)
            dn = dn_out * g_ref[...]
            do = r * (dn - n * _head_mean(dn * n, gm))
            do_ref[...] = do
            if k == 1:
                delta = _head_mean(do * o, gm) * float(HEAD_DIM)
                dldil_ref[...] = delta
                for value, refs in ((do, do_refs), (delta, dl_refs)):
                    _stage(value, sc)
                    for ref, dil in zip(refs[1:], DILATIONS[1:]):
                        _to_residue_rows(sc, ref, dil)

    tok = pl.BlockSpec((tm, d), lambda i: (i, 0))
    hd = pl.BlockSpec((tm, ds), lambda i: (i, 0))
    res = [hd] + [_residue_spec(tm, tpb, ds, dil, lambda i: 0) for dil in DILATIONS[1:]]
    res_shape = [jax.ShapeDtypeStruct((t, ds), F32)] + [_residue_shape(nb, seq, ds, dil, F32) for dil in DILATIONS[1:]]
    gain = pl.BlockSpec((1, ds), lambda i: (0, 0))
    outs = pl.pallas_call(
        body, name="mix_out_bwd", grid=(nt,),
        in_specs=[tok, tok, pl.BlockSpec((None, N_MOD, d), lambda i: (i // tpb, 0, 0)),
                  pl.BlockSpec(wout.shape, lambda i: (0, 0)), hd, hd, gain, gain],
        out_specs=[tok, hd] + res + res
        + [pl.BlockSpec((None, 8, d), lambda i: (i // tpb, 0, 0)), pl.BlockSpec((8, 2 * ds), lambda i: (0, 0))],
        out_shape=[jax.ShapeDtypeStruct((t, d), BF16), jax.ShapeDtypeStruct((t, ds), F32)] + res_shape + res_shape
        + [jax.ShapeDtypeStruct((nb, 8, d), F32), jax.ShapeDtypeStruct((8, 2 * ds), F32)],
        scratch_shapes=[_stage_shape(tm, ds)],
        compiler_params=_params("arbitrary"),
    )(dxo, m, mod, wout, osb, odil, gsb, gdil)
    flat = [a.reshape(t, ds) for a in outs[2:2 + 2 * n_cfg]]
    return outs[0], outs[1], flat[:n_cfg], flat[n_cfg:], outs[-2], outs[-1]


def _merge_dqkv(sb_parts, dil_parts, nb, tm):
    t, ds = sb_parts[0].shape
    nt = t // tm
    tpb = nt // nb
    seq = t // nb
    n_cfg = len(DILATIONS)

    def body(*refs):
        sb_refs, dil_refs = refs[:3], refs[3:3 + 3 * n_cfg]
        o_ref, sc = refs[3 + 3 * n_cfg:]
        for k in range(3):
            o_ref[:, k * ds:(k + 1) * ds] = sb_refs[k][...]
            total = dil_refs[k * n_cfg][...].astype(F32)
            for i, dil in enumerate(DILATIONS[1:]):
                total = total + _from_residue_rows(dil_refs[k * n_cfg + i + 1], sc, dil)
            o_ref[:, (3 + k) * ds:(4 + k) * ds] = total.astype(BF16)

    hd = pl.BlockSpec((tm, ds), lambda i: (i, 0))
    res = [hd] + [_residue_spec(tm, tpb, ds, dil, lambda i: 0) for dil in DILATIONS[1:]]
    views = [v for parts in dil_parts for v in _residue_views(parts, nb, seq)]
    return pl.pallas_call(
        body, name="merge_dqkv", grid=(nt,),
        in_specs=[hd] * 3 + res * 3,
        out_specs=pl.BlockSpec((tm, 6 * ds), lambda i: (i, 0)),
        out_shape=jax.ShapeDtypeStruct((t, 6 * ds), BF16),
        scratch_shapes=[_stage_shape(tm, ds)],
        compiler_params=_params("arbitrary"),
    )(*sb_parts, *views)


def _loss_head(x, target, g, tm):
    t, d = x.shape

    def body(x_ref, t_ref, g_ref, dx_ref, acc_ref):
        @pl.when(pl.program_id(0) == 0)
        def _():
            acc_ref[...] = jnp.zeros_like(acc_ref)

        n, r = _norm(x_ref[...])
        gv = g_ref[...]
        err = n * gv - t_ref[...]
        dy = err * (1.0 / d)
        acc_ref[0:1, :] += jnp.sum(err * err, axis=0, keepdims=True)
        acc_ref[1:2, :] += jnp.sum(dy * n, axis=0, keepdims=True)
        dn = dy * gv
        dx_ref[...] = r * (dn - n * jnp.mean(dn * n, axis=-1, keepdims=True))

    tok = pl.BlockSpec((tm, d), lambda i: (i, 0))
    return pl.pallas_call(
        body, name="loss_head", grid=(t // tm,),
        in_specs=[tok, tok, pl.BlockSpec((1, d), lambda i: (0, 0))],
        out_specs=[tok, pl.BlockSpec((8, d), lambda i: (0, 0))],
        out_shape=[jax.ShapeDtypeStruct((t, d), F32), jax.ShapeDtypeStruct((8, d), F32)],
        compiler_params=_params("arbitrary"),
    )(x, target, g)


def _row_tile(rows):
    if rows <= 256:
        return rows
    for cand in range(256, 15, -16):
        if rows % cand == 0:
            return cand
    return rows


def _adamw(w, parts, m, v, name, transposed=False):
    rows, cols = w.shape
    n_parts = parts.shape[0]
    tr = _row_tile(rows)
    c1 = 1.0 / (1.0 - ADAM_B1 ** ADAM_STEP)
    c2 = 1.0 / (1.0 - ADAM_B2 ** ADAM_STEP)

    def body(w_ref, p_ref, m_ref, v_ref, g_ref, d_ref, nm_ref, nv_ref):
        g = p_ref[0].astype(F32)
        for i in range(1, n_parts):
            g = g + p_ref[i].astype(F32)
        wv, mv, vv = w_ref[...], m_ref[...], v_ref[...]
        if transposed:
            wv, mv, vv = wv.T, mv.T, vv.T
        nm = ADAM_B1 * mv + (1.0 - ADAM_B1) * g
        nv = ADAM_B2 * vv + (1.0 - ADAM_B2) * (g * g)
        g_ref[...] = g
        nm_ref[...] = nm
        nv_ref[...] = nv
        d_ref[...] = -ADAM_LR * ((nm * c1) / (jnp.sqrt(nv * c2) + ADAM_EPS) + ADAM_WD * wv)

    blk = pl.BlockSpec((tr, cols), lambda i: (i, 0))
    if transposed:
        oblk = pl.BlockSpec((cols, tr), lambda i: (0, i))
        pblk = pl.BlockSpec((n_parts, cols, tr), lambda i: (0, 0, i))
        out = jax.ShapeDtypeStruct((cols, rows), F32)
    else:
        oblk, pblk = blk, pl.BlockSpec((n_parts, tr, cols), lambda i: (0, i, 0))
        out = jax.ShapeDtypeStruct((rows, cols), F32)
    return pl.pallas_call(
        body, name=name, grid=(rows // tr,),
        in_specs=[blk, pblk, blk, blk],
        out_specs=[oblk, oblk, oblk, oblk], out_shape=[out, out, out, out],
        compiler_params=_params("arbitrary"),
    )(w, parts, m, v)


def _t5_bucket(n):
    max_exact = N_BUCKETS // 2
    nf = np.maximum(n, 1).astype(np.float32)
    large = max_exact + (np.log(nf / max_exact) / math.log(MAX_DISTANCE / max_exact)
                         * (N_BUCKETS - max_exact)).astype(np.int32)
    large = np.minimum(large, N_BUCKETS - 1)
    return np.where(n < max_exact, n, large).astype(np.int32)


def _bucket_onehot():
    table = np.zeros((len(DILATIONS), 2 * DIL_BLOCK + 1, N_BUCKETS), np.float32)
    for i, dil in enumerate(DILATIONS):
        buckets = _t5_bucket(np.arange(DIL_BLOCK + 1) * dil)
        for m in range(DIL_BLOCK + 1):
            table[i, m, buckets[DIL_BLOCK - m]] = 1.0
    return table


def _bias_blocks(rel_bias):
    row = jnp.einsum("cmn,nh->chm", _bucket_onehot(), rel_bias, precision=lax.Precision.HIGHEST)
    n_cfg, n_heads, width = row.shape
    tiled = jnp.tile(row, (1, 1, DIL_BLOCK))[..., :DIL_BLOCK * (width - 1)]
    return tiled.reshape(n_cfg, n_heads, DIL_BLOCK, width - 1)


def _bias_blocks_bwd(dblocks):
    n_cfg, n_heads = dblocks.shape[:2]
    width = 2 * DIL_BLOCK + 1
    flat = dblocks.reshape(n_cfg, n_heads, DIL_BLOCK * (width - 1))
    flat = jnp.pad(flat, ((0, 0), (0, 0), (0, DIL_BLOCK)))
    drow = jnp.sum(flat.reshape(n_cfg, n_heads, DIL_BLOCK, width), axis=2)
    return jnp.einsum("chm,cmn->nh", drow, _bucket_onehot(), precision=lax.Precision.HIGHEST)


def _pad_to(a, axis, size):
    pad = [(0, 0)] * a.ndim
    pad[axis] = (0, size - a.shape[axis])
    return jnp.pad(a, pad)


def _lane_pad(n):
    return -(-n // LANES) * LANES


def _local_step(x, target, mod, gains, weights, rel_bias, tm, distributed):
    nb, seq, d = x.shape
    t = nb * seq
    g_ffn1, g_mix, g_sb, g_dil, g_ffn2, g_final = gains
    wg1, wu1, wd1 = weights[:3]
    x0 = x.reshape(t, d)
    ds = g_sb.shape[1]
    bias = _bias_blocks(rel_bias)

    def beside(arrays, scatter):
        return _Exchange(arrays, scatter) if distributed else None

    tp, tg = min(PROJ_TILE, seq), min(GRAD_TILE, t)

    (x1, f1, gate1, up1), got = _ffn_fwd(x0, mod, g_ffn1, wg1, wu1, wd1, 0, tp, beside(weights[3:5], False))
    win, wout = got if distributed else weights[3:5]
    wout2 = wout.reshape(-1, d)
    qkv, qkvd, h2 = _qkv_fwd(x1, mod, g_mix, win, tp)
    (osb, csb), got = _sb_fwd(qkv, nb, seq, beside(weights[5:7], False))
    wg2, wu2 = got if distributed else weights[5:7]
    ocs, lses = [], []
    for i, dil in enumerate(DILATIONS):
        (oc, lse), got = _dil_fwd(qkvd[i], bias[i], nb, seq, dil, beside(weights[7:8], False) if i == 0 else None)
        if i == 0:
            wd2 = got[0] if distributed else weights[7]
        ocs.append(oc)
        lses.append(lse)
    x2, on, mix, odil, ldil = _mix_out_fwd(osb, ocs, lses, g_sb, g_dil, wout2, x1, mod, tm)
    (x3, f3, gate3, up3), _ = _ffn_fwd(x2, mod, g_ffn2, wg2, wu2, wd2, 2, tp)
    dx3, head = _loss_head(x3, target.reshape(t, d), g_final, tm)
    loss_sum = 0.5 * jnp.sum(head[0]) / d
    dg_final = head[1:2]

    (dx2, dgate3, dup3, act3, h3, df3, dmod3, dg_ffn2), _ = _ffn_bwd(
        dx3, x2, f3, mod, g_ffn2, gate3, up3, wg2, wu2, wd2, 2, tp)
    gwg2, gwu2, gwd2 = _ffn_weight_grads(h3, dgate3, dup3, act3, df3, tg, 2)

    dm, dosb, dodil, dldil, dmod2b, dg_heads = _mix_out_bwd(
        dx2, mix, mod, wout2, osb, odil, g_sb, g_dil, tm)
    n_out = wout.shape[0]
    gwout = _mm_tn(on, dm,
                   pl.BlockSpec((tg, wout.shape[1]), lambda i, j: (i, j)),
                   pl.BlockSpec((tg, d), lambda i, j: (i, 0)),
                   wout.shape, t // tg, "grad_wout")

    (dq_sb, dk_sb, dv_sb), parts_late = _sb_bwd(qkv, dosb, csb, nb, seq,
                                                beside([gwout, gwg2, gwu2, gwd2], True))
    dil_grads = [_dil_bwd(qkvd[i], bias[i], dodil[i], ldil[i], dldil[i], nb, seq, dil)
                 for i, dil in enumerate(DILATIONS)]
    dqkv = _merge_dqkv([dq_sb, dk_sb, dv_sb], [[g[k] for g in dil_grads] for k in range(3)], nb, tm)
    drel = _bias_blocks_bwd(jnp.stack([g[3] for g in dil_grads]))

    cs = win.shape[2]
    gwin = _mm_tn(h2, dqkv,
                  pl.BlockSpec((tg, d), lambda i, j: (i, 0)),
                  pl.BlockSpec((tg, cs), lambda i, j: (i, j)),
                  win.shape, t // tg, "grad_win")
    (dx1, dmod2a, dg_mix), parts_mid = _qkv_bwd(dqkv, dx2, x1, mod, g_mix, win, tp, beside([gwin], True))

    (dx0, dgate1, dup1, act1, h1, df1, dmod1, dg_ffn1), _ = _ffn_bwd(
        dx1, x0, f1, mod, g_ffn1, gate1, up1, wg1, wu1, wd1, 0, tp)
    gw1 = _ffn_weight_grads(h1, dgate1, dup1, act1, df1, tg, 0, stream=distributed)

    dmod = jnp.concatenate([dmod1[:, 0:3], dmod2a[:, 0:2], dmod2b[:, 2:3], dmod3[:, 0:3]], axis=1)
    wgrads = tuple(gw1) + (tuple(parts_mid + parts_late) if distributed else (gwin, gwout, gwg2, gwu2, gwd2))
    ggrads = (dg_ffn1[0:1], dg_mix[0:1], dg_heads[0:1], drel, dg_ffn2[0:1], dg_final)
    return loss_sum, dx0.reshape(nb, seq, d), wgrads, dmod, ggrads


def kernel(x, c, w_ada, b_ada, g_ffn1, w1_gate, w1_up, w1_down, g_mix, w_in, g_sb_out, g_dil_out, w_out, rel_bias, g_ffn2, w2_gate, w2_up, w2_down, g_final, loss_target, m_w_ada, m_b_ada, m_g_ffn1, m_w1_gate, m_w1_up, m_w1_down, m_g_mix, m_w_in, m_g_sb_out, m_g_dil_out, m_w_out, m_rel_bias, m_g_ffn2, m_w2_gate, m_w2_up, m_w2_down, m_g_final, v_w_ada, v_b_ada, v_g_ffn1, v_w1_gate, v_w1_up, v_w1_down, v_g_mix, v_w_in, v_g_sb_out, v_g_dil_out, v_w_out, v_rel_bias, v_g_ffn2, v_w2_gate, v_w2_up, v_w2_down, v_g_final):
    nb, seq, d = x.shape
    me = 4 * lax.axis_index("x") + 2 * lax.axis_index("y") + lax.axis_index("c")
    tm = min(TOKEN_TILE, seq)
    fs = w1_gate.shape[2]
    fs_pad = _lane_pad(fs)
    ada_cols = w_ada.shape[2]

    def col_shard(w):
        return _pad_to(w[0].astype(BF16), 1, fs_pad)

    def row_shard(w):
        return _pad_to(w[0].astype(BF16), 0, fs_pad)

    shards = [col_shard(w1_gate), col_shard(w1_up), row_shard(w1_down), w_in[0].astype(BF16),
              w_out[0].astype(BF16), col_shard(w2_gate), col_shard(w2_up), row_shard(w2_down)]
    b_cols = lax.dynamic_slice(b_ada, (0, me * ada_cols), (1, ada_cols))
    c_every, mod_all, first = _first_exchange(_pad_to(c, 0, 8), shards[:3], w_ada[0], b_cols)
    c_all = c_every[:, :nb].reshape(N_DEV * nb, d)
    weights = first + shards[3:]
    mod = lax.dynamic_slice(mod_all, (0, me * 8, 0), (N_DEV, nb, ada_cols))
    mod = mod.transpose(1, 0, 2).reshape(nb, N_MOD, d)

    n_sb = g_sb_out.shape[1] * g_sb_out.shape[2]
    gains = (g_ffn1, g_mix, g_sb_out.reshape(1, n_sb), g_dil_out.reshape(1, -1), g_ffn2,
             g_final.reshape(1, d))
    loss_sum, grad_x, parts, dmod, ggrads = _local_step(x, loss_target, mod, gains, weights, rel_bias, tm, True)
    loss = lax.psum(loss_sum, ("x", "y", "c"))

    dg_ffn1, dg_mix, dg_heads, drel, dg_ffn2, dg_final = ggrads
    width = max(d, dg_heads.shape[1], drel.size)
    small = jnp.concatenate(
        [_pad_to(a.reshape(1, -1), 1, width) for a in (dg_ffn1, dg_mix, dg_ffn2, dg_final, dg_heads, drel)]
        + [jnp.zeros((2, width), F32)], axis=0)
    dmod_pad = _pad_to(dmod.reshape(nb, N_MOD * d), 0, 8)
    last_part, dmod_all, small_all = _exchange(
        [parts[2], jnp.broadcast_to(dmod_pad, (N_DEV,) + dmod_pad.shape),
         jnp.broadcast_to(small, (N_DEV,) + small.shape)], True, "scatter_last", chips=[True, False, False])
    parts = parts[:2] + (last_part,) + parts[3:]
    dmod_all = dmod_all[:, :nb].reshape(N_DEV * nb, N_MOD * d)
    dmod_cols = lax.dynamic_slice(dmod_all, (0, me * ada_cols), (N_DEV * nb, ada_cols))
    gw_ada, gb_ada = _ada_bwd(c_all, dmod_cols, dmod_all)

    def small_part(row, size, shape):
        return small_all[:, row, :size].reshape((N_DEV,) + shape)

    n_rel = rel_bias.shape
    updates = {
        "w_ada": (w_ada[0], gw_ada[None], m_w_ada[0], v_w_ada[0]),
        "b_ada": (b_ada, gb_ada[None], m_b_ada, v_b_ada),
        "g_ffn1": (g_ffn1, small_part(0, d, (1, d)), m_g_ffn1, v_g_ffn1),
        "w1_gate": (w1_gate[0], parts[0][:, :fs, :], m_w1_gate[0], v_w1_gate[0]),
        "w1_up": (w1_up[0], parts[1][:, :fs, :], m_w1_up[0], v_w1_up[0]),
        "w1_down": (w1_down[0], parts[2][:, :fs, :], m_w1_down[0], v_w1_down[0]),
        "g_mix": (g_mix, small_part(1, d, (1, d)), m_g_mix, v_g_mix),
        "w_in": (w_in[0], parts[3], m_w_in[0], v_w_in[0]),
        "g_sb_out": (g_sb_out[0], small_all[:, 4, :n_sb].reshape((N_DEV,) + g_sb_out.shape[1:]),
                     m_g_sb_out[0], v_g_sb_out[0]),
        "g_dil_out": (g_dil_out[0], small_all[:, 4, n_sb:dg_heads.shape[1]].reshape((N_DEV,) + g_dil_out.shape[1:]),
                      m_g_dil_out[0], v_g_dil_out[0]),
        "w_out": (w_out[0], parts[4], m_w_out[0], v_w_out[0]),
        "rel_bias": (rel_bias, small_part(5, drel.size, n_rel), m_rel_bias, v_rel_bias),
        "g_ffn2": (g_ffn2, small_part(2, d, (1, d)), m_g_ffn2, v_g_ffn2),
        "w2_gate": (w2_gate[0], parts[5][:, :fs, :], m_w2_gate[0], v_w2_gate[0]),
        "w2_up": (w2_up[0], parts[6][:, :fs, :], m_w2_up[0], v_w2_up[0]),
        "w2_down": (w2_down[0], parts[7][:, :fs, :], m_w2_down[0], v_w2_down[0]),
        "g_final": (g_final.reshape(1, d), small_part(3, d, (1, d)), m_g_final.reshape(1, d), v_g_final.reshape(1, d)),
    }
    shapes = {"w_ada": w_ada.shape, "b_ada": b_ada.shape, "g_ffn1": g_ffn1.shape, "w1_gate": w1_gate.shape,
              "w1_up": w1_up.shape, "w1_down": w1_down.shape, "g_mix": g_mix.shape, "w_in": w_in.shape,
              "g_sb_out": g_sb_out.shape, "g_dil_out": g_dil_out.shape, "w_out": w_out.shape,
              "rel_bias": rel_bias.shape, "g_ffn2": g_ffn2.shape, "w2_gate": w2_gate.shape,
              "w2_up": w2_up.shape, "w2_down": w2_down.shape, "g_final": g_final.shape}
    grads, deltas, new_m, new_v = [], [], [], []
    for name, (w, p, m, v) in updates.items():
        transposed = name in ("w1_gate", "w1_up", "w2_gate", "w2_up")
        outs = _adamw(w, p, m, v, f"adamw_{name}", transposed)
        for dst, a in zip((grads, deltas, new_m, new_v), outs):
            dst.append((a.T if transposed else a).reshape(shapes[name]))
    return (loss, grad_x, *grads, *deltas, *new_m, *new_v)
```

```python
import functools
import math

import numpy as np
import jax
import jax.numpy as jnp
from jax import lax
from jax.experimental import pallas as pl
from jax.experimental.pallas import tpu as pltpu

F32 = jnp.float32
BF16 = jnp.bfloat16

EPS = 1e-6
NEG_INF = -1e30
HEAD_DIM = 64
LANES = 128
DIL_BLOCK = 128
DILATIONS = (1, 4, 16)
N_BUCKETS = 32
MAX_DISTANCE = 2048
N_MOD = 9
N_DEV = 8
SB_BLOCK = 256
SB_HEADS = 4
SB_WIDTH = SB_HEADS * HEAD_DIM
DIL_HEADS = 4
DIL_WIDTH = DIL_HEADS * HEAD_DIM
TOKEN_TILE = 512
PROJ_TILE = 1024
GRAD_TILE = 1024
FFN_CHUNKS = 2
VMEM_LIMIT_BYTES = 56 * 1024 * 1024

ADAM_LR = 0.001
ADAM_B1 = 0.9
ADAM_B2 = 0.999
ADAM_EPS = 1e-08
ADAM_WD = 0.01
ADAM_STEP = 10

NT_DIMS = (((1,), (1,)), ((), ()))
TN_DIMS = (((0,), (0,)), ((), ()))


def _params(*sem):
    return pltpu.CompilerParams(dimension_semantics=sem, vmem_limit_bytes=VMEM_LIMIT_BYTES)


def _once(spec):
    return pl.BlockSpec(spec.block_shape, spec.index_map, pipeline_mode=pl.Buffered(1))


def _dot(a, b):
    return jnp.dot(a, b, preferred_element_type=F32)


def _dot_nt(a, b):
    return lax.dot_general(a, b, NT_DIMS, preferred_element_type=F32)


def _dot_tn(a, b):
    return lax.dot_general(a, b, TN_DIMS, preferred_element_type=F32)


def _split_dot(a, b):
    hi = a.astype(BF16)
    lo = (a - hi.astype(F32)).astype(BF16)
    return _dot(hi, b) + _dot(lo, b)


def _sigmoid(z):
    return 1.0 / (1.0 + jnp.exp(-z))


def _norm(x):
    r = lax.rsqrt(jnp.mean(x * x, axis=-1, keepdims=True) + EPS)
    return x * r, r


def _modulate(x, g, mod_ref, k):
    n, _ = _norm(x)
    shift = mod_ref[3 * k:3 * k + 1, :]
    scale = mod_ref[3 * k + 1:3 * k + 2, :]
    return n * g * (1.0 + scale) + shift


def _modulate_bwd(dh, x, g, mod_ref, k):
    n, r = _norm(x)
    scale = mod_ref[3 * k + 1:3 * k + 2, :]
    dshift = jnp.sum(dh, axis=0, keepdims=True)
    dscale = jnp.sum(dh * n * g, axis=0, keepdims=True)
    dg = jnp.sum(dh * n * (1.0 + scale), axis=0, keepdims=True)
    dn = dh * g * (1.0 + scale)
    dx = r * (dn - n * jnp.mean(dn * n, axis=-1, keepdims=True))
    return dx, dshift, dscale, dg


class _Exchange:
    def __init__(self, arrays, scatter, relay=False, chips=None):
        assert not (scatter and relay)
        self.arrays = list(arrays)
        self.scatter = scatter
        self.relay = relay
        self.n = len(self.arrays)
        self.chips = list(chips) if chips is not None else [False] * self.n
        assert scatter or not any(self.chips)
        self.out_shape = [
            jax.ShapeDtypeStruct((N_DEV // 2 if ch else N_DEV,) + tuple(a.shape[1:] if scatter else a.shape), a.dtype)
            for a, ch in zip(self.arrays, self.chips)]
        n_remote = self.n * (N_DEV - 1)
        self.scratch_shapes = [pltpu.SemaphoreType.DMA((n_remote,)), pltpu.SemaphoreType.DMA((n_remote,)),
                               pltpu.SemaphoreType.DMA((self.n,))]

    def _copies(self, in_refs, out_refs, sems):
        send_sems, recv_sems, local_sems = sems
        x, y, c = lax.axis_index("x"), lax.axis_index("y"), lax.axis_index("c")
        me = 4 * x + 2 * y + c
        local, remote, relayed = [], {}, {}
        for a in range(self.n):
            if self.chips[a]:
                mine = 2 * x + y
                local.append(pltpu.make_async_copy(in_refs[a].at[mine], out_refs[a].at[mine], local_sems.at[a]))
                for k in (2, 4, 6):
                    px = 1 - x if k & 4 else x
                    py = 1 - y if k & 2 else y
                    sem = a * (N_DEV - 1) + k - 1
                    remote[a, k] = pltpu.make_async_remote_copy(
                        src_ref=in_refs[a].at[2 * px + py], dst_ref=out_refs[a].at[mine],
                        send_sem=send_sems.at[sem], recv_sem=recv_sems.at[sem],
                        device_id=(px, py, c), device_id_type=pl.DeviceIdType.MESH)
                continue
            src = in_refs[a].at[me] if self.scatter else in_refs[a]
            local.append(pltpu.make_async_copy(src, out_refs[a].at[me], local_sems.at[a]))
            for k in range(1, N_DEV):
                px = 1 - x if k & 4 else x
                py = 1 - y if k & 2 else y
                pc = 1 - c if k & 1 else c
                sem = a * (N_DEV - 1) + k - 1
                if self.relay and k & 1 and k > 1:
                    slot = 4 * px + 2 * py + c
                    relayed[a, k] = pltpu.make_async_remote_copy(
                        src_ref=out_refs[a].at[slot], dst_ref=out_refs[a].at[slot],
                        send_sem=send_sems.at[sem], recv_sem=recv_sems.at[sem],
                        device_id=(x, y, 1 - c), device_id_type=pl.DeviceIdType.MESH)
                    continue
                src = in_refs[a].at[4 * px + 2 * py + pc] if self.scatter else in_refs[a]
                remote[a, k] = pltpu.make_async_remote_copy(
                    src_ref=src, dst_ref=out_refs[a].at[me],
                    send_sem=send_sems.at[sem], recv_sem=recv_sems.at[sem],
                    device_id=(px, py, pc), device_id_type=pl.DeviceIdType.MESH)
        return local, remote, relayed

    def start(self, in_refs, out_refs, sems):
        local, remote, _ = self._copies(in_refs, out_refs, sems)
        for cp in local + list(remote.values()):
            cp.start()

    def wait(self, in_refs, out_refs, sems):
        local, remote, relayed = self._copies(in_refs, out_refs, sems)
        for (a, k), cp in relayed.items():
            remote[a, k - 1].wait_recv()
            cp.start()
        for (a, k), cp in remote.items():
            if (a, k + 1) not in relayed:
                cp.wait_recv()
        for cp in relayed.values():
            cp.wait_recv()
        for cp in list(remote.values()) + list(relayed.values()):
            cp.wait_send()
        for cp in local:
            cp.wait()


def _call(body, *, name, args, in_specs, out_specs, out_shape, scratch_shapes=(), grid=(),
          params=None, exchange=None):
    n_in, n_out = len(args), len(out_shape)
    if exchange is None:
        outs = pl.pallas_call(
            body, name=name, grid=grid, in_specs=list(in_specs), out_specs=list(out_specs),
            out_shape=list(out_shape), scratch_shapes=list(scratch_shapes), compiler_params=params,
        )(*args)
        return list(outs), []
    n_ex = exchange.n

    def wrapped(*refs):
        ins, refs = refs[:n_in], refs[n_in:]
        ex_in, refs = refs[:n_ex], refs[n_ex:]
        outs, refs = refs[:n_out], refs[n_out:]
        ex_out, refs = refs[:n_ex], refs[n_ex:]
        scratch, sems = refs[:len(refs) - 3], refs[len(refs) - 3:]
        if not grid:
            exchange.start(ex_in, ex_out, sems)
            body(*ins, *outs, *scratch)
            exchange.wait(ex_in, ex_out, sems)
            return
        first = functools.reduce(jnp.logical_and, [pl.program_id(a) == 0 for a in range(len(grid))])
        last = functools.reduce(jnp.logical_and, [pl.program_id(a) == grid[a] - 1 for a in range(len(grid))])

        @pl.when(first)
        def _():
            exchange.start(ex_in, ex_out, sems)

        body(*ins, *outs, *scratch)

        @pl.when(last)
        def _():
            exchange.wait(ex_in, ex_out, sems)

    any_spec = pl.BlockSpec(memory_space=pl.ANY)
    outs = pl.pallas_call(
        wrapped, name=name, grid=grid,
        in_specs=list(in_specs) + [any_spec] * n_ex, out_specs=list(out_specs) + [any_spec] * n_ex,
        out_shape=list(out_shape) + exchange.out_shape,
        scratch_shapes=list(scratch_shapes) + exchange.scratch_shapes, compiler_params=params,
    )(*args, *exchange.arrays)
    return list(outs[:n_out]), list(outs[n_out:])


def _exchange(arrays, scatter, name, relay=False, chips=None):
    return _call(lambda: None, name=name, args=(), in_specs=(), out_specs=(), out_shape=(),
                 exchange=_Exchange(arrays, scatter, relay, chips))[1]


def _first_exchange(c_pad, w, b):
    rows, d = c_pad.shape
    cols = w.shape[1]
    ex_c = _Exchange([c_pad], False)
    ex_m = _Exchange([jax.ShapeDtypeStruct((N_DEV * rows, cols), F32)], False)

    def body(c_ref, wa_ref, b_ref, cg_ref, mg_ref, *scratch):
        sems_c, sems_m, c_vm, m_vm = scratch[0:3], scratch[3:6], scratch[6], scratch[7]
        ex_c.start([c_ref], [cg_ref], sems_c)
        ex_c.wait([c_ref], [cg_ref], sems_c)
        pltpu.sync_copy(cg_ref, c_vm)
        cv = c_vm[...].reshape(N_DEV * rows, d)
        s = (cv * _sigmoid(cv)).astype(BF16)
        m_vm[...] = _dot(s, wa_ref[...].astype(BF16)) + b_ref[...]
        ex_m.start([m_vm], [mg_ref], sems_m)
        ex_m.wait([m_vm], [mg_ref], sems_m)

    any_spec = pl.BlockSpec(memory_space=pl.ANY)
    vmem_spec = pl.BlockSpec(memory_space=pltpu.VMEM)
    return pl.pallas_call(
        body, name="first_exchange",
        in_specs=[any_spec, vmem_spec, vmem_spec], out_specs=[any_spec, any_spec],
        out_shape=ex_c.out_shape + ex_m.out_shape,
        scratch_shapes=ex_c.scratch_shapes + ex_m.scratch_shapes
        + [pltpu.VMEM((N_DEV, rows, d), F32), pltpu.VMEM((N_DEV * rows, cols), F32)],
        compiler_params=pltpu.CompilerParams(vmem_limit_bytes=VMEM_LIMIT_BYTES),
    )(c_pad, w, b)


def _ada_bwd(c_all, dmod_cols, dmod_all):
    def body(c_ref, dc_ref, da_ref, gw_ref, gb_ref):
        cv = c_ref[...]
        s = cv * _sigmoid(cv)
        gw_ref[...] = lax.dot_general(s, dc_ref[...], TN_DIMS, preferred_element_type=F32,
                                      precision=lax.Precision.HIGHEST)
        gb_ref[...] = jnp.sum(da_ref[...], axis=0, keepdims=True)

    return pl.pallas_call(
        body, name="ada_bwd",
        out_shape=(jax.ShapeDtypeStruct((c_all.shape[1], dmod_cols.shape[1]), F32),
                   jax.ShapeDtypeStruct((1, dmod_all.shape[1]), F32)),
        compiler_params=pltpu.CompilerParams(vmem_limit_bytes=VMEM_LIMIT_BYTES),
    )(c_all, dmod_cols, dmod_all)


def _ffn_fwd(x, mod, g, wg, wu, wd, k, tm, exchange=None):
    t, d = x.shape
    ns, _, fs = wg.shape
    nt = t // tm
    tpb = nt // mod.shape[0]
    rows = tm // FFN_CHUNKS

    def body(x_ref, mod_ref, g_ref, wg_ref, wu_ref, wd_ref, xo_ref, f_ref, gg_ref, uu_ref, h_sc, acc):
        j = pl.program_id(1)

        @pl.when(j == 0)
        def _():
            h_sc[...] = _modulate(x_ref[...], g_ref[...], mod_ref, k).astype(BF16)
            acc[...] = jnp.zeros_like(acc)

        chunks = [pl.ds(c * rows, rows) for c in range(FFN_CHUNKS)]
        wg, wu, wd = wg_ref[...], wu_ref[...], wd_ref[...]
        gates, ups = [], []
        for rs in chunks:
            h = h_sc[rs, :]
            gates.append(_dot(h, wg))
            ups.append(_dot(h, wu))
        acts = [(g * _sigmoid(g) * u).astype(BF16) for g, u in zip(gates, ups)]
        for rs, g, u in zip(chunks, gates, ups):
            gg_ref[rs, :] = g.astype(BF16)
            uu_ref[rs, :] = u.astype(BF16)
        downs = [_dot(a, wd) for a in acts]
        for rs, dn in zip(chunks, downs):
            acc[rs, :] += dn

        @pl.when(j == ns - 1)
        def _():
            f = acc[...]
            f_ref[...] = f.astype(BF16)
            xo_ref[...] = x_ref[...] + 0.5 * mod_ref[3 * k + 2:3 * k + 3, :] * f

    tok = pl.BlockSpec((tm, d), lambda i, j: (i, 0))
    hid = pl.BlockSpec((None, tm, fs), lambda i, j: (j, i, 0))
    return _call(
        body, name=f"ffn_fwd{k}", grid=(nt, ns), args=(x, mod, g, wg, wu, wd),
        in_specs=[tok,
                  pl.BlockSpec((None, N_MOD, d), lambda i, j: (i // tpb, 0, 0)),
                  pl.BlockSpec((1, d), lambda i, j: (0, 0)),
                  pl.BlockSpec((None, d, fs), lambda i, j: (j, 0, 0)),
                  pl.BlockSpec((None, d, fs), lambda i, j: (j, 0, 0)),
                  pl.BlockSpec((None, fs, d), lambda i, j: (j, 0, 0))],
        out_specs=[tok, tok, hid, hid],
        out_shape=[jax.ShapeDtypeStruct((t, d), F32), jax.ShapeDtypeStruct((t, d), BF16),
                   jax.ShapeDtypeStruct((ns, t, fs), BF16), jax.ShapeDtypeStruct((ns, t, fs), BF16)],
        scratch_shapes=[pltpu.VMEM((tm, d), BF16), pltpu.VMEM((tm, d), F32)],
        params=_params("arbitrary", "arbitrary"), exchange=exchange)


def _ffn_fwd_gathering(x, mod, g, shards, me, tm, later):
    t, d = x.shape
    fs = shards[0].shape[1]
    nt = t // tm
    tpb = nt // mod.shape[0]
    ns = N_DEV
    ex = _Exchange(shards, False, relay=True)
    ex_later = _Exchange(later, False)
    n_later = ex_later.n

    def body(me_ref, x_ref, mod_ref, g_ref, wg_s, wu_s, wd_s, *rest):
        later_in, rest = rest[:n_later], rest[n_later:]
        xo_ref, f_ref, gg_ref, uu_ref, wg_a, wu_a, wd_a = rest[:7]
        later_out, rest = rest[7:7 + n_later], rest[7 + n_later:]
        acc, h_sc, wg_v, wu_v, wd_v, send_sems, recv_sems, local_sems = rest[:8]
        later_sems = rest[8:]
        k, i = pl.program_id(0), pl.program_id(1)
        own, every = [wg_s, wu_s, wd_s], [wg_a, wu_a, wd_a]
        local, remote, relayed = ex._copies(own, every, (send_sems, recv_sems, local_sems))
        arrays = range(ex.n)
        rows = pl.ds(pl.multiple_of(i * tm, tm), tm)

        @pl.when(jnp.logical_and(k == 0, i == 0))
        def _():
            for cp in local:
                cp.start()
            for peer in (1, 2, 4, 6):
                for a in arrays:
                    remote[a, peer].start()
            ex_later.start(later_in, later_out, later_sems)

        @pl.when(i == 0)
        def _():
            for peer in range(ns):
                @pl.when(k == peer)
                def _(peer=peer):
                    for a in arrays:
                        if peer == 0:
                            local[a].wait()
                        elif peer == 1:
                            remote[a, peer].wait_recv()
                        elif peer % 2 == 0:
                            remote[a, peer].wait_recv()
                            relayed[a, peer + 1].start()
                        else:
                            relayed[a, peer].wait_recv()

            shard = jnp.bitwise_xor(me_ref[0], k)
            for src, dst in zip(every, (wg_v, wu_v, wd_v)):
                pltpu.sync_copy(src.at[shard], dst)

        @pl.when(k == 0)
        def _():
            h_sc[rows, :] = _modulate(x_ref[...], g_ref[...], mod_ref, 0).astype(BF16)
            acc[rows, :] = jnp.zeros((tm, d), F32)

        h = h_sc[rows, :]
        gate = _dot(h, wg_v[...])
        up = _dot(h, wu_v[...])
        act = gate * _sigmoid(gate) * up
        acc[rows, :] += _dot(act.astype(BF16), wd_v[...])
        gg_ref[...] = gate.astype(BF16)
        uu_ref[...] = up.astype(BF16)

        @pl.when(k == ns - 1)
        def _():
            f = acc[rows, :]
            f_ref[...] = f.astype(BF16)
            xo_ref[...] = x_ref[...] + 0.5 * mod_ref[2:3, :] * f

            @pl.when(i == nt - 1)
            def _():
                for cp in list(remote.values()) + list(relayed.values()):
                    cp.wait_send()
                ex_later.wait(later_in, later_out, later_sems)

    def edge_tile(k, i):
        return jnp.where(jnp.logical_or(k == 0, k == ns - 1), i, 0)

    def last_tile(k, i):
        return jnp.where(k == ns - 1, i, 0)

    any_spec = pl.BlockSpec(memory_space=pl.ANY)
    hid = pl.BlockSpec((None, tm, fs), lambda k, i, me_ref: (jnp.bitwise_xor(me_ref[0], k), i, 0))
    grid_spec = pltpu.PrefetchScalarGridSpec(
        num_scalar_prefetch=1, grid=(ns, nt),
        in_specs=[pl.BlockSpec((tm, d), lambda k, i, me_ref: (edge_tile(k, i), 0)),
                  pl.BlockSpec((None, N_MOD, d), lambda k, i, me_ref: (edge_tile(k, i) // tpb, 0, 0)),
                  pl.BlockSpec((1, d), lambda k, i, me_ref: (0, 0)),
                  any_spec, any_spec, any_spec] + [any_spec] * n_later,
        out_specs=[pl.BlockSpec((tm, d), lambda k, i, me_ref: (last_tile(k, i), 0)),
                   pl.BlockSpec((tm, d), lambda k, i, me_ref: (last_tile(k, i), 0)),
                   hid, hid, any_spec, any_spec, any_spec] + [any_spec] * n_later,
        scratch_shapes=[pltpu.VMEM((t, d), F32), pltpu.VMEM((t, d), BF16),
                        pltpu.VMEM(shards[0].shape, BF16), pltpu.VMEM(shards[1].shape, BF16),
                        pltpu.VMEM(shards[2].shape, BF16)] + ex.scratch_shapes + ex_later.scratch_shapes)
    outs = pl.pallas_call(
        body, name="ffn_fwd_gathering", grid_spec=grid_spec,
        out_shape=[jax.ShapeDtypeStruct((t, d), F32), jax.ShapeDtypeStruct((t, d), BF16),
                   jax.ShapeDtypeStruct((ns, t, fs), BF16), jax.ShapeDtypeStruct((ns, t, fs), BF16)]
        + ex.out_shape + ex_later.out_shape,
        compiler_params=_params("arbitrary", "arbitrary"),
    )(me, x, mod, g, *shards, *later)
    return outs[:4], list(outs[4:7]), list(outs[7:])


def _ffn_bwd(dxo, x, f, mod, g, gate, up, wg, wu, wd, k, tm, exchange=None):
    t, d = x.shape
    ns, _, fs = wg.shape
    nt = t // tm
    nb = mod.shape[0]
    tpb = nt // nb
    rows = tm // FFN_CHUNKS

    def body(dxo_ref, x_ref, f_ref, mod_ref, g_ref, gg_ref, uu_ref, wg_ref, wu_ref, wd_ref,
             dx_ref, dgg_ref, duu_ref, act_ref, h_ref, df_ref, dmod_ref, dg_ref, acc):
        i, j = pl.program_id(0), pl.program_id(1)

        @pl.when(j == 0)
        def _():
            df = 0.5 * mod_ref[3 * k + 2:3 * k + 3, :] * dxo_ref[...]
            df_ref[...] = df.astype(BF16)
            h_ref[...] = _modulate(x_ref[...], g_ref[...], mod_ref, k).astype(BF16)
            acc[...] = jnp.zeros_like(acc)

        chunks = [pl.ds(c * rows, rows) for c in range(FFN_CHUNKS)]
        wg, wu, wd = wg_ref[...], wu_ref[...], wd_ref[...]
        dacts = [_dot_nt(df_ref[rs, :], wd) for rs in chunks]
        dgates, dups = [], []
        for rs, dact in zip(chunks, dacts):
            gv, uv = gg_ref[rs, :].astype(F32), uu_ref[rs, :].astype(F32)
            sig = _sigmoid(gv)
            s = gv * sig
            act_ref[rs, :] = (s * uv).astype(BF16)
            dups.append((dact * s).astype(BF16))
            dgates.append((dact * uv * (sig * (1.0 + gv * (1.0 - sig)))).astype(BF16))
        dhs = [_dot_nt(dg, wg) + _dot_nt(du, wu) for dg, du in zip(dgates, dups)]
        for rs, dg, du, dh in zip(chunks, dgates, dups, dhs):
            dgg_ref[rs, :] = dg
            duu_ref[rs, :] = du
            acc[rs, :] += dh

        @pl.when(j == ns - 1)
        def _():
            dx, dshift, dscale, dg = _modulate_bwd(acc[...], x_ref[...], g_ref[...], mod_ref, k)
            dxo_v = dxo_ref[...]
            dx_ref[...] = dxo_v + dx
            dgt = jnp.sum(0.5 * f_ref[...].astype(F32) * dxo_v, axis=0, keepdims=True)

            @pl.when(i % tpb == 0)
            def _():
                dmod_ref[...] = jnp.zeros_like(dmod_ref)

            @pl.when(i == 0)
            def _():
                dg_ref[...] = jnp.zeros_like(dg_ref)

            dmod_ref[0:1, :] += dshift
            dmod_ref[1:2, :] += dscale
            dmod_ref[2:3, :] += dgt
            dg_ref[0:1, :] += dg

    tok = pl.BlockSpec((tm, d), lambda i, j: (i, 0))
    hid = pl.BlockSpec((None, tm, fs), lambda i, j: (j, i, 0))
    return _call(
        body, name=f"ffn_bwd{k}", grid=(nt, ns), args=(dxo, x, f, mod, g, gate, up, wg, wu, wd),
        in_specs=[tok, _once(tok), _once(tok),
                  pl.BlockSpec((None, N_MOD, d), lambda i, j: (i // tpb, 0, 0)),
                  pl.BlockSpec((1, d), lambda i, j: (0, 0)),
                  hid, hid,
                  pl.BlockSpec((None, d, fs), lambda i, j: (j, 0, 0)),
                  pl.BlockSpec((None, d, fs), lambda i, j: (j, 0, 0)),
                  pl.BlockSpec((None, fs, d), lambda i, j: (j, 0, 0))],
        out_specs=[tok, hid, hid, hid, tok, tok,
                   pl.BlockSpec((None, 8, d), lambda i, j: (i // tpb, 0, 0)),
                   pl.BlockSpec((8, d), lambda i, j: (0, 0))],
        out_shape=[jax.ShapeDtypeStruct((t, d), F32),
                   jax.ShapeDtypeStruct((ns, t, fs), BF16), jax.ShapeDtypeStruct((ns, t, fs), BF16),
                   jax.ShapeDtypeStruct((ns, t, fs), BF16),
                   jax.ShapeDtypeStruct((t, d), BF16), jax.ShapeDtypeStruct((t, d), BF16),
                   jax.ShapeDtypeStruct((nb, 8, d), F32), jax.ShapeDtypeStruct((8, d), F32)],
        scratch_shapes=[pltpu.VMEM((tm, d), F32)],
        params=_params("arbitrary", "arbitrary"), exchange=exchange)


def _mm_tn(a, b, a_spec, b_spec, out_shape, n_tiles, name, exchange=None, keep_transposed=False,
           pair_reduce=False):
    n_out = out_shape[0]
    block = tuple(out_shape[1:])
    last = n_tiles - 1
    flip = block[0] > block[1]
    if flip:
        block = block[::-1]
    if flip and keep_transposed:
        flip_back, out_shape = False, (n_out,) + block
    else:
        flip_back = flip
    full_shape = tuple(out_shape)
    n_pairs = n_out // 2
    if pair_reduce:
        out_shape = (n_pairs,) + full_shape[1:]

    def body(a_ref, b_ref, o_ref, acc, *pair):
        i, j = pl.program_id(0), pl.program_id(1)
        prod = _dot_tn(b_ref[...], a_ref[...]) if flip else _dot_tn(a_ref[...], b_ref[...])
        full_ref = pair[0] if pair_reduce else o_ref

        @pl.when(i == 0)
        def _():
            acc[j] = prod

        @pl.when(i > 0)
        def _():
            acc[j] += prod

        @pl.when(i == last)
        def _():
            total = acc[j]
            full_ref[j] = (total.T if flip_back else total).astype(BF16)

        if pair_reduce:
            _, landed, send_sems, recv_sems = pair

            @pl.when(jnp.logical_and(i == last, j == n_out - 1))
            def _():
                x, y, c = lax.axis_index("x"), lax.axis_index("y"), lax.axis_index("c")
                copies = [pltpu.make_async_remote_copy(
                    src_ref=full_ref.at[2 * q + 1 - c], dst_ref=landed.at[q],
                    send_sem=send_sems.at[q], recv_sem=recv_sems.at[q],
                    device_id=(x, y, 1 - c), device_id_type=pl.DeviceIdType.MESH) for q in range(n_pairs)]
                for cp in copies:
                    cp.start()
                for q, cp in enumerate(copies):
                    cp.wait_recv()
                    o_ref[q] = (full_ref[2 * q + c].astype(F32) + landed[q].astype(F32)).astype(BF16)
                for cp in copies:
                    cp.wait_send()

    scratch = [pltpu.VMEM((n_out,) + block, F32)]
    if pair_reduce:
        scratch += [pltpu.VMEM(full_shape, BF16), pltpu.VMEM(out_shape, BF16),
                    pltpu.SemaphoreType.DMA((n_pairs,)), pltpu.SemaphoreType.DMA((n_pairs,))]
    outs, sent = _call(
        body, name=name, grid=(n_tiles, n_out), args=(a, b), in_specs=[a_spec, b_spec],
        out_specs=[pl.BlockSpec(out_shape, lambda i, j: (0,) * len(out_shape))],
        out_shape=[jax.ShapeDtypeStruct(out_shape, BF16)],
        scratch_shapes=scratch,
        params=_params("arbitrary", "arbitrary"), exchange=exchange)
    return (outs[0], sent) if exchange is not None else outs[0]


def _ffn_weight_grads(h, dgate, dup, act, df, tm, tag, stream=False):
    t, d = h.shape
    ns, _, fs = dgate.shape
    nt = t // tm
    tok = pl.BlockSpec((tm, d), lambda i, j: (i, 0))
    hid = pl.BlockSpec((None, tm, fs), lambda i, j: (j, i, 0))
    if not stream:
        gwg = _mm_tn(h, dgate, tok, hid, (ns, d, fs), nt, f"grad_wg{tag}", keep_transposed=True)
        gwu = _mm_tn(h, dup, tok, hid, (ns, d, fs), nt, f"grad_wu{tag}", keep_transposed=True)
        gwd = _mm_tn(act, df, hid, tok, (ns, fs, d), nt, f"grad_wd{tag}")
        return gwg, gwu, gwd
    gwg = _mm_tn(h, dgate, tok, hid, (ns, d, fs), nt, f"grad_wg{tag}", keep_transposed=True, pair_reduce=True)
    gwu, sent_g = _mm_tn(h, dup, tok, hid, (ns, d, fs), nt, f"grad_wu{tag}",
                         _Exchange([gwg], True, chips=[True]), keep_transposed=True, pair_reduce=True)
    gwd, sent_u = _mm_tn(act, df, hid, tok, (ns, fs, d), nt, f"grad_wd{tag}",
                         _Exchange([gwu], True, chips=[True]), pair_reduce=True)
    return sent_g[0], sent_u[0], gwd


def _stage_shape(rows, cols):
    return pltpu.VMEM((cols // LANES, rows, LANES), F32)


def _stage(value, stage_ref):
    for k in range(stage_ref.shape[0]):
        stage_ref[k] = value[:, k * LANES:(k + 1) * LANES]


def _to_residue_rows(stage_ref, dst_ref, dil):
    rows = stage_ref.shape[1] // dil
    for r in range(dil):
        for k in range(stage_ref.shape[0]):
            dst_ref[r, :, k * LANES:(k + 1) * LANES] = (
                stage_ref.at[k][pl.ds(r, rows, stride=dil), :].astype(dst_ref.dtype))


def _from_residue_rows(src_ref, stage_ref, dil):
    rows = stage_ref.shape[1] // dil
    chunks = range(stage_ref.shape[0])
    for r in range(dil):
        for k in chunks:
            stage_ref.at[k][pl.ds(r, rows, stride=dil), :] = src_ref[r, :, k * LANES:(k + 1) * LANES].astype(F32)
    return jnp.concatenate([stage_ref[k] for k in chunks], axis=1)


def _residue_shape(nb, seq, width, dil, dtype):
    return jax.ShapeDtypeStruct((nb, dil, seq // dil, width), dtype)


def _residue_spec(tm, tpb, cols, dil, col_block):
    return pl.BlockSpec((None, dil, tm // dil, cols),
                        lambda i, *rest: (i // tpb, 0, i % tpb, col_block(i, *rest)))


def _qkv_fwd(x, mod, g, win, tm):
    t, d = x.shape
    ns, _, cs = win.shape
    nt = t // tm
    nb = mod.shape[0]
    tpb = nt // nb
    seq = t // nb
    half = ns // 2
    n_res = len(DILATIONS) - 1

    def body(x_ref, mod_ref, g_ref, w_ref, sb_ref, dil_ref, *rest):
        res_refs, h_ref, sc = rest[:n_res], rest[n_res], rest[n_res + 1]
        j = pl.program_id(1)

        @pl.when(j == 0)
        def _():
            h_ref[...] = _modulate(x_ref[...], g_ref[...], mod_ref, 1).astype(BF16)

        res = _dot(h_ref[...], w_ref[...])

        @pl.when(j < half)
        def _():
            sb_ref[...] = res.astype(BF16)

        @pl.when(j >= half)
        def _():
            dil_ref[...] = res.astype(BF16)
            _stage(res, sc)
            for ref, dil in zip(res_refs, DILATIONS[1:]):
                _to_residue_rows(sc, ref, dil)

    def dil_col(i, j):
        return jnp.maximum(j - half, 0)

    tok = pl.BlockSpec((tm, d), lambda i, j: (i, 0))
    wide = jax.ShapeDtypeStruct((t, half * cs), BF16)
    outs = pl.pallas_call(
        body, name="qkv_fwd", grid=(nt, ns),
        in_specs=[tok,
                  pl.BlockSpec((None, N_MOD, d), lambda i, j: (i // tpb, 0, 0)),
                  pl.BlockSpec((1, d), lambda i, j: (0, 0)),
                  pl.BlockSpec((None, d, cs), lambda i, j: (j, 0, 0))],
        out_specs=[pl.BlockSpec((tm, cs), lambda i, j: (i, jnp.minimum(j, half - 1))),
                   pl.BlockSpec((tm, cs), lambda i, j: (i, dil_col(i, j)))]
        + [_residue_spec(tm, tpb, cs, dil, dil_col) for dil in DILATIONS[1:]] + [tok],
        out_shape=[wide, wide] + [_residue_shape(nb, seq, half * cs, dil, BF16) for dil in DILATIONS[1:]]
        + [jax.ShapeDtypeStruct((t, d), BF16)],
        scratch_shapes=[_stage_shape(tm, cs)],
        compiler_params=_params("arbitrary", "arbitrary"),
    )(x, mod, g, win)
    qkv_dil = [outs[1]] + [a.reshape(t, half * cs) for a in outs[2:2 + n_res]]
    return outs[0], qkv_dil, outs[-1]


def _qkv_bwd(dqkv, dxo, x, mod, g, win, tm, exchange=None):
    t, d = x.shape
    ns, _, cs = win.shape
    nt = t // tm
    nb = mod.shape[0]
    tpb = nt // nb

    def body(dq_ref, dxo_ref, x_ref, mod_ref, g_ref, w_ref, dx_ref, dmod_ref, dg_ref, acc):
        i, j = pl.program_id(0), pl.program_id(1)

        @pl.when(j == 0)
        def _():
            acc[...] = jnp.zeros_like(acc)

        acc[...] += _dot_nt(dq_ref[...], w_ref[...])

        @pl.when(j == ns - 1)
        def _():
            dx, dshift, dscale, dg = _modulate_bwd(acc[...], x_ref[...], g_ref[...], mod_ref, 1)
            dx_ref[...] = dxo_ref[...] + dx

            @pl.when(i % tpb == 0)
            def _():
                dmod_ref[...] = jnp.zeros_like(dmod_ref)

            @pl.when(i == 0)
            def _():
                dg_ref[...] = jnp.zeros_like(dg_ref)

            dmod_ref[0:1, :] += dshift
            dmod_ref[1:2, :] += dscale
            dg_ref[0:1, :] += dg

    tok = pl.BlockSpec((tm, d), lambda i, j: (i, 0))
    return _call(
        body, name="qkv_bwd", grid=(nt, ns), args=(dqkv, dxo, x, mod, g, win),
        in_specs=[pl.BlockSpec((tm, cs), lambda i, j: (i, j)), tok, tok,
                  pl.BlockSpec((None, N_MOD, d), lambda i, j: (i // tpb, 0, 0)),
                  pl.BlockSpec((1, d), lambda i, j: (0, 0)),
                  pl.BlockSpec((None, d, cs), lambda i, j: (j, 0, 0))],
        out_specs=[tok,
                   pl.BlockSpec((None, 8, d), lambda i, j: (i // tpb, 0, 0)),
                   pl.BlockSpec((8, d), lambda i, j: (0, 0))],
        out_shape=[jax.ShapeDtypeStruct((t, d), F32),
                   jax.ShapeDtypeStruct((nb, 8, d), F32), jax.ShapeDtypeStruct((8, d), F32)],
        scratch_shapes=[pltpu.VMEM((tm, d), F32)],
        params=_params("arbitrary", "arbitrary"), exchange=exchange)


def _heads(a):
    return [a[:, h * HEAD_DIM:(h + 1) * HEAD_DIM] for h in range(a.shape[1] // HEAD_DIM)]


def _own_lanes():
    lane = lax.broadcasted_iota(jnp.int32, (1, LANES), 1)
    return [lane < HEAD_DIM, lane >= HEAD_DIM]


def _pair_tiles(a):
    return [a[:, (h // 2) * LANES:(h // 2 + 1) * LANES] for h in range(a.shape[1] // HEAD_DIM)]


def _own_tiles(a, own):
    return [jnp.where(own[h % 2], tile, jnp.zeros_like(tile)) for h, tile in enumerate(_pair_tiles(a))]


def _merge_tiles(per_head, own):
    return jnp.concatenate([jnp.where(own[0], per_head[h], per_head[h + 1])
                            for h in range(0, len(per_head), 2)], axis=1)


def _scaled(q):
    return (q.astype(F32) * (HEAD_DIM ** -0.5)).astype(BF16)


def _sb_logits(qh, kh, tri, causal):
    zs = [_dot_nt(q, k) for q, k in zip(qh, kh)]
    es = [jnp.exp(-jnp.abs(z)) for z in zs]
    log_nots = [-(jnp.maximum(z, 0.0) + jnp.log(1.0 + e)) for z, e in zip(zs, es)]
    if causal is not None:
        log_nots = [jnp.where(causal, ln, 0.0) for ln in log_nots]
    return zs, es, [_split_dot(ln, tri) for ln in log_nots]


def _sb_masks():
    rows = lax.broadcasted_iota(jnp.int32, (SB_BLOCK, SB_BLOCK), 0)
    cols = lax.broadcasted_iota(jnp.int32, (SB_BLOCK, SB_BLOCK), 1)
    return (rows >= cols).astype(BF16), (rows <= cols).astype(BF16), cols < rows


def _sb_fwd(qkv, nb, seq, exchange=None):
    t = qkv.shape[0]
    n_pairs = (qkv.shape[1] // 3) // SB_WIDTH
    tb = SB_BLOCK
    n_blk = seq // tb

    def body(q_ref, k_ref, v_ref, o_ref, c_ref):
        tri, _, causal = _sb_masks()
        own = _own_lanes()

        def key_block(qh, kj, carry, mask):
            ks = pl.multiple_of(kj * tb, tb)
            kh, vh = _pair_tiles(k_ref[pl.ds(ks, tb), :]), _pair_tiles(v_ref[pl.ds(ks, tb), :])
            zs, _, suffixes = _sb_logits(qh, kh, tri, mask)
            ws = [jnp.exp(z + suffix + cr[1]) for z, suffix, cr in zip(zs, suffixes, carry)]
            if mask is not None:
                ws = [jnp.where(mask, w, 0.0) for w in ws]
            pv = [_dot(w.astype(BF16), v) for w, v in zip(ws, vh)]
            return tuple((cr[0] + p, cr[1] + suffix[:, 0:1]) for cr, p, suffix in zip(carry, pv, suffixes))

        def query_block(qi, _):
            qs = pl.multiple_of(qi * tb, tb)
            qh = _own_tiles(_scaled(q_ref[pl.ds(qs, tb), :]), own)
            zero = (jnp.zeros((tb, LANES), F32), jnp.zeros((tb, 1), F32))
            carry = key_block(qh, qi, (zero,) * SB_HEADS, causal)
            carry = lax.fori_loop(0, qi, lambda it, cr: key_block(qh, qi - 1 - it, cr, None), carry)
            o_ref[pl.ds(qs, tb), :] = _merge_tiles([cr[0] for cr in carry], own)
            c_ref[pl.ds(qs, tb), :] = _merge_tiles([jnp.broadcast_to(cr[1], (tb, LANES)) for cr in carry], own)
            return 0

        lax.fori_loop(0, n_blk, query_block, 0)

    def spec(offset):
        return pl.BlockSpec((seq, SB_WIDTH), lambda b, p: (b, offset + p))

    out = jax.ShapeDtypeStruct((t, n_pairs * SB_WIDTH), F32)
    return _call(
        body, name="sb_fwd", grid=(nb, n_pairs), args=(qkv, qkv, qkv),
        in_specs=[spec(0), spec(n_pairs), spec(2 * n_pairs)],
        out_specs=[spec(0), spec(0)], out_shape=[out, out],
        params=_params("arbitrary", "arbitrary"), exchange=exchange)


def _sb_bwd(qkv, do, csum, nb, seq, exchange=None):
    t = qkv.shape[0]
    n_pairs = (qkv.shape[1] // 3) // SB_WIDTH
    tb = SB_BLOCK
    n_blk = seq // tb
    scale = HEAD_DIM ** -0.5

    def body(q_ref, k_ref, v_ref, do_ref, c_ref, dq_ref, dk_ref, dv_ref, dkt_acc, dvt_acc):
        tri, tri_prefix, causal = _sb_masks()
        own = _own_lanes()
        dkt_acc[...] = jnp.zeros_like(dkt_acc)
        dvt_acc[...] = jnp.zeros_like(dvt_acc)

        def key_block(qh, qth, doh, doth, ch, kj, carry, mask):
            ks = pl.multiple_of(kj * tb, tb)
            kh, vh = _pair_tiles(k_ref[pl.ds(ks, tb), :]), _pair_tiles(v_ref[pl.ds(ks, tb), :])
            heads = range(SB_HEADS)
            zs, es, suffixes = _sb_logits(qh, kh, tri, mask)
            dws = [_dot_nt(doh[h], vh[h]) for h in heads]
            lefts = [carry[h][1] + suffixes[h][:, 0:1] for h in heads]
            ws = [jnp.exp(zs[h] + suffixes[h] + (ch[h] - lefts[h])) for h in heads]
            if mask is not None:
                ws = [jnp.where(mask, w, 0.0) for w in ws]
            dlws = [ws[h] * dws[h] for h in heads]
            dprefixes = [_split_dot(dlw, tri_prefix) for dlw in dlws]
            dvts = [_dot(doth[h], ws[h].astype(BF16)) for h in heads]
            dzbs = []
            for h in heads:
                sig = jnp.where(zs[h] >= 0.0, 1.0, es[h]) * pl.reciprocal(1.0 + es[h], approx=True)
                dz = dlws[h] - sig * (carry[h][2] + dprefixes[h])
                if mask is not None:
                    dz = jnp.where(mask, dz, 0.0)
                dzbs.append(dz.astype(BF16))
            dkts = [_dot(qth[h], dzbs[h]) for h in heads]
            dqs = [_dot(dzbs[h], kh[h]) for h in heads]
            dkt_acc[:, pl.ds(ks, tb)] += jnp.concatenate([dkts[h] + dkts[h + 1] for h in heads[::2]], axis=0)
            dvt_acc[:, pl.ds(ks, tb)] += jnp.concatenate([dvts[h] + dvts[h + 1] for h in heads[::2]], axis=0)
            return tuple((carry[h][0] + dqs[h], lefts[h], carry[h][2] + dprefixes[h][:, tb - 1:tb])
                         for h in heads)

        def query_block(qi, _):
            qs = pl.multiple_of(qi * tb, tb)
            qh = _own_tiles(_scaled(q_ref[pl.ds(qs, tb), :]), own)
            doh = _own_tiles(do_ref[pl.ds(qs, tb), :], own)
            qth = [a.astype(F32).T.astype(BF16) for a in qh]
            doth = [a.T.astype(BF16) for a in doh]
            doh = [a.astype(BF16) for a in doh]
            cv = c_ref[pl.ds(qs, tb), :]
            ch = [cv[:, h * HEAD_DIM:h * HEAD_DIM + 1] for h in range(SB_HEADS)]
            zero = (jnp.zeros((tb, LANES), F32), jnp.zeros((tb, 1), F32), jnp.zeros((tb, 1), F32))
            carry = lax.fori_loop(
                0, qi, lambda kj, cr: key_block(qh, qth, doh, doth, ch, kj, cr, None), (zero,) * SB_HEADS)
            carry = key_block(qh, qth, doh, doth, ch, qi, carry, causal)
            dq = _merge_tiles([cr[0] for cr in carry], own) * scale
            dq_ref[pl.ds(qs, tb), :] = dq.astype(BF16)
            return 0

        lax.fori_loop(0, n_blk, query_block, 0)
        dk_ref[...] = dkt_acc[...].T.astype(BF16)
        dv_ref[...] = dvt_acc[...].T.astype(BF16)

    def spec(offset):
        return pl.BlockSpec((seq, SB_WIDTH), lambda b, p: (b, offset + p))

    out = jax.ShapeDtypeStruct((t, n_pairs * SB_WIDTH), BF16)
    return _call(
        body, name="sb_bwd", grid=(nb, n_pairs), args=(qkv, qkv, qkv, do, csum),
        in_specs=[spec(0), spec(n_pairs), spec(2 * n_pairs), spec(0), spec(0)],
        out_specs=[spec(0), spec(0), spec(0)],
        out_shape=[out, out, out],
        scratch_shapes=[pltpu.VMEM((SB_WIDTH, seq), F32), pltpu.VMEM((SB_WIDTH, seq), F32)],
        params=_params("arbitrary", "arbitrary"), exchange=exchange)


def _dil_block_scores(qh, kph, kch, bias_ref, has_prev, band_prev, band_cur):
    scale = HEAD_DIM ** -0.5
    heads = range(len(qh))
    no_prev = jnp.where(has_prev, 0.0, NEG_INF)
    zps = [_dot_nt(qh[h], kph[h]) for h in heads]
    zcs = [_dot_nt(qh[h], kch[h]) for h in heads]
    zps = [jnp.where(band_prev, zps[h] * scale + bias_ref[h, :, 0:DIL_BLOCK], NEG_INF) + no_prev for h in heads]
    zcs = [jnp.where(band_cur, zcs[h] * scale + bias_ref[h, :, DIL_BLOCK:2 * DIL_BLOCK], NEG_INF) for h in heads]
    return zps, zcs


def _dil_bands():
    rows = lax.broadcasted_iota(jnp.int32, (DIL_BLOCK, DIL_BLOCK), 0)
    cols = lax.broadcasted_iota(jnp.int32, (DIL_BLOCK, DIL_BLOCK), 1)
    return cols >= rows, cols <= rows


def _dil_fwd(qkv, bias, nb, seq, dil, exchange=None):
    t, width = qkv.shape
    n_pairs = (width // 3) // DIL_WIDTH
    bq = DIL_BLOCK
    n_blk = seq // bq
    per_seq = n_blk // dil
    heads = range(DIL_HEADS)

    def body(q_ref, k_ref, v_ref, bias_ref, o_ref, lse_ref):
        band_prev, band_cur = _dil_bands()
        own = _own_lanes()

        def block(n, _):
            has_prev = (n & (per_seq - 1)) != 0
            qs = pl.multiple_of(n * bq, bq)
            ps = pl.multiple_of(jnp.maximum(n - 1, 0) * bq, bq)
            qh = _own_tiles(q_ref[pl.ds(qs, bq), :], own)
            kp, kc = _pair_tiles(k_ref[pl.ds(ps, bq), :]), _pair_tiles(k_ref[pl.ds(qs, bq), :])
            vp, vc = _pair_tiles(v_ref[pl.ds(ps, bq), :]), _pair_tiles(v_ref[pl.ds(qs, bq), :])
            zps, zcs = _dil_block_scores(qh, kp, kc, bias_ref, has_prev, band_prev, band_cur)
            ms = [jnp.maximum(jnp.max(zps[h], axis=1, keepdims=True), jnp.max(zcs[h], axis=1, keepdims=True))
                  for h in heads]
            eps = [jnp.exp(zps[h] - ms[h]) for h in heads]
            ecs = [jnp.exp(zcs[h] - ms[h]) for h in heads]
            pvs = [_dot(eps[h].astype(BF16), vp[h]) + _dot(ecs[h].astype(BF16), vc[h]) for h in heads]
            dens = [jnp.sum(eps[h], axis=1, keepdims=True) + jnp.sum(ecs[h], axis=1, keepdims=True) for h in heads]
            o_ref[pl.ds(qs, bq), :] = _merge_tiles([pvs[h] / dens[h] for h in heads], own)
            lse_ref[pl.ds(qs, bq), :] = _merge_tiles(
                [jnp.broadcast_to(ms[h] + jnp.log(dens[h]), (bq, LANES)) for h in heads], own)
            return 0

        lax.fori_loop(0, n_blk, block, 0)

    def spec(offset):
        return pl.BlockSpec((seq, DIL_WIDTH), lambda b, p: (b, offset + p))

    out = jax.ShapeDtypeStruct((t, n_pairs * DIL_WIDTH), F32)
    return _call(
        body, name=f"dil_fwd{dil}", grid=(nb, n_pairs), args=(qkv, qkv, qkv, bias),
        in_specs=[spec(0), spec(n_pairs), spec(2 * n_pairs),
                  pl.BlockSpec((DIL_HEADS, bq, 2 * bq), lambda b, p: (p, 0, 0))],
        out_specs=[spec(0), spec(0)], out_shape=[out, out],
        params=_params("arbitrary", "arbitrary"), exchange=exchange)


def _dil_bwd(qkv, bias, do, lse, delta, nb, seq, dil):
    t, width = qkv.shape
    n_pairs = (width // 3) // DIL_WIDTH
    bq = DIL_BLOCK
    n_blk = seq // bq
    per_seq = n_blk // dil
    scale = HEAD_DIM ** -0.5
    heads = range(DIL_HEADS)

    def body(q_ref, k_ref, v_ref, bias_ref, do_ref, lse_ref, dl_ref, dq_ref, dk_ref, dv_ref, db_ref,
             dk_acc, dv_acc):
        band_prev, band_cur = _dil_bands()
        own = _own_lanes()
        dk_acc[...] = jnp.zeros_like(dk_acc)
        dv_acc[...] = jnp.zeros_like(dv_acc)

        @pl.when(pl.program_id(1) == 0)
        def _():
            db_ref[...] = jnp.zeros_like(db_ref)

        def block(n, _):
            has_prev = (n & (per_seq - 1)) != 0
            qs = pl.multiple_of(n * bq, bq)
            ps = pl.multiple_of(jnp.maximum(n - 1, 0) * bq, bq)
            qh = _own_tiles(q_ref[pl.ds(qs, bq), :], own)
            kp, kc = _pair_tiles(k_ref[pl.ds(ps, bq), :]), _pair_tiles(k_ref[pl.ds(qs, bq), :])
            vp, vc = _pair_tiles(v_ref[pl.ds(ps, bq), :]), _pair_tiles(v_ref[pl.ds(qs, bq), :])
            doh = _own_tiles(do_ref[pl.ds(qs, bq), :].astype(BF16), own)
            lse_v, dl_v = lse_ref[pl.ds(qs, bq), :], dl_ref[pl.ds(qs, bq), :]
            zps, zcs = _dil_block_scores(qh, kp, kc, bias_ref, has_prev, band_prev, band_cur)
            dpp = [_dot_nt(doh[h], vp[h]) for h in heads]
            dpc = [_dot_nt(doh[h], vc[h]) for h in heads]
            lse_h = [lse_v[:, h * HEAD_DIM:h * HEAD_DIM + 1] for h in heads]
            dl_h = [dl_v[:, h * HEAD_DIM:h * HEAD_DIM + 1] for h in heads]
            pps = [jnp.exp(zps[h] - lse_h[h]) for h in heads]
            pcs = [jnp.exp(zcs[h] - lse_h[h]) for h in heads]
            dvp = [_dot_tn(pps[h].astype(BF16), doh[h]) for h in heads]
            dvc = [_dot_tn(pcs[h].astype(BF16), doh[h]) for h in heads]
            dzps = [pps[h] * (dpp[h] - dl_h[h]) for h in heads]
            dzcs = [pcs[h] * (dpc[h] - dl_h[h]) for h in heads]
            dzp_b = [(dzps[h] * scale).astype(BF16) for h in heads]
            dzc_b = [(dzcs[h] * scale).astype(BF16) for h in heads]
            dqs = [_dot(dzp_b[h], kp[h]) + _dot(dzc_b[h], kc[h]) for h in heads]
            dkp = [_dot_tn(dzp_b[h], qh[h]) for h in heads]
            dkc = [_dot_tn(dzc_b[h], qh[h]) for h in heads]
            for h in heads:
                db_ref[h, :, 0:bq] += dzps[h]
                db_ref[h, :, bq:2 * bq] += dzcs[h]
            def pair_sums(per_head):
                return jnp.concatenate([per_head[h] + per_head[h + 1] for h in heads[::2]], axis=1)

            dq_ref[pl.ds(qs, bq), :] = _merge_tiles(dqs, own).astype(BF16)
            dk_acc[pl.ds(ps, bq), :] += pair_sums(dkp)
            dk_acc[pl.ds(qs, bq), :] += pair_sums(dkc)
            dv_acc[pl.ds(ps, bq), :] += pair_sums(dvp)
            dv_acc[pl.ds(qs, bq), :] += pair_sums(dvc)
            return 0

        lax.fori_loop(0, n_blk, block, 0)
        dk_ref[...] = dk_acc[...].astype(BF16)
        dv_ref[...] = dv_acc[...].astype(BF16)

    def spec(offset):
        return pl.BlockSpec((seq, DIL_WIDTH), lambda p, b: (b, offset + p))

    bias_spec = pl.BlockSpec((DIL_HEADS, bq, 2 * bq), lambda p, b: (p, 0, 0))
    out = jax.ShapeDtypeStruct((t, n_pairs * DIL_WIDTH), BF16)
    return pl.pallas_call(
        body, name=f"dil_bwd{dil}", grid=(n_pairs, nb),
        in_specs=[spec(0), spec(n_pairs), spec(2 * n_pairs), bias_spec, spec(0), spec(0), spec(0)],
        out_specs=[spec(0), spec(0), spec(0), bias_spec],
        out_shape=[out, out, out, jax.ShapeDtypeStruct(bias.shape, F32)],
        scratch_shapes=[pltpu.VMEM((seq, DIL_WIDTH), F32), pltpu.VMEM((seq, DIL_WIDTH), F32)],
        compiler_params=_params("arbitrary", "arbitrary"),
    )(qkv, qkv, qkv, bias, do, lse, delta)


def _head_blocks(width):
    rows = lax.broadcasted_iota(jnp.int32, (width, width), 0) // HEAD_DIM
    cols = lax.broadcasted_iota(jnp.int32, (width, width), 1) // HEAD_DIM
    return (rows == cols).astype(BF16)


def _head_mean(v, gmat):
    return _split_dot(v, gmat) * (1.0 / HEAD_DIM)


def _residue_views(arrays, nb, seq):
    return [a if dil == 1 else a.reshape(nb, dil, seq // dil, a.shape[1]) for a, dil in zip(arrays, DILATIONS)]


def _mix_out_fwd(osb, ocs, lses, gsb, gdil, wout, x, mod, tm):
    t, d = x.shape
    ds = osb.shape[1]
    nt = t // tm
    nb = mod.shape[0]
    tpb = nt // nb
    seq = t // nb
    n_cfg = len(DILATIONS)

    def body(osb_ref, *refs):
        oc_refs, lse_refs = refs[:n_cfg], refs[n_cfg:2 * n_cfg]
        gsb_ref, gdil_ref, w_ref, x_ref, mod_ref = refs[2 * n_cfg:2 * n_cfg + 5]
        xo_ref, on_ref, m_ref, odil_ref = refs[2 * n_cfg + 5:2 * n_cfg + 9]
        ld_refs = refs[2 * n_cfg + 9:3 * n_cfg + 9]
        stages, sc = refs[3 * n_cfg + 9:]
        ocv, lsev = [oc_refs[0][...]], [lse_refs[0][...]]
        for i, dil in enumerate(DILATIONS[1:]):
            ocv.append(_from_residue_rows(oc_refs[i + 1], stages.at[2 * i], dil))
            lsev.append(_from_residue_rows(lse_refs[i + 1], stages.at[2 * i + 1], dil))
        top = functools.reduce(jnp.maximum, lsev)
        total = top + jnp.log(sum(jnp.exp(l - top) for l in lsev))
        odil = sum(jnp.exp(l - total) * o for o, l in zip(ocv, lsev))
        odil_ref[...] = odil
        ld_refs[0][...] = total
        _stage(total, sc)
        for ref, dil in zip(ld_refs[1:], DILATIONS[1:]):
            _to_residue_rows(sc, ref, dil)
        gm = _head_blocks(ds)
        parts = []
        for o, g_ref in ((osb_ref[...], gsb_ref), (odil, gdil_ref)):
            parts.append(o * lax.rsqrt(_head_mean(o * o, gm) + EPS) * g_ref[...])
        on = jnp.concatenate(parts, axis=1).astype(BF16)
        on_ref[...] = on
        m = _dot(on, w_ref[...])
        m_ref[...] = m
        xo_ref[...] = x_ref[...] + mod_ref[5:6, :] * m

    tok = pl.BlockSpec((tm, d), lambda i: (i, 0))
    hd = pl.BlockSpec((tm, ds), lambda i: (i, 0))
    res = [hd] + [_residue_spec(tm, tpb, ds, dil, lambda i: 0) for dil in DILATIONS[1:]]
    res_shape = [jax.ShapeDtypeStruct((t, ds), F32)] + [_residue_shape(nb, seq, ds, dil, F32) for dil in DILATIONS[1:]]
    gain = pl.BlockSpec((1, ds), lambda i: (0, 0))
    outs = pl.pallas_call(
        body, name="mix_out_fwd", grid=(nt,),
        in_specs=[hd] + res + res + [gain, gain,
                  pl.BlockSpec(wout.shape, lambda i: (0, 0)),
                  tok, pl.BlockSpec((None, N_MOD, d), lambda i: (i // tpb, 0, 0))],
        out_specs=[tok, pl.BlockSpec((tm, 2 * ds), lambda i: (i, 0)), tok, hd] + res,
        out_shape=[jax.ShapeDtypeStruct((t, d), F32), jax.ShapeDtypeStruct((t, 2 * ds), BF16),
                   jax.ShapeDtypeStruct((t, d), F32), jax.ShapeDtypeStruct((t, ds), F32)] + res_shape,
        scratch_shapes=[pltpu.VMEM((2 * (n_cfg - 1), ds // LANES, tm, LANES), F32), _stage_shape(tm, ds)],
        compiler_params=_params("arbitrary"),
    )(osb, *_residue_views(ocs, nb, seq), *_residue_views(lses, nb, seq), gsb, gdil, wout, x, mod)
    return outs[0], outs[1], outs[2], outs[3], [a.reshape(t, ds) for a in outs[4:]]


def _mix_out_bwd(dxo, m, mod, wout, osb, odil, gsb, gdil, tm):
    t, d = dxo.shape
    ds = osb.shape[1]
    nt = t // tm
    nb = mod.shape[0]
    tpb = nt // nb
    seq = t // nb
    n_cfg = len(DILATIONS)

    def body(dxo_ref, m_ref, mod_ref, w_ref, osb_ref, odil_ref, gsb_ref, gdil_ref,
             dm_ref, dosb_ref, *rest):
        do_refs, dl_refs = rest[:n_cfg], rest[n_cfg:2 * n_cfg]
        dmod_ref, dg_ref, sc = rest[2 * n_cfg:]
        dodil_ref, dldil_ref = do_refs[0], dl_refs[0]
        i = pl.program_id(0)
        dxo_v = dxo_ref[...]
        dm = (mod_ref[5:6, :] * dxo_v).astype(BF16)
        dm_ref[...] = dm
        dgt = jnp.sum(m_ref[...] * dxo_v, axis=0, keepdims=True)
        don = _dot_nt(dm, w_ref[...])
        gm = _head_blocks(ds)

        @pl.when(i % tpb == 0)
        def _():
            dmod_ref[...] = jnp.zeros_like(dmod_ref)

        @pl.when(i == 0)
        def _():
            dg_ref[...] = jnp.zeros_like(dg_ref)

        dmod_ref[2:3, :] += dgt
        groups = ((osb_ref, gsb_ref, dosb_ref), (odil_ref, gdil_ref, dodil_ref))
        for k, (o_ref, g_ref, do_ref) in enumerate(groups):
            o = o_ref[...]
            dn_out = don[:, k * ds:(k + 1) * ds]
            r = lax.rsqrt(_head_mean(o * o, gm) + EPS)
            n = o * r
            dg_ref[0:1, k * ds:(k + 1) * ds] += jnp.sum(dn_out * n, axis=0, keepdims=True)
            dn = dn_out * g_ref[...]
            do = r * (dn - n * _head_mean(dn * n, gm))
            do_ref[...] = do
            if k == 1:
                delta = _head_mean(do * o, gm) * float(HEAD_DIM)
                dldil_ref[...] = delta
                for value, refs in ((do, do_refs), (delta, dl_refs)):
                    _stage(value, sc)
                    for ref, dil in zip(refs[1:], DILATIONS[1:]):
                        _to_residue_rows(sc, ref, dil)

    tok = pl.BlockSpec((tm, d), lambda i: (i, 0))
    hd = pl.BlockSpec((tm, ds), lambda i: (i, 0))
    res = [hd] + [_residue_spec(tm, tpb, ds, dil, lambda i: 0) for dil in DILATIONS[1:]]
    res_shape = [jax.ShapeDtypeStruct((t, ds), F32)] + [_residue_shape(nb, seq, ds, dil, F32) for dil in DILATIONS[1:]]
    gain = pl.BlockSpec((1, ds), lambda i: (0, 0))
    outs = pl.pallas_call(
        body, name="mix_out_bwd", grid=(nt,),
        in_specs=[tok, tok, pl.BlockSpec((None, N_MOD, d), lambda i: (i // tpb, 0, 0)),
                  pl.BlockSpec(wout.shape, lambda i: (0, 0)), hd, hd, gain, gain],
        out_specs=[tok, hd] + res + res
        + [pl.BlockSpec((None, 8, d), lambda i: (i // tpb, 0, 0)), pl.BlockSpec((8, 2 * ds), lambda i: (0, 0))],
        out_shape=[jax.ShapeDtypeStruct((t, d), BF16), jax.ShapeDtypeStruct((t, ds), F32)] + res_shape + res_shape
        + [jax.ShapeDtypeStruct((nb, 8, d), F32), jax.ShapeDtypeStruct((8, 2 * ds), F32)],
        scratch_shapes=[_stage_shape(tm, ds)],
        compiler_params=_params("arbitrary"),
    )(dxo, m, mod, wout, osb, odil, gsb, gdil)
    flat = [a.reshape(t, ds) for a in outs[2:2 + 2 * n_cfg]]
    return outs[0], outs[1], flat[:n_cfg], flat[n_cfg:], outs[-2], outs[-1]


def _merge_dqkv(sb_parts, dil_parts, nb, tm):
    t, ds = sb_parts[0].shape
    nt = t // tm
    tpb = nt // nb
    seq = t // nb
    n_cfg = len(DILATIONS)

    def body(*refs):
        sb_refs, dil_refs = refs[:3], refs[3:3 + 3 * n_cfg]
        o_ref, sc = refs[3 + 3 * n_cfg:]
        for k in range(3):
            o_ref[:, k * ds:(k + 1) * ds] = sb_refs[k][...]
            total = dil_refs[k * n_cfg][...].astype(F32)
            for i, dil in enumerate(DILATIONS[1:]):
                total = total + _from_residue_rows(dil_refs[k * n_cfg + i + 1], sc, dil)
            o_ref[:, (3 + k) * ds:(4 + k) * ds] = total.astype(BF16)

    hd = pl.BlockSpec((tm, ds), lambda i: (i, 0))
    res = [hd] + [_residue_spec(tm, tpb, ds, dil, lambda i: 0) for dil in DILATIONS[1:]]
    views = [v for parts in dil_parts for v in _residue_views(parts, nb, seq)]
    return pl.pallas_call(
        body, name="merge_dqkv", grid=(nt,),
        in_specs=[hd] * 3 + res * 3,
        out_specs=pl.BlockSpec((tm, 6 * ds), lambda i: (i, 0)),
        out_shape=jax.ShapeDtypeStruct((t, 6 * ds), BF16),
        scratch_shapes=[_stage_shape(tm, ds)],
        compiler_params=_params("arbitrary"),
    )(*sb_parts, *views)


def _loss_head(x, target, g, tm):
    t, d = x.shape

    def body(x_ref, t_ref, g_ref, dx_ref, acc_ref):
        @pl.when(pl.program_id(0) == 0)
        def _():
            acc_ref[...] = jnp.zeros_like(acc_ref)

        n, r = _norm(x_ref[...])
        gv = g_ref[...]
        err = n * gv - t_ref[...]
        dy = err * (1.0 / d)
        acc_ref[0:1, :] += jnp.sum(err * err, axis=0, keepdims=True)
        acc_ref[1:2, :] += jnp.sum(dy * n, axis=0, keepdims=True)
        dn = dy * gv
        dx_ref[...] = r * (dn - n * jnp.mean(dn * n, axis=-1, keepdims=True))

    tok = pl.BlockSpec((tm, d), lambda i: (i, 0))
    return pl.pallas_call(
        body, name="loss_head", grid=(t // tm,),
        in_specs=[tok, tok, pl.BlockSpec((1, d), lambda i: (0, 0))],
        out_specs=[tok, pl.BlockSpec((8, d), lambda i: (0, 0))],
        out_shape=[jax.ShapeDtypeStruct((t, d), F32), jax.ShapeDtypeStruct((8, d), F32)],
        compiler_params=_params("arbitrary"),
    )(x, target, g)


def _row_tile(rows):
    if rows <= 256:
        return rows
    for cand in range(256, 15, -16):
        if rows % cand == 0:
            return cand
    return rows


def _adamw(w, parts, m, v, name, transposed=False):
    rows, cols = w.shape
    n_parts = parts.shape[0]
    tr = _row_tile(rows)
    c1 = 1.0 / (1.0 - ADAM_B1 ** ADAM_STEP)
    c2 = 1.0 / (1.0 - ADAM_B2 ** ADAM_STEP)

    def body(w_ref, p_ref, m_ref, v_ref, g_ref, d_ref, nm_ref, nv_ref):
        g = p_ref[0].astype(F32)
        for i in range(1, n_parts):
            g = g + p_ref[i].astype(F32)
        wv, mv, vv = w_ref[...], m_ref[...], v_ref[...]
        if transposed:
            wv, mv, vv = wv.T, mv.T, vv.T
        nm = ADAM_B1 * mv + (1.0 - ADAM_B1) * g
        nv = ADAM_B2 * vv + (1.0 - ADAM_B2) * (g * g)
        g_ref[...] = g
        nm_ref[...] = nm
        nv_ref[...] = nv
        d_ref[...] = -ADAM_LR * ((nm * c1) / (jnp.sqrt(nv * c2) + ADAM_EPS) + ADAM_WD * wv)

    blk = pl.BlockSpec((tr, cols), lambda i: (i, 0))
    if transposed:
        oblk = pl.BlockSpec((cols, tr), lambda i: (0, i))
        pblk = pl.BlockSpec((n_parts, cols, tr), lambda i: (0, 0, i))
        out = jax.ShapeDtypeStruct((cols, rows), F32)
    else:
        oblk, pblk = blk, pl.BlockSpec((n_parts, tr, cols), lambda i: (0, i, 0))
        out = jax.ShapeDtypeStruct((rows, cols), F32)
    return pl.pallas_call(
        body, name=name, grid=(rows // tr,),
        in_specs=[blk, pblk, blk, blk],
        out_specs=[oblk, oblk, oblk, oblk], out_shape=[out, out, out, out],
        compiler_params=_params("arbitrary"),
    )(w, parts, m, v)


def _t5_bucket(n):
    max_exact = N_BUCKETS // 2
    nf = np.maximum(n, 1).astype(np.float32)
    large = max_exact + (np.log(nf / max_exact) / math.log(MAX_DISTANCE / max_exact)
                         * (N_BUCKETS - max_exact)).astype(np.int32)
    large = np.minimum(large, N_BUCKETS - 1)
    return np.where(n < max_exact, n, large).astype(np.int32)


def _bucket_onehot():
    table = np.zeros((len(DILATIONS), 2 * DIL_BLOCK + 1, N_BUCKETS), np.float32)
    for i, dil in enumerate(DILATIONS):
        buckets = _t5_bucket(np.arange(DIL_BLOCK + 1) * dil)
        for m in range(DIL_BLOCK + 1):
            table[i, m, buckets[DIL_BLOCK - m]] = 1.0
    return table


def _bias_blocks(rel_bias):
    row = jnp.einsum("cmn,nh->chm", _bucket_onehot(), rel_bias, precision=lax.Precision.HIGHEST)
    n_cfg, n_heads, width = row.shape
    tiled = jnp.tile(row, (1, 1, DIL_BLOCK))[..., :DIL_BLOCK * (width - 1)]
    return tiled.reshape(n_cfg, n_heads, DIL_BLOCK, width - 1)


def _bias_blocks_bwd(dblocks):
    n_cfg, n_heads = dblocks.shape[:2]
    width = 2 * DIL_BLOCK + 1
    flat = dblocks.reshape(n_cfg, n_heads, DIL_BLOCK * (width - 1))
    flat = jnp.pad(flat, ((0, 0), (0, 0), (0, DIL_BLOCK)))
    drow = jnp.sum(flat.reshape(n_cfg, n_heads, DIL_BLOCK, width), axis=2)
    return jnp.einsum("chm,cmn->nh", drow, _bucket_onehot(), precision=lax.Precision.HIGHEST)


def _pad_to(a, axis, size):
    pad = [(0, 0)] * a.ndim
    pad[axis] = (0, size - a.shape[axis])
    return jnp.pad(a, pad)


def _lane_pad(n):
    return -(-n // LANES) * LANES


def _local_step(x, target, mod, gains, weights, rel_bias, tm, me=None):
    nb, seq, d = x.shape
    t = nb * seq
    distributed = me is not None
    g_ffn1, g_mix, g_sb, g_dil, g_ffn2, g_final = gains
    x0 = x.reshape(t, d)
    ds = g_sb.shape[1]
    bias = _bias_blocks(rel_bias)

    def beside(arrays, scatter):
        return _Exchange(arrays, scatter) if distributed else None

    tp, tg = min(PROJ_TILE, seq), min(GRAD_TILE, t)

    if distributed:
        (x1, f1, gate1, up1), (wg1, wu1, wd1), (win, wout) = _ffn_fwd_gathering(
            x0, mod, g_ffn1, weights[:3], me, tm, weights[3:5])
    else:
        wg1, wu1, wd1, win, wout = weights[:5]
        (x1, f1, gate1, up1), _ = _ffn_fwd(x0, mod, g_ffn1, wg1, wu1, wd1, 0, tp)
    wout2 = wout.reshape(-1, d)
    qkv, qkvd, h2 = _qkv_fwd(x1, mod, g_mix, win, tp)
    (osb, csb), got = _sb_fwd(qkv, nb, seq, beside(weights[5:7], False))
    wg2, wu2 = got if distributed else weights[5:7]
    ocs, lses = [], []
    for i, dil in enumerate(DILATIONS):
        (oc, lse), got = _dil_fwd(qkvd[i], bias[i], nb, seq, dil, beside(weights[7:8], False) if i == 0 else None)
        if i == 0:
            wd2 = got[0] if distributed else weights[7]
        ocs.append(oc)
        lses.append(lse)
    x2, on, mix, odil, ldil = _mix_out_fwd(osb, ocs, lses, g_sb, g_dil, wout2, x1, mod, tm)
    (x3, f3, gate3, up3), _ = _ffn_fwd(x2, mod, g_ffn2, wg2, wu2, wd2, 2, tp)
    dx3, head = _loss_head(x3, target.reshape(t, d), g_final, tm)
    loss_sum = 0.5 * jnp.sum(head[0]) / d
    dg_final = head[1:2]

    (dx2, dgate3, dup3, act3, h3, df3, dmod3, dg_ffn2), _ = _ffn_bwd(
        dx3, x2, f3, mod, g_ffn2, gate3, up3, wg2, wu2, wd2, 2, tp)
    gwg2, gwu2, gwd2 = _ffn_weight_grads(h3, dgate3, dup3, act3, df3, tg, 2)

    dm, dosb, dodil, dldil, dmod2b, dg_heads = _mix_out_bwd(
        dx2, mix, mod, wout2, osb, odil, g_sb, g_dil, tm)
    n_out = wout.shape[0]
    gwout = _mm_tn(on, dm,
                   pl.BlockSpec((tg, wout.shape[1]), lambda i, j: (i, j)),
                   pl.BlockSpec((tg, d), lambda i, j: (i, 0)),
                   wout.shape, t // tg, "grad_wout")

    (dq_sb, dk_sb, dv_sb), parts_late = _sb_bwd(qkv, dosb, csb, nb, seq,
                                                beside([gwout, gwg2, gwu2, gwd2], True))
    dil_grads = [_dil_bwd(qkvd[i], bias[i], dodil[i], ldil[i], dldil[i], nb, seq, dil)
                 for i, dil in enumerate(DILATIONS)]
    dqkv = _merge_dqkv([dq_sb, dk_sb, dv_sb], [[g[k] for g in dil_grads] for k in range(3)], nb, tm)
    drel = _bias_blocks_bwd(jnp.stack([g[3] for g in dil_grads]))

    cs = win.shape[2]
    gwin = _mm_tn(h2, dqkv,
                  pl.BlockSpec((tg, d), lambda i, j: (i, 0)),
                  pl.BlockSpec((tg, cs), lambda i, j: (i, j)),
                  win.shape, t // tg, "grad_win")
    (dx1, dmod2a, dg_mix), parts_mid = _qkv_bwd(dqkv, dx2, x1, mod, g_mix, win, tp, beside([gwin], True))

    (dx0, dgate1, dup1, act1, h1, df1, dmod1, dg_ffn1), _ = _ffn_bwd(
        dx1, x0, f1, mod, g_ffn1, gate1, up1, wg1, wu1, wd1, 0, tp)
    gw1 = _ffn_weight_grads(h1, dgate1, dup1, act1, df1, tg, 0, stream=distributed)

    dmod = jnp.concatenate([dmod1[:, 0:3], dmod2a[:, 0:2], dmod2b[:, 2:3], dmod3[:, 0:3]], axis=1)
    wgrads = tuple(gw1) + (tuple(parts_mid + parts_late) if distributed else (gwin, gwout, gwg2, gwu2, gwd2))
    ggrads = (dg_ffn1[0:1], dg_mix[0:1], dg_heads[0:1], drel, dg_ffn2[0:1], dg_final)
    return loss_sum, dx0.reshape(nb, seq, d), wgrads, dmod, ggrads


def kernel(x, c, w_ada, b_ada, g_ffn1, w1_gate, w1_up, w1_down, g_mix, w_in, g_sb_out, g_dil_out, w_out, rel_bias, g_ffn2, w2_gate, w2_up, w2_down, g_final, loss_target, m_w_ada, m_b_ada, m_g_ffn1, m_w1_gate, m_w1_up, m_w1_down, m_g_mix, m_w_in, m_g_sb_out, m_g_dil_out, m_w_out, m_rel_bias, m_g_ffn2, m_w2_gate, m_w2_up, m_w2_down, m_g_final, v_w_ada, v_b_ada, v_g_ffn1, v_w1_gate, v_w1_up, v_w1_down, v_g_mix, v_w_in, v_g_sb_out, v_g_dil_out, v_w_out, v_rel_bias, v_g_ffn2, v_w2_gate, v_w2_up, v_w2_down, v_g_final):
    nb, seq, d = x.shape
    me = 4 * lax.axis_index("x") + 2 * lax.axis_index("y") + lax.axis_index("c")
    tm = min(TOKEN_TILE, seq)
    fs = w1_gate.shape[2]
    fs_pad = _lane_pad(fs)
    ada_cols = w_ada.shape[2]

    def col_shard(w):
        return _pad_to(w[0].astype(BF16), 1, fs_pad)

    def row_shard(w):
        return _pad_to(w[0].astype(BF16), 0, fs_pad)

    shards = [col_shard(w1_gate), col_shard(w1_up), row_shard(w1_down), w_in[0].astype(BF16),
              w_out[0].astype(BF16), col_shard(w2_gate), col_shard(w2_up), row_shard(w2_down)]
    b_cols = lax.dynamic_slice(b_ada, (0, me * ada_cols), (1, ada_cols))
    c_every, mod_all = _first_exchange(_pad_to(c, 0, 8), w_ada[0], b_cols)
    c_all = c_every[:, :nb].reshape(N_DEV * nb, d)
    mod = lax.dynamic_slice(mod_all, (0, me * 8, 0), (N_DEV, nb, ada_cols))
    mod = mod.transpose(1, 0, 2).reshape(nb, N_MOD, d)

    n_sb = g_sb_out.shape[1] * g_sb_out.shape[2]
    gains = (g_ffn1, g_mix, g_sb_out.reshape(1, n_sb), g_dil_out.reshape(1, -1), g_ffn2,
             g_final.reshape(1, d))
    loss_sum, grad_x, parts, dmod, ggrads = _local_step(
        x, loss_target, mod, gains, shards, rel_bias, tm, jnp.reshape(me, (1,)).astype(jnp.int32))
    loss = lax.psum(loss_sum, ("x", "y", "c"))

    dg_ffn1, dg_mix, dg_heads, drel, dg_ffn2, dg_final = ggrads
    width = max(d, dg_heads.shape[1], drel.size)
    small = jnp.concatenate(
        [_pad_to(a.reshape(1, -1), 1, width) for a in (dg_ffn1, dg_mix, dg_ffn2, dg_final, dg_heads, drel)]
        + [jnp.zeros((2, width), F32)], axis=0)
    dmod_pad = _pad_to(dmod.reshape(nb, N_MOD * d), 0, 8)
    last_part, dmod_all, small_all = _exchange(
        [parts[2], jnp.broadcast_to(dmod_pad, (N_DEV,) + dmod_pad.shape),
         jnp.broadcast_to(small, (N_DEV,) + small.shape)], True, "scatter_last", chips=[True, False, False])
    parts = parts[:2] + (last_part,) + parts[3:]
    dmod_all = dmod_all[:, :nb].reshape(N_DEV * nb, N_MOD * d)
    dmod_cols = lax.dynamic_slice(dmod_all, (0, me * ada_cols), (N_DEV * nb, ada_cols))
    gw_ada, gb_ada = _ada_bwd(c_all, dmod_cols, dmod_all)

    def small_part(row, size, shape):
        return small_all[:, row, :size].reshape((N_DEV,) + shape)

    n_rel = rel_bias.shape
    updates = {
        "w_ada": (w_ada[0], gw_ada[None], m_w_ada[0], v_w_ada[0]),
        "b_ada": (b_ada, gb_ada[None], m_b_ada, v_b_ada),
        "g_ffn1": (g_ffn1, small_part(0, d, (1, d)), m_g_ffn1, v_g_ffn1),
        "w1_gate": (w1_gate[0], parts[0][:, :fs, :], m_w1_gate[0], v_w1_gate[0]),
        "w1_up": (w1_up[0], parts[1][:, :fs, :], m_w1_up[0], v_w1_up[0]),
        "w1_down": (w1_down[0], parts[2][:, :fs, :], m_w1_down[0], v_w1_down[0]),
        "g_mix": (g_mix, small_part(1, d, (1, d)), m_g_mix, v_g_mix),
        "w_in": (w_in[0], parts[3], m_w_in[0], v_w_in[0]),
        "g_sb_out": (g_sb_out[0], small_all[:, 4, :n_sb].reshape((N_DEV,) + g_sb_out.shape[1:]),
                     m_g_sb_out[0], v_g_sb_out[0]),
        "g_dil_out": (g_dil_out[0], small_all[:, 4, n_sb:dg_heads.shape[1]].reshape((N_DEV,) + g_dil_out.shape[1:]),
                      m_g_dil_out[0], v_g_dil_out[0]),
        "w_out": (w_out[0], parts[4], m_w_out[0], v_w_out[0]),
        "rel_bias": (rel_bias, small_part(5, drel.size, n_rel), m_rel_bias, v_rel_bias),
        "g_ffn2": (g_ffn2, small_part(2, d, (1, d)), m_g_ffn2, v_g_ffn2),
        "w2_gate": (w2_gate[0], parts[5][:, :fs, :], m_w2_gate[0], v_w2_gate[0]),
        "w2_up": (w2_up[0], parts[6][:, :fs, :], m_w2_up[0], v_w2_up[0]),
        "w2_down": (w2_down[0], parts[7][:, :fs, :], m_w2_down[0], v_w2_down[0]),
        "g_final": (g_final.reshape(1, d), small_part(3, d, (1, d)), m_g_final.reshape(1, d), v_g_final.reshape(1, d)),
    }
    shapes = {"w_ada": w_ada.shape, "b_ada": b_ada.shape, "g_ffn1": g_ffn1.shape, "w1_gate": w1_gate.shape,
              "w1_up": w1_up.shape, "w1_down": w1_down.shape, "g_mix": g_mix.shape, "w_in": w_in.shape,
              "g_sb_out": g_sb_out.shape, "g_dil_out": g_dil_out.shape, "w_out": w_out.shape,
              "rel_bias": rel_bias.shape, "g_ffn2": g_ffn2.shape, "w2_gate": w2_gate.shape,
              "w2_up": w2_up.shape, "w2_down": w2_down.shape, "g_final": g_final.shape}
    grads, deltas, new_m, new_v = [], [], [], []
    for name, (w, p, m, v) in updates.items():
        transposed = name in ("w1_gate", "w1_up", "w2_gate", "w2_up")
        outs = _adamw(w, p, m, v, f"adamw_{name}", transposed)
        for dst, a in zip((grads, deltas, new_m, new_v), outs):
            dst.append((a.T if transposed else a).reshape(shapes[name]))
    return (loss, grad_x, *grads, *deltas, *new_m, *new_v)
```

```python
import functools
import math

import numpy as np
import jax
import jax.numpy as jnp
from jax import lax
from jax.experimental import pallas as pl
from jax.experimental.pallas import tpu as pltpu

F32 = jnp.float32
BF16 = jnp.bfloat16

EPS = 1e-6
NEG_INF = -1e30
HEAD_DIM = 64
LANES = 128
DIL_BLOCK = 128
DILATIONS = (1, 4, 16)
N_BUCKETS = 32
MAX_DISTANCE = 2048
N_MOD = 9
N_DEV = 8
SB_BLOCK = 256
SB_HEADS = 4
SB_WIDTH = SB_HEADS * HEAD_DIM
DIL_HEADS = 4
DIL_WIDTH = DIL_HEADS * HEAD_DIM
TOKEN_TILE = 512
PROJ_TILE = 1024
GRAD_TILE = 1024
FFN_CHUNKS = 2
VMEM_LIMIT_BYTES = 56 * 1024 * 1024

ADAM_LR = 0.001
ADAM_B1 = 0.9
ADAM_B2 = 0.999
ADAM_EPS = 1e-08
ADAM_WD = 0.01
ADAM_STEP = 10

NT_DIMS = (((1,), (1,)), ((), ()))
TN_DIMS = (((0,), (0,)), ((), ()))


def _params(*sem):
    return pltpu.CompilerParams(dimension_semantics=sem, vmem_limit_bytes=VMEM_LIMIT_BYTES)


def _once(spec):
    return pl.BlockSpec(spec.block_shape, spec.index_map, pipeline_mode=pl.Buffered(1))


def _dot(a, b):
    return jnp.dot(a, b, preferred_element_type=F32)


def _dot_nt(a, b):
    return lax.dot_general(a, b, NT_DIMS, preferred_element_type=F32)


def _dot_tn(a, b):
    return lax.dot_general(a, b, TN_DIMS, preferred_element_type=F32)


def _split_dot(a, b):
    hi = a.astype(BF16)
    lo = (a - hi.astype(F32)).astype(BF16)
    return _dot(hi, b) + _dot(lo, b)


def _sigmoid(z):
    return 1.0 / (1.0 + jnp.exp(-z))


def _norm(x):
    r = lax.rsqrt(jnp.mean(x * x, axis=-1, keepdims=True) + EPS)
    return x * r, r


def _modulate(x, g, mod_ref, k):
    n, _ = _norm(x)
    shift = mod_ref[3 * k:3 * k + 1, :]
    scale = mod_ref[3 * k + 1:3 * k + 2, :]
    return n * g * (1.0 + scale) + shift


def _modulate_bwd(dh, x, g, mod_ref, k):
    n, r = _norm(x)
    scale = mod_ref[3 * k + 1:3 * k + 2, :]
    dshift = jnp.sum(dh, axis=0, keepdims=True)
    dscale = jnp.sum(dh * n * g, axis=0, keepdims=True)
    dg = jnp.sum(dh * n * (1.0 + scale), axis=0, keepdims=True)
    dn = dh * g * (1.0 + scale)
    dx = r * (dn - n * jnp.mean(dn * n, axis=-1, keepdims=True))
    return dx, dshift, dscale, dg


class _Exchange:
    def __init__(self, arrays, scatter, relay=False, chips=None):
        assert not (scatter and relay)
        self.arrays = list(arrays)
        self.scatter = scatter
        self.relay = relay
        self.n = len(self.arrays)
        self.chips = list(chips) if chips is not None else [False] * self.n
        assert scatter or not any(self.chips)
        self.out_shape = [
            jax.ShapeDtypeStruct((N_DEV // 2 if ch else N_DEV,) + tuple(a.shape[1:] if scatter else a.shape), a.dtype)
            for a, ch in zip(self.arrays, self.chips)]
        n_remote = self.n * (N_DEV - 1)
        self.scratch_shapes = [pltpu.SemaphoreType.DMA((n_remote,)), pltpu.SemaphoreType.DMA((n_remote,)),
                               pltpu.SemaphoreType.DMA((self.n,))]

    def _copies(self, in_refs, out_refs, sems):
        send_sems, recv_sems, local_sems = sems
        x, y, c = lax.axis_index("x"), lax.axis_index("y"), lax.axis_index("c")
        me = 4 * x + 2 * y + c
        local, remote, relayed = [], {}, {}
        for a in range(self.n):
            if self.chips[a]:
                mine = 2 * x + y
                local.append(pltpu.make_async_copy(in_refs[a].at[mine], out_refs[a].at[mine], local_sems.at[a]))
                for k in (2, 4, 6):
                    px = 1 - x if k & 4 else x
                    py = 1 - y if k & 2 else y
                    sem = a * (N_DEV - 1) + k - 1
                    remote[a, k] = pltpu.make_async_remote_copy(
                        src_ref=in_refs[a].at[2 * px + py], dst_ref=out_refs[a].at[mine],
                        send_sem=send_sems.at[sem], recv_sem=recv_sems.at[sem],
                        device_id=(px, py, c), device_id_type=pl.DeviceIdType.MESH)
                continue
            src = in_refs[a].at[me] if self.scatter else in_refs[a]
            local.append(pltpu.make_async_copy(src, out_refs[a].at[me], local_sems.at[a]))
            for k in range(1, N_DEV):
                px = 1 - x if k & 4 else x
                py = 1 - y if k & 2 else y
                pc = 1 - c if k & 1 else c
                sem = a * (N_DEV - 1) + k - 1
                if self.relay and k & 1 and k > 1:
                    slot = 4 * px + 2 * py + c
                    relayed[a, k] = pltpu.make_async_remote_copy(
                        src_ref=out_refs[a].at[slot], dst_ref=out_refs[a].at[slot],
                        send_sem=send_sems.at[sem], recv_sem=recv_sems.at[sem],
                        device_id=(x, y, 1 - c), device_id_type=pl.DeviceIdType.MESH)
                    continue
                src = in_refs[a].at[4 * px + 2 * py + pc] if self.scatter else in_refs[a]
                remote[a, k] = pltpu.make_async_remote_copy(
                    src_ref=src, dst_ref=out_refs[a].at[me],
                    send_sem=send_sems.at[sem], recv_sem=recv_sems.at[sem],
                    device_id=(px, py, pc), device_id_type=pl.DeviceIdType.MESH)
        return local, remote, relayed

    def start(self, in_refs, out_refs, sems):
        local, remote, _ = self._copies(in_refs, out_refs, sems)
        for cp in local + list(remote.values()):
            cp.start()

    def wait(self, in_refs, out_refs, sems):
        local, remote, relayed = self._copies(in_refs, out_refs, sems)
        for (a, k), cp in relayed.items():
            remote[a, k - 1].wait_recv()
            cp.start()
        for (a, k), cp in remote.items():
            if (a, k + 1) not in relayed:
                cp.wait_recv()
        for cp in relayed.values():
            cp.wait_recv()
        for cp in list(remote.values()) + list(relayed.values()):
            cp.wait_send()
        for cp in local:
            cp.wait()


def _call(body, *, name, args, in_specs, out_specs, out_shape, scratch_shapes=(), grid=(),
          params=None, exchange=None):
    n_in, n_out = len(args), len(out_shape)
    if exchange is None:
        outs = pl.pallas_call(
            body, name=name, grid=grid, in_specs=list(in_specs), out_specs=list(out_specs),
            out_shape=list(out_shape), scratch_shapes=list(scratch_shapes), compiler_params=params,
        )(*args)
        return list(outs), []
    n_ex = exchange.n

    def wrapped(*refs):
        ins, refs = refs[:n_in], refs[n_in:]
        ex_in, refs = refs[:n_ex], refs[n_ex:]
        outs, refs = refs[:n_out], refs[n_out:]
        ex_out, refs = refs[:n_ex], refs[n_ex:]
        scratch, sems = refs[:len(refs) - 3], refs[len(refs) - 3:]
        if not grid:
            exchange.start(ex_in, ex_out, sems)
            body(*ins, *outs, *scratch)
            exchange.wait(ex_in, ex_out, sems)
            return
        first = functools.reduce(jnp.logical_and, [pl.program_id(a) == 0 for a in range(len(grid))])
        last = functools.reduce(jnp.logical_and, [pl.program_id(a) == grid[a] - 1 for a in range(len(grid))])

        @pl.when(first)
        def _():
            exchange.start(ex_in, ex_out, sems)

        body(*ins, *outs, *scratch)

        @pl.when(last)
        def _():
            exchange.wait(ex_in, ex_out, sems)

    any_spec = pl.BlockSpec(memory_space=pl.ANY)
    outs = pl.pallas_call(
        wrapped, name=name, grid=grid,
        in_specs=list(in_specs) + [any_spec] * n_ex, out_specs=list(out_specs) + [any_spec] * n_ex,
        out_shape=list(out_shape) + exchange.out_shape,
        scratch_shapes=list(scratch_shapes) + exchange.scratch_shapes, compiler_params=params,
    )(*args, *exchange.arrays)
    return list(outs[:n_out]), list(outs[n_out:])


def _exchange(arrays, scatter, name, relay=False, chips=None):
    return _call(lambda: None, name=name, args=(), in_specs=(), out_specs=(), out_shape=(),
                 exchange=_Exchange(arrays, scatter, relay, chips))[1]


def _first_exchange(c_pad, w, b):
    rows, d = c_pad.shape
    cols = w.shape[1]
    ex_c = _Exchange([c_pad], False)
    ex_m = _Exchange([jax.ShapeDtypeStruct((N_DEV * rows, cols), F32)], False)

    def body(c_ref, wa_ref, b_ref, cg_ref, mg_ref, *scratch):
        sems_c, sems_m, c_vm, m_vm = scratch[0:3], scratch[3:6], scratch[6], scratch[7]
        ex_c.start([c_ref], [cg_ref], sems_c)
        ex_c.wait([c_ref], [cg_ref], sems_c)
        pltpu.sync_copy(cg_ref, c_vm)
        cv = c_vm[...].reshape(N_DEV * rows, d)
        s = (cv * _sigmoid(cv)).astype(BF16)
        m_vm[...] = _dot(s, wa_ref[...].astype(BF16)) + b_ref[...]
        ex_m.start([m_vm], [mg_ref], sems_m)
        ex_m.wait([m_vm], [mg_ref], sems_m)

    any_spec = pl.BlockSpec(memory_space=pl.ANY)
    vmem_spec = pl.BlockSpec(memory_space=pltpu.VMEM)
    return pl.pallas_call(
        body, name="first_exchange",
        in_specs=[any_spec, vmem_spec, vmem_spec], out_specs=[any_spec, any_spec],
        out_shape=ex_c.out_shape + ex_m.out_shape,
        scratch_shapes=ex_c.scratch_shapes + ex_m.scratch_shapes
        + [pltpu.VMEM((N_DEV, rows, d), F32), pltpu.VMEM((N_DEV * rows, cols), F32)],
        compiler_params=pltpu.CompilerParams(vmem_limit_bytes=VMEM_LIMIT_BYTES),
    )(c_pad, w, b)


def _ada_bwd(c_all, dmod_cols, dmod_all):
    def body(c_ref, dc_ref, da_ref, gw_ref, gb_ref):
        cv = c_ref[...]
        s = cv * _sigmoid(cv)
        gw_ref[...] = lax.dot_general(s, dc_ref[...], TN_DIMS, preferred_element_type=F32,
                                      precision=lax.Precision.HIGHEST)
        gb_ref[...] = jnp.sum(da_ref[...], axis=0, keepdims=True)

    return pl.pallas_call(
        body, name="ada_bwd",
        out_shape=(jax.ShapeDtypeStruct((c_all.shape[1], dmod_cols.shape[1]), F32),
                   jax.ShapeDtypeStruct((1, dmod_all.shape[1]), F32)),
        compiler_params=pltpu.CompilerParams(vmem_limit_bytes=VMEM_LIMIT_BYTES),
    )(c_all, dmod_cols, dmod_all)


def _ffn_fwd(x, mod, g, wg, wu, wd, k, tm, exchange=None):
    t, d = x.shape
    ns, _, fs = wg.shape
    nt = t // tm
    tpb = nt // mod.shape[0]
    rows = tm // FFN_CHUNKS

    def body(x_ref, mod_ref, g_ref, wg_ref, wu_ref, wd_ref, xo_ref, f_ref, gg_ref, uu_ref, h_sc, acc):
        j = pl.program_id(1)

        @pl.when(j == 0)
        def _():
            h_sc[...] = _modulate(x_ref[...], g_ref[...], mod_ref, k).astype(BF16)
            acc[...] = jnp.zeros_like(acc)

        chunks = [pl.ds(c * rows, rows) for c in range(FFN_CHUNKS)]
        wg, wu, wd = wg_ref[...], wu_ref[...], wd_ref[...]
        gates, ups = [], []
        for rs in chunks:
            h = h_sc[rs, :]
            gates.append(_dot(h, wg))
            ups.append(_dot(h, wu))
        acts = [(g * _sigmoid(g) * u).astype(BF16) for g, u in zip(gates, ups)]
        for rs, g, u in zip(chunks, gates, ups):
            gg_ref[rs, :] = g.astype(BF16)
            uu_ref[rs, :] = u.astype(BF16)
        downs = [_dot(a, wd) for a in acts]
        for rs, dn in zip(chunks, downs):
            acc[rs, :] += dn

        @pl.when(j == ns - 1)
        def _():
            f = acc[...]
            f_ref[...] = f.astype(BF16)
            xo_ref[...] = x_ref[...] + 0.5 * mod_ref[3 * k + 2:3 * k + 3, :] * f

    tok = pl.BlockSpec((tm, d), lambda i, j: (i, 0))
    hid = pl.BlockSpec((None, tm, fs), lambda i, j: (j, i, 0))
    return _call(
        body, name=f"ffn_fwd{k}", grid=(nt, ns), args=(x, mod, g, wg, wu, wd),
        in_specs=[tok,
                  pl.BlockSpec((None, N_MOD, d), lambda i, j: (i // tpb, 0, 0)),
                  pl.BlockSpec((1, d), lambda i, j: (0, 0)),
                  pl.BlockSpec((None, d, fs), lambda i, j: (j, 0, 0)),
                  pl.BlockSpec((None, d, fs), lambda i, j: (j, 0, 0)),
                  pl.BlockSpec((None, fs, d), lambda i, j: (j, 0, 0))],
        out_specs=[tok, tok, hid, hid],
        out_shape=[jax.ShapeDtypeStruct((t, d), F32), jax.ShapeDtypeStruct((t, d), BF16),
                   jax.ShapeDtypeStruct((ns, t, fs), BF16), jax.ShapeDtypeStruct((ns, t, fs), BF16)],
        scratch_shapes=[pltpu.VMEM((tm, d), BF16), pltpu.VMEM((tm, d), F32)],
        params=_params("arbitrary", "arbitrary"), exchange=exchange)


def _ffn_fwd_gathering(x, mod, g, shards, me, tm, later):
    t, d = x.shape
    fs = shards[0].shape[1]
    nt = t // tm
    tpb = nt // mod.shape[0]
    ns = N_DEV
    ex = _Exchange(shards, False, relay=True)
    ex_later = _Exchange(later, False)
    n_later = ex_later.n

    def body(me_ref, x_ref, mod_ref, g_ref, wg_s, wu_s, wd_s, *rest):
        later_in, rest = rest[:n_later], rest[n_later:]
        xo_ref, f_ref, gg_ref, uu_ref, wg_a, wu_a, wd_a = rest[:7]
        later_out, rest = rest[7:7 + n_later], rest[7 + n_later:]
        acc, h_sc, wg_v, wu_v, wd_v, send_sems, recv_sems, local_sems = rest[:8]
        later_sems = rest[8:]
        k, i = pl.program_id(0), pl.program_id(1)
        own, every = [wg_s, wu_s, wd_s], [wg_a, wu_a, wd_a]
        local, remote, relayed = ex._copies(own, every, (send_sems, recv_sems, local_sems))
        arrays = range(ex.n)
        rows = pl.ds(pl.multiple_of(i * tm, tm), tm)

        @pl.when(jnp.logical_and(k == 0, i == 0))
        def _():
            for cp in local:
                cp.start()
            for peer in (1, 2, 4):
                for a in arrays:
                    remote[a, peer].start()

        @pl.when(i == 0)
        def _():
            for peer in range(ns):
                @pl.when(k == peer)
                def _(peer=peer):
                    for a in arrays:
                        if peer == 0:
                            local[a].wait()
                        elif peer == 1:
                            remote[a, peer].wait_recv()
                        elif peer % 2 == 0:
                            remote[a, peer].wait_recv()
                            relayed[a, peer + 1].start()
                        else:
                            relayed[a, peer].wait_recv()
                    if peer == 2:
                        for a in arrays:
                            remote[a, 6].start()
                    if peer == 6:
                        ex_later.start(later_in, later_out, later_sems)

            shard = jnp.bitwise_xor(me_ref[0], k)
            for src, dst in zip(every, (wg_v, wu_v, wd_v)):
                pltpu.sync_copy(src.at[shard], dst)

        @pl.when(k == 0)
        def _():
            h_sc[rows, :] = _modulate(x_ref[...], g_ref[...], mod_ref, 0).astype(BF16)
            acc[rows, :] = jnp.zeros((tm, d), F32)

        h = h_sc[rows, :]
        gate = _dot(h, wg_v[...])
        up = _dot(h, wu_v[...])
        act = gate * _sigmoid(gate) * up
        acc[rows, :] += _dot(act.astype(BF16), wd_v[...])
        gg_ref[...] = gate.astype(BF16)
        uu_ref[...] = up.astype(BF16)

        @pl.when(k == ns - 1)
        def _():
            f = acc[rows, :]
            f_ref[...] = f.astype(BF16)
            xo_ref[...] = x_ref[...] + 0.5 * mod_ref[2:3, :] * f

            @pl.when(i == nt - 1)
            def _():
                for cp in list(remote.values()) + list(relayed.values()):
                    cp.wait_send()
                ex_later.wait(later_in, later_out, later_sems)

    def edge_tile(k, i):
        return jnp.where(jnp.logical_or(k == 0, k == ns - 1), i, 0)

    def last_tile(k, i):
        return jnp.where(k == ns - 1, i, 0)

    any_spec = pl.BlockSpec(memory_space=pl.ANY)
    hid = pl.BlockSpec((None, tm, fs), lambda k, i, me_ref: (jnp.bitwise_xor(me_ref[0], k), i, 0))
    grid_spec = pltpu.PrefetchScalarGridSpec(
        num_scalar_prefetch=1, grid=(ns, nt),
        in_specs=[pl.BlockSpec((tm, d), lambda k, i, me_ref: (edge_tile(k, i), 0)),
                  pl.BlockSpec((None, N_MOD, d), lambda k, i, me_ref: (edge_tile(k, i) // tpb, 0, 0)),
                  pl.BlockSpec((1, d), lambda k, i, me_ref: (0, 0)),
                  any_spec, any_spec, any_spec] + [any_spec] * n_later,
        out_specs=[pl.BlockSpec((tm, d), lambda k, i, me_ref: (last_tile(k, i), 0)),
                   pl.BlockSpec((tm, d), lambda k, i, me_ref: (last_tile(k, i), 0)),
                   hid, hid, any_spec, any_spec, any_spec] + [any_spec] * n_later,
        scratch_shapes=[pltpu.VMEM((t, d), F32), pltpu.VMEM((t, d), BF16),
                        pltpu.VMEM(shards[0].shape, BF16), pltpu.VMEM(shards[1].shape, BF16),
                        pltpu.VMEM(shards[2].shape, BF16)] + ex.scratch_shapes + ex_later.scratch_shapes)
    outs = pl.pallas_call(
        body, name="ffn_fwd_gathering", grid_spec=grid_spec,
        out_shape=[jax.ShapeDtypeStruct((t, d), F32), jax.ShapeDtypeStruct((t, d), BF16),
                   jax.ShapeDtypeStruct((ns, t, fs), BF16), jax.ShapeDtypeStruct((ns, t, fs), BF16)]
        + ex.out_shape + ex_later.out_shape,
        compiler_params=_params("arbitrary", "arbitrary"),
    )(me, x, mod, g, *shards, *later)
    return outs[:4], list(outs[4:7]), list(outs[7:])


def _ffn_bwd(dxo, x, f, mod, g, gate, up, wg, wu, wd, k, tm, exchange=None):
    t, d = x.shape
    ns, _, fs = wg.shape
    nt = t // tm
    nb = mod.shape[0]
    tpb = nt // nb
    rows = tm // FFN_CHUNKS

    def body(dxo_ref, x_ref, f_ref, mod_ref, g_ref, gg_ref, uu_ref, wg_ref, wu_ref, wd_ref,
             dx_ref, dgg_ref, duu_ref, act_ref, h_ref, df_ref, dmod_ref, dg_ref, acc):
        i, j = pl.program_id(0), pl.program_id(1)

        @pl.when(j == 0)
        def _():
            df = 0.5 * mod_ref[3 * k + 2:3 * k + 3, :] * dxo_ref[...]
            df_ref[...] = df.astype(BF16)
            h_ref[...] = _modulate(x_ref[...], g_ref[...], mod_ref, k).astype(BF16)
            acc[...] = jnp.zeros_like(acc)

        chunks = [pl.ds(c * rows, rows) for c in range(FFN_CHUNKS)]
        wg, wu, wd = wg_ref[...], wu_ref[...], wd_ref[...]
        dacts = [_dot_nt(df_ref[rs, :], wd) for rs in chunks]
        dgates, dups = [], []
        for rs, dact in zip(chunks, dacts):
            gv, uv = gg_ref[rs, :].astype(F32), uu_ref[rs, :].astype(F32)
            sig = _sigmoid(gv)
            s = gv * sig
            act_ref[rs, :] = (s * uv).astype(BF16)
            dups.append((dact * s).astype(BF16))
            dgates.append((dact * uv * (sig * (1.0 + gv * (1.0 - sig)))).astype(BF16))
        dhs = [_dot_nt(dg, wg) + _dot_nt(du, wu) for dg, du in zip(dgates, dups)]
        for rs, dg, du, dh in zip(chunks, dgates, dups, dhs):
            dgg_ref[rs, :] = dg
            duu_ref[rs, :] = du
            acc[rs, :] += dh

        @pl.when(j == ns - 1)
        def _():
            dx, dshift, dscale, dg = _modulate_bwd(acc[...], x_ref[...], g_ref[...], mod_ref, k)
            dxo_v = dxo_ref[...]
            dx_ref[...] = dxo_v + dx
            dgt = jnp.sum(0.5 * f_ref[...].astype(F32) * dxo_v, axis=0, keepdims=True)

            @pl.when(i % tpb == 0)
            def _():
                dmod_ref[...] = jnp.zeros_like(dmod_ref)

            @pl.when(i == 0)
            def _():
                dg_ref[...] = jnp.zeros_like(dg_ref)

            dmod_ref[0:1, :] += dshift
            dmod_ref[1:2, :] += dscale
            dmod_ref[2:3, :] += dgt
            dg_ref[0:1, :] += dg

    tok = pl.BlockSpec((tm, d), lambda i, j: (i, 0))
    hid = pl.BlockSpec((None, tm, fs), lambda i, j: (j, i, 0))
    return _call(
        body, name=f"ffn_bwd{k}", grid=(nt, ns), args=(dxo, x, f, mod, g, gate, up, wg, wu, wd),
        in_specs=[tok, _once(tok), _once(tok),
                  pl.BlockSpec((None, N_MOD, d), lambda i, j: (i // tpb, 0, 0)),
                  pl.BlockSpec((1, d), lambda i, j: (0, 0)),
                  hid, hid,
                  pl.BlockSpec((None, d, fs), lambda i, j: (j, 0, 0)),
                  pl.BlockSpec((None, d, fs), lambda i, j: (j, 0, 0)),
                  pl.BlockSpec((None, fs, d), lambda i, j: (j, 0, 0))],
        out_specs=[tok, hid, hid, hid, tok, tok,
                   pl.BlockSpec((None, 8, d), lambda i, j: (i // tpb, 0, 0)),
                   pl.BlockSpec((8, d), lambda i, j: (0, 0))],
        out_shape=[jax.ShapeDtypeStruct((t, d), F32),
                   jax.ShapeDtypeStruct((ns, t, fs), BF16), jax.ShapeDtypeStruct((ns, t, fs), BF16),
                   jax.ShapeDtypeStruct((ns, t, fs), BF16),
                   jax.ShapeDtypeStruct((t, d), BF16), jax.ShapeDtypeStruct((t, d), BF16),
                   jax.ShapeDtypeStruct((nb, 8, d), F32), jax.ShapeDtypeStruct((8, d), F32)],
        scratch_shapes=[pltpu.VMEM((tm, d), F32)],
        params=_params("arbitrary", "arbitrary"), exchange=exchange)


def _mm_tn(a, b, a_spec, b_spec, out_shape, n_tiles, name, exchange=None, keep_transposed=False,
           pair_reduce=False):
    n_out = out_shape[0]
    block = tuple(out_shape[1:])
    last = n_tiles - 1
    flip = block[0] > block[1]
    if flip:
        block = block[::-1]
    if flip and keep_transposed:
        flip_back, out_shape = False, (n_out,) + block
    else:
        flip_back = flip
    full_shape = tuple(out_shape)
    n_pairs = n_out // 2
    if pair_reduce:
        out_shape = (n_pairs,) + full_shape[1:]

    def body(a_ref, b_ref, o_ref, acc, *pair):
        i, j = pl.program_id(0), pl.program_id(1)
        prod = _dot_tn(b_ref[...], a_ref[...]) if flip else _dot_tn(a_ref[...], b_ref[...])
        full_ref = pair[0] if pair_reduce else o_ref

        @pl.when(i == 0)
        def _():
            acc[j] = prod

        @pl.when(i > 0)
        def _():
            acc[j] += prod

        @pl.when(i == last)
        def _():
            total = acc[j]
            full_ref[j] = (total.T if flip_back else total).astype(BF16)

        if pair_reduce:
            _, landed, send_sems, recv_sems = pair

            @pl.when(jnp.logical_and(i == last, j == n_out - 1))
            def _():
                x, y, c = lax.axis_index("x"), lax.axis_index("y"), lax.axis_index("c")
                copies = [pltpu.make_async_remote_copy(
                    src_ref=full_ref.at[2 * q + 1 - c], dst_ref=landed.at[q],
                    send_sem=send_sems.at[q], recv_sem=recv_sems.at[q],
                    device_id=(x, y, 1 - c), device_id_type=pl.DeviceIdType.MESH) for q in range(n_pairs)]
                for cp in copies:
                    cp.start()
                for q, cp in enumerate(copies):
                    cp.wait_recv()
                    o_ref[q] = (full_ref[2 * q + c].astype(F32) + landed[q].astype(F32)).astype(BF16)
                for cp in copies:
                    cp.wait_send()

    scratch = [pltpu.VMEM((n_out,) + block, F32)]
    if pair_reduce:
        scratch += [pltpu.VMEM(full_shape, BF16), pltpu.VMEM(out_shape, BF16),
                    pltpu.SemaphoreType.DMA((n_pairs,)), pltpu.SemaphoreType.DMA((n_pairs,))]
    outs, sent = _call(
        body, name=name, grid=(n_tiles, n_out), args=(a, b), in_specs=[a_spec, b_spec],
        out_specs=[pl.BlockSpec(out_shape, lambda i, j: (0,) * len(out_shape))],
        out_shape=[jax.ShapeDtypeStruct(out_shape, BF16)],
        scratch_shapes=scratch,
        params=_params("arbitrary", "arbitrary"), exchange=exchange)
    return (outs[0], sent) if exchange is not None else outs[0]


def _ffn_weight_grads(h, dgate, dup, act, df, tm, tag, stream=False):
    t, d = h.shape
    ns, _, fs = dgate.shape
    nt = t // tm
    tok = pl.BlockSpec((tm, d), lambda i, j: (i, 0))
    hid = pl.BlockSpec((None, tm, fs), lambda i, j: (j, i, 0))
    if not stream:
        gwg = _mm_tn(h, dgate, tok, hid, (ns, d, fs), nt, f"grad_wg{tag}", keep_transposed=True)
        gwu = _mm_tn(h, dup, tok, hid, (ns, d, fs), nt, f"grad_wu{tag}", keep_transposed=True)
        gwd = _mm_tn(act, df, hid, tok, (ns, fs, d), nt, f"grad_wd{tag}")
        return gwg, gwu, gwd
    gwg = _mm_tn(h, dgate, tok, hid, (ns, d, fs), nt, f"grad_wg{tag}", keep_transposed=True, pair_reduce=True)
    gwu, sent_g = _mm_tn(h, dup, tok, hid, (ns, d, fs), nt, f"grad_wu{tag}",
                         _Exchange([gwg], True, chips=[True]), keep_transposed=True, pair_reduce=True)
    gwd, sent_u = _mm_tn(act, df, hid, tok, (ns, fs, d), nt, f"grad_wd{tag}",
                         _Exchange([gwu], True, chips=[True]), pair_reduce=True)
    return sent_g[0], sent_u[0], gwd


def _stage_shape(rows, cols):
    return pltpu.VMEM((cols // LANES, rows, LANES), F32)


def _stage(value, stage_ref):
    for k in range(stage_ref.shape[0]):
        stage_ref[k] = value[:, k * LANES:(k + 1) * LANES]


def _to_residue_rows(stage_ref, dst_ref, dil):
    rows = stage_ref.shape[1] // dil
    for r in range(dil):
        for k in range(stage_ref.shape[0]):
            dst_ref[r, :, k * LANES:(k + 1) * LANES] = (
                stage_ref.at[k][pl.ds(r, rows, stride=dil), :].astype(dst_ref.dtype))


def _from_residue_rows(src_ref, stage_ref, dil):
    rows = stage_ref.shape[1] // dil
    chunks = range(stage_ref.shape[0])
    for r in range(dil):
        for k in chunks:
            stage_ref.at[k][pl.ds(r, rows, stride=dil), :] = src_ref[r, :, k * LANES:(k + 1) * LANES].astype(F32)
    return jnp.concatenate([stage_ref[k] for k in chunks], axis=1)


def _residue_shape(nb, seq, width, dil, dtype):
    return jax.ShapeDtypeStruct((nb, dil, seq // dil, width), dtype)


def _residue_spec(tm, tpb, cols, dil, col_block):
    return pl.BlockSpec((None, dil, tm // dil, cols),
                        lambda i, *rest: (i // tpb, 0, i % tpb, col_block(i, *rest)))


def _qkv_fwd(x, mod, g, win, tm):
    t, d = x.shape
    ns, _, cs = win.shape
    nt = t // tm
    nb = mod.shape[0]
    tpb = nt // nb
    seq = t // nb
    half = ns // 2
    n_res = len(DILATIONS) - 1

    def body(x_ref, mod_ref, g_ref, w_ref, sb_ref, dil_ref, *rest):
        res_refs, h_ref, sc = rest[:n_res], rest[n_res], rest[n_res + 1]
        j = pl.program_id(1)

        @pl.when(j == 0)
        def _():
            h_ref[...] = _modulate(x_ref[...], g_ref[...], mod_ref, 1).astype(BF16)

        res = _dot(h_ref[...], w_ref[...])

        @pl.when(j < half)
        def _():
            sb_ref[...] = res.astype(BF16)

        @pl.when(j >= half)
        def _():
            dil_ref[...] = res.astype(BF16)
            _stage(res, sc)
            for ref, dil in zip(res_refs, DILATIONS[1:]):
                _to_residue_rows(sc, ref, dil)

    def dil_col(i, j):
        return jnp.maximum(j - half, 0)

    tok = pl.BlockSpec((tm, d), lambda i, j: (i, 0))
    wide = jax.ShapeDtypeStruct((t, half * cs), BF16)
    outs = pl.pallas_call(
        body, name="qkv_fwd", grid=(nt, ns),
        in_specs=[tok,
                  pl.BlockSpec((None, N_MOD, d), lambda i, j: (i // tpb, 0, 0)),
                  pl.BlockSpec((1, d), lambda i, j: (0, 0)),
                  pl.BlockSpec((None, d, cs), lambda i, j: (j, 0, 0))],
        out_specs=[pl.BlockSpec((tm, cs), lambda i, j: (i, jnp.minimum(j, half - 1))),
                   pl.BlockSpec((tm, cs), lambda i, j: (i, dil_col(i, j)))]
        + [_residue_spec(tm, tpb, cs, dil, dil_col) for dil in DILATIONS[1:]] + [tok],
        out_shape=[wide, wide] + [_residue_shape(nb, seq, half * cs, dil, BF16) for dil in DILATIONS[1:]]
        + [jax.ShapeDtypeStruct((t, d), BF16)],
        scratch_shapes=[_stage_shape(tm, cs)],
        compiler_params=_params("arbitrary", "arbitrary"),
    )(x, mod, g, win)
    qkv_dil = [outs[1]] + [a.reshape(t, half * cs) for a in outs[2:2 + n_res]]
    return outs[0], qkv_dil, outs[-1]


def _qkv_bwd(dqkv, dxo, x, mod, g, win, tm, exchange=None):
    t, d = x.shape
    ns, _, cs = win.shape
    nt = t // tm
    nb = mod.shape[0]
    tpb = nt // nb

    def body(dq_ref, dxo_ref, x_ref, mod_ref, g_ref, w_ref, dx_ref, dmod_ref, dg_ref, acc):
        i, j = pl.program_id(0), pl.program_id(1)

        @pl.when(j == 0)
        def _():
            acc[...] = jnp.zeros_like(acc)

        acc[...] += _dot_nt(dq_ref[...], w_ref[...])

        @pl.when(j == ns - 1)
        def _():
            dx, dshift, dscale, dg = _modulate_bwd(acc[...], x_ref[...], g_ref[...], mod_ref, 1)
            dx_ref[...] = dxo_ref[...] + dx

            @pl.when(i % tpb == 0)
            def _():
                dmod_ref[...] = jnp.zeros_like(dmod_ref)

            @pl.when(i == 0)
            def _():
                dg_ref[...] = jnp.zeros_like(dg_ref)

            dmod_ref[0:1, :] += dshift
            dmod_ref[1:2, :] += dscale
            dg_ref[0:1, :] += dg

    tok = pl.BlockSpec((tm, d), lambda i, j: (i, 0))
    return _call(
        body, name="qkv_bwd", grid=(nt, ns), args=(dqkv, dxo, x, mod, g, win),
        in_specs=[pl.BlockSpec((tm, cs), lambda i, j: (i, j)), tok, tok,
                  pl.BlockSpec((None, N_MOD, d), lambda i, j: (i // tpb, 0, 0)),
                  pl.BlockSpec((1, d), lambda i, j: (0, 0)),
                  pl.BlockSpec((None, d, cs), lambda i, j: (j, 0, 0))],
        out_specs=[tok,
                   pl.BlockSpec((None, 8, d), lambda i, j: (i // tpb, 0, 0)),
                   pl.BlockSpec((8, d), lambda i, j: (0, 0))],
        out_shape=[jax.ShapeDtypeStruct((t, d), F32),
                   jax.ShapeDtypeStruct((nb, 8, d), F32), jax.ShapeDtypeStruct((8, d), F32)],
        scratch_shapes=[pltpu.VMEM((tm, d), F32)],
        params=_params("arbitrary", "arbitrary"), exchange=exchange)


def _heads(a):
    return [a[:, h * HEAD_DIM:(h + 1) * HEAD_DIM] for h in range(a.shape[1] // HEAD_DIM)]


def _own_lanes():
    lane = lax.broadcasted_iota(jnp.int32, (1, LANES), 1)
    return [lane < HEAD_DIM, lane >= HEAD_DIM]


def _pair_tiles(a):
    return [a[:, (h // 2) * LANES:(h // 2 + 1) * LANES] for h in range(a.shape[1] // HEAD_DIM)]


def _own_tiles(a, own):
    return [jnp.where(own[h % 2], tile, jnp.zeros_like(tile)) for h, tile in enumerate(_pair_tiles(a))]


def _merge_tiles(per_head, own):
    return jnp.concatenate([jnp.where(own[0], per_head[h], per_head[h + 1])
                            for h in range(0, len(per_head), 2)], axis=1)


def _scaled(q):
    return (q.astype(F32) * (HEAD_DIM ** -0.5)).astype(BF16)


def _sb_logits(qh, kh, tri, causal):
    zs = [_dot_nt(q, k) for q, k in zip(qh, kh)]
    es = [jnp.exp(-jnp.abs(z)) for z in zs]
    log_nots = [-(jnp.maximum(z, 0.0) + jnp.log(1.0 + e)) for z, e in zip(zs, es)]
    if causal is not None:
        log_nots = [jnp.where(causal, ln, 0.0) for ln in log_nots]
    return zs, es, [_split_dot(ln, tri) for ln in log_nots]


def _sb_masks():
    rows = lax.broadcasted_iota(jnp.int32, (SB_BLOCK, SB_BLOCK), 0)
    cols = lax.broadcasted_iota(jnp.int32, (SB_BLOCK, SB_BLOCK), 1)
    return (rows >= cols).astype(BF16), (rows <= cols).astype(BF16), cols < rows


def _sb_fwd(qkv, nb, seq, exchange=None):
    t = qkv.shape[0]
    n_pairs = (qkv.shape[1] // 3) // SB_WIDTH
    tb = SB_BLOCK
    n_blk = seq // tb

    def body(q_ref, k_ref, v_ref, o_ref, c_ref):
        tri, _, causal = _sb_masks()
        own = _own_lanes()

        def key_block(qh, kj, carry, mask):
            ks = pl.multiple_of(kj * tb, tb)
            kh, vh = _pair_tiles(k_ref[pl.ds(ks, tb), :]), _pair_tiles(v_ref[pl.ds(ks, tb), :])
            zs, _, suffixes = _sb_logits(qh, kh, tri, mask)
            ws = [jnp.exp(z + suffix + cr[1]) for z, suffix, cr in zip(zs, suffixes, carry)]
            if mask is not None:
                ws = [jnp.where(mask, w, 0.0) for w in ws]
            pv = [_dot(w.astype(BF16), v) for w, v in zip(ws, vh)]
            return tuple((cr[0] + p, cr[1] + suffix[:, 0:1]) for cr, p, suffix in zip(carry, pv, suffixes))

        def query_block(qi, _):
            qs = pl.multiple_of(qi * tb, tb)
            qh = _own_tiles(_scaled(q_ref[pl.ds(qs, tb), :]), own)
            zero = (jnp.zeros((tb, LANES), F32), jnp.zeros((tb, 1), F32))
            carry = key_block(qh, qi, (zero,) * SB_HEADS, causal)
            carry = lax.fori_loop(0, qi, lambda it, cr: key_block(qh, qi - 1 - it, cr, None), carry)
            o_ref[pl.ds(qs, tb), :] = _merge_tiles([cr[0] for cr in carry], own)
            c_ref[pl.ds(qs, tb), :] = _merge_tiles([jnp.broadcast_to(cr[1], (tb, LANES)) for cr in carry], own)
            return 0

        lax.fori_loop(0, n_blk, query_block, 0)

    def spec(offset):
        return pl.BlockSpec((seq, SB_WIDTH), lambda b, p: (b, offset + p))

    out = jax.ShapeDtypeStruct((t, n_pairs * SB_WIDTH), F32)
    return _call(
        body, name="sb_fwd", grid=(nb, n_pairs), args=(qkv, qkv, qkv),
        in_specs=[spec(0), spec(n_pairs), spec(2 * n_pairs)],
        out_specs=[spec(0), spec(0)], out_shape=[out, out],
        params=_params("arbitrary", "arbitrary"), exchange=exchange)


def _sb_bwd(qkv, do, csum, nb, seq, exchange=None):
    t = qkv.shape[0]
    n_pairs = (qkv.shape[1] // 3) // SB_WIDTH
    tb = SB_BLOCK
    n_blk = seq // tb
    scale = HEAD_DIM ** -0.5

    def body(q_ref, k_ref, v_ref, do_ref, c_ref, dq_ref, dk_ref, dv_ref, dkt_acc, dvt_acc):
        tri, tri_prefix, causal = _sb_masks()
        own = _own_lanes()
        dkt_acc[...] = jnp.zeros_like(dkt_acc)
        dvt_acc[...] = jnp.zeros_like(dvt_acc)

        def key_block(qh, qth, doh, doth, ch, kj, carry, mask):
            ks = pl.multiple_of(kj * tb, tb)
            kh, vh = _pair_tiles(k_ref[pl.ds(ks, tb), :]), _pair_tiles(v_ref[pl.ds(ks, tb), :])
            heads = range(SB_HEADS)
            zs, es, suffixes = _sb_logits(qh, kh, tri, mask)
            dws = [_dot_nt(doh[h], vh[h]) for h in heads]
            lefts = [carry[h][1] + suffixes[h][:, 0:1] for h in heads]
            ws = [jnp.exp(zs[h] + suffixes[h] + (ch[h] - lefts[h])) for h in heads]
            if mask is not None:
                ws = [jnp.where(mask, w, 0.0) for w in ws]
            dlws = [ws[h] * dws[h] for h in heads]
            dprefixes = [_split_dot(dlw, tri_prefix) for dlw in dlws]
            dvts = [_dot(doth[h], ws[h].astype(BF16)) for h in heads]
            dzbs = []
            for h in heads:
                sig = jnp.where(zs[h] >= 0.0, 1.0, es[h]) * pl.reciprocal(1.0 + es[h], approx=True)
                dz = dlws[h] - sig * (carry[h][2] + dprefixes[h])
                if mask is not None:
                    dz = jnp.where(mask, dz, 0.0)
                dzbs.append(dz.astype(BF16))
            dkts = [_dot(qth[h], dzbs[h]) for h in heads]
            dqs = [_dot(dzbs[h], kh[h]) for h in heads]
            dkt_acc[:, pl.ds(ks, tb)] += jnp.concatenate([dkts[h] + dkts[h + 1] for h in heads[::2]], axis=0)
            dvt_acc[:, pl.ds(ks, tb)] += jnp.concatenate([dvts[h] + dvts[h + 1] for h in heads[::2]], axis=0)
            return tuple((carry[h][0] + dqs[h], lefts[h], carry[h][2] + dprefixes[h][:, tb - 1:tb])
                         for h in heads)

        def query_block(qi, _):
            qs = pl.multiple_of(qi * tb, tb)
            qh = _own_tiles(_scaled(q_ref[pl.ds(qs, tb), :]), own)
            doh = _own_tiles(do_ref[pl.ds(qs, tb), :], own)
            qth = [a.astype(F32).T.astype(BF16) for a in qh]
            doth = [a.T.astype(BF16) for a in doh]
            doh = [a.astype(BF16) for a in doh]
            cv = c_ref[pl.ds(qs, tb), :]
            ch = [cv[:, h * HEAD_DIM:h * HEAD_DIM + 1] for h in range(SB_HEADS)]
            zero = (jnp.zeros((tb, LANES), F32), jnp.zeros((tb, 1), F32), jnp.zeros((tb, 1), F32))
            carry = lax.fori_loop(
                0, qi, lambda kj, cr: key_block(qh, qth, doh, doth, ch, kj, cr, None), (zero,) * SB_HEADS)
            carry = key_block(qh, qth, doh, doth, ch, qi, carry, causal)
            dq = _merge_tiles([cr[0] for cr in carry], own) * scale
            dq_ref[pl.ds(qs, tb), :] = dq.astype(BF16)
            return 0

        lax.fori_loop(0, n_blk, query_block, 0)
        dk_ref[...] = dkt_acc[...].T.astype(BF16)
        dv_ref[...] = dvt_acc[...].T.astype(BF16)

    def spec(offset):
        return pl.BlockSpec((seq, SB_WIDTH), lambda b, p: (b, offset + p))

    out = jax.ShapeDtypeStruct((t, n_pairs * SB_WIDTH), BF16)
    return _call(
        body, name="sb_bwd", grid=(nb, n_pairs), args=(qkv, qkv, qkv, do, csum),
        in_specs=[spec(0), spec(n_pairs), spec(2 * n_pairs), spec(0), spec(0)],
        out_specs=[spec(0), spec(0), spec(0)],
        out_shape=[out, out, out],
        scratch_shapes=[pltpu.VMEM((SB_WIDTH, seq), F32), pltpu.VMEM((SB_WIDTH, seq), F32)],
        params=_params("arbitrary", "arbitrary"), exchange=exchange)


def _dil_block_scores(qh, kph, kch, bias_ref, has_prev, band_prev, band_cur):
    scale = HEAD_DIM ** -0.5
    heads = range(len(qh))
    no_prev = jnp.where(has_prev, 0.0, NEG_INF)
    zps = [_dot_nt(qh[h], kph[h]) for h in heads]
    zcs = [_dot_nt(qh[h], kch[h]) for h in heads]
    zps = [jnp.where(band_prev, zps[h] * scale + bias_ref[h, :, 0:DIL_BLOCK], NEG_INF) + no_prev for h in heads]
    zcs = [jnp.where(band_cur, zcs[h] * scale + bias_ref[h, :, DIL_BLOCK:2 * DIL_BLOCK], NEG_INF) for h in heads]
    return zps, zcs


def _dil_bands():
    rows = lax.broadcasted_iota(jnp.int32, (DIL_BLOCK, DIL_BLOCK), 0)
    cols = lax.broadcasted_iota(jnp.int32, (DIL_BLOCK, DIL_BLOCK), 1)
    return cols >= rows, cols <= rows


def _dil_fwd(qkv, bias, nb, seq, dil, exchange=None):
    t, width = qkv.shape
    n_pairs = (width // 3) // DIL_WIDTH
    bq = DIL_BLOCK
    n_blk = seq // bq
    per_seq = n_blk // dil
    heads = range(DIL_HEADS)

    def body(q_ref, k_ref, v_ref, bias_ref, o_ref, lse_ref):
        band_prev, band_cur = _dil_bands()
        own = _own_lanes()

        def block(n, _):
            has_prev = (n & (per_seq - 1)) != 0
            qs = pl.multiple_of(n * bq, bq)
            ps = pl.multiple_of(jnp.maximum(n - 1, 0) * bq, bq)
            qh = _own_tiles(q_ref[pl.ds(qs, bq), :], own)
            kp, kc = _pair_tiles(k_ref[pl.ds(ps, bq), :]), _pair_tiles(k_ref[pl.ds(qs, bq), :])
            vp, vc = _pair_tiles(v_ref[pl.ds(ps, bq), :]), _pair_tiles(v_ref[pl.ds(qs, bq), :])
            zps, zcs = _dil_block_scores(qh, kp, kc, bias_ref, has_prev, band_prev, band_cur)
            ms = [jnp.maximum(jnp.max(zps[h], axis=1, keepdims=True), jnp.max(zcs[h], axis=1, keepdims=True))
                  for h in heads]
            eps = [jnp.exp(zps[h] - ms[h]) for h in heads]
            ecs = [jnp.exp(zcs[h] - ms[h]) for h in heads]
            pvs = [_dot(eps[h].astype(BF16), vp[h]) + _dot(ecs[h].astype(BF16), vc[h]) for h in heads]
            dens = [jnp.sum(eps[h], axis=1, keepdims=True) + jnp.sum(ecs[h], axis=1, keepdims=True) for h in heads]
            o_ref[pl.ds(qs, bq), :] = _merge_tiles([pvs[h] / dens[h] for h in heads], own)
            lse_ref[pl.ds(qs, bq), :] = _merge_tiles(
                [jnp.broadcast_to(ms[h] + jnp.log(dens[h]), (bq, LANES)) for h in heads], own)
            return 0

        lax.fori_loop(0, n_blk, block, 0)

    def spec(offset):
        return pl.BlockSpec((seq, DIL_WIDTH), lambda b, p: (b, offset + p))

    out = jax.ShapeDtypeStruct((t, n_pairs * DIL_WIDTH), F32)
    return _call(
        body, name=f"dil_fwd{dil}", grid=(nb, n_pairs), args=(qkv, qkv, qkv, bias),
        in_specs=[spec(0), spec(n_pairs), spec(2 * n_pairs),
                  pl.BlockSpec((DIL_HEADS, bq, 2 * bq), lambda b, p: (p, 0, 0))],
        out_specs=[spec(0), spec(0)], out_shape=[out, out],
        params=_params("arbitrary", "arbitrary"), exchange=exchange)


def _dil_bwd(qkv, bias, do, lse, delta, nb, seq, dil):
    t, width = qkv.shape
    n_pairs = (width // 3) // DIL_WIDTH
    bq = DIL_BLOCK
    n_blk = seq // bq
    per_seq = n_blk // dil
    scale = HEAD_DIM ** -0.5
    heads = range(DIL_HEADS)

    def body(q_ref, k_ref, v_ref, bias_ref, do_ref, lse_ref, dl_ref, dq_ref, dk_ref, dv_ref, db_ref,
             dk_acc, dv_acc):
        band_prev, band_cur = _dil_bands()
        own = _own_lanes()
        dk_acc[...] = jnp.zeros_like(dk_acc)
        dv_acc[...] = jnp.zeros_like(dv_acc)

        @pl.when(pl.program_id(1) == 0)
        def _():
            db_ref[...] = jnp.zeros_like(db_ref)

        def block(n, _):
            has_prev = (n & (per_seq - 1)) != 0
            qs = pl.multiple_of(n * bq, bq)
            ps = pl.multiple_of(jnp.maximum(n - 1, 0) * bq, bq)
            qh = _own_tiles(q_ref[pl.ds(qs, bq), :], own)
            kp, kc = _pair_tiles(k_ref[pl.ds(ps, bq), :]), _pair_tiles(k_ref[pl.ds(qs, bq), :])
            vp, vc = _pair_tiles(v_ref[pl.ds(ps, bq), :]), _pair_tiles(v_ref[pl.ds(qs, bq), :])
            doh = _own_tiles(do_ref[pl.ds(qs, bq), :].astype(BF16), own)
            lse_v, dl_v = lse_ref[pl.ds(qs, bq), :], dl_ref[pl.ds(qs, bq), :]
            zps, zcs = _dil_block_scores(qh, kp, kc, bias_ref, has_prev, band_prev, band_cur)
            dpp = [_dot_nt(doh[h], vp[h]) for h in heads]
            dpc = [_dot_nt(doh[h], vc[h]) for h in heads]
            lse_h = [lse_v[:, h * HEAD_DIM:h * HEAD_DIM + 1] for h in heads]
            dl_h = [dl_v[:, h * HEAD_DIM:h * HEAD_DIM + 1] for h in heads]
            pps = [jnp.exp(zps[h] - lse_h[h]) for h in heads]
            pcs = [jnp.exp(zcs[h] - lse_h[h]) for h in heads]
            dvp = [_dot_tn(pps[h].astype(BF16), doh[h]) for h in heads]
            dvc = [_dot_tn(pcs[h].astype(BF16), doh[h]) for h in heads]
            dzps = [pps[h] * (dpp[h] - dl_h[h]) for h in heads]
            dzcs = [pcs[h] * (dpc[h] - dl_h[h]) for h in heads]
            dzp_b = [(dzps[h] * scale).astype(BF16) for h in heads]
            dzc_b = [(dzcs[h] * scale).astype(BF16) for h in heads]
            dqs = [_dot(dzp_b[h], kp[h]) + _dot(dzc_b[h], kc[h]) for h in heads]
            dkp = [_dot_tn(dzp_b[h], qh[h]) for h in heads]
            dkc = [_dot_tn(dzc_b[h], qh[h]) for h in heads]
            for h in heads:
                db_ref[h, :, 0:bq] += dzps[h]
                db_ref[h, :, bq:2 * bq] += dzcs[h]
            def pair_sums(per_head):
                return jnp.concatenate([per_head[h] + per_head[h + 1] for h in heads[::2]], axis=1)

            dq_ref[pl.ds(qs, bq), :] = _merge_tiles(dqs, own).astype(BF16)
            dk_acc[pl.ds(ps, bq), :] += pair_sums(dkp)
            dk_acc[pl.ds(qs, bq), :] += pair_sums(dkc)
            dv_acc[pl.ds(ps, bq), :] += pair_sums(dvp)
            dv_acc[pl.ds(qs, bq), :] += pair_sums(dvc)
            return 0

        lax.fori_loop(0, n_blk, block, 0)
        dk_ref[...] = dk_acc[...].astype(BF16)
        dv_ref[...] = dv_acc[...].astype(BF16)

    def spec(offset):
        return pl.BlockSpec((seq, DIL_WIDTH), lambda p, b: (b, offset + p))

    bias_spec = pl.BlockSpec((DIL_HEADS, bq, 2 * bq), lambda p, b: (p, 0, 0))
    out = jax.ShapeDtypeStruct((t, n_pairs * DIL_WIDTH), BF16)
    return pl.pallas_call(
        body, name=f"dil_bwd{dil}", grid=(n_pairs, nb),
        in_specs=[spec(0), spec(n_pairs), spec(2 * n_pairs), bias_spec, spec(0), spec(0), spec(0)],
        out_specs=[spec(0), spec(0), spec(0), bias_spec],
        out_shape=[out, out, out, jax.ShapeDtypeStruct(bias.shape, F32)],
        scratch_shapes=[pltpu.VMEM((seq, DIL_WIDTH), F32), pltpu.VMEM((seq, DIL_WIDTH), F32)],
        compiler_params=_params("arbitrary", "arbitrary"),
    )(qkv, qkv, qkv, bias, do, lse, delta)


def _head_blocks(width):
    rows = lax.broadcasted_iota(jnp.int32, (width, width), 0) // HEAD_DIM
    cols = lax.broadcasted_iota(jnp.int32, (width, width), 1) // HEAD_DIM
    return (rows == cols).astype(BF16)


def _head_mean(v, gmat):
    return _split_dot(v, gmat) * (1.0 / HEAD_DIM)


def _residue_views(arrays, nb, seq):
    return [a if dil == 1 else a.reshape(nb, dil, seq // dil, a.shape[1]) for a, dil in zip(arrays, DILATIONS)]


def _mix_out_fwd(osb, ocs, lses, gsb, gdil, wout, x, mod, tm):
    t, d = x.shape
    ds = osb.shape[1]
    nt = t // tm
    nb = mod.shape[0]
    tpb = nt // nb
    seq = t // nb
    n_cfg = len(DILATIONS)

    def body(osb_ref, *refs):
        oc_refs, lse_refs = refs[:n_cfg], refs[n_cfg:2 * n_cfg]
        gsb_ref, gdil_ref, w_ref, x_ref, mod_ref = refs[2 * n_cfg:2 * n_cfg + 5]
        xo_ref, on_ref, m_ref, odil_ref = refs[2 * n_cfg + 5:2 * n_cfg + 9]
        ld_refs = refs[2 * n_cfg + 9:3 * n_cfg + 9]
        stages, sc = refs[3 * n_cfg + 9:]
        ocv, lsev = [oc_refs[0][...]], [lse_refs[0][...]]
        for i, dil in enumerate(DILATIONS[1:]):
            ocv.append(_from_residue_rows(oc_refs[i + 1], stages.at[2 * i], dil))
            lsev.append(_from_residue_rows(lse_refs[i + 1], stages.at[2 * i + 1], dil))
        top = functools.reduce(jnp.maximum, lsev)
        total = top + jnp.log(sum(jnp.exp(l - top) for l in lsev))
        odil = sum(jnp.exp(l - total) * o for o, l in zip(ocv, lsev))
        odil_ref[...] = odil
        ld_refs[0][...] = total
        _stage(total, sc)
        for ref, dil in zip(ld_refs[1:], DILATIONS[1:]):
            _to_residue_rows(sc, ref, dil)
        gm = _head_blocks(ds)
        parts = []
        for o, g_ref in ((osb_ref[...], gsb_ref), (odil, gdil_ref)):
            parts.append(o * lax.rsqrt(_head_mean(o * o, gm) + EPS) * g_ref[...])
        on = jnp.concatenate(parts, axis=1).astype(BF16)
        on_ref[...] = on
        m = _dot(on, w_ref[...])
        m_ref[...] = m
        xo_ref[...] = x_ref[...] + mod_ref[5:6, :] * m

    tok = pl.BlockSpec((tm, d), lambda i: (i, 0))
    hd = pl.BlockSpec((tm, ds), lambda i: (i, 0))
    res = [hd] + [_residue_spec(tm, tpb, ds, dil, lambda i: 0) for dil in DILATIONS[1:]]
    res_shape = [jax.ShapeDtypeStruct((t, ds), F32)] + [_residue_shape(nb, seq, ds, dil, F32) for dil in DILATIONS[1:]]
    gain = pl.BlockSpec((1, ds), lambda i: (0, 0))
    outs = pl.pallas_call(
        body, name="mix_out_fwd", grid=(nt,),
        in_specs=[hd] + res + res + [gain, gain,
                  pl.BlockSpec(wout.shape, lambda i: (0, 0)),
                  tok, pl.BlockSpec((None, N_MOD, d), lambda i: (i // tpb, 0, 0))],
        out_specs=[tok, pl.BlockSpec((tm, 2 * ds), lambda i: (i, 0)), tok, hd] + res,
        out_shape=[jax.ShapeDtypeStruct((t, d), F32), jax.ShapeDtypeStruct((t, 2 * ds), BF16),
                   jax.ShapeDtypeStruct((t, d), F32), jax.ShapeDtypeStruct((t, ds), F32)] + res_shape,
        scratch_shapes=[pltpu.VMEM((2 * (n_cfg - 1), ds // LANES, tm, LANES), F32), _stage_shape(tm, ds)],
        compiler_params=_params("arbitrary"),
    )(osb, *_residue_views(ocs, nb, seq), *_residue_views(lses, nb, seq), gsb, gdil, wout, x, mod)
    return outs[0], outs[1], outs[2], outs[3], [a.reshape(t, ds) for a in outs[4:]]


def _mix_out_bwd(dxo, m, mod, wout, osb, odil, gsb, gdil, tm):
    t, d = dxo.shape
    ds = osb.shape[1]
    nt = t // tm
    nb = mod.shape[0]
    tpb = nt // nb
    seq = t // nb
    n_cfg = len(DILATIONS)

    def body(dxo_ref, m_ref, mod_ref, w_ref, osb_ref, odil_ref, gsb_ref, gdil_ref,
             dm_ref, dosb_ref, *rest):
        do_refs, dl_refs = rest[:n_cfg], rest[n_cfg:2 * n_cfg]
        dmod_ref, dg_ref, sc = rest[2 * n_cfg:]
        dodil_ref, dldil_ref = do_refs[0], dl_refs[0]
        i = pl.program_id(0)
        dxo_v = dxo_ref[...]
        dm = (mod_ref[5:6, :] * dxo_v).astype(BF16)
        dm_ref[...] = dm
        dgt = jnp.sum(m_ref[...] * dxo_v, axis=0, keepdims=True)
        don = _dot_nt(dm, w_ref[...])
        gm = _head_blocks(ds)

        @pl.when(i % tpb == 0)
        def _():
            dmod_ref[...] = jnp.zeros_like(dmod_ref)

        @pl.when(i == 0)
        def _():
            dg_ref[...] = jnp.zeros_like(dg_ref)

        dmod_ref[2:3, :] += dgt
        groups = ((osb_ref, gsb_ref, dosb_ref), (odil_ref, gdil_ref, dodil_ref))
        for k, (o_ref, g_ref, do_ref) in enumerate(groups):
            o = o_ref[...]
            dn_out = don[:, k * ds:(k + 1) * ds]
            r = lax.rsqrt(_head_mean(o * o, gm) + EPS)
            n = o * r
            dg_ref[0:1, k * ds:(k + 1) * ds] += jnp.sum(dn_out * n, axis=0, keepdims=True)
            dn = dn_out * g_ref[...]
            do = r * (dn - n * _head_mean(dn * n, gm))
            do_ref[...] = do
            if k == 1:
                delta = _head_mean(do * o, gm) * float(HEAD_DIM)
                dldil_ref[...] = delta
                for value, refs in ((do, do_refs), (delta, dl_refs)):
                    _stage(value, sc)
                    for ref, dil in zip(refs[1:], DILATIONS[1:]):
                        _to_residue_rows(sc, ref, dil)

    tok = pl.BlockSpec((tm, d), lambda i: (i, 0))
    hd = pl.BlockSpec((tm, ds), lambda i: (i, 0))
    res = [hd] + [_residue_spec(tm, tpb, ds, dil, lambda i: 0) for dil in DILATIONS[1:]]
    res_shape = [jax.ShapeDtypeStruct((t, ds), F32)] + [_residue_shape(nb, seq, ds, dil, F32) for dil in DILATIONS[1:]]
    gain = pl.BlockSpec((1, ds), lambda i: (0, 0))
    outs = pl.pallas_call(
        body, name="mix_out_bwd", grid=(nt,),
        in_specs=[tok, tok, pl.BlockSpec((None, N_MOD, d), lambda i: (i // tpb, 0, 0)),
                  pl.BlockSpec(wout.shape, lambda i: (0, 0)), hd, hd, gain, gain],
        out_specs=[tok, hd] + res + res
        + [pl.BlockSpec((None, 8, d), lambda i: (i // tpb, 0, 0)), pl.BlockSpec((8, 2 * ds), lambda i: (0, 0))],
        out_shape=[jax.ShapeDtypeStruct((t, d), BF16), jax.ShapeDtypeStruct((t, ds), F32)] + res_shape + res_shape
        + [jax.ShapeDtypeStruct((nb, 8, d), F32), jax.ShapeDtypeStruct((8, 2 * ds), F32)],
        scratch_shapes=[_stage_shape(tm, ds)],
        compiler_params=_params("arbitrary"),
    )(dxo, m, mod, wout, osb, odil, gsb, gdil)
    flat = [a.reshape(t, ds) for a in outs[2:2 + 2 * n_cfg]]
    return outs[0], outs[1], flat[:n_cfg], flat[n_cfg:], outs[-2], outs[-1]


def _merge_dqkv(sb_parts, dil_parts, nb, tm):
    t, ds = sb_parts[0].shape
    nt = t // tm
    tpb = nt // nb
    seq = t // nb
    n_cfg = len(DILATIONS)

    def body(*refs):
        sb_refs, dil_refs = refs[:3], refs[3:3 + 3 * n_cfg]
        o_ref, sc = refs[3 + 3 * n_cfg:]
        for k in range(3):
            o_ref[:, k * ds:(k + 1) * ds] = sb_refs[k][...]
            total = dil_refs[k * n_cfg][...].astype(F32)
            for i, dil in enumerate(DILATIONS[1:]):
                total = total + _from_residue_rows(dil_refs[k * n_cfg + i + 1], sc, dil)
            o_ref[:, (3 + k) * ds:(4 + k) * ds] = total.astype(BF16)

    hd = pl.BlockSpec((tm, ds), lambda i: (i, 0))
    res = [hd] + [_residue_spec(tm, tpb, ds, dil, lambda i: 0) for dil in DILATIONS[1:]]
    views = [v for parts in dil_parts for v in _residue_views(parts, nb, seq)]
    return pl.pallas_call(
        body, name="merge_dqkv", grid=(nt,),
        in_specs=[hd] * 3 + res * 3,
        out_specs=pl.BlockSpec((tm, 6 * ds), lambda i: (i, 0)),
        out_shape=jax.ShapeDtypeStruct((t, 6 * ds), BF16),
        scratch_shapes=[_stage_shape(tm, ds)],
        compiler_params=_params("arbitrary"),
    )(*sb_parts, *views)


def _loss_head(x, target, g, tm):
    t, d = x.shape

    def body(x_ref, t_ref, g_ref, dx_ref, acc_ref):
        @pl.when(pl.program_id(0) == 0)
        def _():
            acc_ref[...] = jnp.zeros_like(acc_ref)

        n, r = _norm(x_ref[...])
        gv = g_ref[...]
        err = n * gv - t_ref[...]
        dy = err * (1.0 / d)
        acc_ref[0:1, :] += jnp.sum(err * err, axis=0, keepdims=True)
        acc_ref[1:2, :] += jnp.sum(dy * n, axis=0, keepdims=True)
        dn = dy * gv
        dx_ref[...] = r * (dn - n * jnp.mean(dn * n, axis=-1, keepdims=True))

    tok = pl.BlockSpec((tm, d), lambda i: (i, 0))
    return pl.pallas_call(
        body, name="loss_head", grid=(t // tm,),
        in_specs=[tok, tok, pl.BlockSpec((1, d), lambda i: (0, 0))],
        out_specs=[tok, pl.BlockSpec((8, d), lambda i: (0, 0))],
        out_shape=[jax.ShapeDtypeStruct((t, d), F32), jax.ShapeDtypeStruct((8, d), F32)],
        compiler_params=_params("arbitrary"),
    )(x, target, g)


def _row_tile(rows):
    if rows <= 256:
        return rows
    for cand in range(256, 15, -16):
        if rows % cand == 0:
            return cand
    return rows


def _adamw(w, parts, m, v, name, transposed=False):
    rows, cols = w.shape
    n_parts = parts.shape[0]
    tr = _row_tile(rows)
    c1 = 1.0 / (1.0 - ADAM_B1 ** ADAM_STEP)
    c2 = 1.0 / (1.0 - ADAM_B2 ** ADAM_STEP)

    def body(w_ref, p_ref, m_ref, v_ref, g_ref, d_ref, nm_ref, nv_ref):
        g = p_ref[0].astype(F32)
        for i in range(1, n_parts):
            g = g + p_ref[i].astype(F32)
        wv, mv, vv = w_ref[...], m_ref[...], v_ref[...]
        if transposed:
            wv, mv, vv = wv.T, mv.T, vv.T
        nm = ADAM_B1 * mv + (1.0 - ADAM_B1) * g
        nv = ADAM_B2 * vv + (1.0 - ADAM_B2) * (g * g)
        g_ref[...] = g
        nm_ref[...] = nm
        nv_ref[...] = nv
        d_ref[...] = -ADAM_LR * ((nm * c1) / (jnp.sqrt(nv * c2) + ADAM_EPS) + ADAM_WD * wv)

    blk = pl.BlockSpec((tr, cols), lambda i: (i, 0))
    if transposed:
        oblk = pl.BlockSpec((cols, tr), lambda i: (0, i))
        pblk = pl.BlockSpec((n_parts, cols, tr), lambda i: (0, 0, i))
        out = jax.ShapeDtypeStruct((cols, rows), F32)
    else:
        oblk, pblk = blk, pl.BlockSpec((n_parts, tr, cols), lambda i: (0, i, 0))
        out = jax.ShapeDtypeStruct((rows, cols), F32)
    return pl.pallas_call(
        body, name=name, grid=(rows // tr,),
        in_specs=[blk, pblk, blk, blk],
        out_specs=[oblk, oblk, oblk, oblk], out_shape=[out, out, out, out],
        compiler_params=_params("arbitrary"),
    )(w, parts, m, v)


def _t5_bucket(n):
    max_exact = N_BUCKETS // 2
    nf = np.maximum(n, 1).astype(np.float32)
    large = max_exact + (np.log(nf / max_exact) / math.log(MAX_DISTANCE / max_exact)
                         * (N_BUCKETS - max_exact)).astype(np.int32)
    large = np.minimum(large, N_BUCKETS - 1)
    return np.where(n < max_exact, n, large).astype(np.int32)


def _bucket_onehot():
    table = np.zeros((len(DILATIONS), 2 * DIL_BLOCK + 1, N_BUCKETS), np.float32)
    for i, dil in enumerate(DILATIONS):
        buckets = _t5_bucket(np.arange(DIL_BLOCK + 1) * dil)
        for m in range(DIL_BLOCK + 1):
            table[i, m, buckets[DIL_BLOCK - m]] = 1.0
    return table


def _bias_blocks(rel_bias):
    row = jnp.einsum("cmn,nh->chm", _bucket_onehot(), rel_bias, precision=lax.Precision.HIGHEST)
    n_cfg, n_heads, width = row.shape
    tiled = jnp.tile(row, (1, 1, DIL_BLOCK))[..., :DIL_BLOCK * (width - 1)]
    return tiled.reshape(n_cfg, n_heads, DIL_BLOCK, width - 1)


def _bias_blocks_bwd(dblocks):
    n_cfg, n_heads = dblocks.shape[:2]
    width = 2 * DIL_BLOCK + 1
    flat = dblocks.reshape(n_cfg, n_heads, DIL_BLOCK * (width - 1))
    flat = jnp.pad(flat, ((0, 0), (0, 0), (0, DIL_BLOCK)))
    drow = jnp.sum(flat.reshape(n_cfg, n_heads, DIL_BLOCK, width), axis=2)
    return jnp.einsum("chm,cmn->nh", drow, _bucket_onehot(), precision=lax.Precision.HIGHEST)


def _pad_to(a, axis, size):
    pad = [(0, 0)] * a.ndim
    pad[axis] = (0, size - a.shape[axis])
    return jnp.pad(a, pad)


def _lane_pad(n):
    return -(-n // LANES) * LANES


def _local_step(x, target, mod, gains, weights, rel_bias, tm, me=None):
    nb, seq, d = x.shape
    t = nb * seq
    distributed = me is not None
    g_ffn1, g_mix, g_sb, g_dil, g_ffn2, g_final = gains
    x0 = x.reshape(t, d)
    ds = g_sb.shape[1]
    bias = _bias_blocks(rel_bias)

    def beside(arrays, scatter):
        return _Exchange(arrays, scatter) if distributed else None

    tp, tg = min(PROJ_TILE, seq), min(GRAD_TILE, t)

    if distributed:
        (x1, f1, gate1, up1), (wg1, wu1, wd1), (win, wout) = _ffn_fwd_gathering(
            x0, mod, g_ffn1, weights[:3], me, tm, weights[3:5])
    else:
        wg1, wu1, wd1, win, wout = weights[:5]
        (x1, f1, gate1, up1), _ = _ffn_fwd(x0, mod, g_ffn1, wg1, wu1, wd1, 0, tp)
    wout2 = wout.reshape(-1, d)
    qkv, qkvd, h2 = _qkv_fwd(x1, mod, g_mix, win, tp)
    (osb, csb), got = _sb_fwd(qkv, nb, seq, beside(weights[5:7], False))
    wg2, wu2 = got if distributed else weights[5:7]
    ocs, lses = [], []
    for i, dil in enumerate(DILATIONS):
        (oc, lse), got = _dil_fwd(qkvd[i], bias[i], nb, seq, dil, beside(weights[7:8], False) if i == 0 else None)
        if i == 0:
            wd2 = got[0] if distributed else weights[7]
        ocs.append(oc)
        lses.append(lse)
    x2, on, mix, odil, ldil = _mix_out_fwd(osb, ocs, lses, g_sb, g_dil, wout2, x1, mod, tm)
    (x3, f3, gate3, up3), _ = _ffn_fwd(x2, mod, g_ffn2, wg2, wu2, wd2, 2, tp)
    dx3, head = _loss_head(x3, target.reshape(t, d), g_final, tm)
    loss_sum = 0.5 * jnp.sum(head[0]) / d
    dg_final = head[1:2]

    (dx2, dgate3, dup3, act3, h3, df3, dmod3, dg_ffn2), _ = _ffn_bwd(
        dx3, x2, f3, mod, g_ffn2, gate3, up3, wg2, wu2, wd2, 2, tp)
    gwg2, gwu2, gwd2 = _ffn_weight_grads(h3, dgate3, dup3, act3, df3, tg, 2)

    dm, dosb, dodil, dldil, dmod2b, dg_heads = _mix_out_bwd(
        dx2, mix, mod, wout2, osb, odil, g_sb, g_dil, tm)
    n_out = wout.shape[0]
    gwout = _mm_tn(on, dm,
                   pl.BlockSpec((tg, wout.shape[1]), lambda i, j: (i, j)),
                   pl.BlockSpec((tg, d), lambda i, j: (i, 0)),
                   wout.shape, t // tg, "grad_wout")

    (dq_sb, dk_sb, dv_sb), parts_late = _sb_bwd(qkv, dosb, csb, nb, seq,
                                                beside([gwout, gwg2, gwu2, gwd2], True))
    dil_grads = [_dil_bwd(qkvd[i], bias[i], dodil[i], ldil[i], dldil[i], nb, seq, dil)
                 for i, dil in enumerate(DILATIONS)]
    dqkv = _merge_dqkv([dq_sb, dk_sb, dv_sb], [[g[k] for g in dil_grads] for k in range(3)], nb, tm)
    drel = _bias_blocks_bwd(jnp.stack([g[3] for g in dil_grads]))

    cs = win.shape[2]
    gwin = _mm_tn(h2, dqkv,
                  pl.BlockSpec((tg, d), lambda i, j: (i, 0)),
                  pl.BlockSpec((tg, cs), lambda i, j: (i, j)),
                  win.shape, t // tg, "grad_win")
    (dx1, dmod2a, dg_mix), parts_mid = _qkv_bwd(dqkv, dx2, x1, mod, g_mix, win, tp, beside([gwin], True))

    (dx0, dgate1, dup1, act1, h1, df1, dmod1, dg_ffn1), _ = _ffn_bwd(
        dx1, x0, f1, mod, g_ffn1, gate1, up1, wg1, wu1, wd1, 0, tp)
    gw1 = _ffn_weight_grads(h1, dgate1, dup1, act1, df1, tg, 0, stream=distributed)

    dmod = jnp.concatenate([dmod1[:, 0:3], dmod2a[:, 0:2], dmod2b[:, 2:3], dmod3[:, 0:3]], axis=1)
    wgrads = tuple(gw1) + (tuple(parts_mid + parts_late) if distributed else (gwin, gwout, gwg2, gwu2, gwd2))
    ggrads = (dg_ffn1[0:1], dg_mix[0:1], dg_heads[0:1], drel, dg_ffn2[0:1], dg_final)
    return loss_sum, dx0.reshape(nb, seq, d), wgrads, dmod, ggrads


def kernel(x, c, w_ada, b_ada, g_ffn1, w1_gate, w1_up, w1_down, g_mix, w_in, g_sb_out, g_dil_out, w_out, rel_bias, g_ffn2, w2_gate, w2_up, w2_down, g_final, loss_target, m_w_ada, m_b_ada, m_g_ffn1, m_w1_gate, m_w1_up, m_w1_down, m_g_mix, m_w_in, m_g_sb_out, m_g_dil_out, m_w_out, m_rel_bias, m_g_ffn2, m_w2_gate, m_w2_up, m_w2_down, m_g_final, v_w_ada, v_b_ada, v_g_ffn1, v_w1_gate, v_w1_up, v_w1_down, v_g_mix, v_w_in, v_g_sb_out, v_g_dil_out, v_w_out, v_rel_bias, v_g_ffn2, v_w2_gate, v_w2_up, v_w2_down, v_g_final):
    nb, seq, d = x.shape
    me = 4 * lax.axis_index("x") + 2 * lax.axis_index("y") + lax.axis_index("c")
    tm = min(TOKEN_TILE, seq)
    fs = w1_gate.shape[2]
    fs_pad = _lane_pad(fs)
    ada_cols = w_ada.shape[2]

    def col_shard(w):
        return _pad_to(w[0].astype(BF16), 1, fs_pad)

    def row_shard(w):
        return _pad_to(w[0].astype(BF16), 0, fs_pad)

    shards = [col_shard(w1_gate), col_shard(w1_up), row_shard(w1_down), w_in[0].astype(BF16),
              w_out[0].astype(BF16), col_shard(w2_gate), col_shard(w2_up), row_shard(w2_down)]
    b_cols = lax.dynamic_slice(b_ada, (0, me * ada_cols), (1, ada_cols))
    c_every, mod_all = _first_exchange(_pad_to(c, 0, 8), w_ada[0], b_cols)
    c_all = c_every[:, :nb].reshape(N_DEV * nb, d)
    mod = lax.dynamic_slice(mod_all, (0, me * 8, 0), (N_DEV, nb, ada_cols))
    mod = mod.transpose(1, 0, 2).reshape(nb, N_MOD, d)

    n_sb = g_sb_out.shape[1] * g_sb_out.shape[2]
    gains = (g_ffn1, g_mix, g_sb_out.reshape(1, n_sb), g_dil_out.reshape(1, -1), g_ffn2,
             g_final.reshape(1, d))
    loss_sum, grad_x, parts, dmod, ggrads = _local_step(
        x, loss_target, mod, gains, shards, rel_bias, tm, jnp.reshape(me, (1,)).astype(jnp.int32))
    loss = lax.psum(loss_sum, ("x", "y", "c"))

    dg_ffn1, dg_mix, dg_heads, drel, dg_ffn2, dg_final = ggrads
    width = max(d, dg_heads.shape[1], drel.size)
    small = jnp.concatenate(
        [_pad_to(a.reshape(1, -1), 1, width) for a in (dg_ffn1, dg_mix, dg_ffn2, dg_final, dg_heads, drel)]
        + [jnp.zeros((2, width), F32)], axis=0)
    dmod_pad = _pad_to(dmod.reshape(nb, N_MOD * d), 0, 8)
    last_part, dmod_all, small_all = _exchange(
        [parts[2], jnp.broadcast_to(dmod_pad, (N_DEV,) + dmod_pad.shape),
         jnp.broadcast_to(small, (N_DEV,) + small.shape)], True, "scatter_last", chips=[True, False, False])
    parts = parts[:2] + (last_part,) + parts[3:]
    dmod_all = dmod_all[:, :nb].reshape(N_DEV * nb, N_MOD * d)
    dmod_cols = lax.dynamic_slice(dmod_all, (0, me * ada_cols), (N_DEV * nb, ada_cols))
    gw_ada, gb_ada = _ada_bwd(c_all, dmod_cols, dmod_all)

    def small_part(row, size, shape):
        return small_all[:, row, :size].reshape((N_DEV,) + shape)

    n_rel = rel_bias.shape
    updates = {
        "w_ada": (w_ada[0], gw_ada[None], m_w_ada[0], v_w_ada[0]),
        "b_ada": (b_ada, gb_ada[None], m_b_ada, v_b_ada),
        "g_ffn1": (g_ffn1, small_part(0, d, (1, d)), m_g_ffn1, v_g_ffn1),
        "w1_gate": (w1_gate[0], parts[0][:, :fs, :], m_w1_gate[0], v_w1_gate[0]),
        "w1_up": (w1_up[0], parts[1][:, :fs, :], m_w1_up[0], v_w1_up[0]),
        "w1_down": (w1_down[0], parts[2][:, :fs, :], m_w1_down[0], v_w1_down[0]),
        "g_mix": (g_mix, small_part(1, d, (1, d)), m_g_mix, v_g_mix),
        "w_in": (w_in[0], parts[3], m_w_in[0], v_w_in[0]),
        "g_sb_out": (g_sb_out[0], small_all[:, 4, :n_sb].reshape((N_DEV,) + g_sb_out.shape[1:]),
                     m_g_sb_out[0], v_g_sb_out[0]),
        "g_dil_out": (g_dil_out[0], small_all[:, 4, n_sb:dg_heads.shape[1]].reshape((N_DEV,) + g_dil_out.shape[1:]),
                      m_g_dil_out[0], v_g_dil_out[0]),
        "w_out": (w_out[0], parts[4], m_w_out[0], v_w_out[0]),
        "rel_bias": (rel_bias, small_part(5, drel.size, n_rel), m_rel_bias, v_rel_bias),
        "g_ffn2": (g_ffn2, small_part(2, d, (1, d)), m_g_ffn2, v_g_ffn2),
        "w2_gate": (w2_gate[0], parts[5][:, :fs, :], m_w2_gate[0], v_w2_gate[0]),
        "w2_up": (w2_up[0], parts[6][:, :fs, :], m_w2_up[0], v_w2_up[0]),
        "w2_down": (w2_down[0], parts[7][:, :fs, :], m_w2_down[0], v_w2_down[0]),
        "g_final": (g_final.reshape(1, d), small_part(3, d, (1, d)), m_g_final.reshape(1, d), v_g_final.reshape(1, d)),
    }
    shapes = {"w_ada": w_ada.shape, "b_ada": b_ada.shape, "g_ffn1": g_ffn1.shape, "w1_gate": w1_gate.shape,
              "w1_up": w1_up.shape, "w1_down": w1_down.shape, "g_mix": g_mix.shape, "w_in": w_in.shape,
              "g_sb_out": g_sb_out.shape, "g_dil_out": g_dil_out.shape, "w_out": w_out.shape,
              "rel_bias": rel_bias.shape, "g_ffn2": g_ffn2.shape, "w2_gate": w2_gate.shape,
              "w2_up": w2_up.shape, "w2_down": w2_down.shape, "g_final": g_final.shape}
    grads, deltas, new_m, new_v = [], [], [], []
    for name, (w, p, m, v) in updates.items():
        transposed = name in ("w1_gate", "w1_up", "w2_gate", "w2_up")
        outs = _adamw(w, p, m, v, f"adamw_{name}", transposed)
        for dst, a in zip((grads, deltas, new_m, new_v), outs):
            dst.append((a.T if transposed else a).reshape(shapes[name]))
    return (loss, grad_x, *grads, *deltas, *new_m, *new_v)
```

```python
import functools
import math

import numpy as np
import jax
import jax.numpy as jnp
from jax import lax
from jax.experimental import pallas as pl
from jax.experimental.pallas import tpu as pltpu

F32 = jnp.float32
BF16 = jnp.bfloat16

EPS = 1e-6
NEG_INF = -1e30
HEAD_DIM = 64
LANES = 128
DIL_BLOCK = 128
DILATIONS = (1, 4, 16)
N_BUCKETS = 32
MAX_DISTANCE = 2048
N_MOD = 9
N_DEV = 8
SB_BLOCK = 256
SB_HEADS = 4
SB_WIDTH = SB_HEADS * HEAD_DIM
DIL_HEADS = 4
DIL_WIDTH = DIL_HEADS * HEAD_DIM
TOKEN_TILE = 512
PROJ_TILE = 1024
GRAD_TILE = 1024
FFN_CHUNKS = 2
VMEM_LIMIT_BYTES = 56 * 1024 * 1024

ADAM_LR = 0.001
ADAM_B1 = 0.9
ADAM_B2 = 0.999
ADAM_EPS = 1e-08
ADAM_WD = 0.01
ADAM_STEP = 10

NT_DIMS = (((1,), (1,)), ((), ()))
TN_DIMS = (((0,), (0,)), ((), ()))


def _params(*sem):
    return pltpu.CompilerParams(dimension_semantics=sem, vmem_limit_bytes=VMEM_LIMIT_BYTES)


def _once(spec):
    return pl.BlockSpec(spec.block_shape, spec.index_map, pipeline_mode=pl.Buffered(1))


def _dot(a, b):
    return jnp.dot(a, b, preferred_element_type=F32)


def _dot_nt(a, b):
    return lax.dot_general(a, b, NT_DIMS, preferred_element_type=F32)


def _dot_tn(a, b):
    return lax.dot_general(a, b, TN_DIMS, preferred_element_type=F32)


def _split_dot(a, b):
    hi = a.astype(BF16)
    lo = (a - hi.astype(F32)).astype(BF16)
    return _dot(hi, b) + _dot(lo, b)


def _sigmoid(z):
    return 1.0 / (1.0 + jnp.exp(-z))


def _norm(x):
    r = lax.rsqrt(jnp.mean(x * x, axis=-1, keepdims=True) + EPS)
    return x * r, r


def _modulate(x, g, mod_ref, k):
    n, _ = _norm(x)
    shift = mod_ref[3 * k:3 * k + 1, :]
    scale = mod_ref[3 * k + 1:3 * k + 2, :]
    return n * g * (1.0 + scale) + shift


def _modulate_bwd(dh, x, g, mod_ref, k):
    n, r = _norm(x)
    scale = mod_ref[3 * k + 1:3 * k + 2, :]
    dshift = jnp.sum(dh, axis=0, keepdims=True)
    dscale = jnp.sum(dh * n * g, axis=0, keepdims=True)
    dg = jnp.sum(dh * n * (1.0 + scale), axis=0, keepdims=True)
    dn = dh * g * (1.0 + scale)
    dx = r * (dn - n * jnp.mean(dn * n, axis=-1, keepdims=True))
    return dx, dshift, dscale, dg


class _Exchange:
    def __init__(self, arrays, scatter, relay=False, chips=None):
        assert not (scatter and relay)
        self.arrays = list(arrays)
        self.scatter = scatter
        self.relay = relay
        self.n = len(self.arrays)
        self.chips = list(chips) if chips is not None else [False] * self.n
        assert scatter or not any(self.chips)
        self.out_shape = [
            jax.ShapeDtypeStruct((N_DEV // 2 if ch else N_DEV,) + tuple(a.shape[1:] if scatter else a.shape), a.dtype)
            for a, ch in zip(self.arrays, self.chips)]
        n_remote = self.n * (N_DEV - 1)
        self.scratch_shapes = [pltpu.SemaphoreType.DMA((n_remote,)), pltpu.SemaphoreType.DMA((n_remote,)),
                               pltpu.SemaphoreType.DMA((self.n,))]

    def _copies(self, in_refs, out_refs, sems):
        send_sems, recv_sems, local_sems = sems
        x, y, c = lax.axis_index("x"), lax.axis_index("y"), lax.axis_index("c")
        me = 4 * x + 2 * y + c
        local, remote, relayed = [], {}, {}
        for a in range(self.n):
            if self.chips[a]:
                mine = 2 * x + y
                local.append(pltpu.make_async_copy(in_refs[a].at[mine], out_refs[a].at[mine], local_sems.at[a]))
                for k in (2, 4, 6):
                    px = 1 - x if k & 4 else x
                    py = 1 - y if k & 2 else y
                    sem = a * (N_DEV - 1) + k - 1
                    remote[a, k] = pltpu.make_async_remote_copy(
                        src_ref=in_refs[a].at[2 * px + py], dst_ref=out_refs[a].at[mine],
                        send_sem=send_sems.at[sem], recv_sem=recv_sems.at[sem],
                        device_id=(px, py, c), device_id_type=pl.DeviceIdType.MESH)
                continue
            src = in_refs[a].at[me] if self.scatter else in_refs[a]
            local.append(pltpu.make_async_copy(src, out_refs[a].at[me], local_sems.at[a]))
            for k in range(1, N_DEV):
                px = 1 - x if k & 4 else x
                py = 1 - y if k & 2 else y
                pc = 1 - c if k & 1 else c
                sem = a * (N_DEV - 1) + k - 1
                if self.relay and k & 1 and k > 1:
                    slot = 4 * px + 2 * py + c
                    relayed[a, k] = pltpu.make_async_remote_copy(
                        src_ref=out_refs[a].at[slot], dst_ref=out_refs[a].at[slot],
                        send_sem=send_sems.at[sem], recv_sem=recv_sems.at[sem],
                        device_id=(x, y, 1 - c), device_id_type=pl.DeviceIdType.MESH)
                    continue
                src = in_refs[a].at[4 * px + 2 * py + pc] if self.scatter else in_refs[a]
                remote[a, k] = pltpu.make_async_remote_copy(
                    src_ref=src, dst_ref=out_refs[a].at[me],
                    send_sem=send_sems.at[sem], recv_sem=recv_sems.at[sem],
                    device_id=(px, py, pc), device_id_type=pl.DeviceIdType.MESH)
        return local, remote, relayed

    def start(self, in_refs, out_refs, sems):
        local, remote, _ = self._copies(in_refs, out_refs, sems)
        for cp in local + list(remote.values()):
            cp.start()

    def wait(self, in_refs, out_refs, sems):
        local, remote, relayed = self._copies(in_refs, out_refs, sems)
        for (a, k), cp in relayed.items():
            remote[a, k - 1].wait_recv()
            cp.start()
        for (a, k), cp in remote.items():
            if (a, k + 1) not in relayed:
                cp.wait_recv()
        for cp in relayed.values():
            cp.wait_recv()
        for cp in list(remote.values()) + list(relayed.values()):
            cp.wait_send()
        for cp in local:
            cp.wait()


def _call(body, *, name, args, in_specs, out_specs, out_shape, scratch_shapes=(), grid=(),
          params=None, exchange=None):
    n_in, n_out = len(args), len(out_shape)
    if exchange is None:
        outs = pl.pallas_call(
            body, name=name, grid=grid, in_specs=list(in_specs), out_specs=list(out_specs),
            out_shape=list(out_shape), scratch_shapes=list(scratch_shapes), compiler_params=params,
        )(*args)
        return list(outs), []
    n_ex = exchange.n

    def wrapped(*refs):
        ins, refs = refs[:n_in], refs[n_in:]
        ex_in, refs = refs[:n_ex], refs[n_ex:]
        outs, refs = refs[:n_out], refs[n_out:]
        ex_out, refs = refs[:n_ex], refs[n_ex:]
        scratch, sems = refs[:len(refs) - 3], refs[len(refs) - 3:]
        if not grid:
            exchange.start(ex_in, ex_out, sems)
            body(*ins, *outs, *scratch)
            exchange.wait(ex_in, ex_out, sems)
            return
        first = functools.reduce(jnp.logical_and, [pl.program_id(a) == 0 for a in range(len(grid))])
        last = functools.reduce(jnp.logical_and, [pl.program_id(a) == grid[a] - 1 for a in range(len(grid))])

        @pl.when(first)
        def _():
            exchange.start(ex_in, ex_out, sems)

        body(*ins, *outs, *scratch)

        @pl.when(last)
        def _():
            exchange.wait(ex_in, ex_out, sems)

    any_spec = pl.BlockSpec(memory_space=pl.ANY)
    outs = pl.pallas_call(
        wrapped, name=name, grid=grid,
        in_specs=list(in_specs) + [any_spec] * n_ex, out_specs=list(out_specs) + [any_spec] * n_ex,
        out_shape=list(out_shape) + exchange.out_shape,
        scratch_shapes=list(scratch_shapes) + exchange.scratch_shapes, compiler_params=params,
    )(*args, *exchange.arrays)
    return list(outs[:n_out]), list(outs[n_out:])


def _exchange(arrays, scatter, name, relay=False, chips=None):
    return _call(lambda: None, name=name, args=(), in_specs=(), out_specs=(), out_shape=(),
                 exchange=_Exchange(arrays, scatter, relay, chips))[1]


def _first_exchange(c_pad, shards, w, b):
    rows, d = c_pad.shape
    cols = w.shape[1]
    ex_c = _Exchange([c_pad], False)
    ex_w = _Exchange(shards, False, relay=True)
    ex_m = _Exchange([jax.ShapeDtypeStruct((N_DEV * rows, cols), F32)], False)
    n_w = ex_w.n

    def body(*refs):
        c_ref, w_refs, wa_ref, b_ref = refs[0], refs[1:1 + n_w], refs[1 + n_w], refs[2 + n_w]
        outs = refs[3 + n_w:]
        cg_ref, wg_refs, mg_ref = outs[0], outs[1:1 + n_w], outs[1 + n_w]
        scratch = outs[2 + n_w:]
        sems_c, sems_w, sems_m, c_vm, m_vm = scratch[0:3], scratch[3:6], scratch[6:9], scratch[9], scratch[10]
        ex_c.start([c_ref], [cg_ref], sems_c)
        ex_c.wait([c_ref], [cg_ref], sems_c)
        pltpu.sync_copy(cg_ref, c_vm)
        cv = c_vm[...].reshape(N_DEV * rows, d)
        s = (cv * _sigmoid(cv)).astype(BF16)
        m_vm[...] = _dot(s, wa_ref[...].astype(BF16)) + b_ref[...]
        ex_m.start([m_vm], [mg_ref], sems_m)
        ex_w.start(w_refs, wg_refs, sems_w)
        ex_m.wait([m_vm], [mg_ref], sems_m)
        ex_w.wait(w_refs, wg_refs, sems_w)

    any_spec = pl.BlockSpec(memory_space=pl.ANY)
    vmem_spec = pl.BlockSpec(memory_space=pltpu.VMEM)
    outs = pl.pallas_call(
        body, name="first_exchange",
        in_specs=[any_spec] * (1 + n_w) + [vmem_spec, vmem_spec],
        out_specs=[any_spec] * (2 + n_w),
        out_shape=ex_c.out_shape + ex_w.out_shape + ex_m.out_shape,
        scratch_shapes=ex_c.scratch_shapes + ex_w.scratch_shapes + ex_m.scratch_shapes
        + [pltpu.VMEM((N_DEV, rows, d), F32), pltpu.VMEM((N_DEV * rows, cols), F32)],
        compiler_params=pltpu.CompilerParams(vmem_limit_bytes=VMEM_LIMIT_BYTES),
    )(c_pad, *shards, w, b)
    return outs[0], outs[1 + n_w], list(outs[1:1 + n_w])


def _ada_bwd(c_all, dmod_cols, dmod_all):
    def body(c_ref, dc_ref, da_ref, gw_ref, gb_ref):
        cv = c_ref[...]
        s = cv * _sigmoid(cv)
        gw_ref[...] = lax.dot_general(s, dc_ref[...], TN_DIMS, preferred_element_type=F32,
                                      precision=lax.Precision.HIGHEST)
        gb_ref[...] = jnp.sum(da_ref[...], axis=0, keepdims=True)

    return pl.pallas_call(
        body, name="ada_bwd",
        out_shape=(jax.ShapeDtypeStruct((c_all.shape[1], dmod_cols.shape[1]), F32),
                   jax.ShapeDtypeStruct((1, dmod_all.shape[1]), F32)),
        compiler_params=pltpu.CompilerParams(vmem_limit_bytes=VMEM_LIMIT_BYTES),
    )(c_all, dmod_cols, dmod_all)


def _ffn_fwd(x, mod, g, wg, wu, wd, k, tm, exchange=None):
    t, d = x.shape
    ns, _, fs = wg.shape
    nt = t // tm
    tpb = nt // mod.shape[0]
    rows = tm // FFN_CHUNKS

    def body(x_ref, mod_ref, g_ref, wg_ref, wu_ref, wd_ref, xo_ref, f_ref, gg_ref, uu_ref, h_sc, acc):
        j = pl.program_id(1)

        @pl.when(j == 0)
        def _():
            h_sc[...] = _modulate(x_ref[...], g_ref[...], mod_ref, k).astype(BF16)
            acc[...] = jnp.zeros_like(acc)

        chunks = [pl.ds(c * rows, rows) for c in range(FFN_CHUNKS)]
        wg, wu, wd = wg_ref[...], wu_ref[...], wd_ref[...]
        gates, ups = [], []
        for rs in chunks:
            h = h_sc[rs, :]
            gates.append(_dot(h, wg))
            ups.append(_dot(h, wu))
        acts = [(g * _sigmoid(g) * u).astype(BF16) for g, u in zip(gates, ups)]
        for rs, g, u in zip(chunks, gates, ups):
            gg_ref[rs, :] = g.astype(BF16)
            uu_ref[rs, :] = u.astype(BF16)
        downs = [_dot(a, wd) for a in acts]
        for rs, dn in zip(chunks, downs):
            acc[rs, :] += dn

        @pl.when(j == ns - 1)
        def _():
            f = acc[...]
            f_ref[...] = f.astype(BF16)
            xo_ref[...] = x_ref[...] + 0.5 * mod_ref[3 * k + 2:3 * k + 3, :] * f

    tok = pl.BlockSpec((tm, d), lambda i, j: (i, 0))
    hid = pl.BlockSpec((None, tm, fs), lambda i, j: (j, i, 0))
    return _call(
        body, name=f"ffn_fwd{k}", grid=(nt, ns), args=(x, mod, g, wg, wu, wd),
        in_specs=[tok,
                  pl.BlockSpec((None, N_MOD, d), lambda i, j: (i // tpb, 0, 0)),
                  pl.BlockSpec((1, d), lambda i, j: (0, 0)),
                  pl.BlockSpec((None, d, fs), lambda i, j: (j, 0, 0)),
                  pl.BlockSpec((None, d, fs), lambda i, j: (j, 0, 0)),
                  pl.BlockSpec((None, fs, d), lambda i, j: (j, 0, 0))],
        out_specs=[tok, tok, hid, hid],
        out_shape=[jax.ShapeDtypeStruct((t, d), F32), jax.ShapeDtypeStruct((t, d), BF16),
                   jax.ShapeDtypeStruct((ns, t, fs), BF16), jax.ShapeDtypeStruct((ns, t, fs), BF16)],
        scratch_shapes=[pltpu.VMEM((tm, d), BF16), pltpu.VMEM((tm, d), F32)],
        params=_params("arbitrary", "arbitrary"), exchange=exchange)


def _ffn_bwd(dxo, x, f, mod, g, gate, up, wg, wu, wd, k, tm, exchange=None):
    t, d = x.shape
    ns, _, fs = wg.shape
    nt = t // tm
    nb = mod.shape[0]
    tpb = nt // nb
    rows = tm // FFN_CHUNKS

    def body(dxo_ref, x_ref, f_ref, mod_ref, g_ref, gg_ref, uu_ref, wg_ref, wu_ref, wd_ref,
             dx_ref, dgg_ref, duu_ref, act_ref, h_ref, df_ref, dmod_ref, dg_ref, acc):
        i, j = pl.program_id(0), pl.program_id(1)

        @pl.when(j == 0)
        def _():
            df = 0.5 * mod_ref[3 * k + 2:3 * k + 3, :] * dxo_ref[...]
            df_ref[...] = df.astype(BF16)
            h_ref[...] = _modulate(x_ref[...], g_ref[...], mod_ref, k).astype(BF16)
            acc[...] = jnp.zeros_like(acc)

        chunks = [pl.ds(c * rows, rows) for c in range(FFN_CHUNKS)]
        wg, wu, wd = wg_ref[...], wu_ref[...], wd_ref[...]
        dacts = [_dot_nt(df_ref[rs, :], wd) for rs in chunks]
        dgates, dups = [], []
        for rs, dact in zip(chunks, dacts):
            gv, uv = gg_ref[rs, :].astype(F32), uu_ref[rs, :].astype(F32)
            sig = _sigmoid(gv)
            s = gv * sig
            act_ref[rs, :] = (s * uv).astype(BF16)
            dups.append((dact * s).astype(BF16))
            dgates.append((dact * uv * (sig * (1.0 + gv * (1.0 - sig)))).astype(BF16))
        dhs = [_dot_nt(dg, wg) + _dot_nt(du, wu) for dg, du in zip(dgates, dups)]
        for rs, dg, du, dh in zip(chunks, dgates, dups, dhs):
            dgg_ref[rs, :] = dg
            duu_ref[rs, :] = du
            acc[rs, :] += dh

        @pl.when(j == ns - 1)
        def _():
            dx, dshift, dscale, dg = _modulate_bwd(acc[...], x_ref[...], g_ref[...], mod_ref, k)
            dxo_v = dxo_ref[...]
            dx_ref[...] = dxo_v + dx
            dgt = jnp.sum(0.5 * f_ref[...].astype(F32) * dxo_v, axis=0, keepdims=True)

            @pl.when(i % tpb == 0)
            def _():
                dmod_ref[...] = jnp.zeros_like(dmod_ref)

            @pl.when(i == 0)
            def _():
                dg_ref[...] = jnp.zeros_like(dg_ref)

            dmod_ref[0:1, :] += dshift
            dmod_ref[1:2, :] += dscale
            dmod_ref[2:3, :] += dgt
            dg_ref[0:1, :] += dg

    tok = pl.BlockSpec((tm, d), lambda i, j: (i, 0))
    hid = pl.BlockSpec((None, tm, fs), lambda i, j: (j, i, 0))
    return _call(
        body, name=f"ffn_bwd{k}", grid=(nt, ns), args=(dxo, x, f, mod, g, gate, up, wg, wu, wd),
        in_specs=[tok, _once(tok), _once(tok),
                  pl.BlockSpec((None, N_MOD, d), lambda i, j: (i // tpb, 0, 0)),
                  pl.BlockSpec((1, d), lambda i, j: (0, 0)),
                  hid, hid,
                  pl.BlockSpec((None, d, fs), lambda i, j: (j, 0, 0)),
                  pl.BlockSpec((None, d, fs), lambda i, j: (j, 0, 0)),
                  pl.BlockSpec((None, fs, d), lambda i, j: (j, 0, 0))],
        out_specs=[tok, hid, hid, hid, tok, tok,
                   pl.BlockSpec((None, 8, d), lambda i, j: (i // tpb, 0, 0)),
                   pl.BlockSpec((8, d), lambda i, j: (0, 0))],
        out_shape=[jax.ShapeDtypeStruct((t, d), F32),
                   jax.ShapeDtypeStruct((ns, t, fs), BF16), jax.ShapeDtypeStruct((ns, t, fs), BF16),
                   jax.ShapeDtypeStruct((ns, t, fs), BF16),
                   jax.ShapeDtypeStruct((t, d), BF16), jax.ShapeDtypeStruct((t, d), BF16),
                   jax.ShapeDtypeStruct((nb, 8, d), F32), jax.ShapeDtypeStruct((8, d), F32)],
        scratch_shapes=[pltpu.VMEM((tm, d), F32)],
        params=_params("arbitrary", "arbitrary"), exchange=exchange)


def _mm_tn(a, b, a_spec, b_spec, out_shape, n_tiles, name, exchange=None, keep_transposed=False,
           pair_reduce=False):
    n_out = out_shape[0]
    block = tuple(out_shape[1:])
    last = n_tiles - 1
    flip = block[0] > block[1]
    if flip:
        block = block[::-1]
    if flip and keep_transposed:
        flip_back, out_shape = False, (n_out,) + block
    else:
        flip_back = flip
    full_shape = tuple(out_shape)
    n_pairs = n_out // 2
    if pair_reduce:
        out_shape = (n_pairs,) + full_shape[1:]

    def body(a_ref, b_ref, o_ref, acc, *pair):
        i, j = pl.program_id(0), pl.program_id(1)
        prod = _dot_tn(b_ref[...], a_ref[...]) if flip else _dot_tn(a_ref[...], b_ref[...])
        full_ref = pair[0] if pair_reduce else o_ref

        @pl.when(i == 0)
        def _():
            acc[j] = prod

        @pl.when(i > 0)
        def _():
            acc[j] += prod

        @pl.when(i == last)
        def _():
            total = acc[j]
            full_ref[j] = (total.T if flip_back else total).astype(BF16)

        if pair_reduce:
            _, landed, send_sems, recv_sems = pair

            @pl.when(jnp.logical_and(i == last, j == n_out - 1))
            def _():
                x, y, c = lax.axis_index("x"), lax.axis_index("y"), lax.axis_index("c")
                copies = [pltpu.make_async_remote_copy(
                    src_ref=full_ref.at[2 * q + 1 - c], dst_ref=landed.at[q],
                    send_sem=send_sems.at[q], recv_sem=recv_sems.at[q],
                    device_id=(x, y, 1 - c), device_id_type=pl.DeviceIdType.MESH) for q in range(n_pairs)]
                for cp in copies:
                    cp.start()
                for q, cp in enumerate(copies):
                    cp.wait_recv()
                    o_ref[q] = (full_ref[2 * q + c].astype(F32) + landed[q].astype(F32)).astype(BF16)
                for cp in copies:
                    cp.wait_send()

    scratch = [pltpu.VMEM((n_out,) + block, F32)]
    if pair_reduce:
        scratch += [pltpu.VMEM(full_shape, BF16), pltpu.VMEM(out_shape, BF16),
                    pltpu.SemaphoreType.DMA((n_pairs,)), pltpu.SemaphoreType.DMA((n_pairs,))]
    outs, sent = _call(
        body, name=name, grid=(n_tiles, n_out), args=(a, b), in_specs=[a_spec, b_spec],
        out_specs=[pl.BlockSpec(out_shape, lambda i, j: (0,) * len(out_shape))],
        out_shape=[jax.ShapeDtypeStruct(out_shape, BF16)],
        scratch_shapes=scratch,
        params=_params("arbitrary", "arbitrary"), exchange=exchange)
    return (outs[0], sent) if exchange is not None else outs[0]


def _ffn_weight_grads(h, dgate, dup, act, df, tm, tag, stream=False):
    t, d = h.shape
    ns, _, fs = dgate.shape
    nt = t // tm
    tok = pl.BlockSpec((tm, d), lambda i, j: (i, 0))
    hid = pl.BlockSpec((None, tm, fs), lambda i, j: (j, i, 0))
    if not stream:
        gwg = _mm_tn(h, dgate, tok, hid, (ns, d, fs), nt, f"grad_wg{tag}", keep_transposed=True)
        gwu = _mm_tn(h, dup, tok, hid, (ns, d, fs), nt, f"grad_wu{tag}", keep_transposed=True)
        gwd = _mm_tn(act, df, hid, tok, (ns, fs, d), nt, f"grad_wd{tag}")
        return gwg, gwu, gwd
    gwg = _mm_tn(h, dgate, tok, hid, (ns, d, fs), nt, f"grad_wg{tag}", keep_transposed=True, pair_reduce=True)
    gwu, sent_g = _mm_tn(h, dup, tok, hid, (ns, d, fs), nt, f"grad_wu{tag}",
                         _Exchange([gwg], True, chips=[True]), keep_transposed=True, pair_reduce=True)
    gwd, sent_u = _mm_tn(act, df, hid, tok, (ns, fs, d), nt, f"grad_wd{tag}",
                         _Exchange([gwu], True, chips=[True]), pair_reduce=True)
    return sent_g[0], sent_u[0], gwd


def _stage_shape(rows, cols):
    return pltpu.VMEM((cols // LANES, rows, LANES), F32)


def _stage(value, stage_ref):
    for k in range(stage_ref.shape[0]):
        stage_ref[k] = value[:, k * LANES:(k + 1) * LANES]


def _to_residue_rows(stage_ref, dst_ref, dil):
    rows = stage_ref.shape[1] // dil
    for r in range(dil):
        for k in range(stage_ref.shape[0]):
            dst_ref[r, :, k * LANES:(k + 1) * LANES] = (
                stage_ref.at[k][pl.ds(r, rows, stride=dil), :].astype(dst_ref.dtype))


def _from_residue_rows(src_ref, stage_ref, dil):
    rows = stage_ref.shape[1] // dil
    chunks = range(stage_ref.shape[0])
    for r in range(dil):
        for k in chunks:
            stage_ref.at[k][pl.ds(r, rows, stride=dil), :] = src_ref[r, :, k * LANES:(k + 1) * LANES].astype(F32)
    return jnp.concatenate([stage_ref[k] for k in chunks], axis=1)


def _residue_shape(nb, seq, width, dil, dtype):
    return jax.ShapeDtypeStruct((nb, dil, seq // dil, width), dtype)


def _residue_spec(tm, tpb, cols, dil, col_block):
    return pl.BlockSpec((None, dil, tm // dil, cols),
                        lambda i, *rest: (i // tpb, 0, i % tpb, col_block(i, *rest)))


def _qkv_fwd(x, mod, g, win, tm):
    t, d = x.shape
    ns, _, cs = win.shape
    nt = t // tm
    nb = mod.shape[0]
    tpb = nt // nb
    seq = t // nb
    half = ns // 2
    n_res = len(DILATIONS) - 1

    def body(x_ref, mod_ref, g_ref, w_ref, sb_ref, dil_ref, *rest):
        res_refs, h_ref, sc = rest[:n_res], rest[n_res], rest[n_res + 1]
        j = pl.program_id(1)

        @pl.when(j == 0)
        def _():
            h_ref[...] = _modulate(x_ref[...], g_ref[...], mod_ref, 1).astype(BF16)

        res = _dot(h_ref[...], w_ref[...])

        @pl.when(j < half)
        def _():
            sb_ref[...] = res.astype(BF16)

        @pl.when(j >= half)
        def _():
            dil_ref[...] = res.astype(BF16)
            _stage(res, sc)
            for ref, dil in zip(res_refs, DILATIONS[1:]):
                _to_residue_rows(sc, ref, dil)

    def dil_col(i, j):
        return jnp.maximum(j - half, 0)

    tok = pl.BlockSpec((tm, d), lambda i, j: (i, 0))
    wide = jax.ShapeDtypeStruct((t, half * cs), BF16)
    outs = pl.pallas_call(
        body, name="qkv_fwd", grid=(nt, ns),
        in_specs=[tok,
                  pl.BlockSpec((None, N_MOD, d), lambda i, j: (i // tpb, 0, 0)),
                  pl.BlockSpec((1, d), lambda i, j: (0, 0)),
                  pl.BlockSpec((None, d, cs), lambda i, j: (j, 0, 0))],
        out_specs=[pl.BlockSpec((tm, cs), lambda i, j: (i, jnp.minimum(j, half - 1))),
                   pl.BlockSpec((tm, cs), lambda i, j: (i, dil_col(i, j)))]
        + [_residue_spec(tm, tpb, cs, dil, dil_col) for dil in DILATIONS[1:]] + [tok],
        out_shape=[wide, wide] + [_residue_shape(nb, seq, half * cs, dil, BF16) for dil in DILATIONS[1:]]
        + [jax.ShapeDtypeStruct((t, d), BF16)],
        scratch_shapes=[_stage_shape(tm, cs)],
        compiler_params=_params("arbitrary", "arbitrary"),
    )(x, mod, g, win)
    qkv_dil = [outs[1]] + [a.reshape(t, half * cs) for a in outs[2:2 + n_res]]
    return outs[0], qkv_dil, outs[-1]


def _qkv_bwd(dqkv, dxo, x, mod, g, win, tm, exchange=None):
    t, d = x.shape
    ns, _, cs = win.shape
    nt = t // tm
    nb = mod.shape[0]
    tpb = nt // nb

    def body(dq_ref, dxo_ref, x_ref, mod_ref, g_ref, w_ref, dx_ref, dmod_ref, dg_ref, acc):
        i, j = pl.program_id(0), pl.program_id(1)

        @pl.when(j == 0)
        def _():
            acc[...] = jnp.zeros_like(acc)

        acc[...] += _dot_nt(dq_ref[...], w_ref[...])

        @pl.when(j == ns - 1)
        def _():
            dx, dshift, dscale, dg = _modulate_bwd(acc[...], x_ref[...], g_ref[...], mod_ref, 1)
            dx_ref[...] = dxo_ref[...] + dx

            @pl.when(i % tpb == 0)
            def _():
                dmod_ref[...] = jnp.zeros_like(dmod_ref)

            @pl.when(i == 0)
            def _():
                dg_ref[...] = jnp.zeros_like(dg_ref)

            dmod_ref[0:1, :] += dshift
            dmod_ref[1:2, :] += dscale
            dg_ref[0:1, :] += dg

    tok = pl.BlockSpec((tm, d), lambda i, j: (i, 0))
    return _call(
        body, name="qkv_bwd", grid=(nt, ns), args=(dqkv, dxo, x, mod, g, win),
        in_specs=[pl.BlockSpec((tm, cs), lambda i, j: (i, j)), tok, tok,
                  pl.BlockSpec((None, N_MOD, d), lambda i, j: (i // tpb, 0, 0)),
                  pl.BlockSpec((1, d), lambda i, j: (0, 0)),
                  pl.BlockSpec((None, d, cs), lambda i, j: (j, 0, 0))],
        out_specs=[tok,
                   pl.BlockSpec((None, 8, d), lambda i, j: (i // tpb, 0, 0)),
                   pl.BlockSpec((8, d), lambda i, j: (0, 0))],
        out_shape=[jax.ShapeDtypeStruct((t, d), F32),
                   jax.ShapeDtypeStruct((nb, 8, d), F32), jax.ShapeDtypeStruct((8, d), F32)],
        scratch_shapes=[pltpu.VMEM((tm, d), F32)],
        params=_params("arbitrary", "arbitrary"), exchange=exchange)


def _heads(a):
    return [a[:, h * HEAD_DIM:(h + 1) * HEAD_DIM] for h in range(a.shape[1] // HEAD_DIM)]


def _own_lanes():
    lane = lax.broadcasted_iota(jnp.int32, (1, LANES), 1)
    return [lane < HEAD_DIM, lane >= HEAD_DIM]


def _pair_tiles(a):
    return [a[:, (h // 2) * LANES:(h // 2 + 1) * LANES] for h in range(a.shape[1] // HEAD_DIM)]


def _own_tiles(a, own):
    return [jnp.where(own[h % 2], tile, jnp.zeros_like(tile)) for h, tile in enumerate(_pair_tiles(a))]


def _merge_tiles(per_head, own):
    return jnp.concatenate([jnp.where(own[0], per_head[h], per_head[h + 1])
                            for h in range(0, len(per_head), 2)], axis=1)


def _scaled(q):
    return (q.astype(F32) * (HEAD_DIM ** -0.5)).astype(BF16)


def _sb_logits(qh, kh, tri, causal):
    zs = [_dot_nt(q, k) for q, k in zip(qh, kh)]
    es = [jnp.exp(-jnp.abs(z)) for z in zs]
    log_nots = [-(jnp.maximum(z, 0.0) + jnp.log(1.0 + e)) for z, e in zip(zs, es)]
    if causal is not None:
        log_nots = [jnp.where(causal, ln, 0.0) for ln in log_nots]
    return zs, es, [_split_dot(ln, tri) for ln in log_nots]


def _sb_masks():
    rows = lax.broadcasted_iota(jnp.int32, (SB_BLOCK, SB_BLOCK), 0)
    cols = lax.broadcasted_iota(jnp.int32, (SB_BLOCK, SB_BLOCK), 1)
    return (rows >= cols).astype(BF16), (rows <= cols).astype(BF16), cols < rows


def _sb_fwd(qkv, nb, seq, exchange=None):
    t = qkv.shape[0]
    n_pairs = (qkv.shape[1] // 3) // SB_WIDTH
    tb = SB_BLOCK
    n_blk = seq // tb

    def body(q_ref, k_ref, v_ref, o_ref, c_ref):
        tri, _, causal = _sb_masks()
        own = _own_lanes()

        def key_block(qh, kj, carry, mask):
            ks = pl.multiple_of(kj * tb, tb)
            kh, vh = _pair_tiles(k_ref[pl.ds(ks, tb), :]), _pair_tiles(v_ref[pl.ds(ks, tb), :])
            zs, _, suffixes = _sb_logits(qh, kh, tri, mask)
            ws = [jnp.exp(z + suffix + cr[1]) for z, suffix, cr in zip(zs, suffixes, carry)]
            if mask is not None:
                ws = [jnp.where(mask, w, 0.0) for w in ws]
            pv = [_dot(w.astype(BF16), v) for w, v in zip(ws, vh)]
            return tuple((cr[0] + p, cr[1] + suffix[:, 0:1]) for cr, p, suffix in zip(carry, pv, suffixes))

        def query_block(qi, _):
            qs = pl.multiple_of(qi * tb, tb)
            qh = _own_tiles(_scaled(q_ref[pl.ds(qs, tb), :]), own)
            zero = (jnp.zeros((tb, LANES), F32), jnp.zeros((tb, 1), F32))
            carry = key_block(qh, qi, (zero,) * SB_HEADS, causal)
            carry = lax.fori_loop(0, qi, lambda it, cr: key_block(qh, qi - 1 - it, cr, None), carry)
            o_ref[pl.ds(qs, tb), :] = _merge_tiles([cr[0] for cr in carry], own)
            c_ref[pl.ds(qs, tb), :] = _merge_tiles([jnp.broadcast_to(cr[1], (tb, LANES)) for cr in carry], own)
            return 0

        lax.fori_loop(0, n_blk, query_block, 0)

    def spec(offset):
        return pl.BlockSpec((seq, SB_WIDTH), lambda b, p: (b, offset + p))

    out = jax.ShapeDtypeStruct((t, n_pairs * SB_WIDTH), F32)
    return _call(
        body, name="sb_fwd", grid=(nb, n_pairs), args=(qkv, qkv, qkv),
        in_specs=[spec(0), spec(n_pairs), spec(2 * n_pairs)],
        out_specs=[spec(0), spec(0)], out_shape=[out, out],
        params=_params("arbitrary", "arbitrary"), exchange=exchange)


def _sb_bwd(qkv, do, csum, nb, seq, exchange=None):
    t = qkv.shape[0]
    n_pairs = (qkv.shape[1] // 3) // SB_WIDTH
    tb = SB_BLOCK
    n_blk = seq // tb
    scale = HEAD_DIM ** -0.5

    def body(q_ref, k_ref, v_ref, do_ref, c_ref, dq_ref, dk_ref, dv_ref, dkt_acc, dvt_acc):
        tri, tri_prefix, causal = _sb_masks()
        own = _own_lanes()
        dkt_acc[...] = jnp.zeros_like(dkt_acc)
        dvt_acc[...] = jnp.zeros_like(dvt_acc)

        def key_block(qh, qth, doh, doth, ch, kj, carry, mask):
            ks = pl.multiple_of(kj * tb, tb)
            kh, vh = _pair_tiles(k_ref[pl.ds(ks, tb), :]), _pair_tiles(v_ref[pl.ds(ks, tb), :])
            heads = range(SB_HEADS)
            zs, es, suffixes = _sb_logits(qh, kh, tri, mask)
            dws = [_dot_nt(doh[h], vh[h]) for h in heads]
            lefts = [carry[h][1] + suffixes[h][:, 0:1] for h in heads]
            ws = [jnp.exp(zs[h] + suffixes[h] + (ch[h] - lefts[h])) for h in heads]
            if mask is not None:
                ws = [jnp.where(mask, w, 0.0) for w in ws]
            dlws = [ws[h] * dws[h] for h in heads]
            dprefixes = [_split_dot(dlw, tri_prefix) for dlw in dlws]
            dvts = [_dot(doth[h], ws[h].astype(BF16)) for h in heads]
            dzbs = []
            for h in heads:
                sig = jnp.where(zs[h] >= 0.0, 1.0, es[h]) * pl.reciprocal(1.0 + es[h], approx=True)
                dz = dlws[h] - sig * (carry[h][2] + dprefixes[h])
                if mask is not None:
                    dz = jnp.where(mask, dz, 0.0)
                dzbs.append(dz.astype(BF16))
            dkts = [_dot(qth[h], dzbs[h]) for h in heads]
            dqs = [_dot(dzbs[h], kh[h]) for h in heads]
            dkt_acc[:, pl.ds(ks, tb)] += jnp.concatenate([dkts[h] + dkts[h + 1] for h in heads[::2]], axis=0)
            dvt_acc[:, pl.ds(ks, tb)] += jnp.concatenate([dvts[h] + dvts[h + 1] for h in heads[::2]], axis=0)
            return tuple((carry[h][0] + dqs[h], lefts[h], carry[h][2] + dprefixes[h][:, tb - 1:tb])
                         for h in heads)

        def query_block(qi, _):
            qs = pl.multiple_of(qi * tb, tb)
            qh = _own_tiles(_scaled(q_ref[pl.ds(qs, tb), :]), own)
            doh = _own_tiles(do_ref[pl.ds(qs, tb), :], own)
            qth = [a.astype(F32).T.astype(BF16) for a in qh]
            doth = [a.T.astype(BF16) for a in doh]
            doh = [a.astype(BF16) for a in doh]
            cv = c_ref[pl.ds(qs, tb), :]
            ch = [cv[:, h * HEAD_DIM:h * HEAD_DIM + 1] for h in range(SB_HEADS)]
            zero = (jnp.zeros((tb, LANES), F32), jnp.zeros((tb, 1), F32), jnp.zeros((tb, 1), F32))
            carry = lax.fori_loop(
                0, qi, lambda kj, cr: key_block(qh, qth, doh, doth, ch, kj, cr, None), (zero,) * SB_HEADS)
            carry = key_block(qh, qth, doh, doth, ch, qi, carry, causal)
            dq = _merge_tiles([cr[0] for cr in carry], own) * scale
            dq_ref[pl.ds(qs, tb), :] = dq.astype(BF16)
            return 0

        lax.fori_loop(0, n_blk, query_block, 0)
        dk_ref[...] = dkt_acc[...].T.astype(BF16)
        dv_ref[...] = dvt_acc[...].T.astype(BF16)

    def spec(offset):
        return pl.BlockSpec((seq, SB_WIDTH), lambda b, p: (b, offset + p))

    out = jax.ShapeDtypeStruct((t, n_pairs * SB_WIDTH), BF16)
    return _call(
        body, name="sb_bwd", grid=(nb, n_pairs), args=(qkv, qkv, qkv, do, csum),
        in_specs=[spec(0), spec(n_pairs), spec(2 * n_pairs), spec(0), spec(0)],
        out_specs=[spec(0), spec(0), spec(0)],
        out_shape=[out, out, out],
        scratch_shapes=[pltpu.VMEM((SB_WIDTH, seq), F32), pltpu.VMEM((SB_WIDTH, seq), F32)],
        params=_params("arbitrary", "arbitrary"), exchange=exchange)


def _dil_block_scores(qh, kph, kch, bias_ref, has_prev, band_prev, band_cur):
    scale = HEAD_DIM ** -0.5
    heads = range(len(qh))
    no_prev = jnp.where(has_prev, 0.0, NEG_INF)
    zps = [_dot_nt(qh[h], kph[h]) for h in heads]
    zcs = [_dot_nt(qh[h], kch[h]) for h in heads]
    zps = [jnp.where(band_prev, zps[h] * scale + bias_ref[h, :, 0:DIL_BLOCK], NEG_INF) + no_prev for h in heads]
    zcs = [jnp.where(band_cur, zcs[h] * scale + bias_ref[h, :, DIL_BLOCK:2 * DIL_BLOCK], NEG_INF) for h in heads]
    return zps, zcs


def _dil_bands():
    rows = lax.broadcasted_iota(jnp.int32, (DIL_BLOCK, DIL_BLOCK), 0)
    cols = lax.broadcasted_iota(jnp.int32, (DIL_BLOCK, DIL_BLOCK), 1)
    return cols >= rows, cols <= rows


def _dil_fwd(qkv, bias, nb, seq, dil, exchange=None):
    t, width = qkv.shape
    n_pairs = (width // 3) // DIL_WIDTH
    bq = DIL_BLOCK
    n_blk = seq // bq
    per_seq = n_blk // dil
    heads = range(DIL_HEADS)

    def body(q_ref, k_ref, v_ref, bias_ref, o_ref, lse_ref):
        band_prev, band_cur = _dil_bands()
        own = _own_lanes()

        def block(n, _):
            has_prev = (n & (per_seq - 1)) != 0
            qs = pl.multiple_of(n * bq, bq)
            ps = pl.multiple_of(jnp.maximum(n - 1, 0) * bq, bq)
            qh = _own_tiles(q_ref[pl.ds(qs, bq), :], own)
            kp, kc = _pair_tiles(k_ref[pl.ds(ps, bq), :]), _pair_tiles(k_ref[pl.ds(qs, bq), :])
            vp, vc = _pair_tiles(v_ref[pl.ds(ps, bq), :]), _pair_tiles(v_ref[pl.ds(qs, bq), :])
            zps, zcs = _dil_block_scores(qh, kp, kc, bias_ref, has_prev, band_prev, band_cur)
            ms = [jnp.maximum(jnp.max(zps[h], axis=1, keepdims=True), jnp.max(zcs[h], axis=1, keepdims=True))
                  for h in heads]
            eps = [jnp.exp(zps[h] - ms[h]) for h in heads]
            ecs = [jnp.exp(zcs[h] - ms[h]) for h in heads]
            pvs = [_dot(eps[h].astype(BF16), vp[h]) + _dot(ecs[h].astype(BF16), vc[h]) for h in heads]
            dens = [jnp.sum(eps[h], axis=1, keepdims=True) + jnp.sum(ecs[h], axis=1, keepdims=True) for h in heads]
            o_ref[pl.ds(qs, bq), :] = _merge_tiles([pvs[h] / dens[h] for h in heads], own)
            lse_ref[pl.ds(qs, bq), :] = _merge_tiles(
                [jnp.broadcast_to(ms[h] + jnp.log(dens[h]), (bq, LANES)) for h in heads], own)
            return 0

        lax.fori_loop(0, n_blk, block, 0)

    def spec(offset):
        return pl.BlockSpec((seq, DIL_WIDTH), lambda b, p: (b, offset + p))

    out = jax.ShapeDtypeStruct((t, n_pairs * DIL_WIDTH), F32)
    return _call(
        body, name=f"dil_fwd{dil}", grid=(nb, n_pairs), args=(qkv, qkv, qkv, bias),
        in_specs=[spec(0), spec(n_pairs), spec(2 * n_pairs),
                  pl.BlockSpec((DIL_HEADS, bq, 2 * bq), lambda b, p: (p, 0, 0))],
        out_specs=[spec(0), spec(0)], out_shape=[out, out],
        params=_params("arbitrary", "arbitrary"), exchange=exchange)


def _dil_bwd(qkv, bias, do, lse, delta, nb, seq, dil):
    t, width = qkv.shape
    n_pairs = (width // 3) // DIL_WIDTH
    bq = DIL_BLOCK
    n_blk = seq // bq
    per_seq = n_blk // dil
    scale = HEAD_DIM ** -0.5
    heads = range(DIL_HEADS)

    def body(q_ref, k_ref, v_ref, bias_ref, do_ref, lse_ref, dl_ref, dq_ref, dk_ref, dv_ref, db_ref,
             dk_acc, dv_acc):
        band_prev, band_cur = _dil_bands()
        own = _own_lanes()
        dk_acc[...] = jnp.zeros_like(dk_acc)
        dv_acc[...] = jnp.zeros_like(dv_acc)

        @pl.when(pl.program_id(1) == 0)
        def _():
            db_ref[...] = jnp.zeros_like(db_ref)

        def block(n, _):
            has_prev = (n & (per_seq - 1)) != 0
            qs = pl.multiple_of(n * bq, bq)
            ps = pl.multiple_of(jnp.maximum(n - 1, 0) * bq, bq)
            qh = _own_tiles(q_ref[pl.ds(qs, bq), :], own)
            kp, kc = _pair_tiles(k_ref[pl.ds(ps, bq), :]), _pair_tiles(k_ref[pl.ds(qs, bq), :])
            vp, vc = _pair_tiles(v_ref[pl.ds(ps, bq), :]), _pair_tiles(v_ref[pl.ds(qs, bq), :])
            doh = _own_tiles(do_ref[pl.ds(qs, bq), :].astype(BF16), own)
            lse_v, dl_v = lse_ref[pl.ds(qs, bq), :], dl_ref[pl.ds(qs, bq), :]
            zps, zcs = _dil_block_scores(qh, kp, kc, bias_ref, has_prev, band_prev, band_cur)
            dpp = [_dot_nt(doh[h], vp[h]) for h in heads]
            dpc = [_dot_nt(doh[h], vc[h]) for h in heads]
            lse_h = [lse_v[:, h * HEAD_DIM:h * HEAD_DIM + 1] for h in heads]
            dl_h = [dl_v[:, h * HEAD_DIM:h * HEAD_DIM + 1] for h in heads]
            pps = [jnp.exp(zps[h] - lse_h[h]) for h in heads]
            pcs = [jnp.exp(zcs[h] - lse_h[h]) for h in heads]
            dvp = [_dot_tn(pps[h].astype(BF16), doh[h]) for h in heads]
            dvc = [_dot_tn(pcs[h].astype(BF16), doh[h]) for h in heads]
            dzps = [pps[h] * (dpp[h] - dl_h[h]) for h in heads]
            dzcs = [pcs[h] * (dpc[h] - dl_h[h]) for h in heads]
            dzp_b = [(dzps[h] * scale).astype(BF16) for h in heads]
            dzc_b = [(dzcs[h] * scale).astype(BF16) for h in heads]
            dqs = [_dot(dzp_b[h], kp[h]) + _dot(dzc_b[h], kc[h]) for h in heads]
            dkp = [_dot_tn(dzp_b[h], qh[h]) for h in heads]
            dkc = [_dot_tn(dzc_b[h], qh[h]) for h in heads]
            for h in heads:
                db_ref[h, :, 0:bq] += dzps[h]
                db_ref[h, :, bq:2 * bq] += dzcs[h]
            def pair_sums(per_head):
                return jnp.concatenate([per_head[h] + per_head[h + 1] for h in heads[::2]], axis=1)

            dq_ref[pl.ds(qs, bq), :] = _merge_tiles(dqs, own).astype(BF16)
            dk_acc[pl.ds(ps, bq), :] += pair_sums(dkp)
            dk_acc[pl.ds(qs, bq), :] += pair_sums(dkc)
            dv_acc[pl.ds(ps, bq), :] += pair_sums(dvp)
            dv_acc[pl.ds(qs, bq), :] += pair_sums(dvc)
            return 0

        lax.fori_loop(0, n_blk, block, 0)
        dk_ref[...] = dk_acc[...].astype(BF16)
        dv_ref[...] = dv_acc[...].astype(BF16)

    def spec(offset):
        return pl.BlockSpec((seq, DIL_WIDTH), lambda p, b: (b, offset + p))

    bias_spec = pl.BlockSpec((DIL_HEADS, bq, 2 * bq), lambda p, b: (p, 0, 0))
    out = jax.ShapeDtypeStruct((t, n_pairs * DIL_WIDTH), BF16)
    return pl.pallas_call(
        body, name=f"dil_bwd{dil}", grid=(n_pairs, nb),
        in_specs=[spec(0), spec(n_pairs), spec(2 * n_pairs), bias_spec, spec(0), spec(0), spec(0)],
        out_specs=[spec(0), spec(0), spec(0), bias_spec],
        out_shape=[out, out, out, jax.ShapeDtypeStruct(bias.shape, F32)],
        scratch_shapes=[pltpu.VMEM((seq, DIL_WIDTH), F32), pltpu.VMEM((seq, DIL_WIDTH), F32)],
        compiler_params=_params("arbitrary", "arbitrary"),
    )(qkv, qkv, qkv, bias, do, lse, delta)


def _head_blocks(width):
    rows = lax.broadcasted_iota(jnp.int32, (width, width), 0) // HEAD_DIM
    cols = lax.broadcasted_iota(jnp.int32, (width, width), 1) // HEAD_DIM
    return (rows == cols).astype(BF16)


def _head_mean(v, gmat):
    return _split_dot(v, gmat) * (1.0 / HEAD_DIM)


def _residue_views(arrays, nb, seq):
    return [a if dil == 1 else a.reshape(nb, dil, seq // dil, a.shape[1]) for a, dil in zip(arrays, DILATIONS)]


def _mix_out_fwd(osb, ocs, lses, gsb, gdil, wout, x, mod, tm):
    t, d = x.shape
    ds = osb.shape[1]
    nt = t // tm
    nb = mod.shape[0]
    tpb = nt // nb
    seq = t // nb
    n_cfg = len(DILATIONS)

    def body(osb_ref, *refs):
        oc_refs, lse_refs = refs[:n_cfg], refs[n_cfg:2 * n_cfg]
        gsb_ref, gdil_ref, w_ref, x_ref, mod_ref = refs[2 * n_cfg:2 * n_cfg + 5]
        xo_ref, on_ref, m_ref, odil_ref = refs[2 * n_cfg + 5:2 * n_cfg + 9]
        ld_refs = refs[2 * n_cfg + 9:3 * n_cfg + 9]
        stages, sc = refs[3 * n_cfg + 9:]
        ocv, lsev = [oc_refs[0][...]], [lse_refs[0][...]]
        for i, dil in enumerate(DILATIONS[1:]):
            ocv.append(_from_residue_rows(oc_refs[i + 1], stages.at[2 * i], dil))
            lsev.append(_from_residue_rows(lse_refs[i + 1], stages.at[2 * i + 1], dil))
        top = functools.reduce(jnp.maximum, lsev)
        total = top + jnp.log(sum(jnp.exp(l - top) for l in lsev))
        odil = sum(jnp.exp(l - total) * o for o, l in zip(ocv, lsev))
        odil_ref[...] = odil
        ld_refs[0][...] = total
        _stage(total, sc)
        for ref, dil in zip(ld_refs[1:], DILATIONS[1:]):
            _to_residue_rows(sc, ref, dil)
        gm = _head_blocks(ds)
        parts = []
        for o, g_ref in ((osb_ref[...], gsb_ref), (odil, gdil_ref)):
            parts.append(o * lax.rsqrt(_head_mean(o * o, gm) + EPS) * g_ref[...])
        on = jnp.concatenate(parts, axis=1).astype(BF16)
        on_ref[...] = on
        m = _dot(on, w_ref[...])
        m_ref[...] = m
        xo_ref[...] = x_ref[...] + mod_ref[5:6, :] * m

    tok = pl.BlockSpec((tm, d), lambda i: (i, 0))
    hd = pl.BlockSpec((tm, ds), lambda i: (i, 0))
    res = [hd] + [_residue_spec(tm, tpb, ds, dil, lambda i: 0) for dil in DILATIONS[1:]]
    res_shape = [jax.ShapeDtypeStruct((t, ds), F32)] + [_residue_shape(nb, seq, ds, dil, F32) for dil in DILATIONS[1:]]
    gain = pl.BlockSpec((1, ds), lambda i: (0, 0))
    outs = pl.pallas_call(
        body, name="mix_out_fwd", grid=(nt,),
        in_specs=[hd] + res + res + [gain, gain,
                  pl.BlockSpec(wout.shape, lambda i: (0, 0)),
                  tok, pl.BlockSpec((None, N_MOD, d), lambda i: (i // tpb, 0, 0))],
        out_specs=[tok, pl.BlockSpec((tm, 2 * ds), lambda i: (i, 0)), tok, hd] + res,
        out_shape=[jax.ShapeDtypeStruct((t, d), F32), jax.ShapeDtypeStruct((t, 2 * ds), BF16),
                   jax.ShapeDtypeStruct((t, d), F32), jax.ShapeDtypeStruct((t, ds), F32)] + res_shape,
        scratch_shapes=[pltpu.VMEM((2 * (n_cfg - 1), ds // LANES, tm, LANES), F32), _stage_shape(tm, ds)],
        compiler_params=_params("arbitrary"),
    )(osb, *_residue_views(ocs, nb, seq), *_residue_views(lses, nb, seq), gsb, gdil, wout, x, mod)
    return outs[0], outs[1], outs[2], outs[3], [a.reshape(t, ds) for a in outs[4:]]


def _mix_out_bwd(dxo, m, mod, wout, osb, odil, gsb, gdil, tm):
    t, d = dxo.shape
    ds = osb.shape[1]
    nt = t // tm
    nb = mod.shape[0]
    tpb = nt // nb
    seq = t // nb
    n_cfg = len(DILATIONS)

    def body(dxo_ref, m_ref, mod_ref, w_ref, osb_ref, odil_ref, gsb_ref, gdil_ref,
             dm_ref, dosb_ref, *rest):
        do_refs, dl_refs = rest[:n_cfg], rest[n_cfg:2 * n_cfg]
        dmod_ref, dg_ref, sc = rest[2 * n_cfg:]
        dodil_ref, dldil_ref = do_refs[0], dl_refs[0]
        i = pl.program_id(0)
        dxo_v = dxo_ref[...]
        dm = (mod_ref[5:6, :] * dxo_v).astype(BF16)
        dm_ref[...] = dm
        dgt = jnp.sum(m_ref[...] * dxo_v, axis=0, keepdims=True)
        don = _dot_nt(dm, w_ref[...])
        gm = _head_blocks(ds)

        @pl.when(i % tpb == 0)
        def _():
            dmod_ref[...] = jnp.zeros_like(dmod_ref)

        @pl.when(i == 0)
        def _():
            dg_ref[...] = jnp.zeros_like(dg_ref)

        dmod_ref[2:3, :] += dgt
        groups = ((osb_ref, gsb_ref, dosb_ref), (odil_ref, gdil_ref, dodil_ref))
        for k, (o_ref, g_ref, do_ref) in enumerate(groups):
            o = o_ref[...]
            dn_out = don[:, k * ds:(k + 1) * ds]
            r = lax.rsqrt(_head_mean(o * o, gm) + EPS)
            n = o * r
            dg_ref[0:1, k * ds:(k + 1) * ds] += jnp.sum(dn_out * n, axis=0, keepdims=True)
            dn = dn_out * g_ref[...]
            do = r * (dn - n * _head_mean(dn * n, gm))
            do_ref[...] = do
            if k == 1:
                delta = _head_mean(do * o, gm) * float(HEAD_DIM)
                dldil_ref[...] = delta
                for value, refs in ((do, do_refs), (delta, dl_refs)):
                    _stage(value, sc)
                    for ref, dil in zip(refs[1:], DILATIONS[1:]):
                        _to_residue_rows(sc, ref, dil)

    tok = pl.BlockSpec((tm, d), lambda i: (i, 0))
    hd = pl.BlockSpec((tm, ds), lambda i: (i, 0))
    res = [hd] + [_residue_spec(tm, tpb, ds, dil, lambda i: 0) for dil in DILATIONS[1:]]
    res_shape = [jax.ShapeDtypeStruct((t, ds), F32)] + [_residue_shape(nb, seq, ds, dil, F32) for dil in DILATIONS[1:]]
    gain = pl.BlockSpec((1, ds), lambda i: (0, 0))
    outs = pl.pallas_call(
        body, name="mix_out_bwd", grid=(nt,),
        in_specs=[tok, tok, pl.BlockSpec((None, N_MOD, d), lambda i: (i // tpb, 0, 0)),
                  pl.BlockSpec(wout.shape, lambda i: (0, 0)), hd, hd, gain, gain],
        out_specs=[tok, hd] + res + res
        + [pl.BlockSpec((None, 8, d), lambda i: (i // tpb, 0, 0)), pl.BlockSpec((8, 2 * ds), lambda i: (0, 0))],
        out_shape=[jax.ShapeDtypeStruct((t, d), BF16), jax.ShapeDtypeStruct((t, ds), F32)] + res_shape + res_shape
        + [jax.ShapeDtypeStruct((nb, 8, d), F32), jax.ShapeDtypeStruct((8, 2 * ds), F32)],
        scratch_shapes=[_stage_shape(tm, ds)],
        compiler_params=_params("arbitrary"),
    )(dxo, m, mod, wout, osb, odil, gsb, gdil)
    flat = [a.reshape(t, ds) for a in outs[2:2 + 2 * n_cfg]]
    return outs[0], outs[1], flat[:n_cfg], flat[n_cfg:], outs[-2], outs[-1]


def _merge_dqkv(sb_parts, dil_parts, nb, tm):
    t, ds = sb_parts[0].shape
    nt = t // tm
    tpb = nt // nb
    seq = t // nb
    n_cfg = len(DILATIONS)

    def body(*refs):
        sb_refs, dil_refs = refs[:3], refs[3:3 + 3 * n_cfg]
        o_ref, sc = refs[3 + 3 * n_cfg:]
        for k in range(3):
            o_ref[:, k * ds:(k + 1) * ds] = sb_refs[k][...]
            total = dil_refs[k * n_cfg][...].astype(F32)
            for i, dil in enumerate(DILATIONS[1:]):
                total = total + _from_residue_rows(dil_refs[k * n_cfg + i + 1], sc, dil)
            o_ref[:, (3 + k) * ds:(4 + k) * ds] = total.astype(BF16)

    hd = pl.BlockSpec((tm, ds), lambda i: (i, 0))
    res = [hd] + [_residue_spec(tm, tpb, ds, dil, lambda i: 0) for dil in DILATIONS[1:]]
    views = [v for parts in dil_parts for v in _residue_views(parts, nb, seq)]
    return pl.pallas_call(
        body, name="merge_dqkv", grid=(nt,),
        in_specs=[hd] * 3 + res * 3,
        out_specs=pl.BlockSpec((tm, 6 * ds), lambda i: (i, 0)),
        out_shape=jax.ShapeDtypeStruct((t, 6 * ds), BF16),
        scratch_shapes=[_stage_shape(tm, ds)],
        compiler_params=_params("arbitrary"),
    )(*sb_parts, *views)


def _loss_head(x, target, g, tm):
    t, d = x.shape

    def body(x_ref, t_ref, g_ref, dx_ref, acc_ref):
        @pl.when(pl.program_id(0) == 0)
        def _():
            acc_ref[...] = jnp.zeros_like(acc_ref)

        n, r = _norm(x_ref[...])
        gv = g_ref[...]
        err = n * gv - t_ref[...]
        dy = err * (1.0 / d)
        acc_ref[0:1, :] += jnp.sum(err * err, axis=0, keepdims=True)
        acc_ref[1:2, :] += jnp.sum(dy * n, axis=0, keepdims=True)
        dn = dy * gv
        dx_ref[...] = r * (dn - n * jnp.mean(dn * n, axis=-1, keepdims=True))

    tok = pl.BlockSpec((tm, d), lambda i: (i, 0))
    return pl.pallas_call(
        body, name="loss_head", grid=(t // tm,),
        in_specs=[tok, tok, pl.BlockSpec((1, d), lambda i: (0, 0))],
        out_specs=[tok, pl.BlockSpec((8, d), lambda i: (0, 0))],
        out_shape=[jax.ShapeDtypeStruct((t, d), F32), jax.ShapeDtypeStruct((8, d), F32)],
        compiler_params=_params("arbitrary"),
    )(x, target, g)


def _row_tile(rows):
    if rows <= 256:
        return rows
    for cand in range(256, 15, -16):
        if rows % cand == 0:
            return cand
    return rows


def _adamw(w, parts, m, v, name, transposed=False):
    rows, cols = w.shape
    n_parts = parts.shape[0]
    tr = _row_tile(rows)
    c1 = 1.0 / (1.0 - ADAM_B1 ** ADAM_STEP)
    c2 = 1.0 / (1.0 - ADAM_B2 ** ADAM_STEP)

    def body(w_ref, p_ref, m_ref, v_ref, g_ref, d_ref, nm_ref, nv_ref):
        g = p_ref[0].astype(F32)
        for i in range(1, n_parts):
            g = g + p_ref[i].astype(F32)
        wv, mv, vv = w_ref[...], m_ref[...], v_ref[...]
        if transposed:
            wv, mv, vv = wv.T, mv.T, vv.T
        nm = ADAM_B1 * mv + (1.0 - ADAM_B1) * g
        nv = ADAM_B2 * vv + (1.0 - ADAM_B2) * (g * g)
        g_ref[...] = g
        nm_ref[...] = nm
        nv_ref[...] = nv
        d_ref[...] = -ADAM_LR * ((nm * c1) / (jnp.sqrt(nv * c2) + ADAM_EPS) + ADAM_WD * wv)

    blk = pl.BlockSpec((tr, cols), lambda i: (i, 0))
    if transposed:
        oblk = pl.BlockSpec((cols, tr), lambda i: (0, i))
        pblk = pl.BlockSpec((n_parts, cols, tr), lambda i: (0, 0, i))
        out = jax.ShapeDtypeStruct((cols, rows), F32)
    else:
        oblk, pblk = blk, pl.BlockSpec((n_parts, tr, cols), lambda i: (0, i, 0))
        out = jax.ShapeDtypeStruct((rows, cols), F32)
    return pl.pallas_call(
        body, name=name, grid=(rows // tr,),
        in_specs=[blk, pblk, blk, blk],
        out_specs=[oblk, oblk, oblk, oblk], out_shape=[out, out, out, out],
        compiler_params=_params("arbitrary"),
    )(w, parts, m, v)


def _t5_bucket(n):
    max_exact = N_BUCKETS // 2
    nf = np.maximum(n, 1).astype(np.float32)
    large = max_exact + (np.log(nf / max_exact) / math.log(MAX_DISTANCE / max_exact)
                         * (N_BUCKETS - max_exact)).astype(np.int32)
    large = np.minimum(large, N_BUCKETS - 1)
    return np.where(n < max_exact, n, large).astype(np.int32)


def _bucket_onehot():
    table = np.zeros((len(DILATIONS), 2 * DIL_BLOCK + 1, N_BUCKETS), np.float32)
    for i, dil in enumerate(DILATIONS):
        buckets = _t5_bucket(np.arange(DIL_BLOCK + 1) * dil)
        for m in range(DIL_BLOCK + 1):
            table[i, m, buckets[DIL_BLOCK - m]] = 1.0
    return table


def _bias_blocks(rel_bias):
    row = jnp.einsum("cmn,nh->chm", _bucket_onehot(), rel_bias, precision=lax.Precision.HIGHEST)
    n_cfg, n_heads, width = row.shape
    tiled = jnp.tile(row, (1, 1, DIL_BLOCK))[..., :DIL_BLOCK * (width - 1)]
    return tiled.reshape(n_cfg, n_heads, DIL_BLOCK, width - 1)


def _bias_blocks_bwd(dblocks):
    n_cfg, n_heads = dblocks.shape[:2]
    width = 2 * DIL_BLOCK + 1
    flat = dblocks.reshape(n_cfg, n_heads, DIL_BLOCK * (width - 1))
    flat = jnp.pad(flat, ((0, 0), (0, 0), (0, DIL_BLOCK)))
    drow = jnp.sum(flat.reshape(n_cfg, n_heads, DIL_BLOCK, width), axis=2)
    return jnp.einsum("chm,cmn->nh", drow, _bucket_onehot(), precision=lax.Precision.HIGHEST)


def _pad_to(a, axis, size):
    pad = [(0, 0)] * a.ndim
    pad[axis] = (0, size - a.shape[axis])
    return jnp.pad(a, pad)


def _lane_pad(n):
    return -(-n // LANES) * LANES


def _local_step(x, target, mod, gains, weights, rel_bias, tm, distributed):
    nb, seq, d = x.shape
    t = nb * seq
    g_ffn1, g_mix, g_sb, g_dil, g_ffn2, g_final = gains
    wg1, wu1, wd1 = weights[:3]
    x0 = x.reshape(t, d)
    ds = g_sb.shape[1]
    bias = _bias_blocks(rel_bias)

    def beside(arrays, scatter):
        return _Exchange(arrays, scatter) if distributed else None

    tp, tg = min(PROJ_TILE, seq), min(GRAD_TILE, t)

    (x1, f1, gate1, up1), got = _ffn_fwd(x0, mod, g_ffn1, wg1, wu1, wd1, 0, tp, beside(weights[3:5], False))
    win, wout = got if distributed else weights[3:5]
    wout2 = wout.reshape(-1, d)
    qkv, qkvd, h2 = _qkv_fwd(x1, mod, g_mix, win, tp)
    (osb, csb), got = _sb_fwd(qkv, nb, seq, beside(weights[5:7], False))
    wg2, wu2 = got if distributed else weights[5:7]
    ocs, lses = [], []
    for i, dil in enumerate(DILATIONS):
        (oc, lse), got = _dil_fwd(qkvd[i], bias[i], nb, seq, dil, beside(weights[7:8], False) if i == 0 else None)
        if i == 0:
            wd2 = got[0] if distributed else weights[7]
        ocs.append(oc)
        lses.append(lse)
    x2, on, mix, odil, ldil = _mix_out_fwd(osb, ocs, lses, g_sb, g_dil, wout2, x1, mod, tm)
    (x3, f3, gate3, up3), _ = _ffn_fwd(x2, mod, g_ffn2, wg2, wu2, wd2, 2, tp)
    dx3, head = _loss_head(x3, target.reshape(t, d), g_final, tm)
    loss_sum = 0.5 * jnp.sum(head[0]) / d
    dg_final = head[1:2]

    (dx2, dgate3, dup3, act3, h3, df3, dmod3, dg_ffn2), _ = _ffn_bwd(
        dx3, x2, f3, mod, g_ffn2, gate3, up3, wg2, wu2, wd2, 2, tp)
    gwg2, gwu2, gwd2 = _ffn_weight_grads(h3, dgate3, dup3, act3, df3, tg, 2)

    dm, dosb, dodil, dldil, dmod2b, dg_heads = _mix_out_bwd(
        dx2, mix, mod, wout2, osb, odil, g_sb, g_dil, tm)
    n_out = wout.shape[0]
    gwout = _mm_tn(on, dm,
                   pl.BlockSpec((tg, wout.shape[1]), lambda i, j: (i, j)),
                   pl.BlockSpec((tg, d), lambda i, j: (i, 0)),
                   wout.shape, t // tg, "grad_wout")

    (dq_sb, dk_sb, dv_sb), parts_late = _sb_bwd(qkv, dosb, csb, nb, seq,
                                                beside([gwout, gwg2, gwu2, gwd2], True))
    dil_grads = [_dil_bwd(qkvd[i], bias[i], dodil[i], ldil[i], dldil[i], nb, seq, dil)
                 for i, dil in enumerate(DILATIONS)]
    dqkv = _merge_dqkv([dq_sb, dk_sb, dv_sb], [[g[k] for g in dil_grads] for k in range(3)], nb, tm)
    drel = _bias_blocks_bwd(jnp.stack([g[3] for g in dil_grads]))

    cs = win.shape[2]
    gwin = _mm_tn(h2, dqkv,
                  pl.BlockSpec((tg, d), lambda i, j: (i, 0)),
                  pl.BlockSpec((tg, cs), lambda i, j: (i, j)),
                  win.shape, t // tg, "grad_win")
    (dx1, dmod2a, dg_mix), parts_mid = _qkv_bwd(dqkv, dx2, x1, mod, g_mix, win, tp, beside([gwin], True))

    (dx0, dgate1, dup1, act1, h1, df1, dmod1, dg_ffn1), _ = _ffn_bwd(
        dx1, x0, f1, mod, g_ffn1, gate1, up1, wg1, wu1, wd1, 0, tp)
    gw1 = _ffn_weight_grads(h1, dgate1, dup1, act1, df1, tg, 0, stream=distributed)

    dmod = jnp.concatenate([dmod1[:, 0:3], dmod2a[:, 0:2], dmod2b[:, 2:3], dmod3[:, 0:3]], axis=1)
    wgrads = tuple(gw1) + (tuple(parts_mid + parts_late) if distributed else (gwin, gwout, gwg2, gwu2, gwd2))
    ggrads = (dg_ffn1[0:1], dg_mix[0:1], dg_heads[0:1], drel, dg_ffn2[0:1], dg_final)
    return loss_sum, dx0.reshape(nb, seq, d), wgrads, dmod, ggrads


def kernel(x, c, w_ada, b_ada, g_ffn1, w1_gate, w1_up, w1_down, g_mix, w_in, g_sb_out, g_dil_out, w_out, rel_bias, g_ffn2, w2_gate, w2_up, w2_down, g_final, loss_target, m_w_ada, m_b_ada, m_g_ffn1, m_w1_gate, m_w1_up, m_w1_down, m_g_mix, m_w_in, m_g_sb_out, m_g_dil_out, m_w_out, m_rel_bias, m_g_ffn2, m_w2_gate, m_w2_up, m_w2_down, m_g_final, v_w_ada, v_b_ada, v_g_ffn1, v_w1_gate, v_w1_up, v_w1_down, v_g_mix, v_w_in, v_g_sb_out, v_g_dil_out, v_w_out, v_rel_bias, v_g_ffn2, v_w2_gate, v_w2_up, v_w2_down, v_g_final):
    nb, seq, d = x.shape
    me = 4 * lax.axis_index("x") + 2 * lax.axis_index("y") + lax.axis_index("c")
    tm = min(TOKEN_TILE, seq)
    fs = w1_gate.shape[2]
    fs_pad = _lane_pad(fs)
    ada_cols = w_ada.shape[2]

    def col_shard(w):
        return _pad_to(w[0].astype(BF16), 1, fs_pad)

    def row_shard(w):
        return _pad_to(w[0].astype(BF16), 0, fs_pad)

    shards = [col_shard(w1_gate), col_shard(w1_up), row_shard(w1_down), w_in[0].astype(BF16),
              w_out[0].astype(BF16), col_shard(w2_gate), col_shard(w2_up), row_shard(w2_down)]
    b_cols = lax.dynamic_slice(b_ada, (0, me * ada_cols), (1, ada_cols))
    c_every, mod_all, first = _first_exchange(_pad_to(c, 0, 8), shards[:3], w_ada[0], b_cols)
    c_all = c_every[:, :nb].reshape(N_DEV * nb, d)
    weights = first + shards[3:]
    mod = lax.dynamic_slice(mod_all, (0, me * 8, 0), (N_DEV, nb, ada_cols))
    mod = mod.transpose(1, 0, 2).reshape(nb, N_MOD, d)

    n_sb = g_sb_out.shape[1] * g_sb_out.shape[2]
    gains = (g_ffn1, g_mix, g_sb_out.reshape(1, n_sb), g_dil_out.reshape(1, -1), g_ffn2,
             g_final.reshape(1, d))
    loss_sum, grad_x, parts, dmod, ggrads = _local_step(x, loss_target, mod, gains, weights, rel_bias, tm, True)

    dg_ffn1, dg_mix, dg_heads, drel, dg_ffn2, dg_final = ggrads
    width = max(d, dg_heads.shape[1], drel.size)
    small = jnp.concatenate(
        [_pad_to(a.reshape(1, -1), 1, width)
         for a in (dg_ffn1, dg_mix, dg_ffn2, dg_final, dg_heads, drel, loss_sum)]
        + [jnp.zeros((1, width), F32)], axis=0)
    dmod_pad = _pad_to(dmod.reshape(nb, N_MOD * d), 0, 8)
    last_part, dmod_all, small_all = _exchange(
        [parts[2], jnp.broadcast_to(dmod_pad, (N_DEV,) + dmod_pad.shape),
         jnp.broadcast_to(small, (N_DEV,) + small.shape)], True, "scatter_last", chips=[True, False, False])
    parts = parts[:2] + (last_part,) + parts[3:]
    dmod_all = dmod_all[:, :nb].reshape(N_DEV * nb, N_MOD * d)
    dmod_cols = lax.dynamic_slice(dmod_all, (0, me * ada_cols), (N_DEV * nb, ada_cols))
    gw_ada, gb_ada = _ada_bwd(c_all, dmod_cols, dmod_all)

    def small_part(row, size, shape):
        return small_all[:, row, :size].reshape((N_DEV,) + shape)

    loss = jnp.sum(small_all[:, 6, 0])

    n_rel = rel_bias.shape
    updates = {
        "w_ada": (w_ada[0], gw_ada[None], m_w_ada[0], v_w_ada[0]),
        "b_ada": (b_ada, gb_ada[None], m_b_ada, v_b_ada),
        "g_ffn1": (g_ffn1, small_part(0, d, (1, d)), m_g_ffn1, v_g_ffn1),
        "w1_gate": (w1_gate[0], parts[0], m_w1_gate[0], v_w1_gate[0]),
        "w1_up": (w1_up[0], parts[1], m_w1_up[0], v_w1_up[0]),
        "w1_down": (w1_down[0], parts[2], m_w1_down[0], v_w1_down[0]),
        "g_mix": (g_mix, small_part(1, d, (1, d)), m_g_mix, v_g_mix),
        "w_in": (w_in[0], parts[3], m_w_in[0], v_w_in[0]),
        "g_sb_out": (g_sb_out[0], small_all[:, 4, :n_sb].reshape((N_DEV,) + g_sb_out.shape[1:]),
                     m_g_sb_out[0], v_g_sb_out[0]),
        "g_dil_out": (g_dil_out[0], small_all[:, 4, n_sb:dg_heads.shape[1]].reshape((N_DEV,) + g_dil_out.shape[1:]),
                      m_g_dil_out[0], v_g_dil_out[0]),
        "w_out": (w_out[0], parts[4], m_w_out[0], v_w_out[0]),
        "rel_bias": (rel_bias, small_part(5, drel.size, n_rel), m_rel_bias, v_rel_bias),
        "g_ffn2": (g_ffn2, small_part(2, d, (1, d)), m_g_ffn2, v_g_ffn2),
        "w2_gate": (w2_gate[0], parts[5], m_w2_gate[0], v_w2_gate[0]),
        "w2_up": (w2_up[0], parts[6], m_w2_up[0], v_w2_up[0]),
        "w2_down": (w2_down[0], parts[7], m_w2_down[0], v_w2_down[0]),
        "g_final": (g_final.reshape(1, d), small_part(3, d, (1, d)), m_g_final.reshape(1, d), v_g_final.reshape(1, d)),
    }
    shapes = {"w_ada": w_ada.shape, "b_ada": b_ada.shape, "g_ffn1": g_ffn1.shape, "w1_gate": w1_gate.shape,
              "w1_up": w1_up.shape, "w1_down": w1_down.shape, "g_mix": g_mix.shape, "w_in": w_in.shape,
              "g_sb_out": g_sb_out.shape, "g_dil_out": g_dil_out.shape, "w_out": w_out.shape,
              "rel_bias": rel_bias.shape, "g_ffn2": g_ffn2.shape, "w2_gate": w2_gate.shape,
              "w2_up": w2_up.shape, "w2_down": w2_down.shape, "g_final": g_final.shape}
    grads, deltas, new_m, new_v = [], [], [], []
    for name, (w, p, m, v) in updates.items():
        transposed = name in ("w1_gate", "w1_up", "w2_gate", "w2_up")
        outs = _adamw(w, p, m, v, f"adamw_{name}", transposed)
        for dst, a in zip((grads, deltas, new_m, new_v), outs):
            dst.append((a.T if transposed else a).reshape(shapes[name]))
    return (loss, grad_x, *grads, *deltas, *new_m, *new_v)
```

```python
import functools
import math

import numpy as np
import jax
import jax.numpy as jnp
from jax import lax
from jax.experimental import pallas as pl
from jax.experimental.pallas import tpu as pltpu

F32 = jnp.float32
BF16 = jnp.bfloat16

EPS = 1e-6
NEG_INF = -1e30
HEAD_DIM = 64
LANES = 128
DIL_BLOCK = 128
DILATIONS = (1, 4, 16)
N_BUCKETS = 32
MAX_DISTANCE = 2048
N_MOD = 9
N_DEV = 8
SB_BLOCK = 256
SB_HEADS = 4
SB_WIDTH = SB_HEADS * HEAD_DIM
DIL_HEADS = 4
DIL_WIDTH = DIL_HEADS * HEAD_DIM
TOKEN_TILE = 512
PROJ_TILE = 1024
GRAD_TILE = 1024
FFN_CHUNKS = 2
VMEM_LIMIT_BYTES = 56 * 1024 * 1024

ADAM_LR = 0.001
ADAM_B1 = 0.9
ADAM_B2 = 0.999
ADAM_EPS = 1e-08
ADAM_WD = 0.01
ADAM_STEP = 10

NT_DIMS = (((1,), (1,)), ((), ()))
TN_DIMS = (((0,), (0,)), ((), ()))


def _params(*sem):
    return pltpu.CompilerParams(dimension_semantics=sem, vmem_limit_bytes=VMEM_LIMIT_BYTES)


def _once(spec):
    return pl.BlockSpec(spec.block_shape, spec.index_map, pipeline_mode=pl.Buffered(1))


def _dot(a, b):
    return jnp.dot(a, b, preferred_element_type=F32)


def _dot_nt(a, b):
    return lax.dot_general(a, b, NT_DIMS, preferred_element_type=F32)


def _dot_tn(a, b):
    return lax.dot_general(a, b, TN_DIMS, preferred_element_type=F32)


def _split_dot(a, b):
    hi = a.astype(BF16)
    lo = (a - hi.astype(F32)).astype(BF16)
    return _dot(hi, b) + _dot(lo, b)


def _sigmoid(z):
    return 1.0 / (1.0 + jnp.exp(-z))


def _norm(x):
    r = lax.rsqrt(jnp.mean(x * x, axis=-1, keepdims=True) + EPS)
    return x * r, r


def _modulate(x, g, mod_ref, k):
    n, _ = _norm(x)
    shift = mod_ref[3 * k:3 * k + 1, :]
    scale = mod_ref[3 * k + 1:3 * k + 2, :]
    return n * g * (1.0 + scale) + shift


def _modulate_bwd(dh, x, g, mod_ref, k):
    n, r = _norm(x)
    scale = mod_ref[3 * k + 1:3 * k + 2, :]
    dshift = jnp.sum(dh, axis=0, keepdims=True)
    dscale = jnp.sum(dh * n * g, axis=0, keepdims=True)
    dg = jnp.sum(dh * n * (1.0 + scale), axis=0, keepdims=True)
    dn = dh * g * (1.0 + scale)
    dx = r * (dn - n * jnp.mean(dn * n, axis=-1, keepdims=True))
    return dx, dshift, dscale, dg


class _Exchange:
    def __init__(self, arrays, scatter, relay=False, chips=None):
        assert not (scatter and relay)
        self.arrays = list(arrays)
        self.scatter = scatter
        self.relay = relay
        self.n = len(self.arrays)
        self.chips = list(chips) if chips is not None else [False] * self.n
        assert scatter or not any(self.chips)
        self.out_shape = [
            jax.ShapeDtypeStruct((N_DEV // 2 if ch else N_DEV,) + tuple(a.shape[1:] if scatter else a.shape), a.dtype)
            for a, ch in zip(self.arrays, self.chips)]
        n_remote = self.n * (N_DEV - 1)
        self.scratch_shapes = [pltpu.SemaphoreType.DMA((n_remote,)), pltpu.SemaphoreType.DMA((n_remote,)),
                               pltpu.SemaphoreType.DMA((self.n,))]

    def _copies(self, in_refs, out_refs, sems):
        send_sems, recv_sems, local_sems = sems
        x, y, c = lax.axis_index("x"), lax.axis_index("y"), lax.axis_index("c")
        me = 4 * x + 2 * y + c
        local, remote, relayed = [], {}, {}
        for a in range(self.n):
            if self.chips[a]:
                mine = 2 * x + y
                local.append(pltpu.make_async_copy(in_refs[a].at[mine], out_refs[a].at[mine], local_sems.at[a]))
                for k in (2, 4, 6):
                    px = 1 - x if k & 4 else x
                    py = 1 - y if k & 2 else y
                    sem = a * (N_DEV - 1) + k - 1
                    remote[a, k] = pltpu.make_async_remote_copy(
                        src_ref=in_refs[a].at[2 * px + py], dst_ref=out_refs[a].at[mine],
                        send_sem=send_sems.at[sem], recv_sem=recv_sems.at[sem],
                        device_id=(px, py, c), device_id_type=pl.DeviceIdType.MESH)
                continue
            src = in_refs[a].at[me] if self.scatter else in_refs[a]
            local.append(pltpu.make_async_copy(src, out_refs[a].at[me], local_sems.at[a]))
            for k in range(1, N_DEV):
                px = 1 - x if k & 4 else x
                py = 1 - y if k & 2 else y
                pc = 1 - c if k & 1 else c
                sem = a * (N_DEV - 1) + k - 1
                if self.relay and k & 1 and k > 1:
                    slot = 4 * px + 2 * py + c
                    relayed[a, k] = pltpu.make_async_remote_copy(
                        src_ref=out_refs[a].at[slot], dst_ref=out_refs[a].at[slot],
                        send_sem=send_sems.at[sem], recv_sem=recv_sems.at[sem],
                        device_id=(x, y, 1 - c), device_id_type=pl.DeviceIdType.MESH)
                    continue
                src = in_refs[a].at[4 * px + 2 * py + pc] if self.scatter else in_refs[a]
                remote[a, k] = pltpu.make_async_remote_copy(
                    src_ref=src, dst_ref=out_refs[a].at[me],
                    send_sem=send_sems.at[sem], recv_sem=recv_sems.at[sem],
                    device_id=(px, py, pc), device_id_type=pl.DeviceIdType.MESH)
        return local, remote, relayed

    def start(self, in_refs, out_refs, sems):
        local, remote, _ = self._copies(in_refs, out_refs, sems)
        for cp in local + list(remote.values()):
            cp.start()

    def wait(self, in_refs, out_refs, sems):
        local, remote, relayed = self._copies(in_refs, out_refs, sems)
        for (a, k), cp in relayed.items():
            remote[a, k - 1].wait_recv()
            cp.start()
        for (a, k), cp in remote.items():
            if (a, k + 1) not in relayed:
                cp.wait_recv()
        for cp in relayed.values():
            cp.wait_recv()
        for cp in list(remote.values()) + list(relayed.values()):
            cp.wait_send()
        for cp in local:
            cp.wait()


def _call(body, *, name, args, in_specs, out_specs, out_shape, scratch_shapes=(), grid=(),
          params=None, exchange=None):
    n_in, n_out = len(args), len(out_shape)
    if exchange is None:
        outs = pl.pallas_call(
            body, name=name, grid=grid, in_specs=list(in_specs), out_specs=list(out_specs),
            out_shape=list(out_shape), scratch_shapes=list(scratch_shapes), compiler_params=params,
        )(*args)
        return list(outs), []
    n_ex = exchange.n

    def wrapped(*refs):
        ins, refs = refs[:n_in], refs[n_in:]
        ex_in, refs = refs[:n_ex], refs[n_ex:]
        outs, refs = refs[:n_out], refs[n_out:]
        ex_out, refs = refs[:n_ex], refs[n_ex:]
        scratch, sems = refs[:len(refs) - 3], refs[len(refs) - 3:]
        if not grid:
            exchange.start(ex_in, ex_out, sems)
            body(*ins, *outs, *scratch)
            exchange.wait(ex_in, ex_out, sems)
            return
        first = functools.reduce(jnp.logical_and, [pl.program_id(a) == 0 for a in range(len(grid))])
        last = functools.reduce(jnp.logical_and, [pl.program_id(a) == grid[a] - 1 for a in range(len(grid))])

        @pl.when(first)
        def _():
            exchange.start(ex_in, ex_out, sems)

        body(*ins, *outs, *scratch)

        @pl.when(last)
        def _():
            exchange.wait(ex_in, ex_out, sems)

    any_spec = pl.BlockSpec(memory_space=pl.ANY)
    outs = pl.pallas_call(
        wrapped, name=name, grid=grid,
        in_specs=list(in_specs) + [any_spec] * n_ex, out_specs=list(out_specs) + [any_spec] * n_ex,
        out_shape=list(out_shape) + exchange.out_shape,
        scratch_shapes=list(scratch_shapes) + exchange.scratch_shapes, compiler_params=params,
    )(*args, *exchange.arrays)
    return list(outs[:n_out]), list(outs[n_out:])


def _exchange(arrays, scatter, name, relay=False, chips=None):
    return _call(lambda: None, name=name, args=(), in_specs=(), out_specs=(), out_shape=(),
                 exchange=_Exchange(arrays, scatter, relay, chips))[1]


def _first_exchange(c_pad, shards, w, b):
    rows, d = c_pad.shape
    cols = w.shape[1]
    ex_c = _Exchange([c_pad], False)
    ex_w = _Exchange(shards, False, relay=True)
    ex_m = _Exchange([jax.ShapeDtypeStruct((N_DEV * rows, cols), F32)], False)
    n_w = ex_w.n

    def body(*refs):
        c_ref, w_refs, wa_ref, b_ref = refs[0], refs[1:1 + n_w], refs[1 + n_w], refs[2 + n_w]
        outs = refs[3 + n_w:]
        cg_ref, wg_refs, mg_ref = outs[0], outs[1:1 + n_w], outs[1 + n_w]
        scratch = outs[2 + n_w:]
        sems_c, sems_w, sems_m, c_vm, m_vm = scratch[0:3], scratch[3:6], scratch[6:9], scratch[9], scratch[10]
        ex_c.start([c_ref], [cg_ref], sems_c)
        ex_c.wait([c_ref], [cg_ref], sems_c)
        pltpu.sync_copy(cg_ref, c_vm)
        cv = c_vm[...].reshape(N_DEV * rows, d)
        s = (cv * _sigmoid(cv)).astype(BF16)
        m_vm[...] = _dot(s, wa_ref[...].astype(BF16)) + b_ref[...]
        ex_m.start([m_vm], [mg_ref], sems_m)
        ex_w.start(w_refs, wg_refs, sems_w)
        ex_m.wait([m_vm], [mg_ref], sems_m)
        ex_w.wait(w_refs, wg_refs, sems_w)

    any_spec = pl.BlockSpec(memory_space=pl.ANY)
    vmem_spec = pl.BlockSpec(memory_space=pltpu.VMEM)
    outs = pl.pallas_call(
        body, name="first_exchange",
        in_specs=[any_spec] * (1 + n_w) + [vmem_spec, vmem_spec],
        out_specs=[any_spec] * (2 + n_w),
        out_shape=ex_c.out_shape + ex_w.out_shape + ex_m.out_shape,
        scratch_shapes=ex_c.scratch_shapes + ex_w.scratch_shapes + ex_m.scratch_shapes
        + [pltpu.VMEM((N_DEV, rows, d), F32), pltpu.VMEM((N_DEV * rows, cols), F32)],
        compiler_params=pltpu.CompilerParams(vmem_limit_bytes=VMEM_LIMIT_BYTES),
    )(c_pad, *shards, w, b)
    return outs[0], outs[1 + n_w], list(outs[1:1 + n_w])


def _ada_bwd(c_all, dmod_cols, dmod_all):
    def body(c_ref, dc_ref, da_ref, gw_ref, gb_ref):
        cv = c_ref[...]
        s = cv * _sigmoid(cv)
        gw_ref[...] = lax.dot_general(s, dc_ref[...], TN_DIMS, preferred_element_type=F32,
                                      precision=lax.Precision.HIGHEST)
        gb_ref[...] = jnp.sum(da_ref[...], axis=0, keepdims=True)

    return pl.pallas_call(
        body, name="ada_bwd",
        out_shape=(jax.ShapeDtypeStruct((c_all.shape[1], dmod_cols.shape[1]), F32),
                   jax.ShapeDtypeStruct((1, dmod_all.shape[1]), F32)),
        compiler_params=pltpu.CompilerParams(vmem_limit_bytes=VMEM_LIMIT_BYTES),
    )(c_all, dmod_cols, dmod_all)


def _ffn_fwd(x, mod, g, wg, wu, wd, k, tm, exchange=None):
    t, d = x.shape
    ns, _, fs = wg.shape
    nt = t // tm
    tpb = nt // mod.shape[0]
    rows = tm // FFN_CHUNKS

    def body(x_ref, mod_ref, g_ref, wg_ref, wu_ref, wd_ref, xo_ref, f_ref, gg_ref, uu_ref, h_sc, acc):
        j = pl.program_id(1)

        @pl.when(j == 0)
        def _():
            h_sc[...] = _modulate(x_ref[...], g_ref[...], mod_ref, k).astype(BF16)
            acc[...] = jnp.zeros_like(acc)

        chunks = [pl.ds(c * rows, rows) for c in range(FFN_CHUNKS)]
        wg, wu, wd = wg_ref[...], wu_ref[...], wd_ref[...]
        gates, ups = [], []
        for rs in chunks:
            h = h_sc[rs, :]
            gates.append(_dot(h, wg))
            ups.append(_dot(h, wu))
        acts = [(g * _sigmoid(g) * u).astype(BF16) for g, u in zip(gates, ups)]
        for rs, g, u in zip(chunks, gates, ups):
            gg_ref[rs, :] = g.astype(BF16)
            uu_ref[rs, :] = u.astype(BF16)
        downs = [_dot(a, wd) for a in acts]
        for rs, dn in zip(chunks, downs):
            acc[rs, :] += dn

        @pl.when(j == ns - 1)
        def _():
            f = acc[...]
            f_ref[...] = f.astype(BF16)
            xo_ref[...] = x_ref[...] + 0.5 * mod_ref[3 * k + 2:3 * k + 3, :] * f

    tok = pl.BlockSpec((tm, d), lambda i, j: (i, 0))
    hid = pl.BlockSpec((None, tm, fs), lambda i, j: (j, i, 0))
    return _call(
        body, name=f"ffn_fwd{k}", grid=(nt, ns), args=(x, mod, g, wg, wu, wd),
        in_specs=[tok,
                  pl.BlockSpec((None, N_MOD, d), lambda i, j: (i // tpb, 0, 0)),
                  pl.BlockSpec((1, d), lambda i, j: (0, 0)),
                  pl.BlockSpec((None, d, fs), lambda i, j: (j, 0, 0)),
                  pl.BlockSpec((None, d, fs), lambda i, j: (j, 0, 0)),
                  pl.BlockSpec((None, fs, d), lambda i, j: (j, 0, 0))],
        out_specs=[tok, tok, hid, hid],
        out_shape=[jax.ShapeDtypeStruct((t, d), F32), jax.ShapeDtypeStruct((t, d), BF16),
                   jax.ShapeDtypeStruct((ns, t, fs), BF16), jax.ShapeDtypeStruct((ns, t, fs), BF16)],
        scratch_shapes=[pltpu.VMEM((tm, d), BF16), pltpu.VMEM((tm, d), F32)],
        params=_params("arbitrary", "arbitrary"), exchange=exchange)


def _ffn_bwd(dxo, x, f, mod, g, gate, up, wg, wu, wd, k, tm, exchange=None):
    t, d = x.shape
    ns, _, fs = wg.shape
    nt = t // tm
    nb = mod.shape[0]
    tpb = nt // nb
    rows = tm // FFN_CHUNKS

    def body(dxo_ref, x_ref, f_ref, mod_ref, g_ref, gg_ref, uu_ref, wg_ref, wu_ref, wd_ref,
             dx_ref, dgg_ref, duu_ref, act_ref, h_ref, df_ref, dmod_ref, dg_ref, acc):
        i, j = pl.program_id(0), pl.program_id(1)

        @pl.when(j == 0)
        def _():
            df = 0.5 * mod_ref[3 * k + 2:3 * k + 3, :] * dxo_ref[...]
            df_ref[...] = df.astype(BF16)
            h_ref[...] = _modulate(x_ref[...], g_ref[...], mod_ref, k).astype(BF16)
            acc[...] = jnp.zeros_like(acc)

        chunks = [pl.ds(c * rows, rows) for c in range(FFN_CHUNKS)]
        wg, wu, wd = wg_ref[...], wu_ref[...], wd_ref[...]
        dacts = [_dot_nt(df_ref[rs, :], wd) for rs in chunks]
        dgates, dups = [], []
        for rs, dact in zip(chunks, dacts):
            gv, uv = gg_ref[rs, :].astype(F32), uu_ref[rs, :].astype(F32)
            sig = _sigmoid(gv)
            s = gv * sig
            act_ref[rs, :] = (s * uv).astype(BF16)
            dups.append((dact * s).astype(BF16))
            dgates.append((dact * uv * (sig * (1.0 + gv * (1.0 - sig)))).astype(BF16))
        dhs = [_dot_nt(dg, wg) + _dot_nt(du, wu) for dg, du in zip(dgates, dups)]
        for rs, dg, du, dh in zip(chunks, dgates, dups, dhs):
            dgg_ref[rs, :] = dg
            duu_ref[rs, :] = du
            acc[rs, :] += dh

        @pl.when(j == ns - 1)
        def _():
            dx, dshift, dscale, dg = _modulate_bwd(acc[...], x_ref[...], g_ref[...], mod_ref, k)
            dxo_v = dxo_ref[...]
            dx_ref[...] = dxo_v + dx
            dgt = jnp.sum(0.5 * f_ref[...].astype(F32) * dxo_v, axis=0, keepdims=True)

            @pl.when(i % tpb == 0)
            def _():
                dmod_ref[...] = jnp.zeros_like(dmod_ref)

            @pl.when(i == 0)
            def _():
                dg_ref[...] = jnp.zeros_like(dg_ref)

            dmod_ref[0:1, :] += dshift
            dmod_ref[1:2, :] += dscale
            dmod_ref[2:3, :] += dgt
            dg_ref[0:1, :] += dg

    tok = pl.BlockSpec((tm, d), lambda i, j: (i, 0))
    hid = pl.BlockSpec((None, tm, fs), lambda i, j: (j, i, 0))
    return _call(
        body, name=f"ffn_bwd{k}", grid=(nt, ns), args=(dxo, x, f, mod, g, gate, up, wg, wu, wd),
        in_specs=[tok, _once(tok), _once(tok),
                  pl.BlockSpec((None, N_MOD, d), lambda i, j: (i // tpb, 0, 0)),
                  pl.BlockSpec((1, d), lambda i, j: (0, 0)),
                  hid, hid,
                  pl.BlockSpec((None, d, fs), lambda i, j: (j, 0, 0)),
                  pl.BlockSpec((None, d, fs), lambda i, j: (j, 0, 0)),
                  pl.BlockSpec((None, fs, d), lambda i, j: (j, 0, 0))],
        out_specs=[tok, hid, hid, hid, tok, tok,
                   pl.BlockSpec((None, 8, d), lambda i, j: (i // tpb, 0, 0)),
                   pl.BlockSpec((8, d), lambda i, j: (0, 0))],
        out_shape=[jax.ShapeDtypeStruct((t, d), F32),
                   jax.ShapeDtypeStruct((ns, t, fs), BF16), jax.ShapeDtypeStruct((ns, t, fs), BF16),
                   jax.ShapeDtypeStruct((ns, t, fs), BF16),
                   jax.ShapeDtypeStruct((t, d), BF16), jax.ShapeDtypeStruct((t, d), BF16),
                   jax.ShapeDtypeStruct((nb, 8, d), F32), jax.ShapeDtypeStruct((8, d), F32)],
        scratch_shapes=[pltpu.VMEM((tm, d), F32)],
        params=_params("arbitrary", "arbitrary"), exchange=exchange)


def _mm_tn(a, b, a_spec, b_spec, out_shape, n_tiles, name, exchange=None, keep_transposed=False,
           pair_reduce=False):
    n_out = out_shape[0]
    block = tuple(out_shape[1:])
    last = n_tiles - 1
    flip = block[0] > block[1]
    if flip:
        block = block[::-1]
    if flip and keep_transposed:
        flip_back, out_shape = False, (n_out,) + block
    else:
        flip_back = flip
    full_shape = tuple(out_shape)
    n_pairs = n_out // 2
    if pair_reduce:
        out_shape = (n_pairs,) + full_shape[1:]

    def body(a_ref, b_ref, o_ref, acc, *pair):
        i, j = pl.program_id(0), pl.program_id(1)
        prod = _dot_tn(b_ref[...], a_ref[...]) if flip else _dot_tn(a_ref[...], b_ref[...])
        full_ref = pair[0] if pair_reduce else o_ref

        @pl.when(i == 0)
        def _():
            acc[j] = prod

        @pl.when(i > 0)
        def _():
            acc[j] += prod

        @pl.when(i == last)
        def _():
            total = acc[j]
            full_ref[j] = (total.T if flip_back else total).astype(BF16)

        if pair_reduce:
            _, landed, send_sems, recv_sems = pair

            @pl.when(jnp.logical_and(i == last, j == n_out - 1))
            def _():
                x, y, c = lax.axis_index("x"), lax.axis_index("y"), lax.axis_index("c")
                copies = [pltpu.make_async_remote_copy(
                    src_ref=full_ref.at[2 * q + 1 - c], dst_ref=landed.at[q],
                    send_sem=send_sems.at[q], recv_sem=recv_sems.at[q],
                    device_id=(x, y, 1 - c), device_id_type=pl.DeviceIdType.MESH) for q in range(n_pairs)]
                for cp in copies:
                    cp.start()
                for q, cp in enumerate(copies):
                    cp.wait_recv()
                    o_ref[q] = (full_ref[2 * q + c].astype(F32) + landed[q].astype(F32)).astype(BF16)
                for cp in copies:
                    cp.wait_send()

    scratch = [pltpu.VMEM((n_out,) + block, F32)]
    if pair_reduce:
        scratch += [pltpu.VMEM(full_shape, BF16), pltpu.VMEM(out_shape, BF16),
                    pltpu.SemaphoreType.DMA((n_pairs,)), pltpu.SemaphoreType.DMA((n_pairs,))]
    outs, sent = _call(
        body, name=name, grid=(n_tiles, n_out), args=(a, b), in_specs=[a_spec, b_spec],
        out_specs=[pl.BlockSpec(out_shape, lambda i, j: (0,) * len(out_shape))],
        out_shape=[jax.ShapeDtypeStruct(out_shape, BF16)],
        scratch_shapes=scratch,
        params=_params("arbitrary", "arbitrary"), exchange=exchange)
    return (outs[0], sent) if exchange is not None else outs[0]


def _ffn_weight_grads(h, dgate, dup, act, df, tm, tag, stream=False, first=None):
    t, d = h.shape
    ns, _, fs = dgate.shape
    nt = t // tm
    tok = pl.BlockSpec((tm, d), lambda i, j: (i, 0))
    hid = pl.BlockSpec((None, tm, fs), lambda i, j: (j, i, 0))
    if not stream:
        gwg = _mm_tn(h, dgate, tok, hid, (ns, d, fs), nt, f"grad_wg{tag}", keep_transposed=True)
        gwu = _mm_tn(h, dup, tok, hid, (ns, d, fs), nt, f"grad_wu{tag}", keep_transposed=True)
        gwd = _mm_tn(act, df, hid, tok, (ns, fs, d), nt, f"grad_wd{tag}")
        return gwg, gwu, gwd
    gwg, brought = _mm_tn(h, dgate, tok, hid, (ns, d, fs), nt, f"grad_wg{tag}", first,
                          keep_transposed=True, pair_reduce=True)
    gwu, sent_g = _mm_tn(h, dup, tok, hid, (ns, d, fs), nt, f"grad_wu{tag}",
                         _Exchange([gwg], True, chips=[True]), keep_transposed=True, pair_reduce=True)
    gwd, sent_u = _mm_tn(act, df, hid, tok, (ns, fs, d), nt, f"grad_wd{tag}",
                         _Exchange([gwu], True, chips=[True]), pair_reduce=True)
    return sent_g[0], sent_u[0], gwd, brought


def _stage_shape(rows, cols):
    return pltpu.VMEM((cols // LANES, rows, LANES), F32)


def _stage(value, stage_ref):
    for k in range(stage_ref.shape[0]):
        stage_ref[k] = value[:, k * LANES:(k + 1) * LANES]


def _to_residue_rows(stage_ref, dst_ref, dil):
    rows = stage_ref.shape[1] // dil
    for r in range(dil):
        for k in range(stage_ref.shape[0]):
            dst_ref[r, :, k * LANES:(k + 1) * LANES] = (
                stage_ref.at[k][pl.ds(r, rows, stride=dil), :].astype(dst_ref.dtype))


def _from_residue_rows(src_ref, stage_ref, dil):
    rows = stage_ref.shape[1] // dil
    chunks = range(stage_ref.shape[0])
    for r in range(dil):
        for k in chunks:
            stage_ref.at[k][pl.ds(r, rows, stride=dil), :] = src_ref[r, :, k * LANES:(k + 1) * LANES].astype(F32)
    return jnp.concatenate([stage_ref[k] for k in chunks], axis=1)


def _residue_shape(nb, seq, width, dil, dtype):
    return jax.ShapeDtypeStruct((nb, dil, seq // dil, width), dtype)


def _residue_spec(tm, tpb, cols, dil, col_block):
    return pl.BlockSpec((None, dil, tm // dil, cols),
                        lambda i, *rest: (i // tpb, 0, i % tpb, col_block(i, *rest)))


def _qkv_fwd(x, mod, g, win, tm, exchange=None):
    t, d = x.shape
    ns, _, cs = win.shape
    nt = t // tm
    nb = mod.shape[0]
    tpb = nt // nb
    seq = t // nb
    half = ns // 2
    n_res = len(DILATIONS) - 1

    def body(x_ref, mod_ref, g_ref, w_ref, sb_ref, dil_ref, *rest):
        res_refs, h_ref, sc = rest[:n_res], rest[n_res], rest[n_res + 1]
        j = pl.program_id(1)

        @pl.when(j == 0)
        def _():
            h_ref[...] = _modulate(x_ref[...], g_ref[...], mod_ref, 1).astype(BF16)

        res = _dot(h_ref[...], w_ref[...])

        @pl.when(j < half)
        def _():
            sb_ref[...] = res.astype(BF16)

        @pl.when(j >= half)
        def _():
            dil_ref[...] = res.astype(BF16)
            _stage(res, sc)
            for ref, dil in zip(res_refs, DILATIONS[1:]):
                _to_residue_rows(sc, ref, dil)

    def dil_col(i, j):
        return jnp.maximum(j - half, 0)

    tok = pl.BlockSpec((tm, d), lambda i, j: (i, 0))
    wide = jax.ShapeDtypeStruct((t, half * cs), BF16)
    outs, got = _call(
        body, name="qkv_fwd", grid=(nt, ns), args=(x, mod, g, win),
        in_specs=[tok,
                  pl.BlockSpec((None, N_MOD, d), lambda i, j: (i // tpb, 0, 0)),
                  pl.BlockSpec((1, d), lambda i, j: (0, 0)),
                  pl.BlockSpec((None, d, cs), lambda i, j: (j, 0, 0))],
        out_specs=[pl.BlockSpec((tm, cs), lambda i, j: (i, jnp.minimum(j, half - 1))),
                   pl.BlockSpec((tm, cs), lambda i, j: (i, dil_col(i, j)))]
        + [_residue_spec(tm, tpb, cs, dil, dil_col) for dil in DILATIONS[1:]] + [tok],
        out_shape=[wide, wide] + [_residue_shape(nb, seq, half * cs, dil, BF16) for dil in DILATIONS[1:]]
        + [jax.ShapeDtypeStruct((t, d), BF16)],
        scratch_shapes=[_stage_shape(tm, cs)],
        params=_params("arbitrary", "arbitrary"), exchange=exchange)
    qkv_dil = [outs[1]] + [a.reshape(t, half * cs) for a in outs[2:2 + n_res]]
    return (outs[0], qkv_dil, outs[-1]), got


def _qkv_bwd(dqkv, dxo, x, mod, g, win, tm, exchange=None):
    t, d = x.shape
    ns, _, cs = win.shape
    nt = t // tm
    nb = mod.shape[0]
    tpb = nt // nb

    def body(dq_ref, dxo_ref, x_ref, mod_ref, g_ref, w_ref, dx_ref, dmod_ref, dg_ref, acc):
        i, j = pl.program_id(0), pl.program_id(1)

        @pl.when(j == 0)
        def _():
            acc[...] = jnp.zeros_like(acc)

        acc[...] += _dot_nt(dq_ref[...], w_ref[...])

        @pl.when(j == ns - 1)
        def _():
            dx, dshift, dscale, dg = _modulate_bwd(acc[...], x_ref[...], g_ref[...], mod_ref, 1)
            dx_ref[...] = dxo_ref[...] + dx

            @pl.when(i % tpb == 0)
            def _():
                dmod_ref[...] = jnp.zeros_like(dmod_ref)

            @pl.when(i == 0)
            def _():
                dg_ref[...] = jnp.zeros_like(dg_ref)

            dmod_ref[0:1, :] += dshift
            dmod_ref[1:2, :] += dscale
            dg_ref[0:1, :] += dg

    tok = pl.BlockSpec((tm, d), lambda i, j: (i, 0))
    return _call(
        body, name="qkv_bwd", grid=(nt, ns), args=(dqkv, dxo, x, mod, g, win),
        in_specs=[pl.BlockSpec((tm, cs), lambda i, j: (i, j)), tok, tok,
                  pl.BlockSpec((None, N_MOD, d), lambda i, j: (i // tpb, 0, 0)),
                  pl.BlockSpec((1, d), lambda i, j: (0, 0)),
                  pl.BlockSpec((None, d, cs), lambda i, j: (j, 0, 0))],
        out_specs=[tok,
                   pl.BlockSpec((None, 8, d), lambda i, j: (i // tpb, 0, 0)),
                   pl.BlockSpec((8, d), lambda i, j: (0, 0))],
        out_shape=[jax.ShapeDtypeStruct((t, d), F32),
                   jax.ShapeDtypeStruct((nb, 8, d), F32), jax.ShapeDtypeStruct((8, d), F32)],
        scratch_shapes=[pltpu.VMEM((tm, d), F32)],
        params=_params("arbitrary", "arbitrary"), exchange=exchange)


def _heads(a):
    return [a[:, h * HEAD_DIM:(h + 1) * HEAD_DIM] for h in range(a.shape[1] // HEAD_DIM)]


def _own_lanes():
    lane = lax.broadcasted_iota(jnp.int32, (1, LANES), 1)
    return [lane < HEAD_DIM, lane >= HEAD_DIM]


def _pair_tiles(a):
    return [a[:, (h // 2) * LANES:(h // 2 + 1) * LANES] for h in range(a.shape[1] // HEAD_DIM)]


def _own_tiles(a, own):
    return [jnp.where(own[h % 2], tile, jnp.zeros_like(tile)) for h, tile in enumerate(_pair_tiles(a))]


def _merge_tiles(per_head, own):
    return jnp.concatenate([jnp.where(own[0], per_head[h], per_head[h + 1])
                            for h in range(0, len(per_head), 2)], axis=1)


def _scaled(q):
    return (q.astype(F32) * (HEAD_DIM ** -0.5)).astype(BF16)


def _sb_logits(qh, kh, tri, causal):
    zs = [_dot_nt(q, k) for q, k in zip(qh, kh)]
    es = [jnp.exp(-jnp.abs(z)) for z in zs]
    log_nots = [-(jnp.maximum(z, 0.0) + jnp.log(1.0 + e)) for z, e in zip(zs, es)]
    if causal is not None:
        log_nots = [jnp.where(causal, ln, 0.0) for ln in log_nots]
    return zs, es, [_split_dot(ln, tri) for ln in log_nots]


def _sb_masks():
    rows = lax.broadcasted_iota(jnp.int32, (SB_BLOCK, SB_BLOCK), 0)
    cols = lax.broadcasted_iota(jnp.int32, (SB_BLOCK, SB_BLOCK), 1)
    return (rows >= cols).astype(BF16), (rows <= cols).astype(BF16), cols < rows


def _sb_fwd(qkv, nb, seq, exchange=None):
    t = qkv.shape[0]
    n_pairs = (qkv.shape[1] // 3) // SB_WIDTH
    tb = SB_BLOCK
    n_blk = seq // tb

    def body(q_ref, k_ref, v_ref, o_ref, c_ref):
        tri, _, causal = _sb_masks()
        own = _own_lanes()

        def key_block(qh, kj, carry, mask):
            ks = pl.multiple_of(kj * tb, tb)
            kh, vh = _pair_tiles(k_ref[pl.ds(ks, tb), :]), _pair_tiles(v_ref[pl.ds(ks, tb), :])
            zs, _, suffixes = _sb_logits(qh, kh, tri, mask)
            ws = [jnp.exp(z + suffix + cr[1]) for z, suffix, cr in zip(zs, suffixes, carry)]
            if mask is not None:
                ws = [jnp.where(mask, w, 0.0) for w in ws]
            pv = [_dot(w.astype(BF16), v) for w, v in zip(ws, vh)]
            return tuple((cr[0] + p, cr[1] + suffix[:, 0:1]) for cr, p, suffix in zip(carry, pv, suffixes))

        def query_block(qi, _):
            qs = pl.multiple_of(qi * tb, tb)
            qh = _own_tiles(_scaled(q_ref[pl.ds(qs, tb), :]), own)
            zero = (jnp.zeros((tb, LANES), F32), jnp.zeros((tb, 1), F32))
            carry = key_block(qh, qi, (zero,) * SB_HEADS, causal)
            carry = lax.fori_loop(0, qi, lambda it, cr: key_block(qh, qi - 1 - it, cr, None), carry)
            o_ref[pl.ds(qs, tb), :] = _merge_tiles([cr[0] for cr in carry], own)
            c_ref[pl.ds(qs, tb), :] = _merge_tiles([jnp.broadcast_to(cr[1], (tb, LANES)) for cr in carry], own)
            return 0

        lax.fori_loop(0, n_blk, query_block, 0)

    def spec(offset):
        return pl.BlockSpec((seq, SB_WIDTH), lambda b, p: (b, offset + p))

    out = jax.ShapeDtypeStruct((t, n_pairs * SB_WIDTH), F32)
    return _call(
        body, name="sb_fwd", grid=(nb, n_pairs), args=(qkv, qkv, qkv),
        in_specs=[spec(0), spec(n_pairs), spec(2 * n_pairs)],
        out_specs=[spec(0), spec(0)], out_shape=[out, out],
        params=_params("arbitrary", "arbitrary"), exchange=exchange)


def _sb_bwd(qkv, do, csum, nb, seq, exchange=None):
    t = qkv.shape[0]
    n_pairs = (qkv.shape[1] // 3) // SB_WIDTH
    tb = SB_BLOCK
    n_blk = seq // tb
    scale = HEAD_DIM ** -0.5

    def body(q_ref, k_ref, v_ref, do_ref, c_ref, dq_ref, dk_ref, dv_ref, dkt_acc, dvt_acc):
        tri, tri_prefix, causal = _sb_masks()
        own = _own_lanes()
        dkt_acc[...] = jnp.zeros_like(dkt_acc)
        dvt_acc[...] = jnp.zeros_like(dvt_acc)

        def key_block(qh, qth, doh, doth, ch, kj, carry, mask):
            ks = pl.multiple_of(kj * tb, tb)
            kh, vh = _pair_tiles(k_ref[pl.ds(ks, tb), :]), _pair_tiles(v_ref[pl.ds(ks, tb), :])
            heads = range(SB_HEADS)
            zs, es, suffixes = _sb_logits(qh, kh, tri, mask)
            dws = [_dot_nt(doh[h], vh[h]) for h in heads]
            lefts = [carry[h][1] + suffixes[h][:, 0:1] for h in heads]
            ws = [jnp.exp(zs[h] + suffixes[h] + (ch[h] - lefts[h])) for h in heads]
            if mask is not None:
                ws = [jnp.where(mask, w, 0.0) for w in ws]
            dlws = [ws[h] * dws[h] for h in heads]
            dprefixes = [_split_dot(dlw, tri_prefix) for dlw in dlws]
            dvts = [_dot(doth[h], ws[h].astype(BF16)) for h in heads]
            dzbs = []
            for h in heads:
                sig = jnp.where(zs[h] >= 0.0, 1.0, es[h]) * pl.reciprocal(1.0 + es[h], approx=True)
                dz = dlws[h] - sig * (carry[h][2] + dprefixes[h])
                if mask is not None:
                    dz = jnp.where(mask, dz, 0.0)
                dzbs.append(dz.astype(BF16))
            dkts = [_dot(qth[h], dzbs[h]) for h in heads]
            dqs = [_dot(dzbs[h], kh[h]) for h in heads]
            dkt_acc[:, pl.ds(ks, tb)] += jnp.concatenate([dkts[h] + dkts[h + 1] for h in heads[::2]], axis=0)
            dvt_acc[:, pl.ds(ks, tb)] += jnp.concatenate([dvts[h] + dvts[h + 1] for h in heads[::2]], axis=0)
            return tuple((carry[h][0] + dqs[h], lefts[h], carry[h][2] + dprefixes[h][:, tb - 1:tb])
                         for h in heads)

        def query_block(qi, _):
            qs = pl.multiple_of(qi * tb, tb)
            qh = _own_tiles(_scaled(q_ref[pl.ds(qs, tb), :]), own)
            doh = _own_tiles(do_ref[pl.ds(qs, tb), :], own)
            qth = [a.astype(F32).T.astype(BF16) for a in qh]
            doth = [a.T.astype(BF16) for a in doh]
            doh = [a.astype(BF16) for a in doh]
            cv = c_ref[pl.ds(qs, tb), :]
            ch = [cv[:, h * HEAD_DIM:h * HEAD_DIM + 1] for h in range(SB_HEADS)]
            zero = (jnp.zeros((tb, LANES), F32), jnp.zeros((tb, 1), F32), jnp.zeros((tb, 1), F32))
            carry = lax.fori_loop(
                0, qi, lambda kj, cr: key_block(qh, qth, doh, doth, ch, kj, cr, None), (zero,) * SB_HEADS)
            carry = key_block(qh, qth, doh, doth, ch, qi, carry, causal)
            dq = _merge_tiles([cr[0] for cr in carry], own) * scale
            dq_ref[pl.ds(qs, tb), :] = dq.astype(BF16)
            return 0

        lax.fori_loop(0, n_blk, query_block, 0)
        dk_ref[...] = dkt_acc[...].T.astype(BF16)
        dv_ref[...] = dvt_acc[...].T.astype(BF16)

    def spec(offset):
        return pl.BlockSpec((seq, SB_WIDTH), lambda b, p: (b, offset + p))

    out = jax.ShapeDtypeStruct((t, n_pairs * SB_WIDTH), BF16)
    return _call(
        body, name="sb_bwd", grid=(nb, n_pairs), args=(qkv, qkv, qkv, do, csum),
        in_specs=[spec(0), spec(n_pairs), spec(2 * n_pairs), spec(0), spec(0)],
        out_specs=[spec(0), spec(0), spec(0)],
        out_shape=[out, out, out],
        scratch_shapes=[pltpu.VMEM((SB_WIDTH, seq), F32), pltpu.VMEM((SB_WIDTH, seq), F32)],
        params=_params("arbitrary", "arbitrary"), exchange=exchange)


def _dil_block_scores(qh, kph, kch, bias_ref, has_prev, band_prev, band_cur):
    scale = HEAD_DIM ** -0.5
    heads = range(len(qh))
    no_prev = jnp.where(has_prev, 0.0, NEG_INF)
    zps = [_dot_nt(qh[h], kph[h]) for h in heads]
    zcs = [_dot_nt(qh[h], kch[h]) for h in heads]
    zps = [jnp.where(band_prev, zps[h] * scale + bias_ref[h, :, 0:DIL_BLOCK], NEG_INF) + no_prev for h in heads]
    zcs = [jnp.where(band_cur, zcs[h] * scale + bias_ref[h, :, DIL_BLOCK:2 * DIL_BLOCK], NEG_INF) for h in heads]
    return zps, zcs


def _dil_bands():
    rows = lax.broadcasted_iota(jnp.int32, (DIL_BLOCK, DIL_BLOCK), 0)
    cols = lax.broadcasted_iota(jnp.int32, (DIL_BLOCK, DIL_BLOCK), 1)
    return cols >= rows, cols <= rows


def _dil_fwd(qkv, bias, nb, seq, dil, exchange=None):
    t, width = qkv.shape
    n_pairs = (width // 3) // DIL_WIDTH
    bq = DIL_BLOCK
    n_blk = seq // bq
    per_seq = n_blk // dil
    heads = range(DIL_HEADS)

    def body(q_ref, k_ref, v_ref, bias_ref, o_ref, lse_ref):
        band_prev, band_cur = _dil_bands()
        own = _own_lanes()

        def block(n, _):
            has_prev = (n & (per_seq - 1)) != 0
            qs = pl.multiple_of(n * bq, bq)
            ps = pl.multiple_of(jnp.maximum(n - 1, 0) * bq, bq)
            qh = _own_tiles(q_ref[pl.ds(qs, bq), :], own)
            kp, kc = _pair_tiles(k_ref[pl.ds(ps, bq), :]), _pair_tiles(k_ref[pl.ds(qs, bq), :])
            vp, vc = _pair_tiles(v_ref[pl.ds(ps, bq), :]), _pair_tiles(v_ref[pl.ds(qs, bq), :])
            zps, zcs = _dil_block_scores(qh, kp, kc, bias_ref, has_prev, band_prev, band_cur)
            ms = [jnp.maximum(jnp.max(zps[h], axis=1, keepdims=True), jnp.max(zcs[h], axis=1, keepdims=True))
                  for h in heads]
            eps = [jnp.exp(zps[h] - ms[h]) for h in heads]
            ecs = [jnp.exp(zcs[h] - ms[h]) for h in heads]
            pvs = [_dot(eps[h].astype(BF16), vp[h]) + _dot(ecs[h].astype(BF16), vc[h]) for h in heads]
            dens = [jnp.sum(eps[h], axis=1, keepdims=True) + jnp.sum(ecs[h], axis=1, keepdims=True) for h in heads]
            o_ref[pl.ds(qs, bq), :] = _merge_tiles([pvs[h] / dens[h] for h in heads], own)
            lse_ref[pl.ds(qs, bq), :] = _merge_tiles(
                [jnp.broadcast_to(ms[h] + jnp.log(dens[h]), (bq, LANES)) for h in heads], own)
            return 0

        lax.fori_loop(0, n_blk, block, 0)

    def spec(offset):
        return pl.BlockSpec((seq, DIL_WIDTH), lambda b, p: (b, offset + p))

    out = jax.ShapeDtypeStruct((t, n_pairs * DIL_WIDTH), F32)
    return _call(
        body, name=f"dil_fwd{dil}", grid=(nb, n_pairs), args=(qkv, qkv, qkv, bias),
        in_specs=[spec(0), spec(n_pairs), spec(2 * n_pairs),
                  pl.BlockSpec((DIL_HEADS, bq, 2 * bq), lambda b, p: (p, 0, 0))],
        out_specs=[spec(0), spec(0)], out_shape=[out, out],
        params=_params("arbitrary", "arbitrary"), exchange=exchange)


def _dil_bwd(qkv, bias, do, lse, delta, nb, seq, dil):
    t, width = qkv.shape
    n_pairs = (width // 3) // DIL_WIDTH
    bq = DIL_BLOCK
    n_blk = seq // bq
    per_seq = n_blk // dil
    scale = HEAD_DIM ** -0.5
    heads = range(DIL_HEADS)

    def body(q_ref, k_ref, v_ref, bias_ref, do_ref, lse_ref, dl_ref, dq_ref, dk_ref, dv_ref, db_ref,
             dk_acc, dv_acc):
        band_prev, band_cur = _dil_bands()
        own = _own_lanes()
        dk_acc[...] = jnp.zeros_like(dk_acc)
        dv_acc[...] = jnp.zeros_like(dv_acc)

        @pl.when(pl.program_id(1) == 0)
        def _():
            db_ref[...] = jnp.zeros_like(db_ref)

        def block(n, _):
            has_prev = (n & (per_seq - 1)) != 0
            qs = pl.multiple_of(n * bq, bq)
            ps = pl.multiple_of(jnp.maximum(n - 1, 0) * bq, bq)
            qh = _own_tiles(q_ref[pl.ds(qs, bq), :], own)
            kp, kc = _pair_tiles(k_ref[pl.ds(ps, bq), :]), _pair_tiles(k_ref[pl.ds(qs, bq), :])
            vp, vc = _pair_tiles(v_ref[pl.ds(ps, bq), :]), _pair_tiles(v_ref[pl.ds(qs, bq), :])
            doh = _own_tiles(do_ref[pl.ds(qs, bq), :].astype(BF16), own)
            lse_v, dl_v = lse_ref[pl.ds(qs, bq), :], dl_ref[pl.ds(qs, bq), :]
            zps, zcs = _dil_block_scores(qh, kp, kc, bias_ref, has_prev, band_prev, band_cur)
            dpp = [_dot_nt(doh[h], vp[h]) for h in heads]
            dpc = [_dot_nt(doh[h], vc[h]) for h in heads]
            lse_h = [lse_v[:, h * HEAD_DIM:h * HEAD_DIM + 1] for h in heads]
            dl_h = [dl_v[:, h * HEAD_DIM:h * HEAD_DIM + 1] for h in heads]
            pps = [jnp.exp(zps[h] - lse_h[h]) for h in heads]
            pcs = [jnp.exp(zcs[h] - lse_h[h]) for h in heads]
            dvp = [_dot_tn(pps[h].astype(BF16), doh[h]) for h in heads]
            dvc = [_dot_tn(pcs[h].astype(BF16), doh[h]) for h in heads]
            dzps = [pps[h] * (dpp[h] - dl_h[h]) for h in heads]
            dzcs = [pcs[h] * (dpc[h] - dl_h[h]) for h in heads]
            dzp_b = [(dzps[h] * scale).astype(BF16) for h in heads]
            dzc_b = [(dzcs[h] * scale).astype(BF16) for h in heads]
            dqs = [_dot(dzp_b[h], kp[h]) + _dot(dzc_b[h], kc[h]) for h in heads]
            dkp = [_dot_tn(dzp_b[h], qh[h]) for h in heads]
            dkc = [_dot_tn(dzc_b[h], qh[h]) for h in heads]
            for h in heads:
                db_ref[h, :, 0:bq] += dzps[h]
                db_ref[h, :, bq:2 * bq] += dzcs[h]
            def pair_sums(per_head):
                return jnp.concatenate([per_head[h] + per_head[h + 1] for h in heads[::2]], axis=1)

            dq_ref[pl.ds(qs, bq), :] = _merge_tiles(dqs, own).astype(BF16)
            dk_acc[pl.ds(ps, bq), :] += pair_sums(dkp)
            dk_acc[pl.ds(qs, bq), :] += pair_sums(dkc)
            dv_acc[pl.ds(ps, bq), :] += pair_sums(dvp)
            dv_acc[pl.ds(qs, bq), :] += pair_sums(dvc)
            return 0

        lax.fori_loop(0, n_blk, block, 0)
        dk_ref[...] = dk_acc[...].astype(BF16)
        dv_ref[...] = dv_acc[...].astype(BF16)

    def spec(offset):
        return pl.BlockSpec((seq, DIL_WIDTH), lambda p, b: (b, offset + p))

    bias_spec = pl.BlockSpec((DIL_HEADS, bq, 2 * bq), lambda p, b: (p, 0, 0))
    out = jax.ShapeDtypeStruct((t, n_pairs * DIL_WIDTH), BF16)
    return pl.pallas_call(
        body, name=f"dil_bwd{dil}", grid=(n_pairs, nb),
        in_specs=[spec(0), spec(n_pairs), spec(2 * n_pairs), bias_spec, spec(0), spec(0), spec(0)],
        out_specs=[spec(0), spec(0), spec(0), bias_spec],
        out_shape=[out, out, out, jax.ShapeDtypeStruct(bias.shape, F32)],
        scratch_shapes=[pltpu.VMEM((seq, DIL_WIDTH), F32), pltpu.VMEM((seq, DIL_WIDTH), F32)],
        compiler_params=_params("arbitrary", "arbitrary"),
    )(qkv, qkv, qkv, bias, do, lse, delta)


def _head_blocks(width):
    rows = lax.broadcasted_iota(jnp.int32, (width, width), 0) // HEAD_DIM
    cols = lax.broadcasted_iota(jnp.int32, (width, width), 1) // HEAD_DIM
    return (rows == cols).astype(BF16)


def _head_mean(v, gmat):
    return _split_dot(v, gmat) * (1.0 / HEAD_DIM)


def _residue_views(arrays, nb, seq):
    return [a if dil == 1 else a.reshape(nb, dil, seq // dil, a.shape[1]) for a, dil in zip(arrays, DILATIONS)]


def _mix_out_fwd(osb, ocs, lses, gsb, gdil, wout, x, mod, tm):
    t, d = x.shape
    ds = osb.shape[1]
    nt = t // tm
    nb = mod.shape[0]
    tpb = nt // nb
    seq = t // nb
    n_cfg = len(DILATIONS)

    def body(osb_ref, *refs):
        oc_refs, lse_refs = refs[:n_cfg], refs[n_cfg:2 * n_cfg]
        gsb_ref, gdil_ref, w_ref, x_ref, mod_ref = refs[2 * n_cfg:2 * n_cfg + 5]
        xo_ref, on_ref, m_ref, odil_ref = refs[2 * n_cfg + 5:2 * n_cfg + 9]
        ld_refs = refs[2 * n_cfg + 9:3 * n_cfg + 9]
        stages, sc = refs[3 * n_cfg + 9:]
        ocv, lsev = [oc_refs[0][...]], [lse_refs[0][...]]
        for i, dil in enumerate(DILATIONS[1:]):
            ocv.append(_from_residue_rows(oc_refs[i + 1], stages.at[2 * i], dil))
            lsev.append(_from_residue_rows(lse_refs[i + 1], stages.at[2 * i + 1], dil))
        top = functools.reduce(jnp.maximum, lsev)
        total = top + jnp.log(sum(jnp.exp(l - top) for l in lsev))
        odil = sum(jnp.exp(l - total) * o for o, l in zip(ocv, lsev))
        odil_ref[...] = odil
        ld_refs[0][...] = total
        _stage(total, sc)
        for ref, dil in zip(ld_refs[1:], DILATIONS[1:]):
            _to_residue_rows(sc, ref, dil)
        gm = _head_blocks(ds)
        parts = []
        for o, g_ref in ((osb_ref[...], gsb_ref), (odil, gdil_ref)):
            parts.append(o * lax.rsqrt(_head_mean(o * o, gm) + EPS) * g_ref[...])
        on = jnp.concatenate(parts, axis=1).astype(BF16)
        on_ref[...] = on
        m = _dot(on, w_ref[...])
        m_ref[...] = m
        xo_ref[...] = x_ref[...] + mod_ref[5:6, :] * m

    tok = pl.BlockSpec((tm, d), lambda i: (i, 0))
    hd = pl.BlockSpec((tm, ds), lambda i: (i, 0))
    res = [hd] + [_residue_spec(tm, tpb, ds, dil, lambda i: 0) for dil in DILATIONS[1:]]
    res_shape = [jax.ShapeDtypeStruct((t, ds), F32)] + [_residue_shape(nb, seq, ds, dil, F32) for dil in DILATIONS[1:]]
    gain = pl.BlockSpec((1, ds), lambda i: (0, 0))
    outs = pl.pallas_call(
        body, name="mix_out_fwd", grid=(nt,),
        in_specs=[hd] + res + res + [gain, gain,
                  pl.BlockSpec(wout.shape, lambda i: (0, 0)),
                  tok, pl.BlockSpec((None, N_MOD, d), lambda i: (i // tpb, 0, 0))],
        out_specs=[tok, pl.BlockSpec((tm, 2 * ds), lambda i: (i, 0)), tok, hd] + res,
        out_shape=[jax.ShapeDtypeStruct((t, d), F32), jax.ShapeDtypeStruct((t, 2 * ds), BF16),
                   jax.ShapeDtypeStruct((t, d), F32), jax.ShapeDtypeStruct((t, ds), F32)] + res_shape,
        scratch_shapes=[pltpu.VMEM((2 * (n_cfg - 1), ds // LANES, tm, LANES), F32), _stage_shape(tm, ds)],
        compiler_params=_params("arbitrary"),
    )(osb, *_residue_views(ocs, nb, seq), *_residue_views(lses, nb, seq), gsb, gdil, wout, x, mod)
    return outs[0], outs[1], outs[2], outs[3], [a.reshape(t, ds) for a in outs[4:]]


def _mix_out_bwd(dxo, m, mod, wout, osb, odil, gsb, gdil, tm):
    t, d = dxo.shape
    ds = osb.shape[1]
    nt = t // tm
    nb = mod.shape[0]
    tpb = nt // nb
    seq = t // nb
    n_cfg = len(DILATIONS)

    def body(dxo_ref, m_ref, mod_ref, w_ref, osb_ref, odil_ref, gsb_ref, gdil_ref,
             dm_ref, dosb_ref, *rest):
        do_refs, dl_refs = rest[:n_cfg], rest[n_cfg:2 * n_cfg]
        dmod_ref, dg_ref, sc = rest[2 * n_cfg:]
        dodil_ref, dldil_ref = do_refs[0], dl_refs[0]
        i = pl.program_id(0)
        dxo_v = dxo_ref[...]
        dm = (mod_ref[5:6, :] * dxo_v).astype(BF16)
        dm_ref[...] = dm
        dgt = jnp.sum(m_ref[...] * dxo_v, axis=0, keepdims=True)
        don = _dot_nt(dm, w_ref[...])
        gm = _head_blocks(ds)

        @pl.when(i % tpb == 0)
        def _():
            dmod_ref[...] = jnp.zeros_like(dmod_ref)

        @pl.when(i == 0)
        def _():
            dg_ref[...] = jnp.zeros_like(dg_ref)

        dmod_ref[2:3, :] += dgt
        groups = ((osb_ref, gsb_ref, dosb_ref), (odil_ref, gdil_ref, dodil_ref))
        for k, (o_ref, g_ref, do_ref) in enumerate(groups):
            o = o_ref[...]
            dn_out = don[:, k * ds:(k + 1) * ds]
            r = lax.rsqrt(_head_mean(o * o, gm) + EPS)
            n = o * r
            dg_ref[0:1, k * ds:(k + 1) * ds] += jnp.sum(dn_out * n, axis=0, keepdims=True)
            dn = dn_out * g_ref[...]
            do = r * (dn - n * _head_mean(dn * n, gm))
            do_ref[...] = do
            if k == 1:
                delta = _head_mean(do * o, gm) * float(HEAD_DIM)
                dldil_ref[...] = delta
                for value, refs in ((do, do_refs), (delta, dl_refs)):
                    _stage(value, sc)
                    for ref, dil in zip(refs[1:], DILATIONS[1:]):
                        _to_residue_rows(sc, ref, dil)

    tok = pl.BlockSpec((tm, d), lambda i: (i, 0))
    hd = pl.BlockSpec((tm, ds), lambda i: (i, 0))
    res = [hd] + [_residue_spec(tm, tpb, ds, dil, lambda i: 0) for dil in DILATIONS[1:]]
    res_shape = [jax.ShapeDtypeStruct((t, ds), F32)] + [_residue_shape(nb, seq, ds, dil, F32) for dil in DILATIONS[1:]]
    gain = pl.BlockSpec((1, ds), lambda i: (0, 0))
    outs = pl.pallas_call(
        body, name="mix_out_bwd", grid=(nt,),
        in_specs=[tok, tok, pl.BlockSpec((None, N_MOD, d), lambda i: (i // tpb, 0, 0)),
                  pl.BlockSpec(wout.shape, lambda i: (0, 0)), hd, hd, gain, gain],
        out_specs=[tok, hd] + res + res
        + [pl.BlockSpec((None, 8, d), lambda i: (i // tpb, 0, 0)), pl.BlockSpec((8, 2 * ds), lambda i: (0, 0))],
        out_shape=[jax.ShapeDtypeStruct((t, d), BF16), jax.ShapeDtypeStruct((t, ds), F32)] + res_shape + res_shape
        + [jax.ShapeDtypeStruct((nb, 8, d), F32), jax.ShapeDtypeStruct((8, 2 * ds), F32)],
        scratch_shapes=[_stage_shape(tm, ds)],
        compiler_params=_params("arbitrary"),
    )(dxo, m, mod, wout, osb, odil, gsb, gdil)
    flat = [a.reshape(t, ds) for a in outs[2:2 + 2 * n_cfg]]
    return outs[0], outs[1], flat[:n_cfg], flat[n_cfg:], outs[-2], outs[-1]


def _merge_dqkv(sb_parts, dil_parts, nb, tm):
    t, ds = sb_parts[0].shape
    nt = t // tm
    tpb = nt // nb
    seq = t // nb
    n_cfg = len(DILATIONS)

    def body(*refs):
        sb_refs, dil_refs = refs[:3], refs[3:3 + 3 * n_cfg]
        o_ref, sc = refs[3 + 3 * n_cfg:]
        for k in range(3):
            o_ref[:, k * ds:(k + 1) * ds] = sb_refs[k][...]
            total = dil_refs[k * n_cfg][...].astype(F32)
            for i, dil in enumerate(DILATIONS[1:]):
                total = total + _from_residue_rows(dil_refs[k * n_cfg + i + 1], sc, dil)
            o_ref[:, (3 + k) * ds:(4 + k) * ds] = total.astype(BF16)

    hd = pl.BlockSpec((tm, ds), lambda i: (i, 0))
    res = [hd] + [_residue_spec(tm, tpb, ds, dil, lambda i: 0) for dil in DILATIONS[1:]]
    views = [v for parts in dil_parts for v in _residue_views(parts, nb, seq)]
    return pl.pallas_call(
        body, name="merge_dqkv", grid=(nt,),
        in_specs=[hd] * 3 + res * 3,
        out_specs=pl.BlockSpec((tm, 6 * ds), lambda i: (i, 0)),
        out_shape=jax.ShapeDtypeStruct((t, 6 * ds), BF16),
        scratch_shapes=[_stage_shape(tm, ds)],
        compiler_params=_params("arbitrary"),
    )(*sb_parts, *views)


def _loss_head(x, target, g, tm):
    t, d = x.shape

    def body(x_ref, t_ref, g_ref, dx_ref, acc_ref):
        @pl.when(pl.program_id(0) == 0)
        def _():
            acc_ref[...] = jnp.zeros_like(acc_ref)

        n, r = _norm(x_ref[...])
        gv = g_ref[...]
        err = n * gv - t_ref[...]
        dy = err * (1.0 / d)
        acc_ref[0:1, :] += jnp.sum(err * err, axis=0, keepdims=True)
        acc_ref[1:2, :] += jnp.sum(dy * n, axis=0, keepdims=True)
        dn = dy * gv
        dx_ref[...] = r * (dn - n * jnp.mean(dn * n, axis=-1, keepdims=True))

    tok = pl.BlockSpec((tm, d), lambda i: (i, 0))
    return pl.pallas_call(
        body, name="loss_head", grid=(t // tm,),
        in_specs=[tok, tok, pl.BlockSpec((1, d), lambda i: (0, 0))],
        out_specs=[tok, pl.BlockSpec((8, d), lambda i: (0, 0))],
        out_shape=[jax.ShapeDtypeStruct((t, d), F32), jax.ShapeDtypeStruct((8, d), F32)],
        compiler_params=_params("arbitrary"),
    )(x, target, g)


def _row_tile(rows):
    if rows <= 256:
        return rows
    for cand in range(256, 15, -16):
        if rows % cand == 0:
            return cand
    return rows


def _adamw(w, parts, m, v, name, transposed=False):
    rows, cols = w.shape
    n_parts = parts.shape[0]
    tr = _row_tile(rows)
    c1 = 1.0 / (1.0 - ADAM_B1 ** ADAM_STEP)
    c2 = 1.0 / (1.0 - ADAM_B2 ** ADAM_STEP)

    def body(w_ref, p_ref, m_ref, v_ref, g_ref, d_ref, nm_ref, nv_ref):
        g = p_ref[0].astype(F32)
        for i in range(1, n_parts):
            g = g + p_ref[i].astype(F32)
        wv, mv, vv = w_ref[...], m_ref[...], v_ref[...]
        if transposed:
            wv, mv, vv = wv.T, mv.T, vv.T
        nm = ADAM_B1 * mv + (1.0 - ADAM_B1) * g
        nv = ADAM_B2 * vv + (1.0 - ADAM_B2) * (g * g)
        g_ref[...] = g
        nm_ref[...] = nm
        nv_ref[...] = nv
        d_ref[...] = -ADAM_LR * ((nm * c1) / (jnp.sqrt(nv * c2) + ADAM_EPS) + ADAM_WD * wv)

    blk = pl.BlockSpec((tr, cols), lambda i: (i, 0))
    if transposed:
        oblk = pl.BlockSpec((cols, tr), lambda i: (0, i))
        pblk = pl.BlockSpec((n_parts, cols, tr), lambda i: (0, 0, i))
        out = jax.ShapeDtypeStruct((cols, rows), F32)
    else:
        oblk, pblk = blk, pl.BlockSpec((n_parts, tr, cols), lambda i: (0, i, 0))
        out = jax.ShapeDtypeStruct((rows, cols), F32)
    return pl.pallas_call(
        body, name=name, grid=(rows // tr,),
        in_specs=[blk, pblk, blk, blk],
        out_specs=[oblk, oblk, oblk, oblk], out_shape=[out, out, out, out],
        compiler_params=_params("arbitrary"),
    )(w, parts, m, v)


def _t5_bucket(n):
    max_exact = N_BUCKETS // 2
    nf = np.maximum(n, 1).astype(np.float32)
    large = max_exact + (np.log(nf / max_exact) / math.log(MAX_DISTANCE / max_exact)
                         * (N_BUCKETS - max_exact)).astype(np.int32)
    large = np.minimum(large, N_BUCKETS - 1)
    return np.where(n < max_exact, n, large).astype(np.int32)


def _bucket_onehot():
    table = np.zeros((len(DILATIONS), 2 * DIL_BLOCK + 1, N_BUCKETS), np.float32)
    for i, dil in enumerate(DILATIONS):
        buckets = _t5_bucket(np.arange(DIL_BLOCK + 1) * dil)
        for m in range(DIL_BLOCK + 1):
            table[i, m, buckets[DIL_BLOCK - m]] = 1.0
    return table


def _bias_blocks(rel_bias):
    row = jnp.einsum("cmn,nh->chm", _bucket_onehot(), rel_bias, precision=lax.Precision.HIGHEST)
    n_cfg, n_heads, width = row.shape
    tiled = jnp.tile(row, (1, 1, DIL_BLOCK))[..., :DIL_BLOCK * (width - 1)]
    return tiled.reshape(n_cfg, n_heads, DIL_BLOCK, width - 1)


def _bias_blocks_bwd(dblocks):
    n_cfg, n_heads = dblocks.shape[:2]
    width = 2 * DIL_BLOCK + 1
    flat = dblocks.reshape(n_cfg, n_heads, DIL_BLOCK * (width - 1))
    flat = jnp.pad(flat, ((0, 0), (0, 0), (0, DIL_BLOCK)))
    drow = jnp.sum(flat.reshape(n_cfg, n_heads, DIL_BLOCK, width), axis=2)
    return jnp.einsum("chm,cmn->nh", drow, _bucket_onehot(), precision=lax.Precision.HIGHEST)


def _pad_to(a, axis, size):
    pad = [(0, 0)] * a.ndim
    pad[axis] = (0, size - a.shape[axis])
    return jnp.pad(a, pad)


def _lane_pad(n):
    return -(-n // LANES) * LANES


def _local_step(x, target, mod, gains, weights, rel_bias, tm, distributed):
    nb, seq, d = x.shape
    t = nb * seq
    g_ffn1, g_mix, g_sb, g_dil, g_ffn2, g_final = gains
    wg1, wu1, wd1 = weights[:3]
    x0 = x.reshape(t, d)
    ds = g_sb.shape[1]
    bias = _bias_blocks(rel_bias)

    def beside(arrays, scatter):
        return _Exchange(arrays, scatter) if distributed else None

    tp, tg = min(PROJ_TILE, seq), min(GRAD_TILE, t)

    (x1, f1, gate1, up1), got = _ffn_fwd(x0, mod, g_ffn1, wg1, wu1, wd1, 0, tp, beside(weights[3:5], False))
    win, wout = got if distributed else weights[3:5]
    wout2 = wout.reshape(-1, d)
    (qkv, qkvd, h2), got = _qkv_fwd(x1, mod, g_mix, win, tp, beside(weights[7:8], False))
    wd2 = got[0] if distributed else weights[7]
    (osb, csb), got = _sb_fwd(qkv, nb, seq, beside(weights[5:7], False))
    wg2, wu2 = got if distributed else weights[5:7]
    ocs, lses = [], []
    for i, dil in enumerate(DILATIONS):
        (oc, lse), _ = _dil_fwd(qkvd[i], bias[i], nb, seq, dil)
        ocs.append(oc)
        lses.append(lse)
    x2, on, mix, odil, ldil = _mix_out_fwd(osb, ocs, lses, g_sb, g_dil, wout2, x1, mod, tm)
    (x3, f3, gate3, up3), _ = _ffn_fwd(x2, mod, g_ffn2, wg2, wu2, wd2, 2, tp)
    dx3, head = _loss_head(x3, target.reshape(t, d), g_final, tm)
    loss_sum = 0.5 * jnp.sum(head[0]) / d
    dg_final = head[1:2]

    (dx2, dgate3, dup3, act3, h3, df3, dmod3, dg_ffn2), _ = _ffn_bwd(
        dx3, x2, f3, mod, g_ffn2, gate3, up3, wg2, wu2, wd2, 2, tp)
    gwg2, gwu2, gwd2 = _ffn_weight_grads(h3, dgate3, dup3, act3, df3, tg, 2)

    dm, dosb, dodil, dldil, dmod2b, dg_heads = _mix_out_bwd(
        dx2, mix, mod, wout2, osb, odil, g_sb, g_dil, tm)
    n_out = wout.shape[0]
    gwout = _mm_tn(on, dm,
                   pl.BlockSpec((tg, wout.shape[1]), lambda i, j: (i, j)),
                   pl.BlockSpec((tg, d), lambda i, j: (i, 0)),
                   wout.shape, t // tg, "grad_wout")

    (dq_sb, dk_sb, dv_sb), parts_late = _sb_bwd(qkv, dosb, csb, nb, seq,
                                                beside([gwout, gwg2, gwu2, gwd2], True))
    dil_grads = [_dil_bwd(qkvd[i], bias[i], dodil[i], ldil[i], dldil[i], nb, seq, dil)
                 for i, dil in enumerate(DILATIONS)]
    dqkv = _merge_dqkv([dq_sb, dk_sb, dv_sb], [[g[k] for g in dil_grads] for k in range(3)], nb, tm)
    drel = _bias_blocks_bwd(jnp.stack([g[3] for g in dil_grads]))

    cs = win.shape[2]
    gwin = _mm_tn(h2, dqkv,
                  pl.BlockSpec((tg, d), lambda i, j: (i, 0)),
                  pl.BlockSpec((tg, cs), lambda i, j: (i, j)),
                  win.shape, t // tg, "grad_win")
    (dx1, dmod2a, dg_mix), parts_mid = _qkv_bwd(dqkv, dx2, x1, mod, g_mix, win, tp, beside([gwin], True))

    (dx0, dgate1, dup1, act1, h1, df1, dmod1, dg_ffn1), _ = _ffn_bwd(
        dx1, x0, f1, mod, g_ffn1, gate1, up1, wg1, wu1, wd1, 0, tp)
    dmod = jnp.concatenate([dmod1[:, 0:3], dmod2a[:, 0:2], dmod2b[:, 2:3], dmod3[:, 0:3]], axis=1)
    ggrads = (dg_ffn1[0:1], dg_mix[0:1], dg_heads[0:1], drel, dg_ffn2[0:1], dg_final)
    if not distributed:
        gw1 = _ffn_weight_grads(h1, dgate1, dup1, act1, df1, tg, 0)
        return loss_sum, dx0.reshape(nb, seq, d), tuple(gw1) + (gwin, gwout, gwg2, gwu2, gwd2), dmod, ggrads

    dg_heads_row, drel_flat = dg_heads[0:1], drel.reshape(1, -1)
    width = max(d, dg_heads_row.shape[1], drel_flat.shape[1])
    small = jnp.concatenate(
        [_pad_to(a.reshape(1, -1), 1, width)
         for a in (dg_ffn1[0:1], dg_mix[0:1], dg_ffn2[0:1], dg_final, dg_heads_row, drel_flat, loss_sum)]
        + [jnp.zeros((1, width), F32)], axis=0)
    dmod_pad = _pad_to(dmod.reshape(nb, N_MOD * d), 0, 8)
    everyone = _Exchange([jnp.broadcast_to(dmod_pad, (N_DEV,) + dmod_pad.shape),
                          jnp.broadcast_to(small, (N_DEV,) + small.shape)], True)
    sent_g, sent_u, gwd1, (dmod_all, small_all) = _ffn_weight_grads(
        h1, dgate1, dup1, act1, df1, tg, 0, stream=True, first=everyone)
    wgrads = (sent_g, sent_u, gwd1) + tuple(parts_mid + parts_late)
    return dx0.reshape(nb, seq, d), wgrads, dmod_all, small_all


def kernel(x, c, w_ada, b_ada, g_ffn1, w1_gate, w1_up, w1_down, g_mix, w_in, g_sb_out, g_dil_out, w_out, rel_bias, g_ffn2, w2_gate, w2_up, w2_down, g_final, loss_target, m_w_ada, m_b_ada, m_g_ffn1, m_w1_gate, m_w1_up, m_w1_down, m_g_mix, m_w_in, m_g_sb_out, m_g_dil_out, m_w_out, m_rel_bias, m_g_ffn2, m_w2_gate, m_w2_up, m_w2_down, m_g_final, v_w_ada, v_b_ada, v_g_ffn1, v_w1_gate, v_w1_up, v_w1_down, v_g_mix, v_w_in, v_g_sb_out, v_g_dil_out, v_w_out, v_rel_bias, v_g_ffn2, v_w2_gate, v_w2_up, v_w2_down, v_g_final):
    nb, seq, d = x.shape
    me = 4 * lax.axis_index("x") + 2 * lax.axis_index("y") + lax.axis_index("c")
    tm = min(TOKEN_TILE, seq)
    fs = w1_gate.shape[2]
    fs_pad = _lane_pad(fs)
    ada_cols = w_ada.shape[2]

    def col_shard(w):
        return _pad_to(w[0].astype(BF16), 1, fs_pad)

    def row_shard(w):
        return _pad_to(w[0].astype(BF16), 0, fs_pad)

    shards = [col_shard(w1_gate), col_shard(w1_up), row_shard(w1_down), w_in[0].astype(BF16),
              w_out[0].astype(BF16), col_shard(w2_gate), col_shard(w2_up), row_shard(w2_down)]
    b_cols = lax.dynamic_slice(b_ada, (0, me * ada_cols), (1, ada_cols))
    c_every, mod_all, first = _first_exchange(_pad_to(c, 0, 8), shards[:3], w_ada[0], b_cols)
    c_all = c_every[:, :nb].reshape(N_DEV * nb, d)
    weights = first + shards[3:]
    mod = lax.dynamic_slice(mod_all, (0, me * 8, 0), (N_DEV, nb, ada_cols))
    mod = mod.transpose(1, 0, 2).reshape(nb, N_MOD, d)

    n_sb = g_sb_out.shape[1] * g_sb_out.shape[2]
    gains = (g_ffn1, g_mix, g_sb_out.reshape(1, n_sb), g_dil_out.reshape(1, -1), g_ffn2,
             g_final.reshape(1, d))
    grad_x, parts, dmod_all, small_all = _local_step(x, loss_target, mod, gains, weights, rel_bias, tm, True)

    last_part = _exchange([parts[2]], True, "scatter_last", chips=[True])[0]
    parts = parts[:2] + (last_part,) + parts[3:]
    dmod_all = dmod_all[:, :nb].reshape(N_DEV * nb, N_MOD * d)
    dmod_cols = lax.dynamic_slice(dmod_all, (0, me * ada_cols), (N_DEV * nb, ada_cols))
    gw_ada, gb_ada = _ada_bwd(c_all, dmod_cols, dmod_all)

    def small_part(row, size, shape):
        return small_all[:, row, :size].reshape((N_DEV,) + shape)

    loss = jnp.sum(small_all[:, 6, 0])

    n_rel = rel_bias.shape
    updates = {
        "w_ada": (w_ada[0], gw_ada[None], m_w_ada[0], v_w_ada[0]),
        "b_ada": (b_ada, gb_ada[None], m_b_ada, v_b_ada),
        "g_ffn1": (g_ffn1, small_part(0, d, (1, d)), m_g_ffn1, v_g_ffn1),
        "w1_gate": (w1_gate[0], parts[0], m_w1_gate[0], v_w1_gate[0]),
        "w1_up": (w1_up[0], parts[1], m_w1_up[0], v_w1_up[0]),
        "w1_down": (w1_down[0], parts[2], m_w1_down[0], v_w1_down[0]),
        "g_mix": (g_mix, small_part(1, d, (1, d)), m_g_mix, v_g_mix),
        "w_in": (w_in[0], parts[3], m_w_in[0], v_w_in[0]),
        "g_sb_out": (g_sb_out[0], small_all[:, 4, :n_sb].reshape((N_DEV,) + g_sb_out.shape[1:]),
                     m_g_sb_out[0], v_g_sb_out[0]),
        "g_dil_out": (g_dil_out[0], small_all[:, 4, n_sb:n_sb + g_dil_out[0].size].reshape((N_DEV,) + g_dil_out.shape[1:]),
                      m_g_dil_out[0], v_g_dil_out[0]),
        "w_out": (w_out[0], parts[4], m_w_out[0], v_w_out[0]),
        "rel_bias": (rel_bias, small_part(5, rel_bias.size, n_rel), m_rel_bias, v_rel_bias),
        "g_ffn2": (g_ffn2, small_part(2, d, (1, d)), m_g_ffn2, v_g_ffn2),
        "w2_gate": (w2_gate[0], parts[5], m_w2_gate[0], v_w2_gate[0]),
        "w2_up": (w2_up[0], parts[6], m_w2_up[0], v_w2_up[0]),
        "w2_down": (w2_down[0], parts[7], m_w2_down[0], v_w2_down[0]),
        "g_final": (g_final.reshape(1, d), small_part(3, d, (1, d)), m_g_final.reshape(1, d), v_g_final.reshape(1, d)),
    }
    shapes = {"w_ada": w_ada.shape, "b_ada": b_ada.shape, "g_ffn1": g_ffn1.shape, "w1_gate": w1_gate.shape,
              "w1_up": w1_up.shape, "w1_down": w1_down.shape, "g_mix": g_mix.shape, "w_in": w_in.shape,
              "g_sb_out": g_sb_out.shape, "g_dil_out": g_dil_out.shape, "w_out": w_out.shape,
              "rel_bias": rel_bias.shape, "g_ffn2": g_ffn2.shape, "w2_gate": w2_gate.shape,
              "w2_up": w2_up.shape, "w2_down": w2_down.shape, "g_final": g_final.shape}
    grads, deltas, new_m, new_v = [], [], [], []
    for name, (w, p, m, v) in updates.items():
        transposed = name in ("w1_gate", "w1_up", "w2_gate", "w2_up")
        outs = _adamw(w, p, m, v, f"adamw_{name}", transposed)
        for dst, a in zip((grads, deltas, new_m, new_v), outs):
            dst.append((a.T if transposed else a).reshape(shapes[name]))
    return (loss, grad_x, *grads, *deltas, *new_m, *new_v)
```

```python
import functools
import math

import numpy as np
import jax
import jax.numpy as jnp
from jax import lax
from jax.experimental import pallas as pl
from jax.experimental.pallas import tpu as pltpu

F32 = jnp.float32
BF16 = jnp.bfloat16

EPS = 1e-6
NEG_INF = -1e30
HEAD_DIM = 64
LANES = 128
DIL_BLOCK = 128
DILATIONS = (1, 4, 16)
N_BUCKETS = 32
MAX_DISTANCE = 2048
N_MOD = 9
N_DEV = 8
SB_BLOCK = 256
SB_HEADS = 4
SB_WIDTH = SB_HEADS * HEAD_DIM
DIL_HEADS = 4
DIL_WIDTH = DIL_HEADS * HEAD_DIM
TOKEN_TILE = 512
PROJ_TILE = 1024
GRAD_TILE = 1024
FFN_CHUNKS = 2
VMEM_LIMIT_BYTES = 56 * 1024 * 1024

ADAM_LR = 0.001
ADAM_B1 = 0.9
ADAM_B2 = 0.999
ADAM_EPS = 1e-08
ADAM_WD = 0.01
ADAM_STEP = 10

NT_DIMS = (((1,), (1,)), ((), ()))
TN_DIMS = (((0,), (0,)), ((), ()))


def _params(*sem):
    return pltpu.CompilerParams(dimension_semantics=sem, vmem_limit_bytes=VMEM_LIMIT_BYTES)


def _once(spec):
    return pl.BlockSpec(spec.block_shape, spec.index_map, pipeline_mode=pl.Buffered(1))


def _dot(a, b):
    return jnp.dot(a, b, preferred_element_type=F32)


def _dot_nt(a, b):
    return lax.dot_general(a, b, NT_DIMS, preferred_element_type=F32)


def _dot_tn(a, b):
    return lax.dot_general(a, b, TN_DIMS, preferred_element_type=F32)


def _split_dot(a, b):
    hi = a.astype(BF16)
    lo = (a - hi.astype(F32)).astype(BF16)
    return _dot(hi, b) + _dot(lo, b)


def _sigmoid(z):
    return 1.0 / (1.0 + jnp.exp(-z))


def _norm(x):
    r = lax.rsqrt(jnp.mean(x * x, axis=-1, keepdims=True) + EPS)
    return x * r, r


def _modulate(x, g, mod_ref, k):
    n, _ = _norm(x)
    shift = mod_ref[3 * k:3 * k + 1, :]
    scale = mod_ref[3 * k + 1:3 * k + 2, :]
    return n * g * (1.0 + scale) + shift


def _modulate_bwd(dh, x, g, mod_ref, k):
    n, r = _norm(x)
    scale = mod_ref[3 * k + 1:3 * k + 2, :]
    dshift = jnp.sum(dh, axis=0, keepdims=True)
    dscale = jnp.sum(dh * n * g, axis=0, keepdims=True)
    dg = jnp.sum(dh * n * (1.0 + scale), axis=0, keepdims=True)
    dn = dh * g * (1.0 + scale)
    dx = r * (dn - n * jnp.mean(dn * n, axis=-1, keepdims=True))
    return dx, dshift, dscale, dg


class _Exchange:
    def __init__(self, arrays, scatter, relay=False, chips=None):
        assert not (scatter and relay)
        self.arrays = list(arrays)
        self.scatter = scatter
        self.relay = relay
        self.n = len(self.arrays)
        self.chips = list(chips) if chips is not None else [False] * self.n
        assert scatter or not any(self.chips)
        self.out_shape = [
            jax.ShapeDtypeStruct((N_DEV // 2 if ch else N_DEV,) + tuple(a.shape[1:] if scatter else a.shape), a.dtype)
            for a, ch in zip(self.arrays, self.chips)]
        n_remote = self.n * (N_DEV - 1)
        self.scratch_shapes = [pltpu.SemaphoreType.DMA((n_remote,)), pltpu.SemaphoreType.DMA((n_remote,)),
                               pltpu.SemaphoreType.DMA((self.n,))]

    def _copies(self, in_refs, out_refs, sems):
        send_sems, recv_sems, local_sems = sems
        x, y, c = lax.axis_index("x"), lax.axis_index("y"), lax.axis_index("c")
        me = 4 * x + 2 * y + c
        local, remote, relayed = [], {}, {}
        for a in range(self.n):
            if self.chips[a]:
                mine = 2 * x + y
                local.append(pltpu.make_async_copy(in_refs[a].at[mine], out_refs[a].at[mine], local_sems.at[a]))
                for k in (2, 4, 6):
                    px = 1 - x if k & 4 else x
                    py = 1 - y if k & 2 else y
                    sem = a * (N_DEV - 1) + k - 1
                    remote[a, k] = pltpu.make_async_remote_copy(
                        src_ref=in_refs[a].at[2 * px + py], dst_ref=out_refs[a].at[mine],
                        send_sem=send_sems.at[sem], recv_sem=recv_sems.at[sem],
                        device_id=(px, py, c), device_id_type=pl.DeviceIdType.MESH)
                continue
            src = in_refs[a].at[me] if self.scatter else in_refs[a]
            local.append(pltpu.make_async_copy(src, out_refs[a].at[me], local_sems.at[a]))
            for k in range(1, N_DEV):
                px = 1 - x if k & 4 else x
                py = 1 - y if k & 2 else y
                pc = 1 - c if k & 1 else c
                sem = a * (N_DEV - 1) + k - 1
                if self.relay and k & 1 and k > 1:
                    slot = 4 * px + 2 * py + c
                    relayed[a, k] = pltpu.make_async_remote_copy(
                        src_ref=out_refs[a].at[slot], dst_ref=out_refs[a].at[slot],
                        send_sem=send_sems.at[sem], recv_sem=recv_sems.at[sem],
                        device_id=(x, y, 1 - c), device_id_type=pl.DeviceIdType.MESH)
                    continue
                src = in_refs[a].at[4 * px + 2 * py + pc] if self.scatter else in_refs[a]
                remote[a, k] = pltpu.make_async_remote_copy(
                    src_ref=src, dst_ref=out_refs[a].at[me],
                    send_sem=send_sems.at[sem], recv_sem=recv_sems.at[sem],
                    device_id=(px, py, pc), device_id_type=pl.DeviceIdType.MESH)
        return local, remote, relayed

    def start(self, in_refs, out_refs, sems):
        local, remote, _ = self._copies(in_refs, out_refs, sems)
        for cp in local + list(remote.values()):
            cp.start()

    def wait(self, in_refs, out_refs, sems):
        local, remote, relayed = self._copies(in_refs, out_refs, sems)
        for (a, k), cp in relayed.items():
            remote[a, k - 1].wait_recv()
            cp.start()
        for (a, k), cp in remote.items():
            if (a, k + 1) not in relayed:
                cp.wait_recv()
        for cp in relayed.values():
            cp.wait_recv()
        for cp in list(remote.values()) + list(relayed.values()):
            cp.wait_send()
        for cp in local:
            cp.wait()


def _call(body, *, name, args, in_specs, out_specs, out_shape, scratch_shapes=(), grid=(),
          params=None, exchange=None):
    n_in, n_out = len(args), len(out_shape)
    if exchange is None:
        outs = pl.pallas_call(
            body, name=name, grid=grid, in_specs=list(in_specs), out_specs=list(out_specs),
            out_shape=list(out_shape), scratch_shapes=list(scratch_shapes), compiler_params=params,
        )(*args)
        return list(outs), []
    n_ex = exchange.n

    def wrapped(*refs):
        ins, refs = refs[:n_in], refs[n_in:]
        ex_in, refs = refs[:n_ex], refs[n_ex:]
        outs, refs = refs[:n_out], refs[n_out:]
        ex_out, refs = refs[:n_ex], refs[n_ex:]
        scratch, sems = refs[:len(refs) - 3], refs[len(refs) - 3:]
        if not grid:
            exchange.start(ex_in, ex_out, sems)
            body(*ins, *outs, *scratch)
            exchange.wait(ex_in, ex_out, sems)
            return
        first = functools.reduce(jnp.logical_and, [pl.program_id(a) == 0 for a in range(len(grid))])
        last = functools.reduce(jnp.logical_and, [pl.program_id(a) == grid[a] - 1 for a in range(len(grid))])

        @pl.when(first)
        def _():
            exchange.start(ex_in, ex_out, sems)

        body(*ins, *outs, *scratch)

        @pl.when(last)
        def _():
            exchange.wait(ex_in, ex_out, sems)

    any_spec = pl.BlockSpec(memory_space=pl.ANY)
    outs = pl.pallas_call(
        wrapped, name=name, grid=grid,
        in_specs=list(in_specs) + [any_spec] * n_ex, out_specs=list(out_specs) + [any_spec] * n_ex,
        out_shape=list(out_shape) + exchange.out_shape,
        scratch_shapes=list(scratch_shapes) + exchange.scratch_shapes, compiler_params=params,
    )(*args, *exchange.arrays)
    return list(outs[:n_out]), list(outs[n_out:])


def _exchange(arrays, scatter, name, relay=False, chips=None):
    return _call(lambda: None, name=name, args=(), in_specs=(), out_specs=(), out_shape=(),
                 exchange=_Exchange(arrays, scatter, relay, chips))[1]


def _first_exchange(c_pad, shards, w, b):
    rows, d = c_pad.shape
    cols = w.shape[1]
    ex_c = _Exchange([c_pad], False)
    ex_w = _Exchange(shards, False, relay=True)
    ex_m = _Exchange([jax.ShapeDtypeStruct((N_DEV * rows, cols), F32)], False)
    n_w = ex_w.n

    def body(*refs):
        c_ref, w_refs, wa_ref, b_ref = refs[0], refs[1:1 + n_w], refs[1 + n_w], refs[2 + n_w]
        outs = refs[3 + n_w:]
        cg_ref, wg_refs, mg_ref = outs[0], outs[1:1 + n_w], outs[1 + n_w]
        scratch = outs[2 + n_w:]
        sems_c, sems_w, sems_m, c_vm, m_vm = scratch[0:3], scratch[3:6], scratch[6:9], scratch[9], scratch[10]
        ex_c.start([c_ref], [cg_ref], sems_c)
        ex_c.wait([c_ref], [cg_ref], sems_c)
        pltpu.sync_copy(cg_ref, c_vm)
        cv = c_vm[...].reshape(N_DEV * rows, d)
        s = (cv * _sigmoid(cv)).astype(BF16)
        m_vm[...] = _dot(s, wa_ref[...].astype(BF16)) + b_ref[...]
        ex_m.start([m_vm], [mg_ref], sems_m)
        ex_w.start(w_refs, wg_refs, sems_w)
        ex_m.wait([m_vm], [mg_ref], sems_m)
        ex_w.wait(w_refs, wg_refs, sems_w)

    any_spec = pl.BlockSpec(memory_space=pl.ANY)
    vmem_spec = pl.BlockSpec(memory_space=pltpu.VMEM)
    outs = pl.pallas_call(
        body, name="first_exchange",
        in_specs=[any_spec] * (1 + n_w) + [vmem_spec, vmem_spec],
        out_specs=[any_spec] * (2 + n_w),
        out_shape=ex_c.out_shape + ex_w.out_shape + ex_m.out_shape,
        scratch_shapes=ex_c.scratch_shapes + ex_w.scratch_shapes + ex_m.scratch_shapes
        + [pltpu.VMEM((N_DEV, rows, d), F32), pltpu.VMEM((N_DEV * rows, cols), F32)],
        compiler_params=pltpu.CompilerParams(vmem_limit_bytes=VMEM_LIMIT_BYTES),
    )(c_pad, *shards, w, b)
    return outs[0], outs[1 + n_w], list(outs[1:1 + n_w])


def _ada_bwd(c_all, dmod_cols, dmod_all):
    def body(c_ref, dc_ref, da_ref, gw_ref, gb_ref):
        cv = c_ref[...]
        s = cv * _sigmoid(cv)
        gw_ref[...] = lax.dot_general(s, dc_ref[...], TN_DIMS, preferred_element_type=F32,
                                      precision=lax.Precision.HIGHEST)
        gb_ref[...] = jnp.sum(da_ref[...], axis=0, keepdims=True)

    return pl.pallas_call(
        body, name="ada_bwd",
        out_shape=(jax.ShapeDtypeStruct((c_all.shape[1], dmod_cols.shape[1]), F32),
                   jax.ShapeDtypeStruct((1, dmod_all.shape[1]), F32)),
        compiler_params=pltpu.CompilerParams(vmem_limit_bytes=VMEM_LIMIT_BYTES),
    )(c_all, dmod_cols, dmod_all)


def _loss_tile(x, target, g, acc_ref):
    d = x.shape[1]
    n, r = _norm(x)
    err = n * g - target
    dy = err * (1.0 / d)
    acc_ref[0:1, :] += jnp.sum(err * err, axis=0, keepdims=True)
    acc_ref[1:2, :] += jnp.sum(dy * n, axis=0, keepdims=True)
    dn = dy * g
    return r * (dn - n * jnp.mean(dn * n, axis=-1, keepdims=True))


def _ffn_fwd(x, mod, g, wg, wu, wd, k, tm, exchange=None, head=None):
    t, d = x.shape
    ns, _, fs = wg.shape
    nt = t // tm
    tpb = nt // mod.shape[0]
    rows = tm // FFN_CHUNKS
    extra = list(head) if head is not None else []

    def body(x_ref, mod_ref, g_ref, wg_ref, wu_ref, wd_ref, *rest):
        if head is not None:
            t_ref, gf_ref, xo_ref, f_ref, gg_ref, uu_ref, head_ref, h_sc, acc = rest
        else:
            xo_ref, f_ref, gg_ref, uu_ref, h_sc, acc = rest
        i, j = pl.program_id(0), pl.program_id(1)

        @pl.when(j == 0)
        def _():
            h_sc[...] = _modulate(x_ref[...], g_ref[...], mod_ref, k).astype(BF16)
            acc[...] = jnp.zeros_like(acc)

        chunks = [pl.ds(c * rows, rows) for c in range(FFN_CHUNKS)]
        wg, wu, wd = wg_ref[...], wu_ref[...], wd_ref[...]
        gates, ups = [], []
        for rs in chunks:
            h = h_sc[rs, :]
            gates.append(_dot(h, wg))
            ups.append(_dot(h, wu))
        acts = [(g * _sigmoid(g) * u).astype(BF16) for g, u in zip(gates, ups)]
        for rs, g, u in zip(chunks, gates, ups):
            gg_ref[rs, :] = g.astype(BF16)
            uu_ref[rs, :] = u.astype(BF16)
        downs = [_dot(a, wd) for a in acts]
        for rs, dn in zip(chunks, downs):
            acc[rs, :] += dn

        @pl.when(j == ns - 1)
        def _():
            f = acc[...]
            f_ref[...] = f.astype(BF16)
            xo = x_ref[...] + 0.5 * mod_ref[3 * k + 2:3 * k + 3, :] * f
            if head is None:
                xo_ref[...] = xo
            else:
                @pl.when(i == 0)
                def _():
                    head_ref[...] = jnp.zeros_like(head_ref)

                xo_ref[...] = _loss_tile(xo, t_ref[...], gf_ref[...], head_ref)

    tok = pl.BlockSpec((tm, d), lambda i, j: (i, 0))
    row = pl.BlockSpec((1, d), lambda i, j: (0, 0))
    hid = pl.BlockSpec((None, tm, fs), lambda i, j: (j, i, 0))
    head_specs = [_once(tok), row] if head is not None else []
    head_out = [pl.BlockSpec((8, d), lambda i, j: (0, 0))] if head is not None else []
    head_shape = [jax.ShapeDtypeStruct((8, d), F32)] if head is not None else []
    return _call(
        body, name=f"ffn_fwd{k}", grid=(nt, ns), args=(x, mod, g, wg, wu, wd, *extra),
        in_specs=[tok,
                  pl.BlockSpec((None, N_MOD, d), lambda i, j: (i // tpb, 0, 0)),
                  row,
                  pl.BlockSpec((None, d, fs), lambda i, j: (j, 0, 0)),
                  pl.BlockSpec((None, d, fs), lambda i, j: (j, 0, 0)),
                  pl.BlockSpec((None, fs, d), lambda i, j: (j, 0, 0))] + head_specs,
        out_specs=[tok, tok, hid, hid] + head_out,
        out_shape=[jax.ShapeDtypeStruct((t, d), F32), jax.ShapeDtypeStruct((t, d), BF16),
                   jax.ShapeDtypeStruct((ns, t, fs), BF16), jax.ShapeDtypeStruct((ns, t, fs), BF16)]
        + head_shape,
        scratch_shapes=[pltpu.VMEM((tm, d), BF16), pltpu.VMEM((tm, d), F32)],
        params=_params("arbitrary", "arbitrary"), exchange=exchange)


def _ffn_bwd(dxo, x, f, mod, g, gate, up, wg, wu, wd, k, tm, exchange=None):
    t, d = x.shape
    ns, _, fs = wg.shape
    nt = t // tm
    nb = mod.shape[0]
    tpb = nt // nb
    rows = tm // FFN_CHUNKS

    def body(dxo_ref, x_ref, f_ref, mod_ref, g_ref, gg_ref, uu_ref, wg_ref, wu_ref, wd_ref,
             dx_ref, dgg_ref, duu_ref, act_ref, h_ref, df_ref, dmod_ref, dg_ref, acc):
        i, j = pl.program_id(0), pl.program_id(1)

        @pl.when(j == 0)
        def _():
            df = 0.5 * mod_ref[3 * k + 2:3 * k + 3, :] * dxo_ref[...]
            df_ref[...] = df.astype(BF16)
            h_ref[...] = _modulate(x_ref[...], g_ref[...], mod_ref, k).astype(BF16)
            acc[...] = jnp.zeros_like(acc)

        chunks = [pl.ds(c * rows, rows) for c in range(FFN_CHUNKS)]
        wg, wu, wd = wg_ref[...], wu_ref[...], wd_ref[...]
        dacts = [_dot_nt(df_ref[rs, :], wd) for rs in chunks]
        dgates, dups = [], []
        for rs, dact in zip(chunks, dacts):
            gv, uv = gg_ref[rs, :].astype(F32), uu_ref[rs, :].astype(F32)
            sig = _sigmoid(gv)
            s = gv * sig
            act_ref[rs, :] = (s * uv).astype(BF16)
            dups.append((dact * s).astype(BF16))
            dgates.append((dact * uv * (sig * (1.0 + gv * (1.0 - sig)))).astype(BF16))
        dhs = [_dot_nt(dg, wg) + _dot_nt(du, wu) for dg, du in zip(dgates, dups)]
        for rs, dg, du, dh in zip(chunks, dgates, dups, dhs):
            dgg_ref[rs, :] = dg
            duu_ref[rs, :] = du
            acc[rs, :] += dh

        @pl.when(j == ns - 1)
        def _():
            dx, dshift, dscale, dg = _modulate_bwd(acc[...], x_ref[...], g_ref[...], mod_ref, k)
            dxo_v = dxo_ref[...]
            dx_ref[...] = dxo_v + dx
            dgt = jnp.sum(0.5 * f_ref[...].astype(F32) * dxo_v, axis=0, keepdims=True)

            @pl.when(i % tpb == 0)
            def _():
                dmod_ref[...] = jnp.zeros_like(dmod_ref)

            @pl.when(i == 0)
            def _():
                dg_ref[...] = jnp.zeros_like(dg_ref)

            dmod_ref[0:1, :] += dshift
            dmod_ref[1:2, :] += dscale
            dmod_ref[2:3, :] += dgt
            dg_ref[0:1, :] += dg

    tok = pl.BlockSpec((tm, d), lambda i, j: (i, 0))
    hid = pl.BlockSpec((None, tm, fs), lambda i, j: (j, i, 0))
    return _call(
        body, name=f"ffn_bwd{k}", grid=(nt, ns), args=(dxo, x, f, mod, g, gate, up, wg, wu, wd),
        in_specs=[tok, _once(tok), _once(tok),
                  pl.BlockSpec((None, N_MOD, d), lambda i, j: (i // tpb, 0, 0)),
                  pl.BlockSpec((1, d), lambda i, j: (0, 0)),
                  hid, hid,
                  pl.BlockSpec((None, d, fs), lambda i, j: (j, 0, 0)),
                  pl.BlockSpec((None, d, fs), lambda i, j: (j, 0, 0)),
                  pl.BlockSpec((None, fs, d), lambda i, j: (j, 0, 0))],
        out_specs=[tok, hid, hid, hid, tok, tok,
                   pl.BlockSpec((None, 8, d), lambda i, j: (i // tpb, 0, 0)),
                   pl.BlockSpec((8, d), lambda i, j: (0, 0))],
        out_shape=[jax.ShapeDtypeStruct((t, d), F32),
                   jax.ShapeDtypeStruct((ns, t, fs), BF16), jax.ShapeDtypeStruct((ns, t, fs), BF16),
                   jax.ShapeDtypeStruct((ns, t, fs), BF16),
                   jax.ShapeDtypeStruct((t, d), BF16), jax.ShapeDtypeStruct((t, d), BF16),
                   jax.ShapeDtypeStruct((nb, 8, d), F32), jax.ShapeDtypeStruct((8, d), F32)],
        scratch_shapes=[pltpu.VMEM((tm, d), F32)],
        params=_params("arbitrary", "arbitrary"), exchange=exchange)


def _mm_tn(a, b, a_spec, b_spec, out_shape, n_tiles, name, exchange=None, keep_transposed=False,
           pair_reduce=False):
    n_out = out_shape[0]
    block = tuple(out_shape[1:])
    last = n_tiles - 1
    flip = block[0] > block[1]
    if flip:
        block = block[::-1]
    if flip and keep_transposed:
        flip_back, out_shape = False, (n_out,) + block
    else:
        flip_back = flip
    full_shape = tuple(out_shape)
    n_pairs = n_out // 2
    if pair_reduce:
        out_shape = (n_pairs,) + full_shape[1:]

    def body(a_ref, b_ref, o_ref, acc, *pair):
        i, j = pl.program_id(0), pl.program_id(1)
        prod = _dot_tn(b_ref[...], a_ref[...]) if flip else _dot_tn(a_ref[...], b_ref[...])
        full_ref = pair[0] if pair_reduce else o_ref

        @pl.when(i == 0)
        def _():
            acc[j] = prod

        @pl.when(i > 0)
        def _():
            acc[j] += prod

        @pl.when(i == last)
        def _():
            total = acc[j]
            full_ref[j] = (total.T if flip_back else total).astype(BF16)

        if pair_reduce:
            _, landed, send_sems, recv_sems = pair

            @pl.when(jnp.logical_and(i == last, j == n_out - 1))
            def _():
                x, y, c = lax.axis_index("x"), lax.axis_index("y"), lax.axis_index("c")
                copies = [pltpu.make_async_remote_copy(
                    src_ref=full_ref.at[2 * q + 1 - c], dst_ref=landed.at[q],
                    send_sem=send_sems.at[q], recv_sem=recv_sems.at[q],
                    device_id=(x, y, 1 - c), device_id_type=pl.DeviceIdType.MESH) for q in range(n_pairs)]
                for cp in copies:
                    cp.start()
                for q, cp in enumerate(copies):
                    cp.wait_recv()
                    o_ref[q] = (full_ref[2 * q + c].astype(F32) + landed[q].astype(F32)).astype(BF16)
                for cp in copies:
                    cp.wait_send()

    scratch = [pltpu.VMEM((n_out,) + block, F32)]
    if pair_reduce:
        scratch += [pltpu.VMEM(full_shape, BF16), pltpu.VMEM(out_shape, BF16),
                    pltpu.SemaphoreType.DMA((n_pairs,)), pltpu.SemaphoreType.DMA((n_pairs,))]
    outs, sent = _call(
        body, name=name, grid=(n_tiles, n_out), args=(a, b), in_specs=[a_spec, b_spec],
        out_specs=[pl.BlockSpec(out_shape, lambda i, j: (0,) * len(out_shape))],
        out_shape=[jax.ShapeDtypeStruct(out_shape, BF16)],
        scratch_shapes=scratch,
        params=_params("arbitrary", "arbitrary"), exchange=exchange)
    return (outs[0], sent) if exchange is not None else outs[0]


def _ffn_weight_grads(h, dgate, dup, act, df, tm, tag, stream=False, first=None):
    t, d = h.shape
    ns, _, fs = dgate.shape
    nt = t // tm
    tok = pl.BlockSpec((tm, d), lambda i, j: (i, 0))
    hid = pl.BlockSpec((None, tm, fs), lambda i, j: (j, i, 0))
    if not stream:
        gwg = _mm_tn(h, dgate, tok, hid, (ns, d, fs), nt, f"grad_wg{tag}", keep_transposed=True)
        gwu = _mm_tn(h, dup, tok, hid, (ns, d, fs), nt, f"grad_wu{tag}", keep_transposed=True)
        gwd = _mm_tn(act, df, hid, tok, (ns, fs, d), nt, f"grad_wd{tag}")
        return gwg, gwu, gwd
    gwg, brought = _mm_tn(h, dgate, tok, hid, (ns, d, fs), nt, f"grad_wg{tag}", first,
                          keep_transposed=True, pair_reduce=True)
    gwu, sent_g = _mm_tn(h, dup, tok, hid, (ns, d, fs), nt, f"grad_wu{tag}",
                         _Exchange([gwg], True, chips=[True]), keep_transposed=True, pair_reduce=True)
    gwd, sent_u = _mm_tn(act, df, hid, tok, (ns, fs, d), nt, f"grad_wd{tag}",
                         _Exchange([gwu], True, chips=[True]), pair_reduce=True)
    return sent_g[0], sent_u[0], gwd, brought


def _stage_shape(rows, cols):
    return pltpu.VMEM((cols // LANES, rows, LANES), F32)


def _stage(value, stage_ref):
    for k in range(stage_ref.shape[0]):
        stage_ref[k] = value[:, k * LANES:(k + 1) * LANES]


def _to_residue_rows(stage_ref, dst_ref, dil):
    rows = stage_ref.shape[1] // dil
    for r in range(dil):
        for k in range(stage_ref.shape[0]):
            dst_ref[r, :, k * LANES:(k + 1) * LANES] = (
                stage_ref.at[k][pl.ds(r, rows, stride=dil), :].astype(dst_ref.dtype))


def _from_residue_rows(src_ref, stage_ref, dil):
    rows = stage_ref.shape[1] // dil
    chunks = range(stage_ref.shape[0])
    for r in range(dil):
        for k in chunks:
            stage_ref.at[k][pl.ds(r, rows, stride=dil), :] = src_ref[r, :, k * LANES:(k + 1) * LANES].astype(F32)
    return jnp.concatenate([stage_ref[k] for k in chunks], axis=1)


def _residue_shape(nb, seq, width, dil, dtype):
    return jax.ShapeDtypeStruct((nb, dil, seq // dil, width), dtype)


def _residue_spec(tm, tpb, cols, dil, col_block):
    return pl.BlockSpec((None, dil, tm // dil, cols),
                        lambda i, *rest: (i // tpb, 0, i % tpb, col_block(i, *rest)))


def _qkv_fwd(x, mod, g, win, tm, exchange=None):
    t, d = x.shape
    ns, _, cs = win.shape
    nt = t // tm
    nb = mod.shape[0]
    tpb = nt // nb
    seq = t // nb
    half = ns // 2
    n_res = len(DILATIONS) - 1

    def body(x_ref, mod_ref, g_ref, w_ref, sb_ref, dil_ref, *rest):
        res_refs, h_ref, sc = rest[:n_res], rest[n_res], rest[n_res + 1]
        j = pl.program_id(1)

        @pl.when(j == 0)
        def _():
            h_ref[...] = _modulate(x_ref[...], g_ref[...], mod_ref, 1).astype(BF16)

        res = _dot(h_ref[...], w_ref[...])

        @pl.when(j < half)
        def _():
            sb_ref[...] = res.astype(BF16)

        @pl.when(j >= half)
        def _():
            dil_ref[...] = res.astype(BF16)
            _stage(res, sc)
            for ref, dil in zip(res_refs, DILATIONS[1:]):
                _to_residue_rows(sc, ref, dil)

    def dil_col(i, j):
        return jnp.maximum(j - half, 0)

    tok = pl.BlockSpec((tm, d), lambda i, j: (i, 0))
    wide = jax.ShapeDtypeStruct((t, half * cs), BF16)
    outs, got = _call(
        body, name="qkv_fwd", grid=(nt, ns), args=(x, mod, g, win),
        in_specs=[tok,
                  pl.BlockSpec((None, N_MOD, d), lambda i, j: (i // tpb, 0, 0)),
                  pl.BlockSpec((1, d), lambda i, j: (0, 0)),
                  pl.BlockSpec((None, d, cs), lambda i, j: (j, 0, 0))],
        out_specs=[pl.BlockSpec((tm, cs), lambda i, j: (i, jnp.minimum(j, half - 1))),
                   pl.BlockSpec((tm, cs), lambda i, j: (i, dil_col(i, j)))]
        + [_residue_spec(tm, tpb, cs, dil, dil_col) for dil in DILATIONS[1:]] + [tok],
        out_shape=[wide, wide] + [_residue_shape(nb, seq, half * cs, dil, BF16) for dil in DILATIONS[1:]]
        + [jax.ShapeDtypeStruct((t, d), BF16)],
        scratch_shapes=[_stage_shape(tm, cs)],
        params=_params("arbitrary", "arbitrary"), exchange=exchange)
    qkv_dil = [outs[1]] + [a.reshape(t, half * cs) for a in outs[2:2 + n_res]]
    return (outs[0], qkv_dil, outs[-1]), got


def _qkv_bwd(dqkv, dxo, x, mod, g, win, tm, exchange=None):
    t, d = x.shape
    ns, _, cs = win.shape
    nt = t // tm
    nb = mod.shape[0]
    tpb = nt // nb

    def body(dq_ref, dxo_ref, x_ref, mod_ref, g_ref, w_ref, dx_ref, dmod_ref, dg_ref, acc):
        i, j = pl.program_id(0), pl.program_id(1)

        @pl.when(j == 0)
        def _():
            acc[...] = jnp.zeros_like(acc)

        acc[...] += _dot_nt(dq_ref[...], w_ref[...])

        @pl.when(j == ns - 1)
        def _():
            dx, dshift, dscale, dg = _modulate_bwd(acc[...], x_ref[...], g_ref[...], mod_ref, 1)
            dx_ref[...] = dxo_ref[...] + dx

            @pl.when(i % tpb == 0)
            def _():
                dmod_ref[...] = jnp.zeros_like(dmod_ref)

            @pl.when(i == 0)
            def _():
                dg_ref[...] = jnp.zeros_like(dg_ref)

            dmod_ref[0:1, :] += dshift
            dmod_ref[1:2, :] += dscale
            dg_ref[0:1, :] += dg

    tok = pl.BlockSpec((tm, d), lambda i, j: (i, 0))
    return _call(
        body, name="qkv_bwd", grid=(nt, ns), args=(dqkv, dxo, x, mod, g, win),
        in_specs=[pl.BlockSpec((tm, cs), lambda i, j: (i, j)), tok, tok,
                  pl.BlockSpec((None, N_MOD, d), lambda i, j: (i // tpb, 0, 0)),
                  pl.BlockSpec((1, d), lambda i, j: (0, 0)),
                  pl.BlockSpec((None, d, cs), lambda i, j: (j, 0, 0))],
        out_specs=[tok,
                   pl.BlockSpec((None, 8, d), lambda i, j: (i // tpb, 0, 0)),
                   pl.BlockSpec((8, d), lambda i, j: (0, 0))],
        out_shape=[jax.ShapeDtypeStruct((t, d), F32),
                   jax.ShapeDtypeStruct((nb, 8, d), F32), jax.ShapeDtypeStruct((8, d), F32)],
        scratch_shapes=[pltpu.VMEM((tm, d), F32)],
        params=_params("arbitrary", "arbitrary"), exchange=exchange)


def _heads(a):
    return [a[:, h * HEAD_DIM:(h + 1) * HEAD_DIM] for h in range(a.shape[1] // HEAD_DIM)]


def _own_lanes():
    lane = lax.broadcasted_iota(jnp.int32, (1, LANES), 1)
    return [lane < HEAD_DIM, lane >= HEAD_DIM]


def _pair_tiles(a):
    return [a[:, (h // 2) * LANES:(h // 2 + 1) * LANES] for h in range(a.shape[1] // HEAD_DIM)]


def _own_tiles(a, own):
    return [jnp.where(own[h % 2], tile, jnp.zeros_like(tile)) for h, tile in enumerate(_pair_tiles(a))]


def _merge_tiles(per_head, own):
    return jnp.concatenate([jnp.where(own[0], per_head[h], per_head[h + 1])
                            for h in range(0, len(per_head), 2)], axis=1)


def _scaled(q):
    return (q.astype(F32) * (HEAD_DIM ** -0.5)).astype(BF16)


def _sb_logits(qh, kh, tri, causal):
    zs = [_dot_nt(q, k) for q, k in zip(qh, kh)]
    es = [jnp.exp(-jnp.abs(z)) for z in zs]
    log_nots = [-(jnp.maximum(z, 0.0) + jnp.log(1.0 + e)) for z, e in zip(zs, es)]
    if causal is not None:
        log_nots = [jnp.where(causal, ln, 0.0) for ln in log_nots]
    return zs, es, [_split_dot(ln, tri) for ln in log_nots]


def _sb_masks():
    rows = lax.broadcasted_iota(jnp.int32, (SB_BLOCK, SB_BLOCK), 0)
    cols = lax.broadcasted_iota(jnp.int32, (SB_BLOCK, SB_BLOCK), 1)
    return (rows >= cols).astype(BF16), (rows <= cols).astype(BF16), cols < rows


def _sb_fwd(qkv, nb, seq, exchange=None):
    t = qkv.shape[0]
    n_pairs = (qkv.shape[1] // 3) // SB_WIDTH
    tb = SB_BLOCK
    n_blk = seq // tb

    def body(q_ref, k_ref, v_ref, o_ref, c_ref):
        tri, _, causal = _sb_masks()
        own = _own_lanes()

        def key_block(qh, kj, carry, mask):
            ks = pl.multiple_of(kj * tb, tb)
            kh, vh = _pair_tiles(k_ref[pl.ds(ks, tb), :]), _pair_tiles(v_ref[pl.ds(ks, tb), :])
            zs, _, suffixes = _sb_logits(qh, kh, tri, mask)
            ws = [jnp.exp(z + suffix + cr[1]) for z, suffix, cr in zip(zs, suffixes, carry)]
            if mask is not None:
                ws = [jnp.where(mask, w, 0.0) for w in ws]
            pv = [_dot(w.astype(BF16), v) for w, v in zip(ws, vh)]
            return tuple((cr[0] + p, cr[1] + suffix[:, 0:1]) for cr, p, suffix in zip(carry, pv, suffixes))

        def query_block(qi, _):
            qs = pl.multiple_of(qi * tb, tb)
            qh = _own_tiles(_scaled(q_ref[pl.ds(qs, tb), :]), own)
            zero = (jnp.zeros((tb, LANES), F32), jnp.zeros((tb, 1), F32))
            carry = key_block(qh, qi, (zero,) * SB_HEADS, causal)
            carry = lax.fori_loop(0, qi, lambda it, cr: key_block(qh, qi - 1 - it, cr, None), carry)
            o_ref[pl.ds(qs, tb), :] = _merge_tiles([cr[0] for cr in carry], own)
            c_ref[pl.ds(qs, tb), :] = _merge_tiles([jnp.broadcast_to(cr[1], (tb, LANES)) for cr in carry], own)
            return 0

        lax.fori_loop(0, n_blk, query_block, 0)

    def spec(offset):
        return pl.BlockSpec((seq, SB_WIDTH), lambda b, p: (b, offset + p))

    out = jax.ShapeDtypeStruct((t, n_pairs * SB_WIDTH), F32)
    return _call(
        body, name="sb_fwd", grid=(nb, n_pairs), args=(qkv, qkv, qkv),
        in_specs=[spec(0), spec(n_pairs), spec(2 * n_pairs)],
        out_specs=[spec(0), spec(0)], out_shape=[out, out],
        params=_params("arbitrary", "arbitrary"), exchange=exchange)


def _sb_bwd(qkv, do, csum, nb, seq, exchange=None):
    t = qkv.shape[0]
    n_pairs = (qkv.shape[1] // 3) // SB_WIDTH
    tb = SB_BLOCK
    n_blk = seq // tb
    scale = HEAD_DIM ** -0.5

    def body(q_ref, k_ref, v_ref, do_ref, c_ref, dq_ref, dk_ref, dv_ref, dkt_acc, dvt_acc):
        tri, tri_prefix, causal = _sb_masks()
        own = _own_lanes()
        dkt_acc[...] = jnp.zeros_like(dkt_acc)
        dvt_acc[...] = jnp.zeros_like(dvt_acc)

        def key_blocks(qh, qth, doh, doth, ch, kjs, carry, mask):
            nh = SB_HEADS
            chains = range(nh * len(kjs))
            kss = [pl.multiple_of(kj * tb, tb) for kj in kjs]
            kh = [tile for ks in kss for tile in _pair_tiles(k_ref[pl.ds(ks, tb), :])]
            vh = [tile for ks in kss for tile in _pair_tiles(v_ref[pl.ds(ks, tb), :])]
            zs, es, suffixes = _sb_logits(qh * len(kjs), kh, tri, mask)
            dws = [_dot_nt(doh[c % nh], vh[c]) for c in chains]
            lefts = []
            for c in chains:
                before = carry[c][1] if c < nh else lefts[c - nh]
                lefts.append(before + suffixes[c][:, 0:1])
            ws = [jnp.exp(zs[c] + suffixes[c] + (ch[c % nh] - lefts[c])) for c in chains]
            if mask is not None:
                ws = [jnp.where(mask, w, 0.0) for w in ws]
            dlws = [ws[c] * dws[c] for c in chains]
            dprefixes = [_split_dot(dlw, tri_prefix) for dlw in dlws]
            dvts = [_dot(doth[c % nh], ws[c].astype(BF16)) for c in chains]
            dlefts, dzbs = [], []
            for c in chains:
                dlefts.append(carry[c][2] if c < nh else dlefts[c - nh] + dprefixes[c - nh][:, tb - 1:tb])
                sig = jnp.where(zs[c] >= 0.0, 1.0, es[c]) * pl.reciprocal(1.0 + es[c], approx=True)
                dz = dlws[c] - sig * (dlefts[c] + dprefixes[c])
                if mask is not None:
                    dz = jnp.where(mask, dz, 0.0)
                dzbs.append(dz.astype(BF16))
            dkts = [_dot(qth[c % nh], dzbs[c]) for c in chains]
            dqs = [_dot(dzbs[c], kh[c]) for c in chains]
            for b, ks in enumerate(kss):
                pairs = range(b * nh, (b + 1) * nh, 2)
                dkt_acc[:, pl.ds(ks, tb)] += jnp.concatenate([dkts[c] + dkts[c + 1] for c in pairs], axis=0)
                dvt_acc[:, pl.ds(ks, tb)] += jnp.concatenate([dvts[c] + dvts[c + 1] for c in pairs], axis=0)
            last = (len(kjs) - 1) * nh
            return tuple((carry[h][0] + sum(dqs[h::nh]), lefts[last + h],
                          dlefts[last + h] + dprefixes[last + h][:, tb - 1:tb]) for h in range(nh))

        def query_block(qi, _):
            qs = pl.multiple_of(qi * tb, tb)
            qh = _own_tiles(_scaled(q_ref[pl.ds(qs, tb), :]), own)
            doh = _own_tiles(do_ref[pl.ds(qs, tb), :], own)
            qth = [a.astype(F32).T.astype(BF16) for a in qh]
            doth = [a.T.astype(BF16) for a in doh]
            doh = [a.astype(BF16) for a in doh]
            cv = c_ref[pl.ds(qs, tb), :]
            ch = [cv[:, h * HEAD_DIM:h * HEAD_DIM + 1] for h in range(SB_HEADS)]
            zero = (jnp.zeros((tb, LANES), F32), jnp.zeros((tb, 1), F32), jnp.zeros((tb, 1), F32))

            def key_block(kjs, cr, mask):
                return key_blocks(qh, qth, doh, doth, ch, kjs, cr, mask)

            carry = lax.fori_loop(0, qi // 2, lambda p, cr: key_block([2 * p, 2 * p + 1], cr, None),
                                  (zero,) * SB_HEADS)
            carry = lax.fori_loop(0, qi % 2, lambda _, cr: key_block([qi - 1], cr, None), carry)
            carry = key_block([qi], carry, causal)
            dq = _merge_tiles([cr[0] for cr in carry], own) * scale
            dq_ref[pl.ds(qs, tb), :] = dq.astype(BF16)
            return 0

        lax.fori_loop(0, n_blk, query_block, 0)
        dk_ref[...] = dkt_acc[...].T.astype(BF16)
        dv_ref[...] = dvt_acc[...].T.astype(BF16)

    def spec(offset):
        return pl.BlockSpec((seq, SB_WIDTH), lambda b, p: (b, offset + p))

    out = jax.ShapeDtypeStruct((t, n_pairs * SB_WIDTH), BF16)
    return _call(
        body, name="sb_bwd", grid=(nb, n_pairs), args=(qkv, qkv, qkv, do, csum),
        in_specs=[spec(0), spec(n_pairs), spec(2 * n_pairs), spec(0), spec(0)],
        out_specs=[spec(0), spec(0), spec(0)],
        out_shape=[out, out, out],
        scratch_shapes=[pltpu.VMEM((SB_WIDTH, seq), F32), pltpu.VMEM((SB_WIDTH, seq), F32)],
        params=_params("arbitrary", "arbitrary"), exchange=exchange)


def _dil_block_scores(qh, kph, kch, bias_ref, has_prev, band_prev, band_cur):
    scale = HEAD_DIM ** -0.5
    heads = range(len(qh))
    no_prev = jnp.where(has_prev, 0.0, NEG_INF)
    zps = [_dot_nt(qh[h], kph[h]) for h in heads]
    zcs = [_dot_nt(qh[h], kch[h]) for h in heads]
    zps = [jnp.where(band_prev, zps[h] * scale + bias_ref[h, :, 0:DIL_BLOCK], NEG_INF) + no_prev for h in heads]
    zcs = [jnp.where(band_cur, zcs[h] * scale + bias_ref[h, :, DIL_BLOCK:2 * DIL_BLOCK], NEG_INF) for h in heads]
    return zps, zcs


def _dil_bands():
    rows = lax.broadcasted_iota(jnp.int32, (DIL_BLOCK, DIL_BLOCK), 0)
    cols = lax.broadcasted_iota(jnp.int32, (DIL_BLOCK, DIL_BLOCK), 1)
    return cols >= rows, cols <= rows


def _dil_fwd(qkv, bias, nb, seq, dil, exchange=None):
    t, width = qkv.shape
    n_pairs = (width // 3) // DIL_WIDTH
    bq = DIL_BLOCK
    n_blk = seq // bq
    per_seq = n_blk // dil
    heads = range(DIL_HEADS)

    def body(q_ref, k_ref, v_ref, bias_ref, o_ref, lse_ref):
        band_prev, band_cur = _dil_bands()
        own = _own_lanes()

        def block(n, _):
            has_prev = (n & (per_seq - 1)) != 0
            qs = pl.multiple_of(n * bq, bq)
            ps = pl.multiple_of(jnp.maximum(n - 1, 0) * bq, bq)
            qh = _own_tiles(q_ref[pl.ds(qs, bq), :], own)
            kp, kc = _pair_tiles(k_ref[pl.ds(ps, bq), :]), _pair_tiles(k_ref[pl.ds(qs, bq), :])
            vp, vc = _pair_tiles(v_ref[pl.ds(ps, bq), :]), _pair_tiles(v_ref[pl.ds(qs, bq), :])
            zps, zcs = _dil_block_scores(qh, kp, kc, bias_ref, has_prev, band_prev, band_cur)
            ms = [jnp.maximum(jnp.max(zps[h], axis=1, keepdims=True), jnp.max(zcs[h], axis=1, keepdims=True))
                  for h in heads]
            eps = [jnp.exp(zps[h] - ms[h]) for h in heads]
            ecs = [jnp.exp(zcs[h] - ms[h]) for h in heads]
            pvs = [_dot(eps[h].astype(BF16), vp[h]) + _dot(ecs[h].astype(BF16), vc[h]) for h in heads]
            dens = [jnp.sum(eps[h], axis=1, keepdims=True) + jnp.sum(ecs[h], axis=1, keepdims=True) for h in heads]
            o_ref[pl.ds(qs, bq), :] = _merge_tiles([pvs[h] / dens[h] for h in heads], own)
            lse_ref[pl.ds(qs, bq), :] = _merge_tiles(
                [jnp.broadcast_to(ms[h] + jnp.log(dens[h]), (bq, LANES)) for h in heads], own)
            return 0

        lax.fori_loop(0, n_blk, block, 0)

    def spec(offset):
        return pl.BlockSpec((seq, DIL_WIDTH), lambda b, p: (b, offset + p))

    out = jax.ShapeDtypeStruct((t, n_pairs * DIL_WIDTH), F32)
    return _call(
        body, name=f"dil_fwd{dil}", grid=(nb, n_pairs), args=(qkv, qkv, qkv, bias),
        in_specs=[spec(0), spec(n_pairs), spec(2 * n_pairs),
                  pl.BlockSpec((DIL_HEADS, bq, 2 * bq), lambda b, p: (p, 0, 0))],
        out_specs=[spec(0), spec(0)], out_shape=[out, out],
        params=_params("arbitrary", "arbitrary"), exchange=exchange)


def _dil_bwd(qkv, bias, do, lse, delta, nb, seq, dil):
    t, width = qkv.shape
    n_pairs = (width // 3) // DIL_WIDTH
    bq = DIL_BLOCK
    n_blk = seq // bq
    per_seq = n_blk // dil
    scale = HEAD_DIM ** -0.5
    heads = range(DIL_HEADS)

    def body(q_ref, k_ref, v_ref, bias_ref, do_ref, lse_ref, dl_ref, dq_ref, dk_ref, dv_ref, db_ref,
             dk_acc, dv_acc):
        band_prev, band_cur = _dil_bands()
        own = _own_lanes()
        dk_acc[...] = jnp.zeros_like(dk_acc)
        dv_acc[...] = jnp.zeros_like(dv_acc)

        @pl.when(pl.program_id(1) == 0)
        def _():
            db_ref[...] = jnp.zeros_like(db_ref)

        def block(n, _):
            has_prev = (n & (per_seq - 1)) != 0
            qs = pl.multiple_of(n * bq, bq)
            ps = pl.multiple_of(jnp.maximum(n - 1, 0) * bq, bq)
            qh = _own_tiles(q_ref[pl.ds(qs, bq), :], own)
            kp, kc = _pair_tiles(k_ref[pl.ds(ps, bq), :]), _pair_tiles(k_ref[pl.ds(qs, bq), :])
            vp, vc = _pair_tiles(v_ref[pl.ds(ps, bq), :]), _pair_tiles(v_ref[pl.ds(qs, bq), :])
            doh = _own_tiles(do_ref[pl.ds(qs, bq), :].astype(BF16), own)
            lse_v, dl_v = lse_ref[pl.ds(qs, bq), :], dl_ref[pl.ds(qs, bq), :]
            zps, zcs = _dil_block_scores(qh, kp, kc, bias_ref, has_prev, band_prev, band_cur)
            dpp = [_dot_nt(doh[h], vp[h]) for h in heads]
            dpc = [_dot_nt(doh[h], vc[h]) for h in heads]
            lse_h = [lse_v[:, h * HEAD_DIM:h * HEAD_DIM + 1] for h in heads]
            dl_h = [dl_v[:, h * HEAD_DIM:h * HEAD_DIM + 1] for h in heads]
            pps = [jnp.exp(zps[h] - lse_h[h]) for h in heads]
            pcs = [jnp.exp(zcs[h] - lse_h[h]) for h in heads]
            dvp = [_dot_tn(pps[h].astype(BF16), doh[h]) for h in heads]
            dvc = [_dot_tn(pcs[h].astype(BF16), doh[h]) for h in heads]
            dzps = [pps[h] * (dpp[h] - dl_h[h]) for h in heads]
            dzcs = [pcs[h] * (dpc[h] - dl_h[h]) for h in heads]
            dzp_b = [(dzps[h] * scale).astype(BF16) for h in heads]
            dzc_b = [(dzcs[h] * scale).astype(BF16) for h in heads]
            dqs = [_dot(dzp_b[h], kp[h]) + _dot(dzc_b[h], kc[h]) for h in heads]
            dkp = [_dot_tn(dzp_b[h], qh[h]) for h in heads]
            dkc = [_dot_tn(dzc_b[h], qh[h]) for h in heads]
            for h in heads:
                db_ref[h, :, 0:bq] += dzps[h]
                db_ref[h, :, bq:2 * bq] += dzcs[h]
            def pair_sums(per_head):
                return jnp.concatenate([per_head[h] + per_head[h + 1] for h in heads[::2]], axis=1)

            dq_ref[pl.ds(qs, bq), :] = _merge_tiles(dqs, own).astype(BF16)
            dk_acc[pl.ds(ps, bq), :] += pair_sums(dkp)
            dk_acc[pl.ds(qs, bq), :] += pair_sums(dkc)
            dv_acc[pl.ds(ps, bq), :] += pair_sums(dvp)
            dv_acc[pl.ds(qs, bq), :] += pair_sums(dvc)
            return 0

        lax.fori_loop(0, n_blk, block, 0)
        dk_ref[...] = dk_acc[...].astype(BF16)
        dv_ref[...] = dv_acc[...].astype(BF16)

    def spec(offset):
        return pl.BlockSpec((seq, DIL_WIDTH), lambda p, b: (b, offset + p))

    bias_spec = pl.BlockSpec((DIL_HEADS, bq, 2 * bq), lambda p, b: (p, 0, 0))
    out = jax.ShapeDtypeStruct((t, n_pairs * DIL_WIDTH), BF16)
    return pl.pallas_call(
        body, name=f"dil_bwd{dil}", grid=(n_pairs, nb),
        in_specs=[spec(0), spec(n_pairs), spec(2 * n_pairs), bias_spec, spec(0), spec(0), spec(0)],
        out_specs=[spec(0), spec(0), spec(0), bias_spec],
        out_shape=[out, out, out, jax.ShapeDtypeStruct(bias.shape, F32)],
        scratch_shapes=[pltpu.VMEM((seq, DIL_WIDTH), F32), pltpu.VMEM((seq, DIL_WIDTH), F32)],
        compiler_params=_params("arbitrary", "arbitrary"),
    )(qkv, qkv, qkv, bias, do, lse, delta)


def _head_blocks(width):
    rows = lax.broadcasted_iota(jnp.int32, (width, width), 0) // HEAD_DIM
    cols = lax.broadcasted_iota(jnp.int32, (width, width), 1) // HEAD_DIM
    return (rows == cols).astype(BF16)


def _head_mean(v, gmat):
    return _split_dot(v, gmat) * (1.0 / HEAD_DIM)


def _residue_views(arrays, nb, seq):
    return [a if dil == 1 else a.reshape(nb, dil, seq // dil, a.shape[1]) for a, dil in zip(arrays, DILATIONS)]


def _mix_out_fwd(osb, ocs, lses, gsb, gdil, wout, x, mod, tm):
    t, d = x.shape
    ds = osb.shape[1]
    nt = t // tm
    nb = mod.shape[0]
    tpb = nt // nb
    seq = t // nb
    n_cfg = len(DILATIONS)

    def body(osb_ref, *refs):
        oc_refs, lse_refs = refs[:n_cfg], refs[n_cfg:2 * n_cfg]
        gsb_ref, gdil_ref, w_ref, x_ref, mod_ref = refs[2 * n_cfg:2 * n_cfg + 5]
        xo_ref, on_ref, m_ref, odil_ref = refs[2 * n_cfg + 5:2 * n_cfg + 9]
        ld_refs = refs[2 * n_cfg + 9:3 * n_cfg + 9]
        stages, sc = refs[3 * n_cfg + 9:]
        ocv, lsev = [oc_refs[0][...]], [lse_refs[0][...]]
        for i, dil in enumerate(DILATIONS[1:]):
            ocv.append(_from_residue_rows(oc_refs[i + 1], stages.at[2 * i], dil))
            lsev.append(_from_residue_rows(lse_refs[i + 1], stages.at[2 * i + 1], dil))
        top = functools.reduce(jnp.maximum, lsev)
        total = top + jnp.log(sum(jnp.exp(l - top) for l in lsev))
        odil = sum(jnp.exp(l - total) * o for o, l in zip(ocv, lsev))
        odil_ref[...] = odil
        ld_refs[0][...] = total
        _stage(total, sc)
        for ref, dil in zip(ld_refs[1:], DILATIONS[1:]):
            _to_residue_rows(sc, ref, dil)
        gm = _head_blocks(ds)
        parts = []
        for o, g_ref in ((osb_ref[...], gsb_ref), (odil, gdil_ref)):
            parts.append(o * lax.rsqrt(_head_mean(o * o, gm) + EPS) * g_ref[...])
        on = jnp.concatenate(parts, axis=1).astype(BF16)
        on_ref[...] = on
        m = _dot(on, w_ref[...])
        m_ref[...] = m
        xo_ref[...] = x_ref[...] + mod_ref[5:6, :] * m

    tok = pl.BlockSpec((tm, d), lambda i: (i, 0))
    hd = pl.BlockSpec((tm, ds), lambda i: (i, 0))
    res = [hd] + [_residue_spec(tm, tpb, ds, dil, lambda i: 0) for dil in DILATIONS[1:]]
    res_shape = [jax.ShapeDtypeStruct((t, ds), F32)] + [_residue_shape(nb, seq, ds, dil, F32) for dil in DILATIONS[1:]]
    gain = pl.BlockSpec((1, ds), lambda i: (0, 0))
    outs = pl.pallas_call(
        body, name="mix_out_fwd", grid=(nt,),
        in_specs=[hd] + res + res + [gain, gain,
                  pl.BlockSpec(wout.shape, lambda i: (0, 0)),
                  tok, pl.BlockSpec((None, N_MOD, d), lambda i: (i // tpb, 0, 0))],
        out_specs=[tok, pl.BlockSpec((tm, 2 * ds), lambda i: (i, 0)), tok, hd] + res,
        out_shape=[jax.ShapeDtypeStruct((t, d), F32), jax.ShapeDtypeStruct((t, 2 * ds), BF16),
                   jax.ShapeDtypeStruct((t, d), F32), jax.ShapeDtypeStruct((t, ds), F32)] + res_shape,
        scratch_shapes=[pltpu.VMEM((2 * (n_cfg - 1), ds // LANES, tm, LANES), F32), _stage_shape(tm, ds)],
        compiler_params=_params("arbitrary"),
    )(osb, *_residue_views(ocs, nb, seq), *_residue_views(lses, nb, seq), gsb, gdil, wout, x, mod)
    return outs[0], outs[1], outs[2], outs[3], [a.reshape(t, ds) for a in outs[4:]]


def _mix_out_bwd(dxo, m, mod, wout, osb, odil, gsb, gdil, tm):
    t, d = dxo.shape
    ds = osb.shape[1]
    nt = t // tm
    nb = mod.shape[0]
    tpb = nt // nb
    seq = t // nb
    n_cfg = len(DILATIONS)

    def body(dxo_ref, m_ref, mod_ref, w_ref, osb_ref, odil_ref, gsb_ref, gdil_ref,
             dm_ref, dosb_ref, *rest):
        do_refs, dl_refs = rest[:n_cfg], rest[n_cfg:2 * n_cfg]
        dmod_ref, dg_ref, sc = rest[2 * n_cfg:]
        dodil_ref, dldil_ref = do_refs[0], dl_refs[0]
        i = pl.program_id(0)
        dxo_v = dxo_ref[...]
        dm = (mod_ref[5:6, :] * dxo_v).astype(BF16)
        dm_ref[...] = dm
        dgt = jnp.sum(m_ref[...] * dxo_v, axis=0, keepdims=True)
        don = _dot_nt(dm, w_ref[...])
        gm = _head_blocks(ds)

        @pl.when(i % tpb == 0)
        def _():
            dmod_ref[...] = jnp.zeros_like(dmod_ref)

        @pl.when(i == 0)
        def _():
            dg_ref[...] = jnp.zeros_like(dg_ref)

        dmod_ref[2:3, :] += dgt
        groups = ((osb_ref, gsb_ref, dosb_ref), (odil_ref, gdil_ref, dodil_ref))
        for k, (o_ref, g_ref, do_ref) in enumerate(groups):
            o = o_ref[...]
            dn_out = don[:, k * ds:(k + 1) * ds]
            r = lax.rsqrt(_head_mean(o * o, gm) + EPS)
            n = o * r
            dg_ref[0:1, k * ds:(k + 1) * ds] += jnp.sum(dn_out * n, axis=0, keepdims=True)
            dn = dn_out * g_ref[...]
            do = r * (dn - n * _head_mean(dn * n, gm))
            do_ref[...] = do
            if k == 1:
                delta = _head_mean(do * o, gm) * float(HEAD_DIM)
                dldil_ref[...] = delta
                for value, refs in ((do, do_refs), (delta, dl_refs)):
                    _stage(value, sc)
                    for ref, dil in zip(refs[1:], DILATIONS[1:]):
                        _to_residue_rows(sc, ref, dil)

    tok = pl.BlockSpec((tm, d), lambda i: (i, 0))
    hd = pl.BlockSpec((tm, ds), lambda i: (i, 0))
    res = [hd] + [_residue_spec(tm, tpb, ds, dil, lambda i: 0) for dil in DILATIONS[1:]]
    res_shape = [jax.ShapeDtypeStruct((t, ds), F32)] + [_residue_shape(nb, seq, ds, dil, F32) for dil in DILATIONS[1:]]
    gain = pl.BlockSpec((1, ds), lambda i: (0, 0))
    outs = pl.pallas_call(
        body, name="mix_out_bwd", grid=(nt,),
        in_specs=[tok, tok, pl.BlockSpec((None, N_MOD, d), lambda i: (i // tpb, 0, 0)),
                  pl.BlockSpec(wout.shape, lambda i: (0, 0)), hd, hd, gain, gain],
        out_specs=[tok, hd] + res + res
        + [pl.BlockSpec((None, 8, d), lambda i: (i // tpb, 0, 0)), pl.BlockSpec((8, 2 * ds), lambda i: (0, 0))],
        out_shape=[jax.ShapeDtypeStruct((t, d), BF16), jax.ShapeDtypeStruct((t, ds), F32)] + res_shape + res_shape
        + [jax.ShapeDtypeStruct((nb, 8, d), F32), jax.ShapeDtypeStruct((8, 2 * ds), F32)],
        scratch_shapes=[_stage_shape(tm, ds)],
        compiler_params=_params("arbitrary"),
    )(dxo, m, mod, wout, osb, odil, gsb, gdil)
    flat = [a.reshape(t, ds) for a in outs[2:2 + 2 * n_cfg]]
    return outs[0], outs[1], flat[:n_cfg], flat[n_cfg:], outs[-2], outs[-1]


def _merge_dqkv(sb_parts, dil_parts, nb, tm):
    t, ds = sb_parts[0].shape
    nt = t // tm
    tpb = nt // nb
    seq = t // nb
    n_cfg = len(DILATIONS)

    def body(*refs):
        sb_refs, dil_refs = refs[:3], refs[3:3 + 3 * n_cfg]
        o_ref, sc = refs[3 + 3 * n_cfg:]
        for k in range(3):
            o_ref[:, k * ds:(k + 1) * ds] = sb_refs[k][...]
            total = dil_refs[k * n_cfg][...].astype(F32)
            for i, dil in enumerate(DILATIONS[1:]):
                total = total + _from_residue_rows(dil_refs[k * n_cfg + i + 1], sc, dil)
            o_ref[:, (3 + k) * ds:(4 + k) * ds] = total.astype(BF16)

    hd = pl.BlockSpec((tm, ds), lambda i: (i, 0))
    res = [hd] + [_residue_spec(tm, tpb, ds, dil, lambda i: 0) for dil in DILATIONS[1:]]
    views = [v for parts in dil_parts for v in _residue_views(parts, nb, seq)]
    return pl.pallas_call(
        body, name="merge_dqkv", grid=(nt,),
        in_specs=[hd] * 3 + res * 3,
        out_specs=pl.BlockSpec((tm, 6 * ds), lambda i: (i, 0)),
        out_shape=jax.ShapeDtypeStruct((t, 6 * ds), BF16),
        scratch_shapes=[_stage_shape(tm, ds)],
        compiler_params=_params("arbitrary"),
    )(*sb_parts, *views)


def _row_tile(rows):
    if rows <= 256:
        return rows
    for cand in range(256, 15, -16):
        if rows % cand == 0:
            return cand
    return rows


def _adamw(w, parts, m, v, name, transposed=False):
    rows, cols = w.shape
    n_parts = parts.shape[0]
    tr = _row_tile(rows)
    c1 = 1.0 / (1.0 - ADAM_B1 ** ADAM_STEP)
    c2 = 1.0 / (1.0 - ADAM_B2 ** ADAM_STEP)

    def body(w_ref, p_ref, m_ref, v_ref, g_ref, d_ref, nm_ref, nv_ref):
        g = p_ref[0].astype(F32)
        for i in range(1, n_parts):
            g = g + p_ref[i].astype(F32)
        wv, mv, vv = w_ref[...], m_ref[...], v_ref[...]
        if transposed:
            wv, mv, vv = wv.T, mv.T, vv.T
        nm = ADAM_B1 * mv + (1.0 - ADAM_B1) * g
        nv = ADAM_B2 * vv + (1.0 - ADAM_B2) * (g * g)
        g_ref[...] = g
        nm_ref[...] = nm
        nv_ref[...] = nv
        d_ref[...] = -ADAM_LR * ((nm * c1) / (jnp.sqrt(nv * c2) + ADAM_EPS) + ADAM_WD * wv)

    blk = pl.BlockSpec((tr, cols), lambda i: (i, 0))
    if transposed:
        oblk = pl.BlockSpec((cols, tr), lambda i: (0, i))
        pblk = pl.BlockSpec((n_parts, cols, tr), lambda i: (0, 0, i))
        out = jax.ShapeDtypeStruct((cols, rows), F32)
    else:
        oblk, pblk = blk, pl.BlockSpec((n_parts, tr, cols), lambda i: (0, i, 0))
        out = jax.ShapeDtypeStruct((rows, cols), F32)
    return pl.pallas_call(
        body, name=name, grid=(rows // tr,),
        in_specs=[blk, pblk, blk, blk],
        out_specs=[oblk, oblk, oblk, oblk], out_shape=[out, out, out, out],
        compiler_params=_params("arbitrary"),
    )(w, parts, m, v)


def _t5_bucket(n):
    max_exact = N_BUCKETS // 2
    nf = np.maximum(n, 1).astype(np.float32)
    large = max_exact + (np.log(nf / max_exact) / math.log(MAX_DISTANCE / max_exact)
                         * (N_BUCKETS - max_exact)).astype(np.int32)
    large = np.minimum(large, N_BUCKETS - 1)
    return np.where(n < max_exact, n, large).astype(np.int32)


def _bucket_onehot():
    table = np.zeros((len(DILATIONS), 2 * DIL_BLOCK + 1, N_BUCKETS), np.float32)
    for i, dil in enumerate(DILATIONS):
        buckets = _t5_bucket(np.arange(DIL_BLOCK + 1) * dil)
        for m in range(DIL_BLOCK + 1):
            table[i, m, buckets[DIL_BLOCK - m]] = 1.0
    return table


def _bias_blocks(rel_bias):
    row = jnp.einsum("cmn,nh->chm", _bucket_onehot(), rel_bias, precision=lax.Precision.HIGHEST)
    n_cfg, n_heads, width = row.shape
    tiled = jnp.tile(row, (1, 1, DIL_BLOCK))[..., :DIL_BLOCK * (width - 1)]
    return tiled.reshape(n_cfg, n_heads, DIL_BLOCK, width - 1)


def _bias_blocks_bwd(dblocks):
    n_cfg, n_heads = dblocks.shape[:2]
    width = 2 * DIL_BLOCK + 1
    flat = dblocks.reshape(n_cfg, n_heads, DIL_BLOCK * (width - 1))
    flat = jnp.pad(flat, ((0, 0), (0, 0), (0, DIL_BLOCK)))
    drow = jnp.sum(flat.reshape(n_cfg, n_heads, DIL_BLOCK, width), axis=2)
    return jnp.einsum("chm,cmn->nh", drow, _bucket_onehot(), precision=lax.Precision.HIGHEST)


def _pad_to(a, axis, size):
    pad = [(0, 0)] * a.ndim
    pad[axis] = (0, size - a.shape[axis])
    return jnp.pad(a, pad)


def _lane_pad(n):
    return -(-n // LANES) * LANES


def _local_step(x, target, mod, gains, weights, rel_bias, tm, distributed):
    nb, seq, d = x.shape
    t = nb * seq
    g_ffn1, g_mix, g_sb, g_dil, g_ffn2, g_final = gains
    wg1, wu1, wd1 = weights[:3]
    x0 = x.reshape(t, d)
    ds = g_sb.shape[1]
    bias = _bias_blocks(rel_bias)

    def beside(arrays, scatter):
        return _Exchange(arrays, scatter) if distributed else None

    tp, tg = min(PROJ_TILE, seq), min(GRAD_TILE, t)

    (x1, f1, gate1, up1), got = _ffn_fwd(x0, mod, g_ffn1, wg1, wu1, wd1, 0, tp, beside(weights[3:5], False))
    win, wout = got if distributed else weights[3:5]
    wout2 = wout.reshape(-1, d)
    (qkv, qkvd, h2), got = _qkv_fwd(x1, mod, g_mix, win, tp, beside(weights[7:8], False))
    wd2 = got[0] if distributed else weights[7]
    (osb, csb), got = _sb_fwd(qkv, nb, seq, beside(weights[5:7], False))
    wg2, wu2 = got if distributed else weights[5:7]
    ocs, lses = [], []
    for i, dil in enumerate(DILATIONS):
        (oc, lse), _ = _dil_fwd(qkvd[i], bias[i], nb, seq, dil)
        ocs.append(oc)
        lses.append(lse)
    x2, on, mix, odil, ldil = _mix_out_fwd(osb, ocs, lses, g_sb, g_dil, wout2, x1, mod, tm)
    (dx3, f3, gate3, up3, head), _ = _ffn_fwd(x2, mod, g_ffn2, wg2, wu2, wd2, 2, tp,
                                              head=(target.reshape(t, d), g_final))
    loss_sum = 0.5 * jnp.sum(head[0]) / d
    dg_final = head[1:2]

    (dx2, dgate3, dup3, act3, h3, df3, dmod3, dg_ffn2), _ = _ffn_bwd(
        dx3, x2, f3, mod, g_ffn2, gate3, up3, wg2, wu2, wd2, 2, tp)
    gwg2, gwu2, gwd2 = _ffn_weight_grads(h3, dgate3, dup3, act3, df3, tg, 2)

    dm, dosb, dodil, dldil, dmod2b, dg_heads = _mix_out_bwd(
        dx2, mix, mod, wout2, osb, odil, g_sb, g_dil, tm)
    n_out = wout.shape[0]
    gwout = _mm_tn(on, dm,
                   pl.BlockSpec((tg, wout.shape[1]), lambda i, j: (i, j)),
                   pl.BlockSpec((tg, d), lambda i, j: (i, 0)),
                   wout.shape, t // tg, "grad_wout")

    (dq_sb, dk_sb, dv_sb), parts_late = _sb_bwd(qkv, dosb, csb, nb, seq,
                                                beside([gwout, gwg2, gwu2, gwd2], True))
    dil_grads = [_dil_bwd(qkvd[i], bias[i], dodil[i], ldil[i], dldil[i], nb, seq, dil)
                 for i, dil in enumerate(DILATIONS)]
    dqkv = _merge_dqkv([dq_sb, dk_sb, dv_sb], [[g[k] for g in dil_grads] for k in range(3)], nb, tm)
    drel = _bias_blocks_bwd(jnp.stack([g[3] for g in dil_grads]))

    cs = win.shape[2]
    gwin = _mm_tn(h2, dqkv,
                  pl.BlockSpec((tg, d), lambda i, j: (i, 0)),
                  pl.BlockSpec((tg, cs), lambda i, j: (i, j)),
                  win.shape, t // tg, "grad_win")
    (dx1, dmod2a, dg_mix), parts_mid = _qkv_bwd(dqkv, dx2, x1, mod, g_mix, win, tp, beside([gwin], True))

    (dx0, dgate1, dup1, act1, h1, df1, dmod1, dg_ffn1), _ = _ffn_bwd(
        dx1, x0, f1, mod, g_ffn1, gate1, up1, wg1, wu1, wd1, 0, tp)
    dmod = jnp.concatenate([dmod1[:, 0:3], dmod2a[:, 0:2], dmod2b[:, 2:3], dmod3[:, 0:3]], axis=1)
    ggrads = (dg_ffn1[0:1], dg_mix[0:1], dg_heads[0:1], drel, dg_ffn2[0:1], dg_final)
    if not distributed:
        gw1 = _ffn_weight_grads(h1, dgate1, dup1, act1, df1, tg, 0)
        return loss_sum, dx0.reshape(nb, seq, d), tuple(gw1) + (gwin, gwout, gwg2, gwu2, gwd2), dmod, ggrads

    dg_heads_row, drel_flat = dg_heads[0:1], drel.reshape(1, -1)
    width = max(d, dg_heads_row.shape[1], drel_flat.shape[1])
    small = jnp.concatenate(
        [_pad_to(a.reshape(1, -1), 1, width)
         for a in (dg_ffn1[0:1], dg_mix[0:1], dg_ffn2[0:1], dg_final, dg_heads_row, drel_flat, loss_sum)]
        + [jnp.zeros((1, width), F32)], axis=0)
    dmod_pad = _pad_to(dmod.reshape(nb, N_MOD * d), 0, 8)
    everyone = _Exchange([jnp.broadcast_to(dmod_pad, (N_DEV,) + dmod_pad.shape),
                          jnp.broadcast_to(small, (N_DEV,) + small.shape)], True)
    sent_g, sent_u, gwd1, (dmod_all, small_all) = _ffn_weight_grads(
        h1, dgate1, dup1, act1, df1, tg, 0, stream=True, first=everyone)
    wgrads = (sent_g, sent_u, gwd1) + tuple(parts_mid + parts_late)
    return dx0.reshape(nb, seq, d), wgrads, dmod_all, small_all


def kernel(x, c, w_ada, b_ada, g_ffn1, w1_gate, w1_up, w1_down, g_mix, w_in, g_sb_out, g_dil_out, w_out, rel_bias, g_ffn2, w2_gate, w2_up, w2_down, g_final, loss_target, m_w_ada, m_b_ada, m_g_ffn1, m_w1_gate, m_w1_up, m_w1_down, m_g_mix, m_w_in, m_g_sb_out, m_g_dil_out, m_w_out, m_rel_bias, m_g_ffn2, m_w2_gate, m_w2_up, m_w2_down, m_g_final, v_w_ada, v_b_ada, v_g_ffn1, v_w1_gate, v_w1_up, v_w1_down, v_g_mix, v_w_in, v_g_sb_out, v_g_dil_out, v_w_out, v_rel_bias, v_g_ffn2, v_w2_gate, v_w2_up, v_w2_down, v_g_final):
    nb, seq, d = x.shape
    me = 4 * lax.axis_index("x") + 2 * lax.axis_index("y") + lax.axis_index("c")
    tm = min(TOKEN_TILE, seq)
    fs = w1_gate.shape[2]
    fs_pad = _lane_pad(fs)
    ada_cols = w_ada.shape[2]

    def col_shard(w):
        return _pad_to(w[0].astype(BF16), 1, fs_pad)

    def row_shard(w):
        return _pad_to(w[0].astype(BF16), 0, fs_pad)

    shards = [col_shard(w1_gate), col_shard(w1_up), row_shard(w1_down), w_in[0].astype(BF16),
              w_out[0].astype(BF16), col_shard(w2_gate), col_shard(w2_up), row_shard(w2_down)]
    b_cols = lax.dynamic_slice(b_ada, (0, me * ada_cols), (1, ada_cols))
    c_every, mod_all, first = _first_exchange(_pad_to(c, 0, 8), shards[:3], w_ada[0], b_cols)
    c_all = c_every[:, :nb].reshape(N_DEV * nb, d)
    weights = first + shards[3:]
    mod = lax.dynamic_slice(mod_all, (0, me * 8, 0), (N_DEV, nb, ada_cols))
    mod = mod.transpose(1, 0, 2).reshape(nb, N_MOD, d)

    n_sb = g_sb_out.shape[1] * g_sb_out.shape[2]
    gains = (g_ffn1, g_mix, g_sb_out.reshape(1, n_sb), g_dil_out.reshape(1, -1), g_ffn2,
             g_final.reshape(1, d))
    grad_x, parts, dmod_all, small_all = _local_step(x, loss_target, mod, gains, weights, rel_bias, tm, True)

    last_part = _exchange([parts[2]], True, "scatter_last", chips=[True])[0]
    parts = parts[:2] + (last_part,) + parts[3:]
    dmod_all = dmod_all[:, :nb].reshape(N_DEV * nb, N_MOD * d)
    dmod_cols = lax.dynamic_slice(dmod_all, (0, me * ada_cols), (N_DEV * nb, ada_cols))
    gw_ada, gb_ada = _ada_bwd(c_all, dmod_cols, dmod_all)

    def small_part(row, size, shape):
        return small_all[:, row, :size].reshape((N_DEV,) + shape)

    loss = jnp.sum(small_all[:, 6, 0])

    n_rel = rel_bias.shape
    updates = {
        "w_ada": (w_ada[0], gw_ada[None], m_w_ada[0], v_w_ada[0]),
        "b_ada": (b_ada, gb_ada[None], m_b_ada, v_b_ada),
        "g_ffn1": (g_ffn1, small_part(0, d, (1, d)), m_g_ffn1, v_g_ffn1),
        "w1_gate": (w1_gate[0], parts[0], m_w1_gate[0], v_w1_gate[0]),
        "w1_up": (w1_up[0], parts[1], m_w1_up[0], v_w1_up[0]),
        "w1_down": (w1_down[0], parts[2], m_w1_down[0], v_w1_down[0]),
        "g_mix": (g_mix, small_part(1, d, (1, d)), m_g_mix, v_g_mix),
        "w_in": (w_in[0], parts[3], m_w_in[0], v_w_in[0]),
        "g_sb_out": (g_sb_out[0], small_all[:, 4, :n_sb].reshape((N_DEV,) + g_sb_out.shape[1:]),
                     m_g_sb_out[0], v_g_sb_out[0]),
        "g_dil_out": (g_dil_out[0], small_all[:, 4, n_sb:n_sb + g_dil_out[0].size].reshape((N_DEV,) + g_dil_out.shape[1:]),
                      m_g_dil_out[0], v_g_dil_out[0]),
        "w_out": (w_out[0], parts[4], m_w_out[0], v_w_out[0]),
        "rel_bias": (rel_bias, small_part(5, rel_bias.size, n_rel), m_rel_bias, v_rel_bias),
        "g_ffn2": (g_ffn2, small_part(2, d, (1, d)), m_g_ffn2, v_g_ffn2),
        "w2_gate": (w2_gate[0], parts[5], m_w2_gate[0], v_w2_gate[0]),
        "w2_up": (w2_up[0], parts[6], m_w2_up[0], v_w2_up[0]),
        "w2_down": (w2_down[0], parts[7], m_w2_down[0], v_w2_down[0]),
        "g_final": (g_final.reshape(1, d), small_part(3, d, (1, d)), m_g_final.reshape(1, d), v_g_final.reshape(1, d)),
    }
    shapes = {"w_ada": w_ada.shape, "b_ada": b_ada.shape, "g_ffn1": g_ffn1.shape, "w1_gate": w1_gate.shape,
              "w1_up": w1_up.shape, "w1_down": w1_down.shape, "g_mix": g_mix.shape, "w_in": w_in.shape,
              "g_sb_out": g_sb_out.shape, "g_dil_out": g_dil_out.shape, "w_out": w_out.shape,
              "rel_bias": rel_bias.shape, "g_ffn2": g_ffn2.shape, "w2_gate": w2_gate.shape,
              "w2_up": w2_up.shape, "w2_down": w2_down.shape, "g_final": g_final.shape}
    grads, deltas, new_m, new_v = [], [], [], []
    for name, (w, p, m, v) in updates.items():
        transposed = name in ("w1_gate", "w1_up", "w2_gate", "w2_up")
        outs = _adamw(w, p, m, v, f"adamw_{name}", transposed)
        for dst, a in zip((grads, deltas, new_m, new_v), outs):
            dst.append((a.T if transposed else a).reshape(shapes[name]))
    return (loss, grad_x, *grads, *deltas, *new_m, *new_v)
```

```python
import functools
import math

import numpy as np
import jax
import jax.numpy as jnp
from jax import lax
from jax.experimental import pallas as pl
from jax.experimental.pallas import tpu as pltpu

F32 = jnp.float32
BF16 = jnp.bfloat16

EPS = 1e-6
NEG_INF = -1e30
HEAD_DIM = 64
LANES = 128
DIL_BLOCK = 128
DILATIONS = (1, 4, 16)
N_BUCKETS = 32
MAX_DISTANCE = 2048
N_MOD = 9
N_DEV = 8
SB_BLOCK = 256
SB_HEADS = 4
SB_WIDTH = SB_HEADS * HEAD_DIM
DIL_HEADS = 4
DIL_WIDTH = DIL_HEADS * HEAD_DIM
TOKEN_TILE = 512
PROJ_TILE = 1024
GRAD_TILE = 1024
SHARD_GROUP = 2
FFN_CHUNKS = 2
VMEM_LIMIT_BYTES = 56 * 1024 * 1024

ADAM_LR = 0.001
ADAM_B1 = 0.9
ADAM_B2 = 0.999
ADAM_EPS = 1e-08
ADAM_WD = 0.01
ADAM_STEP = 10

NT_DIMS = (((1,), (1,)), ((), ()))
TN_DIMS = (((0,), (0,)), ((), ()))


def _params(*sem):
    return pltpu.CompilerParams(dimension_semantics=sem, vmem_limit_bytes=VMEM_LIMIT_BYTES)


def _once(spec):
    return pl.BlockSpec(spec.block_shape, spec.index_map, pipeline_mode=pl.Buffered(1))


def _dot(a, b):
    return jnp.dot(a, b, preferred_element_type=F32)


def _dot_nt(a, b):
    return lax.dot_general(a, b, NT_DIMS, preferred_element_type=F32)


def _dot_tn(a, b):
    return lax.dot_general(a, b, TN_DIMS, preferred_element_type=F32)


def _split_dot(a, b):
    hi = a.astype(BF16)
    lo = (a - hi.astype(F32)).astype(BF16)
    return _dot(hi, b) + _dot(lo, b)


def _sigmoid(z):
    return 1.0 / (1.0 + jnp.exp(-z))


def _norm(x):
    r = lax.rsqrt(jnp.mean(x * x, axis=-1, keepdims=True) + EPS)
    return x * r, r


def _modulate(x, g, mod_ref, k):
    n, _ = _norm(x)
    shift = mod_ref[3 * k:3 * k + 1, :]
    scale = mod_ref[3 * k + 1:3 * k + 2, :]
    return n * g * (1.0 + scale) + shift


def _modulate_bwd(dh, x, g, mod_ref, k):
    n, r = _norm(x)
    scale = mod_ref[3 * k + 1:3 * k + 2, :]
    dshift = jnp.sum(dh, axis=0, keepdims=True)
    dscale = jnp.sum(dh * n * g, axis=0, keepdims=True)
    dg = jnp.sum(dh * n * (1.0 + scale), axis=0, keepdims=True)
    dn = dh * g * (1.0 + scale)
    dx = r * (dn - n * jnp.mean(dn * n, axis=-1, keepdims=True))
    return dx, dshift, dscale, dg


class _Exchange:
    def __init__(self, arrays, scatter, relay=False, chips=None):
        assert not (scatter and relay)
        self.arrays = list(arrays)
        self.scatter = scatter
        self.relay = relay
        self.n = len(self.arrays)
        self.chips = list(chips) if chips is not None else [False] * self.n
        assert scatter or not any(self.chips)
        self.out_shape = [
            jax.ShapeDtypeStruct((N_DEV // 2 if ch else N_DEV,) + tuple(a.shape[1:] if scatter else a.shape), a.dtype)
            for a, ch in zip(self.arrays, self.chips)]
        n_remote = self.n * (N_DEV - 1)
        self.scratch_shapes = [pltpu.SemaphoreType.DMA((n_remote,)), pltpu.SemaphoreType.DMA((n_remote,)),
                               pltpu.SemaphoreType.DMA((self.n,))]

    def _copies(self, in_refs, out_refs, sems):
        send_sems, recv_sems, local_sems = sems
        x, y, c = lax.axis_index("x"), lax.axis_index("y"), lax.axis_index("c")
        me = 4 * x + 2 * y + c
        local, remote, relayed = [], {}, {}
        for a in range(self.n):
            if self.chips[a]:
                mine = 2 * x + y
                local.append(pltpu.make_async_copy(in_refs[a].at[mine], out_refs[a].at[mine], local_sems.at[a]))
                for k in (2, 4, 6):
                    px = 1 - x if k & 4 else x
                    py = 1 - y if k & 2 else y
                    sem = a * (N_DEV - 1) + k - 1
                    remote[a, k] = pltpu.make_async_remote_copy(
                        src_ref=in_refs[a].at[2 * px + py], dst_ref=out_refs[a].at[mine],
                        send_sem=send_sems.at[sem], recv_sem=recv_sems.at[sem],
                        device_id=(px, py, c), device_id_type=pl.DeviceIdType.MESH)
                continue
            src = in_refs[a].at[me] if self.scatter else in_refs[a]
            local.append(pltpu.make_async_copy(src, out_refs[a].at[me], local_sems.at[a]))
            for k in range(1, N_DEV):
                px = 1 - x if k & 4 else x
                py = 1 - y if k & 2 else y
                pc = 1 - c if k & 1 else c
                sem = a * (N_DEV - 1) + k - 1
                if self.relay and k & 1 and k > 1:
                    slot = 4 * px + 2 * py + c
                    relayed[a, k] = pltpu.make_async_remote_copy(
                        src_ref=out_refs[a].at[slot], dst_ref=out_refs[a].at[slot],
                        send_sem=send_sems.at[sem], recv_sem=recv_sems.at[sem],
                        device_id=(x, y, 1 - c), device_id_type=pl.DeviceIdType.MESH)
                    continue
                src = in_refs[a].at[4 * px + 2 * py + pc] if self.scatter else in_refs[a]
                remote[a, k] = pltpu.make_async_remote_copy(
                    src_ref=src, dst_ref=out_refs[a].at[me],
                    send_sem=send_sems.at[sem], recv_sem=recv_sems.at[sem],
                    device_id=(px, py, pc), device_id_type=pl.DeviceIdType.MESH)
        return local, remote, relayed

    def start(self, in_refs, out_refs, sems):
        local, remote, _ = self._copies(in_refs, out_refs, sems)
        for cp in local + list(remote.values()):
            cp.start()

    def wait(self, in_refs, out_refs, sems):
        local, remote, relayed = self._copies(in_refs, out_refs, sems)
        for (a, k), cp in relayed.items():
            remote[a, k - 1].wait_recv()
            cp.start()
        for (a, k), cp in remote.items():
            if (a, k + 1) not in relayed:
                cp.wait_recv()
        for cp in relayed.values():
            cp.wait_recv()
        for cp in list(remote.values()) + list(relayed.values()):
            cp.wait_send()
        for cp in local:
            cp.wait()


def _call(body, *, name, args, in_specs, out_specs, out_shape, scratch_shapes=(), grid=(),
          params=None, exchange=None):
    n_in, n_out = len(args), len(out_shape)
    if exchange is None:
        outs = pl.pallas_call(
            body, name=name, grid=grid, in_specs=list(in_specs), out_specs=list(out_specs),
            out_shape=list(out_shape), scratch_shapes=list(scratch_shapes), compiler_params=params,
        )(*args)
        return list(outs), []
    n_ex = exchange.n

    def wrapped(*refs):
        ins, refs = refs[:n_in], refs[n_in:]
        ex_in, refs = refs[:n_ex], refs[n_ex:]
        outs, refs = refs[:n_out], refs[n_out:]
        ex_out, refs = refs[:n_ex], refs[n_ex:]
        scratch, sems = refs[:len(refs) - 3], refs[len(refs) - 3:]
        if not grid:
            exchange.start(ex_in, ex_out, sems)
            body(*ins, *outs, *scratch)
            exchange.wait(ex_in, ex_out, sems)
            return
        first = functools.reduce(jnp.logical_and, [pl.program_id(a) == 0 for a in range(len(grid))])
        last = functools.reduce(jnp.logical_and, [pl.program_id(a) == grid[a] - 1 for a in range(len(grid))])

        @pl.when(first)
        def _():
            exchange.start(ex_in, ex_out, sems)

        body(*ins, *outs, *scratch)

        @pl.when(last)
        def _():
            exchange.wait(ex_in, ex_out, sems)

    any_spec = pl.BlockSpec(memory_space=pl.ANY)
    outs = pl.pallas_call(
        wrapped, name=name, grid=grid,
        in_specs=list(in_specs) + [any_spec] * n_ex, out_specs=list(out_specs) + [any_spec] * n_ex,
        out_shape=list(out_shape) + exchange.out_shape,
        scratch_shapes=list(scratch_shapes) + exchange.scratch_shapes, compiler_params=params,
    )(*args, *exchange.arrays)
    return list(outs[:n_out]), list(outs[n_out:])


def _exchange(arrays, scatter, name, relay=False, chips=None):
    return _call(lambda: None, name=name, args=(), in_specs=(), out_specs=(), out_shape=(),
                 exchange=_Exchange(arrays, scatter, relay, chips))[1]


def _first_exchange(c_pad, shards, w, b):
    rows, d = c_pad.shape
    cols = w.shape[1]
    ex_c = _Exchange([c_pad], False)
    ex_w = _Exchange(shards, False, relay=True)
    ex_m = _Exchange([jax.ShapeDtypeStruct((N_DEV * rows, cols), F32)], False)
    n_w = ex_w.n

    def body(*refs):
        c_ref, w_refs, wa_ref, b_ref = refs[0], refs[1:1 + n_w], refs[1 + n_w], refs[2 + n_w]
        outs = refs[3 + n_w:]
        cg_ref, wg_refs, mg_ref = outs[0], outs[1:1 + n_w], outs[1 + n_w]
        scratch = outs[2 + n_w:]
        sems_c, sems_w, sems_m, c_vm, m_vm = scratch[0:3], scratch[3:6], scratch[6:9], scratch[9], scratch[10]
        ex_c.start([c_ref], [cg_ref], sems_c)
        ex_c.wait([c_ref], [cg_ref], sems_c)
        pltpu.sync_copy(cg_ref, c_vm)
        cv = c_vm[...].reshape(N_DEV * rows, d)
        s = (cv * _sigmoid(cv)).astype(BF16)
        m_vm[...] = _dot(s, wa_ref[...].astype(BF16)) + b_ref[...]
        ex_m.start([m_vm], [mg_ref], sems_m)
        ex_w.start(w_refs, wg_refs, sems_w)
        ex_m.wait([m_vm], [mg_ref], sems_m)
        ex_w.wait(w_refs, wg_refs, sems_w)

    any_spec = pl.BlockSpec(memory_space=pl.ANY)
    vmem_spec = pl.BlockSpec(memory_space=pltpu.VMEM)
    outs = pl.pallas_call(
        body, name="first_exchange",
        in_specs=[any_spec] * (1 + n_w) + [vmem_spec, vmem_spec],
        out_specs=[any_spec] * (2 + n_w),
        out_shape=ex_c.out_shape + ex_w.out_shape + ex_m.out_shape,
        scratch_shapes=ex_c.scratch_shapes + ex_w.scratch_shapes + ex_m.scratch_shapes
        + [pltpu.VMEM((N_DEV, rows, d), F32), pltpu.VMEM((N_DEV * rows, cols), F32)],
        compiler_params=pltpu.CompilerParams(vmem_limit_bytes=VMEM_LIMIT_BYTES),
    )(c_pad, *shards, w, b)
    return outs[0], outs[1 + n_w], list(outs[1:1 + n_w])


def _ada_bwd(c_all, dmod_cols, dmod_all):
    def body(c_ref, dc_ref, da_ref, gw_ref, gb_ref):
        cv = c_ref[...]
        s = cv * _sigmoid(cv)
        gw_ref[...] = lax.dot_general(s, dc_ref[...], TN_DIMS, preferred_element_type=F32,
                                      precision=lax.Precision.HIGHEST)
        gb_ref[...] = jnp.sum(da_ref[...], axis=0, keepdims=True)

    return pl.pallas_call(
        body, name="ada_bwd",
        out_shape=(jax.ShapeDtypeStruct((c_all.shape[1], dmod_cols.shape[1]), F32),
                   jax.ShapeDtypeStruct((1, dmod_all.shape[1]), F32)),
        compiler_params=pltpu.CompilerParams(vmem_limit_bytes=VMEM_LIMIT_BYTES),
    )(c_all, dmod_cols, dmod_all)


def _side_by_side(w_ref):
    return jnp.concatenate([w_ref[s] for s in range(w_ref.shape[0])], axis=1)


def _stacked(w_ref):
    return jnp.concatenate([w_ref[s] for s in range(w_ref.shape[0])], axis=0)


def _loss_tile(x, target, g, acc_ref):
    d = x.shape[1]
    n, r = _norm(x)
    err = n * g - target
    dy = err * (1.0 / d)
    acc_ref[0:1, :] += jnp.sum(err * err, axis=0, keepdims=True)
    acc_ref[1:2, :] += jnp.sum(dy * n, axis=0, keepdims=True)
    dn = dy * g
    return r * (dn - n * jnp.mean(dn * n, axis=-1, keepdims=True))


def _ffn_fwd(x, mod, g, wg, wu, wd, k, tm, exchange=None, head=None):
    t, d = x.shape
    ns, _, fs = wg.shape
    nt = t // tm
    tpb = nt // mod.shape[0]
    rows = tm // FFN_CHUNKS
    extra = list(head) if head is not None else []

    def body(x_ref, mod_ref, g_ref, wg_ref, wu_ref, wd_ref, *rest):
        if head is not None:
            t_ref, gf_ref, xo_ref, f_ref, gg_ref, uu_ref, head_ref, h_sc, acc = rest
        else:
            xo_ref, f_ref, gg_ref, uu_ref, h_sc, acc = rest
        i, j = pl.program_id(0), pl.program_id(1)

        @pl.when(j == 0)
        def _():
            h_sc[...] = _modulate(x_ref[...], g_ref[...], mod_ref, k).astype(BF16)
            acc[...] = jnp.zeros_like(acc)

        chunks = [pl.ds(c * rows, rows) for c in range(FFN_CHUNKS)]
        wg, wu, wd = _side_by_side(wg_ref), _side_by_side(wu_ref), _stacked(wd_ref)
        gates, ups = [], []
        for rs in chunks:
            h = h_sc[rs, :]
            gates.append(_dot(h, wg))
            ups.append(_dot(h, wu))
        acts = [(g * _sigmoid(g) * u).astype(BF16) for g, u in zip(gates, ups)]
        for rs, g, u in zip(chunks, gates, ups):
            for s in range(SHARD_GROUP):
                gg_ref[s, rs, :] = g[:, s * fs:(s + 1) * fs].astype(BF16)
                uu_ref[s, rs, :] = u[:, s * fs:(s + 1) * fs].astype(BF16)
        downs = [_dot(a, wd) for a in acts]
        for rs, dn in zip(chunks, downs):
            acc[rs, :] += dn

        @pl.when(j == ns // SHARD_GROUP - 1)
        def _():
            f = acc[...]
            f_ref[...] = f.astype(BF16)
            xo = x_ref[...] + 0.5 * mod_ref[3 * k + 2:3 * k + 3, :] * f
            if head is None:
                xo_ref[...] = xo
            else:
                @pl.when(i == 0)
                def _():
                    head_ref[...] = jnp.zeros_like(head_ref)

                xo_ref[...] = _loss_tile(xo, t_ref[...], gf_ref[...], head_ref)

    tok = pl.BlockSpec((tm, d), lambda i, j: (i, 0))
    row = pl.BlockSpec((1, d), lambda i, j: (0, 0))
    hid = pl.BlockSpec((SHARD_GROUP, tm, fs), lambda i, j: (j, i, 0))
    head_specs = [_once(tok), row] if head is not None else []
    head_out = [pl.BlockSpec((8, d), lambda i, j: (0, 0))] if head is not None else []
    head_shape = [jax.ShapeDtypeStruct((8, d), F32)] if head is not None else []
    return _call(
        body, name=f"ffn_fwd{k}", grid=(nt, ns // SHARD_GROUP), args=(x, mod, g, wg, wu, wd, *extra),
        in_specs=[tok,
                  pl.BlockSpec((None, N_MOD, d), lambda i, j: (i // tpb, 0, 0)),
                  row,
                  pl.BlockSpec((SHARD_GROUP, d, fs), lambda i, j: (j, 0, 0)),
                  pl.BlockSpec((SHARD_GROUP, d, fs), lambda i, j: (j, 0, 0)),
                  pl.BlockSpec((SHARD_GROUP, fs, d), lambda i, j: (j, 0, 0))] + head_specs,
        out_specs=[tok, tok, hid, hid] + head_out,
        out_shape=[jax.ShapeDtypeStruct((t, d), F32), jax.ShapeDtypeStruct((t, d), BF16),
                   jax.ShapeDtypeStruct((ns, t, fs), BF16), jax.ShapeDtypeStruct((ns, t, fs), BF16)]
        + head_shape,
        scratch_shapes=[pltpu.VMEM((tm, d), BF16), pltpu.VMEM((tm, d), F32)],
        params=_params("arbitrary", "arbitrary"), exchange=exchange)


def _ffn_bwd(dxo, x, f, mod, g, gate, up, wg, wu, wd, k, tm, exchange=None):
    t, d = x.shape
    ns, _, fs = wg.shape
    nt = t // tm
    nb = mod.shape[0]
    tpb = nt // nb
    rows = tm // FFN_CHUNKS

    def body(dxo_ref, x_ref, f_ref, mod_ref, g_ref, gg_ref, uu_ref, wg_ref, wu_ref, wd_ref,
             dx_ref, dgg_ref, duu_ref, act_ref, h_ref, df_ref, dmod_ref, dg_ref, acc):
        i, j = pl.program_id(0), pl.program_id(1)

        @pl.when(j == 0)
        def _():
            df = 0.5 * mod_ref[3 * k + 2:3 * k + 3, :] * dxo_ref[...]
            df_ref[...] = df.astype(BF16)
            h_ref[...] = _modulate(x_ref[...], g_ref[...], mod_ref, k).astype(BF16)
            acc[...] = jnp.zeros_like(acc)

        chunks = [pl.ds(c * rows, rows) for c in range(FFN_CHUNKS)]
        group = range(SHARD_GROUP)
        wg, wu, wd = _side_by_side(wg_ref), _side_by_side(wu_ref), _stacked(wd_ref)
        dacts = [_dot_nt(df_ref[rs, :], wd) for rs in chunks]
        dgates, dups = [], []
        for rs, dact in zip(chunks, dacts):
            gv = jnp.concatenate([gg_ref[s, rs, :] for s in group], axis=1).astype(F32)
            uv = jnp.concatenate([uu_ref[s, rs, :] for s in group], axis=1).astype(F32)
            sig = _sigmoid(gv)
            s_act = gv * sig
            act = (s_act * uv).astype(BF16)
            for s in group:
                act_ref[s, rs, :] = act[:, s * fs:(s + 1) * fs]
            dups.append((dact * s_act).astype(BF16))
            dgates.append((dact * uv * (sig * (1.0 + gv * (1.0 - sig)))).astype(BF16))
        dhs = [_dot_nt(dg, wg) + _dot_nt(du, wu) for dg, du in zip(dgates, dups)]
        for rs, dg, du, dh in zip(chunks, dgates, dups, dhs):
            for s in group:
                dgg_ref[s, rs, :] = dg[:, s * fs:(s + 1) * fs]
                duu_ref[s, rs, :] = du[:, s * fs:(s + 1) * fs]
            acc[rs, :] += dh

        @pl.when(j == ns // SHARD_GROUP - 1)
        def _():
            dx, dshift, dscale, dg = _modulate_bwd(acc[...], x_ref[...], g_ref[...], mod_ref, k)
            dxo_v = dxo_ref[...]
            dx_ref[...] = dxo_v + dx
            dgt = jnp.sum(0.5 * f_ref[...].astype(F32) * dxo_v, axis=0, keepdims=True)

            @pl.when(i % tpb == 0)
            def _():
                dmod_ref[...] = jnp.zeros_like(dmod_ref)

            @pl.when(i == 0)
            def _():
                dg_ref[...] = jnp.zeros_like(dg_ref)

            dmod_ref[0:1, :] += dshift
            dmod_ref[1:2, :] += dscale
            dmod_ref[2:3, :] += dgt
            dg_ref[0:1, :] += dg

    tok = pl.BlockSpec((tm, d), lambda i, j: (i, 0))
    hid = pl.BlockSpec((SHARD_GROUP, tm, fs), lambda i, j: (j, i, 0))
    return _call(
        body, name=f"ffn_bwd{k}", grid=(nt, ns // SHARD_GROUP), args=(dxo, x, f, mod, g, gate, up, wg, wu, wd),
        in_specs=[tok, tok, tok,
                  pl.BlockSpec((None, N_MOD, d), lambda i, j: (i // tpb, 0, 0)),
                  pl.BlockSpec((1, d), lambda i, j: (0, 0)),
                  hid, hid,
                  pl.BlockSpec((SHARD_GROUP, d, fs), lambda i, j: (j, 0, 0)),
                  pl.BlockSpec((SHARD_GROUP, d, fs), lambda i, j: (j, 0, 0)),
                  pl.BlockSpec((SHARD_GROUP, fs, d), lambda i, j: (j, 0, 0))],
        out_specs=[tok, hid, hid, hid, tok, tok,
                   pl.BlockSpec((None, 8, d), lambda i, j: (i // tpb, 0, 0)),
                   pl.BlockSpec((8, d), lambda i, j: (0, 0))],
        out_shape=[jax.ShapeDtypeStruct((t, d), F32),
                   jax.ShapeDtypeStruct((ns, t, fs), BF16), jax.ShapeDtypeStruct((ns, t, fs), BF16),
                   jax.ShapeDtypeStruct((ns, t, fs), BF16),
                   jax.ShapeDtypeStruct((t, d), BF16), jax.ShapeDtypeStruct((t, d), BF16),
                   jax.ShapeDtypeStruct((nb, 8, d), F32), jax.ShapeDtypeStruct((8, d), F32)],
        scratch_shapes=[pltpu.VMEM((tm, d), F32)],
        params=_params("arbitrary", "arbitrary"), exchange=exchange)


def _mm_tn(a, b, a_spec, b_spec, out_shape, n_tiles, name, exchange=None, keep_transposed=False,
           pair_reduce=False):
    n_out = out_shape[0]
    block = tuple(out_shape[1:])
    last = n_tiles - 1
    flip = block[0] > block[1]
    if flip:
        block = block[::-1]
    if flip and keep_transposed:
        flip_back, out_shape = False, (n_out,) + block
    else:
        flip_back = flip
    full_shape = tuple(out_shape)
    n_pairs = n_out // 2
    if pair_reduce:
        out_shape = (n_pairs,) + full_shape[1:]

    def body(a_ref, b_ref, o_ref, acc, *pair):
        i, j = pl.program_id(0), pl.program_id(1)
        prod = _dot_tn(b_ref[...], a_ref[...]) if flip else _dot_tn(a_ref[...], b_ref[...])
        full_ref = pair[0] if pair_reduce else o_ref

        @pl.when(i == 0)
        def _():
            acc[j] = prod

        @pl.when(i > 0)
        def _():
            acc[j] += prod

        @pl.when(i == last)
        def _():
            total = acc[j]
            full_ref[j] = (total.T if flip_back else total).astype(BF16)

        if pair_reduce:
            _, landed, send_sems, recv_sems = pair

            @pl.when(jnp.logical_and(i == last, j == n_out - 1))
            def _():
                x, y, c = lax.axis_index("x"), lax.axis_index("y"), lax.axis_index("c")
                copies = [pltpu.make_async_remote_copy(
                    src_ref=full_ref.at[2 * q + 1 - c], dst_ref=landed.at[q],
                    send_sem=send_sems.at[q], recv_sem=recv_sems.at[q],
                    device_id=(x, y, 1 - c), device_id_type=pl.DeviceIdType.MESH) for q in range(n_pairs)]
                for cp in copies:
                    cp.start()
                for q, cp in enumerate(copies):
                    cp.wait_recv()
                    o_ref[q] = (full_ref[2 * q + c].astype(F32) + landed[q].astype(F32)).astype(BF16)
                for cp in copies:
                    cp.wait_send()

    scratch = [pltpu.VMEM((n_out,) + block, F32)]
    if pair_reduce:
        scratch += [pltpu.VMEM(full_shape, BF16), pltpu.VMEM(out_shape, BF16),
                    pltpu.SemaphoreType.DMA((n_pairs,)), pltpu.SemaphoreType.DMA((n_pairs,))]
    outs, sent = _call(
        body, name=name, grid=(n_tiles, n_out), args=(a, b), in_specs=[a_spec, b_spec],
        out_specs=[pl.BlockSpec(out_shape, lambda i, j: (0,) * len(out_shape))],
        out_shape=[jax.ShapeDtypeStruct(out_shape, BF16)],
        scratch_shapes=scratch,
        params=_params("arbitrary", "arbitrary"), exchange=exchange)
    return (outs[0], sent) if exchange is not None else outs[0]


def _ffn_weight_grads(h, dgate, dup, act, df, tm, tag, stream=False, first=None):
    t, d = h.shape
    ns, _, fs = dgate.shape
    nt = t // tm
    tok = pl.BlockSpec((tm, d), lambda i, j: (i, 0))
    hid = pl.BlockSpec((None, tm, fs), lambda i, j: (j, i, 0))
    if not stream:
        gwg = _mm_tn(h, dgate, tok, hid, (ns, d, fs), nt, f"grad_wg{tag}", keep_transposed=True)
        gwu = _mm_tn(h, dup, tok, hid, (ns, d, fs), nt, f"grad_wu{tag}", keep_transposed=True)
        gwd = _mm_tn(act, df, hid, tok, (ns, fs, d), nt, f"grad_wd{tag}")
        return gwg, gwu, gwd
    gwg, brought = _mm_tn(h, dgate, tok, hid, (ns, d, fs), nt, f"grad_wg{tag}", first,
                          keep_transposed=True, pair_reduce=True)
    gwu, sent_g = _mm_tn(h, dup, tok, hid, (ns, d, fs), nt, f"grad_wu{tag}",
                         _Exchange([gwg], True, chips=[True]), keep_transposed=True, pair_reduce=True)
    gwd, sent_u = _mm_tn(act, df, hid, tok, (ns, fs, d), nt, f"grad_wd{tag}",
                         _Exchange([gwu], True, chips=[True]), pair_reduce=True)
    return sent_g[0], sent_u[0], gwd, brought


def _stage_shape(rows, cols):
    return pltpu.VMEM((cols // LANES, rows, LANES), F32)


def _stage(value, stage_ref):
    for k in range(stage_ref.shape[0]):
        stage_ref[k] = value[:, k * LANES:(k + 1) * LANES]


def _to_residue_rows(stage_ref, dst_ref, dil):
    rows = stage_ref.shape[1] // dil
    for r in range(dil):
        for k in range(stage_ref.shape[0]):
            dst_ref[r, :, k * LANES:(k + 1) * LANES] = (
                stage_ref.at[k][pl.ds(r, rows, stride=dil), :].astype(dst_ref.dtype))


def _from_residue_rows(src_ref, stage_ref, dil):
    rows = stage_ref.shape[1] // dil
    chunks = range(stage_ref.shape[0])
    for r in range(dil):
        for k in chunks:
            stage_ref.at[k][pl.ds(r, rows, stride=dil), :] = src_ref[r, :, k * LANES:(k + 1) * LANES].astype(F32)
    return jnp.concatenate([stage_ref[k] for k in chunks], axis=1)


def _residue_shape(nb, seq, width, dil, dtype):
    return jax.ShapeDtypeStruct((nb, dil, seq // dil, width), dtype)


def _residue_spec(tm, tpb, cols, dil, col_block):
    return pl.BlockSpec((None, dil, tm // dil, cols),
                        lambda i, *rest: (i // tpb, 0, i % tpb, col_block(i, *rest)))


def _qkv_fwd(x, mod, g, win, tm, exchange=None):
    t, d = x.shape
    ns, _, cs = win.shape
    nt = t // tm
    nb = mod.shape[0]
    tpb = nt // nb
    seq = t // nb
    width = ns * cs // 2
    cs, ns = cs * SHARD_GROUP, ns // SHARD_GROUP
    half = ns // 2
    n_res = len(DILATIONS) - 1

    def body(x_ref, mod_ref, g_ref, w_ref, sb_ref, dil_ref, *rest):
        res_refs, h_ref, sc = rest[:n_res], rest[n_res], rest[n_res + 1]
        j = pl.program_id(1)

        @pl.when(j == 0)
        def _():
            h_ref[...] = _modulate(x_ref[...], g_ref[...], mod_ref, 1).astype(BF16)

        res = _dot(h_ref[...], _side_by_side(w_ref))

        @pl.when(j < half)
        def _():
            sb_ref[...] = res.astype(BF16)

        @pl.when(j >= half)
        def _():
            dil_ref[...] = res.astype(BF16)
            _stage(res, sc)
            for ref, dil in zip(res_refs, DILATIONS[1:]):
                _to_residue_rows(sc, ref, dil)

    def dil_col(i, j):
        return jnp.maximum(j - half, 0)

    tok = pl.BlockSpec((tm, d), lambda i, j: (i, 0))
    wide = jax.ShapeDtypeStruct((t, width), BF16)
    outs, got = _call(
        body, name="qkv_fwd", grid=(nt, ns), args=(x, mod, g, win),
        in_specs=[tok,
                  pl.BlockSpec((None, N_MOD, d), lambda i, j: (i // tpb, 0, 0)),
                  pl.BlockSpec((1, d), lambda i, j: (0, 0)),
                  pl.BlockSpec((SHARD_GROUP, d, cs // SHARD_GROUP), lambda i, j: (j, 0, 0))],
        out_specs=[pl.BlockSpec((tm, cs), lambda i, j: (i, jnp.minimum(j, half - 1))),
                   pl.BlockSpec((tm, cs), lambda i, j: (i, dil_col(i, j)))]
        + [_residue_spec(tm, tpb, cs, dil, dil_col) for dil in DILATIONS[1:]] + [tok],
        out_shape=[wide, wide] + [_residue_shape(nb, seq, width, dil, BF16) for dil in DILATIONS[1:]]
        + [jax.ShapeDtypeStruct((t, d), BF16)],
        scratch_shapes=[_stage_shape(tm, cs)],
        params=_params("arbitrary", "arbitrary"), exchange=exchange)
    qkv_dil = [outs[1]] + [a.reshape(t, width) for a in outs[2:2 + n_res]]
    return (outs[0], qkv_dil, outs[-1]), got


def _qkv_bwd(dqkv, dxo, x, mod, g, win, tm, exchange=None):
    t, d = x.shape
    ns, _, cs = win.shape
    nt = t // tm
    nb = mod.shape[0]
    tpb = nt // nb
    cs, ns = cs * SHARD_GROUP, ns // SHARD_GROUP

    def body(dq_ref, dxo_ref, x_ref, mod_ref, g_ref, w_ref, dx_ref, dmod_ref, dg_ref, acc):
        i, j = pl.program_id(0), pl.program_id(1)

        @pl.when(j == 0)
        def _():
            acc[...] = jnp.zeros_like(acc)

        acc[...] += _dot_nt(dq_ref[...], _side_by_side(w_ref))

        @pl.when(j == ns - 1)
        def _():
            dx, dshift, dscale, dg = _modulate_bwd(acc[...], x_ref[...], g_ref[...], mod_ref, 1)
            dx_ref[...] = dxo_ref[...] + dx

            @pl.when(i % tpb == 0)
            def _():
                dmod_ref[...] = jnp.zeros_like(dmod_ref)

            @pl.when(i == 0)
            def _():
                dg_ref[...] = jnp.zeros_like(dg_ref)

            dmod_ref[0:1, :] += dshift
            dmod_ref[1:2, :] += dscale
            dg_ref[0:1, :] += dg

    tok = pl.BlockSpec((tm, d), lambda i, j: (i, 0))
    return _call(
        body, name="qkv_bwd", grid=(nt, ns), args=(dqkv, dxo, x, mod, g, win),
        in_specs=[pl.BlockSpec((tm, cs), lambda i, j: (i, j)), tok, tok,
                  pl.BlockSpec((None, N_MOD, d), lambda i, j: (i // tpb, 0, 0)),
                  pl.BlockSpec((1, d), lambda i, j: (0, 0)),
                  pl.BlockSpec((SHARD_GROUP, d, cs // SHARD_GROUP), lambda i, j: (j, 0, 0))],
        out_specs=[tok,
                   pl.BlockSpec((None, 8, d), lambda i, j: (i // tpb, 0, 0)),
                   pl.BlockSpec((8, d), lambda i, j: (0, 0))],
        out_shape=[jax.ShapeDtypeStruct((t, d), F32),
                   jax.ShapeDtypeStruct((nb, 8, d), F32), jax.ShapeDtypeStruct((8, d), F32)],
        scratch_shapes=[pltpu.VMEM((tm, d), F32)],
        params=_params("arbitrary", "arbitrary"), exchange=exchange)


def _heads(a):
    return [a[:, h * HEAD_DIM:(h + 1) * HEAD_DIM] for h in range(a.shape[1] // HEAD_DIM)]


def _own_lanes():
    lane = lax.broadcasted_iota(jnp.int32, (1, LANES), 1)
    return [lane < HEAD_DIM, lane >= HEAD_DIM]


def _pair_tiles(a):
    return [a[:, (h // 2) * LANES:(h // 2 + 1) * LANES] for h in range(a.shape[1] // HEAD_DIM)]


def _own_tiles(a, own):
    return [jnp.where(own[h % 2], tile, jnp.zeros_like(tile)) for h, tile in enumerate(_pair_tiles(a))]


def _merge_tiles(per_head, own):
    return jnp.concatenate([jnp.where(own[0], per_head[h], per_head[h + 1])
                            for h in range(0, len(per_head), 2)], axis=1)


def _scaled(q):
    return (q.astype(F32) * (HEAD_DIM ** -0.5)).astype(BF16)


def _sb_logits(qh, kh, tri, causal):
    zs = [_dot_nt(q, k) for q, k in zip(qh, kh)]
    es = [jnp.exp(-jnp.abs(z)) for z in zs]
    log_nots = [-(jnp.maximum(z, 0.0) + jnp.log(1.0 + e)) for z, e in zip(zs, es)]
    if causal is not None:
        log_nots = [jnp.where(causal, ln, 0.0) for ln in log_nots]
    return zs, es, [_split_dot(ln, tri) for ln in log_nots]


def _sb_masks():
    rows = lax.broadcasted_iota(jnp.int32, (SB_BLOCK, SB_BLOCK), 0)
    cols = lax.broadcasted_iota(jnp.int32, (SB_BLOCK, SB_BLOCK), 1)
    return (rows >= cols).astype(BF16), (rows <= cols).astype(BF16), cols < rows


def _sb_fwd(qkv, nb, seq, exchange=None):
    t = qkv.shape[0]
    n_pairs = (qkv.shape[1] // 3) // SB_WIDTH
    tb = SB_BLOCK
    n_blk = seq // tb

    def body(q_ref, k_ref, v_ref, o_ref, c_ref):
        tri, _, causal = _sb_masks()
        own = _own_lanes()

        def key_block(qh, kj, carry, mask):
            ks = pl.multiple_of(kj * tb, tb)
            kh, vh = _pair_tiles(k_ref[pl.ds(ks, tb), :]), _pair_tiles(v_ref[pl.ds(ks, tb), :])
            zs, _, suffixes = _sb_logits(qh, kh, tri, mask)
            ws = [jnp.exp(z + suffix + cr[1]) for z, suffix, cr in zip(zs, suffixes, carry)]
            if mask is not None:
                ws = [jnp.where(mask, w, 0.0) for w in ws]
            pv = [_dot(w.astype(BF16), v) for w, v in zip(ws, vh)]
            return tuple((cr[0] + p, cr[1] + suffix[:, 0:1]) for cr, p, suffix in zip(carry, pv, suffixes))

        def query_block(qi, _):
            qs = pl.multiple_of(qi * tb, tb)
            qh = _own_tiles(_scaled(q_ref[pl.ds(qs, tb), :]), own)
            zero = (jnp.zeros((tb, LANES), F32), jnp.zeros((tb, 1), F32))
            carry = key_block(qh, qi, (zero,) * SB_HEADS, causal)
            carry = lax.fori_loop(0, qi, lambda it, cr: key_block(qh, qi - 1 - it, cr, None), carry)
            o_ref[pl.ds(qs, tb), :] = _merge_tiles([cr[0] for cr in carry], own)
            c_ref[pl.ds(qs, tb), :] = _merge_tiles([jnp.broadcast_to(cr[1], (tb, LANES)) for cr in carry], own)
            return 0

        lax.fori_loop(0, n_blk, query_block, 0)

    def spec(offset):
        return pl.BlockSpec((seq, SB_WIDTH), lambda b, p: (b, offset + p))

    out = jax.ShapeDtypeStruct((t, n_pairs * SB_WIDTH), F32)
    return _call(
        body, name="sb_fwd", grid=(nb, n_pairs), args=(qkv, qkv, qkv),
        in_specs=[spec(0), spec(n_pairs), spec(2 * n_pairs)],
        out_specs=[spec(0), spec(0)], out_shape=[out, out],
        params=_params("arbitrary", "arbitrary"), exchange=exchange)


def _sb_bwd(qkv, do, csum, nb, seq, exchange=None):
    t = qkv.shape[0]
    n_pairs = (qkv.shape[1] // 3) // SB_WIDTH
    tb = SB_BLOCK
    n_blk = seq // tb
    scale = HEAD_DIM ** -0.5

    def body(q_ref, k_ref, v_ref, do_ref, c_ref, dq_ref, dk_ref, dv_ref, dkt_acc, dvt_acc):
        tri, tri_prefix, causal = _sb_masks()
        own = _own_lanes()
        dkt_acc[...] = jnp.zeros_like(dkt_acc)
        dvt_acc[...] = jnp.zeros_like(dvt_acc)

        def key_blocks(qh, qth, doh, doth, ch, kjs, carry, mask):
            nh = SB_HEADS
            chains = range(nh * len(kjs))
            kss = [pl.multiple_of(kj * tb, tb) for kj in kjs]
            kh = [tile for ks in kss for tile in _pair_tiles(k_ref[pl.ds(ks, tb), :])]
            vh = [tile for ks in kss for tile in _pair_tiles(v_ref[pl.ds(ks, tb), :])]
            zs, es, suffixes = _sb_logits(qh * len(kjs), kh, tri, mask)
            dws = [_dot_nt(doh[c % nh], vh[c]) for c in chains]
            lefts = []
            for c in chains:
                before = carry[c][1] if c < nh else lefts[c - nh]
                lefts.append(before + suffixes[c][:, 0:1])
            ws = [jnp.exp(zs[c] + suffixes[c] + (ch[c % nh] - lefts[c])) for c in chains]
            if mask is not None:
                ws = [jnp.where(mask, w, 0.0) for w in ws]
            dlws = [ws[c] * dws[c] for c in chains]
            dprefixes = [_split_dot(dlw, tri_prefix) for dlw in dlws]
            dvts = [_dot(doth[c % nh], ws[c].astype(BF16)) for c in chains]
            dlefts, dzbs = [], []
            for c in chains:
                dlefts.append(carry[c][2] if c < nh else dlefts[c - nh] + dprefixes[c - nh][:, tb - 1:tb])
                sig = jnp.where(zs[c] >= 0.0, 1.0, es[c]) * pl.reciprocal(1.0 + es[c], approx=True)
                dz = dlws[c] - sig * (dlefts[c] + dprefixes[c])
                if mask is not None:
                    dz = jnp.where(mask, dz, 0.0)
                dzbs.append(dz.astype(BF16))
            dkts = [_dot(qth[c % nh], dzbs[c]) for c in chains]
            dqs = [_dot(dzbs[c], kh[c]) for c in chains]
            for b, ks in enumerate(kss):
                pairs = range(b * nh, (b + 1) * nh, 2)
                dkt_acc[:, pl.ds(ks, tb)] += jnp.concatenate([dkts[c] + dkts[c + 1] for c in pairs], axis=0)
                dvt_acc[:, pl.ds(ks, tb)] += jnp.concatenate([dvts[c] + dvts[c + 1] for c in pairs], axis=0)
            last = (len(kjs) - 1) * nh
            return tuple((carry[h][0] + sum(dqs[h::nh]), lefts[last + h],
                          dlefts[last + h] + dprefixes[last + h][:, tb - 1:tb]) for h in range(nh))

        def query_block(qi, _):
            qs = pl.multiple_of(qi * tb, tb)
            qh = _own_tiles(_scaled(q_ref[pl.ds(qs, tb), :]), own)
            doh = _own_tiles(do_ref[pl.ds(qs, tb), :], own)
            qth = [a.astype(F32).T.astype(BF16) for a in qh]
            doth = [a.T.astype(BF16) for a in doh]
            doh = [a.astype(BF16) for a in doh]
            cv = c_ref[pl.ds(qs, tb), :]
            ch = [cv[:, h * HEAD_DIM:h * HEAD_DIM + 1] for h in range(SB_HEADS)]
            zero = (jnp.zeros((tb, LANES), F32), jnp.zeros((tb, 1), F32), jnp.zeros((tb, 1), F32))

            def key_block(kjs, cr, mask):
                return key_blocks(qh, qth, doh, doth, ch, kjs, cr, mask)

            carry = lax.fori_loop(0, qi // 2, lambda p, cr: key_block([2 * p, 2 * p + 1], cr, None),
                                  (zero,) * SB_HEADS)
            carry = lax.fori_loop(0, qi % 2, lambda _, cr: key_block([qi - 1], cr, None), carry)
            carry = key_block([qi], carry, causal)
            dq = _merge_tiles([cr[0] for cr in carry], own) * scale
            dq_ref[pl.ds(qs, tb), :] = dq.astype(BF16)
            return 0

        lax.fori_loop(0, n_blk, query_block, 0)
        dk_ref[...] = dkt_acc[...].T.astype(BF16)
        dv_ref[...] = dvt_acc[...].T.astype(BF16)

    def spec(offset):
        return pl.BlockSpec((seq, SB_WIDTH), lambda b, p: (b, offset + p))

    out = jax.ShapeDtypeStruct((t, n_pairs * SB_WIDTH), BF16)
    return _call(
        body, name="sb_bwd", grid=(nb, n_pairs), args=(qkv, qkv, qkv, do, csum),
        in_specs=[spec(0), spec(n_pairs), spec(2 * n_pairs), spec(0), spec(0)],
        out_specs=[spec(0), spec(0), spec(0)],
        out_shape=[out, out, out],
        scratch_shapes=[pltpu.VMEM((SB_WIDTH, seq), F32), pltpu.VMEM((SB_WIDTH, seq), F32)],
        params=_params("arbitrary", "arbitrary"), exchange=exchange)


def _dil_block_scores(qh, kph, kch, bias_ref, has_prev, band_prev, band_cur):
    scale = HEAD_DIM ** -0.5
    heads = range(len(qh))
    no_prev = jnp.where(has_prev, 0.0, NEG_INF)
    zps = [_dot_nt(qh[h], kph[h]) for h in heads]
    zcs = [_dot_nt(qh[h], kch[h]) for h in heads]
    zps = [jnp.where(band_prev, zps[h] * scale + bias_ref[h, :, 0:DIL_BLOCK], NEG_INF) + no_prev for h in heads]
    zcs = [jnp.where(band_cur, zcs[h] * scale + bias_ref[h, :, DIL_BLOCK:2 * DIL_BLOCK], NEG_INF) for h in heads]
    return zps, zcs


def _dil_bands():
    rows = lax.broadcasted_iota(jnp.int32, (DIL_BLOCK, DIL_BLOCK), 0)
    cols = lax.broadcasted_iota(jnp.int32, (DIL_BLOCK, DIL_BLOCK), 1)
    return cols >= rows, cols <= rows


def _dil_fwd(qkv, bias, nb, seq, dil, exchange=None):
    t, width = qkv.shape
    n_pairs = (width // 3) // DIL_WIDTH
    bq = DIL_BLOCK
    n_blk = seq // bq
    per_seq = n_blk // dil
    heads = range(DIL_HEADS)

    def body(q_ref, k_ref, v_ref, bias_ref, o_ref, lse_ref):
        band_prev, band_cur = _dil_bands()
        own = _own_lanes()

        def block(n, _):
            has_prev = (n & (per_seq - 1)) != 0
            qs = pl.multiple_of(n * bq, bq)
            ps = pl.multiple_of(jnp.maximum(n - 1, 0) * bq, bq)
            qh = _own_tiles(q_ref[pl.ds(qs, bq), :], own)
            kp, kc = _pair_tiles(k_ref[pl.ds(ps, bq), :]), _pair_tiles(k_ref[pl.ds(qs, bq), :])
            vp, vc = _pair_tiles(v_ref[pl.ds(ps, bq), :]), _pair_tiles(v_ref[pl.ds(qs, bq), :])
            zps, zcs = _dil_block_scores(qh, kp, kc, bias_ref, has_prev, band_prev, band_cur)
            ms = [jnp.maximum(jnp.max(zps[h], axis=1, keepdims=True), jnp.max(zcs[h], axis=1, keepdims=True))
                  for h in heads]
            eps = [jnp.exp(zps[h] - ms[h]) for h in heads]
            ecs = [jnp.exp(zcs[h] - ms[h]) for h in heads]
            pvs = [_dot(eps[h].astype(BF16), vp[h]) + _dot(ecs[h].astype(BF16), vc[h]) for h in heads]
            dens = [jnp.sum(eps[h], axis=1, keepdims=True) + jnp.sum(ecs[h], axis=1, keepdims=True) for h in heads]
            o_ref[pl.ds(qs, bq), :] = _merge_tiles([pvs[h] / dens[h] for h in heads], own)
            lse_ref[pl.ds(qs, bq), :] = _merge_tiles(
                [jnp.broadcast_to(ms[h] + jnp.log(dens[h]), (bq, LANES)) for h in heads], own)
            return 0

        lax.fori_loop(0, n_blk, block, 0)

    def spec(offset):
        return pl.BlockSpec((seq, DIL_WIDTH), lambda b, p: (b, offset + p))

    out = jax.ShapeDtypeStruct((t, n_pairs * DIL_WIDTH), F32)
    return _call(
        body, name=f"dil_fwd{dil}", grid=(nb, n_pairs), args=(qkv, qkv, qkv, bias),
        in_specs=[spec(0), spec(n_pairs), spec(2 * n_pairs),
                  pl.BlockSpec((DIL_HEADS, bq, 2 * bq), lambda b, p: (p, 0, 0))],
        out_specs=[spec(0), spec(0)], out_shape=[out, out],
        params=_params("arbitrary", "arbitrary"), exchange=exchange)


def _dil_bwd(qkv, bias, do, lse, delta, nb, seq, dil):
    t, width = qkv.shape
    n_pairs = (width // 3) // DIL_WIDTH
    bq = DIL_BLOCK
    n_blk = seq // bq
    per_seq = n_blk // dil
    scale = HEAD_DIM ** -0.5
    heads = range(DIL_HEADS)

    def body(q_ref, k_ref, v_ref, bias_ref, do_ref, lse_ref, dl_ref, dq_ref, dk_ref, dv_ref, db_ref,
             dk_acc, dv_acc):
        band_prev, band_cur = _dil_bands()
        own = _own_lanes()
        dk_acc[...] = jnp.zeros_like(dk_acc)
        dv_acc[...] = jnp.zeros_like(dv_acc)

        @pl.when(pl.program_id(1) == 0)
        def _():
            db_ref[...] = jnp.zeros_like(db_ref)

        def block(n, _):
            has_prev = (n & (per_seq - 1)) != 0
            qs = pl.multiple_of(n * bq, bq)
            ps = pl.multiple_of(jnp.maximum(n - 1, 0) * bq, bq)
            qh = _own_tiles(q_ref[pl.ds(qs, bq), :], own)
            kp, kc = _pair_tiles(k_ref[pl.ds(ps, bq), :]), _pair_tiles(k_ref[pl.ds(qs, bq), :])
            vp, vc = _pair_tiles(v_ref[pl.ds(ps, bq), :]), _pair_tiles(v_ref[pl.ds(qs, bq), :])
            doh = _own_tiles(do_ref[pl.ds(qs, bq), :].astype(BF16), own)
            lse_v, dl_v = lse_ref[pl.ds(qs, bq), :], dl_ref[pl.ds(qs, bq), :]
            zps, zcs = _dil_block_scores(qh, kp, kc, bias_ref, has_prev, band_prev, band_cur)
            dpp = [_dot_nt(doh[h], vp[h]) for h in heads]
            dpc = [_dot_nt(doh[h], vc[h]) for h in heads]
            lse_h = [lse_v[:, h * HEAD_DIM:h * HEAD_DIM + 1] for h in heads]
            dl_h = [dl_v[:, h * HEAD_DIM:h * HEAD_DIM + 1] for h in heads]
            pps = [jnp.exp(zps[h] - lse_h[h]) for h in heads]
            pcs = [jnp.exp(zcs[h] - lse_h[h]) for h in heads]
            dvp = [_dot_tn(pps[h].astype(BF16), doh[h]) for h in heads]
            dvc = [_dot_tn(pcs[h].astype(BF16), doh[h]) for h in heads]
            dzps = [pps[h] * (dpp[h] - dl_h[h]) for h in heads]
            dzcs = [pcs[h] * (dpc[h] - dl_h[h]) for h in heads]
            dzp_b = [(dzps[h] * scale).astype(BF16) for h in heads]
            dzc_b = [(dzcs[h] * scale).astype(BF16) for h in heads]
            dqs = [_dot(dzp_b[h], kp[h]) + _dot(dzc_b[h], kc[h]) for h in heads]
            dkp = [_dot_tn(dzp_b[h], qh[h]) for h in heads]
            dkc = [_dot_tn(dzc_b[h], qh[h]) for h in heads]
            for h in heads:
                db_ref[h, :, 0:bq] += dzps[h]
                db_ref[h, :, bq:2 * bq] += dzcs[h]
            def pair_sums(per_head):
                return jnp.concatenate([per_head[h] + per_head[h + 1] for h in heads[::2]], axis=1)

            dq_ref[pl.ds(qs, bq), :] = _merge_tiles(dqs, own).astype(BF16)
            dk_acc[pl.ds(ps, bq), :] += pair_sums(dkp)
            dk_acc[pl.ds(qs, bq), :] += pair_sums(dkc)
            dv_acc[pl.ds(ps, bq), :] += pair_sums(dvp)
            dv_acc[pl.ds(qs, bq), :] += pair_sums(dvc)
            return 0

        lax.fori_loop(0, n_blk, block, 0)
        dk_ref[...] = dk_acc[...].astype(BF16)
        dv_ref[...] = dv_acc[...].astype(BF16)

    def spec(offset):
        return pl.BlockSpec((seq, DIL_WIDTH), lambda p, b: (b, offset + p))

    bias_spec = pl.BlockSpec((DIL_HEADS, bq, 2 * bq), lambda p, b: (p, 0, 0))
    out = jax.ShapeDtypeStruct((t, n_pairs * DIL_WIDTH), BF16)
    return pl.pallas_call(
        body, name=f"dil_bwd{dil}", grid=(n_pairs, nb),
        in_specs=[spec(0), spec(n_pairs), spec(2 * n_pairs), bias_spec, spec(0), spec(0), spec(0)],
        out_specs=[spec(0), spec(0), spec(0), bias_spec],
        out_shape=[out, out, out, jax.ShapeDtypeStruct(bias.shape, F32)],
        scratch_shapes=[pltpu.VMEM((seq, DIL_WIDTH), F32), pltpu.VMEM((seq, DIL_WIDTH), F32)],
        compiler_params=_params("arbitrary", "arbitrary"),
    )(qkv, qkv, qkv, bias, do, lse, delta)


def _head_blocks(width):
    rows = lax.broadcasted_iota(jnp.int32, (width, width), 0) // HEAD_DIM
    cols = lax.broadcasted_iota(jnp.int32, (width, width), 1) // HEAD_DIM
    return (rows == cols).astype(BF16)


def _head_mean(v, gmat):
    return _split_dot(v, gmat) * (1.0 / HEAD_DIM)


def _residue_views(arrays, nb, seq):
    return [a if dil == 1 else a.reshape(nb, dil, seq // dil, a.shape[1]) for a, dil in zip(arrays, DILATIONS)]


def _mix_out_fwd(osb, ocs, lses, gsb, gdil, wout, x, mod, tm):
    t, d = x.shape
    ds = osb.shape[1]
    nt = t // tm
    nb = mod.shape[0]
    tpb = nt // nb
    seq = t // nb
    n_cfg = len(DILATIONS)

    def body(osb_ref, *refs):
        oc_refs, lse_refs = refs[:n_cfg], refs[n_cfg:2 * n_cfg]
        gsb_ref, gdil_ref, w_ref, x_ref, mod_ref = refs[2 * n_cfg:2 * n_cfg + 5]
        xo_ref, on_ref, m_ref, odil_ref = refs[2 * n_cfg + 5:2 * n_cfg + 9]
        ld_refs = refs[2 * n_cfg + 9:3 * n_cfg + 9]
        stages, sc = refs[3 * n_cfg + 9:]
        ocv, lsev = [oc_refs[0][...]], [lse_refs[0][...]]
        for i, dil in enumerate(DILATIONS[1:]):
            ocv.append(_from_residue_rows(oc_refs[i + 1], stages.at[2 * i], dil))
            lsev.append(_from_residue_rows(lse_refs[i + 1], stages.at[2 * i + 1], dil))
        top = functools.reduce(jnp.maximum, lsev)
        total = top + jnp.log(sum(jnp.exp(l - top) for l in lsev))
        odil = sum(jnp.exp(l - total) * o for o, l in zip(ocv, lsev))
        odil_ref[...] = odil
        ld_refs[0][...] = total
        _stage(total, sc)
        for ref, dil in zip(ld_refs[1:], DILATIONS[1:]):
            _to_residue_rows(sc, ref, dil)
        gm = _head_blocks(ds)
        parts = []
        for o, g_ref in ((osb_ref[...], gsb_ref), (odil, gdil_ref)):
            parts.append(o * lax.rsqrt(_head_mean(o * o, gm) + EPS) * g_ref[...])
        on = jnp.concatenate(parts, axis=1).astype(BF16)
        on_ref[...] = on
        m = _dot(on, w_ref[...])
        m_ref[...] = m
        xo_ref[...] = x_ref[...] + mod_ref[5:6, :] * m

    tok = pl.BlockSpec((tm, d), lambda i: (i, 0))
    hd = pl.BlockSpec((tm, ds), lambda i: (i, 0))
    res = [hd] + [_residue_spec(tm, tpb, ds, dil, lambda i: 0) for dil in DILATIONS[1:]]
    res_shape = [jax.ShapeDtypeStruct((t, ds), F32)] + [_residue_shape(nb, seq, ds, dil, F32) for dil in DILATIONS[1:]]
    gain = pl.BlockSpec((1, ds), lambda i: (0, 0))
    outs = pl.pallas_call(
        body, name="mix_out_fwd", grid=(nt,),
        in_specs=[hd] + res + res + [gain, gain,
                  pl.BlockSpec(wout.shape, lambda i: (0, 0)),
                  tok, pl.BlockSpec((None, N_MOD, d), lambda i: (i // tpb, 0, 0))],
        out_specs=[tok, pl.BlockSpec((tm, 2 * ds), lambda i: (i, 0)), tok, hd] + res,
        out_shape=[jax.ShapeDtypeStruct((t, d), F32), jax.ShapeDtypeStruct((t, 2 * ds), BF16),
                   jax.ShapeDtypeStruct((t, d), F32), jax.ShapeDtypeStruct((t, ds), F32)] + res_shape,
        scratch_shapes=[pltpu.VMEM((2 * (n_cfg - 1), ds // LANES, tm, LANES), F32), _stage_shape(tm, ds)],
        compiler_params=_params("arbitrary"),
    )(osb, *_residue_views(ocs, nb, seq), *_residue_views(lses, nb, seq), gsb, gdil, wout, x, mod)
    return outs[0], outs[1], outs[2], outs[3], [a.reshape(t, ds) for a in outs[4:]]


def _mix_out_bwd(dxo, m, mod, wout, osb, odil, gsb, gdil, tm):
    t, d = dxo.shape
    ds = osb.shape[1]
    nt = t // tm
    nb = mod.shape[0]
    tpb = nt // nb
    seq = t // nb
    n_cfg = len(DILATIONS)

    def body(dxo_ref, m_ref, mod_ref, w_ref, osb_ref, odil_ref, gsb_ref, gdil_ref,
             dm_ref, dosb_ref, *rest):
        do_refs, dl_refs = rest[:n_cfg], rest[n_cfg:2 * n_cfg]
        dmod_ref, dg_ref, sc = rest[2 * n_cfg:]
        dodil_ref, dldil_ref = do_refs[0], dl_refs[0]
        i = pl.program_id(0)
        dxo_v = dxo_ref[...]
        dm = (mod_ref[5:6, :] * dxo_v).astype(BF16)
        dm_ref[...] = dm
        dgt = jnp.sum(m_ref[...] * dxo_v, axis=0, keepdims=True)
        don = _dot_nt(dm, w_ref[...])
        gm = _head_blocks(ds)

        @pl.when(i % tpb == 0)
        def _():
            dmod_ref[...] = jnp.zeros_like(dmod_ref)

        @pl.when(i == 0)
        def _():
            dg_ref[...] = jnp.zeros_like(dg_ref)

        dmod_ref[2:3, :] += dgt
        groups = ((osb_ref, gsb_ref, dosb_ref), (odil_ref, gdil_ref, dodil_ref))
        for k, (o_ref, g_ref, do_ref) in enumerate(groups):
            o = o_ref[...]
            dn_out = don[:, k * ds:(k + 1) * ds]
            r = lax.rsqrt(_head_mean(o * o, gm) + EPS)
            n = o * r
            dg_ref[0:1, k * ds:(k + 1) * ds] += jnp.sum(dn_out * n, axis=0, keepdims=True)
            dn = dn_out * g_ref[...]
            do = r * (dn - n * _head_mean(dn * n, gm))
            do_ref[...] = do
            if k == 1:
                delta = _head_mean(do * o, gm) * float(HEAD_DIM)
                dldil_ref[...] = delta
                for value, refs in ((do, do_refs), (delta, dl_refs)):
                    _stage(value, sc)
                    for ref, dil in zip(refs[1:], DILATIONS[1:]):
                        _to_residue_rows(sc, ref, dil)

    tok = pl.BlockSpec((tm, d), lambda i: (i, 0))
    hd = pl.BlockSpec((tm, ds), lambda i: (i, 0))
    res = [hd] + [_residue_spec(tm, tpb, ds, dil, lambda i: 0) for dil in DILATIONS[1:]]
    res_shape = [jax.ShapeDtypeStruct((t, ds), F32)] + [_residue_shape(nb, seq, ds, dil, F32) for dil in DILATIONS[1:]]
    gain = pl.BlockSpec((1, ds), lambda i: (0, 0))
    outs = pl.pallas_call(
        body, name="mix_out_bwd", grid=(nt,),
        in_specs=[tok, tok, pl.BlockSpec((None, N_MOD, d), lambda i: (i // tpb, 0, 0)),
                  pl.BlockSpec(wout.shape, lambda i: (0, 0)), hd, hd, gain, gain],
        out_specs=[tok, hd] + res + res
        + [pl.BlockSpec((None, 8, d), lambda i: (i // tpb, 0, 0)), pl.BlockSpec((8, 2 * ds), lambda i: (0, 0))],
        out_shape=[jax.ShapeDtypeStruct((t, d), BF16), jax.ShapeDtypeStruct((t, ds), F32)] + res_shape + res_shape
        + [jax.ShapeDtypeStruct((nb, 8, d), F32), jax.ShapeDtypeStruct((8, 2 * ds), F32)],
        scratch_shapes=[_stage_shape(tm, ds)],
        compiler_params=_params("arbitrary"),
    )(dxo, m, mod, wout, osb, odil, gsb, gdil)
    flat = [a.reshape(t, ds) for a in outs[2:2 + 2 * n_cfg]]
    return outs[0], outs[1], flat[:n_cfg], flat[n_cfg:], outs[-2], outs[-1]


def _merge_dqkv(sb_parts, dil_parts, nb, tm):
    t, ds = sb_parts[0].shape
    nt = t // tm
    tpb = nt // nb
    seq = t // nb
    n_cfg = len(DILATIONS)

    def body(*refs):
        sb_refs, dil_refs = refs[:3], refs[3:3 + 3 * n_cfg]
        o_ref, sc = refs[3 + 3 * n_cfg:]
        for k in range(3):
            o_ref[:, k * ds:(k + 1) * ds] = sb_refs[k][...]
            total = dil_refs[k * n_cfg][...].astype(F32)
            for i, dil in enumerate(DILATIONS[1:]):
                total = total + _from_residue_rows(dil_refs[k * n_cfg + i + 1], sc, dil)
            o_ref[:, (3 + k) * ds:(4 + k) * ds] = total.astype(BF16)

    hd = pl.BlockSpec((tm, ds), lambda i: (i, 0))
    res = [hd] + [_residue_spec(tm, tpb, ds, dil, lambda i: 0) for dil in DILATIONS[1:]]
    views = [v for parts in dil_parts for v in _residue_views(parts, nb, seq)]
    return pl.pallas_call(
        body, name="merge_dqkv", grid=(nt,),
        in_specs=[hd] * 3 + res * 3,
        out_specs=pl.BlockSpec((tm, 6 * ds), lambda i: (i, 0)),
        out_shape=jax.ShapeDtypeStruct((t, 6 * ds), BF16),
        scratch_shapes=[_stage_shape(tm, ds)],
        compiler_params=_params("arbitrary"),
    )(*sb_parts, *views)


def _row_tile(rows):
    if rows <= 256:
        return rows
    for cand in range(256, 15, -16):
        if rows % cand == 0:
            return cand
    return rows


def _adamw(w, parts, m, v, name, transposed=False):
    rows, cols = w.shape
    n_parts = parts.shape[0]
    tr = _row_tile(rows)
    c1 = 1.0 / (1.0 - ADAM_B1 ** ADAM_STEP)
    c2 = 1.0 / (1.0 - ADAM_B2 ** ADAM_STEP)

    def body(w_ref, p_ref, m_ref, v_ref, g_ref, d_ref, nm_ref, nv_ref):
        g = p_ref[0].astype(F32)
        for i in range(1, n_parts):
            g = g + p_ref[i].astype(F32)
        wv, mv, vv = w_ref[...], m_ref[...], v_ref[...]
        if transposed:
            wv, mv, vv = wv.T, mv.T, vv.T
        nm = ADAM_B1 * mv + (1.0 - ADAM_B1) * g
        nv = ADAM_B2 * vv + (1.0 - ADAM_B2) * (g * g)
        g_ref[...] = g
        nm_ref[...] = nm
        nv_ref[...] = nv
        d_ref[...] = -ADAM_LR * ((nm * c1) / (jnp.sqrt(nv * c2) + ADAM_EPS) + ADAM_WD * wv)

    blk = pl.BlockSpec((tr, cols), lambda i: (i, 0))
    if transposed:
        oblk = pl.BlockSpec((cols, tr), lambda i: (0, i))
        pblk = pl.BlockSpec((n_parts, cols, tr), lambda i: (0, 0, i))
        out = jax.ShapeDtypeStruct((cols, rows), F32)
    else:
        oblk, pblk = blk, pl.BlockSpec((n_parts, tr, cols), lambda i: (0, i, 0))
        out = jax.ShapeDtypeStruct((rows, cols), F32)
    return pl.pallas_call(
        body, name=name, grid=(rows // tr,),
        in_specs=[blk, pblk, blk, blk],
        out_specs=[oblk, oblk, oblk, oblk], out_shape=[out, out, out, out],
        compiler_params=_params("arbitrary"),
    )(w, parts, m, v)


def _t5_bucket(n):
    max_exact = N_BUCKETS // 2
    nf = np.maximum(n, 1).astype(np.float32)
    large = max_exact + (np.log(nf / max_exact) / math.log(MAX_DISTANCE / max_exact)
                         * (N_BUCKETS - max_exact)).astype(np.int32)
    large = np.minimum(large, N_BUCKETS - 1)
    return np.where(n < max_exact, n, large).astype(np.int32)


def _bucket_onehot():
    table = np.zeros((len(DILATIONS), 2 * DIL_BLOCK + 1, N_BUCKETS), np.float32)
    for i, dil in enumerate(DILATIONS):
        buckets = _t5_bucket(np.arange(DIL_BLOCK + 1) * dil)
        for m in range(DIL_BLOCK + 1):
            table[i, m, buckets[DIL_BLOCK - m]] = 1.0
    return table


def _bias_blocks(rel_bias):
    row = jnp.einsum("cmn,nh->chm", _bucket_onehot(), rel_bias, precision=lax.Precision.HIGHEST)
    n_cfg, n_heads, width = row.shape
    tiled = jnp.tile(row, (1, 1, DIL_BLOCK))[..., :DIL_BLOCK * (width - 1)]
    return tiled.reshape(n_cfg, n_heads, DIL_BLOCK, width - 1)


def _bias_blocks_bwd(dblocks):
    n_cfg, n_heads = dblocks.shape[:2]
    width = 2 * DIL_BLOCK + 1
    flat = dblocks.reshape(n_cfg, n_heads, DIL_BLOCK * (width - 1))
    flat = jnp.pad(flat, ((0, 0), (0, 0), (0, DIL_BLOCK)))
    drow = jnp.sum(flat.reshape(n_cfg, n_heads, DIL_BLOCK, width), axis=2)
    return jnp.einsum("chm,cmn->nh", drow, _bucket_onehot(), precision=lax.Precision.HIGHEST)


def _pad_to(a, axis, size):
    pad = [(0, 0)] * a.ndim
    pad[axis] = (0, size - a.shape[axis])
    return jnp.pad(a, pad)


def _lane_pad(n):
    return -(-n // LANES) * LANES


def _local_step(x, target, mod, gains, weights, rel_bias, tm, distributed):
    nb, seq, d = x.shape
    t = nb * seq
    g_ffn1, g_mix, g_sb, g_dil, g_ffn2, g_final = gains
    wg1, wu1, wd1 = weights[:3]
    x0 = x.reshape(t, d)
    ds = g_sb.shape[1]
    bias = _bias_blocks(rel_bias)

    def beside(arrays, scatter):
        return _Exchange(arrays, scatter) if distributed else None

    tp, tg = min(PROJ_TILE, seq), min(GRAD_TILE, t)

    (x1, f1, gate1, up1), got = _ffn_fwd(x0, mod, g_ffn1, wg1, wu1, wd1, 0, tp, beside(weights[3:5], False))
    win, wout = got if distributed else weights[3:5]
    wout2 = wout.reshape(-1, d)
    (qkv, qkvd, h2), got = _qkv_fwd(x1, mod, g_mix, win, tp, beside(weights[7:8], False))
    wd2 = got[0] if distributed else weights[7]
    (osb, csb), got = _sb_fwd(qkv, nb, seq, beside(weights[5:7], False))
    wg2, wu2 = got if distributed else weights[5:7]
    ocs, lses = [], []
    for i, dil in enumerate(DILATIONS):
        (oc, lse), _ = _dil_fwd(qkvd[i], bias[i], nb, seq, dil)
        ocs.append(oc)
        lses.append(lse)
    x2, on, mix, odil, ldil = _mix_out_fwd(osb, ocs, lses, g_sb, g_dil, wout2, x1, mod, tm)
    (dx3, f3, gate3, up3, head), _ = _ffn_fwd(x2, mod, g_ffn2, wg2, wu2, wd2, 2, tp,
                                              head=(target.reshape(t, d), g_final))
    loss_sum = 0.5 * jnp.sum(head[0]) / d
    dg_final = head[1:2]

    (dx2, dgate3, dup3, act3, h3, df3, dmod3, dg_ffn2), _ = _ffn_bwd(
        dx3, x2, f3, mod, g_ffn2, gate3, up3, wg2, wu2, wd2, 2, tm)
    gwg2, gwu2, gwd2 = _ffn_weight_grads(h3, dgate3, dup3, act3, df3, tg, 2)

    dm, dosb, dodil, dldil, dmod2b, dg_heads = _mix_out_bwd(
        dx2, mix, mod, wout2, osb, odil, g_sb, g_dil, tm)
    n_out = wout.shape[0]
    gwout = _mm_tn(on, dm,
                   pl.BlockSpec((tg, wout.shape[1]), lambda i, j: (i, j)),
                   pl.BlockSpec((tg, d), lambda i, j: (i, 0)),
                   wout.shape, t // tg, "grad_wout")

    (dq_sb, dk_sb, dv_sb), parts_late = _sb_bwd(qkv, dosb, csb, nb, seq,
                                                beside([gwout, gwg2, gwu2, gwd2], True))
    dil_grads = [_dil_bwd(qkvd[i], bias[i], dodil[i], ldil[i], dldil[i], nb, seq, dil)
                 for i, dil in enumerate(DILATIONS)]
    dqkv = _merge_dqkv([dq_sb, dk_sb, dv_sb], [[g[k] for g in dil_grads] for k in range(3)], nb, tm)
    drel = _bias_blocks_bwd(jnp.stack([g[3] for g in dil_grads]))

    cs = win.shape[2]
    gwin = _mm_tn(h2, dqkv,
                  pl.BlockSpec((tg, d), lambda i, j: (i, 0)),
                  pl.BlockSpec((tg, cs), lambda i, j: (i, j)),
                  win.shape, t // tg, "grad_win")
    (dx1, dmod2a, dg_mix), parts_mid = _qkv_bwd(dqkv, dx2, x1, mod, g_mix, win, tp, beside([gwin], True))

    (dx0, dgate1, dup1, act1, h1, df1, dmod1, dg_ffn1), _ = _ffn_bwd(
        dx1, x0, f1, mod, g_ffn1, gate1, up1, wg1, wu1, wd1, 0, tm)
    dmod = jnp.concatenate([dmod1[:, 0:3], dmod2a[:, 0:2], dmod2b[:, 2:3], dmod3[:, 0:3]], axis=1)
    ggrads = (dg_ffn1[0:1], dg_mix[0:1], dg_heads[0:1], drel, dg_ffn2[0:1], dg_final)
    if not distributed:
        gw1 = _ffn_weight_grads(h1, dgate1, dup1, act1, df1, tg, 0)
        return loss_sum, dx0.reshape(nb, seq, d), tuple(gw1) + (gwin, gwout, gwg2, gwu2, gwd2), dmod, ggrads

    dg_heads_row, drel_flat = dg_heads[0:1], drel.reshape(1, -1)
    width = max(d, dg_heads_row.shape[1], drel_flat.shape[1])
    small = jnp.concatenate(
        [_pad_to(a.reshape(1, -1), 1, width)
         for a in (dg_ffn1[0:1], dg_mix[0:1], dg_ffn2[0:1], dg_final, dg_heads_row, drel_flat, loss_sum)]
        + [jnp.zeros((1, width), F32)], axis=0)
    dmod_pad = _pad_to(dmod.reshape(nb, N_MOD * d), 0, 8)
    everyone = _Exchange([jnp.broadcast_to(dmod_pad, (N_DEV,) + dmod_pad.shape),
                          jnp.broadcast_to(small, (N_DEV,) + small.shape)], True)
    sent_g, sent_u, gwd1, (dmod_all, small_all) = _ffn_weight_grads(
        h1, dgate1, dup1, act1, df1, tg, 0, stream=True, first=everyone)
    wgrads = (sent_g, sent_u, gwd1) + tuple(parts_mid + parts_late)
    return dx0.reshape(nb, seq, d), wgrads, dmod_all, small_all


def kernel(x, c, w_ada, b_ada, g_ffn1, w1_gate, w1_up, w1_down, g_mix, w_in, g_sb_out, g_dil_out, w_out, rel_bias, g_ffn2, w2_gate, w2_up, w2_down, g_final, loss_target, m_w_ada, m_b_ada, m_g_ffn1, m_w1_gate, m_w1_up, m_w1_down, m_g_mix, m_w_in, m_g_sb_out, m_g_dil_out, m_w_out, m_rel_bias, m_g_ffn2, m_w2_gate, m_w2_up, m_w2_down, m_g_final, v_w_ada, v_b_ada, v_g_ffn1, v_w1_gate, v_w1_up, v_w1_down, v_g_mix, v_w_in, v_g_sb_out, v_g_dil_out, v_w_out, v_rel_bias, v_g_ffn2, v_w2_gate, v_w2_up, v_w2_down, v_g_final):
    nb, seq, d = x.shape
    me = 4 * lax.axis_index("x") + 2 * lax.axis_index("y") + lax.axis_index("c")
    tm = min(TOKEN_TILE, seq)
    fs = w1_gate.shape[2]
    fs_pad = _lane_pad(fs)
    ada_cols = w_ada.shape[2]

    def col_shard(w):
        return _pad_to(w[0].astype(BF16), 1, fs_pad)

    def row_shard(w):
        return _pad_to(w[0].astype(BF16), 0, fs_pad)

    shards = [col_shard(w1_gate), col_shard(w1_up), row_shard(w1_down), w_in[0].astype(BF16),
              w_out[0].astype(BF16), col_shard(w2_gate), col_shard(w2_up), row_shard(w2_down)]
    b_cols = lax.dynamic_slice(b_ada, (0, me * ada_cols), (1, ada_cols))
    c_every, mod_all, first = _first_exchange(_pad_to(c, 0, 8), shards[:3], w_ada[0], b_cols)
    c_all = c_every[:, :nb].reshape(N_DEV * nb, d)
    weights = first + shards[3:]
    mod = lax.dynamic_slice(mod_all, (0, me * 8, 0), (N_DEV, nb, ada_cols))
    mod = mod.transpose(1, 0, 2).reshape(nb, N_MOD, d)

    n_sb = g_sb_out.shape[1] * g_sb_out.shape[2]
    gains = (g_ffn1, g_mix, g_sb_out.reshape(1, n_sb), g_dil_out.reshape(1, -1), g_ffn2,
             g_final.reshape(1, d))
    grad_x, parts, dmod_all, small_all = _local_step(x, loss_target, mod, gains, weights, rel_bias, tm, True)

    last_part = _exchange([parts[2]], True, "scatter_last", chips=[True])[0]
    parts = parts[:2] + (last_part,) + parts[3:]
    dmod_all = dmod_all[:, :nb].reshape(N_DEV * nb, N_MOD * d)
    dmod_cols = lax.dynamic_slice(dmod_all, (0, me * ada_cols), (N_DEV * nb, ada_cols))
    gw_ada, gb_ada = _ada_bwd(c_all, dmod_cols, dmod_all)

    def small_part(row, size, shape):
        return small_all[:, row, :size].reshape((N_DEV,) + shape)

    loss = jnp.sum(small_all[:, 6, 0])

    n_rel = rel_bias.shape
    updates = {
        "w_ada": (w_ada[0], gw_ada[None], m_w_ada[0], v_w_ada[0]),
        "b_ada": (b_ada, gb_ada[None], m_b_ada, v_b_ada),
        "g_ffn1": (g_ffn1, small_part(0, d, (1, d)), m_g_ffn1, v_g_ffn1),
        "w1_gate": (w1_gate[0], parts[0], m_w1_gate[0], v_w1_gate[0]),
        "w1_up": (w1_up[0], parts[1], m_w1_up[0], v_w1_up[0]),
        "w1_down": (w1_down[0], parts[2], m_w1_down[0], v_w1_down[0]),
        "g_mix": (g_mix, small_part(1, d, (1, d)), m_g_mix, v_g_mix),
        "w_in": (w_in[0], parts[3], m_w_in[0], v_w_in[0]),
        "g_sb_out": (g_sb_out[0], small_all[:, 4, :n_sb].reshape((N_DEV,) + g_sb_out.shape[1:]),
                     m_g_sb_out[0], v_g_sb_out[0]),
        "g_dil_out": (g_dil_out[0], small_all[:, 4, n_sb:n_sb + g_dil_out[0].size].reshape((N_DEV,) + g_dil_out.shape[1:]),
                      m_g_dil_out[0], v_g_dil_out[0]),
        "w_out": (w_out[0], parts[4], m_w_out[0], v_w_out[0]),
        "rel_bias": (rel_bias, small_part(5, rel_bias.size, n_rel), m_rel_bias, v_rel_bias),
        "g_ffn2": (g_ffn2, small_part(2, d, (1, d)), m_g_ffn2, v_g_ffn2),
        "w2_gate": (w2_gate[0], parts[5], m_w2_gate[0], v_w2_gate[0]),
        "w2_up": (w2_up[0], parts[6], m_w2_up[0], v_w2_up[0]),
        "w2_down": (w2_down[0], parts[7], m_w2_down[0], v_w2_down[0]),
        "g_final": (g_final.reshape(1, d), small_part(3, d, (1, d)), m_g_final.reshape(1, d), v_g_final.reshape(1, d)),
    }
    shapes = {"w_ada": w_ada.shape, "b_ada": b_ada.shape, "g_ffn1": g_ffn1.shape, "w1_gate": w1_gate.shape,
              "w1_up": w1_up.shape, "w1_down": w1_down.shape, "g_mix": g_mix.shape, "w_in": w_in.shape,
              "g_sb_out": g_sb_out.shape, "g_dil_out": g_dil_out.shape, "w_out": w_out.shape,
              "rel_bias": rel_bias.shape, "g_ffn2": g_ffn2.shape, "w2_gate": w2_gate.shape,
              "w2_up": w2_up.shape, "w2_down": w2_down.shape, "g_final": g_final.shape}
    grads, deltas, new_m, new_v = [], [], [], []
    for name, (w, p, m, v) in updates.items():
        transposed = name in ("w1_gate", "w1_up", "w2_gate", "w2_up")
        outs = _adamw(w, p, m, v, f"adamw_{name}", transposed)
        for dst, a in zip((grads, deltas, new_m, new_v), outs):
            dst.append((a.T if transposed else a).reshape(shapes[name]))
    return (loss, grad_x, *grads, *deltas, *new_m, *new_v)
```

```python
import functools
import math

import numpy as np
import jax
import jax.numpy as jnp
from jax import lax
from jax.experimental import pallas as pl
from jax.experimental.pallas import tpu as pltpu

F32 = jnp.float32
BF16 = jnp.bfloat16

EPS = 1e-6
NEG_INF = -1e30
HEAD_DIM = 64
LANES = 128
DIL_BLOCK = 128
DILATIONS = (1, 4, 16)
N_BUCKETS = 32
MAX_DISTANCE = 2048
N_MOD = 9
N_DEV = 8
SB_BLOCK = 256
SB_HEADS = 4
SB_WIDTH = SB_HEADS * HEAD_DIM
DIL_HEADS = 4
DIL_WIDTH = DIL_HEADS * HEAD_DIM
TOKEN_TILE = 512
PROJ_TILE = 1024
GRAD_TILE = 1024
SHARD_GROUP = 2
FFN_CHUNKS = 2
VMEM_LIMIT_BYTES = 56 * 1024 * 1024

ADAM_LR = 0.001
ADAM_B1 = 0.9
ADAM_B2 = 0.999
ADAM_EPS = 1e-08
ADAM_WD = 0.01
ADAM_STEP = 10

NT_DIMS = (((1,), (1,)), ((), ()))
TN_DIMS = (((0,), (0,)), ((), ()))


def _params(*sem):
    return pltpu.CompilerParams(dimension_semantics=sem, vmem_limit_bytes=VMEM_LIMIT_BYTES)


def _once(spec):
    return pl.BlockSpec(spec.block_shape, spec.index_map, pipeline_mode=pl.Buffered(1))


def _dot(a, b):
    return jnp.dot(a, b, preferred_element_type=F32)


def _dot_nt(a, b):
    return lax.dot_general(a, b, NT_DIMS, preferred_element_type=F32)


def _dot_tn(a, b):
    return lax.dot_general(a, b, TN_DIMS, preferred_element_type=F32)


def _split_dot(a, b):
    hi = a.astype(BF16)
    lo = (a - hi.astype(F32)).astype(BF16)
    return _dot(hi, b) + _dot(lo, b)


def _sigmoid(z):
    return 1.0 / (1.0 + jnp.exp(-z))


def _norm(x):
    r = lax.rsqrt(jnp.mean(x * x, axis=-1, keepdims=True) + EPS)
    return x * r, r


def _modulate(x, g, mod_ref, k):
    n, _ = _norm(x)
    shift = mod_ref[3 * k:3 * k + 1, :]
    scale = mod_ref[3 * k + 1:3 * k + 2, :]
    return n * g * (1.0 + scale) + shift


def _modulate_bwd(dh, x, g, mod_ref, k):
    n, r = _norm(x)
    scale = mod_ref[3 * k + 1:3 * k + 2, :]
    dshift = jnp.sum(dh, axis=0, keepdims=True)
    dscale = jnp.sum(dh * n * g, axis=0, keepdims=True)
    dg = jnp.sum(dh * n * (1.0 + scale), axis=0, keepdims=True)
    dn = dh * g * (1.0 + scale)
    dx = r * (dn - n * jnp.mean(dn * n, axis=-1, keepdims=True))
    return dx, dshift, dscale, dg


class _Exchange:
    def __init__(self, arrays, scatter, relay=False, chips=None):
        assert not (scatter and relay)
        self.arrays = list(arrays)
        self.scatter = scatter
        self.relay = relay
        self.n = len(self.arrays)
        self.chips = list(chips) if chips is not None else [False] * self.n
        assert scatter or not any(self.chips)
        self.out_shape = [
            jax.ShapeDtypeStruct((N_DEV // 2 if ch else N_DEV,) + tuple(a.shape[1:] if scatter else a.shape), a.dtype)
            for a, ch in zip(self.arrays, self.chips)]
        n_remote = self.n * (N_DEV - 1)
        self.scratch_shapes = [pltpu.SemaphoreType.DMA((n_remote,)), pltpu.SemaphoreType.DMA((n_remote,)),
                               pltpu.SemaphoreType.DMA((self.n,))]

    def _copies(self, in_refs, out_refs, sems):
        send_sems, recv_sems, local_sems = sems
        x, y, c = lax.axis_index("x"), lax.axis_index("y"), lax.axis_index("c")
        me = 4 * x + 2 * y + c
        local, remote, relayed = [], {}, {}
        for a in range(self.n):
            if self.chips[a]:
                mine = 2 * x + y
                local.append(pltpu.make_async_copy(in_refs[a].at[mine], out_refs[a].at[mine], local_sems.at[a]))
                for k in (2, 4, 6):
                    px = 1 - x if k & 4 else x
                    py = 1 - y if k & 2 else y
                    sem = a * (N_DEV - 1) + k - 1
                    remote[a, k] = pltpu.make_async_remote_copy(
                        src_ref=in_refs[a].at[2 * px + py], dst_ref=out_refs[a].at[mine],
                        send_sem=send_sems.at[sem], recv_sem=recv_sems.at[sem],
                        device_id=(px, py, c), device_id_type=pl.DeviceIdType.MESH)
                continue
            src = in_refs[a].at[me] if self.scatter else in_refs[a]
            local.append(pltpu.make_async_copy(src, out_refs[a].at[me], local_sems.at[a]))
            for k in range(1, N_DEV):
                px = 1 - x if k & 4 else x
                py = 1 - y if k & 2 else y
                pc = 1 - c if k & 1 else c
                sem = a * (N_DEV - 1) + k - 1
                if self.relay and k & 1 and k > 1:
                    slot = 4 * px + 2 * py + c
                    relayed[a, k] = pltpu.make_async_remote_copy(
                        src_ref=out_refs[a].at[slot], dst_ref=out_refs[a].at[slot],
                        send_sem=send_sems.at[sem], recv_sem=recv_sems.at[sem],
                        device_id=(x, y, 1 - c), device_id_type=pl.DeviceIdType.MESH)
                    continue
                src = in_refs[a].at[4 * px + 2 * py + pc] if self.scatter else in_refs[a]
                remote[a, k] = pltpu.make_async_remote_copy(
                    src_ref=src, dst_ref=out_refs[a].at[me],
                    send_sem=send_sems.at[sem], recv_sem=recv_sems.at[sem],
                    device_id=(px, py, pc), device_id_type=pl.DeviceIdType.MESH)
        return local, remote, relayed

    def start(self, in_refs, out_refs, sems):
        local, remote, _ = self._copies(in_refs, out_refs, sems)
        for cp in local + list(remote.values()):
            cp.start()

    def wait(self, in_refs, out_refs, sems):
        local, remote, relayed = self._copies(in_refs, out_refs, sems)
        for (a, k), cp in relayed.items():
            remote[a, k - 1].wait_recv()
            cp.start()
        for (a, k), cp in remote.items():
            if (a, k + 1) not in relayed:
                cp.wait_recv()
        for cp in relayed.values():
            cp.wait_recv()
        for cp in list(remote.values()) + list(relayed.values()):
            cp.wait_send()
        for cp in local:
            cp.wait()


def _call(body, *, name, args, in_specs, out_specs, out_shape, scratch_shapes=(), grid=(),
          params=None, exchange=None):
    n_in, n_out = len(args), len(out_shape)
    if exchange is None:
        outs = pl.pallas_call(
            body, name=name, grid=grid, in_specs=list(in_specs), out_specs=list(out_specs),
            out_shape=list(out_shape), scratch_shapes=list(scratch_shapes), compiler_params=params,
        )(*args)
        return list(outs), []
    n_ex = exchange.n

    def wrapped(*refs):
        ins, refs = refs[:n_in], refs[n_in:]
        ex_in, refs = refs[:n_ex], refs[n_ex:]
        outs, refs = refs[:n_out], refs[n_out:]
        ex_out, refs = refs[:n_ex], refs[n_ex:]
        scratch, sems = refs[:len(refs) - 3], refs[len(refs) - 3:]
        if not grid:
            exchange.start(ex_in, ex_out, sems)
            body(*ins, *outs, *scratch)
            exchange.wait(ex_in, ex_out, sems)
            return
        first = functools.reduce(jnp.logical_and, [pl.program_id(a) == 0 for a in range(len(grid))])
        last = functools.reduce(jnp.logical_and, [pl.program_id(a) == grid[a] - 1 for a in range(len(grid))])

        @pl.when(first)
        def _():
            exchange.start(ex_in, ex_out, sems)

        body(*ins, *outs, *scratch)

        @pl.when(last)
        def _():
            exchange.wait(ex_in, ex_out, sems)

    any_spec = pl.BlockSpec(memory_space=pl.ANY)
    outs = pl.pallas_call(
        wrapped, name=name, grid=grid,
        in_specs=list(in_specs) + [any_spec] * n_ex, out_specs=list(out_specs) + [any_spec] * n_ex,
        out_shape=list(out_shape) + exchange.out_shape,
        scratch_shapes=list(scratch_shapes) + exchange.scratch_shapes, compiler_params=params,
    )(*args, *exchange.arrays)
    return list(outs[:n_out]), list(outs[n_out:])


def _exchange(arrays, scatter, name, relay=False, chips=None):
    return _call(lambda: None, name=name, args=(), in_specs=(), out_specs=(), out_shape=(),
                 exchange=_Exchange(arrays, scatter, relay, chips))[1]


def _first_exchange(c_pad, shards, w, b):
    rows, d = c_pad.shape
    cols = w.shape[1]
    ex_c = _Exchange([c_pad], False)
    ex_w = _Exchange(shards, False, relay=True)
    ex_m = _Exchange([jax.ShapeDtypeStruct((N_DEV * rows, cols), F32)], False)
    n_w = ex_w.n

    def body(*refs):
        c_ref, w_refs, wa_ref, b_ref = refs[0], refs[1:1 + n_w], refs[1 + n_w], refs[2 + n_w]
        outs = refs[3 + n_w:]
        cg_ref, wg_refs, mg_ref = outs[0], outs[1:1 + n_w], outs[1 + n_w]
        scratch = outs[2 + n_w:]
        sems_c, sems_w, sems_m, c_vm, m_vm = scratch[0:3], scratch[3:6], scratch[6:9], scratch[9], scratch[10]
        ex_c.start([c_ref], [cg_ref], sems_c)
        ex_c.wait([c_ref], [cg_ref], sems_c)
        pltpu.sync_copy(cg_ref, c_vm)
        cv = c_vm[...].reshape(N_DEV * rows, d)
        s = (cv * _sigmoid(cv)).astype(BF16)
        m_vm[...] = _dot(s, wa_ref[...].astype(BF16)) + b_ref[...]
        ex_m.start([m_vm], [mg_ref], sems_m)
        ex_w.start(w_refs, wg_refs, sems_w)
        ex_m.wait([m_vm], [mg_ref], sems_m)
        ex_w.wait(w_refs, wg_refs, sems_w)

    any_spec = pl.BlockSpec(memory_space=pl.ANY)
    vmem_spec = pl.BlockSpec(memory_space=pltpu.VMEM)
    outs = pl.pallas_call(
        body, name="first_exchange",
        in_specs=[any_spec] * (1 + n_w) + [vmem_spec, vmem_spec],
        out_specs=[any_spec] * (2 + n_w),
        out_shape=ex_c.out_shape + ex_w.out_shape + ex_m.out_shape,
        scratch_shapes=ex_c.scratch_shapes + ex_w.scratch_shapes + ex_m.scratch_shapes
        + [pltpu.VMEM((N_DEV, rows, d), F32), pltpu.VMEM((N_DEV * rows, cols), F32)],
        compiler_params=pltpu.CompilerParams(vmem_limit_bytes=VMEM_LIMIT_BYTES),
    )(c_pad, *shards, w, b)
    return outs[0], outs[1 + n_w], list(outs[1:1 + n_w])


def _ada_bwd(c_all, dmod_cols, dmod_all):
    def body(c_ref, dc_ref, da_ref, gw_ref, gb_ref):
        cv = c_ref[...]
        s = cv * _sigmoid(cv)
        gw_ref[...] = lax.dot_general(s, dc_ref[...], TN_DIMS, preferred_element_type=F32,
                                      precision=lax.Precision.HIGHEST)
        gb_ref[...] = jnp.sum(da_ref[...], axis=0, keepdims=True)

    return pl.pallas_call(
        body, name="ada_bwd",
        out_shape=(jax.ShapeDtypeStruct((c_all.shape[1], dmod_cols.shape[1]), F32),
                   jax.ShapeDtypeStruct((1, dmod_all.shape[1]), F32)),
        compiler_params=pltpu.CompilerParams(vmem_limit_bytes=VMEM_LIMIT_BYTES),
    )(c_all, dmod_cols, dmod_all)


def _side_by_side(w_ref):
    return jnp.concatenate([w_ref[s] for s in range(w_ref.shape[0])], axis=1)


def _stacked(w_ref):
    return jnp.concatenate([w_ref[s] for s in range(w_ref.shape[0])], axis=0)


def _loss_tile(x, target, g, acc_ref):
    d = x.shape[1]
    n, r = _norm(x)
    err = n * g - target
    dy = err * (1.0 / d)
    acc_ref[0:1, :] += jnp.sum(err * err, axis=0, keepdims=True)
    acc_ref[1:2, :] += jnp.sum(dy * n, axis=0, keepdims=True)
    dn = dy * g
    return r * (dn - n * jnp.mean(dn * n, axis=-1, keepdims=True))


def _ffn_fwd(x, mod, g, wg, wu, wd, k, tm, exchange=None, head=None):
    t, d = x.shape
    ns, _, fs = wg.shape
    nt = t // tm
    tpb = nt // mod.shape[0]
    rows = tm // FFN_CHUNKS
    extra = list(head) if head is not None else []

    def body(x_ref, mod_ref, g_ref, wg_ref, wu_ref, wd_ref, *rest):
        if head is not None:
            t_ref, gf_ref, xo_ref, f_ref, gg_ref, uu_ref, head_ref, h_sc, acc = rest
        else:
            xo_ref, f_ref, gg_ref, uu_ref, h_sc, acc = rest
        i, j = pl.program_id(0), pl.program_id(1)

        @pl.when(j == 0)
        def _():
            h_sc[...] = _modulate(x_ref[...], g_ref[...], mod_ref, k).astype(BF16)
            acc[...] = jnp.zeros_like(acc)

        chunks = [pl.ds(c * rows, rows) for c in range(FFN_CHUNKS)]
        wg, wu, wd = _side_by_side(wg_ref), _side_by_side(wu_ref), _stacked(wd_ref)
        gates, ups = [], []
        for rs in chunks:
            h = h_sc[rs, :]
            gates.append(_dot(h, wg))
            ups.append(_dot(h, wu))
        acts = [(g * _sigmoid(g) * u).astype(BF16) for g, u in zip(gates, ups)]
        for rs, g, u in zip(chunks, gates, ups):
            for s in range(SHARD_GROUP):
                gg_ref[s, rs, :] = g[:, s * fs:(s + 1) * fs].astype(BF16)
                uu_ref[s, rs, :] = u[:, s * fs:(s + 1) * fs].astype(BF16)
        downs = [_dot(a, wd) for a in acts]
        for rs, dn in zip(chunks, downs):
            acc[rs, :] += dn

        @pl.when(j == ns // SHARD_GROUP - 1)
        def _():
            f = acc[...]
            f_ref[...] = f.astype(BF16)
            xo = x_ref[...] + 0.5 * mod_ref[3 * k + 2:3 * k + 3, :] * f
            if head is None:
                xo_ref[...] = xo
            else:
                @pl.when(i == 0)
                def _():
                    head_ref[...] = jnp.zeros_like(head_ref)

                xo_ref[...] = _loss_tile(xo, t_ref[...], gf_ref[...], head_ref)

    tok = pl.BlockSpec((tm, d), lambda i, j: (i, 0))
    row = pl.BlockSpec((1, d), lambda i, j: (0, 0))
    hid = pl.BlockSpec((SHARD_GROUP, tm, fs), lambda i, j: (j, i, 0))
    head_specs = [_once(tok), row] if head is not None else []
    head_out = [pl.BlockSpec((8, d), lambda i, j: (0, 0))] if head is not None else []
    head_shape = [jax.ShapeDtypeStruct((8, d), F32)] if head is not None else []
    return _call(
        body, name=f"ffn_fwd{k}", grid=(nt, ns // SHARD_GROUP), args=(x, mod, g, wg, wu, wd, *extra),
        in_specs=[tok,
                  pl.BlockSpec((None, N_MOD, d), lambda i, j: (i // tpb, 0, 0)),
                  row,
                  pl.BlockSpec((SHARD_GROUP, d, fs), lambda i, j: (j, 0, 0)),
                  pl.BlockSpec((SHARD_GROUP, d, fs), lambda i, j: (j, 0, 0)),
                  pl.BlockSpec((SHARD_GROUP, fs, d), lambda i, j: (j, 0, 0))] + head_specs,
        out_specs=[tok, tok, hid, hid] + head_out,
        out_shape=[jax.ShapeDtypeStruct((t, d), F32), jax.ShapeDtypeStruct((t, d), BF16),
                   jax.ShapeDtypeStruct((ns, t, fs), BF16), jax.ShapeDtypeStruct((ns, t, fs), BF16)]
        + head_shape,
        scratch_shapes=[pltpu.VMEM((tm, d), BF16), pltpu.VMEM((tm, d), F32)],
        params=_params("arbitrary", "arbitrary"), exchange=exchange)


def _ffn_bwd(dxo, x, f, mod, g, gate, up, wg, wu, wd, k, tm, exchange=None):
    t, d = x.shape
    ns, _, fs = wg.shape
    nt = t // tm
    nb = mod.shape[0]
    tpb = nt // nb
    rows = tm // FFN_CHUNKS

    def body(dxo_ref, x_ref, f_ref, mod_ref, g_ref, gg_ref, uu_ref, wg_ref, wu_ref, wd_ref,
             dx_ref, dgg_ref, duu_ref, act_ref, h_ref, df_ref, dmod_ref, dg_ref, acc):
        i, j = pl.program_id(0), pl.program_id(1)

        @pl.when(j == 0)
        def _():
            df = 0.5 * mod_ref[3 * k + 2:3 * k + 3, :] * dxo_ref[...]
            df_ref[...] = df.astype(BF16)
            h_ref[...] = _modulate(x_ref[...], g_ref[...], mod_ref, k).astype(BF16)
            acc[...] = jnp.zeros_like(acc)

        chunks = [pl.ds(c * rows, rows) for c in range(FFN_CHUNKS)]
        group = range(SHARD_GROUP)
        wg, wu, wd = _side_by_side(wg_ref), _side_by_side(wu_ref), _stacked(wd_ref)
        dacts = [_dot_nt(df_ref[rs, :], wd) for rs in chunks]
        dgates, dups = [], []
        for rs, dact in zip(chunks, dacts):
            gv = jnp.concatenate([gg_ref[s, rs, :] for s in group], axis=1).astype(F32)
            uv = jnp.concatenate([uu_ref[s, rs, :] for s in group], axis=1).astype(F32)
            sig = _sigmoid(gv)
            s_act = gv * sig
            act = (s_act * uv).astype(BF16)
            for s in group:
                act_ref[s, rs, :] = act[:, s * fs:(s + 1) * fs]
            dups.append((dact * s_act).astype(BF16))
            dgates.append((dact * uv * (sig * (1.0 + gv * (1.0 - sig)))).astype(BF16))
        dhs = [_dot_nt(dg, wg) + _dot_nt(du, wu) for dg, du in zip(dgates, dups)]
        for rs, dg, du, dh in zip(chunks, dgates, dups, dhs):
            for s in group:
                dgg_ref[s, rs, :] = dg[:, s * fs:(s + 1) * fs]
                duu_ref[s, rs, :] = du[:, s * fs:(s + 1) * fs]
            acc[rs, :] += dh

        @pl.when(j == ns // SHARD_GROUP - 1)
        def _():
            dx, dshift, dscale, dg = _modulate_bwd(acc[...], x_ref[...], g_ref[...], mod_ref, k)
            dxo_v = dxo_ref[...]
            dx_ref[...] = dxo_v + dx
            dgt = jnp.sum(0.5 * f_ref[...].astype(F32) * dxo_v, axis=0, keepdims=True)

            @pl.when(i % tpb == 0)
            def _():
                dmod_ref[...] = jnp.zeros_like(dmod_ref)

            @pl.when(i == 0)
            def _():
                dg_ref[...] = jnp.zeros_like(dg_ref)

            dmod_ref[0:1, :] += dshift
            dmod_ref[1:2, :] += dscale
            dmod_ref[2:3, :] += dgt
            dg_ref[0:1, :] += dg

    tok = pl.BlockSpec((tm, d), lambda i, j: (i, 0))
    hid = pl.BlockSpec((SHARD_GROUP, tm, fs), lambda i, j: (j, i, 0))
    return _call(
        body, name=f"ffn_bwd{k}", grid=(nt, ns // SHARD_GROUP), args=(dxo, x, f, mod, g, gate, up, wg, wu, wd),
        in_specs=[tok, tok, tok,
                  pl.BlockSpec((None, N_MOD, d), lambda i, j: (i // tpb, 0, 0)),
                  pl.BlockSpec((1, d), lambda i, j: (0, 0)),
                  hid, hid,
                  pl.BlockSpec((SHARD_GROUP, d, fs), lambda i, j: (j, 0, 0)),
                  pl.BlockSpec((SHARD_GROUP, d, fs), lambda i, j: (j, 0, 0)),
                  pl.BlockSpec((SHARD_GROUP, fs, d), lambda i, j: (j, 0, 0))],
        out_specs=[tok, hid, hid, hid, tok, tok,
                   pl.BlockSpec((None, 8, d), lambda i, j: (i // tpb, 0, 0)),
                   pl.BlockSpec((8, d), lambda i, j: (0, 0))],
        out_shape=[jax.ShapeDtypeStruct((t, d), F32),
                   jax.ShapeDtypeStruct((ns, t, fs), BF16), jax.ShapeDtypeStruct((ns, t, fs), BF16),
                   jax.ShapeDtypeStruct((ns, t, fs), BF16),
                   jax.ShapeDtypeStruct((t, d), BF16), jax.ShapeDtypeStruct((t, d), BF16),
                   jax.ShapeDtypeStruct((nb, 8, d), F32), jax.ShapeDtypeStruct((8, d), F32)],
        scratch_shapes=[pltpu.VMEM((tm, d), F32)],
        params=_params("arbitrary", "arbitrary"), exchange=exchange)


def _mm_tn(a, b, a_spec, b_spec, out_shape, n_tiles, name, exchange=None, keep_transposed=False,
           pair_reduce=False):
    n_out = out_shape[0]
    block = tuple(out_shape[1:])
    last = n_tiles - 1
    flip = block[0] > block[1]
    if flip:
        block = block[::-1]
    if flip and keep_transposed:
        flip_back, out_shape = False, (n_out,) + block
    else:
        flip_back = flip
    full_shape = tuple(out_shape)
    n_pairs = n_out // 2
    if pair_reduce:
        out_shape = (n_pairs,) + full_shape[1:]

    def body(a_ref, b_ref, o_ref, acc, *pair):
        i, j = pl.program_id(0), pl.program_id(1)
        prod = _dot_tn(b_ref[...], a_ref[...]) if flip else _dot_tn(a_ref[...], b_ref[...])
        full_ref = pair[0] if pair_reduce else o_ref

        @pl.when(i == 0)
        def _():
            acc[j] = prod

        @pl.when(i > 0)
        def _():
            acc[j] += prod

        @pl.when(i == last)
        def _():
            total = acc[j]
            full_ref[j] = (total.T if flip_back else total).astype(BF16)

        if pair_reduce:
            _, landed, send_sems, recv_sems = pair

            @pl.when(jnp.logical_and(i == last, j == n_out - 1))
            def _():
                x, y, c = lax.axis_index("x"), lax.axis_index("y"), lax.axis_index("c")
                copies = [pltpu.make_async_remote_copy(
                    src_ref=full_ref.at[2 * q + 1 - c], dst_ref=landed.at[q],
                    send_sem=send_sems.at[q], recv_sem=recv_sems.at[q],
                    device_id=(x, y, 1 - c), device_id_type=pl.DeviceIdType.MESH) for q in range(n_pairs)]
                for cp in copies:
                    cp.start()
                for q, cp in enumerate(copies):
                    cp.wait_recv()
                    o_ref[q] = (full_ref[2 * q + c].astype(F32) + landed[q].astype(F32)).astype(BF16)
                for cp in copies:
                    cp.wait_send()

    scratch = [pltpu.VMEM((n_out,) + block, F32)]
    if pair_reduce:
        scratch += [pltpu.VMEM(full_shape, BF16), pltpu.VMEM(out_shape, BF16),
                    pltpu.SemaphoreType.DMA((n_pairs,)), pltpu.SemaphoreType.DMA((n_pairs,))]
    outs, sent = _call(
        body, name=name, grid=(n_tiles, n_out), args=(a, b), in_specs=[a_spec, b_spec],
        out_specs=[pl.BlockSpec(out_shape, lambda i, j: (0,) * len(out_shape))],
        out_shape=[jax.ShapeDtypeStruct(out_shape, BF16)],
        scratch_shapes=scratch,
        params=_params("arbitrary", "arbitrary"), exchange=exchange)
    return (outs[0], sent) if exchange is not None else outs[0]


def _ffn_weight_grads(h, dgate, dup, act, df, tm, tag, stream=False, first=None):
    t, d = h.shape
    ns, _, fs = dgate.shape
    nt = t // tm
    tok = pl.BlockSpec((tm, d), lambda i, j: (i, 0))
    hid = pl.BlockSpec((None, tm, fs), lambda i, j: (j, i, 0))
    if not stream:
        gwg = _mm_tn(h, dgate, tok, hid, (ns, d, fs), nt, f"grad_wg{tag}", keep_transposed=True)
        gwu = _mm_tn(h, dup, tok, hid, (ns, d, fs), nt, f"grad_wu{tag}", keep_transposed=True)
        gwd = _mm_tn(act, df, hid, tok, (ns, fs, d), nt, f"grad_wd{tag}")
        return gwg, gwu, gwd
    gwg, brought = _mm_tn(h, dgate, tok, hid, (ns, d, fs), nt, f"grad_wg{tag}", first,
                          keep_transposed=True, pair_reduce=True)
    gwu, sent_g = _mm_tn(h, dup, tok, hid, (ns, d, fs), nt, f"grad_wu{tag}",
                         _Exchange([gwg], True, chips=[True]), keep_transposed=True, pair_reduce=True)
    gwd, sent_u = _mm_tn(act, df, hid, tok, (ns, fs, d), nt, f"grad_wd{tag}",
                         _Exchange([gwu], True, chips=[True]), pair_reduce=True)
    return sent_g[0], sent_u[0], gwd, brought


def _stage_shape(rows, cols):
    return pltpu.VMEM((cols // LANES, rows, LANES), F32)


def _stage(value, stage_ref):
    for k in range(stage_ref.shape[0]):
        stage_ref[k] = value[:, k * LANES:(k + 1) * LANES]


def _to_residue_rows(stage_ref, dst_ref, dil):
    rows = stage_ref.shape[1] // dil
    for r in range(dil):
        for k in range(stage_ref.shape[0]):
            dst_ref[r, :, k * LANES:(k + 1) * LANES] = (
                stage_ref.at[k][pl.ds(r, rows, stride=dil), :].astype(dst_ref.dtype))


def _from_residue_rows(src_ref, stage_ref, dil):
    rows = stage_ref.shape[1] // dil
    chunks = range(stage_ref.shape[0])
    for r in range(dil):
        for k in chunks:
            stage_ref.at[k][pl.ds(r, rows, stride=dil), :] = src_ref[r, :, k * LANES:(k + 1) * LANES].astype(F32)
    return jnp.concatenate([stage_ref[k] for k in chunks], axis=1)


def _residue_shape(nb, seq, width, dil, dtype):
    return jax.ShapeDtypeStruct((nb, dil, seq // dil, width), dtype)


def _residue_spec(tm, tpb, cols, dil, col_block):
    return pl.BlockSpec((None, dil, tm // dil, cols),
                        lambda i, *rest: (i // tpb, 0, i % tpb, col_block(i, *rest)))


def _qkv_fwd(x, mod, g, win, tm, exchange=None):
    t, d = x.shape
    ns, _, cs = win.shape
    nt = t // tm
    nb = mod.shape[0]
    tpb = nt // nb
    seq = t // nb
    width = ns * cs // 2
    cs, ns = cs * SHARD_GROUP, ns // SHARD_GROUP
    half = ns // 2
    n_res = len(DILATIONS) - 1

    def body(x_ref, mod_ref, g_ref, w_ref, sb_ref, dil_ref, *rest):
        res_refs, h_ref, sc = rest[:n_res], rest[n_res], rest[n_res + 1]
        j = pl.program_id(1)

        @pl.when(j == 0)
        def _():
            h_ref[...] = _modulate(x_ref[...], g_ref[...], mod_ref, 1).astype(BF16)

        res = _dot(h_ref[...], _side_by_side(w_ref))

        @pl.when(j < half)
        def _():
            sb_ref[...] = res.astype(BF16)

        @pl.when(j >= half)
        def _():
            dil_ref[...] = res.astype(BF16)
            _stage(res, sc)
            for ref, dil in zip(res_refs, DILATIONS[1:]):
                _to_residue_rows(sc, ref, dil)

    def dil_col(i, j):
        return jnp.maximum(j - half, 0)

    tok = pl.BlockSpec((tm, d), lambda i, j: (i, 0))
    wide = jax.ShapeDtypeStruct((t, width), BF16)
    outs, got = _call(
        body, name="qkv_fwd", grid=(nt, ns), args=(x, mod, g, win),
        in_specs=[tok,
                  pl.BlockSpec((None, N_MOD, d), lambda i, j: (i // tpb, 0, 0)),
                  pl.BlockSpec((1, d), lambda i, j: (0, 0)),
                  pl.BlockSpec((SHARD_GROUP, d, cs // SHARD_GROUP), lambda i, j: (j, 0, 0))],
        out_specs=[pl.BlockSpec((tm, cs), lambda i, j: (i, jnp.minimum(j, half - 1))),
                   pl.BlockSpec((tm, cs), lambda i, j: (i, dil_col(i, j)))]
        + [_residue_spec(tm, tpb, cs, dil, dil_col) for dil in DILATIONS[1:]] + [tok],
        out_shape=[wide, wide] + [_residue_shape(nb, seq, width, dil, BF16) for dil in DILATIONS[1:]]
        + [jax.ShapeDtypeStruct((t, d), BF16)],
        scratch_shapes=[_stage_shape(tm, cs)],
        params=_params("arbitrary", "arbitrary"), exchange=exchange)
    qkv_dil = [outs[1]] + [a.reshape(t, width) for a in outs[2:2 + n_res]]
    return (outs[0], qkv_dil, outs[-1]), got


def _qkv_bwd(dqkv, dxo, x, mod, g, win, tm, exchange=None):
    t, d = x.shape
    ns, _, cs = win.shape
    nt = t // tm
    nb = mod.shape[0]
    tpb = nt // nb
    cs, ns = cs * SHARD_GROUP, ns // SHARD_GROUP

    def body(dq_ref, dxo_ref, x_ref, mod_ref, g_ref, w_ref, dx_ref, dmod_ref, dg_ref, acc):
        i, j = pl.program_id(0), pl.program_id(1)

        @pl.when(j == 0)
        def _():
            acc[...] = jnp.zeros_like(acc)

        acc[...] += _dot_nt(dq_ref[...], _side_by_side(w_ref))

        @pl.when(j == ns - 1)
        def _():
            dx, dshift, dscale, dg = _modulate_bwd(acc[...], x_ref[...], g_ref[...], mod_ref, 1)
            dx_ref[...] = dxo_ref[...] + dx

            @pl.when(i % tpb == 0)
            def _():
                dmod_ref[...] = jnp.zeros_like(dmod_ref)

            @pl.when(i == 0)
            def _():
                dg_ref[...] = jnp.zeros_like(dg_ref)

            dmod_ref[0:1, :] += dshift
            dmod_ref[1:2, :] += dscale
            dg_ref[0:1, :] += dg

    tok = pl.BlockSpec((tm, d), lambda i, j: (i, 0))
    return _call(
        body, name="qkv_bwd", grid=(nt, ns), args=(dqkv, dxo, x, mod, g, win),
        in_specs=[pl.BlockSpec((tm, cs), lambda i, j: (i, j)), tok, tok,
                  pl.BlockSpec((None, N_MOD, d), lambda i, j: (i // tpb, 0, 0)),
                  pl.BlockSpec((1, d), lambda i, j: (0, 0)),
                  pl.BlockSpec((SHARD_GROUP, d, cs // SHARD_GROUP), lambda i, j: (j, 0, 0))],
        out_specs=[tok,
                   pl.BlockSpec((None, 8, d), lambda i, j: (i // tpb, 0, 0)),
                   pl.BlockSpec((8, d), lambda i, j: (0, 0))],
        out_shape=[jax.ShapeDtypeStruct((t, d), F32),
                   jax.ShapeDtypeStruct((nb, 8, d), F32), jax.ShapeDtypeStruct((8, d), F32)],
        scratch_shapes=[pltpu.VMEM((tm, d), F32)],
        params=_params("arbitrary", "arbitrary"), exchange=exchange)


def _heads(a):
    return [a[:, h * HEAD_DIM:(h + 1) * HEAD_DIM] for h in range(a.shape[1] // HEAD_DIM)]


def _own_lanes():
    lane = lax.broadcasted_iota(jnp.int32, (1, LANES), 1)
    return [lane < HEAD_DIM, lane >= HEAD_DIM]


def _pair_tiles(a):
    return [a[:, (h // 2) * LANES:(h // 2 + 1) * LANES] for h in range(a.shape[1] // HEAD_DIM)]


def _own_tiles(a, own):
    return [jnp.where(own[h % 2], tile, jnp.zeros_like(tile)) for h, tile in enumerate(_pair_tiles(a))]


def _merge_tiles(per_head, own):
    return jnp.concatenate([jnp.where(own[0], per_head[h], per_head[h + 1])
                            for h in range(0, len(per_head), 2)], axis=1)


def _scaled(q):
    return (q.astype(F32) * (HEAD_DIM ** -0.5)).astype(BF16)


def _sb_logits(qh, kh, tri, causal):
    zs = [_dot_nt(q, k) for q, k in zip(qh, kh)]
    es = [jnp.exp(-jnp.abs(z)) for z in zs]
    log_nots = [-(jnp.maximum(z, 0.0) + jnp.log(1.0 + e)) for z, e in zip(zs, es)]
    if causal is not None:
        log_nots = [jnp.where(causal, ln, 0.0) for ln in log_nots]
    return zs, es, [_split_dot(ln, tri) for ln in log_nots]


def _sb_masks():
    rows = lax.broadcasted_iota(jnp.int32, (SB_BLOCK, SB_BLOCK), 0)
    cols = lax.broadcasted_iota(jnp.int32, (SB_BLOCK, SB_BLOCK), 1)
    return (rows >= cols).astype(BF16), (rows <= cols).astype(BF16), cols < rows


def _sb_fwd(qkv, nb, seq, exchange=None):
    t = qkv.shape[0]
    n_pairs = (qkv.shape[1] // 3) // SB_WIDTH
    tb = SB_BLOCK
    n_blk = seq // tb

    def body(q_ref, k_ref, v_ref, o_ref, c_ref):
        tri, _, causal = _sb_masks()
        own = _own_lanes()

        def key_block(qh, kj, carry, mask):
            ks = pl.multiple_of(kj * tb, tb)
            kh, vh = _pair_tiles(k_ref[pl.ds(ks, tb), :]), _pair_tiles(v_ref[pl.ds(ks, tb), :])
            zs, _, suffixes = _sb_logits(qh, kh, tri, mask)
            ws = [jnp.exp(z + suffix + cr[1]) for z, suffix, cr in zip(zs, suffixes, carry)]
            if mask is not None:
                ws = [jnp.where(mask, w, 0.0) for w in ws]
            pv = [_dot(w.astype(BF16), v) for w, v in zip(ws, vh)]
            return tuple((cr[0] + p, cr[1] + suffix[:, 0:1]) for cr, p, suffix in zip(carry, pv, suffixes))

        def query_block(qi, _):
            qs = pl.multiple_of(qi * tb, tb)
            qh = _own_tiles(_scaled(q_ref[pl.ds(qs, tb), :]), own)
            zero = (jnp.zeros((tb, LANES), F32), jnp.zeros((tb, 1), F32))
            carry = key_block(qh, qi, (zero,) * SB_HEADS, causal)
            carry = lax.fori_loop(0, qi, lambda it, cr: key_block(qh, qi - 1 - it, cr, None), carry)
            o_ref[pl.ds(qs, tb), :] = _merge_tiles([cr[0] for cr in carry], own)
            c_ref[pl.ds(qs, tb), :] = _merge_tiles([jnp.broadcast_to(cr[1], (tb, LANES)) for cr in carry], own)
            return 0

        lax.fori_loop(0, n_blk, query_block, 0)

    def spec(offset):
        return pl.BlockSpec((seq, SB_WIDTH), lambda b, p: (b, offset + p))

    out = jax.ShapeDtypeStruct((t, n_pairs * SB_WIDTH), F32)
    return _call(
        body, name="sb_fwd", grid=(nb, n_pairs), args=(qkv, qkv, qkv),
        in_specs=[spec(0), spec(n_pairs), spec(2 * n_pairs)],
        out_specs=[spec(0), spec(0)], out_shape=[out, out],
        params=_params("arbitrary", "arbitrary"), exchange=exchange)


def _sb_bwd(qkv, do, csum, nb, seq, exchange=None):
    t = qkv.shape[0]
    n_pairs = (qkv.shape[1] // 3) // SB_WIDTH
    tb = SB_BLOCK
    n_blk = seq // tb
    scale = HEAD_DIM ** -0.5

    def body(q_ref, k_ref, v_ref, do_ref, c_ref, dq_ref, dk_ref, dv_ref, dkt_acc, dvt_acc):
        tri, tri_prefix, causal = _sb_masks()
        own = _own_lanes()
        dkt_acc[...] = jnp.zeros_like(dkt_acc)
        dvt_acc[...] = jnp.zeros_like(dvt_acc)

        def key_blocks(qh, qth, doh, doth, ch, kjs, carry, mask):
            nh = SB_HEADS
            chains = range(nh * len(kjs))
            kss = [pl.multiple_of(kj * tb, tb) for kj in kjs]
            kh = [tile for ks in kss for tile in _pair_tiles(k_ref[pl.ds(ks, tb), :])]
            vh = [tile for ks in kss for tile in _pair_tiles(v_ref[pl.ds(ks, tb), :])]
            zs, es, suffixes = _sb_logits(qh * len(kjs), kh, tri, mask)
            dws = [_dot_nt(doh[c % nh], vh[c]) for c in chains]
            lefts = []
            for c in chains:
                before = carry[c][1] if c < nh else lefts[c - nh]
                lefts.append(before + suffixes[c][:, 0:1])
            ws = [jnp.exp(zs[c] + suffixes[c] + (ch[c % nh] - lefts[c])) for c in chains]
            if mask is not None:
                ws = [jnp.where(mask, w, 0.0) for w in ws]
            dlws = [ws[c] * dws[c] for c in chains]
            dprefixes = [_split_dot(dlw, tri_prefix) for dlw in dlws]
            dvts = [_dot(doth[c % nh], ws[c].astype(BF16)) for c in chains]
            dlefts, dzbs = [], []
            for c in chains:
                dlefts.append(carry[c][2] if c < nh else dlefts[c - nh] + dprefixes[c - nh][:, tb - 1:tb])
                sig = jnp.where(zs[c] >= 0.0, 1.0, es[c]) * pl.reciprocal(1.0 + es[c], approx=True)
                dz = dlws[c] - sig * (dlefts[c] + dprefixes[c])
                if mask is not None:
                    dz = jnp.where(mask, dz, 0.0)
                dzbs.append(dz.astype(BF16))
            dkts = [_dot(qth[c % nh], dzbs[c]) for c in chains]
            dqs = [_dot(dzbs[c], kh[c]) for c in chains]
            for b, ks in enumerate(kss):
                pairs = range(b * nh, (b + 1) * nh, 2)
                dkt_acc[:, pl.ds(ks, tb)] += jnp.concatenate([dkts[c] + dkts[c + 1] for c in pairs], axis=0)
                dvt_acc[:, pl.ds(ks, tb)] += jnp.concatenate([dvts[c] + dvts[c + 1] for c in pairs], axis=0)
            last = (len(kjs) - 1) * nh
            return tuple((carry[h][0] + sum(dqs[h::nh]), lefts[last + h],
                          dlefts[last + h] + dprefixes[last + h][:, tb - 1:tb]) for h in range(nh))

        def query_block(qi, _):
            qs = pl.multiple_of(qi * tb, tb)
            qh = _own_tiles(_scaled(q_ref[pl.ds(qs, tb), :]), own)
            doh = _own_tiles(do_ref[pl.ds(qs, tb), :], own)
            qth = [a.astype(F32).T.astype(BF16) for a in qh]
            doth = [a.T.astype(BF16) for a in doh]
            doh = [a.astype(BF16) for a in doh]
            cv = c_ref[pl.ds(qs, tb), :]
            ch = [cv[:, h * HEAD_DIM:h * HEAD_DIM + 1] for h in range(SB_HEADS)]
            zero = (jnp.zeros((tb, LANES), F32), jnp.zeros((tb, 1), F32), jnp.zeros((tb, 1), F32))

            def key_block(kjs, cr, mask):
                return key_blocks(qh, qth, doh, doth, ch, kjs, cr, mask)

            carry = lax.fori_loop(0, qi // 2, lambda p, cr: key_block([2 * p, 2 * p + 1], cr, None),
                                  (zero,) * SB_HEADS)
            carry = lax.fori_loop(0, qi % 2, lambda _, cr: key_block([qi - 1], cr, None), carry)
            carry = key_block([qi], carry, causal)
            dq = _merge_tiles([cr[0] for cr in carry], own) * scale
            dq_ref[pl.ds(qs, tb), :] = dq.astype(BF16)
            return 0

        lax.fori_loop(0, n_blk, query_block, 0)
        dk_ref[...] = dkt_acc[...].T.astype(BF16)
        dv_ref[...] = dvt_acc[...].T.astype(BF16)

    def spec(offset):
        return pl.BlockSpec((seq, SB_WIDTH), lambda b, p: (b, offset + p))

    out = jax.ShapeDtypeStruct((t, n_pairs * SB_WIDTH), BF16)
    return _call(
        body, name="sb_bwd", grid=(nb, n_pairs), args=(qkv, qkv, qkv, do, csum),
        in_specs=[spec(0), spec(n_pairs), spec(2 * n_pairs), spec(0), spec(0)],
        out_specs=[spec(0), spec(0), spec(0)],
        out_shape=[out, out, out],
        scratch_shapes=[pltpu.VMEM((SB_WIDTH, seq), F32), pltpu.VMEM((SB_WIDTH, seq), F32)],
        params=_params("arbitrary", "arbitrary"), exchange=exchange)


def _dil_block_scores(qh, kph, kch, bias_ref, has_prev, band_prev, band_cur):
    scale = HEAD_DIM ** -0.5
    heads = range(len(qh))
    no_prev = jnp.where(has_prev, 0.0, NEG_INF)
    zps = [_dot_nt(qh[h], kph[h]) for h in heads]
    zcs = [_dot_nt(qh[h], kch[h]) for h in heads]
    zps = [jnp.where(band_prev, zps[h] * scale + bias_ref[h, :, 0:DIL_BLOCK], NEG_INF) + no_prev for h in heads]
    zcs = [jnp.where(band_cur, zcs[h] * scale + bias_ref[h, :, DIL_BLOCK:2 * DIL_BLOCK], NEG_INF) for h in heads]
    return zps, zcs


def _dil_bands():
    rows = lax.broadcasted_iota(jnp.int32, (DIL_BLOCK, DIL_BLOCK), 0)
    cols = lax.broadcasted_iota(jnp.int32, (DIL_BLOCK, DIL_BLOCK), 1)
    return cols >= rows, cols <= rows


def _dil_fwd(qkv, bias, nb, seq, dil, exchange=None):
    t, width = qkv.shape
    n_pairs = (width // 3) // DIL_WIDTH
    bq = DIL_BLOCK
    n_blk = seq // bq
    per_seq = n_blk // dil
    heads = range(DIL_HEADS)

    def body(q_ref, k_ref, v_ref, bias_ref, o_ref, lse_ref):
        band_prev, band_cur = _dil_bands()
        own = _own_lanes()

        def block(n, _):
            has_prev = (n & (per_seq - 1)) != 0
            qs = pl.multiple_of(n * bq, bq)
            ps = pl.multiple_of(jnp.maximum(n - 1, 0) * bq, bq)
            qh = _own_tiles(q_ref[pl.ds(qs, bq), :], own)
            kp, kc = _pair_tiles(k_ref[pl.ds(ps, bq), :]), _pair_tiles(k_ref[pl.ds(qs, bq), :])
            vp, vc = _pair_tiles(v_ref[pl.ds(ps, bq), :]), _pair_tiles(v_ref[pl.ds(qs, bq), :])
            zps, zcs = _dil_block_scores(qh, kp, kc, bias_ref, has_prev, band_prev, band_cur)
            ms = [jnp.maximum(jnp.max(zps[h], axis=1, keepdims=True), jnp.max(zcs[h], axis=1, keepdims=True))
                  for h in heads]
            eps = [jnp.exp(zps[h] - ms[h]) for h in heads]
            ecs = [jnp.exp(zcs[h] - ms[h]) for h in heads]
            pvs = [_dot(eps[h].astype(BF16), vp[h]) + _dot(ecs[h].astype(BF16), vc[h]) for h in heads]
            dens = [jnp.sum(eps[h], axis=1, keepdims=True) + jnp.sum(ecs[h], axis=1, keepdims=True) for h in heads]
            o_ref[pl.ds(qs, bq), :] = _merge_tiles([pvs[h] / dens[h] for h in heads], own)
            lse_ref[pl.ds(qs, bq), :] = _merge_tiles(
                [jnp.broadcast_to(ms[h] + jnp.log(dens[h]), (bq, LANES)) for h in heads], own)
            return 0

        lax.fori_loop(0, n_blk, block, 0)

    def spec(offset):
        return pl.BlockSpec((seq, DIL_WIDTH), lambda b, p: (b, offset + p))

    out = jax.ShapeDtypeStruct((t, n_pairs * DIL_WIDTH), F32)
    return _call(
        body, name=f"dil_fwd{dil}", grid=(nb, n_pairs), args=(qkv, qkv, qkv, bias),
        in_specs=[spec(0), spec(n_pairs), spec(2 * n_pairs),
                  pl.BlockSpec((DIL_HEADS, bq, 2 * bq), lambda b, p: (p, 0, 0))],
        out_specs=[spec(0), spec(0)], out_shape=[out, out],
        params=_params("arbitrary", "arbitrary"), exchange=exchange)


def _dil_bwd(qkv, bias, do, lse, delta, nb, seq, dil):
    t, width = qkv.shape
    n_pairs = (width // 3) // DIL_WIDTH
    bq = DIL_BLOCK
    n_blk = seq // bq
    per_seq = n_blk // dil
    scale = HEAD_DIM ** -0.5
    heads = range(DIL_HEADS)

    def body(q_ref, k_ref, v_ref, bias_ref, do_ref, lse_ref, dl_ref, dq_ref, dk_ref, dv_ref, db_ref,
             dk_acc, dv_acc):
        band_prev, band_cur = _dil_bands()
        own = _own_lanes()
        dk_acc[...] = jnp.zeros_like(dk_acc)
        dv_acc[...] = jnp.zeros_like(dv_acc)

        @pl.when(pl.program_id(1) == 0)
        def _():
            db_ref[...] = jnp.zeros_like(db_ref)

        def block(n, _):
            has_prev = (n & (per_seq - 1)) != 0
            qs = pl.multiple_of(n * bq, bq)
            ps = pl.multiple_of(jnp.maximum(n - 1, 0) * bq, bq)
            qh = _own_tiles(q_ref[pl.ds(qs, bq), :], own)
            kp, kc = _pair_tiles(k_ref[pl.ds(ps, bq), :]), _pair_tiles(k_ref[pl.ds(qs, bq), :])
            vp, vc = _pair_tiles(v_ref[pl.ds(ps, bq), :]), _pair_tiles(v_ref[pl.ds(qs, bq), :])
            doh = _own_tiles(do_ref[pl.ds(qs, bq), :].astype(BF16), own)
            lse_v, dl_v = lse_ref[pl.ds(qs, bq), :], dl_ref[pl.ds(qs, bq), :]
            zps, zcs = _dil_block_scores(qh, kp, kc, bias_ref, has_prev, band_prev, band_cur)
            dpp = [_dot_nt(doh[h], vp[h]) for h in heads]
            dpc = [_dot_nt(doh[h], vc[h]) for h in heads]
            lse_h = [lse_v[:, h * HEAD_DIM:h * HEAD_DIM + 1] for h in heads]
            dl_h = [dl_v[:, h * HEAD_DIM:h * HEAD_DIM + 1] for h in heads]
            pps = [jnp.exp(zps[h] - lse_h[h]) for h in heads]
            pcs = [jnp.exp(zcs[h] - lse_h[h]) for h in heads]
            dvp = [_dot_tn(pps[h].astype(BF16), doh[h]) for h in heads]
            dvc = [_dot_tn(pcs[h].astype(BF16), doh[h]) for h in heads]
            dzps = [pps[h] * (dpp[h] - dl_h[h]) for h in heads]
            dzcs = [pcs[h] * (dpc[h] - dl_h[h]) for h in heads]
            dzp_b = [(dzps[h] * scale).astype(BF16) for h in heads]
            dzc_b = [(dzcs[h] * scale).astype(BF16) for h in heads]
            dqs = [_dot(dzp_b[h], kp[h]) + _dot(dzc_b[h], kc[h]) for h in heads]
            dkp = [_dot_tn(dzp_b[h], qh[h]) for h in heads]
            dkc = [_dot_tn(dzc_b[h], qh[h]) for h in heads]
            for h in heads:
                db_ref[h, :, 0:bq] += dzps[h]
                db_ref[h, :, bq:2 * bq] += dzcs[h]
            def pair_sums(per_head):
                return jnp.concatenate([per_head[h] + per_head[h + 1] for h in heads[::2]], axis=1)

            dq_ref[pl.ds(qs, bq), :] = _merge_tiles(dqs, own).astype(BF16)
            dk_acc[pl.ds(ps, bq), :] += pair_sums(dkp)
            dk_acc[pl.ds(qs, bq), :] += pair_sums(dkc)
            dv_acc[pl.ds(ps, bq), :] += pair_sums(dvp)
            dv_acc[pl.ds(qs, bq), :] += pair_sums(dvc)
            return 0

        lax.fori_loop(0, n_blk, block, 0)
        dk_ref[...] = dk_acc[...].astype(BF16)
        dv_ref[...] = dv_acc[...].astype(BF16)

    def spec(offset):
        return pl.BlockSpec((seq, DIL_WIDTH), lambda p, b: (b, offset + p))

    bias_spec = pl.BlockSpec((DIL_HEADS, bq, 2 * bq), lambda p, b: (p, 0, 0))
    out = jax.ShapeDtypeStruct((t, n_pairs * DIL_WIDTH), BF16)
    return pl.pallas_call(
        body, name=f"dil_bwd{dil}", grid=(n_pairs, nb),
        in_specs=[spec(0), spec(n_pairs), spec(2 * n_pairs), bias_spec, spec(0), spec(0), spec(0)],
        out_specs=[spec(0), spec(0), spec(0), bias_spec],
        out_shape=[out, out, out, jax.ShapeDtypeStruct(bias.shape, F32)],
        scratch_shapes=[pltpu.VMEM((seq, DIL_WIDTH), F32), pltpu.VMEM((seq, DIL_WIDTH), F32)],
        compiler_params=_params("arbitrary", "arbitrary"),
    )(qkv, qkv, qkv, bias, do, lse, delta)


def _head_blocks(width):
    rows = lax.broadcasted_iota(jnp.int32, (width, width), 0) // HEAD_DIM
    cols = lax.broadcasted_iota(jnp.int32, (width, width), 1) // HEAD_DIM
    return (rows == cols).astype(BF16)


def _head_mean(v, gmat):
    return _split_dot(v, gmat) * (1.0 / HEAD_DIM)


def _residue_views(arrays, nb, seq):
    return [a if dil == 1 else a.reshape(nb, dil, seq // dil, a.shape[1]) for a, dil in zip(arrays, DILATIONS)]


def _mix_out_fwd(osb, ocs, lses, gsb, gdil, wout, x, mod, tm):
    t, d = x.shape
    ds = osb.shape[1]
    nt = t // tm
    nb = mod.shape[0]
    tpb = nt // nb
    seq = t // nb
    n_cfg = len(DILATIONS)

    def body(osb_ref, *refs):
        oc_refs, lse_refs = refs[:n_cfg], refs[n_cfg:2 * n_cfg]
        gsb_ref, gdil_ref, w_ref, x_ref, mod_ref = refs[2 * n_cfg:2 * n_cfg + 5]
        xo_ref, on_ref, m_ref, odil_ref = refs[2 * n_cfg + 5:2 * n_cfg + 9]
        ld_refs = refs[2 * n_cfg + 9:3 * n_cfg + 9]
        stages, sc = refs[3 * n_cfg + 9:]
        ocv, lsev = [oc_refs[0][...]], [lse_refs[0][...]]
        for i, dil in enumerate(DILATIONS[1:]):
            ocv.append(_from_residue_rows(oc_refs[i + 1], stages.at[2 * i], dil))
            lsev.append(_from_residue_rows(lse_refs[i + 1], stages.at[2 * i + 1], dil))
        top = functools.reduce(jnp.maximum, lsev)
        total = top + jnp.log(sum(jnp.exp(l - top) for l in lsev))
        odil = sum(jnp.exp(l - total) * o for o, l in zip(ocv, lsev))
        odil_ref[...] = odil
        ld_refs[0][...] = total
        _stage(total, sc)
        for ref, dil in zip(ld_refs[1:], DILATIONS[1:]):
            _to_residue_rows(sc, ref, dil)
        gm = _head_blocks(ds)
        parts = []
        for o, g_ref in ((osb_ref[...], gsb_ref), (odil, gdil_ref)):
            parts.append(o * lax.rsqrt(_head_mean(o * o, gm) + EPS) * g_ref[...])
        on = jnp.concatenate(parts, axis=1).astype(BF16)
        on_ref[...] = on
        m = _dot(on, w_ref[...])
        m_ref[...] = m
        xo_ref[...] = x_ref[...] + mod_ref[5:6, :] * m

    tok = pl.BlockSpec((tm, d), lambda i: (i, 0))
    hd = pl.BlockSpec((tm, ds), lambda i: (i, 0))
    res = [hd] + [_residue_spec(tm, tpb, ds, dil, lambda i: 0) for dil in DILATIONS[1:]]
    res_shape = [jax.ShapeDtypeStruct((t, ds), F32)] + [_residue_shape(nb, seq, ds, dil, F32) for dil in DILATIONS[1:]]
    gain = pl.BlockSpec((1, ds), lambda i: (0, 0))
    outs = pl.pallas_call(
        body, name="mix_out_fwd", grid=(nt,),
        in_specs=[hd] + res + res + [gain, gain,
                  pl.BlockSpec(wout.shape, lambda i: (0, 0)),
                  tok, pl.BlockSpec((None, N_MOD, d), lambda i: (i // tpb, 0, 0))],
        out_specs=[tok, pl.BlockSpec((tm, 2 * ds), lambda i: (i, 0)), tok, hd] + res,
        out_shape=[jax.ShapeDtypeStruct((t, d), F32), jax.ShapeDtypeStruct((t, 2 * ds), BF16),
                   jax.ShapeDtypeStruct((t, d), F32), jax.ShapeDtypeStruct((t, ds), F32)] + res_shape,
        scratch_shapes=[pltpu.VMEM((2 * (n_cfg - 1), ds // LANES, tm, LANES), F32), _stage_shape(tm, ds)],
        compiler_params=_params("arbitrary"),
    )(osb, *_residue_views(ocs, nb, seq), *_residue_views(lses, nb, seq), gsb, gdil, wout, x, mod)
    return outs[0], outs[1], outs[2], outs[3], [a.reshape(t, ds) for a in outs[4:]]


def _mix_out_bwd(dxo, m, mod, wout, osb, odil, gsb, gdil, tm):
    t, d = dxo.shape
    ds = osb.shape[1]
    nt = t // tm
    nb = mod.shape[0]
    tpb = nt // nb
    seq = t // nb
    n_cfg = len(DILATIONS)

    def body(dxo_ref, m_ref, mod_ref, w_ref, osb_ref, odil_ref, gsb_ref, gdil_ref,
             dm_ref, dosb_ref, *rest):
        do_refs, dl_refs = rest[:n_cfg], rest[n_cfg:2 * n_cfg]
        dmod_ref, dg_ref, sc = rest[2 * n_cfg:]
        dodil_ref, dldil_ref = do_refs[0], dl_refs[0]
        i = pl.program_id(0)
        dxo_v = dxo_ref[...]
        dm = (mod_ref[5:6, :] * dxo_v).astype(BF16)
        dm_ref[...] = dm
        dgt = jnp.sum(m_ref[...] * dxo_v, axis=0, keepdims=True)
        don = _dot_nt(dm, w_ref[...])
        gm = _head_blocks(ds)

        @pl.when(i % tpb == 0)
        def _():
            dmod_ref[...] = jnp.zeros_like(dmod_ref)

        @pl.when(i == 0)
        def _():
            dg_ref[...] = jnp.zeros_like(dg_ref)

        dmod_ref[2:3, :] += dgt
        groups = ((osb_ref, gsb_ref, dosb_ref), (odil_ref, gdil_ref, dodil_ref))
        for k, (o_ref, g_ref, do_ref) in enumerate(groups):
            o = o_ref[...]
            dn_out = don[:, k * ds:(k + 1) * ds]
            r = lax.rsqrt(_head_mean(o * o, gm) + EPS)
            n = o * r
            dg_ref[0:1, k * ds:(k + 1) * ds] += jnp.sum(dn_out * n, axis=0, keepdims=True)
            dn = dn_out * g_ref[...]
            do = r * (dn - n * _head_mean(dn * n, gm))
            do_ref[...] = do
            if k == 1:
                delta = _head_mean(do * o, gm) * float(HEAD_DIM)
                dldil_ref[...] = delta
                for value, refs in ((do, do_refs), (delta, dl_refs)):
                    _stage(value, sc)
                    for ref, dil in zip(refs[1:], DILATIONS[1:]):
                        _to_residue_rows(sc, ref, dil)

    tok = pl.BlockSpec((tm, d), lambda i: (i, 0))
    hd = pl.BlockSpec((tm, ds), lambda i: (i, 0))
    res = [hd] + [_residue_spec(tm, tpb, ds, dil, lambda i: 0) for dil in DILATIONS[1:]]
    res_shape = [jax.ShapeDtypeStruct((t, ds), F32)] + [_residue_shape(nb, seq, ds, dil, F32) for dil in DILATIONS[1:]]
    gain = pl.BlockSpec((1, ds), lambda i: (0, 0))
    outs = pl.pallas_call(
        body, name="mix_out_bwd", grid=(nt,),
        in_specs=[tok, tok, pl.BlockSpec((None, N_MOD, d), lambda i: (i // tpb, 0, 0)),
                  pl.BlockSpec(wout.shape, lambda i: (0, 0)), hd, hd, gain, gain],
        out_specs=[tok, hd] + res + res
        + [pl.BlockSpec((None, 8, d), lambda i: (i // tpb, 0, 0)), pl.BlockSpec((8, 2 * ds), lambda i: (0, 0))],
        out_shape=[jax.ShapeDtypeStruct((t, d), BF16), jax.ShapeDtypeStruct((t, ds), F32)] + res_shape + res_shape
        + [jax.ShapeDtypeStruct((nb, 8, d), F32), jax.ShapeDtypeStruct((8, 2 * ds), F32)],
        scratch_shapes=[_stage_shape(tm, ds)],
        compiler_params=_params("arbitrary"),
    )(dxo, m, mod, wout, osb, odil, gsb, gdil)
    flat = [a.reshape(t, ds) for a in outs[2:2 + 2 * n_cfg]]
    return outs[0], outs[1], flat[:n_cfg], flat[n_cfg:], outs[-2], outs[-1]


def _merge_dqkv(sb_parts, dil_parts, nb, tm):
    t, ds = sb_parts[0].shape
    nt = t // tm
    tpb = nt // nb
    seq = t // nb
    n_cfg = len(DILATIONS)

    def body(*refs):
        sb_refs, dil_refs = refs[:3], refs[3:3 + 3 * n_cfg]
        o_ref, sc = refs[3 + 3 * n_cfg:]
        for k in range(3):
            o_ref[:, k * ds:(k + 1) * ds] = sb_refs[k][...]
            total = dil_refs[k * n_cfg][...].astype(F32)
            for i, dil in enumerate(DILATIONS[1:]):
                total = total + _from_residue_rows(dil_refs[k * n_cfg + i + 1], sc, dil)
            o_ref[:, (3 + k) * ds:(4 + k) * ds] = total.astype(BF16)

    hd = pl.BlockSpec((tm, ds), lambda i: (i, 0))
    res = [hd] + [_residue_spec(tm, tpb, ds, dil, lambda i: 0) for dil in DILATIONS[1:]]
    views = [v for parts in dil_parts for v in _residue_views(parts, nb, seq)]
    return pl.pallas_call(
        body, name="merge_dqkv", grid=(nt,),
        in_specs=[hd] * 3 + res * 3,
        out_specs=pl.BlockSpec((tm, 6 * ds), lambda i: (i, 0)),
        out_shape=jax.ShapeDtypeStruct((t, 6 * ds), BF16),
        scratch_shapes=[_stage_shape(tm, ds)],
        compiler_params=_params("arbitrary"),
    )(*sb_parts, *views)


def _row_tile(rows):
    if rows <= 256:
        return rows
    for cand in range(256, 15, -16):
        if rows % cand == 0:
            return cand
    return rows


def _adamw(w, parts, m, v, name, transposed=False):
    rows, cols = w.shape
    n_parts = parts.shape[0]
    tr = _row_tile(rows)
    c1 = 1.0 / (1.0 - ADAM_B1 ** ADAM_STEP)
    c2 = 1.0 / (1.0 - ADAM_B2 ** ADAM_STEP)

    def body(w_ref, p_ref, m_ref, v_ref, g_ref, d_ref, nm_ref, nv_ref):
        g = p_ref[0].astype(F32)
        for i in range(1, n_parts):
            g = g + p_ref[i].astype(F32)
        wv, mv, vv = w_ref[...], m_ref[...], v_ref[...]
        if transposed:
            wv, mv, vv = wv.T, mv.T, vv.T
        nm = ADAM_B1 * mv + (1.0 - ADAM_B1) * g
        nv = ADAM_B2 * vv + (1.0 - ADAM_B2) * (g * g)
        g_ref[...] = g
        nm_ref[...] = nm
        nv_ref[...] = nv
        d_ref[...] = -ADAM_LR * ((nm * c1) / (jnp.sqrt(nv * c2) + ADAM_EPS) + ADAM_WD * wv)

    blk = pl.BlockSpec((tr, cols), lambda i: (i, 0))
    if transposed:
        oblk = pl.BlockSpec((cols, tr), lambda i: (0, i))
        pblk = pl.BlockSpec((n_parts, cols, tr), lambda i: (0, 0, i))
        out = jax.ShapeDtypeStruct((cols, rows), F32)
    else:
        oblk, pblk = blk, pl.BlockSpec((n_parts, tr, cols), lambda i: (0, i, 0))
        out = jax.ShapeDtypeStruct((rows, cols), F32)
    return pl.pallas_call(
        body, name=name, grid=(rows // tr,),
        in_specs=[blk, pblk, blk, blk],
        out_specs=[oblk, oblk, oblk, oblk], out_shape=[out, out, out, out],
        compiler_params=_params("arbitrary"),
    )(w, parts, m, v)


def _t5_bucket(n):
    max_exact = N_BUCKETS // 2
    nf = np.maximum(n, 1).astype(np.float32)
    large = max_exact + (np.log(nf / max_exact) / math.log(MAX_DISTANCE / max_exact)
                         * (N_BUCKETS - max_exact)).astype(np.int32)
    large = np.minimum(large, N_BUCKETS - 1)
    return np.where(n < max_exact, n, large).astype(np.int32)


def _bucket_onehot():
    table = np.zeros((len(DILATIONS), 2 * DIL_BLOCK + 1, N_BUCKETS), np.float32)
    for i, dil in enumerate(DILATIONS):
        buckets = _t5_bucket(np.arange(DIL_BLOCK + 1) * dil)
        for m in range(DIL_BLOCK + 1):
            table[i, m, buckets[DIL_BLOCK - m]] = 1.0
    return table


def _bias_blocks(rel_bias):
    row = jnp.einsum("cmn,nh->chm", _bucket_onehot(), rel_bias, precision=lax.Precision.HIGHEST)
    n_cfg, n_heads, width = row.shape
    tiled = jnp.tile(row, (1, 1, DIL_BLOCK))[..., :DIL_BLOCK * (width - 1)]
    return tiled.reshape(n_cfg, n_heads, DIL_BLOCK, width - 1)


def _bias_blocks_bwd(dblocks):
    n_cfg, n_heads = dblocks.shape[:2]
    width = 2 * DIL_BLOCK + 1
    flat = dblocks.reshape(n_cfg, n_heads, DIL_BLOCK * (width - 1))
    flat = jnp.pad(flat, ((0, 0), (0, 0), (0, DIL_BLOCK)))
    drow = jnp.sum(flat.reshape(n_cfg, n_heads, DIL_BLOCK, width), axis=2)
    return jnp.einsum("chm,cmn->nh", drow, _bucket_onehot(), precision=lax.Precision.HIGHEST)


def _pad_to(a, axis, size):
    pad = [(0, 0)] * a.ndim
    pad[axis] = (0, size - a.shape[axis])
    return jnp.pad(a, pad)


def _lane_pad(n):
    return -(-n // LANES) * LANES


def _local_step(x, target, mod, gains, weights, rel_bias, tm, distributed):
    nb, seq, d = x.shape
    t = nb * seq
    g_ffn1, g_mix, g_sb, g_dil, g_ffn2, g_final = gains
    wg1, wu1, wd1 = weights[:3]
    x0 = x.reshape(t, d)
    ds = g_sb.shape[1]
    bias = _bias_blocks(rel_bias)

    def beside(arrays, scatter):
        return _Exchange(arrays, scatter) if distributed else None

    tp, tg = min(PROJ_TILE, seq), min(GRAD_TILE, t)

    (x1, f1, gate1, up1), got = _ffn_fwd(x0, mod, g_ffn1, wg1, wu1, wd1, 0, tp, beside(weights[3:4], False))
    win = got[0] if distributed else weights[3]
    (qkv, qkvd, h2), got = _qkv_fwd(x1, mod, g_mix, win, tp, beside(weights[4:5], False))
    wout = got[0] if distributed else weights[4]
    wout2 = wout.reshape(-1, d)
    (osb, csb), got = _sb_fwd(qkv, nb, seq, beside(weights[5:8], False))
    wg2, wu2, wd2 = got if distributed else weights[5:8]
    ocs, lses = [], []
    for i, dil in enumerate(DILATIONS):
        (oc, lse), _ = _dil_fwd(qkvd[i], bias[i], nb, seq, dil)
        ocs.append(oc)
        lses.append(lse)
    x2, on, mix, odil, ldil = _mix_out_fwd(osb, ocs, lses, g_sb, g_dil, wout2, x1, mod, tm)
    (dx3, f3, gate3, up3, head), _ = _ffn_fwd(x2, mod, g_ffn2, wg2, wu2, wd2, 2, tp,
                                              head=(target.reshape(t, d), g_final))
    loss_sum = 0.5 * jnp.sum(head[0]) / d
    dg_final = head[1:2]

    (dx2, dgate3, dup3, act3, h3, df3, dmod3, dg_ffn2), _ = _ffn_bwd(
        dx3, x2, f3, mod, g_ffn2, gate3, up3, wg2, wu2, wd2, 2, tm)
    gwg2, gwu2, gwd2 = _ffn_weight_grads(h3, dgate3, dup3, act3, df3, tg, 2)

    dm, dosb, dodil, dldil, dmod2b, dg_heads = _mix_out_bwd(
        dx2, mix, mod, wout2, osb, odil, g_sb, g_dil, tm)
    n_out = wout.shape[0]
    gwout = _mm_tn(on, dm,
                   pl.BlockSpec((tg, wout.shape[1]), lambda i, j: (i, j)),
                   pl.BlockSpec((tg, d), lambda i, j: (i, 0)),
                   wout.shape, t // tg, "grad_wout")

    (dq_sb, dk_sb, dv_sb), parts_late = _sb_bwd(qkv, dosb, csb, nb, seq,
                                                beside([gwout, gwg2, gwu2, gwd2], True))
    dil_grads = [_dil_bwd(qkvd[i], bias[i], dodil[i], ldil[i], dldil[i], nb, seq, dil)
                 for i, dil in enumerate(DILATIONS)]
    dqkv = _merge_dqkv([dq_sb, dk_sb, dv_sb], [[g[k] for g in dil_grads] for k in range(3)], nb, tm)
    drel = _bias_blocks_bwd(jnp.stack([g[3] for g in dil_grads]))

    cs = win.shape[2]
    gwin = _mm_tn(h2, dqkv,
                  pl.BlockSpec((tg, d), lambda i, j: (i, 0)),
                  pl.BlockSpec((tg, cs), lambda i, j: (i, j)),
                  win.shape, t // tg, "grad_win", pair_reduce=distributed)
    (dx1, dmod2a, dg_mix), parts_mid = _qkv_bwd(
        dqkv, dx2, x1, mod, g_mix, win, tp, _Exchange([gwin], True, chips=[True]) if distributed else None)

    (dx0, dgate1, dup1, act1, h1, df1, dmod1, dg_ffn1), _ = _ffn_bwd(
        dx1, x0, f1, mod, g_ffn1, gate1, up1, wg1, wu1, wd1, 0, tm)
    dmod = jnp.concatenate([dmod1[:, 0:3], dmod2a[:, 0:2], dmod2b[:, 2:3], dmod3[:, 0:3]], axis=1)
    ggrads = (dg_ffn1[0:1], dg_mix[0:1], dg_heads[0:1], drel, dg_ffn2[0:1], dg_final)
    if not distributed:
        gw1 = _ffn_weight_grads(h1, dgate1, dup1, act1, df1, tg, 0)
        return loss_sum, dx0.reshape(nb, seq, d), tuple(gw1) + (gwin, gwout, gwg2, gwu2, gwd2), dmod, ggrads

    dg_heads_row, drel_flat = dg_heads[0:1], drel.reshape(1, -1)
    width = max(d, dg_heads_row.shape[1], drel_flat.shape[1])
    small = jnp.concatenate(
        [_pad_to(a.reshape(1, -1), 1, width)
         for a in (dg_ffn1[0:1], dg_mix[0:1], dg_ffn2[0:1], dg_final, dg_heads_row, drel_flat, loss_sum)]
        + [jnp.zeros((1, width), F32)], axis=0)
    dmod_pad = _pad_to(dmod.reshape(nb, N_MOD * d), 0, 8)
    everyone = _Exchange([jnp.broadcast_to(dmod_pad, (N_DEV,) + dmod_pad.shape),
                          jnp.broadcast_to(small, (N_DEV,) + small.shape)], True)
    sent_g, sent_u, gwd1, (dmod_all, small_all) = _ffn_weight_grads(
        h1, dgate1, dup1, act1, df1, tg, 0, stream=True, first=everyone)
    wgrads = (sent_g, sent_u, gwd1) + tuple(parts_mid + parts_late)
    return dx0.reshape(nb, seq, d), wgrads, dmod_all, small_all


def kernel(x, c, w_ada, b_ada, g_ffn1, w1_gate, w1_up, w1_down, g_mix, w_in, g_sb_out, g_dil_out, w_out, rel_bias, g_ffn2, w2_gate, w2_up, w2_down, g_final, loss_target, m_w_ada, m_b_ada, m_g_ffn1, m_w1_gate, m_w1_up, m_w1_down, m_g_mix, m_w_in, m_g_sb_out, m_g_dil_out, m_w_out, m_rel_bias, m_g_ffn2, m_w2_gate, m_w2_up, m_w2_down, m_g_final, v_w_ada, v_b_ada, v_g_ffn1, v_w1_gate, v_w1_up, v_w1_down, v_g_mix, v_w_in, v_g_sb_out, v_g_dil_out, v_w_out, v_rel_bias, v_g_ffn2, v_w2_gate, v_w2_up, v_w2_down, v_g_final):
    nb, seq, d = x.shape
    me = 4 * lax.axis_index("x") + 2 * lax.axis_index("y") + lax.axis_index("c")
    tm = min(TOKEN_TILE, seq)
    fs = w1_gate.shape[2]
    fs_pad = _lane_pad(fs)
    ada_cols = w_ada.shape[2]

    def col_shard(w):
        return _pad_to(w[0].astype(BF16), 1, fs_pad)

    def row_shard(w):
        return _pad_to(w[0].astype(BF16), 0, fs_pad)

    shards = [col_shard(w1_gate), col_shard(w1_up), row_shard(w1_down), w_in[0].astype(BF16),
              w_out[0].astype(BF16), col_shard(w2_gate), col_shard(w2_up), row_shard(w2_down)]
    b_cols = lax.dynamic_slice(b_ada, (0, me * ada_cols), (1, ada_cols))
    c_every, mod_all, first = _first_exchange(_pad_to(c, 0, 8), shards[:3], w_ada[0], b_cols)
    c_all = c_every[:, :nb].reshape(N_DEV * nb, d)
    weights = first + shards[3:]
    mod = lax.dynamic_slice(mod_all, (0, me * 8, 0), (N_DEV, nb, ada_cols))
    mod = mod.transpose(1, 0, 2).reshape(nb, N_MOD, d)

    n_sb = g_sb_out.shape[1] * g_sb_out.shape[2]
    gains = (g_ffn1, g_mix, g_sb_out.reshape(1, n_sb), g_dil_out.reshape(1, -1), g_ffn2,
             g_final.reshape(1, d))
    grad_x, parts, dmod_all, small_all = _local_step(x, loss_target, mod, gains, weights, rel_bias, tm, True)

    last_part = _exchange([parts[2]], True, "scatter_last", chips=[True])[0]
    parts = parts[:2] + (last_part,) + parts[3:]
    dmod_all = dmod_all[:, :nb].reshape(N_DEV * nb, N_MOD * d)
    dmod_cols = lax.dynamic_slice(dmod_all, (0, me * ada_cols), (N_DEV * nb, ada_cols))
    gw_ada, gb_ada = _ada_bwd(c_all, dmod_cols, dmod_all)

    def small_part(row, size, shape):
        return small_all[:, row, :size].reshape((N_DEV,) + shape)

    loss = jnp.sum(small_all[:, 6, 0])

    n_rel = rel_bias.shape
    updates = {
        "w_ada": (w_ada[0], gw_ada[None], m_w_ada[0], v_w_ada[0]),
        "b_ada": (b_ada, gb_ada[None], m_b_ada, v_b_ada),
        "g_ffn1": (g_ffn1, small_part(0, d, (1, d)), m_g_ffn1, v_g_ffn1),
        "w1_gate": (w1_gate[0], parts[0], m_w1_gate[0], v_w1_gate[0]),
        "w1_up": (w1_up[0], parts[1], m_w1_up[0], v_w1_up[0]),
        "w1_down": (w1_down[0], parts[2], m_w1_down[0], v_w1_down[0]),
        "g_mix": (g_mix, small_part(1, d, (1, d)), m_g_mix, v_g_mix),
        "w_in": (w_in[0], parts[3], m_w_in[0], v_w_in[0]),
        "g_sb_out": (g_sb_out[0], small_all[:, 4, :n_sb].reshape((N_DEV,) + g_sb_out.shape[1:]),
                     m_g_sb_out[0], v_g_sb_out[0]),
        "g_dil_out": (g_dil_out[0], small_all[:, 4, n_sb:n_sb + g_dil_out[0].size].reshape((N_DEV,) + g_dil_out.shape[1:]),
                      m_g_dil_out[0], v_g_dil_out[0]),
        "w_out": (w_out[0], parts[4], m_w_out[0], v_w_out[0]),
        "rel_bias": (rel_bias, small_part(5, rel_bias.size, n_rel), m_rel_bias, v_rel_bias),
        "g_ffn2": (g_ffn2, small_part(2, d, (1, d)), m_g_ffn2, v_g_ffn2),
        "w2_gate": (w2_gate[0], parts[5], m_w2_gate[0], v_w2_gate[0]),
        "w2_up": (w2_up[0], parts[6], m_w2_up[0], v_w2_up[0]),
        "w2_down": (w2_down[0], parts[7], m_w2_down[0], v_w2_down[0]),
        "g_final": (g_final.reshape(1, d), small_part(3, d, (1, d)), m_g_final.reshape(1, d), v_g_final.reshape(1, d)),
    }
    shapes = {"w_ada": w_ada.shape, "b_ada": b_ada.shape, "g_ffn1": g_ffn1.shape, "w1_gate": w1_gate.shape,
              "w1_up": w1_up.shape, "w1_down": w1_down.shape, "g_mix": g_mix.shape, "w_in": w_in.shape,
              "g_sb_out": g_sb_out.shape, "g_dil_out": g_dil_out.shape, "w_out": w_out.shape,
              "rel_bias": rel_bias.shape, "g_ffn2": g_ffn2.shape, "w2_gate": w2_gate.shape,
              "w2_up": w2_up.shape, "w2_down": w2_down.shape, "g_final": g_final.shape}
    grads, deltas, new_m, new_v = [], [], [], []
    for name, (w, p, m, v) in updates.items():
        transposed = name in ("w1_gate", "w1_up", "w2_gate", "w2_up")
        outs = _adamw(w, p, m, v, f"adamw_{name}", transposed)
        for dst, a in zip((grads, deltas, new_m, new_v), outs):
            dst.append((a.T if transposed else a).reshape(shapes[name]))
    return (loss, grad_x, *grads, *deltas, *new_m, *new_v)
```

```python
import functools
import math

import numpy as np
import jax
import jax.numpy as jnp
from jax import lax
from jax.experimental import pallas as pl
from jax.experimental.pallas import tpu as pltpu

F32 = jnp.float32
BF16 = jnp.bfloat16

EPS = 1e-6
NEG_INF = -1e30
HEAD_DIM = 64
LANES = 128
DIL_BLOCK = 128
DILATIONS = (1, 4, 16)
N_BUCKETS = 32
MAX_DISTANCE = 2048
N_MOD = 9
N_DEV = 8
SB_BLOCK = 256
SB_HEADS = 4
SB_WIDTH = SB_HEADS * HEAD_DIM
DIL_HEADS = 4
DIL_WIDTH = DIL_HEADS * HEAD_DIM
TOKEN_TILE = 512
PROJ_TILE = 1024
GRAD_TILE = 1024
SHARD_GROUP = 2
FFN_CHUNKS = 2
VMEM_LIMIT_BYTES = 56 * 1024 * 1024

ADAM_LR = 0.001
ADAM_B1 = 0.9
ADAM_B2 = 0.999
ADAM_EPS = 1e-08
ADAM_WD = 0.01
ADAM_STEP = 10

NT_DIMS = (((1,), (1,)), ((), ()))
TN_DIMS = (((0,), (0,)), ((), ()))


def _params(*sem):
    return pltpu.CompilerParams(dimension_semantics=sem, vmem_limit_bytes=VMEM_LIMIT_BYTES)


def _once(spec):
    return pl.BlockSpec(spec.block_shape, spec.index_map, pipeline_mode=pl.Buffered(1))


def _dot(a, b):
    return jnp.dot(a, b, preferred_element_type=F32)


def _dot_nt(a, b):
    return lax.dot_general(a, b, NT_DIMS, preferred_element_type=F32)


def _dot_tn(a, b):
    return lax.dot_general(a, b, TN_DIMS, preferred_element_type=F32)


def _split_dot(a, b):
    hi = a.astype(BF16)
    lo = (a - hi.astype(F32)).astype(BF16)
    return _dot(hi, b) + _dot(lo, b)


def _sigmoid(z):
    return 1.0 / (1.0 + jnp.exp(-z))


def _norm(x):
    r = lax.rsqrt(jnp.mean(x * x, axis=-1, keepdims=True) + EPS)
    return x * r, r


def _modulate(x, g, mod_ref, k):
    n, _ = _norm(x)
    shift = mod_ref[3 * k:3 * k + 1, :]
    scale = mod_ref[3 * k + 1:3 * k + 2, :]
    return n * g * (1.0 + scale) + shift


def _modulate_bwd(dh, x, g, mod_ref, k):
    n, r = _norm(x)
    scale = mod_ref[3 * k + 1:3 * k + 2, :]
    dshift = jnp.sum(dh, axis=0, keepdims=True)
    dscale = jnp.sum(dh * n * g, axis=0, keepdims=True)
    dg = jnp.sum(dh * n * (1.0 + scale), axis=0, keepdims=True)
    dn = dh * g * (1.0 + scale)
    dx = r * (dn - n * jnp.mean(dn * n, axis=-1, keepdims=True))
    return dx, dshift, dscale, dg


class _Exchange:
    def __init__(self, arrays, scatter, relay=False, chips=None):
        assert not (scatter and relay)
        self.arrays = list(arrays)
        self.scatter = scatter
        self.relay = relay
        self.n = len(self.arrays)
        self.chips = list(chips) if chips is not None else [False] * self.n
        assert scatter or not any(self.chips)
        self.out_shape = [
            jax.ShapeDtypeStruct((N_DEV // 2 if ch else N_DEV,) + tuple(a.shape[1:] if scatter else a.shape), a.dtype)
            for a, ch in zip(self.arrays, self.chips)]
        n_remote = self.n * (N_DEV - 1)
        self.scratch_shapes = [pltpu.SemaphoreType.DMA((n_remote,)), pltpu.SemaphoreType.DMA((n_remote,)),
                               pltpu.SemaphoreType.DMA((self.n,))]

    def _copies(self, in_refs, out_refs, sems):
        send_sems, recv_sems, local_sems = sems
        x, y, c = lax.axis_index("x"), lax.axis_index("y"), lax.axis_index("c")
        me = 4 * x + 2 * y + c
        local, remote, relayed = [], {}, {}
        for a in range(self.n):
            if self.chips[a]:
                mine = 2 * x + y
                local.append(pltpu.make_async_copy(in_refs[a].at[mine], out_refs[a].at[mine], local_sems.at[a]))
                for k in (2, 4, 6):
                    px = 1 - x if k & 4 else x
                    py = 1 - y if k & 2 else y
                    sem = a * (N_DEV - 1) + k - 1
                    remote[a, k] = pltpu.make_async_remote_copy(
                        src_ref=in_refs[a].at[2 * px + py], dst_ref=out_refs[a].at[mine],
                        send_sem=send_sems.at[sem], recv_sem=recv_sems.at[sem],
                        device_id=(px, py, c), device_id_type=pl.DeviceIdType.MESH)
                continue
            src = in_refs[a].at[me] if self.scatter else in_refs[a]
            local.append(pltpu.make_async_copy(src, out_refs[a].at[me], local_sems.at[a]))
            for k in range(1, N_DEV):
                px = 1 - x if k & 4 else x
                py = 1 - y if k & 2 else y
                pc = 1 - c if k & 1 else c
                sem = a * (N_DEV - 1) + k - 1
                if self.relay and k & 1 and k > 1:
                    slot = 4 * px + 2 * py + c
                    relayed[a, k] = pltpu.make_async_remote_copy(
                        src_ref=out_refs[a].at[slot], dst_ref=out_refs[a].at[slot],
                        send_sem=send_sems.at[sem], recv_sem=recv_sems.at[sem],
                        device_id=(x, y, 1 - c), device_id_type=pl.DeviceIdType.MESH)
                    continue
                src = in_refs[a].at[4 * px + 2 * py + pc] if self.scatter else in_refs[a]
                remote[a, k] = pltpu.make_async_remote_copy(
                    src_ref=src, dst_ref=out_refs[a].at[me],
                    send_sem=send_sems.at[sem], recv_sem=recv_sems.at[sem],
                    device_id=(px, py, pc), device_id_type=pl.DeviceIdType.MESH)
        return local, remote, relayed

    def start(self, in_refs, out_refs, sems):
        local, remote, _ = self._copies(in_refs, out_refs, sems)
        for cp in local + list(remote.values()):
            cp.start()

    def wait(self, in_refs, out_refs, sems):
        local, remote, relayed = self._copies(in_refs, out_refs, sems)
        for (a, k), cp in relayed.items():
            remote[a, k - 1].wait_recv()
            cp.start()
        for (a, k), cp in remote.items():
            if (a, k + 1) not in relayed:
                cp.wait_recv()
        for cp in relayed.values():
            cp.wait_recv()
        for cp in list(remote.values()) + list(relayed.values()):
            cp.wait_send()
        for cp in local:
            cp.wait()


def _call(body, *, name, args, in_specs, out_specs, out_shape, scratch_shapes=(), grid=(),
          params=None, exchange=None):
    n_in, n_out = len(args), len(out_shape)
    if exchange is None:
        outs = pl.pallas_call(
            body, name=name, grid=grid, in_specs=list(in_specs), out_specs=list(out_specs),
            out_shape=list(out_shape), scratch_shapes=list(scratch_shapes), compiler_params=params,
        )(*args)
        return list(outs), []
    n_ex = exchange.n

    def wrapped(*refs):
        ins, refs = refs[:n_in], refs[n_in:]
        ex_in, refs = refs[:n_ex], refs[n_ex:]
        outs, refs = refs[:n_out], refs[n_out:]
        ex_out, refs = refs[:n_ex], refs[n_ex:]
        scratch, sems = refs[:len(refs) - 3], refs[len(refs) - 3:]
        if not grid:
            exchange.start(ex_in, ex_out, sems)
            body(*ins, *outs, *scratch)
            exchange.wait(ex_in, ex_out, sems)
            return
        first = functools.reduce(jnp.logical_and, [pl.program_id(a) == 0 for a in range(len(grid))])
        last = functools.reduce(jnp.logical_and, [pl.program_id(a) == grid[a] - 1 for a in range(len(grid))])

        @pl.when(first)
        def _():
            exchange.start(ex_in, ex_out, sems)

        body(*ins, *outs, *scratch)

        @pl.when(last)
        def _():
            exchange.wait(ex_in, ex_out, sems)

    any_spec = pl.BlockSpec(memory_space=pl.ANY)
    outs = pl.pallas_call(
        wrapped, name=name, grid=grid,
        in_specs=list(in_specs) + [any_spec] * n_ex, out_specs=list(out_specs) + [any_spec] * n_ex,
        out_shape=list(out_shape) + exchange.out_shape,
        scratch_shapes=list(scratch_shapes) + exchange.scratch_shapes, compiler_params=params,
    )(*args, *exchange.arrays)
    return list(outs[:n_out]), list(outs[n_out:])


def _exchange(arrays, scatter, name, relay=False, chips=None):
    return _call(lambda: None, name=name, args=(), in_specs=(), out_specs=(), out_shape=(),
                 exchange=_Exchange(arrays, scatter, relay, chips))[1]


def _first_exchange(c_pad, shards, w, b):
    rows, d = c_pad.shape
    cols = w.shape[1]
    ex_c = _Exchange([c_pad], False)
    ex_w = _Exchange(shards, False, relay=True)
    ex_m = _Exchange([jax.ShapeDtypeStruct((N_DEV * rows, cols), F32)], False)
    n_w = ex_w.n

    def body(*refs):
        c_ref, w_refs, wa_ref, b_ref = refs[0], refs[1:1 + n_w], refs[1 + n_w], refs[2 + n_w]
        outs = refs[3 + n_w:]
        cg_ref, wg_refs, mg_ref = outs[0], outs[1:1 + n_w], outs[1 + n_w]
        scratch = outs[2 + n_w:]
        sems_c, sems_w, sems_m, c_vm, m_vm = scratch[0:3], scratch[3:6], scratch[6:9], scratch[9], scratch[10]
        ex_c.start([c_ref], [cg_ref], sems_c)
        ex_c.wait([c_ref], [cg_ref], sems_c)
        pltpu.sync_copy(cg_ref, c_vm)
        cv = c_vm[...].reshape(N_DEV * rows, d)
        s = (cv * _sigmoid(cv)).astype(BF16)
        m_vm[...] = _dot(s, wa_ref[...].astype(BF16)) + b_ref[...]
        ex_m.start([m_vm], [mg_ref], sems_m)
        ex_w.start(w_refs, wg_refs, sems_w)
        ex_m.wait([m_vm], [mg_ref], sems_m)
        ex_w.wait(w_refs, wg_refs, sems_w)

    any_spec = pl.BlockSpec(memory_space=pl.ANY)
    vmem_spec = pl.BlockSpec(memory_space=pltpu.VMEM)
    outs = pl.pallas_call(
        body, name="first_exchange",
        in_specs=[any_spec] * (1 + n_w) + [vmem_spec, vmem_spec],
        out_specs=[any_spec] * (2 + n_w),
        out_shape=ex_c.out_shape + ex_w.out_shape + ex_m.out_shape,
        scratch_shapes=ex_c.scratch_shapes + ex_w.scratch_shapes + ex_m.scratch_shapes
        + [pltpu.VMEM((N_DEV, rows, d), F32), pltpu.VMEM((N_DEV * rows, cols), F32)],
        compiler_params=pltpu.CompilerParams(vmem_limit_bytes=VMEM_LIMIT_BYTES),
    )(c_pad, *shards, w, b)
    return outs[0], outs[1 + n_w], list(outs[1:1 + n_w])


def _ada_bwd(c_all, dmod_cols, dmod_all):
    def body(c_ref, dc_ref, da_ref, gw_ref, gb_ref):
        cv = c_ref[...]
        s = cv * _sigmoid(cv)
        gw_ref[...] = lax.dot_general(s, dc_ref[...], TN_DIMS, preferred_element_type=F32,
                                      precision=lax.Precision.HIGHEST)
        gb_ref[...] = jnp.sum(da_ref[...], axis=0, keepdims=True)

    return pl.pallas_call(
        body, name="ada_bwd",
        out_shape=(jax.ShapeDtypeStruct((c_all.shape[1], dmod_cols.shape[1]), F32),
                   jax.ShapeDtypeStruct((1, dmod_all.shape[1]), F32)),
        compiler_params=pltpu.CompilerParams(vmem_limit_bytes=VMEM_LIMIT_BYTES),
    )(c_all, dmod_cols, dmod_all)


def _side_by_side(w_ref):
    return jnp.concatenate([w_ref[s] for s in range(w_ref.shape[0])], axis=1)


def _stacked(w_ref):
    return jnp.concatenate([w_ref[s] for s in range(w_ref.shape[0])], axis=0)


def _loss_tile(x, target, g, acc_ref):
    d = x.shape[1]
    n, r = _norm(x)
    err = n * g - target
    dy = err * (1.0 / d)
    acc_ref[0:1, :] += jnp.sum(err * err, axis=0, keepdims=True)
    acc_ref[1:2, :] += jnp.sum(dy * n, axis=0, keepdims=True)
    dn = dy * g
    return r * (dn - n * jnp.mean(dn * n, axis=-1, keepdims=True))


def _ffn_fwd(x, mod, g, wg, wu, wd, k, tm, exchange=None, head=None):
    t, d = x.shape
    ns, _, fs = wg.shape
    nt = t // tm
    tpb = nt // mod.shape[0]
    rows = tm // FFN_CHUNKS
    extra = list(head) if head is not None else []

    def body(x_ref, mod_ref, g_ref, wg_ref, wu_ref, wd_ref, *rest):
        if head is not None:
            t_ref, gf_ref, xo_ref, f_ref, gg_ref, uu_ref, head_ref, h_sc, acc = rest
        else:
            xo_ref, f_ref, gg_ref, uu_ref, h_sc, acc = rest
        i, j = pl.program_id(0), pl.program_id(1)

        @pl.when(j == 0)
        def _():
            h_sc[...] = _modulate(x_ref[...], g_ref[...], mod_ref, k).astype(BF16)
            acc[...] = jnp.zeros_like(acc)

        chunks = [pl.ds(c * rows, rows) for c in range(FFN_CHUNKS)]
        wg, wu, wd = _side_by_side(wg_ref), _side_by_side(wu_ref), _stacked(wd_ref)
        gates, ups = [], []
        for rs in chunks:
            h = h_sc[rs, :]
            gates.append(_dot(h, wg))
            ups.append(_dot(h, wu))
        acts = [(g * _sigmoid(g) * u).astype(BF16) for g, u in zip(gates, ups)]
        for rs, g, u in zip(chunks, gates, ups):
            for s in range(SHARD_GROUP):
                gg_ref[s, rs, :] = g[:, s * fs:(s + 1) * fs].astype(BF16)
                uu_ref[s, rs, :] = u[:, s * fs:(s + 1) * fs].astype(BF16)
        downs = [_dot(a, wd) for a in acts]
        for rs, dn in zip(chunks, downs):
            acc[rs, :] += dn

        @pl.when(j == ns // SHARD_GROUP - 1)
        def _():
            f = acc[...]
            f_ref[...] = f.astype(BF16)
            xo = x_ref[...] + 0.5 * mod_ref[3 * k + 2:3 * k + 3, :] * f
            if head is None:
                xo_ref[...] = xo
            else:
                @pl.when(i == 0)
                def _():
                    head_ref[...] = jnp.zeros_like(head_ref)

                xo_ref[...] = _loss_tile(xo, t_ref[...], gf_ref[...], head_ref)

    tok = pl.BlockSpec((tm, d), lambda i, j: (i, 0))
    row = pl.BlockSpec((1, d), lambda i, j: (0, 0))
    hid = pl.BlockSpec((SHARD_GROUP, tm, fs), lambda i, j: (j, i, 0))
    head_specs = [_once(tok), row] if head is not None else []
    head_out = [pl.BlockSpec((8, d), lambda i, j: (0, 0))] if head is not None else []
    head_shape = [jax.ShapeDtypeStruct((8, d), F32)] if head is not None else []
    return _call(
        body, name=f"ffn_fwd{k}", grid=(nt, ns // SHARD_GROUP), args=(x, mod, g, wg, wu, wd, *extra),
        in_specs=[tok,
                  pl.BlockSpec((None, N_MOD, d), lambda i, j: (i // tpb, 0, 0)),
                  row,
                  pl.BlockSpec((SHARD_GROUP, d, fs), lambda i, j: (j, 0, 0)),
                  pl.BlockSpec((SHARD_GROUP, d, fs), lambda i, j: (j, 0, 0)),
                  pl.BlockSpec((SHARD_GROUP, fs, d), lambda i, j: (j, 0, 0))] + head_specs,
        out_specs=[tok, tok, hid, hid] + head_out,
        out_shape=[jax.ShapeDtypeStruct((t, d), F32), jax.ShapeDtypeStruct((t, d), BF16),
                   jax.ShapeDtypeStruct((ns, t, fs), BF16), jax.ShapeDtypeStruct((ns, t, fs), BF16)]
        + head_shape,
        scratch_shapes=[pltpu.VMEM((tm, d), BF16), pltpu.VMEM((tm, d), F32)],
        params=_params("arbitrary", "arbitrary"), exchange=exchange)


def _ffn_bwd(dxo, x, f, mod, g, gate, up, wg, wu, wd, k, tm, exchange=None):
    t, d = x.shape
    ns, _, fs = wg.shape
    nt = t // tm
    nb = mod.shape[0]
    tpb = nt // nb
    rows = tm // FFN_CHUNKS

    def body(dxo_ref, x_ref, f_ref, mod_ref, g_ref, gg_ref, uu_ref, wg_ref, wu_ref, wd_ref,
             dx_ref, dgg_ref, duu_ref, act_ref, h_ref, df_ref, dmod_ref, dg_ref, acc):
        i, j = pl.program_id(0), pl.program_id(1)

        @pl.when(j == 0)
        def _():
            df = 0.5 * mod_ref[3 * k + 2:3 * k + 3, :] * dxo_ref[...]
            df_ref[...] = df.astype(BF16)
            h_ref[...] = _modulate(x_ref[...], g_ref[...], mod_ref, k).astype(BF16)
            acc[...] = jnp.zeros_like(acc)

        chunks = [pl.ds(c * rows, rows) for c in range(FFN_CHUNKS)]
        group = range(SHARD_GROUP)
        wg, wu, wd = _side_by_side(wg_ref), _side_by_side(wu_ref), _stacked(wd_ref)
        dacts = [_dot_nt(df_ref[rs, :], wd) for rs in chunks]
        dgates, dups = [], []
        for rs, dact in zip(chunks, dacts):
            gv = jnp.concatenate([gg_ref[s, rs, :] for s in group], axis=1).astype(F32)
            uv = jnp.concatenate([uu_ref[s, rs, :] for s in group], axis=1).astype(F32)
            sig = _sigmoid(gv)
            s_act = gv * sig
            act = (s_act * uv).astype(BF16)
            for s in group:
                act_ref[s, rs, :] = act[:, s * fs:(s + 1) * fs]
            dups.append((dact * s_act).astype(BF16))
            dgates.append((dact * uv * (sig * (1.0 + gv * (1.0 - sig)))).astype(BF16))
        dhs = [_dot_nt(dg, wg) + _dot_nt(du, wu) for dg, du in zip(dgates, dups)]
        for rs, dg, du, dh in zip(chunks, dgates, dups, dhs):
            for s in group:
                dgg_ref[s, rs, :] = dg[:, s * fs:(s + 1) * fs]
                duu_ref[s, rs, :] = du[:, s * fs:(s + 1) * fs]
            acc[rs, :] += dh

        @pl.when(j == ns // SHARD_GROUP - 1)
        def _():
            dx, dshift, dscale, dg = _modulate_bwd(acc[...], x_ref[...], g_ref[...], mod_ref, k)
            dxo_v = dxo_ref[...]
            dx_ref[...] = dxo_v + dx
            dgt = jnp.sum(0.5 * f_ref[...].astype(F32) * dxo_v, axis=0, keepdims=True)

            @pl.when(i % tpb == 0)
            def _():
                dmod_ref[...] = jnp.zeros_like(dmod_ref)

            @pl.when(i == 0)
            def _():
                dg_ref[...] = jnp.zeros_like(dg_ref)

            dmod_ref[0:1, :] += dshift
            dmod_ref[1:2, :] += dscale
            dmod_ref[2:3, :] += dgt
            dg_ref[0:1, :] += dg

    tok = pl.BlockSpec((tm, d), lambda i, j: (i, 0))
    hid = pl.BlockSpec((SHARD_GROUP, tm, fs), lambda i, j: (j, i, 0))
    return _call(
        body, name=f"ffn_bwd{k}", grid=(nt, ns // SHARD_GROUP), args=(dxo, x, f, mod, g, gate, up, wg, wu, wd),
        in_specs=[tok, tok, tok,
                  pl.BlockSpec((None, N_MOD, d), lambda i, j: (i // tpb, 0, 0)),
                  pl.BlockSpec((1, d), lambda i, j: (0, 0)),
                  hid, hid,
                  pl.BlockSpec((SHARD_GROUP, d, fs), lambda i, j: (j, 0, 0)),
                  pl.BlockSpec((SHARD_GROUP, d, fs), lambda i, j: (j, 0, 0)),
                  pl.BlockSpec((SHARD_GROUP, fs, d), lambda i, j: (j, 0, 0))],
        out_specs=[tok, hid, hid, hid, tok, tok,
                   pl.BlockSpec((None, 8, d), lambda i, j: (i // tpb, 0, 0)),
                   pl.BlockSpec((8, d), lambda i, j: (0, 0))],
        out_shape=[jax.ShapeDtypeStruct((t, d), F32),
                   jax.ShapeDtypeStruct((ns, t, fs), BF16), jax.ShapeDtypeStruct((ns, t, fs), BF16),
                   jax.ShapeDtypeStruct((ns, t, fs), BF16),
                   jax.ShapeDtypeStruct((t, d), BF16), jax.ShapeDtypeStruct((t, d), BF16),
                   jax.ShapeDtypeStruct((nb, 8, d), F32), jax.ShapeDtypeStruct((8, d), F32)],
        scratch_shapes=[pltpu.VMEM((tm, d), F32)],
        params=_params("arbitrary", "arbitrary"), exchange=exchange)


def _mm_tn(a, b, a_spec, b_spec, out_shape, n_tiles, name, exchange=None, keep_transposed=False,
           pair_reduce=False):
    n_out = out_shape[0]
    block = tuple(out_shape[1:])
    last = n_tiles - 1
    flip = block[0] > block[1]
    if flip:
        block = block[::-1]
    if flip and keep_transposed:
        flip_back, out_shape = False, (n_out,) + block
    else:
        flip_back = flip
    full_shape = tuple(out_shape)
    n_pairs = n_out // 2
    if pair_reduce:
        out_shape = (n_pairs,) + full_shape[1:]

    def body(a_ref, b_ref, o_ref, acc, *pair):
        i, j = pl.program_id(0), pl.program_id(1)
        prod = _dot_tn(b_ref[...], a_ref[...]) if flip else _dot_tn(a_ref[...], b_ref[...])
        full_ref = pair[0] if pair_reduce else o_ref

        @pl.when(i == 0)
        def _():
            acc[j] = prod

        @pl.when(i > 0)
        def _():
            acc[j] += prod

        @pl.when(i == last)
        def _():
            total = acc[j]
            full_ref[j] = (total.T if flip_back else total).astype(BF16)

        if pair_reduce:
            _, landed, send_sems, recv_sems = pair

            @pl.when(jnp.logical_and(i == last, j == n_out - 1))
            def _():
                x, y, c = lax.axis_index("x"), lax.axis_index("y"), lax.axis_index("c")
                copies = [pltpu.make_async_remote_copy(
                    src_ref=full_ref.at[2 * q + 1 - c], dst_ref=landed.at[q],
                    send_sem=send_sems.at[q], recv_sem=recv_sems.at[q],
                    device_id=(x, y, 1 - c), device_id_type=pl.DeviceIdType.MESH) for q in range(n_pairs)]
                for cp in copies:
                    cp.start()
                for q, cp in enumerate(copies):
                    cp.wait_recv()
                    o_ref[q] = (full_ref[2 * q + c].astype(F32) + landed[q].astype(F32)).astype(BF16)
                for cp in copies:
                    cp.wait_send()

    scratch = [pltpu.VMEM((n_out,) + block, F32)]
    if pair_reduce:
        scratch += [pltpu.VMEM(full_shape, BF16), pltpu.VMEM(out_shape, BF16),
                    pltpu.SemaphoreType.DMA((n_pairs,)), pltpu.SemaphoreType.DMA((n_pairs,))]
    outs, sent = _call(
        body, name=name, grid=(n_tiles, n_out), args=(a, b), in_specs=[a_spec, b_spec],
        out_specs=[pl.BlockSpec(out_shape, lambda i, j: (0,) * len(out_shape))],
        out_shape=[jax.ShapeDtypeStruct(out_shape, BF16)],
        scratch_shapes=scratch,
        params=_params("arbitrary", "arbitrary"), exchange=exchange)
    return (outs[0], sent) if exchange is not None else outs[0]


def _ffn_weight_grads(h, dgate, dup, act, df, tm, tag, stream=False, first=None):
    t, d = h.shape
    ns, _, fs = dgate.shape
    nt = t // tm
    tok = pl.BlockSpec((tm, d), lambda i, j: (i, 0))
    hid = pl.BlockSpec((None, tm, fs), lambda i, j: (j, i, 0))
    if not stream:
        gwg = _mm_tn(h, dgate, tok, hid, (ns, d, fs), nt, f"grad_wg{tag}", keep_transposed=True)
        gwu = _mm_tn(h, dup, tok, hid, (ns, d, fs), nt, f"grad_wu{tag}", keep_transposed=True)
        gwd = _mm_tn(act, df, hid, tok, (ns, fs, d), nt, f"grad_wd{tag}")
        return gwg, gwu, gwd
    gwg, brought = _mm_tn(h, dgate, tok, hid, (ns, d, fs), nt, f"grad_wg{tag}", first,
                          keep_transposed=True, pair_reduce=True)
    gwu, sent_g = _mm_tn(h, dup, tok, hid, (ns, d, fs), nt, f"grad_wu{tag}",
                         _Exchange([gwg], True, chips=[True]), keep_transposed=True, pair_reduce=True)
    gwd, sent_u = _mm_tn(act, df, hid, tok, (ns, fs, d), nt, f"grad_wd{tag}",
                         _Exchange([gwu], True, chips=[True]), pair_reduce=True)
    return sent_g[0], sent_u[0], gwd, brought


def _stage_shape(rows, cols):
    return pltpu.VMEM((cols // LANES, rows, LANES), F32)


def _stage(value, stage_ref):
    for k in range(stage_ref.shape[0]):
        stage_ref[k] = value[:, k * LANES:(k + 1) * LANES]


def _to_residue_rows(stage_ref, dst_ref, dil):
    rows = stage_ref.shape[1] // dil
    for r in range(dil):
        for k in range(stage_ref.shape[0]):
            dst_ref[r, :, k * LANES:(k + 1) * LANES] = (
                stage_ref.at[k][pl.ds(r, rows, stride=dil), :].astype(dst_ref.dtype))


def _from_residue_rows(src_ref, stage_ref, dil):
    rows = stage_ref.shape[1] // dil
    chunks = range(stage_ref.shape[0])
    for r in range(dil):
        for k in chunks:
            stage_ref.at[k][pl.ds(r, rows, stride=dil), :] = src_ref[r, :, k * LANES:(k + 1) * LANES].astype(F32)
    return jnp.concatenate([stage_ref[k] for k in chunks], axis=1)


def _residue_shape(nb, seq, width, dil, dtype):
    return jax.ShapeDtypeStruct((nb, dil, seq // dil, width), dtype)


def _residue_spec(tm, tpb, cols, dil, col_block):
    return pl.BlockSpec((None, dil, tm // dil, cols),
                        lambda i, *rest: (i // tpb, 0, i % tpb, col_block(i, *rest)))


def _qkv_fwd(x, mod, g, win, tm, exchange=None):
    t, d = x.shape
    ns, _, cs = win.shape
    nt = t // tm
    nb = mod.shape[0]
    tpb = nt // nb
    seq = t // nb
    width = ns * cs // 2
    cs, ns = cs * SHARD_GROUP, ns // SHARD_GROUP
    half = ns // 2
    n_res = len(DILATIONS) - 1

    def body(x_ref, mod_ref, g_ref, w_ref, sb_ref, dil_ref, *rest):
        res_refs, h_ref, sc = rest[:n_res], rest[n_res], rest[n_res + 1]
        j = pl.program_id(1)

        @pl.when(j == 0)
        def _():
            h_ref[...] = _modulate(x_ref[...], g_ref[...], mod_ref, 1).astype(BF16)

        res = _dot(h_ref[...], _side_by_side(w_ref))

        @pl.when(j < half)
        def _():
            sb_ref[...] = res.astype(BF16)

        @pl.when(j >= half)
        def _():
            dil_ref[...] = res.astype(BF16)
            _stage(res, sc)
            for ref, dil in zip(res_refs, DILATIONS[1:]):
                _to_residue_rows(sc, ref, dil)

    def dil_col(i, j):
        return jnp.maximum(j - half, 0)

    tok = pl.BlockSpec((tm, d), lambda i, j: (i, 0))
    wide = jax.ShapeDtypeStruct((t, width), BF16)
    outs, got = _call(
        body, name="qkv_fwd", grid=(nt, ns), args=(x, mod, g, win),
        in_specs=[tok,
                  pl.BlockSpec((None, N_MOD, d), lambda i, j: (i // tpb, 0, 0)),
                  pl.BlockSpec((1, d), lambda i, j: (0, 0)),
                  pl.BlockSpec((SHARD_GROUP, d, cs // SHARD_GROUP), lambda i, j: (j, 0, 0))],
        out_specs=[pl.BlockSpec((tm, cs), lambda i, j: (i, jnp.minimum(j, half - 1))),
                   pl.BlockSpec((tm, cs), lambda i, j: (i, dil_col(i, j)))]
        + [_residue_spec(tm, tpb, cs, dil, dil_col) for dil in DILATIONS[1:]] + [tok],
        out_shape=[wide, wide] + [_residue_shape(nb, seq, width, dil, BF16) for dil in DILATIONS[1:]]
        + [jax.ShapeDtypeStruct((t, d), BF16)],
        scratch_shapes=[_stage_shape(tm, cs)],
        params=_params("arbitrary", "arbitrary"), exchange=exchange)
    qkv_dil = [outs[1]] + [a.reshape(t, width) for a in outs[2:2 + n_res]]
    return (outs[0], qkv_dil, outs[-1]), got


def _qkv_bwd(dqkv, dxo, x, mod, g, win, tm, exchange=None):
    t, d = x.shape
    ns, _, cs = win.shape
    nt = t // tm
    nb = mod.shape[0]
    tpb = nt // nb
    cs, ns = cs * SHARD_GROUP, ns // SHARD_GROUP

    def body(dq_ref, dxo_ref, x_ref, mod_ref, g_ref, w_ref, dx_ref, dmod_ref, dg_ref, acc):
        i, j = pl.program_id(0), pl.program_id(1)

        @pl.when(j == 0)
        def _():
            acc[...] = jnp.zeros_like(acc)

        acc[...] += _dot_nt(dq_ref[...], _side_by_side(w_ref))

        @pl.when(j == ns - 1)
        def _():
            dx, dshift, dscale, dg = _modulate_bwd(acc[...], x_ref[...], g_ref[...], mod_ref, 1)
            dx_ref[...] = dxo_ref[...] + dx

            @pl.when(i % tpb == 0)
            def _():
                dmod_ref[...] = jnp.zeros_like(dmod_ref)

            @pl.when(i == 0)
            def _():
                dg_ref[...] = jnp.zeros_like(dg_ref)

            dmod_ref[0:1, :] += dshift
            dmod_ref[1:2, :] += dscale
            dg_ref[0:1, :] += dg

    tok = pl.BlockSpec((tm, d), lambda i, j: (i, 0))
    return _call(
        body, name="qkv_bwd", grid=(nt, ns), args=(dqkv, dxo, x, mod, g, win),
        in_specs=[pl.BlockSpec((tm, cs), lambda i, j: (i, j)), tok, tok,
                  pl.BlockSpec((None, N_MOD, d), lambda i, j: (i // tpb, 0, 0)),
                  pl.BlockSpec((1, d), lambda i, j: (0, 0)),
                  pl.BlockSpec((SHARD_GROUP, d, cs // SHARD_GROUP), lambda i, j: (j, 0, 0))],
        out_specs=[tok,
                   pl.BlockSpec((None, 8, d), lambda i, j: (i // tpb, 0, 0)),
                   pl.BlockSpec((8, d), lambda i, j: (0, 0))],
        out_shape=[jax.ShapeDtypeStruct((t, d), F32),
                   jax.ShapeDtypeStruct((nb, 8, d), F32), jax.ShapeDtypeStruct((8, d), F32)],
        scratch_shapes=[pltpu.VMEM((tm, d), F32)],
        params=_params("arbitrary", "arbitrary"), exchange=exchange)


def _heads(a):
    return [a[:, h * HEAD_DIM:(h + 1) * HEAD_DIM] for h in range(a.shape[1] // HEAD_DIM)]


def _own_lanes():
    lane = lax.broadcasted_iota(jnp.int32, (1, LANES), 1)
    return [lane < HEAD_DIM, lane >= HEAD_DIM]


def _pair_tiles(a):
    return [a[:, (h // 2) * LANES:(h // 2 + 1) * LANES] for h in range(a.shape[1] // HEAD_DIM)]


def _own_tiles(a, own):
    return [jnp.where(own[h % 2], tile, jnp.zeros_like(tile)) for h, tile in enumerate(_pair_tiles(a))]


def _merge_tiles(per_head, own):
    return jnp.concatenate([jnp.where(own[0], per_head[h], per_head[h + 1])
                            for h in range(0, len(per_head), 2)], axis=1)


def _scaled(q):
    return (q.astype(F32) * (HEAD_DIM ** -0.5)).astype(BF16)


def _sb_logits(qh, kh, tri, causal):
    zs = [_dot_nt(q, k) for q, k in zip(qh, kh)]
    es = [jnp.exp(-jnp.abs(z)) for z in zs]
    log_nots = [-(jnp.maximum(z, 0.0) + jnp.log(1.0 + e)) for z, e in zip(zs, es)]
    if causal is not None:
        log_nots = [jnp.where(causal, ln, 0.0) for ln in log_nots]
    return zs, es, [_split_dot(ln, tri) for ln in log_nots]


def _sb_masks():
    rows = lax.broadcasted_iota(jnp.int32, (SB_BLOCK, SB_BLOCK), 0)
    cols = lax.broadcasted_iota(jnp.int32, (SB_BLOCK, SB_BLOCK), 1)
    return (rows >= cols).astype(BF16), (rows <= cols).astype(BF16), cols < rows


def _sb_fwd(qkv, nb, seq, exchange=None):
    t = qkv.shape[0]
    n_pairs = (qkv.shape[1] // 3) // SB_WIDTH
    tb = SB_BLOCK
    n_blk = seq // tb

    def body(q_ref, k_ref, v_ref, o_ref, c_ref):
        tri, _, causal = _sb_masks()
        own = _own_lanes()

        def key_block(qh, kj, carry, mask):
            ks = pl.multiple_of(kj * tb, tb)
            kh, vh = _pair_tiles(k_ref[pl.ds(ks, tb), :]), _pair_tiles(v_ref[pl.ds(ks, tb), :])
            zs, _, suffixes = _sb_logits(qh, kh, tri, mask)
            ws = [jnp.exp(z + suffix + cr[1]) for z, suffix, cr in zip(zs, suffixes, carry)]
            if mask is not None:
                ws = [jnp.where(mask, w, 0.0) for w in ws]
            pv = [_dot(w.astype(BF16), v) for w, v in zip(ws, vh)]
            return tuple((cr[0] + p, cr[1] + suffix[:, 0:1]) for cr, p, suffix in zip(carry, pv, suffixes))

        def query_block(qi, _):
            qs = pl.multiple_of(qi * tb, tb)
            qh = _own_tiles(_scaled(q_ref[pl.ds(qs, tb), :]), own)
            zero = (jnp.zeros((tb, LANES), F32), jnp.zeros((tb, 1), F32))
            carry = key_block(qh, qi, (zero,) * SB_HEADS, causal)
            carry = lax.fori_loop(0, qi, lambda it, cr: key_block(qh, qi - 1 - it, cr, None), carry)
            o_ref[pl.ds(qs, tb), :] = _merge_tiles([cr[0] for cr in carry], own)
            c_ref[pl.ds(qs, tb), :] = _merge_tiles([jnp.broadcast_to(cr[1], (tb, LANES)) for cr in carry], own)
            return 0

        lax.fori_loop(0, n_blk, query_block, 0)

    def spec(offset):
        return pl.BlockSpec((seq, SB_WIDTH), lambda b, p: (b, offset + p))

    out = jax.ShapeDtypeStruct((t, n_pairs * SB_WIDTH), F32)
    return _call(
        body, name="sb_fwd", grid=(nb, n_pairs), args=(qkv, qkv, qkv),
        in_specs=[spec(0), spec(n_pairs), spec(2 * n_pairs)],
        out_specs=[spec(0), spec(0)], out_shape=[out, out],
        params=_params("arbitrary", "arbitrary"), exchange=exchange)


def _sb_bwd(qkv, do, csum, nb, seq, exchange=None):
    t = qkv.shape[0]
    n_pairs = (qkv.shape[1] // 3) // SB_WIDTH
    tb = SB_BLOCK
    n_blk = seq // tb
    scale = HEAD_DIM ** -0.5

    def body(q_ref, k_ref, v_ref, do_ref, c_ref, dq_ref, dk_ref, dv_ref, dkt_acc, dvt_acc):
        tri, tri_prefix, causal = _sb_masks()
        own = _own_lanes()
        dkt_acc[...] = jnp.zeros_like(dkt_acc)
        dvt_acc[...] = jnp.zeros_like(dvt_acc)

        def key_blocks(qh, qth, doh, doth, ch, kjs, carry, mask):
            nh = SB_HEADS
            chains = range(nh * len(kjs))
            kss = [pl.multiple_of(kj * tb, tb) for kj in kjs]
            kh = [tile for ks in kss for tile in _pair_tiles(k_ref[pl.ds(ks, tb), :])]
            vh = [tile for ks in kss for tile in _pair_tiles(v_ref[pl.ds(ks, tb), :])]
            zs, es, suffixes = _sb_logits(qh * len(kjs), kh, tri, mask)
            dws = [_dot_nt(doh[c % nh], vh[c]) for c in chains]
            lefts = []
            for c in chains:
                before = carry[c][1] if c < nh else lefts[c - nh]
                lefts.append(before + suffixes[c][:, 0:1])
            ws = [jnp.exp(zs[c] + suffixes[c] + (ch[c % nh] - lefts[c])) for c in chains]
            if mask is not None:
                ws = [jnp.where(mask, w, 0.0) for w in ws]
            dlws = [ws[c] * dws[c] for c in chains]
            dprefixes = [_split_dot(dlw, tri_prefix) for dlw in dlws]
            dvts = [_dot(doth[c % nh], ws[c].astype(BF16)) for c in chains]
            dlefts, dzbs = [], []
            for c in chains:
                dlefts.append(carry[c][2] if c < nh else dlefts[c - nh] + dprefixes[c - nh][:, tb - 1:tb])
                sig = jnp.where(zs[c] >= 0.0, 1.0, es[c]) * pl.reciprocal(1.0 + es[c], approx=True)
                dz = dlws[c] - sig * (dlefts[c] + dprefixes[c])
                if mask is not None:
                    dz = jnp.where(mask, dz, 0.0)
                dzbs.append(dz.astype(BF16))
            dkts = [_dot(qth[c % nh], dzbs[c]) for c in chains]
            dqs = [_dot(dzbs[c], kh[c]) for c in chains]
            for b, ks in enumerate(kss):
                pairs = range(b * nh, (b + 1) * nh, 2)
                dkt_acc[:, pl.ds(ks, tb)] += jnp.concatenate([dkts[c] + dkts[c + 1] for c in pairs], axis=0)
                dvt_acc[:, pl.ds(ks, tb)] += jnp.concatenate([dvts[c] + dvts[c + 1] for c in pairs], axis=0)
            last = (len(kjs) - 1) * nh
            return tuple((carry[h][0] + sum(dqs[h::nh]), lefts[last + h],
                          dlefts[last + h] + dprefixes[last + h][:, tb - 1:tb]) for h in range(nh))

        def query_block(qi, _):
            qs = pl.multiple_of(qi * tb, tb)
            qh = _own_tiles(_scaled(q_ref[pl.ds(qs, tb), :]), own)
            doh = _own_tiles(do_ref[pl.ds(qs, tb), :], own)
            qth = [a.astype(F32).T.astype(BF16) for a in qh]
            doth = [a.T.astype(BF16) for a in doh]
            doh = [a.astype(BF16) for a in doh]
            cv = c_ref[pl.ds(qs, tb), :]
            ch = [cv[:, h * HEAD_DIM:h * HEAD_DIM + 1] for h in range(SB_HEADS)]
            zero = (jnp.zeros((tb, LANES), F32), jnp.zeros((tb, 1), F32), jnp.zeros((tb, 1), F32))

            def key_block(kjs, cr, mask):
                return key_blocks(qh, qth, doh, doth, ch, kjs, cr, mask)

            carry = lax.fori_loop(0, qi // 2, lambda p, cr: key_block([2 * p, 2 * p + 1], cr, None),
                                  (zero,) * SB_HEADS)
            carry = lax.fori_loop(0, qi % 2, lambda _, cr: key_block([qi - 1], cr, None), carry)
            carry = key_block([qi], carry, causal)
            dq = _merge_tiles([cr[0] for cr in carry], own) * scale
            dq_ref[pl.ds(qs, tb), :] = dq.astype(BF16)
            return 0

        lax.fori_loop(0, n_blk, query_block, 0)
        dk_ref[...] = dkt_acc[...].T.astype(BF16)
        dv_ref[...] = dvt_acc[...].T.astype(BF16)

    def spec(offset):
        return pl.BlockSpec((seq, SB_WIDTH), lambda b, p: (b, offset + p))

    out = jax.ShapeDtypeStruct((t, n_pairs * SB_WIDTH), BF16)
    return _call(
        body, name="sb_bwd", grid=(nb, n_pairs), args=(qkv, qkv, qkv, do, csum),
        in_specs=[spec(0), spec(n_pairs), spec(2 * n_pairs), spec(0), spec(0)],
        out_specs=[spec(0), spec(0), spec(0)],
        out_shape=[out, out, out],
        scratch_shapes=[pltpu.VMEM((SB_WIDTH, seq), F32), pltpu.VMEM((SB_WIDTH, seq), F32)],
        params=_params("arbitrary", "arbitrary"), exchange=exchange)


def _dil_block_scores(qh, kph, kch, bias_ref, has_prev, band_prev, band_cur):
    scale = HEAD_DIM ** -0.5
    heads = range(len(qh))
    no_prev = jnp.where(has_prev, 0.0, NEG_INF)
    zps = [_dot_nt(qh[h], kph[h]) for h in heads]
    zcs = [_dot_nt(qh[h], kch[h]) for h in heads]
    zps = [jnp.where(band_prev, zps[h] * scale + bias_ref[h, :, 0:DIL_BLOCK], NEG_INF) + no_prev for h in heads]
    zcs = [jnp.where(band_cur, zcs[h] * scale + bias_ref[h, :, DIL_BLOCK:2 * DIL_BLOCK], NEG_INF) for h in heads]
    return zps, zcs


def _dil_bands():
    rows = lax.broadcasted_iota(jnp.int32, (DIL_BLOCK, DIL_BLOCK), 0)
    cols = lax.broadcasted_iota(jnp.int32, (DIL_BLOCK, DIL_BLOCK), 1)
    return cols >= rows, cols <= rows


def _dil_fwd(qkv, bias, nb, seq, dil, exchange=None):
    t, width = qkv.shape
    n_pairs = (width // 3) // DIL_WIDTH
    bq = DIL_BLOCK
    n_blk = seq // bq
    per_seq = n_blk // dil
    heads = range(DIL_HEADS)

    def body(q_ref, k_ref, v_ref, bias_ref, o_ref, lse_ref):
        band_prev, band_cur = _dil_bands()
        own = _own_lanes()

        def block(n, _):
            has_prev = (n & (per_seq - 1)) != 0
            qs = pl.multiple_of(n * bq, bq)
            ps = pl.multiple_of(jnp.maximum(n - 1, 0) * bq, bq)
            qh = _own_tiles(q_ref[pl.ds(qs, bq), :], own)
            kp, kc = _pair_tiles(k_ref[pl.ds(ps, bq), :]), _pair_tiles(k_ref[pl.ds(qs, bq), :])
            vp, vc = _pair_tiles(v_ref[pl.ds(ps, bq), :]), _pair_tiles(v_ref[pl.ds(qs, bq), :])
            zps, zcs = _dil_block_scores(qh, kp, kc, bias_ref, has_prev, band_prev, band_cur)
            ms = [jnp.maximum(jnp.max(zps[h], axis=1, keepdims=True), jnp.max(zcs[h], axis=1, keepdims=True))
                  for h in heads]
            eps = [jnp.exp(zps[h] - ms[h]) for h in heads]
            ecs = [jnp.exp(zcs[h] - ms[h]) for h in heads]
            pvs = [_dot(eps[h].astype(BF16), vp[h]) + _dot(ecs[h].astype(BF16), vc[h]) for h in heads]
            dens = [jnp.sum(eps[h], axis=1, keepdims=True) + jnp.sum(ecs[h], axis=1, keepdims=True) for h in heads]
            o_ref[pl.ds(qs, bq), :] = _merge_tiles([pvs[h] / dens[h] for h in heads], own)
            lse_ref[pl.ds(qs, bq), :] = _merge_tiles(
                [jnp.broadcast_to(ms[h] + jnp.log(dens[h]), (bq, LANES)) for h in heads], own)
            return 0

        lax.fori_loop(0, n_blk, block, 0)

    def spec(offset):
        return pl.BlockSpec((seq, DIL_WIDTH), lambda b, p: (b, offset + p))

    out = jax.ShapeDtypeStruct((t, n_pairs * DIL_WIDTH), F32)
    return _call(
        body, name=f"dil_fwd{dil}", grid=(nb, n_pairs), args=(qkv, qkv, qkv, bias),
        in_specs=[spec(0), spec(n_pairs), spec(2 * n_pairs),
                  pl.BlockSpec((DIL_HEADS, bq, 2 * bq), lambda b, p: (p, 0, 0))],
        out_specs=[spec(0), spec(0)], out_shape=[out, out],
        params=_params("arbitrary", "arbitrary"), exchange=exchange)


def _dil_bwd(qkv, bias, do, lse, delta, nb, seq, dil):
    t, width = qkv.shape
    n_pairs = (width // 3) // DIL_WIDTH
    bq = DIL_BLOCK
    n_blk = seq // bq
    per_seq = n_blk // dil
    scale = HEAD_DIM ** -0.5
    heads = range(DIL_HEADS)

    def body(q_ref, k_ref, v_ref, bias_ref, do_ref, lse_ref, dl_ref, dq_ref, dk_ref, dv_ref, db_ref,
             dk_acc, dv_acc):
        band_prev, band_cur = _dil_bands()
        own = _own_lanes()
        dk_acc[...] = jnp.zeros_like(dk_acc)
        dv_acc[...] = jnp.zeros_like(dv_acc)

        @pl.when(pl.program_id(1) == 0)
        def _():
            db_ref[...] = jnp.zeros_like(db_ref)

        def block(n, _):
            has_prev = (n & (per_seq - 1)) != 0
            qs = pl.multiple_of(n * bq, bq)
            ps = pl.multiple_of(jnp.maximum(n - 1, 0) * bq, bq)
            qh = _own_tiles(q_ref[pl.ds(qs, bq), :], own)
            kp, kc = _pair_tiles(k_ref[pl.ds(ps, bq), :]), _pair_tiles(k_ref[pl.ds(qs, bq), :])
            vp, vc = _pair_tiles(v_ref[pl.ds(ps, bq), :]), _pair_tiles(v_ref[pl.ds(qs, bq), :])
            doh = _own_tiles(do_ref[pl.ds(qs, bq), :].astype(BF16), own)
            lse_v, dl_v = lse_ref[pl.ds(qs, bq), :], dl_ref[pl.ds(qs, bq), :]
            zps, zcs = _dil_block_scores(qh, kp, kc, bias_ref, has_prev, band_prev, band_cur)
            dpp = [_dot_nt(doh[h], vp[h]) for h in heads]
            dpc = [_dot_nt(doh[h], vc[h]) for h in heads]
            lse_h = [lse_v[:, h * HEAD_DIM:h * HEAD_DIM + 1] for h in heads]
            dl_h = [dl_v[:, h * HEAD_DIM:h * HEAD_DIM + 1] for h in heads]
            pps = [jnp.exp(zps[h] - lse_h[h]) for h in heads]
            pcs = [jnp.exp(zcs[h] - lse_h[h]) for h in heads]
            dvp = [_dot_tn(pps[h].astype(BF16), doh[h]) for h in heads]
            dvc = [_dot_tn(pcs[h].astype(BF16), doh[h]) for h in heads]
            dzps = [pps[h] * (dpp[h] - dl_h[h]) for h in heads]
            dzcs = [pcs[h] * (dpc[h] - dl_h[h]) for h in heads]
            dzp_b = [(dzps[h] * scale).astype(BF16) for h in heads]
            dzc_b = [(dzcs[h] * scale).astype(BF16) for h in heads]
            dqs = [_dot(dzp_b[h], kp[h]) + _dot(dzc_b[h], kc[h]) for h in heads]
            dkp = [_dot_tn(dzp_b[h], qh[h]) for h in heads]
            dkc = [_dot_tn(dzc_b[h], qh[h]) for h in heads]
            for h in heads:
                db_ref[h, :, 0:bq] += dzps[h]
                db_ref[h, :, bq:2 * bq] += dzcs[h]
            def pair_sums(per_head):
                return jnp.concatenate([per_head[h] + per_head[h + 1] for h in heads[::2]], axis=1)

            dq_ref[pl.ds(qs, bq), :] = _merge_tiles(dqs, own).astype(BF16)
            dk_acc[pl.ds(ps, bq), :] += pair_sums(dkp)
            dk_acc[pl.ds(qs, bq), :] += pair_sums(dkc)
            dv_acc[pl.ds(ps, bq), :] += pair_sums(dvp)
            dv_acc[pl.ds(qs, bq), :] += pair_sums(dvc)
            return 0

        lax.fori_loop(0, n_blk, block, 0)
        dk_ref[...] = dk_acc[...].astype(BF16)
        dv_ref[...] = dv_acc[...].astype(BF16)

    def spec(offset):
        return pl.BlockSpec((seq, DIL_WIDTH), lambda p, b: (b, offset + p))

    bias_spec = pl.BlockSpec((DIL_HEADS, bq, 2 * bq), lambda p, b: (p, 0, 0))
    out = jax.ShapeDtypeStruct((t, n_pairs * DIL_WIDTH), BF16)
    return pl.pallas_call(
        body, name=f"dil_bwd{dil}", grid=(n_pairs, nb),
        in_specs=[spec(0), spec(n_pairs), spec(2 * n_pairs), bias_spec, spec(0), spec(0), spec(0)],
        out_specs=[spec(0), spec(0), spec(0), bias_spec],
        out_shape=[out, out, out, jax.ShapeDtypeStruct(bias.shape, F32)],
        scratch_shapes=[pltpu.VMEM((seq, DIL_WIDTH), F32), pltpu.VMEM((seq, DIL_WIDTH), F32)],
        compiler_params=_params("arbitrary", "arbitrary"),
    )(qkv, qkv, qkv, bias, do, lse, delta)


def _head_blocks(width):
    rows = lax.broadcasted_iota(jnp.int32, (width, width), 0) // HEAD_DIM
    cols = lax.broadcasted_iota(jnp.int32, (width, width), 1) // HEAD_DIM
    return (rows == cols).astype(BF16)


def _head_mean(v, gmat):
    return _split_dot(v, gmat) * (1.0 / HEAD_DIM)


def _residue_views(arrays, nb, seq):
    return [a if dil == 1 else a.reshape(nb, dil, seq // dil, a.shape[1]) for a, dil in zip(arrays, DILATIONS)]


def _mix_out_fwd(osb, ocs, lses, gsb, gdil, wout, x, mod, tm):
    t, d = x.shape
    ds = osb.shape[1]
    nt = t // tm
    nb = mod.shape[0]
    tpb = nt // nb
    seq = t // nb
    n_cfg = len(DILATIONS)

    def body(osb_ref, *refs):
        oc_refs, lse_refs = refs[:n_cfg], refs[n_cfg:2 * n_cfg]
        gsb_ref, gdil_ref, w_ref, x_ref, mod_ref = refs[2 * n_cfg:2 * n_cfg + 5]
        xo_ref, on_ref, m_ref, odil_ref = refs[2 * n_cfg + 5:2 * n_cfg + 9]
        ld_refs = refs[2 * n_cfg + 9:3 * n_cfg + 9]
        stages, sc = refs[3 * n_cfg + 9:]
        ocv, lsev = [oc_refs[0][...]], [lse_refs[0][...]]
        for i, dil in enumerate(DILATIONS[1:]):
            ocv.append(_from_residue_rows(oc_refs[i + 1], stages.at[2 * i], dil))
            lsev.append(_from_residue_rows(lse_refs[i + 1], stages.at[2 * i + 1], dil))
        top = functools.reduce(jnp.maximum, lsev)
        total = top + jnp.log(sum(jnp.exp(l - top) for l in lsev))
        odil = sum(jnp.exp(l - total) * o for o, l in zip(ocv, lsev))
        odil_ref[...] = odil
        ld_refs[0][...] = total
        _stage(total, sc)
        for ref, dil in zip(ld_refs[1:], DILATIONS[1:]):
            _to_residue_rows(sc, ref, dil)
        gm = _head_blocks(ds)
        parts = []
        for o, g_ref in ((osb_ref[...], gsb_ref), (odil, gdil_ref)):
            parts.append(o * lax.rsqrt(_head_mean(o * o, gm) + EPS) * g_ref[...])
        on = jnp.concatenate(parts, axis=1).astype(BF16)
        on_ref[...] = on
        m = _dot(on, w_ref[...])
        m_ref[...] = m
        xo_ref[...] = x_ref[...] + mod_ref[5:6, :] * m

    tok = pl.BlockSpec((tm, d), lambda i: (i, 0))
    hd = pl.BlockSpec((tm, ds), lambda i: (i, 0))
    res = [hd] + [_residue_spec(tm, tpb, ds, dil, lambda i: 0) for dil in DILATIONS[1:]]
    res_shape = [jax.ShapeDtypeStruct((t, ds), F32)] + [_residue_shape(nb, seq, ds, dil, F32) for dil in DILATIONS[1:]]
    gain = pl.BlockSpec((1, ds), lambda i: (0, 0))
    outs = pl.pallas_call(
        body, name="mix_out_fwd", grid=(nt,),
        in_specs=[hd] + res + res + [gain, gain,
                  pl.BlockSpec(wout.shape, lambda i: (0, 0)),
                  tok, pl.BlockSpec((None, N_MOD, d), lambda i: (i // tpb, 0, 0))],
        out_specs=[tok, pl.BlockSpec((tm, 2 * ds), lambda i: (i, 0)), tok, hd] + res,
        out_shape=[jax.ShapeDtypeStruct((t, d), F32), jax.ShapeDtypeStruct((t, 2 * ds), BF16),
                   jax.ShapeDtypeStruct((t, d), F32), jax.ShapeDtypeStruct((t, ds), F32)] + res_shape,
        scratch_shapes=[pltpu.VMEM((2 * (n_cfg - 1), ds // LANES, tm, LANES), F32), _stage_shape(tm, ds)],
        compiler_params=_params("arbitrary"),
    )(osb, *_residue_views(ocs, nb, seq), *_residue_views(lses, nb, seq), gsb, gdil, wout, x, mod)
    return outs[0], outs[1], outs[2], outs[3], [a.reshape(t, ds) for a in outs[4:]]


def _mix_out_bwd(dxo, m, mod, wout, osb, odil, gsb, gdil, tm):
    t, d = dxo.shape
    ds = osb.shape[1]
    nt = t // tm
    nb = mod.shape[0]
    tpb = nt // nb
    seq = t // nb
    n_cfg = len(DILATIONS)

    def body(dxo_ref, m_ref, mod_ref, w_ref, osb_ref, odil_ref, gsb_ref, gdil_ref,
             dm_ref, dosb_ref, *rest):
        do_refs, dl_refs = rest[:n_cfg], rest[n_cfg:2 * n_cfg]
        dmod_ref, dg_ref, sc = rest[2 * n_cfg:]
        dodil_ref, dldil_ref = do_refs[0], dl_refs[0]
        i = pl.program_id(0)
        dxo_v = dxo_ref[...]
        dm = (mod_ref[5:6, :] * dxo_v).astype(BF16)
        dm_ref[...] = dm
        dgt = jnp.sum(m_ref[...] * dxo_v, axis=0, keepdims=True)
        don = _dot_nt(dm, w_ref[...])
        gm = _head_blocks(ds)

        @pl.when(i % tpb == 0)
        def _():
            dmod_ref[...] = jnp.zeros_like(dmod_ref)

        @pl.when(i == 0)
        def _():
            dg_ref[...] = jnp.zeros_like(dg_ref)

        dmod_ref[2:3, :] += dgt
        groups = ((osb_ref, gsb_ref, dosb_ref), (odil_ref, gdil_ref, dodil_ref))
        for k, (o_ref, g_ref, do_ref) in enumerate(groups):
            o = o_ref[...]
            dn_out = don[:, k * ds:(k + 1) * ds]
            r = lax.rsqrt(_head_mean(o * o, gm) + EPS)
            n = o * r
            dg_ref[0:1, k * ds:(k + 1) * ds] += jnp.sum(dn_out * n, axis=0, keepdims=True)
            dn = dn_out * g_ref[...]
            do = r * (dn - n * _head_mean(dn * n, gm))
            do_ref[...] = do
            if k == 1:
                delta = _head_mean(do * o, gm) * float(HEAD_DIM)
                dldil_ref[...] = delta
                for value, refs in ((do, do_refs), (delta, dl_refs)):
                    _stage(value, sc)
                    for ref, dil in zip(refs[1:], DILATIONS[1:]):
                        _to_residue_rows(sc, ref, dil)

    tok = pl.BlockSpec((tm, d), lambda i: (i, 0))
    hd = pl.BlockSpec((tm, ds), lambda i: (i, 0))
    res = [hd] + [_residue_spec(tm, tpb, ds, dil, lambda i: 0) for dil in DILATIONS[1:]]
    res_shape = [jax.ShapeDtypeStruct((t, ds), F32)] + [_residue_shape(nb, seq, ds, dil, F32) for dil in DILATIONS[1:]]
    gain = pl.BlockSpec((1, ds), lambda i: (0, 0))
    outs = pl.pallas_call(
        body, name="mix_out_bwd", grid=(nt,),
        in_specs=[tok, tok, pl.BlockSpec((None, N_MOD, d), lambda i: (i // tpb, 0, 0)),
                  pl.BlockSpec(wout.shape, lambda i: (0, 0)), hd, hd, gain, gain],
        out_specs=[tok, hd] + res + res
        + [pl.BlockSpec((None, 8, d), lambda i: (i // tpb, 0, 0)), pl.BlockSpec((8, 2 * ds), lambda i: (0, 0))],
        out_shape=[jax.ShapeDtypeStruct((t, d), BF16), jax.ShapeDtypeStruct((t, ds), F32)] + res_shape + res_shape
        + [jax.ShapeDtypeStruct((nb, 8, d), F32), jax.ShapeDtypeStruct((8, 2 * ds), F32)],
        scratch_shapes=[_stage_shape(tm, ds)],
        compiler_params=_params("arbitrary"),
    )(dxo, m, mod, wout, osb, odil, gsb, gdil)
    flat = [a.reshape(t, ds) for a in outs[2:2 + 2 * n_cfg]]
    return outs[0], outs[1], flat[:n_cfg], flat[n_cfg:], outs[-2], outs[-1]


def _merge_dqkv(sb_parts, dil_parts, nb, tm):
    t, ds = sb_parts[0].shape
    nt = t // tm
    tpb = nt // nb
    seq = t // nb
    n_cfg = len(DILATIONS)

    def body(*refs):
        sb_refs, dil_refs = refs[:3], refs[3:3 + 3 * n_cfg]
        o_ref, sc = refs[3 + 3 * n_cfg:]
        for k in range(3):
            o_ref[:, k * ds:(k + 1) * ds] = sb_refs[k][...]
            total = dil_refs[k * n_cfg][...].astype(F32)
            for i, dil in enumerate(DILATIONS[1:]):
                total = total + _from_residue_rows(dil_refs[k * n_cfg + i + 1], sc, dil)
            o_ref[:, (3 + k) * ds:(4 + k) * ds] = total.astype(BF16)

    hd = pl.BlockSpec((tm, ds), lambda i: (i, 0))
    res = [hd] + [_residue_spec(tm, tpb, ds, dil, lambda i: 0) for dil in DILATIONS[1:]]
    views = [v for parts in dil_parts for v in _residue_views(parts, nb, seq)]
    return pl.pallas_call(
        body, name="merge_dqkv", grid=(nt,),
        in_specs=[hd] * 3 + res * 3,
        out_specs=pl.BlockSpec((tm, 6 * ds), lambda i: (i, 0)),
        out_shape=jax.ShapeDtypeStruct((t, 6 * ds), BF16),
        scratch_shapes=[_stage_shape(tm, ds)],
        compiler_params=_params("arbitrary"),
    )(*sb_parts, *views)


def _row_tile(rows):
    if rows <= 256:
        return rows
    for cand in range(256, 15, -16):
        if rows % cand == 0:
            return cand
    return rows


def _adamw(w, parts, m, v, name, transposed=False):
    rows, cols = w.shape
    n_parts = parts.shape[0]
    tr = _row_tile(rows)
    c1 = 1.0 / (1.0 - ADAM_B1 ** ADAM_STEP)
    c2 = 1.0 / (1.0 - ADAM_B2 ** ADAM_STEP)

    def body(w_ref, p_ref, m_ref, v_ref, g_ref, d_ref, nm_ref, nv_ref):
        g = p_ref[0].astype(F32)
        for i in range(1, n_parts):
            g = g + p_ref[i].astype(F32)
        wv, mv, vv = w_ref[...], m_ref[...], v_ref[...]
        if transposed:
            wv, mv, vv = wv.T, mv.T, vv.T
        nm = ADAM_B1 * mv + (1.0 - ADAM_B1) * g
        nv = ADAM_B2 * vv + (1.0 - ADAM_B2) * (g * g)
        g_ref[...] = g
        nm_ref[...] = nm
        nv_ref[...] = nv
        d_ref[...] = -ADAM_LR * ((nm * c1) / (jnp.sqrt(nv * c2) + ADAM_EPS) + ADAM_WD * wv)

    blk = pl.BlockSpec((tr, cols), lambda i: (i, 0))
    if transposed:
        oblk = pl.BlockSpec((cols, tr), lambda i: (0, i))
        pblk = pl.BlockSpec((n_parts, cols, tr), lambda i: (0, 0, i))
        out = jax.ShapeDtypeStruct((cols, rows), F32)
    else:
        oblk, pblk = blk, pl.BlockSpec((n_parts, tr, cols), lambda i: (0, i, 0))
        out = jax.ShapeDtypeStruct((rows, cols), F32)
    return pl.pallas_call(
        body, name=name, grid=(rows // tr,),
        in_specs=[blk, pblk, blk, blk],
        out_specs=[oblk, oblk, oblk, oblk], out_shape=[out, out, out, out],
        compiler_params=_params("arbitrary"),
    )(w, parts, m, v)


def _t5_bucket(n):
    max_exact = N_BUCKETS // 2
    nf = np.maximum(n, 1).astype(np.float32)
    large = max_exact + (np.log(nf / max_exact) / math.log(MAX_DISTANCE / max_exact)
                         * (N_BUCKETS - max_exact)).astype(np.int32)
    large = np.minimum(large, N_BUCKETS - 1)
    return np.where(n < max_exact, n, large).astype(np.int32)


def _bucket_onehot():
    table = np.zeros((len(DILATIONS), 2 * DIL_BLOCK + 1, N_BUCKETS), np.float32)
    for i, dil in enumerate(DILATIONS):
        buckets = _t5_bucket(np.arange(DIL_BLOCK + 1) * dil)
        for m in range(DIL_BLOCK + 1):
            table[i, m, buckets[DIL_BLOCK - m]] = 1.0
    return table


def _bias_blocks(rel_bias):
    row = jnp.einsum("cmn,nh->chm", _bucket_onehot(), rel_bias, precision=lax.Precision.HIGHEST)
    n_cfg, n_heads, width = row.shape
    tiled = jnp.tile(row, (1, 1, DIL_BLOCK))[..., :DIL_BLOCK * (width - 1)]
    return tiled.reshape(n_cfg, n_heads, DIL_BLOCK, width - 1)


def _bias_blocks_bwd(dblocks):
    n_cfg, n_heads = dblocks.shape[:2]
    width = 2 * DIL_BLOCK + 1
    flat = dblocks.reshape(n_cfg, n_heads, DIL_BLOCK * (width - 1))
    flat = jnp.pad(flat, ((0, 0), (0, 0), (0, DIL_BLOCK)))
    drow = jnp.sum(flat.reshape(n_cfg, n_heads, DIL_BLOCK, width), axis=2)
    return jnp.einsum("chm,cmn->nh", drow, _bucket_onehot(), precision=lax.Precision.HIGHEST)


def _pad_to(a, axis, size):
    pad = [(0, 0)] * a.ndim
    pad[axis] = (0, size - a.shape[axis])
    return jnp.pad(a, pad)


def _lane_pad(n):
    return -(-n // LANES) * LANES


def _local_step(x, target, mod, gains, weights, rel_bias, tm, distributed):
    nb, seq, d = x.shape
    t = nb * seq
    g_ffn1, g_mix, g_sb, g_dil, g_ffn2, g_final = gains
    wg1, wu1, wd1 = weights[:3]
    x0 = x.reshape(t, d)
    ds = g_sb.shape[1]
    bias = _bias_blocks(rel_bias)

    def beside(arrays, scatter):
        return _Exchange(arrays, scatter) if distributed else None

    tp, tg = min(PROJ_TILE, seq), min(GRAD_TILE, t)

    (x1, f1, gate1, up1), got = _ffn_fwd(x0, mod, g_ffn1, wg1, wu1, wd1, 0, tp, beside(weights[3:4], False))
    win = got[0] if distributed else weights[3]
    (qkv, qkvd, h2), got = _qkv_fwd(x1, mod, g_mix, win, tp, beside(weights[4:5], False))
    wout = got[0] if distributed else weights[4]
    wout2 = wout.reshape(-1, d)
    (osb, csb), got = _sb_fwd(qkv, nb, seq, beside(weights[5:7], False))
    wg2, wu2 = got if distributed else weights[5:7]
    n_cfg = len(DILATIONS)
    piece = -(-weights[7].shape[-2] // n_cfg // 16) * 16
    ocs, lses, wd2_pieces = [], [], []
    for i, dil in enumerate(DILATIONS):
        rows = weights[7][..., i * piece:(i + 1) * piece, :]
        (oc, lse), got = _dil_fwd(qkvd[i], bias[i], nb, seq, dil, beside([rows], False))
        wd2_pieces.append(got[0] if distributed else rows)
        ocs.append(oc)
        lses.append(lse)
    wd2 = jnp.concatenate(wd2_pieces, axis=-2)
    x2, on, mix, odil, ldil = _mix_out_fwd(osb, ocs, lses, g_sb, g_dil, wout2, x1, mod, tm)
    (dx3, f3, gate3, up3, head), _ = _ffn_fwd(x2, mod, g_ffn2, wg2, wu2, wd2, 2, tp,
                                              head=(target.reshape(t, d), g_final))
    loss_sum = 0.5 * jnp.sum(head[0]) / d
    dg_final = head[1:2]

    (dx2, dgate3, dup3, act3, h3, df3, dmod3, dg_ffn2), _ = _ffn_bwd(
        dx3, x2, f3, mod, g_ffn2, gate3, up3, wg2, wu2, wd2, 2, tm)
    gwg2, gwu2, gwd2 = _ffn_weight_grads(h3, dgate3, dup3, act3, df3, tg, 2)

    dm, dosb, dodil, dldil, dmod2b, dg_heads = _mix_out_bwd(
        dx2, mix, mod, wout2, osb, odil, g_sb, g_dil, tm)
    n_out = wout.shape[0]
    gwout = _mm_tn(on, dm,
                   pl.BlockSpec((tg, wout.shape[1]), lambda i, j: (i, j)),
                   pl.BlockSpec((tg, d), lambda i, j: (i, 0)),
                   wout.shape, t // tg, "grad_wout")

    (dq_sb, dk_sb, dv_sb), parts_late = _sb_bwd(qkv, dosb, csb, nb, seq,
                                                beside([gwout, gwg2, gwu2, gwd2], True))
    dil_grads = [_dil_bwd(qkvd[i], bias[i], dodil[i], ldil[i], dldil[i], nb, seq, dil)
                 for i, dil in enumerate(DILATIONS)]
    dqkv = _merge_dqkv([dq_sb, dk_sb, dv_sb], [[g[k] for g in dil_grads] for k in range(3)], nb, tm)
    drel = _bias_blocks_bwd(jnp.stack([g[3] for g in dil_grads]))

    cs = win.shape[2]
    gwin = _mm_tn(h2, dqkv,
                  pl.BlockSpec((tg, d), lambda i, j: (i, 0)),
                  pl.BlockSpec((tg, cs), lambda i, j: (i, j)),
                  win.shape, t // tg, "grad_win", pair_reduce=distributed)
    (dx1, dmod2a, dg_mix), parts_mid = _qkv_bwd(
        dqkv, dx2, x1, mod, g_mix, win, tp, _Exchange([gwin], True, chips=[True]) if distributed else None)

    (dx0, dgate1, dup1, act1, h1, df1, dmod1, dg_ffn1), _ = _ffn_bwd(
        dx1, x0, f1, mod, g_ffn1, gate1, up1, wg1, wu1, wd1, 0, tm)
    dmod = jnp.concatenate([dmod1[:, 0:3], dmod2a[:, 0:2], dmod2b[:, 2:3], dmod3[:, 0:3]], axis=1)
    ggrads = (dg_ffn1[0:1], dg_mix[0:1], dg_heads[0:1], drel, dg_ffn2[0:1], dg_final)
    if not distributed:
        gw1 = _ffn_weight_grads(h1, dgate1, dup1, act1, df1, tg, 0)
        return loss_sum, dx0.reshape(nb, seq, d), tuple(gw1) + (gwin, gwout, gwg2, gwu2, gwd2), dmod, ggrads

    dg_heads_row, drel_flat = dg_heads[0:1], drel.reshape(1, -1)
    width = max(d, dg_heads_row.shape[1], drel_flat.shape[1])
    small = jnp.concatenate(
        [_pad_to(a.reshape(1, -1), 1, width)
         for a in (dg_ffn1[0:1], dg_mix[0:1], dg_ffn2[0:1], dg_final, dg_heads_row, drel_flat, loss_sum)]
        + [jnp.zeros((1, width), F32)], axis=0)
    dmod_pad = _pad_to(dmod.reshape(nb, N_MOD * d), 0, 8)
    everyone = _Exchange([jnp.broadcast_to(dmod_pad, (N_DEV,) + dmod_pad.shape),
                          jnp.broadcast_to(small, (N_DEV,) + small.shape)], True)
    sent_g, sent_u, gwd1, (dmod_all, small_all) = _ffn_weight_grads(
        h1, dgate1, dup1, act1, df1, tg, 0, stream=True, first=everyone)
    wgrads = (sent_g, sent_u, gwd1) + tuple(parts_mid + parts_late)
    return dx0.reshape(nb, seq, d), wgrads, dmod_all, small_all


def kernel(x, c, w_ada, b_ada, g_ffn1, w1_gate, w1_up, w1_down, g_mix, w_in, g_sb_out, g_dil_out, w_out, rel_bias, g_ffn2, w2_gate, w2_up, w2_down, g_final, loss_target, m_w_ada, m_b_ada, m_g_ffn1, m_w1_gate, m_w1_up, m_w1_down, m_g_mix, m_w_in, m_g_sb_out, m_g_dil_out, m_w_out, m_rel_bias, m_g_ffn2, m_w2_gate, m_w2_up, m_w2_down, m_g_final, v_w_ada, v_b_ada, v_g_ffn1, v_w1_gate, v_w1_up, v_w1_down, v_g_mix, v_w_in, v_g_sb_out, v_g_dil_out, v_w_out, v_rel_bias, v_g_ffn2, v_w2_gate, v_w2_up, v_w2_down, v_g_final):
    nb, seq, d = x.shape
    me = 4 * lax.axis_index("x") + 2 * lax.axis_index("y") + lax.axis_index("c")
    tm = min(TOKEN_TILE, seq)
    fs = w1_gate.shape[2]
    fs_pad = _lane_pad(fs)
    ada_cols = w_ada.shape[2]

    def col_shard(w):
        return _pad_to(w[0].astype(BF16), 1, fs_pad)

    def row_shard(w):
        return _pad_to(w[0].astype(BF16), 0, fs_pad)

    shards = [col_shard(w1_gate), col_shard(w1_up), row_shard(w1_down), w_in[0].astype(BF16),
              w_out[0].astype(BF16), col_shard(w2_gate), col_shard(w2_up), row_shard(w2_down)]
    b_cols = lax.dynamic_slice(b_ada, (0, me * ada_cols), (1, ada_cols))
    c_every, mod_all, first = _first_exchange(_pad_to(c, 0, 8), shards[:3], w_ada[0], b_cols)
    c_all = c_every[:, :nb].reshape(N_DEV * nb, d)
    weights = first + shards[3:]
    mod = lax.dynamic_slice(mod_all, (0, me * 8, 0), (N_DEV, nb, ada_cols))
    mod = mod.transpose(1, 0, 2).reshape(nb, N_MOD, d)

    n_sb = g_sb_out.shape[1] * g_sb_out.shape[2]
    gains = (g_ffn1, g_mix, g_sb_out.reshape(1, n_sb), g_dil_out.reshape(1, -1), g_ffn2,
             g_final.reshape(1, d))
    grad_x, parts, dmod_all, small_all = _local_step(x, loss_target, mod, gains, weights, rel_bias, tm, True)

    last_part = _exchange([parts[2]], True, "scatter_last", chips=[True])[0]
    parts = parts[:2] + (last_part,) + parts[3:]
    dmod_all = dmod_all[:, :nb].reshape(N_DEV * nb, N_MOD * d)
    dmod_cols = lax.dynamic_slice(dmod_all, (0, me * ada_cols), (N_DEV * nb, ada_cols))
    gw_ada, gb_ada = _ada_bwd(c_all, dmod_cols, dmod_all)

    def small_part(row, size, shape):
        return small_all[:, row, :size].reshape((N_DEV,) + shape)

    loss = jnp.sum(small_all[:, 6, 0])

    n_rel = rel_bias.shape
    updates = {
        "w_ada": (w_ada[0], gw_ada[None], m_w_ada[0], v_w_ada[0]),
        "b_ada": (b_ada, gb_ada[None], m_b_ada, v_b_ada),
        "g_ffn1": (g_ffn1, small_part(0, d, (1, d)), m_g_ffn1, v_g_ffn1),
        "w1_gate": (w1_gate[0], parts[0], m_w1_gate[0], v_w1_gate[0]),
        "w1_up": (w1_up[0], parts[1], m_w1_up[0], v_w1_up[0]),
        "w1_down": (w1_down[0], parts[2], m_w1_down[0], v_w1_down[0]),
        "g_mix": (g_mix, small_part(1, d, (1, d)), m_g_mix, v_g_mix),
        "w_in": (w_in[0], parts[3], m_w_in[0], v_w_in[0]),
        "g_sb_out": (g_sb_out[0], small_all[:, 4, :n_sb].reshape((N_DEV,) + g_sb_out.shape[1:]),
                     m_g_sb_out[0], v_g_sb_out[0]),
        "g_dil_out": (g_dil_out[0], small_all[:, 4, n_sb:n_sb + g_dil_out[0].size].reshape((N_DEV,) + g_dil_out.shape[1:]),
                      m_g_dil_out[0], v_g_dil_out[0]),
        "w_out": (w_out[0], parts[4], m_w_out[0], v_w_out[0]),
        "rel_bias": (rel_bias, small_part(5, rel_bias.size, n_rel), m_rel_bias, v_rel_bias),
        "g_ffn2": (g_ffn2, small_part(2, d, (1, d)), m_g_ffn2, v_g_ffn2),
        "w2_gate": (w2_gate[0], parts[5], m_w2_gate[0], v_w2_gate[0]),
        "w2_up": (w2_up[0], parts[6], m_w2_up[0], v_w2_up[0]),
        "w2_down": (w2_down[0], parts[7], m_w2_down[0], v_w2_down[0]),
        "g_final": (g_final.reshape(1, d), small_part(3, d, (1, d)), m_g_final.reshape(1, d), v_g_final.reshape(1, d)),
    }
    shapes = {"w_ada": w_ada.shape, "b_ada": b_ada.shape, "g_ffn1": g_ffn1.shape, "w1_gate": w1_gate.shape,
              "w1_up": w1_up.shape, "w1_down": w1_down.shape, "g_mix": g_mix.shape, "w_in": w_in.shape,
              "g_sb_out": g_sb_out.shape, "g_dil_out": g_dil_out.shape, "w_out": w_out.shape,
              "rel_bias": rel_bias.shape, "g_ffn2": g_ffn2.shape, "w2_gate": w2_gate.shape,
              "w2_up": w2_up.shape, "w2_down": w2_down.shape, "g_final": g_final.shape}
    grads, deltas, new_m, new_v = [], [], [], []
    for name, (w, p, m, v) in updates.items():
        transposed = name in ("w1_gate", "w1_up", "w2_gate", "w2_up")
        outs = _adamw(w, p, m, v, f"adamw_{name}", transposed)
        for dst, a in zip((grads, deltas, new_m, new_v), outs):
            dst.append((a.T if transposed else a).reshape(shapes[name]))
    return (loss, grad_x, *grads, *deltas, *new_m, *new_v)
```

```python
import functools
import math

import numpy as np
import jax
import jax.numpy as jnp
from jax import lax
from jax.experimental import pallas as pl
from jax.experimental.pallas import tpu as pltpu

F32 = jnp.float32
BF16 = jnp.bfloat16

EPS = 1e-6
NEG_INF = -1e30
HEAD_DIM = 64
LANES = 128
DIL_BLOCK = 128
DILATIONS = (1, 4, 16)
N_BUCKETS = 32
MAX_DISTANCE = 2048
N_MOD = 9
N_DEV = 8
SB_BLOCK = 256
SB_HEADS = 4
SB_WIDTH = SB_HEADS * HEAD_DIM
DIL_HEADS = 4
DIL_WIDTH = DIL_HEADS * HEAD_DIM
TOKEN_TILE = 512
PROJ_TILE = 1024
GRAD_TILE = 1024
SHARD_GROUP = 2
FFN_CHUNKS = 2
VMEM_LIMIT_BYTES = 56 * 1024 * 1024

ADAM_LR = 0.001
ADAM_B1 = 0.9
ADAM_B2 = 0.999
ADAM_EPS = 1e-08
ADAM_WD = 0.01
ADAM_STEP = 10

NT_DIMS = (((1,), (1,)), ((), ()))
TN_DIMS = (((0,), (0,)), ((), ()))


def _params(*sem):
    return pltpu.CompilerParams(dimension_semantics=sem, vmem_limit_bytes=VMEM_LIMIT_BYTES)


def _once(spec):
    return pl.BlockSpec(spec.block_shape, spec.index_map, pipeline_mode=pl.Buffered(1))


def _dot(a, b):
    return jnp.dot(a, b, preferred_element_type=F32)


def _dot_nt(a, b):
    return lax.dot_general(a, b, NT_DIMS, preferred_element_type=F32)


def _dot_tn(a, b):
    return lax.dot_general(a, b, TN_DIMS, preferred_element_type=F32)


def _split_dot(a, b):
    hi = a.astype(BF16)
    lo = (a - hi.astype(F32)).astype(BF16)
    return _dot(hi, b) + _dot(lo, b)


def _sigmoid(z):
    return 1.0 / (1.0 + jnp.exp(-z))


def _norm(x):
    r = lax.rsqrt(jnp.mean(x * x, axis=-1, keepdims=True) + EPS)
    return x * r, r


def _modulate(x, g, mod_ref, k):
    n, _ = _norm(x)
    shift = mod_ref[3 * k:3 * k + 1, :]
    scale = mod_ref[3 * k + 1:3 * k + 2, :]
    return n * g * (1.0 + scale) + shift


def _modulate_bwd(dh, x, g, mod_ref, k):
    n, r = _norm(x)
    scale = mod_ref[3 * k + 1:3 * k + 2, :]
    dshift = jnp.sum(dh, axis=0, keepdims=True)
    dscale = jnp.sum(dh * n * g, axis=0, keepdims=True)
    dg = jnp.sum(dh * n * (1.0 + scale), axis=0, keepdims=True)
    dn = dh * g * (1.0 + scale)
    dx = r * (dn - n * jnp.mean(dn * n, axis=-1, keepdims=True))
    return dx, dshift, dscale, dg


class _Exchange:
    def __init__(self, arrays, scatter, relay=False, chips=None):
        assert not (scatter and relay)
        self.arrays = list(arrays)
        self.scatter = scatter
        self.relay = relay
        self.n = len(self.arrays)
        self.chips = list(chips) if chips is not None else [False] * self.n
        assert scatter or not any(self.chips)
        self.out_shape = [
            jax.ShapeDtypeStruct((N_DEV // 2 if ch else N_DEV,) + tuple(a.shape[1:] if scatter else a.shape), a.dtype)
            for a, ch in zip(self.arrays, self.chips)]
        n_remote = self.n * (N_DEV - 1)
        self.scratch_shapes = [pltpu.SemaphoreType.DMA((n_remote,)), pltpu.SemaphoreType.DMA((n_remote,)),
                               pltpu.SemaphoreType.DMA((self.n,))]

    def _copies(self, in_refs, out_refs, sems):
        send_sems, recv_sems, local_sems = sems
        x, y, c = lax.axis_index("x"), lax.axis_index("y"), lax.axis_index("c")
        me = 4 * x + 2 * y + c
        local, remote, relayed = [], {}, {}
        for a in range(self.n):
            if self.chips[a]:
                mine = 2 * x + y
                local.append(pltpu.make_async_copy(in_refs[a].at[mine], out_refs[a].at[mine], local_sems.at[a]))
                for k in (2, 4, 6):
                    px = 1 - x if k & 4 else x
                    py = 1 - y if k & 2 else y
                    sem = a * (N_DEV - 1) + k - 1
                    remote[a, k] = pltpu.make_async_remote_copy(
                        src_ref=in_refs[a].at[2 * px + py], dst_ref=out_refs[a].at[mine],
                        send_sem=send_sems.at[sem], recv_sem=recv_sems.at[sem],
                        device_id=(px, py, c), device_id_type=pl.DeviceIdType.MESH)
                continue
            src = in_refs[a].at[me] if self.scatter else in_refs[a]
            local.append(pltpu.make_async_copy(src, out_refs[a].at[me], local_sems.at[a]))
            for k in range(1, N_DEV):
                px = 1 - x if k & 4 else x
                py = 1 - y if k & 2 else y
                pc = 1 - c if k & 1 else c
                sem = a * (N_DEV - 1) + k - 1
                if self.relay and k & 1 and k > 1:
                    slot = 4 * px + 2 * py + c
                    relayed[a, k] = pltpu.make_async_remote_copy(
                        src_ref=out_refs[a].at[slot], dst_ref=out_refs[a].at[slot],
                        send_sem=send_sems.at[sem], recv_sem=recv_sems.at[sem],
                        device_id=(x, y, 1 - c), device_id_type=pl.DeviceIdType.MESH)
                    continue
                src = in_refs[a].at[4 * px + 2 * py + pc] if self.scatter else in_refs[a]
                remote[a, k] = pltpu.make_async_remote_copy(
                    src_ref=src, dst_ref=out_refs[a].at[me],
                    send_sem=send_sems.at[sem], recv_sem=recv_sems.at[sem],
                    device_id=(px, py, pc), device_id_type=pl.DeviceIdType.MESH)
        return local, remote, relayed

    def start(self, in_refs, out_refs, sems):
        local, remote, _ = self._copies(in_refs, out_refs, sems)
        for cp in local + list(remote.values()):
            cp.start()

    def wait(self, in_refs, out_refs, sems):
        local, remote, relayed = self._copies(in_refs, out_refs, sems)
        for (a, k), cp in relayed.items():
            remote[a, k - 1].wait_recv()
            cp.start()
        for (a, k), cp in remote.items():
            if (a, k + 1) not in relayed:
                cp.wait_recv()
        for cp in relayed.values():
            cp.wait_recv()
        for cp in list(remote.values()) + list(relayed.values()):
            cp.wait_send()
        for cp in local:
            cp.wait()


def _call(body, *, name, args, in_specs, out_specs, out_shape, scratch_shapes=(), grid=(),
          params=None, exchange=None):
    n_in, n_out = len(args), len(out_shape)
    if exchange is None:
        outs = pl.pallas_call(
            body, name=name, grid=grid, in_specs=list(in_specs), out_specs=list(out_specs),
            out_shape=list(out_shape), scratch_shapes=list(scratch_shapes), compiler_params=params,
        )(*args)
        return list(outs), []
    n_ex = exchange.n

    def wrapped(*refs):
        ins, refs = refs[:n_in], refs[n_in:]
        ex_in, refs = refs[:n_ex], refs[n_ex:]
        outs, refs = refs[:n_out], refs[n_out:]
        ex_out, refs = refs[:n_ex], refs[n_ex:]
        scratch, sems = refs[:len(refs) - 3], refs[len(refs) - 3:]
        if not grid:
            exchange.start(ex_in, ex_out, sems)
            body(*ins, *outs, *scratch)
            exchange.wait(ex_in, ex_out, sems)
            return
        first = functools.reduce(jnp.logical_and, [pl.program_id(a) == 0 for a in range(len(grid))])
        last = functools.reduce(jnp.logical_and, [pl.program_id(a) == grid[a] - 1 for a in range(len(grid))])

        @pl.when(first)
        def _():
            exchange.start(ex_in, ex_out, sems)

        body(*ins, *outs, *scratch)

        @pl.when(last)
        def _():
            exchange.wait(ex_in, ex_out, sems)

    any_spec = pl.BlockSpec(memory_space=pl.ANY)
    outs = pl.pallas_call(
        wrapped, name=name, grid=grid,
        in_specs=list(in_specs) + [any_spec] * n_ex, out_specs=list(out_specs) + [any_spec] * n_ex,
        out_shape=list(out_shape) + exchange.out_shape,
        scratch_shapes=list(scratch_shapes) + exchange.scratch_shapes, compiler_params=params,
    )(*args, *exchange.arrays)
    return list(outs[:n_out]), list(outs[n_out:])


def _exchange(arrays, scatter, name, relay=False, chips=None):
    return _call(lambda: None, name=name, args=(), in_specs=(), out_specs=(), out_shape=(),
                 exchange=_Exchange(arrays, scatter, relay, chips))[1]


def _first_exchange(c_pad, shards, w, b):
    rows, d = c_pad.shape
    cols = w.shape[1]
    ex_c = _Exchange([c_pad], False)
    ex_w = _Exchange(shards, False, relay=True)
    ex_m = _Exchange([jax.ShapeDtypeStruct((N_DEV * rows, cols), F32)], False)
    n_w = ex_w.n

    def body(*refs):
        c_ref, w_refs, wa_ref, b_ref = refs[0], refs[1:1 + n_w], refs[1 + n_w], refs[2 + n_w]
        outs = refs[3 + n_w:]
        cg_ref, wg_refs, mg_ref = outs[0], outs[1:1 + n_w], outs[1 + n_w]
        scratch = outs[2 + n_w:]
        sems_c, sems_w, sems_m, c_vm, m_vm = scratch[0:3], scratch[3:6], scratch[6:9], scratch[9], scratch[10]
        ex_c.start([c_ref], [cg_ref], sems_c)
        ex_c.wait([c_ref], [cg_ref], sems_c)
        pltpu.sync_copy(cg_ref, c_vm)
        cv = c_vm[...].reshape(N_DEV * rows, d)
        s = (cv * _sigmoid(cv)).astype(BF16)
        m_vm[...] = _dot(s, wa_ref[...].astype(BF16)) + b_ref[...]
        ex_m.start([m_vm], [mg_ref], sems_m)
        ex_w.start(w_refs, wg_refs, sems_w)
        ex_m.wait([m_vm], [mg_ref], sems_m)
        ex_w.wait(w_refs, wg_refs, sems_w)

    any_spec = pl.BlockSpec(memory_space=pl.ANY)
    vmem_spec = pl.BlockSpec(memory_space=pltpu.VMEM)
    outs = pl.pallas_call(
        body, name="first_exchange",
        in_specs=[any_spec] * (1 + n_w) + [vmem_spec, vmem_spec],
        out_specs=[any_spec] * (2 + n_w),
        out_shape=ex_c.out_shape + ex_w.out_shape + ex_m.out_shape,
        scratch_shapes=ex_c.scratch_shapes + ex_w.scratch_shapes + ex_m.scratch_shapes
        + [pltpu.VMEM((N_DEV, rows, d), F32), pltpu.VMEM((N_DEV * rows, cols), F32)],
        compiler_params=pltpu.CompilerParams(vmem_limit_bytes=VMEM_LIMIT_BYTES),
    )(c_pad, *shards, w, b)
    return outs[0], outs[1 + n_w], list(outs[1:1 + n_w])


def _ada_bwd(c_all, dmod_cols, dmod_all):
    def body(c_ref, dc_ref, da_ref, gw_ref, gb_ref):
        cv = c_ref[...]
        s = cv * _sigmoid(cv)
        gw_ref[...] = lax.dot_general(s, dc_ref[...], TN_DIMS, preferred_element_type=F32,
                                      precision=lax.Precision.HIGHEST)
        gb_ref[...] = jnp.sum(da_ref[...], axis=0, keepdims=True)

    return pl.pallas_call(
        body, name="ada_bwd",
        out_shape=(jax.ShapeDtypeStruct((c_all.shape[1], dmod_cols.shape[1]), F32),
                   jax.ShapeDtypeStruct((1, dmod_all.shape[1]), F32)),
        compiler_params=pltpu.CompilerParams(vmem_limit_bytes=VMEM_LIMIT_BYTES),
    )(c_all, dmod_cols, dmod_all)


def _side_by_side(w_ref):
    return jnp.concatenate([w_ref[s] for s in range(w_ref.shape[0])], axis=1)


def _stacked(w_ref):
    return jnp.concatenate([w_ref[s] for s in range(w_ref.shape[0])], axis=0)


def _loss_tile(x, target, g, acc_ref):
    d = x.shape[1]
    n, r = _norm(x)
    err = n * g - target
    dy = err * (1.0 / d)
    acc_ref[0:1, :] += jnp.sum(err * err, axis=0, keepdims=True)
    acc_ref[1:2, :] += jnp.sum(dy * n, axis=0, keepdims=True)
    dn = dy * g
    return r * (dn - n * jnp.mean(dn * n, axis=-1, keepdims=True))


def _ffn_fwd(x, mod, g, wg, wu, wd, k, tm, exchange=None, head=None):
    t, d = x.shape
    ns, _, fs = wg.shape
    nt = t // tm
    tpb = nt // mod.shape[0]
    rows = tm // FFN_CHUNKS
    extra = list(head) if head is not None else []

    def body(x_ref, mod_ref, g_ref, wg_ref, wu_ref, wd_ref, *rest):
        if head is not None:
            t_ref, gf_ref, xo_ref, f_ref, gg_ref, uu_ref, head_ref, h_sc, acc = rest
        else:
            xo_ref, f_ref, gg_ref, uu_ref, h_sc, acc = rest
        i, j = pl.program_id(0), pl.program_id(1)

        @pl.when(j == 0)
        def _():
            h_sc[...] = _modulate(x_ref[...], g_ref[...], mod_ref, k).astype(BF16)
            acc[...] = jnp.zeros_like(acc)

        chunks = [pl.ds(c * rows, rows) for c in range(FFN_CHUNKS)]
        wg, wu, wd = _side_by_side(wg_ref), _side_by_side(wu_ref), _stacked(wd_ref)
        gates, ups = [], []
        for rs in chunks:
            h = h_sc[rs, :]
            gates.append(_dot(h, wg))
            ups.append(_dot(h, wu))
        acts = [(g * _sigmoid(g) * u).astype(BF16) for g, u in zip(gates, ups)]
        for rs, g, u in zip(chunks, gates, ups):
            for s in range(SHARD_GROUP):
                gg_ref[s, rs, :] = g[:, s * fs:(s + 1) * fs].astype(BF16)
                uu_ref[s, rs, :] = u[:, s * fs:(s + 1) * fs].astype(BF16)
        downs = [_dot(a, wd) for a in acts]
        for rs, dn in zip(chunks, downs):
            acc[rs, :] += dn

        @pl.when(j == ns // SHARD_GROUP - 1)
        def _():
            f = acc[...]
            f_ref[...] = f.astype(BF16)
            xo = x_ref[...] + 0.5 * mod_ref[3 * k + 2:3 * k + 3, :] * f
            if head is None:
                xo_ref[...] = xo
            else:
                @pl.when(i == 0)
                def _():
                    head_ref[...] = jnp.zeros_like(head_ref)

                xo_ref[...] = _loss_tile(xo, t_ref[...], gf_ref[...], head_ref)

    tok = pl.BlockSpec((tm, d), lambda i, j: (i, 0))
    row = pl.BlockSpec((1, d), lambda i, j: (0, 0))
    hid = pl.BlockSpec((SHARD_GROUP, tm, fs), lambda i, j: (j, i, 0))
    head_specs = [_once(tok), row] if head is not None else []
    head_out = [pl.BlockSpec((8, d), lambda i, j: (0, 0))] if head is not None else []
    head_shape = [jax.ShapeDtypeStruct((8, d), F32)] if head is not None else []
    return _call(
        body, name=f"ffn_fwd{k}", grid=(nt, ns // SHARD_GROUP), args=(x, mod, g, wg, wu, wd, *extra),
        in_specs=[tok,
                  pl.BlockSpec((None, N_MOD, d), lambda i, j: (i // tpb, 0, 0)),
                  row,
                  pl.BlockSpec((SHARD_GROUP, d, fs), lambda i, j: (j, 0, 0)),
                  pl.BlockSpec((SHARD_GROUP, d, fs), lambda i, j: (j, 0, 0)),
                  pl.BlockSpec((SHARD_GROUP, fs, d), lambda i, j: (j, 0, 0))] + head_specs,
        out_specs=[tok, tok, hid, hid] + head_out,
        out_shape=[jax.ShapeDtypeStruct((t, d), F32), jax.ShapeDtypeStruct((t, d), BF16),
                   jax.ShapeDtypeStruct((ns, t, fs), BF16), jax.ShapeDtypeStruct((ns, t, fs), BF16)]
        + head_shape,
        scratch_shapes=[pltpu.VMEM((tm, d), BF16), pltpu.VMEM((tm, d), F32)],
        params=_params("arbitrary", "arbitrary"), exchange=exchange)


def _ffn_bwd(dxo, x, f, mod, g, gate, up, wg, wu, wd, k, tm, exchange=None):
    t, d = x.shape
    ns, _, fs = wg.shape
    nt = t // tm
    nb = mod.shape[0]
    tpb = nt // nb
    rows = tm // FFN_CHUNKS

    def body(dxo_ref, x_ref, f_ref, mod_ref, g_ref, gg_ref, uu_ref, wg_ref, wu_ref, wd_ref,
             dx_ref, dgg_ref, duu_ref, act_ref, h_ref, df_ref, dmod_ref, dg_ref, acc):
        i, j = pl.program_id(0), pl.program_id(1)

        @pl.when(j == 0)
        def _():
            df = 0.5 * mod_ref[3 * k + 2:3 * k + 3, :] * dxo_ref[...]
            df_ref[...] = df.astype(BF16)
            h_ref[...] = _modulate(x_ref[...], g_ref[...], mod_ref, k).astype(BF16)
            acc[...] = jnp.zeros_like(acc)

        chunks = [pl.ds(c * rows, rows) for c in range(FFN_CHUNKS)]
        group = range(SHARD_GROUP)
        wg, wu, wd = _side_by_side(wg_ref), _side_by_side(wu_ref), _stacked(wd_ref)
        dacts = [_dot_nt(df_ref[rs, :], wd) for rs in chunks]
        dgates, dups = [], []
        for rs, dact in zip(chunks, dacts):
            gv = jnp.concatenate([gg_ref[s, rs, :] for s in group], axis=1).astype(F32)
            uv = jnp.concatenate([uu_ref[s, rs, :] for s in group], axis=1).astype(F32)
            sig = _sigmoid(gv)
            s_act = gv * sig
            act = (s_act * uv).astype(BF16)
            for s in group:
                act_ref[s, rs, :] = act[:, s * fs:(s + 1) * fs]
            dups.append((dact * s_act).astype(BF16))
            dgates.append((dact * uv * (sig * (1.0 + gv * (1.0 - sig)))).astype(BF16))
        dhs = [_dot_nt(dg, wg) + _dot_nt(du, wu) for dg, du in zip(dgates, dups)]
        for rs, dg, du, dh in zip(chunks, dgates, dups, dhs):
            for s in group:
                dgg_ref[s, rs, :] = dg[:, s * fs:(s + 1) * fs]
                duu_ref[s, rs, :] = du[:, s * fs:(s + 1) * fs]
            acc[rs, :] += dh

        @pl.when(j == ns // SHARD_GROUP - 1)
        def _():
            dx, dshift, dscale, dg = _modulate_bwd(acc[...], x_ref[...], g_ref[...], mod_ref, k)
            dxo_v = dxo_ref[...]
            dx_ref[...] = dxo_v + dx
            dgt = jnp.sum(0.5 * f_ref[...].astype(F32) * dxo_v, axis=0, keepdims=True)

            @pl.when(i % tpb == 0)
            def _():
                dmod_ref[...] = jnp.zeros_like(dmod_ref)

            @pl.when(i == 0)
            def _():
                dg_ref[...] = jnp.zeros_like(dg_ref)

            dmod_ref[0:1, :] += dshift
            dmod_ref[1:2, :] += dscale
            dmod_ref[2:3, :] += dgt
            dg_ref[0:1, :] += dg

    tok = pl.BlockSpec((tm, d), lambda i, j: (i, 0))
    hid = pl.BlockSpec((SHARD_GROUP, tm, fs), lambda i, j: (j, i, 0))
    return _call(
        body, name=f"ffn_bwd{k}", grid=(nt, ns // SHARD_GROUP), args=(dxo, x, f, mod, g, gate, up, wg, wu, wd),
        in_specs=[tok, tok, tok,
                  pl.BlockSpec((None, N_MOD, d), lambda i, j: (i // tpb, 0, 0)),
                  pl.BlockSpec((1, d), lambda i, j: (0, 0)),
                  hid, hid,
                  pl.BlockSpec((SHARD_GROUP, d, fs), lambda i, j: (j, 0, 0)),
                  pl.BlockSpec((SHARD_GROUP, d, fs), lambda i, j: (j, 0, 0)),
                  pl.BlockSpec((SHARD_GROUP, fs, d), lambda i, j: (j, 0, 0))],
        out_specs=[tok, hid, hid, hid, tok, tok,
                   pl.BlockSpec((None, 8, d), lambda i, j: (i // tpb, 0, 0)),
                   pl.BlockSpec((8, d), lambda i, j: (0, 0))],
        out_shape=[jax.ShapeDtypeStruct((t, d), F32),
                   jax.ShapeDtypeStruct((ns, t, fs), BF16), jax.ShapeDtypeStruct((ns, t, fs), BF16),
                   jax.ShapeDtypeStruct((ns, t, fs), BF16),
                   jax.ShapeDtypeStruct((t, d), BF16), jax.ShapeDtypeStruct((t, d), BF16),
                   jax.ShapeDtypeStruct((nb, 8, d), F32), jax.ShapeDtypeStruct((8, d), F32)],
        scratch_shapes=[pltpu.VMEM((tm, d), F32)],
        params=_params("arbitrary", "arbitrary"), exchange=exchange)


def _mm_tn(a, b, a_spec, b_spec, out_shape, n_tiles, name, exchange=None, keep_transposed=False,
           pair_reduce=False):
    n_out = out_shape[0]
    block = tuple(out_shape[1:])
    last = n_tiles - 1
    flip = block[0] > block[1]
    if flip:
        block = block[::-1]
    if flip and keep_transposed:
        flip_back, out_shape = False, (n_out,) + block
    else:
        flip_back = flip
    full_shape = tuple(out_shape)
    n_pairs = n_out // 2
    if pair_reduce:
        out_shape = (n_pairs,) + full_shape[1:]

    def body(a_ref, b_ref, o_ref, acc, *pair):
        i, j = pl.program_id(0), pl.program_id(1)
        prod = _dot_tn(b_ref[...], a_ref[...]) if flip else _dot_tn(a_ref[...], b_ref[...])
        full_ref = pair[0] if pair_reduce else o_ref

        @pl.when(i == 0)
        def _():
            acc[j] = prod

        @pl.when(i > 0)
        def _():
            acc[j] += prod

        @pl.when(i == last)
        def _():
            total = acc[j]
            full_ref[j] = (total.T if flip_back else total).astype(BF16)

        if pair_reduce:
            _, landed, send_sems, recv_sems = pair

            @pl.when(jnp.logical_and(i == last, j == n_out - 1))
            def _():
                x, y, c = lax.axis_index("x"), lax.axis_index("y"), lax.axis_index("c")
                copies = [pltpu.make_async_remote_copy(
                    src_ref=full_ref.at[2 * q + 1 - c], dst_ref=landed.at[q],
                    send_sem=send_sems.at[q], recv_sem=recv_sems.at[q],
                    device_id=(x, y, 1 - c), device_id_type=pl.DeviceIdType.MESH) for q in range(n_pairs)]
                for cp in copies:
                    cp.start()
                for q, cp in enumerate(copies):
                    cp.wait_recv()
                    o_ref[q] = (full_ref[2 * q + c].astype(F32) + landed[q].astype(F32)).astype(BF16)
                for cp in copies:
                    cp.wait_send()

    scratch = [pltpu.VMEM((n_out,) + block, F32)]
    if pair_reduce:
        scratch += [pltpu.VMEM(full_shape, BF16), pltpu.VMEM(out_shape, BF16),
                    pltpu.SemaphoreType.DMA((n_pairs,)), pltpu.SemaphoreType.DMA((n_pairs,))]
    outs, sent = _call(
        body, name=name, grid=(n_tiles, n_out), args=(a, b), in_specs=[a_spec, b_spec],
        out_specs=[pl.BlockSpec(out_shape, lambda i, j: (0,) * len(out_shape))],
        out_shape=[jax.ShapeDtypeStruct(out_shape, BF16)],
        scratch_shapes=scratch,
        params=_params("arbitrary", "arbitrary"), exchange=exchange)
    return (outs[0], sent) if exchange is not None else outs[0]


def _ffn_weight_grads(h, dgate, dup, act, df, tm, tag, stream=False, first=None):
    t, d = h.shape
    ns, _, fs = dgate.shape
    nt = t // tm
    tok = pl.BlockSpec((tm, d), lambda i, j: (i, 0))
    hid = pl.BlockSpec((None, tm, fs), lambda i, j: (j, i, 0))
    if not stream:
        gwg = _mm_tn(h, dgate, tok, hid, (ns, d, fs), nt, f"grad_wg{tag}", keep_transposed=True)
        gwu = _mm_tn(h, dup, tok, hid, (ns, d, fs), nt, f"grad_wu{tag}", keep_transposed=True)
        gwd = _mm_tn(act, df, hid, tok, (ns, fs, d), nt, f"grad_wd{tag}")
        return gwg, gwu, gwd
    gwg, brought = _mm_tn(h, dgate, tok, hid, (ns, d, fs), nt, f"grad_wg{tag}", first,
                          keep_transposed=True, pair_reduce=True)
    gwu, sent_g = _mm_tn(h, dup, tok, hid, (ns, d, fs), nt, f"grad_wu{tag}",
                         _Exchange([gwg], True, chips=[True]), keep_transposed=True, pair_reduce=True)
    gwd, sent_u = _mm_tn(act, df, hid, tok, (ns, fs, d), nt, f"grad_wd{tag}",
                         _Exchange([gwu], True, chips=[True]), pair_reduce=True)
    return sent_g[0], sent_u[0], gwd, brought


def _stage_shape(rows, cols):
    return pltpu.VMEM((cols // LANES, rows, LANES), F32)


def _stage(value, stage_ref):
    for k in range(stage_ref.shape[0]):
        stage_ref[k] = value[:, k * LANES:(k + 1) * LANES]


def _to_residue_rows(stage_ref, dst_ref, dil):
    rows = stage_ref.shape[1] // dil
    for r in range(dil):
        for k in range(stage_ref.shape[0]):
            dst_ref[r, :, k * LANES:(k + 1) * LANES] = (
                stage_ref.at[k][pl.ds(r, rows, stride=dil), :].astype(dst_ref.dtype))


def _from_residue_rows(src_ref, stage_ref, dil):
    rows = stage_ref.shape[1] // dil
    chunks = range(stage_ref.shape[0])
    for r in range(dil):
        for k in chunks:
            stage_ref.at[k][pl.ds(r, rows, stride=dil), :] = src_ref[r, :, k * LANES:(k + 1) * LANES].astype(F32)
    return jnp.concatenate([stage_ref[k] for k in chunks], axis=1)


def _residue_shape(nb, seq, width, dil, dtype):
    return jax.ShapeDtypeStruct((nb, dil, seq // dil, width), dtype)


def _residue_spec(tm, tpb, cols, dil, col_block):
    return pl.BlockSpec((None, dil, tm // dil, cols),
                        lambda i, *rest: (i // tpb, 0, i % tpb, col_block(i, *rest)))


def _qkv_fwd(x, mod, g, win, tm, exchange=None):
    t, d = x.shape
    ns, _, cs = win.shape
    nt = t // tm
    nb = mod.shape[0]
    tpb = nt // nb
    seq = t // nb
    width = ns * cs // 2
    cs, ns = cs * SHARD_GROUP, ns // SHARD_GROUP
    half = ns // 2
    n_res = len(DILATIONS) - 1

    def body(x_ref, mod_ref, g_ref, w_ref, sb_ref, dil_ref, *rest):
        res_refs, h_ref, sc = rest[:n_res], rest[n_res], rest[n_res + 1]
        j = pl.program_id(1)

        @pl.when(j == 0)
        def _():
            h_ref[...] = _modulate(x_ref[...], g_ref[...], mod_ref, 1).astype(BF16)

        res = _dot(h_ref[...], _side_by_side(w_ref))

        @pl.when(j < half)
        def _():
            sb_ref[...] = res.astype(BF16)

        @pl.when(j >= half)
        def _():
            dil_ref[...] = res.astype(BF16)
            _stage(res, sc)
            for ref, dil in zip(res_refs, DILATIONS[1:]):
                _to_residue_rows(sc, ref, dil)

    def dil_col(i, j):
        return jnp.maximum(j - half, 0)

    tok = pl.BlockSpec((tm, d), lambda i, j: (i, 0))
    wide = jax.ShapeDtypeStruct((t, width), BF16)
    outs, got = _call(
        body, name="qkv_fwd", grid=(nt, ns), args=(x, mod, g, win),
        in_specs=[tok,
                  pl.BlockSpec((None, N_MOD, d), lambda i, j: (i // tpb, 0, 0)),
                  pl.BlockSpec((1, d), lambda i, j: (0, 0)),
                  pl.BlockSpec((SHARD_GROUP, d, cs // SHARD_GROUP), lambda i, j: (j, 0, 0))],
        out_specs=[pl.BlockSpec((tm, cs), lambda i, j: (i, jnp.minimum(j, half - 1))),
                   pl.BlockSpec((tm, cs), lambda i, j: (i, dil_col(i, j)))]
        + [_residue_spec(tm, tpb, cs, dil, dil_col) for dil in DILATIONS[1:]] + [tok],
        out_shape=[wide, wide] + [_residue_shape(nb, seq, width, dil, BF16) for dil in DILATIONS[1:]]
        + [jax.ShapeDtypeStruct((t, d), BF16)],
        scratch_shapes=[_stage_shape(tm, cs)],
        params=_params("arbitrary", "arbitrary"), exchange=exchange)
    qkv_dil = [outs[1]] + [a.reshape(t, width) for a in outs[2:2 + n_res]]
    return (outs[0], qkv_dil, outs[-1]), got


def _qkv_bwd(dqkv, dxo, x, mod, g, win, tm, exchange=None):
    t, d = x.shape
    ns, _, cs = win.shape
    nt = t // tm
    nb = mod.shape[0]
    tpb = nt // nb
    cs, ns = cs * SHARD_GROUP, ns // SHARD_GROUP

    def body(dq_ref, dxo_ref, x_ref, mod_ref, g_ref, w_ref, dx_ref, dmod_ref, dg_ref, acc):
        i, j = pl.program_id(0), pl.program_id(1)

        @pl.when(j == 0)
        def _():
            acc[...] = jnp.zeros_like(acc)

        acc[...] += _dot_nt(dq_ref[...], _side_by_side(w_ref))

        @pl.when(j == ns - 1)
        def _():
            dx, dshift, dscale, dg = _modulate_bwd(acc[...], x_ref[...], g_ref[...], mod_ref, 1)
            dx_ref[...] = dxo_ref[...] + dx

            @pl.when(i % tpb == 0)
            def _():
                dmod_ref[...] = jnp.zeros_like(dmod_ref)

            @pl.when(i == 0)
            def _():
                dg_ref[...] = jnp.zeros_like(dg_ref)

            dmod_ref[0:1, :] += dshift
            dmod_ref[1:2, :] += dscale
            dg_ref[0:1, :] += dg

    tok = pl.BlockSpec((tm, d), lambda i, j: (i, 0))
    return _call(
        body, name="qkv_bwd", grid=(nt, ns), args=(dqkv, dxo, x, mod, g, win),
        in_specs=[pl.BlockSpec((tm, cs), lambda i, j: (i, j)), tok, tok,
                  pl.BlockSpec((None, N_MOD, d), lambda i, j: (i // tpb, 0, 0)),
                  pl.BlockSpec((1, d), lambda i, j: (0, 0)),
                  pl.BlockSpec((SHARD_GROUP, d, cs // SHARD_GROUP), lambda i, j: (j, 0, 0))],
        out_specs=[tok,
                   pl.BlockSpec((None, 8, d), lambda i, j: (i // tpb, 0, 0)),
                   pl.BlockSpec((8, d), lambda i, j: (0, 0))],
        out_shape=[jax.ShapeDtypeStruct((t, d), F32),
                   jax.ShapeDtypeStruct((nb, 8, d), F32), jax.ShapeDtypeStruct((8, d), F32)],
        scratch_shapes=[pltpu.VMEM((tm, d), F32)],
        params=_params("arbitrary", "arbitrary"), exchange=exchange)


def _own_lanes():
    lane = lax.broadcasted_iota(jnp.int32, (1, LANES), 1)
    return [lane < HEAD_DIM, lane >= HEAD_DIM]


def _pair_tiles(a):
    return [a[:, (h // 2) * LANES:(h // 2 + 1) * LANES] for h in range(a.shape[1] // HEAD_DIM)]


def _own_tiles(a, own):
    return [jnp.where(own[h % 2], tile, jnp.zeros_like(tile)) for h, tile in enumerate(_pair_tiles(a))]


def _merge_tiles(per_head, own):
    return jnp.concatenate([jnp.where(own[0], per_head[h], per_head[h + 1])
                            for h in range(0, len(per_head), 2)], axis=1)


def _scaled(q):
    return (q.astype(F32) * (HEAD_DIM ** -0.5)).astype(BF16)


def _sb_logits(qh, kh, tri, causal):
    zs = [_dot_nt(q, k) for q, k in zip(qh, kh)]
    es = [jnp.exp(-jnp.abs(z)) for z in zs]
    log_nots = [-(jnp.maximum(z, 0.0) + jnp.log(1.0 + e)) for z, e in zip(zs, es)]
    if causal is not None:
        log_nots = [jnp.where(causal, ln, 0.0) for ln in log_nots]
    return zs, es, [_split_dot(ln, tri) for ln in log_nots]


def _sb_masks():
    rows = lax.broadcasted_iota(jnp.int32, (SB_BLOCK, SB_BLOCK), 0)
    cols = lax.broadcasted_iota(jnp.int32, (SB_BLOCK, SB_BLOCK), 1)
    return (rows >= cols).astype(BF16), (rows <= cols).astype(BF16), cols < rows


def _sb_fwd(qkv, nb, seq, exchange=None):
    t = qkv.shape[0]
    n_pairs = (qkv.shape[1] // 3) // SB_WIDTH
    tb = SB_BLOCK
    n_blk = seq // tb

    def body(q_ref, k_ref, v_ref, o_ref, c_ref):
        tri, _, causal = _sb_masks()
        own = _own_lanes()

        def key_block(qh, kj, carry, mask):
            ks = pl.multiple_of(kj * tb, tb)
            kh, vh = _pair_tiles(k_ref[pl.ds(ks, tb), :]), _pair_tiles(v_ref[pl.ds(ks, tb), :])
            zs, _, suffixes = _sb_logits(qh, kh, tri, mask)
            ws = [jnp.exp(z + suffix + cr[1]) for z, suffix, cr in zip(zs, suffixes, carry)]
            if mask is not None:
                ws = [jnp.where(mask, w, 0.0) for w in ws]
            pv = [_dot(w.astype(BF16), v) for w, v in zip(ws, vh)]
            return tuple((cr[0] + p, cr[1] + suffix[:, 0:1]) for cr, p, suffix in zip(carry, pv, suffixes))

        def query_block(qi, _):
            qs = pl.multiple_of(qi * tb, tb)
            qh = _own_tiles(_scaled(q_ref[pl.ds(qs, tb), :]), own)
            zero = (jnp.zeros((tb, LANES), F32), jnp.zeros((tb, 1), F32))
            carry = key_block(qh, qi, (zero,) * SB_HEADS, causal)
            carry = lax.fori_loop(0, qi, lambda it, cr: key_block(qh, qi - 1 - it, cr, None), carry)
            o_ref[pl.ds(qs, tb), :] = _merge_tiles([cr[0] for cr in carry], own)
            c_ref[pl.ds(qs, tb), :] = _merge_tiles([jnp.broadcast_to(cr[1], (tb, LANES)) for cr in carry], own)
            return 0

        lax.fori_loop(0, n_blk, query_block, 0)

    def spec(offset):
        return pl.BlockSpec((seq, SB_WIDTH), lambda b, p: (b, offset + p))

    out = jax.ShapeDtypeStruct((t, n_pairs * SB_WIDTH), F32)
    return _call(
        body, name="sb_fwd", grid=(nb, n_pairs), args=(qkv, qkv, qkv),
        in_specs=[spec(0), spec(n_pairs), spec(2 * n_pairs)],
        out_specs=[spec(0), spec(0)], out_shape=[out, out],
        params=_params("arbitrary", "arbitrary"), exchange=exchange)


def _sb_bwd(qkv, do, csum, nb, seq, exchange=None):
    t = qkv.shape[0]
    n_pairs = (qkv.shape[1] // 3) // SB_WIDTH
    tb = SB_BLOCK
    n_blk = seq // tb
    scale = HEAD_DIM ** -0.5

    def body(q_ref, k_ref, v_ref, do_ref, c_ref, dq_ref, dk_ref, dv_ref, dkt_acc, dvt_acc):
        tri, tri_prefix, causal = _sb_masks()
        own = _own_lanes()
        dkt_acc[...] = jnp.zeros_like(dkt_acc)
        dvt_acc[...] = jnp.zeros_like(dvt_acc)

        def key_blocks(qh, qth, doh, doth, ch, kjs, carry, mask):
            nh = SB_HEADS
            chains = range(nh * len(kjs))
            kss = [pl.multiple_of(kj * tb, tb) for kj in kjs]
            kh = [tile for ks in kss for tile in _pair_tiles(k_ref[pl.ds(ks, tb), :])]
            vh = [tile for ks in kss for tile in _pair_tiles(v_ref[pl.ds(ks, tb), :])]
            zs, es, suffixes = _sb_logits(qh * len(kjs), kh, tri, mask)
            dws = [_dot_nt(doh[c % nh], vh[c]) for c in chains]
            lefts = []
            for c in chains:
                before = carry[c][1] if c < nh else lefts[c - nh]
                lefts.append(before + suffixes[c][:, 0:1])
            ws = [jnp.exp(zs[c] + suffixes[c] + (ch[c % nh] - lefts[c])) for c in chains]
            if mask is not None:
                ws = [jnp.where(mask, w, 0.0) for w in ws]
            dlws = [ws[c] * dws[c] for c in chains]
            dprefixes = [_split_dot(dlw, tri_prefix) for dlw in dlws]
            dvts = [_dot(doth[c % nh], ws[c].astype(BF16)) for c in chains]
            dlefts, dzbs = [], []
            for c in chains:
                dlefts.append(carry[c][2] if c < nh else dlefts[c - nh] + dprefixes[c - nh][:, tb - 1:tb])
                sig = jnp.where(zs[c] >= 0.0, 1.0, es[c]) * pl.reciprocal(1.0 + es[c], approx=True)
                dz = dlws[c] - sig * (dlefts[c] + dprefixes[c])
                if mask is not None:
                    dz = jnp.where(mask, dz, 0.0)
                dzbs.append(dz.astype(BF16))
            dkts = [_dot(qth[c % nh], dzbs[c]) for c in chains]
            dqs = [_dot(dzbs[c], kh[c]) for c in chains]
            for b, ks in enumerate(kss):
                pairs = range(b * nh, (b + 1) * nh, 2)
                dkt_acc[:, pl.ds(ks, tb)] += jnp.concatenate([dkts[c] + dkts[c + 1] for c in pairs], axis=0)
                dvt_acc[:, pl.ds(ks, tb)] += jnp.concatenate([dvts[c] + dvts[c + 1] for c in pairs], axis=0)
            last = (len(kjs) - 1) * nh
            return tuple((carry[h][0] + sum(dqs[h::nh]), lefts[last + h],
                          dlefts[last + h] + dprefixes[last + h][:, tb - 1:tb]) for h in range(nh))

        def query_block(qi, _):
            qs = pl.multiple_of(qi * tb, tb)
            qh = _own_tiles(_scaled(q_ref[pl.ds(qs, tb), :]), own)
            doh = _own_tiles(do_ref[pl.ds(qs, tb), :], own)
            qth = [a.astype(F32).T.astype(BF16) for a in qh]
            doth = [a.T.astype(BF16) for a in doh]
            doh = [a.astype(BF16) for a in doh]
            cv = c_ref[pl.ds(qs, tb), :]
            ch = [cv[:, h * HEAD_DIM:h * HEAD_DIM + 1] for h in range(SB_HEADS)]
            zero = (jnp.zeros((tb, LANES), F32), jnp.zeros((tb, 1), F32), jnp.zeros((tb, 1), F32))

            def key_block(kjs, cr, mask):
                return key_blocks(qh, qth, doh, doth, ch, kjs, cr, mask)

            carry = lax.fori_loop(0, qi, lambda kj, cr: key_block([kj], cr, None), (zero,) * SB_HEADS)
            carry = key_block([qi], carry, causal)
            dq = _merge_tiles([cr[0] for cr in carry], own) * scale
            dq_ref[pl.ds(qs, tb), :] = dq.astype(BF16)
            return 0

        lax.fori_loop(0, n_blk, query_block, 0)
        dk_ref[...] = dkt_acc[...].T.astype(BF16)
        dv_ref[...] = dvt_acc[...].T.astype(BF16)

    def spec(offset):
        return pl.BlockSpec((seq, SB_WIDTH), lambda b, p: (b, offset + p))

    out = jax.ShapeDtypeStruct((t, n_pairs * SB_WIDTH), BF16)
    return _call(
        body, name="sb_bwd", grid=(nb, n_pairs), args=(qkv, qkv, qkv, do, csum),
        in_specs=[spec(0), spec(n_pairs), spec(2 * n_pairs), spec(0), spec(0)],
        out_specs=[spec(0), spec(0), spec(0)],
        out_shape=[out, out, out],
        scratch_shapes=[pltpu.VMEM((SB_WIDTH, seq), F32), pltpu.VMEM((SB_WIDTH, seq), F32)],
        params=_params("arbitrary", "arbitrary"), exchange=exchange)


def _dil_block_scores(qh, kph, kch, bias_ref, has_prev, band_prev, band_cur):
    scale = HEAD_DIM ** -0.5
    heads = range(len(qh))
    no_prev = jnp.where(has_prev, 0.0, NEG_INF)
    zps = [_dot_nt(qh[h], kph[h]) for h in heads]
    zcs = [_dot_nt(qh[h], kch[h]) for h in heads]
    zps = [jnp.where(band_prev, zps[h] * scale + bias_ref[h, :, 0:DIL_BLOCK], NEG_INF) + no_prev for h in heads]
    zcs = [jnp.where(band_cur, zcs[h] * scale + bias_ref[h, :, DIL_BLOCK:2 * DIL_BLOCK], NEG_INF) for h in heads]
    return zps, zcs


def _dil_bands():
    rows = lax.broadcasted_iota(jnp.int32, (DIL_BLOCK, DIL_BLOCK), 0)
    cols = lax.broadcasted_iota(jnp.int32, (DIL_BLOCK, DIL_BLOCK), 1)
    return cols >= rows, cols <= rows


def _dil_fwd(qkv, bias, nb, seq, dil, exchange=None):
    t, width = qkv.shape
    n_pairs = (width // 3) // DIL_WIDTH
    bq = DIL_BLOCK
    n_blk = seq // bq
    per_seq = n_blk // dil
    heads = range(DIL_HEADS)

    def body(q_ref, k_ref, v_ref, bias_ref, o_ref, lse_ref):
        band_prev, band_cur = _dil_bands()
        own = _own_lanes()

        def block(n, _):
            has_prev = (n & (per_seq - 1)) != 0
            qs = pl.multiple_of(n * bq, bq)
            ps = pl.multiple_of(jnp.maximum(n - 1, 0) * bq, bq)
            qh = _own_tiles(q_ref[pl.ds(qs, bq), :], own)
            kp, kc = _pair_tiles(k_ref[pl.ds(ps, bq), :]), _pair_tiles(k_ref[pl.ds(qs, bq), :])
            vp, vc = _pair_tiles(v_ref[pl.ds(ps, bq), :]), _pair_tiles(v_ref[pl.ds(qs, bq), :])
            zps, zcs = _dil_block_scores(qh, kp, kc, bias_ref, has_prev, band_prev, band_cur)
            ms = [jnp.maximum(jnp.max(zps[h], axis=1, keepdims=True), jnp.max(zcs[h], axis=1, keepdims=True))
                  for h in heads]
            eps = [jnp.exp(zps[h] - ms[h]) for h in heads]
            ecs = [jnp.exp(zcs[h] - ms[h]) for h in heads]
            pvs = [_dot(eps[h].astype(BF16), vp[h]) + _dot(ecs[h].astype(BF16), vc[h]) for h in heads]
            dens = [jnp.sum(eps[h], axis=1, keepdims=True) + jnp.sum(ecs[h], axis=1, keepdims=True) for h in heads]
            o_ref[pl.ds(qs, bq), :] = _merge_tiles([pvs[h] / dens[h] for h in heads], own)
            lse_ref[pl.ds(qs, bq), :] = _merge_tiles(
                [jnp.broadcast_to(ms[h] + jnp.log(dens[h]), (bq, LANES)) for h in heads], own)
            return 0

        lax.fori_loop(0, n_blk, block, 0)

    def spec(offset):
        return pl.BlockSpec((seq, DIL_WIDTH), lambda b, p: (b, offset + p))

    out = jax.ShapeDtypeStruct((t, n_pairs * DIL_WIDTH), F32)
    return _call(
        body, name=f"dil_fwd{dil}", grid=(nb, n_pairs), args=(qkv, qkv, qkv, bias),
        in_specs=[spec(0), spec(n_pairs), spec(2 * n_pairs),
                  pl.BlockSpec((DIL_HEADS, bq, 2 * bq), lambda b, p: (p, 0, 0))],
        out_specs=[spec(0), spec(0)], out_shape=[out, out],
        params=_params("arbitrary", "arbitrary"), exchange=exchange)


def _dil_bwd(qkv, bias, do, lse, delta, nb, seq, dil):
    t, width = qkv.shape
    n_pairs = (width // 3) // DIL_WIDTH
    bq = DIL_BLOCK
    n_blk = seq // bq
    per_seq = n_blk // dil
    scale = HEAD_DIM ** -0.5
    heads = range(DIL_HEADS)

    def body(q_ref, k_ref, v_ref, bias_ref, do_ref, lse_ref, dl_ref, dq_ref, dk_ref, dv_ref, db_ref,
             dk_acc, dv_acc):
        band_prev, band_cur = _dil_bands()
        own = _own_lanes()
        dk_acc[...] = jnp.zeros_like(dk_acc)
        dv_acc[...] = jnp.zeros_like(dv_acc)

        @pl.when(pl.program_id(1) == 0)
        def _():
            db_ref[...] = jnp.zeros_like(db_ref)

        def block(n, _):
            has_prev = (n & (per_seq - 1)) != 0
            qs = pl.multiple_of(n * bq, bq)
            ps = pl.multiple_of(jnp.maximum(n - 1, 0) * bq, bq)
            qh = _own_tiles(q_ref[pl.ds(qs, bq), :], own)
            kp, kc = _pair_tiles(k_ref[pl.ds(ps, bq), :]), _pair_tiles(k_ref[pl.ds(qs, bq), :])
            vp, vc = _pair_tiles(v_ref[pl.ds(ps, bq), :]), _pair_tiles(v_ref[pl.ds(qs, bq), :])
            doh = _own_tiles(do_ref[pl.ds(qs, bq), :].astype(BF16), own)
            lse_v, dl_v = lse_ref[pl.ds(qs, bq), :], dl_ref[pl.ds(qs, bq), :]
            zps, zcs = _dil_block_scores(qh, kp, kc, bias_ref, has_prev, band_prev, band_cur)
            dpp = [_dot_nt(doh[h], vp[h]) for h in heads]
            dpc = [_dot_nt(doh[h], vc[h]) for h in heads]
            lse_h = [lse_v[:, h * HEAD_DIM:h * HEAD_DIM + 1] for h in heads]
            dl_h = [dl_v[:, h * HEAD_DIM:h * HEAD_DIM + 1] for h in heads]
            pps = [jnp.exp(zps[h] - lse_h[h]) for h in heads]
            pcs = [jnp.exp(zcs[h] - lse_h[h]) for h in heads]
            dvp = [_dot_tn(pps[h].astype(BF16), doh[h]) for h in heads]
            dvc = [_dot_tn(pcs[h].astype(BF16), doh[h]) for h in heads]
            dzps = [pps[h] * (dpp[h] - dl_h[h]) for h in heads]
            dzcs = [pcs[h] * (dpc[h] - dl_h[h]) for h in heads]
            dzp_b = [(dzps[h] * scale).astype(BF16) for h in heads]
            dzc_b = [(dzcs[h] * scale).astype(BF16) for h in heads]
            dqs = [_dot(dzp_b[h], kp[h]) + _dot(dzc_b[h], kc[h]) for h in heads]
            dkp = [_dot_tn(dzp_b[h], qh[h]) for h in heads]
            dkc = [_dot_tn(dzc_b[h], qh[h]) for h in heads]
            for h in heads:
                db_ref[h, :, 0:bq] += dzps[h]
                db_ref[h, :, bq:2 * bq] += dzcs[h]
            def pair_sums(per_head):
                return jnp.concatenate([per_head[h] + per_head[h + 1] for h in heads[::2]], axis=1)

            dq_ref[pl.ds(qs, bq), :] = _merge_tiles(dqs, own).astype(BF16)
            dk_acc[pl.ds(ps, bq), :] += pair_sums(dkp)
            dk_acc[pl.ds(qs, bq), :] += pair_sums(dkc)
            dv_acc[pl.ds(ps, bq), :] += pair_sums(dvp)
            dv_acc[pl.ds(qs, bq), :] += pair_sums(dvc)
            return 0

        lax.fori_loop(0, n_blk, block, 0)
        dk_ref[...] = dk_acc[...].astype(BF16)
        dv_ref[...] = dv_acc[...].astype(BF16)

    def spec(offset):
        return pl.BlockSpec((seq, DIL_WIDTH), lambda p, b: (b, offset + p))

    bias_spec = pl.BlockSpec((DIL_HEADS, bq, 2 * bq), lambda p, b: (p, 0, 0))
    out = jax.ShapeDtypeStruct((t, n_pairs * DIL_WIDTH), BF16)
    return pl.pallas_call(
        body, name=f"dil_bwd{dil}", grid=(n_pairs, nb),
        in_specs=[spec(0), spec(n_pairs), spec(2 * n_pairs), bias_spec, spec(0), spec(0), spec(0)],
        out_specs=[spec(0), spec(0), spec(0), bias_spec],
        out_shape=[out, out, out, jax.ShapeDtypeStruct(bias.shape, F32)],
        scratch_shapes=[pltpu.VMEM((seq, DIL_WIDTH), F32), pltpu.VMEM((seq, DIL_WIDTH), F32)],
        compiler_params=_params("arbitrary", "arbitrary"),
    )(qkv, qkv, qkv, bias, do, lse, delta)


def _head_blocks(width):
    rows = lax.broadcasted_iota(jnp.int32, (width, width), 0) // HEAD_DIM
    cols = lax.broadcasted_iota(jnp.int32, (width, width), 1) // HEAD_DIM
    return (rows == cols).astype(BF16)


def _head_mean(v, gmat):
    return _split_dot(v, gmat) * (1.0 / HEAD_DIM)


def _residue_views(arrays, nb, seq):
    return [a if dil == 1 else a.reshape(nb, dil, seq // dil, a.shape[1]) for a, dil in zip(arrays, DILATIONS)]


def _mix_out_fwd(osb, ocs, lses, gsb, gdil, wout, x, mod, tm):
    t, d = x.shape
    ds = osb.shape[1]
    nt = t // tm
    nb = mod.shape[0]
    tpb = nt // nb
    seq = t // nb
    n_cfg = len(DILATIONS)

    def body(osb_ref, *refs):
        oc_refs, lse_refs = refs[:n_cfg], refs[n_cfg:2 * n_cfg]
        gsb_ref, gdil_ref, w_ref, x_ref, mod_ref = refs[2 * n_cfg:2 * n_cfg + 5]
        xo_ref, on_ref, m_ref, odil_ref = refs[2 * n_cfg + 5:2 * n_cfg + 9]
        ld_refs = refs[2 * n_cfg + 9:3 * n_cfg + 9]
        stages, sc = refs[3 * n_cfg + 9:]
        ocv, lsev = [oc_refs[0][...]], [lse_refs[0][...]]
        for i, dil in enumerate(DILATIONS[1:]):
            ocv.append(_from_residue_rows(oc_refs[i + 1], stages.at[2 * i], dil))
            lsev.append(_from_residue_rows(lse_refs[i + 1], stages.at[2 * i + 1], dil))
        top = functools.reduce(jnp.maximum, lsev)
        total = top + jnp.log(sum(jnp.exp(l - top) for l in lsev))
        odil = sum(jnp.exp(l - total) * o for o, l in zip(ocv, lsev))
        odil_ref[...] = odil
        ld_refs[0][...] = total
        _stage(total, sc)
        for ref, dil in zip(ld_refs[1:], DILATIONS[1:]):
            _to_residue_rows(sc, ref, dil)
        gm = _head_blocks(ds)
        parts = []
        for o, g_ref in ((osb_ref[...], gsb_ref), (odil, gdil_ref)):
            parts.append(o * lax.rsqrt(_head_mean(o * o, gm) + EPS) * g_ref[...])
        on = jnp.concatenate(parts, axis=1).astype(BF16)
        on_ref[...] = on
        m = _dot(on, w_ref[...])
        m_ref[...] = m
        xo_ref[...] = x_ref[...] + mod_ref[5:6, :] * m

    tok = pl.BlockSpec((tm, d), lambda i: (i, 0))
    hd = pl.BlockSpec((tm, ds), lambda i: (i, 0))
    res = [hd] + [_residue_spec(tm, tpb, ds, dil, lambda i: 0) for dil in DILATIONS[1:]]
    res_shape = [jax.ShapeDtypeStruct((t, ds), F32)] + [_residue_shape(nb, seq, ds, dil, F32) for dil in DILATIONS[1:]]
    gain = pl.BlockSpec((1, ds), lambda i: (0, 0))
    outs = pl.pallas_call(
        body, name="mix_out_fwd", grid=(nt,),
        in_specs=[hd] + res + res + [gain, gain,
                  pl.BlockSpec(wout.shape, lambda i: (0, 0)),
                  tok, pl.BlockSpec((None, N_MOD, d), lambda i: (i // tpb, 0, 0))],
        out_specs=[tok, pl.BlockSpec((tm, 2 * ds), lambda i: (i, 0)), tok, hd] + res,
        out_shape=[jax.ShapeDtypeStruct((t, d), F32), jax.ShapeDtypeStruct((t, 2 * ds), BF16),
                   jax.ShapeDtypeStruct((t, d), F32), jax.ShapeDtypeStruct((t, ds), F32)] + res_shape,
        scratch_shapes=[pltpu.VMEM((2 * (n_cfg - 1), ds // LANES, tm, LANES), F32), _stage_shape(tm, ds)],
        compiler_params=_params("arbitrary"),
    )(osb, *_residue_views(ocs, nb, seq), *_residue_views(lses, nb, seq), gsb, gdil, wout, x, mod)
    return outs[0], outs[1], outs[2], outs[3], [a.reshape(t, ds) for a in outs[4:]]


def _mix_out_bwd(dxo, m, mod, wout, osb, odil, gsb, gdil, tm):
    t, d = dxo.shape
    ds = osb.shape[1]
    nt = t // tm
    nb = mod.shape[0]
    tpb = nt // nb
    seq = t // nb
    n_cfg = len(DILATIONS)

    def body(dxo_ref, m_ref, mod_ref, w_ref, osb_ref, odil_ref, gsb_ref, gdil_ref,
             dm_ref, dosb_ref, *rest):
        do_refs, dl_refs = rest[:n_cfg], rest[n_cfg:2 * n_cfg]
        dmod_ref, dg_ref, sc = rest[2 * n_cfg:]
        dodil_ref, dldil_ref = do_refs[0], dl_refs[0]
        i = pl.program_id(0)
        dxo_v = dxo_ref[...]
        dm = (mod_ref[5:6, :] * dxo_v).astype(BF16)
        dm_ref[...] = dm
        dgt = jnp.sum(m_ref[...] * dxo_v, axis=0, keepdims=True)
        don = _dot_nt(dm, w_ref[...])
        gm = _head_blocks(ds)

        @pl.when(i % tpb == 0)
        def _():
            dmod_ref[...] = jnp.zeros_like(dmod_ref)

        @pl.when(i == 0)
        def _():
            dg_ref[...] = jnp.zeros_like(dg_ref)

        dmod_ref[2:3, :] += dgt
        groups = ((osb_ref, gsb_ref, dosb_ref), (odil_ref, gdil_ref, dodil_ref))
        for k, (o_ref, g_ref, do_ref) in enumerate(groups):
            o = o_ref[...]
            dn_out = don[:, k * ds:(k + 1) * ds]
            r = lax.rsqrt(_head_mean(o * o, gm) + EPS)
            n = o * r
            dg_ref[0:1, k * ds:(k + 1) * ds] += jnp.sum(dn_out * n, axis=0, keepdims=True)
            dn = dn_out * g_ref[...]
            do = r * (dn - n * _head_mean(dn * n, gm))
            do_ref[...] = do
            if k == 1:
                delta = _head_mean(do * o, gm) * float(HEAD_DIM)
                dldil_ref[...] = delta
                for value, refs in ((do, do_refs), (delta, dl_refs)):
                    _stage(value, sc)
                    for ref, dil in zip(refs[1:], DILATIONS[1:]):
                        _to_residue_rows(sc, ref, dil)

    tok = pl.BlockSpec((tm, d), lambda i: (i, 0))
    hd = pl.BlockSpec((tm, ds), lambda i: (i, 0))
    res = [hd] + [_residue_spec(tm, tpb, ds, dil, lambda i: 0) for dil in DILATIONS[1:]]
    res_shape = [jax.ShapeDtypeStruct((t, ds), F32)] + [_residue_shape(nb, seq, ds, dil, F32) for dil in DILATIONS[1:]]
    gain = pl.BlockSpec((1, ds), lambda i: (0, 0))
    outs = pl.pallas_call(
        body, name="mix_out_bwd", grid=(nt,),
        in_specs=[tok, tok, pl.BlockSpec((None, N_MOD, d), lambda i: (i // tpb, 0, 0)),
                  pl.BlockSpec(wout.shape, lambda i: (0, 0)), hd, hd, gain, gain],
        out_specs=[tok, hd] + res + res
        + [pl.BlockSpec((None, 8, d), lambda i: (i // tpb, 0, 0)), pl.BlockSpec((8, 2 * ds), lambda i: (0, 0))],
        out_shape=[jax.ShapeDtypeStruct((t, d), BF16), jax.ShapeDtypeStruct((t, ds), F32)] + res_shape + res_shape
        + [jax.ShapeDtypeStruct((nb, 8, d), F32), jax.ShapeDtypeStruct((8, 2 * ds), F32)],
        scratch_shapes=[_stage_shape(tm, ds)],
        compiler_params=_params("arbitrary"),
    )(dxo, m, mod, wout, osb, odil, gsb, gdil)
    flat = [a.reshape(t, ds) for a in outs[2:2 + 2 * n_cfg]]
    return outs[0], outs[1], flat[:n_cfg], flat[n_cfg:], outs[-2], outs[-1]


def _merge_dqkv(sb_parts, dil_parts, nb, tm):
    t, ds = sb_parts[0].shape
    nt = t // tm
    tpb = nt // nb
    seq = t // nb
    n_cfg = len(DILATIONS)

    def body(*refs):
        sb_refs, dil_refs = refs[:3], refs[3:3 + 3 * n_cfg]
        o_ref, sc = refs[3 + 3 * n_cfg:]
        for k in range(3):
            o_ref[:, k * ds:(k + 1) * ds] = sb_refs[k][...]
            total = dil_refs[k * n_cfg][...].astype(F32)
            for i, dil in enumerate(DILATIONS[1:]):
                total = total + _from_residue_rows(dil_refs[k * n_cfg + i + 1], sc, dil)
            o_ref[:, (3 + k) * ds:(4 + k) * ds] = total.astype(BF16)

    hd = pl.BlockSpec((tm, ds), lambda i: (i, 0))
    res = [hd] + [_residue_spec(tm, tpb, ds, dil, lambda i: 0) for dil in DILATIONS[1:]]
    views = [v for parts in dil_parts for v in _residue_views(parts, nb, seq)]
    return pl.pallas_call(
        body, name="merge_dqkv", grid=(nt,),
        in_specs=[hd] * 3 + res * 3,
        out_specs=pl.BlockSpec((tm, 6 * ds), lambda i: (i, 0)),
        out_shape=jax.ShapeDtypeStruct((t, 6 * ds), BF16),
        scratch_shapes=[_stage_shape(tm, ds)],
        compiler_params=_params("arbitrary"),
    )(*sb_parts, *views)


def _row_tile(rows):
    if rows <= 256:
        return rows
    for cand in range(256, 15, -16):
        if rows % cand == 0:
            return cand
    return rows


def _adamw(w, parts, m, v, name, transposed=False):
    rows, cols = w.shape
    n_parts = parts.shape[0]
    tr = _row_tile(rows)
    c1 = 1.0 / (1.0 - ADAM_B1 ** ADAM_STEP)
    c2 = 1.0 / (1.0 - ADAM_B2 ** ADAM_STEP)

    def body(w_ref, p_ref, m_ref, v_ref, g_ref, d_ref, nm_ref, nv_ref):
        g = p_ref[0].astype(F32)
        for i in range(1, n_parts):
            g = g + p_ref[i].astype(F32)
        wv, mv, vv = w_ref[...], m_ref[...], v_ref[...]
        if transposed:
            wv, mv, vv = wv.T, mv.T, vv.T
        nm = ADAM_B1 * mv + (1.0 - ADAM_B1) * g
        nv = ADAM_B2 * vv + (1.0 - ADAM_B2) * (g * g)
        g_ref[...] = g
        nm_ref[...] = nm
        nv_ref[...] = nv
        d_ref[...] = -ADAM_LR * ((nm * c1) / (jnp.sqrt(nv * c2) + ADAM_EPS) + ADAM_WD * wv)

    blk = pl.BlockSpec((tr, cols), lambda i: (i, 0))
    if transposed:
        oblk = pl.BlockSpec((cols, tr), lambda i: (0, i))
        pblk = pl.BlockSpec((n_parts, cols, tr), lambda i: (0, 0, i))
        out = jax.ShapeDtypeStruct((cols, rows), F32)
    else:
        oblk, pblk = blk, pl.BlockSpec((n_parts, tr, cols), lambda i: (0, i, 0))
        out = jax.ShapeDtypeStruct((rows, cols), F32)
    return pl.pallas_call(
        body, name=name, grid=(rows // tr,),
        in_specs=[blk, pblk, blk, blk],
        out_specs=[oblk, oblk, oblk, oblk], out_shape=[out, out, out, out],
        compiler_params=_params("arbitrary"),
    )(w, parts, m, v)


def _t5_bucket(n):
    max_exact = N_BUCKETS // 2
    nf = np.maximum(n, 1).astype(np.float32)
    large = max_exact + (np.log(nf / max_exact) / math.log(MAX_DISTANCE / max_exact)
                         * (N_BUCKETS - max_exact)).astype(np.int32)
    large = np.minimum(large, N_BUCKETS - 1)
    return np.where(n < max_exact, n, large).astype(np.int32)


def _bucket_onehot():
    table = np.zeros((len(DILATIONS), 2 * DIL_BLOCK + 1, N_BUCKETS), np.float32)
    for i, dil in enumerate(DILATIONS):
        buckets = _t5_bucket(np.arange(DIL_BLOCK + 1) * dil)
        for m in range(DIL_BLOCK + 1):
            table[i, m, buckets[DIL_BLOCK - m]] = 1.0
    return table


def _bias_blocks(rel_bias):
    row = jnp.einsum("cmn,nh->chm", _bucket_onehot(), rel_bias, precision=lax.Precision.HIGHEST)
    n_cfg, n_heads, width = row.shape
    tiled = jnp.tile(row, (1, 1, DIL_BLOCK))[..., :DIL_BLOCK * (width - 1)]
    return tiled.reshape(n_cfg, n_heads, DIL_BLOCK, width - 1)


def _bias_blocks_bwd(dblocks):
    n_cfg, n_heads = dblocks.shape[:2]
    width = 2 * DIL_BLOCK + 1
    flat = dblocks.reshape(n_cfg, n_heads, DIL_BLOCK * (width - 1))
    flat = jnp.pad(flat, ((0, 0), (0, 0), (0, DIL_BLOCK)))
    drow = jnp.sum(flat.reshape(n_cfg, n_heads, DIL_BLOCK, width), axis=2)
    return jnp.einsum("chm,cmn->nh", drow, _bucket_onehot(), precision=lax.Precision.HIGHEST)


def _pad_to(a, axis, size):
    pad = [(0, 0)] * a.ndim
    pad[axis] = (0, size - a.shape[axis])
    return jnp.pad(a, pad)


def _lane_pad(n):
    return -(-n // LANES) * LANES


def _local_step(x, target, mod, gains, weights, rel_bias, tm, distributed):
    nb, seq, d = x.shape
    t = nb * seq
    g_ffn1, g_mix, g_sb, g_dil, g_ffn2, g_final = gains
    wg1, wu1, wd1 = weights[:3]
    x0 = x.reshape(t, d)
    ds = g_sb.shape[1]
    bias = _bias_blocks(rel_bias)

    def beside(arrays, scatter):
        return _Exchange(arrays, scatter) if distributed else None

    tp, tg = min(PROJ_TILE, seq), min(GRAD_TILE, t)

    (x1, f1, gate1, up1), got = _ffn_fwd(x0, mod, g_ffn1, wg1, wu1, wd1, 0, tp, beside(weights[3:4], False))
    win = got[0] if distributed else weights[3]
    (qkv, qkvd, h2), got = _qkv_fwd(x1, mod, g_mix, win, tp, beside(weights[4:5], False))
    wout = got[0] if distributed else weights[4]
    wout2 = wout.reshape(-1, d)
    (osb, csb), got = _sb_fwd(qkv, nb, seq, beside(weights[5:7], False))
    wg2, wu2 = got if distributed else weights[5:7]
    n_cfg = len(DILATIONS)
    piece = -(-weights[7].shape[-2] // n_cfg // 16) * 16
    ocs, lses, wd2_pieces = [], [], []
    for i, dil in enumerate(DILATIONS):
        rows = weights[7][..., i * piece:(i + 1) * piece, :]
        (oc, lse), got = _dil_fwd(qkvd[i], bias[i], nb, seq, dil, beside([rows], False))
        wd2_pieces.append(got[0] if distributed else rows)
        ocs.append(oc)
        lses.append(lse)
    wd2 = jnp.concatenate(wd2_pieces, axis=-2)
    x2, on, mix, odil, ldil = _mix_out_fwd(osb, ocs, lses, g_sb, g_dil, wout2, x1, mod, tm)
    (dx3, f3, gate3, up3, head), _ = _ffn_fwd(x2, mod, g_ffn2, wg2, wu2, wd2, 2, tp,
                                              head=(target.reshape(t, d), g_final))
    loss_sum = 0.5 * jnp.sum(head[0]) / d
    dg_final = head[1:2]

    (dx2, dgate3, dup3, act3, h3, df3, dmod3, dg_ffn2), _ = _ffn_bwd(
        dx3, x2, f3, mod, g_ffn2, gate3, up3, wg2, wu2, wd2, 2, tm)
    gwg2, gwu2, gwd2 = _ffn_weight_grads(h3, dgate3, dup3, act3, df3, tg, 2)

    dm, dosb, dodil, dldil, dmod2b, dg_heads = _mix_out_bwd(
        dx2, mix, mod, wout2, osb, odil, g_sb, g_dil, tm)
    gwout = _mm_tn(on, dm,
                   pl.BlockSpec((tg, wout.shape[1]), lambda i, j: (i, j)),
                   pl.BlockSpec((tg, d), lambda i, j: (i, 0)),
                   wout.shape, t // tg, "grad_wout")

    (dq_sb, dk_sb, dv_sb), parts_late = _sb_bwd(qkv, dosb, csb, nb, seq,
                                                beside([gwout, gwg2, gwu2, gwd2], True))
    dil_grads = [_dil_bwd(qkvd[i], bias[i], dodil[i], ldil[i], dldil[i], nb, seq, dil)
                 for i, dil in enumerate(DILATIONS)]
    dqkv = _merge_dqkv([dq_sb, dk_sb, dv_sb], [[g[k] for g in dil_grads] for k in range(3)], nb, tm)
    drel = _bias_blocks_bwd(jnp.stack([g[3] for g in dil_grads]))

    cs = win.shape[2]
    gwin = _mm_tn(h2, dqkv,
                  pl.BlockSpec((tg, d), lambda i, j: (i, 0)),
                  pl.BlockSpec((tg, cs), lambda i, j: (i, j)),
                  win.shape, t // tg, "grad_win", pair_reduce=distributed)
    (dx1, dmod2a, dg_mix), parts_mid = _qkv_bwd(
        dqkv, dx2, x1, mod, g_mix, win, tp, _Exchange([gwin], True, chips=[True]) if distributed else None)

    (dx0, dgate1, dup1, act1, h1, df1, dmod1, dg_ffn1), _ = _ffn_bwd(
        dx1, x0, f1, mod, g_ffn1, gate1, up1, wg1, wu1, wd1, 0, tm)
    dmod = jnp.concatenate([dmod1[:, 0:3], dmod2a[:, 0:2], dmod2b[:, 2:3], dmod3[:, 0:3]], axis=1)
    ggrads = (dg_ffn1[0:1], dg_mix[0:1], dg_heads[0:1], drel, dg_ffn2[0:1], dg_final)
    if not distributed:
        gw1 = _ffn_weight_grads(h1, dgate1, dup1, act1, df1, tg, 0)
        return loss_sum, dx0.reshape(nb, seq, d), tuple(gw1) + (gwin, gwout, gwg2, gwu2, gwd2), dmod, ggrads

    dg_heads_row, drel_flat = dg_heads[0:1], drel.reshape(1, -1)
    width = max(d, dg_heads_row.shape[1], drel_flat.shape[1])
    small = jnp.concatenate(
        [_pad_to(a.reshape(1, -1), 1, width)
         for a in (dg_ffn1[0:1], dg_mix[0:1], dg_ffn2[0:1], dg_final, dg_heads_row, drel_flat, loss_sum)]
        + [jnp.zeros((1, width), F32)], axis=0)
    dmod_pad = _pad_to(dmod.reshape(nb, N_MOD * d), 0, 8)
    everyone = _Exchange([jnp.broadcast_to(dmod_pad, (N_DEV,) + dmod_pad.shape),
                          jnp.broadcast_to(small, (N_DEV,) + small.shape)], True)
    sent_g, sent_u, gwd1, (dmod_all, small_all) = _ffn_weight_grads(
        h1, dgate1, dup1, act1, df1, tg, 0, stream=True, first=everyone)
    wgrads = (sent_g, sent_u, gwd1) + tuple(parts_mid + parts_late)
    return dx0.reshape(nb, seq, d), wgrads, dmod_all, small_all


def kernel(x, c, w_ada, b_ada, g_ffn1, w1_gate, w1_up, w1_down, g_mix, w_in, g_sb_out, g_dil_out, w_out, rel_bias, g_ffn2, w2_gate, w2_up, w2_down, g_final, loss_target, m_w_ada, m_b_ada, m_g_ffn1, m_w1_gate, m_w1_up, m_w1_down, m_g_mix, m_w_in, m_g_sb_out, m_g_dil_out, m_w_out, m_rel_bias, m_g_ffn2, m_w2_gate, m_w2_up, m_w2_down, m_g_final, v_w_ada, v_b_ada, v_g_ffn1, v_w1_gate, v_w1_up, v_w1_down, v_g_mix, v_w_in, v_g_sb_out, v_g_dil_out, v_w_out, v_rel_bias, v_g_ffn2, v_w2_gate, v_w2_up, v_w2_down, v_g_final):
    nb, seq, d = x.shape
    me = 4 * lax.axis_index("x") + 2 * lax.axis_index("y") + lax.axis_index("c")
    tm = min(TOKEN_TILE, seq)
    fs = w1_gate.shape[2]
    fs_pad = _lane_pad(fs)
    ada_cols = w_ada.shape[2]

    def col_shard(w):
        return _pad_to(w[0].astype(BF16), 1, fs_pad)

    def row_shard(w):
        return _pad_to(w[0].astype(BF16), 0, fs_pad)

    shards = [col_shard(w1_gate), col_shard(w1_up), row_shard(w1_down), w_in[0].astype(BF16),
              w_out[0].astype(BF16), col_shard(w2_gate), col_shard(w2_up), row_shard(w2_down)]
    b_cols = lax.dynamic_slice(b_ada, (0, me * ada_cols), (1, ada_cols))
    c_every, mod_all, first = _first_exchange(_pad_to(c, 0, 8), shards[:3], w_ada[0], b_cols)
    c_all = c_every[:, :nb].reshape(N_DEV * nb, d)
    weights = first + shards[3:]
    mod = lax.dynamic_slice(mod_all, (0, me * 8, 0), (N_DEV, nb, ada_cols))
    mod = mod.transpose(1, 0, 2).reshape(nb, N_MOD, d)

    n_sb = g_sb_out.shape[1] * g_sb_out.shape[2]
    gains = (g_ffn1, g_mix, g_sb_out.reshape(1, n_sb), g_dil_out.reshape(1, -1), g_ffn2,
             g_final.reshape(1, d))
    grad_x, parts, dmod_all, small_all = _local_step(x, loss_target, mod, gains, weights, rel_bias, tm, True)

    last_part = _exchange([parts[2]], True, "scatter_last", chips=[True])[0]
    parts = parts[:2] + (last_part,) + parts[3:]
    dmod_all = dmod_all[:, :nb].reshape(N_DEV * nb, N_MOD * d)
    dmod_cols = lax.dynamic_slice(dmod_all, (0, me * ada_cols), (N_DEV * nb, ada_cols))
    gw_ada, gb_ada = _ada_bwd(c_all, dmod_cols, dmod_all)

    def small_part(row, size, shape):
        return small_all[:, row, :size].reshape((N_DEV,) + shape)

    loss = jnp.sum(small_all[:, 6, 0])

    n_rel = rel_bias.shape
    updates = {
        "w_ada": (w_ada[0], gw_ada[None], m_w_ada[0], v_w_ada[0]),
        "b_ada": (b_ada, gb_ada[None], m_b_ada, v_b_ada),
        "g_ffn1": (g_ffn1, small_part(0, d, (1, d)), m_g_ffn1, v_g_ffn1),
        "w1_gate": (w1_gate[0], parts[0], m_w1_gate[0], v_w1_gate[0]),
        "w1_up": (w1_up[0], parts[1], m_w1_up[0], v_w1_up[0]),
        "w1_down": (w1_down[0], parts[2], m_w1_down[0], v_w1_down[0]),
        "g_mix": (g_mix, small_part(1, d, (1, d)), m_g_mix, v_g_mix),
        "w_in": (w_in[0], parts[3], m_w_in[0], v_w_in[0]),
        "g_sb_out": (g_sb_out[0], small_all[:, 4, :n_sb].reshape((N_DEV,) + g_sb_out.shape[1:]),
                     m_g_sb_out[0], v_g_sb_out[0]),
        "g_dil_out": (g_dil_out[0], small_all[:, 4, n_sb:n_sb + g_dil_out[0].size].reshape((N_DEV,) + g_dil_out.shape[1:]),
                      m_g_dil_out[0], v_g_dil_out[0]),
        "w_out": (w_out[0], parts[4], m_w_out[0], v_w_out[0]),
        "rel_bias": (rel_bias, small_part(5, rel_bias.size, n_rel), m_rel_bias, v_rel_bias),
        "g_ffn2": (g_ffn2, small_part(2, d, (1, d)), m_g_ffn2, v_g_ffn2),
        "w2_gate": (w2_gate[0], parts[5], m_w2_gate[0], v_w2_gate[0]),
        "w2_up": (w2_up[0], parts[6], m_w2_up[0], v_w2_up[0]),
        "w2_down": (w2_down[0], parts[7], m_w2_down[0], v_w2_down[0]),
        "g_final": (g_final.reshape(1, d), small_part(3, d, (1, d)), m_g_final.reshape(1, d), v_g_final.reshape(1, d)),
    }
    shapes = {"w_ada": w_ada.shape, "b_ada": b_ada.shape, "g_ffn1": g_ffn1.shape, "w1_gate": w1_gate.shape,
              "w1_up": w1_up.shape, "w1_down": w1_down.shape, "g_mix": g_mix.shape, "w_in": w_in.shape,
              "g_sb_out": g_sb_out.shape, "g_dil_out": g_dil_out.shape, "w_out": w_out.shape,
              "rel_bias": rel_bias.shape, "g_ffn2": g_ffn2.shape, "w2_gate": w2_gate.shape,
              "w2_up": w2_up.shape, "w2_down": w2_down.shape, "g_final": g_final.shape}
    grads, deltas, new_m, new_v = [], [], [], []
    for name, (w, p, m, v) in updates.items():
        transposed = name in ("w1_gate", "w1_up", "w2_gate", "w2_up")
        outs = _adamw(w, p, m, v, f"adamw_{name}", transposed)
        for dst, a in zip((grads, deltas, new_m, new_v), outs):
            dst.append((a.T if transposed else a).reshape(shapes[name]))
    return (loss, grad_x, *grads, *deltas, *new_m, *new_v)
```

```python
import functools
import math

import numpy as np
import jax
import jax.numpy as jnp
from jax import lax
from jax.experimental import pallas as pl
from jax.experimental.pallas import tpu as pltpu

F32 = jnp.float32
BF16 = jnp.bfloat16

EPS = 1e-6
NEG_INF = -1e30
HEAD_DIM = 64
LANES = 128
DIL_BLOCK = 128
DILATIONS = (1, 4, 16)
N_BUCKETS = 32
MAX_DISTANCE = 2048
N_MOD = 9
N_DEV = 8
SB_BLOCK = 256
SB_HEADS = 4
SB_WIDTH = SB_HEADS * HEAD_DIM
DIL_HEADS = 4
DIL_WIDTH = DIL_HEADS * HEAD_DIM
TOKEN_TILE = 512
PROJ_TILE = 1024
GRAD_TILE = 1024
SHARD_GROUP = 2
FFN_CHUNKS = 2
VMEM_LIMIT_BYTES = 56 * 1024 * 1024

ADAM_LR = 0.001
ADAM_B1 = 0.9
ADAM_B2 = 0.999
ADAM_EPS = 1e-08
ADAM_WD = 0.01
ADAM_STEP = 10

NT_DIMS = (((1,), (1,)), ((), ()))
TN_DIMS = (((0,), (0,)), ((), ()))


def _params(*sem):
    return pltpu.CompilerParams(dimension_semantics=sem, vmem_limit_bytes=VMEM_LIMIT_BYTES)


def _once(spec):
    return pl.BlockSpec(spec.block_shape, spec.index_map, pipeline_mode=pl.Buffered(1))


def _dot(a, b):
    return jnp.dot(a, b, preferred_element_type=F32)


def _dot_nt(a, b):
    return lax.dot_general(a, b, NT_DIMS, preferred_element_type=F32)


def _dot_tn(a, b):
    return lax.dot_general(a, b, TN_DIMS, preferred_element_type=F32)


def _split_dot(a, b):
    hi = a.astype(BF16)
    lo = (a - hi.astype(F32)).astype(BF16)
    return _dot(hi, b) + _dot(lo, b)


def _sigmoid(z):
    return 1.0 / (1.0 + jnp.exp(-z))


def _norm(x):
    r = lax.rsqrt(jnp.mean(x * x, axis=-1, keepdims=True) + EPS)
    return x * r, r


def _modulate(x, g, mod_ref, k):
    n, _ = _norm(x)
    shift = mod_ref[3 * k:3 * k + 1, :]
    scale = mod_ref[3 * k + 1:3 * k + 2, :]
    return n * g * (1.0 + scale) + shift


def _modulate_bwd(dh, x, g, mod_ref, k):
    n, r = _norm(x)
    scale = mod_ref[3 * k + 1:3 * k + 2, :]
    dshift = jnp.sum(dh, axis=0, keepdims=True)
    dscale = jnp.sum(dh * n * g, axis=0, keepdims=True)
    dg = jnp.sum(dh * n * (1.0 + scale), axis=0, keepdims=True)
    dn = dh * g * (1.0 + scale)
    dx = r * (dn - n * jnp.mean(dn * n, axis=-1, keepdims=True))
    return dx, dshift, dscale, dg


class _Exchange:
    def __init__(self, arrays, scatter, relay=False, chips=None):
        assert not (scatter and relay)
        self.arrays = list(arrays)
        self.scatter = scatter
        self.relay = relay
        self.n = len(self.arrays)
        self.chips = list(chips) if chips is not None else [False] * self.n
        assert scatter or not any(self.chips)
        self.out_shape = [
            jax.ShapeDtypeStruct((N_DEV // 2 if ch else N_DEV,) + tuple(a.shape[1:] if scatter else a.shape), a.dtype)
            for a, ch in zip(self.arrays, self.chips)]
        n_remote = self.n * (N_DEV - 1)
        self.scratch_shapes = [pltpu.SemaphoreType.DMA((n_remote,)), pltpu.SemaphoreType.DMA((n_remote,)),
                               pltpu.SemaphoreType.DMA((self.n,))]

    def _copies(self, in_refs, out_refs, sems):
        send_sems, recv_sems, local_sems = sems
        x, y, c = lax.axis_index("x"), lax.axis_index("y"), lax.axis_index("c")
        me = 4 * x + 2 * y + c
        local, remote, relayed = [], {}, {}
        for a in range(self.n):
            if self.chips[a]:
                mine = 2 * x + y
                local.append(pltpu.make_async_copy(in_refs[a].at[mine], out_refs[a].at[mine], local_sems.at[a]))
                for k in (2, 4, 6):
                    px = 1 - x if k & 4 else x
                    py = 1 - y if k & 2 else y
                    sem = a * (N_DEV - 1) + k - 1
                    remote[a, k] = pltpu.make_async_remote_copy(
                        src_ref=in_refs[a].at[2 * px + py], dst_ref=out_refs[a].at[mine],
                        send_sem=send_sems.at[sem], recv_sem=recv_sems.at[sem],
                        device_id=(px, py, c), device_id_type=pl.DeviceIdType.MESH)
                continue
            src = in_refs[a].at[me] if self.scatter else in_refs[a]
            local.append(pltpu.make_async_copy(src, out_refs[a].at[me], local_sems.at[a]))
            for k in range(1, N_DEV):
                px = 1 - x if k & 4 else x
                py = 1 - y if k & 2 else y
                pc = 1 - c if k & 1 else c
                sem = a * (N_DEV - 1) + k - 1
                if self.relay and k & 1 and k > 1:
                    slot = 4 * px + 2 * py + c
                    relayed[a, k] = pltpu.make_async_remote_copy(
                        src_ref=out_refs[a].at[slot], dst_ref=out_refs[a].at[slot],
                        send_sem=send_sems.at[sem], recv_sem=recv_sems.at[sem],
                        device_id=(x, y, 1 - c), device_id_type=pl.DeviceIdType.MESH)
                    continue
                src = in_refs[a].at[4 * px + 2 * py + pc] if self.scatter else in_refs[a]
                remote[a, k] = pltpu.make_async_remote_copy(
                    src_ref=src, dst_ref=out_refs[a].at[me],
                    send_sem=send_sems.at[sem], recv_sem=recv_sems.at[sem],
                    device_id=(px, py, pc), device_id_type=pl.DeviceIdType.MESH)
        return local, remote, relayed

    def start(self, in_refs, out_refs, sems):
        local, remote, _ = self._copies(in_refs, out_refs, sems)
        for cp in local + list(remote.values()):
            cp.start()

    def wait(self, in_refs, out_refs, sems):
        local, remote, relayed = self._copies(in_refs, out_refs, sems)
        for (a, k), cp in relayed.items():
            remote[a, k - 1].wait_recv()
            cp.start()
        for (a, k), cp in remote.items():
            if (a, k + 1) not in relayed:
                cp.wait_recv()
        for cp in relayed.values():
            cp.wait_recv()
        for cp in list(remote.values()) + list(relayed.values()):
            cp.wait_send()
        for cp in local:
            cp.wait()


def _call(body, *, name, args, in_specs, out_specs, out_shape, scratch_shapes=(), grid=(),
          params=None, exchange=None):
    n_in, n_out = len(args), len(out_shape)
    if exchange is None:
        outs = pl.pallas_call(
            body, name=name, grid=grid, in_specs=list(in_specs), out_specs=list(out_specs),
            out_shape=list(out_shape), scratch_shapes=list(scratch_shapes), compiler_params=params,
        )(*args)
        return list(outs), []
    n_ex = exchange.n

    def wrapped(*refs):
        ins, refs = refs[:n_in], refs[n_in:]
        ex_in, refs = refs[:n_ex], refs[n_ex:]
        outs, refs = refs[:n_out], refs[n_out:]
        ex_out, refs = refs[:n_ex], refs[n_ex:]
        scratch, sems = refs[:len(refs) - 3], refs[len(refs) - 3:]
        if not grid:
            exchange.start(ex_in, ex_out, sems)
            body(*ins, *outs, *scratch)
            exchange.wait(ex_in, ex_out, sems)
            return
        first = functools.reduce(jnp.logical_and, [pl.program_id(a) == 0 for a in range(len(grid))])
        last = functools.reduce(jnp.logical_and, [pl.program_id(a) == grid[a] - 1 for a in range(len(grid))])

        @pl.when(first)
        def _():
            exchange.start(ex_in, ex_out, sems)

        body(*ins, *outs, *scratch)

        @pl.when(last)
        def _():
            exchange.wait(ex_in, ex_out, sems)

    any_spec = pl.BlockSpec(memory_space=pl.ANY)
    outs = pl.pallas_call(
        wrapped, name=name, grid=grid,
        in_specs=list(in_specs) + [any_spec] * n_ex, out_specs=list(out_specs) + [any_spec] * n_ex,
        out_shape=list(out_shape) + exchange.out_shape,
        scratch_shapes=list(scratch_shapes) + exchange.scratch_shapes, compiler_params=params,
    )(*args, *exchange.arrays)
    return list(outs[:n_out]), list(outs[n_out:])


def _exchange(arrays, scatter, name, relay=False, chips=None):
    return _call(lambda: None, name=name, args=(), in_specs=(), out_specs=(), out_shape=(),
                 exchange=_Exchange(arrays, scatter, relay, chips))[1]


def _first_exchange(c_pad, shards, w, b):
    rows, d = c_pad.shape
    cols = w.shape[1]
    ex_c = _Exchange([c_pad], False)
    ex_w = _Exchange(shards, False, relay=True)
    ex_m = _Exchange([jax.ShapeDtypeStruct((N_DEV * rows, cols), F32)], False)
    n_w = ex_w.n

    def body(*refs):
        c_ref, w_refs, wa_ref, b_ref = refs[0], refs[1:1 + n_w], refs[1 + n_w], refs[2 + n_w]
        outs = refs[3 + n_w:]
        cg_ref, wg_refs, mg_ref = outs[0], outs[1:1 + n_w], outs[1 + n_w]
        scratch = outs[2 + n_w:]
        sems_c, sems_w, sems_m, c_vm, m_vm = scratch[0:3], scratch[3:6], scratch[6:9], scratch[9], scratch[10]
        ex_c.start([c_ref], [cg_ref], sems_c)
        ex_c.wait([c_ref], [cg_ref], sems_c)
        pltpu.sync_copy(cg_ref, c_vm)
        cv = c_vm[...].reshape(N_DEV * rows, d)
        s = (cv * _sigmoid(cv)).astype(BF16)
        m_vm[...] = _dot(s, wa_ref[...].astype(BF16)) + b_ref[...]
        ex_m.start([m_vm], [mg_ref], sems_m)
        ex_w.start(w_refs, wg_refs, sems_w)
        ex_m.wait([m_vm], [mg_ref], sems_m)
        ex_w.wait(w_refs, wg_refs, sems_w)

    any_spec = pl.BlockSpec(memory_space=pl.ANY)
    vmem_spec = pl.BlockSpec(memory_space=pltpu.VMEM)
    outs = pl.pallas_call(
        body, name="first_exchange",
        in_specs=[any_spec] * (1 + n_w) + [vmem_spec, vmem_spec],
        out_specs=[any_spec] * (2 + n_w),
        out_shape=ex_c.out_shape + ex_w.out_shape + ex_m.out_shape,
        scratch_shapes=ex_c.scratch_shapes + ex_w.scratch_shapes + ex_m.scratch_shapes
        + [pltpu.VMEM((N_DEV, rows, d), F32), pltpu.VMEM((N_DEV * rows, cols), F32)],
        compiler_params=pltpu.CompilerParams(vmem_limit_bytes=VMEM_LIMIT_BYTES),
    )(c_pad, *shards, w, b)
    return outs[0], outs[1 + n_w], list(outs[1:1 + n_w])


def _ada_bwd(c_all, dmod_cols, dmod_all):
    def body(c_ref, dc_ref, da_ref, gw_ref, gb_ref):
        cv = c_ref[...]
        s = cv * _sigmoid(cv)
        gw_ref[...] = lax.dot_general(s, dc_ref[...], TN_DIMS, preferred_element_type=F32,
                                      precision=lax.Precision.HIGHEST)
        gb_ref[...] = jnp.sum(da_ref[...], axis=0, keepdims=True)

    return pl.pallas_call(
        body, name="ada_bwd",
        out_shape=(jax.ShapeDtypeStruct((c_all.shape[1], dmod_cols.shape[1]), F32),
                   jax.ShapeDtypeStruct((1, dmod_all.shape[1]), F32)),
        compiler_params=pltpu.CompilerParams(vmem_limit_bytes=VMEM_LIMIT_BYTES),
    )(c_all, dmod_cols, dmod_all)


def _side_by_side(w_ref):
    return jnp.concatenate([w_ref[s] for s in range(w_ref.shape[0])], axis=1)


def _stacked(w_ref):
    return jnp.concatenate([w_ref[s] for s in range(w_ref.shape[0])], axis=0)


def _loss_tile(x, target, g, acc_ref):
    d = x.shape[1]
    n, r = _norm(x)
    err = n * g - target
    dy = err * (1.0 / d)
    acc_ref[0:1, :] += jnp.sum(err * err, axis=0, keepdims=True)
    acc_ref[1:2, :] += jnp.sum(dy * n, axis=0, keepdims=True)
    dn = dy * g
    return r * (dn - n * jnp.mean(dn * n, axis=-1, keepdims=True))


def _ffn_fwd(x, mod, g, wg, wu, wd, k, tm, exchange=None, head=None):
    t, d = x.shape
    ns, _, fs = wg.shape
    nt = t // tm
    tpb = nt // mod.shape[0]
    rows = tm // FFN_CHUNKS
    extra = list(head) if head is not None else []

    def body(x_ref, mod_ref, g_ref, wg_ref, wu_ref, wd_ref, *rest):
        if head is not None:
            t_ref, gf_ref, xo_ref, f_ref, gg_ref, uu_ref, head_ref, h_sc, acc = rest
        else:
            xo_ref, f_ref, gg_ref, uu_ref, h_sc, acc = rest
        i, j = pl.program_id(0), pl.program_id(1)

        @pl.when(j == 0)
        def _():
            h_sc[...] = _modulate(x_ref[...], g_ref[...], mod_ref, k).astype(BF16)
            acc[...] = jnp.zeros_like(acc)

        chunks = [pl.ds(c * rows, rows) for c in range(FFN_CHUNKS)]
        wg, wu, wd = _side_by_side(wg_ref), _side_by_side(wu_ref), _stacked(wd_ref)
        gates, ups = [], []
        for rs in chunks:
            h = h_sc[rs, :]
            gates.append(_dot(h, wg))
            ups.append(_dot(h, wu))
        acts = [(g * _sigmoid(g) * u).astype(BF16) for g, u in zip(gates, ups)]
        for rs, g, u in zip(chunks, gates, ups):
            for s in range(SHARD_GROUP):
                gg_ref[s, rs, :] = g[:, s * fs:(s + 1) * fs].astype(BF16)
                uu_ref[s, rs, :] = u[:, s * fs:(s + 1) * fs].astype(BF16)
        downs = [_dot(a, wd) for a in acts]
        for rs, dn in zip(chunks, downs):
            acc[rs, :] += dn

        @pl.when(j == ns // SHARD_GROUP - 1)
        def _():
            f = acc[...]
            f_ref[...] = f.astype(BF16)
            xo = x_ref[...] + 0.5 * mod_ref[3 * k + 2:3 * k + 3, :] * f
            if head is None:
                xo_ref[...] = xo
            else:
                @pl.when(i == 0)
                def _():
                    head_ref[...] = jnp.zeros_like(head_ref)

                xo_ref[...] = _loss_tile(xo, t_ref[...], gf_ref[...], head_ref)

    tok = pl.BlockSpec((tm, d), lambda i, j: (i, 0))
    row = pl.BlockSpec((1, d), lambda i, j: (0, 0))
    hid = pl.BlockSpec((SHARD_GROUP, tm, fs), lambda i, j: (j, i, 0))
    head_specs = [_once(tok), row] if head is not None else []
    head_out = [pl.BlockSpec((8, d), lambda i, j: (0, 0))] if head is not None else []
    head_shape = [jax.ShapeDtypeStruct((8, d), F32)] if head is not None else []
    return _call(
        body, name=f"ffn_fwd{k}", grid=(nt, ns // SHARD_GROUP), args=(x, mod, g, wg, wu, wd, *extra),
        in_specs=[tok,
                  pl.BlockSpec((None, N_MOD, d), lambda i, j: (i // tpb, 0, 0)),
                  row,
                  pl.BlockSpec((SHARD_GROUP, d, fs), lambda i, j: (j, 0, 0)),
                  pl.BlockSpec((SHARD_GROUP, d, fs), lambda i, j: (j, 0, 0)),
                  pl.BlockSpec((SHARD_GROUP, fs, d), lambda i, j: (j, 0, 0))] + head_specs,
        out_specs=[tok, tok, hid, hid] + head_out,
        out_shape=[jax.ShapeDtypeStruct((t, d), F32), jax.ShapeDtypeStruct((t, d), BF16),
                   jax.ShapeDtypeStruct((ns, t, fs), BF16), jax.ShapeDtypeStruct((ns, t, fs), BF16)]
        + head_shape,
        scratch_shapes=[pltpu.VMEM((tm, d), BF16), pltpu.VMEM((tm, d), F32)],
        params=_params("arbitrary", "arbitrary"), exchange=exchange)


def _ffn_bwd(dxo, x, f, mod, g, gate, up, wg, wu, wd, k, tm, exchange=None):
    t, d = x.shape
    ns, _, fs = wg.shape
    nt = t // tm
    nb = mod.shape[0]
    tpb = nt // nb
    rows = tm // FFN_CHUNKS

    def body(dxo_ref, x_ref, f_ref, mod_ref, g_ref, gg_ref, uu_ref, wg_ref, wu_ref, wd_ref,
             dx_ref, dgg_ref, duu_ref, act_ref, h_ref, df_ref, dmod_ref, dg_ref, acc):
        i, j = pl.program_id(0), pl.program_id(1)

        @pl.when(j == 0)
        def _():
            df = 0.5 * mod_ref[3 * k + 2:3 * k + 3, :] * dxo_ref[...]
            df_ref[...] = df.astype(BF16)
            h_ref[...] = _modulate(x_ref[...], g_ref[...], mod_ref, k).astype(BF16)
            acc[...] = jnp.zeros_like(acc)

        chunks = [pl.ds(c * rows, rows) for c in range(FFN_CHUNKS)]
        group = range(SHARD_GROUP)
        wg, wu, wd = _side_by_side(wg_ref), _side_by_side(wu_ref), _stacked(wd_ref)
        dacts = [_dot_nt(df_ref[rs, :], wd) for rs in chunks]
        dgates, dups = [], []
        for rs, dact in zip(chunks, dacts):
            gv = jnp.concatenate([gg_ref[s, rs, :] for s in group], axis=1).astype(F32)
            uv = jnp.concatenate([uu_ref[s, rs, :] for s in group], axis=1).astype(F32)
            sig = _sigmoid(gv)
            s_act = gv * sig
            act = (s_act * uv).astype(BF16)
            for s in group:
                act_ref[s, rs, :] = act[:, s * fs:(s + 1) * fs]
            dups.append((dact * s_act).astype(BF16))
            dgates.append((dact * uv * (sig * (1.0 + gv * (1.0 - sig)))).astype(BF16))
        dhs = [_dot_nt(dg, wg) + _dot_nt(du, wu) for dg, du in zip(dgates, dups)]
        for rs, dg, du, dh in zip(chunks, dgates, dups, dhs):
            for s in group:
                dgg_ref[s, rs, :] = dg[:, s * fs:(s + 1) * fs]
                duu_ref[s, rs, :] = du[:, s * fs:(s + 1) * fs]
            acc[rs, :] += dh

        @pl.when(j == ns // SHARD_GROUP - 1)
        def _():
            dx, dshift, dscale, dg = _modulate_bwd(acc[...], x_ref[...], g_ref[...], mod_ref, k)
            dxo_v = dxo_ref[...]
            dx_ref[...] = dxo_v + dx
            dgt = jnp.sum(0.5 * f_ref[...].astype(F32) * dxo_v, axis=0, keepdims=True)

            @pl.when(i % tpb == 0)
            def _():
                dmod_ref[...] = jnp.zeros_like(dmod_ref)

            @pl.when(i == 0)
            def _():
                dg_ref[...] = jnp.zeros_like(dg_ref)

            dmod_ref[0:1, :] += dshift
            dmod_ref[1:2, :] += dscale
            dmod_ref[2:3, :] += dgt
            dg_ref[0:1, :] += dg

    tok = pl.BlockSpec((tm, d), lambda i, j: (i, 0))
    hid = pl.BlockSpec((SHARD_GROUP, tm, fs), lambda i, j: (j, i, 0))
    return _call(
        body, name=f"ffn_bwd{k}", grid=(nt, ns // SHARD_GROUP), args=(dxo, x, f, mod, g, gate, up, wg, wu, wd),
        in_specs=[tok, tok, tok,
                  pl.BlockSpec((None, N_MOD, d), lambda i, j: (i // tpb, 0, 0)),
                  pl.BlockSpec((1, d), lambda i, j: (0, 0)),
                  hid, hid,
                  pl.BlockSpec((SHARD_GROUP, d, fs), lambda i, j: (j, 0, 0)),
                  pl.BlockSpec((SHARD_GROUP, d, fs), lambda i, j: (j, 0, 0)),
                  pl.BlockSpec((SHARD_GROUP, fs, d), lambda i, j: (j, 0, 0))],
        out_specs=[tok, hid, hid, hid, tok, tok,
                   pl.BlockSpec((None, 8, d), lambda i, j: (i // tpb, 0, 0)),
                   pl.BlockSpec((8, d), lambda i, j: (0, 0))],
        out_shape=[jax.ShapeDtypeStruct((t, d), F32),
                   jax.ShapeDtypeStruct((ns, t, fs), BF16), jax.ShapeDtypeStruct((ns, t, fs), BF16),
                   jax.ShapeDtypeStruct((ns, t, fs), BF16),
                   jax.ShapeDtypeStruct((t, d), BF16), jax.ShapeDtypeStruct((t, d), BF16),
                   jax.ShapeDtypeStruct((nb, 8, d), F32), jax.ShapeDtypeStruct((8, d), F32)],
        scratch_shapes=[pltpu.VMEM((tm, d), F32)],
        params=_params("arbitrary", "arbitrary"), exchange=exchange)


def _mm_tn(a, b, a_spec, b_spec, out_shape, n_tiles, name, exchange=None, keep_transposed=False,
           pair_reduce=False):
    n_out = out_shape[0]
    block = tuple(out_shape[1:])
    last = n_tiles - 1
    flip = block[0] > block[1]
    if flip:
        block = block[::-1]
    if flip and keep_transposed:
        flip_back, out_shape = False, (n_out,) + block
    else:
        flip_back = flip
    full_shape = tuple(out_shape)
    n_pairs = n_out // 2
    if pair_reduce:
        out_shape = (n_pairs,) + full_shape[1:]

    def body(a_ref, b_ref, o_ref, acc, *pair):
        i, j = pl.program_id(0), pl.program_id(1)
        prod = _dot_tn(b_ref[...], a_ref[...]) if flip else _dot_tn(a_ref[...], b_ref[...])
        full_ref = pair[0] if pair_reduce else o_ref

        @pl.when(i == 0)
        def _():
            acc[j] = prod

        @pl.when(i > 0)
        def _():
            acc[j] += prod

        @pl.when(i == last)
        def _():
            total = acc[j]
            full_ref[j] = (total.T if flip_back else total).astype(BF16)

        if pair_reduce:
            _, landed, send_sems, recv_sems = pair

            @pl.when(jnp.logical_and(i == last, j == n_out - 1))
            def _():
                x, y, c = lax.axis_index("x"), lax.axis_index("y"), lax.axis_index("c")
                copies = [pltpu.make_async_remote_copy(
                    src_ref=full_ref.at[2 * q + 1 - c], dst_ref=landed.at[q],
                    send_sem=send_sems.at[q], recv_sem=recv_sems.at[q],
                    device_id=(x, y, 1 - c), device_id_type=pl.DeviceIdType.MESH) for q in range(n_pairs)]
                for cp in copies:
                    cp.start()
                for q, cp in enumerate(copies):
                    cp.wait_recv()
                    o_ref[q] = (full_ref[2 * q + c].astype(F32) + landed[q].astype(F32)).astype(BF16)
                for cp in copies:
                    cp.wait_send()

    scratch = [pltpu.VMEM((n_out,) + block, F32)]
    if pair_reduce:
        scratch += [pltpu.VMEM(full_shape, BF16), pltpu.VMEM(out_shape, BF16),
                    pltpu.SemaphoreType.DMA((n_pairs,)), pltpu.SemaphoreType.DMA((n_pairs,))]
    outs, sent = _call(
        body, name=name, grid=(n_tiles, n_out), args=(a, b), in_specs=[a_spec, b_spec],
        out_specs=[pl.BlockSpec(out_shape, lambda i, j: (0,) * len(out_shape))],
        out_shape=[jax.ShapeDtypeStruct(out_shape, BF16)],
        scratch_shapes=scratch,
        params=_params("arbitrary", "arbitrary"), exchange=exchange)
    return (outs[0], sent) if exchange is not None else outs[0]


def _ffn_weight_grads(h, dgate, dup, act, df, tm, tag, stream=False, first=None):
    t, d = h.shape
    ns, _, fs = dgate.shape
    nt = t // tm
    tok = pl.BlockSpec((tm, d), lambda i, j: (i, 0))
    hid = pl.BlockSpec((None, tm, fs), lambda i, j: (j, i, 0))
    if not stream:
        gwg = _mm_tn(h, dgate, tok, hid, (ns, d, fs), nt, f"grad_wg{tag}", keep_transposed=True)
        gwu = _mm_tn(h, dup, tok, hid, (ns, d, fs), nt, f"grad_wu{tag}", keep_transposed=True)
        gwd = _mm_tn(act, df, hid, tok, (ns, fs, d), nt, f"grad_wd{tag}")
        return gwg, gwu, gwd
    gwg, brought = _mm_tn(h, dgate, tok, hid, (ns, d, fs), nt, f"grad_wg{tag}", first,
                          keep_transposed=True, pair_reduce=True)
    gwu, sent_g = _mm_tn(h, dup, tok, hid, (ns, d, fs), nt, f"grad_wu{tag}",
                         _Exchange([gwg], True, chips=[True]), keep_transposed=True, pair_reduce=True)
    gwd, sent_u = _mm_tn(act, df, hid, tok, (ns, fs, d), nt, f"grad_wd{tag}",
                         _Exchange([gwu], True, chips=[True]), pair_reduce=True)
    return sent_g[0], sent_u[0], gwd, brought


def _stage_shape(rows, cols):
    return pltpu.VMEM((cols // LANES, rows, LANES), F32)


def _stage(value, stage_ref):
    for k in range(stage_ref.shape[0]):
        stage_ref[k] = value[:, k * LANES:(k + 1) * LANES]


def _to_residue_rows(stage_ref, dst_ref, dil):
    rows = stage_ref.shape[1] // dil
    for r in range(dil):
        for k in range(stage_ref.shape[0]):
            dst_ref[r, :, k * LANES:(k + 1) * LANES] = (
                stage_ref.at[k][pl.ds(r, rows, stride=dil), :].astype(dst_ref.dtype))


def _from_residue_rows(src_ref, stage_ref, dil):
    rows = stage_ref.shape[1] // dil
    chunks = range(stage_ref.shape[0])
    for r in range(dil):
        for k in chunks:
            stage_ref.at[k][pl.ds(r, rows, stride=dil), :] = src_ref[r, :, k * LANES:(k + 1) * LANES].astype(F32)
    return jnp.concatenate([stage_ref[k] for k in chunks], axis=1)


def _residue_shape(nb, seq, width, dil, dtype):
    return jax.ShapeDtypeStruct((nb, dil, seq // dil, width), dtype)


def _residue_spec(tm, tpb, cols, dil, col_block):
    return pl.BlockSpec((None, dil, tm // dil, cols),
                        lambda i, *rest: (i // tpb, 0, i % tpb, col_block(i, *rest)))


def _qkv_fwd(x, mod, g, win, tm, exchange=None):
    t, d = x.shape
    ns, _, cs = win.shape
    nt = t // tm
    nb = mod.shape[0]
    tpb = nt // nb
    seq = t // nb
    width = ns * cs // 2
    cs, ns = cs * SHARD_GROUP, ns // SHARD_GROUP
    half = ns // 2
    n_res = len(DILATIONS) - 1

    def body(x_ref, mod_ref, g_ref, w_ref, sb_ref, dil_ref, *rest):
        res_refs, h_ref, sc = rest[:n_res], rest[n_res], rest[n_res + 1]
        j = pl.program_id(1)

        @pl.when(j == 0)
        def _():
            h_ref[...] = _modulate(x_ref[...], g_ref[...], mod_ref, 1).astype(BF16)

        res = _dot(h_ref[...], _side_by_side(w_ref))

        @pl.when(j < half)
        def _():
            sb_ref[...] = res.astype(BF16)

        @pl.when(j >= half)
        def _():
            dil_ref[...] = res.astype(BF16)
            _stage(res, sc)
            for ref, dil in zip(res_refs, DILATIONS[1:]):
                _to_residue_rows(sc, ref, dil)

    def dil_col(i, j):
        return jnp.maximum(j - half, 0)

    tok = pl.BlockSpec((tm, d), lambda i, j: (i, 0))
    wide = jax.ShapeDtypeStruct((t, width), BF16)
    outs, got = _call(
        body, name="qkv_fwd", grid=(nt, ns), args=(x, mod, g, win),
        in_specs=[tok,
                  pl.BlockSpec((None, N_MOD, d), lambda i, j: (i // tpb, 0, 0)),
                  pl.BlockSpec((1, d), lambda i, j: (0, 0)),
                  pl.BlockSpec((SHARD_GROUP, d, cs // SHARD_GROUP), lambda i, j: (j, 0, 0))],
        out_specs=[pl.BlockSpec((tm, cs), lambda i, j: (i, jnp.minimum(j, half - 1))),
                   pl.BlockSpec((tm, cs), lambda i, j: (i, dil_col(i, j)))]
        + [_residue_spec(tm, tpb, cs, dil, dil_col) for dil in DILATIONS[1:]] + [tok],
        out_shape=[wide, wide] + [_residue_shape(nb, seq, width, dil, BF16) for dil in DILATIONS[1:]]
        + [jax.ShapeDtypeStruct((t, d), BF16)],
        scratch_shapes=[_stage_shape(tm, cs)],
        params=_params("arbitrary", "arbitrary"), exchange=exchange)
    qkv_dil = [outs[1]] + [a.reshape(t, width) for a in outs[2:2 + n_res]]
    return (outs[0], qkv_dil, outs[-1]), got


def _qkv_bwd(dqkv, dxo, x, mod, g, win, tm, exchange=None):
    t, d = x.shape
    ns, _, cs = win.shape
    nt = t // tm
    nb = mod.shape[0]
    tpb = nt // nb
    cs, ns = cs * SHARD_GROUP, ns // SHARD_GROUP

    def body(dq_ref, dxo_ref, x_ref, mod_ref, g_ref, w_ref, dx_ref, dmod_ref, dg_ref, acc):
        i, j = pl.program_id(0), pl.program_id(1)

        @pl.when(j == 0)
        def _():
            acc[...] = jnp.zeros_like(acc)

        acc[...] += _dot_nt(dq_ref[...], _side_by_side(w_ref))

        @pl.when(j == ns - 1)
        def _():
            dx, dshift, dscale, dg = _modulate_bwd(acc[...], x_ref[...], g_ref[...], mod_ref, 1)
            dx_ref[...] = dxo_ref[...] + dx

            @pl.when(i % tpb == 0)
            def _():
                dmod_ref[...] = jnp.zeros_like(dmod_ref)

            @pl.when(i == 0)
            def _():
                dg_ref[...] = jnp.zeros_like(dg_ref)

            dmod_ref[0:1, :] += dshift
            dmod_ref[1:2, :] += dscale
            dg_ref[0:1, :] += dg

    tok = pl.BlockSpec((tm, d), lambda i, j: (i, 0))
    return _call(
        body, name="qkv_bwd", grid=(nt, ns), args=(dqkv, dxo, x, mod, g, win),
        in_specs=[pl.BlockSpec((tm, cs), lambda i, j: (i, j)), tok, tok,
                  pl.BlockSpec((None, N_MOD, d), lambda i, j: (i // tpb, 0, 0)),
                  pl.BlockSpec((1, d), lambda i, j: (0, 0)),
                  pl.BlockSpec((SHARD_GROUP, d, cs // SHARD_GROUP), lambda i, j: (j, 0, 0))],
        out_specs=[tok,
                   pl.BlockSpec((None, 8, d), lambda i, j: (i // tpb, 0, 0)),
                   pl.BlockSpec((8, d), lambda i, j: (0, 0))],
        out_shape=[jax.ShapeDtypeStruct((t, d), F32),
                   jax.ShapeDtypeStruct((nb, 8, d), F32), jax.ShapeDtypeStruct((8, d), F32)],
        scratch_shapes=[pltpu.VMEM((tm, d), F32)],
        params=_params("arbitrary", "arbitrary"), exchange=exchange)


def _own_lanes():
    lane = lax.broadcasted_iota(jnp.int32, (1, LANES), 1)
    return [lane < HEAD_DIM, lane >= HEAD_DIM]


def _pair_tiles(a):
    return [a[:, (h // 2) * LANES:(h // 2 + 1) * LANES] for h in range(a.shape[1] // HEAD_DIM)]


def _own_tiles(a, own):
    return [jnp.where(own[h % 2], tile, jnp.zeros_like(tile)) for h, tile in enumerate(_pair_tiles(a))]


def _merge_tiles(per_head, own):
    return jnp.concatenate([jnp.where(own[0], per_head[h], per_head[h + 1])
                            for h in range(0, len(per_head), 2)], axis=1)


def _scaled(q):
    return (q.astype(F32) * (HEAD_DIM ** -0.5)).astype(BF16)


def _sb_logits(qh, kh, tri, causal):
    zs = [_dot_nt(q, k) for q, k in zip(qh, kh)]
    es = [jnp.exp(-jnp.abs(z)) for z in zs]
    log_nots = [-(jnp.maximum(z, 0.0) + jnp.log(1.0 + e)) for z, e in zip(zs, es)]
    if causal is not None:
        log_nots = [jnp.where(causal, ln, 0.0) for ln in log_nots]
    return zs, es, [_split_dot(ln, tri) for ln in log_nots]


def _sb_masks():
    rows = lax.broadcasted_iota(jnp.int32, (SB_BLOCK, SB_BLOCK), 0)
    cols = lax.broadcasted_iota(jnp.int32, (SB_BLOCK, SB_BLOCK), 1)
    return (rows >= cols).astype(BF16), (rows <= cols).astype(BF16), cols < rows


def _sb_fwd(qkv, nb, seq, exchange=None):
    t = qkv.shape[0]
    n_pairs = (qkv.shape[1] // 3) // SB_WIDTH
    tb = SB_BLOCK
    n_blk = seq // tb

    def body(q_ref, k_ref, v_ref, o_ref, c_ref):
        tri, _, causal = _sb_masks()
        own = _own_lanes()

        def key_blocks(qh, kjs, carry, mask):
            nh = SB_HEADS
            chains = range(nh * len(kjs))
            kss = [pl.multiple_of(kj * tb, tb) for kj in kjs]
            kh = [tile for ks in kss for tile in _pair_tiles(k_ref[pl.ds(ks, tb), :])]
            vh = [tile for ks in kss for tile in _pair_tiles(v_ref[pl.ds(ks, tb), :])]
            zs, _, suffixes = _sb_logits(qh * len(kjs), kh, tri, mask)
            right = []
            for c in chains:
                right.append(carry[c][1] if c < nh else right[c - nh] + suffixes[c - nh][:, 0:1])
            ws = [jnp.exp(zs[c] + suffixes[c] + right[c]) for c in chains]
            if mask is not None:
                ws = [jnp.where(mask, w, 0.0) for w in ws]
            pv = [_dot(ws[c].astype(BF16), vh[c]) for c in chains]
            last = (len(kjs) - 1) * nh
            return tuple((carry[h][0] + sum(pv[h::nh]), right[last + h] + suffixes[last + h][:, 0:1])
                         for h in range(nh))

        def query_block(qi, _):
            qs = pl.multiple_of(qi * tb, tb)
            qh = _own_tiles(_scaled(q_ref[pl.ds(qs, tb), :]), own)
            zero = (jnp.zeros((tb, LANES), F32), jnp.zeros((tb, 1), F32))
            carry = key_blocks(qh, [qi], (zero,) * SB_HEADS, causal)
            carry = lax.fori_loop(
                0, qi // 2, lambda p, cr: key_blocks(qh, [qi - 1 - 2 * p, qi - 2 - 2 * p], cr, None), carry)
            carry = lax.fori_loop(0, qi % 2, lambda _, cr: key_blocks(qh, [0], cr, None), carry)
            o_ref[pl.ds(qs, tb), :] = _merge_tiles([cr[0] for cr in carry], own)
            c_ref[pl.ds(qs, tb), :] = _merge_tiles([jnp.broadcast_to(cr[1], (tb, LANES)) for cr in carry], own)
            return 0

        lax.fori_loop(0, n_blk, query_block, 0)

    def spec(offset):
        return pl.BlockSpec((seq, SB_WIDTH), lambda b, p: (b, offset + p))

    out = jax.ShapeDtypeStruct((t, n_pairs * SB_WIDTH), F32)
    return _call(
        body, name="sb_fwd", grid=(nb, n_pairs), args=(qkv, qkv, qkv),
        in_specs=[spec(0), spec(n_pairs), spec(2 * n_pairs)],
        out_specs=[spec(0), spec(0)], out_shape=[out, out],
        params=_params("arbitrary", "arbitrary"), exchange=exchange)


def _sb_bwd(qkv, do, csum, nb, seq, exchange=None):
    t = qkv.shape[0]
    n_pairs = (qkv.shape[1] // 3) // SB_WIDTH
    tb = SB_BLOCK
    n_blk = seq // tb
    scale = HEAD_DIM ** -0.5

    def body(q_ref, k_ref, v_ref, do_ref, c_ref, dq_ref, dk_ref, dv_ref, dkt_acc, dvt_acc):
        tri, tri_prefix, causal = _sb_masks()
        own = _own_lanes()
        dkt_acc[...] = jnp.zeros_like(dkt_acc)
        dvt_acc[...] = jnp.zeros_like(dvt_acc)

        def key_blocks(qh, qth, doh, doth, ch, kjs, carry, mask):
            nh = SB_HEADS
            chains = range(nh * len(kjs))
            kss = [pl.multiple_of(kj * tb, tb) for kj in kjs]
            kh = [tile for ks in kss for tile in _pair_tiles(k_ref[pl.ds(ks, tb), :])]
            vh = [tile for ks in kss for tile in _pair_tiles(v_ref[pl.ds(ks, tb), :])]
            zs, es, suffixes = _sb_logits(qh * len(kjs), kh, tri, mask)
            dws = [_dot_nt(doh[c % nh], vh[c]) for c in chains]
            lefts = []
            for c in chains:
                before = carry[c][1] if c < nh else lefts[c - nh]
                lefts.append(before + suffixes[c][:, 0:1])
            ws = [jnp.exp(zs[c] + suffixes[c] + (ch[c % nh] - lefts[c])) for c in chains]
            if mask is not None:
                ws = [jnp.where(mask, w, 0.0) for w in ws]
            dlws = [ws[c] * dws[c] for c in chains]
            dprefixes = [_split_dot(dlw, tri_prefix) for dlw in dlws]
            dvts = [_dot(doth[c % nh], ws[c].astype(BF16)) for c in chains]
            dlefts, dzbs = [], []
            for c in chains:
                dlefts.append(carry[c][2] if c < nh else dlefts[c - nh] + dprefixes[c - nh][:, tb - 1:tb])
                sig = jnp.where(zs[c] >= 0.0, 1.0, es[c]) * pl.reciprocal(1.0 + es[c], approx=True)
                dz = dlws[c] - sig * (dlefts[c] + dprefixes[c])
                if mask is not None:
                    dz = jnp.where(mask, dz, 0.0)
                dzbs.append(dz.astype(BF16))
            dkts = [_dot(qth[c % nh], dzbs[c]) for c in chains]
            dqs = [_dot(dzbs[c], kh[c]) for c in chains]
            for b, ks in enumerate(kss):
                pairs = range(b * nh, (b + 1) * nh, 2)
                dkt_acc[:, pl.ds(ks, tb)] += jnp.concatenate([dkts[c] + dkts[c + 1] for c in pairs], axis=0)
                dvt_acc[:, pl.ds(ks, tb)] += jnp.concatenate([dvts[c] + dvts[c + 1] for c in pairs], axis=0)
            last = (len(kjs) - 1) * nh
            return tuple((carry[h][0] + sum(dqs[h::nh]), lefts[last + h],
                          dlefts[last + h] + dprefixes[last + h][:, tb - 1:tb]) for h in range(nh))

        def query_block(qi, _):
            qs = pl.multiple_of(qi * tb, tb)
            qh = _own_tiles(_scaled(q_ref[pl.ds(qs, tb), :]), own)
            doh = _own_tiles(do_ref[pl.ds(qs, tb), :], own)
            qth = [a.astype(F32).T.astype(BF16) for a in qh]
            doth = [a.T.astype(BF16) for a in doh]
            doh = [a.astype(BF16) for a in doh]
            cv = c_ref[pl.ds(qs, tb), :]
            ch = [cv[:, h * HEAD_DIM:h * HEAD_DIM + 1] for h in range(SB_HEADS)]
            zero = (jnp.zeros((tb, LANES), F32), jnp.zeros((tb, 1), F32), jnp.zeros((tb, 1), F32))

            def key_block(kjs, cr, mask):
                return key_blocks(qh, qth, doh, doth, ch, kjs, cr, mask)

            carry = lax.fori_loop(0, qi // 2, lambda p, cr: key_block([2 * p, 2 * p + 1], cr, None),
                                  (zero,) * SB_HEADS)
            carry = lax.fori_loop(0, qi % 2, lambda _, cr: key_block([qi - 1], cr, None), carry)
            carry = key_block([qi], carry, causal)
            dq = _merge_tiles([cr[0] for cr in carry], own) * scale
            dq_ref[pl.ds(qs, tb), :] = dq.astype(BF16)
            return 0

        lax.fori_loop(0, n_blk, query_block, 0)
        dk_ref[...] = dkt_acc[...].T.astype(BF16)
        dv_ref[...] = dvt_acc[...].T.astype(BF16)

    def spec(offset):
        return pl.BlockSpec((seq, SB_WIDTH), lambda b, p: (b, offset + p))

    out = jax.ShapeDtypeStruct((t, n_pairs * SB_WIDTH), BF16)
    return _call(
        body, name="sb_bwd", grid=(nb, n_pairs), args=(qkv, qkv, qkv, do, csum),
        in_specs=[spec(0), spec(n_pairs), spec(2 * n_pairs), spec(0), spec(0)],
        out_specs=[spec(0), spec(0), spec(0)],
        out_shape=[out, out, out],
        scratch_shapes=[pltpu.VMEM((SB_WIDTH, seq), F32), pltpu.VMEM((SB_WIDTH, seq), F32)],
        params=_params("arbitrary", "arbitrary"), exchange=exchange)


def _dil_block_scores(qh, kph, kch, bias_ref, has_prev, band_prev, band_cur):
    scale = HEAD_DIM ** -0.5
    heads = range(len(qh))
    no_prev = jnp.where(has_prev, 0.0, NEG_INF)
    zps = [_dot_nt(qh[h], kph[h]) for h in heads]
    zcs = [_dot_nt(qh[h], kch[h]) for h in heads]
    zps = [jnp.where(band_prev, zps[h] * scale + bias_ref[h, :, 0:DIL_BLOCK], NEG_INF) + no_prev for h in heads]
    zcs = [jnp.where(band_cur, zcs[h] * scale + bias_ref[h, :, DIL_BLOCK:2 * DIL_BLOCK], NEG_INF) for h in heads]
    return zps, zcs


def _dil_bands():
    rows = lax.broadcasted_iota(jnp.int32, (DIL_BLOCK, DIL_BLOCK), 0)
    cols = lax.broadcasted_iota(jnp.int32, (DIL_BLOCK, DIL_BLOCK), 1)
    return cols >= rows, cols <= rows


def _dil_fwd(qkv, bias, nb, seq, dil, exchange=None):
    t, width = qkv.shape
    n_pairs = (width // 3) // DIL_WIDTH
    bq = DIL_BLOCK
    n_blk = seq // bq
    per_seq = n_blk // dil
    heads = range(DIL_HEADS)

    def body(q_ref, k_ref, v_ref, bias_ref, o_ref, lse_ref):
        band_prev, band_cur = _dil_bands()
        own = _own_lanes()

        def block(n, _):
            has_prev = (n & (per_seq - 1)) != 0
            qs = pl.multiple_of(n * bq, bq)
            ps = pl.multiple_of(jnp.maximum(n - 1, 0) * bq, bq)
            qh = _own_tiles(q_ref[pl.ds(qs, bq), :], own)
            kp, kc = _pair_tiles(k_ref[pl.ds(ps, bq), :]), _pair_tiles(k_ref[pl.ds(qs, bq), :])
            vp, vc = _pair_tiles(v_ref[pl.ds(ps, bq), :]), _pair_tiles(v_ref[pl.ds(qs, bq), :])
            zps, zcs = _dil_block_scores(qh, kp, kc, bias_ref, has_prev, band_prev, band_cur)
            ms = [jnp.maximum(jnp.max(zps[h], axis=1, keepdims=True), jnp.max(zcs[h], axis=1, keepdims=True))
                  for h in heads]
            eps = [jnp.exp(zps[h] - ms[h]) for h in heads]
            ecs = [jnp.exp(zcs[h] - ms[h]) for h in heads]
            pvs = [_dot(eps[h].astype(BF16), vp[h]) + _dot(ecs[h].astype(BF16), vc[h]) for h in heads]
            dens = [jnp.sum(eps[h], axis=1, keepdims=True) + jnp.sum(ecs[h], axis=1, keepdims=True) for h in heads]
            o_ref[pl.ds(qs, bq), :] = _merge_tiles([pvs[h] / dens[h] for h in heads], own)
            lse_ref[pl.ds(qs, bq), :] = _merge_tiles(
                [jnp.broadcast_to(ms[h] + jnp.log(dens[h]), (bq, LANES)) for h in heads], own)
            return 0

        lax.fori_loop(0, n_blk, block, 0)

    def spec(offset):
        return pl.BlockSpec((seq, DIL_WIDTH), lambda b, p: (b, offset + p))

    out = jax.ShapeDtypeStruct((t, n_pairs * DIL_WIDTH), F32)
    return _call(
        body, name=f"dil_fwd{dil}", grid=(nb, n_pairs), args=(qkv, qkv, qkv, bias),
        in_specs=[spec(0), spec(n_pairs), spec(2 * n_pairs),
                  pl.BlockSpec((DIL_HEADS, bq, 2 * bq), lambda b, p: (p, 0, 0))],
        out_specs=[spec(0), spec(0)], out_shape=[out, out],
        params=_params("arbitrary", "arbitrary"), exchange=exchange)


def _dil_bwd(qkv, bias, do, lse, delta, nb, seq, dil):
    t, width = qkv.shape
    n_pairs = (width // 3) // DIL_WIDTH
    bq = DIL_BLOCK
    n_blk = seq // bq
    per_seq = n_blk // dil
    scale = HEAD_DIM ** -0.5
    heads = range(DIL_HEADS)

    def body(q_ref, k_ref, v_ref, bias_ref, do_ref, lse_ref, dl_ref, dq_ref, dk_ref, dv_ref, db_ref,
             dk_acc, dv_acc):
        band_prev, band_cur = _dil_bands()
        own = _own_lanes()
        dk_acc[...] = jnp.zeros_like(dk_acc)
        dv_acc[...] = jnp.zeros_like(dv_acc)

        @pl.when(pl.program_id(1) == 0)
        def _():
            db_ref[...] = jnp.zeros_like(db_ref)

        def block(n, _):
            has_prev = (n & (per_seq - 1)) != 0
            qs = pl.multiple_of(n * bq, bq)
            ps = pl.multiple_of(jnp.maximum(n - 1, 0) * bq, bq)
            qh = _own_tiles(q_ref[pl.ds(qs, bq), :], own)
            kp, kc = _pair_tiles(k_ref[pl.ds(ps, bq), :]), _pair_tiles(k_ref[pl.ds(qs, bq), :])
            vp, vc = _pair_tiles(v_ref[pl.ds(ps, bq), :]), _pair_tiles(v_ref[pl.ds(qs, bq), :])
            doh = _own_tiles(do_ref[pl.ds(qs, bq), :].astype(BF16), own)
            lse_v, dl_v = lse_ref[pl.ds(qs, bq), :], dl_ref[pl.ds(qs, bq), :]
            zps, zcs = _dil_block_scores(qh, kp, kc, bias_ref, has_prev, band_prev, band_cur)
            dpp = [_dot_nt(doh[h], vp[h]) for h in heads]
            dpc = [_dot_nt(doh[h], vc[h]) for h in heads]
            lse_h = [lse_v[:, h * HEAD_DIM:h * HEAD_DIM + 1] for h in heads]
            dl_h = [dl_v[:, h * HEAD_DIM:h * HEAD_DIM + 1] for h in heads]
            pps = [jnp.exp(zps[h] - lse_h[h]) for h in heads]
            pcs = [jnp.exp(zcs[h] - lse_h[h]) for h in heads]
            dvp = [_dot_tn(pps[h].astype(BF16), doh[h]) for h in heads]
            dvc = [_dot_tn(pcs[h].astype(BF16), doh[h]) for h in heads]
            dzps = [pps[h] * (dpp[h] - dl_h[h]) for h in heads]
            dzcs = [pcs[h] * (dpc[h] - dl_h[h]) for h in heads]
            dzp_b = [(dzps[h] * scale).astype(BF16) for h in heads]
            dzc_b = [(dzcs[h] * scale).astype(BF16) for h in heads]
            dqs = [_dot(dzp_b[h], kp[h]) + _dot(dzc_b[h], kc[h]) for h in heads]
            dkp = [_dot_tn(dzp_b[h], qh[h]) for h in heads]
            dkc = [_dot_tn(dzc_b[h], qh[h]) for h in heads]
            for h in heads:
                db_ref[h, :, 0:bq] += dzps[h]
                db_ref[h, :, bq:2 * bq] += dzcs[h]
            def pair_sums(per_head):
                return jnp.concatenate([per_head[h] + per_head[h + 1] for h in heads[::2]], axis=1)

            dq_ref[pl.ds(qs, bq), :] = _merge_tiles(dqs, own).astype(BF16)
            dk_acc[pl.ds(ps, bq), :] += pair_sums(dkp)
            dk_acc[pl.ds(qs, bq), :] += pair_sums(dkc)
            dv_acc[pl.ds(ps, bq), :] += pair_sums(dvp)
            dv_acc[pl.ds(qs, bq), :] += pair_sums(dvc)
            return 0

        lax.fori_loop(0, n_blk, block, 0)
        dk_ref[...] = dk_acc[...].astype(BF16)
        dv_ref[...] = dv_acc[...].astype(BF16)

    def spec(offset):
        return pl.BlockSpec((seq, DIL_WIDTH), lambda p, b: (b, offset + p))

    bias_spec = pl.BlockSpec((DIL_HEADS, bq, 2 * bq), lambda p, b: (p, 0, 0))
    out = jax.ShapeDtypeStruct((t, n_pairs * DIL_WIDTH), BF16)
    return pl.pallas_call(
        body, name=f"dil_bwd{dil}", grid=(n_pairs, nb),
        in_specs=[spec(0), spec(n_pairs), spec(2 * n_pairs), bias_spec, spec(0), spec(0), spec(0)],
        out_specs=[spec(0), spec(0), spec(0), bias_spec],
        out_shape=[out, out, out, jax.ShapeDtypeStruct(bias.shape, F32)],
        scratch_shapes=[pltpu.VMEM((seq, DIL_WIDTH), F32), pltpu.VMEM((seq, DIL_WIDTH), F32)],
        compiler_params=_params("arbitrary", "arbitrary"),
    )(qkv, qkv, qkv, bias, do, lse, delta)


def _head_blocks(width):
    rows = lax.broadcasted_iota(jnp.int32, (width, width), 0) // HEAD_DIM
    cols = lax.broadcasted_iota(jnp.int32, (width, width), 1) // HEAD_DIM
    return (rows == cols).astype(BF16)


def _head_mean(v, gmat):
    return _split_dot(v, gmat) * (1.0 / HEAD_DIM)


def _residue_views(arrays, nb, seq):
    return [a if dil == 1 else a.reshape(nb, dil, seq // dil, a.shape[1]) for a, dil in zip(arrays, DILATIONS)]


def _mix_out_fwd(osb, ocs, lses, gsb, gdil, wout, x, mod, tm):
    t, d = x.shape
    ds = osb.shape[1]
    nt = t // tm
    nb = mod.shape[0]
    tpb = nt // nb
    seq = t // nb
    n_cfg = len(DILATIONS)

    def body(osb_ref, *refs):
        oc_refs, lse_refs = refs[:n_cfg], refs[n_cfg:2 * n_cfg]
        gsb_ref, gdil_ref, w_ref, x_ref, mod_ref = refs[2 * n_cfg:2 * n_cfg + 5]
        xo_ref, on_ref, m_ref, odil_ref = refs[2 * n_cfg + 5:2 * n_cfg + 9]
        ld_refs = refs[2 * n_cfg + 9:3 * n_cfg + 9]
        stages, sc = refs[3 * n_cfg + 9:]
        ocv, lsev = [oc_refs[0][...]], [lse_refs[0][...]]
        for i, dil in enumerate(DILATIONS[1:]):
            ocv.append(_from_residue_rows(oc_refs[i + 1], stages.at[2 * i], dil))
            lsev.append(_from_residue_rows(lse_refs[i + 1], stages.at[2 * i + 1], dil))
        top = functools.reduce(jnp.maximum, lsev)
        total = top + jnp.log(sum(jnp.exp(l - top) for l in lsev))
        odil = sum(jnp.exp(l - total) * o for o, l in zip(ocv, lsev))
        odil_ref[...] = odil
        ld_refs[0][...] = total
        _stage(total, sc)
        for ref, dil in zip(ld_refs[1:], DILATIONS[1:]):
            _to_residue_rows(sc, ref, dil)
        gm = _head_blocks(ds)
        parts = []
        for o, g_ref in ((osb_ref[...], gsb_ref), (odil, gdil_ref)):
            parts.append(o * lax.rsqrt(_head_mean(o * o, gm) + EPS) * g_ref[...])
        on = jnp.concatenate(parts, axis=1).astype(BF16)
        on_ref[...] = on
        m = _dot(on, w_ref[...])
        m_ref[...] = m
        xo_ref[...] = x_ref[...] + mod_ref[5:6, :] * m

    tok = pl.BlockSpec((tm, d), lambda i: (i, 0))
    hd = pl.BlockSpec((tm, ds), lambda i: (i, 0))
    res = [hd] + [_residue_spec(tm, tpb, ds, dil, lambda i: 0) for dil in DILATIONS[1:]]
    res_shape = [jax.ShapeDtypeStruct((t, ds), F32)] + [_residue_shape(nb, seq, ds, dil, F32) for dil in DILATIONS[1:]]
    gain = pl.BlockSpec((1, ds), lambda i: (0, 0))
    outs = pl.pallas_call(
        body, name="mix_out_fwd", grid=(nt,),
        in_specs=[hd] + res + res + [gain, gain,
                  pl.BlockSpec(wout.shape, lambda i: (0, 0)),
                  tok, pl.BlockSpec((None, N_MOD, d), lambda i: (i // tpb, 0, 0))],
        out_specs=[tok, pl.BlockSpec((tm, 2 * ds), lambda i: (i, 0)), tok, hd] + res,
        out_shape=[jax.ShapeDtypeStruct((t, d), F32), jax.ShapeDtypeStruct((t, 2 * ds), BF16),
                   jax.ShapeDtypeStruct((t, d), F32), jax.ShapeDtypeStruct((t, ds), F32)] + res_shape,
        scratch_shapes=[pltpu.VMEM((2 * (n_cfg - 1), ds // LANES, tm, LANES), F32), _stage_shape(tm, ds)],
        compiler_params=_params("arbitrary"),
    )(osb, *_residue_views(ocs, nb, seq), *_residue_views(lses, nb, seq), gsb, gdil, wout, x, mod)
    return outs[0], outs[1], outs[2], outs[3], [a.reshape(t, ds) for a in outs[4:]]


def _mix_out_bwd(dxo, m, mod, wout, osb, odil, gsb, gdil, tm):
    t, d = dxo.shape
    ds = osb.shape[1]
    nt = t // tm
    nb = mod.shape[0]
    tpb = nt // nb
    seq = t // nb
    n_cfg = len(DILATIONS)

    def body(dxo_ref, m_ref, mod_ref, w_ref, osb_ref, odil_ref, gsb_ref, gdil_ref,
             dm_ref, dosb_ref, *rest):
        do_refs, dl_refs = rest[:n_cfg], rest[n_cfg:2 * n_cfg]
        dmod_ref, dg_ref, sc = rest[2 * n_cfg:]
        dodil_ref, dldil_ref = do_refs[0], dl_refs[0]
        i = pl.program_id(0)
        dxo_v = dxo_ref[...]
        dm = (mod_ref[5:6, :] * dxo_v).astype(BF16)
        dm_ref[...] = dm
        dgt = jnp.sum(m_ref[...] * dxo_v, axis=0, keepdims=True)
        don = _dot_nt(dm, w_ref[...])
        gm = _head_blocks(ds)

        @pl.when(i % tpb == 0)
        def _():
            dmod_ref[...] = jnp.zeros_like(dmod_ref)

        @pl.when(i == 0)
        def _():
            dg_ref[...] = jnp.zeros_like(dg_ref)

        dmod_ref[2:3, :] += dgt
        groups = ((osb_ref, gsb_ref, dosb_ref), (odil_ref, gdil_ref, dodil_ref))
        for k, (o_ref, g_ref, do_ref) in enumerate(groups):
            o = o_ref[...]
            dn_out = don[:, k * ds:(k + 1) * ds]
            r = lax.rsqrt(_head_mean(o * o, gm) + EPS)
            n = o * r
            dg_ref[0:1, k * ds:(k + 1) * ds] += jnp.sum(dn_out * n, axis=0, keepdims=True)
            dn = dn_out * g_ref[...]
            do = r * (dn - n * _head_mean(dn * n, gm))
            do_ref[...] = do
            if k == 1:
                delta = _head_mean(do * o, gm) * float(HEAD_DIM)
                dldil_ref[...] = delta
                for value, refs in ((do, do_refs), (delta, dl_refs)):
                    _stage(value, sc)
                    for ref, dil in zip(refs[1:], DILATIONS[1:]):
                        _to_residue_rows(sc, ref, dil)

    tok = pl.BlockSpec((tm, d), lambda i: (i, 0))
    hd = pl.BlockSpec((tm, ds), lambda i: (i, 0))
    res = [hd] + [_residue_spec(tm, tpb, ds, dil, lambda i: 0) for dil in DILATIONS[1:]]
    res_shape = [jax.ShapeDtypeStruct((t, ds), F32)] + [_residue_shape(nb, seq, ds, dil, F32) for dil in DILATIONS[1:]]
    gain = pl.BlockSpec((1, ds), lambda i: (0, 0))
    outs = pl.pallas_call(
        body, name="mix_out_bwd", grid=(nt,),
        in_specs=[tok, tok, pl.BlockSpec((None, N_MOD, d), lambda i: (i // tpb, 0, 0)),
                  pl.BlockSpec(wout.shape, lambda i: (0, 0)), hd, hd, gain, gain],
        out_specs=[tok, hd] + res + res
        + [pl.BlockSpec((None, 8, d), lambda i: (i // tpb, 0, 0)), pl.BlockSpec((8, 2 * ds), lambda i: (0, 0))],
        out_shape=[jax.ShapeDtypeStruct((t, d), BF16), jax.ShapeDtypeStruct((t, ds), F32)] + res_shape + res_shape
        + [jax.ShapeDtypeStruct((nb, 8, d), F32), jax.ShapeDtypeStruct((8, 2 * ds), F32)],
        scratch_shapes=[_stage_shape(tm, ds)],
        compiler_params=_params("arbitrary"),
    )(dxo, m, mod, wout, osb, odil, gsb, gdil)
    flat = [a.reshape(t, ds) for a in outs[2:2 + 2 * n_cfg]]
    return outs[0], outs[1], flat[:n_cfg], flat[n_cfg:], outs[-2], outs[-1]


def _merge_dqkv(sb_parts, dil_parts, nb, tm):
    t, ds = sb_parts[0].shape
    nt = t // tm
    tpb = nt // nb
    seq = t // nb
    n_cfg = len(DILATIONS)

    def body(*refs):
        sb_refs, dil_refs = refs[:3], refs[3:3 + 3 * n_cfg]
        o_ref, sc = refs[3 + 3 * n_cfg:]
        for k in range(3):
            o_ref[:, k * ds:(k + 1) * ds] = sb_refs[k][...]
            total = dil_refs[k * n_cfg][...].astype(F32)
            for i, dil in enumerate(DILATIONS[1:]):
                total = total + _from_residue_rows(dil_refs[k * n_cfg + i + 1], sc, dil)
            o_ref[:, (3 + k) * ds:(4 + k) * ds] = total.astype(BF16)

    hd = pl.BlockSpec((tm, ds), lambda i: (i, 0))
    res = [hd] + [_residue_spec(tm, tpb, ds, dil, lambda i: 0) for dil in DILATIONS[1:]]
    views = [v for parts in dil_parts for v in _residue_views(parts, nb, seq)]
    return pl.pallas_call(
        body, name="merge_dqkv", grid=(nt,),
        in_specs=[hd] * 3 + res * 3,
        out_specs=pl.BlockSpec((tm, 6 * ds), lambda i: (i, 0)),
        out_shape=jax.ShapeDtypeStruct((t, 6 * ds), BF16),
        scratch_shapes=[_stage_shape(tm, ds)],
        compiler_params=_params("arbitrary"),
    )(*sb_parts, *views)


def _row_tile(rows):
    if rows <= 256:
        return rows
    for cand in range(256, 15, -16):
        if rows % cand == 0:
            return cand
    return rows


def _adamw(w, parts, m, v, name, transposed=False):
    rows, cols = w.shape
    n_parts = parts.shape[0]
    tr = _row_tile(rows)
    c1 = 1.0 / (1.0 - ADAM_B1 ** ADAM_STEP)
    c2 = 1.0 / (1.0 - ADAM_B2 ** ADAM_STEP)

    def body(w_ref, p_ref, m_ref, v_ref, g_ref, d_ref, nm_ref, nv_ref):
        g = p_ref[0].astype(F32)
        for i in range(1, n_parts):
            g = g + p_ref[i].astype(F32)
        wv, mv, vv = w_ref[...], m_ref[...], v_ref[...]
        if transposed:
            wv, mv, vv = wv.T, mv.T, vv.T
        nm = ADAM_B1 * mv + (1.0 - ADAM_B1) * g
        nv = ADAM_B2 * vv + (1.0 - ADAM_B2) * (g * g)
        g_ref[...] = g
        nm_ref[...] = nm
        nv_ref[...] = nv
        d_ref[...] = -ADAM_LR * ((nm * c1) / (jnp.sqrt(nv * c2) + ADAM_EPS) + ADAM_WD * wv)

    blk = pl.BlockSpec((tr, cols), lambda i: (i, 0))
    if transposed:
        oblk = pl.BlockSpec((cols, tr), lambda i: (0, i))
        pblk = pl.BlockSpec((n_parts, cols, tr), lambda i: (0, 0, i))
        out = jax.ShapeDtypeStruct((cols, rows), F32)
    else:
        oblk, pblk = blk, pl.BlockSpec((n_parts, tr, cols), lambda i: (0, i, 0))
        out = jax.ShapeDtypeStruct((rows, cols), F32)
    return pl.pallas_call(
        body, name=name, grid=(rows // tr,),
        in_specs=[blk, pblk, blk, blk],
        out_specs=[oblk, oblk, oblk, oblk], out_shape=[out, out, out, out],
        compiler_params=_params("arbitrary"),
    )(w, parts, m, v)


def _t5_bucket(n):
    max_exact = N_BUCKETS // 2
    nf = np.maximum(n, 1).astype(np.float32)
    large = max_exact + (np.log(nf / max_exact) / math.log(MAX_DISTANCE / max_exact)
                         * (N_BUCKETS - max_exact)).astype(np.int32)
    large = np.minimum(large, N_BUCKETS - 1)
    return np.where(n < max_exact, n, large).astype(np.int32)


def _bucket_onehot():
    table = np.zeros((len(DILATIONS), 2 * DIL_BLOCK + 1, N_BUCKETS), np.float32)
    for i, dil in enumerate(DILATIONS):
        buckets = _t5_bucket(np.arange(DIL_BLOCK + 1) * dil)
        for m in range(DIL_BLOCK + 1):
            table[i, m, buckets[DIL_BLOCK - m]] = 1.0
    return table


def _bias_blocks(rel_bias):
    row = jnp.einsum("cmn,nh->chm", _bucket_onehot(), rel_bias, precision=lax.Precision.HIGHEST)
    n_cfg, n_heads, width = row.shape
    tiled = jnp.tile(row, (1, 1, DIL_BLOCK))[..., :DIL_BLOCK * (width - 1)]
    return tiled.reshape(n_cfg, n_heads, DIL_BLOCK, width - 1)


def _bias_blocks_bwd(dblocks):
    n_cfg, n_heads = dblocks.shape[:2]
    width = 2 * DIL_BLOCK + 1
    flat = dblocks.reshape(n_cfg, n_heads, DIL_BLOCK * (width - 1))
    flat = jnp.pad(flat, ((0, 0), (0, 0), (0, DIL_BLOCK)))
    drow = jnp.sum(flat.reshape(n_cfg, n_heads, DIL_BLOCK, width), axis=2)
    return jnp.einsum("chm,cmn->nh", drow, _bucket_onehot(), precision=lax.Precision.HIGHEST)


def _pad_to(a, axis, size):
    pad = [(0, 0)] * a.ndim
    pad[axis] = (0, size - a.shape[axis])
    return jnp.pad(a, pad)


def _lane_pad(n):
    return -(-n // LANES) * LANES


def _local_step(x, target, mod, gains, weights, rel_bias, tm, distributed):
    nb, seq, d = x.shape
    t = nb * seq
    g_ffn1, g_mix, g_sb, g_dil, g_ffn2, g_final = gains
    wg1, wu1, wd1 = weights[:3]
    x0 = x.reshape(t, d)
    ds = g_sb.shape[1]
    bias = _bias_blocks(rel_bias)

    def beside(arrays, scatter):
        return _Exchange(arrays, scatter) if distributed else None

    tp, tg = min(PROJ_TILE, seq), min(GRAD_TILE, t)

    (x1, f1, gate1, up1), got = _ffn_fwd(x0, mod, g_ffn1, wg1, wu1, wd1, 0, tp, beside(weights[3:4], False))
    win = got[0] if distributed else weights[3]
    (qkv, qkvd, h2), got = _qkv_fwd(x1, mod, g_mix, win, tp, beside(weights[4:5], False))
    wout = got[0] if distributed else weights[4]
    wout2 = wout.reshape(-1, d)
    (osb, csb), got = _sb_fwd(qkv, nb, seq, beside(weights[5:7], False))
    wg2, wu2 = got if distributed else weights[5:7]
    n_cfg = len(DILATIONS)
    piece = -(-weights[7].shape[-2] // n_cfg // 16) * 16
    ocs, lses, wd2_pieces = [], [], []
    for i, dil in enumerate(DILATIONS):
        rows = weights[7][..., i * piece:(i + 1) * piece, :]
        (oc, lse), got = _dil_fwd(qkvd[i], bias[i], nb, seq, dil, beside([rows], False))
        wd2_pieces.append(got[0] if distributed else rows)
        ocs.append(oc)
        lses.append(lse)
    wd2 = jnp.concatenate(wd2_pieces, axis=-2)
    x2, on, mix, odil, ldil = _mix_out_fwd(osb, ocs, lses, g_sb, g_dil, wout2, x1, mod, tm)
    (dx3, f3, gate3, up3, head), _ = _ffn_fwd(x2, mod, g_ffn2, wg2, wu2, wd2, 2, tp,
                                              head=(target.reshape(t, d), g_final))
    loss_sum = 0.5 * jnp.sum(head[0]) / d
    dg_final = head[1:2]

    (dx2, dgate3, dup3, act3, h3, df3, dmod3, dg_ffn2), _ = _ffn_bwd(
        dx3, x2, f3, mod, g_ffn2, gate3, up3, wg2, wu2, wd2, 2, tm)
    gwg2, gwu2, gwd2 = _ffn_weight_grads(h3, dgate3, dup3, act3, df3, tg, 2)

    dm, dosb, dodil, dldil, dmod2b, dg_heads = _mix_out_bwd(
        dx2, mix, mod, wout2, osb, odil, g_sb, g_dil, tm)
    gwout = _mm_tn(on, dm,
                   pl.BlockSpec((tg, wout.shape[1]), lambda i, j: (i, j)),
                   pl.BlockSpec((tg, d), lambda i, j: (i, 0)),
                   wout.shape, t // tg, "grad_wout")

    (dq_sb, dk_sb, dv_sb), parts_late = _sb_bwd(qkv, dosb, csb, nb, seq,
                                                beside([gwout, gwg2, gwu2, gwd2], True))
    dil_grads = [_dil_bwd(qkvd[i], bias[i], dodil[i], ldil[i], dldil[i], nb, seq, dil)
                 for i, dil in enumerate(DILATIONS)]
    dqkv = _merge_dqkv([dq_sb, dk_sb, dv_sb], [[g[k] for g in dil_grads] for k in range(3)], nb, tm)
    drel = _bias_blocks_bwd(jnp.stack([g[3] for g in dil_grads]))

    cs = win.shape[2]
    gwin = _mm_tn(h2, dqkv,
                  pl.BlockSpec((tg, d), lambda i, j: (i, 0)),
                  pl.BlockSpec((tg, cs), lambda i, j: (i, j)),
                  win.shape, t // tg, "grad_win", pair_reduce=distributed)
    (dx1, dmod2a, dg_mix), parts_mid = _qkv_bwd(
        dqkv, dx2, x1, mod, g_mix, win, tp, _Exchange([gwin], True, chips=[True]) if distributed else None)

    (dx0, dgate1, dup1, act1, h1, df1, dmod1, dg_ffn1), _ = _ffn_bwd(
        dx1, x0, f1, mod, g_ffn1, gate1, up1, wg1, wu1, wd1, 0, tm)
    dmod = jnp.concatenate([dmod1[:, 0:3], dmod2a[:, 0:2], dmod2b[:, 2:3], dmod3[:, 0:3]], axis=1)
    ggrads = (dg_ffn1[0:1], dg_mix[0:1], dg_heads[0:1], drel, dg_ffn2[0:1], dg_final)
    if not distributed:
        gw1 = _ffn_weight_grads(h1, dgate1, dup1, act1, df1, tg, 0)
        return loss_sum, dx0.reshape(nb, seq, d), tuple(gw1) + (gwin, gwout, gwg2, gwu2, gwd2), dmod, ggrads

    dg_heads_row, drel_flat = dg_heads[0:1], drel.reshape(1, -1)
    width = max(d, dg_heads_row.shape[1], drel_flat.shape[1])
    small = jnp.concatenate(
        [_pad_to(a.reshape(1, -1), 1, width)
         for a in (dg_ffn1[0:1], dg_mix[0:1], dg_ffn2[0:1], dg_final, dg_heads_row, drel_flat, loss_sum)]
        + [jnp.zeros((1, width), F32)], axis=0)
    dmod_pad = _pad_to(dmod.reshape(nb, N_MOD * d), 0, 8)
    everyone = _Exchange([jnp.broadcast_to(dmod_pad, (N_DEV,) + dmod_pad.shape),
                          jnp.broadcast_to(small, (N_DEV,) + small.shape)], True)
    sent_g, sent_u, gwd1, (dmod_all, small_all) = _ffn_weight_grads(
        h1, dgate1, dup1, act1, df1, tg, 0, stream=True, first=everyone)
    wgrads = (sent_g, sent_u, gwd1) + tuple(parts_mid + parts_late)
    return dx0.reshape(nb, seq, d), wgrads, dmod_all, small_all


def kernel(x, c, w_ada, b_ada, g_ffn1, w1_gate, w1_up, w1_down, g_mix, w_in, g_sb_out, g_dil_out, w_out, rel_bias, g_ffn2, w2_gate, w2_up, w2_down, g_final, loss_target, m_w_ada, m_b_ada, m_g_ffn1, m_w1_gate, m_w1_up, m_w1_down, m_g_mix, m_w_in, m_g_sb_out, m_g_dil_out, m_w_out, m_rel_bias, m_g_ffn2, m_w2_gate, m_w2_up, m_w2_down, m_g_final, v_w_ada, v_b_ada, v_g_ffn1, v_w1_gate, v_w1_up, v_w1_down, v_g_mix, v_w_in, v_g_sb_out, v_g_dil_out, v_w_out, v_rel_bias, v_g_ffn2, v_w2_gate, v_w2_up, v_w2_down, v_g_final):
    nb, seq, d = x.shape
    me = 4 * lax.axis_index("x") + 2 * lax.axis_index("y") + lax.axis_index("c")
    tm = min(TOKEN_TILE, seq)
    fs = w1_gate.shape[2]
    fs_pad = _lane_pad(fs)
    ada_cols = w_ada.shape[2]

    def col_shard(w):
        return _pad_to(w[0].astype(BF16), 1, fs_pad)

    def row_shard(w):
        return _pad_to(w[0].astype(BF16), 0, fs_pad)

    shards = [col_shard(w1_gate), col_shard(w1_up), row_shard(w1_down), w_in[0].astype(BF16),
              w_out[0].astype(BF16), col_shard(w2_gate), col_shard(w2_up), row_shard(w2_down)]
    b_cols = lax.dynamic_slice(b_ada, (0, me * ada_cols), (1, ada_cols))
    c_every, mod_all, first = _first_exchange(_pad_to(c, 0, 8), shards[:3], w_ada[0], b_cols)
    c_all = c_every[:, :nb].reshape(N_DEV * nb, d)
    weights = first + shards[3:]
    mod = lax.dynamic_slice(mod_all, (0, me * 8, 0), (N_DEV, nb, ada_cols))
    mod = mod.transpose(1, 0, 2).reshape(nb, N_MOD, d)

    n_sb = g_sb_out.shape[1] * g_sb_out.shape[2]
    gains = (g_ffn1, g_mix, g_sb_out.reshape(1, n_sb), g_dil_out.reshape(1, -1), g_ffn2,
             g_final.reshape(1, d))
    grad_x, parts, dmod_all, small_all = _local_step(x, loss_target, mod, gains, weights, rel_bias, tm, True)

    last_part = _exchange([parts[2]], True, "scatter_last", chips=[True])[0]
    parts = parts[:2] + (last_part,) + parts[3:]
    dmod_all = dmod_all[:, :nb].reshape(N_DEV * nb, N_MOD * d)
    dmod_cols = lax.dynamic_slice(dmod_all, (0, me * ada_cols), (N_DEV * nb, ada_cols))
    gw_ada, gb_ada = _ada_bwd(c_all, dmod_cols, dmod_all)

    def small_part(row, size, shape):
        return small_all[:, row, :size].reshape((N_DEV,) + shape)

    loss = jnp.sum(small_all[:, 6, 0])

    n_rel = rel_bias.shape
    updates = {
        "w_ada": (w_ada[0], gw_ada[None], m_w_ada[0], v_w_ada[0]),
        "b_ada": (b_ada, gb_ada[None], m_b_ada, v_b_ada),
        "g_ffn1": (g_ffn1, small_part(0, d, (1, d)), m_g_ffn1, v_g_ffn1),
        "w1_gate": (w1_gate[0], parts[0], m_w1_gate[0], v_w1_gate[0]),
        "w1_up": (w1_up[0], parts[1], m_w1_up[0], v_w1_up[0]),
        "w1_down": (w1_down[0], parts[2], m_w1_down[0], v_w1_down[0]),
        "g_mix": (g_mix, small_part(1, d, (1, d)), m_g_mix, v_g_mix),
        "w_in": (w_in[0], parts[3], m_w_in[0], v_w_in[0]),
        "g_sb_out": (g_sb_out[0], small_all[:, 4, :n_sb].reshape((N_DEV,) + g_sb_out.shape[1:]),
                     m_g_sb_out[0], v_g_sb_out[0]),
        "g_dil_out": (g_dil_out[0], small_all[:, 4, n_sb:n_sb + g_dil_out[0].size].reshape((N_DEV,) + g_dil_out.shape[1:]),
                      m_g_dil_out[0], v_g_dil_out[0]),
        "w_out": (w_out[0], parts[4], m_w_out[0], v_w_out[0]),
        "rel_bias": (rel_bias, small_part(5, rel_bias.size, n_rel), m_rel_bias, v_rel_bias),
        "g_ffn2": (g_ffn2, small_part(2, d, (1, d)), m_g_ffn2, v_g_ffn2),
        "w2_gate": (w2_gate[0], parts[5], m_w2_gate[0], v_w2_gate[0]),
        "w2_up": (w2_up[0], parts[6], m_w2_up[0], v_w2_up[0]),
        "w2_down": (w2_down[0], parts[7], m_w2_down[0], v_w2_down[0]),
        "g_final": (g_final.reshape(1, d), small_part(3, d, (1, d)), m_g_final.reshape(1, d), v_g_final.reshape(1, d)),
    }
    shapes = {"w_ada": w_ada.shape, "b_ada": b_ada.shape, "g_ffn1": g_ffn1.shape, "w1_gate": w1_gate.shape,
              "w1_up": w1_up.shape, "w1_down": w1_down.shape, "g_mix": g_mix.shape, "w_in": w_in.shape,
              "g_sb_out": g_sb_out.shape, "g_dil_out": g_dil_out.shape, "w_out": w_out.shape,
              "rel_bias": rel_bias.shape, "g_ffn2": g_ffn2.shape, "w2_gate": w2_gate.shape,
              "w2_up": w2_up.shape, "w2_down": w2_down.shape, "g_final": g_final.shape}
    grads, deltas, new_m, new_v = [], [], [], []
    for name, (w, p, m, v) in updates.items():
        transposed = name in ("w1_gate", "w1_up", "w2_gate", "w2_up")
        outs = _adamw(w, p, m, v, f"adamw_{name}", transposed)
        for dst, a in zip((grads, deltas, new_m, new_v), outs):
            dst.append((a.T if transposed else a).reshape(shapes[name]))
    return (loss, grad_x, *grads, *deltas, *new_m, *new_v)
```

```python
import functools
import math

import numpy as np
import jax
import jax.numpy as jnp
from jax import lax
from jax.experimental import pallas as pl
from jax.experimental.pallas import tpu as pltpu

F32 = jnp.float32
BF16 = jnp.bfloat16

EPS = 1e-6
NEG_INF = -1e30
HEAD_DIM = 64
LANES = 128
DIL_BLOCK = 128
DILATIONS = (1, 4, 16)
N_BUCKETS = 32
MAX_DISTANCE = 2048
N_MOD = 9
N_DEV = 8
SB_BLOCK = 256
SB_HEADS = 4
SB_WIDTH = SB_HEADS * HEAD_DIM
DIL_HEADS = 4
DIL_WIDTH = DIL_HEADS * HEAD_DIM
TOKEN_TILE = 512
PROJ_TILE = 1024
GRAD_TILE = 1024
SHARD_GROUP = 2
FFN_CHUNKS = 2
VMEM_LIMIT_BYTES = 56 * 1024 * 1024

ADAM_LR = 0.001
ADAM_B1 = 0.9
ADAM_B2 = 0.999
ADAM_EPS = 1e-08
ADAM_WD = 0.01
ADAM_STEP = 10

NT_DIMS = (((1,), (1,)), ((), ()))
TN_DIMS = (((0,), (0,)), ((), ()))


def _params(*sem):
    return pltpu.CompilerParams(dimension_semantics=sem, vmem_limit_bytes=VMEM_LIMIT_BYTES)


def _once(spec):
    return pl.BlockSpec(spec.block_shape, spec.index_map, pipeline_mode=pl.Buffered(1))


def _dot(a, b):
    return jnp.dot(a, b, preferred_element_type=F32)


def _dot_nt(a, b):
    return lax.dot_general(a, b, NT_DIMS, preferred_element_type=F32)


def _dot_tn(a, b):
    return lax.dot_general(a, b, TN_DIMS, preferred_element_type=F32)


def _split_dot(a, b):
    hi = a.astype(BF16)
    lo = (a - hi.astype(F32)).astype(BF16)
    return _dot(hi, b) + _dot(lo, b)


def _sigmoid(z):
    return 1.0 / (1.0 + jnp.exp(-z))


def _norm(x):
    r = lax.rsqrt(jnp.mean(x * x, axis=-1, keepdims=True) + EPS)
    return x * r, r


def _modulate(x, g, mod_ref, k):
    n, _ = _norm(x)
    shift = mod_ref[3 * k:3 * k + 1, :]
    scale = mod_ref[3 * k + 1:3 * k + 2, :]
    return n * g * (1.0 + scale) + shift


def _modulate_bwd(dh, x, g, mod_ref, k):
    n, r = _norm(x)
    scale = mod_ref[3 * k + 1:3 * k + 2, :]
    dshift = jnp.sum(dh, axis=0, keepdims=True)
    dscale = jnp.sum(dh * n * g, axis=0, keepdims=True)
    dg = jnp.sum(dh * n * (1.0 + scale), axis=0, keepdims=True)
    dn = dh * g * (1.0 + scale)
    dx = r * (dn - n * jnp.mean(dn * n, axis=-1, keepdims=True))
    return dx, dshift, dscale, dg


class _Exchange:
    def __init__(self, arrays, scatter, relay=False, chips=None):
        assert not (scatter and relay)
        self.arrays = list(arrays)
        self.scatter = scatter
        self.relay = relay
        self.n = len(self.arrays)
        self.chips = list(chips) if chips is not None else [False] * self.n
        assert scatter or not any(self.chips)
        self.out_shape = [
            jax.ShapeDtypeStruct((N_DEV // 2 if ch else N_DEV,) + tuple(a.shape[1:] if scatter else a.shape), a.dtype)
            for a, ch in zip(self.arrays, self.chips)]
        n_remote = self.n * (N_DEV - 1)
        self.scratch_shapes = [pltpu.SemaphoreType.DMA((n_remote,)), pltpu.SemaphoreType.DMA((n_remote,)),
                               pltpu.SemaphoreType.DMA((self.n,))]

    def _copies(self, in_refs, out_refs, sems):
        send_sems, recv_sems, local_sems = sems
        x, y, c = lax.axis_index("x"), lax.axis_index("y"), lax.axis_index("c")
        me = 4 * x + 2 * y + c
        local, remote, relayed = [], {}, {}
        for a in range(self.n):
            if self.chips[a]:
                mine = 2 * x + y
                local.append(pltpu.make_async_copy(in_refs[a].at[mine], out_refs[a].at[mine], local_sems.at[a]))
                for k in (2, 4, 6):
                    px = 1 - x if k & 4 else x
                    py = 1 - y if k & 2 else y
                    sem = a * (N_DEV - 1) + k - 1
                    remote[a, k] = pltpu.make_async_remote_copy(
                        src_ref=in_refs[a].at[2 * px + py], dst_ref=out_refs[a].at[mine],
                        send_sem=send_sems.at[sem], recv_sem=recv_sems.at[sem],
                        device_id=(px, py, c), device_id_type=pl.DeviceIdType.MESH)
                continue
            src = in_refs[a].at[me] if self.scatter else in_refs[a]
            local.append(pltpu.make_async_copy(src, out_refs[a].at[me], local_sems.at[a]))
            for k in range(1, N_DEV):
                px = 1 - x if k & 4 else x
                py = 1 - y if k & 2 else y
                pc = 1 - c if k & 1 else c
                sem = a * (N_DEV - 1) + k - 1
                if self.relay and k & 1 and k > 1:
                    slot = 4 * px + 2 * py + c
                    relayed[a, k] = pltpu.make_async_remote_copy(
                        src_ref=out_refs[a].at[slot], dst_ref=out_refs[a].at[slot],
                        send_sem=send_sems.at[sem], recv_sem=recv_sems.at[sem],
                        device_id=(x, y, 1 - c), device_id_type=pl.DeviceIdType.MESH)
                    continue
                src = in_refs[a].at[4 * px + 2 * py + pc] if self.scatter else in_refs[a]
                remote[a, k] = pltpu.make_async_remote_copy(
                    src_ref=src, dst_ref=out_refs[a].at[me],
                    send_sem=send_sems.at[sem], recv_sem=recv_sems.at[sem],
                    device_id=(px, py, pc), device_id_type=pl.DeviceIdType.MESH)
        return local, remote, relayed

    def start(self, in_refs, out_refs, sems):
        local, remote, _ = self._copies(in_refs, out_refs, sems)
        for cp in local + list(remote.values()):
            cp.start()

    def wait(self, in_refs, out_refs, sems):
        local, remote, relayed = self._copies(in_refs, out_refs, sems)
        for (a, k), cp in relayed.items():
            remote[a, k - 1].wait_recv()
            cp.start()
        for (a, k), cp in remote.items():
            if (a, k + 1) not in relayed:
                cp.wait_recv()
        for cp in relayed.values():
            cp.wait_recv()
        for cp in list(remote.values()) + list(relayed.values()):
            cp.wait_send()
        for cp in local:
            cp.wait()


def _call(body, *, name, args, in_specs, out_specs, out_shape, scratch_shapes=(), grid=(),
          params=None, exchange=None):
    n_in, n_out = len(args), len(out_shape)
    if exchange is None:
        outs = pl.pallas_call(
            body, name=name, grid=grid, in_specs=list(in_specs), out_specs=list(out_specs),
            out_shape=list(out_shape), scratch_shapes=list(scratch_shapes), compiler_params=params,
        )(*args)
        return list(outs), []
    n_ex = exchange.n

    def wrapped(*refs):
        ins, refs = refs[:n_in], refs[n_in:]
        ex_in, refs = refs[:n_ex], refs[n_ex:]
        outs, refs = refs[:n_out], refs[n_out:]
        ex_out, refs = refs[:n_ex], refs[n_ex:]
        scratch, sems = refs[:len(refs) - 3], refs[len(refs) - 3:]
        if not grid:
            exchange.start(ex_in, ex_out, sems)
            body(*ins, *outs, *scratch)
            exchange.wait(ex_in, ex_out, sems)
            return
        first = functools.reduce(jnp.logical_and, [pl.program_id(a) == 0 for a in range(len(grid))])
        last = functools.reduce(jnp.logical_and, [pl.program_id(a) == grid[a] - 1 for a in range(len(grid))])

        @pl.when(first)
        def _():
            exchange.start(ex_in, ex_out, sems)

        body(*ins, *outs, *scratch)

        @pl.when(last)
        def _():
            exchange.wait(ex_in, ex_out, sems)

    any_spec = pl.BlockSpec(memory_space=pl.ANY)
    outs = pl.pallas_call(
        wrapped, name=name, grid=grid,
        in_specs=list(in_specs) + [any_spec] * n_ex, out_specs=list(out_specs) + [any_spec] * n_ex,
        out_shape=list(out_shape) + exchange.out_shape,
        scratch_shapes=list(scratch_shapes) + exchange.scratch_shapes, compiler_params=params,
    )(*args, *exchange.arrays)
    return list(outs[:n_out]), list(outs[n_out:])


def _exchange(arrays, scatter, name, relay=False, chips=None):
    return _call(lambda: None, name=name, args=(), in_specs=(), out_specs=(), out_shape=(),
                 exchange=_Exchange(arrays, scatter, relay, chips))[1]


def _first_exchange(c_pad, shards, w, b):
    rows, d = c_pad.shape
    cols = w.shape[1]
    ex_c = _Exchange([c_pad], False)
    ex_w = _Exchange(shards, False, relay=True)
    ex_m = _Exchange([jax.ShapeDtypeStruct((N_DEV * rows, cols), F32)], False)
    n_w = ex_w.n

    def body(*refs):
        c_ref, w_refs, wa_ref, b_ref = refs[0], refs[1:1 + n_w], refs[1 + n_w], refs[2 + n_w]
        outs = refs[3 + n_w:]
        cg_ref, wg_refs, mg_ref = outs[0], outs[1:1 + n_w], outs[1 + n_w]
        scratch = outs[2 + n_w:]
        sems_c, sems_w, sems_m, c_vm, m_vm = scratch[0:3], scratch[3:6], scratch[6:9], scratch[9], scratch[10]
        ex_c.start([c_ref], [cg_ref], sems_c)
        ex_c.wait([c_ref], [cg_ref], sems_c)
        pltpu.sync_copy(cg_ref, c_vm)
        cv = c_vm[...].reshape(N_DEV * rows, d)
        s = (cv * _sigmoid(cv)).astype(BF16)
        m_vm[...] = _dot(s, wa_ref[...].astype(BF16)) + b_ref[...]
        ex_m.start([m_vm], [mg_ref], sems_m)
        ex_w.start(w_refs, wg_refs, sems_w)
        ex_m.wait([m_vm], [mg_ref], sems_m)
        ex_w.wait(w_refs, wg_refs, sems_w)

    any_spec = pl.BlockSpec(memory_space=pl.ANY)
    vmem_spec = pl.BlockSpec(memory_space=pltpu.VMEM)
    outs = pl.pallas_call(
        body, name="first_exchange",
        in_specs=[any_spec] * (1 + n_w) + [vmem_spec, vmem_spec],
        out_specs=[any_spec] * (2 + n_w),
        out_shape=ex_c.out_shape + ex_w.out_shape + ex_m.out_shape,
        scratch_shapes=ex_c.scratch_shapes + ex_w.scratch_shapes + ex_m.scratch_shapes
        + [pltpu.VMEM((N_DEV, rows, d), F32), pltpu.VMEM((N_DEV * rows, cols), F32)],
        compiler_params=pltpu.CompilerParams(vmem_limit_bytes=VMEM_LIMIT_BYTES),
    )(c_pad, *shards, w, b)
    return outs[0], outs[1 + n_w], list(outs[1:1 + n_w])


def _ada_bwd(c_all, dmod_cols, dmod_all):
    def body(c_ref, dc_ref, da_ref, gw_ref, gb_ref):
        cv = c_ref[...]
        s = cv * _sigmoid(cv)
        gw_ref[...] = lax.dot_general(s, dc_ref[...], TN_DIMS, preferred_element_type=F32,
                                      precision=lax.Precision.HIGHEST)
        gb_ref[...] = jnp.sum(da_ref[...], axis=0, keepdims=True)

    return pl.pallas_call(
        body, name="ada_bwd",
        out_shape=(jax.ShapeDtypeStruct((c_all.shape[1], dmod_cols.shape[1]), F32),
                   jax.ShapeDtypeStruct((1, dmod_all.shape[1]), F32)),
        compiler_params=pltpu.CompilerParams(vmem_limit_bytes=VMEM_LIMIT_BYTES),
    )(c_all, dmod_cols, dmod_all)


def _side_by_side(w_ref):
    return jnp.concatenate([w_ref[s] for s in range(w_ref.shape[0])], axis=1)


def _stacked(w_ref):
    return jnp.concatenate([w_ref[s] for s in range(w_ref.shape[0])], axis=0)


def _loss_tile(x, target, g, acc_ref):
    d = x.shape[1]
    n, r = _norm(x)
    err = n * g - target
    dy = err * (1.0 / d)
    acc_ref[0:1, :] += jnp.sum(err * err, axis=0, keepdims=True)
    acc_ref[1:2, :] += jnp.sum(dy * n, axis=0, keepdims=True)
    dn = dy * g
    return r * (dn - n * jnp.mean(dn * n, axis=-1, keepdims=True))


def _ffn_fwd(x, mod, g, wg, wu, wd, k, tm, exchange=None, head=None):
    t, d = x.shape
    ns, _, fs = wg.shape
    nt = t // tm
    tpb = nt // mod.shape[0]
    rows = tm // FFN_CHUNKS
    extra = list(head) if head is not None else []

    def body(x_ref, mod_ref, g_ref, wg_ref, wu_ref, wd_ref, *rest):
        if head is not None:
            t_ref, gf_ref, xo_ref, f_ref, gg_ref, uu_ref, head_ref, h_sc, acc = rest
        else:
            xo_ref, f_ref, gg_ref, uu_ref, h_sc, acc = rest
        i, j = pl.program_id(0), pl.program_id(1)

        @pl.when(j == 0)
        def _():
            h_sc[...] = _modulate(x_ref[...], g_ref[...], mod_ref, k).astype(BF16)
            acc[...] = jnp.zeros_like(acc)

        chunks = [pl.ds(c * rows, rows) for c in range(FFN_CHUNKS)]
        wg, wu, wd = _side_by_side(wg_ref), _side_by_side(wu_ref), _stacked(wd_ref)
        gates, ups = [], []
        for rs in chunks:
            h = h_sc[rs, :]
            gates.append(_dot(h, wg))
            ups.append(_dot(h, wu))
        acts = [(g * _sigmoid(g) * u).astype(BF16) for g, u in zip(gates, ups)]
        for rs, g, u in zip(chunks, gates, ups):
            for s in range(SHARD_GROUP):
                gg_ref[s, rs, :] = g[:, s * fs:(s + 1) * fs].astype(BF16)
                uu_ref[s, rs, :] = u[:, s * fs:(s + 1) * fs].astype(BF16)
        downs = [_dot(a, wd) for a in acts]
        for rs, dn in zip(chunks, downs):
            acc[rs, :] += dn

        @pl.when(j == ns // SHARD_GROUP - 1)
        def _():
            f = acc[...]
            f_ref[...] = f.astype(BF16)
            xo = x_ref[...] + 0.5 * mod_ref[3 * k + 2:3 * k + 3, :] * f
            if head is None:
                xo_ref[...] = xo
            else:
                @pl.when(i == 0)
                def _():
                    head_ref[...] = jnp.zeros_like(head_ref)

                xo_ref[...] = _loss_tile(xo, t_ref[...], gf_ref[...], head_ref)

    tok = pl.BlockSpec((tm, d), lambda i, j: (i, 0))
    row = pl.BlockSpec((1, d), lambda i, j: (0, 0))
    hid = pl.BlockSpec((SHARD_GROUP, tm, fs), lambda i, j: (j, i, 0))
    head_specs = [_once(tok), row] if head is not None else []
    head_out = [pl.BlockSpec((8, d), lambda i, j: (0, 0))] if head is not None else []
    head_shape = [jax.ShapeDtypeStruct((8, d), F32)] if head is not None else []
    return _call(
        body, name=f"ffn_fwd{k}", grid=(nt, ns // SHARD_GROUP), args=(x, mod, g, wg, wu, wd, *extra),
        in_specs=[tok,
                  pl.BlockSpec((None, N_MOD, d), lambda i, j: (i // tpb, 0, 0)),
                  row,
                  pl.BlockSpec((SHARD_GROUP, d, fs), lambda i, j: (j, 0, 0)),
                  pl.BlockSpec((SHARD_GROUP, d, fs), lambda i, j: (j, 0, 0)),
                  pl.BlockSpec((SHARD_GROUP, fs, d), lambda i, j: (j, 0, 0))] + head_specs,
        out_specs=[tok, tok, hid, hid] + head_out,
        out_shape=[jax.ShapeDtypeStruct((t, d), F32), jax.ShapeDtypeStruct((t, d), BF16),
                   jax.ShapeDtypeStruct((ns, t, fs), BF16), jax.ShapeDtypeStruct((ns, t, fs), BF16)]
        + head_shape,
        scratch_shapes=[pltpu.VMEM((tm, d), BF16), pltpu.VMEM((tm, d), F32)],
        params=_params("arbitrary", "arbitrary"), exchange=exchange)


def _ffn_bwd(dxo, x, f, mod, g, gate, up, wg, wu, wd, k, tm, exchange=None):
    t, d = x.shape
    ns, _, fs = wg.shape
    nt = t // tm
    nb = mod.shape[0]
    tpb = nt // nb
    rows = tm // FFN_CHUNKS

    def body(dxo_ref, x_ref, f_ref, mod_ref, g_ref, gg_ref, uu_ref, wg_ref, wu_ref, wd_ref,
             dx_ref, dgg_ref, duu_ref, act_ref, h_ref, df_ref, dmod_ref, dg_ref, acc):
        i, j = pl.program_id(0), pl.program_id(1)

        @pl.when(j == 0)
        def _():
            df = 0.5 * mod_ref[3 * k + 2:3 * k + 3, :] * dxo_ref[...]
            df_ref[...] = df.astype(BF16)
            h_ref[...] = _modulate(x_ref[...], g_ref[...], mod_ref, k).astype(BF16)
            acc[...] = jnp.zeros_like(acc)

        chunks = [pl.ds(c * rows, rows) for c in range(FFN_CHUNKS)]
        group = range(SHARD_GROUP)
        wg, wu, wd = _side_by_side(wg_ref), _side_by_side(wu_ref), _stacked(wd_ref)
        dacts = [_dot_nt(df_ref[rs, :], wd) for rs in chunks]
        dgates, dups = [], []
        for rs, dact in zip(chunks, dacts):
            gv = jnp.concatenate([gg_ref[s, rs, :] for s in group], axis=1).astype(F32)
            uv = jnp.concatenate([uu_ref[s, rs, :] for s in group], axis=1).astype(F32)
            sig = _sigmoid(gv)
            s_act = gv * sig
            act = (s_act * uv).astype(BF16)
            for s in group:
                act_ref[s, rs, :] = act[:, s * fs:(s + 1) * fs]
            dups.append((dact * s_act).astype(BF16))
            dgates.append((dact * uv * (sig * (1.0 + gv * (1.0 - sig)))).astype(BF16))
        dhs = [_dot_nt(dg, wg) + _dot_nt(du, wu) for dg, du in zip(dgates, dups)]
        for rs, dg, du, dh in zip(chunks, dgates, dups, dhs):
            for s in group:
                dgg_ref[s, rs, :] = dg[:, s * fs:(s + 1) * fs]
                duu_ref[s, rs, :] = du[:, s * fs:(s + 1) * fs]
            acc[rs, :] += dh

        @pl.when(j == ns // SHARD_GROUP - 1)
        def _():
            dx, dshift, dscale, dg = _modulate_bwd(acc[...], x_ref[...], g_ref[...], mod_ref, k)
            dxo_v = dxo_ref[...]
            dx_ref[...] = dxo_v + dx
            dgt = jnp.sum(0.5 * f_ref[...].astype(F32) * dxo_v, axis=0, keepdims=True)

            @pl.when(i % tpb == 0)
            def _():
                dmod_ref[...] = jnp.zeros_like(dmod_ref)

            @pl.when(i == 0)
            def _():
                dg_ref[...] = jnp.zeros_like(dg_ref)

            dmod_ref[0:1, :] += dshift
            dmod_ref[1:2, :] += dscale
            dmod_ref[2:3, :] += dgt
            dg_ref[0:1, :] += dg

    tok = pl.BlockSpec((tm, d), lambda i, j: (i, 0))
    hid = pl.BlockSpec((SHARD_GROUP, tm, fs), lambda i, j: (j, i, 0))
    return _call(
        body, name=f"ffn_bwd{k}", grid=(nt, ns // SHARD_GROUP), args=(dxo, x, f, mod, g, gate, up, wg, wu, wd),
        in_specs=[tok, tok, tok,
                  pl.BlockSpec((None, N_MOD, d), lambda i, j: (i // tpb, 0, 0)),
                  pl.BlockSpec((1, d), lambda i, j: (0, 0)),
                  hid, hid,
                  pl.BlockSpec((SHARD_GROUP, d, fs), lambda i, j: (j, 0, 0)),
                  pl.BlockSpec((SHARD_GROUP, d, fs), lambda i, j: (j, 0, 0)),
                  pl.BlockSpec((SHARD_GROUP, fs, d), lambda i, j: (j, 0, 0))],
        out_specs=[tok, hid, hid, hid, tok, tok,
                   pl.BlockSpec((None, 8, d), lambda i, j: (i // tpb, 0, 0)),
                   pl.BlockSpec((8, d), lambda i, j: (0, 0))],
        out_shape=[jax.ShapeDtypeStruct((t, d), F32),
                   jax.ShapeDtypeStruct((ns, t, fs), BF16), jax.ShapeDtypeStruct((ns, t, fs), BF16),
                   jax.ShapeDtypeStruct((ns, t, fs), BF16),
                   jax.ShapeDtypeStruct((t, d), BF16), jax.ShapeDtypeStruct((t, d), BF16),
                   jax.ShapeDtypeStruct((nb, 8, d), F32), jax.ShapeDtypeStruct((8, d), F32)],
        scratch_shapes=[pltpu.VMEM((tm, d), F32)],
        params=_params("arbitrary", "arbitrary"), exchange=exchange)


def _mm_tn(a, b, a_spec, b_spec, out_shape, n_tiles, name, exchange=None, keep_transposed=False,
           pair_reduce=False):
    n_out = out_shape[0]
    block = tuple(out_shape[1:])
    last = n_tiles - 1
    flip = block[0] > block[1]
    if flip:
        block = block[::-1]
    if flip and keep_transposed:
        flip_back, out_shape = False, (n_out,) + block
    else:
        flip_back = flip
    full_shape = tuple(out_shape)
    n_pairs = n_out // 2
    if pair_reduce:
        out_shape = (n_pairs,) + full_shape[1:]

    def body(a_ref, b_ref, o_ref, acc, *pair):
        i, j = pl.program_id(0), pl.program_id(1)
        prod = _dot_tn(b_ref[...], a_ref[...]) if flip else _dot_tn(a_ref[...], b_ref[...])
        full_ref = pair[0] if pair_reduce else o_ref

        @pl.when(i == 0)
        def _():
            acc[j] = prod

        @pl.when(i > 0)
        def _():
            acc[j] += prod

        @pl.when(i == last)
        def _():
            total = acc[j]
            full_ref[j] = (total.T if flip_back else total).astype(BF16)

        if pair_reduce:
            _, landed, send_sems, recv_sems = pair

            @pl.when(jnp.logical_and(i == last, j == n_out - 1))
            def _():
                x, y, c = lax.axis_index("x"), lax.axis_index("y"), lax.axis_index("c")
                copies = [pltpu.make_async_remote_copy(
                    src_ref=full_ref.at[2 * q + 1 - c], dst_ref=landed.at[q],
                    send_sem=send_sems.at[q], recv_sem=recv_sems.at[q],
                    device_id=(x, y, 1 - c), device_id_type=pl.DeviceIdType.MESH) for q in range(n_pairs)]
                for cp in copies:
                    cp.start()
                for q, cp in enumerate(copies):
                    cp.wait_recv()
                    o_ref[q] = (full_ref[2 * q + c].astype(F32) + landed[q].astype(F32)).astype(BF16)
                for cp in copies:
                    cp.wait_send()

    scratch = [pltpu.VMEM((n_out,) + block, F32)]
    if pair_reduce:
        scratch += [pltpu.VMEM(full_shape, BF16), pltpu.VMEM(out_shape, BF16),
                    pltpu.SemaphoreType.DMA((n_pairs,)), pltpu.SemaphoreType.DMA((n_pairs,))]
    outs, sent = _call(
        body, name=name, grid=(n_tiles, n_out), args=(a, b), in_specs=[a_spec, b_spec],
        out_specs=[pl.BlockSpec(out_shape, lambda i, j: (0,) * len(out_shape))],
        out_shape=[jax.ShapeDtypeStruct(out_shape, BF16)],
        scratch_shapes=scratch,
        params=_params("arbitrary", "arbitrary"), exchange=exchange)
    return (outs[0], sent) if exchange is not None else outs[0]


def _ffn_weight_grads(h, dgate, dup, act, df, tm, tag, stream=False, first=None):
    t, d = h.shape
    ns, _, fs = dgate.shape
    nt = t // tm
    tok = pl.BlockSpec((tm, d), lambda i, j: (i, 0))
    hid = pl.BlockSpec((None, tm, fs), lambda i, j: (j, i, 0))
    if not stream:
        gwg = _mm_tn(h, dgate, tok, hid, (ns, d, fs), nt, f"grad_wg{tag}", keep_transposed=True)
        gwu = _mm_tn(h, dup, tok, hid, (ns, d, fs), nt, f"grad_wu{tag}", keep_transposed=True)
        gwd = _mm_tn(act, df, hid, tok, (ns, fs, d), nt, f"grad_wd{tag}")
        return gwg, gwu, gwd
    gwg, brought = _mm_tn(h, dgate, tok, hid, (ns, d, fs), nt, f"grad_wg{tag}", first,
                          keep_transposed=True, pair_reduce=True)
    gwu, sent_g = _mm_tn(h, dup, tok, hid, (ns, d, fs), nt, f"grad_wu{tag}",
                         _Exchange([gwg], True, chips=[True]), keep_transposed=True, pair_reduce=True)
    gwd, sent_u = _mm_tn(act, df, hid, tok, (ns, fs, d), nt, f"grad_wd{tag}",
                         _Exchange([gwu], True, chips=[True]), pair_reduce=True)
    return sent_g[0], sent_u[0], gwd, brought


def _stage_shape(rows, cols):
    return pltpu.VMEM((cols // LANES, rows, LANES), F32)


def _stage(value, stage_ref):
    for k in range(stage_ref.shape[0]):
        stage_ref[k] = value[:, k * LANES:(k + 1) * LANES]


def _to_residue_rows(stage_ref, dst_ref, dil):
    rows = stage_ref.shape[1] // dil
    for r in range(dil):
        for k in range(stage_ref.shape[0]):
            dst_ref[r, :, k * LANES:(k + 1) * LANES] = (
                stage_ref.at[k][pl.ds(r, rows, stride=dil), :].astype(dst_ref.dtype))


def _from_residue_rows(src_ref, stage_ref, dil):
    rows = stage_ref.shape[1] // dil
    chunks = range(stage_ref.shape[0])
    for r in range(dil):
        for k in chunks:
            stage_ref.at[k][pl.ds(r, rows, stride=dil), :] = src_ref[r, :, k * LANES:(k + 1) * LANES].astype(F32)
    return jnp.concatenate([stage_ref[k] for k in chunks], axis=1)


def _residue_shape(nb, seq, width, dil, dtype):
    return jax.ShapeDtypeStruct((nb, dil, seq // dil, width), dtype)


def _residue_spec(tm, tpb, cols, dil, col_block):
    return pl.BlockSpec((None, dil, tm // dil, cols),
                        lambda i, *rest: (i // tpb, 0, i % tpb, col_block(i, *rest)))


def _qkv_fwd(x, mod, g, win, tm, exchange=None):
    t, d = x.shape
    ns, _, cs = win.shape
    nt = t // tm
    nb = mod.shape[0]
    tpb = nt // nb
    seq = t // nb
    width = ns * cs // 2
    cs, ns = cs * SHARD_GROUP, ns // SHARD_GROUP
    half = ns // 2
    n_res = len(DILATIONS) - 1

    def body(x_ref, mod_ref, g_ref, w_ref, sb_ref, dil_ref, *rest):
        res_refs, h_ref, sc = rest[:n_res], rest[n_res], rest[n_res + 1]
        j = pl.program_id(1)

        @pl.when(j == 0)
        def _():
            h_ref[...] = _modulate(x_ref[...], g_ref[...], mod_ref, 1).astype(BF16)

        res = _dot(h_ref[...], _side_by_side(w_ref))

        @pl.when(j < half)
        def _():
            sb_ref[...] = res.astype(BF16)

        @pl.when(j >= half)
        def _():
            dil_ref[...] = res.astype(BF16)
            _stage(res, sc)
            for ref, dil in zip(res_refs, DILATIONS[1:]):
                _to_residue_rows(sc, ref, dil)

    def dil_col(i, j):
        return jnp.maximum(j - half, 0)

    tok = pl.BlockSpec((tm, d), lambda i, j: (i, 0))
    wide = jax.ShapeDtypeStruct((t, width), BF16)
    outs, got = _call(
        body, name="qkv_fwd", grid=(nt, ns), args=(x, mod, g, win),
        in_specs=[tok,
                  pl.BlockSpec((None, N_MOD, d), lambda i, j: (i // tpb, 0, 0)),
                  pl.BlockSpec((1, d), lambda i, j: (0, 0)),
                  pl.BlockSpec((SHARD_GROUP, d, cs // SHARD_GROUP), lambda i, j: (j, 0, 0))],
        out_specs=[pl.BlockSpec((tm, cs), lambda i, j: (i, jnp.minimum(j, half - 1))),
                   pl.BlockSpec((tm, cs), lambda i, j: (i, dil_col(i, j)))]
        + [_residue_spec(tm, tpb, cs, dil, dil_col) for dil in DILATIONS[1:]] + [tok],
        out_shape=[wide, wide] + [_residue_shape(nb, seq, width, dil, BF16) for dil in DILATIONS[1:]]
        + [jax.ShapeDtypeStruct((t, d), BF16)],
        scratch_shapes=[_stage_shape(tm, cs)],
        params=_params("arbitrary", "arbitrary"), exchange=exchange)
    qkv_dil = [outs[1]] + [a.reshape(t, width) for a in outs[2:2 + n_res]]
    return (outs[0], qkv_dil, outs[-1]), got


def _qkv_bwd(dqkv, dxo, x, mod, g, win, tm, exchange=None):
    t, d = x.shape
    ns, _, cs = win.shape
    nt = t // tm
    nb = mod.shape[0]
    tpb = nt // nb
    cs, ns = cs * SHARD_GROUP, ns // SHARD_GROUP

    def body(dq_ref, dxo_ref, x_ref, mod_ref, g_ref, w_ref, dx_ref, dmod_ref, dg_ref, acc):
        i, j = pl.program_id(0), pl.program_id(1)

        @pl.when(j == 0)
        def _():
            acc[...] = jnp.zeros_like(acc)

        acc[...] += _dot_nt(dq_ref[...], _side_by_side(w_ref))

        @pl.when(j == ns - 1)
        def _():
            dx, dshift, dscale, dg = _modulate_bwd(acc[...], x_ref[...], g_ref[...], mod_ref, 1)
            dx_ref[...] = dxo_ref[...] + dx

            @pl.when(i % tpb == 0)
            def _():
                dmod_ref[...] = jnp.zeros_like(dmod_ref)

            @pl.when(i == 0)
            def _():
                dg_ref[...] = jnp.zeros_like(dg_ref)

            dmod_ref[0:1, :] += dshift
            dmod_ref[1:2, :] += dscale
            dg_ref[0:1, :] += dg

    tok = pl.BlockSpec((tm, d), lambda i, j: (i, 0))
    return _call(
        body, name="qkv_bwd", grid=(nt, ns), args=(dqkv, dxo, x, mod, g, win),
        in_specs=[pl.BlockSpec((tm, cs), lambda i, j: (i, j)), tok, tok,
                  pl.BlockSpec((None, N_MOD, d), lambda i, j: (i // tpb, 0, 0)),
                  pl.BlockSpec((1, d), lambda i, j: (0, 0)),
                  pl.BlockSpec((SHARD_GROUP, d, cs // SHARD_GROUP), lambda i, j: (j, 0, 0))],
        out_specs=[tok,
                   pl.BlockSpec((None, 8, d), lambda i, j: (i // tpb, 0, 0)),
                   pl.BlockSpec((8, d), lambda i, j: (0, 0))],
        out_shape=[jax.ShapeDtypeStruct((t, d), F32),
                   jax.ShapeDtypeStruct((nb, 8, d), F32), jax.ShapeDtypeStruct((8, d), F32)],
        scratch_shapes=[pltpu.VMEM((tm, d), F32)],
        params=_params("arbitrary", "arbitrary"), exchange=exchange)


def _own_lanes():
    lane = lax.broadcasted_iota(jnp.int32, (1, LANES), 1)
    return [lane < HEAD_DIM, lane >= HEAD_DIM]


def _pair_tiles(a):
    return [a[:, (h // 2) * LANES:(h // 2 + 1) * LANES] for h in range(a.shape[1] // HEAD_DIM)]


def _own_tiles(a, own):
    return [jnp.where(own[h % 2], tile, jnp.zeros_like(tile)) for h, tile in enumerate(_pair_tiles(a))]


def _merge_tiles(per_head, own):
    return jnp.concatenate([jnp.where(own[0], per_head[h], per_head[h + 1])
                            for h in range(0, len(per_head), 2)], axis=1)


def _scaled(q):
    return (q.astype(F32) * (HEAD_DIM ** -0.5)).astype(BF16)


def _sb_logits(qh, kh, tri, causal):
    zs = [_dot_nt(q, k) for q, k in zip(qh, kh)]
    es = [jnp.exp(-jnp.abs(z)) for z in zs]
    log_nots = [-(jnp.maximum(z, 0.0) + jnp.log(1.0 + e)) for z, e in zip(zs, es)]
    if causal is not None:
        log_nots = [jnp.where(causal, ln, 0.0) for ln in log_nots]
    return zs, es, [_split_dot(ln, tri) for ln in log_nots]


def _sb_masks():
    rows = lax.broadcasted_iota(jnp.int32, (SB_BLOCK, SB_BLOCK), 0)
    cols = lax.broadcasted_iota(jnp.int32, (SB_BLOCK, SB_BLOCK), 1)
    return (rows >= cols).astype(BF16), (rows <= cols).astype(BF16), cols < rows


def _sb_fwd(qkv, nb, seq, exchange=None):
    t = qkv.shape[0]
    n_pairs = (qkv.shape[1] // 3) // SB_WIDTH
    tb = SB_BLOCK
    n_blk = seq // tb

    def body(q_ref, k_ref, v_ref, o_ref, c_ref):
        tri, _, causal = _sb_masks()
        own = _own_lanes()

        def key_blocks(qh, kjs, carry, mask):
            nh = SB_HEADS
            chains = range(nh * len(kjs))
            kss = [pl.multiple_of(kj * tb, tb) for kj in kjs]
            kh = [tile for ks in kss for tile in _pair_tiles(k_ref[pl.ds(ks, tb), :])]
            vh = [tile for ks in kss for tile in _pair_tiles(v_ref[pl.ds(ks, tb), :])]
            zs, _, suffixes = _sb_logits(qh * len(kjs), kh, tri, mask)
            right = []
            for c in chains:
                right.append(carry[c][1] if c < nh else right[c - nh] + suffixes[c - nh][:, 0:1])
            ws = [jnp.exp(zs[c] + suffixes[c] + right[c]) for c in chains]
            if mask is not None:
                ws = [jnp.where(mask, w, 0.0) for w in ws]
            pv = [_dot(ws[c].astype(BF16), vh[c]) for c in chains]
            last = (len(kjs) - 1) * nh
            return tuple((carry[h][0] + sum(pv[h::nh]), right[last + h] + suffixes[last + h][:, 0:1])
                         for h in range(nh))

        def query_block(qi, _):
            qs = pl.multiple_of(qi * tb, tb)
            qh = _own_tiles(_scaled(q_ref[pl.ds(qs, tb), :]), own)
            zero = (jnp.zeros((tb, LANES), F32), jnp.zeros((tb, 1), F32))
            carry = key_blocks(qh, [qi], (zero,) * SB_HEADS, causal)
            carry = lax.fori_loop(
                0, qi // 2, lambda p, cr: key_blocks(qh, [qi - 1 - 2 * p, qi - 2 - 2 * p], cr, None), carry)
            carry = lax.fori_loop(0, qi % 2, lambda _, cr: key_blocks(qh, [0], cr, None), carry)
            o_ref[pl.ds(qs, tb), :] = _merge_tiles([cr[0] for cr in carry], own)
            c_ref[pl.ds(qs, tb), :] = _merge_tiles([jnp.broadcast_to(cr[1], (tb, LANES)) for cr in carry], own)
            return 0

        lax.fori_loop(0, n_blk, query_block, 0)

    def spec(offset):
        return pl.BlockSpec((seq, SB_WIDTH), lambda b, p: (b, offset + p))

    out = jax.ShapeDtypeStruct((t, n_pairs * SB_WIDTH), F32)
    return _call(
        body, name="sb_fwd", grid=(nb, n_pairs), args=(qkv, qkv, qkv),
        in_specs=[spec(0), spec(n_pairs), spec(2 * n_pairs)],
        out_specs=[spec(0), spec(0)], out_shape=[out, out],
        params=_params("arbitrary", "arbitrary"), exchange=exchange)


def _sb_bwd(qkv, do, csum, nb, seq, exchange=None):
    t = qkv.shape[0]
    n_pairs = (qkv.shape[1] // 3) // SB_WIDTH
    tb = SB_BLOCK
    n_blk = seq // tb
    scale = HEAD_DIM ** -0.5

    def body(q_ref, k_ref, v_ref, do_ref, c_ref, dq_ref, dk_ref, dv_ref, dkt_acc, dvt_acc):
        tri, tri_prefix, causal = _sb_masks()
        own = _own_lanes()
        dkt_acc[...] = jnp.zeros_like(dkt_acc)
        dvt_acc[...] = jnp.zeros_like(dvt_acc)

        def key_blocks(qh, qth, doh, doth, ch, kjs, carry, mask):
            nh = SB_HEADS
            chains = range(nh * len(kjs))
            kss = [pl.multiple_of(kj * tb, tb) for kj in kjs]
            kh = [tile for ks in kss for tile in _pair_tiles(k_ref[pl.ds(ks, tb), :])]
            vh = [tile for ks in kss for tile in _pair_tiles(v_ref[pl.ds(ks, tb), :])]
            zs, es, suffixes = _sb_logits(qh * len(kjs), kh, tri, mask)
            dws = [_dot_nt(doh[c % nh], vh[c]) for c in chains]
            lefts = []
            for c in chains:
                before = carry[c][1] if c < nh else lefts[c - nh]
                lefts.append(before + suffixes[c][:, 0:1])
            ws = [jnp.exp(zs[c] + suffixes[c] + (ch[c % nh] - lefts[c])) for c in chains]
            if mask is not None:
                ws = [jnp.where(mask, w, 0.0) for w in ws]
            dlws = [ws[c] * dws[c] for c in chains]
            dprefixes = [_split_dot(dlw, tri_prefix) for dlw in dlws]
            dvts = [_dot(doth[c % nh], ws[c].astype(BF16)) for c in chains]
            dlefts, dzbs = [], []
            for c in chains:
                dlefts.append(carry[c][2] if c < nh else dlefts[c - nh] + dprefixes[c - nh][:, tb - 1:tb])
                sig = jnp.where(zs[c] >= 0.0, 1.0, es[c]) * pl.reciprocal(1.0 + es[c], approx=True)
                dz = dlws[c] - sig * (dlefts[c] + dprefixes[c])
                if mask is not None:
                    dz = jnp.where(mask, dz, 0.0)
                dzbs.append(dz.astype(BF16))
            dkts = [_dot(qth[c % nh], dzbs[c]) for c in chains]
            dqs = [_dot(dzbs[c], kh[c]) for c in chains]
            for b, ks in enumerate(kss):
                pairs = range(b * nh, (b + 1) * nh, 2)
                dkt_acc[:, pl.ds(ks, tb)] += jnp.concatenate([dkts[c] + dkts[c + 1] for c in pairs], axis=0)
                dvt_acc[:, pl.ds(ks, tb)] += jnp.concatenate([dvts[c] + dvts[c + 1] for c in pairs], axis=0)
            last = (len(kjs) - 1) * nh
            return tuple((carry[h][0] + sum(dqs[h::nh]), lefts[last + h],
                          dlefts[last + h] + dprefixes[last + h][:, tb - 1:tb]) for h in range(nh))

        def query_block(qi, _):
            qs = pl.multiple_of(qi * tb, tb)
            qh = _own_tiles(_scaled(q_ref[pl.ds(qs, tb), :]), own)
            doh = _own_tiles(do_ref[pl.ds(qs, tb), :], own)
            qth = [a.astype(F32).T.astype(BF16) for a in qh]
            doth = [a.T.astype(BF16) for a in doh]
            doh = [a.astype(BF16) for a in doh]
            cv = c_ref[pl.ds(qs, tb), :]
            ch = [cv[:, h * HEAD_DIM:h * HEAD_DIM + 1] for h in range(SB_HEADS)]
            zero = (jnp.zeros((tb, LANES), F32), jnp.zeros((tb, 1), F32), jnp.zeros((tb, 1), F32))

            def key_block(kjs, cr, mask):
                return key_blocks(qh, qth, doh, doth, ch, kjs, cr, mask)

            carry = lax.fori_loop(0, qi // 2, lambda p, cr: key_block([2 * p, 2 * p + 1], cr, None),
                                  (zero,) * SB_HEADS)
            carry = lax.fori_loop(0, qi % 2, lambda _, cr: key_block([qi - 1], cr, None), carry)
            carry = key_block([qi], carry, causal)
            dq = _merge_tiles([cr[0] for cr in carry], own) * scale
            dq_ref[pl.ds(qs, tb), :] = dq.astype(BF16)
            return 0

        lax.fori_loop(0, n_blk, query_block, 0)
        dk_ref[...] = dkt_acc[...].T.astype(BF16)
        dv_ref[...] = dvt_acc[...].T.astype(BF16)

    def spec(offset):
        return pl.BlockSpec((seq, SB_WIDTH), lambda b, p: (b, offset + p))

    out = jax.ShapeDtypeStruct((t, n_pairs * SB_WIDTH), BF16)
    return _call(
        body, name="sb_bwd", grid=(nb, n_pairs), args=(qkv, qkv, qkv, do, csum),
        in_specs=[spec(0), spec(n_pairs), spec(2 * n_pairs), spec(0), spec(0)],
        out_specs=[spec(0), spec(0), spec(0)],
        out_shape=[out, out, out],
        scratch_shapes=[pltpu.VMEM((SB_WIDTH, seq), F32), pltpu.VMEM((SB_WIDTH, seq), F32)],
        params=_params("arbitrary", "arbitrary"), exchange=exchange)


def _dil_block_scores(qh, kph, kch, bias_ref, has_prev, band_prev, band_cur):
    scale = HEAD_DIM ** -0.5
    heads = range(len(qh))
    no_prev = jnp.where(has_prev, 0.0, NEG_INF)
    zps = [_dot_nt(qh[h], kph[h]) for h in heads]
    zcs = [_dot_nt(qh[h], kch[h]) for h in heads]
    zps = [jnp.where(band_prev, zps[h] * scale + bias_ref[h, :, 0:DIL_BLOCK], NEG_INF) + no_prev for h in heads]
    zcs = [jnp.where(band_cur, zcs[h] * scale + bias_ref[h, :, DIL_BLOCK:2 * DIL_BLOCK], NEG_INF) for h in heads]
    return zps, zcs


def _dil_bands():
    rows = lax.broadcasted_iota(jnp.int32, (DIL_BLOCK, DIL_BLOCK), 0)
    cols = lax.broadcasted_iota(jnp.int32, (DIL_BLOCK, DIL_BLOCK), 1)
    return cols >= rows, cols <= rows


def _dil_fwd(qkv, bias, nb, seq, dil, exchange=None):
    t, width = qkv.shape
    n_pairs = (width // 3) // DIL_WIDTH
    bq = DIL_BLOCK
    n_blk = seq // bq
    per_seq = n_blk // dil
    heads = range(DIL_HEADS)

    def body(q_ref, k_ref, v_ref, bias_ref, o_ref, lse_ref):
        band_prev, band_cur = _dil_bands()
        own = _own_lanes()

        def block(n, _):
            has_prev = (n & (per_seq - 1)) != 0
            qs = pl.multiple_of(n * bq, bq)
            ps = pl.multiple_of(jnp.maximum(n - 1, 0) * bq, bq)
            qh = _own_tiles(q_ref[pl.ds(qs, bq), :], own)
            kp, kc = _pair_tiles(k_ref[pl.ds(ps, bq), :]), _pair_tiles(k_ref[pl.ds(qs, bq), :])
            vp, vc = _pair_tiles(v_ref[pl.ds(ps, bq), :]), _pair_tiles(v_ref[pl.ds(qs, bq), :])
            zps, zcs = _dil_block_scores(qh, kp, kc, bias_ref, has_prev, band_prev, band_cur)
            ms = [jnp.maximum(jnp.max(zps[h], axis=1, keepdims=True), jnp.max(zcs[h], axis=1, keepdims=True))
                  for h in heads]
            eps = [jnp.exp(zps[h] - ms[h]) for h in heads]
            ecs = [jnp.exp(zcs[h] - ms[h]) for h in heads]
            pvs = [_dot(eps[h].astype(BF16), vp[h]) + _dot(ecs[h].astype(BF16), vc[h]) for h in heads]
            dens = [jnp.sum(eps[h], axis=1, keepdims=True) + jnp.sum(ecs[h], axis=1, keepdims=True) for h in heads]
            o_ref[pl.ds(qs, bq), :] = _merge_tiles([pvs[h] / dens[h] for h in heads], own)
            lse_ref[pl.ds(qs, bq), :] = _merge_tiles(
                [jnp.broadcast_to(ms[h] + jnp.log(dens[h]), (bq, LANES)) for h in heads], own)
            return 0

        lax.fori_loop(0, n_blk, block, 0, unroll=2)

    def spec(offset):
        return pl.BlockSpec((seq, DIL_WIDTH), lambda b, p: (b, offset + p))

    out = jax.ShapeDtypeStruct((t, n_pairs * DIL_WIDTH), F32)
    return _call(
        body, name=f"dil_fwd{dil}", grid=(nb, n_pairs), args=(qkv, qkv, qkv, bias),
        in_specs=[spec(0), spec(n_pairs), spec(2 * n_pairs),
                  pl.BlockSpec((DIL_HEADS, bq, 2 * bq), lambda b, p: (p, 0, 0))],
        out_specs=[spec(0), spec(0)], out_shape=[out, out],
        params=_params("arbitrary", "arbitrary"), exchange=exchange)


def _dil_bwd(qkv, bias, do, lse, delta, nb, seq, dil):
    t, width = qkv.shape
    n_pairs = (width // 3) // DIL_WIDTH
    bq = DIL_BLOCK
    n_blk = seq // bq
    per_seq = n_blk // dil
    scale = HEAD_DIM ** -0.5
    heads = range(DIL_HEADS)

    def body(q_ref, k_ref, v_ref, bias_ref, do_ref, lse_ref, dl_ref, dq_ref, dk_ref, dv_ref, db_ref,
             dk_acc, dv_acc):
        band_prev, band_cur = _dil_bands()
        own = _own_lanes()
        dk_acc[...] = jnp.zeros_like(dk_acc)
        dv_acc[...] = jnp.zeros_like(dv_acc)

        @pl.when(pl.program_id(1) == 0)
        def _():
            db_ref[...] = jnp.zeros_like(db_ref)

        def block(n, _):
            has_prev = (n & (per_seq - 1)) != 0
            qs = pl.multiple_of(n * bq, bq)
            ps = pl.multiple_of(jnp.maximum(n - 1, 0) * bq, bq)
            qh = _own_tiles(q_ref[pl.ds(qs, bq), :], own)
            kp, kc = _pair_tiles(k_ref[pl.ds(ps, bq), :]), _pair_tiles(k_ref[pl.ds(qs, bq), :])
            vp, vc = _pair_tiles(v_ref[pl.ds(ps, bq), :]), _pair_tiles(v_ref[pl.ds(qs, bq), :])
            doh = _own_tiles(do_ref[pl.ds(qs, bq), :].astype(BF16), own)
            lse_v, dl_v = lse_ref[pl.ds(qs, bq), :], dl_ref[pl.ds(qs, bq), :]
            zps, zcs = _dil_block_scores(qh, kp, kc, bias_ref, has_prev, band_prev, band_cur)
            dpp = [_dot_nt(doh[h], vp[h]) for h in heads]
            dpc = [_dot_nt(doh[h], vc[h]) for h in heads]
            lse_h = [lse_v[:, h * HEAD_DIM:h * HEAD_DIM + 1] for h in heads]
            dl_h = [dl_v[:, h * HEAD_DIM:h * HEAD_DIM + 1] for h in heads]
            pps = [jnp.exp(zps[h] - lse_h[h]) for h in heads]
            pcs = [jnp.exp(zcs[h] - lse_h[h]) for h in heads]
            dvp = [_dot_tn(pps[h].astype(BF16), doh[h]) for h in heads]
            dvc = [_dot_tn(pcs[h].astype(BF16), doh[h]) for h in heads]
            dzps = [pps[h] * (dpp[h] - dl_h[h]) for h in heads]
            dzcs = [pcs[h] * (dpc[h] - dl_h[h]) for h in heads]
            dzp_b = [(dzps[h] * scale).astype(BF16) for h in heads]
            dzc_b = [(dzcs[h] * scale).astype(BF16) for h in heads]
            dqs = [_dot(dzp_b[h], kp[h]) + _dot(dzc_b[h], kc[h]) for h in heads]
            dkp = [_dot_tn(dzp_b[h], qh[h]) for h in heads]
            dkc = [_dot_tn(dzc_b[h], qh[h]) for h in heads]
            for h in heads:
                db_ref[h, :, 0:bq] += dzps[h]
                db_ref[h, :, bq:2 * bq] += dzcs[h]
            def pair_sums(per_head):
                return jnp.concatenate([per_head[h] + per_head[h + 1] for h in heads[::2]], axis=1)

            dq_ref[pl.ds(qs, bq), :] = _merge_tiles(dqs, own).astype(BF16)
            dk_acc[pl.ds(ps, bq), :] += pair_sums(dkp)
            dk_acc[pl.ds(qs, bq), :] += pair_sums(dkc)
            dv_acc[pl.ds(ps, bq), :] += pair_sums(dvp)
            dv_acc[pl.ds(qs, bq), :] += pair_sums(dvc)
            return 0

        lax.fori_loop(0, n_blk, block, 0, unroll=2)
        dk_ref[...] = dk_acc[...].astype(BF16)
        dv_ref[...] = dv_acc[...].astype(BF16)

    def spec(offset):
        return pl.BlockSpec((seq, DIL_WIDTH), lambda p, b: (b, offset + p))

    bias_spec = pl.BlockSpec((DIL_HEADS, bq, 2 * bq), lambda p, b: (p, 0, 0))
    out = jax.ShapeDtypeStruct((t, n_pairs * DIL_WIDTH), BF16)
    return pl.pallas_call(
        body, name=f"dil_bwd{dil}", grid=(n_pairs, nb),
        in_specs=[spec(0), spec(n_pairs), spec(2 * n_pairs), bias_spec, spec(0), spec(0), spec(0)],
        out_specs=[spec(0), spec(0), spec(0), bias_spec],
        out_shape=[out, out, out, jax.ShapeDtypeStruct(bias.shape, F32)],
        scratch_shapes=[pltpu.VMEM((seq, DIL_WIDTH), F32), pltpu.VMEM((seq, DIL_WIDTH), F32)],
        compiler_params=_params("arbitrary", "arbitrary"),
    )(qkv, qkv, qkv, bias, do, lse, delta)


def _head_blocks(width):
    rows = lax.broadcasted_iota(jnp.int32, (width, width), 0) // HEAD_DIM
    cols = lax.broadcasted_iota(jnp.int32, (width, width), 1) // HEAD_DIM
    return (rows == cols).astype(BF16)


def _head_mean(v, gmat):
    return _split_dot(v, gmat) * (1.0 / HEAD_DIM)


def _residue_views(arrays, nb, seq):
    return [a if dil == 1 else a.reshape(nb, dil, seq // dil, a.shape[1]) for a, dil in zip(arrays, DILATIONS)]


def _mix_out_fwd(osb, ocs, lses, gsb, gdil, wout, x, mod, tm):
    t, d = x.shape
    ds = osb.shape[1]
    nt = t // tm
    nb = mod.shape[0]
    tpb = nt // nb
    seq = t // nb
    n_cfg = len(DILATIONS)

    def body(osb_ref, *refs):
        oc_refs, lse_refs = refs[:n_cfg], refs[n_cfg:2 * n_cfg]
        gsb_ref, gdil_ref, w_ref, x_ref, mod_ref = refs[2 * n_cfg:2 * n_cfg + 5]
        xo_ref, on_ref, m_ref, odil_ref = refs[2 * n_cfg + 5:2 * n_cfg + 9]
        ld_refs = refs[2 * n_cfg + 9:3 * n_cfg + 9]
        stages, sc = refs[3 * n_cfg + 9:]
        ocv, lsev = [oc_refs[0][...]], [lse_refs[0][...]]
        for i, dil in enumerate(DILATIONS[1:]):
            ocv.append(_from_residue_rows(oc_refs[i + 1], stages.at[2 * i], dil))
            lsev.append(_from_residue_rows(lse_refs[i + 1], stages.at[2 * i + 1], dil))
        top = functools.reduce(jnp.maximum, lsev)
        total = top + jnp.log(sum(jnp.exp(l - top) for l in lsev))
        odil = sum(jnp.exp(l - total) * o for o, l in zip(ocv, lsev))
        odil_ref[...] = odil
        ld_refs[0][...] = total
        _stage(total, sc)
        for ref, dil in zip(ld_refs[1:], DILATIONS[1:]):
            _to_residue_rows(sc, ref, dil)
        gm = _head_blocks(ds)
        parts = []
        for o, g_ref in ((osb_ref[...], gsb_ref), (odil, gdil_ref)):
            parts.append(o * lax.rsqrt(_head_mean(o * o, gm) + EPS) * g_ref[...])
        on = jnp.concatenate(parts, axis=1).astype(BF16)
        on_ref[...] = on
        m = _dot(on, w_ref[...])
        m_ref[...] = m
        xo_ref[...] = x_ref[...] + mod_ref[5:6, :] * m

    tok = pl.BlockSpec((tm, d), lambda i: (i, 0))
    hd = pl.BlockSpec((tm, ds), lambda i: (i, 0))
    res = [hd] + [_residue_spec(tm, tpb, ds, dil, lambda i: 0) for dil in DILATIONS[1:]]
    res_shape = [jax.ShapeDtypeStruct((t, ds), F32)] + [_residue_shape(nb, seq, ds, dil, F32) for dil in DILATIONS[1:]]
    gain = pl.BlockSpec((1, ds), lambda i: (0, 0))
    outs = pl.pallas_call(
        body, name="mix_out_fwd", grid=(nt,),
        in_specs=[hd] + res + res + [gain, gain,
                  pl.BlockSpec(wout.shape, lambda i: (0, 0)),
                  tok, pl.BlockSpec((None, N_MOD, d), lambda i: (i // tpb, 0, 0))],
        out_specs=[tok, pl.BlockSpec((tm, 2 * ds), lambda i: (i, 0)), tok, hd] + res,
        out_shape=[jax.ShapeDtypeStruct((t, d), F32), jax.ShapeDtypeStruct((t, 2 * ds), BF16),
                   jax.ShapeDtypeStruct((t, d), F32), jax.ShapeDtypeStruct((t, ds), F32)] + res_shape,
        scratch_shapes=[pltpu.VMEM((2 * (n_cfg - 1), ds // LANES, tm, LANES), F32), _stage_shape(tm, ds)],
        compiler_params=_params("arbitrary"),
    )(osb, *_residue_views(ocs, nb, seq), *_residue_views(lses, nb, seq), gsb, gdil, wout, x, mod)
    return outs[0], outs[1], outs[2], outs[3], [a.reshape(t, ds) for a in outs[4:]]


def _mix_out_bwd(dxo, m, mod, wout, osb, odil, gsb, gdil, tm):
    t, d = dxo.shape
    ds = osb.shape[1]
    nt = t // tm
    nb = mod.shape[0]
    tpb = nt // nb
    seq = t // nb
    n_cfg = len(DILATIONS)

    def body(dxo_ref, m_ref, mod_ref, w_ref, osb_ref, odil_ref, gsb_ref, gdil_ref,
             dm_ref, dosb_ref, *rest):
        do_refs, dl_refs = rest[:n_cfg], rest[n_cfg:2 * n_cfg]
        dmod_ref, dg_ref, sc = rest[2 * n_cfg:]
        dodil_ref, dldil_ref = do_refs[0], dl_refs[0]
        i = pl.program_id(0)
        dxo_v = dxo_ref[...]
        dm = (mod_ref[5:6, :] * dxo_v).astype(BF16)
        dm_ref[...] = dm
        dgt = jnp.sum(m_ref[...] * dxo_v, axis=0, keepdims=True)
        don = _dot_nt(dm, w_ref[...])
        gm = _head_blocks(ds)

        @pl.when(i % tpb == 0)
        def _():
            dmod_ref[...] = jnp.zeros_like(dmod_ref)

        @pl.when(i == 0)
        def _():
            dg_ref[...] = jnp.zeros_like(dg_ref)

        dmod_ref[2:3, :] += dgt
        groups = ((osb_ref, gsb_ref, dosb_ref), (odil_ref, gdil_ref, dodil_ref))
        for k, (o_ref, g_ref, do_ref) in enumerate(groups):
            o = o_ref[...]
            dn_out = don[:, k * ds:(k + 1) * ds]
            r = lax.rsqrt(_head_mean(o * o, gm) + EPS)
            n = o * r
            dg_ref[0:1, k * ds:(k + 1) * ds] += jnp.sum(dn_out * n, axis=0, keepdims=True)
            dn = dn_out * g_ref[...]
            do = r * (dn - n * _head_mean(dn * n, gm))
            do_ref[...] = do
            if k == 1:
                delta = _head_mean(do * o, gm) * float(HEAD_DIM)
                dldil_ref[...] = delta
                for value, refs in ((do, do_refs), (delta, dl_refs)):
                    _stage(value, sc)
                    for ref, dil in zip(refs[1:], DILATIONS[1:]):
                        _to_residue_rows(sc, ref, dil)

    tok = pl.BlockSpec((tm, d), lambda i: (i, 0))
    hd = pl.BlockSpec((tm, ds), lambda i: (i, 0))
    res = [hd] + [_residue_spec(tm, tpb, ds, dil, lambda i: 0) for dil in DILATIONS[1:]]
    res_shape = [jax.ShapeDtypeStruct((t, ds), F32)] + [_residue_shape(nb, seq, ds, dil, F32) for dil in DILATIONS[1:]]
    gain = pl.BlockSpec((1, ds), lambda i: (0, 0))
    outs = pl.pallas_call(
        body, name="mix_out_bwd", grid=(nt,),
        in_specs=[tok, tok, pl.BlockSpec((None, N_MOD, d), lambda i: (i // tpb, 0, 0)),
                  pl.BlockSpec(wout.shape, lambda i: (0, 0)), hd, hd, gain, gain],
        out_specs=[tok, hd] + res + res
        + [pl.BlockSpec((None, 8, d), lambda i: (i // tpb, 0, 0)), pl.BlockSpec((8, 2 * ds), lambda i: (0, 0))],
        out_shape=[jax.ShapeDtypeStruct((t, d), BF16), jax.ShapeDtypeStruct((t, ds), F32)] + res_shape + res_shape
        + [jax.ShapeDtypeStruct((nb, 8, d), F32), jax.ShapeDtypeStruct((8, 2 * ds), F32)],
        scratch_shapes=[_stage_shape(tm, ds)],
        compiler_params=_params("arbitrary"),
    )(dxo, m, mod, wout, osb, odil, gsb, gdil)
    flat = [a.reshape(t, ds) for a in outs[2:2 + 2 * n_cfg]]
    return outs[0], outs[1], flat[:n_cfg], flat[n_cfg:], outs[-2], outs[-1]


def _merge_dqkv(sb_parts, dil_parts, nb, tm):
    t, ds = sb_parts[0].shape
    nt = t // tm
    tpb = nt // nb
    seq = t // nb
    n_cfg = len(DILATIONS)

    def body(*refs):
        sb_refs, dil_refs = refs[:3], refs[3:3 + 3 * n_cfg]
        o_ref, sc = refs[3 + 3 * n_cfg:]
        for k in range(3):
            o_ref[:, k * ds:(k + 1) * ds] = sb_refs[k][...]
            total = dil_refs[k * n_cfg][...].astype(F32)
            for i, dil in enumerate(DILATIONS[1:]):
                total = total + _from_residue_rows(dil_refs[k * n_cfg + i + 1], sc, dil)
            o_ref[:, (3 + k) * ds:(4 + k) * ds] = total.astype(BF16)

    hd = pl.BlockSpec((tm, ds), lambda i: (i, 0))
    res = [hd] + [_residue_spec(tm, tpb, ds, dil, lambda i: 0) for dil in DILATIONS[1:]]
    views = [v for parts in dil_parts for v in _residue_views(parts, nb, seq)]
    return pl.pallas_call(
        body, name="merge_dqkv", grid=(nt,),
        in_specs=[hd] * 3 + res * 3,
        out_specs=pl.BlockSpec((tm, 6 * ds), lambda i: (i, 0)),
        out_shape=jax.ShapeDtypeStruct((t, 6 * ds), BF16),
        scratch_shapes=[_stage_shape(tm, ds)],
        compiler_params=_params("arbitrary"),
    )(*sb_parts, *views)


def _row_tile(rows):
    if rows <= 256:
        return rows
    for cand in range(256, 15, -16):
        if rows % cand == 0:
            return cand
    return rows


def _adamw(w, parts, m, v, name, transposed=False):
    rows, cols = w.shape
    n_parts = parts.shape[0]
    tr = _row_tile(rows)
    c1 = 1.0 / (1.0 - ADAM_B1 ** ADAM_STEP)
    c2 = 1.0 / (1.0 - ADAM_B2 ** ADAM_STEP)

    def body(w_ref, p_ref, m_ref, v_ref, g_ref, d_ref, nm_ref, nv_ref):
        g = p_ref[0].astype(F32)
        for i in range(1, n_parts):
            g = g + p_ref[i].astype(F32)
        wv, mv, vv = w_ref[...], m_ref[...], v_ref[...]
        if transposed:
            wv, mv, vv = wv.T, mv.T, vv.T
        nm = ADAM_B1 * mv + (1.0 - ADAM_B1) * g
        nv = ADAM_B2 * vv + (1.0 - ADAM_B2) * (g * g)
        g_ref[...] = g
        nm_ref[...] = nm
        nv_ref[...] = nv
        d_ref[...] = -ADAM_LR * ((nm * c1) / (jnp.sqrt(nv * c2) + ADAM_EPS) + ADAM_WD * wv)

    blk = pl.BlockSpec((tr, cols), lambda i: (i, 0))
    if transposed:
        oblk = pl.BlockSpec((cols, tr), lambda i: (0, i))
        pblk = pl.BlockSpec((n_parts, cols, tr), lambda i: (0, 0, i))
        out = jax.ShapeDtypeStruct((cols, rows), F32)
    else:
        oblk, pblk = blk, pl.BlockSpec((n_parts, tr, cols), lambda i: (0, i, 0))
        out = jax.ShapeDtypeStruct((rows, cols), F32)
    return pl.pallas_call(
        body, name=name, grid=(rows // tr,),
        in_specs=[blk, pblk, blk, blk],
        out_specs=[oblk, oblk, oblk, oblk], out_shape=[out, out, out, out],
        compiler_params=_params("arbitrary"),
    )(w, parts, m, v)


def _t5_bucket(n):
    max_exact = N_BUCKETS // 2
    nf = np.maximum(n, 1).astype(np.float32)
    large = max_exact + (np.log(nf / max_exact) / math.log(MAX_DISTANCE / max_exact)
                         * (N_BUCKETS - max_exact)).astype(np.int32)
    large = np.minimum(large, N_BUCKETS - 1)
    return np.where(n < max_exact, n, large).astype(np.int32)


def _bucket_onehot():
    table = np.zeros((len(DILATIONS), 2 * DIL_BLOCK + 1, N_BUCKETS), np.float32)
    for i, dil in enumerate(DILATIONS):
        buckets = _t5_bucket(np.arange(DIL_BLOCK + 1) * dil)
        for m in range(DIL_BLOCK + 1):
            table[i, m, buckets[DIL_BLOCK - m]] = 1.0
    return table


def _bias_blocks(rel_bias):
    row = jnp.einsum("cmn,nh->chm", _bucket_onehot(), rel_bias, precision=lax.Precision.HIGHEST)
    n_cfg, n_heads, width = row.shape
    tiled = jnp.tile(row, (1, 1, DIL_BLOCK))[..., :DIL_BLOCK * (width - 1)]
    return tiled.reshape(n_cfg, n_heads, DIL_BLOCK, width - 1)


def _bias_blocks_bwd(dblocks):
    n_cfg, n_heads = dblocks.shape[:2]
    width = 2 * DIL_BLOCK + 1
    flat = dblocks.reshape(n_cfg, n_heads, DIL_BLOCK * (width - 1))
    flat = jnp.pad(flat, ((0, 0), (0, 0), (0, DIL_BLOCK)))
    drow = jnp.sum(flat.reshape(n_cfg, n_heads, DIL_BLOCK, width), axis=2)
    return jnp.einsum("chm,cmn->nh", drow, _bucket_onehot(), precision=lax.Precision.HIGHEST)


def _pad_to(a, axis, size):
    pad = [(0, 0)] * a.ndim
    pad[axis] = (0, size - a.shape[axis])
    return jnp.pad(a, pad)


def _lane_pad(n):
    return -(-n // LANES) * LANES


def _local_step(x, target, mod, gains, weights, rel_bias, tm, distributed):
    nb, seq, d = x.shape
    t = nb * seq
    g_ffn1, g_mix, g_sb, g_dil, g_ffn2, g_final = gains
    wg1, wu1, wd1 = weights[:3]
    x0 = x.reshape(t, d)
    ds = g_sb.shape[1]
    bias = _bias_blocks(rel_bias)

    def beside(arrays, scatter):
        return _Exchange(arrays, scatter) if distributed else None

    tp, tg = min(PROJ_TILE, seq), min(GRAD_TILE, t)

    (x1, f1, gate1, up1), got = _ffn_fwd(x0, mod, g_ffn1, wg1, wu1, wd1, 0, tp, beside(weights[3:4], False))
    win = got[0] if distributed else weights[3]
    (qkv, qkvd, h2), got = _qkv_fwd(x1, mod, g_mix, win, tp, beside(weights[4:5], False))
    wout = got[0] if distributed else weights[4]
    wout2 = wout.reshape(-1, d)
    (osb, csb), got = _sb_fwd(qkv, nb, seq, beside(weights[5:7], False))
    wg2, wu2 = got if distributed else weights[5:7]
    n_cfg = len(DILATIONS)
    piece = -(-weights[7].shape[-2] // n_cfg // 16) * 16
    ocs, lses, wd2_pieces = [], [], []
    for i, dil in enumerate(DILATIONS):
        rows = weights[7][..., i * piece:(i + 1) * piece, :]
        (oc, lse), got = _dil_fwd(qkvd[i], bias[i], nb, seq, dil, beside([rows], False))
        wd2_pieces.append(got[0] if distributed else rows)
        ocs.append(oc)
        lses.append(lse)
    wd2 = jnp.concatenate(wd2_pieces, axis=-2)
    x2, on, mix, odil, ldil = _mix_out_fwd(osb, ocs, lses, g_sb, g_dil, wout2, x1, mod, tm)
    (dx3, f3, gate3, up3, head), _ = _ffn_fwd(x2, mod, g_ffn2, wg2, wu2, wd2, 2, tp,
                                              head=(target.reshape(t, d), g_final))
    loss_sum = 0.5 * jnp.sum(head[0]) / d
    dg_final = head[1:2]

    (dx2, dgate3, dup3, act3, h3, df3, dmod3, dg_ffn2), _ = _ffn_bwd(
        dx3, x2, f3, mod, g_ffn2, gate3, up3, wg2, wu2, wd2, 2, tm)
    gwg2, gwu2, gwd2 = _ffn_weight_grads(h3, dgate3, dup3, act3, df3, tg, 2)

    dm, dosb, dodil, dldil, dmod2b, dg_heads = _mix_out_bwd(
        dx2, mix, mod, wout2, osb, odil, g_sb, g_dil, tm)
    gwout = _mm_tn(on, dm,
                   pl.BlockSpec((tg, wout.shape[1]), lambda i, j: (i, j)),
                   pl.BlockSpec((tg, d), lambda i, j: (i, 0)),
                   wout.shape, t // tg, "grad_wout")

    (dq_sb, dk_sb, dv_sb), parts_late = _sb_bwd(qkv, dosb, csb, nb, seq,
                                                beside([gwout, gwg2, gwu2, gwd2], True))
    dil_grads = [_dil_bwd(qkvd[i], bias[i], dodil[i], ldil[i], dldil[i], nb, seq, dil)
                 for i, dil in enumerate(DILATIONS)]
    dqkv = _merge_dqkv([dq_sb, dk_sb, dv_sb], [[g[k] for g in dil_grads] for k in range(3)], nb, tm)
    drel = _bias_blocks_bwd(jnp.stack([g[3] for g in dil_grads]))

    cs = win.shape[2]
    gwin = _mm_tn(h2, dqkv,
                  pl.BlockSpec((tg, d), lambda i, j: (i, 0)),
                  pl.BlockSpec((tg, cs), lambda i, j: (i, j)),
                  win.shape, t // tg, "grad_win", pair_reduce=distributed)
    (dx1, dmod2a, dg_mix), parts_mid = _qkv_bwd(
        dqkv, dx2, x1, mod, g_mix, win, tp, _Exchange([gwin], True, chips=[True]) if distributed else None)

    (dx0, dgate1, dup1, act1, h1, df1, dmod1, dg_ffn1), _ = _ffn_bwd(
        dx1, x0, f1, mod, g_ffn1, gate1, up1, wg1, wu1, wd1, 0, tm)
    dmod = jnp.concatenate([dmod1[:, 0:3], dmod2a[:, 0:2], dmod2b[:, 2:3], dmod3[:, 0:3]], axis=1)
    ggrads = (dg_ffn1[0:1], dg_mix[0:1], dg_heads[0:1], drel, dg_ffn2[0:1], dg_final)
    if not distributed:
        gw1 = _ffn_weight_grads(h1, dgate1, dup1, act1, df1, tg, 0)
        return loss_sum, dx0.reshape(nb, seq, d), tuple(gw1) + (gwin, gwout, gwg2, gwu2, gwd2), dmod, ggrads

    dg_heads_row, drel_flat = dg_heads[0:1], drel.reshape(1, -1)
    width = max(d, dg_heads_row.shape[1], drel_flat.shape[1])
    small = jnp.concatenate(
        [_pad_to(a.reshape(1, -1), 1, width)
         for a in (dg_ffn1[0:1], dg_mix[0:1], dg_ffn2[0:1], dg_final, dg_heads_row, drel_flat, loss_sum)]
        + [jnp.zeros((1, width), F32)], axis=0)
    dmod_pad = _pad_to(dmod.reshape(nb, N_MOD * d), 0, 8)
    everyone = _Exchange([jnp.broadcast_to(dmod_pad, (N_DEV,) + dmod_pad.shape),
                          jnp.broadcast_to(small, (N_DEV,) + small.shape)], True)
    sent_g, sent_u, gwd1, (dmod_all, small_all) = _ffn_weight_grads(
        h1, dgate1, dup1, act1, df1, tg, 0, stream=True, first=everyone)
    wgrads = (sent_g, sent_u, gwd1) + tuple(parts_mid + parts_late)
    return dx0.reshape(nb, seq, d), wgrads, dmod_all, small_all


def kernel(x, c, w_ada, b_ada, g_ffn1, w1_gate, w1_up, w1_down, g_mix, w_in, g_sb_out, g_dil_out, w_out, rel_bias, g_ffn2, w2_gate, w2_up, w2_down, g_final, loss_target, m_w_ada, m_b_ada, m_g_ffn1, m_w1_gate, m_w1_up, m_w1_down, m_g_mix, m_w_in, m_g_sb_out, m_g_dil_out, m_w_out, m_rel_bias, m_g_ffn2, m_w2_gate, m_w2_up, m_w2_down, m_g_final, v_w_ada, v_b_ada, v_g_ffn1, v_w1_gate, v_w1_up, v_w1_down, v_g_mix, v_w_in, v_g_sb_out, v_g_dil_out, v_w_out, v_rel_bias, v_g_ffn2, v_w2_gate, v_w2_up, v_w2_down, v_g_final):
    nb, seq, d = x.shape
    me = 4 * lax.axis_index("x") + 2 * lax.axis_index("y") + lax.axis_index("c")
    tm = min(TOKEN_TILE, seq)
    fs = w1_gate.shape[2]
    fs_pad = _lane_pad(fs)
    ada_cols = w_ada.shape[2]

    def col_shard(w):
        return _pad_to(w[0].astype(BF16), 1, fs_pad)

    def row_shard(w):
        return _pad_to(w[0].astype(BF16), 0, fs_pad)

    shards = [col_shard(w1_gate), col_shard(w1_up), row_shard(w1_down), w_in[0].astype(BF16),
              w_out[0].astype(BF16), col_shard(w2_gate), col_shard(w2_up), row_shard(w2_down)]
    b_cols = lax.dynamic_slice(b_ada, (0, me * ada_cols), (1, ada_cols))
    c_every, mod_all, first = _first_exchange(_pad_to(c, 0, 8), shards[:3], w_ada[0], b_cols)
    c_all = c_every[:, :nb].reshape(N_DEV * nb, d)
    weights = first + shards[3:]
    mod = lax.dynamic_slice(mod_all, (0, me * 8, 0), (N_DEV, nb, ada_cols))
    mod = mod.transpose(1, 0, 2).reshape(nb, N_MOD, d)

    n_sb = g_sb_out.shape[1] * g_sb_out.shape[2]
    gains = (g_ffn1, g_mix, g_sb_out.reshape(1, n_sb), g_dil_out.reshape(1, -1), g_ffn2,
             g_final.reshape(1, d))
    grad_x, parts, dmod_all, small_all = _local_step(x, loss_target, mod, gains, weights, rel_bias, tm, True)

    last_part = _exchange([parts[2]], True, "scatter_last", chips=[True])[0]
    parts = parts[:2] + (last_part,) + parts[3:]
    dmod_all = dmod_all[:, :nb].reshape(N_DEV * nb, N_MOD * d)
    dmod_cols = lax.dynamic_slice(dmod_all, (0, me * ada_cols), (N_DEV * nb, ada_cols))
    gw_ada, gb_ada = _ada_bwd(c_all, dmod_cols, dmod_all)

    def small_part(row, size, shape):
        return small_all[:, row, :size].reshape((N_DEV,) + shape)

    loss = jnp.sum(small_all[:, 6, 0])

    n_rel = rel_bias.shape
    updates = {
        "w_ada": (w_ada[0], gw_ada[None], m_w_ada[0], v_w_ada[0]),
        "b_ada": (b_ada, gb_ada[None], m_b_ada, v_b_ada),
        "g_ffn1": (g_ffn1, small_part(0, d, (1, d)), m_g_ffn1, v_g_ffn1),
        "w1_gate": (w1_gate[0], parts[0], m_w1_gate[0], v_w1_gate[0]),
        "w1_up": (w1_up[0], parts[1], m_w1_up[0], v_w1_up[0]),
        "w1_down": (w1_down[0], parts[2], m_w1_down[0], v_w1_down[0]),
        "g_mix": (g_mix, small_part(1, d, (1, d)), m_g_mix, v_g_mix),
        "w_in": (w_in[0], parts[3], m_w_in[0], v_w_in[0]),
        "g_sb_out": (g_sb_out[0], small_all[:, 4, :n_sb].reshape((N_DEV,) + g_sb_out.shape[1:]),
                     m_g_sb_out[0], v_g_sb_out[0]),
        "g_dil_out": (g_dil_out[0], small_all[:, 4, n_sb:n_sb + g_dil_out[0].size].reshape((N_DEV,) + g_dil_out.shape[1:]),
                      m_g_dil_out[0], v_g_dil_out[0]),
        "w_out": (w_out[0], parts[4], m_w_out[0], v_w_out[0]),
        "rel_bias": (rel_bias, small_part(5, rel_bias.size, n_rel), m_rel_bias, v_rel_bias),
        "g_ffn2": (g_ffn2, small_part(2, d, (1, d)), m_g_ffn2, v_g_ffn2),
        "w2_gate": (w2_gate[0], parts[5], m_w2_gate[0], v_w2_gate[0]),
        "w2_up": (w2_up[0], parts[6], m_w2_up[0], v_w2_up[0]),
        "w2_down": (w2_down[0], parts[7], m_w2_down[0], v_w2_down[0]),
        "g_final": (g_final.reshape(1, d), small_part(3, d, (1, d)), m_g_final.reshape(1, d), v_g_final.reshape(1, d)),
    }
    shapes = {"w_ada": w_ada.shape, "b_ada": b_ada.shape, "g_ffn1": g_ffn1.shape, "w1_gate": w1_gate.shape,
              "w1_up": w1_up.shape, "w1_down": w1_down.shape, "g_mix": g_mix.shape, "w_in": w_in.shape,
              "g_sb_out": g_sb_out.shape, "g_dil_out": g_dil_out.shape, "w_out": w_out.shape,
              "rel_bias": rel_bias.shape, "g_ffn2": g_ffn2.shape, "w2_gate": w2_gate.shape,
              "w2_up": w2_up.shape, "w2_down": w2_down.shape, "g_final": g_final.shape}
    grads, deltas, new_m, new_v = [], [], [], []
    for name, (w, p, m, v) in updates.items():
        transposed = name in ("w1_gate", "w1_up", "w2_gate", "w2_up")
        outs = _adamw(w, p, m, v, f"adamw_{name}", transposed)
        for dst, a in zip((grads, deltas, new_m, new_v), outs):
            dst.append((a.T if transposed else a).reshape(shapes[name]))
    return (loss, grad_x, *grads, *deltas, *new_m, *new_v)
```

```python
import functools
import math

import numpy as np
import jax
import jax.numpy as jnp
from jax import lax
from jax.experimental import pallas as pl
from jax.experimental.pallas import tpu as pltpu

F32 = jnp.float32
BF16 = jnp.bfloat16

EPS = 1e-6
NEG_INF = -1e30
HEAD_DIM = 64
LANES = 128
DIL_BLOCK = 128
DILATIONS = (1, 4, 16)
N_BUCKETS = 32
MAX_DISTANCE = 2048
N_MOD = 9
N_DEV = 8
SB_BLOCK = 256
SB_HEADS = 4
SB_WIDTH = SB_HEADS * HEAD_DIM
DIL_HEADS = 4
DIL_WIDTH = DIL_HEADS * HEAD_DIM
TOKEN_TILE = 512
PROJ_TILE = 1024
GRAD_TILE = 1024
SHARD_GROUP = 2
FFN_CHUNKS = 2
VMEM_LIMIT_BYTES = 56 * 1024 * 1024

ADAM_LR = 0.001
ADAM_B1 = 0.9
ADAM_B2 = 0.999
ADAM_EPS = 1e-08
ADAM_WD = 0.01
ADAM_STEP = 10

NT_DIMS = (((1,), (1,)), ((), ()))
TN_DIMS = (((0,), (0,)), ((), ()))


def _params(*sem):
    return pltpu.CompilerParams(dimension_semantics=sem, vmem_limit_bytes=VMEM_LIMIT_BYTES)


def _once(spec):
    return pl.BlockSpec(spec.block_shape, spec.index_map, pipeline_mode=pl.Buffered(1))


def _dot(a, b):
    return jnp.dot(a, b, preferred_element_type=F32)


def _dot_nt(a, b):
    return lax.dot_general(a, b, NT_DIMS, preferred_element_type=F32)


def _dot_tn(a, b):
    return lax.dot_general(a, b, TN_DIMS, preferred_element_type=F32)


def _split_dot(a, b):
    hi = a.astype(BF16)
    lo = (a - hi.astype(F32)).astype(BF16)
    return _dot(hi, b) + _dot(lo, b)


def _sigmoid(z):
    return 1.0 / (1.0 + jnp.exp(-z))


def _norm(x):
    r = lax.rsqrt(jnp.mean(x * x, axis=-1, keepdims=True) + EPS)
    return x * r, r


def _modulate(x, g, mod_ref, k):
    n, _ = _norm(x)
    shift = mod_ref[3 * k:3 * k + 1, :]
    scale = mod_ref[3 * k + 1:3 * k + 2, :]
    return n * g * (1.0 + scale) + shift


def _modulate_bwd(dh, x, g, mod_ref, k):
    n, r = _norm(x)
    scale = mod_ref[3 * k + 1:3 * k + 2, :]
    dshift = jnp.sum(dh, axis=0, keepdims=True)
    dscale = jnp.sum(dh * n * g, axis=0, keepdims=True)
    dg = jnp.sum(dh * n * (1.0 + scale), axis=0, keepdims=True)
    dn = dh * g * (1.0 + scale)
    dx = r * (dn - n * jnp.mean(dn * n, axis=-1, keepdims=True))
    return dx, dshift, dscale, dg


class _Exchange:
    def __init__(self, arrays, scatter, relay=False, chips=None):
        assert not (scatter and relay)
        self.arrays = list(arrays)
        self.scatter = scatter
        self.relay = relay
        self.n = len(self.arrays)
        self.chips = list(chips) if chips is not None else [False] * self.n
        assert scatter or not any(self.chips)
        self.out_shape = [
            jax.ShapeDtypeStruct((N_DEV // 2 if ch else N_DEV,) + tuple(a.shape[1:] if scatter else a.shape), a.dtype)
            for a, ch in zip(self.arrays, self.chips)]
        n_remote = self.n * (N_DEV - 1)
        self.scratch_shapes = [pltpu.SemaphoreType.DMA((n_remote,)), pltpu.SemaphoreType.DMA((n_remote,)),
                               pltpu.SemaphoreType.DMA((self.n,))]

    def _copies(self, in_refs, out_refs, sems):
        send_sems, recv_sems, local_sems = sems
        x, y, c = lax.axis_index("x"), lax.axis_index("y"), lax.axis_index("c")
        me = 4 * x + 2 * y + c
        local, remote, relayed = [], {}, {}
        for a in range(self.n):
            if self.chips[a]:
                mine = 2 * x + y
                local.append(pltpu.make_async_copy(in_refs[a].at[mine], out_refs[a].at[mine], local_sems.at[a]))
                for k in (2, 4, 6):
                    px = 1 - x if k & 4 else x
                    py = 1 - y if k & 2 else y
                    sem = a * (N_DEV - 1) + k - 1
                    remote[a, k] = pltpu.make_async_remote_copy(
                        src_ref=in_refs[a].at[2 * px + py], dst_ref=out_refs[a].at[mine],
                        send_sem=send_sems.at[sem], recv_sem=recv_sems.at[sem],
                        device_id=(px, py, c), device_id_type=pl.DeviceIdType.MESH)
                continue
            src = in_refs[a].at[me] if self.scatter else in_refs[a]
            local.append(pltpu.make_async_copy(src, out_refs[a].at[me], local_sems.at[a]))
            for k in range(1, N_DEV):
                px = 1 - x if k & 4 else x
                py = 1 - y if k & 2 else y
                pc = 1 - c if k & 1 else c
                sem = a * (N_DEV - 1) + k - 1
                if self.relay and k & 1 and k > 1:
                    slot = 4 * px + 2 * py + c
                    relayed[a, k] = pltpu.make_async_remote_copy(
                        src_ref=out_refs[a].at[slot], dst_ref=out_refs[a].at[slot],
                        send_sem=send_sems.at[sem], recv_sem=recv_sems.at[sem],
                        device_id=(x, y, 1 - c), device_id_type=pl.DeviceIdType.MESH)
                    continue
                src = in_refs[a].at[4 * px + 2 * py + pc] if self.scatter else in_refs[a]
                remote[a, k] = pltpu.make_async_remote_copy(
                    src_ref=src, dst_ref=out_refs[a].at[me],
                    send_sem=send_sems.at[sem], recv_sem=recv_sems.at[sem],
                    device_id=(px, py, pc), device_id_type=pl.DeviceIdType.MESH)
        return local, remote, relayed

    def start(self, in_refs, out_refs, sems):
        local, remote, _ = self._copies(in_refs, out_refs, sems)
        for cp in local + list(remote.values()):
            cp.start()

    def wait(self, in_refs, out_refs, sems):
        local, remote, relayed = self._copies(in_refs, out_refs, sems)
        for (a, k), cp in relayed.items():
            remote[a, k - 1].wait_recv()
            cp.start()
        for (a, k), cp in remote.items():
            if (a, k + 1) not in relayed:
                cp.wait_recv()
        for cp in relayed.values():
            cp.wait_recv()
        for cp in list(remote.values()) + list(relayed.values()):
            cp.wait_send()
        for cp in local:
            cp.wait()


def _call(body, *, name, args, in_specs, out_specs, out_shape, scratch_shapes=(), grid=(),
          params=None, exchange=None):
    n_in, n_out = len(args), len(out_shape)
    if exchange is None:
        outs = pl.pallas_call(
            body, name=name, grid=grid, in_specs=list(in_specs), out_specs=list(out_specs),
            out_shape=list(out_shape), scratch_shapes=list(scratch_shapes), compiler_params=params,
        )(*args)
        return list(outs), []
    n_ex = exchange.n

    def wrapped(*refs):
        ins, refs = refs[:n_in], refs[n_in:]
        ex_in, refs = refs[:n_ex], refs[n_ex:]
        outs, refs = refs[:n_out], refs[n_out:]
        ex_out, refs = refs[:n_ex], refs[n_ex:]
        scratch, sems = refs[:len(refs) - 3], refs[len(refs) - 3:]
        if not grid:
            exchange.start(ex_in, ex_out, sems)
            body(*ins, *outs, *scratch)
            exchange.wait(ex_in, ex_out, sems)
            return
        first = functools.reduce(jnp.logical_and, [pl.program_id(a) == 0 for a in range(len(grid))])
        last = functools.reduce(jnp.logical_and, [pl.program_id(a) == grid[a] - 1 for a in range(len(grid))])

        @pl.when(first)
        def _():
            exchange.start(ex_in, ex_out, sems)

        body(*ins, *outs, *scratch)

        @pl.when(last)
        def _():
            exchange.wait(ex_in, ex_out, sems)

    any_spec = pl.BlockSpec(memory_space=pl.ANY)
    outs = pl.pallas_call(
        wrapped, name=name, grid=grid,
        in_specs=list(in_specs) + [any_spec] * n_ex, out_specs=list(out_specs) + [any_spec] * n_ex,
        out_shape=list(out_shape) + exchange.out_shape,
        scratch_shapes=list(scratch_shapes) + exchange.scratch_shapes, compiler_params=params,
    )(*args, *exchange.arrays)
    return list(outs[:n_out]), list(outs[n_out:])


def _exchange(arrays, scatter, name, relay=False, chips=None):
    return _call(lambda: None, name=name, args=(), in_specs=(), out_specs=(), out_shape=(),
                 exchange=_Exchange(arrays, scatter, relay, chips))[1]


def _first_exchange(c_pad, shards, w, b):
    rows, d = c_pad.shape
    cols = w.shape[1]
    ex_c = _Exchange([c_pad], False)
    ex_w = _Exchange(shards, False, relay=True)
    ex_m = _Exchange([jax.ShapeDtypeStruct((N_DEV * rows, cols), F32)], False)
    n_w = ex_w.n

    def body(*refs):
        c_ref, w_refs, wa_ref, b_ref = refs[0], refs[1:1 + n_w], refs[1 + n_w], refs[2 + n_w]
        outs = refs[3 + n_w:]
        cg_ref, wg_refs, mg_ref = outs[0], outs[1:1 + n_w], outs[1 + n_w]
        scratch = outs[2 + n_w:]
        sems_c, sems_w, sems_m, c_vm, m_vm = scratch[0:3], scratch[3:6], scratch[6:9], scratch[9], scratch[10]
        ex_c.start([c_ref], [cg_ref], sems_c)
        ex_c.wait([c_ref], [cg_ref], sems_c)
        pltpu.sync_copy(cg_ref, c_vm)
        cv = c_vm[...].reshape(N_DEV * rows, d)
        s = (cv * _sigmoid(cv)).astype(BF16)
        m_vm[...] = _dot(s, wa_ref[...].astype(BF16)) + b_ref[...]
        ex_m.start([m_vm], [mg_ref], sems_m)
        ex_w.start(w_refs, wg_refs, sems_w)
        ex_m.wait([m_vm], [mg_ref], sems_m)
        ex_w.wait(w_refs, wg_refs, sems_w)

    any_spec = pl.BlockSpec(memory_space=pl.ANY)
    vmem_spec = pl.BlockSpec(memory_space=pltpu.VMEM)
    outs = pl.pallas_call(
        body, name="first_exchange",
        in_specs=[any_spec] * (1 + n_w) + [vmem_spec, vmem_spec],
        out_specs=[any_spec] * (2 + n_w),
        out_shape=ex_c.out_shape + ex_w.out_shape + ex_m.out_shape,
        scratch_shapes=ex_c.scratch_shapes + ex_w.scratch_shapes + ex_m.scratch_shapes
        + [pltpu.VMEM((N_DEV, rows, d), F32), pltpu.VMEM((N_DEV * rows, cols), F32)],
        compiler_params=pltpu.CompilerParams(vmem_limit_bytes=VMEM_LIMIT_BYTES),
    )(c_pad, *shards, w, b)
    return outs[0], outs[1 + n_w], list(outs[1:1 + n_w])


def _ada_bwd(c_all, dmod_cols, dmod_all):
    def body(c_ref, dc_ref, da_ref, gw_ref, gb_ref):
        cv = c_ref[...]
        s = cv * _sigmoid(cv)
        gw_ref[...] = lax.dot_general(s, dc_ref[...], TN_DIMS, preferred_element_type=F32,
                                      precision=lax.Precision.HIGHEST)
        gb_ref[...] = jnp.sum(da_ref[...], axis=0, keepdims=True)

    return pl.pallas_call(
        body, name="ada_bwd",
        out_shape=(jax.ShapeDtypeStruct((c_all.shape[1], dmod_cols.shape[1]), F32),
                   jax.ShapeDtypeStruct((1, dmod_all.shape[1]), F32)),
        compiler_params=pltpu.CompilerParams(vmem_limit_bytes=VMEM_LIMIT_BYTES),
    )(c_all, dmod_cols, dmod_all)


def _side_by_side(w_ref):
    return jnp.concatenate([w_ref[s] for s in range(w_ref.shape[0])], axis=1)


def _stacked(w_ref):
    return jnp.concatenate([w_ref[s] for s in range(w_ref.shape[0])], axis=0)


def _loss_tile(x, target, g, acc_ref):
    d = x.shape[1]
    n, r = _norm(x)
    err = n * g - target
    dy = err * (1.0 / d)
    acc_ref[0:1, :] += jnp.sum(err * err, axis=0, keepdims=True)
    acc_ref[1:2, :] += jnp.sum(dy * n, axis=0, keepdims=True)
    dn = dy * g
    return r * (dn - n * jnp.mean(dn * n, axis=-1, keepdims=True))


def _ffn_fwd(x, mod, g, wg, wu, wd, k, tm, exchange=None, head=None):
    t, d = x.shape
    ns, _, fs = wg.shape
    nt = t // tm
    tpb = nt // mod.shape[0]
    rows = tm // FFN_CHUNKS
    extra = list(head) if head is not None else []

    def body(x_ref, mod_ref, g_ref, wg_ref, wu_ref, wd_ref, *rest):
        if head is not None:
            t_ref, gf_ref, xo_ref, f_ref, gg_ref, uu_ref, head_ref, h_sc, acc = rest
        else:
            xo_ref, f_ref, gg_ref, uu_ref, h_sc, acc = rest
        i, j = pl.program_id(0), pl.program_id(1)

        @pl.when(j == 0)
        def _():
            h_sc[...] = _modulate(x_ref[...], g_ref[...], mod_ref, k).astype(BF16)
            acc[...] = jnp.zeros_like(acc)

        chunks = [pl.ds(c * rows, rows) for c in range(FFN_CHUNKS)]
        wg, wu, wd = _side_by_side(wg_ref), _side_by_side(wu_ref), _stacked(wd_ref)
        gates, ups = [], []
        for rs in chunks:
            h = h_sc[rs, :]
            gates.append(_dot(h, wg))
            ups.append(_dot(h, wu))
        acts = [(g * _sigmoid(g) * u).astype(BF16) for g, u in zip(gates, ups)]
        for rs, g, u in zip(chunks, gates, ups):
            for s in range(SHARD_GROUP):
                gg_ref[s, rs, :] = g[:, s * fs:(s + 1) * fs].astype(BF16)
                uu_ref[s, rs, :] = u[:, s * fs:(s + 1) * fs].astype(BF16)
        downs = [_dot(a, wd) for a in acts]
        for rs, dn in zip(chunks, downs):
            acc[rs, :] += dn

        @pl.when(j == ns // SHARD_GROUP - 1)
        def _():
            f = acc[...]
            f_ref[...] = f.astype(BF16)
            xo = x_ref[...] + 0.5 * mod_ref[3 * k + 2:3 * k + 3, :] * f
            if head is None:
                xo_ref[...] = xo
            else:
                @pl.when(i == 0)
                def _():
                    head_ref[...] = jnp.zeros_like(head_ref)

                xo_ref[...] = _loss_tile(xo, t_ref[...], gf_ref[...], head_ref)

    tok = pl.BlockSpec((tm, d), lambda i, j: (i, 0))
    row = pl.BlockSpec((1, d), lambda i, j: (0, 0))
    hid = pl.BlockSpec((SHARD_GROUP, tm, fs), lambda i, j: (j, i, 0))
    head_specs = [_once(tok), row] if head is not None else []
    head_out = [pl.BlockSpec((8, d), lambda i, j: (0, 0))] if head is not None else []
    head_shape = [jax.ShapeDtypeStruct((8, d), F32)] if head is not None else []
    return _call(
        body, name=f"ffn_fwd{k}", grid=(nt, ns // SHARD_GROUP), args=(x, mod, g, wg, wu, wd, *extra),
        in_specs=[tok,
                  pl.BlockSpec((None, N_MOD, d), lambda i, j: (i // tpb, 0, 0)),
                  row,
                  pl.BlockSpec((SHARD_GROUP, d, fs), lambda i, j: (j, 0, 0)),
                  pl.BlockSpec((SHARD_GROUP, d, fs), lambda i, j: (j, 0, 0)),
                  pl.BlockSpec((SHARD_GROUP, fs, d), lambda i, j: (j, 0, 0))] + head_specs,
        out_specs=[tok, tok, hid, hid] + head_out,
        out_shape=[jax.ShapeDtypeStruct((t, d), F32), jax.ShapeDtypeStruct((t, d), BF16),
                   jax.ShapeDtypeStruct((ns, t, fs), BF16), jax.ShapeDtypeStruct((ns, t, fs), BF16)]
        + head_shape,
        scratch_shapes=[pltpu.VMEM((tm, d), BF16), pltpu.VMEM((tm, d), F32)],
        params=_params("arbitrary", "arbitrary"), exchange=exchange)


def _ffn_bwd(dxo, x, f, mod, g, gate, up, wg, wu, wd, k, tm, exchange=None):
    t, d = x.shape
    ns, _, fs = wg.shape
    nt = t // tm
    nb = mod.shape[0]
    tpb = nt // nb
    rows = tm // FFN_CHUNKS

    def body(dxo_ref, x_ref, f_ref, mod_ref, g_ref, gg_ref, uu_ref, wg_ref, wu_ref, wd_ref,
             dx_ref, dgg_ref, duu_ref, act_ref, h_ref, df_ref, dmod_ref, dg_ref, acc):
        i, j = pl.program_id(0), pl.program_id(1)

        @pl.when(j == 0)
        def _():
            df = 0.5 * mod_ref[3 * k + 2:3 * k + 3, :] * dxo_ref[...]
            df_ref[...] = df.astype(BF16)
            h_ref[...] = _modulate(x_ref[...], g_ref[...], mod_ref, k).astype(BF16)
            acc[...] = jnp.zeros_like(acc)

        chunks = [pl.ds(c * rows, rows) for c in range(FFN_CHUNKS)]
        group = range(SHARD_GROUP)
        wg, wu, wd = _side_by_side(wg_ref), _side_by_side(wu_ref), _stacked(wd_ref)
        dacts = [_dot_nt(df_ref[rs, :], wd) for rs in chunks]
        dgates, dups = [], []
        for rs, dact in zip(chunks, dacts):
            gv = jnp.concatenate([gg_ref[s, rs, :] for s in group], axis=1).astype(F32)
            uv = jnp.concatenate([uu_ref[s, rs, :] for s in group], axis=1).astype(F32)
            sig = _sigmoid(gv)
            s_act = gv * sig
            act = (s_act * uv).astype(BF16)
            for s in group:
                act_ref[s, rs, :] = act[:, s * fs:(s + 1) * fs]
            dups.append((dact * s_act).astype(BF16))
            dgates.append((dact * uv * (sig * (1.0 + gv * (1.0 - sig)))).astype(BF16))
        dhs = [_dot_nt(dg, wg) + _dot_nt(du, wu) for dg, du in zip(dgates, dups)]
        for rs, dg, du, dh in zip(chunks, dgates, dups, dhs):
            for s in group:
                dgg_ref[s, rs, :] = dg[:, s * fs:(s + 1) * fs]
                duu_ref[s, rs, :] = du[:, s * fs:(s + 1) * fs]
            acc[rs, :] += dh

        @pl.when(j == ns // SHARD_GROUP - 1)
        def _():
            dx, dshift, dscale, dg = _modulate_bwd(acc[...], x_ref[...], g_ref[...], mod_ref, k)
            dxo_v = dxo_ref[...]
            dx_ref[...] = dxo_v + dx
            dgt = jnp.sum(0.5 * f_ref[...].astype(F32) * dxo_v, axis=0, keepdims=True)

            @pl.when(i % tpb == 0)
            def _():
                dmod_ref[...] = jnp.zeros_like(dmod_ref)

            @pl.when(i == 0)
            def _():
                dg_ref[...] = jnp.zeros_like(dg_ref)

            dmod_ref[0:1, :] += dshift
            dmod_ref[1:2, :] += dscale
            dmod_ref[2:3, :] += dgt
            dg_ref[0:1, :] += dg

    tok = pl.BlockSpec((tm, d), lambda i, j: (i, 0))
    hid = pl.BlockSpec((SHARD_GROUP, tm, fs), lambda i, j: (j, i, 0))
    return _call(
        body, name=f"ffn_bwd{k}", grid=(nt, ns // SHARD_GROUP), args=(dxo, x, f, mod, g, gate, up, wg, wu, wd),
        in_specs=[tok, tok, tok,
                  pl.BlockSpec((None, N_MOD, d), lambda i, j: (i // tpb, 0, 0)),
                  pl.BlockSpec((1, d), lambda i, j: (0, 0)),
                  hid, hid,
                  pl.BlockSpec((SHARD_GROUP, d, fs), lambda i, j: (j, 0, 0)),
                  pl.BlockSpec((SHARD_GROUP, d, fs), lambda i, j: (j, 0, 0)),
                  pl.BlockSpec((SHARD_GROUP, fs, d), lambda i, j: (j, 0, 0))],
        out_specs=[tok, hid, hid, hid, tok, tok,
                   pl.BlockSpec((None, 8, d), lambda i, j: (i // tpb, 0, 0)),
                   pl.BlockSpec((8, d), lambda i, j: (0, 0))],
        out_shape=[jax.ShapeDtypeStruct((t, d), F32),
                   jax.ShapeDtypeStruct((ns, t, fs), BF16), jax.ShapeDtypeStruct((ns, t, fs), BF16),
                   jax.ShapeDtypeStruct((ns, t, fs), BF16),
                   jax.ShapeDtypeStruct((t, d), BF16), jax.ShapeDtypeStruct((t, d), BF16),
                   jax.ShapeDtypeStruct((nb, 8, d), F32), jax.ShapeDtypeStruct((8, d), F32)],
        scratch_shapes=[pltpu.VMEM((tm, d), F32)],
        params=_params("arbitrary", "arbitrary"), exchange=exchange)


def _mm_tn(a, b, a_spec, b_spec, out_shape, n_tiles, name, exchange=None, keep_transposed=False,
           pair_reduce=False):
    n_out = out_shape[0]
    block = tuple(out_shape[1:])
    last = n_tiles - 1
    flip = block[0] > block[1]
    if flip:
        block = block[::-1]
    if flip and keep_transposed:
        flip_back, out_shape = False, (n_out,) + block
    else:
        flip_back = flip
    full_shape = tuple(out_shape)
    n_pairs = n_out // 2
    if pair_reduce:
        out_shape = (n_pairs,) + full_shape[1:]

    def body(a_ref, b_ref, o_ref, acc, *pair):
        i, j = pl.program_id(0), pl.program_id(1)
        prod = _dot_tn(b_ref[...], a_ref[...]) if flip else _dot_tn(a_ref[...], b_ref[...])
        full_ref = pair[0] if pair_reduce else o_ref

        @pl.when(i == 0)
        def _():
            acc[j] = prod

        @pl.when(i > 0)
        def _():
            acc[j] += prod

        @pl.when(i == last)
        def _():
            total = acc[j]
            full_ref[j] = (total.T if flip_back else total).astype(BF16)

        if pair_reduce:
            _, landed, send_sems, recv_sems = pair

            @pl.when(jnp.logical_and(i == last, j == n_out - 1))
            def _():
                x, y, c = lax.axis_index("x"), lax.axis_index("y"), lax.axis_index("c")
                copies = [pltpu.make_async_remote_copy(
                    src_ref=full_ref.at[2 * q + 1 - c], dst_ref=landed.at[q],
                    send_sem=send_sems.at[q], recv_sem=recv_sems.at[q],
                    device_id=(x, y, 1 - c), device_id_type=pl.DeviceIdType.MESH) for q in range(n_pairs)]
                for cp in copies:
                    cp.start()
                for q, cp in enumerate(copies):
                    cp.wait_recv()
                    o_ref[q] = (full_ref[2 * q + c].astype(F32) + landed[q].astype(F32)).astype(BF16)
                for cp in copies:
                    cp.wait_send()

    scratch = [pltpu.VMEM((n_out,) + block, F32)]
    if pair_reduce:
        scratch += [pltpu.VMEM(full_shape, BF16), pltpu.VMEM(out_shape, BF16),
                    pltpu.SemaphoreType.DMA((n_pairs,)), pltpu.SemaphoreType.DMA((n_pairs,))]
    outs, sent = _call(
        body, name=name, grid=(n_tiles, n_out), args=(a, b), in_specs=[a_spec, b_spec],
        out_specs=[pl.BlockSpec(out_shape, lambda i, j: (0,) * len(out_shape))],
        out_shape=[jax.ShapeDtypeStruct(out_shape, BF16)],
        scratch_shapes=scratch,
        params=_params("arbitrary", "arbitrary"), exchange=exchange)
    return (outs[0], sent) if exchange is not None else outs[0]


def _ffn_weight_grads(h, dgate, dup, act, df, tm, tag, stream=False, first=None):
    t, d = h.shape
    ns, _, fs = dgate.shape
    nt = t // tm
    tok = pl.BlockSpec((tm, d), lambda i, j: (i, 0))
    hid = pl.BlockSpec((None, tm, fs), lambda i, j: (j, i, 0))
    if not stream:
        gwg = _mm_tn(h, dgate, tok, hid, (ns, d, fs), nt, f"grad_wg{tag}", keep_transposed=True)
        gwu = _mm_tn(h, dup, tok, hid, (ns, d, fs), nt, f"grad_wu{tag}", keep_transposed=True)
        gwd = _mm_tn(act, df, hid, tok, (ns, fs, d), nt, f"grad_wd{tag}")
        return gwg, gwu, gwd
    gwg, brought = _mm_tn(h, dgate, tok, hid, (ns, d, fs), nt, f"grad_wg{tag}", first,
                          keep_transposed=True, pair_reduce=True)
    gwu, sent_g = _mm_tn(h, dup, tok, hid, (ns, d, fs), nt, f"grad_wu{tag}",
                         _Exchange([gwg], True, chips=[True]), keep_transposed=True, pair_reduce=True)
    gwd, sent_u = _mm_tn(act, df, hid, tok, (ns, fs, d), nt, f"grad_wd{tag}",
                         _Exchange([gwu], True, chips=[True]), pair_reduce=True)
    return sent_g[0], sent_u[0], gwd, brought


def _stage_shape(rows, cols):
    return pltpu.VMEM((cols // LANES, rows, LANES), F32)


def _stage(value, stage_ref):
    for k in range(stage_ref.shape[0]):
        stage_ref[k] = value[:, k * LANES:(k + 1) * LANES]


def _to_residue_rows(stage_ref, dst_ref, dil):
    rows = stage_ref.shape[1] // dil
    for r in range(dil):
        for k in range(stage_ref.shape[0]):
            dst_ref[r, :, k * LANES:(k + 1) * LANES] = (
                stage_ref.at[k][pl.ds(r, rows, stride=dil), :].astype(dst_ref.dtype))


def _from_residue_rows(src_ref, stage_ref, dil):
    rows = stage_ref.shape[1] // dil
    chunks = range(stage_ref.shape[0])
    for r in range(dil):
        for k in chunks:
            stage_ref.at[k][pl.ds(r, rows, stride=dil), :] = src_ref[r, :, k * LANES:(k + 1) * LANES].astype(F32)
    return jnp.concatenate([stage_ref[k] for k in chunks], axis=1)


def _residue_shape(nb, seq, width, dil, dtype):
    return jax.ShapeDtypeStruct((nb, dil, seq // dil, width), dtype)


def _residue_spec(tm, tpb, cols, dil, col_block):
    return pl.BlockSpec((None, dil, tm // dil, cols),
                        lambda i, *rest: (i // tpb, 0, i % tpb, col_block(i, *rest)))


def _qkv_fwd(x, mod, g, win, tm, exchange=None):
    t, d = x.shape
    ns, _, cs = win.shape
    nt = t // tm
    nb = mod.shape[0]
    tpb = nt // nb
    seq = t // nb
    width = ns * cs // 2
    cs, ns = cs * SHARD_GROUP, ns // SHARD_GROUP
    half = ns // 2
    n_res = len(DILATIONS) - 1

    def body(x_ref, mod_ref, g_ref, w_ref, sb_ref, dil_ref, *rest):
        res_refs, h_ref, sc = rest[:n_res], rest[n_res], rest[n_res + 1]
        j = pl.program_id(1)

        @pl.when(j == 0)
        def _():
            h_ref[...] = _modulate(x_ref[...], g_ref[...], mod_ref, 1).astype(BF16)

        res = _dot(h_ref[...], _side_by_side(w_ref))

        @pl.when(j < half)
        def _():
            sb_ref[...] = res.astype(BF16)

        @pl.when(j >= half)
        def _():
            dil_ref[...] = res.astype(BF16)
            _stage(res, sc)
            for ref, dil in zip(res_refs, DILATIONS[1:]):
                _to_residue_rows(sc, ref, dil)

    def dil_col(i, j):
        return jnp.maximum(j - half, 0)

    tok = pl.BlockSpec((tm, d), lambda i, j: (i, 0))
    wide = jax.ShapeDtypeStruct((t, width), BF16)
    outs, got = _call(
        body, name="qkv_fwd", grid=(nt, ns), args=(x, mod, g, win),
        in_specs=[tok,
                  pl.BlockSpec((None, N_MOD, d), lambda i, j: (i // tpb, 0, 0)),
                  pl.BlockSpec((1, d), lambda i, j: (0, 0)),
                  pl.BlockSpec((SHARD_GROUP, d, cs // SHARD_GROUP), lambda i, j: (j, 0, 0))],
        out_specs=[pl.BlockSpec((tm, cs), lambda i, j: (i, jnp.minimum(j, half - 1))),
                   pl.BlockSpec((tm, cs), lambda i, j: (i, dil_col(i, j)))]
        + [_residue_spec(tm, tpb, cs, dil, dil_col) for dil in DILATIONS[1:]] + [tok],
        out_shape=[wide, wide] + [_residue_shape(nb, seq, width, dil, BF16) for dil in DILATIONS[1:]]
        + [jax.ShapeDtypeStruct((t, d), BF16)],
        scratch_shapes=[_stage_shape(tm, cs)],
        params=_params("arbitrary", "arbitrary"), exchange=exchange)
    qkv_dil = [outs[1]] + [a.reshape(t, width) for a in outs[2:2 + n_res]]
    return (outs[0], qkv_dil, outs[-1]), got


def _qkv_bwd(dqkv, dxo, x, mod, g, win, tm, exchange=None):
    t, d = x.shape
    ns, _, cs = win.shape
    nt = t // tm
    nb = mod.shape[0]
    tpb = nt // nb
    cs, ns = cs * SHARD_GROUP, ns // SHARD_GROUP

    def body(dq_ref, dxo_ref, x_ref, mod_ref, g_ref, w_ref, dx_ref, dmod_ref, dg_ref, acc):
        i, j = pl.program_id(0), pl.program_id(1)

        @pl.when(j == 0)
        def _():
            acc[...] = jnp.zeros_like(acc)

        acc[...] += _dot_nt(dq_ref[...], _side_by_side(w_ref))

        @pl.when(j == ns - 1)
        def _():
            dx, dshift, dscale, dg = _modulate_bwd(acc[...], x_ref[...], g_ref[...], mod_ref, 1)
            dx_ref[...] = dxo_ref[...] + dx

            @pl.when(i % tpb == 0)
            def _():
                dmod_ref[...] = jnp.zeros_like(dmod_ref)

            @pl.when(i == 0)
            def _():
                dg_ref[...] = jnp.zeros_like(dg_ref)

            dmod_ref[0:1, :] += dshift
            dmod_ref[1:2, :] += dscale
            dg_ref[0:1, :] += dg

    tok = pl.BlockSpec((tm, d), lambda i, j: (i, 0))
    return _call(
        body, name="qkv_bwd", grid=(nt, ns), args=(dqkv, dxo, x, mod, g, win),
        in_specs=[pl.BlockSpec((tm, cs), lambda i, j: (i, j)), tok, tok,
                  pl.BlockSpec((None, N_MOD, d), lambda i, j: (i // tpb, 0, 0)),
                  pl.BlockSpec((1, d), lambda i, j: (0, 0)),
                  pl.BlockSpec((SHARD_GROUP, d, cs // SHARD_GROUP), lambda i, j: (j, 0, 0))],
        out_specs=[tok,
                   pl.BlockSpec((None, 8, d), lambda i, j: (i // tpb, 0, 0)),
                   pl.BlockSpec((8, d), lambda i, j: (0, 0))],
        out_shape=[jax.ShapeDtypeStruct((t, d), F32),
                   jax.ShapeDtypeStruct((nb, 8, d), F32), jax.ShapeDtypeStruct((8, d), F32)],
        scratch_shapes=[pltpu.VMEM((tm, d), F32)],
        params=_params("arbitrary", "arbitrary"), exchange=exchange)


def _own_lanes():
    lane = lax.broadcasted_iota(jnp.int32, (1, LANES), 1)
    return [lane < HEAD_DIM, lane >= HEAD_DIM]


def _pair_tiles(a):
    return [a[:, (h // 2) * LANES:(h // 2 + 1) * LANES] for h in range(a.shape[1] // HEAD_DIM)]


def _own_tiles(a, own):
    return [jnp.where(own[h % 2], tile, jnp.zeros_like(tile)) for h, tile in enumerate(_pair_tiles(a))]


def _merge_tiles(per_head, own):
    return jnp.concatenate([jnp.where(own[0], per_head[h], per_head[h + 1])
                            for h in range(0, len(per_head), 2)], axis=1)


def _scaled(q):
    return (q.astype(F32) * (HEAD_DIM ** -0.5)).astype(BF16)


def _sb_logits(qh, kh, tri, causal):
    zs = [_dot_nt(q, k) for q, k in zip(qh, kh)]
    es = [jnp.exp(-jnp.abs(z)) for z in zs]
    log_nots = [-(jnp.maximum(z, 0.0) + jnp.log(1.0 + e)) for z, e in zip(zs, es)]
    if causal is not None:
        log_nots = [jnp.where(causal, ln, 0.0) for ln in log_nots]
    return zs, es, [_split_dot(ln, tri) for ln in log_nots]


def _sb_masks():
    rows = lax.broadcasted_iota(jnp.int32, (SB_BLOCK, SB_BLOCK), 0)
    cols = lax.broadcasted_iota(jnp.int32, (SB_BLOCK, SB_BLOCK), 1)
    return (rows >= cols).astype(BF16), (rows <= cols).astype(BF16), cols < rows


def _sb_fwd(qkv, nb, seq, exchange=None):
    t = qkv.shape[0]
    n_pairs = (qkv.shape[1] // 3) // SB_WIDTH
    tb = SB_BLOCK
    n_blk = seq // tb

    def body(q_ref, k_ref, v_ref, o_ref, c_ref):
        tri, _, causal = _sb_masks()
        own = _own_lanes()

        def key_blocks(qh, kjs, carry, mask):
            nh = SB_HEADS
            chains = range(nh * len(kjs))
            kss = [pl.multiple_of(kj * tb, tb) for kj in kjs]
            kh = [tile for ks in kss for tile in _pair_tiles(k_ref[pl.ds(ks, tb), :])]
            vh = [tile for ks in kss for tile in _pair_tiles(v_ref[pl.ds(ks, tb), :])]
            zs, _, suffixes = _sb_logits(qh * len(kjs), kh, tri, mask)
            right = []
            for c in chains:
                right.append(carry[c][1] if c < nh else right[c - nh] + suffixes[c - nh][:, 0:1])
            ws = [jnp.exp(zs[c] + suffixes[c] + right[c]) for c in chains]
            if mask is not None:
                ws = [jnp.where(mask, w, 0.0) for w in ws]
            pv = [_dot(ws[c].astype(BF16), vh[c]) for c in chains]
            last = (len(kjs) - 1) * nh
            return tuple((carry[h][0] + sum(pv[h::nh]), right[last + h] + suffixes[last + h][:, 0:1])
                         for h in range(nh))

        def query_block(qi, _):
            qs = pl.multiple_of(qi * tb, tb)
            qh = _own_tiles(_scaled(q_ref[pl.ds(qs, tb), :]), own)
            zero = (jnp.zeros((tb, LANES), F32), jnp.zeros((tb, 1), F32))
            carry = key_blocks(qh, [qi], (zero,) * SB_HEADS, causal)
            carry = lax.fori_loop(
                0, qi // 2, lambda p, cr: key_blocks(qh, [qi - 1 - 2 * p, qi - 2 - 2 * p], cr, None), carry)
            carry = lax.fori_loop(0, qi % 2, lambda _, cr: key_blocks(qh, [0], cr, None), carry)
            o_ref[pl.ds(qs, tb), :] = _merge_tiles([cr[0] for cr in carry], own)
            c_ref[pl.ds(qs, tb), :] = _merge_tiles([jnp.broadcast_to(cr[1], (tb, LANES)) for cr in carry], own)
            return 0

        lax.fori_loop(0, n_blk, query_block, 0)

    def spec(offset):
        return pl.BlockSpec((seq, SB_WIDTH), lambda b, p: (b, offset + p))

    out = jax.ShapeDtypeStruct((t, n_pairs * SB_WIDTH), F32)
    return _call(
        body, name="sb_fwd", grid=(nb, n_pairs), args=(qkv, qkv, qkv),
        in_specs=[spec(0), spec(n_pairs), spec(2 * n_pairs)],
        out_specs=[spec(0), spec(0)], out_shape=[out, out],
        params=_params("arbitrary", "arbitrary"), exchange=exchange)


def _sb_bwd(qkv, do, csum, nb, seq, exchange=None):
    t = qkv.shape[0]
    n_pairs = (qkv.shape[1] // 3) // SB_WIDTH
    tb = SB_BLOCK
    n_blk = seq // tb
    scale = HEAD_DIM ** -0.5

    def body(q_ref, k_ref, v_ref, do_ref, c_ref, dq_ref, dk_ref, dv_ref, dkt_acc, dvt_acc):
        tri, tri_prefix, causal = _sb_masks()
        own = _own_lanes()
        dkt_acc[...] = jnp.zeros_like(dkt_acc)
        dvt_acc[...] = jnp.zeros_like(dvt_acc)

        def key_blocks(qh, qth, doh, doth, ch, kjs, carry, mask):
            nh = SB_HEADS
            chains = range(nh * len(kjs))
            kss = [pl.multiple_of(kj * tb, tb) for kj in kjs]
            kh = [tile for ks in kss for tile in _pair_tiles(k_ref[pl.ds(ks, tb), :])]
            vh = [tile for ks in kss for tile in _pair_tiles(v_ref[pl.ds(ks, tb), :])]
            zs, es, suffixes = _sb_logits(qh * len(kjs), kh, tri, mask)
            dws = [_dot_nt(doh[c % nh], vh[c]) for c in chains]
            lefts = []
            for c in chains:
                before = carry[c][1] if c < nh else lefts[c - nh]
                lefts.append(before + suffixes[c][:, 0:1])
            ws = [jnp.exp(zs[c] + suffixes[c] + (ch[c % nh] - lefts[c])) for c in chains]
            if mask is not None:
                ws = [jnp.where(mask, w, 0.0) for w in ws]
            dlws = [ws[c] * dws[c] for c in chains]
            dprefixes = [_split_dot(dlw, tri_prefix) for dlw in dlws]
            dvts = [_dot(doth[c % nh], ws[c].astype(BF16)) for c in chains]
            dlefts, dzbs = [], []
            for c in chains:
                dlefts.append(carry[c][2] if c < nh else dlefts[c - nh] + dprefixes[c - nh][:, tb - 1:tb])
                sig = jnp.where(zs[c] >= 0.0, 1.0, es[c]) * pl.reciprocal(1.0 + es[c], approx=True)
                dz = dlws[c] - sig * (dlefts[c] + dprefixes[c])
                if mask is not None:
                    dz = jnp.where(mask, dz, 0.0)
                dzbs.append(dz.astype(BF16))
            dkts = [_dot(qth[c % nh], dzbs[c]) for c in chains]
            dqs = [_dot(dzbs[c], kh[c]) for c in chains]
            for b, ks in enumerate(kss):
                pairs = range(b * nh, (b + 1) * nh, 2)
                dkt_acc[:, pl.ds(ks, tb)] += jnp.concatenate([dkts[c] + dkts[c + 1] for c in pairs], axis=0)
                dvt_acc[:, pl.ds(ks, tb)] += jnp.concatenate([dvts[c] + dvts[c + 1] for c in pairs], axis=0)
            last = (len(kjs) - 1) * nh
            return tuple((carry[h][0] + sum(dqs[h::nh]), lefts[last + h],
                          dlefts[last + h] + dprefixes[last + h][:, tb - 1:tb]) for h in range(nh))

        def query_block(qi, _):
            qs = pl.multiple_of(qi * tb, tb)
            qh = _own_tiles(_scaled(q_ref[pl.ds(qs, tb), :]), own)
            doh = _own_tiles(do_ref[pl.ds(qs, tb), :], own)
            qth = [a.astype(F32).T.astype(BF16) for a in qh]
            doth = [a.T.astype(BF16) for a in doh]
            doh = [a.astype(BF16) for a in doh]
            cv = c_ref[pl.ds(qs, tb), :]
            ch = [cv[:, h * HEAD_DIM:h * HEAD_DIM + 1] for h in range(SB_HEADS)]
            zero = (jnp.zeros((tb, LANES), F32), jnp.zeros((tb, 1), F32), jnp.zeros((tb, 1), F32))

            def key_block(kjs, cr, mask):
                return key_blocks(qh, qth, doh, doth, ch, kjs, cr, mask)

            carry = lax.fori_loop(0, qi // 2, lambda p, cr: key_block([2 * p, 2 * p + 1], cr, None),
                                  (zero,) * SB_HEADS)
            carry = lax.fori_loop(0, qi % 2, lambda _, cr: key_block([qi - 1], cr, None), carry)
            carry = key_block([qi], carry, causal)
            dq = _merge_tiles([cr[0] for cr in carry], own) * scale
            dq_ref[pl.ds(qs, tb), :] = dq.astype(BF16)
            return 0

        lax.fori_loop(0, n_blk, query_block, 0)
        dk_ref[...] = dkt_acc[...].T.astype(BF16)
        dv_ref[...] = dvt_acc[...].T.astype(BF16)

    def spec(offset):
        return pl.BlockSpec((seq, SB_WIDTH), lambda b, p: (b, offset + p))

    out = jax.ShapeDtypeStruct((t, n_pairs * SB_WIDTH), BF16)
    return _call(
        body, name="sb_bwd", grid=(nb, n_pairs), args=(qkv, qkv, qkv, do, csum),
        in_specs=[spec(0), spec(n_pairs), spec(2 * n_pairs), spec(0), spec(0)],
        out_specs=[spec(0), spec(0), spec(0)],
        out_shape=[out, out, out],
        scratch_shapes=[pltpu.VMEM((SB_WIDTH, seq), F32), pltpu.VMEM((SB_WIDTH, seq), F32)],
        params=_params("arbitrary", "arbitrary"), exchange=exchange)


def _dil_block_scores(qh, kph, kch, bias_ref, has_prev, band_prev, band_cur):
    scale = HEAD_DIM ** -0.5
    heads = range(len(qh))
    no_prev = jnp.where(has_prev, 0.0, NEG_INF)
    zps = [_dot_nt(qh[h], kph[h]) for h in heads]
    zcs = [_dot_nt(qh[h], kch[h]) for h in heads]
    zps = [jnp.where(band_prev, zps[h] * scale + bias_ref[h, :, 0:DIL_BLOCK], NEG_INF) + no_prev for h in heads]
    zcs = [jnp.where(band_cur, zcs[h] * scale + bias_ref[h, :, DIL_BLOCK:2 * DIL_BLOCK], NEG_INF) for h in heads]
    return zps, zcs


def _dil_bands():
    rows = lax.broadcasted_iota(jnp.int32, (DIL_BLOCK, DIL_BLOCK), 0)
    cols = lax.broadcasted_iota(jnp.int32, (DIL_BLOCK, DIL_BLOCK), 1)
    return cols >= rows, cols <= rows


def _dil_fwd(qkv, bias, nb, seq, dil, exchange=None):
    t, width = qkv.shape
    n_pairs = (width // 3) // DIL_WIDTH
    bq = DIL_BLOCK
    n_blk = seq // bq
    per_seq = n_blk // dil
    heads = range(DIL_HEADS)

    def body(q_ref, k_ref, v_ref, bias_ref, o_ref, lse_ref):
        band_prev, band_cur = _dil_bands()
        own = _own_lanes()

        def block(n, _):
            has_prev = (n & (per_seq - 1)) != 0
            qs = pl.multiple_of(n * bq, bq)
            ps = pl.multiple_of(jnp.maximum(n - 1, 0) * bq, bq)
            qh = _own_tiles(q_ref[pl.ds(qs, bq), :], own)
            kp, kc = _pair_tiles(k_ref[pl.ds(ps, bq), :]), _pair_tiles(k_ref[pl.ds(qs, bq), :])
            vp, vc = _pair_tiles(v_ref[pl.ds(ps, bq), :]), _pair_tiles(v_ref[pl.ds(qs, bq), :])
            zps, zcs = _dil_block_scores(qh, kp, kc, bias_ref, has_prev, band_prev, band_cur)
            ms = [jnp.maximum(jnp.max(zps[h], axis=1, keepdims=True), jnp.max(zcs[h], axis=1, keepdims=True))
                  for h in heads]
            eps = [jnp.exp(zps[h] - ms[h]) for h in heads]
            ecs = [jnp.exp(zcs[h] - ms[h]) for h in heads]
            pvs = [_dot(eps[h].astype(BF16), vp[h]) + _dot(ecs[h].astype(BF16), vc[h]) for h in heads]
            dens = [jnp.sum(eps[h], axis=1, keepdims=True) + jnp.sum(ecs[h], axis=1, keepdims=True) for h in heads]
            o_ref[pl.ds(qs, bq), :] = _merge_tiles([pvs[h] / dens[h] for h in heads], own)
            lse_ref[pl.ds(qs, bq), :] = _merge_tiles(
                [jnp.broadcast_to(ms[h] + jnp.log(dens[h]), (bq, LANES)) for h in heads], own)
            return 0

        lax.fori_loop(0, n_blk, block, 0, unroll=4)

    def spec(offset):
        return pl.BlockSpec((seq, DIL_WIDTH), lambda b, p: (b, offset + p))

    out = jax.ShapeDtypeStruct((t, n_pairs * DIL_WIDTH), F32)
    return _call(
        body, name=f"dil_fwd{dil}", grid=(nb, n_pairs), args=(qkv, qkv, qkv, bias),
        in_specs=[spec(0), spec(n_pairs), spec(2 * n_pairs),
                  pl.BlockSpec((DIL_HEADS, bq, 2 * bq), lambda b, p: (p, 0, 0))],
        out_specs=[spec(0), spec(0)], out_shape=[out, out],
        params=_params("arbitrary", "arbitrary"), exchange=exchange)


def _dil_bwd(qkv, bias, do, lse, delta, nb, seq, dil):
    t, width = qkv.shape
    n_pairs = (width // 3) // DIL_WIDTH
    bq = DIL_BLOCK
    n_blk = seq // bq
    per_seq = n_blk // dil
    scale = HEAD_DIM ** -0.5
    heads = range(DIL_HEADS)

    def body(q_ref, k_ref, v_ref, bias_ref, do_ref, lse_ref, dl_ref, dq_ref, dk_ref, dv_ref, db_ref,
             dk_acc, dv_acc):
        band_prev, band_cur = _dil_bands()
        own = _own_lanes()
        dk_acc[...] = jnp.zeros_like(dk_acc)
        dv_acc[...] = jnp.zeros_like(dv_acc)

        @pl.when(pl.program_id(1) == 0)
        def _():
            db_ref[...] = jnp.zeros_like(db_ref)

        def block(n, _):
            has_prev = (n & (per_seq - 1)) != 0
            qs = pl.multiple_of(n * bq, bq)
            ps = pl.multiple_of(jnp.maximum(n - 1, 0) * bq, bq)
            qh = _own_tiles(q_ref[pl.ds(qs, bq), :], own)
            kp, kc = _pair_tiles(k_ref[pl.ds(ps, bq), :]), _pair_tiles(k_ref[pl.ds(qs, bq), :])
            vp, vc = _pair_tiles(v_ref[pl.ds(ps, bq), :]), _pair_tiles(v_ref[pl.ds(qs, bq), :])
            doh = _own_tiles(do_ref[pl.ds(qs, bq), :].astype(BF16), own)
            lse_v, dl_v = lse_ref[pl.ds(qs, bq), :], dl_ref[pl.ds(qs, bq), :]
            zps, zcs = _dil_block_scores(qh, kp, kc, bias_ref, has_prev, band_prev, band_cur)
            dpp = [_dot_nt(doh[h], vp[h]) for h in heads]
            dpc = [_dot_nt(doh[h], vc[h]) for h in heads]
            lse_h = [lse_v[:, h * HEAD_DIM:h * HEAD_DIM + 1] for h in heads]
            dl_h = [dl_v[:, h * HEAD_DIM:h * HEAD_DIM + 1] for h in heads]
            pps = [jnp.exp(zps[h] - lse_h[h]) for h in heads]
            pcs = [jnp.exp(zcs[h] - lse_h[h]) for h in heads]
            dvp = [_dot_tn(pps[h].astype(BF16), doh[h]) for h in heads]
            dvc = [_dot_tn(pcs[h].astype(BF16), doh[h]) for h in heads]
            dzps = [pps[h] * (dpp[h] - dl_h[h]) for h in heads]
            dzcs = [pcs[h] * (dpc[h] - dl_h[h]) for h in heads]
            dzp_b = [(dzps[h] * scale).astype(BF16) for h in heads]
            dzc_b = [(dzcs[h] * scale).astype(BF16) for h in heads]
            dqs = [_dot(dzp_b[h], kp[h]) + _dot(dzc_b[h], kc[h]) for h in heads]
            dkp = [_dot_tn(dzp_b[h], qh[h]) for h in heads]
            dkc = [_dot_tn(dzc_b[h], qh[h]) for h in heads]
            for h in heads:
                db_ref[h, :, 0:bq] += dzps[h]
                db_ref[h, :, bq:2 * bq] += dzcs[h]
            def pair_sums(per_head):
                return jnp.concatenate([per_head[h] + per_head[h + 1] for h in heads[::2]], axis=1)

            dq_ref[pl.ds(qs, bq), :] = _merge_tiles(dqs, own).astype(BF16)
            dk_acc[pl.ds(ps, bq), :] += pair_sums(dkp)
            dk_acc[pl.ds(qs, bq), :] += pair_sums(dkc)
            dv_acc[pl.ds(ps, bq), :] += pair_sums(dvp)
            dv_acc[pl.ds(qs, bq), :] += pair_sums(dvc)
            return 0

        lax.fori_loop(0, n_blk, block, 0, unroll=4)
        dk_ref[...] = dk_acc[...].astype(BF16)
        dv_ref[...] = dv_acc[...].astype(BF16)

    def spec(offset):
        return pl.BlockSpec((seq, DIL_WIDTH), lambda p, b: (b, offset + p))

    bias_spec = pl.BlockSpec((DIL_HEADS, bq, 2 * bq), lambda p, b: (p, 0, 0))
    out = jax.ShapeDtypeStruct((t, n_pairs * DIL_WIDTH), BF16)
    return pl.pallas_call(
        body, name=f"dil_bwd{dil}", grid=(n_pairs, nb),
        in_specs=[spec(0), spec(n_pairs), spec(2 * n_pairs), bias_spec, spec(0), spec(0), spec(0)],
        out_specs=[spec(0), spec(0), spec(0), bias_spec],
        out_shape=[out, out, out, jax.ShapeDtypeStruct(bias.shape, F32)],
        scratch_shapes=[pltpu.VMEM((seq, DIL_WIDTH), F32), pltpu.VMEM((seq, DIL_WIDTH), F32)],
        compiler_params=_params("arbitrary", "arbitrary"),
    )(qkv, qkv, qkv, bias, do, lse, delta)


def _head_blocks(width):
    rows = lax.broadcasted_iota(jnp.int32, (width, width), 0) // HEAD_DIM
    cols = lax.broadcasted_iota(jnp.int32, (width, width), 1) // HEAD_DIM
    return (rows == cols).astype(BF16)


def _head_mean(v, gmat):
    return _split_dot(v, gmat) * (1.0 / HEAD_DIM)


def _residue_views(arrays, nb, seq):
    return [a if dil == 1 else a.reshape(nb, dil, seq // dil, a.shape[1]) for a, dil in zip(arrays, DILATIONS)]


def _mix_out_fwd(osb, ocs, lses, gsb, gdil, wout, x, mod, tm):
    t, d = x.shape
    ds = osb.shape[1]
    nt = t // tm
    nb = mod.shape[0]
    tpb = nt // nb
    seq = t // nb
    n_cfg = len(DILATIONS)

    def body(osb_ref, *refs):
        oc_refs, lse_refs = refs[:n_cfg], refs[n_cfg:2 * n_cfg]
        gsb_ref, gdil_ref, w_ref, x_ref, mod_ref = refs[2 * n_cfg:2 * n_cfg + 5]
        xo_ref, on_ref, m_ref, odil_ref = refs[2 * n_cfg + 5:2 * n_cfg + 9]
        ld_refs = refs[2 * n_cfg + 9:3 * n_cfg + 9]
        stages, sc = refs[3 * n_cfg + 9:]
        ocv, lsev = [oc_refs[0][...]], [lse_refs[0][...]]
        for i, dil in enumerate(DILATIONS[1:]):
            ocv.append(_from_residue_rows(oc_refs[i + 1], stages.at[2 * i], dil))
            lsev.append(_from_residue_rows(lse_refs[i + 1], stages.at[2 * i + 1], dil))
        top = functools.reduce(jnp.maximum, lsev)
        total = top + jnp.log(sum(jnp.exp(l - top) for l in lsev))
        odil = sum(jnp.exp(l - total) * o for o, l in zip(ocv, lsev))
        odil_ref[...] = odil
        ld_refs[0][...] = total
        _stage(total, sc)
        for ref, dil in zip(ld_refs[1:], DILATIONS[1:]):
            _to_residue_rows(sc, ref, dil)
        gm = _head_blocks(ds)
        parts = []
        for o, g_ref in ((osb_ref[...], gsb_ref), (odil, gdil_ref)):
            parts.append(o * lax.rsqrt(_head_mean(o * o, gm) + EPS) * g_ref[...])
        on = jnp.concatenate(parts, axis=1).astype(BF16)
        on_ref[...] = on
        m = _dot(on, w_ref[...])
        m_ref[...] = m
        xo_ref[...] = x_ref[...] + mod_ref[5:6, :] * m

    tok = pl.BlockSpec((tm, d), lambda i: (i, 0))
    hd = pl.BlockSpec((tm, ds), lambda i: (i, 0))
    res = [hd] + [_residue_spec(tm, tpb, ds, dil, lambda i: 0) for dil in DILATIONS[1:]]
    res_shape = [jax.ShapeDtypeStruct((t, ds), F32)] + [_residue_shape(nb, seq, ds, dil, F32) for dil in DILATIONS[1:]]
    gain = pl.BlockSpec((1, ds), lambda i: (0, 0))
    outs = pl.pallas_call(
        body, name="mix_out_fwd", grid=(nt,),
        in_specs=[hd] + res + res + [gain, gain,
                  pl.BlockSpec(wout.shape, lambda i: (0, 0)),
                  tok, pl.BlockSpec((None, N_MOD, d), lambda i: (i // tpb, 0, 0))],
        out_specs=[tok, pl.BlockSpec((tm, 2 * ds), lambda i: (i, 0)), tok, hd] + res,
        out_shape=[jax.ShapeDtypeStruct((t, d), F32), jax.ShapeDtypeStruct((t, 2 * ds), BF16),
                   jax.ShapeDtypeStruct((t, d), F32), jax.ShapeDtypeStruct((t, ds), F32)] + res_shape,
        scratch_shapes=[pltpu.VMEM((2 * (n_cfg - 1), ds // LANES, tm, LANES), F32), _stage_shape(tm, ds)],
        compiler_params=_params("arbitrary"),
    )(osb, *_residue_views(ocs, nb, seq), *_residue_views(lses, nb, seq), gsb, gdil, wout, x, mod)
    return outs[0], outs[1], outs[2], outs[3], [a.reshape(t, ds) for a in outs[4:]]


def _mix_out_bwd(dxo, m, mod, wout, osb, odil, gsb, gdil, tm):
    t, d = dxo.shape
    ds = osb.shape[1]
    nt = t // tm
    nb = mod.shape[0]
    tpb = nt // nb
    seq = t // nb
    n_cfg = len(DILATIONS)

    def body(dxo_ref, m_ref, mod_ref, w_ref, osb_ref, odil_ref, gsb_ref, gdil_ref,
             dm_ref, dosb_ref, *rest):
        do_refs, dl_refs = rest[:n_cfg], rest[n_cfg:2 * n_cfg]
        dmod_ref, dg_ref, sc = rest[2 * n_cfg:]
        dodil_ref, dldil_ref = do_refs[0], dl_refs[0]
        i = pl.program_id(0)
        dxo_v = dxo_ref[...]
        dm = (mod_ref[5:6, :] * dxo_v).astype(BF16)
        dm_ref[...] = dm
        dgt = jnp.sum(m_ref[...] * dxo_v, axis=0, keepdims=True)
        don = _dot_nt(dm, w_ref[...])
        gm = _head_blocks(ds)

        @pl.when(i % tpb == 0)
        def _():
            dmod_ref[...] = jnp.zeros_like(dmod_ref)

        @pl.when(i == 0)
        def _():
            dg_ref[...] = jnp.zeros_like(dg_ref)

        dmod_ref[2:3, :] += dgt
        groups = ((osb_ref, gsb_ref, dosb_ref), (odil_ref, gdil_ref, dodil_ref))
        for k, (o_ref, g_ref, do_ref) in enumerate(groups):
            o = o_ref[...]
            dn_out = don[:, k * ds:(k + 1) * ds]
            r = lax.rsqrt(_head_mean(o * o, gm) + EPS)
            n = o * r
            dg_ref[0:1, k * ds:(k + 1) * ds] += jnp.sum(dn_out * n, axis=0, keepdims=True)
            dn = dn_out * g_ref[...]
            do = r * (dn - n * _head_mean(dn * n, gm))
            do_ref[...] = do
            if k == 1:
                delta = _head_mean(do * o, gm) * float(HEAD_DIM)
                dldil_ref[...] = delta
                for value, refs in ((do, do_refs), (delta, dl_refs)):
                    _stage(value, sc)
                    for ref, dil in zip(refs[1:], DILATIONS[1:]):
                        _to_residue_rows(sc, ref, dil)

    tok = pl.BlockSpec((tm, d), lambda i: (i, 0))
    hd = pl.BlockSpec((tm, ds), lambda i: (i, 0))
    res = [hd] + [_residue_spec(tm, tpb, ds, dil, lambda i: 0) for dil in DILATIONS[1:]]
    res_shape = [jax.ShapeDtypeStruct((t, ds), F32)] + [_residue_shape(nb, seq, ds, dil, F32) for dil in DILATIONS[1:]]
    gain = pl.BlockSpec((1, ds), lambda i: (0, 0))
    outs = pl.pallas_call(
        body, name="mix_out_bwd", grid=(nt,),
        in_specs=[tok, tok, pl.BlockSpec((None, N_MOD, d), lambda i: (i // tpb, 0, 0)),
                  pl.BlockSpec(wout.shape, lambda i: (0, 0)), hd, hd, gain, gain],
        out_specs=[tok, hd] + res + res
        + [pl.BlockSpec((None, 8, d), lambda i: (i // tpb, 0, 0)), pl.BlockSpec((8, 2 * ds), lambda i: (0, 0))],
        out_shape=[jax.ShapeDtypeStruct((t, d), BF16), jax.ShapeDtypeStruct((t, ds), F32)] + res_shape + res_shape
        + [jax.ShapeDtypeStruct((nb, 8, d), F32), jax.ShapeDtypeStruct((8, 2 * ds), F32)],
        scratch_shapes=[_stage_shape(tm, ds)],
        compiler_params=_params("arbitrary"),
    )(dxo, m, mod, wout, osb, odil, gsb, gdil)
    flat = [a.reshape(t, ds) for a in outs[2:2 + 2 * n_cfg]]
    return outs[0], outs[1], flat[:n_cfg], flat[n_cfg:], outs[-2], outs[-1]


def _merge_dqkv(sb_parts, dil_parts, nb, tm):
    t, ds = sb_parts[0].shape
    nt = t // tm
    tpb = nt // nb
    seq = t // nb
    n_cfg = len(DILATIONS)

    def body(*refs):
        sb_refs, dil_refs = refs[:3], refs[3:3 + 3 * n_cfg]
        o_ref, sc = refs[3 + 3 * n_cfg:]
        for k in range(3):
            o_ref[:, k * ds:(k + 1) * ds] = sb_refs[k][...]
            total = dil_refs[k * n_cfg][...].astype(F32)
            for i, dil in enumerate(DILATIONS[1:]):
                total = total + _from_residue_rows(dil_refs[k * n_cfg + i + 1], sc, dil)
            o_ref[:, (3 + k) * ds:(4 + k) * ds] = total.astype(BF16)

    hd = pl.BlockSpec((tm, ds), lambda i: (i, 0))
    res = [hd] + [_residue_spec(tm, tpb, ds, dil, lambda i: 0) for dil in DILATIONS[1:]]
    views = [v for parts in dil_parts for v in _residue_views(parts, nb, seq)]
    return pl.pallas_call(
        body, name="merge_dqkv", grid=(nt,),
        in_specs=[hd] * 3 + res * 3,
        out_specs=pl.BlockSpec((tm, 6 * ds), lambda i: (i, 0)),
        out_shape=jax.ShapeDtypeStruct((t, 6 * ds), BF16),
        scratch_shapes=[_stage_shape(tm, ds)],
        compiler_params=_params("arbitrary"),
    )(*sb_parts, *views)


def _row_tile(rows):
    if rows <= 256:
        return rows
    for cand in range(256, 15, -16):
        if rows % cand == 0:
            return cand
    return rows


def _adamw(w, parts, m, v, name, transposed=False):
    rows, cols = w.shape
    n_parts = parts.shape[0]
    tr = _row_tile(rows)
    c1 = 1.0 / (1.0 - ADAM_B1 ** ADAM_STEP)
    c2 = 1.0 / (1.0 - ADAM_B2 ** ADAM_STEP)

    def body(w_ref, p_ref, m_ref, v_ref, g_ref, d_ref, nm_ref, nv_ref):
        g = p_ref[0].astype(F32)
        for i in range(1, n_parts):
            g = g + p_ref[i].astype(F32)
        wv, mv, vv = w_ref[...], m_ref[...], v_ref[...]
        if transposed:
            wv, mv, vv = wv.T, mv.T, vv.T
        nm = ADAM_B1 * mv + (1.0 - ADAM_B1) * g
        nv = ADAM_B2 * vv + (1.0 - ADAM_B2) * (g * g)
        g_ref[...] = g
        nm_ref[...] = nm
        nv_ref[...] = nv
        d_ref[...] = -ADAM_LR * ((nm * c1) / (jnp.sqrt(nv * c2) + ADAM_EPS) + ADAM_WD * wv)

    blk = pl.BlockSpec((tr, cols), lambda i: (i, 0))
    if transposed:
        oblk = pl.BlockSpec((cols, tr), lambda i: (0, i))
        pblk = pl.BlockSpec((n_parts, cols, tr), lambda i: (0, 0, i))
        out = jax.ShapeDtypeStruct((cols, rows), F32)
    else:
        oblk, pblk = blk, pl.BlockSpec((n_parts, tr, cols), lambda i: (0, i, 0))
        out = jax.ShapeDtypeStruct((rows, cols), F32)
    return pl.pallas_call(
        body, name=name, grid=(rows // tr,),
        in_specs=[blk, pblk, blk, blk],
        out_specs=[oblk, oblk, oblk, oblk], out_shape=[out, out, out, out],
        compiler_params=_params("arbitrary"),
    )(w, parts, m, v)


def _t5_bucket(n):
    max_exact = N_BUCKETS // 2
    nf = np.maximum(n, 1).astype(np.float32)
    large = max_exact + (np.log(nf / max_exact) / math.log(MAX_DISTANCE / max_exact)
                         * (N_BUCKETS - max_exact)).astype(np.int32)
    large = np.minimum(large, N_BUCKETS - 1)
    return np.where(n < max_exact, n, large).astype(np.int32)


def _bucket_onehot():
    table = np.zeros((len(DILATIONS), 2 * DIL_BLOCK + 1, N_BUCKETS), np.float32)
    for i, dil in enumerate(DILATIONS):
        buckets = _t5_bucket(np.arange(DIL_BLOCK + 1) * dil)
        for m in range(DIL_BLOCK + 1):
            table[i, m, buckets[DIL_BLOCK - m]] = 1.0
    return table


def _bias_blocks(rel_bias):
    row = jnp.einsum("cmn,nh->chm", _bucket_onehot(), rel_bias, precision=lax.Precision.HIGHEST)
    n_cfg, n_heads, width = row.shape
    tiled = jnp.tile(row, (1, 1, DIL_BLOCK))[..., :DIL_BLOCK * (width - 1)]
    return tiled.reshape(n_cfg, n_heads, DIL_BLOCK, width - 1)


def _bias_blocks_bwd(dblocks):
    n_cfg, n_heads = dblocks.shape[:2]
    width = 2 * DIL_BLOCK + 1
    flat = dblocks.reshape(n_cfg, n_heads, DIL_BLOCK * (width - 1))
    flat = jnp.pad(flat, ((0, 0), (0, 0), (0, DIL_BLOCK)))
    drow = jnp.sum(flat.reshape(n_cfg, n_heads, DIL_BLOCK, width), axis=2)
    return jnp.einsum("chm,cmn->nh", drow, _bucket_onehot(), precision=lax.Precision.HIGHEST)


def _pad_to(a, axis, size):
    pad = [(0, 0)] * a.ndim
    pad[axis] = (0, size - a.shape[axis])
    return jnp.pad(a, pad)


def _lane_pad(n):
    return -(-n // LANES) * LANES


def _local_step(x, target, mod, gains, weights, rel_bias, tm, distributed):
    nb, seq, d = x.shape
    t = nb * seq
    g_ffn1, g_mix, g_sb, g_dil, g_ffn2, g_final = gains
    wg1, wu1, wd1 = weights[:3]
    x0 = x.reshape(t, d)
    ds = g_sb.shape[1]
    bias = _bias_blocks(rel_bias)

    def beside(arrays, scatter):
        return _Exchange(arrays, scatter) if distributed else None

    tp, tg = min(PROJ_TILE, seq), min(GRAD_TILE, t)

    (x1, f1, gate1, up1), got = _ffn_fwd(x0, mod, g_ffn1, wg1, wu1, wd1, 0, tp, beside(weights[3:4], False))
    win = got[0] if distributed else weights[3]
    (qkv, qkvd, h2), got = _qkv_fwd(x1, mod, g_mix, win, tp, beside(weights[4:5], False))
    wout = got[0] if distributed else weights[4]
    wout2 = wout.reshape(-1, d)
    (osb, csb), got = _sb_fwd(qkv, nb, seq, beside(weights[5:7], False))
    wg2, wu2 = got if distributed else weights[5:7]
    n_cfg = len(DILATIONS)
    piece = -(-weights[7].shape[-2] // n_cfg // 16) * 16
    ocs, lses, wd2_pieces = [], [], []
    for i, dil in enumerate(DILATIONS):
        rows = weights[7][..., i * piece:(i + 1) * piece, :]
        (oc, lse), got = _dil_fwd(qkvd[i], bias[i], nb, seq, dil, beside([rows], False))
        wd2_pieces.append(got[0] if distributed else rows)
        ocs.append(oc)
        lses.append(lse)
    wd2 = jnp.concatenate(wd2_pieces, axis=-2)
    x2, on, mix, odil, ldil = _mix_out_fwd(osb, ocs, lses, g_sb, g_dil, wout2, x1, mod, tm)
    (dx3, f3, gate3, up3, head), _ = _ffn_fwd(x2, mod, g_ffn2, wg2, wu2, wd2, 2, tp,
                                              head=(target.reshape(t, d), g_final))
    loss_sum = 0.5 * jnp.sum(head[0]) / d
    dg_final = head[1:2]

    (dx2, dgate3, dup3, act3, h3, df3, dmod3, dg_ffn2), _ = _ffn_bwd(
        dx3, x2, f3, mod, g_ffn2, gate3, up3, wg2, wu2, wd2, 2, tm)
    gwg2, gwu2, gwd2 = _ffn_weight_grads(h3, dgate3, dup3, act3, df3, tg, 2)

    dm, dosb, dodil, dldil, dmod2b, dg_heads = _mix_out_bwd(
        dx2, mix, mod, wout2, osb, odil, g_sb, g_dil, tm)
    gwout = _mm_tn(on, dm,
                   pl.BlockSpec((tg, wout.shape[1]), lambda i, j: (i, j)),
                   pl.BlockSpec((tg, d), lambda i, j: (i, 0)),
                   wout.shape, t // tg, "grad_wout")

    (dq_sb, dk_sb, dv_sb), parts_late = _sb_bwd(qkv, dosb, csb, nb, seq,
                                                beside([gwout, gwg2, gwu2, gwd2], True))
    dil_grads = [_dil_bwd(qkvd[i], bias[i], dodil[i], ldil[i], dldil[i], nb, seq, dil)
                 for i, dil in enumerate(DILATIONS)]
    dqkv = _merge_dqkv([dq_sb, dk_sb, dv_sb], [[g[k] for g in dil_grads] for k in range(3)], nb, tm)
    drel = _bias_blocks_bwd(jnp.stack([g[3] for g in dil_grads]))

    cs = win.shape[2]
    gwin = _mm_tn(h2, dqkv,
                  pl.BlockSpec((tg, d), lambda i, j: (i, 0)),
                  pl.BlockSpec((tg, cs), lambda i, j: (i, j)),
                  win.shape, t // tg, "grad_win", pair_reduce=distributed)
    (dx1, dmod2a, dg_mix), parts_mid = _qkv_bwd(
        dqkv, dx2, x1, mod, g_mix, win, tp, _Exchange([gwin], True, chips=[True]) if distributed else None)

    (dx0, dgate1, dup1, act1, h1, df1, dmod1, dg_ffn1), _ = _ffn_bwd(
        dx1, x0, f1, mod, g_ffn1, gate1, up1, wg1, wu1, wd1, 0, tm)
    dmod = jnp.concatenate([dmod1[:, 0:3], dmod2a[:, 0:2], dmod2b[:, 2:3], dmod3[:, 0:3]], axis=1)
    ggrads = (dg_ffn1[0:1], dg_mix[0:1], dg_heads[0:1], drel, dg_ffn2[0:1], dg_final)
    if not distributed:
        gw1 = _ffn_weight_grads(h1, dgate1, dup1, act1, df1, tg, 0)
        return loss_sum, dx0.reshape(nb, seq, d), tuple(gw1) + (gwin, gwout, gwg2, gwu2, gwd2), dmod, ggrads

    dg_heads_row, drel_flat = dg_heads[0:1], drel.reshape(1, -1)
    width = max(d, dg_heads_row.shape[1], drel_flat.shape[1])
    small = jnp.concatenate(
        [_pad_to(a.reshape(1, -1), 1, width)
         for a in (dg_ffn1[0:1], dg_mix[0:1], dg_ffn2[0:1], dg_final, dg_heads_row, drel_flat, loss_sum)]
        + [jnp.zeros((1, width), F32)], axis=0)
    dmod_pad = _pad_to(dmod.reshape(nb, N_MOD * d), 0, 8)
    everyone = _Exchange([jnp.broadcast_to(dmod_pad, (N_DEV,) + dmod_pad.shape),
                          jnp.broadcast_to(small, (N_DEV,) + small.shape)], True)
    sent_g, sent_u, gwd1, (dmod_all, small_all) = _ffn_weight_grads(
        h1, dgate1, dup1, act1, df1, tg, 0, stream=True, first=everyone)
    wgrads = (sent_g, sent_u, gwd1) + tuple(parts_mid + parts_late)
    return dx0.reshape(nb, seq, d), wgrads, dmod_all, small_all


def kernel(x, c, w_ada, b_ada, g_ffn1, w1_gate, w1_up, w1_down, g_mix, w_in, g_sb_out, g_dil_out, w_out, rel_bias, g_ffn2, w2_gate, w2_up, w2_down, g_final, loss_target, m_w_ada, m_b_ada, m_g_ffn1, m_w1_gate, m_w1_up, m_w1_down, m_g_mix, m_w_in, m_g_sb_out, m_g_dil_out, m_w_out, m_rel_bias, m_g_ffn2, m_w2_gate, m_w2_up, m_w2_down, m_g_final, v_w_ada, v_b_ada, v_g_ffn1, v_w1_gate, v_w1_up, v_w1_down, v_g_mix, v_w_in, v_g_sb_out, v_g_dil_out, v_w_out, v_rel_bias, v_g_ffn2, v_w2_gate, v_w2_up, v_w2_down, v_g_final):
    nb, seq, d = x.shape
    me = 4 * lax.axis_index("x") + 2 * lax.axis_index("y") + lax.axis_index("c")
    tm = min(TOKEN_TILE, seq)
    fs = w1_gate.shape[2]
    fs_pad = _lane_pad(fs)
    ada_cols = w_ada.shape[2]

    def col_shard(w):
        return _pad_to(w[0].astype(BF16), 1, fs_pad)

    def row_shard(w):
        return _pad_to(w[0].astype(BF16), 0, fs_pad)

    shards = [col_shard(w1_gate), col_shard(w1_up), row_shard(w1_down), w_in[0].astype(BF16),
              w_out[0].astype(BF16), col_shard(w2_gate), col_shard(w2_up), row_shard(w2_down)]
    b_cols = lax.dynamic_slice(b_ada, (0, me * ada_cols), (1, ada_cols))
    c_every, mod_all, first = _first_exchange(_pad_to(c, 0, 8), shards[:3], w_ada[0], b_cols)
    c_all = c_every[:, :nb].reshape(N_DEV * nb, d)
    weights = first + shards[3:]
    mod = lax.dynamic_slice(mod_all, (0, me * 8, 0), (N_DEV, nb, ada_cols))
    mod = mod.transpose(1, 0, 2).reshape(nb, N_MOD, d)

    n_sb = g_sb_out.shape[1] * g_sb_out.shape[2]
    gains = (g_ffn1, g_mix, g_sb_out.reshape(1, n_sb), g_dil_out.reshape(1, -1), g_ffn2,
             g_final.reshape(1, d))
    grad_x, parts, dmod_all, small_all = _local_step(x, loss_target, mod, gains, weights, rel_bias, tm, True)

    last_part = _exchange([parts[2]], True, "scatter_last", chips=[True])[0]
    parts = parts[:2] + (last_part,) + parts[3:]
    dmod_all = dmod_all[:, :nb].reshape(N_DEV * nb, N_MOD * d)
    dmod_cols = lax.dynamic_slice(dmod_all, (0, me * ada_cols), (N_DEV * nb, ada_cols))
    gw_ada, gb_ada = _ada_bwd(c_all, dmod_cols, dmod_all)

    def small_part(row, size, shape):
        return small_all[:, row, :size].reshape((N_DEV,) + shape)

    loss = jnp.sum(small_all[:, 6, 0])

    n_rel = rel_bias.shape
    updates = {
        "w_ada": (w_ada[0], gw_ada[None], m_w_ada[0], v_w_ada[0]),
        "b_ada": (b_ada, gb_ada[None], m_b_ada, v_b_ada),
        "g_ffn1": (g_ffn1, small_part(0, d, (1, d)), m_g_ffn1, v_g_ffn1),
        "w1_gate": (w1_gate[0], parts[0], m_w1_gate[0], v_w1_gate[0]),
        "w1_up": (w1_up[0], parts[1], m_w1_up[0], v_w1_up[0]),
        "w1_down": (w1_down[0], parts[2], m_w1_down[0], v_w1_down[0]),
        "g_mix": (g_mix, small_part(1, d, (1, d)), m_g_mix, v_g_mix),
        "w_in": (w_in[0], parts[3], m_w_in[0], v_w_in[0]),
        "g_sb_out": (g_sb_out[0], small_all[:, 4, :n_sb].reshape((N_DEV,) + g_sb_out.shape[1:]),
                     m_g_sb_out[0], v_g_sb_out[0]),
        "g_dil_out": (g_dil_out[0], small_all[:, 4, n_sb:n_sb + g_dil_out[0].size].reshape((N_DEV,) + g_dil_out.shape[1:]),
                      m_g_dil_out[0], v_g_dil_out[0]),
        "w_out": (w_out[0], parts[4], m_w_out[0], v_w_out[0]),
        "rel_bias": (rel_bias, small_part(5, rel_bias.size, n_rel), m_rel_bias, v_rel_bias),
        "g_ffn2": (g_ffn2, small_part(2, d, (1, d)), m_g_ffn2, v_g_ffn2),
        "w2_gate": (w2_gate[0], parts[5], m_w2_gate[0], v_w2_gate[0]),
        "w2_up": (w2_up[0], parts[6], m_w2_up[0], v_w2_up[0]),
        "w2_down": (w2_down[0], parts[7], m_w2_down[0], v_w2_down[0]),
        "g_final": (g_final.reshape(1, d), small_part(3, d, (1, d)), m_g_final.reshape(1, d), v_g_final.reshape(1, d)),
    }
    shapes = {"w_ada": w_ada.shape, "b_ada": b_ada.shape, "g_ffn1": g_ffn1.shape, "w1_gate": w1_gate.shape,
              "w1_up": w1_up.shape, "w1_down": w1_down.shape, "g_mix": g_mix.shape, "w_in": w_in.shape,
              "g_sb_out": g_sb_out.shape, "g_dil_out": g_dil_out.shape, "w_out": w_out.shape,
              "rel_bias": rel_bias.shape, "g_ffn2": g_ffn2.shape, "w2_gate": w2_gate.shape,
              "w2_up": w2_up.shape, "w2_down": w2_down.shape, "g_final": g_final.shape}
    grads, deltas, new_m, new_v = [], [], [], []
    for name, (w, p, m, v) in updates.items():
        transposed = name in ("w1_gate", "w1_up", "w2_gate", "w2_up")
        outs = _adamw(w, p, m, v, f"adamw_{name}", transposed)
        for dst, a in zip((grads, deltas, new_m, new_v), outs):
            dst.append((a.T if transposed else a).reshape(shapes[name]))
    return (loss, grad_x, *grads, *deltas, *new_m, *new_v)
```

```python
import functools
import math

import numpy as np
import jax
import jax.numpy as jnp
from jax import lax
from jax.experimental import pallas as pl
from jax.experimental.pallas import tpu as pltpu

F32 = jnp.float32
BF16 = jnp.bfloat16

EPS = 1e-6
NEG_INF = -1e30
HEAD_DIM = 64
LANES = 128
DIL_BLOCK = 128
DILATIONS = (1, 4, 16)
N_BUCKETS = 32
MAX_DISTANCE = 2048
N_MOD = 9
N_DEV = 8
SB_BLOCK = 256
SB_HEADS = 4
SB_WIDTH = SB_HEADS * HEAD_DIM
DIL_HEADS = 4
DIL_WIDTH = DIL_HEADS * HEAD_DIM
TOKEN_TILE = 512
PROJ_TILE = 1024
GRAD_TILE = 1024
SHARD_GROUP = 2
FFN_CHUNKS = 2
VMEM_LIMIT_BYTES = 56 * 1024 * 1024

ADAM_LR = 0.001
ADAM_B1 = 0.9
ADAM_B2 = 0.999
ADAM_EPS = 1e-08
ADAM_WD = 0.01
ADAM_STEP = 10

NT_DIMS = (((1,), (1,)), ((), ()))
TN_DIMS = (((0,), (0,)), ((), ()))


def _params(*sem):
    return pltpu.CompilerParams(dimension_semantics=sem, vmem_limit_bytes=VMEM_LIMIT_BYTES)


def _once(spec):
    return pl.BlockSpec(spec.block_shape, spec.index_map, pipeline_mode=pl.Buffered(1))


def _dot(a, b):
    return jnp.dot(a, b, preferred_element_type=F32)


def _dot_nt(a, b):
    return lax.dot_general(a, b, NT_DIMS, preferred_element_type=F32)


def _dot_tn(a, b):
    return lax.dot_general(a, b, TN_DIMS, preferred_element_type=F32)


def _split_dot(a, b):
    hi = a.astype(BF16)
    lo = (a - hi.astype(F32)).astype(BF16)
    return _dot(hi, b) + _dot(lo, b)


def _sigmoid(z):
    return 1.0 / (1.0 + jnp.exp(-z))


def _norm(x):
    r = lax.rsqrt(jnp.mean(x * x, axis=-1, keepdims=True) + EPS)
    return x * r, r


def _modulate(x, g, mod_ref, k):
    n, _ = _norm(x)
    shift = mod_ref[3 * k:3 * k + 1, :]
    scale = mod_ref[3 * k + 1:3 * k + 2, :]
    return n * g * (1.0 + scale) + shift


def _modulate_bwd(dh, x, g, mod_ref, k):
    n, r = _norm(x)
    scale = mod_ref[3 * k + 1:3 * k + 2, :]
    dshift = jnp.sum(dh, axis=0, keepdims=True)
    dscale = jnp.sum(dh * n * g, axis=0, keepdims=True)
    dg = jnp.sum(dh * n * (1.0 + scale), axis=0, keepdims=True)
    dn = dh * g * (1.0 + scale)
    dx = r * (dn - n * jnp.mean(dn * n, axis=-1, keepdims=True))
    return dx, dshift, dscale, dg


class _Exchange:
    def __init__(self, arrays, scatter, relay=False, chips=None):
        assert not (scatter and relay)
        self.arrays = list(arrays)
        self.scatter = scatter
        self.relay = relay
        self.n = len(self.arrays)
        self.chips = list(chips) if chips is not None else [False] * self.n
        assert scatter or not any(self.chips)
        self.out_shape = [
            jax.ShapeDtypeStruct((N_DEV // 2 if ch else N_DEV,) + tuple(a.shape[1:] if scatter else a.shape), a.dtype)
            for a, ch in zip(self.arrays, self.chips)]
        n_remote = self.n * (N_DEV - 1)
        self.scratch_shapes = [pltpu.SemaphoreType.DMA((n_remote,)), pltpu.SemaphoreType.DMA((n_remote,)),
                               pltpu.SemaphoreType.DMA((self.n,))]

    def _copies(self, in_refs, out_refs, sems):
        send_sems, recv_sems, local_sems = sems
        x, y, c = lax.axis_index("x"), lax.axis_index("y"), lax.axis_index("c")
        me = 4 * x + 2 * y + c
        local, remote, relayed = [], {}, {}
        for a in range(self.n):
            if self.chips[a]:
                mine = 2 * x + y
                local.append(pltpu.make_async_copy(in_refs[a].at[mine], out_refs[a].at[mine], local_sems.at[a]))
                for k in (2, 4, 6):
                    px = 1 - x if k & 4 else x
                    py = 1 - y if k & 2 else y
                    sem = a * (N_DEV - 1) + k - 1
                    remote[a, k] = pltpu.make_async_remote_copy(
                        src_ref=in_refs[a].at[2 * px + py], dst_ref=out_refs[a].at[mine],
                        send_sem=send_sems.at[sem], recv_sem=recv_sems.at[sem],
                        device_id=(px, py, c), device_id_type=pl.DeviceIdType.MESH)
                continue
            src = in_refs[a].at[me] if self.scatter else in_refs[a]
            local.append(pltpu.make_async_copy(src, out_refs[a].at[me], local_sems.at[a]))
            for k in range(1, N_DEV):
                px = 1 - x if k & 4 else x
                py = 1 - y if k & 2 else y
                pc = 1 - c if k & 1 else c
                sem = a * (N_DEV - 1) + k - 1
                if self.relay and k & 1 and k > 1:
                    slot = 4 * px + 2 * py + c
                    relayed[a, k] = pltpu.make_async_remote_copy(
                        src_ref=out_refs[a].at[slot], dst_ref=out_refs[a].at[slot],
                        send_sem=send_sems.at[sem], recv_sem=recv_sems.at[sem],
                        device_id=(x, y, 1 - c), device_id_type=pl.DeviceIdType.MESH)
                    continue
                src = in_refs[a].at[4 * px + 2 * py + pc] if self.scatter else in_refs[a]
                remote[a, k] = pltpu.make_async_remote_copy(
                    src_ref=src, dst_ref=out_refs[a].at[me],
                    send_sem=send_sems.at[sem], recv_sem=recv_sems.at[sem],
                    device_id=(px, py, pc), device_id_type=pl.DeviceIdType.MESH)
        return local, remote, relayed

    def start(self, in_refs, out_refs, sems):
        local, remote, _ = self._copies(in_refs, out_refs, sems)
        for cp in local + list(remote.values()):
            cp.start()

    def wait(self, in_refs, out_refs, sems):
        local, remote, relayed = self._copies(in_refs, out_refs, sems)
        for (a, k), cp in relayed.items():
            remote[a, k - 1].wait_recv()
            cp.start()
        for (a, k), cp in remote.items():
            if (a, k + 1) not in relayed:
                cp.wait_recv()
        for cp in relayed.values():
            cp.wait_recv()
        for cp in list(remote.values()) + list(relayed.values()):
            cp.wait_send()
        for cp in local:
            cp.wait()


def _call(body, *, name, args, in_specs, out_specs, out_shape, scratch_shapes=(), grid=(),
          params=None, exchange=None):
    n_in, n_out = len(args), len(out_shape)
    if exchange is None:
        outs = pl.pallas_call(
            body, name=name, grid=grid, in_specs=list(in_specs), out_specs=list(out_specs),
            out_shape=list(out_shape), scratch_shapes=list(scratch_shapes), compiler_params=params,
        )(*args)
        return list(outs), []
    n_ex = exchange.n

    def wrapped(*refs):
        ins, refs = refs[:n_in], refs[n_in:]
        ex_in, refs = refs[:n_ex], refs[n_ex:]
        outs, refs = refs[:n_out], refs[n_out:]
        ex_out, refs = refs[:n_ex], refs[n_ex:]
        scratch, sems = refs[:len(refs) - 3], refs[len(refs) - 3:]
        if not grid:
            exchange.start(ex_in, ex_out, sems)
            body(*ins, *outs, *scratch)
            exchange.wait(ex_in, ex_out, sems)
            return
        first = functools.reduce(jnp.logical_and, [pl.program_id(a) == 0 for a in range(len(grid))])
        last = functools.reduce(jnp.logical_and, [pl.program_id(a) == grid[a] - 1 for a in range(len(grid))])

        @pl.when(first)
        def _():
            exchange.start(ex_in, ex_out, sems)

        body(*ins, *outs, *scratch)

        @pl.when(last)
        def _():
            exchange.wait(ex_in, ex_out, sems)

    any_spec = pl.BlockSpec(memory_space=pl.ANY)
    outs = pl.pallas_call(
        wrapped, name=name, grid=grid,
        in_specs=list(in_specs) + [any_spec] * n_ex, out_specs=list(out_specs) + [any_spec] * n_ex,
        out_shape=list(out_shape) + exchange.out_shape,
        scratch_shapes=list(scratch_shapes) + exchange.scratch_shapes, compiler_params=params,
    )(*args, *exchange.arrays)
    return list(outs[:n_out]), list(outs[n_out:])


def _exchange(arrays, scatter, name, relay=False, chips=None):
    return _call(lambda: None, name=name, args=(), in_specs=(), out_specs=(), out_shape=(),
                 exchange=_Exchange(arrays, scatter, relay, chips))[1]


def _first_exchange(c_pad, shards, w, b):
    rows, d = c_pad.shape
    cols = w.shape[1]
    ex_c = _Exchange([c_pad], False)
    ex_w = _Exchange(shards, False, relay=True)
    ex_m = _Exchange([jax.ShapeDtypeStruct((N_DEV * rows, cols), F32)], False)
    n_w = ex_w.n

    def body(*refs):
        c_ref, w_refs, wa_ref, b_ref = refs[0], refs[1:1 + n_w], refs[1 + n_w], refs[2 + n_w]
        outs = refs[3 + n_w:]
        cg_ref, wg_refs, mg_ref = outs[0], outs[1:1 + n_w], outs[1 + n_w]
        scratch = outs[2 + n_w:]
        sems_c, sems_w, sems_m, c_vm, m_vm = scratch[0:3], scratch[3:6], scratch[6:9], scratch[9], scratch[10]
        ex_c.start([c_ref], [cg_ref], sems_c)
        ex_c.wait([c_ref], [cg_ref], sems_c)
        pltpu.sync_copy(cg_ref, c_vm)
        cv = c_vm[...].reshape(N_DEV * rows, d)
        s = (cv * _sigmoid(cv)).astype(BF16)
        m_vm[...] = _dot(s, wa_ref[...].astype(BF16)) + b_ref[...]
        ex_m.start([m_vm], [mg_ref], sems_m)
        ex_w.start(w_refs, wg_refs, sems_w)
        ex_m.wait([m_vm], [mg_ref], sems_m)
        ex_w.wait(w_refs, wg_refs, sems_w)

    any_spec = pl.BlockSpec(memory_space=pl.ANY)
    vmem_spec = pl.BlockSpec(memory_space=pltpu.VMEM)
    outs = pl.pallas_call(
        body, name="first_exchange",
        in_specs=[any_spec] * (1 + n_w) + [vmem_spec, vmem_spec],
        out_specs=[any_spec] * (2 + n_w),
        out_shape=ex_c.out_shape + ex_w.out_shape + ex_m.out_shape,
        scratch_shapes=ex_c.scratch_shapes + ex_w.scratch_shapes + ex_m.scratch_shapes
        + [pltpu.VMEM((N_DEV, rows, d), F32), pltpu.VMEM((N_DEV * rows, cols), F32)],
        compiler_params=pltpu.CompilerParams(vmem_limit_bytes=VMEM_LIMIT_BYTES),
    )(c_pad, *shards, w, b)
    return outs[0], outs[1 + n_w], list(outs[1:1 + n_w])


def _ada_bwd(c_all, dmod_cols, dmod_all):
    def body(c_ref, dc_ref, da_ref, gw_ref, gb_ref):
        cv = c_ref[...]
        s = cv * _sigmoid(cv)
        gw_ref[...] = lax.dot_general(s, dc_ref[...], TN_DIMS, preferred_element_type=F32,
                                      precision=lax.Precision.HIGHEST)
        gb_ref[...] = jnp.sum(da_ref[...], axis=0, keepdims=True)

    return pl.pallas_call(
        body, name="ada_bwd",
        out_shape=(jax.ShapeDtypeStruct((c_all.shape[1], dmod_cols.shape[1]), F32),
                   jax.ShapeDtypeStruct((1, dmod_all.shape[1]), F32)),
        compiler_params=pltpu.CompilerParams(vmem_limit_bytes=VMEM_LIMIT_BYTES),
    )(c_all, dmod_cols, dmod_all)


def _side_by_side(w_ref):
    return jnp.concatenate([w_ref[s] for s in range(w_ref.shape[0])], axis=1)


def _stacked(w_ref):
    return jnp.concatenate([w_ref[s] for s in range(w_ref.shape[0])], axis=0)


def _loss_tile(x, target, g, acc_ref):
    d = x.shape[1]
    n, r = _norm(x)
    err = n * g - target
    dy = err * (1.0 / d)
    acc_ref[0:1, :] += jnp.sum(err * err, axis=0, keepdims=True)
    acc_ref[1:2, :] += jnp.sum(dy * n, axis=0, keepdims=True)
    dn = dy * g
    return r * (dn - n * jnp.mean(dn * n, axis=-1, keepdims=True))


def _ffn_fwd(x, mod, g, wg, wu, wd, k, tm, exchange=None, head=None):
    t, d = x.shape
    ns, _, fs = wg.shape
    nt = t // tm
    tpb = nt // mod.shape[0]
    rows = tm // FFN_CHUNKS
    extra = list(head) if head is not None else []

    def body(x_ref, mod_ref, g_ref, wg_ref, wu_ref, wd_ref, *rest):
        if head is not None:
            t_ref, gf_ref, xo_ref, f_ref, gg_ref, uu_ref, head_ref, h_sc, acc = rest
        else:
            xo_ref, f_ref, gg_ref, uu_ref, h_sc, acc = rest
        i, j = pl.program_id(0), pl.program_id(1)

        @pl.when(j == 0)
        def _():
            h_sc[...] = _modulate(x_ref[...], g_ref[...], mod_ref, k).astype(BF16)
            acc[...] = jnp.zeros_like(acc)

        chunks = [pl.ds(c * rows, rows) for c in range(FFN_CHUNKS)]
        wg, wu, wd = _side_by_side(wg_ref), _side_by_side(wu_ref), _stacked(wd_ref)
        gates, ups = [], []
        for rs in chunks:
            h = h_sc[rs, :]
            gates.append(_dot(h, wg))
            ups.append(_dot(h, wu))
        acts = [(g * _sigmoid(g) * u).astype(BF16) for g, u in zip(gates, ups)]
        for rs, g, u in zip(chunks, gates, ups):
            for s in range(SHARD_GROUP):
                gg_ref[s, rs, :] = g[:, s * fs:(s + 1) * fs].astype(BF16)
                uu_ref[s, rs, :] = u[:, s * fs:(s + 1) * fs].astype(BF16)
        downs = [_dot(a, wd) for a in acts]
        for rs, dn in zip(chunks, downs):
            acc[rs, :] += dn

        @pl.when(j == ns // SHARD_GROUP - 1)
        def _():
            f = acc[...]
            f_ref[...] = f.astype(BF16)
            xo = x_ref[...] + 0.5 * mod_ref[3 * k + 2:3 * k + 3, :] * f
            if head is None:
                xo_ref[...] = xo
            else:
                @pl.when(i == 0)
                def _():
                    head_ref[...] = jnp.zeros_like(head_ref)

                xo_ref[...] = _loss_tile(xo, t_ref[...], gf_ref[...], head_ref)

    tok = pl.BlockSpec((tm, d), lambda i, j: (i, 0))
    row = pl.BlockSpec((1, d), lambda i, j: (0, 0))
    hid = pl.BlockSpec((SHARD_GROUP, tm, fs), lambda i, j: (j, i, 0))
    head_specs = [_once(tok), row] if head is not None else []
    head_out = [pl.BlockSpec((8, d), lambda i, j: (0, 0))] if head is not None else []
    head_shape = [jax.ShapeDtypeStruct((8, d), F32)] if head is not None else []
    return _call(
        body, name=f"ffn_fwd{k}", grid=(nt, ns // SHARD_GROUP), args=(x, mod, g, wg, wu, wd, *extra),
        in_specs=[tok,
                  pl.BlockSpec((None, N_MOD, d), lambda i, j: (i // tpb, 0, 0)),
                  row,
                  pl.BlockSpec((SHARD_GROUP, d, fs), lambda i, j: (j, 0, 0)),
                  pl.BlockSpec((SHARD_GROUP, d, fs), lambda i, j: (j, 0, 0)),
                  pl.BlockSpec((SHARD_GROUP, fs, d), lambda i, j: (j, 0, 0))] + head_specs,
        out_specs=[tok, tok, hid, hid] + head_out,
        out_shape=[jax.ShapeDtypeStruct((t, d), F32), jax.ShapeDtypeStruct((t, d), BF16),
                   jax.ShapeDtypeStruct((ns, t, fs), BF16), jax.ShapeDtypeStruct((ns, t, fs), BF16)]
        + head_shape,
        scratch_shapes=[pltpu.VMEM((tm, d), BF16), pltpu.VMEM((tm, d), F32)],
        params=_params("arbitrary", "arbitrary"), exchange=exchange)


def _ffn_bwd(dxo, x, f, mod, g, gate, up, wg, wu, wd, k, tm, exchange=None):
    t, d = x.shape
    ns, _, fs = wg.shape
    nt = t // tm
    nb = mod.shape[0]
    tpb = nt // nb
    rows = tm // FFN_CHUNKS

    def body(dxo_ref, x_ref, f_ref, mod_ref, g_ref, gg_ref, uu_ref, wg_ref, wu_ref, wd_ref,
             dx_ref, dgg_ref, duu_ref, act_ref, h_ref, df_ref, dmod_ref, dg_ref, acc):
        i, j = pl.program_id(0), pl.program_id(1)

        @pl.when(j == 0)
        def _():
            df = 0.5 * mod_ref[3 * k + 2:3 * k + 3, :] * dxo_ref[...]
            df_ref[...] = df.astype(BF16)
            h_ref[...] = _modulate(x_ref[...], g_ref[...], mod_ref, k).astype(BF16)
            acc[...] = jnp.zeros_like(acc)

        chunks = [pl.ds(c * rows, rows) for c in range(FFN_CHUNKS)]
        group = range(SHARD_GROUP)
        wg, wu, wd = _side_by_side(wg_ref), _side_by_side(wu_ref), _stacked(wd_ref)
        dacts = [_dot_nt(df_ref[rs, :], wd) for rs in chunks]
        dgates, dups = [], []
        for rs, dact in zip(chunks, dacts):
            gv = jnp.concatenate([gg_ref[s, rs, :] for s in group], axis=1).astype(F32)
            uv = jnp.concatenate([uu_ref[s, rs, :] for s in group], axis=1).astype(F32)
            sig = _sigmoid(gv)
            s_act = gv * sig
            act = (s_act * uv).astype(BF16)
            for s in group:
                act_ref[s, rs, :] = act[:, s * fs:(s + 1) * fs]
            dups.append((dact * s_act).astype(BF16))
            dgates.append((dact * uv * (sig * (1.0 + gv * (1.0 - sig)))).astype(BF16))
        dhs = [_dot_nt(dg, wg) + _dot_nt(du, wu) for dg, du in zip(dgates, dups)]
        for rs, dg, du, dh in zip(chunks, dgates, dups, dhs):
            for s in group:
                dgg_ref[s, rs, :] = dg[:, s * fs:(s + 1) * fs]
                duu_ref[s, rs, :] = du[:, s * fs:(s + 1) * fs]
            acc[rs, :] += dh

        @pl.when(j == ns // SHARD_GROUP - 1)
        def _():
            dx, dshift, dscale, dg = _modulate_bwd(acc[...], x_ref[...], g_ref[...], mod_ref, k)
            dxo_v = dxo_ref[...]
            dx_ref[...] = dxo_v + dx
            dgt = jnp.sum(0.5 * f_ref[...].astype(F32) * dxo_v, axis=0, keepdims=True)

            @pl.when(i % tpb == 0)
            def _():
                dmod_ref[...] = jnp.zeros_like(dmod_ref)

            @pl.when(i == 0)
            def _():
                dg_ref[...] = jnp.zeros_like(dg_ref)

            dmod_ref[0:1, :] += dshift
            dmod_ref[1:2, :] += dscale
            dmod_ref[2:3, :] += dgt
            dg_ref[0:1, :] += dg

    tok = pl.BlockSpec((tm, d), lambda i, j: (i, 0))
    hid = pl.BlockSpec((SHARD_GROUP, tm, fs), lambda i, j: (j, i, 0))
    return _call(
        body, name=f"ffn_bwd{k}", grid=(nt, ns // SHARD_GROUP), args=(dxo, x, f, mod, g, gate, up, wg, wu, wd),
        in_specs=[tok, tok, tok,
                  pl.BlockSpec((None, N_MOD, d), lambda i, j: (i // tpb, 0, 0)),
                  pl.BlockSpec((1, d), lambda i, j: (0, 0)),
                  hid, hid,
                  pl.BlockSpec((SHARD_GROUP, d, fs), lambda i, j: (j, 0, 0)),
                  pl.BlockSpec((SHARD_GROUP, d, fs), lambda i, j: (j, 0, 0)),
                  pl.BlockSpec((SHARD_GROUP, fs, d), lambda i, j: (j, 0, 0))],
        out_specs=[tok, hid, hid, hid, tok, tok,
                   pl.BlockSpec((None, 8, d), lambda i, j: (i // tpb, 0, 0)),
                   pl.BlockSpec((8, d), lambda i, j: (0, 0))],
        out_shape=[jax.ShapeDtypeStruct((t, d), F32),
                   jax.ShapeDtypeStruct((ns, t, fs), BF16), jax.ShapeDtypeStruct((ns, t, fs), BF16),
                   jax.ShapeDtypeStruct((ns, t, fs), BF16),
                   jax.ShapeDtypeStruct((t, d), BF16), jax.ShapeDtypeStruct((t, d), BF16),
                   jax.ShapeDtypeStruct((nb, 8, d), F32), jax.ShapeDtypeStruct((8, d), F32)],
        scratch_shapes=[pltpu.VMEM((tm, d), F32)],
        params=_params("arbitrary", "arbitrary"), exchange=exchange)


def _mm_tn(a, b, a_spec, b_spec, out_shape, n_tiles, name, exchange=None, keep_transposed=False,
           pair_reduce=False):
    n_out = out_shape[0]
    block = tuple(out_shape[1:])
    last = n_tiles - 1
    flip = block[0] > block[1]
    if flip:
        block = block[::-1]
    if flip and keep_transposed:
        flip_back, out_shape = False, (n_out,) + block
    else:
        flip_back = flip
    full_shape = tuple(out_shape)
    n_pairs = n_out // 2
    if pair_reduce:
        out_shape = (n_pairs,) + full_shape[1:]

    def body(a_ref, b_ref, o_ref, acc, *pair):
        i, j = pl.program_id(0), pl.program_id(1)
        prod = _dot_tn(b_ref[...], a_ref[...]) if flip else _dot_tn(a_ref[...], b_ref[...])
        full_ref = pair[0] if pair_reduce else o_ref

        @pl.when(i == 0)
        def _():
            acc[j] = prod

        @pl.when(i > 0)
        def _():
            acc[j] += prod

        @pl.when(i == last)
        def _():
            total = acc[j]
            full_ref[j] = (total.T if flip_back else total).astype(BF16)

        if pair_reduce:
            _, landed, send_sems, recv_sems = pair

            @pl.when(jnp.logical_and(i == last, j == n_out - 1))
            def _():
                x, y, c = lax.axis_index("x"), lax.axis_index("y"), lax.axis_index("c")
                copies = [pltpu.make_async_remote_copy(
                    src_ref=full_ref.at[2 * q + 1 - c], dst_ref=landed.at[q],
                    send_sem=send_sems.at[q], recv_sem=recv_sems.at[q],
                    device_id=(x, y, 1 - c), device_id_type=pl.DeviceIdType.MESH) for q in range(n_pairs)]
                for cp in copies:
                    cp.start()
                for q, cp in enumerate(copies):
                    cp.wait_recv()
                    o_ref[q] = (full_ref[2 * q + c].astype(F32) + landed[q].astype(F32)).astype(BF16)
                for cp in copies:
                    cp.wait_send()

    scratch = [pltpu.VMEM((n_out,) + block, F32)]
    if pair_reduce:
        scratch += [pltpu.VMEM(full_shape, BF16), pltpu.VMEM(out_shape, BF16),
                    pltpu.SemaphoreType.DMA((n_pairs,)), pltpu.SemaphoreType.DMA((n_pairs,))]
    outs, sent = _call(
        body, name=name, grid=(n_tiles, n_out), args=(a, b), in_specs=[a_spec, b_spec],
        out_specs=[pl.BlockSpec(out_shape, lambda i, j: (0,) * len(out_shape))],
        out_shape=[jax.ShapeDtypeStruct(out_shape, BF16)],
        scratch_shapes=scratch,
        params=_params("arbitrary", "arbitrary"), exchange=exchange)
    return (outs[0], sent) if exchange is not None else outs[0]


def _ffn_weight_grads(h, dgate, dup, act, df, tm, tag, stream=False, first=None):
    t, d = h.shape
    ns, _, fs = dgate.shape
    nt = t // tm
    tok = pl.BlockSpec((tm, d), lambda i, j: (i, 0))
    hid = pl.BlockSpec((None, tm, fs), lambda i, j: (j, i, 0))
    if not stream:
        gwg = _mm_tn(h, dgate, tok, hid, (ns, d, fs), nt, f"grad_wg{tag}", keep_transposed=True)
        gwu = _mm_tn(h, dup, tok, hid, (ns, d, fs), nt, f"grad_wu{tag}", keep_transposed=True)
        gwd = _mm_tn(act, df, hid, tok, (ns, fs, d), nt, f"grad_wd{tag}")
        return gwg, gwu, gwd
    gwg, brought = _mm_tn(h, dgate, tok, hid, (ns, d, fs), nt, f"grad_wg{tag}", first,
                          keep_transposed=True, pair_reduce=True)
    gwu, sent_g = _mm_tn(h, dup, tok, hid, (ns, d, fs), nt, f"grad_wu{tag}",
                         _Exchange([gwg], True, chips=[True]), keep_transposed=True, pair_reduce=True)
    gwd, sent_u = _mm_tn(act, df, hid, tok, (ns, fs, d), nt, f"grad_wd{tag}",
                         _Exchange([gwu], True, chips=[True]), pair_reduce=True)
    return sent_g[0], sent_u[0], gwd, brought


def _stage_shape(rows, cols):
    return pltpu.VMEM((cols // LANES, rows, LANES), F32)


def _stage(value, stage_ref):
    for k in range(stage_ref.shape[0]):
        stage_ref[k] = value[:, k * LANES:(k + 1) * LANES]


def _to_residue_rows(stage_ref, dst_ref, dil):
    rows = stage_ref.shape[1] // dil
    for r in range(dil):
        for k in range(stage_ref.shape[0]):
            dst_ref[r, :, k * LANES:(k + 1) * LANES] = (
                stage_ref.at[k][pl.ds(r, rows, stride=dil), :].astype(dst_ref.dtype))


def _from_residue_rows(src_ref, stage_ref, dil):
    rows = stage_ref.shape[1] // dil
    chunks = range(stage_ref.shape[0])
    for r in range(dil):
        for k in chunks:
            stage_ref.at[k][pl.ds(r, rows, stride=dil), :] = src_ref[r, :, k * LANES:(k + 1) * LANES].astype(F32)
    return jnp.concatenate([stage_ref[k] for k in chunks], axis=1)


def _residue_shape(nb, seq, width, dil, dtype):
    return jax.ShapeDtypeStruct((nb, dil, seq // dil, width), dtype)


def _residue_spec(tm, tpb, cols, dil, col_block):
    return pl.BlockSpec((None, dil, tm // dil, cols),
                        lambda i, *rest: (i // tpb, 0, i % tpb, col_block(i, *rest)))


def _qkv_fwd(x, mod, g, win, tm, exchange=None):
    t, d = x.shape
    ns, _, cs = win.shape
    nt = t // tm
    nb = mod.shape[0]
    tpb = nt // nb
    seq = t // nb
    width = ns * cs // 2
    cs, ns = cs * SHARD_GROUP, ns // SHARD_GROUP
    half = ns // 2
    n_res = len(DILATIONS) - 1

    def body(x_ref, mod_ref, g_ref, w_ref, sb_ref, dil_ref, *rest):
        res_refs, h_ref, sc = rest[:n_res], rest[n_res], rest[n_res + 1]
        j = pl.program_id(1)

        @pl.when(j == 0)
        def _():
            h_ref[...] = _modulate(x_ref[...], g_ref[...], mod_ref, 1).astype(BF16)

        res = _dot(h_ref[...], _side_by_side(w_ref))

        @pl.when(j < half)
        def _():
            sb_ref[...] = res.astype(BF16)

        @pl.when(j >= half)
        def _():
            dil_ref[...] = res.astype(BF16)
            _stage(res, sc)
            for ref, dil in zip(res_refs, DILATIONS[1:]):
                _to_residue_rows(sc, ref, dil)

    def dil_col(i, j):
        return jnp.maximum(j - half, 0)

    tok = pl.BlockSpec((tm, d), lambda i, j: (i, 0))
    wide = jax.ShapeDtypeStruct((t, width), BF16)
    outs, got = _call(
        body, name="qkv_fwd", grid=(nt, ns), args=(x, mod, g, win),
        in_specs=[tok,
                  pl.BlockSpec((None, N_MOD, d), lambda i, j: (i // tpb, 0, 0)),
                  pl.BlockSpec((1, d), lambda i, j: (0, 0)),
                  pl.BlockSpec((SHARD_GROUP, d, cs // SHARD_GROUP), lambda i, j: (j, 0, 0))],
        out_specs=[pl.BlockSpec((tm, cs), lambda i, j: (i, jnp.minimum(j, half - 1))),
                   pl.BlockSpec((tm, cs), lambda i, j: (i, dil_col(i, j)))]
        + [_residue_spec(tm, tpb, cs, dil, dil_col) for dil in DILATIONS[1:]] + [tok],
        out_shape=[wide, wide] + [_residue_shape(nb, seq, width, dil, BF16) for dil in DILATIONS[1:]]
        + [jax.ShapeDtypeStruct((t, d), BF16)],
        scratch_shapes=[_stage_shape(tm, cs)],
        params=_params("arbitrary", "arbitrary"), exchange=exchange)
    qkv_dil = [outs[1]] + [a.reshape(t, width) for a in outs[2:2 + n_res]]
    return (outs[0], qkv_dil, outs[-1]), got


def _qkv_bwd(dqkv, dxo, x, mod, g, win, tm, exchange=None):
    t, d = x.shape
    ns, _, cs = win.shape
    nt = t // tm
    nb = mod.shape[0]
    tpb = nt // nb
    cs, ns = cs * SHARD_GROUP, ns // SHARD_GROUP

    def body(dq_ref, dxo_ref, x_ref, mod_ref, g_ref, w_ref, dx_ref, dmod_ref, dg_ref, acc):
        i, j = pl.program_id(0), pl.program_id(1)

        @pl.when(j == 0)
        def _():
            acc[...] = jnp.zeros_like(acc)

        acc[...] += _dot_nt(dq_ref[...], _side_by_side(w_ref))

        @pl.when(j == ns - 1)
        def _():
            dx, dshift, dscale, dg = _modulate_bwd(acc[...], x_ref[...], g_ref[...], mod_ref, 1)
            dx_ref[...] = dxo_ref[...] + dx

            @pl.when(i % tpb == 0)
            def _():
                dmod_ref[...] = jnp.zeros_like(dmod_ref)

            @pl.when(i == 0)
            def _():
                dg_ref[...] = jnp.zeros_like(dg_ref)

            dmod_ref[0:1, :] += dshift
            dmod_ref[1:2, :] += dscale
            dg_ref[0:1, :] += dg

    tok = pl.BlockSpec((tm, d), lambda i, j: (i, 0))
    return _call(
        body, name="qkv_bwd", grid=(nt, ns), args=(dqkv, dxo, x, mod, g, win),
        in_specs=[pl.BlockSpec((tm, cs), lambda i, j: (i, j)), tok, tok,
                  pl.BlockSpec((None, N_MOD, d), lambda i, j: (i // tpb, 0, 0)),
                  pl.BlockSpec((1, d), lambda i, j: (0, 0)),
                  pl.BlockSpec((SHARD_GROUP, d, cs // SHARD_GROUP), lambda i, j: (j, 0, 0))],
        out_specs=[tok,
                   pl.BlockSpec((None, 8, d), lambda i, j: (i // tpb, 0, 0)),
                   pl.BlockSpec((8, d), lambda i, j: (0, 0))],
        out_shape=[jax.ShapeDtypeStruct((t, d), F32),
                   jax.ShapeDtypeStruct((nb, 8, d), F32), jax.ShapeDtypeStruct((8, d), F32)],
        scratch_shapes=[pltpu.VMEM((tm, d), F32)],
        params=_params("arbitrary", "arbitrary"), exchange=exchange)


def _own_lanes():
    lane = lax.broadcasted_iota(jnp.int32, (1, LANES), 1)
    return [lane < HEAD_DIM, lane >= HEAD_DIM]


def _pair_tiles(a):
    return [a[:, (h // 2) * LANES:(h // 2 + 1) * LANES] for h in range(a.shape[1] // HEAD_DIM)]


def _own_tiles(a, own):
    return [jnp.where(own[h % 2], tile, jnp.zeros_like(tile)) for h, tile in enumerate(_pair_tiles(a))]


def _merge_tiles(per_head, own):
    return jnp.concatenate([jnp.where(own[0], per_head[h], per_head[h + 1])
                            for h in range(0, len(per_head), 2)], axis=1)


def _scaled(q):
    return (q.astype(F32) * (HEAD_DIM ** -0.5)).astype(BF16)


def _sb_logits(qh, kh, tri, causal):
    zs = [_dot_nt(q, k) for q, k in zip(qh, kh)]
    es = [jnp.exp(-jnp.abs(z)) for z in zs]
    log_nots = [-(jnp.maximum(z, 0.0) + jnp.log(1.0 + e)) for z, e in zip(zs, es)]
    if causal is not None:
        log_nots = [jnp.where(causal, ln, 0.0) for ln in log_nots]
    return zs, es, [_split_dot(ln, tri) for ln in log_nots]


def _sb_masks():
    rows = lax.broadcasted_iota(jnp.int32, (SB_BLOCK, SB_BLOCK), 0)
    cols = lax.broadcasted_iota(jnp.int32, (SB_BLOCK, SB_BLOCK), 1)
    return (rows >= cols).astype(BF16), (rows <= cols).astype(BF16), cols < rows


def _sb_fwd(qkv, nb, seq, exchange=None):
    t = qkv.shape[0]
    n_pairs = (qkv.shape[1] // 3) // SB_WIDTH
    tb = SB_BLOCK
    n_blk = seq // tb

    def body(q_ref, k_ref, v_ref, o_ref, c_ref):
        tri, _, causal = _sb_masks()
        own = _own_lanes()

        def key_blocks(qh, kjs, carry, mask):
            nh = SB_HEADS
            chains = range(nh * len(kjs))
            kss = [pl.multiple_of(kj * tb, tb) for kj in kjs]
            kh = [tile for ks in kss for tile in _pair_tiles(k_ref[pl.ds(ks, tb), :])]
            vh = [tile for ks in kss for tile in _pair_tiles(v_ref[pl.ds(ks, tb), :])]
            zs, _, suffixes = _sb_logits(qh * len(kjs), kh, tri, mask)
            right = []
            for c in chains:
                right.append(carry[c][1] if c < nh else right[c - nh] + suffixes[c - nh][:, 0:1])
            ws = [jnp.exp(zs[c] + suffixes[c] + right[c]) for c in chains]
            if mask is not None:
                ws = [jnp.where(mask, w, 0.0) for w in ws]
            pv = [_dot(ws[c].astype(BF16), vh[c]) for c in chains]
            last = (len(kjs) - 1) * nh
            return tuple((carry[h][0] + sum(pv[h::nh]), right[last + h] + suffixes[last + h][:, 0:1])
                         for h in range(nh))

        def query_block(qi, _):
            qs = pl.multiple_of(qi * tb, tb)
            qh = _own_tiles(_scaled(q_ref[pl.ds(qs, tb), :]), own)
            zero = (jnp.zeros((tb, LANES), F32), jnp.zeros((tb, 1), F32))
            carry = key_blocks(qh, [qi], (zero,) * SB_HEADS, causal)
            carry = lax.fori_loop(
                0, qi // 2, lambda p, cr: key_blocks(qh, [qi - 1 - 2 * p, qi - 2 - 2 * p], cr, None), carry)
            carry = lax.fori_loop(0, qi % 2, lambda _, cr: key_blocks(qh, [0], cr, None), carry)
            o_ref[pl.ds(qs, tb), :] = _merge_tiles([cr[0] for cr in carry], own)
            c_ref[pl.ds(qs, tb), :] = _merge_tiles([jnp.broadcast_to(cr[1], (tb, LANES)) for cr in carry], own)
            return 0

        lax.fori_loop(0, n_blk, query_block, 0)

    def spec(offset):
        return pl.BlockSpec((seq, SB_WIDTH), lambda b, p: (b, offset + p))

    out = jax.ShapeDtypeStruct((t, n_pairs * SB_WIDTH), F32)
    return _call(
        body, name="sb_fwd", grid=(nb, n_pairs), args=(qkv, qkv, qkv),
        in_specs=[spec(0), spec(n_pairs), spec(2 * n_pairs)],
        out_specs=[spec(0), spec(0)], out_shape=[out, out],
        params=_params("arbitrary", "arbitrary"), exchange=exchange)


def _sb_bwd(qkv, do, csum, nb, seq, exchange=None):
    t = qkv.shape[0]
    n_pairs = (qkv.shape[1] // 3) // SB_WIDTH
    tb = SB_BLOCK
    n_blk = seq // tb
    scale = HEAD_DIM ** -0.5

    def body(q_ref, k_ref, v_ref, do_ref, c_ref, dq_ref, dk_ref, dv_ref, dkt_acc, dvt_acc):
        tri, tri_prefix, causal = _sb_masks()
        own = _own_lanes()
        dkt_acc[...] = jnp.zeros_like(dkt_acc)
        dvt_acc[...] = jnp.zeros_like(dvt_acc)

        def key_blocks(qh, qth, doh, doth, ch, kjs, carry, mask):
            nh = SB_HEADS
            chains = range(nh * len(kjs))
            kss = [pl.multiple_of(kj * tb, tb) for kj in kjs]
            kh = [tile for ks in kss for tile in _pair_tiles(k_ref[pl.ds(ks, tb), :])]
            vh = [tile for ks in kss for tile in _pair_tiles(v_ref[pl.ds(ks, tb), :])]
            zs, es, suffixes = _sb_logits(qh * len(kjs), kh, tri, mask)
            dws = [_dot_nt(doh[c % nh], vh[c]) for c in chains]
            lefts = []
            for c in chains:
                before = carry[c][1] if c < nh else lefts[c - nh]
                lefts.append(before + suffixes[c][:, 0:1])
            ws = [jnp.exp(zs[c] + suffixes[c] + (ch[c % nh] - lefts[c])) for c in chains]
            if mask is not None:
                ws = [jnp.where(mask, w, 0.0) for w in ws]
            dlws = [ws[c] * dws[c] for c in chains]
            dprefixes = [_split_dot(dlw, tri_prefix) for dlw in dlws]
            dvts = [_dot(doth[c % nh], ws[c].astype(BF16)) for c in chains]
            dlefts, dzbs = [], []
            for c in chains:
                dlefts.append(carry[c][2] if c < nh else dlefts[c - nh] + dprefixes[c - nh][:, tb - 1:tb])
                sig = jnp.where(zs[c] >= 0.0, 1.0, es[c]) * pl.reciprocal(1.0 + es[c], approx=True)
                dz = dlws[c] - sig * (dlefts[c] + dprefixes[c])
                if mask is not None:
                    dz = jnp.where(mask, dz, 0.0)
                dzbs.append(dz.astype(BF16))
            dkts = [_dot(qth[c % nh], dzbs[c]) for c in chains]
            dqs = [_dot(dzbs[c], kh[c]) for c in chains]
            for b, ks in enumerate(kss):
                pairs = range(b * nh, (b + 1) * nh, 2)
                dkt_acc[:, pl.ds(ks, tb)] += jnp.concatenate([dkts[c] + dkts[c + 1] for c in pairs], axis=0)
                dvt_acc[:, pl.ds(ks, tb)] += jnp.concatenate([dvts[c] + dvts[c + 1] for c in pairs], axis=0)
            last = (len(kjs) - 1) * nh
            return tuple((carry[h][0] + sum(dqs[h::nh]), lefts[last + h],
                          dlefts[last + h] + dprefixes[last + h][:, tb - 1:tb]) for h in range(nh))

        def query_block(qi, _):
            qs = pl.multiple_of(qi * tb, tb)
            qh = _own_tiles(_scaled(q_ref[pl.ds(qs, tb), :]), own)
            doh = _own_tiles(do_ref[pl.ds(qs, tb), :], own)
            qth = [a.astype(F32).T.astype(BF16) for a in qh]
            doth = [a.T.astype(BF16) for a in doh]
            doh = [a.astype(BF16) for a in doh]
            cv = c_ref[pl.ds(qs, tb), :]
            ch = [cv[:, h * HEAD_DIM:h * HEAD_DIM + 1] for h in range(SB_HEADS)]
            zero = (jnp.zeros((tb, LANES), F32), jnp.zeros((tb, 1), F32), jnp.zeros((tb, 1), F32))

            def key_block(kjs, cr, mask):
                return key_blocks(qh, qth, doh, doth, ch, kjs, cr, mask)

            carry = lax.fori_loop(0, qi // 2, lambda p, cr: key_block([2 * p, 2 * p + 1], cr, None),
                                  (zero,) * SB_HEADS)
            carry = lax.fori_loop(0, qi % 2, lambda _, cr: key_block([qi - 1], cr, None), carry)
            carry = key_block([qi], carry, causal)
            dq = _merge_tiles([cr[0] for cr in carry], own) * scale
            dq_ref[pl.ds(qs, tb), :] = dq.astype(BF16)
            return 0

        lax.fori_loop(0, n_blk, query_block, 0)
        dk_ref[...] = dkt_acc[...].T.astype(BF16)
        dv_ref[...] = dvt_acc[...].T.astype(BF16)

    def spec(offset):
        return pl.BlockSpec((seq, SB_WIDTH), lambda b, p: (b, offset + p))

    out = jax.ShapeDtypeStruct((t, n_pairs * SB_WIDTH), BF16)
    return _call(
        body, name="sb_bwd", grid=(nb, n_pairs), args=(qkv, qkv, qkv, do, csum),
        in_specs=[spec(0), spec(n_pairs), spec(2 * n_pairs), spec(0), spec(0)],
        out_specs=[spec(0), spec(0), spec(0)],
        out_shape=[out, out, out],
        scratch_shapes=[pltpu.VMEM((SB_WIDTH, seq), F32), pltpu.VMEM((SB_WIDTH, seq), F32)],
        params=_params("arbitrary", "arbitrary"), exchange=exchange)


def _dil_block_scores(qh, kph, kch, bias_ref, has_prev, band_prev, band_cur):
    scale = HEAD_DIM ** -0.5
    heads = range(len(qh))
    no_prev = jnp.where(has_prev, 0.0, NEG_INF)
    zps = [_dot_nt(qh[h], kph[h]) for h in heads]
    zcs = [_dot_nt(qh[h], kch[h]) for h in heads]
    zps = [jnp.where(band_prev, zps[h] * scale + bias_ref[h, :, 0:DIL_BLOCK], NEG_INF) + no_prev for h in heads]
    zcs = [jnp.where(band_cur, zcs[h] * scale + bias_ref[h, :, DIL_BLOCK:2 * DIL_BLOCK], NEG_INF) for h in heads]
    return zps, zcs


def _dil_bands():
    rows = lax.broadcasted_iota(jnp.int32, (DIL_BLOCK, DIL_BLOCK), 0)
    cols = lax.broadcasted_iota(jnp.int32, (DIL_BLOCK, DIL_BLOCK), 1)
    return cols >= rows, cols <= rows


def _dil_fwd(qkv, bias, nb, seq, dil, exchange=None):
    t, width = qkv.shape
    n_pairs = (width // 3) // DIL_WIDTH
    bq = DIL_BLOCK
    n_blk = seq // bq
    per_seq = n_blk // dil
    heads = range(DIL_HEADS)

    def body(q_ref, k_ref, v_ref, bias_ref, o_ref, lse_ref):
        band_prev, band_cur = _dil_bands()
        own = _own_lanes()

        def block(n, _):
            has_prev = (n & (per_seq - 1)) != 0
            qs = pl.multiple_of(n * bq, bq)
            ps = pl.multiple_of(jnp.maximum(n - 1, 0) * bq, bq)
            qh = _own_tiles(q_ref[pl.ds(qs, bq), :], own)
            kp, kc = _pair_tiles(k_ref[pl.ds(ps, bq), :]), _pair_tiles(k_ref[pl.ds(qs, bq), :])
            vp, vc = _pair_tiles(v_ref[pl.ds(ps, bq), :]), _pair_tiles(v_ref[pl.ds(qs, bq), :])
            zps, zcs = _dil_block_scores(qh, kp, kc, bias_ref, has_prev, band_prev, band_cur)
            ms = [jnp.maximum(jnp.max(zps[h], axis=1, keepdims=True), jnp.max(zcs[h], axis=1, keepdims=True))
                  for h in heads]
            eps = [jnp.exp(zps[h] - ms[h]) for h in heads]
            ecs = [jnp.exp(zcs[h] - ms[h]) for h in heads]
            pvs = [_dot(eps[h].astype(BF16), vp[h]) + _dot(ecs[h].astype(BF16), vc[h]) for h in heads]
            dens = [jnp.sum(eps[h], axis=1, keepdims=True) + jnp.sum(ecs[h], axis=1, keepdims=True) for h in heads]
            o_ref[pl.ds(qs, bq), :] = _merge_tiles([pvs[h] / dens[h] for h in heads], own)
            lse_ref[pl.ds(qs, bq), :] = _merge_tiles(
                [jnp.broadcast_to(ms[h] + jnp.log(dens[h]), (bq, LANES)) for h in heads], own)
            return 0

        lax.fori_loop(0, n_blk, block, 0, unroll=8)

    def spec(offset):
        return pl.BlockSpec((seq, DIL_WIDTH), lambda b, p: (b, offset + p))

    out = jax.ShapeDtypeStruct((t, n_pairs * DIL_WIDTH), F32)
    return _call(
        body, name=f"dil_fwd{dil}", grid=(nb, n_pairs), args=(qkv, qkv, qkv, bias),
        in_specs=[spec(0), spec(n_pairs), spec(2 * n_pairs),
                  pl.BlockSpec((DIL_HEADS, bq, 2 * bq), lambda b, p: (p, 0, 0))],
        out_specs=[spec(0), spec(0)], out_shape=[out, out],
        params=_params("arbitrary", "arbitrary"), exchange=exchange)


def _dil_bwd(qkv, bias, do, lse, delta, nb, seq, dil):
    t, width = qkv.shape
    n_pairs = (width // 3) // DIL_WIDTH
    bq = DIL_BLOCK
    n_blk = seq // bq
    per_seq = n_blk // dil
    scale = HEAD_DIM ** -0.5
    heads = range(DIL_HEADS)

    def body(q_ref, k_ref, v_ref, bias_ref, do_ref, lse_ref, dl_ref, dq_ref, dk_ref, dv_ref, db_ref,
             dk_acc, dv_acc):
        band_prev, band_cur = _dil_bands()
        own = _own_lanes()
        dk_acc[...] = jnp.zeros_like(dk_acc)
        dv_acc[...] = jnp.zeros_like(dv_acc)

        @pl.when(pl.program_id(1) == 0)
        def _():
            db_ref[...] = jnp.zeros_like(db_ref)

        def block(n, _):
            has_prev = (n & (per_seq - 1)) != 0
            qs = pl.multiple_of(n * bq, bq)
            ps = pl.multiple_of(jnp.maximum(n - 1, 0) * bq, bq)
            qh = _own_tiles(q_ref[pl.ds(qs, bq), :], own)
            kp, kc = _pair_tiles(k_ref[pl.ds(ps, bq), :]), _pair_tiles(k_ref[pl.ds(qs, bq), :])
            vp, vc = _pair_tiles(v_ref[pl.ds(ps, bq), :]), _pair_tiles(v_ref[pl.ds(qs, bq), :])
            doh = _own_tiles(do_ref[pl.ds(qs, bq), :].astype(BF16), own)
            lse_v, dl_v = lse_ref[pl.ds(qs, bq), :], dl_ref[pl.ds(qs, bq), :]
            zps, zcs = _dil_block_scores(qh, kp, kc, bias_ref, has_prev, band_prev, band_cur)
            dpp = [_dot_nt(doh[h], vp[h]) for h in heads]
            dpc = [_dot_nt(doh[h], vc[h]) for h in heads]
            lse_h = [lse_v[:, h * HEAD_DIM:h * HEAD_DIM + 1] for h in heads]
            dl_h = [dl_v[:, h * HEAD_DIM:h * HEAD_DIM + 1] for h in heads]
            pps = [jnp.exp(zps[h] - lse_h[h]) for h in heads]
            pcs = [jnp.exp(zcs[h] - lse_h[h]) for h in heads]
            dvp = [_dot_tn(pps[h].astype(BF16), doh[h]) for h in heads]
            dvc = [_dot_tn(pcs[h].astype(BF16), doh[h]) for h in heads]
            dzps = [pps[h] * (dpp[h] - dl_h[h]) for h in heads]
            dzcs = [pcs[h] * (dpc[h] - dl_h[h]) for h in heads]
            dzp_b = [(dzps[h] * scale).astype(BF16) for h in heads]
            dzc_b = [(dzcs[h] * scale).astype(BF16) for h in heads]
            dqs = [_dot(dzp_b[h], kp[h]) + _dot(dzc_b[h], kc[h]) for h in heads]
            dkp = [_dot_tn(dzp_b[h], qh[h]) for h in heads]
            dkc = [_dot_tn(dzc_b[h], qh[h]) for h in heads]
            for h in heads:
                db_ref[h, :, 0:bq] += dzps[h]
                db_ref[h, :, bq:2 * bq] += dzcs[h]
            def pair_sums(per_head):
                return jnp.concatenate([per_head[h] + per_head[h + 1] for h in heads[::2]], axis=1)

            dq_ref[pl.ds(qs, bq), :] = _merge_tiles(dqs, own).astype(BF16)
            dk_acc[pl.ds(ps, bq), :] += pair_sums(dkp)
            dk_acc[pl.ds(qs, bq), :] += pair_sums(dkc)
            dv_acc[pl.ds(ps, bq), :] += pair_sums(dvp)
            dv_acc[pl.ds(qs, bq), :] += pair_sums(dvc)
            return 0

        lax.fori_loop(0, n_blk, block, 0, unroll=8)
        dk_ref[...] = dk_acc[...].astype(BF16)
        dv_ref[...] = dv_acc[...].astype(BF16)

    def spec(offset):
        return pl.BlockSpec((seq, DIL_WIDTH), lambda p, b: (b, offset + p))

    bias_spec = pl.BlockSpec((DIL_HEADS, bq, 2 * bq), lambda p, b: (p, 0, 0))
    out = jax.ShapeDtypeStruct((t, n_pairs * DIL_WIDTH), BF16)
    return pl.pallas_call(
        body, name=f"dil_bwd{dil}", grid=(n_pairs, nb),
        in_specs=[spec(0), spec(n_pairs), spec(2 * n_pairs), bias_spec, spec(0), spec(0), spec(0)],
        out_specs=[spec(0), spec(0), spec(0), bias_spec],
        out_shape=[out, out, out, jax.ShapeDtypeStruct(bias.shape, F32)],
        scratch_shapes=[pltpu.VMEM((seq, DIL_WIDTH), F32), pltpu.VMEM((seq, DIL_WIDTH), F32)],
        compiler_params=_params("arbitrary", "arbitrary"),
    )(qkv, qkv, qkv, bias, do, lse, delta)


def _head_blocks(width):
    rows = lax.broadcasted_iota(jnp.int32, (width, width), 0) // HEAD_DIM
    cols = lax.broadcasted_iota(jnp.int32, (width, width), 1) // HEAD_DIM
    return (rows == cols).astype(BF16)


def _head_mean(v, gmat):
    return _split_dot(v, gmat) * (1.0 / HEAD_DIM)


def _residue_views(arrays, nb, seq):
    return [a if dil == 1 else a.reshape(nb, dil, seq // dil, a.shape[1]) for a, dil in zip(arrays, DILATIONS)]


def _mix_out_fwd(osb, ocs, lses, gsb, gdil, wout, x, mod, tm):
    t, d = x.shape
    ds = osb.shape[1]
    nt = t // tm
    nb = mod.shape[0]
    tpb = nt // nb
    seq = t // nb
    n_cfg = len(DILATIONS)

    def body(osb_ref, *refs):
        oc_refs, lse_refs = refs[:n_cfg], refs[n_cfg:2 * n_cfg]
        gsb_ref, gdil_ref, w_ref, x_ref, mod_ref = refs[2 * n_cfg:2 * n_cfg + 5]
        xo_ref, on_ref, m_ref, odil_ref = refs[2 * n_cfg + 5:2 * n_cfg + 9]
        ld_refs = refs[2 * n_cfg + 9:3 * n_cfg + 9]
        stages, sc = refs[3 * n_cfg + 9:]
        ocv, lsev = [oc_refs[0][...]], [lse_refs[0][...]]
        for i, dil in enumerate(DILATIONS[1:]):
            ocv.append(_from_residue_rows(oc_refs[i + 1], stages.at[2 * i], dil))
            lsev.append(_from_residue_rows(lse_refs[i + 1], stages.at[2 * i + 1], dil))
        top = functools.reduce(jnp.maximum, lsev)
        total = top + jnp.log(sum(jnp.exp(l - top) for l in lsev))
        odil = sum(jnp.exp(l - total) * o for o, l in zip(ocv, lsev))
        odil_ref[...] = odil
        ld_refs[0][...] = total
        _stage(total, sc)
        for ref, dil in zip(ld_refs[1:], DILATIONS[1:]):
            _to_residue_rows(sc, ref, dil)
        gm = _head_blocks(ds)
        parts = []
        for o, g_ref in ((osb_ref[...], gsb_ref), (odil, gdil_ref)):
            parts.append(o * lax.rsqrt(_head_mean(o * o, gm) + EPS) * g_ref[...])
        on = jnp.concatenate(parts, axis=1).astype(BF16)
        on_ref[...] = on
        m = _dot(on, w_ref[...])
        m_ref[...] = m
        xo_ref[...] = x_ref[...] + mod_ref[5:6, :] * m

    tok = pl.BlockSpec((tm, d), lambda i: (i, 0))
    hd = pl.BlockSpec((tm, ds), lambda i: (i, 0))
    res = [hd] + [_residue_spec(tm, tpb, ds, dil, lambda i: 0) for dil in DILATIONS[1:]]
    res_shape = [jax.ShapeDtypeStruct((t, ds), F32)] + [_residue_shape(nb, seq, ds, dil, F32) for dil in DILATIONS[1:]]
    gain = pl.BlockSpec((1, ds), lambda i: (0, 0))
    outs = pl.pallas_call(
        body, name="mix_out_fwd", grid=(nt,),
        in_specs=[hd] + res + res + [gain, gain,
                  pl.BlockSpec(wout.shape, lambda i: (0, 0)),
                  tok, pl.BlockSpec((None, N_MOD, d), lambda i: (i // tpb, 0, 0))],
        out_specs=[tok, pl.BlockSpec((tm, 2 * ds), lambda i: (i, 0)), tok, hd] + res,
        out_shape=[jax.ShapeDtypeStruct((t, d), F32), jax.ShapeDtypeStruct((t, 2 * ds), BF16),
                   jax.ShapeDtypeStruct((t, d), F32), jax.ShapeDtypeStruct((t, ds), F32)] + res_shape,
        scratch_shapes=[pltpu.VMEM((2 * (n_cfg - 1), ds // LANES, tm, LANES), F32), _stage_shape(tm, ds)],
        compiler_params=_params("arbitrary"),
    )(osb, *_residue_views(ocs, nb, seq), *_residue_views(lses, nb, seq), gsb, gdil, wout, x, mod)
    return outs[0], outs[1], outs[2], outs[3], [a.reshape(t, ds) for a in outs[4:]]


def _mix_out_bwd(dxo, m, mod, wout, osb, odil, gsb, gdil, tm):
    t, d = dxo.shape
    ds = osb.shape[1]
    nt = t // tm
    nb = mod.shape[0]
    tpb = nt // nb
    seq = t // nb
    n_cfg = len(DILATIONS)

    def body(dxo_ref, m_ref, mod_ref, w_ref, osb_ref, odil_ref, gsb_ref, gdil_ref,
             dm_ref, dosb_ref, *rest):
        do_refs, dl_refs = rest[:n_cfg], rest[n_cfg:2 * n_cfg]
        dmod_ref, dg_ref, sc = rest[2 * n_cfg:]
        dodil_ref, dldil_ref = do_refs[0], dl_refs[0]
        i = pl.program_id(0)
        dxo_v = dxo_ref[...]
        dm = (mod_ref[5:6, :] * dxo_v).astype(BF16)
        dm_ref[...] = dm
        dgt = jnp.sum(m_ref[...] * dxo_v, axis=0, keepdims=True)
        don = _dot_nt(dm, w_ref[...])
        gm = _head_blocks(ds)

        @pl.when(i % tpb == 0)
        def _():
            dmod_ref[...] = jnp.zeros_like(dmod_ref)

        @pl.when(i == 0)
        def _():
            dg_ref[...] = jnp.zeros_like(dg_ref)

        dmod_ref[2:3, :] += dgt
        groups = ((osb_ref, gsb_ref, dosb_ref), (odil_ref, gdil_ref, dodil_ref))
        for k, (o_ref, g_ref, do_ref) in enumerate(groups):
            o = o_ref[...]
            dn_out = don[:, k * ds:(k + 1) * ds]
            r = lax.rsqrt(_head_mean(o * o, gm) + EPS)
            n = o * r
            dg_ref[0:1, k * ds:(k + 1) * ds] += jnp.sum(dn_out * n, axis=0, keepdims=True)
            dn = dn_out * g_ref[...]
            do = r * (dn - n * _head_mean(dn * n, gm))
            do_ref[...] = do
            if k == 1:
                delta = _head_mean(do * o, gm) * float(HEAD_DIM)
                dldil_ref[...] = delta
                for value, refs in ((do, do_refs), (delta, dl_refs)):
                    _stage(value, sc)
                    for ref, dil in zip(refs[1:], DILATIONS[1:]):
                        _to_residue_rows(sc, ref, dil)

    tok = pl.BlockSpec((tm, d), lambda i: (i, 0))
    hd = pl.BlockSpec((tm, ds), lambda i: (i, 0))
    res = [hd] + [_residue_spec(tm, tpb, ds, dil, lambda i: 0) for dil in DILATIONS[1:]]
    res_shape = [jax.ShapeDtypeStruct((t, ds), F32)] + [_residue_shape(nb, seq, ds, dil, F32) for dil in DILATIONS[1:]]
    gain = pl.BlockSpec((1, ds), lambda i: (0, 0))
    outs = pl.pallas_call(
        body, name="mix_out_bwd", grid=(nt,),
        in_specs=[tok, tok, pl.BlockSpec((None, N_MOD, d), lambda i: (i // tpb, 0, 0)),
                  pl.BlockSpec(wout.shape, lambda i: (0, 0)), hd, hd, gain, gain],
        out_specs=[tok, hd] + res + res
        + [pl.BlockSpec((None, 8, d), lambda i: (i // tpb, 0, 0)), pl.BlockSpec((8, 2 * ds), lambda i: (0, 0))],
        out_shape=[jax.ShapeDtypeStruct((t, d), BF16), jax.ShapeDtypeStruct((t, ds), F32)] + res_shape + res_shape
        + [jax.ShapeDtypeStruct((nb, 8, d), F32), jax.ShapeDtypeStruct((8, 2 * ds), F32)],
        scratch_shapes=[_stage_shape(tm, ds)],
        compiler_params=_params("arbitrary"),
    )(dxo, m, mod, wout, osb, odil, gsb, gdil)
    flat = [a.reshape(t, ds) for a in outs[2:2 + 2 * n_cfg]]
    return outs[0], outs[1], flat[:n_cfg], flat[n_cfg:], outs[-2], outs[-1]


def _merge_dqkv(sb_parts, dil_parts, nb, tm):
    t, ds = sb_parts[0].shape
    nt = t // tm
    tpb = nt // nb
    seq = t // nb
    n_cfg = len(DILATIONS)

    def body(*refs):
        sb_refs, dil_refs = refs[:3], refs[3:3 + 3 * n_cfg]
        o_ref, sc = refs[3 + 3 * n_cfg:]
        for k in range(3):
            o_ref[:, k * ds:(k + 1) * ds] = sb_refs[k][...]
            total = dil_refs[k * n_cfg][...].astype(F32)
            for i, dil in enumerate(DILATIONS[1:]):
                total = total + _from_residue_rows(dil_refs[k * n_cfg + i + 1], sc, dil)
            o_ref[:, (3 + k) * ds:(4 + k) * ds] = total.astype(BF16)

    hd = pl.BlockSpec((tm, ds), lambda i: (i, 0))
    res = [hd] + [_residue_spec(tm, tpb, ds, dil, lambda i: 0) for dil in DILATIONS[1:]]
    views = [v for parts in dil_parts for v in _residue_views(parts, nb, seq)]
    return pl.pallas_call(
        body, name="merge_dqkv", grid=(nt,),
        in_specs=[hd] * 3 + res * 3,
        out_specs=pl.BlockSpec((tm, 6 * ds), lambda i: (i, 0)),
        out_shape=jax.ShapeDtypeStruct((t, 6 * ds), BF16),
        scratch_shapes=[_stage_shape(tm, ds)],
        compiler_params=_params("arbitrary"),
    )(*sb_parts, *views)


def _row_tile(rows):
    if rows <= 256:
        return rows
    for cand in range(256, 15, -16):
        if rows % cand == 0:
            return cand
    return rows


def _adamw(w, parts, m, v, name, transposed=False):
    rows, cols = w.shape
    n_parts = parts.shape[0]
    tr = _row_tile(rows)
    c1 = 1.0 / (1.0 - ADAM_B1 ** ADAM_STEP)
    c2 = 1.0 / (1.0 - ADAM_B2 ** ADAM_STEP)

    def body(w_ref, p_ref, m_ref, v_ref, g_ref, d_ref, nm_ref, nv_ref):
        g = p_ref[0].astype(F32)
        for i in range(1, n_parts):
            g = g + p_ref[i].astype(F32)
        wv, mv, vv = w_ref[...], m_ref[...], v_ref[...]
        if transposed:
            wv, mv, vv = wv.T, mv.T, vv.T
        nm = ADAM_B1 * mv + (1.0 - ADAM_B1) * g
        nv = ADAM_B2 * vv + (1.0 - ADAM_B2) * (g * g)
        g_ref[...] = g
        nm_ref[...] = nm
        nv_ref[...] = nv
        d_ref[...] = -ADAM_LR * ((nm * c1) / (jnp.sqrt(nv * c2) + ADAM_EPS) + ADAM_WD * wv)

    blk = pl.BlockSpec((tr, cols), lambda i: (i, 0))
    if transposed:
        oblk = pl.BlockSpec((cols, tr), lambda i: (0, i))
        pblk = pl.BlockSpec((n_parts, cols, tr), lambda i: (0, 0, i))
        out = jax.ShapeDtypeStruct((cols, rows), F32)
    else:
        oblk, pblk = blk, pl.BlockSpec((n_parts, tr, cols), lambda i: (0, i, 0))
        out = jax.ShapeDtypeStruct((rows, cols), F32)
    return pl.pallas_call(
        body, name=name, grid=(rows // tr,),
        in_specs=[blk, pblk, blk, blk],
        out_specs=[oblk, oblk, oblk, oblk], out_shape=[out, out, out, out],
        compiler_params=_params("arbitrary"),
    )(w, parts, m, v)


def _t5_bucket(n):
    max_exact = N_BUCKETS // 2
    nf = np.maximum(n, 1).astype(np.float32)
    large = max_exact + (np.log(nf / max_exact) / math.log(MAX_DISTANCE / max_exact)
                         * (N_BUCKETS - max_exact)).astype(np.int32)
    large = np.minimum(large, N_BUCKETS - 1)
    return np.where(n < max_exact, n, large).astype(np.int32)


def _bucket_onehot():
    table = np.zeros((len(DILATIONS), 2 * DIL_BLOCK + 1, N_BUCKETS), np.float32)
    for i, dil in enumerate(DILATIONS):
        buckets = _t5_bucket(np.arange(DIL_BLOCK + 1) * dil)
        for m in range(DIL_BLOCK + 1):
            table[i, m, buckets[DIL_BLOCK - m]] = 1.0
    return table


def _bias_blocks(rel_bias):
    row = jnp.einsum("cmn,nh->chm", _bucket_onehot(), rel_bias, precision=lax.Precision.HIGHEST)
    n_cfg, n_heads, width = row.shape
    tiled = jnp.tile(row, (1, 1, DIL_BLOCK))[..., :DIL_BLOCK * (width - 1)]
    return tiled.reshape(n_cfg, n_heads, DIL_BLOCK, width - 1)


def _bias_blocks_bwd(dblocks):
    n_cfg, n_heads = dblocks.shape[:2]
    width = 2 * DIL_BLOCK + 1
    flat = dblocks.reshape(n_cfg, n_heads, DIL_BLOCK * (width - 1))
    flat = jnp.pad(flat, ((0, 0), (0, 0), (0, DIL_BLOCK)))
    drow = jnp.sum(flat.reshape(n_cfg, n_heads, DIL_BLOCK, width), axis=2)
    return jnp.einsum("chm,cmn->nh", drow, _bucket_onehot(), precision=lax.Precision.HIGHEST)


def _pad_to(a, axis, size):
    pad = [(0, 0)] * a.ndim
    pad[axis] = (0, size - a.shape[axis])
    return jnp.pad(a, pad)


def _lane_pad(n):
    return -(-n // LANES) * LANES


def _local_step(x, target, mod, gains, weights, rel_bias, tm, distributed):
    nb, seq, d = x.shape
    t = nb * seq
    g_ffn1, g_mix, g_sb, g_dil, g_ffn2, g_final = gains
    wg1, wu1, wd1 = weights[:3]
    x0 = x.reshape(t, d)
    ds = g_sb.shape[1]
    bias = _bias_blocks(rel_bias)

    def beside(arrays, scatter):
        return _Exchange(arrays, scatter) if distributed else None

    tp, tg = min(PROJ_TILE, seq), min(GRAD_TILE, t)

    (x1, f1, gate1, up1), got = _ffn_fwd(x0, mod, g_ffn1, wg1, wu1, wd1, 0, tp, beside(weights[3:4], False))
    win = got[0] if distributed else weights[3]
    (qkv, qkvd, h2), got = _qkv_fwd(x1, mod, g_mix, win, tp, beside(weights[4:5], False))
    wout = got[0] if distributed else weights[4]
    wout2 = wout.reshape(-1, d)
    (osb, csb), got = _sb_fwd(qkv, nb, seq, beside(weights[5:7], False))
    wg2, wu2 = got if distributed else weights[5:7]
    n_cfg = len(DILATIONS)
    piece = -(-weights[7].shape[-2] // n_cfg // 16) * 16
    ocs, lses, wd2_pieces = [], [], []
    for i, dil in enumerate(DILATIONS):
        rows = weights[7][..., i * piece:(i + 1) * piece, :]
        (oc, lse), got = _dil_fwd(qkvd[i], bias[i], nb, seq, dil, beside([rows], False))
        wd2_pieces.append(got[0] if distributed else rows)
        ocs.append(oc)
        lses.append(lse)
    wd2 = jnp.concatenate(wd2_pieces, axis=-2)
    x2, on, mix, odil, ldil = _mix_out_fwd(osb, ocs, lses, g_sb, g_dil, wout2, x1, mod, tm)
    (dx3, f3, gate3, up3, head), _ = _ffn_fwd(x2, mod, g_ffn2, wg2, wu2, wd2, 2, tp,
                                              head=(target.reshape(t, d), g_final))
    loss_sum = 0.5 * jnp.sum(head[0]) / d
    dg_final = head[1:2]

    (dx2, dgate3, dup3, act3, h3, df3, dmod3, dg_ffn2), _ = _ffn_bwd(
        dx3, x2, f3, mod, g_ffn2, gate3, up3, wg2, wu2, wd2, 2, tm)
    gwg2, gwu2, gwd2 = _ffn_weight_grads(h3, dgate3, dup3, act3, df3, tg, 2)

    dm, dosb, dodil, dldil, dmod2b, dg_heads = _mix_out_bwd(
        dx2, mix, mod, wout2, osb, odil, g_sb, g_dil, tm)
    gwout = _mm_tn(on, dm,
                   pl.BlockSpec((tg, wout.shape[1]), lambda i, j: (i, j)),
                   pl.BlockSpec((tg, d), lambda i, j: (i, 0)),
                   wout.shape, t // tg, "grad_wout")

    (dq_sb, dk_sb, dv_sb), parts_late = _sb_bwd(qkv, dosb, csb, nb, seq,
                                                beside([gwout, gwg2, gwu2, gwd2], True))
    dil_grads = [_dil_bwd(qkvd[i], bias[i], dodil[i], ldil[i], dldil[i], nb, seq, dil)
                 for i, dil in enumerate(DILATIONS)]
    dqkv = _merge_dqkv([dq_sb, dk_sb, dv_sb], [[g[k] for g in dil_grads] for k in range(3)], nb, tm)
    drel = _bias_blocks_bwd(jnp.stack([g[3] for g in dil_grads]))

    cs = win.shape[2]
    gwin = _mm_tn(h2, dqkv,
                  pl.BlockSpec((tg, d), lambda i, j: (i, 0)),
                  pl.BlockSpec((tg, cs), lambda i, j: (i, j)),
                  win.shape, t // tg, "grad_win", pair_reduce=distributed)
    (dx1, dmod2a, dg_mix), parts_mid = _qkv_bwd(
        dqkv, dx2, x1, mod, g_mix, win, tp, _Exchange([gwin], True, chips=[True]) if distributed else None)

    (dx0, dgate1, dup1, act1, h1, df1, dmod1, dg_ffn1), _ = _ffn_bwd(
        dx1, x0, f1, mod, g_ffn1, gate1, up1, wg1, wu1, wd1, 0, tm)
    dmod = jnp.concatenate([dmod1[:, 0:3], dmod2a[:, 0:2], dmod2b[:, 2:3], dmod3[:, 0:3]], axis=1)
    ggrads = (dg_ffn1[0:1], dg_mix[0:1], dg_heads[0:1], drel, dg_ffn2[0:1], dg_final)
    if not distributed:
        gw1 = _ffn_weight_grads(h1, dgate1, dup1, act1, df1, tg, 0)
        return loss_sum, dx0.reshape(nb, seq, d), tuple(gw1) + (gwin, gwout, gwg2, gwu2, gwd2), dmod, ggrads

    dg_heads_row, drel_flat = dg_heads[0:1], drel.reshape(1, -1)
    width = max(d, dg_heads_row.shape[1], drel_flat.shape[1])
    small = jnp.concatenate(
        [_pad_to(a.reshape(1, -1), 1, width)
         for a in (dg_ffn1[0:1], dg_mix[0:1], dg_ffn2[0:1], dg_final, dg_heads_row, drel_flat, loss_sum)]
        + [jnp.zeros((1, width), F32)], axis=0)
    dmod_pad = _pad_to(dmod.reshape(nb, N_MOD * d), 0, 8)
    everyone = _Exchange([jnp.broadcast_to(dmod_pad, (N_DEV,) + dmod_pad.shape),
                          jnp.broadcast_to(small, (N_DEV,) + small.shape)], True)
    sent_g, sent_u, gwd1, (dmod_all, small_all) = _ffn_weight_grads(
        h1, dgate1, dup1, act1, df1, tg, 0, stream=True, first=everyone)
    wgrads = (sent_g, sent_u, gwd1) + tuple(parts_mid + parts_late)
    return dx0.reshape(nb, seq, d), wgrads, dmod_all, small_all


def kernel(x, c, w_ada, b_ada, g_ffn1, w1_gate, w1_up, w1_down, g_mix, w_in, g_sb_out, g_dil_out, w_out, rel_bias, g_ffn2, w2_gate, w2_up, w2_down, g_final, loss_target, m_w_ada, m_b_ada, m_g_ffn1, m_w1_gate, m_w1_up, m_w1_down, m_g_mix, m_w_in, m_g_sb_out, m_g_dil_out, m_w_out, m_rel_bias, m_g_ffn2, m_w2_gate, m_w2_up, m_w2_down, m_g_final, v_w_ada, v_b_ada, v_g_ffn1, v_w1_gate, v_w1_up, v_w1_down, v_g_mix, v_w_in, v_g_sb_out, v_g_dil_out, v_w_out, v_rel_bias, v_g_ffn2, v_w2_gate, v_w2_up, v_w2_down, v_g_final):
    nb, seq, d = x.shape
    me = 4 * lax.axis_index("x") + 2 * lax.axis_index("y") + lax.axis_index("c")
    tm = min(TOKEN_TILE, seq)
    fs = w1_gate.shape[2]
    fs_pad = _lane_pad(fs)
    ada_cols = w_ada.shape[2]

    def col_shard(w):
        return _pad_to(w[0].astype(BF16), 1, fs_pad)

    def row_shard(w):
        return _pad_to(w[0].astype(BF16), 0, fs_pad)

    shards = [col_shard(w1_gate), col_shard(w1_up), row_shard(w1_down), w_in[0].astype(BF16),
              w_out[0].astype(BF16), col_shard(w2_gate), col_shard(w2_up), row_shard(w2_down)]
    b_cols = lax.dynamic_slice(b_ada, (0, me * ada_cols), (1, ada_cols))
    c_every, mod_all, first = _first_exchange(_pad_to(c, 0, 8), shards[:3], w_ada[0], b_cols)
    c_all = c_every[:, :nb].reshape(N_DEV * nb, d)
    weights = first + shards[3:]
    mod = lax.dynamic_slice(mod_all, (0, me * 8, 0), (N_DEV, nb, ada_cols))
    mod = mod.transpose(1, 0, 2).reshape(nb, N_MOD, d)

    n_sb = g_sb_out.shape[1] * g_sb_out.shape[2]
    gains = (g_ffn1, g_mix, g_sb_out.reshape(1, n_sb), g_dil_out.reshape(1, -1), g_ffn2,
             g_final.reshape(1, d))
    grad_x, parts, dmod_all, small_all = _local_step(x, loss_target, mod, gains, weights, rel_bias, tm, True)

    last_part = _exchange([parts[2]], True, "scatter_last", chips=[True])[0]
    parts = parts[:2] + (last_part,) + parts[3:]
    dmod_all = dmod_all[:, :nb].reshape(N_DEV * nb, N_MOD * d)
    dmod_cols = lax.dynamic_slice(dmod_all, (0, me * ada_cols), (N_DEV * nb, ada_cols))
    gw_ada, gb_ada = _ada_bwd(c_all, dmod_cols, dmod_all)

    def small_part(row, size, shape):
        return small_all[:, row, :size].reshape((N_DEV,) + shape)

    loss = jnp.sum(small_all[:, 6, 0])

    n_rel = rel_bias.shape
    updates = {
        "w_ada": (w_ada[0], gw_ada[None], m_w_ada[0], v_w_ada[0]),
        "b_ada": (b_ada, gb_ada[None], m_b_ada, v_b_ada),
        "g_ffn1": (g_ffn1, small_part(0, d, (1, d)), m_g_ffn1, v_g_ffn1),
        "w1_gate": (w1_gate[0], parts[0], m_w1_gate[0], v_w1_gate[0]),
        "w1_up": (w1_up[0], parts[1], m_w1_up[0], v_w1_up[0]),
        "w1_down": (w1_down[0], parts[2], m_w1_down[0], v_w1_down[0]),
        "g_mix": (g_mix, small_part(1, d, (1, d)), m_g_mix, v_g_mix),
        "w_in": (w_in[0], parts[3], m_w_in[0], v_w_in[0]),
        "g_sb_out": (g_sb_out[0], small_all[:, 4, :n_sb].reshape((N_DEV,) + g_sb_out.shape[1:]),
                     m_g_sb_out[0], v_g_sb_out[0]),
        "g_dil_out": (g_dil_out[0], small_all[:, 4, n_sb:n_sb + g_dil_out[0].size].reshape((N_DEV,) + g_dil_out.shape[1:]),
                      m_g_dil_out[0], v_g_dil_out[0]),
        "w_out": (w_out[0], parts[4], m_w_out[0], v_w_out[0]),
        "rel_bias": (rel_bias, small_part(5, rel_bias.size, n_rel), m_rel_bias, v_rel_bias),
        "g_ffn2": (g_ffn2, small_part(2, d, (1, d)), m_g_ffn2, v_g_ffn2),
        "w2_gate": (w2_gate[0], parts[5], m_w2_gate[0], v_w2_gate[0]),
        "w2_up": (w2_up[0], parts[6], m_w2_up[0], v_w2_up[0]),
        "w2_down": (w2_down[0], parts[7], m_w2_down[0], v_w2_down[0]),
        "g_final": (g_final.reshape(1, d), small_part(3, d, (1, d)), m_g_final.reshape(1, d), v_g_final.reshape(1, d)),
    }
    shapes = {"w_ada": w_ada.shape, "b_ada": b_ada.shape, "g_ffn1": g_ffn1.shape, "w1_gate": w1_gate.shape,
              "w1_up": w1_up.shape, "w1_down": w1_down.shape, "g_mix": g_mix.shape, "w_in": w_in.shape,
              "g_sb_out": g_sb_out.shape, "g_dil_out": g_dil_out.shape, "w_out": w_out.shape,
              "rel_bias": rel_bias.shape, "g_ffn2": g_ffn2.shape, "w2_gate": w2_gate.shape,
              "w2_up": w2_up.shape, "w2_down": w2_down.shape, "g_final": g_final.shape}
    grads, deltas, new_m, new_v = [], [], [], []
    for name, (w, p, m, v) in updates.items():
        transposed = name in ("w1_gate", "w1_up", "w2_gate", "w2_up")
        outs = _adamw(w, p, m, v, f"adamw_{name}", transposed)
        for dst, a in zip((grads, deltas, new_m, new_v), outs):
            dst.append((a.T if transposed else a).reshape(shapes[name]))
    return (loss, grad_x, *grads, *deltas, *new_m, *new_v)
```

```python
import functools
import math

import numpy as np
import jax
import jax.numpy as jnp
from jax import lax
from jax.experimental import pallas as pl
from jax.experimental.pallas import tpu as pltpu

F32 = jnp.float32
BF16 = jnp.bfloat16

EPS = 1e-6
NEG_INF = -1e30
HEAD_DIM = 64
LANES = 128
DIL_BLOCK = 128
DILATIONS = (1, 4, 16)
N_BUCKETS = 32
MAX_DISTANCE = 2048
N_MOD = 9
N_DEV = 8
SB_BLOCK = 256
SB_HEADS = 4
SB_WIDTH = SB_HEADS * HEAD_DIM
DIL_HEADS = 4
DIL_WIDTH = DIL_HEADS * HEAD_DIM
TOKEN_TILE = 512
PROJ_TILE = 1024
GRAD_TILE = 1024
SHARD_GROUP = 2
FFN_CHUNKS = 4
VMEM_LIMIT_BYTES = 56 * 1024 * 1024

ADAM_LR = 0.001
ADAM_B1 = 0.9
ADAM_B2 = 0.999
ADAM_EPS = 1e-08
ADAM_WD = 0.01
ADAM_STEP = 10

NT_DIMS = (((1,), (1,)), ((), ()))
TN_DIMS = (((0,), (0,)), ((), ()))


def _params(*sem):
    return pltpu.CompilerParams(dimension_semantics=sem, vmem_limit_bytes=VMEM_LIMIT_BYTES)


def _once(spec):
    return pl.BlockSpec(spec.block_shape, spec.index_map, pipeline_mode=pl.Buffered(1))


def _dot(a, b):
    return jnp.dot(a, b, preferred_element_type=F32)


def _dot_nt(a, b):
    return lax.dot_general(a, b, NT_DIMS, preferred_element_type=F32)


def _dot_tn(a, b):
    return lax.dot_general(a, b, TN_DIMS, preferred_element_type=F32)


def _split_dot(a, b):
    hi = a.astype(BF16)
    lo = (a - hi.astype(F32)).astype(BF16)
    return _dot(hi, b) + _dot(lo, b)


def _sigmoid(z):
    return 1.0 / (1.0 + jnp.exp(-z))


def _norm(x):
    r = lax.rsqrt(jnp.mean(x * x, axis=-1, keepdims=True) + EPS)
    return x * r, r


def _modulate(x, g, mod_ref, k):
    n, _ = _norm(x)
    shift = mod_ref[3 * k:3 * k + 1, :]
    scale = mod_ref[3 * k + 1:3 * k + 2, :]
    return n * g * (1.0 + scale) + shift


def _modulate_bwd(dh, x, g, mod_ref, k):
    n, r = _norm(x)
    scale = mod_ref[3 * k + 1:3 * k + 2, :]
    dshift = jnp.sum(dh, axis=0, keepdims=True)
    dscale = jnp.sum(dh * n * g, axis=0, keepdims=True)
    dg = jnp.sum(dh * n * (1.0 + scale), axis=0, keepdims=True)
    dn = dh * g * (1.0 + scale)
    dx = r * (dn - n * jnp.mean(dn * n, axis=-1, keepdims=True))
    return dx, dshift, dscale, dg


class _Exchange:
    def __init__(self, arrays, scatter, relay=False, chips=None):
        assert not (scatter and relay)
        self.arrays = list(arrays)
        self.scatter = scatter
        self.relay = relay
        self.n = len(self.arrays)
        self.chips = list(chips) if chips is not None else [False] * self.n
        assert scatter or not any(self.chips)
        self.out_shape = [
            jax.ShapeDtypeStruct((N_DEV // 2 if ch else N_DEV,) + tuple(a.shape[1:] if scatter else a.shape), a.dtype)
            for a, ch in zip(self.arrays, self.chips)]
        n_remote = self.n * (N_DEV - 1)
        self.scratch_shapes = [pltpu.SemaphoreType.DMA((n_remote,)), pltpu.SemaphoreType.DMA((n_remote,)),
                               pltpu.SemaphoreType.DMA((self.n,))]

    def _copies(self, in_refs, out_refs, sems):
        send_sems, recv_sems, local_sems = sems
        x, y, c = lax.axis_index("x"), lax.axis_index("y"), lax.axis_index("c")
        me = 4 * x + 2 * y + c
        local, remote, relayed = [], {}, {}
        for a in range(self.n):
            if self.chips[a]:
                mine = 2 * x + y
                local.append(pltpu.make_async_copy(in_refs[a].at[mine], out_refs[a].at[mine], local_sems.at[a]))
                for k in (2, 4, 6):
                    px = 1 - x if k & 4 else x
                    py = 1 - y if k & 2 else y
                    sem = a * (N_DEV - 1) + k - 1
                    remote[a, k] = pltpu.make_async_remote_copy(
                        src_ref=in_refs[a].at[2 * px + py], dst_ref=out_refs[a].at[mine],
                        send_sem=send_sems.at[sem], recv_sem=recv_sems.at[sem],
                        device_id=(px, py, c), device_id_type=pl.DeviceIdType.MESH)
                continue
            src = in_refs[a].at[me] if self.scatter else in_refs[a]
            local.append(pltpu.make_async_copy(src, out_refs[a].at[me], local_sems.at[a]))
            for k in range(1, N_DEV):
                px = 1 - x if k & 4 else x
                py = 1 - y if k & 2 else y
                pc = 1 - c if k & 1 else c
                sem = a * (N_DEV - 1) + k - 1
                if self.relay and k & 1 and k > 1:
                    slot = 4 * px + 2 * py + c
                    relayed[a, k] = pltpu.make_async_remote_copy(
                        src_ref=out_refs[a].at[slot], dst_ref=out_refs[a].at[slot],
                        send_sem=send_sems.at[sem], recv_sem=recv_sems.at[sem],
                        device_id=(x, y, 1 - c), device_id_type=pl.DeviceIdType.MESH)
                    continue
                src = in_refs[a].at[4 * px + 2 * py + pc] if self.scatter else in_refs[a]
                remote[a, k] = pltpu.make_async_remote_copy(
                    src_ref=src, dst_ref=out_refs[a].at[me],
                    send_sem=send_sems.at[sem], recv_sem=recv_sems.at[sem],
                    device_id=(px, py, pc), device_id_type=pl.DeviceIdType.MESH)
        return local, remote, relayed

    def start(self, in_refs, out_refs, sems):
        local, remote, _ = self._copies(in_refs, out_refs, sems)
        for cp in local + list(remote.values()):
            cp.start()

    def wait(self, in_refs, out_refs, sems):
        local, remote, relayed = self._copies(in_refs, out_refs, sems)
        for (a, k), cp in relayed.items():
            remote[a, k - 1].wait_recv()
            cp.start()
        for (a, k), cp in remote.items():
            if (a, k + 1) not in relayed:
                cp.wait_recv()
        for cp in relayed.values():
            cp.wait_recv()
        for cp in list(remote.values()) + list(relayed.values()):
            cp.wait_send()
        for cp in local:
            cp.wait()


def _call(body, *, name, args, in_specs, out_specs, out_shape, scratch_shapes=(), grid=(),
          params=None, exchange=None):
    n_in, n_out = len(args), len(out_shape)
    if exchange is None:
        outs = pl.pallas_call(
            body, name=name, grid=grid, in_specs=list(in_specs), out_specs=list(out_specs),
            out_shape=list(out_shape), scratch_shapes=list(scratch_shapes), compiler_params=params,
        )(*args)
        return list(outs), []
    n_ex = exchange.n

    def wrapped(*refs):
        ins, refs = refs[:n_in], refs[n_in:]
        ex_in, refs = refs[:n_ex], refs[n_ex:]
        outs, refs = refs[:n_out], refs[n_out:]
        ex_out, refs = refs[:n_ex], refs[n_ex:]
        scratch, sems = refs[:len(refs) - 3], refs[len(refs) - 3:]
        if not grid:
            exchange.start(ex_in, ex_out, sems)
            body(*ins, *outs, *scratch)
            exchange.wait(ex_in, ex_out, sems)
            return
        first = functools.reduce(jnp.logical_and, [pl.program_id(a) == 0 for a in range(len(grid))])
        last = functools.reduce(jnp.logical_and, [pl.program_id(a) == grid[a] - 1 for a in range(len(grid))])

        @pl.when(first)
        def _():
            exchange.start(ex_in, ex_out, sems)

        body(*ins, *outs, *scratch)

        @pl.when(last)
        def _():
            exchange.wait(ex_in, ex_out, sems)

    any_spec = pl.BlockSpec(memory_space=pl.ANY)
    outs = pl.pallas_call(
        wrapped, name=name, grid=grid,
        in_specs=list(in_specs) + [any_spec] * n_ex, out_specs=list(out_specs) + [any_spec] * n_ex,
        out_shape=list(out_shape) + exchange.out_shape,
        scratch_shapes=list(scratch_shapes) + exchange.scratch_shapes, compiler_params=params,
    )(*args, *exchange.arrays)
    return list(outs[:n_out]), list(outs[n_out:])


def _exchange(arrays, scatter, name, relay=False, chips=None):
    return _call(lambda: None, name=name, args=(), in_specs=(), out_specs=(), out_shape=(),
                 exchange=_Exchange(arrays, scatter, relay, chips))[1]


def _first_exchange(c_pad, shards, w, b):
    rows, d = c_pad.shape
    cols = w.shape[1]
    ex_c = _Exchange([c_pad], False)
    ex_w = _Exchange(shards, False, relay=True)
    ex_m = _Exchange([jax.ShapeDtypeStruct((N_DEV * rows, cols), F32)], False)
    n_w = ex_w.n

    def body(*refs):
        c_ref, w_refs, wa_ref, b_ref = refs[0], refs[1:1 + n_w], refs[1 + n_w], refs[2 + n_w]
        outs = refs[3 + n_w:]
        cg_ref, wg_refs, mg_ref = outs[0], outs[1:1 + n_w], outs[1 + n_w]
        scratch = outs[2 + n_w:]
        sems_c, sems_w, sems_m, c_vm, m_vm = scratch[0:3], scratch[3:6], scratch[6:9], scratch[9], scratch[10]
        ex_c.start([c_ref], [cg_ref], sems_c)
        ex_c.wait([c_ref], [cg_ref], sems_c)
        pltpu.sync_copy(cg_ref, c_vm)
        cv = c_vm[...].reshape(N_DEV * rows, d)
        s = (cv * _sigmoid(cv)).astype(BF16)
        m_vm[...] = _dot(s, wa_ref[...].astype(BF16)) + b_ref[...]
        ex_m.start([m_vm], [mg_ref], sems_m)
        ex_w.start(w_refs, wg_refs, sems_w)
        ex_m.wait([m_vm], [mg_ref], sems_m)
        ex_w.wait(w_refs, wg_refs, sems_w)

    any_spec = pl.BlockSpec(memory_space=pl.ANY)
    vmem_spec = pl.BlockSpec(memory_space=pltpu.VMEM)
    outs = pl.pallas_call(
        body, name="first_exchange",
        in_specs=[any_spec] * (1 + n_w) + [vmem_spec, vmem_spec],
        out_specs=[any_spec] * (2 + n_w),
        out_shape=ex_c.out_shape + ex_w.out_shape + ex_m.out_shape,
        scratch_shapes=ex_c.scratch_shapes + ex_w.scratch_shapes + ex_m.scratch_shapes
        + [pltpu.VMEM((N_DEV, rows, d), F32), pltpu.VMEM((N_DEV * rows, cols), F32)],
        compiler_params=pltpu.CompilerParams(vmem_limit_bytes=VMEM_LIMIT_BYTES),
    )(c_pad, *shards, w, b)
    return outs[0], outs[1 + n_w], list(outs[1:1 + n_w])


def _ada_bwd(c_all, dmod_cols, dmod_all):
    def body(c_ref, dc_ref, da_ref, gw_ref, gb_ref):
        cv = c_ref[...]
        s = cv * _sigmoid(cv)
        gw_ref[...] = lax.dot_general(s, dc_ref[...], TN_DIMS, preferred_element_type=F32,
                                      precision=lax.Precision.HIGHEST)
        gb_ref[...] = jnp.sum(da_ref[...], axis=0, keepdims=True)

    return pl.pallas_call(
        body, name="ada_bwd",
        out_shape=(jax.ShapeDtypeStruct((c_all.shape[1], dmod_cols.shape[1]), F32),
                   jax.ShapeDtypeStruct((1, dmod_all.shape[1]), F32)),
        compiler_params=pltpu.CompilerParams(vmem_limit_bytes=VMEM_LIMIT_BYTES),
    )(c_all, dmod_cols, dmod_all)


def _side_by_side(w_ref):
    return jnp.concatenate([w_ref[s] for s in range(w_ref.shape[0])], axis=1)


def _stacked(w_ref):
    return jnp.concatenate([w_ref[s] for s in range(w_ref.shape[0])], axis=0)


def _loss_tile(x, target, g, acc_ref):
    d = x.shape[1]
    n, r = _norm(x)
    err = n * g - target
    dy = err * (1.0 / d)
    acc_ref[0:1, :] += jnp.sum(err * err, axis=0, keepdims=True)
    acc_ref[1:2, :] += jnp.sum(dy * n, axis=0, keepdims=True)
    dn = dy * g
    return r * (dn - n * jnp.mean(dn * n, axis=-1, keepdims=True))


def _ffn_fwd(x, mod, g, wg, wu, wd, k, tm, exchange=None, head=None):
    t, d = x.shape
    ns, _, fs = wg.shape
    nt = t // tm
    tpb = nt // mod.shape[0]
    rows = tm // FFN_CHUNKS
    extra = list(head) if head is not None else []

    def body(x_ref, mod_ref, g_ref, wg_ref, wu_ref, wd_ref, *rest):
        if head is not None:
            t_ref, gf_ref, xo_ref, f_ref, gg_ref, uu_ref, head_ref, h_sc, acc = rest
        else:
            xo_ref, f_ref, gg_ref, uu_ref, h_sc, acc = rest
        i, j = pl.program_id(0), pl.program_id(1)

        @pl.when(j == 0)
        def _():
            h_sc[...] = _modulate(x_ref[...], g_ref[...], mod_ref, k).astype(BF16)
            acc[...] = jnp.zeros_like(acc)

        chunks = [pl.ds(c * rows, rows) for c in range(FFN_CHUNKS)]
        wg, wu, wd = _side_by_side(wg_ref), _side_by_side(wu_ref), _stacked(wd_ref)
        gates, ups = [], []
        for rs in chunks:
            h = h_sc[rs, :]
            gates.append(_dot(h, wg))
            ups.append(_dot(h, wu))
        acts = [(g * _sigmoid(g) * u).astype(BF16) for g, u in zip(gates, ups)]
        for rs, g, u in zip(chunks, gates, ups):
            for s in range(SHARD_GROUP):
                gg_ref[s, rs, :] = g[:, s * fs:(s + 1) * fs].astype(BF16)
                uu_ref[s, rs, :] = u[:, s * fs:(s + 1) * fs].astype(BF16)
        downs = [_dot(a, wd) for a in acts]
        for rs, dn in zip(chunks, downs):
            acc[rs, :] += dn

        @pl.when(j == ns // SHARD_GROUP - 1)
        def _():
            f = acc[...]
            f_ref[...] = f.astype(BF16)
            xo = x_ref[...] + 0.5 * mod_ref[3 * k + 2:3 * k + 3, :] * f
            if head is None:
                xo_ref[...] = xo
            else:
                @pl.when(i == 0)
                def _():
                    head_ref[...] = jnp.zeros_like(head_ref)

                xo_ref[...] = _loss_tile(xo, t_ref[...], gf_ref[...], head_ref)

    tok = pl.BlockSpec((tm, d), lambda i, j: (i, 0))
    row = pl.BlockSpec((1, d), lambda i, j: (0, 0))
    hid = pl.BlockSpec((SHARD_GROUP, tm, fs), lambda i, j: (j, i, 0))
    head_specs = [_once(tok), row] if head is not None else []
    head_out = [pl.BlockSpec((8, d), lambda i, j: (0, 0))] if head is not None else []
    head_shape = [jax.ShapeDtypeStruct((8, d), F32)] if head is not None else []
    return _call(
        body, name=f"ffn_fwd{k}", grid=(nt, ns // SHARD_GROUP), args=(x, mod, g, wg, wu, wd, *extra),
        in_specs=[tok,
                  pl.BlockSpec((None, N_MOD, d), lambda i, j: (i // tpb, 0, 0)),
                  row,
                  pl.BlockSpec((SHARD_GROUP, d, fs), lambda i, j: (j, 0, 0)),
                  pl.BlockSpec((SHARD_GROUP, d, fs), lambda i, j: (j, 0, 0)),
                  pl.BlockSpec((SHARD_GROUP, fs, d), lambda i, j: (j, 0, 0))] + head_specs,
        out_specs=[tok, tok, hid, hid] + head_out,
        out_shape=[jax.ShapeDtypeStruct((t, d), F32), jax.ShapeDtypeStruct((t, d), BF16),
                   jax.ShapeDtypeStruct((ns, t, fs), BF16), jax.ShapeDtypeStruct((ns, t, fs), BF16)]
        + head_shape,
        scratch_shapes=[pltpu.VMEM((tm, d), BF16), pltpu.VMEM((tm, d), F32)],
        params=_params("arbitrary", "arbitrary"), exchange=exchange)


def _ffn_bwd(dxo, x, f, mod, g, gate, up, wg, wu, wd, k, tm, exchange=None):
    t, d = x.shape
    ns, _, fs = wg.shape
    nt = t // tm
    nb = mod.shape[0]
    tpb = nt // nb
    rows = tm // FFN_CHUNKS

    def body(dxo_ref, x_ref, f_ref, mod_ref, g_ref, gg_ref, uu_ref, wg_ref, wu_ref, wd_ref,
             dx_ref, dgg_ref, duu_ref, act_ref, h_ref, df_ref, dmod_ref, dg_ref, acc):
        i, j = pl.program_id(0), pl.program_id(1)

        @pl.when(j == 0)
        def _():
            df = 0.5 * mod_ref[3 * k + 2:3 * k + 3, :] * dxo_ref[...]
            df_ref[...] = df.astype(BF16)
            h_ref[...] = _modulate(x_ref[...], g_ref[...], mod_ref, k).astype(BF16)
            acc[...] = jnp.zeros_like(acc)

        chunks = [pl.ds(c * rows, rows) for c in range(FFN_CHUNKS)]
        group = range(SHARD_GROUP)
        wg, wu, wd = _side_by_side(wg_ref), _side_by_side(wu_ref), _stacked(wd_ref)
        dacts = [_dot_nt(df_ref[rs, :], wd) for rs in chunks]
        dgates, dups = [], []
        for rs, dact in zip(chunks, dacts):
            gv = jnp.concatenate([gg_ref[s, rs, :] for s in group], axis=1).astype(F32)
            uv = jnp.concatenate([uu_ref[s, rs, :] for s in group], axis=1).astype(F32)
            sig = _sigmoid(gv)
            s_act = gv * sig
            act = (s_act * uv).astype(BF16)
            for s in group:
                act_ref[s, rs, :] = act[:, s * fs:(s + 1) * fs]
            dups.append((dact * s_act).astype(BF16))
            dgates.append((dact * uv * (sig * (1.0 + gv * (1.0 - sig)))).astype(BF16))
        dhs = [_dot_nt(dg, wg) + _dot_nt(du, wu) for dg, du in zip(dgates, dups)]
        for rs, dg, du, dh in zip(chunks, dgates, dups, dhs):
            for s in group:
                dgg_ref[s, rs, :] = dg[:, s * fs:(s + 1) * fs]
                duu_ref[s, rs, :] = du[:, s * fs:(s + 1) * fs]
            acc[rs, :] += dh

        @pl.when(j == ns // SHARD_GROUP - 1)
        def _():
            dx, dshift, dscale, dg = _modulate_bwd(acc[...], x_ref[...], g_ref[...], mod_ref, k)
            dxo_v = dxo_ref[...]
            dx_ref[...] = dxo_v + dx
            dgt = jnp.sum(0.5 * f_ref[...].astype(F32) * dxo_v, axis=0, keepdims=True)

            @pl.when(i % tpb == 0)
            def _():
                dmod_ref[...] = jnp.zeros_like(dmod_ref)

            @pl.when(i == 0)
            def _():
                dg_ref[...] = jnp.zeros_like(dg_ref)

            dmod_ref[0:1, :] += dshift
            dmod_ref[1:2, :] += dscale
            dmod_ref[2:3, :] += dgt
            dg_ref[0:1, :] += dg

    tok = pl.BlockSpec((tm, d), lambda i, j: (i, 0))
    hid = pl.BlockSpec((SHARD_GROUP, tm, fs), lambda i, j: (j, i, 0))
    return _call(
        body, name=f"ffn_bwd{k}", grid=(nt, ns // SHARD_GROUP), args=(dxo, x, f, mod, g, gate, up, wg, wu, wd),
        in_specs=[tok, tok, tok,
                  pl.BlockSpec((None, N_MOD, d), lambda i, j: (i // tpb, 0, 0)),
                  pl.BlockSpec((1, d), lambda i, j: (0, 0)),
                  hid, hid,
                  pl.BlockSpec((SHARD_GROUP, d, fs), lambda i, j: (j, 0, 0)),
                  pl.BlockSpec((SHARD_GROUP, d, fs), lambda i, j: (j, 0, 0)),
                  pl.BlockSpec((SHARD_GROUP, fs, d), lambda i, j: (j, 0, 0))],
        out_specs=[tok, hid, hid, hid, tok, tok,
                   pl.BlockSpec((None, 8, d), lambda i, j: (i // tpb, 0, 0)),
                   pl.BlockSpec((8, d), lambda i, j: (0, 0))],
        out_shape=[jax.ShapeDtypeStruct((t, d), F32),
                   jax.ShapeDtypeStruct((ns, t, fs), BF16), jax.ShapeDtypeStruct((ns, t, fs), BF16),
                   jax.ShapeDtypeStruct((ns, t, fs), BF16),
                   jax.ShapeDtypeStruct((t, d), BF16), jax.ShapeDtypeStruct((t, d), BF16),
                   jax.ShapeDtypeStruct((nb, 8, d), F32), jax.ShapeDtypeStruct((8, d), F32)],
        scratch_shapes=[pltpu.VMEM((tm, d), F32)],
        params=_params("arbitrary", "arbitrary"), exchange=exchange)


def _mm_tn(a, b, a_spec, b_spec, out_shape, n_tiles, name, exchange=None, keep_transposed=False,
           pair_reduce=False):
    n_out = out_shape[0]
    block = tuple(out_shape[1:])
    last = n_tiles - 1
    flip = block[0] > block[1]
    if flip:
        block = block[::-1]
    if flip and keep_transposed:
        flip_back, out_shape = False, (n_out,) + block
    else:
        flip_back = flip
    full_shape = tuple(out_shape)
    n_pairs = n_out // 2
    if pair_reduce:
        out_shape = (n_pairs,) + full_shape[1:]

    def body(a_ref, b_ref, o_ref, acc, *pair):
        i, j = pl.program_id(0), pl.program_id(1)
        prod = _dot_tn(b_ref[...], a_ref[...]) if flip else _dot_tn(a_ref[...], b_ref[...])
        full_ref = pair[0] if pair_reduce else o_ref

        @pl.when(i == 0)
        def _():
            acc[j] = prod

        @pl.when(i > 0)
        def _():
            acc[j] += prod

        @pl.when(i == last)
        def _():
            total = acc[j]
            full_ref[j] = (total.T if flip_back else total).astype(BF16)

        if pair_reduce:
            _, landed, send_sems, recv_sems = pair

            @pl.when(jnp.logical_and(i == last, j == n_out - 1))
            def _():
                x, y, c = lax.axis_index("x"), lax.axis_index("y"), lax.axis_index("c")
                copies = [pltpu.make_async_remote_copy(
                    src_ref=full_ref.at[2 * q + 1 - c], dst_ref=landed.at[q],
                    send_sem=send_sems.at[q], recv_sem=recv_sems.at[q],
                    device_id=(x, y, 1 - c), device_id_type=pl.DeviceIdType.MESH) for q in range(n_pairs)]
                for cp in copies:
                    cp.start()
                for q, cp in enumerate(copies):
                    cp.wait_recv()
                    o_ref[q] = (full_ref[2 * q + c].astype(F32) + landed[q].astype(F32)).astype(BF16)
                for cp in copies:
                    cp.wait_send()

    scratch = [pltpu.VMEM((n_out,) + block, F32)]
    if pair_reduce:
        scratch += [pltpu.VMEM(full_shape, BF16), pltpu.VMEM(out_shape, BF16),
                    pltpu.SemaphoreType.DMA((n_pairs,)), pltpu.SemaphoreType.DMA((n_pairs,))]
    outs, sent = _call(
        body, name=name, grid=(n_tiles, n_out), args=(a, b), in_specs=[a_spec, b_spec],
        out_specs=[pl.BlockSpec(out_shape, lambda i, j: (0,) * len(out_shape))],
        out_shape=[jax.ShapeDtypeStruct(out_shape, BF16)],
        scratch_shapes=scratch,
        params=_params("arbitrary", "arbitrary"), exchange=exchange)
    return (outs[0], sent) if exchange is not None else outs[0]


def _ffn_weight_grads(h, dgate, dup, act, df, tm, tag, stream=False, first=None):
    t, d = h.shape
    ns, _, fs = dgate.shape
    nt = t // tm
    tok = pl.BlockSpec((tm, d), lambda i, j: (i, 0))
    hid = pl.BlockSpec((None, tm, fs), lambda i, j: (j, i, 0))
    if not stream:
        gwg = _mm_tn(h, dgate, tok, hid, (ns, d, fs), nt, f"grad_wg{tag}", keep_transposed=True)
        gwu = _mm_tn(h, dup, tok, hid, (ns, d, fs), nt, f"grad_wu{tag}", keep_transposed=True)
        gwd = _mm_tn(act, df, hid, tok, (ns, fs, d), nt, f"grad_wd{tag}")
        return gwg, gwu, gwd
    gwg, brought = _mm_tn(h, dgate, tok, hid, (ns, d, fs), nt, f"grad_wg{tag}", first,
                          keep_transposed=True, pair_reduce=True)
    gwu, sent_g = _mm_tn(h, dup, tok, hid, (ns, d, fs), nt, f"grad_wu{tag}",
                         _Exchange([gwg], True, chips=[True]), keep_transposed=True, pair_reduce=True)
    gwd, sent_u = _mm_tn(act, df, hid, tok, (ns, fs, d), nt, f"grad_wd{tag}",
                         _Exchange([gwu], True, chips=[True]), pair_reduce=True)
    return sent_g[0], sent_u[0], gwd, brought


def _stage_shape(rows, cols):
    return pltpu.VMEM((cols // LANES, rows, LANES), F32)


def _stage(value, stage_ref):
    for k in range(stage_ref.shape[0]):
        stage_ref[k] = value[:, k * LANES:(k + 1) * LANES]


def _to_residue_rows(stage_ref, dst_ref, dil):
    rows = stage_ref.shape[1] // dil
    for r in range(dil):
        for k in range(stage_ref.shape[0]):
            dst_ref[r, :, k * LANES:(k + 1) * LANES] = (
                stage_ref.at[k][pl.ds(r, rows, stride=dil), :].astype(dst_ref.dtype))


def _from_residue_rows(src_ref, stage_ref, dil):
    rows = stage_ref.shape[1] // dil
    chunks = range(stage_ref.shape[0])
    for r in range(dil):
        for k in chunks:
            stage_ref.at[k][pl.ds(r, rows, stride=dil), :] = src_ref[r, :, k * LANES:(k + 1) * LANES].astype(F32)
    return jnp.concatenate([stage_ref[k] for k in chunks], axis=1)


def _residue_shape(nb, seq, width, dil, dtype):
    return jax.ShapeDtypeStruct((nb, dil, seq // dil, width), dtype)


def _residue_spec(tm, tpb, cols, dil, col_block):
    return pl.BlockSpec((None, dil, tm // dil, cols),
                        lambda i, *rest: (i // tpb, 0, i % tpb, col_block(i, *rest)))


def _qkv_fwd(x, mod, g, win, tm, exchange=None):
    t, d = x.shape
    ns, _, cs = win.shape
    nt = t // tm
    nb = mod.shape[0]
    tpb = nt // nb
    seq = t // nb
    width = ns * cs // 2
    cs, ns = cs * SHARD_GROUP, ns // SHARD_GROUP
    half = ns // 2
    n_res = len(DILATIONS) - 1

    def body(x_ref, mod_ref, g_ref, w_ref, sb_ref, dil_ref, *rest):
        res_refs, h_ref, sc = rest[:n_res], rest[n_res], rest[n_res + 1]
        j = pl.program_id(1)

        @pl.when(j == 0)
        def _():
            h_ref[...] = _modulate(x_ref[...], g_ref[...], mod_ref, 1).astype(BF16)

        res = _dot(h_ref[...], _side_by_side(w_ref))

        @pl.when(j < half)
        def _():
            sb_ref[...] = res.astype(BF16)

        @pl.when(j >= half)
        def _():
            dil_ref[...] = res.astype(BF16)
            _stage(res, sc)
            for ref, dil in zip(res_refs, DILATIONS[1:]):
                _to_residue_rows(sc, ref, dil)

    def dil_col(i, j):
        return jnp.maximum(j - half, 0)

    tok = pl.BlockSpec((tm, d), lambda i, j: (i, 0))
    wide = jax.ShapeDtypeStruct((t, width), BF16)
    outs, got = _call(
        body, name="qkv_fwd", grid=(nt, ns), args=(x, mod, g, win),
        in_specs=[tok,
                  pl.BlockSpec((None, N_MOD, d), lambda i, j: (i // tpb, 0, 0)),
                  pl.BlockSpec((1, d), lambda i, j: (0, 0)),
                  pl.BlockSpec((SHARD_GROUP, d, cs // SHARD_GROUP), lambda i, j: (j, 0, 0))],
        out_specs=[pl.BlockSpec((tm, cs), lambda i, j: (i, jnp.minimum(j, half - 1))),
                   pl.BlockSpec((tm, cs), lambda i, j: (i, dil_col(i, j)))]
        + [_residue_spec(tm, tpb, cs, dil, dil_col) for dil in DILATIONS[1:]] + [tok],
        out_shape=[wide, wide] + [_residue_shape(nb, seq, width, dil, BF16) for dil in DILATIONS[1:]]
        + [jax.ShapeDtypeStruct((t, d), BF16)],
        scratch_shapes=[_stage_shape(tm, cs)],
        params=_params("arbitrary", "arbitrary"), exchange=exchange)
    qkv_dil = [outs[1]] + [a.reshape(t, width) for a in outs[2:2 + n_res]]
    return (outs[0], qkv_dil, outs[-1]), got


def _qkv_bwd(dqkv, dxo, x, mod, g, win, tm, exchange=None):
    t, d = x.shape
    ns, _, cs = win.shape
    nt = t // tm
    nb = mod.shape[0]
    tpb = nt // nb
    cs, ns = cs * SHARD_GROUP, ns // SHARD_GROUP

    def body(dq_ref, dxo_ref, x_ref, mod_ref, g_ref, w_ref, dx_ref, dmod_ref, dg_ref, acc):
        i, j = pl.program_id(0), pl.program_id(1)

        @pl.when(j == 0)
        def _():
            acc[...] = jnp.zeros_like(acc)

        acc[...] += _dot_nt(dq_ref[...], _side_by_side(w_ref))

        @pl.when(j == ns - 1)
        def _():
            dx, dshift, dscale, dg = _modulate_bwd(acc[...], x_ref[...], g_ref[...], mod_ref, 1)
            dx_ref[...] = dxo_ref[...] + dx

            @pl.when(i % tpb == 0)
            def _():
                dmod_ref[...] = jnp.zeros_like(dmod_ref)

            @pl.when(i == 0)
            def _():
                dg_ref[...] = jnp.zeros_like(dg_ref)

            dmod_ref[0:1, :] += dshift
            dmod_ref[1:2, :] += dscale
            dg_ref[0:1, :] += dg

    tok = pl.BlockSpec((tm, d), lambda i, j: (i, 0))
    return _call(
        body, name="qkv_bwd", grid=(nt, ns), args=(dqkv, dxo, x, mod, g, win),
        in_specs=[pl.BlockSpec((tm, cs), lambda i, j: (i, j)), tok, tok,
                  pl.BlockSpec((None, N_MOD, d), lambda i, j: (i // tpb, 0, 0)),
                  pl.BlockSpec((1, d), lambda i, j: (0, 0)),
                  pl.BlockSpec((SHARD_GROUP, d, cs // SHARD_GROUP), lambda i, j: (j, 0, 0))],
        out_specs=[tok,
                   pl.BlockSpec((None, 8, d), lambda i, j: (i // tpb, 0, 0)),
                   pl.BlockSpec((8, d), lambda i, j: (0, 0))],
        out_shape=[jax.ShapeDtypeStruct((t, d), F32),
                   jax.ShapeDtypeStruct((nb, 8, d), F32), jax.ShapeDtypeStruct((8, d), F32)],
        scratch_shapes=[pltpu.VMEM((tm, d), F32)],
        params=_params("arbitrary", "arbitrary"), exchange=exchange)


def _own_lanes():
    lane = lax.broadcasted_iota(jnp.int32, (1, LANES), 1)
    return [lane < HEAD_DIM, lane >= HEAD_DIM]


def _pair_tiles(a):
    return [a[:, (h // 2) * LANES:(h // 2 + 1) * LANES] for h in range(a.shape[1] // HEAD_DIM)]


def _own_tiles(a, own):
    return [jnp.where(own[h % 2], tile, jnp.zeros_like(tile)) for h, tile in enumerate(_pair_tiles(a))]


def _merge_tiles(per_head, own):
    return jnp.concatenate([jnp.where(own[0], per_head[h], per_head[h + 1])
                            for h in range(0, len(per_head), 2)], axis=1)


def _scaled(q):
    return (q.astype(F32) * (HEAD_DIM ** -0.5)).astype(BF16)


def _sb_logits(qh, kh, tri, causal):
    zs = [_dot_nt(q, k) for q, k in zip(qh, kh)]
    es = [jnp.exp(-jnp.abs(z)) for z in zs]
    log_nots = [-(jnp.maximum(z, 0.0) + jnp.log(1.0 + e)) for z, e in zip(zs, es)]
    if causal is not None:
        log_nots = [jnp.where(causal, ln, 0.0) for ln in log_nots]
    return zs, es, [_split_dot(ln, tri) for ln in log_nots]


def _sb_masks():
    rows = lax.broadcasted_iota(jnp.int32, (SB_BLOCK, SB_BLOCK), 0)
    cols = lax.broadcasted_iota(jnp.int32, (SB_BLOCK, SB_BLOCK), 1)
    return (rows >= cols).astype(BF16), (rows <= cols).astype(BF16), cols < rows


def _sb_fwd(qkv, nb, seq, exchange=None):
    t = qkv.shape[0]
    n_pairs = (qkv.shape[1] // 3) // SB_WIDTH
    tb = SB_BLOCK
    n_blk = seq // tb

    def body(q_ref, k_ref, v_ref, o_ref, c_ref):
        tri, _, causal = _sb_masks()
        own = _own_lanes()

        def key_blocks(qh, kjs, carry, mask):
            nh = SB_HEADS
            chains = range(nh * len(kjs))
            kss = [pl.multiple_of(kj * tb, tb) for kj in kjs]
            kh = [tile for ks in kss for tile in _pair_tiles(k_ref[pl.ds(ks, tb), :])]
            vh = [tile for ks in kss for tile in _pair_tiles(v_ref[pl.ds(ks, tb), :])]
            zs, _, suffixes = _sb_logits(qh * len(kjs), kh, tri, mask)
            right = []
            for c in chains:
                right.append(carry[c][1] if c < nh else right[c - nh] + suffixes[c - nh][:, 0:1])
            ws = [jnp.exp(zs[c] + suffixes[c] + right[c]) for c in chains]
            if mask is not None:
                ws = [jnp.where(mask, w, 0.0) for w in ws]
            pv = [_dot(ws[c].astype(BF16), vh[c]) for c in chains]
            last = (len(kjs) - 1) * nh
            return tuple((carry[h][0] + sum(pv[h::nh]), right[last + h] + suffixes[last + h][:, 0:1])
                         for h in range(nh))

        def query_block(qi, _):
            qs = pl.multiple_of(qi * tb, tb)
            qh = _own_tiles(_scaled(q_ref[pl.ds(qs, tb), :]), own)
            zero = (jnp.zeros((tb, LANES), F32), jnp.zeros((tb, 1), F32))
            carry = key_blocks(qh, [qi], (zero,) * SB_HEADS, causal)
            carry = lax.fori_loop(
                0, qi // 2, lambda p, cr: key_blocks(qh, [qi - 1 - 2 * p, qi - 2 - 2 * p], cr, None), carry)
            carry = lax.fori_loop(0, qi % 2, lambda _, cr: key_blocks(qh, [0], cr, None), carry)
            o_ref[pl.ds(qs, tb), :] = _merge_tiles([cr[0] for cr in carry], own)
            c_ref[pl.ds(qs, tb), :] = _merge_tiles([jnp.broadcast_to(cr[1], (tb, LANES)) for cr in carry], own)
            return 0

        lax.fori_loop(0, n_blk, query_block, 0)

    def spec(offset):
        return pl.BlockSpec((seq, SB_WIDTH), lambda b, p: (b, offset + p))

    out = jax.ShapeDtypeStruct((t, n_pairs * SB_WIDTH), F32)
    return _call(
        body, name="sb_fwd", grid=(nb, n_pairs), args=(qkv, qkv, qkv),
        in_specs=[spec(0), spec(n_pairs), spec(2 * n_pairs)],
        out_specs=[spec(0), spec(0)], out_shape=[out, out],
        params=_params("arbitrary", "arbitrary"), exchange=exchange)


def _sb_bwd(qkv, do, csum, nb, seq, exchange=None):
    t = qkv.shape[0]
    n_pairs = (qkv.shape[1] // 3) // SB_WIDTH
    tb = SB_BLOCK
    n_blk = seq // tb
    scale = HEAD_DIM ** -0.5

    def body(q_ref, k_ref, v_ref, do_ref, c_ref, dq_ref, dk_ref, dv_ref, dkt_acc, dvt_acc):
        tri, tri_prefix, causal = _sb_masks()
        own = _own_lanes()
        dkt_acc[...] = jnp.zeros_like(dkt_acc)
        dvt_acc[...] = jnp.zeros_like(dvt_acc)

        def key_blocks(qh, qth, doh, doth, ch, kjs, carry, mask):
            nh = SB_HEADS
            chains = range(nh * len(kjs))
            kss = [pl.multiple_of(kj * tb, tb) for kj in kjs]
            kh = [tile for ks in kss for tile in _pair_tiles(k_ref[pl.ds(ks, tb), :])]
            vh = [tile for ks in kss for tile in _pair_tiles(v_ref[pl.ds(ks, tb), :])]
            zs, es, suffixes = _sb_logits(qh * len(kjs), kh, tri, mask)
            dws = [_dot_nt(doh[c % nh], vh[c]) for c in chains]
            lefts = []
            for c in chains:
                before = carry[c][1] if c < nh else lefts[c - nh]
                lefts.append(before + suffixes[c][:, 0:1])
            ws = [jnp.exp(zs[c] + suffixes[c] + (ch[c % nh] - lefts[c])) for c in chains]
            if mask is not None:
                ws = [jnp.where(mask, w, 0.0) for w in ws]
            dlws = [ws[c] * dws[c] for c in chains]
            dprefixes = [_split_dot(dlw, tri_prefix) for dlw in dlws]
            dvts = [_dot(doth[c % nh], ws[c].astype(BF16)) for c in chains]
            dlefts, dzbs = [], []
            for c in chains:
                dlefts.append(carry[c][2] if c < nh else dlefts[c - nh] + dprefixes[c - nh][:, tb - 1:tb])
                sig = jnp.where(zs[c] >= 0.0, 1.0, es[c]) * pl.reciprocal(1.0 + es[c], approx=True)
                dz = dlws[c] - sig * (dlefts[c] + dprefixes[c])
                if mask is not None:
                    dz = jnp.where(mask, dz, 0.0)
                dzbs.append(dz.astype(BF16))
            dkts = [_dot(qth[c % nh], dzbs[c]) for c in chains]
            dqs = [_dot(dzbs[c], kh[c]) for c in chains]
            for b, ks in enumerate(kss):
                pairs = range(b * nh, (b + 1) * nh, 2)
                dkt_acc[:, pl.ds(ks, tb)] += jnp.concatenate([dkts[c] + dkts[c + 1] for c in pairs], axis=0)
                dvt_acc[:, pl.ds(ks, tb)] += jnp.concatenate([dvts[c] + dvts[c + 1] for c in pairs], axis=0)
            last = (len(kjs) - 1) * nh
            return tuple((carry[h][0] + sum(dqs[h::nh]), lefts[last + h],
                          dlefts[last + h] + dprefixes[last + h][:, tb - 1:tb]) for h in range(nh))

        def query_block(qi, _):
            qs = pl.multiple_of(qi * tb, tb)
            qh = _own_tiles(_scaled(q_ref[pl.ds(qs, tb), :]), own)
            doh = _own_tiles(do_ref[pl.ds(qs, tb), :], own)
            qth = [a.astype(F32).T.astype(BF16) for a in qh]
            doth = [a.T.astype(BF16) for a in doh]
            doh = [a.astype(BF16) for a in doh]
            cv = c_ref[pl.ds(qs, tb), :]
            ch = [cv[:, h * HEAD_DIM:h * HEAD_DIM + 1] for h in range(SB_HEADS)]
            zero = (jnp.zeros((tb, LANES), F32), jnp.zeros((tb, 1), F32), jnp.zeros((tb, 1), F32))

            def key_block(kjs, cr, mask):
                return key_blocks(qh, qth, doh, doth, ch, kjs, cr, mask)

            carry = lax.fori_loop(0, qi // 2, lambda p, cr: key_block([2 * p, 2 * p + 1], cr, None),
                                  (zero,) * SB_HEADS)
            carry = lax.fori_loop(0, qi % 2, lambda _, cr: key_block([qi - 1], cr, None), carry)
            carry = key_block([qi], carry, causal)
            dq = _merge_tiles([cr[0] for cr in carry], own) * scale
            dq_ref[pl.ds(qs, tb), :] = dq.astype(BF16)
            return 0

        lax.fori_loop(0, n_blk, query_block, 0)
        dk_ref[...] = dkt_acc[...].T.astype(BF16)
        dv_ref[...] = dvt_acc[...].T.astype(BF16)

    def spec(offset):
        return pl.BlockSpec((seq, SB_WIDTH), lambda b, p: (b, offset + p))

    out = jax.ShapeDtypeStruct((t, n_pairs * SB_WIDTH), BF16)
    return _call(
        body, name="sb_bwd", grid=(nb, n_pairs), args=(qkv, qkv, qkv, do, csum),
        in_specs=[spec(0), spec(n_pairs), spec(2 * n_pairs), spec(0), spec(0)],
        out_specs=[spec(0), spec(0), spec(0)],
        out_shape=[out, out, out],
        scratch_shapes=[pltpu.VMEM((SB_WIDTH, seq), F32), pltpu.VMEM((SB_WIDTH, seq), F32)],
        params=_params("arbitrary", "arbitrary"), exchange=exchange)


def _dil_block_scores(qh, kph, kch, bias_ref, has_prev, band_prev, band_cur):
    scale = HEAD_DIM ** -0.5
    heads = range(len(qh))
    no_prev = jnp.where(has_prev, 0.0, NEG_INF)
    zps = [_dot_nt(qh[h], kph[h]) for h in heads]
    zcs = [_dot_nt(qh[h], kch[h]) for h in heads]
    zps = [jnp.where(band_prev, zps[h] * scale + bias_ref[h, :, 0:DIL_BLOCK], NEG_INF) + no_prev for h in heads]
    zcs = [jnp.where(band_cur, zcs[h] * scale + bias_ref[h, :, DIL_BLOCK:2 * DIL_BLOCK], NEG_INF) for h in heads]
    return zps, zcs


def _dil_bands():
    rows = lax.broadcasted_iota(jnp.int32, (DIL_BLOCK, DIL_BLOCK), 0)
    cols = lax.broadcasted_iota(jnp.int32, (DIL_BLOCK, DIL_BLOCK), 1)
    return cols >= rows, cols <= rows


def _dil_fwd(qkv, bias, nb, seq, dil, exchange=None):
    t, width = qkv.shape
    n_pairs = (width // 3) // DIL_WIDTH
    bq = DIL_BLOCK
    n_blk = seq // bq
    per_seq = n_blk // dil
    heads = range(DIL_HEADS)

    def body(q_ref, k_ref, v_ref, bias_ref, o_ref, lse_ref):
        band_prev, band_cur = _dil_bands()
        own = _own_lanes()

        def block(n, _):
            has_prev = (n & (per_seq - 1)) != 0
            qs = pl.multiple_of(n * bq, bq)
            ps = pl.multiple_of(jnp.maximum(n - 1, 0) * bq, bq)
            qh = _own_tiles(q_ref[pl.ds(qs, bq), :], own)
            kp, kc = _pair_tiles(k_ref[pl.ds(ps, bq), :]), _pair_tiles(k_ref[pl.ds(qs, bq), :])
            vp, vc = _pair_tiles(v_ref[pl.ds(ps, bq), :]), _pair_tiles(v_ref[pl.ds(qs, bq), :])
            zps, zcs = _dil_block_scores(qh, kp, kc, bias_ref, has_prev, band_prev, band_cur)
            ms = [jnp.maximum(jnp.max(zps[h], axis=1, keepdims=True), jnp.max(zcs[h], axis=1, keepdims=True))
                  for h in heads]
            eps = [jnp.exp(zps[h] - ms[h]) for h in heads]
            ecs = [jnp.exp(zcs[h] - ms[h]) for h in heads]
            pvs = [_dot(eps[h].astype(BF16), vp[h]) + _dot(ecs[h].astype(BF16), vc[h]) for h in heads]
            dens = [jnp.sum(eps[h], axis=1, keepdims=True) + jnp.sum(ecs[h], axis=1, keepdims=True) for h in heads]
            o_ref[pl.ds(qs, bq), :] = _merge_tiles([pvs[h] / dens[h] for h in heads], own)
            lse_ref[pl.ds(qs, bq), :] = _merge_tiles(
                [jnp.broadcast_to(ms[h] + jnp.log(dens[h]), (bq, LANES)) for h in heads], own)
            return 0

        lax.fori_loop(0, n_blk, block, 0, unroll=4)

    def spec(offset):
        return pl.BlockSpec((seq, DIL_WIDTH), lambda b, p: (b, offset + p))

    out = jax.ShapeDtypeStruct((t, n_pairs * DIL_WIDTH), F32)
    return _call(
        body, name=f"dil_fwd{dil}", grid=(nb, n_pairs), args=(qkv, qkv, qkv, bias),
        in_specs=[spec(0), spec(n_pairs), spec(2 * n_pairs),
                  pl.BlockSpec((DIL_HEADS, bq, 2 * bq), lambda b, p: (p, 0, 0))],
        out_specs=[spec(0), spec(0)], out_shape=[out, out],
        params=_params("arbitrary", "arbitrary"), exchange=exchange)


def _dil_bwd(qkv, bias, do, lse, delta, nb, seq, dil):
    t, width = qkv.shape
    n_pairs = (width // 3) // DIL_WIDTH
    bq = DIL_BLOCK
    n_blk = seq // bq
    per_seq = n_blk // dil
    scale = HEAD_DIM ** -0.5
    heads = range(DIL_HEADS)

    def body(q_ref, k_ref, v_ref, bias_ref, do_ref, lse_ref, dl_ref, dq_ref, dk_ref, dv_ref, db_ref,
             dk_acc, dv_acc):
        band_prev, band_cur = _dil_bands()
        own = _own_lanes()
        dk_acc[...] = jnp.zeros_like(dk_acc)
        dv_acc[...] = jnp.zeros_like(dv_acc)

        @pl.when(pl.program_id(1) == 0)
        def _():
            db_ref[...] = jnp.zeros_like(db_ref)

        def block(n, _):
            has_prev = (n & (per_seq - 1)) != 0
            qs = pl.multiple_of(n * bq, bq)
            ps = pl.multiple_of(jnp.maximum(n - 1, 0) * bq, bq)
            qh = _own_tiles(q_ref[pl.ds(qs, bq), :], own)
            kp, kc = _pair_tiles(k_ref[pl.ds(ps, bq), :]), _pair_tiles(k_ref[pl.ds(qs, bq), :])
            vp, vc = _pair_tiles(v_ref[pl.ds(ps, bq), :]), _pair_tiles(v_ref[pl.ds(qs, bq), :])
            doh = _own_tiles(do_ref[pl.ds(qs, bq), :].astype(BF16), own)
            lse_v, dl_v = lse_ref[pl.ds(qs, bq), :], dl_ref[pl.ds(qs, bq), :]
            zps, zcs = _dil_block_scores(qh, kp, kc, bias_ref, has_prev, band_prev, band_cur)
            dpp = [_dot_nt(doh[h], vp[h]) for h in heads]
            dpc = [_dot_nt(doh[h], vc[h]) for h in heads]
            lse_h = [lse_v[:, h * HEAD_DIM:h * HEAD_DIM + 1] for h in heads]
            dl_h = [dl_v[:, h * HEAD_DIM:h * HEAD_DIM + 1] for h in heads]
            pps = [jnp.exp(zps[h] - lse_h[h]) for h in heads]
            pcs = [jnp.exp(zcs[h] - lse_h[h]) for h in heads]
            dvp = [_dot_tn(pps[h].astype(BF16), doh[h]) for h in heads]
            dvc = [_dot_tn(pcs[h].astype(BF16), doh[h]) for h in heads]
            dzps = [pps[h] * (dpp[h] - dl_h[h]) for h in heads]
            dzcs = [pcs[h] * (dpc[h] - dl_h[h]) for h in heads]
            dzp_b = [(dzps[h] * scale).astype(BF16) for h in heads]
            dzc_b = [(dzcs[h] * scale).astype(BF16) for h in heads]
            dqs = [_dot(dzp_b[h], kp[h]) + _dot(dzc_b[h], kc[h]) for h in heads]
            dkp = [_dot_tn(dzp_b[h], qh[h]) for h in heads]
            dkc = [_dot_tn(dzc_b[h], qh[h]) for h in heads]
            for h in heads:
                db_ref[h, :, 0:bq] += dzps[h]
                db_ref[h, :, bq:2 * bq] += dzcs[h]
            def pair_sums(per_head):
                return jnp.concatenate([per_head[h] + per_head[h + 1] for h in heads[::2]], axis=1)

            dq_ref[pl.ds(qs, bq), :] = _merge_tiles(dqs, own).astype(BF16)
            dk_acc[pl.ds(ps, bq), :] += pair_sums(dkp)
            dk_acc[pl.ds(qs, bq), :] += pair_sums(dkc)
            dv_acc[pl.ds(ps, bq), :] += pair_sums(dvp)
            dv_acc[pl.ds(qs, bq), :] += pair_sums(dvc)
            return 0

        lax.fori_loop(0, n_blk, block, 0, unroll=4)
        dk_ref[...] = dk_acc[...].astype(BF16)
        dv_ref[...] = dv_acc[...].astype(BF16)

    def spec(offset):
        return pl.BlockSpec((seq, DIL_WIDTH), lambda p, b: (b, offset + p))

    bias_spec = pl.BlockSpec((DIL_HEADS, bq, 2 * bq), lambda p, b: (p, 0, 0))
    out = jax.ShapeDtypeStruct((t, n_pairs * DIL_WIDTH), BF16)
    return pl.pallas_call(
        body, name=f"dil_bwd{dil}", grid=(n_pairs, nb),
        in_specs=[spec(0), spec(n_pairs), spec(2 * n_pairs), bias_spec, spec(0), spec(0), spec(0)],
        out_specs=[spec(0), spec(0), spec(0), bias_spec],
        out_shape=[out, out, out, jax.ShapeDtypeStruct(bias.shape, F32)],
        scratch_shapes=[pltpu.VMEM((seq, DIL_WIDTH), F32), pltpu.VMEM((seq, DIL_WIDTH), F32)],
        compiler_params=_params("arbitrary", "arbitrary"),
    )(qkv, qkv, qkv, bias, do, lse, delta)


def _head_blocks(width):
    rows = lax.broadcasted_iota(jnp.int32, (width, width), 0) // HEAD_DIM
    cols = lax.broadcasted_iota(jnp.int32, (width, width), 1) // HEAD_DIM
    return (rows == cols).astype(BF16)


def _head_mean(v, gmat):
    return _split_dot(v, gmat) * (1.0 / HEAD_DIM)


def _residue_views(arrays, nb, seq):
    return [a if dil == 1 else a.reshape(nb, dil, seq // dil, a.shape[1]) for a, dil in zip(arrays, DILATIONS)]


def _mix_out_fwd(osb, ocs, lses, gsb, gdil, wout, x, mod, tm):
    t, d = x.shape
    ds = osb.shape[1]
    nt = t // tm
    nb = mod.shape[0]
    tpb = nt // nb
    seq = t // nb
    n_cfg = len(DILATIONS)

    def body(osb_ref, *refs):
        oc_refs, lse_refs = refs[:n_cfg], refs[n_cfg:2 * n_cfg]
        gsb_ref, gdil_ref, w_ref, x_ref, mod_ref = refs[2 * n_cfg:2 * n_cfg + 5]
        xo_ref, on_ref, m_ref, odil_ref = refs[2 * n_cfg + 5:2 * n_cfg + 9]
        ld_refs = refs[2 * n_cfg + 9:3 * n_cfg + 9]
        stages, sc = refs[3 * n_cfg + 9:]
        ocv, lsev = [oc_refs[0][...]], [lse_refs[0][...]]
        for i, dil in enumerate(DILATIONS[1:]):
            ocv.append(_from_residue_rows(oc_refs[i + 1], stages.at[2 * i], dil))
            lsev.append(_from_residue_rows(lse_refs[i + 1], stages.at[2 * i + 1], dil))
        top = functools.reduce(jnp.maximum, lsev)
        total = top + jnp.log(sum(jnp.exp(l - top) for l in lsev))
        odil = sum(jnp.exp(l - total) * o for o, l in zip(ocv, lsev))
        odil_ref[...] = odil
        ld_refs[0][...] = total
        _stage(total, sc)
        for ref, dil in zip(ld_refs[1:], DILATIONS[1:]):
            _to_residue_rows(sc, ref, dil)
        gm = _head_blocks(ds)
        parts = []
        for o, g_ref in ((osb_ref[...], gsb_ref), (odil, gdil_ref)):
            parts.append(o * lax.rsqrt(_head_mean(o * o, gm) + EPS) * g_ref[...])
        on = jnp.concatenate(parts, axis=1).astype(BF16)
        on_ref[...] = on
        m = _dot(on, w_ref[...])
        m_ref[...] = m
        xo_ref[...] = x_ref[...] + mod_ref[5:6, :] * m

    tok = pl.BlockSpec((tm, d), lambda i: (i, 0))
    hd = pl.BlockSpec((tm, ds), lambda i: (i, 0))
    res = [hd] + [_residue_spec(tm, tpb, ds, dil, lambda i: 0) for dil in DILATIONS[1:]]
    res_shape = [jax.ShapeDtypeStruct((t, ds), F32)] + [_residue_shape(nb, seq, ds, dil, F32) for dil in DILATIONS[1:]]
    gain = pl.BlockSpec((1, ds), lambda i: (0, 0))
    outs = pl.pallas_call(
        body, name="mix_out_fwd", grid=(nt,),
        in_specs=[hd] + res + res + [gain, gain,
                  pl.BlockSpec(wout.shape, lambda i: (0, 0)),
                  tok, pl.BlockSpec((None, N_MOD, d), lambda i: (i // tpb, 0, 0))],
        out_specs=[tok, pl.BlockSpec((tm, 2 * ds), lambda i: (i, 0)), tok, hd] + res,
        out_shape=[jax.ShapeDtypeStruct((t, d), F32), jax.ShapeDtypeStruct((t, 2 * ds), BF16),
                   jax.ShapeDtypeStruct((t, d), F32), jax.ShapeDtypeStruct((t, ds), F32)] + res_shape,
        scratch_shapes=[pltpu.VMEM((2 * (n_cfg - 1), ds // LANES, tm, LANES), F32), _stage_shape(tm, ds)],
        compiler_params=_params("arbitrary"),
    )(osb, *_residue_views(ocs, nb, seq), *_residue_views(lses, nb, seq), gsb, gdil, wout, x, mod)
    return outs[0], outs[1], outs[2], outs[3], [a.reshape(t, ds) for a in outs[4:]]


def _mix_out_bwd(dxo, m, mod, wout, osb, odil, gsb, gdil, tm):
    t, d = dxo.shape
    ds = osb.shape[1]
    nt = t // tm
    nb = mod.shape[0]
    tpb = nt // nb
    seq = t // nb
    n_cfg = len(DILATIONS)

    def body(dxo_ref, m_ref, mod_ref, w_ref, osb_ref, odil_ref, gsb_ref, gdil_ref,
             dm_ref, dosb_ref, *rest):
        do_refs, dl_refs = rest[:n_cfg], rest[n_cfg:2 * n_cfg]
        dmod_ref, dg_ref, sc = rest[2 * n_cfg:]
        dodil_ref, dldil_ref = do_refs[0], dl_refs[0]
        i = pl.program_id(0)
        dxo_v = dxo_ref[...]
        dm = (mod_ref[5:6, :] * dxo_v).astype(BF16)
        dm_ref[...] = dm
        dgt = jnp.sum(m_ref[...] * dxo_v, axis=0, keepdims=True)
        don = _dot_nt(dm, w_ref[...])
        gm = _head_blocks(ds)

        @pl.when(i % tpb == 0)
        def _():
            dmod_ref[...] = jnp.zeros_like(dmod_ref)

        @pl.when(i == 0)
        def _():
            dg_ref[...] = jnp.zeros_like(dg_ref)

        dmod_ref[2:3, :] += dgt
        groups = ((osb_ref, gsb_ref, dosb_ref), (odil_ref, gdil_ref, dodil_ref))
        for k, (o_ref, g_ref, do_ref) in enumerate(groups):
            o = o_ref[...]
            dn_out = don[:, k * ds:(k + 1) * ds]
            r = lax.rsqrt(_head_mean(o * o, gm) + EPS)
            n = o * r
            dg_ref[0:1, k * ds:(k + 1) * ds] += jnp.sum(dn_out * n, axis=0, keepdims=True)
            dn = dn_out * g_ref[...]
            do = r * (dn - n * _head_mean(dn * n, gm))
            do_ref[...] = do
            if k == 1:
                delta = _head_mean(do * o, gm) * float(HEAD_DIM)
                dldil_ref[...] = delta
                for value, refs in ((do, do_refs), (delta, dl_refs)):
                    _stage(value, sc)
                    for ref, dil in zip(refs[1:], DILATIONS[1:]):
                        _to_residue_rows(sc, ref, dil)

    tok = pl.BlockSpec((tm, d), lambda i: (i, 0))
    hd = pl.BlockSpec((tm, ds), lambda i: (i, 0))
    res = [hd] + [_residue_spec(tm, tpb, ds, dil, lambda i: 0) for dil in DILATIONS[1:]]
    res_shape = [jax.ShapeDtypeStruct((t, ds), F32)] + [_residue_shape(nb, seq, ds, dil, F32) for dil in DILATIONS[1:]]
    gain = pl.BlockSpec((1, ds), lambda i: (0, 0))
    outs = pl.pallas_call(
        body, name="mix_out_bwd", grid=(nt,),
        in_specs=[tok, tok, pl.BlockSpec((None, N_MOD, d), lambda i: (i // tpb, 0, 0)),
                  pl.BlockSpec(wout.shape, lambda i: (0, 0)), hd, hd, gain, gain],
        out_specs=[tok, hd] + res + res
        + [pl.BlockSpec((None, 8, d), lambda i: (i // tpb, 0, 0)), pl.BlockSpec((8, 2 * ds), lambda i: (0, 0))],
        out_shape=[jax.ShapeDtypeStruct((t, d), BF16), jax.ShapeDtypeStruct((t, ds), F32)] + res_shape + res_shape
        + [jax.ShapeDtypeStruct((nb, 8, d), F32), jax.ShapeDtypeStruct((8, 2 * ds), F32)],
        scratch_shapes=[_stage_shape(tm, ds)],
        compiler_params=_params("arbitrary"),
    )(dxo, m, mod, wout, osb, odil, gsb, gdil)
    flat = [a.reshape(t, ds) for a in outs[2:2 + 2 * n_cfg]]
    return outs[0], outs[1], flat[:n_cfg], flat[n_cfg:], outs[-2], outs[-1]


def _merge_dqkv(sb_parts, dil_parts, nb, tm):
    t, ds = sb_parts[0].shape
    nt = t // tm
    tpb = nt // nb
    seq = t // nb
    n_cfg = len(DILATIONS)

    def body(*refs):
        sb_refs, dil_refs = refs[:3], refs[3:3 + 3 * n_cfg]
        o_ref, sc = refs[3 + 3 * n_cfg:]
        for k in range(3):
            o_ref[:, k * ds:(k + 1) * ds] = sb_refs[k][...]
            total = dil_refs[k * n_cfg][...].astype(F32)
            for i, dil in enumerate(DILATIONS[1:]):
                total = total + _from_residue_rows(dil_refs[k * n_cfg + i + 1], sc, dil)
            o_ref[:, (3 + k) * ds:(4 + k) * ds] = total.astype(BF16)

    hd = pl.BlockSpec((tm, ds), lambda i: (i, 0))
    res = [hd] + [_residue_spec(tm, tpb, ds, dil, lambda i: 0) for dil in DILATIONS[1:]]
    views = [v for parts in dil_parts for v in _residue_views(parts, nb, seq)]
    return pl.pallas_call(
        body, name="merge_dqkv", grid=(nt,),
        in_specs=[hd] * 3 + res * 3,
        out_specs=pl.BlockSpec((tm, 6 * ds), lambda i: (i, 0)),
        out_shape=jax.ShapeDtypeStruct((t, 6 * ds), BF16),
        scratch_shapes=[_stage_shape(tm, ds)],
        compiler_params=_params("arbitrary"),
    )(*sb_parts, *views)


def _row_tile(rows):
    if rows <= 256:
        return rows
    for cand in range(256, 15, -16):
        if rows % cand == 0:
            return cand
    return rows


def _adamw(w, parts, m, v, name, transposed=False):
    rows, cols = w.shape
    n_parts = parts.shape[0]
    tr = _row_tile(rows)
    c1 = 1.0 / (1.0 - ADAM_B1 ** ADAM_STEP)
    c2 = 1.0 / (1.0 - ADAM_B2 ** ADAM_STEP)

    def body(w_ref, p_ref, m_ref, v_ref, g_ref, d_ref, nm_ref, nv_ref):
        g = p_ref[0].astype(F32)
        for i in range(1, n_parts):
            g = g + p_ref[i].astype(F32)
        wv, mv, vv = w_ref[...], m_ref[...], v_ref[...]
        if transposed:
            wv, mv, vv = wv.T, mv.T, vv.T
        nm = ADAM_B1 * mv + (1.0 - ADAM_B1) * g
        nv = ADAM_B2 * vv + (1.0 - ADAM_B2) * (g * g)
        g_ref[...] = g
        nm_ref[...] = nm
        nv_ref[...] = nv
        d_ref[...] = -ADAM_LR * ((nm * c1) / (jnp.sqrt(nv * c2) + ADAM_EPS) + ADAM_WD * wv)

    blk = pl.BlockSpec((tr, cols), lambda i: (i, 0))
    if transposed:
        oblk = pl.BlockSpec((cols, tr), lambda i: (0, i))
        pblk = pl.BlockSpec((n_parts, cols, tr), lambda i: (0, 0, i))
        out = jax.ShapeDtypeStruct((cols, rows), F32)
    else:
        oblk, pblk = blk, pl.BlockSpec((n_parts, tr, cols), lambda i: (0, i, 0))
        out = jax.ShapeDtypeStruct((rows, cols), F32)
    return pl.pallas_call(
        body, name=name, grid=(rows // tr,),
        in_specs=[blk, pblk, blk, blk],
        out_specs=[oblk, oblk, oblk, oblk], out_shape=[out, out, out, out],
        compiler_params=_params("arbitrary"),
    )(w, parts, m, v)


def _t5_bucket(n):
    max_exact = N_BUCKETS // 2
    nf = np.maximum(n, 1).astype(np.float32)
    large = max_exact + (np.log(nf / max_exact) / math.log(MAX_DISTANCE / max_exact)
                         * (N_BUCKETS - max_exact)).astype(np.int32)
    large = np.minimum(large, N_BUCKETS - 1)
    return np.where(n < max_exact, n, large).astype(np.int32)


def _bucket_onehot():
    table = np.zeros((len(DILATIONS), 2 * DIL_BLOCK + 1, N_BUCKETS), np.float32)
    for i, dil in enumerate(DILATIONS):
        buckets = _t5_bucket(np.arange(DIL_BLOCK + 1) * dil)
        for m in range(DIL_BLOCK + 1):
            table[i, m, buckets[DIL_BLOCK - m]] = 1.0
    return table


def _bias_blocks(rel_bias):
    row = jnp.einsum("cmn,nh->chm", _bucket_onehot(), rel_bias, precision=lax.Precision.HIGHEST)
    n_cfg, n_heads, width = row.shape
    tiled = jnp.tile(row, (1, 1, DIL_BLOCK))[..., :DIL_BLOCK * (width - 1)]
    return tiled.reshape(n_cfg, n_heads, DIL_BLOCK, width - 1)


def _bias_blocks_bwd(dblocks):
    n_cfg, n_heads = dblocks.shape[:2]
    width = 2 * DIL_BLOCK + 1
    flat = dblocks.reshape(n_cfg, n_heads, DIL_BLOCK * (width - 1))
    flat = jnp.pad(flat, ((0, 0), (0, 0), (0, DIL_BLOCK)))
    drow = jnp.sum(flat.reshape(n_cfg, n_heads, DIL_BLOCK, width), axis=2)
    return jnp.einsum("chm,cmn->nh", drow, _bucket_onehot(), precision=lax.Precision.HIGHEST)


def _pad_to(a, axis, size):
    pad = [(0, 0)] * a.ndim
    pad[axis] = (0, size - a.shape[axis])
    return jnp.pad(a, pad)


def _lane_pad(n):
    return -(-n // LANES) * LANES


def _local_step(x, target, mod, gains, weights, rel_bias, tm, distributed):
    nb, seq, d = x.shape
    t = nb * seq
    g_ffn1, g_mix, g_sb, g_dil, g_ffn2, g_final = gains
    wg1, wu1, wd1 = weights[:3]
    x0 = x.reshape(t, d)
    ds = g_sb.shape[1]
    bias = _bias_blocks(rel_bias)

    def beside(arrays, scatter):
        return _Exchange(arrays, scatter) if distributed else None

    tp, tg = min(PROJ_TILE, seq), min(GRAD_TILE, t)

    (x1, f1, gate1, up1), got = _ffn_fwd(x0, mod, g_ffn1, wg1, wu1, wd1, 0, tp, beside(weights[3:4], False))
    win = got[0] if distributed else weights[3]
    (qkv, qkvd, h2), got = _qkv_fwd(x1, mod, g_mix, win, tp, beside(weights[4:5], False))
    wout = got[0] if distributed else weights[4]
    wout2 = wout.reshape(-1, d)
    (osb, csb), got = _sb_fwd(qkv, nb, seq, beside(weights[5:7], False))
    wg2, wu2 = got if distributed else weights[5:7]
    n_cfg = len(DILATIONS)
    piece = -(-weights[7].shape[-2] // n_cfg // 16) * 16
    ocs, lses, wd2_pieces = [], [], []
    for i, dil in enumerate(DILATIONS):
        rows = weights[7][..., i * piece:(i + 1) * piece, :]
        (oc, lse), got = _dil_fwd(qkvd[i], bias[i], nb, seq, dil, beside([rows], False))
        wd2_pieces.append(got[0] if distributed else rows)
        ocs.append(oc)
        lses.append(lse)
    wd2 = jnp.concatenate(wd2_pieces, axis=-2)
    x2, on, mix, odil, ldil = _mix_out_fwd(osb, ocs, lses, g_sb, g_dil, wout2, x1, mod, tm)
    (dx3, f3, gate3, up3, head), _ = _ffn_fwd(x2, mod, g_ffn2, wg2, wu2, wd2, 2, tp,
                                              head=(target.reshape(t, d), g_final))
    loss_sum = 0.5 * jnp.sum(head[0]) / d
    dg_final = head[1:2]

    (dx2, dgate3, dup3, act3, h3, df3, dmod3, dg_ffn2), _ = _ffn_bwd(
        dx3, x2, f3, mod, g_ffn2, gate3, up3, wg2, wu2, wd2, 2, tm)
    gwg2, gwu2, gwd2 = _ffn_weight_grads(h3, dgate3, dup3, act3, df3, tg, 2)

    dm, dosb, dodil, dldil, dmod2b, dg_heads = _mix_out_bwd(
        dx2, mix, mod, wout2, osb, odil, g_sb, g_dil, tm)
    gwout = _mm_tn(on, dm,
                   pl.BlockSpec((tg, wout.shape[1]), lambda i, j: (i, j)),
                   pl.BlockSpec((tg, d), lambda i, j: (i, 0)),
                   wout.shape, t // tg, "grad_wout")

    (dq_sb, dk_sb, dv_sb), parts_late = _sb_bwd(qkv, dosb, csb, nb, seq,
                                                beside([gwout, gwg2, gwu2, gwd2], True))
    dil_grads = [_dil_bwd(qkvd[i], bias[i], dodil[i], ldil[i], dldil[i], nb, seq, dil)
                 for i, dil in enumerate(DILATIONS)]
    dqkv = _merge_dqkv([dq_sb, dk_sb, dv_sb], [[g[k] for g in dil_grads] for k in range(3)], nb, tm)
    drel = _bias_blocks_bwd(jnp.stack([g[3] for g in dil_grads]))

    cs = win.shape[2]
    gwin = _mm_tn(h2, dqkv,
                  pl.BlockSpec((tg, d), lambda i, j: (i, 0)),
                  pl.BlockSpec((tg, cs), lambda i, j: (i, j)),
                  win.shape, t // tg, "grad_win", pair_reduce=distributed)
    (dx1, dmod2a, dg_mix), parts_mid = _qkv_bwd(
        dqkv, dx2, x1, mod, g_mix, win, tp, _Exchange([gwin], True, chips=[True]) if distributed else None)

    (dx0, dgate1, dup1, act1, h1, df1, dmod1, dg_ffn1), _ = _ffn_bwd(
        dx1, x0, f1, mod, g_ffn1, gate1, up1, wg1, wu1, wd1, 0, tm)
    dmod = jnp.concatenate([dmod1[:, 0:3], dmod2a[:, 0:2], dmod2b[:, 2:3], dmod3[:, 0:3]], axis=1)
    ggrads = (dg_ffn1[0:1], dg_mix[0:1], dg_heads[0:1], drel, dg_ffn2[0:1], dg_final)
    if not distributed:
        gw1 = _ffn_weight_grads(h1, dgate1, dup1, act1, df1, tg, 0)
        return loss_sum, dx0.reshape(nb, seq, d), tuple(gw1) + (gwin, gwout, gwg2, gwu2, gwd2), dmod, ggrads

    dg_heads_row, drel_flat = dg_heads[0:1], drel.reshape(1, -1)
    width = max(d, dg_heads_row.shape[1], drel_flat.shape[1])
    small = jnp.concatenate(
        [_pad_to(a.reshape(1, -1), 1, width)
         for a in (dg_ffn1[0:1], dg_mix[0:1], dg_ffn2[0:1], dg_final, dg_heads_row, drel_flat, loss_sum)]
        + [jnp.zeros((1, width), F32)], axis=0)
    dmod_pad = _pad_to(dmod.reshape(nb, N_MOD * d), 0, 8)
    everyone = _Exchange([jnp.broadcast_to(dmod_pad, (N_DEV,) + dmod_pad.shape),
                          jnp.broadcast_to(small, (N_DEV,) + small.shape)], True)
    sent_g, sent_u, gwd1, (dmod_all, small_all) = _ffn_weight_grads(
        h1, dgate1, dup1, act1, df1, tg, 0, stream=True, first=everyone)
    wgrads = (sent_g, sent_u, gwd1) + tuple(parts_mid + parts_late)
    return dx0.reshape(nb, seq, d), wgrads, dmod_all, small_all


def kernel(x, c, w_ada, b_ada, g_ffn1, w1_gate, w1_up, w1_down, g_mix, w_in, g_sb_out, g_dil_out, w_out, rel_bias, g_ffn2, w2_gate, w2_up, w2_down, g_final, loss_target, m_w_ada, m_b_ada, m_g_ffn1, m_w1_gate, m_w1_up, m_w1_down, m_g_mix, m_w_in, m_g_sb_out, m_g_dil_out, m_w_out, m_rel_bias, m_g_ffn2, m_w2_gate, m_w2_up, m_w2_down, m_g_final, v_w_ada, v_b_ada, v_g_ffn1, v_w1_gate, v_w1_up, v_w1_down, v_g_mix, v_w_in, v_g_sb_out, v_g_dil_out, v_w_out, v_rel_bias, v_g_ffn2, v_w2_gate, v_w2_up, v_w2_down, v_g_final):
    nb, seq, d = x.shape
    me = 4 * lax.axis_index("x") + 2 * lax.axis_index("y") + lax.axis_index("c")
    tm = min(TOKEN_TILE, seq)
    fs = w1_gate.shape[2]
    fs_pad = _lane_pad(fs)
    ada_cols = w_ada.shape[2]

    def col_shard(w):
        return _pad_to(w[0].astype(BF16), 1, fs_pad)

    def row_shard(w):
        return _pad_to(w[0].astype(BF16), 0, fs_pad)

    shards = [col_shard(w1_gate), col_shard(w1_up), row_shard(w1_down), w_in[0].astype(BF16),
              w_out[0].astype(BF16), col_shard(w2_gate), col_shard(w2_up), row_shard(w2_down)]
    b_cols = lax.dynamic_slice(b_ada, (0, me * ada_cols), (1, ada_cols))
    c_every, mod_all, first = _first_exchange(_pad_to(c, 0, 8), shards[:3], w_ada[0], b_cols)
    c_all = c_every[:, :nb].reshape(N_DEV * nb, d)
    weights = first + shards[3:]
    mod = lax.dynamic_slice(mod_all, (0, me * 8, 0), (N_DEV, nb, ada_cols))
    mod = mod.transpose(1, 0, 2).reshape(nb, N_MOD, d)

    n_sb = g_sb_out.shape[1] * g_sb_out.shape[2]
    gains = (g_ffn1, g_mix, g_sb_out.reshape(1, n_sb), g_dil_out.reshape(1, -1), g_ffn2,
             g_final.reshape(1, d))
    grad_x, parts, dmod_all, small_all = _local_step(x, loss_target, mod, gains, weights, rel_bias, tm, True)

    last_part = _exchange([parts[2]], True, "scatter_last", chips=[True])[0]
    parts = parts[:2] + (last_part,) + parts[3:]
    dmod_all = dmod_all[:, :nb].reshape(N_DEV * nb, N_MOD * d)
    dmod_cols = lax.dynamic_slice(dmod_all, (0, me * ada_cols), (N_DEV * nb, ada_cols))
    gw_ada, gb_ada = _ada_bwd(c_all, dmod_cols, dmod_all)

    def small_part(row, size, shape):
        return small_all[:, row, :size].reshape((N_DEV,) + shape)

    loss = jnp.sum(small_all[:, 6, 0])

    n_rel = rel_bias.shape
    updates = {
        "w_ada": (w_ada[0], gw_ada[None], m_w_ada[0], v_w_ada[0]),
        "b_ada": (b_ada, gb_ada[None], m_b_ada, v_b_ada),
        "g_ffn1": (g_ffn1, small_part(0, d, (1, d)), m_g_ffn1, v_g_ffn1),
        "w1_gate": (w1_gate[0], parts[0], m_w1_gate[0], v_w1_gate[0]),
        "w1_up": (w1_up[0], parts[1], m_w1_up[0], v_w1_up[0]),
        "w1_down": (w1_down[0], parts[2], m_w1_down[0], v_w1_down[0]),
        "g_mix": (g_mix, small_part(1, d, (1, d)), m_g_mix, v_g_mix),
        "w_in": (w_in[0], parts[3], m_w_in[0], v_w_in[0]),
        "g_sb_out": (g_sb_out[0], small_all[:, 4, :n_sb].reshape((N_DEV,) + g_sb_out.shape[1:]),
                     m_g_sb_out[0], v_g_sb_out[0]),
        "g_dil_out": (g_dil_out[0], small_all[:, 4, n_sb:n_sb + g_dil_out[0].size].reshape((N_DEV,) + g_dil_out.shape[1:]),
                      m_g_dil_out[0], v_g_dil_out[0]),
        "w_out": (w_out[0], parts[4], m_w_out[0], v_w_out[0]),
        "rel_bias": (rel_bias, small_part(5, rel_bias.size, n_rel), m_rel_bias, v_rel_bias),
        "g_ffn2": (g_ffn2, small_part(2, d, (1, d)), m_g_ffn2, v_g_ffn2),
        "w2_gate": (w2_gate[0], parts[5], m_w2_gate[0], v_w2_gate[0]),
        "w2_up": (w2_up[0], parts[6], m_w2_up[0], v_w2_up[0]),
        "w2_down": (w2_down[0], parts[7], m_w2_down[0], v_w2_down[0]),
        "g_final": (g_final.reshape(1, d), small_part(3, d, (1, d)), m_g_final.reshape(1, d), v_g_final.reshape(1, d)),
    }
    shapes = {"w_ada": w_ada.shape, "b_ada": b_ada.shape, "g_ffn1": g_ffn1.shape, "w1_gate": w1_gate.shape,
              "w1_up": w1_up.shape, "w1_down": w1_down.shape, "g_mix": g_mix.shape, "w_in": w_in.shape,
              "g_sb_out": g_sb_out.shape, "g_dil_out": g_dil_out.shape, "w_out": w_out.shape,
              "rel_bias": rel_bias.shape, "g_ffn2": g_ffn2.shape, "w2_gate": w2_gate.shape,
              "w2_up": w2_up.shape, "w2_down": w2_down.shape, "g_final": g_final.shape}
    grads, deltas, new_m, new_v = [], [], [], []
    for name, (w, p, m, v) in updates.items():
        transposed = name in ("w1_gate", "w1_up", "w2_gate", "w2_up")
        outs = _adamw(w, p, m, v, f"adamw_{name}", transposed)
        for dst, a in zip((grads, deltas, new_m, new_v), outs):
            dst.append((a.T if transposed else a).reshape(shapes[name]))
    return (loss, grad_x, *grads, *deltas, *new_m, *new_v)
```

```python
import functools
import math

import numpy as np
import jax
import jax.numpy as jnp
from jax import lax
from jax.experimental import pallas as pl
from jax.experimental.pallas import tpu as pltpu

F32 = jnp.float32
BF16 = jnp.bfloat16

EPS = 1e-6
NEG_INF = -1e30
HEAD_DIM = 64
LANES = 128
DIL_BLOCK = 128
DILATIONS = (1, 4, 16)
N_BUCKETS = 32
MAX_DISTANCE = 2048
N_MOD = 9
N_DEV = 8
SB_BLOCK = 256
SB_HEADS = 4
SB_WIDTH = SB_HEADS * HEAD_DIM
DIL_HEADS = 4
DIL_WIDTH = DIL_HEADS * HEAD_DIM
TOKEN_TILE = 512
PROJ_TILE = 1024
GRAD_TILE = 2048
SHARD_GROUP = 2
FFN_CHUNKS = 2
VMEM_LIMIT_BYTES = 56 * 1024 * 1024

ADAM_LR = 0.001
ADAM_B1 = 0.9
ADAM_B2 = 0.999
ADAM_EPS = 1e-08
ADAM_WD = 0.01
ADAM_STEP = 10

NT_DIMS = (((1,), (1,)), ((), ()))
TN_DIMS = (((0,), (0,)), ((), ()))


def _params(*sem):
    return pltpu.CompilerParams(dimension_semantics=sem, vmem_limit_bytes=VMEM_LIMIT_BYTES)


def _once(spec):
    return pl.BlockSpec(spec.block_shape, spec.index_map, pipeline_mode=pl.Buffered(1))


def _dot(a, b):
    return jnp.dot(a, b, preferred_element_type=F32)


def _dot_nt(a, b):
    return lax.dot_general(a, b, NT_DIMS, preferred_element_type=F32)


def _dot_tn(a, b):
    return lax.dot_general(a, b, TN_DIMS, preferred_element_type=F32)


def _split_dot(a, b):
    hi = a.astype(BF16)
    lo = (a - hi.astype(F32)).astype(BF16)
    return _dot(hi, b) + _dot(lo, b)


def _sigmoid(z):
    return 1.0 / (1.0 + jnp.exp(-z))


def _norm(x):
    r = lax.rsqrt(jnp.mean(x * x, axis=-1, keepdims=True) + EPS)
    return x * r, r


def _modulate(x, g, mod_ref, k):
    n, _ = _norm(x)
    shift = mod_ref[3 * k:3 * k + 1, :]
    scale = mod_ref[3 * k + 1:3 * k + 2, :]
    return n * g * (1.0 + scale) + shift


def _modulate_bwd(dh, x, g, mod_ref, k):
    n, r = _norm(x)
    scale = mod_ref[3 * k + 1:3 * k + 2, :]
    dshift = jnp.sum(dh, axis=0, keepdims=True)
    dscale = jnp.sum(dh * n * g, axis=0, keepdims=True)
    dg = jnp.sum(dh * n * (1.0 + scale), axis=0, keepdims=True)
    dn = dh * g * (1.0 + scale)
    dx = r * (dn - n * jnp.mean(dn * n, axis=-1, keepdims=True))
    return dx, dshift, dscale, dg


class _Exchange:
    def __init__(self, arrays, scatter, relay=False, chips=None):
        assert not (scatter and relay)
        self.arrays = list(arrays)
        self.scatter = scatter
        self.relay = relay
        self.n = len(self.arrays)
        self.chips = list(chips) if chips is not None else [False] * self.n
        assert scatter or not any(self.chips)
        self.out_shape = [
            jax.ShapeDtypeStruct((N_DEV // 2 if ch else N_DEV,) + tuple(a.shape[1:] if scatter else a.shape), a.dtype)
            for a, ch in zip(self.arrays, self.chips)]
        n_remote = self.n * (N_DEV - 1)
        self.scratch_shapes = [pltpu.SemaphoreType.DMA((n_remote,)), pltpu.SemaphoreType.DMA((n_remote,)),
                               pltpu.SemaphoreType.DMA((self.n,))]

    def _copies(self, in_refs, out_refs, sems):
        send_sems, recv_sems, local_sems = sems
        x, y, c = lax.axis_index("x"), lax.axis_index("y"), lax.axis_index("c")
        me = 4 * x + 2 * y + c
        local, remote, relayed = [], {}, {}
        for a in range(self.n):
            if self.chips[a]:
                mine = 2 * x + y
                local.append(pltpu.make_async_copy(in_refs[a].at[mine], out_refs[a].at[mine], local_sems.at[a]))
                for k in (2, 4, 6):
                    px = 1 - x if k & 4 else x
                    py = 1 - y if k & 2 else y
                    sem = a * (N_DEV - 1) + k - 1
                    remote[a, k] = pltpu.make_async_remote_copy(
                        src_ref=in_refs[a].at[2 * px + py], dst_ref=out_refs[a].at[mine],
                        send_sem=send_sems.at[sem], recv_sem=recv_sems.at[sem],
                        device_id=(px, py, c), device_id_type=pl.DeviceIdType.MESH)
                continue
            src = in_refs[a].at[me] if self.scatter else in_refs[a]
            local.append(pltpu.make_async_copy(src, out_refs[a].at[me], local_sems.at[a]))
            for k in range(1, N_DEV):
                px = 1 - x if k & 4 else x
                py = 1 - y if k & 2 else y
                pc = 1 - c if k & 1 else c
                sem = a * (N_DEV - 1) + k - 1
                if self.relay and k & 1 and k > 1:
                    slot = 4 * px + 2 * py + c
                    relayed[a, k] = pltpu.make_async_remote_copy(
                        src_ref=out_refs[a].at[slot], dst_ref=out_refs[a].at[slot],
                        send_sem=send_sems.at[sem], recv_sem=recv_sems.at[sem],
                        device_id=(x, y, 1 - c), device_id_type=pl.DeviceIdType.MESH)
                    continue
                src = in_refs[a].at[4 * px + 2 * py + pc] if self.scatter else in_refs[a]
                remote[a, k] = pltpu.make_async_remote_copy(
                    src_ref=src, dst_ref=out_refs[a].at[me],
                    send_sem=send_sems.at[sem], recv_sem=recv_sems.at[sem],
                    device_id=(px, py, pc), device_id_type=pl.DeviceIdType.MESH)
        return local, remote, relayed

    def start(self, in_refs, out_refs, sems):
        local, remote, _ = self._copies(in_refs, out_refs, sems)
        for cp in local + list(remote.values()):
            cp.start()

    def wait(self, in_refs, out_refs, sems):
        local, remote, relayed = self._copies(in_refs, out_refs, sems)
        for (a, k), cp in relayed.items():
            remote[a, k - 1].wait_recv()
            cp.start()
        for (a, k), cp in remote.items():
            if (a, k + 1) not in relayed:
                cp.wait_recv()
        for cp in relayed.values():
            cp.wait_recv()
        for cp in list(remote.values()) + list(relayed.values()):
            cp.wait_send()
        for cp in local:
            cp.wait()


def _call(body, *, name, args, in_specs, out_specs, out_shape, scratch_shapes=(), grid=(),
          params=None, exchange=None):
    n_in, n_out = len(args), len(out_shape)
    if exchange is None:
        outs = pl.pallas_call(
            body, name=name, grid=grid, in_specs=list(in_specs), out_specs=list(out_specs),
            out_shape=list(out_shape), scratch_shapes=list(scratch_shapes), compiler_params=params,
        )(*args)
        return list(outs), []
    n_ex = exchange.n

    def wrapped(*refs):
        ins, refs = refs[:n_in], refs[n_in:]
        ex_in, refs = refs[:n_ex], refs[n_ex:]
        outs, refs = refs[:n_out], refs[n_out:]
        ex_out, refs = refs[:n_ex], refs[n_ex:]
        scratch, sems = refs[:len(refs) - 3], refs[len(refs) - 3:]
        if not grid:
            exchange.start(ex_in, ex_out, sems)
            body(*ins, *outs, *scratch)
            exchange.wait(ex_in, ex_out, sems)
            return
        first = functools.reduce(jnp.logical_and, [pl.program_id(a) == 0 for a in range(len(grid))])
        last = functools.reduce(jnp.logical_and, [pl.program_id(a) == grid[a] - 1 for a in range(len(grid))])

        @pl.when(first)
        def _():
            exchange.start(ex_in, ex_out, sems)

        body(*ins, *outs, *scratch)

        @pl.when(last)
        def _():
            exchange.wait(ex_in, ex_out, sems)

    any_spec = pl.BlockSpec(memory_space=pl.ANY)
    outs = pl.pallas_call(
        wrapped, name=name, grid=grid,
        in_specs=list(in_specs) + [any_spec] * n_ex, out_specs=list(out_specs) + [any_spec] * n_ex,
        out_shape=list(out_shape) + exchange.out_shape,
        scratch_shapes=list(scratch_shapes) + exchange.scratch_shapes, compiler_params=params,
    )(*args, *exchange.arrays)
    return list(outs[:n_out]), list(outs[n_out:])


def _exchange(arrays, scatter, name, relay=False, chips=None):
    return _call(lambda: None, name=name, args=(), in_specs=(), out_specs=(), out_shape=(),
                 exchange=_Exchange(arrays, scatter, relay, chips))[1]


def _first_exchange(c_pad, shards, w, b):
    rows, d = c_pad.shape
    cols = w.shape[1]
    ex_c = _Exchange([c_pad], False)
    ex_w = _Exchange(shards, False, relay=True)
    ex_m = _Exchange([jax.ShapeDtypeStruct((N_DEV * rows, cols), F32)], False)
    n_w = ex_w.n

    def body(*refs):
        c_ref, w_refs, wa_ref, b_ref = refs[0], refs[1:1 + n_w], refs[1 + n_w], refs[2 + n_w]
        outs = refs[3 + n_w:]
        cg_ref, wg_refs, mg_ref = outs[0], outs[1:1 + n_w], outs[1 + n_w]
        scratch = outs[2 + n_w:]
        sems_c, sems_w, sems_m, c_vm, m_vm = scratch[0:3], scratch[3:6], scratch[6:9], scratch[9], scratch[10]
        ex_c.start([c_ref], [cg_ref], sems_c)
        ex_c.wait([c_ref], [cg_ref], sems_c)
        pltpu.sync_copy(cg_ref, c_vm)
        cv = c_vm[...].reshape(N_DEV * rows, d)
        s = (cv * _sigmoid(cv)).astype(BF16)
        m_vm[...] = _dot(s, wa_ref[...].astype(BF16)) + b_ref[...]
        ex_m.start([m_vm], [mg_ref], sems_m)
        ex_w.start(w_refs, wg_refs, sems_w)
        ex_m.wait([m_vm], [mg_ref], sems_m)
        ex_w.wait(w_refs, wg_refs, sems_w)

    any_spec = pl.BlockSpec(memory_space=pl.ANY)
    vmem_spec = pl.BlockSpec(memory_space=pltpu.VMEM)
    outs = pl.pallas_call(
        body, name="first_exchange",
        in_specs=[any_spec] * (1 + n_w) + [vmem_spec, vmem_spec],
        out_specs=[any_spec] * (2 + n_w),
        out_shape=ex_c.out_shape + ex_w.out_shape + ex_m.out_shape,
        scratch_shapes=ex_c.scratch_shapes + ex_w.scratch_shapes + ex_m.scratch_shapes
        + [pltpu.VMEM((N_DEV, rows, d), F32), pltpu.VMEM((N_DEV * rows, cols), F32)],
        compiler_params=pltpu.CompilerParams(vmem_limit_bytes=VMEM_LIMIT_BYTES),
    )(c_pad, *shards, w, b)
    return outs[0], outs[1 + n_w], list(outs[1:1 + n_w])


def _ada_bwd(c_all, dmod_cols, dmod_all):
    def body(c_ref, dc_ref, da_ref, gw_ref, gb_ref):
        cv = c_ref[...]
        s = cv * _sigmoid(cv)
        gw_ref[...] = lax.dot_general(s, dc_ref[...], TN_DIMS, preferred_element_type=F32,
                                      precision=lax.Precision.HIGHEST)
        gb_ref[...] = jnp.sum(da_ref[...], axis=0, keepdims=True)

    return pl.pallas_call(
        body, name="ada_bwd",
        out_shape=(jax.ShapeDtypeStruct((c_all.shape[1], dmod_cols.shape[1]), F32),
                   jax.ShapeDtypeStruct((1, dmod_all.shape[1]), F32)),
        compiler_params=pltpu.CompilerParams(vmem_limit_bytes=VMEM_LIMIT_BYTES),
    )(c_all, dmod_cols, dmod_all)


def _side_by_side(w_ref):
    return jnp.concatenate([w_ref[s] for s in range(w_ref.shape[0])], axis=1)


def _stacked(w_ref):
    return jnp.concatenate([w_ref[s] for s in range(w_ref.shape[0])], axis=0)


def _loss_tile(x, target, g, acc_ref):
    d = x.shape[1]
    n, r = _norm(x)
    err = n * g - target
    dy = err * (1.0 / d)
    acc_ref[0:1, :] += jnp.sum(err * err, axis=0, keepdims=True)
    acc_ref[1:2, :] += jnp.sum(dy * n, axis=0, keepdims=True)
    dn = dy * g
    return r * (dn - n * jnp.mean(dn * n, axis=-1, keepdims=True))


def _ffn_fwd(x, mod, g, wg, wu, wd, k, tm, exchange=None, head=None):
    t, d = x.shape
    ns, _, fs = wg.shape
    nt = t // tm
    tpb = nt // mod.shape[0]
    rows = tm // FFN_CHUNKS
    extra = list(head) if head is not None else []

    def body(x_ref, mod_ref, g_ref, wg_ref, wu_ref, wd_ref, *rest):
        if head is not None:
            t_ref, gf_ref, xo_ref, f_ref, gg_ref, uu_ref, head_ref, h_sc, acc = rest
        else:
            xo_ref, f_ref, gg_ref, uu_ref, h_sc, acc = rest
        i, j = pl.program_id(0), pl.program_id(1)

        @pl.when(j == 0)
        def _():
            h_sc[...] = _modulate(x_ref[...], g_ref[...], mod_ref, k).astype(BF16)
            acc[...] = jnp.zeros_like(acc)

        chunks = [pl.ds(c * rows, rows) for c in range(FFN_CHUNKS)]
        wg, wu, wd = _side_by_side(wg_ref), _side_by_side(wu_ref), _stacked(wd_ref)
        gates, ups = [], []
        for rs in chunks:
            h = h_sc[rs, :]
            gates.append(_dot(h, wg))
            ups.append(_dot(h, wu))
        acts = [(g * _sigmoid(g) * u).astype(BF16) for g, u in zip(gates, ups)]
        for rs, g, u in zip(chunks, gates, ups):
            for s in range(SHARD_GROUP):
                gg_ref[s, rs, :] = g[:, s * fs:(s + 1) * fs].astype(BF16)
                uu_ref[s, rs, :] = u[:, s * fs:(s + 1) * fs].astype(BF16)
        downs = [_dot(a, wd) for a in acts]
        for rs, dn in zip(chunks, downs):
            acc[rs, :] += dn

        @pl.when(j == ns // SHARD_GROUP - 1)
        def _():
            f = acc[...]
            f_ref[...] = f.astype(BF16)
            xo = x_ref[...] + 0.5 * mod_ref[3 * k + 2:3 * k + 3, :] * f
            if head is None:
                xo_ref[...] = xo
            else:
                @pl.when(i == 0)
                def _():
                    head_ref[...] = jnp.zeros_like(head_ref)

                xo_ref[...] = _loss_tile(xo, t_ref[...], gf_ref[...], head_ref)

    tok = pl.BlockSpec((tm, d), lambda i, j: (i, 0))
    row = pl.BlockSpec((1, d), lambda i, j: (0, 0))
    hid = pl.BlockSpec((SHARD_GROUP, tm, fs), lambda i, j: (j, i, 0))
    head_specs = [_once(tok), row] if head is not None else []
    head_out = [pl.BlockSpec((8, d), lambda i, j: (0, 0))] if head is not None else []
    head_shape = [jax.ShapeDtypeStruct((8, d), F32)] if head is not None else []
    return _call(
        body, name=f"ffn_fwd{k}", grid=(nt, ns // SHARD_GROUP), args=(x, mod, g, wg, wu, wd, *extra),
        in_specs=[tok,
                  pl.BlockSpec((None, N_MOD, d), lambda i, j: (i // tpb, 0, 0)),
                  row,
                  pl.BlockSpec((SHARD_GROUP, d, fs), lambda i, j: (j, 0, 0)),
                  pl.BlockSpec((SHARD_GROUP, d, fs), lambda i, j: (j, 0, 0)),
                  pl.BlockSpec((SHARD_GROUP, fs, d), lambda i, j: (j, 0, 0))] + head_specs,
        out_specs=[tok, tok, hid, hid] + head_out,
        out_shape=[jax.ShapeDtypeStruct((t, d), F32), jax.ShapeDtypeStruct((t, d), BF16),
                   jax.ShapeDtypeStruct((ns, t, fs), BF16), jax.ShapeDtypeStruct((ns, t, fs), BF16)]
        + head_shape,
        scratch_shapes=[pltpu.VMEM((tm, d), BF16), pltpu.VMEM((tm, d), F32)],
        params=_params("arbitrary", "arbitrary"), exchange=exchange)


def _ffn_bwd(dxo, x, f, mod, g, gate, up, wg, wu, wd, k, tm, exchange=None):
    t, d = x.shape
    ns, _, fs = wg.shape
    nt = t // tm
    nb = mod.shape[0]
    tpb = nt // nb
    rows = tm // FFN_CHUNKS

    def body(dxo_ref, x_ref, f_ref, mod_ref, g_ref, gg_ref, uu_ref, wg_ref, wu_ref, wd_ref,
             dx_ref, dgg_ref, duu_ref, act_ref, h_ref, df_ref, dmod_ref, dg_ref, acc):
        i, j = pl.program_id(0), pl.program_id(1)

        @pl.when(j == 0)
        def _():
            df = 0.5 * mod_ref[3 * k + 2:3 * k + 3, :] * dxo_ref[...]
            df_ref[...] = df.astype(BF16)
            h_ref[...] = _modulate(x_ref[...], g_ref[...], mod_ref, k).astype(BF16)
            acc[...] = jnp.zeros_like(acc)

        chunks = [pl.ds(c * rows, rows) for c in range(FFN_CHUNKS)]
        group = range(SHARD_GROUP)
        wg, wu, wd = _side_by_side(wg_ref), _side_by_side(wu_ref), _stacked(wd_ref)
        dacts = [_dot_nt(df_ref[rs, :], wd) for rs in chunks]
        dgates, dups = [], []
        for rs, dact in zip(chunks, dacts):
            gv = jnp.concatenate([gg_ref[s, rs, :] for s in group], axis=1).astype(F32)
            uv = jnp.concatenate([uu_ref[s, rs, :] for s in group], axis=1).astype(F32)
            sig = _sigmoid(gv)
            s_act = gv * sig
            act = (s_act * uv).astype(BF16)
            for s in group:
                act_ref[s, rs, :] = act[:, s * fs:(s + 1) * fs]
            dups.append((dact * s_act).astype(BF16))
            dgates.append((dact * uv * (sig * (1.0 + gv * (1.0 - sig)))).astype(BF16))
        dhs = [_dot_nt(dg, wg) + _dot_nt(du, wu) for dg, du in zip(dgates, dups)]
        for rs, dg, du, dh in zip(chunks, dgates, dups, dhs):
            for s in group:
                dgg_ref[s, rs, :] = dg[:, s * fs:(s + 1) * fs]
                duu_ref[s, rs, :] = du[:, s * fs:(s + 1) * fs]
            acc[rs, :] += dh

        @pl.when(j == ns // SHARD_GROUP - 1)
        def _():
            dx, dshift, dscale, dg = _modulate_bwd(acc[...], x_ref[...], g_ref[...], mod_ref, k)
            dxo_v = dxo_ref[...]
            dx_ref[...] = dxo_v + dx
            dgt = jnp.sum(0.5 * f_ref[...].astype(F32) * dxo_v, axis=0, keepdims=True)

            @pl.when(i % tpb == 0)
            def _():
                dmod_ref[...] = jnp.zeros_like(dmod_ref)

            @pl.when(i == 0)
            def _():
                dg_ref[...] = jnp.zeros_like(dg_ref)

            dmod_ref[0:1, :] += dshift
            dmod_ref[1:2, :] += dscale
            dmod_ref[2:3, :] += dgt
            dg_ref[0:1, :] += dg

    tok = pl.BlockSpec((tm, d), lambda i, j: (i, 0))
    hid = pl.BlockSpec((SHARD_GROUP, tm, fs), lambda i, j: (j, i, 0))
    return _call(
        body, name=f"ffn_bwd{k}", grid=(nt, ns // SHARD_GROUP), args=(dxo, x, f, mod, g, gate, up, wg, wu, wd),
        in_specs=[tok, tok, tok,
                  pl.BlockSpec((None, N_MOD, d), lambda i, j: (i // tpb, 0, 0)),
                  pl.BlockSpec((1, d), lambda i, j: (0, 0)),
                  hid, hid,
                  pl.BlockSpec((SHARD_GROUP, d, fs), lambda i, j: (j, 0, 0)),
                  pl.BlockSpec((SHARD_GROUP, d, fs), lambda i, j: (j, 0, 0)),
                  pl.BlockSpec((SHARD_GROUP, fs, d), lambda i, j: (j, 0, 0))],
        out_specs=[tok, hid, hid, hid, tok, tok,
                   pl.BlockSpec((None, 8, d), lambda i, j: (i // tpb, 0, 0)),
                   pl.BlockSpec((8, d), lambda i, j: (0, 0))],
        out_shape=[jax.ShapeDtypeStruct((t, d), F32),
                   jax.ShapeDtypeStruct((ns, t, fs), BF16), jax.ShapeDtypeStruct((ns, t, fs), BF16),
                   jax.ShapeDtypeStruct((ns, t, fs), BF16),
                   jax.ShapeDtypeStruct((t, d), BF16), jax.ShapeDtypeStruct((t, d), BF16),
                   jax.ShapeDtypeStruct((nb, 8, d), F32), jax.ShapeDtypeStruct((8, d), F32)],
        scratch_shapes=[pltpu.VMEM((tm, d), F32)],
        params=_params("arbitrary", "arbitrary"), exchange=exchange)


def _mm_tn(a, b, a_spec, b_spec, out_shape, n_tiles, name, exchange=None, keep_transposed=False,
           pair_reduce=False):
    n_out = out_shape[0]
    block = tuple(out_shape[1:])
    last = n_tiles - 1
    flip = block[0] > block[1]
    if flip:
        block = block[::-1]
    if flip and keep_transposed:
        flip_back, out_shape = False, (n_out,) + block
    else:
        flip_back = flip
    full_shape = tuple(out_shape)
    n_pairs = n_out // 2
    if pair_reduce:
        out_shape = (n_pairs,) + full_shape[1:]

    def body(a_ref, b_ref, o_ref, acc, *pair):
        i, j = pl.program_id(0), pl.program_id(1)
        prod = _dot_tn(b_ref[...], a_ref[...]) if flip else _dot_tn(a_ref[...], b_ref[...])
        full_ref = pair[0] if pair_reduce else o_ref

        @pl.when(i == 0)
        def _():
            acc[j] = prod

        @pl.when(i > 0)
        def _():
            acc[j] += prod

        @pl.when(i == last)
        def _():
            total = acc[j]
            full_ref[j] = (total.T if flip_back else total).astype(BF16)

        if pair_reduce:
            _, landed, send_sems, recv_sems = pair

            @pl.when(jnp.logical_and(i == last, j == n_out - 1))
            def _():
                x, y, c = lax.axis_index("x"), lax.axis_index("y"), lax.axis_index("c")
                copies = [pltpu.make_async_remote_copy(
                    src_ref=full_ref.at[2 * q + 1 - c], dst_ref=landed.at[q],
                    send_sem=send_sems.at[q], recv_sem=recv_sems.at[q],
                    device_id=(x, y, 1 - c), device_id_type=pl.DeviceIdType.MESH) for q in range(n_pairs)]
                for cp in copies:
                    cp.start()
                for q, cp in enumerate(copies):
                    cp.wait_recv()
                    o_ref[q] = (full_ref[2 * q + c].astype(F32) + landed[q].astype(F32)).astype(BF16)
                for cp in copies:
                    cp.wait_send()

    scratch = [pltpu.VMEM((n_out,) + block, F32)]
    if pair_reduce:
        scratch += [pltpu.VMEM(full_shape, BF16), pltpu.VMEM(out_shape, BF16),
                    pltpu.SemaphoreType.DMA((n_pairs,)), pltpu.SemaphoreType.DMA((n_pairs,))]
    outs, sent = _call(
        body, name=name, grid=(n_tiles, n_out), args=(a, b), in_specs=[a_spec, b_spec],
        out_specs=[pl.BlockSpec(out_shape, lambda i, j: (0,) * len(out_shape))],
        out_shape=[jax.ShapeDtypeStruct(out_shape, BF16)],
        scratch_shapes=scratch,
        params=_params("arbitrary", "arbitrary"), exchange=exchange)
    return (outs[0], sent) if exchange is not None else outs[0]


def _ffn_weight_grads(h, dgate, dup, act, df, tm, tag, stream=False, first=None):
    t, d = h.shape
    ns, _, fs = dgate.shape
    nt = t // tm
    tok = pl.BlockSpec((tm, d), lambda i, j: (i, 0))
    hid = pl.BlockSpec((None, tm, fs), lambda i, j: (j, i, 0))
    if not stream:
        gwg = _mm_tn(h, dgate, tok, hid, (ns, d, fs), nt, f"grad_wg{tag}", keep_transposed=True)
        gwu = _mm_tn(h, dup, tok, hid, (ns, d, fs), nt, f"grad_wu{tag}", keep_transposed=True)
        gwd = _mm_tn(act, df, hid, tok, (ns, fs, d), nt, f"grad_wd{tag}")
        return gwg, gwu, gwd
    gwg, brought = _mm_tn(h, dgate, tok, hid, (ns, d, fs), nt, f"grad_wg{tag}", first,
                          keep_transposed=True, pair_reduce=True)
    gwu, sent_g = _mm_tn(h, dup, tok, hid, (ns, d, fs), nt, f"grad_wu{tag}",
                         _Exchange([gwg], True, chips=[True]), keep_transposed=True, pair_reduce=True)
    gwd, sent_u = _mm_tn(act, df, hid, tok, (ns, fs, d), nt, f"grad_wd{tag}",
                         _Exchange([gwu], True, chips=[True]), pair_reduce=True)
    return sent_g[0], sent_u[0], gwd, brought


def _stage_shape(rows, cols):
    return pltpu.VMEM((cols // LANES, rows, LANES), F32)


def _stage(value, stage_ref):
    for k in range(stage_ref.shape[0]):
        stage_ref[k] = value[:, k * LANES:(k + 1) * LANES]


def _to_residue_rows(stage_ref, dst_ref, dil):
    rows = stage_ref.shape[1] // dil
    for r in range(dil):
        for k in range(stage_ref.shape[0]):
            dst_ref[r, :, k * LANES:(k + 1) * LANES] = (
                stage_ref.at[k][pl.ds(r, rows, stride=dil), :].astype(dst_ref.dtype))


def _from_residue_rows(src_ref, stage_ref, dil):
    rows = stage_ref.shape[1] // dil
    chunks = range(stage_ref.shape[0])
    for r in range(dil):
        for k in chunks:
            stage_ref.at[k][pl.ds(r, rows, stride=dil), :] = src_ref[r, :, k * LANES:(k + 1) * LANES].astype(F32)
    return jnp.concatenate([stage_ref[k] for k in chunks], axis=1)


def _residue_shape(nb, seq, width, dil, dtype):
    return jax.ShapeDtypeStruct((nb, dil, seq // dil, width), dtype)


def _residue_spec(tm, tpb, cols, dil, col_block):
    return pl.BlockSpec((None, dil, tm // dil, cols),
                        lambda i, *rest: (i // tpb, 0, i % tpb, col_block(i, *rest)))


def _qkv_fwd(x, mod, g, win, tm, exchange=None):
    t, d = x.shape
    ns, _, cs = win.shape
    nt = t // tm
    nb = mod.shape[0]
    tpb = nt // nb
    seq = t // nb
    width = ns * cs // 2
    cs, ns = cs * SHARD_GROUP, ns // SHARD_GROUP
    half = ns // 2
    n_res = len(DILATIONS) - 1

    def body(x_ref, mod_ref, g_ref, w_ref, sb_ref, dil_ref, *rest):
        res_refs, h_ref, sc = rest[:n_res], rest[n_res], rest[n_res + 1]
        j = pl.program_id(1)

        @pl.when(j == 0)
        def _():
            h_ref[...] = _modulate(x_ref[...], g_ref[...], mod_ref, 1).astype(BF16)

        res = _dot(h_ref[...], _side_by_side(w_ref))

        @pl.when(j < half)
        def _():
            sb_ref[...] = res.astype(BF16)

        @pl.when(j >= half)
        def _():
            dil_ref[...] = res.astype(BF16)
            _stage(res, sc)
            for ref, dil in zip(res_refs, DILATIONS[1:]):
                _to_residue_rows(sc, ref, dil)

    def dil_col(i, j):
        return jnp.maximum(j - half, 0)

    tok = pl.BlockSpec((tm, d), lambda i, j: (i, 0))
    wide = jax.ShapeDtypeStruct((t, width), BF16)
    outs, got = _call(
        body, name="qkv_fwd", grid=(nt, ns), args=(x, mod, g, win),
        in_specs=[tok,
                  pl.BlockSpec((None, N_MOD, d), lambda i, j: (i // tpb, 0, 0)),
                  pl.BlockSpec((1, d), lambda i, j: (0, 0)),
                  pl.BlockSpec((SHARD_GROUP, d, cs // SHARD_GROUP), lambda i, j: (j, 0, 0))],
        out_specs=[pl.BlockSpec((tm, cs), lambda i, j: (i, jnp.minimum(j, half - 1))),
                   pl.BlockSpec((tm, cs), lambda i, j: (i, dil_col(i, j)))]
        + [_residue_spec(tm, tpb, cs, dil, dil_col) for dil in DILATIONS[1:]] + [tok],
        out_shape=[wide, wide] + [_residue_shape(nb, seq, width, dil, BF16) for dil in DILATIONS[1:]]
        + [jax.ShapeDtypeStruct((t, d), BF16)],
        scratch_shapes=[_stage_shape(tm, cs)],
        params=_params("arbitrary", "arbitrary"), exchange=exchange)
    qkv_dil = [outs[1]] + [a.reshape(t, width) for a in outs[2:2 + n_res]]
    return (outs[0], qkv_dil, outs[-1]), got


def _qkv_bwd(dqkv, dxo, x, mod, g, win, tm, exchange=None):
    t, d = x.shape
    ns, _, cs = win.shape
    nt = t // tm
    nb = mod.shape[0]
    tpb = nt // nb
    cs, ns = cs * SHARD_GROUP, ns // SHARD_GROUP

    def body(dq_ref, dxo_ref, x_ref, mod_ref, g_ref, w_ref, dx_ref, dmod_ref, dg_ref, acc):
        i, j = pl.program_id(0), pl.program_id(1)

        @pl.when(j == 0)
        def _():
            acc[...] = jnp.zeros_like(acc)

        acc[...] += _dot_nt(dq_ref[...], _side_by_side(w_ref))

        @pl.when(j == ns - 1)
        def _():
            dx, dshift, dscale, dg = _modulate_bwd(acc[...], x_ref[...], g_ref[...], mod_ref, 1)
            dx_ref[...] = dxo_ref[...] + dx

            @pl.when(i % tpb == 0)
            def _():
                dmod_ref[...] = jnp.zeros_like(dmod_ref)

            @pl.when(i == 0)
            def _():
                dg_ref[...] = jnp.zeros_like(dg_ref)

            dmod_ref[0:1, :] += dshift
            dmod_ref[1:2, :] += dscale
            dg_ref[0:1, :] += dg

    tok = pl.BlockSpec((tm, d), lambda i, j: (i, 0))
    return _call(
        body, name="qkv_bwd", grid=(nt, ns), args=(dqkv, dxo, x, mod, g, win),
        in_specs=[pl.BlockSpec((tm, cs), lambda i, j: (i, j)), tok, tok,
                  pl.BlockSpec((None, N_MOD, d), lambda i, j: (i // tpb, 0, 0)),
                  pl.BlockSpec((1, d), lambda i, j: (0, 0)),
                  pl.BlockSpec((SHARD_GROUP, d, cs // SHARD_GROUP), lambda i, j: (j, 0, 0))],
        out_specs=[tok,
                   pl.BlockSpec((None, 8, d), lambda i, j: (i // tpb, 0, 0)),
                   pl.BlockSpec((8, d), lambda i, j: (0, 0))],
        out_shape=[jax.ShapeDtypeStruct((t, d), F32),
                   jax.ShapeDtypeStruct((nb, 8, d), F32), jax.ShapeDtypeStruct((8, d), F32)],
        scratch_shapes=[pltpu.VMEM((tm, d), F32)],
        params=_params("arbitrary", "arbitrary"), exchange=exchange)


def _own_lanes():
    lane = lax.broadcasted_iota(jnp.int32, (1, LANES), 1)
    return [lane < HEAD_DIM, lane >= HEAD_DIM]


def _pair_tiles(a):
    return [a[:, (h // 2) * LANES:(h // 2 + 1) * LANES] for h in range(a.shape[1] // HEAD_DIM)]


def _own_tiles(a, own):
    return [jnp.where(own[h % 2], tile, jnp.zeros_like(tile)) for h, tile in enumerate(_pair_tiles(a))]


def _merge_tiles(per_head, own):
    return jnp.concatenate([jnp.where(own[0], per_head[h], per_head[h + 1])
                            for h in range(0, len(per_head), 2)], axis=1)


def _scaled(q):
    return (q.astype(F32) * (HEAD_DIM ** -0.5)).astype(BF16)


def _sb_logits(qh, kh, tri, causal):
    zs = [_dot_nt(q, k) for q, k in zip(qh, kh)]
    es = [jnp.exp(-jnp.abs(z)) for z in zs]
    log_nots = [-(jnp.maximum(z, 0.0) + jnp.log(1.0 + e)) for z, e in zip(zs, es)]
    if causal is not None:
        log_nots = [jnp.where(causal, ln, 0.0) for ln in log_nots]
    return zs, es, [_split_dot(ln, tri) for ln in log_nots]


def _sb_masks():
    rows = lax.broadcasted_iota(jnp.int32, (SB_BLOCK, SB_BLOCK), 0)
    cols = lax.broadcasted_iota(jnp.int32, (SB_BLOCK, SB_BLOCK), 1)
    return (rows >= cols).astype(BF16), (rows <= cols).astype(BF16), cols < rows


def _sb_fwd(qkv, nb, seq, exchange=None):
    t = qkv.shape[0]
    n_pairs = (qkv.shape[1] // 3) // SB_WIDTH
    tb = SB_BLOCK
    n_blk = seq // tb

    def body(q_ref, k_ref, v_ref, o_ref, c_ref):
        tri, _, causal = _sb_masks()
        own = _own_lanes()

        def key_blocks(qh, kjs, carry, mask):
            nh = SB_HEADS
            chains = range(nh * len(kjs))
            kss = [pl.multiple_of(kj * tb, tb) for kj in kjs]
            kh = [tile for ks in kss for tile in _pair_tiles(k_ref[pl.ds(ks, tb), :])]
            vh = [tile for ks in kss for tile in _pair_tiles(v_ref[pl.ds(ks, tb), :])]
            zs, _, suffixes = _sb_logits(qh * len(kjs), kh, tri, mask)
            right = []
            for c in chains:
                right.append(carry[c][1] if c < nh else right[c - nh] + suffixes[c - nh][:, 0:1])
            ws = [jnp.exp(zs[c] + suffixes[c] + right[c]) for c in chains]
            if mask is not None:
                ws = [jnp.where(mask, w, 0.0) for w in ws]
            pv = [_dot(ws[c].astype(BF16), vh[c]) for c in chains]
            last = (len(kjs) - 1) * nh
            return tuple((carry[h][0] + sum(pv[h::nh]), right[last + h] + suffixes[last + h][:, 0:1])
                         for h in range(nh))

        def query_block(qi, _):
            qs = pl.multiple_of(qi * tb, tb)
            qh = _own_tiles(_scaled(q_ref[pl.ds(qs, tb), :]), own)
            zero = (jnp.zeros((tb, LANES), F32), jnp.zeros((tb, 1), F32))
            carry = key_blocks(qh, [qi], (zero,) * SB_HEADS, causal)
            carry = lax.fori_loop(
                0, qi // 2, lambda p, cr: key_blocks(qh, [qi - 1 - 2 * p, qi - 2 - 2 * p], cr, None), carry)
            carry = lax.fori_loop(0, qi % 2, lambda _, cr: key_blocks(qh, [0], cr, None), carry)
            o_ref[pl.ds(qs, tb), :] = _merge_tiles([cr[0] for cr in carry], own)
            c_ref[pl.ds(qs, tb), :] = _merge_tiles([jnp.broadcast_to(cr[1], (tb, LANES)) for cr in carry], own)
            return 0

        lax.fori_loop(0, n_blk, query_block, 0)

    def spec(offset):
        return pl.BlockSpec((seq, SB_WIDTH), lambda b, p: (b, offset + p))

    out = jax.ShapeDtypeStruct((t, n_pairs * SB_WIDTH), F32)
    return _call(
        body, name="sb_fwd", grid=(nb, n_pairs), args=(qkv, qkv, qkv),
        in_specs=[spec(0), spec(n_pairs), spec(2 * n_pairs)],
        out_specs=[spec(0), spec(0)], out_shape=[out, out],
        params=_params("arbitrary", "arbitrary"), exchange=exchange)


def _sb_bwd(qkv, do, csum, nb, seq, exchange=None):
    t = qkv.shape[0]
    n_pairs = (qkv.shape[1] // 3) // SB_WIDTH
    tb = SB_BLOCK
    n_blk = seq // tb
    scale = HEAD_DIM ** -0.5

    def body(q_ref, k_ref, v_ref, do_ref, c_ref, dq_ref, dk_ref, dv_ref, dkt_acc, dvt_acc):
        tri, tri_prefix, causal = _sb_masks()
        own = _own_lanes()
        dkt_acc[...] = jnp.zeros_like(dkt_acc)
        dvt_acc[...] = jnp.zeros_like(dvt_acc)

        def key_blocks(qh, qth, doh, doth, ch, kjs, carry, mask):
            nh = SB_HEADS
            chains = range(nh * len(kjs))
            kss = [pl.multiple_of(kj * tb, tb) for kj in kjs]
            kh = [tile for ks in kss for tile in _pair_tiles(k_ref[pl.ds(ks, tb), :])]
            vh = [tile for ks in kss for tile in _pair_tiles(v_ref[pl.ds(ks, tb), :])]
            zs, es, suffixes = _sb_logits(qh * len(kjs), kh, tri, mask)
            dws = [_dot_nt(doh[c % nh], vh[c]) for c in chains]
            lefts = []
            for c in chains:
                before = carry[c][1] if c < nh else lefts[c - nh]
                lefts.append(before + suffixes[c][:, 0:1])
            ws = [jnp.exp(zs[c] + suffixes[c] + (ch[c % nh] - lefts[c])) for c in chains]
            if mask is not None:
                ws = [jnp.where(mask, w, 0.0) for w in ws]
            dlws = [ws[c] * dws[c] for c in chains]
            dprefixes = [_split_dot(dlw, tri_prefix) for dlw in dlws]
            dvts = [_dot(doth[c % nh], ws[c].astype(BF16)) for c in chains]
            dlefts, dzbs = [], []
            for c in chains:
                dlefts.append(carry[c][2] if c < nh else dlefts[c - nh] + dprefixes[c - nh][:, tb - 1:tb])
                sig = jnp.where(zs[c] >= 0.0, 1.0, es[c]) * pl.reciprocal(1.0 + es[c], approx=True)
                dz = dlws[c] - sig * (dlefts[c] + dprefixes[c])
                if mask is not None:
                    dz = jnp.where(mask, dz, 0.0)
                dzbs.append(dz.astype(BF16))
            dkts = [_dot(qth[c % nh], dzbs[c]) for c in chains]
            dqs = [_dot(dzbs[c], kh[c]) for c in chains]
            for b, ks in enumerate(kss):
                pairs = range(b * nh, (b + 1) * nh, 2)
                dkt_acc[:, pl.ds(ks, tb)] += jnp.concatenate([dkts[c] + dkts[c + 1] for c in pairs], axis=0)
                dvt_acc[:, pl.ds(ks, tb)] += jnp.concatenate([dvts[c] + dvts[c + 1] for c in pairs], axis=0)
            last = (len(kjs) - 1) * nh
            return tuple((carry[h][0] + sum(dqs[h::nh]), lefts[last + h],
                          dlefts[last + h] + dprefixes[last + h][:, tb - 1:tb]) for h in range(nh))

        def query_block(qi, _):
            qs = pl.multiple_of(qi * tb, tb)
            qh = _own_tiles(_scaled(q_ref[pl.ds(qs, tb), :]), own)
            doh = _own_tiles(do_ref[pl.ds(qs, tb), :], own)
            qth = [a.astype(F32).T.astype(BF16) for a in qh]
            doth = [a.T.astype(BF16) for a in doh]
            doh = [a.astype(BF16) for a in doh]
            cv = c_ref[pl.ds(qs, tb), :]
            ch = [cv[:, h * HEAD_DIM:h * HEAD_DIM + 1] for h in range(SB_HEADS)]
            zero = (jnp.zeros((tb, LANES), F32), jnp.zeros((tb, 1), F32), jnp.zeros((tb, 1), F32))

            def key_block(kjs, cr, mask):
                return key_blocks(qh, qth, doh, doth, ch, kjs, cr, mask)

            carry = lax.fori_loop(0, qi // 2, lambda p, cr: key_block([2 * p, 2 * p + 1], cr, None),
                                  (zero,) * SB_HEADS)
            carry = lax.fori_loop(0, qi % 2, lambda _, cr: key_block([qi - 1], cr, None), carry)
            carry = key_block([qi], carry, causal)
            dq = _merge_tiles([cr[0] for cr in carry], own) * scale
            dq_ref[pl.ds(qs, tb), :] = dq.astype(BF16)
            return 0

        lax.fori_loop(0, n_blk, query_block, 0)
        dk_ref[...] = dkt_acc[...].T.astype(BF16)
        dv_ref[...] = dvt_acc[...].T.astype(BF16)

    def spec(offset):
        return pl.BlockSpec((seq, SB_WIDTH), lambda b, p: (b, offset + p))

    out = jax.ShapeDtypeStruct((t, n_pairs * SB_WIDTH), BF16)
    return _call(
        body, name="sb_bwd", grid=(nb, n_pairs), args=(qkv, qkv, qkv, do, csum),
        in_specs=[spec(0), spec(n_pairs), spec(2 * n_pairs), spec(0), spec(0)],
        out_specs=[spec(0), spec(0), spec(0)],
        out_shape=[out, out, out],
        scratch_shapes=[pltpu.VMEM((SB_WIDTH, seq), F32), pltpu.VMEM((SB_WIDTH, seq), F32)],
        params=_params("arbitrary", "arbitrary"), exchange=exchange)


def _dil_block_scores(qh, kph, kch, bias_ref, has_prev, band_prev, band_cur):
    scale = HEAD_DIM ** -0.5
    heads = range(len(qh))
    no_prev = jnp.where(has_prev, 0.0, NEG_INF)
    zps = [_dot_nt(qh[h], kph[h]) for h in heads]
    zcs = [_dot_nt(qh[h], kch[h]) for h in heads]
    zps = [jnp.where(band_prev, zps[h] * scale + bias_ref[h, :, 0:DIL_BLOCK], NEG_INF) + no_prev for h in heads]
    zcs = [jnp.where(band_cur, zcs[h] * scale + bias_ref[h, :, DIL_BLOCK:2 * DIL_BLOCK], NEG_INF) for h in heads]
    return zps, zcs


def _dil_bands():
    rows = lax.broadcasted_iota(jnp.int32, (DIL_BLOCK, DIL_BLOCK), 0)
    cols = lax.broadcasted_iota(jnp.int32, (DIL_BLOCK, DIL_BLOCK), 1)
    return cols >= rows, cols <= rows


def _dil_fwd(qkv, bias, nb, seq, dil, exchange=None):
    t, width = qkv.shape
    n_pairs = (width // 3) // DIL_WIDTH
    bq = DIL_BLOCK
    n_blk = seq // bq
    per_seq = n_blk // dil
    heads = range(DIL_HEADS)

    def body(q_ref, k_ref, v_ref, bias_ref, o_ref, lse_ref):
        band_prev, band_cur = _dil_bands()
        own = _own_lanes()

        def block(n, _):
            has_prev = (n & (per_seq - 1)) != 0
            qs = pl.multiple_of(n * bq, bq)
            ps = pl.multiple_of(jnp.maximum(n - 1, 0) * bq, bq)
            qh = _own_tiles(q_ref[pl.ds(qs, bq), :], own)
            kp, kc = _pair_tiles(k_ref[pl.ds(ps, bq), :]), _pair_tiles(k_ref[pl.ds(qs, bq), :])
            vp, vc = _pair_tiles(v_ref[pl.ds(ps, bq), :]), _pair_tiles(v_ref[pl.ds(qs, bq), :])
            zps, zcs = _dil_block_scores(qh, kp, kc, bias_ref, has_prev, band_prev, band_cur)
            ms = [jnp.maximum(jnp.max(zps[h], axis=1, keepdims=True), jnp.max(zcs[h], axis=1, keepdims=True))
                  for h in heads]
            eps = [jnp.exp(zps[h] - ms[h]) for h in heads]
            ecs = [jnp.exp(zcs[h] - ms[h]) for h in heads]
            pvs = [_dot(eps[h].astype(BF16), vp[h]) + _dot(ecs[h].astype(BF16), vc[h]) for h in heads]
            dens = [jnp.sum(eps[h], axis=1, keepdims=True) + jnp.sum(ecs[h], axis=1, keepdims=True) for h in heads]
            o_ref[pl.ds(qs, bq), :] = _merge_tiles([pvs[h] / dens[h] for h in heads], own)
            lse_ref[pl.ds(qs, bq), :] = _merge_tiles(
                [jnp.broadcast_to(ms[h] + jnp.log(dens[h]), (bq, LANES)) for h in heads], own)
            return 0

        lax.fori_loop(0, n_blk, block, 0, unroll=4)

    def spec(offset):
        return pl.BlockSpec((seq, DIL_WIDTH), lambda b, p: (b, offset + p))

    out = jax.ShapeDtypeStruct((t, n_pairs * DIL_WIDTH), F32)
    return _call(
        body, name=f"dil_fwd{dil}", grid=(nb, n_pairs), args=(qkv, qkv, qkv, bias),
        in_specs=[spec(0), spec(n_pairs), spec(2 * n_pairs),
                  pl.BlockSpec((DIL_HEADS, bq, 2 * bq), lambda b, p: (p, 0, 0))],
        out_specs=[spec(0), spec(0)], out_shape=[out, out],
        params=_params("arbitrary", "arbitrary"), exchange=exchange)


def _dil_bwd(qkv, bias, do, lse, delta, nb, seq, dil):
    t, width = qkv.shape
    n_pairs = (width // 3) // DIL_WIDTH
    bq = DIL_BLOCK
    n_blk = seq // bq
    per_seq = n_blk // dil
    scale = HEAD_DIM ** -0.5
    heads = range(DIL_HEADS)

    def body(q_ref, k_ref, v_ref, bias_ref, do_ref, lse_ref, dl_ref, dq_ref, dk_ref, dv_ref, db_ref,
             dk_acc, dv_acc):
        band_prev, band_cur = _dil_bands()
        own = _own_lanes()
        dk_acc[...] = jnp.zeros_like(dk_acc)
        dv_acc[...] = jnp.zeros_like(dv_acc)

        @pl.when(pl.program_id(1) == 0)
        def _():
            db_ref[...] = jnp.zeros_like(db_ref)

        def block(n, _):
            has_prev = (n & (per_seq - 1)) != 0
            qs = pl.multiple_of(n * bq, bq)
            ps = pl.multiple_of(jnp.maximum(n - 1, 0) * bq, bq)
            qh = _own_tiles(q_ref[pl.ds(qs, bq), :], own)
            kp, kc = _pair_tiles(k_ref[pl.ds(ps, bq), :]), _pair_tiles(k_ref[pl.ds(qs, bq), :])
            vp, vc = _pair_tiles(v_ref[pl.ds(ps, bq), :]), _pair_tiles(v_ref[pl.ds(qs, bq), :])
            doh = _own_tiles(do_ref[pl.ds(qs, bq), :].astype(BF16), own)
            lse_v, dl_v = lse_ref[pl.ds(qs, bq), :], dl_ref[pl.ds(qs, bq), :]
            zps, zcs = _dil_block_scores(qh, kp, kc, bias_ref, has_prev, band_prev, band_cur)
            dpp = [_dot_nt(doh[h], vp[h]) for h in heads]
            dpc = [_dot_nt(doh[h], vc[h]) for h in heads]
            lse_h = [lse_v[:, h * HEAD_DIM:h * HEAD_DIM + 1] for h in heads]
            dl_h = [dl_v[:, h * HEAD_DIM:h * HEAD_DIM + 1] for h in heads]
            pps = [jnp.exp(zps[h] - lse_h[h]) for h in heads]
            pcs = [jnp.exp(zcs[h] - lse_h[h]) for h in heads]
            dvp = [_dot_tn(pps[h].astype(BF16), doh[h]) for h in heads]
            dvc = [_dot_tn(pcs[h].astype(BF16), doh[h]) for h in heads]
            dzps = [pps[h] * (dpp[h] - dl_h[h]) for h in heads]
            dzcs = [pcs[h] * (dpc[h] - dl_h[h]) for h in heads]
            dzp_b = [(dzps[h] * scale).astype(BF16) for h in heads]
            dzc_b = [(dzcs[h] * scale).astype(BF16) for h in heads]
            dqs = [_dot(dzp_b[h], kp[h]) + _dot(dzc_b[h], kc[h]) for h in heads]
            dkp = [_dot_tn(dzp_b[h], qh[h]) for h in heads]
            dkc = [_dot_tn(dzc_b[h], qh[h]) for h in heads]
            for h in heads:
                db_ref[h, :, 0:bq] += dzps[h]
                db_ref[h, :, bq:2 * bq] += dzcs[h]
            def pair_sums(per_head):
                return jnp.concatenate([per_head[h] + per_head[h + 1] for h in heads[::2]], axis=1)

            dq_ref[pl.ds(qs, bq), :] = _merge_tiles(dqs, own).astype(BF16)
            dk_acc[pl.ds(ps, bq), :] += pair_sums(dkp)
            dk_acc[pl.ds(qs, bq), :] += pair_sums(dkc)
            dv_acc[pl.ds(ps, bq), :] += pair_sums(dvp)
            dv_acc[pl.ds(qs, bq), :] += pair_sums(dvc)
            return 0

        lax.fori_loop(0, n_blk, block, 0, unroll=4)
        dk_ref[...] = dk_acc[...].astype(BF16)
        dv_ref[...] = dv_acc[...].astype(BF16)

    def spec(offset):
        return pl.BlockSpec((seq, DIL_WIDTH), lambda p, b: (b, offset + p))

    bias_spec = pl.BlockSpec((DIL_HEADS, bq, 2 * bq), lambda p, b: (p, 0, 0))
    out = jax.ShapeDtypeStruct((t, n_pairs * DIL_WIDTH), BF16)
    return pl.pallas_call(
        body, name=f"dil_bwd{dil}", grid=(n_pairs, nb),
        in_specs=[spec(0), spec(n_pairs), spec(2 * n_pairs), bias_spec, spec(0), spec(0), spec(0)],
        out_specs=[spec(0), spec(0), spec(0), bias_spec],
        out_shape=[out, out, out, jax.ShapeDtypeStruct(bias.shape, F32)],
        scratch_shapes=[pltpu.VMEM((seq, DIL_WIDTH), F32), pltpu.VMEM((seq, DIL_WIDTH), F32)],
        compiler_params=_params("arbitrary", "arbitrary"),
    )(qkv, qkv, qkv, bias, do, lse, delta)


def _head_blocks(width):
    rows = lax.broadcasted_iota(jnp.int32, (width, width), 0) // HEAD_DIM
    cols = lax.broadcasted_iota(jnp.int32, (width, width), 1) // HEAD_DIM
    return (rows == cols).astype(BF16)


def _head_mean(v, gmat):
    return _split_dot(v, gmat) * (1.0 / HEAD_DIM)


def _residue_views(arrays, nb, seq):
    return [a if dil == 1 else a.reshape(nb, dil, seq // dil, a.shape[1]) for a, dil in zip(arrays, DILATIONS)]


def _mix_out_fwd(osb, ocs, lses, gsb, gdil, wout, x, mod, tm):
    t, d = x.shape
    ds = osb.shape[1]
    nt = t // tm
    nb = mod.shape[0]
    tpb = nt // nb
    seq = t // nb
    n_cfg = len(DILATIONS)

    def body(osb_ref, *refs):
        oc_refs, lse_refs = refs[:n_cfg], refs[n_cfg:2 * n_cfg]
        gsb_ref, gdil_ref, w_ref, x_ref, mod_ref = refs[2 * n_cfg:2 * n_cfg + 5]
        xo_ref, on_ref, m_ref, odil_ref = refs[2 * n_cfg + 5:2 * n_cfg + 9]
        ld_refs = refs[2 * n_cfg + 9:3 * n_cfg + 9]
        stages, sc = refs[3 * n_cfg + 9:]
        ocv, lsev = [oc_refs[0][...]], [lse_refs[0][...]]
        for i, dil in enumerate(DILATIONS[1:]):
            ocv.append(_from_residue_rows(oc_refs[i + 1], stages.at[2 * i], dil))
            lsev.append(_from_residue_rows(lse_refs[i + 1], stages.at[2 * i + 1], dil))
        top = functools.reduce(jnp.maximum, lsev)
        total = top + jnp.log(sum(jnp.exp(l - top) for l in lsev))
        odil = sum(jnp.exp(l - total) * o for o, l in zip(ocv, lsev))
        odil_ref[...] = odil
        ld_refs[0][...] = total
        _stage(total, sc)
        for ref, dil in zip(ld_refs[1:], DILATIONS[1:]):
            _to_residue_rows(sc, ref, dil)
        gm = _head_blocks(ds)
        parts = []
        for o, g_ref in ((osb_ref[...], gsb_ref), (odil, gdil_ref)):
            parts.append(o * lax.rsqrt(_head_mean(o * o, gm) + EPS) * g_ref[...])
        on = jnp.concatenate(parts, axis=1).astype(BF16)
        on_ref[...] = on
        m = _dot(on, w_ref[...])
        m_ref[...] = m
        xo_ref[...] = x_ref[...] + mod_ref[5:6, :] * m

    tok = pl.BlockSpec((tm, d), lambda i: (i, 0))
    hd = pl.BlockSpec((tm, ds), lambda i: (i, 0))
    res = [hd] + [_residue_spec(tm, tpb, ds, dil, lambda i: 0) for dil in DILATIONS[1:]]
    res_shape = [jax.ShapeDtypeStruct((t, ds), F32)] + [_residue_shape(nb, seq, ds, dil, F32) for dil in DILATIONS[1:]]
    gain = pl.BlockSpec((1, ds), lambda i: (0, 0))
    outs = pl.pallas_call(
        body, name="mix_out_fwd", grid=(nt,),
        in_specs=[hd] + res + res + [gain, gain,
                  pl.BlockSpec(wout.shape, lambda i: (0, 0)),
                  tok, pl.BlockSpec((None, N_MOD, d), lambda i: (i // tpb, 0, 0))],
        out_specs=[tok, pl.BlockSpec((tm, 2 * ds), lambda i: (i, 0)), tok, hd] + res,
        out_shape=[jax.ShapeDtypeStruct((t, d), F32), jax.ShapeDtypeStruct((t, 2 * ds), BF16),
                   jax.ShapeDtypeStruct((t, d), F32), jax.ShapeDtypeStruct((t, ds), F32)] + res_shape,
        scratch_shapes=[pltpu.VMEM((2 * (n_cfg - 1), ds // LANES, tm, LANES), F32), _stage_shape(tm, ds)],
        compiler_params=_params("arbitrary"),
    )(osb, *_residue_views(ocs, nb, seq), *_residue_views(lses, nb, seq), gsb, gdil, wout, x, mod)
    return outs[0], outs[1], outs[2], outs[3], [a.reshape(t, ds) for a in outs[4:]]


def _mix_out_bwd(dxo, m, mod, wout, osb, odil, gsb, gdil, tm):
    t, d = dxo.shape
    ds = osb.shape[1]
    nt = t // tm
    nb = mod.shape[0]
    tpb = nt // nb
    seq = t // nb
    n_cfg = len(DILATIONS)

    def body(dxo_ref, m_ref, mod_ref, w_ref, osb_ref, odil_ref, gsb_ref, gdil_ref,
             dm_ref, dosb_ref, *rest):
        do_refs, dl_refs = rest[:n_cfg], rest[n_cfg:2 * n_cfg]
        dmod_ref, dg_ref, sc = rest[2 * n_cfg:]
        dodil_ref, dldil_ref = do_refs[0], dl_refs[0]
        i = pl.program_id(0)
        dxo_v = dxo_ref[...]
        dm = (mod_ref[5:6, :] * dxo_v).astype(BF16)
        dm_ref[...] = dm
        dgt = jnp.sum(m_ref[...] * dxo_v, axis=0, keepdims=True)
        don = _dot_nt(dm, w_ref[...])
        gm = _head_blocks(ds)

        @pl.when(i % tpb == 0)
        def _():
            dmod_ref[...] = jnp.zeros_like(dmod_ref)

        @pl.when(i == 0)
        def _():
            dg_ref[...] = jnp.zeros_like(dg_ref)

        dmod_ref[2:3, :] += dgt
        groups = ((osb_ref, gsb_ref, dosb_ref), (odil_ref, gdil_ref, dodil_ref))
        for k, (o_ref, g_ref, do_ref) in enumerate(groups):
            o = o_ref[...]
            dn_out = don[:, k * ds:(k + 1) * ds]
            r = lax.rsqrt(_head_mean(o * o, gm) + EPS)
            n = o * r
            dg_ref[0:1, k * ds:(k + 1) * ds] += jnp.sum(dn_out * n, axis=0, keepdims=True)
            dn = dn_out * g_ref[...]
            do = r * (dn - n * _head_mean(dn * n, gm))
            do_ref[...] = do
            if k == 1:
                delta = _head_mean(do * o, gm) * float(HEAD_DIM)
                dldil_ref[...] = delta
                for value, refs in ((do, do_refs), (delta, dl_refs)):
                    _stage(value, sc)
                    for ref, dil in zip(refs[1:], DILATIONS[1:]):
                        _to_residue_rows(sc, ref, dil)

    tok = pl.BlockSpec((tm, d), lambda i: (i, 0))
    hd = pl.BlockSpec((tm, ds), lambda i: (i, 0))
    res = [hd] + [_residue_spec(tm, tpb, ds, dil, lambda i: 0) for dil in DILATIONS[1:]]
    res_shape = [jax.ShapeDtypeStruct((t, ds), F32)] + [_residue_shape(nb, seq, ds, dil, F32) for dil in DILATIONS[1:]]
    gain = pl.BlockSpec((1, ds), lambda i: (0, 0))
    outs = pl.pallas_call(
        body, name="mix_out_bwd", grid=(nt,),
        in_specs=[tok, tok, pl.BlockSpec((None, N_MOD, d), lambda i: (i // tpb, 0, 0)),
                  pl.BlockSpec(wout.shape, lambda i: (0, 0)), hd, hd, gain, gain],
        out_specs=[tok, hd] + res + res
        + [pl.BlockSpec((None, 8, d), lambda i: (i // tpb, 0, 0)), pl.BlockSpec((8, 2 * ds), lambda i: (0, 0))],
        out_shape=[jax.ShapeDtypeStruct((t, d), BF16), jax.ShapeDtypeStruct((t, ds), F32)] + res_shape + res_shape
        + [jax.ShapeDtypeStruct((nb, 8, d), F32), jax.ShapeDtypeStruct((8, 2 * ds), F32)],
        scratch_shapes=[_stage_shape(tm, ds)],
        compiler_params=_params("arbitrary"),
    )(dxo, m, mod, wout, osb, odil, gsb, gdil)
    flat = [a.reshape(t, ds) for a in outs[2:2 + 2 * n_cfg]]
    return outs[0], outs[1], flat[:n_cfg], flat[n_cfg:], outs[-2], outs[-1]


def _merge_dqkv(sb_parts, dil_parts, nb, tm):
    t, ds = sb_parts[0].shape
    nt = t // tm
    tpb = nt // nb
    seq = t // nb
    n_cfg = len(DILATIONS)

    def body(*refs):
        sb_refs, dil_refs = refs[:3], refs[3:3 + 3 * n_cfg]
        o_ref, sc = refs[3 + 3 * n_cfg:]
        for k in range(3):
            o_ref[:, k * ds:(k + 1) * ds] = sb_refs[k][...]
            total = dil_refs[k * n_cfg][...].astype(F32)
            for i, dil in enumerate(DILATIONS[1:]):
                total = total + _from_residue_rows(dil_refs[k * n_cfg + i + 1], sc, dil)
            o_ref[:, (3 + k) * ds:(4 + k) * ds] = total.astype(BF16)

    hd = pl.BlockSpec((tm, ds), lambda i: (i, 0))
    res = [hd] + [_residue_spec(tm, tpb, ds, dil, lambda i: 0) for dil in DILATIONS[1:]]
    views = [v for parts in dil_parts for v in _residue_views(parts, nb, seq)]
    return pl.pallas_call(
        body, name="merge_dqkv", grid=(nt,),
        in_specs=[hd] * 3 + res * 3,
        out_specs=pl.BlockSpec((tm, 6 * ds), lambda i: (i, 0)),
        out_shape=jax.ShapeDtypeStruct((t, 6 * ds), BF16),
        scratch_shapes=[_stage_shape(tm, ds)],
        compiler_params=_params("arbitrary"),
    )(*sb_parts, *views)


def _row_tile(rows):
    if rows <= 256:
        return rows
    for cand in range(256, 15, -16):
        if rows % cand == 0:
            return cand
    return rows


def _adamw(w, parts, m, v, name, transposed=False):
    rows, cols = w.shape
    n_parts = parts.shape[0]
    tr = _row_tile(rows)
    c1 = 1.0 / (1.0 - ADAM_B1 ** ADAM_STEP)
    c2 = 1.0 / (1.0 - ADAM_B2 ** ADAM_STEP)

    def body(w_ref, p_ref, m_ref, v_ref, g_ref, d_ref, nm_ref, nv_ref):
        g = p_ref[0].astype(F32)
        for i in range(1, n_parts):
            g = g + p_ref[i].astype(F32)
        wv, mv, vv = w_ref[...], m_ref[...], v_ref[...]
        if transposed:
            wv, mv, vv = wv.T, mv.T, vv.T
        nm = ADAM_B1 * mv + (1.0 - ADAM_B1) * g
        nv = ADAM_B2 * vv + (1.0 - ADAM_B2) * (g * g)
        g_ref[...] = g
        nm_ref[...] = nm
        nv_ref[...] = nv
        d_ref[...] = -ADAM_LR * ((nm * c1) / (jnp.sqrt(nv * c2) + ADAM_EPS) + ADAM_WD * wv)

    blk = pl.BlockSpec((tr, cols), lambda i: (i, 0))
    if transposed:
        oblk = pl.BlockSpec((cols, tr), lambda i: (0, i))
        pblk = pl.BlockSpec((n_parts, cols, tr), lambda i: (0, 0, i))
        out = jax.ShapeDtypeStruct((cols, rows), F32)
    else:
        oblk, pblk = blk, pl.BlockSpec((n_parts, tr, cols), lambda i: (0, i, 0))
        out = jax.ShapeDtypeStruct((rows, cols), F32)
    return pl.pallas_call(
        body, name=name, grid=(rows // tr,),
        in_specs=[blk, pblk, blk, blk],
        out_specs=[oblk, oblk, oblk, oblk], out_shape=[out, out, out, out],
        compiler_params=_params("arbitrary"),
    )(w, parts, m, v)


def _t5_bucket(n):
    max_exact = N_BUCKETS // 2
    nf = np.maximum(n, 1).astype(np.float32)
    large = max_exact + (np.log(nf / max_exact) / math.log(MAX_DISTANCE / max_exact)
                         * (N_BUCKETS - max_exact)).astype(np.int32)
    large = np.minimum(large, N_BUCKETS - 1)
    return np.where(n < max_exact, n, large).astype(np.int32)


def _bucket_onehot():
    table = np.zeros((len(DILATIONS), 2 * DIL_BLOCK + 1, N_BUCKETS), np.float32)
    for i, dil in enumerate(DILATIONS):
        buckets = _t5_bucket(np.arange(DIL_BLOCK + 1) * dil)
        for m in range(DIL_BLOCK + 1):
            table[i, m, buckets[DIL_BLOCK - m]] = 1.0
    return table


def _bias_blocks(rel_bias):
    row = jnp.einsum("cmn,nh->chm", _bucket_onehot(), rel_bias, precision=lax.Precision.HIGHEST)
    n_cfg, n_heads, width = row.shape
    tiled = jnp.tile(row, (1, 1, DIL_BLOCK))[..., :DIL_BLOCK * (width - 1)]
    return tiled.reshape(n_cfg, n_heads, DIL_BLOCK, width - 1)


def _bias_blocks_bwd(dblocks):
    n_cfg, n_heads = dblocks.shape[:2]
    width = 2 * DIL_BLOCK + 1
    flat = dblocks.reshape(n_cfg, n_heads, DIL_BLOCK * (width - 1))
    flat = jnp.pad(flat, ((0, 0), (0, 0), (0, DIL_BLOCK)))
    drow = jnp.sum(flat.reshape(n_cfg, n_heads, DIL_BLOCK, width), axis=2)
    return jnp.einsum("chm,cmn->nh", drow, _bucket_onehot(), precision=lax.Precision.HIGHEST)


def _pad_to(a, axis, size):
    pad = [(0, 0)] * a.ndim
    pad[axis] = (0, size - a.shape[axis])
    return jnp.pad(a, pad)


def _lane_pad(n):
    return -(-n // LANES) * LANES


def _local_step(x, target, mod, gains, weights, rel_bias, tm, distributed):
    nb, seq, d = x.shape
    t = nb * seq
    g_ffn1, g_mix, g_sb, g_dil, g_ffn2, g_final = gains
    wg1, wu1, wd1 = weights[:3]
    x0 = x.reshape(t, d)
    ds = g_sb.shape[1]
    bias = _bias_blocks(rel_bias)

    def beside(arrays, scatter):
        return _Exchange(arrays, scatter) if distributed else None

    tp, tg = min(PROJ_TILE, seq), min(GRAD_TILE, t)

    (x1, f1, gate1, up1), got = _ffn_fwd(x0, mod, g_ffn1, wg1, wu1, wd1, 0, tp, beside(weights[3:4], False))
    win = got[0] if distributed else weights[3]
    (qkv, qkvd, h2), got = _qkv_fwd(x1, mod, g_mix, win, tp, beside(weights[4:5], False))
    wout = got[0] if distributed else weights[4]
    wout2 = wout.reshape(-1, d)
    (osb, csb), got = _sb_fwd(qkv, nb, seq, beside(weights[5:7], False))
    wg2, wu2 = got if distributed else weights[5:7]
    n_cfg = len(DILATIONS)
    piece = -(-weights[7].shape[-2] // n_cfg // 16) * 16
    ocs, lses, wd2_pieces = [], [], []
    for i, dil in enumerate(DILATIONS):
        rows = weights[7][..., i * piece:(i + 1) * piece, :]
        (oc, lse), got = _dil_fwd(qkvd[i], bias[i], nb, seq, dil, beside([rows], False))
        wd2_pieces.append(got[0] if distributed else rows)
        ocs.append(oc)
        lses.append(lse)
    wd2 = jnp.concatenate(wd2_pieces, axis=-2)
    x2, on, mix, odil, ldil = _mix_out_fwd(osb, ocs, lses, g_sb, g_dil, wout2, x1, mod, tm)
    (dx3, f3, gate3, up3, head), _ = _ffn_fwd(x2, mod, g_ffn2, wg2, wu2, wd2, 2, tp,
                                              head=(target.reshape(t, d), g_final))
    loss_sum = 0.5 * jnp.sum(head[0]) / d
    dg_final = head[1:2]

    (dx2, dgate3, dup3, act3, h3, df3, dmod3, dg_ffn2), _ = _ffn_bwd(
        dx3, x2, f3, mod, g_ffn2, gate3, up3, wg2, wu2, wd2, 2, tm)
    gwg2, gwu2, gwd2 = _ffn_weight_grads(h3, dgate3, dup3, act3, df3, tg, 2)

    dm, dosb, dodil, dldil, dmod2b, dg_heads = _mix_out_bwd(
        dx2, mix, mod, wout2, osb, odil, g_sb, g_dil, tm)
    gwout = _mm_tn(on, dm,
                   pl.BlockSpec((tg, wout.shape[1]), lambda i, j: (i, j)),
                   pl.BlockSpec((tg, d), lambda i, j: (i, 0)),
                   wout.shape, t // tg, "grad_wout")

    (dq_sb, dk_sb, dv_sb), parts_late = _sb_bwd(qkv, dosb, csb, nb, seq,
                                                beside([gwout, gwg2, gwu2, gwd2], True))
    dil_grads = [_dil_bwd(qkvd[i], bias[i], dodil[i], ldil[i], dldil[i], nb, seq, dil)
                 for i, dil in enumerate(DILATIONS)]
    dqkv = _merge_dqkv([dq_sb, dk_sb, dv_sb], [[g[k] for g in dil_grads] for k in range(3)], nb, tm)
    drel = _bias_blocks_bwd(jnp.stack([g[3] for g in dil_grads]))

    cs = win.shape[2]
    gwin = _mm_tn(h2, dqkv,
                  pl.BlockSpec((tg, d), lambda i, j: (i, 0)),
                  pl.BlockSpec((tg, cs), lambda i, j: (i, j)),
                  win.shape, t // tg, "grad_win", pair_reduce=distributed)
    (dx1, dmod2a, dg_mix), parts_mid = _qkv_bwd(
        dqkv, dx2, x1, mod, g_mix, win, tp, _Exchange([gwin], True, chips=[True]) if distributed else None)

    (dx0, dgate1, dup1, act1, h1, df1, dmod1, dg_ffn1), _ = _ffn_bwd(
        dx1, x0, f1, mod, g_ffn1, gate1, up1, wg1, wu1, wd1, 0, tm)
    dmod = jnp.concatenate([dmod1[:, 0:3], dmod2a[:, 0:2], dmod2b[:, 2:3], dmod3[:, 0:3]], axis=1)
    ggrads = (dg_ffn1[0:1], dg_mix[0:1], dg_heads[0:1], drel, dg_ffn2[0:1], dg_final)
    if not distributed:
        gw1 = _ffn_weight_grads(h1, dgate1, dup1, act1, df1, tg, 0)
        return loss_sum, dx0.reshape(nb, seq, d), tuple(gw1) + (gwin, gwout, gwg2, gwu2, gwd2), dmod, ggrads

    dg_heads_row, drel_flat = dg_heads[0:1], drel.reshape(1, -1)
    width = max(d, dg_heads_row.shape[1], drel_flat.shape[1])
    small = jnp.concatenate(
        [_pad_to(a.reshape(1, -1), 1, width)
         for a in (dg_ffn1[0:1], dg_mix[0:1], dg_ffn2[0:1], dg_final, dg_heads_row, drel_flat, loss_sum)]
        + [jnp.zeros((1, width), F32)], axis=0)
    dmod_pad = _pad_to(dmod.reshape(nb, N_MOD * d), 0, 8)
    everyone = _Exchange([jnp.broadcast_to(dmod_pad, (N_DEV,) + dmod_pad.shape),
                          jnp.broadcast_to(small, (N_DEV,) + small.shape)], True)
    sent_g, sent_u, gwd1, (dmod_all, small_all) = _ffn_weight_grads(
        h1, dgate1, dup1, act1, df1, tg, 0, stream=True, first=everyone)
    wgrads = (sent_g, sent_u, gwd1) + tuple(parts_mid + parts_late)
    return dx0.reshape(nb, seq, d), wgrads, dmod_all, small_all


def kernel(x, c, w_ada, b_ada, g_ffn1, w1_gate, w1_up, w1_down, g_mix, w_in, g_sb_out, g_dil_out, w_out, rel_bias, g_ffn2, w2_gate, w2_up, w2_down, g_final, loss_target, m_w_ada, m_b_ada, m_g_ffn1, m_w1_gate, m_w1_up, m_w1_down, m_g_mix, m_w_in, m_g_sb_out, m_g_dil_out, m_w_out, m_rel_bias, m_g_ffn2, m_w2_gate, m_w2_up, m_w2_down, m_g_final, v_w_ada, v_b_ada, v_g_ffn1, v_w1_gate, v_w1_up, v_w1_down, v_g_mix, v_w_in, v_g_sb_out, v_g_dil_out, v_w_out, v_rel_bias, v_g_ffn2, v_w2_gate, v_w2_up, v_w2_down, v_g_final):
    nb, seq, d = x.shape
    me = 4 * lax.axis_index("x") + 2 * lax.axis_index("y") + lax.axis_index("c")
    tm = min(TOKEN_TILE, seq)
    fs = w1_gate.shape[2]
    fs_pad = _lane_pad(fs)
    ada_cols = w_ada.shape[2]

    def col_shard(w):
        return _pad_to(w[0].astype(BF16), 1, fs_pad)

    def row_shard(w):
        return _pad_to(w[0].astype(BF16), 0, fs_pad)

    shards = [col_shard(w1_gate), col_shard(w1_up), row_shard(w1_down), w_in[0].astype(BF16),
              w_out[0].astype(BF16), col_shard(w2_gate), col_shard(w2_up), row_shard(w2_down)]
    b_cols = lax.dynamic_slice(b_ada, (0, me * ada_cols), (1, ada_cols))
    c_every, mod_all, first = _first_exchange(_pad_to(c, 0, 8), shards[:3], w_ada[0], b_cols)
    c_all = c_every[:, :nb].reshape(N_DEV * nb, d)
    weights = first + shards[3:]
    mod = lax.dynamic_slice(mod_all, (0, me * 8, 0), (N_DEV, nb, ada_cols))
    mod = mod.transpose(1, 0, 2).reshape(nb, N_MOD, d)

    n_sb = g_sb_out.shape[1] * g_sb_out.shape[2]
    gains = (g_ffn1, g_mix, g_sb_out.reshape(1, n_sb), g_dil_out.reshape(1, -1), g_ffn2,
             g_final.reshape(1, d))
    grad_x, parts, dmod_all, small_all = _local_step(x, loss_target, mod, gains, weights, rel_bias, tm, True)

    last_part = _exchange([parts[2]], True, "scatter_last", chips=[True])[0]
    parts = parts[:2] + (last_part,) + parts[3:]
    dmod_all = dmod_all[:, :nb].reshape(N_DEV * nb, N_MOD * d)
    dmod_cols = lax.dynamic_slice(dmod_all, (0, me * ada_cols), (N_DEV * nb, ada_cols))
    gw_ada, gb_ada = _ada_bwd(c_all, dmod_cols, dmod_all)

    def small_part(row, size, shape):
        return small_all[:, row, :size].reshape((N_DEV,) + shape)

    loss = jnp.sum(small_all[:, 6, 0])

    n_rel = rel_bias.shape
    updates = {
        "w_ada": (w_ada[0], gw_ada[None], m_w_ada[0], v_w_ada[0]),
        "b_ada": (b_ada, gb_ada[None], m_b_ada, v_b_ada),
        "g_ffn1": (g_ffn1, small_part(0, d, (1, d)), m_g_ffn1, v_g_ffn1),
        "w1_gate": (w1_gate[0], parts[0], m_w1_gate[0], v_w1_gate[0]),
        "w1_up": (w1_up[0], parts[1], m_w1_up[0], v_w1_up[0]),
        "w1_down": (w1_down[0], parts[2], m_w1_down[0], v_w1_down[0]),
        "g_mix": (g_mix, small_part(1, d, (1, d)), m_g_mix, v_g_mix),
        "w_in": (w_in[0], parts[3], m_w_in[0], v_w_in[0]),
        "g_sb_out": (g_sb_out[0], small_all[:, 4, :n_sb].reshape((N_DEV,) + g_sb_out.shape[1:]),
                     m_g_sb_out[0], v_g_sb_out[0]),
        "g_dil_out": (g_dil_out[0], small_all[:, 4, n_sb:n_sb + g_dil_out[0].size].reshape((N_DEV,) + g_dil_out.shape[1:]),
                      m_g_dil_out[0], v_g_dil_out[0]),
        "w_out": (w_out[0], parts[4], m_w_out[0], v_w_out[0]),
        "rel_bias": (rel_bias, small_part(5, rel_bias.size, n_rel), m_rel_bias, v_rel_bias),
        "g_ffn2": (g_ffn2, small_part(2, d, (1, d)), m_g_ffn2, v_g_ffn2),
        "w2_gate": (w2_gate[0], parts[5], m_w2_gate[0], v_w2_gate[0]),
        "w2_up": (w2_up[0], parts[6], m_w2_up[0], v_w2_up[0]),
        "w2_down": (w2_down[0], parts[7], m_w2_down[0], v_w2_down[0]),
        "g_final": (g_final.reshape(1, d), small_part(3, d, (1, d)), m_g_final.reshape(1, d), v_g_final.reshape(1, d)),
    }
    shapes = {"w_ada": w_ada.shape, "b_ada": b_ada.shape, "g_ffn1": g_ffn1.shape, "w1_gate": w1_gate.shape,
              "w1_up": w1_up.shape, "w1_down": w1_down.shape, "g_mix": g_mix.shape, "w_in": w_in.shape,
              "g_sb_out": g_sb_out.shape, "g_dil_out": g_dil_out.shape, "w_out": w_out.shape,
              "rel_bias": rel_bias.shape, "g_ffn2": g_ffn2.shape, "w2_gate": w2_gate.shape,
              "w2_up": w2_up.shape, "w2_down": w2_down.shape, "g_final": g_final.shape}
    grads, deltas, new_m, new_v = [], [], [], []
    for name, (w, p, m, v) in updates.items():
        transposed = name in ("w1_gate", "w1_up", "w2_gate", "w2_up")
        outs = _adamw(w, p, m, v, f"adamw_{name}", transposed)
        for dst, a in zip((grads, deltas, new_m, new_v), outs):
            dst.append((a.T if transposed else a).reshape(shapes[name]))
    return (loss, grad_x, *grads, *deltas, *new_m, *new_v)
```

```python
import functools
import math

import numpy as np
import jax
import jax.numpy as jnp
from jax import lax
from jax.experimental import pallas as pl
from jax.experimental.pallas import tpu as pltpu

F32 = jnp.float32
BF16 = jnp.bfloat16

EPS = 1e-6
NEG_INF = -1e30
HEAD_DIM = 64
LANES = 128
DIL_BLOCK = 128
DILATIONS = (1, 4, 16)
N_BUCKETS = 32
MAX_DISTANCE = 2048
N_MOD = 9
N_DEV = 8
SB_BLOCK = 256
SB_HEADS = 4
SB_WIDTH = SB_HEADS * HEAD_DIM
DIL_HEADS = 4
DIL_WIDTH = DIL_HEADS * HEAD_DIM
TOKEN_TILE = 512
PROJ_TILE = 1024
GRAD_TILE = 4096
SHARD_GROUP = 2
FFN_CHUNKS = 2
VMEM_LIMIT_BYTES = 56 * 1024 * 1024

ADAM_LR = 0.001
ADAM_B1 = 0.9
ADAM_B2 = 0.999
ADAM_EPS = 1e-08
ADAM_WD = 0.01
ADAM_STEP = 10

NT_DIMS = (((1,), (1,)), ((), ()))
TN_DIMS = (((0,), (0,)), ((), ()))


def _params(*sem):
    return pltpu.CompilerParams(dimension_semantics=sem, vmem_limit_bytes=VMEM_LIMIT_BYTES)


def _once(spec):
    return pl.BlockSpec(spec.block_shape, spec.index_map, pipeline_mode=pl.Buffered(1))


def _dot(a, b):
    return jnp.dot(a, b, preferred_element_type=F32)


def _dot_nt(a, b):
    return lax.dot_general(a, b, NT_DIMS, preferred_element_type=F32)


def _dot_tn(a, b):
    return lax.dot_general(a, b, TN_DIMS, preferred_element_type=F32)


def _split_dot(a, b):
    hi = a.astype(BF16)
    lo = (a - hi.astype(F32)).astype(BF16)
    return _dot(hi, b) + _dot(lo, b)


def _sigmoid(z):
    return 1.0 / (1.0 + jnp.exp(-z))


def _norm(x):
    r = lax.rsqrt(jnp.mean(x * x, axis=-1, keepdims=True) + EPS)
    return x * r, r


def _modulate(x, g, mod_ref, k):
    n, _ = _norm(x)
    shift = mod_ref[3 * k:3 * k + 1, :]
    scale = mod_ref[3 * k + 1:3 * k + 2, :]
    return n * g * (1.0 + scale) + shift


def _modulate_bwd(dh, x, g, mod_ref, k):
    n, r = _norm(x)
    scale = mod_ref[3 * k + 1:3 * k + 2, :]
    dshift = jnp.sum(dh, axis=0, keepdims=True)
    dscale = jnp.sum(dh * n * g, axis=0, keepdims=True)
    dg = jnp.sum(dh * n * (1.0 + scale), axis=0, keepdims=True)
    dn = dh * g * (1.0 + scale)
    dx = r * (dn - n * jnp.mean(dn * n, axis=-1, keepdims=True))
    return dx, dshift, dscale, dg


class _Exchange:
    def __init__(self, arrays, scatter, relay=False, chips=None):
        assert not (scatter and relay)
        self.arrays = list(arrays)
        self.scatter = scatter
        self.relay = relay
        self.n = len(self.arrays)
        self.chips = list(chips) if chips is not None else [False] * self.n
        assert scatter or not any(self.chips)
        self.out_shape = [
            jax.ShapeDtypeStruct((N_DEV // 2 if ch else N_DEV,) + tuple(a.shape[1:] if scatter else a.shape), a.dtype)
            for a, ch in zip(self.arrays, self.chips)]
        n_remote = self.n * (N_DEV - 1)
        self.scratch_shapes = [pltpu.SemaphoreType.DMA((n_remote,)), pltpu.SemaphoreType.DMA((n_remote,)),
                               pltpu.SemaphoreType.DMA((self.n,))]

    def _copies(self, in_refs, out_refs, sems):
        send_sems, recv_sems, local_sems = sems
        x, y, c = lax.axis_index("x"), lax.axis_index("y"), lax.axis_index("c")
        me = 4 * x + 2 * y + c
        local, remote, relayed = [], {}, {}
        for a in range(self.n):
            if self.chips[a]:
                mine = 2 * x + y
                local.append(pltpu.make_async_copy(in_refs[a].at[mine], out_refs[a].at[mine], local_sems.at[a]))
                for k in (2, 4, 6):
                    px = 1 - x if k & 4 else x
                    py = 1 - y if k & 2 else y
                    sem = a * (N_DEV - 1) + k - 1
                    remote[a, k] = pltpu.make_async_remote_copy(
                        src_ref=in_refs[a].at[2 * px + py], dst_ref=out_refs[a].at[mine],
                        send_sem=send_sems.at[sem], recv_sem=recv_sems.at[sem],
                        device_id=(px, py, c), device_id_type=pl.DeviceIdType.MESH)
                continue
            src = in_refs[a].at[me] if self.scatter else in_refs[a]
            local.append(pltpu.make_async_copy(src, out_refs[a].at[me], local_sems.at[a]))
            for k in range(1, N_DEV):
                px = 1 - x if k & 4 else x
                py = 1 - y if k & 2 else y
                pc = 1 - c if k & 1 else c
                sem = a * (N_DEV - 1) + k - 1
                if self.relay and k & 1 and k > 1:
                    slot = 4 * px + 2 * py + c
                    relayed[a, k] = pltpu.make_async_remote_copy(
                        src_ref=out_refs[a].at[slot], dst_ref=out_refs[a].at[slot],
                        send_sem=send_sems.at[sem], recv_sem=recv_sems.at[sem],
                        device_id=(x, y, 1 - c), device_id_type=pl.DeviceIdType.MESH)
                    continue
                src = in_refs[a].at[4 * px + 2 * py + pc] if self.scatter else in_refs[a]
                remote[a, k] = pltpu.make_async_remote_copy(
                    src_ref=src, dst_ref=out_refs[a].at[me],
                    send_sem=send_sems.at[sem], recv_sem=recv_sems.at[sem],
                    device_id=(px, py, pc), device_id_type=pl.DeviceIdType.MESH)
        return local, remote, relayed

    def start(self, in_refs, out_refs, sems):
        local, remote, _ = self._copies(in_refs, out_refs, sems)
        for cp in local + list(remote.values()):
            cp.start()

    def wait(self, in_refs, out_refs, sems):
        local, remote, relayed = self._copies(in_refs, out_refs, sems)
        for (a, k), cp in relayed.items():
            remote[a, k - 1].wait_recv()
            cp.start()
        for (a, k), cp in remote.items():
            if (a, k + 1) not in relayed:
                cp.wait_recv()
        for cp in relayed.values():
            cp.wait_recv()
        for cp in list(remote.values()) + list(relayed.values()):
            cp.wait_send()
        for cp in local:
            cp.wait()


def _call(body, *, name, args, in_specs, out_specs, out_shape, scratch_shapes=(), grid=(),
          params=None, exchange=None):
    n_in, n_out = len(args), len(out_shape)
    if exchange is None:
        outs = pl.pallas_call(
            body, name=name, grid=grid, in_specs=list(in_specs), out_specs=list(out_specs),
            out_shape=list(out_shape), scratch_shapes=list(scratch_shapes), compiler_params=params,
        )(*args)
        return list(outs), []
    n_ex = exchange.n

    def wrapped(*refs):
        ins, refs = refs[:n_in], refs[n_in:]
        ex_in, refs = refs[:n_ex], refs[n_ex:]
        outs, refs = refs[:n_out], refs[n_out:]
        ex_out, refs = refs[:n_ex], refs[n_ex:]
        scratch, sems = refs[:len(refs) - 3], refs[len(refs) - 3:]
        if not grid:
            exchange.start(ex_in, ex_out, sems)
            body(*ins, *outs, *scratch)
            exchange.wait(ex_in, ex_out, sems)
            return
        first = functools.reduce(jnp.logical_and, [pl.program_id(a) == 0 for a in range(len(grid))])
        last = functools.reduce(jnp.logical_and, [pl.program_id(a) == grid[a] - 1 for a in range(len(grid))])

        @pl.when(first)
        def _():
            exchange.start(ex_in, ex_out, sems)

        body(*ins, *outs, *scratch)

        @pl.when(last)
        def _():
            exchange.wait(ex_in, ex_out, sems)

    any_spec = pl.BlockSpec(memory_space=pl.ANY)
    outs = pl.pallas_call(
        wrapped, name=name, grid=grid,
        in_specs=list(in_specs) + [any_spec] * n_ex, out_specs=list(out_specs) + [any_spec] * n_ex,
        out_shape=list(out_shape) + exchange.out_shape,
        scratch_shapes=list(scratch_shapes) + exchange.scratch_shapes, compiler_params=params,
    )(*args, *exchange.arrays)
    return list(outs[:n_out]), list(outs[n_out:])


def _exchange(arrays, scatter, name, relay=False, chips=None):
    return _call(lambda: None, name=name, args=(), in_specs=(), out_specs=(), out_shape=(),
                 exchange=_Exchange(arrays, scatter, relay, chips))[1]


def _first_exchange(c_pad, shards, w, b):
    rows, d = c_pad.shape
    cols = w.shape[1]
    ex_c = _Exchange([c_pad], False)
    ex_w = _Exchange(shards, False, relay=True)
    ex_m = _Exchange([jax.ShapeDtypeStruct((N_DEV * rows, cols), F32)], False)
    n_w = ex_w.n

    def body(*refs):
        c_ref, w_refs, wa_ref, b_ref = refs[0], refs[1:1 + n_w], refs[1 + n_w], refs[2 + n_w]
        outs = refs[3 + n_w:]
        cg_ref, wg_refs, mg_ref = outs[0], outs[1:1 + n_w], outs[1 + n_w]
        scratch = outs[2 + n_w:]
        sems_c, sems_w, sems_m, c_vm, m_vm = scratch[0:3], scratch[3:6], scratch[6:9], scratch[9], scratch[10]
        ex_c.start([c_ref], [cg_ref], sems_c)
        ex_c.wait([c_ref], [cg_ref], sems_c)
        pltpu.sync_copy(cg_ref, c_vm)
        cv = c_vm[...].reshape(N_DEV * rows, d)
        s = (cv * _sigmoid(cv)).astype(BF16)
        m_vm[...] = _dot(s, wa_ref[...].astype(BF16)) + b_ref[...]
        ex_m.start([m_vm], [mg_ref], sems_m)
        ex_w.start(w_refs, wg_refs, sems_w)
        ex_m.wait([m_vm], [mg_ref], sems_m)
        ex_w.wait(w_refs, wg_refs, sems_w)

    any_spec = pl.BlockSpec(memory_space=pl.ANY)
    vmem_spec = pl.BlockSpec(memory_space=pltpu.VMEM)
    outs = pl.pallas_call(
        body, name="first_exchange",
        in_specs=[any_spec] * (1 + n_w) + [vmem_spec, vmem_spec],
        out_specs=[any_spec] * (2 + n_w),
        out_shape=ex_c.out_shape + ex_w.out_shape + ex_m.out_shape,
        scratch_shapes=ex_c.scratch_shapes + ex_w.scratch_shapes + ex_m.scratch_shapes
        + [pltpu.VMEM((N_DEV, rows, d), F32), pltpu.VMEM((N_DEV * rows, cols), F32)],
        compiler_params=pltpu.CompilerParams(vmem_limit_bytes=VMEM_LIMIT_BYTES),
    )(c_pad, *shards, w, b)
    return outs[0], outs[1 + n_w], list(outs[1:1 + n_w])


def _ada_bwd(c_all, dmod_cols, dmod_all):
    def body(c_ref, dc_ref, da_ref, gw_ref, gb_ref):
        cv = c_ref[...]
        s = cv * _sigmoid(cv)
        gw_ref[...] = lax.dot_general(s, dc_ref[...], TN_DIMS, preferred_element_type=F32,
                                      precision=lax.Precision.HIGHEST)
        gb_ref[...] = jnp.sum(da_ref[...], axis=0, keepdims=True)

    return pl.pallas_call(
        body, name="ada_bwd",
        out_shape=(jax.ShapeDtypeStruct((c_all.shape[1], dmod_cols.shape[1]), F32),
                   jax.ShapeDtypeStruct((1, dmod_all.shape[1]), F32)),
        compiler_params=pltpu.CompilerParams(vmem_limit_bytes=VMEM_LIMIT_BYTES),
    )(c_all, dmod_cols, dmod_all)


def _side_by_side(w_ref):
    return jnp.concatenate([w_ref[s] for s in range(w_ref.shape[0])], axis=1)


def _stacked(w_ref):
    return jnp.concatenate([w_ref[s] for s in range(w_ref.shape[0])], axis=0)


def _loss_tile(x, target, g, acc_ref):
    d = x.shape[1]
    n, r = _norm(x)
    err = n * g - target
    dy = err * (1.0 / d)
    acc_ref[0:1, :] += jnp.sum(err * err, axis=0, keepdims=True)
    acc_ref[1:2, :] += jnp.sum(dy * n, axis=0, keepdims=True)
    dn = dy * g
    return r * (dn - n * jnp.mean(dn * n, axis=-1, keepdims=True))


def _ffn_fwd(x, mod, g, wg, wu, wd, k, tm, exchange=None, head=None):
    t, d = x.shape
    ns, _, fs = wg.shape
    nt = t // tm
    tpb = nt // mod.shape[0]
    rows = tm // FFN_CHUNKS
    extra = list(head) if head is not None else []

    def body(x_ref, mod_ref, g_ref, wg_ref, wu_ref, wd_ref, *rest):
        if head is not None:
            t_ref, gf_ref, xo_ref, f_ref, gg_ref, uu_ref, head_ref, h_sc, acc = rest
        else:
            xo_ref, f_ref, gg_ref, uu_ref, h_sc, acc = rest
        i, j = pl.program_id(0), pl.program_id(1)

        @pl.when(j == 0)
        def _():
            h_sc[...] = _modulate(x_ref[...], g_ref[...], mod_ref, k).astype(BF16)
            acc[...] = jnp.zeros_like(acc)

        chunks = [pl.ds(c * rows, rows) for c in range(FFN_CHUNKS)]
        wg, wu, wd = _side_by_side(wg_ref), _side_by_side(wu_ref), _stacked(wd_ref)
        gates, ups = [], []
        for rs in chunks:
            h = h_sc[rs, :]
            gates.append(_dot(h, wg))
            ups.append(_dot(h, wu))
        acts = [(g * _sigmoid(g) * u).astype(BF16) for g, u in zip(gates, ups)]
        for rs, g, u in zip(chunks, gates, ups):
            for s in range(SHARD_GROUP):
                gg_ref[s, rs, :] = g[:, s * fs:(s + 1) * fs].astype(BF16)
                uu_ref[s, rs, :] = u[:, s * fs:(s + 1) * fs].astype(BF16)
        downs = [_dot(a, wd) for a in acts]
        for rs, dn in zip(chunks, downs):
            acc[rs, :] += dn

        @pl.when(j == ns // SHARD_GROUP - 1)
        def _():
            f = acc[...]
            f_ref[...] = f.astype(BF16)
            xo = x_ref[...] + 0.5 * mod_ref[3 * k + 2:3 * k + 3, :] * f
            if head is None:
                xo_ref[...] = xo
            else:
                @pl.when(i == 0)
                def _():
                    head_ref[...] = jnp.zeros_like(head_ref)

                xo_ref[...] = _loss_tile(xo, t_ref[...], gf_ref[...], head_ref)

    tok = pl.BlockSpec((tm, d), lambda i, j: (i, 0))
    row = pl.BlockSpec((1, d), lambda i, j: (0, 0))
    hid = pl.BlockSpec((SHARD_GROUP, tm, fs), lambda i, j: (j, i, 0))
    head_specs = [_once(tok), row] if head is not None else []
    head_out = [pl.BlockSpec((8, d), lambda i, j: (0, 0))] if head is not None else []
    head_shape = [jax.ShapeDtypeStruct((8, d), F32)] if head is not None else []
    return _call(
        body, name=f"ffn_fwd{k}", grid=(nt, ns // SHARD_GROUP), args=(x, mod, g, wg, wu, wd, *extra),
        in_specs=[tok,
                  pl.BlockSpec((None, N_MOD, d), lambda i, j: (i // tpb, 0, 0)),
                  row,
                  pl.BlockSpec((SHARD_GROUP, d, fs), lambda i, j: (j, 0, 0)),
                  pl.BlockSpec((SHARD_GROUP, d, fs), lambda i, j: (j, 0, 0)),
                  pl.BlockSpec((SHARD_GROUP, fs, d), lambda i, j: (j, 0, 0))] + head_specs,
        out_specs=[tok, tok, hid, hid] + head_out,
        out_shape=[jax.ShapeDtypeStruct((t, d), F32), jax.ShapeDtypeStruct((t, d), BF16),
                   jax.ShapeDtypeStruct((ns, t, fs), BF16), jax.ShapeDtypeStruct((ns, t, fs), BF16)]
        + head_shape,
        scratch_shapes=[pltpu.VMEM((tm, d), BF16), pltpu.VMEM((tm, d), F32)],
        params=_params("arbitrary", "arbitrary"), exchange=exchange)


def _ffn_bwd(dxo, x, f, mod, g, gate, up, wg, wu, wd, k, tm, exchange=None):
    t, d = x.shape
    ns, _, fs = wg.shape
    nt = t // tm
    nb = mod.shape[0]
    tpb = nt // nb
    rows = tm // FFN_CHUNKS

    def body(dxo_ref, x_ref, f_ref, mod_ref, g_ref, gg_ref, uu_ref, wg_ref, wu_ref, wd_ref,
             dx_ref, dgg_ref, duu_ref, act_ref, h_ref, df_ref, dmod_ref, dg_ref, acc):
        i, j = pl.program_id(0), pl.program_id(1)

        @pl.when(j == 0)
        def _():
            df = 0.5 * mod_ref[3 * k + 2:3 * k + 3, :] * dxo_ref[...]
            df_ref[...] = df.astype(BF16)
            h_ref[...] = _modulate(x_ref[...], g_ref[...], mod_ref, k).astype(BF16)
            acc[...] = jnp.zeros_like(acc)

        chunks = [pl.ds(c * rows, rows) for c in range(FFN_CHUNKS)]
        group = range(SHARD_GROUP)
        wg, wu, wd = _side_by_side(wg_ref), _side_by_side(wu_ref), _stacked(wd_ref)
        dacts = [_dot_nt(df_ref[rs, :], wd) for rs in chunks]
        dgates, dups = [], []
        for rs, dact in zip(chunks, dacts):
            gv = jnp.concatenate([gg_ref[s, rs, :] for s in group], axis=1).astype(F32)
            uv = jnp.concatenate([uu_ref[s, rs, :] for s in group], axis=1).astype(F32)
            sig = _sigmoid(gv)
            s_act = gv * sig
            act = (s_act * uv).astype(BF16)
            for s in group:
                act_ref[s, rs, :] = act[:, s * fs:(s + 1) * fs]
            dups.append((dact * s_act).astype(BF16))
            dgates.append((dact * uv * (sig * (1.0 + gv * (1.0 - sig)))).astype(BF16))
        dhs = [_dot_nt(dg, wg) + _dot_nt(du, wu) for dg, du in zip(dgates, dups)]
        for rs, dg, du, dh in zip(chunks, dgates, dups, dhs):
            for s in group:
                dgg_ref[s, rs, :] = dg[:, s * fs:(s + 1) * fs]
                duu_ref[s, rs, :] = du[:, s * fs:(s + 1) * fs]
            acc[rs, :] += dh

        @pl.when(j == ns // SHARD_GROUP - 1)
        def _():
            dx, dshift, dscale, dg = _modulate_bwd(acc[...], x_ref[...], g_ref[...], mod_ref, k)
            dxo_v = dxo_ref[...]
            dx_ref[...] = dxo_v + dx
            dgt = jnp.sum(0.5 * f_ref[...].astype(F32) * dxo_v, axis=0, keepdims=True)

            @pl.when(i % tpb == 0)
            def _():
                dmod_ref[...] = jnp.zeros_like(dmod_ref)

            @pl.when(i == 0)
            def _():
                dg_ref[...] = jnp.zeros_like(dg_ref)

            dmod_ref[0:1, :] += dshift
            dmod_ref[1:2, :] += dscale
            dmod_ref[2:3, :] += dgt
            dg_ref[0:1, :] += dg

    tok = pl.BlockSpec((tm, d), lambda i, j: (i, 0))
    hid = pl.BlockSpec((SHARD_GROUP, tm, fs), lambda i, j: (j, i, 0))
    return _call(
        body, name=f"ffn_bwd{k}", grid=(nt, ns // SHARD_GROUP), args=(dxo, x, f, mod, g, gate, up, wg, wu, wd),
        in_specs=[tok, tok, tok,
                  pl.BlockSpec((None, N_MOD, d), lambda i, j: (i // tpb, 0, 0)),
                  pl.BlockSpec((1, d), lambda i, j: (0, 0)),
                  hid, hid,
                  pl.BlockSpec((SHARD_GROUP, d, fs), lambda i, j: (j, 0, 0)),
                  pl.BlockSpec((SHARD_GROUP, d, fs), lambda i, j: (j, 0, 0)),
                  pl.BlockSpec((SHARD_GROUP, fs, d), lambda i, j: (j, 0, 0))],
        out_specs=[tok, hid, hid, hid, tok, tok,
                   pl.BlockSpec((None, 8, d), lambda i, j: (i // tpb, 0, 0)),
                   pl.BlockSpec((8, d), lambda i, j: (0, 0))],
        out_shape=[jax.ShapeDtypeStruct((t, d), F32),
                   jax.ShapeDtypeStruct((ns, t, fs), BF16), jax.ShapeDtypeStruct((ns, t, fs), BF16),
                   jax.ShapeDtypeStruct((ns, t, fs), BF16),
                   jax.ShapeDtypeStruct((t, d), BF16), jax.ShapeDtypeStruct((t, d), BF16),
                   jax.ShapeDtypeStruct((nb, 8, d), F32), jax.ShapeDtypeStruct((8, d), F32)],
        scratch_shapes=[pltpu.VMEM((tm, d), F32)],
        params=_params("arbitrary", "arbitrary"), exchange=exchange)


def _mm_tn(a, b, a_spec, b_spec, out_shape, n_tiles, name, exchange=None, keep_transposed=False,
           pair_reduce=False):
    n_out = out_shape[0]
    block = tuple(out_shape[1:])
    last = n_tiles - 1
    flip = block[0] > block[1]
    if flip:
        block = block[::-1]
    if flip and keep_transposed:
        flip_back, out_shape = False, (n_out,) + block
    else:
        flip_back = flip
    full_shape = tuple(out_shape)
    n_pairs = n_out // 2
    if pair_reduce:
        out_shape = (n_pairs,) + full_shape[1:]

    def body(a_ref, b_ref, o_ref, acc, *pair):
        i, j = pl.program_id(0), pl.program_id(1)
        prod = _dot_tn(b_ref[...], a_ref[...]) if flip else _dot_tn(a_ref[...], b_ref[...])
        full_ref = pair[0] if pair_reduce else o_ref

        @pl.when(i == 0)
        def _():
            acc[j] = prod

        @pl.when(i > 0)
        def _():
            acc[j] += prod

        @pl.when(i == last)
        def _():
            total = acc[j]
            full_ref[j] = (total.T if flip_back else total).astype(BF16)

        if pair_reduce:
            _, landed, send_sems, recv_sems = pair

            @pl.when(jnp.logical_and(i == last, j == n_out - 1))
            def _():
                x, y, c = lax.axis_index("x"), lax.axis_index("y"), lax.axis_index("c")
                copies = [pltpu.make_async_remote_copy(
                    src_ref=full_ref.at[2 * q + 1 - c], dst_ref=landed.at[q],
                    send_sem=send_sems.at[q], recv_sem=recv_sems.at[q],
                    device_id=(x, y, 1 - c), device_id_type=pl.DeviceIdType.MESH) for q in range(n_pairs)]
                for cp in copies:
                    cp.start()
                for q, cp in enumerate(copies):
                    cp.wait_recv()
                    o_ref[q] = (full_ref[2 * q + c].astype(F32) + landed[q].astype(F32)).astype(BF16)
                for cp in copies:
                    cp.wait_send()

    scratch = [pltpu.VMEM((n_out,) + block, F32)]
    if pair_reduce:
        scratch += [pltpu.VMEM(full_shape, BF16), pltpu.VMEM(out_shape, BF16),
                    pltpu.SemaphoreType.DMA((n_pairs,)), pltpu.SemaphoreType.DMA((n_pairs,))]
    outs, sent = _call(
        body, name=name, grid=(n_tiles, n_out), args=(a, b), in_specs=[a_spec, b_spec],
        out_specs=[pl.BlockSpec(out_shape, lambda i, j: (0,) * len(out_shape))],
        out_shape=[jax.ShapeDtypeStruct(out_shape, BF16)],
        scratch_shapes=scratch,
        params=_params("arbitrary", "arbitrary"), exchange=exchange)
    return (outs[0], sent) if exchange is not None else outs[0]


def _ffn_weight_grads(h, dgate, dup, act, df, tm, tag, stream=False, first=None):
    t, d = h.shape
    ns, _, fs = dgate.shape
    nt = t // tm
    tok = pl.BlockSpec((tm, d), lambda i, j: (i, 0))
    hid = pl.BlockSpec((None, tm, fs), lambda i, j: (j, i, 0))
    if not stream:
        gwg = _mm_tn(h, dgate, tok, hid, (ns, d, fs), nt, f"grad_wg{tag}", keep_transposed=True)
        gwu = _mm_tn(h, dup, tok, hid, (ns, d, fs), nt, f"grad_wu{tag}", keep_transposed=True)
        gwd = _mm_tn(act, df, hid, tok, (ns, fs, d), nt, f"grad_wd{tag}")
        return gwg, gwu, gwd
    gwg, brought = _mm_tn(h, dgate, tok, hid, (ns, d, fs), nt, f"grad_wg{tag}", first,
                          keep_transposed=True, pair_reduce=True)
    gwu, sent_g = _mm_tn(h, dup, tok, hid, (ns, d, fs), nt, f"grad_wu{tag}",
                         _Exchange([gwg], True, chips=[True]), keep_transposed=True, pair_reduce=True)
    gwd, sent_u = _mm_tn(act, df, hid, tok, (ns, fs, d), nt, f"grad_wd{tag}",
                         _Exchange([gwu], True, chips=[True]), pair_reduce=True)
    return sent_g[0], sent_u[0], gwd, brought


def _stage_shape(rows, cols):
    return pltpu.VMEM((cols // LANES, rows, LANES), F32)


def _stage(value, stage_ref):
    for k in range(stage_ref.shape[0]):
        stage_ref[k] = value[:, k * LANES:(k + 1) * LANES]


def _to_residue_rows(stage_ref, dst_ref, dil):
    rows = stage_ref.shape[1] // dil
    for r in range(dil):
        for k in range(stage_ref.shape[0]):
            dst_ref[r, :, k * LANES:(k + 1) * LANES] = (
                stage_ref.at[k][pl.ds(r, rows, stride=dil), :].astype(dst_ref.dtype))


def _from_residue_rows(src_ref, stage_ref, dil):
    rows = stage_ref.shape[1] // dil
    chunks = range(stage_ref.shape[0])
    for r in range(dil):
        for k in chunks:
            stage_ref.at[k][pl.ds(r, rows, stride=dil), :] = src_ref[r, :, k * LANES:(k + 1) * LANES].astype(F32)
    return jnp.concatenate([stage_ref[k] for k in chunks], axis=1)


def _residue_shape(nb, seq, width, dil, dtype):
    return jax.ShapeDtypeStruct((nb, dil, seq // dil, width), dtype)


def _residue_spec(tm, tpb, cols, dil, col_block):
    return pl.BlockSpec((None, dil, tm // dil, cols),
                        lambda i, *rest: (i // tpb, 0, i % tpb, col_block(i, *rest)))


def _qkv_fwd(x, mod, g, win, tm, exchange=None):
    t, d = x.shape
    ns, _, cs = win.shape
    nt = t // tm
    nb = mod.shape[0]
    tpb = nt // nb
    seq = t // nb
    width = ns * cs // 2
    cs, ns = cs * SHARD_GROUP, ns // SHARD_GROUP
    half = ns // 2
    n_res = len(DILATIONS) - 1

    def body(x_ref, mod_ref, g_ref, w_ref, sb_ref, dil_ref, *rest):
        res_refs, h_ref, sc = rest[:n_res], rest[n_res], rest[n_res + 1]
        j = pl.program_id(1)

        @pl.when(j == 0)
        def _():
            h_ref[...] = _modulate(x_ref[...], g_ref[...], mod_ref, 1).astype(BF16)

        res = _dot(h_ref[...], _side_by_side(w_ref))

        @pl.when(j < half)
        def _():
            sb_ref[...] = res.astype(BF16)

        @pl.when(j >= half)
        def _():
            dil_ref[...] = res.astype(BF16)
            _stage(res, sc)
            for ref, dil in zip(res_refs, DILATIONS[1:]):
                _to_residue_rows(sc, ref, dil)

    def dil_col(i, j):
        return jnp.maximum(j - half, 0)

    tok = pl.BlockSpec((tm, d), lambda i, j: (i, 0))
    wide = jax.ShapeDtypeStruct((t, width), BF16)
    outs, got = _call(
        body, name="qkv_fwd", grid=(nt, ns), args=(x, mod, g, win),
        in_specs=[tok,
                  pl.BlockSpec((None, N_MOD, d), lambda i, j: (i // tpb, 0, 0)),
                  pl.BlockSpec((1, d), lambda i, j: (0, 0)),
                  pl.BlockSpec((SHARD_GROUP, d, cs // SHARD_GROUP), lambda i, j: (j, 0, 0))],
        out_specs=[pl.BlockSpec((tm, cs), lambda i, j: (i, jnp.minimum(j, half - 1))),
                   pl.BlockSpec((tm, cs), lambda i, j: (i, dil_col(i, j)))]
        + [_residue_spec(tm, tpb, cs, dil, dil_col) for dil in DILATIONS[1:]] + [tok],
        out_shape=[wide, wide] + [_residue_shape(nb, seq, width, dil, BF16) for dil in DILATIONS[1:]]
        + [jax.ShapeDtypeStruct((t, d), BF16)],
        scratch_shapes=[_stage_shape(tm, cs)],
        params=_params("arbitrary", "arbitrary"), exchange=exchange)
    qkv_dil = [outs[1]] + [a.reshape(t, width) for a in outs[2:2 + n_res]]
    return (outs[0], qkv_dil, outs[-1]), got


def _qkv_bwd(dqkv, dxo, x, mod, g, win, tm, exchange=None):
    t, d = x.shape
    ns, _, cs = win.shape
    nt = t // tm
    nb = mod.shape[0]
    tpb = nt // nb
    cs, ns = cs * SHARD_GROUP, ns // SHARD_GROUP

    def body(dq_ref, dxo_ref, x_ref, mod_ref, g_ref, w_ref, dx_ref, dmod_ref, dg_ref, acc):
        i, j = pl.program_id(0), pl.program_id(1)

        @pl.when(j == 0)
        def _():
            acc[...] = jnp.zeros_like(acc)

        acc[...] += _dot_nt(dq_ref[...], _side_by_side(w_ref))

        @pl.when(j == ns - 1)
        def _():
            dx, dshift, dscale, dg = _modulate_bwd(acc[...], x_ref[...], g_ref[...], mod_ref, 1)
            dx_ref[...] = dxo_ref[...] + dx

            @pl.when(i % tpb == 0)
            def _():
                dmod_ref[...] = jnp.zeros_like(dmod_ref)

            @pl.when(i == 0)
            def _():
                dg_ref[...] = jnp.zeros_like(dg_ref)

            dmod_ref[0:1, :] += dshift
            dmod_ref[1:2, :] += dscale
            dg_ref[0:1, :] += dg

    tok = pl.BlockSpec((tm, d), lambda i, j: (i, 0))
    return _call(
        body, name="qkv_bwd", grid=(nt, ns), args=(dqkv, dxo, x, mod, g, win),
        in_specs=[pl.BlockSpec((tm, cs), lambda i, j: (i, j)), tok, tok,
                  pl.BlockSpec((None, N_MOD, d), lambda i, j: (i // tpb, 0, 0)),
                  pl.BlockSpec((1, d), lambda i, j: (0, 0)),
                  pl.BlockSpec((SHARD_GROUP, d, cs // SHARD_GROUP), lambda i, j: (j, 0, 0))],
        out_specs=[tok,
                   pl.BlockSpec((None, 8, d), lambda i, j: (i // tpb, 0, 0)),
                   pl.BlockSpec((8, d), lambda i, j: (0, 0))],
        out_shape=[jax.ShapeDtypeStruct((t, d), F32),
                   jax.ShapeDtypeStruct((nb, 8, d), F32), jax.ShapeDtypeStruct((8, d), F32)],
        scratch_shapes=[pltpu.VMEM((tm, d), F32)],
        params=_params("arbitrary", "arbitrary"), exchange=exchange)


def _own_lanes():
    lane = lax.broadcasted_iota(jnp.int32, (1, LANES), 1)
    return [lane < HEAD_DIM, lane >= HEAD_DIM]


def _pair_tiles(a):
    return [a[:, (h // 2) * LANES:(h // 2 + 1) * LANES] for h in range(a.shape[1] // HEAD_DIM)]


def _own_tiles(a, own):
    return [jnp.where(own[h % 2], tile, jnp.zeros_like(tile)) for h, tile in enumerate(_pair_tiles(a))]


def _merge_tiles(per_head, own):
    return jnp.concatenate([jnp.where(own[0], per_head[h], per_head[h + 1])
                            for h in range(0, len(per_head), 2)], axis=1)


def _scaled(q):
    return (q.astype(F32) * (HEAD_DIM ** -0.5)).astype(BF16)


def _sb_logits(qh, kh, tri, causal):
    zs = [_dot_nt(q, k) for q, k in zip(qh, kh)]
    es = [jnp.exp(-jnp.abs(z)) for z in zs]
    log_nots = [-(jnp.maximum(z, 0.0) + jnp.log(1.0 + e)) for z, e in zip(zs, es)]
    if causal is not None:
        log_nots = [jnp.where(causal, ln, 0.0) for ln in log_nots]
    return zs, es, [_split_dot(ln, tri) for ln in log_nots]


def _sb_masks():
    rows = lax.broadcasted_iota(jnp.int32, (SB_BLOCK, SB_BLOCK), 0)
    cols = lax.broadcasted_iota(jnp.int32, (SB_BLOCK, SB_BLOCK), 1)
    return (rows >= cols).astype(BF16), (rows <= cols).astype(BF16), cols < rows


def _sb_fwd(qkv, nb, seq, exchange=None):
    t = qkv.shape[0]
    n_pairs = (qkv.shape[1] // 3) // SB_WIDTH
    tb = SB_BLOCK
    n_blk = seq // tb

    def body(q_ref, k_ref, v_ref, o_ref, c_ref):
        tri, _, causal = _sb_masks()
        own = _own_lanes()

        def key_blocks(qh, kjs, carry, mask):
            nh = SB_HEADS
            chains = range(nh * len(kjs))
            kss = [pl.multiple_of(kj * tb, tb) for kj in kjs]
            kh = [tile for ks in kss for tile in _pair_tiles(k_ref[pl.ds(ks, tb), :])]
            vh = [tile for ks in kss for tile in _pair_tiles(v_ref[pl.ds(ks, tb), :])]
            zs, _, suffixes = _sb_logits(qh * len(kjs), kh, tri, mask)
            right = []
            for c in chains:
                right.append(carry[c][1] if c < nh else right[c - nh] + suffixes[c - nh][:, 0:1])
            ws = [jnp.exp(zs[c] + suffixes[c] + right[c]) for c in chains]
            if mask is not None:
                ws = [jnp.where(mask, w, 0.0) for w in ws]
            pv = [_dot(ws[c].astype(BF16), vh[c]) for c in chains]
            last = (len(kjs) - 1) * nh
            return tuple((carry[h][0] + sum(pv[h::nh]), right[last + h] + suffixes[last + h][:, 0:1])
                         for h in range(nh))

        def query_block(qi, _):
            qs = pl.multiple_of(qi * tb, tb)
            qh = _own_tiles(_scaled(q_ref[pl.ds(qs, tb), :]), own)
            zero = (jnp.zeros((tb, LANES), F32), jnp.zeros((tb, 1), F32))
            carry = key_blocks(qh, [qi], (zero,) * SB_HEADS, causal)
            carry = lax.fori_loop(
                0, qi // 2, lambda p, cr: key_blocks(qh, [qi - 1 - 2 * p, qi - 2 - 2 * p], cr, None), carry)
            carry = lax.fori_loop(0, qi % 2, lambda _, cr: key_blocks(qh, [0], cr, None), carry)
            o_ref[pl.ds(qs, tb), :] = _merge_tiles([cr[0] for cr in carry], own)
            c_ref[pl.ds(qs, tb), :] = _merge_tiles([jnp.broadcast_to(cr[1], (tb, LANES)) for cr in carry], own)
            return 0

        lax.fori_loop(0, n_blk, query_block, 0)

    def spec(offset):
        return pl.BlockSpec((seq, SB_WIDTH), lambda b, p: (b, offset + p))

    out = jax.ShapeDtypeStruct((t, n_pairs * SB_WIDTH), F32)
    return _call(
        body, name="sb_fwd", grid=(nb, n_pairs), args=(qkv, qkv, qkv),
        in_specs=[spec(0), spec(n_pairs), spec(2 * n_pairs)],
        out_specs=[spec(0), spec(0)], out_shape=[out, out],
        params=_params("arbitrary", "arbitrary"), exchange=exchange)


def _sb_bwd(qkv, do, csum, nb, seq, exchange=None):
    t = qkv.shape[0]
    n_pairs = (qkv.shape[1] // 3) // SB_WIDTH
    tb = SB_BLOCK
    n_blk = seq // tb
    scale = HEAD_DIM ** -0.5

    def body(q_ref, k_ref, v_ref, do_ref, c_ref, dq_ref, dk_ref, dv_ref, dkt_acc, dvt_acc):
        tri, tri_prefix, causal = _sb_masks()
        own = _own_lanes()
        dkt_acc[...] = jnp.zeros_like(dkt_acc)
        dvt_acc[...] = jnp.zeros_like(dvt_acc)

        def key_blocks(qh, qth, doh, doth, ch, kjs, carry, mask):
            nh = SB_HEADS
            chains = range(nh * len(kjs))
            kss = [pl.multiple_of(kj * tb, tb) for kj in kjs]
            kh = [tile for ks in kss for tile in _pair_tiles(k_ref[pl.ds(ks, tb), :])]
            vh = [tile for ks in kss for tile in _pair_tiles(v_ref[pl.ds(ks, tb), :])]
            zs, es, suffixes = _sb_logits(qh * len(kjs), kh, tri, mask)
            dws = [_dot_nt(doh[c % nh], vh[c]) for c in chains]
            lefts = []
            for c in chains:
                before = carry[c][1] if c < nh else lefts[c - nh]
                lefts.append(before + suffixes[c][:, 0:1])
            ws = [jnp.exp(zs[c] + suffixes[c] + (ch[c % nh] - lefts[c])) for c in chains]
            if mask is not None:
                ws = [jnp.where(mask, w, 0.0) for w in ws]
            dlws = [ws[c] * dws[c] for c in chains]
            dprefixes = [_split_dot(dlw, tri_prefix) for dlw in dlws]
            dvts = [_dot(doth[c % nh], ws[c].astype(BF16)) for c in chains]
            dlefts, dzbs = [], []
            for c in chains:
                dlefts.append(carry[c][2] if c < nh else dlefts[c - nh] + dprefixes[c - nh][:, tb - 1:tb])
                sig = jnp.where(zs[c] >= 0.0, 1.0, es[c]) * pl.reciprocal(1.0 + es[c], approx=True)
                dz = dlws[c] - sig * (dlefts[c] + dprefixes[c])
                if mask is not None:
                    dz = jnp.where(mask, dz, 0.0)
                dzbs.append(dz.astype(BF16))
            dkts = [_dot(qth[c % nh], dzbs[c]) for c in chains]
            dqs = [_dot(dzbs[c], kh[c]) for c in chains]
            for b, ks in enumerate(kss):
                pairs = range(b * nh, (b + 1) * nh, 2)
                dkt_acc[:, pl.ds(ks, tb)] += jnp.concatenate([dkts[c] + dkts[c + 1] for c in pairs], axis=0)
                dvt_acc[:, pl.ds(ks, tb)] += jnp.concatenate([dvts[c] + dvts[c + 1] for c in pairs], axis=0)
            last = (len(kjs) - 1) * nh
            return tuple((carry[h][0] + sum(dqs[h::nh]), lefts[last + h],
                          dlefts[last + h] + dprefixes[last + h][:, tb - 1:tb]) for h in range(nh))

        def query_block(qi, _):
            qs = pl.multiple_of(qi * tb, tb)
            qh = _own_tiles(_scaled(q_ref[pl.ds(qs, tb), :]), own)
            doh = _own_tiles(do_ref[pl.ds(qs, tb), :], own)
            qth = [a.astype(F32).T.astype(BF16) for a in qh]
            doth = [a.T.astype(BF16) for a in doh]
            doh = [a.astype(BF16) for a in doh]
            cv = c_ref[pl.ds(qs, tb), :]
            ch = [cv[:, h * HEAD_DIM:h * HEAD_DIM + 1] for h in range(SB_HEADS)]
            zero = (jnp.zeros((tb, LANES), F32), jnp.zeros((tb, 1), F32), jnp.zeros((tb, 1), F32))

            def key_block(kjs, cr, mask):
                return key_blocks(qh, qth, doh, doth, ch, kjs, cr, mask)

            carry = lax.fori_loop(0, qi // 2, lambda p, cr: key_block([2 * p, 2 * p + 1], cr, None),
                                  (zero,) * SB_HEADS)
            carry = lax.fori_loop(0, qi % 2, lambda _, cr: key_block([qi - 1], cr, None), carry)
            carry = key_block([qi], carry, causal)
            dq = _merge_tiles([cr[0] for cr in carry], own) * scale
            dq_ref[pl.ds(qs, tb), :] = dq.astype(BF16)
            return 0

        lax.fori_loop(0, n_blk, query_block, 0)
        dk_ref[...] = dkt_acc[...].T.astype(BF16)
        dv_ref[...] = dvt_acc[...].T.astype(BF16)

    def spec(offset):
        return pl.BlockSpec((seq, SB_WIDTH), lambda b, p: (b, offset + p))

    out = jax.ShapeDtypeStruct((t, n_pairs * SB_WIDTH), BF16)
    return _call(
        body, name="sb_bwd", grid=(nb, n_pairs), args=(qkv, qkv, qkv, do, csum),
        in_specs=[spec(0), spec(n_pairs), spec(2 * n_pairs), spec(0), spec(0)],
        out_specs=[spec(0), spec(0), spec(0)],
        out_shape=[out, out, out],
        scratch_shapes=[pltpu.VMEM((SB_WIDTH, seq), F32), pltpu.VMEM((SB_WIDTH, seq), F32)],
        params=_params("arbitrary", "arbitrary"), exchange=exchange)


def _dil_block_scores(qh, kph, kch, bias_ref, has_prev, band_prev, band_cur):
    scale = HEAD_DIM ** -0.5
    heads = range(len(qh))
    no_prev = jnp.where(has_prev, 0.0, NEG_INF)
    zps = [_dot_nt(qh[h], kph[h]) for h in heads]
    zcs = [_dot_nt(qh[h], kch[h]) for h in heads]
    zps = [jnp.where(band_prev, zps[h] * scale + bias_ref[h, :, 0:DIL_BLOCK], NEG_INF) + no_prev for h in heads]
    zcs = [jnp.where(band_cur, zcs[h] * scale + bias_ref[h, :, DIL_BLOCK:2 * DIL_BLOCK], NEG_INF) for h in heads]
    return zps, zcs


def _dil_bands():
    rows = lax.broadcasted_iota(jnp.int32, (DIL_BLOCK, DIL_BLOCK), 0)
    cols = lax.broadcasted_iota(jnp.int32, (DIL_BLOCK, DIL_BLOCK), 1)
    return cols >= rows, cols <= rows


def _dil_fwd(qkv, bias, nb, seq, dil, exchange=None):
    t, width = qkv.shape
    n_pairs = (width // 3) // DIL_WIDTH
    bq = DIL_BLOCK
    n_blk = seq // bq
    per_seq = n_blk // dil
    heads = range(DIL_HEADS)

    def body(q_ref, k_ref, v_ref, bias_ref, o_ref, lse_ref):
        band_prev, band_cur = _dil_bands()
        own = _own_lanes()

        def block(n, _):
            has_prev = (n & (per_seq - 1)) != 0
            qs = pl.multiple_of(n * bq, bq)
            ps = pl.multiple_of(jnp.maximum(n - 1, 0) * bq, bq)
            qh = _own_tiles(q_ref[pl.ds(qs, bq), :], own)
            kp, kc = _pair_tiles(k_ref[pl.ds(ps, bq), :]), _pair_tiles(k_ref[pl.ds(qs, bq), :])
            vp, vc = _pair_tiles(v_ref[pl.ds(ps, bq), :]), _pair_tiles(v_ref[pl.ds(qs, bq), :])
            zps, zcs = _dil_block_scores(qh, kp, kc, bias_ref, has_prev, band_prev, band_cur)
            ms = [jnp.maximum(jnp.max(zps[h], axis=1, keepdims=True), jnp.max(zcs[h], axis=1, keepdims=True))
                  for h in heads]
            eps = [jnp.exp(zps[h] - ms[h]) for h in heads]
            ecs = [jnp.exp(zcs[h] - ms[h]) for h in heads]
            pvs = [_dot(eps[h].astype(BF16), vp[h]) + _dot(ecs[h].astype(BF16), vc[h]) for h in heads]
            dens = [jnp.sum(eps[h], axis=1, keepdims=True) + jnp.sum(ecs[h], axis=1, keepdims=True) for h in heads]
            o_ref[pl.ds(qs, bq), :] = _merge_tiles([pvs[h] / dens[h] for h in heads], own)
            lse_ref[pl.ds(qs, bq), :] = _merge_tiles(
                [jnp.broadcast_to(ms[h] + jnp.log(dens[h]), (bq, LANES)) for h in heads], own)
            return 0

        lax.fori_loop(0, n_blk, block, 0, unroll=4)

    def spec(offset):
        return pl.BlockSpec((seq, DIL_WIDTH), lambda b, p: (b, offset + p))

    out = jax.ShapeDtypeStruct((t, n_pairs * DIL_WIDTH), F32)
    return _call(
        body, name=f"dil_fwd{dil}", grid=(nb, n_pairs), args=(qkv, qkv, qkv, bias),
        in_specs=[spec(0), spec(n_pairs), spec(2 * n_pairs),
                  pl.BlockSpec((DIL_HEADS, bq, 2 * bq), lambda b, p: (p, 0, 0))],
        out_specs=[spec(0), spec(0)], out_shape=[out, out],
        params=_params("arbitrary", "arbitrary"), exchange=exchange)


def _dil_bwd(qkv, bias, do, lse, delta, nb, seq, dil):
    t, width = qkv.shape
    n_pairs = (width // 3) // DIL_WIDTH
    bq = DIL_BLOCK
    n_blk = seq // bq
    per_seq = n_blk // dil
    scale = HEAD_DIM ** -0.5
    heads = range(DIL_HEADS)

    def body(q_ref, k_ref, v_ref, bias_ref, do_ref, lse_ref, dl_ref, dq_ref, dk_ref, dv_ref, db_ref,
             dk_acc, dv_acc):
        band_prev, band_cur = _dil_bands()
        own = _own_lanes()
        dk_acc[...] = jnp.zeros_like(dk_acc)
        dv_acc[...] = jnp.zeros_like(dv_acc)

        @pl.when(pl.program_id(1) == 0)
        def _():
            db_ref[...] = jnp.zeros_like(db_ref)

        def block(n, _):
            has_prev = (n & (per_seq - 1)) != 0
            qs = pl.multiple_of(n * bq, bq)
            ps = pl.multiple_of(jnp.maximum(n - 1, 0) * bq, bq)
            qh = _own_tiles(q_ref[pl.ds(qs, bq), :], own)
            kp, kc = _pair_tiles(k_ref[pl.ds(ps, bq), :]), _pair_tiles(k_ref[pl.ds(qs, bq), :])
            vp, vc = _pair_tiles(v_ref[pl.ds(ps, bq), :]), _pair_tiles(v_ref[pl.ds(qs, bq), :])
            doh = _own_tiles(do_ref[pl.ds(qs, bq), :].astype(BF16), own)
            lse_v, dl_v = lse_ref[pl.ds(qs, bq), :], dl_ref[pl.ds(qs, bq), :]
            zps, zcs = _dil_block_scores(qh, kp, kc, bias_ref, has_prev, band_prev, band_cur)
            dpp = [_dot_nt(doh[h], vp[h]) for h in heads]
            dpc = [_dot_nt(doh[h], vc[h]) for h in heads]
            lse_h = [lse_v[:, h * HEAD_DIM:h * HEAD_DIM + 1] for h in heads]
            dl_h = [dl_v[:, h * HEAD_DIM:h * HEAD_DIM + 1] for h in heads]
            pps = [jnp.exp(zps[h] - lse_h[h]) for h in heads]
            pcs = [jnp.exp(zcs[h] - lse_h[h]) for h in heads]
            dvp = [_dot_tn(pps[h].astype(BF16), doh[h]) for h in heads]
            dvc = [_dot_tn(pcs[h].astype(BF16), doh[h]) for h in heads]
            dzps = [pps[h] * (dpp[h] - dl_h[h]) for h in heads]
            dzcs = [pcs[h] * (dpc[h] - dl_h[h]) for h in heads]
            dzp_b = [(dzps[h] * scale).astype(BF16) for h in heads]
            dzc_b = [(dzcs[h] * scale).astype(BF16) for h in heads]
            dqs = [_dot(dzp_b[h], kp[h]) + _dot(dzc_b[h], kc[h]) for h in heads]
            dkp = [_dot_tn(dzp_b[h], qh[h]) for h in heads]
            dkc = [_dot_tn(dzc_b[h], qh[h]) for h in heads]
            for h in heads:
                db_ref[h, :, 0:bq] += dzps[h]
                db_ref[h, :, bq:2 * bq] += dzcs[h]
            def pair_sums(per_head):
                return jnp.concatenate([per_head[h] + per_head[h + 1] for h in heads[::2]], axis=1)

            dq_ref[pl.ds(qs, bq), :] = _merge_tiles(dqs, own).astype(BF16)
            dk_acc[pl.ds(ps, bq), :] += pair_sums(dkp)
            dk_acc[pl.ds(qs, bq), :] += pair_sums(dkc)
            dv_acc[pl.ds(ps, bq), :] += pair_sums(dvp)
            dv_acc[pl.ds(qs, bq), :] += pair_sums(dvc)
            return 0

        lax.fori_loop(0, n_blk, block, 0, unroll=4)
        dk_ref[...] = dk_acc[...].astype(BF16)
        dv_ref[...] = dv_acc[...].astype(BF16)

    def spec(offset):
        return pl.BlockSpec((seq, DIL_WIDTH), lambda p, b: (b, offset + p))

    bias_spec = pl.BlockSpec((DIL_HEADS, bq, 2 * bq), lambda p, b: (p, 0, 0))
    out = jax.ShapeDtypeStruct((t, n_pairs * DIL_WIDTH), BF16)
    return pl.pallas_call(
        body, name=f"dil_bwd{dil}", grid=(n_pairs, nb),
        in_specs=[spec(0), spec(n_pairs), spec(2 * n_pairs), bias_spec, spec(0), spec(0), spec(0)],
        out_specs=[spec(0), spec(0), spec(0), bias_spec],
        out_shape=[out, out, out, jax.ShapeDtypeStruct(bias.shape, F32)],
        scratch_shapes=[pltpu.VMEM((seq, DIL_WIDTH), F32), pltpu.VMEM((seq, DIL_WIDTH), F32)],
        compiler_params=_params("arbitrary", "arbitrary"),
    )(qkv, qkv, qkv, bias, do, lse, delta)


def _head_blocks(width):
    rows = lax.broadcasted_iota(jnp.int32, (width, width), 0) // HEAD_DIM
    cols = lax.broadcasted_iota(jnp.int32, (width, width), 1) // HEAD_DIM
    return (rows == cols).astype(BF16)


def _head_mean(v, gmat):
    return _split_dot(v, gmat) * (1.0 / HEAD_DIM)


def _residue_views(arrays, nb, seq):
    return [a if dil == 1 else a.reshape(nb, dil, seq // dil, a.shape[1]) for a, dil in zip(arrays, DILATIONS)]


def _mix_out_fwd(osb, ocs, lses, gsb, gdil, wout, x, mod, tm):
    t, d = x.shape
    ds = osb.shape[1]
    nt = t // tm
    nb = mod.shape[0]
    tpb = nt // nb
    seq = t // nb
    n_cfg = len(DILATIONS)

    def body(osb_ref, *refs):
        oc_refs, lse_refs = refs[:n_cfg], refs[n_cfg:2 * n_cfg]
        gsb_ref, gdil_ref, w_ref, x_ref, mod_ref = refs[2 * n_cfg:2 * n_cfg + 5]
        xo_ref, on_ref, m_ref, odil_ref = refs[2 * n_cfg + 5:2 * n_cfg + 9]
        ld_refs = refs[2 * n_cfg + 9:3 * n_cfg + 9]
        stages, sc = refs[3 * n_cfg + 9:]
        ocv, lsev = [oc_refs[0][...]], [lse_refs[0][...]]
        for i, dil in enumerate(DILATIONS[1:]):
            ocv.append(_from_residue_rows(oc_refs[i + 1], stages.at[2 * i], dil))
            lsev.append(_from_residue_rows(lse_refs[i + 1], stages.at[2 * i + 1], dil))
        top = functools.reduce(jnp.maximum, lsev)
        total = top + jnp.log(sum(jnp.exp(l - top) for l in lsev))
        odil = sum(jnp.exp(l - total) * o for o, l in zip(ocv, lsev))
        odil_ref[...] = odil
        ld_refs[0][...] = total
        _stage(total, sc)
        for ref, dil in zip(ld_refs[1:], DILATIONS[1:]):
            _to_residue_rows(sc, ref, dil)
        gm = _head_blocks(ds)
        parts = []
        for o, g_ref in ((osb_ref[...], gsb_ref), (odil, gdil_ref)):
            parts.append(o * lax.rsqrt(_head_mean(o * o, gm) + EPS) * g_ref[...])
        on = jnp.concatenate(parts, axis=1).astype(BF16)
        on_ref[...] = on
        m = _dot(on, w_ref[...])
        m_ref[...] = m
        xo_ref[...] = x_ref[...] + mod_ref[5:6, :] * m

    tok = pl.BlockSpec((tm, d), lambda i: (i, 0))
    hd = pl.BlockSpec((tm, ds), lambda i: (i, 0))
    res = [hd] + [_residue_spec(tm, tpb, ds, dil, lambda i: 0) for dil in DILATIONS[1:]]
    res_shape = [jax.ShapeDtypeStruct((t, ds), F32)] + [_residue_shape(nb, seq, ds, dil, F32) for dil in DILATIONS[1:]]
    gain = pl.BlockSpec((1, ds), lambda i: (0, 0))
    outs = pl.pallas_call(
        body, name="mix_out_fwd", grid=(nt,),
        in_specs=[hd] + res + res + [gain, gain,
                  pl.BlockSpec(wout.shape, lambda i: (0, 0)),
                  tok, pl.BlockSpec((None, N_MOD, d), lambda i: (i // tpb, 0, 0))],
        out_specs=[tok, pl.BlockSpec((tm, 2 * ds), lambda i: (i, 0)), tok, hd] + res,
        out_shape=[jax.ShapeDtypeStruct((t, d), F32), jax.ShapeDtypeStruct((t, 2 * ds), BF16),
                   jax.ShapeDtypeStruct((t, d), F32), jax.ShapeDtypeStruct((t, ds), F32)] + res_shape,
        scratch_shapes=[pltpu.VMEM((2 * (n_cfg - 1), ds // LANES, tm, LANES), F32), _stage_shape(tm, ds)],
        compiler_params=_params("arbitrary"),
    )(osb, *_residue_views(ocs, nb, seq), *_residue_views(lses, nb, seq), gsb, gdil, wout, x, mod)
    return outs[0], outs[1], outs[2], outs[3], [a.reshape(t, ds) for a in outs[4:]]


def _mix_out_bwd(dxo, m, mod, wout, osb, odil, gsb, gdil, tm):
    t, d = dxo.shape
    ds = osb.shape[1]
    nt = t // tm
    nb = mod.shape[0]
    tpb = nt // nb
    seq = t // nb
    n_cfg = len(DILATIONS)

    def body(dxo_ref, m_ref, mod_ref, w_ref, osb_ref, odil_ref, gsb_ref, gdil_ref,
             dm_ref, dosb_ref, *rest):
        do_refs, dl_refs = rest[:n_cfg], rest[n_cfg:2 * n_cfg]
        dmod_ref, dg_ref, sc = rest[2 * n_cfg:]
        dodil_ref, dldil_ref = do_refs[0], dl_refs[0]
        i = pl.program_id(0)
        dxo_v = dxo_ref[...]
        dm = (mod_ref[5:6, :] * dxo_v).astype(BF16)
        dm_ref[...] = dm
        dgt = jnp.sum(m_ref[...] * dxo_v, axis=0, keepdims=True)
        don = _dot_nt(dm, w_ref[...])
        gm = _head_blocks(ds)

        @pl.when(i % tpb == 0)
        def _():
            dmod_ref[...] = jnp.zeros_like(dmod_ref)

        @pl.when(i == 0)
        def _():
            dg_ref[...] = jnp.zeros_like(dg_ref)

        dmod_ref[2:3, :] += dgt
        groups = ((osb_ref, gsb_ref, dosb_ref), (odil_ref, gdil_ref, dodil_ref))
        for k, (o_ref, g_ref, do_ref) in enumerate(groups):
            o = o_ref[...]
            dn_out = don[:, k * ds:(k + 1) * ds]
            r = lax.rsqrt(_head_mean(o * o, gm) + EPS)
            n = o * r
            dg_ref[0:1, k * ds:(k + 1) * ds] += jnp.sum(dn_out * n, axis=0, keepdims=True)
            dn = dn_out * g_ref[...]
            do = r * (dn - n * _head_mean(dn * n, gm))
            do_ref[...] = do
            if k == 1:
                delta = _head_mean(do * o, gm) * float(HEAD_DIM)
                dldil_ref[...] = delta
                for value, refs in ((do, do_refs), (delta, dl_refs)):
                    _stage(value, sc)
                    for ref, dil in zip(refs[1:], DILATIONS[1:]):
                        _to_residue_rows(sc, ref, dil)

    tok = pl.BlockSpec((tm, d), lambda i: (i, 0))
    hd = pl.BlockSpec((tm, ds), lambda i: (i, 0))
    res = [hd] + [_residue_spec(tm, tpb, ds, dil, lambda i: 0) for dil in DILATIONS[1:]]
    res_shape = [jax.ShapeDtypeStruct((t, ds), F32)] + [_residue_shape(nb, seq, ds, dil, F32) for dil in DILATIONS[1:]]
    gain = pl.BlockSpec((1, ds), lambda i: (0, 0))
    outs = pl.pallas_call(
        body, name="mix_out_bwd", grid=(nt,),
        in_specs=[tok, tok, pl.BlockSpec((None, N_MOD, d), lambda i: (i // tpb, 0, 0)),
                  pl.BlockSpec(wout.shape, lambda i: (0, 0)), hd, hd, gain, gain],
        out_specs=[tok, hd] + res + res
        + [pl.BlockSpec((None, 8, d), lambda i: (i // tpb, 0, 0)), pl.BlockSpec((8, 2 * ds), lambda i: (0, 0))],
        out_shape=[jax.ShapeDtypeStruct((t, d), BF16), jax.ShapeDtypeStruct((t, ds), F32)] + res_shape + res_shape
        + [jax.ShapeDtypeStruct((nb, 8, d), F32), jax.ShapeDtypeStruct((8, 2 * ds), F32)],
        scratch_shapes=[_stage_shape(tm, ds)],
        compiler_params=_params("arbitrary"),
    )(dxo, m, mod, wout, osb, odil, gsb, gdil)
    flat = [a.reshape(t, ds) for a in outs[2:2 + 2 * n_cfg]]
    return outs[0], outs[1], flat[:n_cfg], flat[n_cfg:], outs[-2], outs[-1]


def _merge_dqkv(sb_parts, dil_parts, nb, tm):
    t, ds = sb_parts[0].shape
    nt = t // tm
    tpb = nt // nb
    seq = t // nb
    n_cfg = len(DILATIONS)

    def body(*refs):
        sb_refs, dil_refs = refs[:3], refs[3:3 + 3 * n_cfg]
        o_ref, sc = refs[3 + 3 * n_cfg:]
        for k in range(3):
            o_ref[:, k * ds:(k + 1) * ds] = sb_refs[k][...]
            total = dil_refs[k * n_cfg][...].astype(F32)
            for i, dil in enumerate(DILATIONS[1:]):
                total = total + _from_residue_rows(dil_refs[k * n_cfg + i + 1], sc, dil)
            o_ref[:, (3 + k) * ds:(4 + k) * ds] = total.astype(BF16)

    hd = pl.BlockSpec((tm, ds), lambda i: (i, 0))
    res = [hd] + [_residue_spec(tm, tpb, ds, dil, lambda i: 0) for dil in DILATIONS[1:]]
    views = [v for parts in dil_parts for v in _residue_views(parts, nb, seq)]
    return pl.pallas_call(
        body, name="merge_dqkv", grid=(nt,),
        in_specs=[hd] * 3 + res * 3,
        out_specs=pl.BlockSpec((tm, 6 * ds), lambda i: (i, 0)),
        out_shape=jax.ShapeDtypeStruct((t, 6 * ds), BF16),
        scratch_shapes=[_stage_shape(tm, ds)],
        compiler_params=_params("arbitrary"),
    )(*sb_parts, *views)


def _row_tile(rows):
    if rows <= 256:
        return rows
    for cand in range(256, 15, -16):
        if rows % cand == 0:
            return cand
    return rows


def _adamw(w, parts, m, v, name, transposed=False):
    rows, cols = w.shape
    n_parts = parts.shape[0]
    tr = _row_tile(rows)
    c1 = 1.0 / (1.0 - ADAM_B1 ** ADAM_STEP)
    c2 = 1.0 / (1.0 - ADAM_B2 ** ADAM_STEP)

    def body(w_ref, p_ref, m_ref, v_ref, g_ref, d_ref, nm_ref, nv_ref):
        g = p_ref[0].astype(F32)
        for i in range(1, n_parts):
            g = g + p_ref[i].astype(F32)
        wv, mv, vv = w_ref[...], m_ref[...], v_ref[...]
        if transposed:
            wv, mv, vv = wv.T, mv.T, vv.T
        nm = ADAM_B1 * mv + (1.0 - ADAM_B1) * g
        nv = ADAM_B2 * vv + (1.0 - ADAM_B2) * (g * g)
        g_ref[...] = g
        nm_ref[...] = nm
        nv_ref[...] = nv
        d_ref[...] = -ADAM_LR * ((nm * c1) / (jnp.sqrt(nv * c2) + ADAM_EPS) + ADAM_WD * wv)

    blk = pl.BlockSpec((tr, cols), lambda i: (i, 0))
    if transposed:
        oblk = pl.BlockSpec((cols, tr), lambda i: (0, i))
        pblk = pl.BlockSpec((n_parts, cols, tr), lambda i: (0, 0, i))
        out = jax.ShapeDtypeStruct((cols, rows), F32)
    else:
        oblk, pblk = blk, pl.BlockSpec((n_parts, tr, cols), lambda i: (0, i, 0))
        out = jax.ShapeDtypeStruct((rows, cols), F32)
    return pl.pallas_call(
        body, name=name, grid=(rows // tr,),
        in_specs=[blk, pblk, blk, blk],
        out_specs=[oblk, oblk, oblk, oblk], out_shape=[out, out, out, out],
        compiler_params=_params("arbitrary"),
    )(w, parts, m, v)


def _t5_bucket(n):
    max_exact = N_BUCKETS // 2
    nf = np.maximum(n, 1).astype(np.float32)
    large = max_exact + (np.log(nf / max_exact) / math.log(MAX_DISTANCE / max_exact)
                         * (N_BUCKETS - max_exact)).astype(np.int32)
    large = np.minimum(large, N_BUCKETS - 1)
    return np.where(n < max_exact, n, large).astype(np.int32)


def _bucket_onehot():
    table = np.zeros((len(DILATIONS), 2 * DIL_BLOCK + 1, N_BUCKETS), np.float32)
    for i, dil in enumerate(DILATIONS):
        buckets = _t5_bucket(np.arange(DIL_BLOCK + 1) * dil)
        for m in range(DIL_BLOCK + 1):
            table[i, m, buckets[DIL_BLOCK - m]] = 1.0
    return table


def _bias_blocks(rel_bias):
    row = jnp.einsum("cmn,nh->chm", _bucket_onehot(), rel_bias, precision=lax.Precision.HIGHEST)
    n_cfg, n_heads, width = row.shape
    tiled = jnp.tile(row, (1, 1, DIL_BLOCK))[..., :DIL_BLOCK * (width - 1)]
    return tiled.reshape(n_cfg, n_heads, DIL_BLOCK, width - 1)


def _bias_blocks_bwd(dblocks):
    n_cfg, n_heads = dblocks.shape[:2]
    width = 2 * DIL_BLOCK + 1
    flat = dblocks.reshape(n_cfg, n_heads, DIL_BLOCK * (width - 1))
    flat = jnp.pad(flat, ((0, 0), (0, 0), (0, DIL_BLOCK)))
    drow = jnp.sum(flat.reshape(n_cfg, n_heads, DIL_BLOCK, width), axis=2)
    return jnp.einsum("chm,cmn->nh", drow, _bucket_onehot(), precision=lax.Precision.HIGHEST)


def _pad_to(a, axis, size):
    pad = [(0, 0)] * a.ndim
    pad[axis] = (0, size - a.shape[axis])
    return jnp.pad(a, pad)


def _lane_pad(n):
    return -(-n // LANES) * LANES


def _local_step(x, target, mod, gains, weights, rel_bias, tm, distributed):
    nb, seq, d = x.shape
    t = nb * seq
    g_ffn1, g_mix, g_sb, g_dil, g_ffn2, g_final = gains
    wg1, wu1, wd1 = weights[:3]
    x0 = x.reshape(t, d)
    ds = g_sb.shape[1]
    bias = _bias_blocks(rel_bias)

    def beside(arrays, scatter):
        return _Exchange(arrays, scatter) if distributed else None

    tp, tg = min(PROJ_TILE, seq), min(GRAD_TILE, t)

    (x1, f1, gate1, up1), got = _ffn_fwd(x0, mod, g_ffn1, wg1, wu1, wd1, 0, tp, beside(weights[3:4], False))
    win = got[0] if distributed else weights[3]
    (qkv, qkvd, h2), got = _qkv_fwd(x1, mod, g_mix, win, tp, beside(weights[4:5], False))
    wout = got[0] if distributed else weights[4]
    wout2 = wout.reshape(-1, d)
    (osb, csb), got = _sb_fwd(qkv, nb, seq, beside(weights[5:7], False))
    wg2, wu2 = got if distributed else weights[5:7]
    n_cfg = len(DILATIONS)
    piece = -(-weights[7].shape[-2] // n_cfg // 16) * 16
    ocs, lses, wd2_pieces = [], [], []
    for i, dil in enumerate(DILATIONS):
        rows = weights[7][..., i * piece:(i + 1) * piece, :]
        (oc, lse), got = _dil_fwd(qkvd[i], bias[i], nb, seq, dil, beside([rows], False))
        wd2_pieces.append(got[0] if distributed else rows)
        ocs.append(oc)
        lses.append(lse)
    wd2 = jnp.concatenate(wd2_pieces, axis=-2)
    x2, on, mix, odil, ldil = _mix_out_fwd(osb, ocs, lses, g_sb, g_dil, wout2, x1, mod, tm)
    (dx3, f3, gate3, up3, head), _ = _ffn_fwd(x2, mod, g_ffn2, wg2, wu2, wd2, 2, tp,
                                              head=(target.reshape(t, d), g_final))
    loss_sum = 0.5 * jnp.sum(head[0]) / d
    dg_final = head[1:2]

    (dx2, dgate3, dup3, act3, h3, df3, dmod3, dg_ffn2), _ = _ffn_bwd(
        dx3, x2, f3, mod, g_ffn2, gate3, up3, wg2, wu2, wd2, 2, tm)
    gwg2, gwu2, gwd2 = _ffn_weight_grads(h3, dgate3, dup3, act3, df3, tg, 2)

    dm, dosb, dodil, dldil, dmod2b, dg_heads = _mix_out_bwd(
        dx2, mix, mod, wout2, osb, odil, g_sb, g_dil, tm)
    gwout = _mm_tn(on, dm,
                   pl.BlockSpec((tg, wout.shape[1]), lambda i, j: (i, j)),
                   pl.BlockSpec((tg, d), lambda i, j: (i, 0)),
                   wout.shape, t // tg, "grad_wout")

    (dq_sb, dk_sb, dv_sb), parts_late = _sb_bwd(qkv, dosb, csb, nb, seq,
                                                beside([gwout, gwg2, gwu2, gwd2], True))
    dil_grads = [_dil_bwd(qkvd[i], bias[i], dodil[i], ldil[i], dldil[i], nb, seq, dil)
                 for i, dil in enumerate(DILATIONS)]
    dqkv = _merge_dqkv([dq_sb, dk_sb, dv_sb], [[g[k] for g in dil_grads] for k in range(3)], nb, tm)
    drel = _bias_blocks_bwd(jnp.stack([g[3] for g in dil_grads]))

    cs = win.shape[2]
    gwin = _mm_tn(h2, dqkv,
                  pl.BlockSpec((tg, d), lambda i, j: (i, 0)),
                  pl.BlockSpec((tg, cs), lambda i, j: (i, j)),
                  win.shape, t // tg, "grad_win", pair_reduce=distributed)
    (dx1, dmod2a, dg_mix), parts_mid = _qkv_bwd(
        dqkv, dx2, x1, mod, g_mix, win, tp, _Exchange([gwin], True, chips=[True]) if distributed else None)

    (dx0, dgate1, dup1, act1, h1, df1, dmod1, dg_ffn1), _ = _ffn_bwd(
        dx1, x0, f1, mod, g_ffn1, gate1, up1, wg1, wu1, wd1, 0, tm)
    dmod = jnp.concatenate([dmod1[:, 0:3], dmod2a[:, 0:2], dmod2b[:, 2:3], dmod3[:, 0:3]], axis=1)
    ggrads = (dg_ffn1[0:1], dg_mix[0:1], dg_heads[0:1], drel, dg_ffn2[0:1], dg_final)
    if not distributed:
        gw1 = _ffn_weight_grads(h1, dgate1, dup1, act1, df1, tg, 0)
        return loss_sum, dx0.reshape(nb, seq, d), tuple(gw1) + (gwin, gwout, gwg2, gwu2, gwd2), dmod, ggrads

    dg_heads_row, drel_flat = dg_heads[0:1], drel.reshape(1, -1)
    width = max(d, dg_heads_row.shape[1], drel_flat.shape[1])
    small = jnp.concatenate(
        [_pad_to(a.reshape(1, -1), 1, width)
         for a in (dg_ffn1[0:1], dg_mix[0:1], dg_ffn2[0:1], dg_final, dg_heads_row, drel_flat, loss_sum)]
        + [jnp.zeros((1, width), F32)], axis=0)
    dmod_pad = _pad_to(dmod.reshape(nb, N_MOD * d), 0, 8)
    everyone = _Exchange([jnp.broadcast_to(dmod_pad, (N_DEV,) + dmod_pad.shape),
                          jnp.broadcast_to(small, (N_DEV,) + small.shape)], True)
    sent_g, sent_u, gwd1, (dmod_all, small_all) = _ffn_weight_grads(
        h1, dgate1, dup1, act1, df1, tg, 0, stream=True, first=everyone)
    wgrads = (sent_g, sent_u, gwd1) + tuple(parts_mid + parts_late)
    return dx0.reshape(nb, seq, d), wgrads, dmod_all, small_all


def kernel(x, c, w_ada, b_ada, g_ffn1, w1_gate, w1_up, w1_down, g_mix, w_in, g_sb_out, g_dil_out, w_out, rel_bias, g_ffn2, w2_gate, w2_up, w2_down, g_final, loss_target, m_w_ada, m_b_ada, m_g_ffn1, m_w1_gate, m_w1_up, m_w1_down, m_g_mix, m_w_in, m_g_sb_out, m_g_dil_out, m_w_out, m_rel_bias, m_g_ffn2, m_w2_gate, m_w2_up, m_w2_down, m_g_final, v_w_ada, v_b_ada, v_g_ffn1, v_w1_gate, v_w1_up, v_w1_down, v_g_mix, v_w_in, v_g_sb_out, v_g_dil_out, v_w_out, v_rel_bias, v_g_ffn2, v_w2_gate, v_w2_up, v_w2_down, v_g_final):
    nb, seq, d = x.shape
    me = 4 * lax.axis_index("x") + 2 * lax.axis_index("y") + lax.axis_index("c")
    tm = min(TOKEN_TILE, seq)
    fs = w1_gate.shape[2]
    fs_pad = _lane_pad(fs)
    ada_cols = w_ada.shape[2]

    def col_shard(w):
        return _pad_to(w[0].astype(BF16), 1, fs_pad)

    def row_shard(w):
        return _pad_to(w[0].astype(BF16), 0, fs_pad)

    shards = [col_shard(w1_gate), col_shard(w1_up), row_shard(w1_down), w_in[0].astype(BF16),
              w_out[0].astype(BF16), col_shard(w2_gate), col_shard(w2_up), row_shard(w2_down)]
    b_cols = lax.dynamic_slice(b_ada, (0, me * ada_cols), (1, ada_cols))
    c_every, mod_all, first = _first_exchange(_pad_to(c, 0, 8), shards[:3], w_ada[0], b_cols)
    c_all = c_every[:, :nb].reshape(N_DEV * nb, d)
    weights = first + shards[3:]
    mod = lax.dynamic_slice(mod_all, (0, me * 8, 0), (N_DEV, nb, ada_cols))
    mod = mod.transpose(1, 0, 2).reshape(nb, N_MOD, d)

    n_sb = g_sb_out.shape[1] * g_sb_out.shape[2]
    gains = (g_ffn1, g_mix, g_sb_out.reshape(1, n_sb), g_dil_out.reshape(1, -1), g_ffn2,
             g_final.reshape(1, d))
    grad_x, parts, dmod_all, small_all = _local_step(x, loss_target, mod, gains, weights, rel_bias, tm, True)

    last_part = _exchange([parts[2]], True, "scatter_last", chips=[True])[0]
    parts = parts[:2] + (last_part,) + parts[3:]
    dmod_all = dmod_all[:, :nb].reshape(N_DEV * nb, N_MOD * d)
    dmod_cols = lax.dynamic_slice(dmod_all, (0, me * ada_cols), (N_DEV * nb, ada_cols))
    gw_ada, gb_ada = _ada_bwd(c_all, dmod_cols, dmod_all)

    def small_part(row, size, shape):
        return small_all[:, row, :size].reshape((N_DEV,) + shape)

    loss = jnp.sum(small_all[:, 6, 0])

    n_rel = rel_bias.shape
    updates = {
        "w_ada": (w_ada[0], gw_ada[None], m_w_ada[0], v_w_ada[0]),
        "b_ada": (b_ada, gb_ada[None], m_b_ada, v_b_ada),
        "g_ffn1": (g_ffn1, small_part(0, d, (1, d)), m_g_ffn1, v_g_ffn1),
        "w1_gate": (w1_gate[0], parts[0], m_w1_gate[0], v_w1_gate[0]),
        "w1_up": (w1_up[0], parts[1], m_w1_up[0], v_w1_up[0]),
        "w1_down": (w1_down[0], parts[2], m_w1_down[0], v_w1_down[0]),
        "g_mix": (g_mix, small_part(1, d, (1, d)), m_g_mix, v_g_mix),
        "w_in": (w_in[0], parts[3], m_w_in[0], v_w_in[0]),
        "g_sb_out": (g_sb_out[0], small_all[:, 4, :n_sb].reshape((N_DEV,) + g_sb_out.shape[1:]),
                     m_g_sb_out[0], v_g_sb_out[0]),
        "g_dil_out": (g_dil_out[0], small_all[:, 4, n_sb:n_sb + g_dil_out[0].size].reshape((N_DEV,) + g_dil_out.shape[1:]),
                      m_g_dil_out[0], v_g_dil_out[0]),
        "w_out": (w_out[0], parts[4], m_w_out[0], v_w_out[0]),
        "rel_bias": (rel_bias, small_part(5, rel_bias.size, n_rel), m_rel_bias, v_rel_bias),
        "g_ffn2": (g_ffn2, small_part(2, d, (1, d)), m_g_ffn2, v_g_ffn2),
        "w2_gate": (w2_gate[0], parts[5], m_w2_gate[0], v_w2_gate[0]),
        "w2_up": (w2_up[0], parts[6], m_w2_up[0], v_w2_up[0]),
        "w2_down": (w2_down[0], parts[7], m_w2_down[0], v_w2_down[0]),
        "g_final": (g_final.reshape(1, d), small_part(3, d, (1, d)), m_g_final.reshape(1, d), v_g_final.reshape(1, d)),
    }
    shapes = {"w_ada": w_ada.shape, "b_ada": b_ada.shape, "g_ffn1": g_ffn1.shape, "w1_gate": w1_gate.shape,
              "w1_up": w1_up.shape, "w1_down": w1_down.shape, "g_mix": g_mix.shape, "w_in": w_in.shape,
              "g_sb_out": g_sb_out.shape, "g_dil_out": g_dil_out.shape, "w_out": w_out.shape,
              "rel_bias": rel_bias.shape, "g_ffn2": g_ffn2.shape, "w2_gate": w2_gate.shape,
              "w2_up": w2_up.shape, "w2_down": w2_down.shape, "g_final": g_final.shape}
    grads, deltas, new_m, new_v = [], [], [], []
    for name, (w, p, m, v) in updates.items():
        transposed = name in ("w1_gate", "w1_up", "w2_gate", "w2_up")
        outs = _adamw(w, p, m, v, f"adamw_{name}", transposed)
        for dst, a in zip((grads, deltas, new_m, new_v), outs):
            dst.append((a.T if transposed else a).reshape(shapes[name]))
    return (loss, grad_x, *grads, *deltas, *new_m, *new_v)
```
